```python
import math
import jax, jax.numpy as jnp
from jax import lax
import numpy as np

D_MODEL = 1024
BATCH = 8
SEQ = 8192
DEPTH = 1

HEAD_DIM = 64
N_ATTN_HEADS = 8
N_KV_HEADS = 2
N_GMLP_GROUPS = 8
GMLP_GROUP_DIM = 64
ATTN_WIDTH = N_ATTN_HEADS * HEAD_DIM
KV_WIDTH = N_KV_HEADS * HEAD_DIM
GMLP_WIDTH = N_GMLP_GROUPS * GMLP_GROUP_DIM
MIX_WIDTH = ATTN_WIDTH + GMLP_WIDTH
IN_WIDTH = ATTN_WIDTH + 2 * KV_WIDTH + 2 * GMLP_WIDTH
WINDOW = 128
BLOCK = 128
CHUNK = 128
N_BUCKETS = 32
MAX_DISTANCE = 128
D_FF = -(-8 * D_MODEL // (3 * 256)) * 256
ALPHA = (2 * DEPTH) ** 0.25
BETA = (8 * DEPTH) ** -0.25
LN_EPS = 1e-5
NEG_INF = -1e30

kernel_name = "hymba_gmlp_swa_sink_deepnorm_adaln"


def layer_norm(x, g, b):
    xf = x.astype(jnp.float32)
    mu = jnp.mean(xf, axis=-1, keepdims=True)
    var = jnp.mean(jnp.square(xf - mu), axis=-1, keepdims=True)
    return ((xf - mu) * lax.rsqrt(var + LN_EPS) * g.astype(jnp.float32) + b.astype(jnp.float32)).astype(x.dtype)


def rms_norm(x, g):
    xf = x.astype(jnp.float32)
    ms = jnp.mean(jnp.square(xf), axis=-1, keepdims=True)
    return (xf * lax.rsqrt(ms + LN_EPS) * g.astype(jnp.float32)).astype(x.dtype)


def t5_bucket(dist):
    max_exact = N_BUCKETS // 2
    n = jnp.maximum(dist, 0)
    nf = jnp.maximum(n, max_exact).astype(jnp.float32)
    large = max_exact + (jnp.log(nf / max_exact) / math.log(MAX_DISTANCE / max_exact)
                         * (N_BUCKETS - max_exact)).astype(jnp.int32)
    large = jnp.minimum(large, N_BUCKETS - 1)
    return jnp.where(n < max_exact, n, large)


def sliding_window_attention(q, k, v, sinks, rel_bias):
    B, S, H, Dh = q.shape
    nb = S // BLOCK
    G = H // N_KV_HEADS
    qb = q.reshape(B, nb, BLOCK, N_KV_HEADS, G, Dh)
    kb = k.reshape(B, nb, BLOCK, N_KV_HEADS, Dh)
    vb = v.reshape(B, nb, BLOCK, N_KV_HEADS, Dh)
    kpad = jnp.zeros_like(kb[:, :1])
    vpad = jnp.zeros_like(vb[:, :1])
    kk = jnp.concatenate([jnp.concatenate([kpad, kb[:, :-1]], axis=1), kb], axis=2)
    vv = jnp.concatenate([jnp.concatenate([vpad, vb[:, :-1]], axis=1), vb], axis=2)
    logits = jnp.einsum('bnqkgd,bnskd->bnkgqs', qb, kk,
                        preferred_element_type=jnp.float32) * (Dh ** -0.5)
    qi = jnp.arange(BLOCK)[:, None]
    si = jnp.arange(2 * BLOCK)[None, :]
    dist = qi + BLOCK - si
    in_window = (dist >= 0) & (dist < WINDOW)
    bias = rel_bias.astype(jnp.float32)[t5_bucket(dist)]
    bias = bias.transpose(2, 0, 1).reshape(N_KV_HEADS, G, BLOCK, 2 * BLOCK)
    valid = in_window[None] & ((jnp.arange(nb)[:, None, None] > 0) | (si[None] >= BLOCK))
    logits = jnp.where(valid[None, :, None, None], logits + bias[None, None], NEG_INF)
    sink = sinks.astype(jnp.float32).reshape(N_KV_HEADS, G)[None, None, :, :, None, None]
    m = jnp.maximum(jnp.max(logits, axis=-1, keepdims=True), sink)
    p = jnp.exp(logits - m)
    p = p / (jnp.sum(p, axis=-1, keepdims=True) + jnp.exp(sink - m))
    out = jnp.einsum('bnkgqs,bnskd->bnqkgd', p.astype(v.dtype), vv)
    return out.reshape(B, S, H * Dh)


def chunked_spatial_gating(u, v, ln_g, ln_b, w_s, b_s):
    B, S, _ = u.shape
    nc = S // CHUNK
    G, Dg = N_GMLP_GROUPS, GMLP_GROUP_DIM
    u = jax.nn.gelu(u).reshape(B, nc, CHUNK, G, Dg)
    v = layer_norm(jax.nn.gelu(v).reshape(B, S, G, Dg), ln_g.reshape(G, Dg), ln_b.reshape(G, Dg))
    v = v.reshape(B, nc, CHUNK, G, Dg)
    causal = jnp.tril(jnp.ones((CHUNK, CHUNK), dtype=bool))
    w = jnp.where(causal[None], w_s, jnp.zeros_like(w_s))
    mixed = jnp.einsum('gts,bnsgc->bntgc', w, v) + b_s.T[None, None, :, :, None]
    return (u * mixed).reshape(B, S, GMLP_WIDTH)


def _normal(key, shape, scale):
    return jax.random.normal(key, shape, dtype=jnp.float32) * scale


def _fwd_setup_inputs(seed: int = 0) -> dict:
    key = jax.random.key(seed)
    ks = jax.random.split(key, 24)
    L = DEPTH
    w_in = _normal(ks[5], (L, D_MODEL, IN_WIDTH), D_MODEL ** -0.5)
    v_lo, v_hi = ATTN_WIDTH + KV_WIDTH, ATTN_WIDTH + 2 * KV_WIDTH
    w_in = w_in.at[:, :, v_lo:v_hi].multiply(BETA)
    return {
        "x": _normal(ks[0], (BATCH, SEQ, D_MODEL), 1.0),
        "c": _normal(ks[1], (BATCH, D_MODEL), 1.0),
        "rel_bias": _normal(ks[2], (N_BUCKETS, N_ATTN_HEADS), 0.5),
        "w_ada": _normal(ks[3], (L, D_MODEL, 6 * D_MODEL), 0.5 * D_MODEL ** -0.5),
        "b_ada": _normal(ks[4], (L, 6 * D_MODEL), 0.01),
        "w_in": w_in,
        "b_in": _normal(ks[6], (L, IN_WIDTH), 0.01),
        "attn_sinks": _normal(ks[7], (L, N_ATTN_HEADS), 0.5),
        "gmlp_ln_g": 1.0 + _normal(ks[8], (L, GMLP_WIDTH), 0.01),
        "gmlp_ln_b": _normal(ks[9], (L, GMLP_WIDTH), 0.01),
        "gmlp_w_s": _normal(ks[10], (L, N_GMLP_GROUPS, CHUNK, CHUNK), CHUNK ** -0.5),
        "gmlp_b_s": 1.0 + _normal(ks[11], (L, N_GMLP_GROUPS, CHUNK), 0.01),
        "attn_out_g": 1.0 + _normal(ks[12], (L, ATTN_WIDTH), 0.01),
        "gmlp_out_g": 1.0 + _normal(ks[13], (L, GMLP_WIDTH), 0.01),
        "w_out": _normal(ks[14], (L, MIX_WIDTH, D_MODEL), BETA * MIX_WIDTH ** -0.5),
        "ln1_g": 1.0 + _normal(ks[15], (L, D_MODEL), 0.01),
        "ln1_b": _normal(ks[16], (L, D_MODEL), 0.01),
        "w_gate_up": _normal(ks[17], (L, D_MODEL, 2 * D_FF), D_MODEL ** -0.5),
        "w_down": _normal(ks[18], (L, D_FF, D_MODEL), BETA * D_FF ** -0.5),
        "ln2_g": 1.0 + _normal(ks[19], (L, D_MODEL), 0.01),
        "ln2_b": _normal(ks[20], (L, D_MODEL), 0.01),
    }


def _fwd_reference(x, c, rel_bias, w_ada, b_ada, w_in, b_in, attn_sinks, gmlp_ln_g, gmlp_ln_b,
              gmlp_w_s, gmlp_b_s, attn_out_g, gmlp_out_g, w_out, ln1_g, ln1_b,
              w_gate_up, w_down, ln2_g, ln2_b):
    B, S, _ = x.shape
    splits = [ATTN_WIDTH, ATTN_WIDTH + KV_WIDTH, ATTN_WIDTH + 2 * KV_WIDTH,
              ATTN_WIDTH + 2 * KV_WIDTH + GMLP_WIDTH]
    for layer in range(DEPTH):
        mod = jax.nn.silu(c) @ w_ada[layer] + b_ada[layer]
        sh1, sc1, g1, sh2, sc2, g2 = jnp.split(mod[:, None, :], 6, axis=-1)

        h = x * (1.0 + sc1) + sh1
        proj = h @ w_in[layer] + b_in[layer]
        q, k, v, gu, gv = jnp.split(proj, splits, axis=-1)
        attn = sliding_window_attention(
            q.reshape(B, S, N_ATTN_HEADS, HEAD_DIM),
            k.reshape(B, S, N_KV_HEADS, HEAD_DIM),
            v.reshape(B, S, N_KV_HEADS, HEAD_DIM),
            attn_sinks[layer], rel_bias)
        gm = chunked_spatial_gating(gu, gv, gmlp_ln_g[layer], gmlp_ln_b[layer],
                                    gmlp_w_s[layer], gmlp_b_s[layer])
        mixed = jnp.concatenate([rms_norm(attn, attn_out_g[layer]),
                                 rms_norm(gm, gmlp_out_g[layer])], axis=-1)
        y = mixed @ w_out[layer]
        x = layer_norm(ALPHA * x + g1 * y, ln1_g[layer], ln1_b[layer])

        h = x * (1.0 + sc2) + sh2
        gate, up = jnp.split(h @ w_gate_up[layer], 2, axis=-1)
        y = (jax.nn.silu(gate) * up) @ w_down[layer]
        x = layer_norm(ALPHA * x + g2 * y, ln2_g[layer], ln2_b[layer])
    return x


import jax as _jax
import jax.numpy as _jnp

TWIN_FORMAT = 'train_step'
FWD_PARAMS = ['x', 'c', 'rel_bias', 'w_ada', 'b_ada', 'w_in', 'b_in', 'attn_sinks', 'gmlp_ln_g', 'gmlp_ln_b', 'gmlp_w_s', 'gmlp_b_s', 'attn_out_g', 'gmlp_out_g', 'w_out', 'ln1_g', 'ln1_b', 'w_gate_up', 'w_down', 'ln2_g', 'ln2_b']
TWIN_WEIGHTS = ['rel_bias', 'w_ada', 'b_ada', 'w_in', 'b_in', 'attn_sinks', 'gmlp_ln_g', 'gmlp_ln_b', 'gmlp_w_s', 'gmlp_b_s', 'attn_out_g', 'gmlp_out_g', 'w_out', 'ln1_g', 'ln1_b', 'w_gate_up', 'w_down', 'ln2_g', 'ln2_b']
TWIN_DIFF_INPUT = 'x'
TWIN_INPUTS = ['x', 'c', 'rel_bias', 'w_ada', 'b_ada', 'w_in', 'b_in', 'attn_sinks', 'gmlp_ln_g', 'gmlp_ln_b', 'gmlp_w_s', 'gmlp_b_s', 'attn_out_g', 'gmlp_out_g', 'w_out', 'ln1_g', 'ln1_b', 'w_gate_up', 'w_down', 'ln2_g', 'ln2_b', 'loss_target', 'm_rel_bias', 'm_w_ada', 'm_b_ada', 'm_w_in', 'm_b_in', 'm_attn_sinks', 'm_gmlp_ln_g', 'm_gmlp_ln_b', 'm_gmlp_w_s', 'm_gmlp_b_s', 'm_attn_out_g', 'm_gmlp_out_g', 'm_w_out', 'm_ln1_g', 'm_ln1_b', 'm_w_gate_up', 'm_w_down', 'm_ln2_g', 'm_ln2_b', 'v_rel_bias', 'v_w_ada', 'v_b_ada', 'v_w_in', 'v_b_in', 'v_attn_sinks', 'v_gmlp_ln_g', 'v_gmlp_ln_b', 'v_gmlp_w_s', 'v_gmlp_b_s', 'v_attn_out_g', 'v_gmlp_out_g', 'v_w_out', 'v_ln1_g', 'v_ln1_b', 'v_w_gate_up', 'v_w_down', 'v_ln2_g', 'v_ln2_b']
TWIN_OUTPUTS = ['loss', 'grad_x', 'grad_rel_bias', 'grad_w_ada', 'grad_b_ada', 'grad_w_in', 'grad_b_in', 'grad_attn_sinks', 'grad_gmlp_ln_g', 'grad_gmlp_ln_b', 'grad_gmlp_w_s', 'grad_gmlp_b_s', 'grad_attn_out_g', 'grad_gmlp_out_g', 'grad_w_out', 'grad_ln1_g', 'grad_ln1_b', 'grad_w_gate_up', 'grad_w_down', 'grad_ln2_g', 'grad_ln2_b', 'delta_rel_bias', 'delta_w_ada', 'delta_b_ada', 'delta_w_in', 'delta_b_in', 'delta_attn_sinks', 'delta_gmlp_ln_g', 'delta_gmlp_ln_b', 'delta_gmlp_w_s', 'delta_gmlp_b_s', 'delta_attn_out_g', 'delta_gmlp_out_g', 'delta_w_out', 'delta_ln1_g', 'delta_ln1_b', 'delta_w_gate_up', 'delta_w_down', 'delta_ln2_g', 'delta_ln2_b', 'new_m_rel_bias', 'new_m_w_ada', 'new_m_b_ada', 'new_m_w_in', 'new_m_b_in', 'new_m_attn_sinks', 'new_m_gmlp_ln_g', 'new_m_gmlp_ln_b', 'new_m_gmlp_w_s', 'new_m_gmlp_b_s', 'new_m_attn_out_g', 'new_m_gmlp_out_g', 'new_m_w_out', 'new_m_ln1_g', 'new_m_ln1_b', 'new_m_w_gate_up', 'new_m_w_down', 'new_m_ln2_g', 'new_m_ln2_b', 'new_v_rel_bias', 'new_v_w_ada', 'new_v_b_ada', 'new_v_w_in', 'new_v_b_in', 'new_v_attn_sinks', 'new_v_gmlp_ln_g', 'new_v_gmlp_ln_b', 'new_v_gmlp_w_s', 'new_v_gmlp_b_s', 'new_v_attn_out_g', 'new_v_gmlp_out_g', 'new_v_w_out', 'new_v_ln1_g', 'new_v_ln1_b', 'new_v_w_gate_up', 'new_v_w_down', 'new_v_ln2_g', 'new_v_ln2_b']
TWIN_LEAF_KINDS = {'loss': 'loss', 'grad_x': 'grad_x', 'grad_rel_bias': 'grad_w', 'grad_w_ada': 'grad_w', 'grad_b_ada': 'grad_w', 'grad_w_in': 'grad_w', 'grad_b_in': 'grad_w', 'grad_attn_sinks': 'grad_w', 'grad_gmlp_ln_g': 'grad_w', 'grad_gmlp_ln_b': 'grad_w', 'grad_gmlp_w_s': 'grad_w', 'grad_gmlp_b_s': 'grad_w', 'grad_attn_out_g': 'grad_w', 'grad_gmlp_out_g': 'grad_w', 'grad_w_out': 'grad_w', 'grad_ln1_g': 'grad_w', 'grad_ln1_b': 'grad_w', 'grad_w_gate_up': 'grad_w', 'grad_w_down': 'grad_w', 'grad_ln2_g': 'grad_w', 'grad_ln2_b': 'grad_w', 'delta_rel_bias': 'delta_w', 'delta_w_ada': 'delta_w', 'delta_b_ada': 'delta_w', 'delta_w_in': 'delta_w', 'delta_b_in': 'delta_w', 'delta_attn_sinks': 'delta_w', 'delta_gmlp_ln_g': 'delta_w', 'delta_gmlp_ln_b': 'delta_w', 'delta_gmlp_w_s': 'delta_w', 'delta_gmlp_b_s': 'delta_w', 'delta_attn_out_g': 'delta_w', 'delta_gmlp_out_g': 'delta_w', 'delta_w_out': 'delta_w', 'delta_ln1_g': 'delta_w', 'delta_ln1_b': 'delta_w', 'delta_w_gate_up': 'delta_w', 'delta_w_down': 'delta_w', 'delta_ln2_g': 'delta_w', 'delta_ln2_b': 'delta_w', 'new_m_rel_bias': 'new_m', 'new_m_w_ada': 'new_m', 'new_m_b_ada': 'new_m', 'new_m_w_in': 'new_m', 'new_m_b_in': 'new_m', 'new_m_attn_sinks': 'new_m', 'new_m_gmlp_ln_g': 'new_m', 'new_m_gmlp_ln_b': 'new_m', 'new_m_gmlp_w_s': 'new_m', 'new_m_gmlp_b_s': 'new_m', 'new_m_attn_out_g': 'new_m', 'new_m_gmlp_out_g': 'new_m', 'new_m_w_out': 'new_m', 'new_m_ln1_g': 'new_m', 'new_m_ln1_b': 'new_m', 'new_m_w_gate_up': 'new_m', 'new_m_w_down': 'new_m', 'new_m_ln2_g': 'new_m', 'new_m_ln2_b': 'new_m', 'new_v_rel_bias': 'new_v', 'new_v_w_ada': 'new_v', 'new_v_b_ada': 'new_v', 'new_v_w_in': 'new_v', 'new_v_b_in': 'new_v', 'new_v_attn_sinks': 'new_v', 'new_v_gmlp_ln_g': 'new_v', 'new_v_gmlp_ln_b': 'new_v', 'new_v_gmlp_w_s': 'new_v', 'new_v_gmlp_b_s': 'new_v', 'new_v_attn_out_g': 'new_v', 'new_v_gmlp_out_g': 'new_v', 'new_v_w_out': 'new_v', 'new_v_ln1_g': 'new_v', 'new_v_ln1_b': 'new_v', 'new_v_w_gate_up': 'new_v', 'new_v_w_down': 'new_v', 'new_v_ln2_g': 'new_v', 'new_v_ln2_b': 'new_v'}


def _forward(args):
    return _fwd_reference(*[args[k] for k in FWD_PARAMS])


def _output_shape():
    out = _jax.eval_shape(lambda: _forward(_fwd_setup_inputs(0)))
    return out.shape, out.dtype

N_MICROBATCH = 1
ADAM_LR = 0.001
ADAM_B1 = 0.9
ADAM_B2 = 0.999
ADAM_EPS = 1e-08
ADAM_WD = 0.01
ADAM_STEP = 10
PER_EXAMPLE_BATCH_AXIS = {'x': 0, 'c': 0, 'loss_target': 0}
SHARED_INPUTS = []
_WEIGHT_DTYPES = {'rel_bias': _jnp.float32, 'w_ada': _jnp.float32, 'b_ada': _jnp.float32, 'w_in': _jnp.float32, 'b_in': _jnp.float32, 'attn_sinks': _jnp.float32, 'gmlp_ln_g': _jnp.float32, 'gmlp_ln_b': _jnp.float32, 'gmlp_w_s': _jnp.float32, 'gmlp_b_s': _jnp.float32, 'attn_out_g': _jnp.float32, 'gmlp_out_g': _jnp.float32, 'w_out': _jnp.float32, 'ln1_g': _jnp.float32, 'ln1_b': _jnp.float32, 'w_gate_up': _jnp.float32, 'w_down': _jnp.float32, 'ln2_g': _jnp.float32, 'ln2_b': _jnp.float32}
MOMENT_SCALE = {'rel_bias': 2.612583e-02, 'w_ada': 4.860663e-02, 'b_ada': 7.767962e-02, 'w_in': 4.486758e-02, 'b_in': 9.760925e-02, 'attn_sinks': 8.401406e-03, 'gmlp_ln_g': 2.093924e-02, 'gmlp_ln_b': 2.193684e-02, 'gmlp_w_s': 1.523721e-02, 'gmlp_b_s': 2.206867e-02, 'attn_out_g': 3.991840e-02, 'gmlp_out_g': 4.087948e-02, 'w_out': 6.642496e-02, 'ln1_g': 7.496955e-01, 'ln1_b': 3.769671e-01, 'w_gate_up': 1.719654e-02, 'w_down': 4.755417e-02, 'ln2_g': 6.389433e+01, 'ln2_b': 2.676422e+00}


def _to_microbatches(a, axis):
    t = _jnp.moveaxis(a, axis, 0)
    t = t.reshape((N_MICROBATCH, t.shape[0] // N_MICROBATCH) + t.shape[1:])
    return _jnp.moveaxis(t, 1, axis + 1)


def setup_inputs(seed: int = 0) -> dict:
    inp = _fwd_setup_inputs(seed)
    key = _jax.random.fold_in(_jax.random.key(seed), 7919)
    shape, _ = _output_shape()
    out = dict(inp)
    out["loss_target"] = _jax.random.normal(_jax.random.fold_in(key, 0), shape, _jnp.float32)
    for i, name in enumerate(TWIN_WEIGHTS):
        w = inp[name].astype(_jnp.float32)
        if MOMENT_SCALE is None:
            s = _jnp.sqrt(_jnp.mean(_jnp.square(w)) + 1e-30)
        else:
            s = MOMENT_SCALE[name]
        km, kv = _jax.random.split(_jax.random.fold_in(key, i + 1))
        out[name] = w
        out["m_" + name] = s * _jax.random.normal(km, w.shape, _jnp.float32)
        out["v_" + name] = (s * s) * _jax.random.uniform(kv, w.shape, _jnp.float32, 0.5, 1.5)
    if N_MICROBATCH > 1:
        for name, axis in PER_EXAMPLE_BATCH_AXIS.items():
            out[name] = _to_microbatches(out[name], axis)
    return {'x': out['x'], 'c': out['c'], 'rel_bias': out['rel_bias'], 'w_ada': out['w_ada'], 'b_ada': out['b_ada'], 'w_in': out['w_in'], 'b_in': out['b_in'], 'attn_sinks': out['attn_sinks'], 'gmlp_ln_g': out['gmlp_ln_g'], 'gmlp_ln_b': out['gmlp_ln_b'], 'gmlp_w_s': out['gmlp_w_s'], 'gmlp_b_s': out['gmlp_b_s'], 'attn_out_g': out['attn_out_g'], 'gmlp_out_g': out['gmlp_out_g'], 'w_out': out['w_out'], 'ln1_g': out['ln1_g'], 'ln1_b': out['ln1_b'], 'w_gate_up': out['w_gate_up'], 'w_down': out['w_down'], 'ln2_g': out['ln2_g'], 'ln2_b': out['ln2_b'], 'loss_target': out['loss_target'], 'm_rel_bias': out['m_rel_bias'], 'm_w_ada': out['m_w_ada'], 'm_b_ada': out['m_b_ada'], 'm_w_in': out['m_w_in'], 'm_b_in': out['m_b_in'], 'm_attn_sinks': out['m_attn_sinks'], 'm_gmlp_ln_g': out['m_gmlp_ln_g'], 'm_gmlp_ln_b': out['m_gmlp_ln_b'], 'm_gmlp_w_s': out['m_gmlp_w_s'], 'm_gmlp_b_s': out['m_gmlp_b_s'], 'm_attn_out_g': out['m_attn_out_g'], 'm_gmlp_out_g': out['m_gmlp_out_g'], 'm_w_out': out['m_w_out'], 'm_ln1_g': out['m_ln1_g'], 'm_ln1_b': out['m_ln1_b'], 'm_w_gate_up': out['m_w_gate_up'], 'm_w_down': out['m_w_down'], 'm_ln2_g': out['m_ln2_g'], 'm_ln2_b': out['m_ln2_b'], 'v_rel_bias': out['v_rel_bias'], 'v_w_ada': out['v_w_ada'], 'v_b_ada': out['v_b_ada'], 'v_w_in': out['v_w_in'], 'v_b_in': out['v_b_in'], 'v_attn_sinks': out['v_attn_sinks'], 'v_gmlp_ln_g': out['v_gmlp_ln_g'], 'v_gmlp_ln_b': out['v_gmlp_ln_b'], 'v_gmlp_w_s': out['v_gmlp_w_s'], 'v_gmlp_b_s': out['v_gmlp_b_s'], 'v_attn_out_g': out['v_attn_out_g'], 'v_gmlp_out_g': out['v_gmlp_out_g'], 'v_w_out': out['v_w_out'], 'v_ln1_g': out['v_ln1_g'], 'v_ln1_b': out['v_ln1_b'], 'v_w_gate_up': out['v_w_gate_up'], 'v_w_down': out['v_w_down'], 'v_ln2_g': out['v_ln2_g'], 'v_ln2_b': out['v_ln2_b']}


def _loss(weights, diff, rest, loss_target):
    with _jax.named_scope("forward"):
        args = {**rest, TWIN_DIFF_INPUT: diff, **{k: w.astype(_WEIGHT_DTYPES[k]) for k, w in weights.items()}}
        y = _forward(args)
    with _jax.named_scope("loss_head"):
        err = _jnp.square(y.astype(_jnp.float32) - loss_target)
        return 0.5 * _jnp.sum(_jnp.mean(err, axis=-1)) if err.ndim else 0.5 * err


def _adamw(w, g, m, v):
    m = ADAM_B1 * m + (1.0 - ADAM_B1) * g
    v = ADAM_B2 * v + (1.0 - ADAM_B2) * _jnp.square(g)
    m_hat = m / (1.0 - ADAM_B1 ** ADAM_STEP)
    v_hat = v / (1.0 - ADAM_B2 ** ADAM_STEP)
    delta = -ADAM_LR * (m_hat / (_jnp.sqrt(v_hat) + ADAM_EPS) + ADAM_WD * w)
    return delta, m, v


def reference(x, c, rel_bias, w_ada, b_ada, w_in, b_in, attn_sinks, gmlp_ln_g, gmlp_ln_b, gmlp_w_s, gmlp_b_s, attn_out_g, gmlp_out_g, w_out, ln1_g, ln1_b, w_gate_up, w_down, ln2_g, ln2_b, loss_target, m_rel_bias, m_w_ada, m_b_ada, m_w_in, m_b_in, m_attn_sinks, m_gmlp_ln_g, m_gmlp_ln_b, m_gmlp_w_s, m_gmlp_b_s, m_attn_out_g, m_gmlp_out_g, m_w_out, m_ln1_g, m_ln1_b, m_w_gate_up, m_w_down, m_ln2_g, m_ln2_b, v_rel_bias, v_w_ada, v_b_ada, v_w_in, v_b_in, v_attn_sinks, v_gmlp_ln_g, v_gmlp_ln_b, v_gmlp_w_s, v_gmlp_b_s, v_attn_out_g, v_gmlp_out_g, v_w_out, v_ln1_g, v_ln1_b, v_w_gate_up, v_w_down, v_ln2_g, v_ln2_b):
    given = dict(x=x, c=c, rel_bias=rel_bias, w_ada=w_ada, b_ada=b_ada, w_in=w_in, b_in=b_in, attn_sinks=attn_sinks, gmlp_ln_g=gmlp_ln_g, gmlp_ln_b=gmlp_ln_b, gmlp_w_s=gmlp_w_s, gmlp_b_s=gmlp_b_s, attn_out_g=attn_out_g, gmlp_out_g=gmlp_out_g, w_out=w_out, ln1_g=ln1_g, ln1_b=ln1_b, w_gate_up=w_gate_up, w_down=w_down, ln2_g=ln2_g, ln2_b=ln2_b, loss_target=loss_target, m_rel_bias=m_rel_bias, m_w_ada=m_w_ada, m_b_ada=m_b_ada, m_w_in=m_w_in, m_b_in=m_b_in, m_attn_sinks=m_attn_sinks, m_gmlp_ln_g=m_gmlp_ln_g, m_gmlp_ln_b=m_gmlp_ln_b, m_gmlp_w_s=m_gmlp_w_s, m_gmlp_b_s=m_gmlp_b_s, m_attn_out_g=m_attn_out_g, m_gmlp_out_g=m_gmlp_out_g, m_w_out=m_w_out, m_ln1_g=m_ln1_g, m_ln1_b=m_ln1_b, m_w_gate_up=m_w_gate_up, m_w_down=m_w_down, m_ln2_g=m_ln2_g, m_ln2_b=m_ln2_b, v_rel_bias=v_rel_bias, v_w_ada=v_w_ada, v_b_ada=v_b_ada, v_w_in=v_w_in, v_b_in=v_b_in, v_attn_sinks=v_attn_sinks, v_gmlp_ln_g=v_gmlp_ln_g, v_gmlp_ln_b=v_gmlp_ln_b, v_gmlp_w_s=v_gmlp_w_s, v_gmlp_b_s=v_gmlp_b_s, v_attn_out_g=v_attn_out_g, v_gmlp_out_g=v_gmlp_out_g, v_w_out=v_w_out, v_ln1_g=v_ln1_g, v_ln1_b=v_ln1_b, v_w_gate_up=v_w_gate_up, v_w_down=v_w_down, v_ln2_g=v_ln2_g, v_ln2_b=v_ln2_b)
    weights = {n: given[n] for n in TWIN_WEIGHTS}
    shared = {n: given[n] for n in SHARED_INPUTS}
    per_example = {n: given[n] for n in ['x', 'c']}
    grad_fn = _jax.value_and_grad(_loss, argnums=(0, 1))

    def one_microbatch(ex, loss_target):
        ex = dict(ex)
        diff = ex.pop(TWIN_DIFF_INPUT)
        return grad_fn(weights, diff, {**shared, **ex}, loss_target)

    if N_MICROBATCH == 1:
        loss, (grad_w, grad_x) = one_microbatch(per_example, given["loss_target"])
    else:
        def body(carry, xs):
            loss_sum, grad_sum = carry
            l_k, (gw_k, gx_k) = one_microbatch(xs[0], xs[1])
            with _jax.named_scope("update"):
                return (loss_sum + l_k, _jax.tree.map(_jnp.add, grad_sum, gw_k)), gx_k

        init = (_jnp.zeros((), _jnp.float32), _jax.tree.map(_jnp.zeros_like, weights))
        (loss, grad_w), grad_x = _jax.lax.scan(body, init, (per_example, given["loss_target"]))
    with _jax.named_scope("update"):
        delta_w, new_m, new_v = {}, {}, {}
        for n in TWIN_WEIGHTS:
            delta_w[n], new_m[n], new_v[n] = _adamw(weights[n], grad_w[n], given["m_" + n], given["v_" + n])
    return (loss, grad_x, *[grad_w[n] for n in TWIN_WEIGHTS], *[delta_w[n] for n in TWIN_WEIGHTS],
            *[new_m[n] for n in TWIN_WEIGHTS], *[new_v[n] for n in TWIN_WEIGHTS])
```

```python
import functools
import math

import numpy as np
import jax
import jax.numpy as jnp
from jax import lax
from jax.experimental import pallas as pl
from jax.experimental.pallas import tpu as pltpu

F32 = jnp.float32
BF16 = jnp.bfloat16
MESH = pl.DeviceIdType.MESH

D_MODEL = 1024
N_HEADS = 8
N_KV = 2
HEAD_DIM = 64
ATTN_W = N_HEADS * HEAD_DIM
KV_W = N_KV * HEAD_DIM
N_GROUPS = 8
GROUP_DIM = 64
GMLP_W = N_GROUPS * GROUP_DIM
IN_W = ATTN_W + 2 * KV_W + 2 * GMLP_W
BLOCK = 128
N_BUCKETS = 32
MAX_DISTANCE = 128
D_FF = 2816
ALPHA = 2.0 ** 0.25
LN_EPS = 1e-5
NEG_INF = -1e30
ADAM_LR, ADAM_B1, ADAM_B2, ADAM_EPS, ADAM_WD, ADAM_STEP = 0.001, 0.9, 0.999, 1e-8, 0.01, 10
N_CHIPS = 4
N_DEV = 8
LANES = 128
V7X_VMEM_LIMIT = 56 * 2 ** 20
GELU_C = math.sqrt(2.0 / math.pi)
Q_SCALE = HEAD_DIM ** -0.5


def _params(sem=None):
    return pltpu.CompilerParams(dimension_semantics=sem, vmem_limit_bytes=V7X_VMEM_LIMIT)


def _const_spec(shape, single=False):
    nd = len(shape)
    if single:
        return pl.BlockSpec(shape, lambda *_: (0,) * nd, pipeline_mode=pl.Buffered(1))
    return pl.BlockSpec(shape, lambda *_: (0,) * nd)


def _dot(a, b):
    return jnp.dot(a, b, preferred_element_type=F32)


def _dot_nt(a, b):
    return lax.dot_general(a, b, (((1,), (1,)), ((), ())), preferred_element_type=F32)


def _dot_tn(a, b):
    return lax.dot_general(a, b, (((0,), (0,)), ((), ())), preferred_element_type=F32)


def _gelu(x):
    t = jnp.tanh(GELU_C * (x + 0.044715 * x * x * x))
    return 0.5 * x * (1.0 + t), t


def _gelu_grad(x, t):
    return 0.5 * (1.0 + t) + 0.5 * x * (1.0 - t * t) * GELU_C * (1.0 + 3.0 * 0.044715 * x * x)


def _split_dot(x, a):
    hi = x.astype(BF16)
    lo = (x - hi.astype(F32)).astype(BF16)
    return _dot(hi, a) + _dot(lo, a)


def _group_mean_matrix():
    g = np.arange(GMLP_W) // GROUP_DIM
    return jnp.asarray((g[:, None] == g[None, :]).astype(np.float32) / GROUP_DIM, dtype=BF16)


def _ln_stats(z):
    mu = jnp.mean(z, axis=-1, keepdims=True)
    d = z - mu
    var = jnp.mean(d * d, axis=-1, keepdims=True)
    rstd = lax.rsqrt(var + LN_EPS)
    return d * rstd, rstd


def _ln_bwd(dxhat, xhat, rstd):
    m1 = jnp.mean(dxhat, axis=-1, keepdims=True)
    m2 = jnp.mean(dxhat * xhat, axis=-1, keepdims=True)
    return rstd * (dxhat - m1 - xhat * m2)


def _colsum(x):
    return jnp.sum(x, axis=0, keepdims=True)


def _my_pos():
    return lax.axis_index("x"), lax.axis_index("y"), lax.axis_index("c")


def _other_chips(x, y):
    return [(1 - x, y), (x, 1 - y), (1 - x, 1 - y)]


def _allgather8(v, name):
    m_per, n = v.shape

    def body(x_ref, out_ref, send_sems, recv_sems, local_sem):
        x, y, c = _my_pos()
        me, sibling = (x, y, c), (x, y, 1 - c)
        chips = _other_chips(x, y)

        def rows(px, py, pc):
            return out_ref.at[pl.ds((4 * px + 2 * py + pc) * m_per, m_per), :]

        def copy(k, block, to, src=None):
            return pltpu.make_async_remote_copy(
                src_ref=rows(*block) if src is None else src, dst_ref=rows(*block),
                send_sem=send_sems.at[k], recv_sem=recv_sems.at[k], device_id=to, device_id_type=MESH)

        mine = pltpu.make_async_copy(x_ref, rows(*me), local_sem)
        mine.start()
        first = [copy(0, me, sibling, src=x_ref)]
        first += [copy(1 + j, me, (*chip, c), src=x_ref) for j, chip in enumerate(chips)]
        for cp in first:
            cp.start()
        passed = [copy(4 + j, (*chip, c), sibling) for j, chip in enumerate(chips)]
        for j, chip in enumerate(chips):
            copy(1 + j, (*chip, c), me).wait_recv()
            passed[j].start()
        copy(0, sibling, me).wait_recv()
        for j, chip in enumerate(chips):
            copy(4 + j, (*chip, 1 - c), me).wait_recv()
        for cp in first + passed:
            cp.wait_send()
        mine.wait()

    return pl.pallas_call(
        body, name=name,
        out_shape=jax.ShapeDtypeStruct((N_DEV * m_per, n), v.dtype),
        in_specs=[pl.BlockSpec(memory_space=pltpu.VMEM)],
        out_specs=pl.BlockSpec(memory_space=pltpu.VMEM),
        scratch_shapes=[pltpu.SemaphoreType.DMA((7,)), pltpu.SemaphoreType.DMA((7,)), pltpu.SemaphoreType.DMA],
        compiler_params=pltpu.CompilerParams(vmem_limit_bytes=V7X_VMEM_LIMIT),
    )(v)


def _weights_allgather(w_in_s, w_out_s, w_gu_s, w_dn_s):
    n_arr = 4

    def body(in_s, out_s, gu_s, dn_s, in_g, out_g, gu_g, dn_g, send_sems, recv_sems, local_sems):
        x, y, c = _my_pos()
        my_chip = 2 * x + y
        sibling = (x, y, 1 - c)
        chips = _other_chips(x, y)
        srcs = (in_s, out_s, gu_s, dn_s)

        def shard_half(a, pc):
            h = srcs[a].shape[0] // 2
            return srcs[a].at[pl.ds(pc * h, h), :]

        def dst_half(a, chip, pc):
            h = srcs[a].shape[0] // 2
            if a == 0:
                return in_g.at[chip, pl.ds(pc * h, h), :]
            if a == 1:
                return out_g.at[pl.ds(chip * 2 * h + pc * h, h), :]
            if a == 2:
                return gu_g.at[pl.ds(pc * h, h), pl.ds(chip * gu_s.shape[1], gu_s.shape[1])]
            return dn_g.at[pl.ds(chip * 2 * h + pc * h, h), :]

        def dst_full(a, chip):
            if a == 0:
                return in_g.at[chip]
            if a == 1:
                return out_g.at[pl.ds(chip * out_s.shape[0], out_s.shape[0]), :]
            if a == 2:
                return gu_g.at[:, pl.ds(chip * gu_s.shape[1], gu_s.shape[1])]
            return dn_g.at[pl.ds(chip * dn_s.shape[0], dn_s.shape[0]), :]

        def copy(a, k, chip, pc, to, src=None):
            d = dst_half(a, chip, pc)
            return pltpu.make_async_remote_copy(
                src_ref=d if src is None else src, dst_ref=d,
                send_sem=send_sems.at[a * 6 + k], recv_sem=recv_sems.at[a * 6 + k], device_id=to, device_id_type=MESH)

        local = [pltpu.make_async_copy(srcs[a], dst_full(a, my_chip), local_sems.at[a]) for a in range(n_arr)]
        for cp in local:
            cp.start()
        first = []
        for a in range(n_arr):
            for j, chip in enumerate(chips):
                cp = copy(a, j, my_chip, c, (*chip, c), src=shard_half(a, c))
                cp.start()
                first.append(cp)
        passed = []
        for a in range(n_arr):
            for j, chip in enumerate(chips):
                cj = 2 * chip[0] + chip[1]
                copy(a, j, cj, c, (x, y, c)).wait_recv()
                cp = copy(a, 3 + j, cj, c, sibling)
                cp.start()
                passed.append(cp)
        for a in range(n_arr):
            for j, chip in enumerate(chips):
                cj = 2 * chip[0] + chip[1]
                copy(a, 3 + j, cj, 1 - c, (x, y, c)).wait_recv()
        for cp in first + passed:
            cp.wait_send()
        for cp in local:
            cp.wait()

    any_spec = pl.BlockSpec(memory_space=pl.ANY)
    k_in, n_in = w_in_s.shape
    outs = (jax.ShapeDtypeStruct((N_CHIPS, k_in, n_in), BF16),
            jax.ShapeDtypeStruct((N_CHIPS * w_out_s.shape[0], w_out_s.shape[1]), BF16),
            jax.ShapeDtypeStruct((w_gu_s.shape[0], N_CHIPS * w_gu_s.shape[1]), BF16),
            jax.ShapeDtypeStruct((N_CHIPS * w_dn_s.shape[0], w_dn_s.shape[1]), BF16))
    return pl.pallas_call(
        body, name="weights_allgather", out_shape=outs,
        in_specs=[any_spec] * 4, out_specs=[any_spec] * 4,
        scratch_shapes=[pltpu.SemaphoreType.DMA((n_arr * 6,)), pltpu.SemaphoreType.DMA((n_arr * 6,)),
                        pltpu.SemaphoreType.DMA((n_arr,))],
    )(w_in_s, w_out_s, w_gu_s, w_dn_s)


def _swap_halves_to_sibling(arrs, name):
    n_arr = len(arrs)

    def half(ref, pc):
        if len(ref.shape) == 3:
            h = ref.shape[1] // 2
            return ref.at[:, pl.ds(pc * h, h), :]
        h = ref.shape[0] // 2
        return ref.at[pl.ds(pc * h, h), :]

    def body(*refs):
        ins, outs = refs[:n_arr], refs[n_arr:2 * n_arr]
        send_sems, recv_sems = refs[2 * n_arr:]
        x, y, c = _my_pos()
        cps = []
        for a in range(n_arr):
            cp = pltpu.make_async_remote_copy(
                src_ref=half(ins[a], 1 - c), dst_ref=outs[a], send_sem=send_sems.at[a], recv_sem=recv_sems.at[a],
                device_id=(x, y, 1 - c), device_id_type=MESH)
            cp.start()
            cps.append(cp)
        for cp in cps:
            cp.wait()

    def half_shape(s):
        return (s[0], s[1] // 2, s[2]) if len(s) == 3 else (s[0] // 2, s[1])

    any_spec = pl.BlockSpec(memory_space=pl.ANY)
    return pl.pallas_call(
        body, name=name,
        out_shape=[jax.ShapeDtypeStruct(half_shape(a.shape), a.dtype) for a in arrs],
        in_specs=[any_spec] * n_arr, out_specs=[any_spec] * n_arr,
        scratch_shapes=[pltpu.SemaphoreType.DMA((n_arr,)), pltpu.SemaphoreType.DMA((n_arr,))],
    )(*arrs)


def _add_halves(full, got, name):
    nb, r, cc = full.shape
    h = r // 2
    c = lax.axis_index("c")

    def body(c_ref, a_ref, b_ref, o_ref):
        o_ref[...] = a_ref[...] + b_ref[...]

    return pl.pallas_call(
        body, name=name, out_shape=jax.ShapeDtypeStruct((nb, h, cc), F32),
        grid_spec=pltpu.PrefetchScalarGridSpec(
            num_scalar_prefetch=1, grid=(nb,),
            in_specs=[pl.BlockSpec((1, h, cc), lambda b, c_ref: (b, c_ref[0], 0)),
                      pl.BlockSpec((1, h, cc), lambda b, c_ref: (b, 0, 0))],
            out_specs=pl.BlockSpec((1, h, cc), lambda b, c_ref: (b, 0, 0))),
        compiler_params=_params(("arbitrary",)),
    )(jnp.reshape(c, (1,)).astype(jnp.int32), full, got)


def _chip_partials_exchange(parts, name):
    n_arr = len(parts)

    def body(*refs):
        ins, outs = refs[:n_arr], refs[n_arr:2 * n_arr]
        send_sems, recv_sems, local_sems = refs[2 * n_arr:]
        x, y, c = _my_pos()
        my_chip = 2 * x + y
        chips = _other_chips(x, y)
        local = [pltpu.make_async_copy(ins[a].at[my_chip], outs[a].at[my_chip], local_sems.at[a]) for a in range(n_arr)]
        for cp in local:
            cp.start()
        sends = []
        for a in range(n_arr):
            for j, chip in enumerate(chips):
                cj = 2 * chip[0] + chip[1]
                cp = pltpu.make_async_remote_copy(
                    src_ref=ins[a].at[cj], dst_ref=outs[a].at[my_chip],
                    send_sem=send_sems.at[a * 3 + j], recv_sem=recv_sems.at[a * 3 + j],
                    device_id=(*chip, c), device_id_type=MESH)
                cp.start()
                sends.append(cp)
        for a in range(n_arr):
            for j, chip in enumerate(chips):
                cj = 2 * chip[0] + chip[1]
                pltpu.make_async_remote_copy(
                    src_ref=ins[a].at[cj], dst_ref=outs[a].at[cj],
                    send_sem=send_sems.at[a * 3 + j], recv_sem=recv_sems.at[a * 3 + j],
                    device_id=(*chip, c), device_id_type=MESH).wait_recv()
        for cp in sends:
            cp.wait_send()
        for cp in local:
            cp.wait()

    any_spec = pl.BlockSpec(memory_space=pl.ANY)
    return pl.pallas_call(
        body, name=name, out_shape=[jax.ShapeDtypeStruct(p.shape, p.dtype) for p in parts],
        in_specs=[any_spec] * n_arr, out_specs=[any_spec] * n_arr,
        scratch_shapes=[pltpu.SemaphoreType.DMA((n_arr * 3,)), pltpu.SemaphoreType.DMA((n_arr * 3,)),
                        pltpu.SemaphoreType.DMA((n_arr,))],
    )(*parts)


def _sum_chips(got, name):
    _, r, cc = got.shape

    def body(g_ref, o_ref):
        o_ref[...] = ((g_ref[0] + g_ref[1]) + g_ref[2]) + g_ref[3]

    return pl.pallas_call(
        body, name=name, out_shape=jax.ShapeDtypeStruct((r, cc), F32),
        grid=(1,), in_specs=[pl.BlockSpec((N_CHIPS, r, cc), lambda i: (0, 0, 0))],
        out_specs=pl.BlockSpec((r, cc), lambda i: (0, 0)),
        compiler_params=_params(("arbitrary",)),
    )(got)


def _share_halves(halves, name):
    n_arr = len(halves)

    def body(*refs):
        ins, outs = refs[:n_arr], refs[n_arr:2 * n_arr]
        send_sems, recv_sems, local_sems = refs[2 * n_arr:]
        x, y, c = _my_pos()
        cps = []
        for a in range(n_arr):
            lc = pltpu.make_async_copy(ins[a], outs[a].at[c], local_sems.at[a])
            lc.start()
            cp = pltpu.make_async_remote_copy(
                src_ref=ins[a], dst_ref=outs[a].at[c], send_sem=send_sems.at[a], recv_sem=recv_sems.at[a],
                device_id=(x, y, 1 - c), device_id_type=MESH)
            cp.start()
            cps.append((lc, cp))
        for lc, cp in cps:
            cp.wait()
            lc.wait()

    any_spec = pl.BlockSpec(memory_space=pl.ANY)
    return pl.pallas_call(
        body, name=name, out_shape=[jax.ShapeDtypeStruct((2,) + h.shape, h.dtype) for h in halves],
        in_specs=[any_spec] * n_arr, out_specs=[any_spec] * n_arr,
        scratch_shapes=[pltpu.SemaphoreType.DMA((n_arr,)), pltpu.SemaphoreType.DMA((n_arr,)),
                        pltpu.SemaphoreType.DMA((n_arr,))],
    )(*halves)


def _mod_part(c_all, w_ada_s, b_ada_s):
    n = w_ada_s.shape[1]

    def body(c_ref, w_ref, b_ref, sc_ref, mod_ref):
        cv = c_ref[...]
        sc = cv * (1.0 / (1.0 + jnp.exp(-cv)))
        sc_ref[...] = sc
        a_hi = sc.astype(BF16)
        a_lo = (sc - a_hi.astype(F32)).astype(BF16)
        w = w_ref[...]
        w_hi = w.astype(BF16)
        w_lo = (w - w_hi.astype(F32)).astype(BF16)
        mod_ref[...] = _dot(a_hi, w_hi) + _dot(a_hi, w_lo) + _dot(a_lo, w_hi) + b_ref[...]

    return pl.pallas_call(
        body, name="mod_part",
        out_shape=(jax.ShapeDtypeStruct((N_DEV, D_MODEL), F32), jax.ShapeDtypeStruct((N_DEV, n), F32)),
        grid=(1,),
        in_specs=[_const_spec((N_DEV, D_MODEL)), _const_spec((D_MODEL, n)), _const_spec((1, n))],
        out_specs=(_const_spec((N_DEV, D_MODEL)), _const_spec((N_DEV, n))),
        compiler_params=_params(("arbitrary",)),
    )(c_all, w_ada_s, b_ada_s)


def _bucket_table():
    qi = jnp.arange(BLOCK)[:, None]
    si = jnp.arange(2 * BLOCK)[None, :]
    dist = qi + BLOCK - si
    max_exact = N_BUCKETS // 2
    n = jnp.maximum(dist, 0)
    nf = jnp.maximum(n, max_exact).astype(F32)
    large = max_exact + (jnp.log(nf / max_exact) / math.log(MAX_DISTANCE / max_exact)
                         * (N_BUCKETS - max_exact)).astype(jnp.int32)
    large = jnp.minimum(large, N_BUCKETS - 1)
    return jnp.where(n < max_exact, n, large).astype(F32)


def _prep_tables(bucket, rel_bias, w_s):
    def body(bucket_ref, rb_ref, ws_ref, bias_ref, wsm_ref):
        qi = lax.broadcasted_iota(jnp.int32, (BLOCK, 2 * BLOCK), 0)
        si = lax.broadcasted_iota(jnp.int32, (BLOCK, 2 * BLOCK), 1)
        dist = qi + BLOCK - si
        in_window = (dist >= 0) & (dist < BLOCK)
        bk = bucket_ref[...]
        for h in range(N_HEADS):
            acc = jnp.zeros((BLOCK, 2 * BLOCK), F32)
            for b in range(N_BUCKETS):
                acc = jnp.where(bk == float(b), rb_ref[b, h], acc)
            bias_ref[h] = jnp.where(in_window, acc, NEG_INF)
        ti = lax.broadcasted_iota(jnp.int32, (BLOCK, BLOCK), 0)
        ui = lax.broadcasted_iota(jnp.int32, (BLOCK, BLOCK), 1)
        for g in range(N_GROUPS):
            wsm_ref[g] = jnp.where(ti >= ui, ws_ref[g], 0.0).astype(BF16)

    return pl.pallas_call(
        body, name="prep_tables",
        out_shape=(jax.ShapeDtypeStruct((N_HEADS, BLOCK, 2 * BLOCK), F32),
                   jax.ShapeDtypeStruct((N_GROUPS, BLOCK, BLOCK), BF16)),
        grid=(1,),
        in_specs=[_const_spec((BLOCK, 2 * BLOCK)), pl.BlockSpec(memory_space=pltpu.SMEM),
                  _const_spec((N_GROUPS, BLOCK, BLOCK))],
        out_specs=(_const_spec((N_HEADS, BLOCK, 2 * BLOCK)), _const_spec((N_GROUPS, BLOCK, BLOCK))),
        compiler_params=_params(("arbitrary",)),
    )(bucket, rel_bias, w_s)


def _fwd_in(x, modr, w_in, b_in, tm):
    s = x.shape[0]

    def body(x_ref, mod_ref, w_ref, b_ref, h1_ref, q_ref, kv_ref, gu_ref, gv_ref):
        h1 = (x_ref[...] * (1.0 + mod_ref[1:2, :]) + mod_ref[0:1, :]).astype(BF16)
        h1_ref[...] = h1
        proj = _dot(h1, w_ref[...]) + b_ref[...]
        q_ref[...] = (proj[:, :ATTN_W] * Q_SCALE).astype(BF16)
        kv_ref[...] = proj[:, ATTN_W:ATTN_W + 2 * KV_W].astype(BF16)
        gu_ref[...] = proj[:, ATTN_W + 2 * KV_W:ATTN_W + 2 * KV_W + GMLP_W]
        gv_ref[...] = proj[:, ATTN_W + 2 * KV_W + GMLP_W:]

    row = lambda w: pl.BlockSpec((tm, w), lambda i: (i, 0))
    return pl.pallas_call(
        body, name="fwd_in",
        out_shape=(jax.ShapeDtypeStruct((s, D_MODEL), BF16), jax.ShapeDtypeStruct((s, ATTN_W), BF16),
                   jax.ShapeDtypeStruct((s, 2 * KV_W), BF16), jax.ShapeDtypeStruct((s, GMLP_W), F32),
                   jax.ShapeDtypeStruct((s, GMLP_W), F32)),
        grid=(s // tm,),
        in_specs=[row(D_MODEL), _const_spec((8, D_MODEL)), _const_spec((D_MODEL, IN_W)), _const_spec((1, IN_W))],
        out_specs=(row(D_MODEL), row(ATTN_W), row(2 * KV_W), row(GMLP_W), row(GMLP_W)),
        compiler_params=_params(("parallel",)),
    )(x, modr, w_in, b_in)


def _kv_variants(kk):
    kf = kk.astype(F32)
    lane = lax.broadcasted_iota(jnp.int32, kf.shape, 1)
    low = lane < HEAD_DIM
    k0_lo = jnp.where(low, kf, 0.0)
    k1_hi = jnp.where(low, 0.0, kf)
    k0_hi = pltpu.roll(k0_lo, HEAD_DIM, 1)
    k1_lo = pltpu.roll(k1_hi, HEAD_DIM, 1)
    return ((k0_lo.astype(BF16), k0_hi.astype(BF16)), (k1_lo.astype(BF16), k1_hi.astype(BF16)))


def _attn_head_probs(q_pair, k_var, bias_h, sink, first_mask):
    logits = _dot_nt(q_pair, k_var) + bias_h
    if first_mask is not None:
        logits = jnp.where(first_mask, NEG_INF, logits)
    m = jnp.maximum(jnp.max(logits, axis=-1, keepdims=True), sink)
    e = jnp.exp(logits - m)
    es = jnp.exp(sink - m)
    inv = 1.0 / (jnp.sum(e, axis=-1, keepdims=True) + es)
    return e * inv, es * inv


def _attn_block_fwd(q_blk, kk, vv, bias_ref, sinks_ref, first_mask):
    kvar = _kv_variants(kk)
    vvar = _kv_variants(vv)
    outs, probs = [], []
    for pair in range(N_HEADS // 2):
        acc = None
        for par in range(2):
            h = 2 * pair + par
            kv = h // (N_HEADS // N_KV)
            p, ps = _attn_head_probs(q_blk[:, pair * LANES:(pair + 1) * LANES], kvar[kv][par], bias_ref[h],
                                     sinks_ref[h], first_mask)
            probs.append((p, ps))
            o = _dot(p.astype(BF16), vvar[kv][par])
            acc = o if acc is None else acc + o
        outs.append(acc)
    return jnp.concatenate(outs, axis=1), probs, kvar, vvar


def _gmlp_chunk_fwd(gu, gv, ln_g, ln_b, wsm_ref, bsx, amat):
    u, tu = _gelu(gu)
    a, ta = _gelu(gv)
    mean = _split_dot(a, amat)
    d = a - mean
    var = _split_dot(d * d, amat)
    rstd = lax.rsqrt(var + LN_EPS)
    xhat = d * rstd
    vb = (xhat * ln_g + ln_b).astype(BF16)
    lane = lax.broadcasted_iota(jnp.int32, (BLOCK, LANES), 1)
    low = lane < GROUP_DIM
    cols = []
    for pair in range(N_GROUPS // 2):
        vp = vb[:, pair * LANES:(pair + 1) * LANES]
        cols.append(jnp.where(low, _dot(wsm_ref[2 * pair], vp), _dot(wsm_ref[2 * pair + 1], vp)))
    mixedv = jnp.concatenate(cols, axis=1) + bsx
    return u * mixedv, (u, tu, ta, xhat, rstd, vb, mixedv)


def _rms(a, g):
    r = lax.rsqrt(jnp.mean(a * a, axis=-1, keepdims=True) + LN_EPS)
    return a * r * g, r


def _fwd_mix(q, kv, gu, gv, x, modr, bias, sinks, gln_g, gln_b, wsm, bsx, amat, aog, gog, w_out, ln1_g, ln1_b, tm):
    s = x.shape[0]
    nb = tm // BLOCK

    def body(q_ref, kv_ref, kvp_ref, gu_ref, gv_ref, x_ref, mod_ref, bias_ref, sinks_ref, glng_ref, glnb_ref, wsm_ref,
             bsx_ref, amat_ref, aog_ref, gog_ref, wout_ref, ln1g_ref, ln1b_ref, x1_ref, y_ref, mixed_ref, mix_scr):
        i = pl.program_id(0)
        col = lax.broadcasted_iota(jnp.int32, (BLOCK, 2 * BLOCK), 1)
        for b in range(nb):
            r0 = b * BLOCK
            if b == 0:
                kvprev = kvp_ref[...]
                first_mask = (col < BLOCK) & (i == 0)
            else:
                kvprev = kv_ref[r0 - BLOCK:r0, :]
                first_mask = None
            kvcur = kv_ref[r0:r0 + BLOCK, :]
            kk = jnp.concatenate([kvprev[:, :KV_W], kvcur[:, :KV_W]], axis=0)
            vv = jnp.concatenate([kvprev[:, KV_W:], kvcur[:, KV_W:]], axis=0)
            attn, _, _, _ = _attn_block_fwd(q_ref[r0:r0 + BLOCK, :], kk, vv, bias_ref, sinks_ref, first_mask)
            na, _ = _rms(attn, aog_ref[...])
            gm, _ = _gmlp_chunk_fwd(gu_ref[r0:r0 + BLOCK, :], gv_ref[r0:r0 + BLOCK, :], glng_ref[...], glnb_ref[...],
                                    wsm_ref, bsx_ref[...], amat_ref[...])
            ng, _ = _rms(gm, gog_ref[...])
            mix_scr[r0:r0 + BLOCK, :ATTN_W] = na.astype(BF16)
            mix_scr[r0:r0 + BLOCK, ATTN_W:] = ng.astype(BF16)
        mixed = mix_scr[...]
        mixed_ref[...] = mixed
        y = _dot(mixed, wout_ref[...])
        y_ref[...] = y
        z1 = ALPHA * x_ref[...] + mod_ref[2:3, :] * y
        xhat, _ = _ln_stats(z1)
        x1_ref[...] = xhat * ln1g_ref[...] + ln1b_ref[...]

    row = lambda w: pl.BlockSpec((tm, w), lambda i: (i, 0))
    prev = pl.BlockSpec((BLOCK, 2 * KV_W), lambda i: (jnp.maximum(i * nb - 1, 0), 0))
    return pl.pallas_call(
        body, name="fwd_mix",
        out_shape=(jax.ShapeDtypeStruct((s, D_MODEL), F32), jax.ShapeDtypeStruct((s, D_MODEL), F32),
                   jax.ShapeDtypeStruct((s, D_MODEL), BF16)),
        grid=(s // tm,),
        in_specs=[row(ATTN_W), row(2 * KV_W), prev, row(GMLP_W), row(GMLP_W), row(D_MODEL), _const_spec((8, D_MODEL)),
                  _const_spec((N_HEADS, BLOCK, 2 * BLOCK)), pl.BlockSpec(memory_space=pltpu.SMEM),
                  _const_spec((1, GMLP_W)), _const_spec((1, GMLP_W)), _const_spec((N_GROUPS, BLOCK, BLOCK)),
                  _const_spec((BLOCK, GMLP_W)), _const_spec((GMLP_W, GMLP_W)), _const_spec((1, ATTN_W)),
                  _const_spec((1, GMLP_W)), _const_spec((D_MODEL, D_MODEL)), _const_spec((1, D_MODEL)),
                  _const_spec((1, D_MODEL))],
        out_specs=(row(D_MODEL), row(D_MODEL), row(D_MODEL)),
        scratch_shapes=[pltpu.VMEM((tm, D_MODEL), BF16)],
        compiler_params=_params(("parallel",)),
    )(q, kv, kv, gu, gv, x, modr, bias, sinks, gln_g, gln_b, wsm, bsx, amat, aog, gog, w_out, ln1_g, ln1_b)


FF_CHUNK = D_FF // 2


def _sigmoid(x):
    return 1.0 / (1.0 + jnp.exp(-x))


def _fwd_ffn(x1, target, modr, ln2_g, ln2_b, w_gu, w_dn, tm):
    s = x1.shape[0]

    def body(x1_ref, t_ref, mod_ref, g_ref, b_ref, wgu_ref, wdn_ref, h2_ref, act_ref, dy2_ref, dx1a_ref, acc_ref):
        i = pl.program_id(0)

        @pl.when(i == 0)
        def _():
            acc_ref[...] = jnp.zeros_like(acc_ref)

        x1v = x1_ref[...]
        h2 = (x1v * (1.0 + mod_ref[4:5, :]) + mod_ref[3:4, :]).astype(BF16)
        h2_ref[...] = h2
        y2 = jnp.zeros((tm, D_MODEL), F32)
        for cc in range(D_FF // FF_CHUNK):
            c0 = cc * FF_CHUNK
            gate = _dot(h2, wgu_ref[:, c0:c0 + FF_CHUNK])
            up = _dot(h2, wgu_ref[:, D_FF + c0:D_FF + c0 + FF_CHUNK])
            act_ref[:, c0:c0 + FF_CHUNK] = gate.astype(BF16)
            act_ref[:, D_FF + c0:D_FF + c0 + FF_CHUNK] = up.astype(BF16)
            a = (gate * _sigmoid(gate) * up).astype(BF16)
            y2 = y2 + _dot(a, wdn_ref[c0:c0 + FF_CHUNK, :])
        g2 = mod_ref[5:6, :]
        z2 = ALPHA * x1v + g2 * y2
        xhat, rstd = _ln_stats(z2)
        gain = g_ref[...]
        diff = xhat * gain + b_ref[...] - t_ref[...]
        dx2 = diff * (1.0 / D_MODEL)
        dz2 = _ln_bwd(dx2 * gain, xhat, rstd)
        dx1a_ref[...] = ALPHA * dz2
        dy2_ref[...] = (g2 * dz2).astype(BF16)
        acc_ref[0:1, :] += _colsum(diff * diff)
        acc_ref[1:2, :] += _colsum(dx2 * xhat)
        acc_ref[2:3, :] += _colsum(dx2)
        acc_ref[3:4, :] += _colsum(dz2 * y2)

    row = lambda w: pl.BlockSpec((tm, w), lambda i: (i, 0))
    return pl.pallas_call(
        body, name="fwd_ffn",
        out_shape=(jax.ShapeDtypeStruct((s, D_MODEL), BF16), jax.ShapeDtypeStruct((s, 2 * D_FF), BF16),
                   jax.ShapeDtypeStruct((s, D_MODEL), BF16), jax.ShapeDtypeStruct((s, D_MODEL), F32),
                   jax.ShapeDtypeStruct((8, D_MODEL), F32)),
        grid=(s // tm,),
        in_specs=[row(D_MODEL), row(D_MODEL), _const_spec((8, D_MODEL)), _const_spec((1, D_MODEL)),
                  _const_spec((1, D_MODEL)), _const_spec((D_MODEL, 2 * D_FF), single=True),
                  _const_spec((D_FF, D_MODEL), single=True)],
        out_specs=(row(D_MODEL), row(2 * D_FF), row(D_MODEL), row(D_MODEL), _const_spec((8, D_MODEL))),
        compiler_params=_params(("arbitrary",)),
    )(x1, target, modr, ln2_g, ln2_b, w_gu, w_dn)


def _bwd_ffn(dy2, act, w_gu, w_dn, tm):
    s = dy2.shape[0]

    def body(dy2_ref, act_ref, wgu_ref, wdn_ref, a_ref, dgu_ref, dh2_ref):
        dy2v = dy2_ref[...]
        dh2 = jnp.zeros((tm, D_MODEL), F32)
        for cc in range(D_FF // FF_CHUNK):
            c0 = cc * FF_CHUNK
            da = _dot_nt(dy2v, wdn_ref[c0:c0 + FF_CHUNK, :])
            gate = act_ref[:, c0:c0 + FF_CHUNK].astype(F32)
            up = act_ref[:, D_FF + c0:D_FF + c0 + FF_CHUNK].astype(F32)
            sg = _sigmoid(gate)
            sl = gate * sg
            a_ref[:, c0:c0 + FF_CHUNK] = (sl * up).astype(BF16)
            dgate = (da * up * (sg * (1.0 + gate * (1.0 - sg)))).astype(BF16)
            dup = (da * sl).astype(BF16)
            dgu_ref[:, c0:c0 + FF_CHUNK] = dgate
            dgu_ref[:, D_FF + c0:D_FF + c0 + FF_CHUNK] = dup
            dh2 = dh2 + _dot_nt(dgate, wgu_ref[:, c0:c0 + FF_CHUNK])
            dh2 = dh2 + _dot_nt(dup, wgu_ref[:, D_FF + c0:D_FF + c0 + FF_CHUNK])
        dh2_ref[...] = dh2

    row = lambda w: pl.BlockSpec((tm, w), lambda i: (i, 0))
    return pl.pallas_call(
        body, name="bwd_ffn",
        out_shape=(jax.ShapeDtypeStruct((s, D_FF), BF16), jax.ShapeDtypeStruct((s, 2 * D_FF), BF16),
                   jax.ShapeDtypeStruct((s, D_MODEL), F32)),
        grid=(s // tm,),
        in_specs=[row(D_MODEL), row(2 * D_FF), _const_spec((D_MODEL, 2 * D_FF), single=True),
                  _const_spec((D_FF, D_MODEL), single=True)],
        out_specs=(row(D_FF), row(2 * D_FF), row(D_MODEL)),
        compiler_params=_params(("parallel",)),
    )(dy2, act, w_gu, w_dn)


def _bwd_mid(dh2, dx1a, x1, x, y, modr, ln1_g, w_out, tm):
    s = x.shape[0]

    def body(dh2_ref, dx1a_ref, x1_ref, x_ref, y_ref, mod_ref, g_ref, wout_ref, dxa_ref, dy_ref, dmix_ref, acc_ref):
        i = pl.program_id(0)

        @pl.when(i == 0)
        def _():
            acc_ref[...] = jnp.zeros_like(acc_ref)

        dh2 = dh2_ref[...]
        x1v = x1_ref[...]
        yv = y_ref[...]
        g1 = mod_ref[2:3, :]
        dx1 = dx1a_ref[...] + dh2 * (1.0 + mod_ref[4:5, :])
        z1 = ALPHA * x_ref[...] + g1 * yv
        xhat, rstd = _ln_stats(z1)
        dz1 = _ln_bwd(dx1 * g_ref[...], xhat, rstd)
        dxa_ref[...] = ALPHA * dz1
        dy = (g1 * dz1).astype(BF16)
        dy_ref[...] = dy
        dmix_ref[...] = _dot_nt(dy, wout_ref[...])
        acc_ref[0:1, :] += _colsum(dh2 * x1v)
        acc_ref[1:2, :] += _colsum(dh2)
        acc_ref[2:3, :] += _colsum(dx1 * xhat)
        acc_ref[3:4, :] += _colsum(dx1)
        acc_ref[4:5, :] += _colsum(dz1 * yv)

    row = lambda w: pl.BlockSpec((tm, w), lambda i: (i, 0))
    return pl.pallas_call(
        body, name="bwd_mid",
        out_shape=(jax.ShapeDtypeStruct((s, D_MODEL), F32), jax.ShapeDtypeStruct((s, D_MODEL), BF16),
                   jax.ShapeDtypeStruct((s, D_MODEL), F32), jax.ShapeDtypeStruct((8, D_MODEL), F32)),
        grid=(s // tm,),
        in_specs=[row(D_MODEL)] * 5 + [_const_spec((8, D_MODEL)), _const_spec((1, D_MODEL)),
                                       _const_spec((D_MODEL, D_MODEL))],
        out_specs=(row(D_MODEL), row(D_MODEL), row(D_MODEL), _const_spec((8, D_MODEL))),
        compiler_params=_params(("arbitrary",)),
    )(dh2, dx1a, x1, x, y, modr, ln1_g, w_out)


def _fold_kv(t0, t1):
    lane = lax.broadcasted_iota(jnp.int32, t0.shape, 1)
    f0 = t0 + pltpu.roll(t0, HEAD_DIM, 1)
    f1 = t1 + pltpu.roll(t1, HEAD_DIM, 1)
    return jnp.where(lane < HEAD_DIM, f0, f1)


def _bwd_mix(q, kv, gu, gv, dmix, bias, sinks, gln_g, gln_b, wsm, bsx, amat, aog, gog):
    s = q.shape[0]
    nblk = s // BLOCK

    def body(q_ref, kv_ref, kvp_ref, gu_ref, gv_ref, dmix_ref, bias_ref, sinks_ref, glng_ref, glnb_ref, wsm_ref,
             bsx_ref, amat_ref, aog_ref, gog_ref,
             dq_ref, dkv_ref, dgu_ref, dgv_ref, gbias_ref, dws_ref, dbs_ref, vec_ref, dsink_ref, carry):
        n = pl.program_id(0)

        @pl.when(n == 0)
        def _():
            carry[...] = jnp.zeros_like(carry)
            gbias_ref[...] = jnp.zeros_like(gbias_ref)
            dws_ref[...] = jnp.zeros_like(dws_ref)
            dbs_ref[...] = jnp.zeros_like(dbs_ref)
            vec_ref[...] = jnp.zeros_like(vec_ref)
            dsink_ref[...] = jnp.zeros_like(dsink_ref)

        @pl.when(n == nblk)
        def _():
            dkv_ref[...] = carry[...].astype(BF16)

        @pl.when(n < nblk)
        def _():
            col = lax.broadcasted_iota(jnp.int32, (BLOCK, 2 * BLOCK), 1)
            lane = lax.broadcasted_iota(jnp.int32, (BLOCK, LANES), 1)
            low = lane < HEAD_DIM
            first_mask = (col < BLOCK) & (n == 0)
            kvprev = kvp_ref[...]
            kvcur = kv_ref[...]
            kk = jnp.concatenate([kvprev[:, :KV_W], kvcur[:, :KV_W]], axis=0)
            vv = jnp.concatenate([kvprev[:, KV_W:], kvcur[:, KV_W:]], axis=0)
            q_blk = q_ref[...]
            attn, probs, kvar, vvar = _attn_block_fwd(q_blk, kk, vv, bias_ref, sinks_ref, first_mask)
            aog_v = aog_ref[...]
            na_unit, r_a = _rms(attn, 1.0)
            gm, (u, tu, ta, xhat, rstd, vb, mixedv) = _gmlp_chunk_fwd(
                gu_ref[...], gv_ref[...], glng_ref[...], glnb_ref[...], wsm_ref, bsx_ref[...], amat_ref[...])
            gog_v = gog_ref[...]
            ng_unit, r_g = _rms(gm, 1.0)

            dmix = dmix_ref[...]
            dn_a = dmix[:, :ATTN_W]
            dn_g = dmix[:, ATTN_W:]
            vec_ref[0:1, :] += _colsum(dn_a * na_unit)
            vec_ref[1:2, :] += _colsum(dn_g * ng_unit)
            t_a = dn_a * aog_v
            d_attn = r_a * t_a - na_unit * (r_a * jnp.mean(t_a * na_unit, axis=-1, keepdims=True))
            t_g = dn_g * gog_v
            d_gm = r_g * t_g - ng_unit * (r_g * jnp.mean(t_g * ng_unit, axis=-1, keepdims=True))

            gu_v = gu_ref[...]
            dgu_ref[...] = (d_gm * mixedv * _gelu_grad(gu_v, tu)).astype(BF16)
            dmx = d_gm * u
            dbs_ref[...] += dmx
            dmxb = dmx.astype(BF16)
            dvn_cols = []
            for pair in range(N_GROUPS // 2):
                dp_ = dmxb[:, pair * LANES:(pair + 1) * LANES]
                vp = vb[:, pair * LANES:(pair + 1) * LANES]
                dvn_cols.append(jnp.where(low, _dot_tn(wsm_ref[2 * pair], dp_), _dot_tn(wsm_ref[2 * pair + 1], dp_)))
                zero = jnp.zeros_like(dp_)
                dws_ref[2 * pair] += _dot_nt(jnp.where(low, dp_, zero), vp)
                dws_ref[2 * pair + 1] += _dot_nt(jnp.where(low, zero, dp_), vp)
            dvn = jnp.concatenate(dvn_cols, axis=1)
            vec_ref[2:3, :] += _colsum(dvn * xhat)
            vec_ref[3:4, :] += _colsum(dvn)
            dxh = dvn * glng_ref[...]
            am = amat_ref[...]
            da = rstd * (dxh - _split_dot(dxh, am) - xhat * _split_dot(dxh * xhat, am))
            dgv_ref[...] = (da * _gelu_grad(gv_ref[...], ta)).astype(BF16)

            tk = [jnp.zeros((2 * BLOCK, LANES), F32) for _ in range(N_KV)]
            tv = [jnp.zeros((2 * BLOCK, LANES), F32) for _ in range(N_KV)]
            dq_cols = []
            for pair in range(N_HEADS // 2):
                d_pair = d_attn[:, pair * LANES:(pair + 1) * LANES]
                q_pair = q_blk[:, pair * LANES:(pair + 1) * LANES]
                dq_pair = None
                for par in range(2):
                    h = 2 * pair + par
                    kvh = h // (N_HEADS // N_KV)
                    p, ps = probs[h]
                    sel = low if par == 0 else jnp.logical_not(low)
                    do_h = jnp.where(sel, d_pair, 0.0).astype(BF16)
                    q_h = jnp.where(sel, q_pair, jnp.zeros_like(q_pair))
                    dp = _dot_nt(do_h, vvar[kvh][par])
                    delta = jnp.sum(p * dp, axis=-1, keepdims=True)
                    ds = p * (dp - delta)
                    dsink_ref[h] += -(ps * delta)
                    gbias_ref[h] += ds
                    dsb = ds.astype(BF16)
                    dqh = _dot(dsb, kvar[kvh][par])
                    dq_pair = dqh if dq_pair is None else dq_pair + dqh
                    tk[kvh] = tk[kvh] + _dot_tn(dsb, q_h)
                    tv[kvh] = tv[kvh] + _dot_tn(p.astype(BF16), do_h)
                dq_cols.append(dq_pair)
            dq_ref[...] = (jnp.concatenate(dq_cols, axis=1) * Q_SCALE).astype(BF16)
            dkk = _fold_kv(tk[0], tk[1])
            dvv = _fold_kv(tv[0], tv[1])
            dkv_ref[...] = (carry[...] + jnp.concatenate([dkk[:BLOCK], dvv[:BLOCK]], axis=1)).astype(BF16)
            carry[...] = jnp.concatenate([dkk[BLOCK:], dvv[BLOCK:]], axis=1)

    last = nblk - 1
    cur = lambda w: pl.BlockSpec((BLOCK, w), lambda n: (jnp.minimum(n, last), 0))
    prev = lambda w: pl.BlockSpec((BLOCK, w), lambda n: (jnp.clip(n - 1, 0, last), 0))
    return pl.pallas_call(
        body, name="bwd_mix",
        out_shape=(jax.ShapeDtypeStruct((s, ATTN_W), BF16), jax.ShapeDtypeStruct((s, 2 * KV_W), BF16),
                   jax.ShapeDtypeStruct((s, GMLP_W), BF16), jax.ShapeDtypeStruct((s, GMLP_W), BF16),
                   jax.ShapeDtypeStruct((N_HEADS, BLOCK, 2 * BLOCK), F32),
                   jax.ShapeDtypeStruct((N_GROUPS, BLOCK, BLOCK), F32),
                   jax.ShapeDtypeStruct((BLOCK, GMLP_W), F32), jax.ShapeDtypeStruct((8, GMLP_W), F32),
                   jax.ShapeDtypeStruct((N_HEADS, BLOCK, 1), F32)),
        grid=(nblk + 1,),
        in_specs=[cur(ATTN_W), cur(2 * KV_W), prev(2 * KV_W), cur(GMLP_W), cur(GMLP_W), cur(D_MODEL),
                  _const_spec((N_HEADS, BLOCK, 2 * BLOCK)), pl.BlockSpec(memory_space=pltpu.SMEM),
                  _const_spec((1, GMLP_W)), _const_spec((1, GMLP_W)), _const_spec((N_GROUPS, BLOCK, BLOCK)),
                  _const_spec((BLOCK, GMLP_W)), _const_spec((GMLP_W, GMLP_W)), _const_spec((1, ATTN_W)),
                  _const_spec((1, GMLP_W))],
        out_specs=(cur(ATTN_W), prev(2 * KV_W), cur(GMLP_W), cur(GMLP_W),
                   _const_spec((N_HEADS, BLOCK, 2 * BLOCK)), _const_spec((N_GROUPS, BLOCK, BLOCK)),
                   _const_spec((BLOCK, GMLP_W)), _const_spec((8, GMLP_W)), _const_spec((N_HEADS, BLOCK, 1))),
        scratch_shapes=[pltpu.VMEM((BLOCK, 2 * KV_W), F32)],
        compiler_params=_params(("arbitrary",)),
    )(q, kv, kv, gu, gv, dmix, bias, sinks, gln_g, gln_b, wsm, bsx, amat, aog, gog)


def _mix_finalize(gbias, bucket, dws, dbs, dsink):
    def body(gb_ref, bucket_ref, dws_ref, dbs_ref, dsink_ref, drb_ref, dwsm_ref, dbsg_ref, dsk_ref):
        bk = bucket_ref[...]
        lane = lax.broadcasted_iota(jnp.int32, (N_BUCKETS, LANES), 1)
        rowi = lax.broadcasted_iota(jnp.int32, (N_BUCKETS, LANES), 0)
        drb = jnp.zeros((N_BUCKETS, LANES), F32)
        dsk = jnp.zeros((8, LANES), F32)
        lane8 = lax.broadcasted_iota(jnp.int32, (8, LANES), 1)
        for h in range(N_HEADS):
            g = gb_ref[h]
            for b in range(N_BUCKETS):
                tot = jnp.sum(_colsum(jnp.where(bk == float(b), g, 0.0)), axis=1, keepdims=True)
                drb = jnp.where((lane == h) & (rowi == b), tot, drb)
            sk = jnp.sum(dsink_ref[h], axis=0, keepdims=True)
            dsk = jnp.where(lane8 == h, sk, dsk)
        drb_ref[...] = drb
        dsk_ref[...] = dsk
        ti = lax.broadcasted_iota(jnp.int32, (BLOCK, BLOCK), 0)
        ui = lax.broadcasted_iota(jnp.int32, (BLOCK, BLOCK), 1)
        for g in range(N_GROUPS):
            dwsm_ref[g] = jnp.where(ti >= ui, dws_ref[g], 0.0)
        gi = lax.broadcasted_iota(jnp.int32, (GMLP_W, LANES), 0) // GROUP_DIM
        li = lax.broadcasted_iota(jnp.int32, (GMLP_W, LANES), 1)
        ind = jnp.where(gi == li, 1.0, 0.0).astype(BF16)
        d = dbs_ref[...]
        hi = d.astype(BF16)
        r1 = d - hi.astype(F32)
        mid = r1.astype(BF16)
        lo = (r1 - mid.astype(F32)).astype(BF16)
        dbsg_ref[...] = _dot(hi, ind) + _dot(mid, ind) + _dot(lo, ind)

    return pl.pallas_call(
        body, name="mix_finalize",
        out_shape=(jax.ShapeDtypeStruct((N_BUCKETS, LANES), F32), jax.ShapeDtypeStruct((N_GROUPS, BLOCK, BLOCK), F32),
                   jax.ShapeDtypeStruct((BLOCK, LANES), F32), jax.ShapeDtypeStruct((8, LANES), F32)),
        grid=(1,),
        in_specs=[_const_spec((N_HEADS, BLOCK, 2 * BLOCK)), _const_spec((BLOCK, 2 * BLOCK)),
                  _const_spec((N_GROUPS, BLOCK, BLOCK)), _const_spec((BLOCK, GMLP_W)),
                  _const_spec((N_HEADS, BLOCK, 1))],
        out_specs=(_const_spec((N_BUCKETS, LANES)), _const_spec((N_GROUPS, BLOCK, BLOCK)),
                   _const_spec((BLOCK, LANES)), _const_spec((8, LANES))),
        compiler_params=_params(("arbitrary",)),
    )(gbias, bucket, dws, dbs, dsink)


def _bwd_in(dq, dkv, dgu, dgv, dxa, x, modr, w_in, tm):
    s = x.shape[0]

    def body(dq_ref, dkv_ref, dgu_ref, dgv_ref, dxa_ref, x_ref, mod_ref, w_ref, dproj_ref, gx_ref, acc_ref, db_ref):
        i = pl.program_id(0)

        @pl.when(i == 0)
        def _():
            acc_ref[...] = jnp.zeros_like(acc_ref)
            db_ref[...] = jnp.zeros_like(db_ref)

        dproj = jnp.concatenate([dq_ref[...], dkv_ref[...], dgu_ref[...], dgv_ref[...]], axis=1)
        dproj_ref[...] = dproj
        dh1 = _dot_nt(dproj, w_ref[...])
        gx_ref[...] = dxa_ref[...] + dh1 * (1.0 + mod_ref[1:2, :])
        acc_ref[0:1, :] += _colsum(dh1 * x_ref[...])
        acc_ref[1:2, :] += _colsum(dh1)
        db_ref[0:1, :] += _colsum(dproj.astype(F32))

    row = lambda w: pl.BlockSpec((tm, w), lambda i: (i, 0))
    return pl.pallas_call(
        body, name="bwd_in",
        out_shape=(jax.ShapeDtypeStruct((s, IN_W), BF16), jax.ShapeDtypeStruct((s, D_MODEL), F32),
                   jax.ShapeDtypeStruct((8, D_MODEL), F32), jax.ShapeDtypeStruct((8, IN_W), F32)),
        grid=(s // tm,),
        in_specs=[row(ATTN_W), row(2 * KV_W), row(GMLP_W), row(GMLP_W), row(D_MODEL), row(D_MODEL),
                  _const_spec((8, D_MODEL)), _const_spec((D_MODEL, IN_W))],
        out_specs=(row(IN_W), row(D_MODEL), _const_spec((8, D_MODEL)), _const_spec((8, IN_W))),
        compiler_params=_params(("arbitrary",)),
    )(dq, dkv, dgu, dgv, dxa, x, modr, w_in)


def _wgrad(a, b, tm, tk, name):
    k_all, m = a.shape
    n = b.shape[1]

    def body(a_ref, b_ref, o_ref):
        @pl.when(pl.program_id(1) == 0)
        def _():
            o_ref[...] = jnp.zeros_like(o_ref)

        o_ref[...] += _dot_tn(a_ref[...], b_ref[...])

    return pl.pallas_call(
        body, name=name, out_shape=jax.ShapeDtypeStruct((m, n), F32),
        grid=(m // tm, k_all // tk),
        in_specs=[pl.BlockSpec((tk, tm), lambda i, k: (k, i)), pl.BlockSpec((tk, n), lambda i, k: (k, 0))],
        out_specs=pl.BlockSpec((tm, n), lambda i, k: (i, 0)),
        compiler_params=_params(("parallel", "arbitrary")),
    )(a, b)


def _adam_math(w, g, m, v):
    m2 = ADAM_B1 * m + (1.0 - ADAM_B1) * g
    v2 = ADAM_B2 * v + (1.0 - ADAM_B2) * (g * g)
    m_hat = m2 / (1.0 - ADAM_B1 ** ADAM_STEP)
    v_hat = v2 / (1.0 - ADAM_B2 ** ADAM_STEP)
    delta = -ADAM_LR * (m_hat / (jnp.sqrt(v_hat) + ADAM_EPS) + ADAM_WD * w)
    return delta, m2, v2


def _adam(w, g, m, v, tr, name):
    r, cc = w.shape

    def body(w_ref, g_ref, m_ref, v_ref, d_ref, m2_ref, v2_ref):
        d, m2, v2 = _adam_math(w_ref[...], g_ref[...], m_ref[...], v_ref[...])
        d_ref[...] = d
        m2_ref[...] = m2
        v2_ref[...] = v2

    spec = pl.BlockSpec((tr, cc), lambda i: (i, 0))
    shp = jax.ShapeDtypeStruct((r, cc), F32)
    return pl.pallas_call(
        body, name=name, out_shape=(shp, shp, shp), grid=(r // tr,),
        in_specs=[spec] * 4, out_specs=(spec, spec, spec), compiler_params=_params(("parallel",)),
    )(w, g, m, v)


def _adam_w_ada(sc_t, dmod_cols, w, m, v, tr):
    r, cc = w.shape

    def body(sct_ref, dm_ref, w_ref, m_ref, v_ref, g_ref, d_ref, m2_ref, v2_ref):
        g = sct_ref[:, 0:1] * dm_ref[0:1, :]
        for k in range(1, N_DEV):
            g = g + sct_ref[:, k:k + 1] * dm_ref[k:k + 1, :]
        g_ref[...] = g
        d, m2, v2 = _adam_math(w_ref[...], g, m_ref[...], v_ref[...])
        d_ref[...] = d
        m2_ref[...] = m2
        v2_ref[...] = v2

    spec = pl.BlockSpec((tr, cc), lambda i: (i, 0))
    shp = jax.ShapeDtypeStruct((r, cc), F32)
    return pl.pallas_call(
        body, name="adam_w_ada", out_shape=(shp, shp, shp, shp), grid=(r // tr,),
        in_specs=[pl.BlockSpec((tr, N_DEV), lambda i: (i, 0)), _const_spec((N_DEV, cc)), spec, spec, spec],
        out_specs=(spec, spec, spec, spec), compiler_params=_params(("parallel",)),
    )(sc_t, dmod_cols, w, m, v)


def _adam_small(gathered, w, m, v, loss_rows):
    _, r, _ = gathered.shape
    lo, hi = loss_rows

    def body(ga_ref, w_ref, m_ref, v_ref, g_ref, d_ref, m2_ref, v2_ref, loss_ref):
        g = ga_ref[0]
        for k in range(1, N_DEV):
            g = g + ga_ref[k]
        g_ref[...] = g
        d, m2, v2 = _adam_math(w_ref[...], g, m_ref[...], v_ref[...])
        d_ref[...] = d
        m2_ref[...] = m2
        v2_ref[...] = v2
        tot = jnp.sum(_colsum(g[lo:hi, :]), axis=1, keepdims=True)
        loss_ref[...] = jnp.broadcast_to(tot * (0.5 / D_MODEL), loss_ref.shape)

    shp = jax.ShapeDtypeStruct((r, LANES), F32)
    spec = _const_spec((r, LANES))
    return pl.pallas_call(
        body, name="adam_small", out_shape=(shp, shp, shp, shp, jax.ShapeDtypeStruct((8, LANES), F32)), grid=(1,),
        in_specs=[_const_spec((N_DEV, r, LANES)), spec, spec, spec],
        out_specs=(spec, spec, spec, spec, _const_spec((8, LANES))),
        compiler_params=_params(("arbitrary",)),
    )(gathered, w, m, v)


SMALL_NAMES = ("loss", "rel_bias", "b_ada", "b_in", "attn_sinks", "gmlp_ln_g", "gmlp_ln_b", "gmlp_w_s", "gmlp_b_s",
               "attn_out_g", "gmlp_out_g", "ln1_g", "ln1_b", "ln2_g", "ln2_b")
PACK_TILE = 8 * LANES


def _pack(items):
    parts, layout, off = [], {}, 0
    for name in SMALL_NAMES:
        a = items[name]
        flat = a.reshape(-1).astype(F32)
        n = flat.shape[0]
        padded = -(-n // PACK_TILE) * PACK_TILE
        parts.append(jnp.pad(flat, (0, padded - n)))
        layout[name] = (off // LANES, padded // LANES, a.shape, n)
        off += padded
    return jnp.concatenate(parts).reshape(-1, LANES), layout


def _unpack(buf, layout, name):
    r0, nr, shape, n = layout[name]
    return buf[r0:r0 + nr].reshape(-1)[:n].reshape(shape)


def kernel(x, c, rel_bias, w_ada, b_ada, w_in, b_in, attn_sinks, gmlp_ln_g, gmlp_ln_b, gmlp_w_s, gmlp_b_s, attn_out_g, gmlp_out_g, w_out, ln1_g, ln1_b, w_gate_up, w_down, ln2_g, ln2_b, loss_target, m_rel_bias, m_w_ada, m_b_ada, m_w_in, m_b_in, m_attn_sinks, m_gmlp_ln_g, m_gmlp_ln_b, m_gmlp_w_s, m_gmlp_b_s, m_attn_out_g, m_gmlp_out_g, m_w_out, m_ln1_g, m_ln1_b, m_w_gate_up, m_w_down, m_ln2_g, m_ln2_b, v_rel_bias, v_w_ada, v_b_ada, v_w_in, v_b_in, v_attn_sinks, v_gmlp_ln_g, v_gmlp_ln_b, v_gmlp_w_s, v_gmlp_b_s, v_attn_out_g, v_gmlp_out_g, v_w_out, v_ln1_g, v_ln1_b, v_w_gate_up, v_w_down, v_ln2_g, v_ln2_b):
    ix, iy, ic = _my_pos()
    chip = 2 * ix + iy
    dev = 4 * ix + 2 * iy + ic
    s = x.shape[1]
    xs = x[0]
    tgt = loss_target[0]
    tm_big = min(512, s)
    tm_ffn = min(256, s)

    c_all = _allgather8(jnp.pad(c, ((0, 7), (0, 0))), "gather_c").reshape(N_DEV, 8, D_MODEL)[:, 0, :]
    n_ada = w_ada.shape[2]
    sc_all, mod_cols = _mod_part(c_all, w_ada[0], lax.dynamic_slice_in_dim(b_ada, chip * n_ada, n_ada, axis=1))
    mod_all = _allgather8(mod_cols, "gather_mod").reshape(N_DEV, N_DEV, -1)
    mod_row = lax.dynamic_index_in_dim(mod_all[0::2], dev, axis=1, keepdims=False)
    modr = jnp.pad(mod_row.reshape(6, D_MODEL), ((0, 2), (0, 0)))

    w_in_g, w_out_f, w_gu_f, w_dn_f = _weights_allgather(
        w_in[0].astype(BF16), w_out[0].astype(BF16), w_gate_up[0].astype(BF16), w_down[0].astype(BF16))
    w_in_f = jnp.transpose(w_in_g, (1, 0, 2)).reshape(D_MODEL, IN_W)

    bucket = _bucket_table()
    bias, wsm = _prep_tables(bucket, rel_bias, gmlp_w_s[0])
    bsx = jnp.repeat(gmlp_b_s[0].T, GROUP_DIM, axis=1)
    amat = _group_mean_matrix()
    sinks = attn_sinks[0]

    h1, q, kv, gu, gv = _fwd_in(xs, modr, w_in_f, b_in, tm_big)
    x1, y, mixed = _fwd_mix(q, kv, gu, gv, xs, modr, bias, sinks, gmlp_ln_g, gmlp_ln_b, wsm, bsx, amat,
                            attn_out_g, gmlp_out_g, w_out_f, ln1_g, ln1_b, tm_big)
    h2, act, dy2, dx1a, acc_f = _fwd_ffn(x1, tgt, modr, ln2_g, ln2_b, w_gu_f, w_dn_f, tm_ffn)

    a_act, dgu_ff, dh2 = _bwd_ffn(dy2, act, w_gu_f, w_dn_f, tm_ffn)
    dxa, dy, dmix, acc_m = _bwd_mid(dh2, dx1a, x1, xs, y, modr, ln1_g, w_out_f, tm_big)
    dq, dkv, dgu, dgv, gbias, dws, dbs, vec, dsink = _bwd_mix(
        q, kv, gu, gv, dmix, bias, sinks, gmlp_ln_g, gmlp_ln_b, wsm, bsx, amat, attn_out_g, gmlp_out_g)
    drb, dwsm, dbsg, dsk = _mix_finalize(gbias, bucket, dws, dbs, dsink)
    dproj, grad_x, acc_i, db_in = _bwd_in(dq, dkv, dgu, dgv, dxa, xs, modr, w_in_f, tm_big)

    g_dn = _wgrad(a_act, dy2, D_FF // 2, min(512, s), "wgrad_down")
    g_gu = _wgrad(h2, dgu_ff, 512, min(256, s), "wgrad_gate_up")
    g_out = _wgrad(mixed, dy, 512, min(512, s), "wgrad_out")
    g_in = _wgrad(h1, dproj, 512, min(512, s), "wgrad_in")

    dmod = jnp.concatenate([acc_i[1], acc_i[0], acc_m[4], acc_m[1], acc_m[0], acc_f[3]])
    small_g = {
        "loss": acc_f[0], "rel_bias": drb[:, :N_HEADS], "b_ada": dmod.reshape(1, -1), "b_in": db_in[0:1],
        "attn_sinks": dsk[0:1, :N_HEADS], "gmlp_ln_g": vec[2:3], "gmlp_ln_b": vec[3:4], "gmlp_w_s": dwsm[None],
        "gmlp_b_s": dbsg[:, :N_GROUPS].T[None], "attn_out_g": vec[0:1], "gmlp_out_g": vec[1:2],
        "ln1_g": acc_m[2:3], "ln1_b": acc_m[3:4], "ln2_g": acc_f[1:2], "ln2_b": acc_f[2:3]}
    zero_loss = jnp.zeros((D_MODEL,), F32)
    small_w = {"loss": zero_loss, "rel_bias": rel_bias, "b_ada": b_ada, "b_in": b_in, "attn_sinks": attn_sinks,
               "gmlp_ln_g": gmlp_ln_g, "gmlp_ln_b": gmlp_ln_b, "gmlp_w_s": gmlp_w_s, "gmlp_b_s": gmlp_b_s,
               "attn_out_g": attn_out_g, "gmlp_out_g": gmlp_out_g, "ln1_g": ln1_g, "ln1_b": ln1_b, "ln2_g": ln2_g,
               "ln2_b": ln2_b}
    small_m = {"loss": zero_loss, "rel_bias": m_rel_bias, "b_ada": m_b_ada, "b_in": m_b_in, "attn_sinks": m_attn_sinks,
               "gmlp_ln_g": m_gmlp_ln_g, "gmlp_ln_b": m_gmlp_ln_b, "gmlp_w_s": m_gmlp_w_s, "gmlp_b_s": m_gmlp_b_s,
               "attn_out_g": m_attn_out_g, "gmlp_out_g": m_gmlp_out_g, "ln1_g": m_ln1_g, "ln1_b": m_ln1_b,
               "ln2_g": m_ln2_g, "ln2_b": m_ln2_b}
    small_v = {"loss": zero_loss + 1.0, "rel_bias": v_rel_bias, "b_ada": v_b_ada, "b_in": v_b_in,
               "attn_sinks": v_attn_sinks, "gmlp_ln_g": v_gmlp_ln_g, "gmlp_ln_b": v_gmlp_ln_b, "gmlp_w_s": v_gmlp_w_s,
               "gmlp_b_s": v_gmlp_b_s, "attn_out_g": v_attn_out_g, "gmlp_out_g": v_gmlp_out_g, "ln1_g": v_ln1_g,
               "ln1_b": v_ln1_b, "ln2_g": v_ln2_g, "ln2_b": v_ln2_b}
    pg, layout = _pack(small_g)
    pw, _ = _pack(small_w)
    pm, _ = _pack(small_m)
    pv, _ = _pack(small_v)
    rows = pg.shape[0]
    gathered = _allgather8(pg, "gather_small").reshape(N_DEV, rows, LANES)
    l0, ln_, _, _ = layout["loss"]
    sg, sd, sm, sv, loss_t = _adam_small(gathered, pw, pm, pv, (l0, l0 + ln_))
    loss = loss_t[0, 0]

    b0, bn, _, _ = layout["b_ada"]
    dmod_all = gathered[:, b0:b0 + bn, :].reshape(N_DEV, -1)[:, :6 * D_MODEL]
    dmod_cols = lax.dynamic_slice_in_dim(dmod_all, chip * n_ada, n_ada, axis=1)
    g_ada, d_ada, m_ada, v_ada = _adam_w_ada(sc_all.T, dmod_cols, w_ada[0], m_w_ada[0], v_w_ada[0], 256)

    n_in = w_in.shape[2]
    g_in_b = jnp.transpose(g_in.reshape(D_MODEL, N_CHIPS, n_in), (1, 0, 2))
    g_out_b = g_out.reshape(N_CHIPS, D_MODEL // N_CHIPS, D_MODEL)
    n_gu = w_gate_up.shape[2]
    g_gu_b = jnp.transpose(g_gu.reshape(D_MODEL, N_CHIPS, n_gu), (1, 0, 2))
    g_dn_b = g_dn.reshape(N_CHIPS, D_FF // N_CHIPS, D_MODEL)
    fulls = [g_in_b, g_out_b, g_gu_b, g_dn_b]
    gots = _swap_halves_to_sibling(fulls, "rs_swap")
    parts = [_add_halves(f, g, "rs_add_%d" % i) for i, (f, g) in enumerate(zip(fulls, gots))]
    chip_parts = _chip_partials_exchange(parts, "rs_chips")
    halves = [_sum_chips(p, "rs_sum_%d" % i) for i, p in enumerate(chip_parts)]
    shared = _share_halves(halves, "rs_share")
    gs_in, gs_out, gs_gu, gs_dn = [sh.reshape(sh.shape[0] * sh.shape[1], sh.shape[2]) for sh in shared]

    d_in, m_in, v_in = _adam(w_in[0], gs_in, m_w_in[0], v_w_in[0], 256, "adam_w_in")
    d_out, m_out, v_out = _adam(w_out[0], gs_out, m_w_out[0], v_w_out[0], 128, "adam_w_out")
    d_gu, m_gu, v_gu = _adam(w_gate_up[0], gs_gu, m_w_gate_up[0], v_w_gate_up[0], 256, "adam_w_gate_up")
    d_dn, m_dn, v_dn = _adam(w_down[0], gs_dn, m_w_down[0], v_w_down[0], 176, "adam_w_down")

    big = {"w_ada": (g_ada, d_ada, m_ada, v_ada), "w_in": (gs_in, d_in, m_in, v_in), "w_out": (gs_out, d_out, m_out, v_out),
           "w_gate_up": (gs_gu, d_gu, m_gu, v_gu), "w_down": (gs_dn, d_dn, m_dn, v_dn)}
    order = ["rel_bias", "w_ada", "b_ada", "w_in", "b_in", "attn_sinks", "gmlp_ln_g", "gmlp_ln_b", "gmlp_w_s", "gmlp_b_s",
             "attn_out_g", "gmlp_out_g", "w_out", "ln1_g", "ln1_b", "w_gate_up", "w_down", "ln2_g", "ln2_b"]
    outs = [loss, grad_x[None]]
    for k, packed in enumerate((sg, sd, sm, sv)):
        for name in order:
            if name in big:
                outs.append(big[name][k][None])
            else:
                outs.append(_unpack(packed, layout, name))
    return tuple(outs)
```

```python
import math

import numpy as np
import jax
import jax.numpy as jnp
from jax import lax
from jax.experimental import pallas as pl
from jax.experimental.pallas import tpu as pltpu

F32 = jnp.float32
BF16 = jnp.bfloat16
MESH = pl.DeviceIdType.MESH

D_MODEL = 1024
N_HEADS = 8
N_KV = 2
HEAD_DIM = 64
ATTN_W = N_HEADS * HEAD_DIM
KV_W = N_KV * HEAD_DIM
N_GROUPS = 8
GROUP_DIM = 64
GMLP_W = N_GROUPS * GROUP_DIM
IN_W = ATTN_W + 2 * KV_W + 2 * GMLP_W
BLOCK = 128
N_BUCKETS = 32
MAX_DISTANCE = 128
D_FF = 2816
ALPHA = 2.0 ** 0.25
LN_EPS = 1e-5
NEG_INF = -1e30
ADAM_LR, ADAM_B1, ADAM_B2, ADAM_EPS, ADAM_WD, ADAM_STEP = 0.001, 0.9, 0.999, 1e-8, 0.01, 10
N_CHIPS = 4
N_DEV = 8
LANES = 128
V7X_VMEM_LIMIT = 56 * 2 ** 20
GELU_C = math.sqrt(2.0 / math.pi)
Q_SCALE = HEAD_DIM ** -0.5
ANY = pl.BlockSpec(memory_space=pl.ANY)


def _params(sem=None):
    return pltpu.CompilerParams(dimension_semantics=sem, vmem_limit_bytes=V7X_VMEM_LIMIT)


def _const_spec(shape, single=False):
    nd = len(shape)
    if single:
        return pl.BlockSpec(shape, lambda *_: (0,) * nd, pipeline_mode=pl.Buffered(1))
    return pl.BlockSpec(shape, lambda *_: (0,) * nd)


def _dot(a, b):
    return jnp.dot(a, b, preferred_element_type=F32)


def _dot_nt(a, b):
    return lax.dot_general(a, b, (((1,), (1,)), ((), ())), preferred_element_type=F32)


def _dot_tn(a, b):
    return lax.dot_general(a, b, (((0,), (0,)), ((), ())), preferred_element_type=F32)


def _gelu(x):
    t = jnp.tanh(GELU_C * (x + 0.044715 * x * x * x))
    return 0.5 * x * (1.0 + t), t


def _gelu_grad(x, t):
    return 0.5 * (1.0 + t) + 0.5 * x * (1.0 - t * t) * GELU_C * (1.0 + 3.0 * 0.044715 * x * x)


def _split_dot(x, a):
    hi = x.astype(BF16)
    lo = (x - hi.astype(F32)).astype(BF16)
    return _dot(hi, a) + _dot(lo, a)


def _group_mean_matrix():
    g = np.arange(GMLP_W) // GROUP_DIM
    return jnp.asarray((g[:, None] == g[None, :]).astype(np.float32) / GROUP_DIM, dtype=BF16)


def _ln_stats(z):
    mu = jnp.mean(z, axis=-1, keepdims=True)
    d = z - mu
    var = jnp.mean(d * d, axis=-1, keepdims=True)
    rstd = lax.rsqrt(var + LN_EPS)
    return d * rstd, rstd


def _ln_bwd(dxhat, xhat, rstd):
    m1 = jnp.mean(dxhat, axis=-1, keepdims=True)
    m2 = jnp.mean(dxhat * xhat, axis=-1, keepdims=True)
    return rstd * (dxhat - m1 - xhat * m2)


def _colsum(x):
    return jnp.sum(x, axis=0, keepdims=True)


def _my_pos():
    return lax.axis_index("x"), lax.axis_index("y"), lax.axis_index("c")


def _other_chips(x, y):
    return [(1 - x, y), (x, 1 - y), (1 - x, 1 - y)]


def _chip_index_scalar():
    ix, iy, _ = _my_pos()
    return jnp.reshape(2 * ix + iy, (1,)).astype(jnp.int32)


def _core_index_scalar():
    return jnp.reshape(lax.axis_index("c"), (1,)).astype(jnp.int32)


def _allgather8(v, name):
    m_per, n = v.shape

    def body(x_ref, out_ref, send_sems, recv_sems, local_sem):
        x, y, c = _my_pos()
        me, sibling = (x, y, c), (x, y, 1 - c)
        chips = _other_chips(x, y)

        def rows(px, py, pc):
            return out_ref.at[pl.ds((4 * px + 2 * py + pc) * m_per, m_per), :]

        def copy(k, block, to, src=None):
            return pltpu.make_async_remote_copy(
                src_ref=rows(*block) if src is None else src, dst_ref=rows(*block),
                send_sem=send_sems.at[k], recv_sem=recv_sems.at[k], device_id=to, device_id_type=MESH)

        mine = pltpu.make_async_copy(x_ref, rows(*me), local_sem)
        mine.start()
        first = [copy(0, me, sibling, src=x_ref)]
        first += [copy(1 + j, me, (*chip, c), src=x_ref) for j, chip in enumerate(chips)]
        for cp in first:
            cp.start()
        passed = [copy(4 + j, (*chip, c), sibling) for j, chip in enumerate(chips)]
        for j, chip in enumerate(chips):
            copy(1 + j, (*chip, c), me).wait_recv()
            passed[j].start()
        copy(0, sibling, me).wait_recv()
        for j, chip in enumerate(chips):
            copy(4 + j, (*chip, 1 - c), me).wait_recv()
        for cp in first + passed:
            cp.wait_send()
        mine.wait()

    return pl.pallas_call(
        body, name=name,
        out_shape=jax.ShapeDtypeStruct((N_DEV * m_per, n), v.dtype),
        in_specs=[pl.BlockSpec(memory_space=pltpu.VMEM)],
        out_specs=pl.BlockSpec(memory_space=pltpu.VMEM),
        scratch_shapes=[pltpu.SemaphoreType.DMA((7,)), pltpu.SemaphoreType.DMA((7,)), pltpu.SemaphoreType.DMA],
        compiler_params=pltpu.CompilerParams(vmem_limit_bytes=V7X_VMEM_LIMIT),
    )(v)


def _gathered_shape(shard, kind):
    r, cc = shard.shape
    return (N_CHIPS, r, cc) if kind == "blk" else (r, N_CHIPS * cc)


class _WeightGather:
    def __init__(self, shards, gathered, kinds, send_sems, recv_sems):
        self.shards, self.gathered, self.kinds = shards, gathered, kinds
        self.send_sems, self.recv_sems = send_sems, recv_sems
        self.x, self.y, self.c = _my_pos()
        self.chips = _other_chips(self.x, self.y)

    def _dst(self, a, chip, pc):
        r, cc = self.shards[a].shape
        h = r // 2
        g = self.gathered[a]
        if self.kinds[a] == "blk":
            return g.at[chip, pl.ds(pc * h, h), :]
        return g.at[pl.ds(pc * h, h), pl.ds(chip * cc, cc)]

    def _copy(self, a, k, chip, pc, to, src=None):
        d = self._dst(a, chip, pc)
        return pltpu.make_async_remote_copy(
            src_ref=d if src is None else src, dst_ref=d, send_sem=self.send_sems.at[a * 6 + k],
            recv_sem=self.recv_sems.at[a * 6 + k], device_id=to, device_id_type=MESH)

    def _each(self):
        for a in range(len(self.shards)):
            for j, chip in enumerate(self.chips):
                yield a, j, chip, 2 * chip[0] + chip[1]

    def start(self):
        my_chip = 2 * self.x + self.y
        for a, j, chip, _ in self._each():
            h = self.shards[a].shape[0] // 2
            self._copy(a, j, my_chip, self.c, (*chip, self.c), src=self.shards[a].at[pl.ds(self.c * h, h), :]).start()

    def forward(self):
        me, sibling = (self.x, self.y, self.c), (self.x, self.y, 1 - self.c)
        for a, j, chip, cj in self._each():
            self._copy(a, j, cj, self.c, me).wait_recv()
            self._copy(a, 3 + j, cj, self.c, sibling).start()

    def finish(self):
        me = (self.x, self.y, self.c)
        for a, j, chip, cj in self._each():
            self._copy(a, 3 + j, cj, 1 - self.c, me).wait_recv()
        for a, j, chip, cj in self._each():
            self._copy(a, j, cj, self.c, me).wait_send()
            self._copy(a, 3 + j, cj, self.c, me).wait_send()

    @staticmethod
    def sems(n_arr):
        return [pltpu.SemaphoreType.DMA((n_arr * 6,)), pltpu.SemaphoreType.DMA((n_arr * 6,))]


def _insert_own(gathered, shard, kind, chip):
    if kind == "blk":
        return lax.dynamic_update_slice(gathered, shard[None], (chip, 0, 0))
    return lax.dynamic_update_slice(gathered, shard, (0, chip * shard.shape[1]))


def _gather_weights(shards, kinds, name):
    n_arr = len(shards)

    def body(*refs):
        g = _WeightGather(refs[:n_arr], refs[n_arr:2 * n_arr], kinds, *refs[2 * n_arr:])
        g.start()
        g.forward()
        g.finish()

    return pl.pallas_call(
        body, name=name,
        out_shape=[jax.ShapeDtypeStruct(_gathered_shape(s, k), BF16) for s, k in zip(shards, kinds)],
        in_specs=[ANY] * n_arr, out_specs=[ANY] * n_arr, scratch_shapes=_WeightGather.sems(n_arr),
    )(*shards)


def _half_of_full(ref, kind, pc):
    if kind == "blk":
        h = ref.shape[1] // 2
        return ref.at[:, pl.ds(pc * h, h), :]
    h = ref.shape[0] // 2
    return ref.at[pl.ds(pc * h, h), :]


def _half_shape(shape, kind):
    return (shape[0], shape[1] // 2, shape[2]) if kind == "blk" else (shape[0] // 2, shape[1])


def _swap_halves(fulls_bf16, kinds, name):
    n_arr = len(fulls_bf16)

    def body(*refs):
        ins, outs = refs[:n_arr], refs[n_arr:2 * n_arr]
        send_sems, recv_sems = refs[2 * n_arr:]
        x, y, c = _my_pos()
        cps = []
        for a in range(n_arr):
            cp = pltpu.make_async_remote_copy(
                src_ref=_half_of_full(ins[a], kinds[a], 1 - c), dst_ref=outs[a], send_sem=send_sems.at[a],
                recv_sem=recv_sems.at[a], device_id=(x, y, 1 - c), device_id_type=MESH)
            cp.start()
            cps.append(cp)
        for cp in cps:
            cp.wait()

    return pl.pallas_call(
        body, name=name,
        out_shape=[jax.ShapeDtypeStruct(_half_shape(a.shape, k), a.dtype) for a, k in zip(fulls_bf16, kinds)],
        in_specs=[ANY] * n_arr, out_specs=[ANY] * n_arr,
        scratch_shapes=[pltpu.SemaphoreType.DMA((n_arr,)), pltpu.SemaphoreType.DMA((n_arr,))],
    )(*fulls_bf16)


def _add_halves(full, got, kind, name):
    hs = _half_shape(full.shape, kind)

    def body(c_ref, a_ref, b_ref, o_ref, ob_ref):
        p = a_ref[...] + b_ref[...].astype(F32)
        o_ref[...] = p
        ob_ref[...] = p.astype(BF16)

    if kind == "blk":
        nb, h, cc = hs
        own = pl.BlockSpec((1, h, cc), lambda b, c_ref: (b, c_ref[0], 0))
        other = pl.BlockSpec((1, h, cc), lambda b, c_ref: (b, 0, 0))
    else:
        h, cc = hs[0], hs[1] // N_CHIPS
        own = pl.BlockSpec((h, cc), lambda b, c_ref: (c_ref[0], b))
        other = pl.BlockSpec((h, cc), lambda b, c_ref: (0, b))
    return pl.pallas_call(
        body, name=name, out_shape=(jax.ShapeDtypeStruct(hs, F32), jax.ShapeDtypeStruct(hs, BF16)),
        grid_spec=pltpu.PrefetchScalarGridSpec(
            num_scalar_prefetch=1, grid=(N_CHIPS,), in_specs=[own, other], out_specs=(other, other)),
        compiler_params=_params(("arbitrary",)),
    )(_core_index_scalar(), full, got)


def _rx_shape(part_shape, kind):
    if kind == "blk":
        return (3, part_shape[1], part_shape[2])
    return (3, part_shape[0], part_shape[1] // N_CHIPS)


class _ChipExchange:
    def __init__(self, parts, rxs, kinds, send_sems, recv_sems):
        self.parts, self.rxs, self.kinds = parts, rxs, kinds
        self.send_sems, self.recv_sems = send_sems, recv_sems
        self.x, self.y, self.c = _my_pos()
        self.chips = _other_chips(self.x, self.y)

    def _copies(self):
        for a in range(len(self.parts)):
            for j, chip in enumerate(self.chips):
                cj = 2 * chip[0] + chip[1]
                if self.kinds[a] == "blk":
                    src = self.parts[a].at[cj]
                else:
                    cc = self.parts[a].shape[1] // N_CHIPS
                    src = self.parts[a].at[:, pl.ds(cj * cc, cc)]
                yield pltpu.make_async_remote_copy(
                    src_ref=src, dst_ref=self.rxs[a].at[j], send_sem=self.send_sems.at[a * 3 + j],
                    recv_sem=self.recv_sems.at[a * 3 + j], device_id=(*chip, self.c), device_id_type=MESH)

    def start(self):
        for cp in self._copies():
            cp.start()

    def wait(self):
        for cp in self._copies():
            cp.wait_recv()
        for cp in self._copies():
            cp.wait_send()

    @staticmethod
    def sems(n_arr):
        return [pltpu.SemaphoreType.DMA((n_arr * 3,)), pltpu.SemaphoreType.DMA((n_arr * 3,))]


def _exchange_chip_partials(parts, kinds, name):
    n_arr = len(parts)

    def body(*refs):
        ex = _ChipExchange(refs[:n_arr], refs[n_arr:2 * n_arr], kinds, *refs[2 * n_arr:])
        ex.start()
        ex.wait()

    return pl.pallas_call(
        body, name=name,
        out_shape=[jax.ShapeDtypeStruct(_rx_shape(p.shape, k), BF16) for p, k in zip(parts, kinds)],
        in_specs=[ANY] * n_arr, out_specs=[ANY] * n_arr, scratch_shapes=_ChipExchange.sems(n_arr),
    )(*parts)


def _sum_chips(part, rx, kind, tr, name):
    _, h, cc = rx.shape
    flips = (2, 1, 3)

    def body(chip_ref, p_ref, rx_ref, o_ref):
        own = p_ref[...].reshape(tr, cc)
        for mc in range(N_CHIPS):
            @pl.when(chip_ref[0] == mc)
            def _():
                terms = sorted([(mc, None)] + [(mc ^ f, j) for j, f in enumerate(flips)])
                acc = None
                for _, j in terms:
                    t = own if j is None else rx_ref[j].astype(F32)
                    acc = t if acc is None else acc + t
                o_ref[...] = acc

    if kind == "blk":
        own_spec = pl.BlockSpec((1, tr, cc), lambda i, chip_ref: (chip_ref[0], i, 0))
    else:
        own_spec = pl.BlockSpec((tr, cc), lambda i, chip_ref: (i, chip_ref[0]))
    return pl.pallas_call(
        body, name=name, out_shape=jax.ShapeDtypeStruct((h, cc), F32),
        grid_spec=pltpu.PrefetchScalarGridSpec(
            num_scalar_prefetch=1, grid=(h // tr,),
            in_specs=[own_spec, pl.BlockSpec((3, tr, cc), lambda i, chip_ref: (0, i, 0))],
            out_specs=pl.BlockSpec((tr, cc), lambda i, chip_ref: (i, 0))),
        compiler_params=_params(("arbitrary",)),
    )(_chip_index_scalar(), part, rx)


def _share_halves(halves, name):
    n_arr = len(halves)

    def body(*refs):
        ins, outs = refs[:n_arr], refs[n_arr:2 * n_arr]
        send_sems, recv_sems = refs[2 * n_arr:]
        x, y, c = _my_pos()
        cps = []
        for a in range(n_arr):
            cp = pltpu.make_async_remote_copy(
                src_ref=ins[a], dst_ref=outs[a], send_sem=send_sems.at[a], recv_sem=recv_sems.at[a],
                device_id=(x, y, 1 - c), device_id_type=MESH)
            cp.start()
            cps.append(cp)
        for cp in cps:
            cp.wait()

    return pl.pallas_call(
        body, name=name, out_shape=[jax.ShapeDtypeStruct(h.shape, h.dtype) for h in halves],
        in_specs=[ANY] * n_arr, out_specs=[ANY] * n_arr,
        scratch_shapes=[pltpu.SemaphoreType.DMA((n_arr,)), pltpu.SemaphoreType.DMA((n_arr,))],
    )(*halves)


def _mod_part(c_all, w_ada_s, b_ada_s):
    n = w_ada_s.shape[1]

    def body(c_ref, w_ref, b_ref, sc_ref, mod_ref):
        cv = c_ref[...]
        sc = cv * (1.0 / (1.0 + jnp.exp(-cv)))
        sc_ref[...] = sc
        a_hi = sc.astype(BF16)
        a_lo = (sc - a_hi.astype(F32)).astype(BF16)
        w = w_ref[...]
        w_hi = w.astype(BF16)
        w_lo = (w - w_hi.astype(F32)).astype(BF16)
        mod_ref[...] = _dot(a_hi, w_hi) + _dot(a_hi, w_lo) + _dot(a_lo, w_hi) + b_ref[...]

    return pl.pallas_call(
        body, name="mod_part",
        out_shape=(jax.ShapeDtypeStruct((N_DEV, D_MODEL), F32), jax.ShapeDtypeStruct((N_DEV, n), F32)),
        grid=(1,),
        in_specs=[_const_spec((N_DEV, D_MODEL)), _const_spec((D_MODEL, n)), _const_spec((1, n))],
        out_specs=(_const_spec((N_DEV, D_MODEL)), _const_spec((N_DEV, n))),
        compiler_params=_params(("arbitrary",)),
    )(c_all, w_ada_s, b_ada_s)


def _bucket_table():
    qi = jnp.arange(BLOCK)[:, None]
    si = jnp.arange(2 * BLOCK)[None, :]
    dist = qi + BLOCK - si
    max_exact = N_BUCKETS // 2
    n = jnp.maximum(dist, 0)
    nf = jnp.maximum(n, max_exact).astype(F32)
    large = max_exact + (jnp.log(nf / max_exact) / math.log(MAX_DISTANCE / max_exact)
                         * (N_BUCKETS - max_exact)).astype(jnp.int32)
    large = jnp.minimum(large, N_BUCKETS - 1)
    return jnp.where(n < max_exact, n, large).astype(F32)


def _prep_tables(bucket, rel_bias, w_s):
    def body(bucket_ref, rb_ref, ws_ref, bias_ref, wsm_ref):
        qi = lax.broadcasted_iota(jnp.int32, (BLOCK, 2 * BLOCK), 0)
        si = lax.broadcasted_iota(jnp.int32, (BLOCK, 2 * BLOCK), 1)
        dist = qi + BLOCK - si
        in_window = (dist >= 0) & (dist < BLOCK)
        bk = bucket_ref[...]
        for h in range(N_HEADS):
            acc = jnp.zeros((BLOCK, 2 * BLOCK), F32)
            for b in range(N_BUCKETS):
                acc = jnp.where(bk == float(b), rb_ref[b, h], acc)
            bias_ref[h] = jnp.where(in_window, acc, NEG_INF)
        ti = lax.broadcasted_iota(jnp.int32, (BLOCK, BLOCK), 0)
        ui = lax.broadcasted_iota(jnp.int32, (BLOCK, BLOCK), 1)
        for g in range(N_GROUPS):
            wsm_ref[g] = jnp.where(ti >= ui, ws_ref[g], 0.0).astype(BF16)

    return pl.pallas_call(
        body, name="prep_tables",
        out_shape=(jax.ShapeDtypeStruct((N_HEADS, BLOCK, 2 * BLOCK), F32),
                   jax.ShapeDtypeStruct((N_GROUPS, BLOCK, BLOCK), BF16)),
        grid=(1,),
        in_specs=[_const_spec((BLOCK, 2 * BLOCK)), pl.BlockSpec(memory_space=pltpu.SMEM),
                  _const_spec((N_GROUPS, BLOCK, BLOCK))],
        out_specs=(_const_spec((N_HEADS, BLOCK, 2 * BLOCK)), _const_spec((N_GROUPS, BLOCK, BLOCK))),
        compiler_params=_params(("arbitrary",)),
    )(bucket, rel_bias, w_s)


def _fwd_in(x, modr, w_in, b_in, tm):
    s = x.shape[0]

    def body(x_ref, mod_ref, w_ref, b_ref, h1_ref, q_ref, kv_ref, gu_ref, gv_ref):
        h1 = (x_ref[...] * (1.0 + mod_ref[1:2, :]) + mod_ref[0:1, :]).astype(BF16)
        h1_ref[...] = h1
        proj = _dot(h1, w_ref[...]) + b_ref[...]
        q_ref[...] = (proj[:, :ATTN_W] * Q_SCALE).astype(BF16)
        kv_ref[...] = proj[:, ATTN_W:ATTN_W + 2 * KV_W].astype(BF16)
        gu_ref[...] = proj[:, ATTN_W + 2 * KV_W:ATTN_W + 2 * KV_W + GMLP_W]
        gv_ref[...] = proj[:, ATTN_W + 2 * KV_W + GMLP_W:]

    row = lambda w: pl.BlockSpec((tm, w), lambda i: (i, 0))
    return pl.pallas_call(
        body, name="fwd_in",
        out_shape=(jax.ShapeDtypeStruct((s, D_MODEL), BF16), jax.ShapeDtypeStruct((s, ATTN_W), BF16),
                   jax.ShapeDtypeStruct((s, 2 * KV_W), BF16), jax.ShapeDtypeStruct((s, GMLP_W), F32),
                   jax.ShapeDtypeStruct((s, GMLP_W), F32)),
        grid=(s // tm,),
        in_specs=[row(D_MODEL), _const_spec((8, D_MODEL)), _const_spec((D_MODEL, IN_W)), _const_spec((1, IN_W))],
        out_specs=(row(D_MODEL), row(ATTN_W), row(2 * KV_W), row(GMLP_W), row(GMLP_W)),
        compiler_params=_params(("parallel",)),
    )(x, modr, w_in, b_in)


def _kv_variants(kk):
    kf = kk.astype(F32)
    lane = lax.broadcasted_iota(jnp.int32, kf.shape, 1)
    low = lane < HEAD_DIM
    k0_lo = jnp.where(low, kf, 0.0)
    k1_hi = jnp.where(low, 0.0, kf)
    k0_hi = pltpu.roll(k0_lo, HEAD_DIM, 1)
    k1_lo = pltpu.roll(k1_hi, HEAD_DIM, 1)
    return ((k0_lo.astype(BF16), k0_hi.astype(BF16)), (k1_lo.astype(BF16), k1_hi.astype(BF16)))


def _attn_head_probs(q_pair, k_var, bias_h, sink, first_mask):
    logits = _dot_nt(q_pair, k_var) + bias_h
    if first_mask is not None:
        logits = jnp.where(first_mask, NEG_INF, logits)
    m = jnp.maximum(jnp.max(logits, axis=-1, keepdims=True), sink)
    e = jnp.exp(logits - m)
    es = jnp.exp(sink - m)
    inv = 1.0 / (jnp.sum(e, axis=-1, keepdims=True) + es)
    return e * inv, es * inv


def _attn_block_fwd(q_blk, kk, vv, bias_ref, sinks_ref, first_mask):
    kvar = _kv_variants(kk)
    vvar = _kv_variants(vv)
    outs, probs = [], []
    for pair in range(N_HEADS // 2):
        acc = None
        for par in range(2):
            h = 2 * pair + par
            kv = h // (N_HEADS // N_KV)
            p, ps = _attn_head_probs(q_blk[:, pair * LANES:(pair + 1) * LANES], kvar[kv][par], bias_ref[h],
                                     sinks_ref[h], first_mask)
            probs.append((p, ps))
            o = _dot(p.astype(BF16), vvar[kv][par])
            acc = o if acc is None else acc + o
        outs.append(acc)
    return jnp.concatenate(outs, axis=1), probs, kvar, vvar


def _gmlp_chunk_fwd(gu, gv, ln_g, ln_b, wsm_ref, bsx, amat):
    u, tu = _gelu(gu)
    a, ta = _gelu(gv)
    mean = _split_dot(a, amat)
    d = a - mean
    var = _split_dot(d * d, amat)
    rstd = lax.rsqrt(var + LN_EPS)
    xhat = d * rstd
    vb = (xhat * ln_g + ln_b).astype(BF16)
    lane = lax.broadcasted_iota(jnp.int32, (BLOCK, LANES), 1)
    low = lane < GROUP_DIM
    cols = []
    for pair in range(N_GROUPS // 2):
        vp = vb[:, pair * LANES:(pair + 1) * LANES]
        cols.append(jnp.where(low, _dot(wsm_ref[2 * pair], vp), _dot(wsm_ref[2 * pair + 1], vp)))
    mixedv = jnp.concatenate(cols, axis=1) + bsx
    return u * mixedv, (u, tu, ta, xhat, rstd, vb, mixedv)


def _rms(a, g):
    r = lax.rsqrt(jnp.mean(a * a, axis=-1, keepdims=True) + LN_EPS)
    return a * r * g, r


def _fwd_mix(q, kv, gu, gv, x, modr, bias, sinks, gln_g, gln_b, wsm, bsx, amat, aog, gog, w_out, ln1_g, ln1_b, tm,
             ffn_shards, ffn_kinds):
    s = x.shape[0]
    nb = tm // BLOCK
    n_steps = s // tm
    fwd_step = (3 * n_steps) // 4
    n_w = len(ffn_shards)

    def body(q_ref, kv_ref, kvp_ref, gu_ref, gv_ref, x_ref, mod_ref, bias_ref, sinks_ref, glng_ref, glnb_ref, wsm_ref,
             bsx_ref, amat_ref, aog_ref, gog_ref, wout_ref, ln1g_ref, ln1b_ref, *rest):
        shard_refs = rest[:n_w]
        x1_ref, y_ref, mixed_ref = rest[n_w:n_w + 3]
        gathered_refs = rest[n_w + 3:2 * n_w + 3]
        mix_scr, send_sems, recv_sems = rest[2 * n_w + 3:]
        i = pl.program_id(0)
        gather = _WeightGather(shard_refs, gathered_refs, ffn_kinds, send_sems, recv_sems)

        @pl.when(i == 0)
        def _():
            gather.start()

        col = lax.broadcasted_iota(jnp.int32, (BLOCK, 2 * BLOCK), 1)
        for b in range(nb):
            r0 = b * BLOCK
            if b == 0:
                kvprev = kvp_ref[...]
                first_mask = (col < BLOCK) & (i == 0)
            else:
                kvprev = kv_ref[r0 - BLOCK:r0, :]
                first_mask = None
            kvcur = kv_ref[r0:r0 + BLOCK, :]
            kk = jnp.concatenate([kvprev[:, :KV_W], kvcur[:, :KV_W]], axis=0)
            vv = jnp.concatenate([kvprev[:, KV_W:], kvcur[:, KV_W:]], axis=0)
            attn, _, _, _ = _attn_block_fwd(q_ref[r0:r0 + BLOCK, :], kk, vv, bias_ref, sinks_ref, first_mask)
            na, _ = _rms(attn, aog_ref[...])
            gm, _ = _gmlp_chunk_fwd(gu_ref[r0:r0 + BLOCK, :], gv_ref[r0:r0 + BLOCK, :], glng_ref[...], glnb_ref[...],
                                    wsm_ref, bsx_ref[...], amat_ref[...])
            ng, _ = _rms(gm, gog_ref[...])
            mix_scr[r0:r0 + BLOCK, :ATTN_W] = na.astype(BF16)
            mix_scr[r0:r0 + BLOCK, ATTN_W:] = ng.astype(BF16)
        mixed = mix_scr[...]
        mixed_ref[...] = mixed
        y = _dot(mixed, wout_ref[...])
        y_ref[...] = y
        z1 = ALPHA * x_ref[...] + mod_ref[2:3, :] * y
        xhat, _ = _ln_stats(z1)
        x1_ref[...] = xhat * ln1g_ref[...] + ln1b_ref[...]

        @pl.when(i == fwd_step)
        def _():
            gather.forward()

        @pl.when(i == n_steps - 1)
        def _():
            gather.finish()

    row = lambda w: pl.BlockSpec((tm, w), lambda i: (i, 0))
    prev = pl.BlockSpec((BLOCK, 2 * KV_W), lambda i: (jnp.maximum(i * nb - 1, 0), 0))
    outs = pl.pallas_call(
        body, name="fwd_mix",
        out_shape=[jax.ShapeDtypeStruct((s, D_MODEL), F32), jax.ShapeDtypeStruct((s, D_MODEL), F32),
                   jax.ShapeDtypeStruct((s, D_MODEL), BF16)]
        + [jax.ShapeDtypeStruct(_gathered_shape(sh, k), BF16) for sh, k in zip(ffn_shards, ffn_kinds)],
        grid=(n_steps,),
        in_specs=[row(ATTN_W), row(2 * KV_W), prev, row(GMLP_W), row(GMLP_W), row(D_MODEL), _const_spec((8, D_MODEL)),
                  _const_spec((N_HEADS, BLOCK, 2 * BLOCK)), pl.BlockSpec(memory_space=pltpu.SMEM),
                  _const_spec((1, GMLP_W)), _const_spec((1, GMLP_W)), _const_spec((N_GROUPS, BLOCK, BLOCK)),
                  _const_spec((BLOCK, GMLP_W)), _const_spec((GMLP_W, GMLP_W)), _const_spec((1, ATTN_W)),
                  _const_spec((1, GMLP_W)), _const_spec((D_MODEL, D_MODEL)), _const_spec((1, D_MODEL)),
                  _const_spec((1, D_MODEL))] + [ANY] * n_w,
        out_specs=[row(D_MODEL), row(D_MODEL), row(D_MODEL)] + [ANY] * n_w,
        scratch_shapes=[pltpu.VMEM((tm, D_MODEL), BF16)] + _WeightGather.sems(n_w),
        compiler_params=_params(("arbitrary",)),
    )(q, kv, kv, gu, gv, x, modr, bias, sinks, gln_g, gln_b, wsm, bsx, amat, aog, gog, w_out, ln1_g, ln1_b, *ffn_shards)
    return outs[0], outs[1], outs[2], outs[3:]


FF_CHUNK = D_FF // 2


def _sigmoid(x):
    return 1.0 / (1.0 + jnp.exp(-x))


def _fwd_ffn(x1, target, modr, ln2_g, ln2_b, w_gu, w_dn, tm):
    s = x1.shape[0]

    def body(x1_ref, t_ref, mod_ref, g_ref, b_ref, wgu_ref, wdn_ref, h2_ref, act_ref, dy2_ref, dx1a_ref, acc_ref):
        i = pl.program_id(0)

        @pl.when(i == 0)
        def _():
            acc_ref[...] = jnp.zeros_like(acc_ref)

        x1v = x1_ref[...]
        h2 = (x1v * (1.0 + mod_ref[4:5, :]) + mod_ref[3:4, :]).astype(BF16)
        h2_ref[...] = h2
        y2 = jnp.zeros((tm, D_MODEL), F32)
        for cc in range(D_FF // FF_CHUNK):
            c0 = cc * FF_CHUNK
            gate = _dot(h2, wgu_ref[:, c0:c0 + FF_CHUNK])
            up = _dot(h2, wgu_ref[:, D_FF + c0:D_FF + c0 + FF_CHUNK])
            act_ref[:, c0:c0 + FF_CHUNK] = gate.astype(BF16)
            act_ref[:, D_FF + c0:D_FF + c0 + FF_CHUNK] = up.astype(BF16)
            a = (gate * _sigmoid(gate) * up).astype(BF16)
            y2 = y2 + _dot(a, wdn_ref[c0:c0 + FF_CHUNK, :])
        g2 = mod_ref[5:6, :]
        z2 = ALPHA * x1v + g2 * y2
        xhat, rstd = _ln_stats(z2)
        gain = g_ref[...]
        diff = xhat * gain + b_ref[...] - t_ref[...]
        dx2 = diff * (1.0 / D_MODEL)
        dz2 = _ln_bwd(dx2 * gain, xhat, rstd)
        dx1a_ref[...] = ALPHA * dz2
        dy2_ref[...] = (g2 * dz2).astype(BF16)
        acc_ref[0:1, :] += _colsum(diff * diff)
        acc_ref[1:2, :] += _colsum(dx2 * xhat)
        acc_ref[2:3, :] += _colsum(dx2)
        acc_ref[3:4, :] += _colsum(dz2 * y2)

    row = lambda w: pl.BlockSpec((tm, w), lambda i: (i, 0))
    return pl.pallas_call(
        body, name="fwd_ffn",
        out_shape=(jax.ShapeDtypeStruct((s, D_MODEL), BF16), jax.ShapeDtypeStruct((s, 2 * D_FF), BF16),
                   jax.ShapeDtypeStruct((s, D_MODEL), BF16), jax.ShapeDtypeStruct((s, D_MODEL), F32),
                   jax.ShapeDtypeStruct((8, D_MODEL), F32)),
        grid=(s // tm,),
        in_specs=[row(D_MODEL), row(D_MODEL), _const_spec((8, D_MODEL)), _const_spec((1, D_MODEL)),
                  _const_spec((1, D_MODEL)), _const_spec((D_MODEL, 2 * D_FF), single=True),
                  _const_spec((D_FF, D_MODEL), single=True)],
        out_specs=(row(D_MODEL), row(2 * D_FF), row(D_MODEL), row(D_MODEL), _const_spec((8, D_MODEL))),
        compiler_params=_params(("arbitrary",)),
    )(x1, target, modr, ln2_g, ln2_b, w_gu, w_dn)


def _bwd_ffn(dy2, act, w_gu, w_dn, tm):
    s = dy2.shape[0]

    def body(dy2_ref, act_ref, wgu_ref, wdn_ref, a_ref, dgu_ref, dh2_ref):
        dy2v = dy2_ref[...]
        dh2 = jnp.zeros((tm, D_MODEL), F32)
        for cc in range(D_FF // FF_CHUNK):
            c0 = cc * FF_CHUNK
            da = _dot_nt(dy2v, wdn_ref[c0:c0 + FF_CHUNK, :])
            gate = act_ref[:, c0:c0 + FF_CHUNK].astype(F32)
            up = act_ref[:, D_FF + c0:D_FF + c0 + FF_CHUNK].astype(F32)
            sg = _sigmoid(gate)
            sl = gate * sg
            a_ref[:, c0:c0 + FF_CHUNK] = (sl * up).astype(BF16)
            dgate = (da * up * (sg * (1.0 + gate * (1.0 - sg)))).astype(BF16)
            dup = (da * sl).astype(BF16)
            dgu_ref[:, c0:c0 + FF_CHUNK] = dgate
            dgu_ref[:, D_FF + c0:D_FF + c0 + FF_CHUNK] = dup
            dh2 = dh2 + _dot_nt(dgate, wgu_ref[:, c0:c0 + FF_CHUNK])
            dh2 = dh2 + _dot_nt(dup, wgu_ref[:, D_FF + c0:D_FF + c0 + FF_CHUNK])
        dh2_ref[...] = dh2

    row = lambda w: pl.BlockSpec((tm, w), lambda i: (i, 0))
    return pl.pallas_call(
        body, name="bwd_ffn",
        out_shape=(jax.ShapeDtypeStruct((s, D_FF), BF16), jax.ShapeDtypeStruct((s, 2 * D_FF), BF16),
                   jax.ShapeDtypeStruct((s, D_MODEL), F32)),
        grid=(s // tm,),
        in_specs=[row(D_MODEL), row(2 * D_FF), _const_spec((D_MODEL, 2 * D_FF), single=True),
                  _const_spec((D_FF, D_MODEL), single=True)],
        out_specs=(row(D_FF), row(2 * D_FF), row(D_MODEL)),
        compiler_params=_params(("parallel",)),
    )(dy2, act, w_gu, w_dn)


def _bwd_mid(dh2, dx1a, x1, x, y, modr, ln1_g, w_out, tm):
    s = x.shape[0]

    def body(dh2_ref, dx1a_ref, x1_ref, x_ref, y_ref, mod_ref, g_ref, wout_ref, dxa_ref, dy_ref, dmix_ref, acc_ref):
        i = pl.program_id(0)

        @pl.when(i == 0)
        def _():
            acc_ref[...] = jnp.zeros_like(acc_ref)

        dh2 = dh2_ref[...]
        x1v = x1_ref[...]
        yv = y_ref[...]
        g1 = mod_ref[2:3, :]
        dx1 = dx1a_ref[...] + dh2 * (1.0 + mod_ref[4:5, :])
        z1 = ALPHA * x_ref[...] + g1 * yv
        xhat, rstd = _ln_stats(z1)
        dz1 = _ln_bwd(dx1 * g_ref[...], xhat, rstd)
        dxa_ref[...] = ALPHA * dz1
        dy = (g1 * dz1).astype(BF16)
        dy_ref[...] = dy
        dmix_ref[...] = _dot_nt(dy, wout_ref[...])
        acc_ref[0:1, :] += _colsum(dh2 * x1v)
        acc_ref[1:2, :] += _colsum(dh2)
        acc_ref[2:3, :] += _colsum(dx1 * xhat)
        acc_ref[3:4, :] += _colsum(dx1)
        acc_ref[4:5, :] += _colsum(dz1 * yv)

    row = lambda w: pl.BlockSpec((tm, w), lambda i: (i, 0))
    return pl.pallas_call(
        body, name="bwd_mid",
        out_shape=(jax.ShapeDtypeStruct((s, D_MODEL), F32), jax.ShapeDtypeStruct((s, D_MODEL), BF16),
                   jax.ShapeDtypeStruct((s, D_MODEL), F32), jax.ShapeDtypeStruct((8, D_MODEL), F32)),
        grid=(s // tm,),
        in_specs=[row(D_MODEL)] * 5 + [_const_spec((8, D_MODEL)), _const_spec((1, D_MODEL)),
                                       _const_spec((D_MODEL, D_MODEL))],
        out_specs=(row(D_MODEL), row(D_MODEL), row(D_MODEL), _const_spec((8, D_MODEL))),
        compiler_params=_params(("arbitrary",)),
    )(dh2, dx1a, x1, x, y, modr, ln1_g, w_out)


def _fold_kv(t0, t1):
    lane = lax.broadcasted_iota(jnp.int32, t0.shape, 1)
    f0 = t0 + pltpu.roll(t0, HEAD_DIM, 1)
    f1 = t1 + pltpu.roll(t1, HEAD_DIM, 1)
    return jnp.where(lane < HEAD_DIM, f0, f1)


def _bwd_mix(q, kv, gu, gv, dmix, bias, sinks, gln_g, gln_b, wsm, bsx, amat, aog, gog, grad_parts, grad_kinds):
    s = q.shape[0]
    nblk = s // BLOCK
    n_g = len(grad_parts)

    def body(q_ref, kv_ref, kvp_ref, gu_ref, gv_ref, dmix_ref, bias_ref, sinks_ref, glng_ref, glnb_ref, wsm_ref,
             bsx_ref, amat_ref, aog_ref, gog_ref, *rest):
        part_refs = rest[:n_g]
        dq_ref, dkv_ref, dgu_ref, dgv_ref, gbias_ref, dws_ref, dbs_ref, vec_ref, dsink_ref = rest[n_g:n_g + 9]
        rx_refs = rest[n_g + 9:2 * n_g + 9]
        carry, send_sems, recv_sems = rest[2 * n_g + 9:]
        n = pl.program_id(0)
        exchange = _ChipExchange(part_refs, rx_refs, grad_kinds, send_sems, recv_sems)

        @pl.when(n == 0)
        def _():
            exchange.start()
            carry[...] = jnp.zeros_like(carry)
            gbias_ref[...] = jnp.zeros_like(gbias_ref)
            dws_ref[...] = jnp.zeros_like(dws_ref)
            dbs_ref[...] = jnp.zeros_like(dbs_ref)
            vec_ref[...] = jnp.zeros_like(vec_ref)
            dsink_ref[...] = jnp.zeros_like(dsink_ref)

        @pl.when(n == nblk)
        def _():
            dkv_ref[...] = carry[...].astype(BF16)
            exchange.wait()

        @pl.when(n < nblk)
        def _():
            col = lax.broadcasted_iota(jnp.int32, (BLOCK, 2 * BLOCK), 1)
            lane = lax.broadcasted_iota(jnp.int32, (BLOCK, LANES), 1)
            low = lane < HEAD_DIM
            first_mask = (col < BLOCK) & (n == 0)
            kvprev = kvp_ref[...]
            kvcur = kv_ref[...]
            kk = jnp.concatenate([kvprev[:, :KV_W], kvcur[:, :KV_W]], axis=0)
            vv = jnp.concatenate([kvprev[:, KV_W:], kvcur[:, KV_W:]], axis=0)
            q_blk = q_ref[...]
            attn, probs, kvar, vvar = _attn_block_fwd(q_blk, kk, vv, bias_ref, sinks_ref, first_mask)
            aog_v = aog_ref[...]
            na_unit, r_a = _rms(attn, 1.0)
            gm, (u, tu, ta, xhat, rstd, vb, mixedv) = _gmlp_chunk_fwd(
                gu_ref[...], gv_ref[...], glng_ref[...], glnb_ref[...], wsm_ref, bsx_ref[...], amat_ref[...])
            gog_v = gog_ref[...]
            ng_unit, r_g = _rms(gm, 1.0)

            dmix = dmix_ref[...]
            dn_a = dmix[:, :ATTN_W]
            dn_g = dmix[:, ATTN_W:]
            vec_ref[0:1, :] += _colsum(dn_a * na_unit)
            vec_ref[1:2, :] += _colsum(dn_g * ng_unit)
            t_a = dn_a * aog_v
            d_attn = r_a * t_a - na_unit * (r_a * jnp.mean(t_a * na_unit, axis=-1, keepdims=True))
            t_g = dn_g * gog_v
            d_gm = r_g * t_g - ng_unit * (r_g * jnp.mean(t_g * ng_unit, axis=-1, keepdims=True))

            gu_v = gu_ref[...]
            dgu_ref[...] = (d_gm * mixedv * _gelu_grad(gu_v, tu)).astype(BF16)
            dmx = d_gm * u
            dbs_ref[...] += dmx
            dmxb = dmx.astype(BF16)
            dvn_cols = []
            for pair in range(N_GROUPS // 2):
                dp_ = dmxb[:, pair * LANES:(pair + 1) * LANES]
                vp = vb[:, pair * LANES:(pair + 1) * LANES]
                dvn_cols.append(jnp.where(low, _dot_tn(wsm_ref[2 * pair], dp_), _dot_tn(wsm_ref[2 * pair + 1], dp_)))
                zero = jnp.zeros_like(dp_)
                dws_ref[2 * pair] += _dot_nt(jnp.where(low, dp_, zero), vp)
                dws_ref[2 * pair + 1] += _dot_nt(jnp.where(low, zero, dp_), vp)
            dvn = jnp.concatenate(dvn_cols, axis=1)
            vec_ref[2:3, :] += _colsum(dvn * xhat)
            vec_ref[3:4, :] += _colsum(dvn)
            dxh = dvn * glng_ref[...]
            am = amat_ref[...]
            da = rstd * (dxh - _split_dot(dxh, am) - xhat * _split_dot(dxh * xhat, am))
            dgv_ref[...] = (da * _gelu_grad(gv_ref[...], ta)).astype(BF16)

            tk = [jnp.zeros((2 * BLOCK, LANES), F32) for _ in range(N_KV)]
            tv = [jnp.zeros((2 * BLOCK, LANES), F32) for _ in range(N_KV)]
            dq_cols = []
            for pair in range(N_HEADS // 2):
                d_pair = d_attn[:, pair * LANES:(pair + 1) * LANES]
                q_pair = q_blk[:, pair * LANES:(pair + 1) * LANES]
                dq_pair = None
                for par in range(2):
                    h = 2 * pair + par
                    kvh = h // (N_HEADS // N_KV)
                    p, ps = probs[h]
                    sel = low if par == 0 else jnp.logical_not(low)
                    do_h = jnp.where(sel, d_pair, 0.0).astype(BF16)
                    q_h = jnp.where(sel, q_pair, jnp.zeros_like(q_pair))
                    dp = _dot_nt(do_h, vvar[kvh][par])
                    delta = jnp.sum(p * dp, axis=-1, keepdims=True)
                    ds = p * (dp - delta)
                    dsink_ref[h] += -(ps * delta)
                    gbias_ref[h] += ds
                    dsb = ds.astype(BF16)
                    dqh = _dot(dsb, kvar[kvh][par])
                    dq_pair = dqh if dq_pair is None else dq_pair + dqh
                    tk[kvh] = tk[kvh] + _dot_tn(dsb, q_h)
                    tv[kvh] = tv[kvh] + _dot_tn(p.astype(BF16), do_h)
                dq_cols.append(dq_pair)
            dq_ref[...] = (jnp.concatenate(dq_cols, axis=1) * Q_SCALE).astype(BF16)
            dkk = _fold_kv(tk[0], tk[1])
            dvv = _fold_kv(tv[0], tv[1])
            dkv_ref[...] = (carry[...] + jnp.concatenate([dkk[:BLOCK], dvv[:BLOCK]], axis=1)).astype(BF16)
            carry[...] = jnp.concatenate([dkk[BLOCK:], dvv[BLOCK:]], axis=1)

    last = nblk - 1
    cur = lambda w: pl.BlockSpec((BLOCK, w), lambda n: (jnp.minimum(n, last), 0))
    prev = lambda w: pl.BlockSpec((BLOCK, w), lambda n: (jnp.clip(n - 1, 0, last), 0))
    outs = pl.pallas_call(
        body, name="bwd_mix",
        out_shape=[jax.ShapeDtypeStruct((s, ATTN_W), BF16), jax.ShapeDtypeStruct((s, 2 * KV_W), BF16),
                   jax.ShapeDtypeStruct((s, GMLP_W), BF16), jax.ShapeDtypeStruct((s, GMLP_W), BF16),
                   jax.ShapeDtypeStruct((N_HEADS, BLOCK, 2 * BLOCK), F32),
                   jax.ShapeDtypeStruct((N_GROUPS, BLOCK, BLOCK), F32),
                   jax.ShapeDtypeStruct((BLOCK, GMLP_W), F32), jax.ShapeDtypeStruct((8, GMLP_W), F32),
                   jax.ShapeDtypeStruct((N_HEADS, BLOCK, 1), F32)]
        + [jax.ShapeDtypeStruct(_rx_shape(p.shape, k), BF16) for p, k in zip(grad_parts, grad_kinds)],
        grid=(nblk + 1,),
        in_specs=[cur(ATTN_W), cur(2 * KV_W), prev(2 * KV_W), cur(GMLP_W), cur(GMLP_W), cur(D_MODEL),
                  _const_spec((N_HEADS, BLOCK, 2 * BLOCK)), pl.BlockSpec(memory_space=pltpu.SMEM),
                  _const_spec((1, GMLP_W)), _const_spec((1, GMLP_W)), _const_spec((N_GROUPS, BLOCK, BLOCK)),
                  _const_spec((BLOCK, GMLP_W)), _const_spec((GMLP_W, GMLP_W)), _const_spec((1, ATTN_W)),
                  _const_spec((1, GMLP_W))] + [ANY] * n_g,
        out_specs=[cur(ATTN_W), prev(2 * KV_W), cur(GMLP_W), cur(GMLP_W),
                   _const_spec((N_HEADS, BLOCK, 2 * BLOCK)), _const_spec((N_GROUPS, BLOCK, BLOCK)),
                   _const_spec((BLOCK, GMLP_W)), _const_spec((8, GMLP_W)), _const_spec((N_HEADS, BLOCK, 1))]
        + [ANY] * n_g,
        scratch_shapes=[pltpu.VMEM((BLOCK, 2 * KV_W), F32)] + _ChipExchange.sems(n_g),
        compiler_params=_params(("arbitrary",)),
    )(q, kv, kv, gu, gv, dmix, bias, sinks, gln_g, gln_b, wsm, bsx, amat, aog, gog, *grad_parts)
    return outs[:9], outs[9:]


def _mix_finalize(gbias, bucket, dws, dbs, dsink):
    def body(gb_ref, bucket_ref, dws_ref, dbs_ref, dsink_ref, drb_ref, dwsm_ref, dbsg_ref, dsk_ref):
        bk = bucket_ref[...]
        lane = lax.broadcasted_iota(jnp.int32, (N_BUCKETS, LANES), 1)
        rowi = lax.broadcasted_iota(jnp.int32, (N_BUCKETS, LANES), 0)
        drb = jnp.zeros((N_BUCKETS, LANES), F32)
        dsk = jnp.zeros((8, LANES), F32)
        lane8 = lax.broadcasted_iota(jnp.int32, (8, LANES), 1)
        for h in range(N_HEADS):
            g = gb_ref[h]
            for b in range(N_BUCKETS):
                tot = jnp.sum(_colsum(jnp.where(bk == float(b), g, 0.0)), axis=1, keepdims=True)
                drb = jnp.where((lane == h) & (rowi == b), tot, drb)
            sk = jnp.sum(dsink_ref[h], axis=0, keepdims=True)
            dsk = jnp.where(lane8 == h, sk, dsk)
        drb_ref[...] = drb
        dsk_ref[...] = dsk
        ti = lax.broadcasted_iota(jnp.int32, (BLOCK, BLOCK), 0)
        ui = lax.broadcasted_iota(jnp.int32, (BLOCK, BLOCK), 1)
        for g in range(N_GROUPS):
            dwsm_ref[g] = jnp.where(ti >= ui, dws_ref[g], 0.0)
        gi = lax.broadcasted_iota(jnp.int32, (GMLP_W, LANES), 0) // GROUP_DIM
        li = lax.broadcasted_iota(jnp.int32, (GMLP_W, LANES), 1)
        ind = jnp.where(gi == li, 1.0, 0.0).astype(BF16)
        d = dbs_ref[...]
        hi = d.astype(BF16)
        r1 = d - hi.astype(F32)
        mid = r1.astype(BF16)
        lo = (r1 - mid.astype(F32)).astype(BF16)
        dbsg_ref[...] = _dot(hi, ind) + _dot(mid, ind) + _dot(lo, ind)

    return pl.pallas_call(
        body, name="mix_finalize",
        out_shape=(jax.ShapeDtypeStruct((N_BUCKETS, LANES), F32), jax.ShapeDtypeStruct((N_GROUPS, BLOCK, BLOCK), F32),
                   jax.ShapeDtypeStruct((BLOCK, LANES), F32), jax.ShapeDtypeStruct((8, LANES), F32)),
        grid=(1,),
        in_specs=[_const_spec((N_HEADS, BLOCK, 2 * BLOCK)), _const_spec((BLOCK, 2 * BLOCK)),
                  _const_spec((N_GROUPS, BLOCK, BLOCK)), _const_spec((BLOCK, GMLP_W)),
                  _const_spec((N_HEADS, BLOCK, 1))],
        out_specs=(_const_spec((N_BUCKETS, LANES)), _const_spec((N_GROUPS, BLOCK, BLOCK)),
                   _const_spec((BLOCK, LANES)), _const_spec((8, LANES))),
        compiler_params=_params(("arbitrary",)),
    )(gbias, bucket, dws, dbs, dsink)


def _bwd_in(dq, dkv, dgu, dgv, dxa, x, modr, w_in, tm):
    s = x.shape[0]

    def body(dq_ref, dkv_ref, dgu_ref, dgv_ref, dxa_ref, x_ref, mod_ref, w_ref, dproj_ref, gx_ref, acc_ref, db_ref):
        i = pl.program_id(0)

        @pl.when(i == 0)
        def _():
            acc_ref[...] = jnp.zeros_like(acc_ref)
            db_ref[...] = jnp.zeros_like(db_ref)

        dproj = jnp.concatenate([dq_ref[...], dkv_ref[...], dgu_ref[...], dgv_ref[...]], axis=1)
        dproj_ref[...] = dproj
        dh1 = _dot_nt(dproj, w_ref[...])
        gx_ref[...] = dxa_ref[...] + dh1 * (1.0 + mod_ref[1:2, :])
        acc_ref[0:1, :] += _colsum(dh1 * x_ref[...])
        acc_ref[1:2, :] += _colsum(dh1)
        db_ref[0:1, :] += _colsum(dproj.astype(F32))

    row = lambda w: pl.BlockSpec((tm, w), lambda i: (i, 0))
    return pl.pallas_call(
        body, name="bwd_in",
        out_shape=(jax.ShapeDtypeStruct((s, IN_W), BF16), jax.ShapeDtypeStruct((s, D_MODEL), F32),
                   jax.ShapeDtypeStruct((8, D_MODEL), F32), jax.ShapeDtypeStruct((8, IN_W), F32)),
        grid=(s // tm,),
        in_specs=[row(ATTN_W), row(2 * KV_W), row(GMLP_W), row(GMLP_W), row(D_MODEL), row(D_MODEL),
                  _const_spec((8, D_MODEL)), _const_spec((D_MODEL, IN_W))],
        out_specs=(row(IN_W), row(D_MODEL), _const_spec((8, D_MODEL)), _const_spec((8, IN_W))),
        compiler_params=_params(("arbitrary",)),
    )(dq, dkv, dgu, dgv, dxa, x, modr, w_in)


def _wgrad(a, b, tm, tk, name):
    k_all, m = a.shape
    n = b.shape[1]
    nk = k_all // tk

    def body(a_ref, b_ref, o_ref, ob_ref):
        k = pl.program_id(1)

        @pl.when(k == 0)
        def _():
            o_ref[...] = jnp.zeros_like(o_ref)

        o_ref[...] += _dot_tn(a_ref[...], b_ref[...])

        @pl.when(k == nk - 1)
        def _():
            ob_ref[...] = o_ref[...].astype(BF16)

    out_spec = pl.BlockSpec((tm, n), lambda i, k: (i, 0))
    return pl.pallas_call(
        body, name=name, out_shape=(jax.ShapeDtypeStruct((m, n), F32), jax.ShapeDtypeStruct((m, n), BF16)),
        grid=(m // tm, nk),
        in_specs=[pl.BlockSpec((tk, tm), lambda i, k: (k, i)), pl.BlockSpec((tk, n), lambda i, k: (k, 0))],
        out_specs=(out_spec, out_spec),
        compiler_params=_params(("parallel", "arbitrary")),
    )(a, b)


def _adam_math(w, g, m, v):
    m2 = ADAM_B1 * m + (1.0 - ADAM_B1) * g
    v2 = ADAM_B2 * v + (1.0 - ADAM_B2) * (g * g)
    m_hat = m2 / (1.0 - ADAM_B1 ** ADAM_STEP)
    v_hat = v2 / (1.0 - ADAM_B2 ** ADAM_STEP)
    delta = -ADAM_LR * (m_hat / (jnp.sqrt(v_hat) + ADAM_EPS) + ADAM_WD * w)
    return delta, m2, v2


def _adam_halves(w, mine, got, m, v, tr, name):
    r, cc = w.shape
    h = r // 2
    nt = h // tr

    def body(c_ref, w_ref, mine_ref, got_ref, m_ref, v_ref, g_ref, d_ref, m2_ref, v2_ref):
        g = jnp.where(pl.program_id(0) == c_ref[0], mine_ref[...], got_ref[...])
        g_ref[...] = g
        d, m2, v2 = _adam_math(w_ref[...], g, m_ref[...], v_ref[...])
        d_ref[...] = d
        m2_ref[...] = m2
        v2_ref[...] = v2

    full = pl.BlockSpec((tr, cc), lambda hh, i, c_ref: (hh * nt + i, 0))
    half = pl.BlockSpec((tr, cc), lambda hh, i, c_ref: (i, 0))
    shp = jax.ShapeDtypeStruct((r, cc), F32)
    return pl.pallas_call(
        body, name=name, out_shape=(shp, shp, shp, shp),
        grid_spec=pltpu.PrefetchScalarGridSpec(
            num_scalar_prefetch=1, grid=(2, nt), in_specs=[full, half, half, full, full],
            out_specs=(full, full, full, full)),
        compiler_params=_params(("arbitrary", "arbitrary")),
    )(_core_index_scalar(), w, mine, got, m, v)


def _adam_w_ada(sc_t, dmod_cols, w, m, v, tr):
    r, cc = w.shape

    def body(sct_ref, dm_ref, w_ref, m_ref, v_ref, g_ref, d_ref, m2_ref, v2_ref):
        g = sct_ref[:, 0:1] * dm_ref[0:1, :]
        for k in range(1, N_DEV):
            g = g + sct_ref[:, k:k + 1] * dm_ref[k:k + 1, :]
        g_ref[...] = g
        d, m2, v2 = _adam_math(w_ref[...], g, m_ref[...], v_ref[...])
        d_ref[...] = d
        m2_ref[...] = m2
        v2_ref[...] = v2

    spec = pl.BlockSpec((tr, cc), lambda i: (i, 0))
    shp = jax.ShapeDtypeStruct((r, cc), F32)
    return pl.pallas_call(
        body, name="adam_w_ada", out_shape=(shp, shp, shp, shp), grid=(r // tr,),
        in_specs=[pl.BlockSpec((tr, N_DEV), lambda i: (i, 0)), _const_spec((N_DEV, cc)), spec, spec, spec],
        out_specs=(spec, spec, spec, spec), compiler_params=_params(("parallel",)),
    )(sc_t, dmod_cols, w, m, v)


def _adam_small(gathered, w, m, v, loss_rows):
    _, r, _ = gathered.shape
    lo, hi = loss_rows

    def body(ga_ref, w_ref, m_ref, v_ref, g_ref, d_ref, m2_ref, v2_ref, loss_ref):
        g = ga_ref[0]
        for k in range(1, N_DEV):
            g = g + ga_ref[k]
        g_ref[...] = g
        d, m2, v2 = _adam_math(w_ref[...], g, m_ref[...], v_ref[...])
        d_ref[...] = d
        m2_ref[...] = m2
        v2_ref[...] = v2
        tot = jnp.sum(_colsum(g[lo:hi, :]), axis=1, keepdims=True)
        loss_ref[...] = jnp.broadcast_to(tot * (0.5 / D_MODEL), loss_ref.shape)

    shp = jax.ShapeDtypeStruct((r, LANES), F32)
    spec = _const_spec((r, LANES))
    return pl.pallas_call(
        body, name="adam_small", out_shape=(shp, shp, shp, shp, jax.ShapeDtypeStruct((8, LANES), F32)), grid=(1,),
        in_specs=[_const_spec((N_DEV, r, LANES)), spec, spec, spec],
        out_specs=(spec, spec, spec, spec, _const_spec((8, LANES))),
        compiler_params=_params(("arbitrary",)),
    )(gathered, w, m, v)


SMALL_NAMES = ("loss", "rel_bias", "b_ada", "b_in", "attn_sinks", "gmlp_ln_g", "gmlp_ln_b", "gmlp_w_s", "gmlp_b_s",
               "attn_out_g", "gmlp_out_g", "ln1_g", "ln1_b", "ln2_g", "ln2_b")
PACK_TILE = 8 * LANES


def _pack(items):
    parts, layout, off = [], {}, 0
    for name in SMALL_NAMES:
        a = items[name]
        flat = a.reshape(-1).astype(F32)
        n = flat.shape[0]
        padded = -(-n // PACK_TILE) * PACK_TILE
        parts.append(jnp.pad(flat, (0, padded - n)))
        layout[name] = (off // LANES, padded // LANES, a.shape, n)
        off += padded
    return jnp.concatenate(parts).reshape(-1, LANES), layout


def _unpack(buf, layout, name):
    r0, nr, shape, n = layout[name]
    return buf[r0:r0 + nr].reshape(-1)[:n].reshape(shape)


def _reduce_scatter_tail(fulls, parts, rxs, kinds, tiles, tag):
    mine = [_sum_chips(p, rx, k, tr, "rs_sum_%s_%d" % (tag, i))
            for i, (p, rx, k, tr) in enumerate(zip(parts, rxs, kinds, tiles))]
    got = _share_halves(mine, "rs_share_" + tag)
    return mine, got


def kernel(x, c, rel_bias, w_ada, b_ada, w_in, b_in, attn_sinks, gmlp_ln_g, gmlp_ln_b, gmlp_w_s, gmlp_b_s, attn_out_g, gmlp_out_g, w_out, ln1_g, ln1_b, w_gate_up, w_down, ln2_g, ln2_b, loss_target, m_rel_bias, m_w_ada, m_b_ada, m_w_in, m_b_in, m_attn_sinks, m_gmlp_ln_g, m_gmlp_ln_b, m_gmlp_w_s, m_gmlp_b_s, m_attn_out_g, m_gmlp_out_g, m_w_out, m_ln1_g, m_ln1_b, m_w_gate_up, m_w_down, m_ln2_g, m_ln2_b, v_rel_bias, v_w_ada, v_b_ada, v_w_in, v_b_in, v_attn_sinks, v_gmlp_ln_g, v_gmlp_ln_b, v_gmlp_w_s, v_gmlp_b_s, v_attn_out_g, v_gmlp_out_g, v_w_out, v_ln1_g, v_ln1_b, v_w_gate_up, v_w_down, v_ln2_g, v_ln2_b):
    ix, iy, ic = _my_pos()
    chip = 2 * ix + iy
    dev = 4 * ix + 2 * iy + ic
    s = x.shape[1]
    xs = x[0]
    tgt = loss_target[0]
    tm_big = min(512, s)
    tm_ffn = min(256, s)
    n_ada, n_in, n_gu = w_ada.shape[2], w_in.shape[2], w_gate_up.shape[2]

    c_all = _allgather8(jnp.pad(c, ((0, 7), (0, 0))), "gather_c").reshape(N_DEV, 8, D_MODEL)[:, 0, :]
    sc_all, mod_cols = _mod_part(c_all, w_ada[0], lax.dynamic_slice_in_dim(b_ada, chip * n_ada, n_ada, axis=1))
    mod_all = _allgather8(mod_cols, "gather_mod").reshape(N_DEV, N_DEV, -1)
    mod_row = lax.dynamic_index_in_dim(mod_all[0::2], dev, axis=1, keepdims=False)
    modr = jnp.pad(mod_row.reshape(6, D_MODEL), ((0, 2), (0, 0)))

    w_in_s, w_out_s = w_in[0].astype(BF16), w_out[0].astype(BF16)
    w_gu_s, w_dn_s = w_gate_up[0].astype(BF16), w_down[0].astype(BF16)
    w_in_g, w_out_g = _gather_weights([w_in_s, w_out_s], ["blk", "blk"], "gather_w_in_out")
    w_in_g = _insert_own(w_in_g, w_in_s, "blk", chip)
    w_out_f = _insert_own(w_out_g, w_out_s, "blk", chip).reshape(D_MODEL, D_MODEL)
    w_in_f = jnp.transpose(w_in_g, (1, 0, 2)).reshape(D_MODEL, IN_W)

    bucket = _bucket_table()
    bias, wsm = _prep_tables(bucket, rel_bias, gmlp_w_s[0])
    bsx = jnp.repeat(gmlp_b_s[0].T, GROUP_DIM, axis=1)
    amat = _group_mean_matrix()
    sinks = attn_sinks[0]

    h1, q, kv, gu, gv = _fwd_in(xs, modr, w_in_f, b_in, tm_big)
    x1, y, mixed, (w_gu_g, w_dn_g) = _fwd_mix(
        q, kv, gu, gv, xs, modr, bias, sinks, gmlp_ln_g, gmlp_ln_b, wsm, bsx, amat, attn_out_g, gmlp_out_g, w_out_f,
        ln1_g, ln1_b, tm_big, [w_gu_s, w_dn_s], ["cols", "blk"])
    w_gu_f = _insert_own(w_gu_g, w_gu_s, "cols", chip)
    w_dn_f = _insert_own(w_dn_g, w_dn_s, "blk", chip).reshape(D_FF, D_MODEL)
    h2, act, dy2, dx1a, acc_f = _fwd_ffn(x1, tgt, modr, ln2_g, ln2_b, w_gu_f, w_dn_f, tm_ffn)

    a_act, dgu_ff, dh2 = _bwd_ffn(dy2, act, w_gu_f, w_dn_f, tm_ffn)
    g_dn, g_dn_b = _wgrad(a_act, dy2, D_FF // 2, min(512, s), "wgrad_down")
    g_gu, g_gu_b = _wgrad(h2, dgu_ff, 512, min(256, s), "wgrad_gate_up")
    dxa, dy, dmix, acc_m = _bwd_mid(dh2, dx1a, x1, xs, y, modr, ln1_g, w_out_f, tm_big)
    g_out, g_out_b = _wgrad(mixed, dy, 512, min(512, s), "wgrad_out")
    blk3 = lambda a, rows: a.reshape(N_CHIPS, rows, a.shape[1])
    kinds_a = ["blk", "cols", "blk"]
    fulls_a = [blk3(g_dn, D_FF // N_CHIPS), g_gu, blk3(g_out, D_MODEL // N_CHIPS)]
    fulls_a_b = [blk3(g_dn_b, D_FF // N_CHIPS), g_gu_b, blk3(g_out_b, D_MODEL // N_CHIPS)]
    gots_a = _swap_halves(fulls_a_b, kinds_a, "rs_swap_a")
    parts_a = [_add_halves(f, g, k, "rs_add_a%d" % i) for i, (f, g, k) in enumerate(zip(fulls_a, gots_a, kinds_a))]
    (dq, dkv, dgu, dgv, gbias, dws, dbs, vec, dsink), rxs_a = _bwd_mix(
        q, kv, gu, gv, dmix, bias, sinks, gmlp_ln_g, gmlp_ln_b, wsm, bsx, amat, attn_out_g, gmlp_out_g,
        [p[1] for p in parts_a], kinds_a)
    drb, dwsm, dbsg, dsk = _mix_finalize(gbias, bucket, dws, dbs, dsink)
    dproj, grad_x, acc_i, db_in = _bwd_in(dq, dkv, dgu, dgv, dxa, xs, modr, w_in_f, tm_big)
    g_in, g_in_b = _wgrad(h1, dproj, 512, min(512, s), "wgrad_in")

    dmod = jnp.concatenate([acc_i[1], acc_i[0], acc_m[4], acc_m[1], acc_m[0], acc_f[3]])
    small_g = {
        "loss": acc_f[0], "rel_bias": drb[:, :N_HEADS], "b_ada": dmod.reshape(1, -1), "b_in": db_in[0:1],
        "attn_sinks": dsk[0:1, :N_HEADS], "gmlp_ln_g": vec[2:3], "gmlp_ln_b": vec[3:4], "gmlp_w_s": dwsm[None],
        "gmlp_b_s": dbsg[:, :N_GROUPS].T[None], "attn_out_g": vec[0:1], "gmlp_out_g": vec[1:2],
        "ln1_g": acc_m[2:3], "ln1_b": acc_m[3:4], "ln2_g": acc_f[1:2], "ln2_b": acc_f[2:3]}
    zero_loss = jnp.zeros((D_MODEL,), F32)
    small_w = {"loss": zero_loss, "rel_bias": rel_bias, "b_ada": b_ada, "b_in": b_in, "attn_sinks": attn_sinks,
               "gmlp_ln_g": gmlp_ln_g, "gmlp_ln_b": gmlp_ln_b, "gmlp_w_s": gmlp_w_s, "gmlp_b_s": gmlp_b_s,
               "attn_out_g": attn_out_g, "gmlp_out_g": gmlp_out_g, "ln1_g": ln1_g, "ln1_b": ln1_b, "ln2_g": ln2_g,
               "ln2_b": ln2_b}
    small_m = {"loss": zero_loss, "rel_bias": m_rel_bias, "b_ada": m_b_ada, "b_in": m_b_in, "attn_sinks": m_attn_sinks,
               "gmlp_ln_g": m_gmlp_ln_g, "gmlp_ln_b": m_gmlp_ln_b, "gmlp_w_s": m_gmlp_w_s, "gmlp_b_s": m_gmlp_b_s,
               "attn_out_g": m_attn_out_g, "gmlp_out_g": m_gmlp_out_g, "ln1_g": m_ln1_g, "ln1_b": m_ln1_b,
               "ln2_g": m_ln2_g, "ln2_b": m_ln2_b}
    small_v = {"loss": zero_loss + 1.0, "rel_bias": v_rel_bias, "b_ada": v_b_ada, "b_in": v_b_in,
               "attn_sinks": v_attn_sinks, "gmlp_ln_g": v_gmlp_ln_g, "gmlp_ln_b": v_gmlp_ln_b, "gmlp_w_s": v_gmlp_w_s,
               "gmlp_b_s": v_gmlp_b_s, "attn_out_g": v_attn_out_g, "gmlp_out_g": v_gmlp_out_g, "ln1_g": v_ln1_g,
               "ln1_b": v_ln1_b, "ln2_g": v_ln2_g, "ln2_b": v_ln2_b}
    pg, layout = _pack(small_g)
    pw, _ = _pack(small_w)
    pm, _ = _pack(small_m)
    pv, _ = _pack(small_v)
    rows = pg.shape[0]
    gathered = _allgather8(pg, "gather_small").reshape(N_DEV, rows, LANES)
    l0, ln_, _, _ = layout["loss"]
    sg, sd, sm, sv, loss_t = _adam_small(gathered, pw, pm, pv, (l0, l0 + ln_))
    loss = loss_t[0, 0]

    b0, bn, _, _ = layout["b_ada"]
    dmod_all = gathered[:, b0:b0 + bn, :].reshape(N_DEV, -1)[:, :6 * D_MODEL]
    dmod_cols = lax.dynamic_slice_in_dim(dmod_all, chip * n_ada, n_ada, axis=1)
    g_ada, d_ada, m_ada, v_ada = _adam_w_ada(sc_all.T, dmod_cols, w_ada[0], m_w_ada[0], v_w_ada[0], 256)

    to_blk = lambda a: jnp.transpose(a.reshape(D_MODEL, N_CHIPS, n_in), (1, 0, 2))
    full_in, full_in_b = to_blk(g_in), to_blk(g_in_b)
    (got_in,) = _swap_halves([full_in_b], ["blk"], "rs_swap_b")
    part_in = _add_halves(full_in, got_in, "blk", "rs_add_b")
    (rx_in,) = _exchange_chip_partials([part_in[1]], ["blk"], "rs_chips_b")
    mine_a, got_a = _reduce_scatter_tail(fulls_a, [p[0] for p in parts_a], rxs_a, kinds_a, [176, 256, 128], "a")
    mine_b, got_b = _reduce_scatter_tail([full_in], [part_in[0]], [rx_in], ["blk"], [256], "b")

    gs_dn, d_dn, m_dn, v_dn = _adam_halves(w_down[0], mine_a[0], got_a[0], m_w_down[0], v_w_down[0], 176, "adam_w_down")
    gs_gu, d_gu, m_gu, v_gu = _adam_halves(w_gate_up[0], mine_a[1], got_a[1], m_w_gate_up[0], v_w_gate_up[0], 256,
                                           "adam_w_gate_up")
    gs_out, d_out, m_out, v_out = _adam_halves(w_out[0], mine_a[2], got_a[2], m_w_out[0], v_w_out[0], 128, "adam_w_out")
    gs_in, d_in, m_in, v_in = _adam_halves(w_in[0], mine_b[0], got_b[0], m_w_in[0], v_w_in[0], 256, "adam_w_in")

    big = {"w_ada": (g_ada, d_ada, m_ada, v_ada), "w_in": (gs_in, d_in, m_in, v_in), "w_out": (gs_out, d_out, m_out, v_out),
           "w_gate_up": (gs_gu, d_gu, m_gu, v_gu), "w_down": (gs_dn, d_dn, m_dn, v_dn)}
    order = ["rel_bias", "w_ada", "b_ada", "w_in", "b_in", "attn_sinks", "gmlp_ln_g", "gmlp_ln_b", "gmlp_w_s", "gmlp_b_s",
             "attn_out_g", "gmlp_out_g", "w_out", "ln1_g", "ln1_b", "w_gate_up", "w_down", "ln2_g", "ln2_b"]
    outs = [loss, grad_x[None]]
    for k, packed in enumerate((sg, sd, sm, sv)):
        for name in order:
            if name in big:
                outs.append(big[name][k][None])
            else:
                outs.append(_unpack(packed, layout, name))
    return tuple(outs)
```

```python
import math

import numpy as np
import jax
import jax.numpy as jnp
from jax import lax
from jax.experimental import pallas as pl
from jax.experimental.pallas import tpu as pltpu

F32 = jnp.float32
BF16 = jnp.bfloat16
MESH = pl.DeviceIdType.MESH

D_MODEL = 1024
N_HEADS = 8
N_KV = 2
HEAD_DIM = 64
ATTN_W = N_HEADS * HEAD_DIM
KV_W = N_KV * HEAD_DIM
N_GROUPS = 8
GROUP_DIM = 64
GMLP_W = N_GROUPS * GROUP_DIM
IN_W = ATTN_W + 2 * KV_W + 2 * GMLP_W
BLOCK = 128
N_BUCKETS = 32
MAX_DISTANCE = 128
D_FF = 2816
ALPHA = 2.0 ** 0.25
LN_EPS = 1e-5
NEG_INF = -1e30
ADAM_LR, ADAM_B1, ADAM_B2, ADAM_EPS, ADAM_WD, ADAM_STEP = 0.001, 0.9, 0.999, 1e-8, 0.01, 10
N_CHIPS = 4
N_DEV = 8
LANES = 128
V7X_VMEM_LIMIT = 56 * 2 ** 20
GELU_C = math.sqrt(2.0 / math.pi)
Q_SCALE = HEAD_DIM ** -0.5
ANY = pl.BlockSpec(memory_space=pl.ANY)


def _params(sem=None):
    return pltpu.CompilerParams(dimension_semantics=sem, vmem_limit_bytes=V7X_VMEM_LIMIT)


def _const_spec(shape, single=False):
    nd = len(shape)
    if single:
        return pl.BlockSpec(shape, lambda *_: (0,) * nd, pipeline_mode=pl.Buffered(1))
    return pl.BlockSpec(shape, lambda *_: (0,) * nd)


def _dot(a, b):
    return jnp.dot(a, b, preferred_element_type=F32)


def _dot_nt(a, b):
    return lax.dot_general(a, b, (((1,), (1,)), ((), ())), preferred_element_type=F32)


def _dot_tn(a, b):
    return lax.dot_general(a, b, (((0,), (0,)), ((), ())), preferred_element_type=F32)


def _gelu(x):
    t = jnp.tanh(GELU_C * (x + 0.044715 * x * x * x))
    return 0.5 * x * (1.0 + t), t


def _gelu_grad(x, t):
    return 0.5 * (1.0 + t) + 0.5 * x * (1.0 - t * t) * GELU_C * (1.0 + 3.0 * 0.044715 * x * x)


def _split_dot(x, a):
    hi = x.astype(BF16)
    lo = (x - hi.astype(F32)).astype(BF16)
    return _dot(hi, a) + _dot(lo, a)


def _group_mean_matrix():
    g = np.arange(GMLP_W) // GROUP_DIM
    return jnp.asarray((g[:, None] == g[None, :]).astype(np.float32) / GROUP_DIM, dtype=BF16)


def _ln_stats(z):
    mu = jnp.mean(z, axis=-1, keepdims=True)
    d = z - mu
    var = jnp.mean(d * d, axis=-1, keepdims=True)
    rstd = lax.rsqrt(var + LN_EPS)
    return d * rstd, rstd


def _ln_bwd(dxhat, xhat, rstd):
    m1 = jnp.mean(dxhat, axis=-1, keepdims=True)
    m2 = jnp.mean(dxhat * xhat, axis=-1, keepdims=True)
    return rstd * (dxhat - m1 - xhat * m2)


def _colsum(x):
    return jnp.sum(x, axis=0, keepdims=True)


def _my_pos():
    return lax.axis_index("x"), lax.axis_index("y"), lax.axis_index("c")


def _other_chips(x, y):
    return [(1 - x, y), (x, 1 - y), (1 - x, 1 - y)]


def _chip_index_scalar():
    ix, iy, _ = _my_pos()
    return jnp.reshape(2 * ix + iy, (1,)).astype(jnp.int32)


def _core_index_scalar():
    return jnp.reshape(lax.axis_index("c"), (1,)).astype(jnp.int32)


def _allgather8(v, name):
    m_per, n = v.shape

    def body(x_ref, out_ref, send_sems, recv_sems, local_sem):
        x, y, c = _my_pos()
        me, sibling = (x, y, c), (x, y, 1 - c)
        chips = _other_chips(x, y)

        def rows(px, py, pc):
            return out_ref.at[pl.ds((4 * px + 2 * py + pc) * m_per, m_per), :]

        def copy(k, block, to, src=None):
            return pltpu.make_async_remote_copy(
                src_ref=rows(*block) if src is None else src, dst_ref=rows(*block),
                send_sem=send_sems.at[k], recv_sem=recv_sems.at[k], device_id=to, device_id_type=MESH)

        mine = pltpu.make_async_copy(x_ref, rows(*me), local_sem)
        mine.start()
        first = [copy(0, me, sibling, src=x_ref)]
        first += [copy(1 + j, me, (*chip, c), src=x_ref) for j, chip in enumerate(chips)]
        for cp in first:
            cp.start()
        passed = [copy(4 + j, (*chip, c), sibling) for j, chip in enumerate(chips)]
        for j, chip in enumerate(chips):
            copy(1 + j, (*chip, c), me).wait_recv()
            passed[j].start()
        copy(0, sibling, me).wait_recv()
        for j, chip in enumerate(chips):
            copy(4 + j, (*chip, 1 - c), me).wait_recv()
        for cp in first + passed:
            cp.wait_send()
        mine.wait()

    return pl.pallas_call(
        body, name=name,
        out_shape=jax.ShapeDtypeStruct((N_DEV * m_per, n), v.dtype),
        in_specs=[pl.BlockSpec(memory_space=pltpu.VMEM)],
        out_specs=pl.BlockSpec(memory_space=pltpu.VMEM),
        scratch_shapes=[pltpu.SemaphoreType.DMA((7,)), pltpu.SemaphoreType.DMA((7,)), pltpu.SemaphoreType.DMA],
        compiler_params=pltpu.CompilerParams(vmem_limit_bytes=V7X_VMEM_LIMIT),
    )(v)


def _gathered_shape(shard, kind):
    r, cc = shard.shape
    return (N_CHIPS, r, cc) if kind == "blk" else (r, N_CHIPS * cc)


class _WeightGather:
    def __init__(self, shards, gathered, kinds, send_sems, recv_sems):
        self.shards, self.gathered, self.kinds = shards, gathered, kinds
        self.send_sems, self.recv_sems = send_sems, recv_sems
        self.x, self.y, self.c = _my_pos()
        self.chips = _other_chips(self.x, self.y)

    def _dst(self, a, chip, pc):
        r, cc = self.shards[a].shape
        h = r // 2
        g = self.gathered[a]
        if self.kinds[a] == "blk":
            return g.at[chip, pl.ds(pc * h, h), :]
        return g.at[pl.ds(pc * h, h), pl.ds(chip * cc, cc)]

    def _copy(self, a, k, chip, pc, to, src=None):
        d = self._dst(a, chip, pc)
        return pltpu.make_async_remote_copy(
            src_ref=d if src is None else src, dst_ref=d, send_sem=self.send_sems.at[a * 6 + k],
            recv_sem=self.recv_sems.at[a * 6 + k], device_id=to, device_id_type=MESH)

    def _each(self):
        for a in range(len(self.shards)):
            for j, chip in enumerate(self.chips):
                yield a, j, chip, 2 * chip[0] + chip[1]

    def start(self):
        my_chip = 2 * self.x + self.y
        for a, j, chip, _ in self._each():
            h = self.shards[a].shape[0] // 2
            self._copy(a, j, my_chip, self.c, (*chip, self.c), src=self.shards[a].at[pl.ds(self.c * h, h), :]).start()

    def forward(self):
        me, sibling = (self.x, self.y, self.c), (self.x, self.y, 1 - self.c)
        for a, j, chip, cj in self._each():
            self._copy(a, j, cj, self.c, me).wait_recv()
            self._copy(a, 3 + j, cj, self.c, sibling).start()

    def finish(self):
        me = (self.x, self.y, self.c)
        for a, j, chip, cj in self._each():
            self._copy(a, 3 + j, cj, 1 - self.c, me).wait_recv()
        for a, j, chip, cj in self._each():
            self._copy(a, j, cj, self.c, me).wait_send()
            self._copy(a, 3 + j, cj, self.c, me).wait_send()

    @staticmethod
    def sems(n_arr):
        return [pltpu.SemaphoreType.DMA((n_arr * 6,)), pltpu.SemaphoreType.DMA((n_arr * 6,))]


def _insert_own(gathered, shard, kind, chip):
    if kind == "blk":
        return lax.dynamic_update_slice(gathered, shard[None], (chip, 0, 0))
    return lax.dynamic_update_slice(gathered, shard, (0, chip * shard.shape[1]))


def _gather_weights(shards, kinds, name):
    n_arr = len(shards)

    def body(*refs):
        g = _WeightGather(refs[:n_arr], refs[n_arr:2 * n_arr], kinds, *refs[2 * n_arr:])
        g.start()
        g.forward()
        g.finish()

    return pl.pallas_call(
        body, name=name,
        out_shape=[jax.ShapeDtypeStruct(_gathered_shape(s, k), BF16) for s, k in zip(shards, kinds)],
        in_specs=[ANY] * n_arr, out_specs=[ANY] * n_arr, scratch_shapes=_WeightGather.sems(n_arr),
    )(*shards)


def _half_of_full(ref, kind, pc):
    if kind == "blk":
        h = ref.shape[1] // 2
        return ref.at[:, pl.ds(pc * h, h), :]
    h = ref.shape[0] // 2
    return ref.at[pl.ds(pc * h, h), :]


def _half_shape(shape, kind):
    return (shape[0], shape[1] // 2, shape[2]) if kind == "blk" else (shape[0] // 2, shape[1])


def _swap_halves(fulls_bf16, kinds, name):
    n_arr = len(fulls_bf16)

    def body(*refs):
        ins, outs = refs[:n_arr], refs[n_arr:2 * n_arr]
        send_sems, recv_sems = refs[2 * n_arr:]
        x, y, c = _my_pos()
        cps = []
        for a in range(n_arr):
            cp = pltpu.make_async_remote_copy(
                src_ref=_half_of_full(ins[a], kinds[a], 1 - c), dst_ref=outs[a], send_sem=send_sems.at[a],
                recv_sem=recv_sems.at[a], device_id=(x, y, 1 - c), device_id_type=MESH)
            cp.start()
            cps.append(cp)
        for cp in cps:
            cp.wait()

    return pl.pallas_call(
        body, name=name,
        out_shape=[jax.ShapeDtypeStruct(_half_shape(a.shape, k), a.dtype) for a, k in zip(fulls_bf16, kinds)],
        in_specs=[ANY] * n_arr, out_specs=[ANY] * n_arr,
        scratch_shapes=[pltpu.SemaphoreType.DMA((n_arr,)), pltpu.SemaphoreType.DMA((n_arr,))],
    )(*fulls_bf16)


def _add_halves(full, got, kind, name):
    hs = _half_shape(full.shape, kind)

    def body(c_ref, a_ref, b_ref, o_ref, ob_ref):
        p = a_ref[...] + b_ref[...].astype(F32)
        o_ref[...] = p
        ob_ref[...] = p.astype(BF16)

    if kind == "blk":
        nb, h, cc = hs
        own = pl.BlockSpec((1, h, cc), lambda b, c_ref: (b, c_ref[0], 0))
        other = pl.BlockSpec((1, h, cc), lambda b, c_ref: (b, 0, 0))
    else:
        h, cc = hs[0], hs[1] // N_CHIPS
        own = pl.BlockSpec((h, cc), lambda b, c_ref: (c_ref[0], b))
        other = pl.BlockSpec((h, cc), lambda b, c_ref: (0, b))
    return pl.pallas_call(
        body, name=name, out_shape=(jax.ShapeDtypeStruct(hs, F32), jax.ShapeDtypeStruct(hs, BF16)),
        grid_spec=pltpu.PrefetchScalarGridSpec(
            num_scalar_prefetch=1, grid=(N_CHIPS,), in_specs=[own, other], out_specs=(other, other)),
        compiler_params=_params(("arbitrary",)),
    )(_core_index_scalar(), full, got)


def _rx_shape(part_shape, kind):
    if kind == "blk":
        return (3, part_shape[1], part_shape[2])
    return (3, part_shape[0], part_shape[1] // N_CHIPS)


class _ChipExchange:
    def __init__(self, parts, rxs, kinds, send_sems, recv_sems):
        self.parts, self.rxs, self.kinds = parts, rxs, kinds
        self.send_sems, self.recv_sems = send_sems, recv_sems
        self.x, self.y, self.c = _my_pos()
        self.chips = _other_chips(self.x, self.y)

    def _copies(self):
        for a in range(len(self.parts)):
            for j, chip in enumerate(self.chips):
                cj = 2 * chip[0] + chip[1]
                if self.kinds[a] == "blk":
                    src = self.parts[a].at[cj]
                else:
                    cc = self.parts[a].shape[1] // N_CHIPS
                    src = self.parts[a].at[:, pl.ds(cj * cc, cc)]
                yield pltpu.make_async_remote_copy(
                    src_ref=src, dst_ref=self.rxs[a].at[j], send_sem=self.send_sems.at[a * 3 + j],
                    recv_sem=self.recv_sems.at[a * 3 + j], device_id=(*chip, self.c), device_id_type=MESH)

    def start(self):
        for cp in self._copies():
            cp.start()

    def wait(self):
        for cp in self._copies():
            cp.wait_recv()
        for cp in self._copies():
            cp.wait_send()

    @staticmethod
    def sems(n_arr):
        return [pltpu.SemaphoreType.DMA((n_arr * 3,)), pltpu.SemaphoreType.DMA((n_arr * 3,))]


def _sum_chips(part, rx, kind, tr, name):
    _, h, cc = rx.shape
    flips = (2, 1, 3)

    def body(chip_ref, p_ref, rx_ref, o_ref):
        own = p_ref[...].reshape(tr, cc)
        for mc in range(N_CHIPS):
            @pl.when(chip_ref[0] == mc)
            def _():
                terms = sorted([(mc, None)] + [(mc ^ f, j) for j, f in enumerate(flips)])
                acc = None
                for _, j in terms:
                    t = own if j is None else rx_ref[j].astype(F32)
                    acc = t if acc is None else acc + t
                o_ref[...] = acc

    if kind == "blk":
        own_spec = pl.BlockSpec((1, tr, cc), lambda i, chip_ref: (chip_ref[0], i, 0))
    else:
        own_spec = pl.BlockSpec((tr, cc), lambda i, chip_ref: (i, chip_ref[0]))
    return pl.pallas_call(
        body, name=name, out_shape=jax.ShapeDtypeStruct((h, cc), F32),
        grid_spec=pltpu.PrefetchScalarGridSpec(
            num_scalar_prefetch=1, grid=(h // tr,),
            in_specs=[own_spec, pl.BlockSpec((3, tr, cc), lambda i, chip_ref: (0, i, 0))],
            out_specs=pl.BlockSpec((tr, cc), lambda i, chip_ref: (i, 0))),
        compiler_params=_params(("arbitrary",)),
    )(_chip_index_scalar(), part, rx)


def _share_halves(halves, name):
    n_arr = len(halves)

    def body(*refs):
        ins, outs = refs[:n_arr], refs[n_arr:2 * n_arr]
        send_sems, recv_sems = refs[2 * n_arr:]
        x, y, c = _my_pos()
        cps = []
        for a in range(n_arr):
            cp = pltpu.make_async_remote_copy(
                src_ref=ins[a], dst_ref=outs[a], send_sem=send_sems.at[a], recv_sem=recv_sems.at[a],
                device_id=(x, y, 1 - c), device_id_type=MESH)
            cp.start()
            cps.append(cp)
        for cp in cps:
            cp.wait()

    return pl.pallas_call(
        body, name=name, out_shape=[jax.ShapeDtypeStruct(h.shape, h.dtype) for h in halves],
        in_specs=[ANY] * n_arr, out_specs=[ANY] * n_arr,
        scratch_shapes=[pltpu.SemaphoreType.DMA((n_arr,)), pltpu.SemaphoreType.DMA((n_arr,))],
    )(*halves)


def _mod_part(c_all, w_ada_s, b_ada_s):
    n = w_ada_s.shape[1]

    def body(c_ref, w_ref, b_ref, sc_ref, mod_ref):
        cv = c_ref[...]
        sc = cv * (1.0 / (1.0 + jnp.exp(-cv)))
        sc_ref[...] = sc
        a_hi = sc.astype(BF16)
        a_lo = (sc - a_hi.astype(F32)).astype(BF16)
        w = w_ref[...]
        w_hi = w.astype(BF16)
        w_lo = (w - w_hi.astype(F32)).astype(BF16)
        mod_ref[...] = _dot(a_hi, w_hi) + _dot(a_hi, w_lo) + _dot(a_lo, w_hi) + b_ref[...]

    return pl.pallas_call(
        body, name="mod_part",
        out_shape=(jax.ShapeDtypeStruct((N_DEV, D_MODEL), F32), jax.ShapeDtypeStruct((N_DEV, n), F32)),
        grid=(1,),
        in_specs=[_const_spec((N_DEV, D_MODEL)), _const_spec((D_MODEL, n)), _const_spec((1, n))],
        out_specs=(_const_spec((N_DEV, D_MODEL)), _const_spec((N_DEV, n))),
        compiler_params=_params(("arbitrary",)),
    )(c_all, w_ada_s, b_ada_s)


def _bucket_table():
    qi = jnp.arange(BLOCK)[:, None]
    si = jnp.arange(2 * BLOCK)[None, :]
    dist = qi + BLOCK - si
    max_exact = N_BUCKETS // 2
    n = jnp.maximum(dist, 0)
    nf = jnp.maximum(n, max_exact).astype(F32)
    large = max_exact + (jnp.log(nf / max_exact) / math.log(MAX_DISTANCE / max_exact)
                         * (N_BUCKETS - max_exact)).astype(jnp.int32)
    large = jnp.minimum(large, N_BUCKETS - 1)
    return jnp.where(n < max_exact, n, large).astype(F32)


def _prep_tables(bucket, rel_bias, w_s):
    def body(bucket_ref, rb_ref, ws_ref, bias_ref, wsm_ref):
        qi = lax.broadcasted_iota(jnp.int32, (BLOCK, 2 * BLOCK), 0)
        si = lax.broadcasted_iota(jnp.int32, (BLOCK, 2 * BLOCK), 1)
        dist = qi + BLOCK - si
        in_window = (dist >= 0) & (dist < BLOCK)
        bk = bucket_ref[...]
        for h in range(N_HEADS):
            acc = jnp.zeros((BLOCK, 2 * BLOCK), F32)
            for b in range(N_BUCKETS):
                acc = jnp.where(bk == float(b), rb_ref[b, h], acc)
            bias_ref[h] = jnp.where(in_window, acc, NEG_INF)
        ti = lax.broadcasted_iota(jnp.int32, (BLOCK, BLOCK), 0)
        ui = lax.broadcasted_iota(jnp.int32, (BLOCK, BLOCK), 1)
        for g in range(N_GROUPS):
            wsm_ref[g] = jnp.where(ti >= ui, ws_ref[g], 0.0).astype(BF16)

    return pl.pallas_call(
        body, name="prep_tables",
        out_shape=(jax.ShapeDtypeStruct((N_HEADS, BLOCK, 2 * BLOCK), F32),
                   jax.ShapeDtypeStruct((N_GROUPS, BLOCK, BLOCK), BF16)),
        grid=(1,),
        in_specs=[_const_spec((BLOCK, 2 * BLOCK)), pl.BlockSpec(memory_space=pltpu.SMEM),
                  _const_spec((N_GROUPS, BLOCK, BLOCK))],
        out_specs=(_const_spec((N_HEADS, BLOCK, 2 * BLOCK)), _const_spec((N_GROUPS, BLOCK, BLOCK))),
        compiler_params=_params(("arbitrary",)),
    )(bucket, rel_bias, w_s)


def _fwd_in(x, modr, w_in, b_in, tm, shards, kinds):
    s = x.shape[0]
    n_steps = s // tm
    fwd_step = (3 * n_steps) // 4
    n_w = len(shards)

    def body(x_ref, mod_ref, w_ref, b_ref, *rest):
        shard_refs = rest[:n_w]
        h1_ref, q_ref, kv_ref, gu_ref, gv_ref = rest[n_w:n_w + 5]
        gathered_refs = rest[n_w + 5:2 * n_w + 5]
        send_sems, recv_sems = rest[2 * n_w + 5:]
        i = pl.program_id(0)
        gather = _WeightGather(shard_refs, gathered_refs, kinds, send_sems, recv_sems)

        @pl.when(i == 0)
        def _():
            gather.start()

        h1 = (x_ref[...] * (1.0 + mod_ref[1:2, :]) + mod_ref[0:1, :]).astype(BF16)
        h1_ref[...] = h1
        proj = _dot(h1, w_ref[...]) + b_ref[...]
        q_ref[...] = (proj[:, :ATTN_W] * Q_SCALE).astype(BF16)
        kv_ref[...] = proj[:, ATTN_W:ATTN_W + 2 * KV_W].astype(BF16)
        gu_ref[...] = proj[:, ATTN_W + 2 * KV_W:ATTN_W + 2 * KV_W + GMLP_W]
        gv_ref[...] = proj[:, ATTN_W + 2 * KV_W + GMLP_W:]

        @pl.when(i == fwd_step)
        def _():
            gather.forward()

        @pl.when(i == n_steps - 1)
        def _():
            gather.finish()

    row = lambda w: pl.BlockSpec((tm, w), lambda i: (i, 0))
    outs = pl.pallas_call(
        body, name="fwd_in",
        out_shape=[jax.ShapeDtypeStruct((s, D_MODEL), BF16), jax.ShapeDtypeStruct((s, ATTN_W), BF16),
                   jax.ShapeDtypeStruct((s, 2 * KV_W), BF16), jax.ShapeDtypeStruct((s, GMLP_W), F32),
                   jax.ShapeDtypeStruct((s, GMLP_W), F32)]
        + [jax.ShapeDtypeStruct(_gathered_shape(sh, k), BF16) for sh, k in zip(shards, kinds)],
        grid=(n_steps,),
        in_specs=[row(D_MODEL), _const_spec((8, D_MODEL)), _const_spec((D_MODEL, IN_W)), _const_spec((1, IN_W))]
        + [ANY] * n_w,
        out_specs=[row(D_MODEL), row(ATTN_W), row(2 * KV_W), row(GMLP_W), row(GMLP_W)] + [ANY] * n_w,
        scratch_shapes=_WeightGather.sems(n_w),
        compiler_params=_params(("arbitrary",)),
    )(x, modr, w_in, b_in, *shards)
    return outs[:5], outs[5:]


def _kv_variants(kk):
    kf = kk.astype(F32)
    lane = lax.broadcasted_iota(jnp.int32, kf.shape, 1)
    low = lane < HEAD_DIM
    k0_lo = jnp.where(low, kf, 0.0)
    k1_hi = jnp.where(low, 0.0, kf)
    k0_hi = pltpu.roll(k0_lo, HEAD_DIM, 1)
    k1_lo = pltpu.roll(k1_hi, HEAD_DIM, 1)
    return ((k0_lo.astype(BF16), k0_hi.astype(BF16)), (k1_lo.astype(BF16), k1_hi.astype(BF16)))


def _attn_head_probs(q_pair, k_var, bias_h, sink, first_mask):
    logits = _dot_nt(q_pair, k_var) + bias_h
    if first_mask is not None:
        logits = jnp.where(first_mask, NEG_INF, logits)
    m = jnp.maximum(jnp.max(logits, axis=-1, keepdims=True), sink)
    e = jnp.exp(logits - m)
    es = jnp.exp(sink - m)
    inv = 1.0 / (jnp.sum(e, axis=-1, keepdims=True) + es)
    return e * inv, es * inv


def _attn_block_fwd(q_blk, kk, vv, bias_ref, sinks_ref, first_mask):
    kvar = _kv_variants(kk)
    vvar = _kv_variants(vv)
    outs, probs = [], []
    for pair in range(N_HEADS // 2):
        acc = None
        for par in range(2):
            h = 2 * pair + par
            kv = h // (N_HEADS // N_KV)
            p, ps = _attn_head_probs(q_blk[:, pair * LANES:(pair + 1) * LANES], kvar[kv][par], bias_ref[h],
                                     sinks_ref[h], first_mask)
            probs.append((p, ps))
            o = _dot(p.astype(BF16), vvar[kv][par])
            acc = o if acc is None else acc + o
        outs.append(acc)
    return jnp.concatenate(outs, axis=1), probs, kvar, vvar


def _gmlp_chunk_fwd(gu, gv, ln_g, ln_b, wsm_ref, bsx, amat):
    u, tu = _gelu(gu)
    a, ta = _gelu(gv)
    mean = _split_dot(a, amat)
    d = a - mean
    var = _split_dot(d * d, amat)
    rstd = lax.rsqrt(var + LN_EPS)
    xhat = d * rstd
    vb = (xhat * ln_g + ln_b).astype(BF16)
    lane = lax.broadcasted_iota(jnp.int32, (BLOCK, LANES), 1)
    low = lane < GROUP_DIM
    cols = []
    for pair in range(N_GROUPS // 2):
        vp = vb[:, pair * LANES:(pair + 1) * LANES]
        cols.append(jnp.where(low, _dot(wsm_ref[2 * pair], vp), _dot(wsm_ref[2 * pair + 1], vp)))
    mixedv = jnp.concatenate(cols, axis=1) + bsx
    return u * mixedv, (u, tu, ta, xhat, rstd, vb, mixedv)


def _rms(a, g):
    r = lax.rsqrt(jnp.mean(a * a, axis=-1, keepdims=True) + LN_EPS)
    return a * r * g, r


def _fwd_mix(q, kv, gu, gv, x, modr, bias, sinks, gln_g, gln_b, wsm, bsx, amat, aog, gog, w_out, ln1_g, ln1_b, tm,
             ffn_shards, ffn_kinds):
    s = x.shape[0]
    nb = tm // BLOCK
    n_steps = s // tm
    fwd_step = (3 * n_steps) // 4
    n_w = len(ffn_shards)

    def body(q_ref, kv_ref, kvp_ref, gu_ref, gv_ref, x_ref, mod_ref, bias_ref, sinks_ref, glng_ref, glnb_ref, wsm_ref,
             bsx_ref, amat_ref, aog_ref, gog_ref, wout_ref, ln1g_ref, ln1b_ref, *rest):
        shard_refs = rest[:n_w]
        x1_ref, y_ref, mixed_ref = rest[n_w:n_w + 3]
        gathered_refs = rest[n_w + 3:2 * n_w + 3]
        mix_scr, send_sems, recv_sems = rest[2 * n_w + 3:]
        i = pl.program_id(0)
        gather = _WeightGather(shard_refs, gathered_refs, ffn_kinds, send_sems, recv_sems)

        @pl.when(i == 0)
        def _():
            gather.start()

        col = lax.broadcasted_iota(jnp.int32, (BLOCK, 2 * BLOCK), 1)
        for b in range(nb):
            r0 = b * BLOCK
            if b == 0:
                kvprev = kvp_ref[...]
                first_mask = (col < BLOCK) & (i == 0)
            else:
                kvprev = kv_ref[r0 - BLOCK:r0, :]
                first_mask = None
            kvcur = kv_ref[r0:r0 + BLOCK, :]
            kk = jnp.concatenate([kvprev[:, :KV_W], kvcur[:, :KV_W]], axis=0)
            vv = jnp.concatenate([kvprev[:, KV_W:], kvcur[:, KV_W:]], axis=0)
            attn, _, _, _ = _attn_block_fwd(q_ref[r0:r0 + BLOCK, :], kk, vv, bias_ref, sinks_ref, first_mask)
            na, _ = _rms(attn, aog_ref[...])
            gm, _ = _gmlp_chunk_fwd(gu_ref[r0:r0 + BLOCK, :], gv_ref[r0:r0 + BLOCK, :], glng_ref[...], glnb_ref[...],
                                    wsm_ref, bsx_ref[...], amat_ref[...])
            ng, _ = _rms(gm, gog_ref[...])
            mix_scr[r0:r0 + BLOCK, :ATTN_W] = na.astype(BF16)
            mix_scr[r0:r0 + BLOCK, ATTN_W:] = ng.astype(BF16)
        mixed = mix_scr[...]
        mixed_ref[...] = mixed
        y = _dot(mixed, wout_ref[...])
        y_ref[...] = y
        z1 = ALPHA * x_ref[...] + mod_ref[2:3, :] * y
        xhat, _ = _ln_stats(z1)
        x1_ref[...] = xhat * ln1g_ref[...] + ln1b_ref[...]

        @pl.when(i == fwd_step)
        def _():
            gather.forward()

        @pl.when(i == n_steps - 1)
        def _():
            gather.finish()

    row = lambda w: pl.BlockSpec((tm, w), lambda i: (i, 0))
    prev = pl.BlockSpec((BLOCK, 2 * KV_W), lambda i: (jnp.maximum(i * nb - 1, 0), 0))
    outs = pl.pallas_call(
        body, name="fwd_mix",
        out_shape=[jax.ShapeDtypeStruct((s, D_MODEL), F32), jax.ShapeDtypeStruct((s, D_MODEL), F32),
                   jax.ShapeDtypeStruct((s, D_MODEL), BF16)]
        + [jax.ShapeDtypeStruct(_gathered_shape(sh, k), BF16) for sh, k in zip(ffn_shards, ffn_kinds)],
        grid=(n_steps,),
        in_specs=[row(ATTN_W), row(2 * KV_W), prev, row(GMLP_W), row(GMLP_W), row(D_MODEL), _const_spec((8, D_MODEL)),
                  _const_spec((N_HEADS, BLOCK, 2 * BLOCK)), pl.BlockSpec(memory_space=pltpu.SMEM),
                  _const_spec((1, GMLP_W)), _const_spec((1, GMLP_W)), _const_spec((N_GROUPS, BLOCK, BLOCK)),
                  _const_spec((BLOCK, GMLP_W)), _const_spec((GMLP_W, GMLP_W)), _const_spec((1, ATTN_W)),
                  _const_spec((1, GMLP_W)), _const_spec((D_MODEL, D_MODEL)), _const_spec((1, D_MODEL)),
                  _const_spec((1, D_MODEL))] + [ANY] * n_w,
        out_specs=[row(D_MODEL), row(D_MODEL), row(D_MODEL)] + [ANY] * n_w,
        scratch_shapes=[pltpu.VMEM((tm, D_MODEL), BF16)] + _WeightGather.sems(n_w),
        compiler_params=_params(("arbitrary",)),
    )(q, kv, kv, gu, gv, x, modr, bias, sinks, gln_g, gln_b, wsm, bsx, amat, aog, gog, w_out, ln1_g, ln1_b, *ffn_shards)
    return outs[0], outs[1], outs[2], outs[3:]


FF_BLOCKS = N_CHIPS // 2
FF_CHUNK = D_FF // FF_BLOCKS


def _sigmoid(x):
    return 1.0 / (1.0 + jnp.exp(-x))


def _fwd_ffn(x1, target, modr, ln2_g, ln2_b, w_gu, w_dn, tm):
    s = x1.shape[0]

    def body(x1_ref, t_ref, mod_ref, g_ref, b_ref, wgu_ref, wdn_ref, h2_ref, act_ref, dy2_ref, dx1a_ref, acc_ref):
        i = pl.program_id(0)

        @pl.when(i == 0)
        def _():
            acc_ref[...] = jnp.zeros_like(acc_ref)

        x1v = x1_ref[...]
        h2 = (x1v * (1.0 + mod_ref[4:5, :]) + mod_ref[3:4, :]).astype(BF16)
        h2_ref[...] = h2
        y2 = jnp.zeros((tm, D_MODEL), F32)
        for cc in range(D_FF // FF_CHUNK):
            c0 = cc * FF_CHUNK
            gate = _dot(h2, wgu_ref[cc])
            up = _dot(h2, wgu_ref[FF_BLOCKS + cc])
            act_ref[:, c0:c0 + FF_CHUNK] = gate.astype(BF16)
            act_ref[:, D_FF + c0:D_FF + c0 + FF_CHUNK] = up.astype(BF16)
            a = (gate * _sigmoid(gate) * up).astype(BF16)
            y2 = y2 + _dot(a, wdn_ref[c0:c0 + FF_CHUNK, :])
        g2 = mod_ref[5:6, :]
        z2 = ALPHA * x1v + g2 * y2
        xhat, rstd = _ln_stats(z2)
        gain = g_ref[...]
        diff = xhat * gain + b_ref[...] - t_ref[...]
        dx2 = diff * (1.0 / D_MODEL)
        dz2 = _ln_bwd(dx2 * gain, xhat, rstd)
        dx1a_ref[...] = ALPHA * dz2
        dy2_ref[...] = (g2 * dz2).astype(BF16)
        acc_ref[0:1, :] += _colsum(diff * diff)
        acc_ref[1:2, :] += _colsum(dx2 * xhat)
        acc_ref[2:3, :] += _colsum(dx2)
        acc_ref[3:4, :] += _colsum(dz2 * y2)

    row = lambda w: pl.BlockSpec((tm, w), lambda i: (i, 0))
    return pl.pallas_call(
        body, name="fwd_ffn",
        out_shape=(jax.ShapeDtypeStruct((s, D_MODEL), BF16), jax.ShapeDtypeStruct((s, 2 * D_FF), BF16),
                   jax.ShapeDtypeStruct((s, D_MODEL), BF16), jax.ShapeDtypeStruct((s, D_MODEL), F32),
                   jax.ShapeDtypeStruct((8, D_MODEL), F32)),
        grid=(s // tm,),
        in_specs=[row(D_MODEL), row(D_MODEL), _const_spec((8, D_MODEL)), _const_spec((1, D_MODEL)),
                  _const_spec((1, D_MODEL)), _const_spec((N_CHIPS, D_MODEL, FF_CHUNK), single=True),
                  _const_spec((D_FF, D_MODEL), single=True)],
        out_specs=(row(D_MODEL), row(2 * D_FF), row(D_MODEL), row(D_MODEL), _const_spec((8, D_MODEL))),
        compiler_params=_params(("arbitrary",)),
    )(x1, target, modr, ln2_g, ln2_b, w_gu, w_dn)


def _bwd_ffn(dy2, act, w_gu, w_dn, tm):
    s = dy2.shape[0]

    def body(dy2_ref, act_ref, wgu_ref, wdn_ref, a_ref, dgu_ref, dh2_ref):
        dy2v = dy2_ref[...]
        dh2 = jnp.zeros((tm, D_MODEL), F32)
        for cc in range(D_FF // FF_CHUNK):
            c0 = cc * FF_CHUNK
            da = _dot_nt(dy2v, wdn_ref[c0:c0 + FF_CHUNK, :])
            gate = act_ref[:, c0:c0 + FF_CHUNK].astype(F32)
            up = act_ref[:, D_FF + c0:D_FF + c0 + FF_CHUNK].astype(F32)
            sg = _sigmoid(gate)
            sl = gate * sg
            a_ref[:, c0:c0 + FF_CHUNK] = (sl * up).astype(BF16)
            dgate = (da * up * (sg * (1.0 + gate * (1.0 - sg)))).astype(BF16)
            dup = (da * sl).astype(BF16)
            dgu_ref[:, c0:c0 + FF_CHUNK] = dgate
            dgu_ref[:, D_FF + c0:D_FF + c0 + FF_CHUNK] = dup
            dh2 = dh2 + _dot_nt(dgate, wgu_ref[cc])
            dh2 = dh2 + _dot_nt(dup, wgu_ref[FF_BLOCKS + cc])
        dh2_ref[...] = dh2

    row = lambda w: pl.BlockSpec((tm, w), lambda i: (i, 0))
    return pl.pallas_call(
        body, name="bwd_ffn",
        out_shape=(jax.ShapeDtypeStruct((s, D_FF), BF16), jax.ShapeDtypeStruct((s, 2 * D_FF), BF16),
                   jax.ShapeDtypeStruct((s, D_MODEL), F32)),
        grid=(s // tm,),
        in_specs=[row(D_MODEL), row(2 * D_FF), _const_spec((N_CHIPS, D_MODEL, FF_CHUNK), single=True),
                  _const_spec((D_FF, D_MODEL), single=True)],
        out_specs=(row(D_FF), row(2 * D_FF), row(D_MODEL)),
        compiler_params=_params(("parallel",)),
    )(dy2, act, w_gu, w_dn)


def _bwd_mid(dh2, dx1a, x1, x, y, modr, ln1_g, w_out, tm):
    s = x.shape[0]

    def body(dh2_ref, dx1a_ref, x1_ref, x_ref, y_ref, mod_ref, g_ref, wout_ref, dxa_ref, dy_ref, dmix_ref, acc_ref):
        i = pl.program_id(0)

        @pl.when(i == 0)
        def _():
            acc_ref[...] = jnp.zeros_like(acc_ref)

        dh2 = dh2_ref[...]
        x1v = x1_ref[...]
        yv = y_ref[...]
        g1 = mod_ref[2:3, :]
        dx1 = dx1a_ref[...] + dh2 * (1.0 + mod_ref[4:5, :])
        z1 = ALPHA * x_ref[...] + g1 * yv
        xhat, rstd = _ln_stats(z1)
        dz1 = _ln_bwd(dx1 * g_ref[...], xhat, rstd)
        dxa_ref[...] = ALPHA * dz1
        dy = (g1 * dz1).astype(BF16)
        dy_ref[...] = dy
        dmix_ref[...] = _dot_nt(dy, wout_ref[...])
        acc_ref[0:1, :] += _colsum(dh2 * x1v)
        acc_ref[1:2, :] += _colsum(dh2)
        acc_ref[2:3, :] += _colsum(dx1 * xhat)
        acc_ref[3:4, :] += _colsum(dx1)
        acc_ref[4:5, :] += _colsum(dz1 * yv)

    row = lambda w: pl.BlockSpec((tm, w), lambda i: (i, 0))
    return pl.pallas_call(
        body, name="bwd_mid",
        out_shape=(jax.ShapeDtypeStruct((s, D_MODEL), F32), jax.ShapeDtypeStruct((s, D_MODEL), BF16),
                   jax.ShapeDtypeStruct((s, D_MODEL), F32), jax.ShapeDtypeStruct((8, D_MODEL), F32)),
        grid=(s // tm,),
        in_specs=[row(D_MODEL)] * 5 + [_const_spec((8, D_MODEL)), _const_spec((1, D_MODEL)),
                                       _const_spec((D_MODEL, D_MODEL))],
        out_specs=(row(D_MODEL), row(D_MODEL), row(D_MODEL), _const_spec((8, D_MODEL))),
        compiler_params=_params(("arbitrary",)),
    )(dh2, dx1a, x1, x, y, modr, ln1_g, w_out)


def _fold_kv(t0, t1):
    lane = lax.broadcasted_iota(jnp.int32, t0.shape, 1)
    f0 = t0 + pltpu.roll(t0, HEAD_DIM, 1)
    f1 = t1 + pltpu.roll(t1, HEAD_DIM, 1)
    return jnp.where(lane < HEAD_DIM, f0, f1)


def _bwd_mix(q, kv, gu, gv, dmix, bias, sinks, gln_g, gln_b, wsm, bsx, amat, aog, gog, grad_parts, grad_kinds):
    s = q.shape[0]
    nblk = s // BLOCK
    n_g = len(grad_parts)

    def body(q_ref, kv_ref, kvp_ref, gu_ref, gv_ref, dmix_ref, bias_ref, sinks_ref, glng_ref, glnb_ref, wsm_ref,
             bsx_ref, amat_ref, aog_ref, gog_ref, *rest):
        part_refs = rest[:n_g]
        dq_ref, dkv_ref, dgu_ref, dgv_ref, gbias_ref, dws_ref, dbs_ref, vec_ref, dsink_ref = rest[n_g:n_g + 9]
        rx_refs = rest[n_g + 9:2 * n_g + 9]
        carry, send_sems, recv_sems = rest[2 * n_g + 9:]
        n = pl.program_id(0)
        exchange = _ChipExchange(part_refs, rx_refs, grad_kinds, send_sems, recv_sems)

        @pl.when(n == 0)
        def _():
            exchange.start()
            carry[...] = jnp.zeros_like(carry)
            gbias_ref[...] = jnp.zeros_like(gbias_ref)
            dws_ref[...] = jnp.zeros_like(dws_ref)
            dbs_ref[...] = jnp.zeros_like(dbs_ref)
            vec_ref[...] = jnp.zeros_like(vec_ref)
            dsink_ref[...] = jnp.zeros_like(dsink_ref)

        @pl.when(n == nblk)
        def _():
            dkv_ref[...] = carry[...].astype(BF16)
            exchange.wait()

        @pl.when(n < nblk)
        def _():
            col = lax.broadcasted_iota(jnp.int32, (BLOCK, 2 * BLOCK), 1)
            lane = lax.broadcasted_iota(jnp.int32, (BLOCK, LANES), 1)
            low = lane < HEAD_DIM
            first_mask = (col < BLOCK) & (n == 0)
            kvprev = kvp_ref[...]
            kvcur = kv_ref[...]
            kk = jnp.concatenate([kvprev[:, :KV_W], kvcur[:, :KV_W]], axis=0)
            vv = jnp.concatenate([kvprev[:, KV_W:], kvcur[:, KV_W:]], axis=0)
            q_blk = q_ref[...]
            attn, probs, kvar, vvar = _attn_block_fwd(q_blk, kk, vv, bias_ref, sinks_ref, first_mask)
            aog_v = aog_ref[...]
            na_unit, r_a = _rms(attn, 1.0)
            gm, (u, tu, ta, xhat, rstd, vb, mixedv) = _gmlp_chunk_fwd(
                gu_ref[...], gv_ref[...], glng_ref[...], glnb_ref[...], wsm_ref, bsx_ref[...], amat_ref[...])
            gog_v = gog_ref[...]
            ng_unit, r_g = _rms(gm, 1.0)

            dmix = dmix_ref[...]
            dn_a = dmix[:, :ATTN_W]
            dn_g = dmix[:, ATTN_W:]
            vec_ref[0:1, :] += _colsum(dn_a * na_unit)
            vec_ref[1:2, :] += _colsum(dn_g * ng_unit)
            t_a = dn_a * aog_v
            d_attn = r_a * t_a - na_unit * (r_a * jnp.mean(t_a * na_unit, axis=-1, keepdims=True))
            t_g = dn_g * gog_v
            d_gm = r_g * t_g - ng_unit * (r_g * jnp.mean(t_g * ng_unit, axis=-1, keepdims=True))

            gu_v = gu_ref[...]
            dgu_ref[...] = (d_gm * mixedv * _gelu_grad(gu_v, tu)).astype(BF16)
            dmx = d_gm * u
            dbs_ref[...] += dmx
            dmxb = dmx.astype(BF16)
            dvn_cols = []
            for pair in range(N_GROUPS // 2):
                dp_ = dmxb[:, pair * LANES:(pair + 1) * LANES]
                vp = vb[:, pair * LANES:(pair + 1) * LANES]
                dvn_cols.append(jnp.where(low, _dot_tn(wsm_ref[2 * pair], dp_), _dot_tn(wsm_ref[2 * pair + 1], dp_)))
                zero = jnp.zeros_like(dp_)
                dws_ref[2 * pair] += _dot_nt(jnp.where(low, dp_, zero), vp)
                dws_ref[2 * pair + 1] += _dot_nt(jnp.where(low, zero, dp_), vp)
            dvn = jnp.concatenate(dvn_cols, axis=1)
            vec_ref[2:3, :] += _colsum(dvn * xhat)
            vec_ref[3:4, :] += _colsum(dvn)
            dxh = dvn * glng_ref[...]
            am = amat_ref[...]
            da = rstd * (dxh - _split_dot(dxh, am) - xhat * _split_dot(dxh * xhat, am))
            dgv_ref[...] = (da * _gelu_grad(gv_ref[...], ta)).astype(BF16)

            tk = [jnp.zeros((2 * BLOCK, LANES), F32) for _ in range(N_KV)]
            tv = [jnp.zeros((2 * BLOCK, LANES), F32) for _ in range(N_KV)]
            dq_cols = []
            for pair in range(N_HEADS // 2):
                d_pair = d_attn[:, pair * LANES:(pair + 1) * LANES]
                q_pair = q_blk[:, pair * LANES:(pair + 1) * LANES]
                dq_pair = None
                for par in range(2):
                    h = 2 * pair + par
                    kvh = h // (N_HEADS // N_KV)
                    p, ps = probs[h]
                    sel = low if par == 0 else jnp.logical_not(low)
                    do_h = jnp.where(sel, d_pair, 0.0).astype(BF16)
                    q_h = jnp.where(sel, q_pair, jnp.zeros_like(q_pair))
                    dp = _dot_nt(do_h, vvar[kvh][par])
                    delta = jnp.sum(p * dp, axis=-1, keepdims=True)
                    ds = p * (dp - delta)
                    dsink_ref[h] += -(ps * delta)
                    gbias_ref[h] += ds
                    dsb = ds.astype(BF16)
                    dqh = _dot(dsb, kvar[kvh][par])
                    dq_pair = dqh if dq_pair is None else dq_pair + dqh
                    tk[kvh] = tk[kvh] + _dot_tn(dsb, q_h)
                    tv[kvh] = tv[kvh] + _dot_tn(p.astype(BF16), do_h)
                dq_cols.append(dq_pair)
            dq_ref[...] = (jnp.concatenate(dq_cols, axis=1) * Q_SCALE).astype(BF16)
            dkk = _fold_kv(tk[0], tk[1])
            dvv = _fold_kv(tv[0], tv[1])
            dkv_ref[...] = (carry[...] + jnp.concatenate([dkk[:BLOCK], dvv[:BLOCK]], axis=1)).astype(BF16)
            carry[...] = jnp.concatenate([dkk[BLOCK:], dvv[BLOCK:]], axis=1)

    last = nblk - 1
    cur = lambda w: pl.BlockSpec((BLOCK, w), lambda n: (jnp.minimum(n, last), 0))
    prev = lambda w: pl.BlockSpec((BLOCK, w), lambda n: (jnp.clip(n - 1, 0, last), 0))
    outs = pl.pallas_call(
        body, name="bwd_mix",
        out_shape=[jax.ShapeDtypeStruct((s, ATTN_W), BF16), jax.ShapeDtypeStruct((s, 2 * KV_W), BF16),
                   jax.ShapeDtypeStruct((s, GMLP_W), BF16), jax.ShapeDtypeStruct((s, GMLP_W), BF16),
                   jax.ShapeDtypeStruct((N_HEADS, BLOCK, 2 * BLOCK), F32),
                   jax.ShapeDtypeStruct((N_GROUPS, BLOCK, BLOCK), F32),
                   jax.ShapeDtypeStruct((BLOCK, GMLP_W), F32), jax.ShapeDtypeStruct((8, GMLP_W), F32),
                   jax.ShapeDtypeStruct((N_HEADS, BLOCK, 1), F32)]
        + [jax.ShapeDtypeStruct(_rx_shape(p.shape, k), BF16) for p, k in zip(grad_parts, grad_kinds)],
        grid=(nblk + 1,),
        in_specs=[cur(ATTN_W), cur(2 * KV_W), prev(2 * KV_W), cur(GMLP_W), cur(GMLP_W), cur(D_MODEL),
                  _const_spec((N_HEADS, BLOCK, 2 * BLOCK)), pl.BlockSpec(memory_space=pltpu.SMEM),
                  _const_spec((1, GMLP_W)), _const_spec((1, GMLP_W)), _const_spec((N_GROUPS, BLOCK, BLOCK)),
                  _const_spec((BLOCK, GMLP_W)), _const_spec((GMLP_W, GMLP_W)), _const_spec((1, ATTN_W)),
                  _const_spec((1, GMLP_W))] + [ANY] * n_g,
        out_specs=[cur(ATTN_W), prev(2 * KV_W), cur(GMLP_W), cur(GMLP_W),
                   _const_spec((N_HEADS, BLOCK, 2 * BLOCK)), _const_spec((N_GROUPS, BLOCK, BLOCK)),
                   _const_spec((BLOCK, GMLP_W)), _const_spec((8, GMLP_W)), _const_spec((N_HEADS, BLOCK, 1))]
        + [ANY] * n_g,
        scratch_shapes=[pltpu.VMEM((BLOCK, 2 * KV_W), F32)] + _ChipExchange.sems(n_g),
        compiler_params=_params(("arbitrary",)),
    )(q, kv, kv, gu, gv, dmix, bias, sinks, gln_g, gln_b, wsm, bsx, amat, aog, gog, *grad_parts)
    return outs[:9], outs[9:]


def _mix_finalize(gbias, bucket, dws, dbs, dsink):
    def body(gb_ref, bucket_ref, dws_ref, dbs_ref, dsink_ref, drb_ref, dwsm_ref, dbsg_ref, dsk_ref):
        bk = bucket_ref[...]
        lane = lax.broadcasted_iota(jnp.int32, (N_BUCKETS, LANES), 1)
        rowi = lax.broadcasted_iota(jnp.int32, (N_BUCKETS, LANES), 0)
        drb = jnp.zeros((N_BUCKETS, LANES), F32)
        dsk = jnp.zeros((8, LANES), F32)
        lane8 = lax.broadcasted_iota(jnp.int32, (8, LANES), 1)
        for h in range(N_HEADS):
            g = gb_ref[h]
            for b in range(N_BUCKETS):
                tot = jnp.sum(_colsum(jnp.where(bk == float(b), g, 0.0)), axis=1, keepdims=True)
                drb = jnp.where((lane == h) & (rowi == b), tot, drb)
            sk = jnp.sum(dsink_ref[h], axis=0, keepdims=True)
            dsk = jnp.where(lane8 == h, sk, dsk)
        drb_ref[...] = drb
        dsk_ref[...] = dsk
        ti = lax.broadcasted_iota(jnp.int32, (BLOCK, BLOCK), 0)
        ui = lax.broadcasted_iota(jnp.int32, (BLOCK, BLOCK), 1)
        for g in range(N_GROUPS):
            dwsm_ref[g] = jnp.where(ti >= ui, dws_ref[g], 0.0)
        gi = lax.broadcasted_iota(jnp.int32, (GMLP_W, LANES), 0) // GROUP_DIM
        li = lax.broadcasted_iota(jnp.int32, (GMLP_W, LANES), 1)
        ind = jnp.where(gi == li, 1.0, 0.0).astype(BF16)
        d = dbs_ref[...]
        hi = d.astype(BF16)
        r1 = d - hi.astype(F32)
        mid = r1.astype(BF16)
        lo = (r1 - mid.astype(F32)).astype(BF16)
        dbsg_ref[...] = _dot(hi, ind) + _dot(mid, ind) + _dot(lo, ind)

    return pl.pallas_call(
        body, name="mix_finalize",
        out_shape=(jax.ShapeDtypeStruct((N_BUCKETS, LANES), F32), jax.ShapeDtypeStruct((N_GROUPS, BLOCK, BLOCK), F32),
                   jax.ShapeDtypeStruct((BLOCK, LANES), F32), jax.ShapeDtypeStruct((8, LANES), F32)),
        grid=(1,),
        in_specs=[_const_spec((N_HEADS, BLOCK, 2 * BLOCK)), _const_spec((BLOCK, 2 * BLOCK)),
                  _const_spec((N_GROUPS, BLOCK, BLOCK)), _const_spec((BLOCK, GMLP_W)),
                  _const_spec((N_HEADS, BLOCK, 1))],
        out_specs=(_const_spec((N_BUCKETS, LANES)), _const_spec((N_GROUPS, BLOCK, BLOCK)),
                   _const_spec((BLOCK, LANES)), _const_spec((8, LANES))),
        compiler_params=_params(("arbitrary",)),
    )(gbias, bucket, dws, dbs, dsink)


def _bwd_in(dq, dkv, dgu, dgv, dxa, x, modr, w_in, tm, grad_parts, grad_kinds):
    s = x.shape[0]
    n_steps = s // tm
    n_g = len(grad_parts)

    def body(dq_ref, dkv_ref, dgu_ref, dgv_ref, dxa_ref, x_ref, mod_ref, w_ref, *rest):
        part_refs = rest[:n_g]
        gx_ref, acc_ref, db_ref = rest[n_g:n_g + 3]
        rx_refs = rest[n_g + 3:2 * n_g + 3]
        send_sems, recv_sems = rest[2 * n_g + 3:]
        i = pl.program_id(0)
        exchange = _ChipExchange(part_refs, rx_refs, grad_kinds, send_sems, recv_sems)

        @pl.when(i == 0)
        def _():
            exchange.start()
            acc_ref[...] = jnp.zeros_like(acc_ref)
            db_ref[...] = jnp.zeros_like(db_ref)

        dproj = jnp.concatenate([dq_ref[...], dkv_ref[...], dgu_ref[...], dgv_ref[...]], axis=1)
        dh1 = _dot_nt(dproj, w_ref[...])
        gx_ref[...] = dxa_ref[...] + dh1 * (1.0 + mod_ref[1:2, :])
        acc_ref[0:1, :] += _colsum(dh1 * x_ref[...])
        acc_ref[1:2, :] += _colsum(dh1)
        db_ref[0:1, :] += _colsum(dproj.astype(F32))

        @pl.when(i == n_steps - 1)
        def _():
            exchange.wait()

    row = lambda w: pl.BlockSpec((tm, w), lambda i: (i, 0))
    outs = pl.pallas_call(
        body, name="bwd_in",
        out_shape=[jax.ShapeDtypeStruct((s, D_MODEL), F32), jax.ShapeDtypeStruct((8, D_MODEL), F32),
                   jax.ShapeDtypeStruct((8, IN_W), F32)]
        + [jax.ShapeDtypeStruct(_rx_shape(p.shape, k), BF16) for p, k in zip(grad_parts, grad_kinds)],
        grid=(n_steps,),
        in_specs=[row(ATTN_W), row(2 * KV_W), row(GMLP_W), row(GMLP_W), row(D_MODEL), row(D_MODEL),
                  _const_spec((8, D_MODEL)), _const_spec((D_MODEL, IN_W))] + [ANY] * n_g,
        out_specs=[row(D_MODEL), _const_spec((8, D_MODEL)), _const_spec((8, IN_W))] + [ANY] * n_g,
        scratch_shapes=_ChipExchange.sems(n_g),
        compiler_params=_params(("arbitrary",)),
    )(dq, dkv, dgu, dgv, dxa, x, modr, w_in, *grad_parts)
    return outs[:3], outs[3:]


def _wgrad(a, bs, tm, tk, name):
    k_all, m = a.shape
    n = sum(b.shape[1] for b in bs)
    nk = k_all // tk
    n_b = len(bs)

    def body(a_ref, *rest):
        b_refs, (o_ref, ob_ref) = rest[:n_b], rest[n_b:]
        k = pl.program_id(1)

        @pl.when(k == 0)
        def _():
            o_ref[...] = jnp.zeros_like(o_ref)

        b = b_refs[0][...] if n_b == 1 else jnp.concatenate([r[...] for r in b_refs], axis=1)
        o_ref[...] += _dot_tn(a_ref[...], b)

        @pl.when(k == nk - 1)
        def _():
            ob_ref[...] = o_ref[...].astype(BF16)

    out_spec = pl.BlockSpec((tm, n), lambda i, k: (i, 0))
    return pl.pallas_call(
        body, name=name, out_shape=(jax.ShapeDtypeStruct((m, n), F32), jax.ShapeDtypeStruct((m, n), BF16)),
        grid=(m // tm, nk),
        in_specs=[pl.BlockSpec((tk, tm), lambda i, k: (k, i))]
        + [pl.BlockSpec((tk, b.shape[1]), lambda i, k: (k, 0)) for b in bs],
        out_specs=(out_spec, out_spec),
        compiler_params=_params(("parallel", "arbitrary")),
    )(a, *bs)


def _adam_math(w, g, m, v):
    m2 = ADAM_B1 * m + (1.0 - ADAM_B1) * g
    v2 = ADAM_B2 * v + (1.0 - ADAM_B2) * (g * g)
    m_hat = m2 / (1.0 - ADAM_B1 ** ADAM_STEP)
    v_hat = v2 / (1.0 - ADAM_B2 ** ADAM_STEP)
    delta = -ADAM_LR * (m_hat / (jnp.sqrt(v_hat) + ADAM_EPS) + ADAM_WD * w)
    return delta, m2, v2


def _adam_halves(w, mine, got, m, v, tr, name):
    r, cc = w.shape
    h = r // 2
    nt = h // tr

    def body(c_ref, w_ref, mine_ref, got_ref, m_ref, v_ref, g_ref, d_ref, m2_ref, v2_ref):
        g = jnp.where(pl.program_id(0) == c_ref[0], mine_ref[...], got_ref[...])
        g_ref[...] = g
        d, m2, v2 = _adam_math(w_ref[...], g, m_ref[...], v_ref[...])
        d_ref[...] = d
        m2_ref[...] = m2
        v2_ref[...] = v2

    full = pl.BlockSpec((tr, cc), lambda hh, i, c_ref: (hh * nt + i, 0))
    half = pl.BlockSpec((tr, cc), lambda hh, i, c_ref: (i, 0))
    shp = jax.ShapeDtypeStruct((r, cc), F32)
    return pl.pallas_call(
        body, name=name, out_shape=(shp, shp, shp, shp),
        grid_spec=pltpu.PrefetchScalarGridSpec(
            num_scalar_prefetch=1, grid=(2, nt), in_specs=[full, half, half, full, full],
            out_specs=(full, full, full, full)),
        compiler_params=_params(("arbitrary", "arbitrary")),
    )(_core_index_scalar(), w, mine, got, m, v)


def _adam_w_ada(sc_t, dmod_cols, w, m, v, tr):
    r, cc = w.shape

    def body(sct_ref, dm_ref, w_ref, m_ref, v_ref, g_ref, d_ref, m2_ref, v2_ref):
        g = sct_ref[:, 0:1] * dm_ref[0:1, :]
        for k in range(1, N_DEV):
            g = g + sct_ref[:, k:k + 1] * dm_ref[k:k + 1, :]
        g_ref[...] = g
        d, m2, v2 = _adam_math(w_ref[...], g, m_ref[...], v_ref[...])
        d_ref[...] = d
        m2_ref[...] = m2
        v2_ref[...] = v2

    spec = pl.BlockSpec((tr, cc), lambda i: (i, 0))
    shp = jax.ShapeDtypeStruct((r, cc), F32)
    return pl.pallas_call(
        body, name="adam_w_ada", out_shape=(shp, shp, shp, shp), grid=(r // tr,),
        in_specs=[pl.BlockSpec((tr, N_DEV), lambda i: (i, 0)), _const_spec((N_DEV, cc)), spec, spec, spec],
        out_specs=(spec, spec, spec, spec), compiler_params=_params(("parallel",)),
    )(sc_t, dmod_cols, w, m, v)


def _adam_small(gathered, w, m, v, loss_rows):
    _, r, _ = gathered.shape
    lo, hi = loss_rows

    def body(ga_ref, w_ref, m_ref, v_ref, g_ref, d_ref, m2_ref, v2_ref, loss_ref):
        g = ga_ref[0]
        for k in range(1, N_DEV):
            g = g + ga_ref[k]
        g_ref[...] = g
        d, m2, v2 = _adam_math(w_ref[...], g, m_ref[...], v_ref[...])
        d_ref[...] = d
        m2_ref[...] = m2
        v2_ref[...] = v2
        tot = jnp.sum(_colsum(g[lo:hi, :]), axis=1, keepdims=True)
        loss_ref[...] = jnp.broadcast_to(tot * (0.5 / D_MODEL), loss_ref.shape)

    shp = jax.ShapeDtypeStruct((r, LANES), F32)
    spec = _const_spec((r, LANES))
    return pl.pallas_call(
        body, name="adam_small", out_shape=(shp, shp, shp, shp, jax.ShapeDtypeStruct((8, LANES), F32)), grid=(1,),
        in_specs=[_const_spec((N_DEV, r, LANES)), spec, spec, spec],
        out_specs=(spec, spec, spec, spec, _const_spec((8, LANES))),
        compiler_params=_params(("arbitrary",)),
    )(gathered, w, m, v)


SMALL_NAMES = ("loss", "rel_bias", "b_ada", "b_in", "attn_sinks", "gmlp_ln_g", "gmlp_ln_b", "gmlp_w_s", "gmlp_b_s",
               "attn_out_g", "gmlp_out_g", "ln1_g", "ln1_b", "ln2_g", "ln2_b")
PACK_TILE = 8 * LANES


def _pack(items):
    parts, layout, off = [], {}, 0
    for name in SMALL_NAMES:
        a = items[name]
        flat = a.reshape(-1).astype(F32)
        n = flat.shape[0]
        padded = -(-n // PACK_TILE) * PACK_TILE
        parts.append(jnp.pad(flat, (0, padded - n)))
        layout[name] = (off // LANES, padded // LANES, a.shape, n)
        off += padded
    return jnp.concatenate(parts).reshape(-1, LANES), layout


def _unpack(buf, layout, name):
    r0, nr, shape, n = layout[name]
    return buf[r0:r0 + nr].reshape(-1)[:n].reshape(shape)


def kernel(x, c, rel_bias, w_ada, b_ada, w_in, b_in, attn_sinks, gmlp_ln_g, gmlp_ln_b, gmlp_w_s, gmlp_b_s, attn_out_g, gmlp_out_g, w_out, ln1_g, ln1_b, w_gate_up, w_down, ln2_g, ln2_b, loss_target, m_rel_bias, m_w_ada, m_b_ada, m_w_in, m_b_in, m_attn_sinks, m_gmlp_ln_g, m_gmlp_ln_b, m_gmlp_w_s, m_gmlp_b_s, m_attn_out_g, m_gmlp_out_g, m_w_out, m_ln1_g, m_ln1_b, m_w_gate_up, m_w_down, m_ln2_g, m_ln2_b, v_rel_bias, v_w_ada, v_b_ada, v_w_in, v_b_in, v_attn_sinks, v_gmlp_ln_g, v_gmlp_ln_b, v_gmlp_w_s, v_gmlp_b_s, v_attn_out_g, v_gmlp_out_g, v_w_out, v_ln1_g, v_ln1_b, v_w_gate_up, v_w_down, v_ln2_g, v_ln2_b):
    ix, iy, ic = _my_pos()
    chip = 2 * ix + iy
    dev = 4 * ix + 2 * iy + ic
    s = x.shape[1]
    xs = x[0]
    tgt = loss_target[0]
    tm_big = min(512, s)
    tm_ffn = min(256, s)
    n_ada, n_in, n_gu = w_ada.shape[2], w_in.shape[2], w_gate_up.shape[2]

    c_all = _allgather8(jnp.pad(c, ((0, 7), (0, 0))), "gather_c").reshape(N_DEV, 8, D_MODEL)[:, 0, :]
    sc_all, mod_cols = _mod_part(c_all, w_ada[0], lax.dynamic_slice_in_dim(b_ada, chip * n_ada, n_ada, axis=1))
    mod_all = _allgather8(mod_cols, "gather_mod").reshape(N_DEV, N_DEV, -1)
    mod_row = lax.dynamic_index_in_dim(mod_all[0::2], dev, axis=1, keepdims=False)
    modr = jnp.pad(mod_row.reshape(6, D_MODEL), ((0, 2), (0, 0)))

    w_in_s, w_out_s = w_in[0].astype(BF16), w_out[0].astype(BF16)
    w_gu_s, w_dn_s = w_gate_up[0].astype(BF16), w_down[0].astype(BF16)
    (w_in_g,) = _gather_weights([w_in_s], ["blk"], "gather_w_in")
    w_in_g = _insert_own(w_in_g, w_in_s, "blk", chip)
    w_in_f = jnp.transpose(w_in_g, (1, 0, 2)).reshape(D_MODEL, IN_W)

    bucket = _bucket_table()
    bias, wsm = _prep_tables(bucket, rel_bias, gmlp_w_s[0])
    bsx = jnp.repeat(gmlp_b_s[0].T, GROUP_DIM, axis=1)
    amat = _group_mean_matrix()
    sinks = attn_sinks[0]

    (h1, q, kv, gu, gv), (w_out_g,) = _fwd_in(xs, modr, w_in_f, b_in, tm_big, [w_out_s], ["blk"])
    w_out_f = _insert_own(w_out_g, w_out_s, "blk", chip).reshape(D_MODEL, D_MODEL)
    x1, y, mixed, (w_gu_g, w_dn_g) = _fwd_mix(
        q, kv, gu, gv, xs, modr, bias, sinks, gmlp_ln_g, gmlp_ln_b, wsm, bsx, amat, attn_out_g, gmlp_out_g, w_out_f,
        ln1_g, ln1_b, tm_big, [w_gu_s, w_dn_s], ["blk", "blk"])
    assert n_gu == FF_CHUNK
    w_gu_f = _insert_own(w_gu_g, w_gu_s, "blk", chip)
    w_dn_f = _insert_own(w_dn_g, w_dn_s, "blk", chip).reshape(D_FF, D_MODEL)
    h2, act, dy2, dx1a, acc_f = _fwd_ffn(x1, tgt, modr, ln2_g, ln2_b, w_gu_f, w_dn_f, tm_ffn)

    a_act, dgu_ff, dh2 = _bwd_ffn(dy2, act, w_gu_f, w_dn_f, tm_ffn)
    g_dn, g_dn_b = _wgrad(a_act, [dy2], D_FF // 2, min(512, s), "wgrad_down")
    g_gu, g_gu_b = _wgrad(h2, [dgu_ff], 512, min(256, s), "wgrad_gate_up")
    dxa, dy, dmix, acc_m = _bwd_mid(dh2, dx1a, x1, xs, y, modr, ln1_g, w_out_f, tm_big)
    g_out, g_out_b = _wgrad(mixed, [dy], 512, min(512, s), "wgrad_out")
    blk3 = lambda a, rows: a.reshape(N_CHIPS, rows, a.shape[1])
    kinds_a = ["blk", "cols", "blk"]
    fulls_a = [blk3(g_dn, D_FF // N_CHIPS), g_gu, blk3(g_out, D_MODEL // N_CHIPS)]
    fulls_a_b = [blk3(g_dn_b, D_FF // N_CHIPS), g_gu_b, blk3(g_out_b, D_MODEL // N_CHIPS)]
    gots_a = _swap_halves(fulls_a_b, kinds_a, "rs_swap_a")
    parts_a = [_add_halves(f, g, k, "rs_add_a%d" % i) for i, (f, g, k) in enumerate(zip(fulls_a, gots_a, kinds_a))]
    (dq, dkv, dgu, dgv, gbias, dws, dbs, vec, dsink), rxs_a = _bwd_mix(
        q, kv, gu, gv, dmix, bias, sinks, gmlp_ln_g, gmlp_ln_b, wsm, bsx, amat, attn_out_g, gmlp_out_g,
        [p[1] for p in parts_a], kinds_a)
    drb, dwsm, dbsg, dsk = _mix_finalize(gbias, bucket, dws, dbs, dsink)
    g_in, g_in_b = _wgrad(h1, [dq, dkv, dgu, dgv], 512, min(512, s), "wgrad_in")
    to_blk = lambda a: jnp.transpose(a.reshape(D_MODEL, N_CHIPS, n_in), (1, 0, 2))
    full_in, full_in_b = to_blk(g_in), to_blk(g_in_b)
    (got_in,) = _swap_halves([full_in_b], ["blk"], "rs_swap_b")
    part_in = _add_halves(full_in, got_in, "blk", "rs_add_b")
    (grad_x, acc_i, db_in), (rx_in,) = _bwd_in(dq, dkv, dgu, dgv, dxa, xs, modr, w_in_f, tm_big, [part_in[1]], ["blk"])

    dmod = jnp.concatenate([acc_i[1], acc_i[0], acc_m[4], acc_m[1], acc_m[0], acc_f[3]])
    small_g = {
        "loss": acc_f[0], "rel_bias": drb[:, :N_HEADS], "b_ada": dmod.reshape(1, -1), "b_in": db_in[0:1],
        "attn_sinks": dsk[0:1, :N_HEADS], "gmlp_ln_g": vec[2:3], "gmlp_ln_b": vec[3:4], "gmlp_w_s": dwsm[None],
        "gmlp_b_s": dbsg[:, :N_GROUPS].T[None], "attn_out_g": vec[0:1], "gmlp_out_g": vec[1:2],
        "ln1_g": acc_m[2:3], "ln1_b": acc_m[3:4], "ln2_g": acc_f[1:2], "ln2_b": acc_f[2:3]}
    zero_loss = jnp.zeros((D_MODEL,), F32)
    small_w = {"loss": zero_loss, "rel_bias": rel_bias, "b_ada": b_ada, "b_in": b_in, "attn_sinks": attn_sinks,
               "gmlp_ln_g": gmlp_ln_g, "gmlp_ln_b": gmlp_ln_b, "gmlp_w_s": gmlp_w_s, "gmlp_b_s": gmlp_b_s,
               "attn_out_g": attn_out_g, "gmlp_out_g": gmlp_out_g, "ln1_g": ln1_g, "ln1_b": ln1_b, "ln2_g": ln2_g,
               "ln2_b": ln2_b}
    small_m = {"loss": zero_loss, "rel_bias": m_rel_bias, "b_ada": m_b_ada, "b_in": m_b_in, "attn_sinks": m_attn_sinks,
               "gmlp_ln_g": m_gmlp_ln_g, "gmlp_ln_b": m_gmlp_ln_b, "gmlp_w_s": m_gmlp_w_s, "gmlp_b_s": m_gmlp_b_s,
               "attn_out_g": m_attn_out_g, "gmlp_out_g": m_gmlp_out_g, "ln1_g": m_ln1_g, "ln1_b": m_ln1_b,
               "ln2_g": m_ln2_g, "ln2_b": m_ln2_b}
    small_v = {"loss": zero_loss + 1.0, "rel_bias": v_rel_bias, "b_ada": v_b_ada, "b_in": v_b_in,
               "attn_sinks": v_attn_sinks, "gmlp_ln_g": v_gmlp_ln_g, "gmlp_ln_b": v_gmlp_ln_b, "gmlp_w_s": v_gmlp_w_s,
               "gmlp_b_s": v_gmlp_b_s, "attn_out_g": v_attn_out_g, "gmlp_out_g": v_gmlp_out_g, "ln1_g": v_ln1_g,
               "ln1_b": v_ln1_b, "ln2_g": v_ln2_g, "ln2_b": v_ln2_b}
    pg, layout = _pack(small_g)
    pw, _ = _pack(small_w)
    pm, _ = _pack(small_m)
    pv, _ = _pack(small_v)
    rows = pg.shape[0]
    gathered = _allgather8(pg, "gather_small").reshape(N_DEV, rows, LANES)
    l0, ln_, _, _ = layout["loss"]
    sg, sd, sm, sv, loss_t = _adam_small(gathered, pw, pm, pv, (l0, l0 + ln_))
    loss = loss_t[0, 0]

    b0, bn, _, _ = layout["b_ada"]
    dmod_all = gathered[:, b0:b0 + bn, :].reshape(N_DEV, -1)[:, :6 * D_MODEL]
    dmod_cols = lax.dynamic_slice_in_dim(dmod_all, chip * n_ada, n_ada, axis=1)
    g_ada, d_ada, m_ada, v_ada = _adam_w_ada(sc_all.T, dmod_cols, w_ada[0], m_w_ada[0], v_w_ada[0], 256)

    sums = [(parts_a[0][0], rxs_a[0], "blk", 176), (parts_a[1][0], rxs_a[1], "cols", 256),
            (parts_a[2][0], rxs_a[2], "blk", 128), (part_in[0], rx_in, "blk", 256)]
    mine = [_sum_chips(p, rx, k, tr, "rs_sum_%d" % i) for i, (p, rx, k, tr) in enumerate(sums)]
    got = _share_halves(mine, "rs_share")

    gs_dn, d_dn, m_dn, v_dn = _adam_halves(w_down[0], mine[0], got[0], m_w_down[0], v_w_down[0], 176, "adam_w_down")
    gs_gu, d_gu, m_gu, v_gu = _adam_halves(w_gate_up[0], mine[1], got[1], m_w_gate_up[0], v_w_gate_up[0], 256,
                                           "adam_w_gate_up")
    gs_out, d_out, m_out, v_out = _adam_halves(w_out[0], mine[2], got[2], m_w_out[0], v_w_out[0], 128, "adam_w_out")
    gs_in, d_in, m_in, v_in = _adam_halves(w_in[0], mine[3], got[3], m_w_in[0], v_w_in[0], 256, "adam_w_in")

    big = {"w_ada": (g_ada, d_ada, m_ada, v_ada), "w_in": (gs_in, d_in, m_in, v_in), "w_out": (gs_out, d_out, m_out, v_out),
           "w_gate_up": (gs_gu, d_gu, m_gu, v_gu), "w_down": (gs_dn, d_dn, m_dn, v_dn)}
    order = ["rel_bias", "w_ada", "b_ada", "w_in", "b_in", "attn_sinks", "gmlp_ln_g", "gmlp_ln_b", "gmlp_w_s", "gmlp_b_s",
             "attn_out_g", "gmlp_out_g", "w_out", "ln1_g", "ln1_b", "w_gate_up", "w_down", "ln2_g", "ln2_b"]
    outs = [loss, grad_x[None]]
    for k, packed in enumerate((sg, sd, sm, sv)):
        for name in order:
            if name in big:
                outs.append(big[name][k][None])
            else:
                outs.append(_unpack(packed, layout, name))
    return tuple(outs)
```

```python
import math

import numpy as np
import jax
import jax.numpy as jnp
from jax import lax
from jax.experimental import pallas as pl
from jax.experimental.pallas import tpu as pltpu

F32 = jnp.float32
BF16 = jnp.bfloat16
MESH = pl.DeviceIdType.MESH

D_MODEL = 1024
N_HEADS = 8
N_KV = 2
HEAD_DIM = 64
ATTN_W = N_HEADS * HEAD_DIM
KV_W = N_KV * HEAD_DIM
N_GROUPS = 8
GROUP_DIM = 64
GMLP_W = N_GROUPS * GROUP_DIM
IN_W = ATTN_W + 2 * KV_W + 2 * GMLP_W
BLOCK = 128
N_BUCKETS = 32
MAX_DISTANCE = 128
D_FF = 2816
ALPHA = 2.0 ** 0.25
LN_EPS = 1e-5
NEG_INF = -1e30
ADAM_LR, ADAM_B1, ADAM_B2, ADAM_EPS, ADAM_WD, ADAM_STEP = 0.001, 0.9, 0.999, 1e-8, 0.01, 10
N_CHIPS = 4
N_DEV = 8
LANES = 128
V7X_VMEM_LIMIT = 56 * 2 ** 20
GELU_C = math.sqrt(2.0 / math.pi)
Q_SCALE = HEAD_DIM ** -0.5
ANY = pl.BlockSpec(memory_space=pl.ANY)

TALL_BS = N_GROUPS * BLOCK
TALL_RB = TALL_BS + 8
TALL_SK = TALL_RB + N_BUCKETS
TALL_ROWS = TALL_SK + 8
WIDE_W = 6 * D_MODEL
WIDE_LAYOUT = {
    "b_ada": (0, 0, 6 * D_MODEL),
    "b_in": (1, 0, IN_W), "ln1_g": (1, IN_W, D_MODEL), "ln1_b": (1, IN_W + D_MODEL, D_MODEL),
    "ln2_g": (1, IN_W + 2 * D_MODEL, D_MODEL), "ln2_b": (1, IN_W + 3 * D_MODEL, D_MODEL),
    "gmlp_ln_g": (2, 0, GMLP_W), "gmlp_ln_b": (2, GMLP_W, GMLP_W), "attn_out_g": (2, 2 * GMLP_W, ATTN_W),
    "gmlp_out_g": (2, 2 * GMLP_W + ATTN_W, GMLP_W), "loss": (2, 3 * GMLP_W + ATTN_W, D_MODEL)}
WIDE_PARAMS = tuple(n for n in WIDE_LAYOUT if n != "loss")


def _params(sem=None):
    return pltpu.CompilerParams(dimension_semantics=sem, vmem_limit_bytes=V7X_VMEM_LIMIT)


def _const_spec(shape, single=False):
    nd = len(shape)
    if single:
        return pl.BlockSpec(shape, lambda *_: (0,) * nd, pipeline_mode=pl.Buffered(1))
    return pl.BlockSpec(shape, lambda *_: (0,) * nd)


def _dot(a, b):
    return jnp.dot(a, b, preferred_element_type=F32)


def _dot_nt(a, b):
    return lax.dot_general(a, b, (((1,), (1,)), ((), ())), preferred_element_type=F32)


def _dot_tn(a, b):
    return lax.dot_general(a, b, (((0,), (0,)), ((), ())), preferred_element_type=F32)


def _gelu(x):
    t = jnp.tanh(GELU_C * (x + 0.044715 * x * x * x))
    return 0.5 * x * (1.0 + t), t


def _gelu_grad(x, t):
    return 0.5 * (1.0 + t) + 0.5 * x * (1.0 - t * t) * GELU_C * (1.0 + 3.0 * 0.044715 * x * x)


def _split_dot(x, a):
    hi = x.astype(BF16)
    lo = (x - hi.astype(F32)).astype(BF16)
    return _dot(hi, a) + _dot(lo, a)


def _group_mean_matrix():
    g = np.arange(GMLP_W) // GROUP_DIM
    return jnp.asarray((g[:, None] == g[None, :]).astype(np.float32) / GROUP_DIM, dtype=BF16)


def _ln_stats(z):
    mu = jnp.mean(z, axis=-1, keepdims=True)
    d = z - mu
    var = jnp.mean(d * d, axis=-1, keepdims=True)
    rstd = lax.rsqrt(var + LN_EPS)
    return d * rstd, rstd


def _ln_bwd(dxhat, xhat, rstd):
    m1 = jnp.mean(dxhat, axis=-1, keepdims=True)
    m2 = jnp.mean(dxhat * xhat, axis=-1, keepdims=True)
    return rstd * (dxhat - m1 - xhat * m2)


def _colsum(x):
    return jnp.sum(x, axis=0, keepdims=True)


def _my_pos():
    return lax.axis_index("x"), lax.axis_index("y"), lax.axis_index("c")


def _other_chips(x, y):
    return [(1 - x, y), (x, 1 - y), (1 - x, 1 - y)]


def _chip_index_scalar():
    ix, iy, _ = _my_pos()
    return jnp.reshape(2 * ix + iy, (1,)).astype(jnp.int32)


def _core_index_scalar():
    return jnp.reshape(lax.axis_index("c"), (1,)).astype(jnp.int32)


def _allgather8(vs, name, grad_parts=(), grad_kinds=()):
    n_v, n_g = len(vs), len(grad_parts)

    def body(*refs):
        x_refs, part_refs = refs[:n_v], refs[n_v:n_v + n_g]
        out_refs, rx_refs = refs[n_v + n_g:2 * n_v + n_g], refs[2 * n_v + n_g:2 * n_v + 2 * n_g]
        send_sems, recv_sems, local_sems, ex_send, ex_recv = refs[2 * n_v + 2 * n_g:]
        x, y, c = _my_pos()
        me, sibling = (x, y, c), (x, y, 1 - c)
        chips = _other_chips(x, y)
        exchange = _ChipExchange(part_refs, rx_refs, grad_kinds, ex_send, ex_recv)
        exchange.start()

        def rows(a, px, py, pc):
            m_per = x_refs[a].shape[0]
            return out_refs[a].at[pl.ds((4 * px + 2 * py + pc) * m_per, m_per), :]

        def copy(a, k, block, to, src=None):
            return pltpu.make_async_remote_copy(
                src_ref=rows(a, *block) if src is None else src, dst_ref=rows(a, *block),
                send_sem=send_sems.at[7 * a + k], recv_sem=recv_sems.at[7 * a + k], device_id=to, device_id_type=MESH)

        mine = [pltpu.make_async_copy(x_refs[a], rows(a, *me), local_sems.at[a]) for a in range(n_v)]
        for cp in mine:
            cp.start()
        first, passed = [], []
        for a in range(n_v):
            first.append(copy(a, 0, me, sibling, src=x_refs[a]))
            first += [copy(a, 1 + j, me, (*chip, c), src=x_refs[a]) for j, chip in enumerate(chips)]
        for cp in first:
            cp.start()
        for a in range(n_v):
            for j, chip in enumerate(chips):
                copy(a, 1 + j, (*chip, c), me).wait_recv()
                cp = copy(a, 4 + j, (*chip, c), sibling)
                cp.start()
                passed.append(cp)
        for a in range(n_v):
            copy(a, 0, sibling, me).wait_recv()
            for j, chip in enumerate(chips):
                copy(a, 4 + j, (*chip, 1 - c), me).wait_recv()
        for cp in first + passed:
            cp.wait_send()
        for cp in mine:
            cp.wait()
        exchange.wait()

    vmem = pl.BlockSpec(memory_space=pltpu.VMEM)
    outs = pl.pallas_call(
        body, name=name,
        out_shape=[jax.ShapeDtypeStruct((N_DEV * v.shape[0], v.shape[1]), v.dtype) for v in vs]
        + [jax.ShapeDtypeStruct(_rx_shape(p.shape, k), BF16) for p, k in zip(grad_parts, grad_kinds)],
        in_specs=[vmem] * n_v + [ANY] * n_g, out_specs=[vmem] * n_v + [ANY] * n_g,
        scratch_shapes=[pltpu.SemaphoreType.DMA((7 * n_v,)), pltpu.SemaphoreType.DMA((7 * n_v,)),
                        pltpu.SemaphoreType.DMA((n_v,))] + _ChipExchange.sems(max(n_g, 1)),
        compiler_params=pltpu.CompilerParams(vmem_limit_bytes=V7X_VMEM_LIMIT),
    )(*vs, *grad_parts)
    return outs[:n_v], outs[n_v:]


def _gathered_shape(shard, kind):
    r, cc = shard.shape
    return (N_CHIPS, r, cc) if kind == "blk" else (r, N_CHIPS * cc)


class _WeightGather:
    def __init__(self, shards, gathered, kinds, send_sems, recv_sems):
        self.shards, self.gathered, self.kinds = shards, gathered, kinds
        self.send_sems, self.recv_sems = send_sems, recv_sems
        self.x, self.y, self.c = _my_pos()
        self.chips = _other_chips(self.x, self.y)

    def _dst(self, a, chip, pc):
        r, cc = self.shards[a].shape
        h = r // 2
        g = self.gathered[a]
        if self.kinds[a] == "blk":
            return g.at[chip, pl.ds(pc * h, h), :]
        return g.at[pl.ds(pc * h, h), pl.ds(chip * cc, cc)]

    def _copy(self, a, k, chip, pc, to, src=None):
        d = self._dst(a, chip, pc)
        return pltpu.make_async_remote_copy(
            src_ref=d if src is None else src, dst_ref=d, send_sem=self.send_sems.at[a * 6 + k],
            recv_sem=self.recv_sems.at[a * 6 + k], device_id=to, device_id_type=MESH)

    def _each(self):
        for a in range(len(self.shards)):
            for j, chip in enumerate(self.chips):
                yield a, j, chip, 2 * chip[0] + chip[1]

    def start(self):
        my_chip = 2 * self.x + self.y
        for a, j, chip, _ in self._each():
            h = self.shards[a].shape[0] // 2
            self._copy(a, j, my_chip, self.c, (*chip, self.c), src=self.shards[a].at[pl.ds(self.c * h, h), :]).start()

    def forward(self):
        me, sibling = (self.x, self.y, self.c), (self.x, self.y, 1 - self.c)
        for a, j, chip, cj in self._each():
            self._copy(a, j, cj, self.c, me).wait_recv()
            self._copy(a, 3 + j, cj, self.c, sibling).start()

    def finish(self):
        me = (self.x, self.y, self.c)
        for a, j, chip, cj in self._each():
            self._copy(a, 3 + j, cj, 1 - self.c, me).wait_recv()
        for a, j, chip, cj in self._each():
            self._copy(a, j, cj, self.c, me).wait_send()
            self._copy(a, 3 + j, cj, self.c, me).wait_send()

    @staticmethod
    def sems(n_arr):
        return [pltpu.SemaphoreType.DMA((n_arr * 6,)), pltpu.SemaphoreType.DMA((n_arr * 6,))]


def _insert_own(gathered, shard, kind, chip):
    if kind == "blk":
        return lax.dynamic_update_slice(gathered, shard[None], (chip, 0, 0))
    return lax.dynamic_update_slice(gathered, shard, (0, chip * shard.shape[1]))


def _gather_weights(shards, kinds, name):
    n_arr = len(shards)

    def body(*refs):
        g = _WeightGather(refs[:n_arr], refs[n_arr:2 * n_arr], kinds, *refs[2 * n_arr:])
        g.start()
        g.forward()
        g.finish()

    return pl.pallas_call(
        body, name=name,
        out_shape=[jax.ShapeDtypeStruct(_gathered_shape(s, k), BF16) for s, k in zip(shards, kinds)],
        in_specs=[ANY] * n_arr, out_specs=[ANY] * n_arr, scratch_shapes=_WeightGather.sems(n_arr),
    )(*shards)


def _half_of_full(ref, kind, pc):
    if kind == "blk":
        h = ref.shape[1] // 2
        return ref.at[:, pl.ds(pc * h, h), :]
    h = ref.shape[0] // 2
    return ref.at[pl.ds(pc * h, h), :]


def _half_shape(shape, kind):
    return (shape[0], shape[1] // 2, shape[2]) if kind == "blk" else (shape[0] // 2, shape[1])


def _swap_halves(fulls_bf16, kinds, name):
    n_arr = len(fulls_bf16)

    def body(*refs):
        ins, outs = refs[:n_arr], refs[n_arr:2 * n_arr]
        send_sems, recv_sems = refs[2 * n_arr:]
        x, y, c = _my_pos()
        cps = []
        for a in range(n_arr):
            cp = pltpu.make_async_remote_copy(
                src_ref=_half_of_full(ins[a], kinds[a], 1 - c), dst_ref=outs[a], send_sem=send_sems.at[a],
                recv_sem=recv_sems.at[a], device_id=(x, y, 1 - c), device_id_type=MESH)
            cp.start()
            cps.append(cp)
        for cp in cps:
            cp.wait()

    return pl.pallas_call(
        body, name=name,
        out_shape=[jax.ShapeDtypeStruct(_half_shape(a.shape, k), a.dtype) for a, k in zip(fulls_bf16, kinds)],
        in_specs=[ANY] * n_arr, out_specs=[ANY] * n_arr,
        scratch_shapes=[pltpu.SemaphoreType.DMA((n_arr,)), pltpu.SemaphoreType.DMA((n_arr,))],
    )(*fulls_bf16)


def _add_halves(full, got, kind, name):
    hs = _half_shape(full.shape, kind)

    def body(c_ref, a_ref, b_ref, o_ref, ob_ref):
        p = a_ref[...] + b_ref[...].astype(F32)
        o_ref[...] = p
        ob_ref[...] = p.astype(BF16)

    if kind == "blk":
        nb, h, cc = hs
        own = pl.BlockSpec((1, h, cc), lambda b, c_ref: (b, c_ref[0], 0))
        other = pl.BlockSpec((1, h, cc), lambda b, c_ref: (b, 0, 0))
    else:
        h, cc = hs[0], hs[1] // N_CHIPS
        own = pl.BlockSpec((h, cc), lambda b, c_ref: (c_ref[0], b))
        other = pl.BlockSpec((h, cc), lambda b, c_ref: (0, b))
    return pl.pallas_call(
        body, name=name, out_shape=(jax.ShapeDtypeStruct(hs, F32), jax.ShapeDtypeStruct(hs, BF16)),
        grid_spec=pltpu.PrefetchScalarGridSpec(
            num_scalar_prefetch=1, grid=(N_CHIPS,), in_specs=[own, other], out_specs=(other, other)),
        compiler_params=_params(("arbitrary",)),
    )(_core_index_scalar(), full, got)


def _rx_shape(part_shape, kind):
    if kind == "blk":
        return (3, part_shape[1], part_shape[2])
    return (3, part_shape[0], part_shape[1] // N_CHIPS)


class _ChipExchange:
    def __init__(self, parts, rxs, kinds, send_sems, recv_sems):
        self.parts, self.rxs, self.kinds = parts, rxs, kinds
        self.send_sems, self.recv_sems = send_sems, recv_sems
        self.x, self.y, self.c = _my_pos()
        self.chips = _other_chips(self.x, self.y)

    def _copies(self):
        for a in range(len(self.parts)):
            for j, chip in enumerate(self.chips):
                cj = 2 * chip[0] + chip[1]
                if self.kinds[a] == "blk":
                    src = self.parts[a].at[cj]
                else:
                    cc = self.parts[a].shape[1] // N_CHIPS
                    src = self.parts[a].at[:, pl.ds(cj * cc, cc)]
                yield pltpu.make_async_remote_copy(
                    src_ref=src, dst_ref=self.rxs[a].at[j], send_sem=self.send_sems.at[a * 3 + j],
                    recv_sem=self.recv_sems.at[a * 3 + j], device_id=(*chip, self.c), device_id_type=MESH)

    def start(self):
        for cp in self._copies():
            cp.start()

    def wait(self):
        for cp in self._copies():
            cp.wait_recv()
        for cp in self._copies():
            cp.wait_send()

    @staticmethod
    def sems(n_arr):
        return [pltpu.SemaphoreType.DMA((n_arr * 3,)), pltpu.SemaphoreType.DMA((n_arr * 3,))]


def _sum_chips(part, rx, kind, tr, name):
    _, h, cc = rx.shape
    flips = (2, 1, 3)

    def body(chip_ref, p_ref, rx_ref, o_ref):
        own = p_ref[...].reshape(tr, cc)
        for mc in range(N_CHIPS):
            @pl.when(chip_ref[0] == mc)
            def _():
                terms = sorted([(mc, None)] + [(mc ^ f, j) for j, f in enumerate(flips)])
                acc = None
                for _, j in terms:
                    t = own if j is None else rx_ref[j].astype(F32)
                    acc = t if acc is None else acc + t
                o_ref[...] = acc

    if kind == "blk":
        own_spec = pl.BlockSpec((1, tr, cc), lambda i, chip_ref: (chip_ref[0], i, 0))
    else:
        own_spec = pl.BlockSpec((tr, cc), lambda i, chip_ref: (i, chip_ref[0]))
    return pl.pallas_call(
        body, name=name, out_shape=jax.ShapeDtypeStruct((h, cc), F32),
        grid_spec=pltpu.PrefetchScalarGridSpec(
            num_scalar_prefetch=1, grid=(h // tr,),
            in_specs=[own_spec, pl.BlockSpec((3, tr, cc), lambda i, chip_ref: (0, i, 0))],
            out_specs=pl.BlockSpec((tr, cc), lambda i, chip_ref: (i, 0))),
        compiler_params=_params(("arbitrary",)),
    )(_chip_index_scalar(), part, rx)


def _share_halves(halves, name):
    n_arr = len(halves)

    def body(*refs):
        ins, outs = refs[:n_arr], refs[n_arr:2 * n_arr]
        send_sems, recv_sems = refs[2 * n_arr:]
        x, y, c = _my_pos()
        cps = []
        for a in range(n_arr):
            cp = pltpu.make_async_remote_copy(
                src_ref=ins[a], dst_ref=outs[a], send_sem=send_sems.at[a], recv_sem=recv_sems.at[a],
                device_id=(x, y, 1 - c), device_id_type=MESH)
            cp.start()
            cps.append(cp)
        for cp in cps:
            cp.wait()

    return pl.pallas_call(
        body, name=name, out_shape=[jax.ShapeDtypeStruct(h.shape, h.dtype) for h in halves],
        in_specs=[ANY] * n_arr, out_specs=[ANY] * n_arr,
        scratch_shapes=[pltpu.SemaphoreType.DMA((n_arr,)), pltpu.SemaphoreType.DMA((n_arr,))],
    )(*halves)


def _mod_part(c_all, w_ada_s, b_ada_s):
    n = w_ada_s.shape[1]

    def body(c_ref, w_ref, b_ref, sc_ref, mod_ref):
        cv = c_ref[...]
        sc = cv * (1.0 / (1.0 + jnp.exp(-cv)))
        sc_ref[...] = sc
        a_hi = sc.astype(BF16)
        a_lo = (sc - a_hi.astype(F32)).astype(BF16)
        w = w_ref[...]
        w_hi = w.astype(BF16)
        w_lo = (w - w_hi.astype(F32)).astype(BF16)
        mod_ref[...] = _dot(a_hi, w_hi) + _dot(a_hi, w_lo) + _dot(a_lo, w_hi) + b_ref[...]

    return pl.pallas_call(
        body, name="mod_part",
        out_shape=(jax.ShapeDtypeStruct((N_DEV, D_MODEL), F32), jax.ShapeDtypeStruct((N_DEV, n), F32)),
        grid=(1,),
        in_specs=[_const_spec((N_DEV, D_MODEL)), _const_spec((D_MODEL, n)), _const_spec((1, n))],
        out_specs=(_const_spec((N_DEV, D_MODEL)), _const_spec((N_DEV, n))),
        compiler_params=_params(("arbitrary",)),
    )(c_all, w_ada_s, b_ada_s)


def _bucket_table():
    qi = jnp.arange(BLOCK)[:, None]
    si = jnp.arange(2 * BLOCK)[None, :]
    dist = qi + BLOCK - si
    max_exact = N_BUCKETS // 2
    n = jnp.maximum(dist, 0)
    nf = jnp.maximum(n, max_exact).astype(F32)
    large = max_exact + (jnp.log(nf / max_exact) / math.log(MAX_DISTANCE / max_exact)
                         * (N_BUCKETS - max_exact)).astype(jnp.int32)
    large = jnp.minimum(large, N_BUCKETS - 1)
    return jnp.where(n < max_exact, n, large).astype(F32)


def _prep_tables(bucket, rel_bias, w_s):
    def body(bucket_ref, rb_ref, ws_ref, bias_ref, wsm_ref):
        qi = lax.broadcasted_iota(jnp.int32, (BLOCK, 2 * BLOCK), 0)
        si = lax.broadcasted_iota(jnp.int32, (BLOCK, 2 * BLOCK), 1)
        dist = qi + BLOCK - si
        in_window = (dist >= 0) & (dist < BLOCK)
        bk = bucket_ref[...]
        for h in range(N_HEADS):
            acc = jnp.zeros((BLOCK, 2 * BLOCK), F32)
            for b in range(N_BUCKETS):
                acc = jnp.where(bk == float(b), rb_ref[b, h], acc)
            bias_ref[h] = jnp.where(in_window, acc, NEG_INF)
        ti = lax.broadcasted_iota(jnp.int32, (BLOCK, BLOCK), 0)
        ui = lax.broadcasted_iota(jnp.int32, (BLOCK, BLOCK), 1)
        for g in range(N_GROUPS):
            wsm_ref[g] = jnp.where(ti >= ui, ws_ref[g], 0.0).astype(BF16)

    return pl.pallas_call(
        body, name="prep_tables",
        out_shape=(jax.ShapeDtypeStruct((N_HEADS, BLOCK, 2 * BLOCK), F32),
                   jax.ShapeDtypeStruct((N_GROUPS, BLOCK, BLOCK), BF16)),
        grid=(1,),
        in_specs=[_const_spec((BLOCK, 2 * BLOCK)), pl.BlockSpec(memory_space=pltpu.SMEM),
                  _const_spec((N_GROUPS, BLOCK, BLOCK))],
        out_specs=(_const_spec((N_HEADS, BLOCK, 2 * BLOCK)), _const_spec((N_GROUPS, BLOCK, BLOCK))),
        compiler_params=_params(("arbitrary",)),
    )(bucket, rel_bias, w_s)


def _fwd_in(x, modr, w_in, b_in, tm, shards, kinds):
    s = x.shape[0]
    n_steps = s // tm
    fwd_step = (3 * n_steps) // 4
    n_w = len(shards)

    def body(x_ref, mod_ref, w_ref, b_ref, *rest):
        shard_refs = rest[:n_w]
        h1_ref, q_ref, kv_ref, gu_ref, gv_ref = rest[n_w:n_w + 5]
        gathered_refs = rest[n_w + 5:2 * n_w + 5]
        send_sems, recv_sems = rest[2 * n_w + 5:]
        i = pl.program_id(0)
        gather = _WeightGather(shard_refs, gathered_refs, kinds, send_sems, recv_sems)

        @pl.when(i == 0)
        def _():
            gather.start()

        h1 = (x_ref[...] * (1.0 + mod_ref[1:2, :]) + mod_ref[0:1, :]).astype(BF16)
        h1_ref[...] = h1
        proj = _dot(h1, w_ref[...]) + b_ref[...]
        q_ref[...] = (proj[:, :ATTN_W] * Q_SCALE).astype(BF16)
        kv_ref[...] = proj[:, ATTN_W:ATTN_W + 2 * KV_W].astype(BF16)
        gu_ref[...] = proj[:, ATTN_W + 2 * KV_W:ATTN_W + 2 * KV_W + GMLP_W]
        gv_ref[...] = proj[:, ATTN_W + 2 * KV_W + GMLP_W:]

        @pl.when(i == fwd_step)
        def _():
            gather.forward()

        @pl.when(i == n_steps - 1)
        def _():
            gather.finish()

    row = lambda w: pl.BlockSpec((tm, w), lambda i: (i, 0))
    outs = pl.pallas_call(
        body, name="fwd_in",
        out_shape=[jax.ShapeDtypeStruct((s, D_MODEL), BF16), jax.ShapeDtypeStruct((s, ATTN_W), BF16),
                   jax.ShapeDtypeStruct((s, 2 * KV_W), BF16), jax.ShapeDtypeStruct((s, GMLP_W), F32),
                   jax.ShapeDtypeStruct((s, GMLP_W), F32)]
        + [jax.ShapeDtypeStruct(_gathered_shape(sh, k), BF16) for sh, k in zip(shards, kinds)],
        grid=(n_steps,),
        in_specs=[row(D_MODEL), _const_spec((8, D_MODEL)), _const_spec((D_MODEL, IN_W)), _const_spec((1, IN_W))]
        + [ANY] * n_w,
        out_specs=[row(D_MODEL), row(ATTN_W), row(2 * KV_W), row(GMLP_W), row(GMLP_W)] + [ANY] * n_w,
        scratch_shapes=_WeightGather.sems(n_w),
        compiler_params=_params(("arbitrary",)),
    )(x, modr, w_in, b_in, *shards)
    return outs[:5], outs[5:]


def _kv_variants(kk):
    kf = kk.astype(F32)
    lane = lax.broadcasted_iota(jnp.int32, kf.shape, 1)
    low = lane < HEAD_DIM
    k0_lo = jnp.where(low, kf, 0.0)
    k1_hi = jnp.where(low, 0.0, kf)
    k0_hi = pltpu.roll(k0_lo, HEAD_DIM, 1)
    k1_lo = pltpu.roll(k1_hi, HEAD_DIM, 1)
    return ((k0_lo.astype(BF16), k0_hi.astype(BF16)), (k1_lo.astype(BF16), k1_hi.astype(BF16)))


def _attn_head_probs(q_pair, k_var, bias_h, sink, first_mask):
    logits = _dot_nt(q_pair, k_var) + bias_h
    if first_mask is not None:
        logits = jnp.where(first_mask, NEG_INF, logits)
    m = jnp.maximum(jnp.max(logits, axis=-1, keepdims=True), sink)
    e = jnp.exp(logits - m)
    es = jnp.exp(sink - m)
    inv = 1.0 / (jnp.sum(e, axis=-1, keepdims=True) + es)
    return e * inv, es * inv


def _attn_block_fwd(q_blk, kk, vv, bias_ref, sinks_ref, first_mask):
    kvar = _kv_variants(kk)
    vvar = _kv_variants(vv)
    outs, probs = [], []
    for pair in range(N_HEADS // 2):
        acc = None
        for par in range(2):
            h = 2 * pair + par
            kv = h // (N_HEADS // N_KV)
            p, ps = _attn_head_probs(q_blk[:, pair * LANES:(pair + 1) * LANES], kvar[kv][par], bias_ref[h],
                                     sinks_ref[h], first_mask)
            probs.append((p, ps))
            o = _dot(p.astype(BF16), vvar[kv][par])
            acc = o if acc is None else acc + o
        outs.append(acc)
    return jnp.concatenate(outs, axis=1), probs, kvar, vvar


def _gmlp_chunk_fwd(gu, gv, ln_g, ln_b, wsm_ref, bsx, amat):
    u, tu = _gelu(gu)
    a, ta = _gelu(gv)
    mean = _split_dot(a, amat)
    d = a - mean
    var = _split_dot(d * d, amat)
    rstd = lax.rsqrt(var + LN_EPS)
    xhat = d * rstd
    vb = (xhat * ln_g + ln_b).astype(BF16)
    lane = lax.broadcasted_iota(jnp.int32, (BLOCK, LANES), 1)
    low = lane < GROUP_DIM
    cols = []
    for pair in range(N_GROUPS // 2):
        vp = vb[:, pair * LANES:(pair + 1) * LANES]
        cols.append(jnp.where(low, _dot(wsm_ref[2 * pair], vp), _dot(wsm_ref[2 * pair + 1], vp)))
    mixedv = jnp.concatenate(cols, axis=1) + bsx
    return u * mixedv, (u, tu, ta, xhat, rstd, vb, mixedv)


def _rms(a, g):
    r = lax.rsqrt(jnp.mean(a * a, axis=-1, keepdims=True) + LN_EPS)
    return a * r * g, r


def _fwd_mix(q, kv, gu, gv, x, modr, bias, sinks, gln_g, gln_b, wsm, bsx, amat, aog, gog, w_out, ln1_g, ln1_b, tm,
             ffn_shards, ffn_kinds):
    s = x.shape[0]
    nb = tm // BLOCK
    n_steps = s // tm
    fwd_step = (3 * n_steps) // 4
    n_w = len(ffn_shards)

    def body(q_ref, kv_ref, kvp_ref, gu_ref, gv_ref, x_ref, mod_ref, bias_ref, sinks_ref, glng_ref, glnb_ref, wsm_ref,
             bsx_ref, amat_ref, aog_ref, gog_ref, wout_ref, ln1g_ref, ln1b_ref, *rest):
        shard_refs = rest[:n_w]
        x1_ref, y_ref, mixed_ref = rest[n_w:n_w + 3]
        gathered_refs = rest[n_w + 3:2 * n_w + 3]
        mix_scr, send_sems, recv_sems = rest[2 * n_w + 3:]
        i = pl.program_id(0)
        gather = _WeightGather(shard_refs, gathered_refs, ffn_kinds, send_sems, recv_sems)

        @pl.when(i == 0)
        def _():
            gather.start()

        col = lax.broadcasted_iota(jnp.int32, (BLOCK, 2 * BLOCK), 1)
        for b in range(nb):
            r0 = b * BLOCK
            if b == 0:
                kvprev = kvp_ref[...]
                first_mask = (col < BLOCK) & (i == 0)
            else:
                kvprev = kv_ref[r0 - BLOCK:r0, :]
                first_mask = None
            kvcur = kv_ref[r0:r0 + BLOCK, :]
            kk = jnp.concatenate([kvprev[:, :KV_W], kvcur[:, :KV_W]], axis=0)
            vv = jnp.concatenate([kvprev[:, KV_W:], kvcur[:, KV_W:]], axis=0)
            attn, _, _, _ = _attn_block_fwd(q_ref[r0:r0 + BLOCK, :], kk, vv, bias_ref, sinks_ref, first_mask)
            na, _ = _rms(attn, aog_ref[...])
            gm, _ = _gmlp_chunk_fwd(gu_ref[r0:r0 + BLOCK, :], gv_ref[r0:r0 + BLOCK, :], glng_ref[...], glnb_ref[...],
                                    wsm_ref, bsx_ref[...], amat_ref[...])
            ng, _ = _rms(gm, gog_ref[...])
            mix_scr[r0:r0 + BLOCK, :ATTN_W] = na.astype(BF16)
            mix_scr[r0:r0 + BLOCK, ATTN_W:] = ng.astype(BF16)
        mixed = mix_scr[...]
        mixed_ref[...] = mixed
        y = _dot(mixed, wout_ref[...])
        y_ref[...] = y
        z1 = ALPHA * x_ref[...] + mod_ref[2:3, :] * y
        xhat, _ = _ln_stats(z1)
        x1_ref[...] = xhat * ln1g_ref[...] + ln1b_ref[...]

        @pl.when(i == fwd_step)
        def _():
            gather.forward()

        @pl.when(i == n_steps - 1)
        def _():
            gather.finish()

    row = lambda w: pl.BlockSpec((tm, w), lambda i: (i, 0))
    prev = pl.BlockSpec((BLOCK, 2 * KV_W), lambda i: (jnp.maximum(i * nb - 1, 0), 0))
    outs = pl.pallas_call(
        body, name="fwd_mix",
        out_shape=[jax.ShapeDtypeStruct((s, D_MODEL), F32), jax.ShapeDtypeStruct((s, D_MODEL), F32),
                   jax.ShapeDtypeStruct((s, D_MODEL), BF16)]
        + [jax.ShapeDtypeStruct(_gathered_shape(sh, k), BF16) for sh, k in zip(ffn_shards, ffn_kinds)],
        grid=(n_steps,),
        in_specs=[row(ATTN_W), row(2 * KV_W), prev, row(GMLP_W), row(GMLP_W), row(D_MODEL), _const_spec((8, D_MODEL)),
                  _const_spec((N_HEADS, BLOCK, 2 * BLOCK)), pl.BlockSpec(memory_space=pltpu.SMEM),
                  _const_spec((1, GMLP_W)), _const_spec((1, GMLP_W)), _const_spec((N_GROUPS, BLOCK, BLOCK)),
                  _const_spec((BLOCK, GMLP_W)), _const_spec((GMLP_W, GMLP_W)), _const_spec((1, ATTN_W)),
                  _const_spec((1, GMLP_W)), _const_spec((D_MODEL, D_MODEL)), _const_spec((1, D_MODEL)),
                  _const_spec((1, D_MODEL))] + [ANY] * n_w,
        out_specs=[row(D_MODEL), row(D_MODEL), row(D_MODEL)] + [ANY] * n_w,
        scratch_shapes=[pltpu.VMEM((tm, D_MODEL), BF16)] + _WeightGather.sems(n_w),
        compiler_params=_params(("arbitrary",)),
    )(q, kv, kv, gu, gv, x, modr, bias, sinks, gln_g, gln_b, wsm, bsx, amat, aog, gog, w_out, ln1_g, ln1_b, *ffn_shards)
    return outs[0], outs[1], outs[2], outs[3:]


FF_BLOCKS = N_CHIPS // 2
FF_CHUNK = D_FF // FF_BLOCKS


def _sigmoid(x):
    return 1.0 / (1.0 + jnp.exp(-x))


def _fwd_ffn(x1, target, modr, ln2_g, ln2_b, w_gu, w_dn, tm):
    s = x1.shape[0]

    def body(x1_ref, t_ref, mod_ref, g_ref, b_ref, wgu_ref, wdn_ref, h2_ref, act_ref, dy2_ref, dx1a_ref, acc_ref):
        i = pl.program_id(0)

        @pl.when(i == 0)
        def _():
            acc_ref[...] = jnp.zeros_like(acc_ref)

        x1v = x1_ref[...]
        h2 = (x1v * (1.0 + mod_ref[4:5, :]) + mod_ref[3:4, :]).astype(BF16)
        h2_ref[...] = h2
        y2 = jnp.zeros((tm, D_MODEL), F32)
        for cc in range(D_FF // FF_CHUNK):
            c0 = cc * FF_CHUNK
            gate = _dot(h2, wgu_ref[cc])
            up = _dot(h2, wgu_ref[FF_BLOCKS + cc])
            act_ref[:, c0:c0 + FF_CHUNK] = gate.astype(BF16)
            act_ref[:, D_FF + c0:D_FF + c0 + FF_CHUNK] = up.astype(BF16)
            a = (gate * _sigmoid(gate) * up).astype(BF16)
            y2 = y2 + _dot(a, wdn_ref[c0:c0 + FF_CHUNK, :])
        g2 = mod_ref[5:6, :]
        z2 = ALPHA * x1v + g2 * y2
        xhat, rstd = _ln_stats(z2)
        gain = g_ref[...]
        diff = xhat * gain + b_ref[...] - t_ref[...]
        dx2 = diff * (1.0 / D_MODEL)
        dz2 = _ln_bwd(dx2 * gain, xhat, rstd)
        dx1a_ref[...] = ALPHA * dz2
        dy2_ref[...] = (g2 * dz2).astype(BF16)
        acc_ref[0:1, :] += _colsum(diff * diff)
        acc_ref[1:2, :] += _colsum(dx2 * xhat)
        acc_ref[2:3, :] += _colsum(dx2)
        acc_ref[3:4, :] += _colsum(dz2 * y2)

    row = lambda w: pl.BlockSpec((tm, w), lambda i: (i, 0))
    return pl.pallas_call(
        body, name="fwd_ffn",
        out_shape=(jax.ShapeDtypeStruct((s, D_MODEL), BF16), jax.ShapeDtypeStruct((s, 2 * D_FF), BF16),
                   jax.ShapeDtypeStruct((s, D_MODEL), BF16), jax.ShapeDtypeStruct((s, D_MODEL), F32),
                   jax.ShapeDtypeStruct((8, D_MODEL), F32)),
        grid=(s // tm,),
        in_specs=[row(D_MODEL), row(D_MODEL), _const_spec((8, D_MODEL)), _const_spec((1, D_MODEL)),
                  _const_spec((1, D_MODEL)), _const_spec((N_CHIPS, D_MODEL, FF_CHUNK), single=True),
                  _const_spec((D_FF, D_MODEL), single=True)],
        out_specs=(row(D_MODEL), row(2 * D_FF), row(D_MODEL), row(D_MODEL), _const_spec((8, D_MODEL))),
        compiler_params=_params(("arbitrary",)),
    )(x1, target, modr, ln2_g, ln2_b, w_gu, w_dn)


def _bwd_ffn(dy2, act, w_gu, w_dn, tm):
    s = dy2.shape[0]

    def body(dy2_ref, act_ref, wgu_ref, wdn_ref, a_ref, dgu_ref, dh2_ref):
        dy2v = dy2_ref[...]
        dh2 = jnp.zeros((tm, D_MODEL), F32)
        for cc in range(D_FF // FF_CHUNK):
            c0 = cc * FF_CHUNK
            da = _dot_nt(dy2v, wdn_ref[c0:c0 + FF_CHUNK, :])
            gate = act_ref[:, c0:c0 + FF_CHUNK].astype(F32)
            up = act_ref[:, D_FF + c0:D_FF + c0 + FF_CHUNK].astype(F32)
            sg = _sigmoid(gate)
            sl = gate * sg
            a_ref[:, c0:c0 + FF_CHUNK] = (sl * up).astype(BF16)
            dgate = (da * up * (sg * (1.0 + gate * (1.0 - sg)))).astype(BF16)
            dup = (da * sl).astype(BF16)
            dgu_ref[:, c0:c0 + FF_CHUNK] = dgate
            dgu_ref[:, D_FF + c0:D_FF + c0 + FF_CHUNK] = dup
            dh2 = dh2 + _dot_nt(dgate, wgu_ref[cc])
            dh2 = dh2 + _dot_nt(dup, wgu_ref[FF_BLOCKS + cc])
        dh2_ref[...] = dh2

    row = lambda w: pl.BlockSpec((tm, w), lambda i: (i, 0))
    return pl.pallas_call(
        body, name="bwd_ffn",
        out_shape=(jax.ShapeDtypeStruct((s, D_FF), BF16), jax.ShapeDtypeStruct((s, 2 * D_FF), BF16),
                   jax.ShapeDtypeStruct((s, D_MODEL), F32)),
        grid=(s // tm,),
        in_specs=[row(D_MODEL), row(2 * D_FF), _const_spec((N_CHIPS, D_MODEL, FF_CHUNK), single=True),
                  _const_spec((D_FF, D_MODEL), single=True)],
        out_specs=(row(D_FF), row(2 * D_FF), row(D_MODEL)),
        compiler_params=_params(("parallel",)),
    )(dy2, act, w_gu, w_dn)


def _bwd_mid(dh2, dx1a, x1, x, y, modr, ln1_g, w_out, tm):
    s = x.shape[0]

    def body(dh2_ref, dx1a_ref, x1_ref, x_ref, y_ref, mod_ref, g_ref, wout_ref, dxa_ref, dy_ref, dmix_ref, acc_ref):
        i = pl.program_id(0)

        @pl.when(i == 0)
        def _():
            acc_ref[...] = jnp.zeros_like(acc_ref)

        dh2 = dh2_ref[...]
        x1v = x1_ref[...]
        yv = y_ref[...]
        g1 = mod_ref[2:3, :]
        dx1 = dx1a_ref[...] + dh2 * (1.0 + mod_ref[4:5, :])
        z1 = ALPHA * x_ref[...] + g1 * yv
        xhat, rstd = _ln_stats(z1)
        dz1 = _ln_bwd(dx1 * g_ref[...], xhat, rstd)
        dxa_ref[...] = ALPHA * dz1
        dy = (g1 * dz1).astype(BF16)
        dy_ref[...] = dy
        dmix_ref[...] = _dot_nt(dy, wout_ref[...])
        acc_ref[0:1, :] += _colsum(dh2 * x1v)
        acc_ref[1:2, :] += _colsum(dh2)
        acc_ref[2:3, :] += _colsum(dx1 * xhat)
        acc_ref[3:4, :] += _colsum(dx1)
        acc_ref[4:5, :] += _colsum(dz1 * yv)

    row = lambda w: pl.BlockSpec((tm, w), lambda i: (i, 0))
    return pl.pallas_call(
        body, name="bwd_mid",
        out_shape=(jax.ShapeDtypeStruct((s, D_MODEL), F32), jax.ShapeDtypeStruct((s, D_MODEL), BF16),
                   jax.ShapeDtypeStruct((s, D_MODEL), F32), jax.ShapeDtypeStruct((8, D_MODEL), F32)),
        grid=(s // tm,),
        in_specs=[row(D_MODEL)] * 5 + [_const_spec((8, D_MODEL)), _const_spec((1, D_MODEL)),
                                       _const_spec((D_MODEL, D_MODEL))],
        out_specs=(row(D_MODEL), row(D_MODEL), row(D_MODEL), _const_spec((8, D_MODEL))),
        compiler_params=_params(("arbitrary",)),
    )(dh2, dx1a, x1, x, y, modr, ln1_g, w_out)


def _fold_kv(t0, t1):
    lane = lax.broadcasted_iota(jnp.int32, t0.shape, 1)
    f0 = t0 + pltpu.roll(t0, HEAD_DIM, 1)
    f1 = t1 + pltpu.roll(t1, HEAD_DIM, 1)
    return jnp.where(lane < HEAD_DIM, f0, f1)


def _bwd_mix(q, kv, gu, gv, dmix, bias, sinks, gln_g, gln_b, wsm, bsx, amat, aog, gog, grad_parts, grad_kinds):
    s = q.shape[0]
    nblk = s // BLOCK
    n_g = len(grad_parts)

    def body(q_ref, kv_ref, kvp_ref, gu_ref, gv_ref, dmix_ref, bias_ref, sinks_ref, glng_ref, glnb_ref, wsm_ref,
             bsx_ref, amat_ref, aog_ref, gog_ref, *rest):
        part_refs = rest[:n_g]
        dq_ref, dkv_ref, dgu_ref, dgv_ref, gbias_ref, dws_ref, dbs_ref, vec_ref, dsink_ref = rest[n_g:n_g + 9]
        rx_refs = rest[n_g + 9:2 * n_g + 9]
        carry, send_sems, recv_sems = rest[2 * n_g + 9:]
        n = pl.program_id(0)
        exchange = _ChipExchange(part_refs, rx_refs, grad_kinds, send_sems, recv_sems)

        @pl.when(n == 0)
        def _():
            exchange.start()
            carry[...] = jnp.zeros_like(carry)
            gbias_ref[...] = jnp.zeros_like(gbias_ref)
            dws_ref[...] = jnp.zeros_like(dws_ref)
            dbs_ref[...] = jnp.zeros_like(dbs_ref)
            vec_ref[...] = jnp.zeros_like(vec_ref)
            dsink_ref[...] = jnp.zeros_like(dsink_ref)

        @pl.when(n == nblk)
        def _():
            dkv_ref[...] = carry[...].astype(BF16)
            exchange.wait()

        @pl.when(n < nblk)
        def _():
            col = lax.broadcasted_iota(jnp.int32, (BLOCK, 2 * BLOCK), 1)
            lane = lax.broadcasted_iota(jnp.int32, (BLOCK, LANES), 1)
            low = lane < HEAD_DIM
            first_mask = (col < BLOCK) & (n == 0)
            kvprev = kvp_ref[...]
            kvcur = kv_ref[...]
            kk = jnp.concatenate([kvprev[:, :KV_W], kvcur[:, :KV_W]], axis=0)
            vv = jnp.concatenate([kvprev[:, KV_W:], kvcur[:, KV_W:]], axis=0)
            q_blk = q_ref[...]
            attn, probs, kvar, vvar = _attn_block_fwd(q_blk, kk, vv, bias_ref, sinks_ref, first_mask)
            aog_v = aog_ref[...]
            na_unit, r_a = _rms(attn, 1.0)
            gm, (u, tu, ta, xhat, rstd, vb, mixedv) = _gmlp_chunk_fwd(
                gu_ref[...], gv_ref[...], glng_ref[...], glnb_ref[...], wsm_ref, bsx_ref[...], amat_ref[...])
            gog_v = gog_ref[...]
            ng_unit, r_g = _rms(gm, 1.0)

            dmix = dmix_ref[...]
            dn_a = dmix[:, :ATTN_W]
            dn_g = dmix[:, ATTN_W:]
            vec_ref[0:1, :] += _colsum(dn_a * na_unit)
            vec_ref[1:2, :] += _colsum(dn_g * ng_unit)
            t_a = dn_a * aog_v
            d_attn = r_a * t_a - na_unit * (r_a * jnp.mean(t_a * na_unit, axis=-1, keepdims=True))
            t_g = dn_g * gog_v
            d_gm = r_g * t_g - ng_unit * (r_g * jnp.mean(t_g * ng_unit, axis=-1, keepdims=True))

            gu_v = gu_ref[...]
            dgu_ref[...] = (d_gm * mixedv * _gelu_grad(gu_v, tu)).astype(BF16)
            dmx = d_gm * u
            dbs_ref[...] += dmx
            dmxb = dmx.astype(BF16)
            dvn_cols = []
            for pair in range(N_GROUPS // 2):
                dp_ = dmxb[:, pair * LANES:(pair + 1) * LANES]
                vp = vb[:, pair * LANES:(pair + 1) * LANES]
                dvn_cols.append(jnp.where(low, _dot_tn(wsm_ref[2 * pair], dp_), _dot_tn(wsm_ref[2 * pair + 1], dp_)))
                zero = jnp.zeros_like(dp_)
                dws_ref[2 * pair] += _dot_nt(jnp.where(low, dp_, zero), vp)
                dws_ref[2 * pair + 1] += _dot_nt(jnp.where(low, zero, dp_), vp)
            dvn = jnp.concatenate(dvn_cols, axis=1)
            vec_ref[2:3, :] += _colsum(dvn * xhat)
            vec_ref[3:4, :] += _colsum(dvn)
            dxh = dvn * glng_ref[...]
            am = amat_ref[...]
            da = rstd * (dxh - _split_dot(dxh, am) - xhat * _split_dot(dxh * xhat, am))
            dgv_ref[...] = (da * _gelu_grad(gv_ref[...], ta)).astype(BF16)

            tk = [jnp.zeros((2 * BLOCK, LANES), F32) for _ in range(N_KV)]
            tv = [jnp.zeros((2 * BLOCK, LANES), F32) for _ in range(N_KV)]
            dq_cols = []
            for pair in range(N_HEADS // 2):
                d_pair = d_attn[:, pair * LANES:(pair + 1) * LANES]
                q_pair = q_blk[:, pair * LANES:(pair + 1) * LANES]
                dq_pair = None
                for par in range(2):
                    h = 2 * pair + par
                    kvh = h // (N_HEADS // N_KV)
                    p, ps = probs[h]
                    sel = low if par == 0 else jnp.logical_not(low)
                    do_h = jnp.where(sel, d_pair, 0.0).astype(BF16)
                    q_h = jnp.where(sel, q_pair, jnp.zeros_like(q_pair))
                    dp = _dot_nt(do_h, vvar[kvh][par])
                    delta = jnp.sum(p * dp, axis=-1, keepdims=True)
                    ds = p * (dp - delta)
                    dsink_ref[h] += -(ps * delta)
                    gbias_ref[h] += ds
                    dsb = ds.astype(BF16)
                    dqh = _dot(dsb, kvar[kvh][par])
                    dq_pair = dqh if dq_pair is None else dq_pair + dqh
                    tk[kvh] = tk[kvh] + _dot_tn(dsb, q_h)
                    tv[kvh] = tv[kvh] + _dot_tn(p.astype(BF16), do_h)
                dq_cols.append(dq_pair)
            dq_ref[...] = (jnp.concatenate(dq_cols, axis=1) * Q_SCALE).astype(BF16)
            dkk = _fold_kv(tk[0], tk[1])
            dvv = _fold_kv(tv[0], tv[1])
            dkv_ref[...] = (carry[...] + jnp.concatenate([dkk[:BLOCK], dvv[:BLOCK]], axis=1)).astype(BF16)
            carry[...] = jnp.concatenate([dkk[BLOCK:], dvv[BLOCK:]], axis=1)

    last = nblk - 1
    cur = lambda w: pl.BlockSpec((BLOCK, w), lambda n: (jnp.minimum(n, last), 0))
    prev = lambda w: pl.BlockSpec((BLOCK, w), lambda n: (jnp.clip(n - 1, 0, last), 0))
    outs = pl.pallas_call(
        body, name="bwd_mix",
        out_shape=[jax.ShapeDtypeStruct((s, ATTN_W), BF16), jax.ShapeDtypeStruct((s, 2 * KV_W), BF16),
                   jax.ShapeDtypeStruct((s, GMLP_W), BF16), jax.ShapeDtypeStruct((s, GMLP_W), BF16),
                   jax.ShapeDtypeStruct((N_HEADS, BLOCK, 2 * BLOCK), F32),
                   jax.ShapeDtypeStruct((N_GROUPS, BLOCK, BLOCK), F32),
                   jax.ShapeDtypeStruct((BLOCK, GMLP_W), F32), jax.ShapeDtypeStruct((8, GMLP_W), F32),
                   jax.ShapeDtypeStruct((N_HEADS, BLOCK, 1), F32)]
        + [jax.ShapeDtypeStruct(_rx_shape(p.shape, k), BF16) for p, k in zip(grad_parts, grad_kinds)],
        grid=(nblk + 1,),
        in_specs=[cur(ATTN_W), cur(2 * KV_W), prev(2 * KV_W), cur(GMLP_W), cur(GMLP_W), cur(D_MODEL),
                  _const_spec((N_HEADS, BLOCK, 2 * BLOCK)), pl.BlockSpec(memory_space=pltpu.SMEM),
                  _const_spec((1, GMLP_W)), _const_spec((1, GMLP_W)), _const_spec((N_GROUPS, BLOCK, BLOCK)),
                  _const_spec((BLOCK, GMLP_W)), _const_spec((GMLP_W, GMLP_W)), _const_spec((1, ATTN_W)),
                  _const_spec((1, GMLP_W))] + [ANY] * n_g,
        out_specs=[cur(ATTN_W), prev(2 * KV_W), cur(GMLP_W), cur(GMLP_W),
                   _const_spec((N_HEADS, BLOCK, 2 * BLOCK)), _const_spec((N_GROUPS, BLOCK, BLOCK)),
                   _const_spec((BLOCK, GMLP_W)), _const_spec((8, GMLP_W)), _const_spec((N_HEADS, BLOCK, 1))]
        + [ANY] * n_g,
        scratch_shapes=[pltpu.VMEM((BLOCK, 2 * KV_W), F32)] + _ChipExchange.sems(n_g),
        compiler_params=_params(("arbitrary",)),
    )(q, kv, kv, gu, gv, dmix, bias, sinks, gln_g, gln_b, wsm, bsx, amat, aog, gog, *grad_parts)
    return outs[:9], outs[9:]


def _mix_finalize(gbias, bucket, dws, dbs, dsink):
    def body(gb_ref, bucket_ref, dws_ref, dbs_ref, dsink_ref, tall_ref):
        bk = bucket_ref[...]
        lane = lax.broadcasted_iota(jnp.int32, (N_BUCKETS, LANES), 1)
        rowi = lax.broadcasted_iota(jnp.int32, (N_BUCKETS, LANES), 0)
        drb = jnp.zeros((N_BUCKETS, LANES), F32)
        dsk = jnp.zeros((8, LANES), F32)
        lane8 = lax.broadcasted_iota(jnp.int32, (8, LANES), 1)
        for h in range(N_HEADS):
            g = gb_ref[h]
            for b in range(N_BUCKETS):
                tot = jnp.sum(_colsum(jnp.where(bk == float(b), g, 0.0)), axis=1, keepdims=True)
                drb = jnp.where((lane == h) & (rowi == b), tot, drb)
            sk = jnp.sum(dsink_ref[h], axis=0, keepdims=True)
            dsk = jnp.where(lane8 == h, sk, dsk)
        tall_ref[TALL_RB:TALL_RB + N_BUCKETS, :] = drb
        tall_ref[TALL_SK:TALL_SK + 8, :] = dsk
        ti = lax.broadcasted_iota(jnp.int32, (BLOCK, BLOCK), 0)
        ui = lax.broadcasted_iota(jnp.int32, (BLOCK, BLOCK), 1)
        for g in range(N_GROUPS):
            tall_ref[g * BLOCK:(g + 1) * BLOCK, :] = jnp.where(ti >= ui, dws_ref[g], 0.0)
        gi = lax.broadcasted_iota(jnp.int32, (GMLP_W, LANES), 0) // GROUP_DIM
        li = lax.broadcasted_iota(jnp.int32, (GMLP_W, LANES), 1)
        ind = jnp.where(gi == li, 1.0, 0.0).astype(BF16)
        d = dbs_ref[...]
        hi = d.astype(BF16)
        r1 = d - hi.astype(F32)
        mid = r1.astype(BF16)
        lo = (r1 - mid.astype(F32)).astype(BF16)
        dbsg = _dot(hi, ind) + _dot(mid, ind) + _dot(lo, ind)
        tall_ref[TALL_BS:TALL_BS + N_GROUPS, :] = dbsg.T[:N_GROUPS, :]

    return pl.pallas_call(
        body, name="mix_finalize", out_shape=jax.ShapeDtypeStruct((TALL_ROWS, LANES), F32), grid=(1,),
        in_specs=[_const_spec((N_HEADS, BLOCK, 2 * BLOCK)), _const_spec((BLOCK, 2 * BLOCK)),
                  _const_spec((N_GROUPS, BLOCK, BLOCK)), _const_spec((BLOCK, GMLP_W)),
                  _const_spec((N_HEADS, BLOCK, 1))],
        out_specs=_const_spec((TALL_ROWS, LANES)),
        compiler_params=_params(("arbitrary",)),
    )(gbias, bucket, dws, dbs, dsink)


def _bwd_in(dq, dkv, dgu, dgv, dxa, x, modr, w_in, tm):
    s = x.shape[0]

    def body(dq_ref, dkv_ref, dgu_ref, dgv_ref, dxa_ref, x_ref, mod_ref, w_ref, gx_ref, acc_ref, db_ref):
        @pl.when(pl.program_id(0) == 0)
        def _():
            acc_ref[...] = jnp.zeros_like(acc_ref)
            db_ref[...] = jnp.zeros_like(db_ref)

        dproj = jnp.concatenate([dq_ref[...], dkv_ref[...], dgu_ref[...], dgv_ref[...]], axis=1)
        dh1 = _dot_nt(dproj, w_ref[...])
        gx_ref[...] = dxa_ref[...] + dh1 * (1.0 + mod_ref[1:2, :])
        acc_ref[0:1, :] += _colsum(dh1 * x_ref[...])
        acc_ref[1:2, :] += _colsum(dh1)
        db_ref[0:1, :] += _colsum(dproj.astype(F32))

    row = lambda w: pl.BlockSpec((tm, w), lambda i: (i, 0))
    return pl.pallas_call(
        body, name="bwd_in",
        out_shape=(jax.ShapeDtypeStruct((s, D_MODEL), F32), jax.ShapeDtypeStruct((8, D_MODEL), F32),
                   jax.ShapeDtypeStruct((8, IN_W), F32)),
        grid=(s // tm,),
        in_specs=[row(ATTN_W), row(2 * KV_W), row(GMLP_W), row(GMLP_W), row(D_MODEL), row(D_MODEL),
                  _const_spec((8, D_MODEL)), _const_spec((D_MODEL, IN_W))],
        out_specs=(row(D_MODEL), _const_spec((8, D_MODEL)), _const_spec((8, IN_W))),
        compiler_params=_params(("arbitrary",)),
    )(dq, dkv, dgu, dgv, dxa, x, modr, w_in)


def _wgrad(a, bs, tm, tk, name, owner_blocks=False):
    k_all, m = a.shape
    n = sum(b.shape[1] for b in bs)
    nk = k_all // tk
    n_b = len(bs)
    wb = n // N_CHIPS

    def body(a_ref, *rest):
        b_refs, (o_ref, ob_ref) = rest[:n_b], rest[n_b:]
        k = pl.program_id(1)

        @pl.when(k == 0)
        def _():
            o_ref[...] = jnp.zeros_like(o_ref)

        b = b_refs[0][...] if n_b == 1 else jnp.concatenate([r[...] for r in b_refs], axis=1)
        if owner_blocks:
            av = a_ref[...]
            for j in range(N_CHIPS):
                o_ref[j] += _dot_tn(av, b[:, j * wb:(j + 1) * wb])
        else:
            o_ref[...] += _dot_tn(a_ref[...], b)

        @pl.when(k == nk - 1)
        def _():
            ob_ref[...] = o_ref[...].astype(BF16)

    if owner_blocks:
        out_spec = pl.BlockSpec((N_CHIPS, tm, wb), lambda i, k: (0, i, 0))
        shape = (N_CHIPS, m, wb)
    else:
        out_spec = pl.BlockSpec((tm, n), lambda i, k: (i, 0))
        shape = (m, n)
    return pl.pallas_call(
        body, name=name, out_shape=(jax.ShapeDtypeStruct(shape, F32), jax.ShapeDtypeStruct(shape, BF16)),
        grid=(m // tm, nk),
        in_specs=[pl.BlockSpec((tk, tm), lambda i, k: (k, i))]
        + [pl.BlockSpec((tk, b.shape[1]), lambda i, k: (k, 0)) for b in bs],
        out_specs=(out_spec, out_spec),
        compiler_params=_params(("parallel", "arbitrary")),
    )(a, *bs)


def _adam_math(w, g, m, v):
    m2 = ADAM_B1 * m + (1.0 - ADAM_B1) * g
    v2 = ADAM_B2 * v + (1.0 - ADAM_B2) * (g * g)
    m_hat = m2 / (1.0 - ADAM_B1 ** ADAM_STEP)
    v_hat = v2 / (1.0 - ADAM_B2 ** ADAM_STEP)
    delta = -ADAM_LR * (m_hat / (jnp.sqrt(v_hat) + ADAM_EPS) + ADAM_WD * w)
    return delta, m2, v2


def _adam_halves(w, mine, got, m, v, tr, name):
    r, cc = w.shape
    h = r // 2
    nt = h // tr

    def body(c_ref, w_ref, mine_ref, got_ref, m_ref, v_ref, g_ref, d_ref, m2_ref, v2_ref):
        g = jnp.where(pl.program_id(0) == c_ref[0], mine_ref[...], got_ref[...])
        g_ref[...] = g
        d, m2, v2 = _adam_math(w_ref[...], g, m_ref[...], v_ref[...])
        d_ref[...] = d
        m2_ref[...] = m2
        v2_ref[...] = v2

    full = pl.BlockSpec((tr, cc), lambda hh, i, c_ref: (hh * nt + i, 0))
    half = pl.BlockSpec((tr, cc), lambda hh, i, c_ref: (i, 0))
    shp = jax.ShapeDtypeStruct((r, cc), F32)
    return pl.pallas_call(
        body, name=name, out_shape=(shp, shp, shp, shp),
        grid_spec=pltpu.PrefetchScalarGridSpec(
            num_scalar_prefetch=1, grid=(2, nt), in_specs=[full, half, half, full, full],
            out_specs=(full, full, full, full)),
        compiler_params=_params(("arbitrary", "arbitrary")),
    )(_core_index_scalar(), w, mine, got, m, v)


def _adam_w_ada(sc_t, dmod_cols, w, m, v, tr):
    r, cc = w.shape

    def body(sct_ref, dm_ref, w_ref, m_ref, v_ref, g_ref, d_ref, m2_ref, v2_ref):
        g = sct_ref[:, 0:1] * dm_ref[0:1, :]
        for k in range(1, N_DEV):
            g = g + sct_ref[:, k:k + 1] * dm_ref[k:k + 1, :]
        g_ref[...] = g
        d, m2, v2 = _adam_math(w_ref[...], g, m_ref[...], v_ref[...])
        d_ref[...] = d
        m2_ref[...] = m2
        v2_ref[...] = v2

    spec = pl.BlockSpec((tr, cc), lambda i: (i, 0))
    shp = jax.ShapeDtypeStruct((r, cc), F32)
    return pl.pallas_call(
        body, name="adam_w_ada", out_shape=(shp, shp, shp, shp), grid=(r // tr,),
        in_specs=[pl.BlockSpec((tr, N_DEV), lambda i: (i, 0)), _const_spec((N_DEV, cc)), spec, spec, spec],
        out_specs=(spec, spec, spec, spec), compiler_params=_params(("parallel",)),
    )(sc_t, dmod_cols, w, m, v)


def _pack_wide(acc_i, acc_m, acc_f, db_in, vec):
    arrs = [acc_i, acc_m, acc_f, db_in, vec]
    i_, m_, f_, b_, v_ = range(5)
    src = {"b_in": (b_, 0), "ln1_g": (m_, 2), "ln1_b": (m_, 3), "ln2_g": (f_, 1), "ln2_b": (f_, 2),
           "gmlp_ln_g": (v_, 2), "gmlp_ln_b": (v_, 3), "attn_out_g": (v_, 0), "gmlp_out_g": (v_, 1), "loss": (f_, 0)}
    dmod = [(i_, 1), (i_, 0), (m_, 4), (m_, 1), (m_, 0), (f_, 3)]

    def body(*refs):
        ins, wide_ref = refs[:5], refs[5]
        wide_ref[...] = jnp.zeros_like(wide_ref)
        for k, (a, row) in enumerate(dmod):
            wide_ref[0:1, k * D_MODEL:(k + 1) * D_MODEL] = ins[a][row:row + 1, :]
        for name, (a, row) in src.items():
            r, off, n = WIDE_LAYOUT[name]
            wide_ref[r:r + 1, off:off + n] = ins[a][row:row + 1, :]

    return pl.pallas_call(
        body, name="pack_wide", out_shape=jax.ShapeDtypeStruct((8, WIDE_W), F32), grid=(1,),
        in_specs=[_const_spec(a.shape) for a in arrs], out_specs=_const_spec((8, WIDE_W)),
        compiler_params=_params(("arbitrary",)),
    )(*arrs)


def _adam_small(gw, gt, wide_wmv, w_s, b_s, rel_bias, sinks):
    names = list(WIDE_PARAMS)
    tall = [("gmlp_w_s", w_s), ("gmlp_b_s", b_s), ("rel_bias", rel_bias), ("attn_sinks", sinks)]
    ins = [gw, gt]
    for n in names:
        ins += list(wide_wmv[n])
    for _, t in tall:
        ins += list(t)
    n_in = len(ins)

    def body(*refs):
        gw_ref, gt_ref = refs[0], refs[1]
        wmv = refs[2:n_in]
        dmod_ref, loss_ref = refs[n_in], refs[n_in + 1]
        outs = refs[n_in + 2:]

        def tall_sum(r0, nr):
            g = gt_ref[r0:r0 + nr, :]
            for d in range(1, N_DEV):
                g = g + gt_ref[d * TALL_ROWS + r0:d * TALL_ROWS + r0 + nr, :]
            return g

        def emit(k, g, w_ref, m_ref, v_ref):
            d, m2, v2 = _adam_math(w_ref[...], g, m_ref[...], v_ref[...])
            outs[4 * k][...] = g
            outs[4 * k + 1][...] = d
            outs[4 * k + 2][...] = m2
            outs[4 * k + 3][...] = v2

        gsum = gw_ref[0:8, :]
        for d in range(1, N_DEV):
            gsum = gsum + gw_ref[8 * d:8 * d + 8, :]
        for d in range(N_DEV):
            dmod_ref[d:d + 1, :] = gw_ref[8 * d:8 * d + 1, :]
        for k, n in enumerate(names):
            r, off, sz = WIDE_LAYOUT[n]
            emit(k, gsum[r:r + 1, off:off + sz], *wmv[3 * k:3 * k + 3])
        r, off, sz = WIDE_LAYOUT["loss"]
        tot = jnp.sum(gsum[r:r + 1, off:off + sz], axis=1, keepdims=True)
        loss_ref[...] = jnp.broadcast_to(tot * (0.5 / D_MODEL), loss_ref.shape)

        k0 = len(names)
        ws_refs = wmv[3 * k0:3 * k0 + 3]
        for g in range(N_GROUPS):
            rows = slice(g * BLOCK, (g + 1) * BLOCK)
            gg = tall_sum(g * BLOCK, BLOCK)
            d, m2, v2 = _adam_math(ws_refs[0][rows, :], gg, ws_refs[1][rows, :], ws_refs[2][rows, :])
            outs[4 * k0][rows, :] = gg
            outs[4 * k0 + 1][rows, :] = d
            outs[4 * k0 + 2][rows, :] = m2
            outs[4 * k0 + 3][rows, :] = v2
        emit(k0 + 1, tall_sum(TALL_BS, N_GROUPS), *wmv[3 * (k0 + 1):3 * (k0 + 1) + 3])
        emit(k0 + 2, tall_sum(TALL_RB, N_BUCKETS)[:, :N_HEADS], *wmv[3 * (k0 + 2):3 * (k0 + 2) + 3])
        emit(k0 + 3, tall_sum(TALL_SK, 8)[0:1, :N_HEADS], *wmv[3 * (k0 + 3):3 * (k0 + 3) + 3])

    out_shapes = [jax.ShapeDtypeStruct((N_DEV, WIDE_W), F32), jax.ShapeDtypeStruct((8, LANES), F32)]
    for n in names:
        out_shapes += [jax.ShapeDtypeStruct(wide_wmv[n][0].shape, F32)] * 4
    for _, t in tall:
        out_shapes += [jax.ShapeDtypeStruct(t[0].shape, F32)] * 4
    res = pl.pallas_call(
        body, name="adam_small", out_shape=out_shapes, grid=(1,),
        in_specs=[_const_spec(a.shape) for a in ins], out_specs=[_const_spec(o.shape) for o in out_shapes],
        compiler_params=_params(("arbitrary",)),
    )(*ins)
    out = {}
    for k, n in enumerate(names + [t[0] for t in tall]):
        out[n] = tuple(res[2 + 4 * k:6 + 4 * k])
    return res[0], res[1], out


def kernel(x, c, rel_bias, w_ada, b_ada, w_in, b_in, attn_sinks, gmlp_ln_g, gmlp_ln_b, gmlp_w_s, gmlp_b_s, attn_out_g, gmlp_out_g, w_out, ln1_g, ln1_b, w_gate_up, w_down, ln2_g, ln2_b, loss_target, m_rel_bias, m_w_ada, m_b_ada, m_w_in, m_b_in, m_attn_sinks, m_gmlp_ln_g, m_gmlp_ln_b, m_gmlp_w_s, m_gmlp_b_s, m_attn_out_g, m_gmlp_out_g, m_w_out, m_ln1_g, m_ln1_b, m_w_gate_up, m_w_down, m_ln2_g, m_ln2_b, v_rel_bias, v_w_ada, v_b_ada, v_w_in, v_b_in, v_attn_sinks, v_gmlp_ln_g, v_gmlp_ln_b, v_gmlp_w_s, v_gmlp_b_s, v_attn_out_g, v_gmlp_out_g, v_w_out, v_ln1_g, v_ln1_b, v_w_gate_up, v_w_down, v_ln2_g, v_ln2_b):
    ix, iy, ic = _my_pos()
    chip = 2 * ix + iy
    dev = 4 * ix + 2 * iy + ic
    s = x.shape[1]
    xs = x[0]
    tgt = loss_target[0]
    tm_big = min(512, s)
    tm_ffn = min(256, s)
    n_ada, n_in, n_gu = w_ada.shape[2], w_in.shape[2], w_gate_up.shape[2]

    (c_rows,), _ = _allgather8([jnp.pad(c, ((0, 7), (0, 0)))], "gather_c")
    c_all = c_rows.reshape(N_DEV, 8, D_MODEL)[:, 0, :]
    sc_all, mod_cols = _mod_part(c_all, w_ada[0], lax.dynamic_slice_in_dim(b_ada, chip * n_ada, n_ada, axis=1))
    (mod_rows,), _ = _allgather8([mod_cols], "gather_mod")
    mod_all = mod_rows.reshape(N_DEV, N_DEV, -1)
    mod_row = lax.dynamic_index_in_dim(mod_all[0::2], dev, axis=1, keepdims=False)
    modr = jnp.pad(mod_row.reshape(6, D_MODEL), ((0, 2), (0, 0)))

    w_in_s, w_out_s = w_in[0].astype(BF16), w_out[0].astype(BF16)
    w_gu_s, w_dn_s = w_gate_up[0].astype(BF16), w_down[0].astype(BF16)
    (w_in_g,) = _gather_weights([w_in_s], ["blk"], "gather_w_in")
    w_in_g = _insert_own(w_in_g, w_in_s, "blk", chip)
    w_in_f = jnp.transpose(w_in_g, (1, 0, 2)).reshape(D_MODEL, IN_W)

    bucket = _bucket_table()
    bias, wsm = _prep_tables(bucket, rel_bias, gmlp_w_s[0])
    bsx = jnp.repeat(gmlp_b_s[0].T, GROUP_DIM, axis=1)
    amat = _group_mean_matrix()
    sinks = attn_sinks[0]

    (h1, q, kv, gu, gv), (w_out_g,) = _fwd_in(xs, modr, w_in_f, b_in, tm_big, [w_out_s], ["blk"])
    w_out_f = _insert_own(w_out_g, w_out_s, "blk", chip).reshape(D_MODEL, D_MODEL)
    x1, y, mixed, (w_gu_g, w_dn_g) = _fwd_mix(
        q, kv, gu, gv, xs, modr, bias, sinks, gmlp_ln_g, gmlp_ln_b, wsm, bsx, amat, attn_out_g, gmlp_out_g, w_out_f,
        ln1_g, ln1_b, tm_big, [w_gu_s, w_dn_s], ["blk", "blk"])
    assert n_gu == FF_CHUNK
    w_gu_f = _insert_own(w_gu_g, w_gu_s, "blk", chip)
    w_dn_f = _insert_own(w_dn_g, w_dn_s, "blk", chip).reshape(D_FF, D_MODEL)
    h2, act, dy2, dx1a, acc_f = _fwd_ffn(x1, tgt, modr, ln2_g, ln2_b, w_gu_f, w_dn_f, tm_ffn)

    a_act, dgu_ff, dh2 = _bwd_ffn(dy2, act, w_gu_f, w_dn_f, tm_ffn)
    g_dn, g_dn_b = _wgrad(a_act, [dy2], D_FF // 2, min(512, s), "wgrad_down")
    g_gu, g_gu_b = _wgrad(h2, [dgu_ff], 512, min(256, s), "wgrad_gate_up")
    dxa, dy, dmix, acc_m = _bwd_mid(dh2, dx1a, x1, xs, y, modr, ln1_g, w_out_f, tm_big)
    g_out, g_out_b = _wgrad(mixed, [dy], 512, min(512, s), "wgrad_out")
    blk3 = lambda a, rows: a.reshape(N_CHIPS, rows, a.shape[1])
    kinds_a = ["blk", "cols", "blk"]
    fulls_a = [blk3(g_dn, D_FF // N_CHIPS), g_gu, blk3(g_out, D_MODEL // N_CHIPS)]
    fulls_a_b = [blk3(g_dn_b, D_FF // N_CHIPS), g_gu_b, blk3(g_out_b, D_MODEL // N_CHIPS)]
    gots_a = _swap_halves(fulls_a_b, kinds_a, "rs_swap_a")
    parts_a = [_add_halves(f, g, k, "rs_add_a%d" % i) for i, (f, g, k) in enumerate(zip(fulls_a, gots_a, kinds_a))]
    (dq, dkv, dgu, dgv, gbias, dws, dbs, vec, dsink), rxs_a = _bwd_mix(
        q, kv, gu, gv, dmix, bias, sinks, gmlp_ln_g, gmlp_ln_b, wsm, bsx, amat, attn_out_g, gmlp_out_g,
        [p[1] for p in parts_a], kinds_a)
    tall_g = _mix_finalize(gbias, bucket, dws, dbs, dsink)
    full_in, full_in_b = _wgrad(h1, [dq, dkv, dgu, dgv], 512, min(512, s), "wgrad_in", owner_blocks=True)
    (got_in,) = _swap_halves([full_in_b], ["blk"], "rs_swap_b")
    part_in = _add_halves(full_in, got_in, "blk", "rs_add_b")
    grad_x, acc_i, db_in = _bwd_in(dq, dkv, dgu, dgv, dxa, xs, modr, w_in_f, tm_big)

    wide_g = _pack_wide(acc_i, acc_m, acc_f, db_in, vec)
    (gw, gt), (rx_in,) = _allgather8([wide_g, tall_g], "gather_small", [part_in[1]], ["blk"])
    wide_wmv = {"b_ada": (b_ada, m_b_ada, v_b_ada), "b_in": (b_in, m_b_in, v_b_in),
                "ln1_g": (ln1_g, m_ln1_g, v_ln1_g), "ln1_b": (ln1_b, m_ln1_b, v_ln1_b),
                "ln2_g": (ln2_g, m_ln2_g, v_ln2_g), "ln2_b": (ln2_b, m_ln2_b, v_ln2_b),
                "gmlp_ln_g": (gmlp_ln_g, m_gmlp_ln_g, v_gmlp_ln_g), "gmlp_ln_b": (gmlp_ln_b, m_gmlp_ln_b, v_gmlp_ln_b),
                "attn_out_g": (attn_out_g, m_attn_out_g, v_attn_out_g),
                "gmlp_out_g": (gmlp_out_g, m_gmlp_out_g, v_gmlp_out_g)}
    rows2 = lambda a: a.reshape(-1, a.shape[-1])
    dmod_all, loss_t, small = _adam_small(
        gw, gt, wide_wmv, tuple(rows2(a) for a in (gmlp_w_s, m_gmlp_w_s, v_gmlp_w_s)),
        tuple(rows2(a) for a in (gmlp_b_s, m_gmlp_b_s, v_gmlp_b_s)), (rel_bias, m_rel_bias, v_rel_bias),
        (attn_sinks, m_attn_sinks, v_attn_sinks))
    loss = loss_t[0, 0]

    dmod_cols = lax.dynamic_slice_in_dim(dmod_all, chip * n_ada, n_ada, axis=1)
    g_ada, d_ada, m_ada, v_ada = _adam_w_ada(sc_all.T, dmod_cols, w_ada[0], m_w_ada[0], v_w_ada[0], 256)

    sums = [(parts_a[0][0], rxs_a[0], "blk", 176), (parts_a[1][0], rxs_a[1], "cols", 256),
            (parts_a[2][0], rxs_a[2], "blk", 128), (part_in[0], rx_in, "blk", 256)]
    mine = [_sum_chips(p, rx, k, tr, "rs_sum_%d" % i) for i, (p, rx, k, tr) in enumerate(sums)]
    got = _share_halves(mine, "rs_share")

    gs_dn, d_dn, m_dn, v_dn = _adam_halves(w_down[0], mine[0], got[0], m_w_down[0], v_w_down[0], 176, "adam_w_down")
    gs_gu, d_gu, m_gu, v_gu = _adam_halves(w_gate_up[0], mine[1], got[1], m_w_gate_up[0], v_w_gate_up[0], 256,
                                           "adam_w_gate_up")
    gs_out, d_out, m_out, v_out = _adam_halves(w_out[0], mine[2], got[2], m_w_out[0], v_w_out[0], 128, "adam_w_out")
    gs_in, d_in, m_in, v_in = _adam_halves(w_in[0], mine[3], got[3], m_w_in[0], v_w_in[0], 256, "adam_w_in")

    big = {"w_ada": (g_ada, d_ada, m_ada, v_ada), "w_in": (gs_in, d_in, m_in, v_in), "w_out": (gs_out, d_out, m_out, v_out),
           "w_gate_up": (gs_gu, d_gu, m_gu, v_gu), "w_down": (gs_dn, d_dn, m_dn, v_dn)}
    order = ["rel_bias", "w_ada", "b_ada", "w_in", "b_in", "attn_sinks", "gmlp_ln_g", "gmlp_ln_b", "gmlp_w_s", "gmlp_b_s",
             "attn_out_g", "gmlp_out_g", "w_out", "ln1_g", "ln1_b", "w_gate_up", "w_down", "ln2_g", "ln2_b"]
    shapes = {"gmlp_w_s": gmlp_w_s.shape, "gmlp_b_s": gmlp_b_s.shape}
    outs = [loss, grad_x[None]]
    for k in range(4):
        for name in order:
            if name in big:
                outs.append(big[name][k][None])
            elif name in shapes:
                outs.append(small[name][k].reshape(shapes[name]))
            else:
                outs.append(small[name][k])
    return tuple(outs)
```

```python
import math

import numpy as np
import jax
import jax.numpy as jnp
from jax import lax
from jax.experimental import pallas as pl
from jax.experimental.pallas import tpu as pltpu

F32 = jnp.float32
BF16 = jnp.bfloat16
MESH = pl.DeviceIdType.MESH

D_MODEL = 1024
N_HEADS = 8
N_KV = 2
HEAD_DIM = 64
ATTN_W = N_HEADS * HEAD_DIM
KV_W = N_KV * HEAD_DIM
N_GROUPS = 8
GROUP_DIM = 64
GMLP_W = N_GROUPS * GROUP_DIM
IN_W = ATTN_W + 2 * KV_W + 2 * GMLP_W
BLOCK = 128
N_BUCKETS = 32
MAX_DISTANCE = 128
D_FF = 2816
ALPHA = 2.0 ** 0.25
LN_EPS = 1e-5
NEG_INF = -1e30
ADAM_LR, ADAM_B1, ADAM_B2, ADAM_EPS, ADAM_WD, ADAM_STEP = 0.001, 0.9, 0.999, 1e-8, 0.01, 10
N_CHIPS = 4
N_DEV = 8
LANES = 128
V7X_VMEM_LIMIT = 56 * 2 ** 20
GELU_C = math.sqrt(2.0 / math.pi)
Q_SCALE = HEAD_DIM ** -0.5
ANY = pl.BlockSpec(memory_space=pl.ANY)

TALL_BS = N_GROUPS * BLOCK
TALL_RB = TALL_BS + 8
TALL_SK = TALL_RB + N_BUCKETS
TALL_ROWS = TALL_SK + 8
WIDE_W = 6 * D_MODEL
WIDE_LAYOUT = {
    "b_ada": (0, 0, 6 * D_MODEL),
    "b_in": (1, 0, IN_W), "ln1_g": (1, IN_W, D_MODEL), "ln1_b": (1, IN_W + D_MODEL, D_MODEL),
    "ln2_g": (1, IN_W + 2 * D_MODEL, D_MODEL), "ln2_b": (1, IN_W + 3 * D_MODEL, D_MODEL),
    "gmlp_ln_g": (2, 0, GMLP_W), "gmlp_ln_b": (2, GMLP_W, GMLP_W), "attn_out_g": (2, 2 * GMLP_W, ATTN_W),
    "gmlp_out_g": (2, 2 * GMLP_W + ATTN_W, GMLP_W), "loss": (2, 3 * GMLP_W + ATTN_W, D_MODEL)}
WIDE_PARAMS = tuple(n for n in WIDE_LAYOUT if n != "loss")


def _params(sem=None):
    return pltpu.CompilerParams(dimension_semantics=sem, vmem_limit_bytes=V7X_VMEM_LIMIT)


def _const_spec(shape, single=False):
    nd = len(shape)
    if single:
        return pl.BlockSpec(shape, lambda *_: (0,) * nd, pipeline_mode=pl.Buffered(1))
    return pl.BlockSpec(shape, lambda *_: (0,) * nd)


def _dot(a, b):
    return jnp.dot(a, b, preferred_element_type=F32)


def _dot_nt(a, b):
    return lax.dot_general(a, b, (((1,), (1,)), ((), ())), preferred_element_type=F32)


def _dot_tn(a, b):
    return lax.dot_general(a, b, (((0,), (0,)), ((), ())), preferred_element_type=F32)


def _gelu(x):
    t = jnp.tanh(GELU_C * (x + 0.044715 * x * x * x))
    return 0.5 * x * (1.0 + t), t


def _gelu_grad(x, t):
    return 0.5 * (1.0 + t) + 0.5 * x * (1.0 - t * t) * GELU_C * (1.0 + 3.0 * 0.044715 * x * x)


def _split_dot(x, a):
    hi = x.astype(BF16)
    lo = (x - hi.astype(F32)).astype(BF16)
    return _dot(hi, a) + _dot(lo, a)


def _group_mean_matrix():
    g = np.arange(GMLP_W) // GROUP_DIM
    return jnp.asarray((g[:, None] == g[None, :]).astype(np.float32) / GROUP_DIM, dtype=BF16)


def _ln_stats(z):
    mu = jnp.mean(z, axis=-1, keepdims=True)
    d = z - mu
    var = jnp.mean(d * d, axis=-1, keepdims=True)
    rstd = lax.rsqrt(var + LN_EPS)
    return d * rstd, rstd


def _ln_bwd(dxhat, xhat, rstd):
    m1 = jnp.mean(dxhat, axis=-1, keepdims=True)
    m2 = jnp.mean(dxhat * xhat, axis=-1, keepdims=True)
    return rstd * (dxhat - m1 - xhat * m2)


def _colsum(x):
    return jnp.sum(x, axis=0, keepdims=True)


def _my_pos():
    return lax.axis_index("x"), lax.axis_index("y"), lax.axis_index("c")


def _other_chips(x, y):
    return [(1 - x, y), (x, 1 - y), (1 - x, 1 - y)]


def _chip_index_scalar():
    ix, iy, _ = _my_pos()
    return jnp.reshape(2 * ix + iy, (1,)).astype(jnp.int32)


def _core_index_scalar():
    return jnp.reshape(lax.axis_index("c"), (1,)).astype(jnp.int32)


def _allgather8(vs, name, grad_parts=(), grad_kinds=()):
    n_v, n_g = len(vs), len(grad_parts)

    def body(*refs):
        x_refs, part_refs = refs[:n_v], refs[n_v:n_v + n_g]
        out_refs, rx_refs = refs[n_v + n_g:2 * n_v + n_g], refs[2 * n_v + n_g:2 * n_v + 2 * n_g]
        send_sems, recv_sems, local_sems, ex_send, ex_recv = refs[2 * n_v + 2 * n_g:]
        x, y, c = _my_pos()
        me, sibling = (x, y, c), (x, y, 1 - c)
        chips = _other_chips(x, y)
        exchange = _ChipExchange(part_refs, rx_refs, grad_kinds, ex_send, ex_recv)
        exchange.start()

        def rows(a, px, py, pc):
            m_per = x_refs[a].shape[0]
            return out_refs[a].at[pl.ds((4 * px + 2 * py + pc) * m_per, m_per), :]

        def copy(a, k, block, to, src=None):
            return pltpu.make_async_remote_copy(
                src_ref=rows(a, *block) if src is None else src, dst_ref=rows(a, *block),
                send_sem=send_sems.at[7 * a + k], recv_sem=recv_sems.at[7 * a + k], device_id=to, device_id_type=MESH)

        mine = [pltpu.make_async_copy(x_refs[a], rows(a, *me), local_sems.at[a]) for a in range(n_v)]
        for cp in mine:
            cp.start()
        first, passed = [], []
        for a in range(n_v):
            first.append(copy(a, 0, me, sibling, src=x_refs[a]))
            first += [copy(a, 1 + j, me, (*chip, c), src=x_refs[a]) for j, chip in enumerate(chips)]
        for cp in first:
            cp.start()
        for a in range(n_v):
            for j, chip in enumerate(chips):
                copy(a, 1 + j, (*chip, c), me).wait_recv()
                cp = copy(a, 4 + j, (*chip, c), sibling)
                cp.start()
                passed.append(cp)
        for a in range(n_v):
            copy(a, 0, sibling, me).wait_recv()
            for j, chip in enumerate(chips):
                copy(a, 4 + j, (*chip, 1 - c), me).wait_recv()
        for cp in first + passed:
            cp.wait_send()
        for cp in mine:
            cp.wait()
        exchange.wait()

    vmem = pl.BlockSpec(memory_space=pltpu.VMEM)
    outs = pl.pallas_call(
        body, name=name,
        out_shape=[jax.ShapeDtypeStruct((N_DEV * v.shape[0], v.shape[1]), v.dtype) for v in vs]
        + [jax.ShapeDtypeStruct(_rx_shape(p.shape, k), BF16) for p, k in zip(grad_parts, grad_kinds)],
        in_specs=[vmem] * n_v + [ANY] * n_g, out_specs=[vmem] * n_v + [ANY] * n_g,
        scratch_shapes=[pltpu.SemaphoreType.DMA((7 * n_v,)), pltpu.SemaphoreType.DMA((7 * n_v,)),
                        pltpu.SemaphoreType.DMA((n_v,))] + _ChipExchange.sems(max(n_g, 1)),
        compiler_params=pltpu.CompilerParams(vmem_limit_bytes=V7X_VMEM_LIMIT),
    )(*vs, *grad_parts)
    return outs[:n_v], outs[n_v:]


def _gathered_shape(shard, kind):
    r, cc = shard.shape
    return (N_CHIPS, r, cc) if kind == "blk" else (r, N_CHIPS * cc)


class _WeightGather:
    def __init__(self, shards, gathered, kinds, send_sems, recv_sems):
        self.shards, self.gathered, self.kinds = shards, gathered, kinds
        self.send_sems, self.recv_sems = send_sems, recv_sems
        self.x, self.y, self.c = _my_pos()
        self.chips = _other_chips(self.x, self.y)

    def _dst(self, a, chip, pc):
        r, cc = self.shards[a].shape
        h = r // 2
        g = self.gathered[a]
        if self.kinds[a] == "blk":
            return g.at[chip, pl.ds(pc * h, h), :]
        return g.at[pl.ds(pc * h, h), pl.ds(chip * cc, cc)]

    def _copy(self, a, k, chip, pc, to, src=None):
        d = self._dst(a, chip, pc)
        return pltpu.make_async_remote_copy(
            src_ref=d if src is None else src, dst_ref=d, send_sem=self.send_sems.at[a * 6 + k],
            recv_sem=self.recv_sems.at[a * 6 + k], device_id=to, device_id_type=MESH)

    def _each(self):
        for a in range(len(self.shards)):
            for j, chip in enumerate(self.chips):
                yield a, j, chip, 2 * chip[0] + chip[1]

    def start(self):
        my_chip = 2 * self.x + self.y
        for a, j, chip, _ in self._each():
            h = self.shards[a].shape[0] // 2
            self._copy(a, j, my_chip, self.c, (*chip, self.c), src=self.shards[a].at[pl.ds(self.c * h, h), :]).start()

    def forward(self):
        me, sibling = (self.x, self.y, self.c), (self.x, self.y, 1 - self.c)
        for a, j, chip, cj in self._each():
            self._copy(a, j, cj, self.c, me).wait_recv()
            self._copy(a, 3 + j, cj, self.c, sibling).start()

    def finish(self):
        me = (self.x, self.y, self.c)
        for a, j, chip, cj in self._each():
            self._copy(a, 3 + j, cj, 1 - self.c, me).wait_recv()
        for a, j, chip, cj in self._each():
            self._copy(a, j, cj, self.c, me).wait_send()
            self._copy(a, 3 + j, cj, self.c, me).wait_send()

    @staticmethod
    def sems(n_arr):
        return [pltpu.SemaphoreType.DMA((n_arr * 6,)), pltpu.SemaphoreType.DMA((n_arr * 6,))]


def _insert_own(gathered, shard, kind, chip):
    if kind == "blk":
        return lax.dynamic_update_slice(gathered, shard[None], (chip, 0, 0))
    return lax.dynamic_update_slice(gathered, shard, (0, chip * shard.shape[1]))


def _gather_weights(shards, kinds, name):
    n_arr = len(shards)

    def body(*refs):
        g = _WeightGather(refs[:n_arr], refs[n_arr:2 * n_arr], kinds, *refs[2 * n_arr:])
        g.start()
        g.forward()
        g.finish()

    return pl.pallas_call(
        body, name=name,
        out_shape=[jax.ShapeDtypeStruct(_gathered_shape(s, k), BF16) for s, k in zip(shards, kinds)],
        in_specs=[ANY] * n_arr, out_specs=[ANY] * n_arr, scratch_shapes=_WeightGather.sems(n_arr),
    )(*shards)


def _half_of_full(ref, kind, pc):
    if kind == "blk":
        h = ref.shape[1] // 2
        return ref.at[:, pl.ds(pc * h, h), :]
    h = ref.shape[0] // 2
    return ref.at[pl.ds(pc * h, h), :]


def _half_shape(shape, kind):
    return (shape[0], shape[1] // 2, shape[2]) if kind == "blk" else (shape[0] // 2, shape[1])


def _swap_halves(fulls_bf16, kinds, name):
    n_arr = len(fulls_bf16)

    def body(*refs):
        ins, outs = refs[:n_arr], refs[n_arr:2 * n_arr]
        send_sems, recv_sems = refs[2 * n_arr:]
        x, y, c = _my_pos()
        cps = []
        for a in range(n_arr):
            cp = pltpu.make_async_remote_copy(
                src_ref=_half_of_full(ins[a], kinds[a], 1 - c), dst_ref=outs[a], send_sem=send_sems.at[a],
                recv_sem=recv_sems.at[a], device_id=(x, y, 1 - c), device_id_type=MESH)
            cp.start()
            cps.append(cp)
        for cp in cps:
            cp.wait()

    return pl.pallas_call(
        body, name=name,
        out_shape=[jax.ShapeDtypeStruct(_half_shape(a.shape, k), a.dtype) for a, k in zip(fulls_bf16, kinds)],
        in_specs=[ANY] * n_arr, out_specs=[ANY] * n_arr,
        scratch_shapes=[pltpu.SemaphoreType.DMA((n_arr,)), pltpu.SemaphoreType.DMA((n_arr,))],
    )(*fulls_bf16)


def _add_halves(full, got, kind, name):
    hs = _half_shape(full.shape, kind)

    def body(c_ref, a_ref, b_ref, o_ref, ob_ref):
        p = a_ref[...] + b_ref[...].astype(F32)
        o_ref[...] = p
        ob_ref[...] = p.astype(BF16)

    if kind == "blk":
        nb, h, cc = hs
        own = pl.BlockSpec((1, h, cc), lambda b, c_ref: (b, c_ref[0], 0))
        other = pl.BlockSpec((1, h, cc), lambda b, c_ref: (b, 0, 0))
    else:
        h, cc = hs[0], hs[1] // N_CHIPS
        own = pl.BlockSpec((h, cc), lambda b, c_ref: (c_ref[0], b))
        other = pl.BlockSpec((h, cc), lambda b, c_ref: (0, b))
    return pl.pallas_call(
        body, name=name, out_shape=(jax.ShapeDtypeStruct(hs, F32), jax.ShapeDtypeStruct(hs, BF16)),
        grid_spec=pltpu.PrefetchScalarGridSpec(
            num_scalar_prefetch=1, grid=(N_CHIPS,), in_specs=[own, other], out_specs=(other, other)),
        compiler_params=_params(("arbitrary",)),
    )(_core_index_scalar(), full, got)


def _rx_shape(part_shape, kind):
    if kind == "blk":
        return (3, part_shape[1], part_shape[2])
    return (3, part_shape[0], part_shape[1] // N_CHIPS)


class _ChipExchange:
    def __init__(self, parts, rxs, kinds, send_sems, recv_sems):
        self.parts, self.rxs, self.kinds = parts, rxs, kinds
        self.send_sems, self.recv_sems = send_sems, recv_sems
        self.x, self.y, self.c = _my_pos()
        self.chips = _other_chips(self.x, self.y)

    def _copies(self):
        for a in range(len(self.parts)):
            for j, chip in enumerate(self.chips):
                cj = 2 * chip[0] + chip[1]
                if self.kinds[a] == "blk":
                    src = self.parts[a].at[cj]
                else:
                    cc = self.parts[a].shape[1] // N_CHIPS
                    src = self.parts[a].at[:, pl.ds(cj * cc, cc)]
                yield pltpu.make_async_remote_copy(
                    src_ref=src, dst_ref=self.rxs[a].at[j], send_sem=self.send_sems.at[a * 3 + j],
                    recv_sem=self.recv_sems.at[a * 3 + j], device_id=(*chip, self.c), device_id_type=MESH)

    def start(self):
        for cp in self._copies():
            cp.start()

    def wait(self):
        for cp in self._copies():
            cp.wait_recv()
        for cp in self._copies():
            cp.wait_send()

    @staticmethod
    def sems(n_arr):
        return [pltpu.SemaphoreType.DMA((n_arr * 3,)), pltpu.SemaphoreType.DMA((n_arr * 3,))]


def _sum_chips(part, rx, kind, tr, name):
    _, h, cc = rx.shape
    flips = (2, 1, 3)

    def body(chip_ref, p_ref, rx_ref, o_ref):
        own = p_ref[...].reshape(tr, cc)
        for mc in range(N_CHIPS):
            @pl.when(chip_ref[0] == mc)
            def _():
                terms = sorted([(mc, None)] + [(mc ^ f, j) for j, f in enumerate(flips)])
                acc = None
                for _, j in terms:
                    t = own if j is None else rx_ref[j].astype(F32)
                    acc = t if acc is None else acc + t
                o_ref[...] = acc

    if kind == "blk":
        own_spec = pl.BlockSpec((1, tr, cc), lambda i, chip_ref: (chip_ref[0], i, 0))
    else:
        own_spec = pl.BlockSpec((tr, cc), lambda i, chip_ref: (i, chip_ref[0]))
    return pl.pallas_call(
        body, name=name, out_shape=jax.ShapeDtypeStruct((h, cc), F32),
        grid_spec=pltpu.PrefetchScalarGridSpec(
            num_scalar_prefetch=1, grid=(h // tr,),
            in_specs=[own_spec, pl.BlockSpec((3, tr, cc), lambda i, chip_ref: (0, i, 0))],
            out_specs=pl.BlockSpec((tr, cc), lambda i, chip_ref: (i, 0))),
        compiler_params=_params(("arbitrary",)),
    )(_chip_index_scalar(), part, rx)


def _share_halves(halves, name):
    n_arr = len(halves)

    def body(*refs):
        ins, outs = refs[:n_arr], refs[n_arr:2 * n_arr]
        send_sems, recv_sems = refs[2 * n_arr:]
        x, y, c = _my_pos()
        cps = []
        for a in range(n_arr):
            cp = pltpu.make_async_remote_copy(
                src_ref=ins[a], dst_ref=outs[a], send_sem=send_sems.at[a], recv_sem=recv_sems.at[a],
                device_id=(x, y, 1 - c), device_id_type=MESH)
            cp.start()
            cps.append(cp)
        for cp in cps:
            cp.wait()

    return pl.pallas_call(
        body, name=name, out_shape=[jax.ShapeDtypeStruct(h.shape, h.dtype) for h in halves],
        in_specs=[ANY] * n_arr, out_specs=[ANY] * n_arr,
        scratch_shapes=[pltpu.SemaphoreType.DMA((n_arr,)), pltpu.SemaphoreType.DMA((n_arr,))],
    )(*halves)


def _mod_part(c_all, w_ada_s, b_ada_s):
    n = w_ada_s.shape[1]

    def body(c_ref, w_ref, b_ref, sc_ref, mod_ref):
        cv = c_ref[...]
        sc = cv * (1.0 / (1.0 + jnp.exp(-cv)))
        sc_ref[...] = sc
        a_hi = sc.astype(BF16)
        a_lo = (sc - a_hi.astype(F32)).astype(BF16)
        w = w_ref[...]
        w_hi = w.astype(BF16)
        w_lo = (w - w_hi.astype(F32)).astype(BF16)
        mod_ref[...] = _dot(a_hi, w_hi) + _dot(a_hi, w_lo) + _dot(a_lo, w_hi) + b_ref[...]

    return pl.pallas_call(
        body, name="mod_part",
        out_shape=(jax.ShapeDtypeStruct((N_DEV, D_MODEL), F32), jax.ShapeDtypeStruct((N_DEV, n), F32)),
        grid=(1,),
        in_specs=[_const_spec((N_DEV, D_MODEL)), _const_spec((D_MODEL, n)), _const_spec((1, n))],
        out_specs=(_const_spec((N_DEV, D_MODEL)), _const_spec((N_DEV, n))),
        compiler_params=_params(("arbitrary",)),
    )(c_all, w_ada_s, b_ada_s)


def _bucket_table():
    qi = jnp.arange(BLOCK)[:, None]
    si = jnp.arange(2 * BLOCK)[None, :]
    dist = qi + BLOCK - si
    max_exact = N_BUCKETS // 2
    n = jnp.maximum(dist, 0)
    nf = jnp.maximum(n, max_exact).astype(F32)
    large = max_exact + (jnp.log(nf / max_exact) / math.log(MAX_DISTANCE / max_exact)
                         * (N_BUCKETS - max_exact)).astype(jnp.int32)
    large = jnp.minimum(large, N_BUCKETS - 1)
    return jnp.where(n < max_exact, n, large).astype(F32)


def _prep_tables(bucket, rel_bias, w_s):
    def body(bucket_ref, rb_ref, ws_ref, bias_ref, wsm_ref):
        qi = lax.broadcasted_iota(jnp.int32, (BLOCK, 2 * BLOCK), 0)
        si = lax.broadcasted_iota(jnp.int32, (BLOCK, 2 * BLOCK), 1)
        dist = qi + BLOCK - si
        in_window = (dist >= 0) & (dist < BLOCK)
        bk = bucket_ref[...]
        for h in range(N_HEADS):
            acc = jnp.zeros((BLOCK, 2 * BLOCK), F32)
            for b in range(N_BUCKETS):
                acc = jnp.where(bk == float(b), rb_ref[b, h], acc)
            bias_ref[h] = jnp.where(in_window, acc, NEG_INF)
        ti = lax.broadcasted_iota(jnp.int32, (BLOCK, BLOCK), 0)
        ui = lax.broadcasted_iota(jnp.int32, (BLOCK, BLOCK), 1)
        for g in range(N_GROUPS):
            wsm_ref[g] = jnp.where(ti >= ui, ws_ref[g], 0.0).astype(BF16)

    return pl.pallas_call(
        body, name="prep_tables",
        out_shape=(jax.ShapeDtypeStruct((N_HEADS, BLOCK, 2 * BLOCK), F32),
                   jax.ShapeDtypeStruct((N_GROUPS, BLOCK, BLOCK), BF16)),
        grid=(1,),
        in_specs=[_const_spec((BLOCK, 2 * BLOCK)), pl.BlockSpec(memory_space=pltpu.SMEM),
                  _const_spec((N_GROUPS, BLOCK, BLOCK))],
        out_specs=(_const_spec((N_HEADS, BLOCK, 2 * BLOCK)), _const_spec((N_GROUPS, BLOCK, BLOCK))),
        compiler_params=_params(("arbitrary",)),
    )(bucket, rel_bias, w_s)


def _fwd_in(x, modr, w_in, b_in, tm, shards, kinds):
    s = x.shape[0]
    n_steps = s // tm
    fwd_step = (3 * n_steps) // 4
    n_w = len(shards)

    def body(x_ref, mod_ref, w_ref, b_ref, *rest):
        shard_refs = rest[:n_w]
        h1_ref, q_ref, kv_ref, gu_ref, gv_ref = rest[n_w:n_w + 5]
        gathered_refs = rest[n_w + 5:2 * n_w + 5]
        send_sems, recv_sems = rest[2 * n_w + 5:]
        i = pl.program_id(0)
        gather = _WeightGather(shard_refs, gathered_refs, kinds, send_sems, recv_sems)

        @pl.when(i == 0)
        def _():
            gather.start()

        h1 = (x_ref[...] * (1.0 + mod_ref[1:2, :]) + mod_ref[0:1, :]).astype(BF16)
        h1_ref[...] = h1
        proj = _dot(h1, w_ref[...]) + b_ref[...]
        q_ref[...] = (proj[:, :ATTN_W] * Q_SCALE).astype(BF16)
        kv_ref[...] = proj[:, ATTN_W:ATTN_W + 2 * KV_W].astype(BF16)
        gu_ref[...] = proj[:, ATTN_W + 2 * KV_W:ATTN_W + 2 * KV_W + GMLP_W]
        gv_ref[...] = proj[:, ATTN_W + 2 * KV_W + GMLP_W:]

        @pl.when(i == fwd_step)
        def _():
            gather.forward()

        @pl.when(i == n_steps - 1)
        def _():
            gather.finish()

    row = lambda w: pl.BlockSpec((tm, w), lambda i: (i, 0))
    outs = pl.pallas_call(
        body, name="fwd_in",
        out_shape=[jax.ShapeDtypeStruct((s, D_MODEL), BF16), jax.ShapeDtypeStruct((s, ATTN_W), BF16),
                   jax.ShapeDtypeStruct((s, 2 * KV_W), BF16), jax.ShapeDtypeStruct((s, GMLP_W), F32),
                   jax.ShapeDtypeStruct((s, GMLP_W), F32)]
        + [jax.ShapeDtypeStruct(_gathered_shape(sh, k), BF16) for sh, k in zip(shards, kinds)],
        grid=(n_steps,),
        in_specs=[row(D_MODEL), _const_spec((8, D_MODEL)), _const_spec((D_MODEL, IN_W)), _const_spec((1, IN_W))]
        + [ANY] * n_w,
        out_specs=[row(D_MODEL), row(ATTN_W), row(2 * KV_W), row(GMLP_W), row(GMLP_W)] + [ANY] * n_w,
        scratch_shapes=_WeightGather.sems(n_w),
        compiler_params=_params(("arbitrary",)),
    )(x, modr, w_in, b_in, *shards)
    return outs[:5], outs[5:]


def _kv_variants(kk):
    kf = kk.astype(F32)
    lane = lax.broadcasted_iota(jnp.int32, kf.shape, 1)
    low = lane < HEAD_DIM
    k0_lo = jnp.where(low, kf, 0.0)
    k1_hi = jnp.where(low, 0.0, kf)
    k0_hi = pltpu.roll(k0_lo, HEAD_DIM, 1)
    k1_lo = pltpu.roll(k1_hi, HEAD_DIM, 1)
    return ((k0_lo.astype(BF16), k0_hi.astype(BF16)), (k1_lo.astype(BF16), k1_hi.astype(BF16)))


def _head_kv(h):
    return h // (N_HEADS // N_KV), h % 2


MIX_GROUP = 2


def _interleave(*gens):
    results = [None] * len(gens)
    active = list(enumerate(gens))
    while active:
        still = []
        for i, g in active:
            try:
                next(g)
                still.append((i, g))
            except StopIteration as done:
                results[i] = done.value
        active = still
    return results


def _attn_block_fwd(q_blk, kk, vv, bias_ref, sinks_ref, first_mask):
    kvar = _kv_variants(kk)
    vvar = _kv_variants(vv)
    heads = range(N_HEADS)
    q_pairs = [q_blk[:, (h // 2) * LANES:(h // 2 + 1) * LANES] for h in heads]
    logits = [_dot_nt(q_pairs[h], kvar[_head_kv(h)[0]][_head_kv(h)[1]]) + bias_ref[h] for h in heads]
    if first_mask is not None:
        logits = [jnp.where(first_mask, NEG_INF, lg) for lg in logits]
    yield
    ms = [jnp.maximum(jnp.max(logits[h], axis=-1, keepdims=True), sinks_ref[h]) for h in heads]
    yield
    es = [jnp.exp(logits[h] - ms[h]) for h in heads]
    ess = [jnp.exp(sinks_ref[h] - ms[h]) for h in heads]
    yield
    invs = [1.0 / (jnp.sum(es[h], axis=-1, keepdims=True) + ess[h]) for h in heads]
    probs = [(es[h] * invs[h], ess[h] * invs[h]) for h in heads]
    yield
    outs = [_dot(probs[h][0].astype(BF16), vvar[_head_kv(h)[0]][_head_kv(h)[1]]) for h in heads]
    pairs = [outs[2 * i] + outs[2 * i + 1] for i in range(N_HEADS // 2)]
    return jnp.concatenate(pairs, axis=1), probs, kvar, vvar


def _gmlp_chunk_fwd(gu, gv, ln_g, ln_b, wsm_ref, bsx, amat):
    u, tu = _gelu(gu)
    a, ta = _gelu(gv)
    yield
    mean = _split_dot(a, amat)
    d = a - mean
    yield
    var = _split_dot(d * d, amat)
    yield
    rstd = lax.rsqrt(var + LN_EPS)
    xhat = d * rstd
    vb = (xhat * ln_g + ln_b).astype(BF16)
    yield
    lane = lax.broadcasted_iota(jnp.int32, (BLOCK, LANES), 1)
    low = lane < GROUP_DIM
    cols = []
    for pair in range(N_GROUPS // 2):
        vp = vb[:, pair * LANES:(pair + 1) * LANES]
        cols.append(jnp.where(low, _dot(wsm_ref[2 * pair], vp), _dot(wsm_ref[2 * pair + 1], vp)))
    mixedv = jnp.concatenate(cols, axis=1) + bsx
    return u * mixedv, (u, tu, ta, xhat, rstd, vb, mixedv)


def _rms(a, g):
    r = lax.rsqrt(jnp.mean(a * a, axis=-1, keepdims=True) + LN_EPS)
    return a * r * g, r


def _fwd_mix(q, kv, gu, gv, x, modr, bias, sinks, gln_g, gln_b, wsm, bsx, amat, aog, gog, w_out, ln1_g, ln1_b, tm,
             ffn_shards, ffn_kinds):
    s = x.shape[0]
    nb = tm // BLOCK
    n_steps = s // tm
    fwd_step = (3 * n_steps) // 4
    n_w = len(ffn_shards)

    def body(q_ref, kv_ref, kvp_ref, gu_ref, gv_ref, x_ref, mod_ref, bias_ref, sinks_ref, glng_ref, glnb_ref, wsm_ref,
             bsx_ref, amat_ref, aog_ref, gog_ref, wout_ref, ln1g_ref, ln1b_ref, *rest):
        shard_refs = rest[:n_w]
        x1_ref, y_ref, mixed_ref = rest[n_w:n_w + 3]
        gathered_refs = rest[n_w + 3:2 * n_w + 3]
        mix_scr, send_sems, recv_sems = rest[2 * n_w + 3:]
        i = pl.program_id(0)
        gather = _WeightGather(shard_refs, gathered_refs, ffn_kinds, send_sems, recv_sems)

        @pl.when(i == 0)
        def _():
            gather.start()

        col = lax.broadcasted_iota(jnp.int32, (BLOCK, 2 * BLOCK), 1)
        for b0 in range(0, nb, MIX_GROUP):
            gens = []
            for b in range(b0, min(b0 + MIX_GROUP, nb)):
                r0 = b * BLOCK
                if b == 0:
                    kvprev = kvp_ref[...]
                    first_mask = (col < BLOCK) & (i == 0)
                else:
                    kvprev = kv_ref[r0 - BLOCK:r0, :]
                    first_mask = None
                kvcur = kv_ref[r0:r0 + BLOCK, :]
                kk = jnp.concatenate([kvprev[:, :KV_W], kvcur[:, :KV_W]], axis=0)
                vv = jnp.concatenate([kvprev[:, KV_W:], kvcur[:, KV_W:]], axis=0)
                gens.append(_attn_block_fwd(q_ref[r0:r0 + BLOCK, :], kk, vv, bias_ref, sinks_ref, first_mask))
                gens.append(_gmlp_chunk_fwd(gu_ref[r0:r0 + BLOCK, :], gv_ref[r0:r0 + BLOCK, :], glng_ref[...],
                                            glnb_ref[...], wsm_ref, bsx_ref[...], amat_ref[...]))
            res = _interleave(*gens)
            for k, b in enumerate(range(b0, min(b0 + MIX_GROUP, nb))):
                r0 = b * BLOCK
                na, _ = _rms(res[2 * k][0], aog_ref[...])
                ng, _ = _rms(res[2 * k + 1][0], gog_ref[...])
                mix_scr[r0:r0 + BLOCK, :ATTN_W] = na.astype(BF16)
                mix_scr[r0:r0 + BLOCK, ATTN_W:] = ng.astype(BF16)
        mixed = mix_scr[...]
        mixed_ref[...] = mixed
        y = _dot(mixed, wout_ref[...])
        y_ref[...] = y
        z1 = ALPHA * x_ref[...] + mod_ref[2:3, :] * y
        xhat, _ = _ln_stats(z1)
        x1_ref[...] = xhat * ln1g_ref[...] + ln1b_ref[...]

        @pl.when(i == fwd_step)
        def _():
            gather.forward()

        @pl.when(i == n_steps - 1)
        def _():
            gather.finish()

    row = lambda w: pl.BlockSpec((tm, w), lambda i: (i, 0))
    prev = pl.BlockSpec((BLOCK, 2 * KV_W), lambda i: (jnp.maximum(i * nb - 1, 0), 0))
    outs = pl.pallas_call(
        body, name="fwd_mix",
        out_shape=[jax.ShapeDtypeStruct((s, D_MODEL), F32), jax.ShapeDtypeStruct((s, D_MODEL), F32),
                   jax.ShapeDtypeStruct((s, D_MODEL), BF16)]
        + [jax.ShapeDtypeStruct(_gathered_shape(sh, k), BF16) for sh, k in zip(ffn_shards, ffn_kinds)],
        grid=(n_steps,),
        in_specs=[row(ATTN_W), row(2 * KV_W), prev, row(GMLP_W), row(GMLP_W), row(D_MODEL), _const_spec((8, D_MODEL)),
                  _const_spec((N_HEADS, BLOCK, 2 * BLOCK)), pl.BlockSpec(memory_space=pltpu.SMEM),
                  _const_spec((1, GMLP_W)), _const_spec((1, GMLP_W)), _const_spec((N_GROUPS, BLOCK, BLOCK)),
                  _const_spec((BLOCK, GMLP_W)), _const_spec((GMLP_W, GMLP_W)), _const_spec((1, ATTN_W)),
                  _const_spec((1, GMLP_W)), _const_spec((D_MODEL, D_MODEL)), _const_spec((1, D_MODEL)),
                  _const_spec((1, D_MODEL))] + [ANY] * n_w,
        out_specs=[row(D_MODEL), row(D_MODEL), row(D_MODEL)] + [ANY] * n_w,
        scratch_shapes=[pltpu.VMEM((tm, D_MODEL), BF16)] + _WeightGather.sems(n_w),
        compiler_params=_params(("arbitrary",)),
    )(q, kv, kv, gu, gv, x, modr, bias, sinks, gln_g, gln_b, wsm, bsx, amat, aog, gog, w_out, ln1_g, ln1_b, *ffn_shards)
    return outs[0], outs[1], outs[2], outs[3:]


FF_BLOCKS = N_CHIPS // 2
FF_CHUNK = D_FF // FF_BLOCKS


def _sigmoid(x):
    return 1.0 / (1.0 + jnp.exp(-x))


def _fwd_ffn(x1, target, modr, ln2_g, ln2_b, w_gu, w_dn, tm):
    s = x1.shape[0]

    def body(x1_ref, t_ref, mod_ref, g_ref, b_ref, wgu_ref, wdn_ref, h2_ref, act_ref, dy2_ref, dx1a_ref, acc_ref):
        i = pl.program_id(0)

        @pl.when(i == 0)
        def _():
            acc_ref[...] = jnp.zeros_like(acc_ref)

        x1v = x1_ref[...]
        h2 = (x1v * (1.0 + mod_ref[4:5, :]) + mod_ref[3:4, :]).astype(BF16)
        h2_ref[...] = h2
        y2 = jnp.zeros((tm, D_MODEL), F32)
        for cc in range(D_FF // FF_CHUNK):
            c0 = cc * FF_CHUNK
            gate = _dot(h2, wgu_ref[cc])
            up = _dot(h2, wgu_ref[FF_BLOCKS + cc])
            act_ref[:, c0:c0 + FF_CHUNK] = gate.astype(BF16)
            act_ref[:, D_FF + c0:D_FF + c0 + FF_CHUNK] = up.astype(BF16)
            a = (gate * _sigmoid(gate) * up).astype(BF16)
            y2 = y2 + _dot(a, wdn_ref[c0:c0 + FF_CHUNK, :])
        g2 = mod_ref[5:6, :]
        z2 = ALPHA * x1v + g2 * y2
        xhat, rstd = _ln_stats(z2)
        gain = g_ref[...]
        diff = xhat * gain + b_ref[...] - t_ref[...]
        dx2 = diff * (1.0 / D_MODEL)
        dz2 = _ln_bwd(dx2 * gain, xhat, rstd)
        dx1a_ref[...] = ALPHA * dz2
        dy2_ref[...] = (g2 * dz2).astype(BF16)
        acc_ref[0:1, :] += _colsum(diff * diff)
        acc_ref[1:2, :] += _colsum(dx2 * xhat)
        acc_ref[2:3, :] += _colsum(dx2)
        acc_ref[3:4, :] += _colsum(dz2 * y2)

    row = lambda w: pl.BlockSpec((tm, w), lambda i: (i, 0))
    return pl.pallas_call(
        body, name="fwd_ffn",
        out_shape=(jax.ShapeDtypeStruct((s, D_MODEL), BF16), jax.ShapeDtypeStruct((s, 2 * D_FF), BF16),
                   jax.ShapeDtypeStruct((s, D_MODEL), BF16), jax.ShapeDtypeStruct((s, D_MODEL), F32),
                   jax.ShapeDtypeStruct((8, D_MODEL), F32)),
        grid=(s // tm,),
        in_specs=[row(D_MODEL), row(D_MODEL), _const_spec((8, D_MODEL)), _const_spec((1, D_MODEL)),
                  _const_spec((1, D_MODEL)), _const_spec((N_CHIPS, D_MODEL, FF_CHUNK), single=True),
                  _const_spec((D_FF, D_MODEL), single=True)],
        out_specs=(row(D_MODEL), row(2 * D_FF), row(D_MODEL), row(D_MODEL), _const_spec((8, D_MODEL))),
        compiler_params=_params(("arbitrary",)),
    )(x1, target, modr, ln2_g, ln2_b, w_gu, w_dn)


def _bwd_ffn(dy2, act, w_gu, w_dn, tm):
    s = dy2.shape[0]

    def body(dy2_ref, act_ref, wgu_ref, wdn_ref, a_ref, dgu_ref, dh2_ref):
        dy2v = dy2_ref[...]
        dh2 = jnp.zeros((tm, D_MODEL), F32)
        for cc in range(D_FF // FF_CHUNK):
            c0 = cc * FF_CHUNK
            da = _dot_nt(dy2v, wdn_ref[c0:c0 + FF_CHUNK, :])
            gate = act_ref[:, c0:c0 + FF_CHUNK].astype(F32)
            up = act_ref[:, D_FF + c0:D_FF + c0 + FF_CHUNK].astype(F32)
            sg = _sigmoid(gate)
            sl = gate * sg
            a_ref[:, c0:c0 + FF_CHUNK] = (sl * up).astype(BF16)
            dgate = (da * up * (sg * (1.0 + gate * (1.0 - sg)))).astype(BF16)
            dup = (da * sl).astype(BF16)
            dgu_ref[:, c0:c0 + FF_CHUNK] = dgate
            dgu_ref[:, D_FF + c0:D_FF + c0 + FF_CHUNK] = dup
            dh2 = dh2 + _dot_nt(dgate, wgu_ref[cc])
            dh2 = dh2 + _dot_nt(dup, wgu_ref[FF_BLOCKS + cc])
        dh2_ref[...] = dh2

    row = lambda w: pl.BlockSpec((tm, w), lambda i: (i, 0))
    return pl.pallas_call(
        body, name="bwd_ffn",
        out_shape=(jax.ShapeDtypeStruct((s, D_FF), BF16), jax.ShapeDtypeStruct((s, 2 * D_FF), BF16),
                   jax.ShapeDtypeStruct((s, D_MODEL), F32)),
        grid=(s // tm,),
        in_specs=[row(D_MODEL), row(2 * D_FF), _const_spec((N_CHIPS, D_MODEL, FF_CHUNK), single=True),
                  _const_spec((D_FF, D_MODEL), single=True)],
        out_specs=(row(D_FF), row(2 * D_FF), row(D_MODEL)),
        compiler_params=_params(("parallel",)),
    )(dy2, act, w_gu, w_dn)


def _bwd_mid(dh2, dx1a, x1, x, y, modr, ln1_g, w_out, tm):
    s = x.shape[0]

    def body(dh2_ref, dx1a_ref, x1_ref, x_ref, y_ref, mod_ref, g_ref, wout_ref, dxa_ref, dy_ref, dmix_ref, acc_ref):
        i = pl.program_id(0)

        @pl.when(i == 0)
        def _():
            acc_ref[...] = jnp.zeros_like(acc_ref)

        dh2 = dh2_ref[...]
        x1v = x1_ref[...]
        yv = y_ref[...]
        g1 = mod_ref[2:3, :]
        dx1 = dx1a_ref[...] + dh2 * (1.0 + mod_ref[4:5, :])
        z1 = ALPHA * x_ref[...] + g1 * yv
        xhat, rstd = _ln_stats(z1)
        dz1 = _ln_bwd(dx1 * g_ref[...], xhat, rstd)
        dxa_ref[...] = ALPHA * dz1
        dy = (g1 * dz1).astype(BF16)
        dy_ref[...] = dy
        dmix_ref[...] = _dot_nt(dy, wout_ref[...])
        acc_ref[0:1, :] += _colsum(dh2 * x1v)
        acc_ref[1:2, :] += _colsum(dh2)
        acc_ref[2:3, :] += _colsum(dx1 * xhat)
        acc_ref[3:4, :] += _colsum(dx1)
        acc_ref[4:5, :] += _colsum(dz1 * yv)

    row = lambda w: pl.BlockSpec((tm, w), lambda i: (i, 0))
    return pl.pallas_call(
        body, name="bwd_mid",
        out_shape=(jax.ShapeDtypeStruct((s, D_MODEL), F32), jax.ShapeDtypeStruct((s, D_MODEL), BF16),
                   jax.ShapeDtypeStruct((s, D_MODEL), F32), jax.ShapeDtypeStruct((8, D_MODEL), F32)),
        grid=(s // tm,),
        in_specs=[row(D_MODEL)] * 5 + [_const_spec((8, D_MODEL)), _const_spec((1, D_MODEL)),
                                       _const_spec((D_MODEL, D_MODEL))],
        out_specs=(row(D_MODEL), row(D_MODEL), row(D_MODEL), _const_spec((8, D_MODEL))),
        compiler_params=_params(("arbitrary",)),
    )(dh2, dx1a, x1, x, y, modr, ln1_g, w_out)


def _fold_kv(t0, t1):
    lane = lax.broadcasted_iota(jnp.int32, t0.shape, 1)
    f0 = t0 + pltpu.roll(t0, HEAD_DIM, 1)
    f1 = t1 + pltpu.roll(t1, HEAD_DIM, 1)
    return jnp.where(lane < HEAD_DIM, f0, f1)


def _bwd_mix(q, kv, gu, gv, dmix, bias, sinks, gln_g, gln_b, wsm, bsx, amat, aog, gog, grad_parts, grad_kinds):
    s = q.shape[0]
    nblk = s // BLOCK
    n_g = len(grad_parts)

    def body(q_ref, kv_ref, kvp_ref, gu_ref, gv_ref, dmix_ref, bias_ref, sinks_ref, glng_ref, glnb_ref, wsm_ref,
             bsx_ref, amat_ref, aog_ref, gog_ref, *rest):
        part_refs = rest[:n_g]
        dq_ref, dkv_ref, dgu_ref, dgv_ref, gbias_ref, dws_ref, dbs_ref, vec_ref, dsink_ref = rest[n_g:n_g + 9]
        rx_refs = rest[n_g + 9:2 * n_g + 9]
        carry, send_sems, recv_sems = rest[2 * n_g + 9:]
        n = pl.program_id(0)
        exchange = _ChipExchange(part_refs, rx_refs, grad_kinds, send_sems, recv_sems)

        @pl.when(n == 0)
        def _():
            exchange.start()
            carry[...] = jnp.zeros_like(carry)
            gbias_ref[...] = jnp.zeros_like(gbias_ref)
            dws_ref[...] = jnp.zeros_like(dws_ref)
            dbs_ref[...] = jnp.zeros_like(dbs_ref)
            vec_ref[...] = jnp.zeros_like(vec_ref)
            dsink_ref[...] = jnp.zeros_like(dsink_ref)

        @pl.when(n == nblk)
        def _():
            dkv_ref[...] = carry[...].astype(BF16)
            exchange.wait()

        @pl.when(n < nblk)
        def _():
            col = lax.broadcasted_iota(jnp.int32, (BLOCK, 2 * BLOCK), 1)
            lane = lax.broadcasted_iota(jnp.int32, (BLOCK, LANES), 1)
            low = lane < HEAD_DIM
            first_mask = (col < BLOCK) & (n == 0)
            kvprev = kvp_ref[...]
            kvcur = kv_ref[...]
            kk = jnp.concatenate([kvprev[:, :KV_W], kvcur[:, :KV_W]], axis=0)
            vv = jnp.concatenate([kvprev[:, KV_W:], kvcur[:, KV_W:]], axis=0)
            q_blk = q_ref[...]
            (attn, probs, kvar, vvar), (gm, (u, tu, ta, xhat, rstd, vb, mixedv)) = _interleave(
                _attn_block_fwd(q_blk, kk, vv, bias_ref, sinks_ref, first_mask),
                _gmlp_chunk_fwd(gu_ref[...], gv_ref[...], glng_ref[...], glnb_ref[...], wsm_ref, bsx_ref[...],
                                amat_ref[...]))
            na_unit, r_a = _rms(attn, 1.0)
            ng_unit, r_g = _rms(gm, 1.0)

            dmix = dmix_ref[...]
            dn_a = dmix[:, :ATTN_W]
            dn_g = dmix[:, ATTN_W:]
            vec_ref[0:1, :] += _colsum(dn_a * na_unit)
            vec_ref[1:2, :] += _colsum(dn_g * ng_unit)
            t_a = dn_a * aog_ref[...]
            d_attn = r_a * t_a - na_unit * (r_a * jnp.mean(t_a * na_unit, axis=-1, keepdims=True))
            t_g = dn_g * gog_ref[...]
            d_gm = r_g * t_g - ng_unit * (r_g * jnp.mean(t_g * ng_unit, axis=-1, keepdims=True))

            def gating_bwd():
                dgu_ref[...] = (d_gm * mixedv * _gelu_grad(gu_ref[...], tu)).astype(BF16)
                dmx = d_gm * u
                dbs_ref[...] += dmx
                dmxb = dmx.astype(BF16)
                yield
                dvn_cols = []
                for pair in range(N_GROUPS // 2):
                    dp_ = dmxb[:, pair * LANES:(pair + 1) * LANES]
                    vp = vb[:, pair * LANES:(pair + 1) * LANES]
                    dvn_cols.append(
                        jnp.where(low, _dot_tn(wsm_ref[2 * pair], dp_), _dot_tn(wsm_ref[2 * pair + 1], dp_)))
                    zero = jnp.zeros_like(dp_)
                    dws_ref[2 * pair] += _dot_nt(jnp.where(low, dp_, zero), vp)
                    dws_ref[2 * pair + 1] += _dot_nt(jnp.where(low, zero, dp_), vp)
                dvn = jnp.concatenate(dvn_cols, axis=1)
                yield
                vec_ref[2:3, :] += _colsum(dvn * xhat)
                vec_ref[3:4, :] += _colsum(dvn)
                dxh = dvn * glng_ref[...]
                am = amat_ref[...]
                m1 = _split_dot(dxh, am)
                m2 = _split_dot(dxh * xhat, am)
                yield
                da = rstd * (dxh - m1 - xhat * m2)
                dgv_ref[...] = (da * _gelu_grad(gv_ref[...], ta)).astype(BF16)

            def attention_bwd():
                heads = range(N_HEADS)
                sels = [low if h % 2 == 0 else jnp.logical_not(low) for h in heads]
                pair_of = lambda a, h: a[:, (h // 2) * LANES:(h // 2 + 1) * LANES]
                do_hs = [jnp.where(sels[h], pair_of(d_attn, h), 0.0).astype(BF16) for h in heads]
                q_hs = [jnp.where(sels[h], pair_of(q_blk, h), jnp.zeros((BLOCK, LANES), BF16)) for h in heads]
                dps = [_dot_nt(do_hs[h], vvar[_head_kv(h)[0]][_head_kv(h)[1]]) for h in heads]
                yield
                deltas = [jnp.sum(probs[h][0] * dps[h], axis=-1, keepdims=True) for h in heads]
                yield
                dss = [probs[h][0] * (dps[h] - deltas[h]) for h in heads]
                for h in heads:
                    dsink_ref[h] += -(probs[h][1] * deltas[h])
                    gbias_ref[h] += dss[h]
                dsbs = [ds.astype(BF16) for ds in dss]
                pbs = [probs[h][0].astype(BF16) for h in heads]
                yield
                dqs = [_dot(dsbs[h], kvar[_head_kv(h)[0]][_head_kv(h)[1]]) for h in heads]
                tks = [_dot_tn(dsbs[h], q_hs[h]) for h in heads]
                tvs = [_dot_tn(pbs[h], do_hs[h]) for h in heads]
                dq_cols = [dqs[2 * i] + dqs[2 * i + 1] for i in range(N_HEADS // 2)]
                dq_ref[...] = (jnp.concatenate(dq_cols, axis=1) * Q_SCALE).astype(BF16)
                per_kv = N_HEADS // N_KV
                kv_sum = lambda ts, kvh: sum(ts[kvh * per_kv + 1:(kvh + 1) * per_kv], ts[kvh * per_kv])
                dkk = _fold_kv(kv_sum(tks, 0), kv_sum(tks, 1))
                dvv = _fold_kv(kv_sum(tvs, 0), kv_sum(tvs, 1))
                dkv_ref[...] = (carry[...] + jnp.concatenate([dkk[:BLOCK], dvv[:BLOCK]], axis=1)).astype(BF16)
                carry[...] = jnp.concatenate([dkk[BLOCK:], dvv[BLOCK:]], axis=1)

            _interleave(attention_bwd(), gating_bwd())

    last = nblk - 1
    cur = lambda w: pl.BlockSpec((BLOCK, w), lambda n: (jnp.minimum(n, last), 0))
    prev = lambda w: pl.BlockSpec((BLOCK, w), lambda n: (jnp.clip(n - 1, 0, last), 0))
    outs = pl.pallas_call(
        body, name="bwd_mix",
        out_shape=[jax.ShapeDtypeStruct((s, ATTN_W), BF16), jax.ShapeDtypeStruct((s, 2 * KV_W), BF16),
                   jax.ShapeDtypeStruct((s, GMLP_W), BF16), jax.ShapeDtypeStruct((s, GMLP_W), BF16),
                   jax.ShapeDtypeStruct((N_HEADS, BLOCK, 2 * BLOCK), F32),
                   jax.ShapeDtypeStruct((N_GROUPS, BLOCK, BLOCK), F32),
                   jax.ShapeDtypeStruct((BLOCK, GMLP_W), F32), jax.ShapeDtypeStruct((8, GMLP_W), F32),
                   jax.ShapeDtypeStruct((N_HEADS, BLOCK, 1), F32)]
        + [jax.ShapeDtypeStruct(_rx_shape(p.shape, k), BF16) for p, k in zip(grad_parts, grad_kinds)],
        grid=(nblk + 1,),
        in_specs=[cur(ATTN_W), cur(2 * KV_W), prev(2 * KV_W), cur(GMLP_W), cur(GMLP_W), cur(D_MODEL),
                  _const_spec((N_HEADS, BLOCK, 2 * BLOCK)), pl.BlockSpec(memory_space=pltpu.SMEM),
                  _const_spec((1, GMLP_W)), _const_spec((1, GMLP_W)), _const_spec((N_GROUPS, BLOCK, BLOCK)),
                  _const_spec((BLOCK, GMLP_W)), _const_spec((GMLP_W, GMLP_W)), _const_spec((1, ATTN_W)),
                  _const_spec((1, GMLP_W))] + [ANY] * n_g,
        out_specs=[cur(ATTN_W), prev(2 * KV_W), cur(GMLP_W), cur(GMLP_W),
                   _const_spec((N_HEADS, BLOCK, 2 * BLOCK)), _const_spec((N_GROUPS, BLOCK, BLOCK)),
                   _const_spec((BLOCK, GMLP_W)), _const_spec((8, GMLP_W)), _const_spec((N_HEADS, BLOCK, 1))]
        + [ANY] * n_g,
        scratch_shapes=[pltpu.VMEM((BLOCK, 2 * KV_W), F32)] + _ChipExchange.sems(n_g),
        compiler_params=_params(("arbitrary",)),
    )(q, kv, kv, gu, gv, dmix, bias, sinks, gln_g, gln_b, wsm, bsx, amat, aog, gog, *grad_parts)
    return outs[:9], outs[9:]


def _mix_finalize(gbias, bucket, dws, dbs, dsink):
    def body(gb_ref, bucket_ref, dws_ref, dbs_ref, dsink_ref, tall_ref):
        bk = bucket_ref[...]
        lane = lax.broadcasted_iota(jnp.int32, (N_BUCKETS, LANES), 1)
        rowi = lax.broadcasted_iota(jnp.int32, (N_BUCKETS, LANES), 0)
        drb = jnp.zeros((N_BUCKETS, LANES), F32)
        dsk = jnp.zeros((8, LANES), F32)
        lane8 = lax.broadcasted_iota(jnp.int32, (8, LANES), 1)
        for h in range(N_HEADS):
            g = gb_ref[h]
            for b in range(N_BUCKETS):
                tot = jnp.sum(_colsum(jnp.where(bk == float(b), g, 0.0)), axis=1, keepdims=True)
                drb = jnp.where((lane == h) & (rowi == b), tot, drb)
            sk = jnp.sum(dsink_ref[h], axis=0, keepdims=True)
            dsk = jnp.where(lane8 == h, sk, dsk)
        tall_ref[TALL_RB:TALL_RB + N_BUCKETS, :] = drb
        tall_ref[TALL_SK:TALL_SK + 8, :] = dsk
        ti = lax.broadcasted_iota(jnp.int32, (BLOCK, BLOCK), 0)
        ui = lax.broadcasted_iota(jnp.int32, (BLOCK, BLOCK), 1)
        for g in range(N_GROUPS):
            tall_ref[g * BLOCK:(g + 1) * BLOCK, :] = jnp.where(ti >= ui, dws_ref[g], 0.0)
        gi = lax.broadcasted_iota(jnp.int32, (GMLP_W, LANES), 0) // GROUP_DIM
        li = lax.broadcasted_iota(jnp.int32, (GMLP_W, LANES), 1)
        ind = jnp.where(gi == li, 1.0, 0.0).astype(BF16)
        d = dbs_ref[...]
        hi = d.astype(BF16)
        r1 = d - hi.astype(F32)
        mid = r1.astype(BF16)
        lo = (r1 - mid.astype(F32)).astype(BF16)
        dbsg = _dot(hi, ind) + _dot(mid, ind) + _dot(lo, ind)
        tall_ref[TALL_BS:TALL_BS + N_GROUPS, :] = dbsg.T[:N_GROUPS, :]

    return pl.pallas_call(
        body, name="mix_finalize", out_shape=jax.ShapeDtypeStruct((TALL_ROWS, LANES), F32), grid=(1,),
        in_specs=[_const_spec((N_HEADS, BLOCK, 2 * BLOCK)), _const_spec((BLOCK, 2 * BLOCK)),
                  _const_spec((N_GROUPS, BLOCK, BLOCK)), _const_spec((BLOCK, GMLP_W)),
                  _const_spec((N_HEADS, BLOCK, 1))],
        out_specs=_const_spec((TALL_ROWS, LANES)),
        compiler_params=_params(("arbitrary",)),
    )(gbias, bucket, dws, dbs, dsink)


def _bwd_in(dq, dkv, dgu, dgv, dxa, x, modr, w_in, tm):
    s = x.shape[0]

    def body(dq_ref, dkv_ref, dgu_ref, dgv_ref, dxa_ref, x_ref, mod_ref, w_ref, gx_ref, acc_ref, db_ref):
        @pl.when(pl.program_id(0) == 0)
        def _():
            acc_ref[...] = jnp.zeros_like(acc_ref)
            db_ref[...] = jnp.zeros_like(db_ref)

        dproj = jnp.concatenate([dq_ref[...], dkv_ref[...], dgu_ref[...], dgv_ref[...]], axis=1)
        dh1 = _dot_nt(dproj, w_ref[...])
        gx_ref[...] = dxa_ref[...] + dh1 * (1.0 + mod_ref[1:2, :])
        acc_ref[0:1, :] += _colsum(dh1 * x_ref[...])
        acc_ref[1:2, :] += _colsum(dh1)
        db_ref[0:1, :] += _colsum(dproj.astype(F32))

    row = lambda w: pl.BlockSpec((tm, w), lambda i: (i, 0))
    return pl.pallas_call(
        body, name="bwd_in",
        out_shape=(jax.ShapeDtypeStruct((s, D_MODEL), F32), jax.ShapeDtypeStruct((8, D_MODEL), F32),
                   jax.ShapeDtypeStruct((8, IN_W), F32)),
        grid=(s // tm,),
        in_specs=[row(ATTN_W), row(2 * KV_W), row(GMLP_W), row(GMLP_W), row(D_MODEL), row(D_MODEL),
                  _const_spec((8, D_MODEL)), _const_spec((D_MODEL, IN_W))],
        out_specs=(row(D_MODEL), _const_spec((8, D_MODEL)), _const_spec((8, IN_W))),
        compiler_params=_params(("arbitrary",)),
    )(dq, dkv, dgu, dgv, dxa, x, modr, w_in)


def _wgrad(a, bs, tm, tk, name, owner_blocks=False):
    k_all, m = a.shape
    n = sum(b.shape[1] for b in bs)
    nk = k_all // tk
    n_b = len(bs)
    wb = n // N_CHIPS

    def body(a_ref, *rest):
        b_refs, (o_ref, ob_ref) = rest[:n_b], rest[n_b:]
        k = pl.program_id(1)

        @pl.when(k == 0)
        def _():
            o_ref[...] = jnp.zeros_like(o_ref)

        b = b_refs[0][...] if n_b == 1 else jnp.concatenate([r[...] for r in b_refs], axis=1)
        if owner_blocks:
            av = a_ref[...]
            for j in range(N_CHIPS):
                o_ref[j] += _dot_tn(av, b[:, j * wb:(j + 1) * wb])
        else:
            o_ref[...] += _dot_tn(a_ref[...], b)

        @pl.when(k == nk - 1)
        def _():
            ob_ref[...] = o_ref[...].astype(BF16)

    if owner_blocks:
        out_spec = pl.BlockSpec((N_CHIPS, tm, wb), lambda i, k: (0, i, 0))
        shape = (N_CHIPS, m, wb)
    else:
        out_spec = pl.BlockSpec((tm, n), lambda i, k: (i, 0))
        shape = (m, n)
    return pl.pallas_call(
        body, name=name, out_shape=(jax.ShapeDtypeStruct(shape, F32), jax.ShapeDtypeStruct(shape, BF16)),
        grid=(m // tm, nk),
        in_specs=[pl.BlockSpec((tk, tm), lambda i, k: (k, i))]
        + [pl.BlockSpec((tk, b.shape[1]), lambda i, k: (k, 0)) for b in bs],
        out_specs=(out_spec, out_spec),
        compiler_params=_params(("parallel", "arbitrary")),
    )(a, *bs)


def _adam_math(w, g, m, v):
    m2 = ADAM_B1 * m + (1.0 - ADAM_B1) * g
    v2 = ADAM_B2 * v + (1.0 - ADAM_B2) * (g * g)
    m_hat = m2 / (1.0 - ADAM_B1 ** ADAM_STEP)
    v_hat = v2 / (1.0 - ADAM_B2 ** ADAM_STEP)
    delta = -ADAM_LR * (m_hat / (jnp.sqrt(v_hat) + ADAM_EPS) + ADAM_WD * w)
    return delta, m2, v2


def _adam_halves(w, mine, got, m, v, tr, name):
    r, cc = w.shape
    h = r // 2
    nt = h // tr

    def body(c_ref, w_ref, mine_ref, got_ref, m_ref, v_ref, g_ref, d_ref, m2_ref, v2_ref):
        g = jnp.where(pl.program_id(0) == c_ref[0], mine_ref[...], got_ref[...])
        g_ref[...] = g
        d, m2, v2 = _adam_math(w_ref[...], g, m_ref[...], v_ref[...])
        d_ref[...] = d
        m2_ref[...] = m2
        v2_ref[...] = v2

    full = pl.BlockSpec((tr, cc), lambda hh, i, c_ref: (hh * nt + i, 0))
    half = pl.BlockSpec((tr, cc), lambda hh, i, c_ref: (i, 0))
    shp = jax.ShapeDtypeStruct((r, cc), F32)
    return pl.pallas_call(
        body, name=name, out_shape=(shp, shp, shp, shp),
        grid_spec=pltpu.PrefetchScalarGridSpec(
            num_scalar_prefetch=1, grid=(2, nt), in_specs=[full, half, half, full, full],
            out_specs=(full, full, full, full)),
        compiler_params=_params(("arbitrary", "arbitrary")),
    )(_core_index_scalar(), w, mine, got, m, v)


def _adam_w_ada(sc_t, dmod_cols, w, m, v, tr):
    r, cc = w.shape

    def body(sct_ref, dm_ref, w_ref, m_ref, v_ref, g_ref, d_ref, m2_ref, v2_ref):
        g = sct_ref[:, 0:1] * dm_ref[0:1, :]
        for k in range(1, N_DEV):
            g = g + sct_ref[:, k:k + 1] * dm_ref[k:k + 1, :]
        g_ref[...] = g
        d, m2, v2 = _adam_math(w_ref[...], g, m_ref[...], v_ref[...])
        d_ref[...] = d
        m2_ref[...] = m2
        v2_ref[...] = v2

    spec = pl.BlockSpec((tr, cc), lambda i: (i, 0))
    shp = jax.ShapeDtypeStruct((r, cc), F32)
    return pl.pallas_call(
        body, name="adam_w_ada", out_shape=(shp, shp, shp, shp), grid=(r // tr,),
        in_specs=[pl.BlockSpec((tr, N_DEV), lambda i: (i, 0)), _const_spec((N_DEV, cc)), spec, spec, spec],
        out_specs=(spec, spec, spec, spec), compiler_params=_params(("parallel",)),
    )(sc_t, dmod_cols, w, m, v)


def _pack_wide(acc_i, acc_m, acc_f, db_in, vec):
    arrs = [acc_i, acc_m, acc_f, db_in, vec]
    i_, m_, f_, b_, v_ = range(5)
    src = {"b_in": (b_, 0), "ln1_g": (m_, 2), "ln1_b": (m_, 3), "ln2_g": (f_, 1), "ln2_b": (f_, 2),
           "gmlp_ln_g": (v_, 2), "gmlp_ln_b": (v_, 3), "attn_out_g": (v_, 0), "gmlp_out_g": (v_, 1), "loss": (f_, 0)}
    dmod = [(i_, 1), (i_, 0), (m_, 4), (m_, 1), (m_, 0), (f_, 3)]

    def body(*refs):
        ins, wide_ref = refs[:5], refs[5]
        wide_ref[...] = jnp.zeros_like(wide_ref)
        for k, (a, row) in enumerate(dmod):
            wide_ref[0:1, k * D_MODEL:(k + 1) * D_MODEL] = ins[a][row:row + 1, :]
        for name, (a, row) in src.items():
            r, off, n = WIDE_LAYOUT[name]
            wide_ref[r:r + 1, off:off + n] = ins[a][row:row + 1, :]

    return pl.pallas_call(
        body, name="pack_wide", out_shape=jax.ShapeDtypeStruct((8, WIDE_W), F32), grid=(1,),
        in_specs=[_const_spec(a.shape) for a in arrs], out_specs=_const_spec((8, WIDE_W)),
        compiler_params=_params(("arbitrary",)),
    )(*arrs)


def _adam_small(gw, gt, wide_wmv, w_s, b_s, rel_bias, sinks):
    names = list(WIDE_PARAMS)
    tall = [("gmlp_w_s", w_s), ("gmlp_b_s", b_s), ("rel_bias", rel_bias), ("attn_sinks", sinks)]
    ins = [gw, gt]
    for n in names:
        ins += list(wide_wmv[n])
    for _, t in tall:
        ins += list(t)
    n_in = len(ins)

    def body(*refs):
        gw_ref, gt_ref = refs[0], refs[1]
        wmv = refs[2:n_in]
        dmod_ref, loss_ref = refs[n_in], refs[n_in + 1]
        outs = refs[n_in + 2:]

        def tall_sum(r0, nr):
            g = gt_ref[r0:r0 + nr, :]
            for d in range(1, N_DEV):
                g = g + gt_ref[d * TALL_ROWS + r0:d * TALL_ROWS + r0 + nr, :]
            return g

        def emit(k, g, w_ref, m_ref, v_ref):
            d, m2, v2 = _adam_math(w_ref[...], g, m_ref[...], v_ref[...])
            outs[4 * k][...] = g
            outs[4 * k + 1][...] = d
            outs[4 * k + 2][...] = m2
            outs[4 * k + 3][...] = v2

        gsum = gw_ref[0:8, :]
        for d in range(1, N_DEV):
            gsum = gsum + gw_ref[8 * d:8 * d + 8, :]
        for d in range(N_DEV):
            dmod_ref[d:d + 1, :] = gw_ref[8 * d:8 * d + 1, :]
        for k, n in enumerate(names):
            r, off, sz = WIDE_LAYOUT[n]
            emit(k, gsum[r:r + 1, off:off + sz], *wmv[3 * k:3 * k + 3])
        r, off, sz = WIDE_LAYOUT["loss"]
        tot = jnp.sum(gsum[r:r + 1, off:off + sz], axis=1, keepdims=True)
        loss_ref[...] = jnp.broadcast_to(tot * (0.5 / D_MODEL), loss_ref.shape)

        k0 = len(names)
        ws_refs = wmv[3 * k0:3 * k0 + 3]
        for g in range(N_GROUPS):
            rows = slice(g * BLOCK, (g + 1) * BLOCK)
            gg = tall_sum(g * BLOCK, BLOCK)
            d, m2, v2 = _adam_math(ws_refs[0][rows, :], gg, ws_refs[1][rows, :], ws_refs[2][rows, :])
            outs[4 * k0][rows, :] = gg
            outs[4 * k0 + 1][rows, :] = d
            outs[4 * k0 + 2][rows, :] = m2
            outs[4 * k0 + 3][rows, :] = v2
        emit(k0 + 1, tall_sum(TALL_BS, N_GROUPS), *wmv[3 * (k0 + 1):3 * (k0 + 1) + 3])
        emit(k0 + 2, tall_sum(TALL_RB, N_BUCKETS)[:, :N_HEADS], *wmv[3 * (k0 + 2):3 * (k0 + 2) + 3])
        emit(k0 + 3, tall_sum(TALL_SK, 8)[0:1, :N_HEADS], *wmv[3 * (k0 + 3):3 * (k0 + 3) + 3])

    out_shapes = [jax.ShapeDtypeStruct((N_DEV, WIDE_W), F32), jax.ShapeDtypeStruct((8, LANES), F32)]
    for n in names:
        out_shapes += [jax.ShapeDtypeStruct(wide_wmv[n][0].shape, F32)] * 4
    for _, t in tall:
        out_shapes += [jax.ShapeDtypeStruct(t[0].shape, F32)] * 4
    res = pl.pallas_call(
        body, name="adam_small", out_shape=out_shapes, grid=(1,),
        in_specs=[_const_spec(a.shape) for a in ins], out_specs=[_const_spec(o.shape) for o in out_shapes],
        compiler_params=_params(("arbitrary",)),
    )(*ins)
    out = {}
    for k, n in enumerate(names + [t[0] for t in tall]):
        out[n] = tuple(res[2 + 4 * k:6 + 4 * k])
    return res[0], res[1], out


def kernel(x, c, rel_bias, w_ada, b_ada, w_in, b_in, attn_sinks, gmlp_ln_g, gmlp_ln_b, gmlp_w_s, gmlp_b_s, attn_out_g, gmlp_out_g, w_out, ln1_g, ln1_b, w_gate_up, w_down, ln2_g, ln2_b, loss_target, m_rel_bias, m_w_ada, m_b_ada, m_w_in, m_b_in, m_attn_sinks, m_gmlp_ln_g, m_gmlp_ln_b, m_gmlp_w_s, m_gmlp_b_s, m_attn_out_g, m_gmlp_out_g, m_w_out, m_ln1_g, m_ln1_b, m_w_gate_up, m_w_down, m_ln2_g, m_ln2_b, v_rel_bias, v_w_ada, v_b_ada, v_w_in, v_b_in, v_attn_sinks, v_gmlp_ln_g, v_gmlp_ln_b, v_gmlp_w_s, v_gmlp_b_s, v_attn_out_g, v_gmlp_out_g, v_w_out, v_ln1_g, v_ln1_b, v_w_gate_up, v_w_down, v_ln2_g, v_ln2_b):
    ix, iy, ic = _my_pos()
    chip = 2 * ix + iy
    dev = 4 * ix + 2 * iy + ic
    s = x.shape[1]
    xs = x[0]
    tgt = loss_target[0]
    tm_big = min(512, s)
    tm_ffn = min(256, s)
    n_ada, n_in, n_gu = w_ada.shape[2], w_in.shape[2], w_gate_up.shape[2]

    (c_rows,), _ = _allgather8([jnp.pad(c, ((0, 7), (0, 0)))], "gather_c")
    c_all = c_rows.reshape(N_DEV, 8, D_MODEL)[:, 0, :]
    sc_all, mod_cols = _mod_part(c_all, w_ada[0], lax.dynamic_slice_in_dim(b_ada, chip * n_ada, n_ada, axis=1))
    (mod_rows,), _ = _allgather8([mod_cols], "gather_mod")
    mod_all = mod_rows.reshape(N_DEV, N_DEV, -1)
    mod_row = lax.dynamic_index_in_dim(mod_all[0::2], dev, axis=1, keepdims=False)
    modr = jnp.pad(mod_row.reshape(6, D_MODEL), ((0, 2), (0, 0)))

    w_in_s, w_out_s = w_in[0].astype(BF16), w_out[0].astype(BF16)
    w_gu_s, w_dn_s = w_gate_up[0].astype(BF16), w_down[0].astype(BF16)
    (w_in_g,) = _gather_weights([w_in_s], ["blk"], "gather_w_in")
    w_in_g = _insert_own(w_in_g, w_in_s, "blk", chip)
    w_in_f = jnp.transpose(w_in_g, (1, 0, 2)).reshape(D_MODEL, IN_W)

    bucket = _bucket_table()
    bias, wsm = _prep_tables(bucket, rel_bias, gmlp_w_s[0])
    bsx = jnp.repeat(gmlp_b_s[0].T, GROUP_DIM, axis=1)
    amat = _group_mean_matrix()
    sinks = attn_sinks[0]

    (h1, q, kv, gu, gv), (w_out_g,) = _fwd_in(xs, modr, w_in_f, b_in, tm_big, [w_out_s], ["blk"])
    w_out_f = _insert_own(w_out_g, w_out_s, "blk", chip).reshape(D_MODEL, D_MODEL)
    x1, y, mixed, (w_gu_g, w_dn_g) = _fwd_mix(
        q, kv, gu, gv, xs, modr, bias, sinks, gmlp_ln_g, gmlp_ln_b, wsm, bsx, amat, attn_out_g, gmlp_out_g, w_out_f,
        ln1_g, ln1_b, tm_big, [w_gu_s, w_dn_s], ["blk", "blk"])
    assert n_gu == FF_CHUNK
    w_gu_f = _insert_own(w_gu_g, w_gu_s, "blk", chip)
    w_dn_f = _insert_own(w_dn_g, w_dn_s, "blk", chip).reshape(D_FF, D_MODEL)
    h2, act, dy2, dx1a, acc_f = _fwd_ffn(x1, tgt, modr, ln2_g, ln2_b, w_gu_f, w_dn_f, tm_ffn)

    a_act, dgu_ff, dh2 = _bwd_ffn(dy2, act, w_gu_f, w_dn_f, tm_ffn)
    g_dn, g_dn_b = _wgrad(a_act, [dy2], D_FF // 2, min(512, s), "wgrad_down")
    g_gu, g_gu_b = _wgrad(h2, [dgu_ff], 512, min(256, s), "wgrad_gate_up")
    dxa, dy, dmix, acc_m = _bwd_mid(dh2, dx1a, x1, xs, y, modr, ln1_g, w_out_f, tm_big)
    g_out, g_out_b = _wgrad(mixed, [dy], 512, min(512, s), "wgrad_out")
    blk3 = lambda a, rows: a.reshape(N_CHIPS, rows, a.shape[1])
    kinds_a = ["blk", "cols", "blk"]
    fulls_a = [blk3(g_dn, D_FF // N_CHIPS), g_gu, blk3(g_out, D_MODEL // N_CHIPS)]
    fulls_a_b = [blk3(g_dn_b, D_FF // N_CHIPS), g_gu_b, blk3(g_out_b, D_MODEL // N_CHIPS)]
    gots_a = _swap_halves(fulls_a_b, kinds_a, "rs_swap_a")
    parts_a = [_add_halves(f, g, k, "rs_add_a%d" % i) for i, (f, g, k) in enumerate(zip(fulls_a, gots_a, kinds_a))]
    (dq, dkv, dgu, dgv, gbias, dws, dbs, vec, dsink), rxs_a = _bwd_mix(
        q, kv, gu, gv, dmix, bias, sinks, gmlp_ln_g, gmlp_ln_b, wsm, bsx, amat, attn_out_g, gmlp_out_g,
        [p[1] for p in parts_a], kinds_a)
    tall_g = _mix_finalize(gbias, bucket, dws, dbs, dsink)
    full_in, full_in_b = _wgrad(h1, [dq, dkv, dgu, dgv], 512, min(512, s), "wgrad_in", owner_blocks=True)
    (got_in,) = _swap_halves([full_in_b], ["blk"], "rs_swap_b")
    part_in = _add_halves(full_in, got_in, "blk", "rs_add_b")
    grad_x, acc_i, db_in = _bwd_in(dq, dkv, dgu, dgv, dxa, xs, modr, w_in_f, tm_big)

    wide_g = _pack_wide(acc_i, acc_m, acc_f, db_in, vec)
    (gw, gt), (rx_in,) = _allgather8([wide_g, tall_g], "gather_small", [part_in[1]], ["blk"])
    wide_wmv = {"b_ada": (b_ada, m_b_ada, v_b_ada), "b_in": (b_in, m_b_in, v_b_in),
                "ln1_g": (ln1_g, m_ln1_g, v_ln1_g), "ln1_b": (ln1_b, m_ln1_b, v_ln1_b),
                "ln2_g": (ln2_g, m_ln2_g, v_ln2_g), "ln2_b": (ln2_b, m_ln2_b, v_ln2_b),
                "gmlp_ln_g": (gmlp_ln_g, m_gmlp_ln_g, v_gmlp_ln_g), "gmlp_ln_b": (gmlp_ln_b, m_gmlp_ln_b, v_gmlp_ln_b),
                "attn_out_g": (attn_out_g, m_attn_out_g, v_attn_out_g),
                "gmlp_out_g": (gmlp_out_g, m_gmlp_out_g, v_gmlp_out_g)}
    rows2 = lambda a: a.reshape(-1, a.shape[-1])
    dmod_all, loss_t, small = _adam_small(
        gw, gt, wide_wmv, tuple(rows2(a) for a in (gmlp_w_s, m_gmlp_w_s, v_gmlp_w_s)),
        tuple(rows2(a) for a in (gmlp_b_s, m_gmlp_b_s, v_gmlp_b_s)), (rel_bias, m_rel_bias, v_rel_bias),
        (attn_sinks, m_attn_sinks, v_attn_sinks))
    loss = loss_t[0, 0]

    dmod_cols = lax.dynamic_slice_in_dim(dmod_all, chip * n_ada, n_ada, axis=1)
    g_ada, d_ada, m_ada, v_ada = _adam_w_ada(sc_all.T, dmod_cols, w_ada[0], m_w_ada[0], v_w_ada[0], 256)

    sums = [(parts_a[0][0], rxs_a[0], "blk", 176), (parts_a[1][0], rxs_a[1], "cols", 256),
            (parts_a[2][0], rxs_a[2], "blk", 128), (part_in[0], rx_in, "blk", 256)]
    mine = [_sum_chips(p, rx, k, tr, "rs_sum_%d" % i) for i, (p, rx, k, tr) in enumerate(sums)]
    got = _share_halves(mine, "rs_share")

    gs_dn, d_dn, m_dn, v_dn = _adam_halves(w_down[0], mine[0], got[0], m_w_down[0], v_w_down[0], 176, "adam_w_down")
    gs_gu, d_gu, m_gu, v_gu = _adam_halves(w_gate_up[0], mine[1], got[1], m_w_gate_up[0], v_w_gate_up[0], 256,
                                           "adam_w_gate_up")
    gs_out, d_out, m_out, v_out = _adam_halves(w_out[0], mine[2], got[2], m_w_out[0], v_w_out[0], 128, "adam_w_out")
    gs_in, d_in, m_in, v_in = _adam_halves(w_in[0], mine[3], got[3], m_w_in[0], v_w_in[0], 256, "adam_w_in")

    big = {"w_ada": (g_ada, d_ada, m_ada, v_ada), "w_in": (gs_in, d_in, m_in, v_in), "w_out": (gs_out, d_out, m_out, v_out),
           "w_gate_up": (gs_gu, d_gu, m_gu, v_gu), "w_down": (gs_dn, d_dn, m_dn, v_dn)}
    order = ["rel_bias", "w_ada", "b_ada", "w_in", "b_in", "attn_sinks", "gmlp_ln_g", "gmlp_ln_b", "gmlp_w_s", "gmlp_b_s",
             "attn_out_g", "gmlp_out_g", "w_out", "ln1_g", "ln1_b", "w_gate_up", "w_down", "ln2_g", "ln2_b"]
    shapes = {"gmlp_w_s": gmlp_w_s.shape, "gmlp_b_s": gmlp_b_s.shape}
    outs = [loss, grad_x[None]]
    for k in range(4):
        for name in order:
            if name in big:
                outs.append(big[name][k][None])
            elif name in shapes:
                outs.append(small[name][k].reshape(shapes[name]))
            else:
                outs.append(small[name][k])
    return tuple(outs)
```

```python
import math

import numpy as np
import jax
import jax.numpy as jnp
from jax import lax
from jax.experimental import pallas as pl
from jax.experimental.pallas import tpu as pltpu

F32 = jnp.float32
BF16 = jnp.bfloat16
MESH = pl.DeviceIdType.MESH

D_MODEL = 1024
N_HEADS = 8
N_KV = 2
HEAD_DIM = 64
ATTN_W = N_HEADS * HEAD_DIM
KV_W = N_KV * HEAD_DIM
N_GROUPS = 8
GROUP_DIM = 64
GMLP_W = N_GROUPS * GROUP_DIM
IN_W = ATTN_W + 2 * KV_W + 2 * GMLP_W
BLOCK = 128
N_BUCKETS = 32
MAX_DISTANCE = 128
D_FF = 2816
ALPHA = 2.0 ** 0.25
LN_EPS = 1e-5
NEG_INF = -1e30
ADAM_LR, ADAM_B1, ADAM_B2, ADAM_EPS, ADAM_WD, ADAM_STEP = 0.001, 0.9, 0.999, 1e-8, 0.01, 10
N_CHIPS = 4
N_DEV = 8
LANES = 128
V7X_VMEM_LIMIT = 56 * 2 ** 20
GELU_C = math.sqrt(2.0 / math.pi)
Q_SCALE = HEAD_DIM ** -0.5
ANY = pl.BlockSpec(memory_space=pl.ANY)

TALL_BS = N_GROUPS * BLOCK
TALL_RB = TALL_BS + 8
TALL_SK = TALL_RB + N_BUCKETS
TALL_ROWS = TALL_SK + 8
WIDE_W = 6 * D_MODEL
WIDE_LAYOUT = {
    "b_ada": (0, 0, 6 * D_MODEL),
    "b_in": (1, 0, IN_W), "ln1_g": (1, IN_W, D_MODEL), "ln1_b": (1, IN_W + D_MODEL, D_MODEL),
    "ln2_g": (1, IN_W + 2 * D_MODEL, D_MODEL), "ln2_b": (1, IN_W + 3 * D_MODEL, D_MODEL),
    "gmlp_ln_g": (2, 0, GMLP_W), "gmlp_ln_b": (2, GMLP_W, GMLP_W), "attn_out_g": (2, 2 * GMLP_W, ATTN_W),
    "gmlp_out_g": (2, 2 * GMLP_W + ATTN_W, GMLP_W), "loss": (2, 3 * GMLP_W + ATTN_W, D_MODEL)}
WIDE_PARAMS = tuple(n for n in WIDE_LAYOUT if n != "loss")


def _params(sem=None):
    return pltpu.CompilerParams(dimension_semantics=sem, vmem_limit_bytes=V7X_VMEM_LIMIT)


def _const_spec(shape, single=False):
    nd = len(shape)
    if single:
        return pl.BlockSpec(shape, lambda *_: (0,) * nd, pipeline_mode=pl.Buffered(1))
    return pl.BlockSpec(shape, lambda *_: (0,) * nd)


def _dot(a, b):
    return jnp.dot(a, b, preferred_element_type=F32)


def _dot_nt(a, b):
    return lax.dot_general(a, b, (((1,), (1,)), ((), ())), preferred_element_type=F32)


def _dot_tn(a, b):
    return lax.dot_general(a, b, (((0,), (0,)), ((), ())), preferred_element_type=F32)


def _gelu(x):
    t = jnp.tanh(GELU_C * (x + 0.044715 * x * x * x))
    return 0.5 * x * (1.0 + t), t


def _gelu_grad(x, t):
    return 0.5 * (1.0 + t) + 0.5 * x * (1.0 - t * t) * GELU_C * (1.0 + 3.0 * 0.044715 * x * x)


def _split_dot(x, a):
    hi = x.astype(BF16)
    lo = (x - hi.astype(F32)).astype(BF16)
    return _dot(hi, a) + _dot(lo, a)


def _group_mean_matrix():
    g = np.arange(GMLP_W) // GROUP_DIM
    return jnp.asarray((g[:, None] == g[None, :]).astype(np.float32) / GROUP_DIM, dtype=BF16)


def _ln_stats(z):
    mu = jnp.mean(z, axis=-1, keepdims=True)
    d = z - mu
    var = jnp.mean(d * d, axis=-1, keepdims=True)
    rstd = lax.rsqrt(var + LN_EPS)
    return d * rstd, rstd


def _ln_bwd(dxhat, xhat, rstd):
    m1 = jnp.mean(dxhat, axis=-1, keepdims=True)
    m2 = jnp.mean(dxhat * xhat, axis=-1, keepdims=True)
    return rstd * (dxhat - m1 - xhat * m2)


def _colsum(x):
    return jnp.sum(x, axis=0, keepdims=True)


def _my_pos():
    return lax.axis_index("x"), lax.axis_index("y"), lax.axis_index("c")


def _other_chips(x, y):
    return [(1 - x, y), (x, 1 - y), (1 - x, 1 - y)]


def _chip_index_scalar():
    ix, iy, _ = _my_pos()
    return jnp.reshape(2 * ix + iy, (1,)).astype(jnp.int32)


def _core_index_scalar():
    return jnp.reshape(lax.axis_index("c"), (1,)).astype(jnp.int32)


class _Gather8:
    def __init__(self, x_refs, out_refs, send_sems, recv_sems, local_sems):
        self.x_refs, self.out_refs = x_refs, out_refs
        self.send_sems, self.recv_sems, self.local_sems = send_sems, recv_sems, local_sems
        self.x, self.y, self.c = _my_pos()
        self.me, self.sibling = (self.x, self.y, self.c), (self.x, self.y, 1 - self.c)
        self.chips = _other_chips(self.x, self.y)

    def _rows(self, a, px, py, pc):
        m_per = self.x_refs[a].shape[0]
        return self.out_refs[a].at[pl.ds((4 * px + 2 * py + pc) * m_per, m_per), :]

    def _copy(self, a, k, block, to, src=None):
        return pltpu.make_async_remote_copy(
            src_ref=self._rows(a, *block) if src is None else src, dst_ref=self._rows(a, *block),
            send_sem=self.send_sems.at[7 * a + k], recv_sem=self.recv_sems.at[7 * a + k], device_id=to,
            device_id_type=MESH)

    def _local(self, a):
        return pltpu.make_async_copy(self.x_refs[a], self._rows(a, *self.me), self.local_sems.at[a])

    def start(self):
        for a in range(len(self.x_refs)):
            self._local(a).start()
            self._copy(a, 0, self.me, self.sibling, src=self.x_refs[a]).start()
            for j, chip in enumerate(self.chips):
                self._copy(a, 1 + j, self.me, (*chip, self.c), src=self.x_refs[a]).start()

    def forward(self):
        for a in range(len(self.x_refs)):
            for j, chip in enumerate(self.chips):
                self._copy(a, 1 + j, (*chip, self.c), self.me).wait_recv()
                self._copy(a, 4 + j, (*chip, self.c), self.sibling).start()

    def finish(self):
        for a in range(len(self.x_refs)):
            self._copy(a, 0, self.sibling, self.me).wait_recv()
            for j, chip in enumerate(self.chips):
                self._copy(a, 4 + j, (*chip, 1 - self.c), self.me).wait_recv()
        for a in range(len(self.x_refs)):
            for k in range(7):
                self._copy(a, k, self.me, self.me).wait_send()
            self._local(a).wait()

    @staticmethod
    def sems(n_v):
        return [pltpu.SemaphoreType.DMA((7 * n_v,)), pltpu.SemaphoreType.DMA((7 * n_v,)),
                pltpu.SemaphoreType.DMA((n_v,))]


def _gathered8_shapes(vs):
    return [jax.ShapeDtypeStruct((N_DEV * v.shape[0], v.shape[1]), v.dtype) for v in vs]


VMEM_WHOLE = pl.BlockSpec(memory_space=pltpu.VMEM)


def _prologue(c_pad, w_ada_s, b_ada_s, w_in_s):
    n = w_ada_s.shape[1]

    def body(c_ref, w_ref, b_ref, win_ref, sc_ref, modc_ref, modg_ref, wing_ref, call_ref, *sems):
        weights = _WeightGather([win_ref], [wing_ref], ["blk"], sems[0], sems[1])
        gather_c = _Gather8([c_ref], [call_ref], sems[2], sems[3], sems[4])
        gather_mod = _Gather8([modc_ref], [modg_ref], sems[5], sems[6], sems[7])
        weights.start()
        gather_c.start()
        gather_c.forward()
        gather_c.finish()
        cv = call_ref[...]
        sc = cv * _sigmoid(cv)
        a_hi = sc.astype(BF16)
        a_lo = (sc - a_hi.astype(F32)).astype(BF16)
        w = w_ref[...]
        w_hi = w.astype(BF16)
        w_lo = (w - w_hi.astype(F32)).astype(BF16)
        mod = _dot(a_hi, w_hi) + _dot(a_hi, w_lo) + _dot(a_lo, w_hi) + b_ref[...]
        for d in range(N_DEV):
            sc_ref[d:d + 1, :] = sc[8 * d:8 * d + 1, :]
            modc_ref[d:d + 1, :] = mod[8 * d:8 * d + 1, :]
        gather_mod.start()
        gather_mod.forward()
        gather_mod.finish()
        weights.forward()
        weights.finish()

    return pl.pallas_call(
        body, name="prologue",
        out_shape=(jax.ShapeDtypeStruct((N_DEV, D_MODEL), F32), jax.ShapeDtypeStruct((N_DEV, n), F32),
                   jax.ShapeDtypeStruct((N_DEV * N_DEV, n), F32),
                   jax.ShapeDtypeStruct(_gathered_shape(w_in_s, "blk"), BF16)),
        in_specs=[VMEM_WHOLE, VMEM_WHOLE, VMEM_WHOLE, ANY],
        out_specs=(VMEM_WHOLE, VMEM_WHOLE, VMEM_WHOLE, ANY),
        scratch_shapes=[pltpu.VMEM((N_DEV * 8, D_MODEL), F32)] + _WeightGather.sems(1) + _Gather8.sems(1)
        + _Gather8.sems(1),
        compiler_params=pltpu.CompilerParams(vmem_limit_bytes=V7X_VMEM_LIMIT),
    )(c_pad, w_ada_s, b_ada_s, w_in_s)


def _gathered_shape(shard, kind):
    r, cc = shard.shape
    return (N_CHIPS, r, cc) if kind == "blk" else (r, N_CHIPS * cc)


class _WeightGather:
    def __init__(self, shards, gathered, kinds, send_sems, recv_sems):
        self.shards, self.gathered, self.kinds = shards, gathered, kinds
        self.send_sems, self.recv_sems = send_sems, recv_sems
        self.x, self.y, self.c = _my_pos()
        self.chips = _other_chips(self.x, self.y)

    def _dst(self, a, chip, pc):
        r, cc = self.shards[a].shape
        h = r // 2
        g = self.gathered[a]
        if self.kinds[a] == "blk":
            return g.at[chip, pl.ds(pc * h, h), :]
        return g.at[pl.ds(pc * h, h), pl.ds(chip * cc, cc)]

    def _copy(self, a, k, chip, pc, to, src=None):
        d = self._dst(a, chip, pc)
        return pltpu.make_async_remote_copy(
            src_ref=d if src is None else src, dst_ref=d, send_sem=self.send_sems.at[a * 6 + k],
            recv_sem=self.recv_sems.at[a * 6 + k], device_id=to, device_id_type=MESH)

    def _each(self):
        for a in range(len(self.shards)):
            for j, chip in enumerate(self.chips):
                yield a, j, chip, 2 * chip[0] + chip[1]

    def start(self):
        my_chip = 2 * self.x + self.y
        for a, j, chip, _ in self._each():
            h = self.shards[a].shape[0] // 2
            self._copy(a, j, my_chip, self.c, (*chip, self.c), src=self.shards[a].at[pl.ds(self.c * h, h), :]).start()

    def forward(self):
        me, sibling = (self.x, self.y, self.c), (self.x, self.y, 1 - self.c)
        for a, j, chip, cj in self._each():
            self._copy(a, j, cj, self.c, me).wait_recv()
            self._copy(a, 3 + j, cj, self.c, sibling).start()

    def finish(self):
        me = (self.x, self.y, self.c)
        for a, j, chip, cj in self._each():
            self._copy(a, 3 + j, cj, 1 - self.c, me).wait_recv()
        for a, j, chip, cj in self._each():
            self._copy(a, j, cj, self.c, me).wait_send()
            self._copy(a, 3 + j, cj, self.c, me).wait_send()

    @staticmethod
    def sems(n_arr):
        return [pltpu.SemaphoreType.DMA((n_arr * 6,)), pltpu.SemaphoreType.DMA((n_arr * 6,))]


def _insert_own(gathered, shard, kind, chip):
    if kind == "blk":
        return lax.dynamic_update_slice(gathered, shard[None], (chip, 0, 0))
    return lax.dynamic_update_slice(gathered, shard, (0, chip * shard.shape[1]))


def _half_of_full(ref, kind, pc):
    if kind == "blk":
        h = ref.shape[1] // 2
        return ref.at[:, pl.ds(pc * h, h), :]
    h = ref.shape[0] // 2
    return ref.at[pl.ds(pc * h, h), :]


def _half_shape(shape, kind):
    return (shape[0], shape[1] // 2, shape[2]) if kind == "blk" else (shape[0] // 2, shape[1])


class _HalfSwap:
    def __init__(self, ins, outs, kinds, send_sems, recv_sems):
        self.ins, self.outs, self.kinds = ins, outs, kinds
        self.send_sems, self.recv_sems = send_sems, recv_sems
        self.x, self.y, self.c = _my_pos()

    def _copies(self):
        for a in range(len(self.ins)):
            yield pltpu.make_async_remote_copy(
                src_ref=_half_of_full(self.ins[a], self.kinds[a], 1 - self.c), dst_ref=self.outs[a],
                send_sem=self.send_sems.at[a], recv_sem=self.recv_sems.at[a],
                device_id=(self.x, self.y, 1 - self.c), device_id_type=MESH)

    def start(self):
        for cp in self._copies():
            cp.start()

    def wait(self):
        for cp in self._copies():
            cp.wait()

    @staticmethod
    def sems(n_arr):
        return [pltpu.SemaphoreType.DMA((n_arr,)), pltpu.SemaphoreType.DMA((n_arr,))]

    @staticmethod
    def out_shapes(fulls, kinds):
        return [jax.ShapeDtypeStruct(_half_shape(a.shape, k), a.dtype) for a, k in zip(fulls, kinds)]


def _swap_halves(fulls_bf16, kinds, name):
    n_arr = len(fulls_bf16)

    def body(*refs):
        swap = _HalfSwap(refs[:n_arr], refs[n_arr:2 * n_arr], kinds, *refs[2 * n_arr:])
        swap.start()
        swap.wait()

    return pl.pallas_call(
        body, name=name, out_shape=_HalfSwap.out_shapes(fulls_bf16, kinds),
        in_specs=[ANY] * n_arr, out_specs=[ANY] * n_arr, scratch_shapes=_HalfSwap.sems(n_arr),
    )(*fulls_bf16)


def _add_halves(full, got, kind, name):
    hs = _half_shape(full.shape, kind)

    def body(c_ref, a_ref, b_ref, o_ref, ob_ref):
        p = a_ref[...] + b_ref[...].astype(F32)
        o_ref[...] = p
        ob_ref[...] = p.astype(BF16)

    if kind == "blk":
        nb, h, cc = hs
        own = pl.BlockSpec((1, h, cc), lambda b, c_ref: (b, c_ref[0], 0))
        other = pl.BlockSpec((1, h, cc), lambda b, c_ref: (b, 0, 0))
    else:
        h, cc = hs[0], hs[1] // N_CHIPS
        own = pl.BlockSpec((h, cc), lambda b, c_ref: (c_ref[0], b))
        other = pl.BlockSpec((h, cc), lambda b, c_ref: (0, b))
    return pl.pallas_call(
        body, name=name, out_shape=(jax.ShapeDtypeStruct(hs, F32), jax.ShapeDtypeStruct(hs, BF16)),
        grid_spec=pltpu.PrefetchScalarGridSpec(
            num_scalar_prefetch=1, grid=(N_CHIPS,), in_specs=[own, other], out_specs=(other, other)),
        compiler_params=_params(("arbitrary",)),
    )(_core_index_scalar(), full, got)


def _rx_shape(part_shape, kind):
    if kind == "blk":
        return (3, part_shape[1], part_shape[2])
    return (3, part_shape[0], part_shape[1] // N_CHIPS)


class _ChipExchange:
    def __init__(self, parts, rxs, kinds, send_sems, recv_sems):
        self.parts, self.rxs, self.kinds = parts, rxs, kinds
        self.send_sems, self.recv_sems = send_sems, recv_sems
        self.x, self.y, self.c = _my_pos()
        self.chips = _other_chips(self.x, self.y)

    def _copies(self):
        for a in range(len(self.parts)):
            for j, chip in enumerate(self.chips):
                cj = 2 * chip[0] + chip[1]
                if self.kinds[a] == "blk":
                    src = self.parts[a].at[cj]
                else:
                    cc = self.parts[a].shape[1] // N_CHIPS
                    src = self.parts[a].at[:, pl.ds(cj * cc, cc)]
                yield pltpu.make_async_remote_copy(
                    src_ref=src, dst_ref=self.rxs[a].at[j], send_sem=self.send_sems.at[a * 3 + j],
                    recv_sem=self.recv_sems.at[a * 3 + j], device_id=(*chip, self.c), device_id_type=MESH)

    def start(self):
        for cp in self._copies():
            cp.start()

    def wait(self):
        for cp in self._copies():
            cp.wait_recv()
        for cp in self._copies():
            cp.wait_send()

    @staticmethod
    def sems(n_arr):
        return [pltpu.SemaphoreType.DMA((n_arr * 3,)), pltpu.SemaphoreType.DMA((n_arr * 3,))]


def _exchange_chip_partials(parts, kinds, name):
    n_arr = len(parts)

    def body(*refs):
        exchange = _ChipExchange(refs[:n_arr], refs[n_arr:2 * n_arr], kinds, *refs[2 * n_arr:])
        exchange.start()
        exchange.wait()

    return pl.pallas_call(
        body, name=name,
        out_shape=[jax.ShapeDtypeStruct(_rx_shape(p.shape, k), BF16) for p, k in zip(parts, kinds)],
        in_specs=[ANY] * n_arr, out_specs=[ANY] * n_arr, scratch_shapes=_ChipExchange.sems(n_arr),
    )(*parts)


def _sum_chips(part, rx, kind, tr, name):
    _, h, cc = rx.shape
    flips = (2, 1, 3)

    def body(chip_ref, p_ref, rx_ref, o_ref):
        own = p_ref[...].reshape(tr, cc)
        for mc in range(N_CHIPS):
            @pl.when(chip_ref[0] == mc)
            def _():
                terms = sorted([(mc, None)] + [(mc ^ f, j) for j, f in enumerate(flips)])
                acc = None
                for _, j in terms:
                    t = own if j is None else rx_ref[j].astype(F32)
                    acc = t if acc is None else acc + t
                o_ref[...] = acc

    if kind == "blk":
        own_spec = pl.BlockSpec((1, tr, cc), lambda i, chip_ref: (chip_ref[0], i, 0))
    else:
        own_spec = pl.BlockSpec((tr, cc), lambda i, chip_ref: (i, chip_ref[0]))
    return pl.pallas_call(
        body, name=name, out_shape=jax.ShapeDtypeStruct((h, cc), F32),
        grid_spec=pltpu.PrefetchScalarGridSpec(
            num_scalar_prefetch=1, grid=(h // tr,),
            in_specs=[own_spec, pl.BlockSpec((3, tr, cc), lambda i, chip_ref: (0, i, 0))],
            out_specs=pl.BlockSpec((tr, cc), lambda i, chip_ref: (i, 0))),
        compiler_params=_params(("arbitrary",)),
    )(_chip_index_scalar(), part, rx)


def _share_halves(halves, name):
    n_arr = len(halves)

    def body(*refs):
        ins, outs = refs[:n_arr], refs[n_arr:2 * n_arr]
        send_sems, recv_sems = refs[2 * n_arr:]
        x, y, c = _my_pos()
        cps = []
        for a in range(n_arr):
            cp = pltpu.make_async_remote_copy(
                src_ref=ins[a], dst_ref=outs[a], send_sem=send_sems.at[a], recv_sem=recv_sems.at[a],
                device_id=(x, y, 1 - c), device_id_type=MESH)
            cp.start()
            cps.append(cp)
        for cp in cps:
            cp.wait()

    return pl.pallas_call(
        body, name=name, out_shape=[jax.ShapeDtypeStruct(h.shape, h.dtype) for h in halves],
        in_specs=[ANY] * n_arr, out_specs=[ANY] * n_arr,
        scratch_shapes=[pltpu.SemaphoreType.DMA((n_arr,)), pltpu.SemaphoreType.DMA((n_arr,))],
    )(*halves)


def _bucket_table():
    qi = jnp.arange(BLOCK)[:, None]
    si = jnp.arange(2 * BLOCK)[None, :]
    dist = qi + BLOCK - si
    max_exact = N_BUCKETS // 2
    n = jnp.maximum(dist, 0)
    nf = jnp.maximum(n, max_exact).astype(F32)
    large = max_exact + (jnp.log(nf / max_exact) / math.log(MAX_DISTANCE / max_exact)
                         * (N_BUCKETS - max_exact)).astype(jnp.int32)
    large = jnp.minimum(large, N_BUCKETS - 1)
    return jnp.where(n < max_exact, n, large).astype(F32)


def _prep_tables(bucket, rel_bias, w_s):
    def body(bucket_ref, rb_ref, ws_ref, bias_ref, wsm_ref):
        qi = lax.broadcasted_iota(jnp.int32, (BLOCK, 2 * BLOCK), 0)
        si = lax.broadcasted_iota(jnp.int32, (BLOCK, 2 * BLOCK), 1)
        dist = qi + BLOCK - si
        in_window = (dist >= 0) & (dist < BLOCK)
        bk = bucket_ref[...]
        for h in range(N_HEADS):
            acc = jnp.zeros((BLOCK, 2 * BLOCK), F32)
            for b in range(N_BUCKETS):
                acc = jnp.where(bk == float(b), rb_ref[b, h], acc)
            bias_ref[h] = jnp.where(in_window, acc, NEG_INF)
        ti = lax.broadcasted_iota(jnp.int32, (BLOCK, BLOCK), 0)
        ui = lax.broadcasted_iota(jnp.int32, (BLOCK, BLOCK), 1)
        for g in range(N_GROUPS):
            wsm_ref[g] = jnp.where(ti >= ui, ws_ref[g], 0.0).astype(BF16)

    return pl.pallas_call(
        body, name="prep_tables",
        out_shape=(jax.ShapeDtypeStruct((N_HEADS, BLOCK, 2 * BLOCK), F32),
                   jax.ShapeDtypeStruct((N_GROUPS, BLOCK, BLOCK), BF16)),
        grid=(1,),
        in_specs=[_const_spec((BLOCK, 2 * BLOCK)), pl.BlockSpec(memory_space=pltpu.SMEM),
                  _const_spec((N_GROUPS, BLOCK, BLOCK))],
        out_specs=(_const_spec((N_HEADS, BLOCK, 2 * BLOCK)), _const_spec((N_GROUPS, BLOCK, BLOCK))),
        compiler_params=_params(("arbitrary",)),
    )(bucket, rel_bias, w_s)


def _fwd_in(x, modr, w_in, b_in, tm, shards, kinds):
    s = x.shape[0]
    n_steps = s // tm
    fwd_step = (3 * n_steps) // 4
    n_w = len(shards)

    def body(x_ref, mod_ref, w_ref, b_ref, *rest):
        shard_refs = rest[:n_w]
        h1_ref, q_ref, kv_ref, gu_ref, gv_ref = rest[n_w:n_w + 5]
        gathered_refs = rest[n_w + 5:2 * n_w + 5]
        send_sems, recv_sems = rest[2 * n_w + 5:]
        i = pl.program_id(0)
        gather = _WeightGather(shard_refs, gathered_refs, kinds, send_sems, recv_sems)

        @pl.when(i == 0)
        def _():
            gather.start()

        h1 = (x_ref[...] * (1.0 + mod_ref[1:2, :]) + mod_ref[0:1, :]).astype(BF16)
        h1_ref[...] = h1
        proj = _dot(h1, w_ref[...]) + b_ref[...]
        q_ref[...] = (proj[:, :ATTN_W] * Q_SCALE).astype(BF16)
        kv_ref[...] = proj[:, ATTN_W:ATTN_W + 2 * KV_W].astype(BF16)
        gu_ref[...] = proj[:, ATTN_W + 2 * KV_W:ATTN_W + 2 * KV_W + GMLP_W]
        gv_ref[...] = proj[:, ATTN_W + 2 * KV_W + GMLP_W:]

        @pl.when(i == fwd_step)
        def _():
            gather.forward()

        @pl.when(i == n_steps - 1)
        def _():
            gather.finish()

    row = lambda w: pl.BlockSpec((tm, w), lambda i: (i, 0))
    outs = pl.pallas_call(
        body, name="fwd_in",
        out_shape=[jax.ShapeDtypeStruct((s, D_MODEL), BF16), jax.ShapeDtypeStruct((s, ATTN_W), BF16),
                   jax.ShapeDtypeStruct((s, 2 * KV_W), BF16), jax.ShapeDtypeStruct((s, GMLP_W), F32),
                   jax.ShapeDtypeStruct((s, GMLP_W), F32)]
        + [jax.ShapeDtypeStruct(_gathered_shape(sh, k), BF16) for sh, k in zip(shards, kinds)],
        grid=(n_steps,),
        in_specs=[row(D_MODEL), _const_spec((8, D_MODEL)), _const_spec((D_MODEL, IN_W)), _const_spec((1, IN_W))]
        + [ANY] * n_w,
        out_specs=[row(D_MODEL), row(ATTN_W), row(2 * KV_W), row(GMLP_W), row(GMLP_W)] + [ANY] * n_w,
        scratch_shapes=_WeightGather.sems(n_w),
        compiler_params=_params(("arbitrary",)),
    )(x, modr, w_in, b_in, *shards)
    return outs[:5], outs[5:]


def _kv_variants(kk):
    kf = kk.astype(F32)
    lane = lax.broadcasted_iota(jnp.int32, kf.shape, 1)
    low = lane < HEAD_DIM
    k0_lo = jnp.where(low, kf, 0.0)
    k1_hi = jnp.where(low, 0.0, kf)
    k0_hi = pltpu.roll(k0_lo, HEAD_DIM, 1)
    k1_lo = pltpu.roll(k1_hi, HEAD_DIM, 1)
    return ((k0_lo.astype(BF16), k0_hi.astype(BF16)), (k1_lo.astype(BF16), k1_hi.astype(BF16)))


def _head_kv(h):
    return h // (N_HEADS // N_KV), h % 2


MIX_GROUP = 2


def _interleave(*gens):
    results = [None] * len(gens)
    active = list(enumerate(gens))
    while active:
        still = []
        for i, g in active:
            try:
                next(g)
                still.append((i, g))
            except StopIteration as done:
                results[i] = done.value
        active = still
    return results


def _attn_block_fwd(q_blk, kk, vv, bias_ref, sinks_ref, first_mask):
    kvar = _kv_variants(kk)
    vvar = _kv_variants(vv)
    heads = range(N_HEADS)
    q_pairs = [q_blk[:, (h // 2) * LANES:(h // 2 + 1) * LANES] for h in heads]
    logits = [_dot_nt(q_pairs[h], kvar[_head_kv(h)[0]][_head_kv(h)[1]]) + bias_ref[h] for h in heads]
    if first_mask is not None:
        logits = [jnp.where(first_mask, NEG_INF, lg) for lg in logits]
    yield
    ms = [jnp.maximum(jnp.max(logits[h], axis=-1, keepdims=True), sinks_ref[h]) for h in heads]
    yield
    es = [jnp.exp(logits[h] - ms[h]) for h in heads]
    ess = [jnp.exp(sinks_ref[h] - ms[h]) for h in heads]
    yield
    invs = [1.0 / (jnp.sum(es[h], axis=-1, keepdims=True) + ess[h]) for h in heads]
    probs = [(es[h] * invs[h], ess[h] * invs[h]) for h in heads]
    yield
    outs = [_dot(probs[h][0].astype(BF16), vvar[_head_kv(h)[0]][_head_kv(h)[1]]) for h in heads]
    pairs = [outs[2 * i] + outs[2 * i + 1] for i in range(N_HEADS // 2)]
    return jnp.concatenate(pairs, axis=1), probs, kvar, vvar


def _gmlp_chunk_fwd(gu, gv, ln_g, ln_b, wsm_ref, bsx, amat):
    u, tu = _gelu(gu)
    a, ta = _gelu(gv)
    yield
    mean = _split_dot(a, amat)
    d = a - mean
    yield
    var = _split_dot(d * d, amat)
    yield
    rstd = lax.rsqrt(var + LN_EPS)
    xhat = d * rstd
    vb = (xhat * ln_g + ln_b).astype(BF16)
    yield
    lane = lax.broadcasted_iota(jnp.int32, (BLOCK, LANES), 1)
    low = lane < GROUP_DIM
    cols = []
    for pair in range(N_GROUPS // 2):
        vp = vb[:, pair * LANES:(pair + 1) * LANES]
        cols.append(jnp.where(low, _dot(wsm_ref[2 * pair], vp), _dot(wsm_ref[2 * pair + 1], vp)))
    mixedv = jnp.concatenate(cols, axis=1) + bsx
    return u * mixedv, (u, tu, ta, xhat, rstd, vb, mixedv)


def _rms(a, g):
    r = lax.rsqrt(jnp.mean(a * a, axis=-1, keepdims=True) + LN_EPS)
    return a * r * g, r


def _fwd_mix(q, kv, gu, gv, x, modr, bias, sinks, gln_g, gln_b, wsm, bsx, amat, aog, gog, w_out, ln1_g, ln1_b, tm,
             ffn_shards, ffn_kinds):
    s = x.shape[0]
    nb = tm // BLOCK
    n_steps = s // tm
    fwd_step = (3 * n_steps) // 4
    n_w = len(ffn_shards)

    def body(q_ref, kv_ref, kvp_ref, gu_ref, gv_ref, x_ref, mod_ref, bias_ref, sinks_ref, glng_ref, glnb_ref, wsm_ref,
             bsx_ref, amat_ref, aog_ref, gog_ref, wout_ref, ln1g_ref, ln1b_ref, *rest):
        shard_refs = rest[:n_w]
        x1_ref, y_ref, mixed_ref = rest[n_w:n_w + 3]
        gathered_refs = rest[n_w + 3:2 * n_w + 3]
        mix_scr, send_sems, recv_sems = rest[2 * n_w + 3:]
        i = pl.program_id(0)
        gather = _WeightGather(shard_refs, gathered_refs, ffn_kinds, send_sems, recv_sems)

        @pl.when(i == 0)
        def _():
            gather.start()

        col = lax.broadcasted_iota(jnp.int32, (BLOCK, 2 * BLOCK), 1)
        for b0 in range(0, nb, MIX_GROUP):
            gens = []
            for b in range(b0, min(b0 + MIX_GROUP, nb)):
                r0 = b * BLOCK
                if b == 0:
                    kvprev = kvp_ref[...]
                    first_mask = (col < BLOCK) & (i == 0)
                else:
                    kvprev = kv_ref[r0 - BLOCK:r0, :]
                    first_mask = None
                kvcur = kv_ref[r0:r0 + BLOCK, :]
                kk = jnp.concatenate([kvprev[:, :KV_W], kvcur[:, :KV_W]], axis=0)
                vv = jnp.concatenate([kvprev[:, KV_W:], kvcur[:, KV_W:]], axis=0)
                gens.append(_attn_block_fwd(q_ref[r0:r0 + BLOCK, :], kk, vv, bias_ref, sinks_ref, first_mask))
                gens.append(_gmlp_chunk_fwd(gu_ref[r0:r0 + BLOCK, :], gv_ref[r0:r0 + BLOCK, :], glng_ref[...],
                                            glnb_ref[...], wsm_ref, bsx_ref[...], amat_ref[...]))
            res = _interleave(*gens)
            for k, b in enumerate(range(b0, min(b0 + MIX_GROUP, nb))):
                r0 = b * BLOCK
                na, _ = _rms(res[2 * k][0], aog_ref[...])
                ng, _ = _rms(res[2 * k + 1][0], gog_ref[...])
                mix_scr[r0:r0 + BLOCK, :ATTN_W] = na.astype(BF16)
                mix_scr[r0:r0 + BLOCK, ATTN_W:] = ng.astype(BF16)
        mixed = mix_scr[...]
        mixed_ref[...] = mixed
        y = _dot(mixed, wout_ref[...])
        y_ref[...] = y
        z1 = ALPHA * x_ref[...] + mod_ref[2:3, :] * y
        xhat, _ = _ln_stats(z1)
        x1_ref[...] = xhat * ln1g_ref[...] + ln1b_ref[...]

        @pl.when(i == fwd_step)
        def _():
            gather.forward()

        @pl.when(i == n_steps - 1)
        def _():
            gather.finish()

    row = lambda w: pl.BlockSpec((tm, w), lambda i: (i, 0))
    prev = pl.BlockSpec((BLOCK, 2 * KV_W), lambda i: (jnp.maximum(i * nb - 1, 0), 0))
    outs = pl.pallas_call(
        body, name="fwd_mix",
        out_shape=[jax.ShapeDtypeStruct((s, D_MODEL), F32), jax.ShapeDtypeStruct((s, D_MODEL), F32),
                   jax.ShapeDtypeStruct((s, D_MODEL), BF16)]
        + [jax.ShapeDtypeStruct(_gathered_shape(sh, k), BF16) for sh, k in zip(ffn_shards, ffn_kinds)],
        grid=(n_steps,),
        in_specs=[row(ATTN_W), row(2 * KV_W), prev, row(GMLP_W), row(GMLP_W), row(D_MODEL), _const_spec((8, D_MODEL)),
                  _const_spec((N_HEADS, BLOCK, 2 * BLOCK)), pl.BlockSpec(memory_space=pltpu.SMEM),
                  _const_spec((1, GMLP_W)), _const_spec((1, GMLP_W)), _const_spec((N_GROUPS, BLOCK, BLOCK)),
                  _const_spec((BLOCK, GMLP_W)), _const_spec((GMLP_W, GMLP_W)), _const_spec((1, ATTN_W)),
                  _const_spec((1, GMLP_W)), _const_spec((D_MODEL, D_MODEL)), _const_spec((1, D_MODEL)),
                  _const_spec((1, D_MODEL))] + [ANY] * n_w,
        out_specs=[row(D_MODEL), row(D_MODEL), row(D_MODEL)] + [ANY] * n_w,
        scratch_shapes=[pltpu.VMEM((tm, D_MODEL), BF16)] + _WeightGather.sems(n_w),
        compiler_params=_params(("arbitrary",)),
    )(q, kv, kv, gu, gv, x, modr, bias, sinks, gln_g, gln_b, wsm, bsx, amat, aog, gog, w_out, ln1_g, ln1_b, *ffn_shards)
    return outs[0], outs[1], outs[2], outs[3:]


FF_BLOCKS = N_CHIPS // 2
FF_CHUNK = D_FF // FF_BLOCKS


def _sigmoid(x):
    return 1.0 / (1.0 + jnp.exp(-x))


def _fwd_ffn(x1, target, modr, ln2_g, ln2_b, w_gu, w_dn, tm):
    s = x1.shape[0]

    def body(x1_ref, t_ref, mod_ref, g_ref, b_ref, wgu_ref, wdn_ref, h2_ref, act_ref, dy2_ref, dx1a_ref, acc_ref):
        i = pl.program_id(0)

        @pl.when(i == 0)
        def _():
            acc_ref[...] = jnp.zeros_like(acc_ref)

        x1v = x1_ref[...]
        h2 = (x1v * (1.0 + mod_ref[4:5, :]) + mod_ref[3:4, :]).astype(BF16)
        h2_ref[...] = h2
        y2 = jnp.zeros((tm, D_MODEL), F32)
        for cc in range(D_FF // FF_CHUNK):
            c0 = cc * FF_CHUNK
            gate = _dot(h2, wgu_ref[cc])
            up = _dot(h2, wgu_ref[FF_BLOCKS + cc])
            act_ref[:, c0:c0 + FF_CHUNK] = gate.astype(BF16)
            act_ref[:, D_FF + c0:D_FF + c0 + FF_CHUNK] = up.astype(BF16)
            a = (gate * _sigmoid(gate) * up).astype(BF16)
            y2 = y2 + _dot(a, wdn_ref[c0:c0 + FF_CHUNK, :])
        g2 = mod_ref[5:6, :]
        z2 = ALPHA * x1v + g2 * y2
        xhat, rstd = _ln_stats(z2)
        gain = g_ref[...]
        diff = xhat * gain + b_ref[...] - t_ref[...]
        dx2 = diff * (1.0 / D_MODEL)
        dz2 = _ln_bwd(dx2 * gain, xhat, rstd)
        dx1a_ref[...] = ALPHA * dz2
        dy2_ref[...] = (g2 * dz2).astype(BF16)
        acc_ref[0:1, :] += _colsum(diff * diff)
        acc_ref[1:2, :] += _colsum(dx2 * xhat)
        acc_ref[2:3, :] += _colsum(dx2)
        acc_ref[3:4, :] += _colsum(dz2 * y2)

    row = lambda w: pl.BlockSpec((tm, w), lambda i: (i, 0))
    return pl.pallas_call(
        body, name="fwd_ffn",
        out_shape=(jax.ShapeDtypeStruct((s, D_MODEL), BF16), jax.ShapeDtypeStruct((s, 2 * D_FF), BF16),
                   jax.ShapeDtypeStruct((s, D_MODEL), BF16), jax.ShapeDtypeStruct((s, D_MODEL), F32),
                   jax.ShapeDtypeStruct((8, D_MODEL), F32)),
        grid=(s // tm,),
        in_specs=[row(D_MODEL), row(D_MODEL), _const_spec((8, D_MODEL)), _const_spec((1, D_MODEL)),
                  _const_spec((1, D_MODEL)), _const_spec((N_CHIPS, D_MODEL, FF_CHUNK), single=True),
                  _const_spec((D_FF, D_MODEL), single=True)],
        out_specs=(row(D_MODEL), row(2 * D_FF), row(D_MODEL), row(D_MODEL), _const_spec((8, D_MODEL))),
        compiler_params=_params(("arbitrary",)),
    )(x1, target, modr, ln2_g, ln2_b, w_gu, w_dn)


def _bwd_ffn(dy2, act, w_gu, w_dn, tm):
    s = dy2.shape[0]

    def body(dy2_ref, act_ref, wgu_ref, wdn_ref, a_ref, dgu_ref, dh2_ref):
        dy2v = dy2_ref[...]
        dh2 = jnp.zeros((tm, D_MODEL), F32)
        for cc in range(D_FF // FF_CHUNK):
            c0 = cc * FF_CHUNK
            da = _dot_nt(dy2v, wdn_ref[c0:c0 + FF_CHUNK, :])
            gate = act_ref[:, c0:c0 + FF_CHUNK].astype(F32)
            up = act_ref[:, D_FF + c0:D_FF + c0 + FF_CHUNK].astype(F32)
            sg = _sigmoid(gate)
            sl = gate * sg
            a_ref[:, c0:c0 + FF_CHUNK] = (sl * up).astype(BF16)
            dgate = (da * up * (sg * (1.0 + gate * (1.0 - sg)))).astype(BF16)
            dup = (da * sl).astype(BF16)
            dgu_ref[:, c0:c0 + FF_CHUNK] = dgate
            dgu_ref[:, D_FF + c0:D_FF + c0 + FF_CHUNK] = dup
            dh2 = dh2 + _dot_nt(dgate, wgu_ref[cc])
            dh2 = dh2 + _dot_nt(dup, wgu_ref[FF_BLOCKS + cc])
        dh2_ref[...] = dh2

    row = lambda w: pl.BlockSpec((tm, w), lambda i: (i, 0))
    return pl.pallas_call(
        body, name="bwd_ffn",
        out_shape=(jax.ShapeDtypeStruct((s, D_FF), BF16), jax.ShapeDtypeStruct((s, 2 * D_FF), BF16),
                   jax.ShapeDtypeStruct((s, D_MODEL), F32)),
        grid=(s // tm,),
        in_specs=[row(D_MODEL), row(2 * D_FF), _const_spec((N_CHIPS, D_MODEL, FF_CHUNK), single=True),
                  _const_spec((D_FF, D_MODEL), single=True)],
        out_specs=(row(D_FF), row(2 * D_FF), row(D_MODEL)),
        compiler_params=_params(("parallel",)),
    )(dy2, act, w_gu, w_dn)


def _bwd_mid(dh2, dx1a, x1, x, y, modr, ln1_g, w_out, tm, swap_fulls, swap_kinds):
    s = x.shape[0]
    n_steps = s // tm
    n_g = len(swap_fulls)

    def body(dh2_ref, dx1a_ref, x1_ref, x_ref, y_ref, mod_ref, g_ref, wout_ref, *rest):
        full_refs = rest[:n_g]
        dxa_ref, dy_ref, dmix_ref, acc_ref = rest[n_g:n_g + 4]
        got_refs = rest[n_g + 4:2 * n_g + 4]
        swap = _HalfSwap(full_refs, got_refs, swap_kinds, *rest[2 * n_g + 4:])
        i = pl.program_id(0)

        @pl.when(i == 0)
        def _():
            swap.start()
            acc_ref[...] = jnp.zeros_like(acc_ref)

        dh2 = dh2_ref[...]
        x1v = x1_ref[...]
        yv = y_ref[...]
        g1 = mod_ref[2:3, :]
        dx1 = dx1a_ref[...] + dh2 * (1.0 + mod_ref[4:5, :])
        z1 = ALPHA * x_ref[...] + g1 * yv
        xhat, rstd = _ln_stats(z1)
        dz1 = _ln_bwd(dx1 * g_ref[...], xhat, rstd)
        dxa_ref[...] = ALPHA * dz1
        dy = (g1 * dz1).astype(BF16)
        dy_ref[...] = dy
        dmix_ref[...] = _dot_nt(dy, wout_ref[...])
        acc_ref[0:1, :] += _colsum(dh2 * x1v)
        acc_ref[1:2, :] += _colsum(dh2)
        acc_ref[2:3, :] += _colsum(dx1 * xhat)
        acc_ref[3:4, :] += _colsum(dx1)
        acc_ref[4:5, :] += _colsum(dz1 * yv)

        @pl.when(i == n_steps - 1)
        def _():
            swap.wait()

    row = lambda w: pl.BlockSpec((tm, w), lambda i: (i, 0))
    outs = pl.pallas_call(
        body, name="bwd_mid",
        out_shape=[jax.ShapeDtypeStruct((s, D_MODEL), F32), jax.ShapeDtypeStruct((s, D_MODEL), BF16),
                   jax.ShapeDtypeStruct((s, D_MODEL), F32), jax.ShapeDtypeStruct((8, D_MODEL), F32)]
        + _HalfSwap.out_shapes(swap_fulls, swap_kinds),
        grid=(n_steps,),
        in_specs=[row(D_MODEL)] * 5 + [_const_spec((8, D_MODEL)), _const_spec((1, D_MODEL)),
                                       _const_spec((D_MODEL, D_MODEL))] + [ANY] * n_g,
        out_specs=[row(D_MODEL), row(D_MODEL), row(D_MODEL), _const_spec((8, D_MODEL))] + [ANY] * n_g,
        scratch_shapes=_HalfSwap.sems(n_g),
        compiler_params=_params(("arbitrary",)),
    )(dh2, dx1a, x1, x, y, modr, ln1_g, w_out, *swap_fulls)
    return outs[:4], outs[4:]


def _fold_kv(t0, t1):
    lane = lax.broadcasted_iota(jnp.int32, t0.shape, 1)
    f0 = t0 + pltpu.roll(t0, HEAD_DIM, 1)
    f1 = t1 + pltpu.roll(t1, HEAD_DIM, 1)
    return jnp.where(lane < HEAD_DIM, f0, f1)


def _bwd_mix(q, kv, gu, gv, dmix, bias, sinks, gln_g, gln_b, wsm, bsx, amat, aog, gog, grad_parts, grad_kinds):
    s = q.shape[0]
    nblk = s // BLOCK
    n_g = len(grad_parts)

    def body(q_ref, kv_ref, kvp_ref, gu_ref, gv_ref, dmix_ref, bias_ref, sinks_ref, glng_ref, glnb_ref, wsm_ref,
             bsx_ref, amat_ref, aog_ref, gog_ref, *rest):
        part_refs = rest[:n_g]
        dq_ref, dkv_ref, dgu_ref, dgv_ref, gbias_ref, dws_ref, dbs_ref, vec_ref, dsink_ref = rest[n_g:n_g + 9]
        rx_refs = rest[n_g + 9:2 * n_g + 9]
        carry, send_sems, recv_sems = rest[2 * n_g + 9:]
        n = pl.program_id(0)
        exchange = _ChipExchange(part_refs, rx_refs, grad_kinds, send_sems, recv_sems)

        @pl.when(n == 0)
        def _():
            exchange.start()
            carry[...] = jnp.zeros_like(carry)
            gbias_ref[...] = jnp.zeros_like(gbias_ref)
            dws_ref[...] = jnp.zeros_like(dws_ref)
            dbs_ref[...] = jnp.zeros_like(dbs_ref)
            vec_ref[...] = jnp.zeros_like(vec_ref)
            dsink_ref[...] = jnp.zeros_like(dsink_ref)

        @pl.when(n == nblk)
        def _():
            dkv_ref[...] = carry[...].astype(BF16)
            exchange.wait()

        @pl.when(n < nblk)
        def _():
            col = lax.broadcasted_iota(jnp.int32, (BLOCK, 2 * BLOCK), 1)
            lane = lax.broadcasted_iota(jnp.int32, (BLOCK, LANES), 1)
            low = lane < HEAD_DIM
            first_mask = (col < BLOCK) & (n == 0)
            kvprev = kvp_ref[...]
            kvcur = kv_ref[...]
            kk = jnp.concatenate([kvprev[:, :KV_W], kvcur[:, :KV_W]], axis=0)
            vv = jnp.concatenate([kvprev[:, KV_W:], kvcur[:, KV_W:]], axis=0)
            q_blk = q_ref[...]
            (attn, probs, kvar, vvar), (gm, (u, tu, ta, xhat, rstd, vb, mixedv)) = _interleave(
                _attn_block_fwd(q_blk, kk, vv, bias_ref, sinks_ref, first_mask),
                _gmlp_chunk_fwd(gu_ref[...], gv_ref[...], glng_ref[...], glnb_ref[...], wsm_ref, bsx_ref[...],
                                amat_ref[...]))
            na_unit, r_a = _rms(attn, 1.0)
            ng_unit, r_g = _rms(gm, 1.0)

            dmix = dmix_ref[...]
            dn_a = dmix[:, :ATTN_W]
            dn_g = dmix[:, ATTN_W:]
            vec_ref[0:1, :] += _colsum(dn_a * na_unit)
            vec_ref[1:2, :] += _colsum(dn_g * ng_unit)
            t_a = dn_a * aog_ref[...]
            d_attn = r_a * t_a - na_unit * (r_a * jnp.mean(t_a * na_unit, axis=-1, keepdims=True))
            t_g = dn_g * gog_ref[...]
            d_gm = r_g * t_g - ng_unit * (r_g * jnp.mean(t_g * ng_unit, axis=-1, keepdims=True))

            def gating_bwd():
                dgu_ref[...] = (d_gm * mixedv * _gelu_grad(gu_ref[...], tu)).astype(BF16)
                dmx = d_gm * u
                dbs_ref[...] += dmx
                dmxb = dmx.astype(BF16)
                yield
                dvn_cols = []
                for pair in range(N_GROUPS // 2):
                    dp_ = dmxb[:, pair * LANES:(pair + 1) * LANES]
                    vp = vb[:, pair * LANES:(pair + 1) * LANES]
                    dvn_cols.append(
                        jnp.where(low, _dot_tn(wsm_ref[2 * pair], dp_), _dot_tn(wsm_ref[2 * pair + 1], dp_)))
                    zero = jnp.zeros_like(dp_)
                    dws_ref[2 * pair] += _dot_nt(jnp.where(low, dp_, zero), vp)
                    dws_ref[2 * pair + 1] += _dot_nt(jnp.where(low, zero, dp_), vp)
                dvn = jnp.concatenate(dvn_cols, axis=1)
                yield
                vec_ref[2:3, :] += _colsum(dvn * xhat)
                vec_ref[3:4, :] += _colsum(dvn)
                dxh = dvn * glng_ref[...]
                am = amat_ref[...]
                m1 = _split_dot(dxh, am)
                m2 = _split_dot(dxh * xhat, am)
                yield
                da = rstd * (dxh - m1 - xhat * m2)
                dgv_ref[...] = (da * _gelu_grad(gv_ref[...], ta)).astype(BF16)

            def attention_bwd():
                heads = range(N_HEADS)
                sels = [low if h % 2 == 0 else jnp.logical_not(low) for h in heads]
                pair_of = lambda a, h: a[:, (h // 2) * LANES:(h // 2 + 1) * LANES]
                do_hs = [jnp.where(sels[h], pair_of(d_attn, h), 0.0).astype(BF16) for h in heads]
                q_hs = [jnp.where(sels[h], pair_of(q_blk, h), jnp.zeros((BLOCK, LANES), BF16)) for h in heads]
                dps = [_dot_nt(do_hs[h], vvar[_head_kv(h)[0]][_head_kv(h)[1]]) for h in heads]
                yield
                deltas = [jnp.sum(probs[h][0] * dps[h], axis=-1, keepdims=True) for h in heads]
                yield
                dss = [probs[h][0] * (dps[h] - deltas[h]) for h in heads]
                for h in heads:
                    dsink_ref[h] += -(probs[h][1] * deltas[h])
                    gbias_ref[h] += dss[h]
                dsbs = [ds.astype(BF16) for ds in dss]
                pbs = [probs[h][0].astype(BF16) for h in heads]
                yield
                dqs = [_dot(dsbs[h], kvar[_head_kv(h)[0]][_head_kv(h)[1]]) for h in heads]
                tks = [_dot_tn(dsbs[h], q_hs[h]) for h in heads]
                tvs = [_dot_tn(pbs[h], do_hs[h]) for h in heads]
                dq_cols = [dqs[2 * i] + dqs[2 * i + 1] for i in range(N_HEADS // 2)]
                dq_ref[...] = (jnp.concatenate(dq_cols, axis=1) * Q_SCALE).astype(BF16)
                per_kv = N_HEADS // N_KV
                kv_sum = lambda ts, kvh: sum(ts[kvh * per_kv + 1:(kvh + 1) * per_kv], ts[kvh * per_kv])
                dkk = _fold_kv(kv_sum(tks, 0), kv_sum(tks, 1))
                dvv = _fold_kv(kv_sum(tvs, 0), kv_sum(tvs, 1))
                dkv_ref[...] = (carry[...] + jnp.concatenate([dkk[:BLOCK], dvv[:BLOCK]], axis=1)).astype(BF16)
                carry[...] = jnp.concatenate([dkk[BLOCK:], dvv[BLOCK:]], axis=1)

            _interleave(attention_bwd(), gating_bwd())

    last = nblk - 1
    cur = lambda w: pl.BlockSpec((BLOCK, w), lambda n: (jnp.minimum(n, last), 0))
    prev = lambda w: pl.BlockSpec((BLOCK, w), lambda n: (jnp.clip(n - 1, 0, last), 0))
    outs = pl.pallas_call(
        body, name="bwd_mix",
        out_shape=[jax.ShapeDtypeStruct((s, ATTN_W), BF16), jax.ShapeDtypeStruct((s, 2 * KV_W), BF16),
                   jax.ShapeDtypeStruct((s, GMLP_W), BF16), jax.ShapeDtypeStruct((s, GMLP_W), BF16),
                   jax.ShapeDtypeStruct((N_HEADS, BLOCK, 2 * BLOCK), F32),
                   jax.ShapeDtypeStruct((N_GROUPS, BLOCK, BLOCK), F32),
                   jax.ShapeDtypeStruct((BLOCK, GMLP_W), F32), jax.ShapeDtypeStruct((8, GMLP_W), F32),
                   jax.ShapeDtypeStruct((N_HEADS, BLOCK, 1), F32)]
        + [jax.ShapeDtypeStruct(_rx_shape(p.shape, k), BF16) for p, k in zip(grad_parts, grad_kinds)],
        grid=(nblk + 1,),
        in_specs=[cur(ATTN_W), cur(2 * KV_W), prev(2 * KV_W), cur(GMLP_W), cur(GMLP_W), cur(D_MODEL),
                  _const_spec((N_HEADS, BLOCK, 2 * BLOCK)), pl.BlockSpec(memory_space=pltpu.SMEM),
                  _const_spec((1, GMLP_W)), _const_spec((1, GMLP_W)), _const_spec((N_GROUPS, BLOCK, BLOCK)),
                  _const_spec((BLOCK, GMLP_W)), _const_spec((GMLP_W, GMLP_W)), _const_spec((1, ATTN_W)),
                  _const_spec((1, GMLP_W))] + [ANY] * n_g,
        out_specs=[cur(ATTN_W), prev(2 * KV_W), cur(GMLP_W), cur(GMLP_W),
                   _const_spec((N_HEADS, BLOCK, 2 * BLOCK)), _const_spec((N_GROUPS, BLOCK, BLOCK)),
                   _const_spec((BLOCK, GMLP_W)), _const_spec((8, GMLP_W)), _const_spec((N_HEADS, BLOCK, 1))]
        + [ANY] * n_g,
        scratch_shapes=[pltpu.VMEM((BLOCK, 2 * KV_W), F32)] + _ChipExchange.sems(n_g),
        compiler_params=_params(("arbitrary",)),
    )(q, kv, kv, gu, gv, dmix, bias, sinks, gln_g, gln_b, wsm, bsx, amat, aog, gog, *grad_parts)
    return outs[:9], outs[9:]


def _mix_finalize(gbias, bucket, dws, dbs, dsink):
    def body(gb_ref, bucket_ref, dws_ref, dbs_ref, dsink_ref, tall_ref):
        bk = bucket_ref[...]
        lane = lax.broadcasted_iota(jnp.int32, (N_BUCKETS, LANES), 1)
        rowi = lax.broadcasted_iota(jnp.int32, (N_BUCKETS, LANES), 0)
        drb = jnp.zeros((N_BUCKETS, LANES), F32)
        dsk = jnp.zeros((8, LANES), F32)
        lane8 = lax.broadcasted_iota(jnp.int32, (8, LANES), 1)
        for h in range(N_HEADS):
            g = gb_ref[h]
            for b in range(N_BUCKETS):
                tot = jnp.sum(_colsum(jnp.where(bk == float(b), g, 0.0)), axis=1, keepdims=True)
                drb = jnp.where((lane == h) & (rowi == b), tot, drb)
            sk = jnp.sum(dsink_ref[h], axis=0, keepdims=True)
            dsk = jnp.where(lane8 == h, sk, dsk)
        tall_ref[TALL_RB:TALL_RB + N_BUCKETS, :] = drb
        tall_ref[TALL_SK:TALL_SK + 8, :] = dsk
        ti = lax.broadcasted_iota(jnp.int32, (BLOCK, BLOCK), 0)
        ui = lax.broadcasted_iota(jnp.int32, (BLOCK, BLOCK), 1)
        for g in range(N_GROUPS):
            tall_ref[g * BLOCK:(g + 1) * BLOCK, :] = jnp.where(ti >= ui, dws_ref[g], 0.0)
        gi = lax.broadcasted_iota(jnp.int32, (GMLP_W, LANES), 0) // GROUP_DIM
        li = lax.broadcasted_iota(jnp.int32, (GMLP_W, LANES), 1)
        ind = jnp.where(gi == li, 1.0, 0.0).astype(BF16)
        d = dbs_ref[...]
        hi = d.astype(BF16)
        r1 = d - hi.astype(F32)
        mid = r1.astype(BF16)
        lo = (r1 - mid.astype(F32)).astype(BF16)
        dbsg = _dot(hi, ind) + _dot(mid, ind) + _dot(lo, ind)
        tall_ref[TALL_BS:TALL_BS + N_GROUPS, :] = dbsg.T[:N_GROUPS, :]

    return pl.pallas_call(
        body, name="mix_finalize", out_shape=jax.ShapeDtypeStruct((TALL_ROWS, LANES), F32), grid=(1,),
        in_specs=[_const_spec((N_HEADS, BLOCK, 2 * BLOCK)), _const_spec((BLOCK, 2 * BLOCK)),
                  _const_spec((N_GROUPS, BLOCK, BLOCK)), _const_spec((BLOCK, GMLP_W)),
                  _const_spec((N_HEADS, BLOCK, 1))],
        out_specs=_const_spec((TALL_ROWS, LANES)),
        compiler_params=_params(("arbitrary",)),
    )(gbias, bucket, dws, dbs, dsink)


def _bwd_in(dq, dkv, dgu, dgv, dxa, x, modr, w_in, tm):
    s = x.shape[0]

    def body(dq_ref, dkv_ref, dgu_ref, dgv_ref, dxa_ref, x_ref, mod_ref, w_ref, gx_ref, acc_ref, db_ref):
        @pl.when(pl.program_id(0) == 0)
        def _():
            acc_ref[...] = jnp.zeros_like(acc_ref)
            db_ref[...] = jnp.zeros_like(db_ref)

        dproj = jnp.concatenate([dq_ref[...], dkv_ref[...], dgu_ref[...], dgv_ref[...]], axis=1)
        dh1 = _dot_nt(dproj, w_ref[...])
        gx_ref[...] = dxa_ref[...] + dh1 * (1.0 + mod_ref[1:2, :])
        acc_ref[0:1, :] += _colsum(dh1 * x_ref[...])
        acc_ref[1:2, :] += _colsum(dh1)
        db_ref[0:1, :] += _colsum(dproj.astype(F32))

    row = lambda w: pl.BlockSpec((tm, w), lambda i: (i, 0))
    return pl.pallas_call(
        body, name="bwd_in",
        out_shape=(jax.ShapeDtypeStruct((s, D_MODEL), F32), jax.ShapeDtypeStruct((8, D_MODEL), F32),
                   jax.ShapeDtypeStruct((8, IN_W), F32)),
        grid=(s // tm,),
        in_specs=[row(ATTN_W), row(2 * KV_W), row(GMLP_W), row(GMLP_W), row(D_MODEL), row(D_MODEL),
                  _const_spec((8, D_MODEL)), _const_spec((D_MODEL, IN_W))],
        out_specs=(row(D_MODEL), _const_spec((8, D_MODEL)), _const_spec((8, IN_W))),
        compiler_params=_params(("arbitrary",)),
    )(dq, dkv, dgu, dgv, dxa, x, modr, w_in)


def _wgrad(a, bs, tm, tk, name, owner_blocks=False, gather_vs=()):
    k_all, m = a.shape
    n = sum(b.shape[1] for b in bs)
    nk = k_all // tk
    nm = m // tm
    n_b = len(bs)
    n_v = len(gather_vs)
    wb = n // N_CHIPS

    def body(a_ref, *rest):
        b_refs, v_refs = rest[:n_b], rest[n_b:n_b + n_v]
        o_ref, ob_ref = rest[n_b + n_v:n_b + n_v + 2]
        vg_refs = rest[n_b + n_v + 2:n_b + 2 * n_v + 2]
        i, k = pl.program_id(0), pl.program_id(1)
        if n_v:
            gather = _Gather8(v_refs, vg_refs, *rest[n_b + 2 * n_v + 2:])

            @pl.when((i == 0) & (k == 0))
            def _():
                gather.start()

            @pl.when((i == nm - 1) & (k == 0))
            def _():
                gather.forward()

        @pl.when(k == 0)
        def _():
            o_ref[...] = jnp.zeros_like(o_ref)

        b = b_refs[0][...] if n_b == 1 else jnp.concatenate([r[...] for r in b_refs], axis=1)
        if owner_blocks:
            av = a_ref[...]
            for j in range(N_CHIPS):
                o_ref[j] += _dot_tn(av, b[:, j * wb:(j + 1) * wb])
        else:
            o_ref[...] += _dot_tn(a_ref[...], b)

        @pl.when(k == nk - 1)
        def _():
            ob_ref[...] = o_ref[...].astype(BF16)

        if n_v:
            @pl.when((i == nm - 1) & (k == nk - 1))
            def _():
                gather.finish()

    if owner_blocks:
        out_spec = pl.BlockSpec((N_CHIPS, tm, wb), lambda i, k: (0, i, 0))
        shape = (N_CHIPS, m, wb)
    else:
        out_spec = pl.BlockSpec((tm, n), lambda i, k: (i, 0))
        shape = (m, n)
    outs = pl.pallas_call(
        body, name=name,
        out_shape=[jax.ShapeDtypeStruct(shape, F32), jax.ShapeDtypeStruct(shape, BF16)] + _gathered8_shapes(gather_vs),
        grid=(nm, nk),
        in_specs=[pl.BlockSpec((tk, tm), lambda i, k: (k, i))]
        + [pl.BlockSpec((tk, b.shape[1]), lambda i, k: (k, 0)) for b in bs] + [ANY] * n_v,
        out_specs=[out_spec, out_spec] + [ANY] * n_v,
        scratch_shapes=_Gather8.sems(n_v) if n_v else [],
        compiler_params=_params(("arbitrary", "arbitrary") if n_v else ("parallel", "arbitrary")),
    )(a, *bs, *gather_vs)
    return outs[0], outs[1], outs[2:]


def _adam_math(w, g, m, v):
    m2 = ADAM_B1 * m + (1.0 - ADAM_B1) * g
    v2 = ADAM_B2 * v + (1.0 - ADAM_B2) * (g * g)
    m_hat = m2 / (1.0 - ADAM_B1 ** ADAM_STEP)
    v_hat = v2 / (1.0 - ADAM_B2 ** ADAM_STEP)
    delta = -ADAM_LR * (m_hat / (jnp.sqrt(v_hat) + ADAM_EPS) + ADAM_WD * w)
    return delta, m2, v2


def _adam_halves(w, mine, got, m, v, tr, name):
    r, cc = w.shape
    h = r // 2
    nt = h // tr

    def body(c_ref, w_ref, mine_ref, got_ref, m_ref, v_ref, g_ref, d_ref, m2_ref, v2_ref):
        g = jnp.where(pl.program_id(0) == c_ref[0], mine_ref[...], got_ref[...])
        g_ref[...] = g
        d, m2, v2 = _adam_math(w_ref[...], g, m_ref[...], v_ref[...])
        d_ref[...] = d
        m2_ref[...] = m2
        v2_ref[...] = v2

    full = pl.BlockSpec((tr, cc), lambda hh, i, c_ref: (hh * nt + i, 0))
    half = pl.BlockSpec((tr, cc), lambda hh, i, c_ref: (i, 0))
    shp = jax.ShapeDtypeStruct((r, cc), F32)
    return pl.pallas_call(
        body, name=name, out_shape=(shp, shp, shp, shp),
        grid_spec=pltpu.PrefetchScalarGridSpec(
            num_scalar_prefetch=1, grid=(2, nt), in_specs=[full, half, half, full, full],
            out_specs=(full, full, full, full)),
        compiler_params=_params(("arbitrary", "arbitrary")),
    )(_core_index_scalar(), w, mine, got, m, v)


def _adam_w_ada(sc_t, dmod_cols, w, m, v, tr):
    r, cc = w.shape

    def body(sct_ref, dm_ref, w_ref, m_ref, v_ref, g_ref, d_ref, m2_ref, v2_ref):
        g = sct_ref[:, 0:1] * dm_ref[0:1, :]
        for k in range(1, N_DEV):
            g = g + sct_ref[:, k:k + 1] * dm_ref[k:k + 1, :]
        g_ref[...] = g
        d, m2, v2 = _adam_math(w_ref[...], g, m_ref[...], v_ref[...])
        d_ref[...] = d
        m2_ref[...] = m2
        v2_ref[...] = v2

    spec = pl.BlockSpec((tr, cc), lambda i: (i, 0))
    shp = jax.ShapeDtypeStruct((r, cc), F32)
    return pl.pallas_call(
        body, name="adam_w_ada", out_shape=(shp, shp, shp, shp), grid=(r // tr,),
        in_specs=[pl.BlockSpec((tr, N_DEV), lambda i: (i, 0)), _const_spec((N_DEV, cc)), spec, spec, spec],
        out_specs=(spec, spec, spec, spec), compiler_params=_params(("parallel",)),
    )(sc_t, dmod_cols, w, m, v)


def _pack_wide(acc_i, acc_m, acc_f, db_in, vec):
    arrs = [acc_i, acc_m, acc_f, db_in, vec]
    i_, m_, f_, b_, v_ = range(5)
    src = {"b_in": (b_, 0), "ln1_g": (m_, 2), "ln1_b": (m_, 3), "ln2_g": (f_, 1), "ln2_b": (f_, 2),
           "gmlp_ln_g": (v_, 2), "gmlp_ln_b": (v_, 3), "attn_out_g": (v_, 0), "gmlp_out_g": (v_, 1), "loss": (f_, 0)}
    dmod = [(i_, 1), (i_, 0), (m_, 4), (m_, 1), (m_, 0), (f_, 3)]

    def body(*refs):
        ins, wide_ref = refs[:5], refs[5]
        wide_ref[...] = jnp.zeros_like(wide_ref)
        for k, (a, row) in enumerate(dmod):
            wide_ref[0:1, k * D_MODEL:(k + 1) * D_MODEL] = ins[a][row:row + 1, :]
        for name, (a, row) in src.items():
            r, off, n = WIDE_LAYOUT[name]
            wide_ref[r:r + 1, off:off + n] = ins[a][row:row + 1, :]

    return pl.pallas_call(
        body, name="pack_wide", out_shape=jax.ShapeDtypeStruct((8, WIDE_W), F32), grid=(1,),
        in_specs=[_const_spec(a.shape) for a in arrs], out_specs=_const_spec((8, WIDE_W)),
        compiler_params=_params(("arbitrary",)),
    )(*arrs)


def _adam_small(gw, gt, wide_wmv, w_s, b_s, rel_bias, sinks):
    names = list(WIDE_PARAMS)
    tall = [("gmlp_w_s", w_s), ("gmlp_b_s", b_s), ("rel_bias", rel_bias), ("attn_sinks", sinks)]
    ins = [gw, gt]
    for n in names:
        ins += list(wide_wmv[n])
    for _, t in tall:
        ins += list(t)
    n_in = len(ins)

    def body(*refs):
        gw_ref, gt_ref = refs[0], refs[1]
        wmv = refs[2:n_in]
        dmod_ref, loss_ref = refs[n_in], refs[n_in + 1]
        outs = refs[n_in + 2:]

        def tall_sum(r0, nr):
            g = gt_ref[r0:r0 + nr, :]
            for d in range(1, N_DEV):
                g = g + gt_ref[d * TALL_ROWS + r0:d * TALL_ROWS + r0 + nr, :]
            return g

        def emit(k, g, w_ref, m_ref, v_ref):
            d, m2, v2 = _adam_math(w_ref[...], g, m_ref[...], v_ref[...])
            outs[4 * k][...] = g
            outs[4 * k + 1][...] = d
            outs[4 * k + 2][...] = m2
            outs[4 * k + 3][...] = v2

        gsum = gw_ref[0:8, :]
        for d in range(1, N_DEV):
            gsum = gsum + gw_ref[8 * d:8 * d + 8, :]
        for d in range(N_DEV):
            dmod_ref[d:d + 1, :] = gw_ref[8 * d:8 * d + 1, :]
        for k, n in enumerate(names):
            r, off, sz = WIDE_LAYOUT[n]
            emit(k, gsum[r:r + 1, off:off + sz], *wmv[3 * k:3 * k + 3])
        r, off, sz = WIDE_LAYOUT["loss"]
        tot = jnp.sum(gsum[r:r + 1, off:off + sz], axis=1, keepdims=True)
        loss_ref[...] = jnp.broadcast_to(tot * (0.5 / D_MODEL), loss_ref.shape)

        k0 = len(names)
        ws_refs = wmv[3 * k0:3 * k0 + 3]
        for g in range(N_GROUPS):
            rows = slice(g * BLOCK, (g + 1) * BLOCK)
            gg = tall_sum(g * BLOCK, BLOCK)
            d, m2, v2 = _adam_math(ws_refs[0][rows, :], gg, ws_refs[1][rows, :], ws_refs[2][rows, :])
            outs[4 * k0][rows, :] = gg
            outs[4 * k0 + 1][rows, :] = d
            outs[4 * k0 + 2][rows, :] = m2
            outs[4 * k0 + 3][rows, :] = v2
        emit(k0 + 1, tall_sum(TALL_BS, N_GROUPS), *wmv[3 * (k0 + 1):3 * (k0 + 1) + 3])
        emit(k0 + 2, tall_sum(TALL_RB, N_BUCKETS)[:, :N_HEADS], *wmv[3 * (k0 + 2):3 * (k0 + 2) + 3])
        emit(k0 + 3, tall_sum(TALL_SK, 8)[0:1, :N_HEADS], *wmv[3 * (k0 + 3):3 * (k0 + 3) + 3])

    out_shapes = [jax.ShapeDtypeStruct((N_DEV, WIDE_W), F32), jax.ShapeDtypeStruct((8, LANES), F32)]
    for n in names:
        out_shapes += [jax.ShapeDtypeStruct(wide_wmv[n][0].shape, F32)] * 4
    for _, t in tall:
        out_shapes += [jax.ShapeDtypeStruct(t[0].shape, F32)] * 4
    res = pl.pallas_call(
        body, name="adam_small", out_shape=out_shapes, grid=(1,),
        in_specs=[_const_spec(a.shape) for a in ins], out_specs=[_const_spec(o.shape) for o in out_shapes],
        compiler_params=_params(("arbitrary",)),
    )(*ins)
    out = {}
    for k, n in enumerate(names + [t[0] for t in tall]):
        out[n] = tuple(res[2 + 4 * k:6 + 4 * k])
    return res[0], res[1], out


def kernel(x, c, rel_bias, w_ada, b_ada, w_in, b_in, attn_sinks, gmlp_ln_g, gmlp_ln_b, gmlp_w_s, gmlp_b_s, attn_out_g, gmlp_out_g, w_out, ln1_g, ln1_b, w_gate_up, w_down, ln2_g, ln2_b, loss_target, m_rel_bias, m_w_ada, m_b_ada, m_w_in, m_b_in, m_attn_sinks, m_gmlp_ln_g, m_gmlp_ln_b, m_gmlp_w_s, m_gmlp_b_s, m_attn_out_g, m_gmlp_out_g, m_w_out, m_ln1_g, m_ln1_b, m_w_gate_up, m_w_down, m_ln2_g, m_ln2_b, v_rel_bias, v_w_ada, v_b_ada, v_w_in, v_b_in, v_attn_sinks, v_gmlp_ln_g, v_gmlp_ln_b, v_gmlp_w_s, v_gmlp_b_s, v_attn_out_g, v_gmlp_out_g, v_w_out, v_ln1_g, v_ln1_b, v_w_gate_up, v_w_down, v_ln2_g, v_ln2_b):
    ix, iy, ic = _my_pos()
    chip = 2 * ix + iy
    dev = 4 * ix + 2 * iy + ic
    s = x.shape[1]
    xs = x[0]
    tgt = loss_target[0]
    tm_big = min(512, s)
    tm_ffn = min(256, s)
    n_ada = w_ada.shape[2]

    w_in_s, w_out_s = w_in[0].astype(BF16), w_out[0].astype(BF16)
    w_gu_s, w_dn_s = w_gate_up[0].astype(BF16), w_down[0].astype(BF16)
    sc_all, _, mod_rows, w_in_g = _prologue(
        jnp.pad(c, ((0, 7), (0, 0))), w_ada[0], lax.dynamic_slice_in_dim(b_ada, chip * n_ada, n_ada, axis=1), w_in_s)
    mod_all = mod_rows.reshape(N_DEV, N_DEV, -1)
    mod_row = lax.dynamic_index_in_dim(mod_all[0::2], dev, axis=1, keepdims=False)
    modr = jnp.pad(mod_row.reshape(6, D_MODEL), ((0, 2), (0, 0)))
    w_in_g = _insert_own(w_in_g, w_in_s, "blk", chip)
    w_in_f = jnp.transpose(w_in_g, (1, 0, 2)).reshape(D_MODEL, IN_W)

    bucket = _bucket_table()
    bias, wsm = _prep_tables(bucket, rel_bias, gmlp_w_s[0])
    bsx = jnp.repeat(gmlp_b_s[0].T, GROUP_DIM, axis=1)
    amat = _group_mean_matrix()
    sinks = attn_sinks[0]

    (h1, q, kv, gu, gv), (w_out_g, w_dn_g) = _fwd_in(xs, modr, w_in_f, b_in, tm_big, [w_out_s, w_dn_s], ["blk", "blk"])
    w_out_f = _insert_own(w_out_g, w_out_s, "blk", chip).reshape(D_MODEL, D_MODEL)
    x1, y, mixed, (w_gu_g,) = _fwd_mix(
        q, kv, gu, gv, xs, modr, bias, sinks, gmlp_ln_g, gmlp_ln_b, wsm, bsx, amat, attn_out_g, gmlp_out_g, w_out_f,
        ln1_g, ln1_b, tm_big, [w_gu_s], ["blk"])
    assert w_gate_up.shape[2] == FF_CHUNK
    w_gu_f = _insert_own(w_gu_g, w_gu_s, "blk", chip)
    w_dn_f = _insert_own(w_dn_g, w_dn_s, "blk", chip).reshape(D_FF, D_MODEL)
    h2, act, dy2, dx1a, acc_f = _fwd_ffn(x1, tgt, modr, ln2_g, ln2_b, w_gu_f, w_dn_f, tm_ffn)

    a_act, dgu_ff, dh2 = _bwd_ffn(dy2, act, w_gu_f, w_dn_f, tm_ffn)
    g_dn, g_dn_b, _ = _wgrad(a_act, [dy2], D_FF // 2, min(512, s), "wgrad_down")
    g_gu, g_gu_b, _ = _wgrad(h2, [dgu_ff], 512, min(256, s), "wgrad_gate_up")
    blk3 = lambda a, rows: a.reshape(N_CHIPS, rows, a.shape[1])
    (dxa, dy, dmix, acc_m), (got_dn, got_gu) = _bwd_mid(
        dh2, dx1a, x1, xs, y, modr, ln1_g, w_out_f, tm_big, [blk3(g_dn_b, D_FF // N_CHIPS), g_gu_b], ["blk", "cols"])
    g_out, g_out_b, _ = _wgrad(mixed, [dy], 512, min(512, s), "wgrad_out")
    (got_out,) = _swap_halves([blk3(g_out_b, D_MODEL // N_CHIPS)], ["blk"], "rs_swap_out")
    kinds_a = ["blk", "cols", "blk"]
    fulls_a = [blk3(g_dn, D_FF // N_CHIPS), g_gu, blk3(g_out, D_MODEL // N_CHIPS)]
    gots_a = [got_dn, got_gu, got_out]
    parts_a = [_add_halves(f, g, k, "rs_add_a%d" % i) for i, (f, g, k) in enumerate(zip(fulls_a, gots_a, kinds_a))]
    (dq, dkv, dgu, dgv, gbias, dws, dbs, vec, dsink), rxs_a = _bwd_mix(
        q, kv, gu, gv, dmix, bias, sinks, gmlp_ln_g, gmlp_ln_b, wsm, bsx, amat, attn_out_g, gmlp_out_g,
        [p[1] for p in parts_a], kinds_a)
    tall_g = _mix_finalize(gbias, bucket, dws, dbs, dsink)
    grad_x, acc_i, db_in = _bwd_in(dq, dkv, dgu, dgv, dxa, xs, modr, w_in_f, tm_big)

    wide_g = _pack_wide(acc_i, acc_m, acc_f, db_in, vec)
    full_in, full_in_b, (gw, gt) = _wgrad(h1, [dq, dkv, dgu, dgv], 512, min(512, s), "wgrad_in", owner_blocks=True,
                                          gather_vs=[wide_g, tall_g])
    (got_in,) = _swap_halves([full_in_b], ["blk"], "rs_swap_in")
    part_in = _add_halves(full_in, got_in, "blk", "rs_add_in")
    (rx_in,) = _exchange_chip_partials([part_in[1]], ["blk"], "rs_chips_in")
    wide_wmv = {"b_ada": (b_ada, m_b_ada, v_b_ada), "b_in": (b_in, m_b_in, v_b_in),
                "ln1_g": (ln1_g, m_ln1_g, v_ln1_g), "ln1_b": (ln1_b, m_ln1_b, v_ln1_b),
                "ln2_g": (ln2_g, m_ln2_g, v_ln2_g), "ln2_b": (ln2_b, m_ln2_b, v_ln2_b),
                "gmlp_ln_g": (gmlp_ln_g, m_gmlp_ln_g, v_gmlp_ln_g), "gmlp_ln_b": (gmlp_ln_b, m_gmlp_ln_b, v_gmlp_ln_b),
                "attn_out_g": (attn_out_g, m_attn_out_g, v_attn_out_g),
                "gmlp_out_g": (gmlp_out_g, m_gmlp_out_g, v_gmlp_out_g)}
    rows2 = lambda a: a.reshape(-1, a.shape[-1])
    dmod_all, loss_t, small = _adam_small(
        gw, gt, wide_wmv, tuple(rows2(a) for a in (gmlp_w_s, m_gmlp_w_s, v_gmlp_w_s)),
        tuple(rows2(a) for a in (gmlp_b_s, m_gmlp_b_s, v_gmlp_b_s)), (rel_bias, m_rel_bias, v_rel_bias),
        (attn_sinks, m_attn_sinks, v_attn_sinks))
    loss = loss_t[0, 0]

    dmod_cols = lax.dynamic_slice_in_dim(dmod_all, chip * n_ada, n_ada, axis=1)
    g_ada, d_ada, m_ada, v_ada = _adam_w_ada(sc_all.T, dmod_cols, w_ada[0], m_w_ada[0], v_w_ada[0], 256)

    sums = [(parts_a[0][0], rxs_a[0], "blk", 176), (parts_a[1][0], rxs_a[1], "cols", 256),
            (parts_a[2][0], rxs_a[2], "blk", 128), (part_in[0], rx_in, "blk", 256)]
    mine = [_sum_chips(p, rx, k, tr, "rs_sum_%d" % i) for i, (p, rx, k, tr) in enumerate(sums)]
    got = _share_halves(mine, "rs_share")

    gs_dn, d_dn, m_dn, v_dn = _adam_halves(w_down[0], mine[0], got[0], m_w_down[0], v_w_down[0], 176, "adam_w_down")
    gs_gu, d_gu, m_gu, v_gu = _adam_halves(w_gate_up[0], mine[1], got[1], m_w_gate_up[0], v_w_gate_up[0], 256,
                                           "adam_w_gate_up")
    gs_out, d_out, m_out, v_out = _adam_halves(w_out[0], mine[2], got[2], m_w_out[0], v_w_out[0], 128, "adam_w_out")
    gs_in, d_in, m_in, v_in = _adam_halves(w_in[0], mine[3], got[3], m_w_in[0], v_w_in[0], 256, "adam_w_in")

    big = {"w_ada": (g_ada, d_ada, m_ada, v_ada), "w_in": (gs_in, d_in, m_in, v_in), "w_out": (gs_out, d_out, m_out, v_out),
           "w_gate_up": (gs_gu, d_gu, m_gu, v_gu), "w_down": (gs_dn, d_dn, m_dn, v_dn)}
    order = ["rel_bias", "w_ada", "b_ada", "w_in", "b_in", "attn_sinks", "gmlp_ln_g", "gmlp_ln_b", "gmlp_w_s", "gmlp_b_s",
             "attn_out_g", "gmlp_out_g", "w_out", "ln1_g", "ln1_b", "w_gate_up", "w_down", "ln2_g", "ln2_b"]
    shapes = {"gmlp_w_s": gmlp_w_s.shape, "gmlp_b_s": gmlp_b_s.shape}
    outs = [loss, grad_x[None]]
    for k in range(4):
        for name in order:
            if name in big:
                outs.append(big[name][k][None])
            elif name in shapes:
                outs.append(small[name][k].reshape(shapes[name]))
            else:
                outs.append(small[name][k])
    return tuple(outs)
```

```python
import math

import numpy as np
import jax
import jax.numpy as jnp
from jax import lax
from jax.experimental import pallas as pl
from jax.experimental.pallas import tpu as pltpu

F32 = jnp.float32
BF16 = jnp.bfloat16
MESH = pl.DeviceIdType.MESH

D_MODEL = 1024
N_HEADS = 8
N_KV = 2
HEAD_DIM = 64
ATTN_W = N_HEADS * HEAD_DIM
KV_W = N_KV * HEAD_DIM
N_GROUPS = 8
GROUP_DIM = 64
GMLP_W = N_GROUPS * GROUP_DIM
IN_W = ATTN_W + 2 * KV_W + 2 * GMLP_W
BLOCK = 128
N_BUCKETS = 32
MAX_DISTANCE = 128
D_FF = 2816
ALPHA = 2.0 ** 0.25
LN_EPS = 1e-5
NEG_INF = -1e30
ADAM_LR, ADAM_B1, ADAM_B2, ADAM_EPS, ADAM_WD, ADAM_STEP = 0.001, 0.9, 0.999, 1e-8, 0.01, 10
N_CHIPS = 4
N_DEV = 8
LANES = 128
V7X_VMEM_LIMIT = 56 * 2 ** 20
GELU_C = math.sqrt(2.0 / math.pi)
Q_SCALE = HEAD_DIM ** -0.5
ANY = pl.BlockSpec(memory_space=pl.ANY)

TALL_BS = N_GROUPS * BLOCK
TALL_RB = TALL_BS + 8
TALL_SK = TALL_RB + N_BUCKETS
TALL_ROWS = TALL_SK + 8
WIDE_W = 6 * D_MODEL
WIDE_LAYOUT = {
    "b_ada": (0, 0, 6 * D_MODEL),
    "b_in": (1, 0, IN_W), "ln1_g": (1, IN_W, D_MODEL), "ln1_b": (1, IN_W + D_MODEL, D_MODEL),
    "ln2_g": (1, IN_W + 2 * D_MODEL, D_MODEL), "ln2_b": (1, IN_W + 3 * D_MODEL, D_MODEL),
    "gmlp_ln_g": (2, 0, GMLP_W), "gmlp_ln_b": (2, GMLP_W, GMLP_W), "attn_out_g": (2, 2 * GMLP_W, ATTN_W),
    "gmlp_out_g": (2, 2 * GMLP_W + ATTN_W, GMLP_W), "loss": (2, 3 * GMLP_W + ATTN_W, D_MODEL)}
WIDE_PARAMS = tuple(n for n in WIDE_LAYOUT if n != "loss")


def _params(sem=None):
    return pltpu.CompilerParams(dimension_semantics=sem, vmem_limit_bytes=V7X_VMEM_LIMIT)


def _const_spec(shape, single=False):
    nd = len(shape)
    if single:
        return pl.BlockSpec(shape, lambda *_: (0,) * nd, pipeline_mode=pl.Buffered(1))
    return pl.BlockSpec(shape, lambda *_: (0,) * nd)


def _dot(a, b):
    return jnp.dot(a, b, preferred_element_type=F32)


def _dot_nt(a, b):
    return lax.dot_general(a, b, (((1,), (1,)), ((), ())), preferred_element_type=F32)


def _dot_tn(a, b):
    return lax.dot_general(a, b, (((0,), (0,)), ((), ())), preferred_element_type=F32)


def _gelu(x):
    t = jnp.tanh(GELU_C * (x + 0.044715 * x * x * x))
    return 0.5 * x * (1.0 + t), t


def _gelu_grad(x, t):
    return 0.5 * (1.0 + t) + 0.5 * x * (1.0 - t * t) * GELU_C * (1.0 + 3.0 * 0.044715 * x * x)


def _split_dot(x, a):
    hi = x.astype(BF16)
    lo = (x - hi.astype(F32)).astype(BF16)
    return _dot(hi, a) + _dot(lo, a)


def _group_mean_matrix():
    g = np.arange(GMLP_W) // GROUP_DIM
    return jnp.asarray((g[:, None] == g[None, :]).astype(np.float32) / GROUP_DIM, dtype=BF16)


def _ln_stats(z):
    mu = jnp.mean(z, axis=-1, keepdims=True)
    d = z - mu
    var = jnp.mean(d * d, axis=-1, keepdims=True)
    rstd = lax.rsqrt(var + LN_EPS)
    return d * rstd, rstd


def _ln_bwd(dxhat, xhat, rstd):
    m1 = jnp.mean(dxhat, axis=-1, keepdims=True)
    m2 = jnp.mean(dxhat * xhat, axis=-1, keepdims=True)
    return rstd * (dxhat - m1 - xhat * m2)


def _colsum(x):
    return jnp.sum(x, axis=0, keepdims=True)


def _my_pos():
    return lax.axis_index("x"), lax.axis_index("y"), lax.axis_index("c")


def _other_chips(x, y):
    return [(1 - x, y), (x, 1 - y), (1 - x, 1 - y)]


def _chip_index_scalar():
    ix, iy, _ = _my_pos()
    return jnp.reshape(2 * ix + iy, (1,)).astype(jnp.int32)


def _core_index_scalar():
    return jnp.reshape(lax.axis_index("c"), (1,)).astype(jnp.int32)


class _Gather8:
    def __init__(self, x_refs, out_refs, send_sems, recv_sems, local_sems):
        self.x_refs, self.out_refs = x_refs, out_refs
        self.send_sems, self.recv_sems, self.local_sems = send_sems, recv_sems, local_sems
        self.x, self.y, self.c = _my_pos()
        self.me, self.sibling = (self.x, self.y, self.c), (self.x, self.y, 1 - self.c)
        self.chips = _other_chips(self.x, self.y)

    def _rows(self, a, px, py, pc):
        m_per = self.x_refs[a].shape[0]
        return self.out_refs[a].at[pl.ds((4 * px + 2 * py + pc) * m_per, m_per), :]

    def _copy(self, a, k, block, to, src=None):
        return pltpu.make_async_remote_copy(
            src_ref=self._rows(a, *block) if src is None else src, dst_ref=self._rows(a, *block),
            send_sem=self.send_sems.at[7 * a + k], recv_sem=self.recv_sems.at[7 * a + k], device_id=to,
            device_id_type=MESH)

    def _local(self, a):
        return pltpu.make_async_copy(self.x_refs[a], self._rows(a, *self.me), self.local_sems.at[a])

    def start(self):
        for a in range(len(self.x_refs)):
            self._local(a).start()
            self._copy(a, 0, self.me, self.sibling, src=self.x_refs[a]).start()
            for j, chip in enumerate(self.chips):
                self._copy(a, 1 + j, self.me, (*chip, self.c), src=self.x_refs[a]).start()

    def forward(self):
        for a in range(len(self.x_refs)):
            for j, chip in enumerate(self.chips):
                self._copy(a, 1 + j, (*chip, self.c), self.me).wait_recv()
                self._copy(a, 4 + j, (*chip, self.c), self.sibling).start()

    def finish(self):
        for a in range(len(self.x_refs)):
            self._copy(a, 0, self.sibling, self.me).wait_recv()
            for j, chip in enumerate(self.chips):
                self._copy(a, 4 + j, (*chip, 1 - self.c), self.me).wait_recv()
        for a in range(len(self.x_refs)):
            for k in range(7):
                self._copy(a, k, self.me, self.me).wait_send()
            self._local(a).wait()

    @staticmethod
    def sems(n_v):
        return [pltpu.SemaphoreType.DMA((7 * n_v,)), pltpu.SemaphoreType.DMA((7 * n_v,)),
                pltpu.SemaphoreType.DMA((n_v,))]


def _gathered8_shapes(vs):
    return [jax.ShapeDtypeStruct((N_DEV * v.shape[0], v.shape[1]), v.dtype) for v in vs]


VMEM_WHOLE = pl.BlockSpec(memory_space=pltpu.VMEM)


def _prologue(c_pad, w_ada_s, b_ada_s, w_in_s):
    n = w_ada_s.shape[1]

    def body(c_ref, w_ref, b_ref, win_ref, sc_ref, modc_ref, modg_ref, wing_ref, call_ref, *sems):
        weights = _WeightGather([win_ref], [wing_ref], ["blk"], sems[0], sems[1])
        gather_c = _Gather8([c_ref], [call_ref], sems[2], sems[3], sems[4])
        gather_mod = _Gather8([modc_ref], [modg_ref], sems[5], sems[6], sems[7])
        weights.start()
        gather_c.start()
        gather_c.forward()
        gather_c.finish()
        cv = call_ref[...]
        sc = cv * _sigmoid(cv)
        a_hi = sc.astype(BF16)
        a_lo = (sc - a_hi.astype(F32)).astype(BF16)
        w = w_ref[...]
        w_hi = w.astype(BF16)
        w_lo = (w - w_hi.astype(F32)).astype(BF16)
        mod = _dot(a_hi, w_hi) + _dot(a_hi, w_lo) + _dot(a_lo, w_hi) + b_ref[...]
        for d in range(N_DEV):
            sc_ref[d:d + 1, :] = sc[8 * d:8 * d + 1, :]
            modc_ref[d:d + 1, :] = mod[8 * d:8 * d + 1, :]
        gather_mod.start()
        gather_mod.forward()
        gather_mod.finish()
        weights.forward()
        weights.finish()

    return pl.pallas_call(
        body, name="prologue",
        out_shape=(jax.ShapeDtypeStruct((N_DEV, D_MODEL), F32), jax.ShapeDtypeStruct((N_DEV, n), F32),
                   jax.ShapeDtypeStruct((N_DEV * N_DEV, n), F32),
                   jax.ShapeDtypeStruct(_gathered_shape(w_in_s, "blk"), BF16)),
        in_specs=[VMEM_WHOLE, VMEM_WHOLE, VMEM_WHOLE, ANY],
        out_specs=(VMEM_WHOLE, VMEM_WHOLE, VMEM_WHOLE, ANY),
        scratch_shapes=[pltpu.VMEM((N_DEV * 8, D_MODEL), F32)] + _WeightGather.sems(1) + _Gather8.sems(1)
        + _Gather8.sems(1),
        compiler_params=pltpu.CompilerParams(vmem_limit_bytes=V7X_VMEM_LIMIT),
    )(c_pad, w_ada_s, b_ada_s, w_in_s)


def _gathered_shape(shard, kind):
    r, cc = shard.shape
    return (N_CHIPS, r, cc) if kind == "blk" else (r, N_CHIPS * cc)


class _WeightGather:
    def __init__(self, shards, gathered, kinds, send_sems, recv_sems):
        self.shards, self.gathered, self.kinds = shards, gathered, kinds
        self.send_sems, self.recv_sems = send_sems, recv_sems
        self.x, self.y, self.c = _my_pos()
        self.chips = _other_chips(self.x, self.y)

    def _dst(self, a, chip, pc):
        r, cc = self.shards[a].shape
        h = r // 2
        g = self.gathered[a]
        if self.kinds[a] == "blk":
            return g.at[chip, pl.ds(pc * h, h), :]
        return g.at[pl.ds(pc * h, h), pl.ds(chip * cc, cc)]

    def _copy(self, a, k, chip, pc, to, src=None):
        d = self._dst(a, chip, pc)
        return pltpu.make_async_remote_copy(
            src_ref=d if src is None else src, dst_ref=d, send_sem=self.send_sems.at[a * 6 + k],
            recv_sem=self.recv_sems.at[a * 6 + k], device_id=to, device_id_type=MESH)

    def _each(self):
        for a in range(len(self.shards)):
            for j, chip in enumerate(self.chips):
                yield a, j, chip, 2 * chip[0] + chip[1]

    def start(self):
        my_chip = 2 * self.x + self.y
        for a, j, chip, _ in self._each():
            h = self.shards[a].shape[0] // 2
            self._copy(a, j, my_chip, self.c, (*chip, self.c), src=self.shards[a].at[pl.ds(self.c * h, h), :]).start()

    def forward(self):
        me, sibling = (self.x, self.y, self.c), (self.x, self.y, 1 - self.c)
        for a, j, chip, cj in self._each():
            self._copy(a, j, cj, self.c, me).wait_recv()
            self._copy(a, 3 + j, cj, self.c, sibling).start()

    def finish(self):
        me = (self.x, self.y, self.c)
        for a, j, chip, cj in self._each():
            self._copy(a, 3 + j, cj, 1 - self.c, me).wait_recv()
        for a, j, chip, cj in self._each():
            self._copy(a, j, cj, self.c, me).wait_send()
            self._copy(a, 3 + j, cj, self.c, me).wait_send()

    @staticmethod
    def sems(n_arr):
        return [pltpu.SemaphoreType.DMA((n_arr * 6,)), pltpu.SemaphoreType.DMA((n_arr * 6,))]


def _insert_own(gathered, shard, kind, chip):
    if kind == "blk":
        return lax.dynamic_update_slice(gathered, shard[None], (chip, 0, 0))
    return lax.dynamic_update_slice(gathered, shard, (0, chip * shard.shape[1]))


def _half_of_full(ref, kind, pc):
    if kind == "blk":
        h = ref.shape[1] // 2
        return ref.at[:, pl.ds(pc * h, h), :]
    h = ref.shape[0] // 2
    return ref.at[pl.ds(pc * h, h), :]


def _half_shape(shape, kind):
    return (shape[0], shape[1] // 2, shape[2]) if kind == "blk" else (shape[0] // 2, shape[1])


class _HalfSwap:
    def __init__(self, ins, outs, kinds, send_sems, recv_sems):
        self.ins, self.outs, self.kinds = ins, outs, kinds
        self.send_sems, self.recv_sems = send_sems, recv_sems
        self.x, self.y, self.c = _my_pos()

    def _copies(self):
        for a in range(len(self.ins)):
            yield pltpu.make_async_remote_copy(
                src_ref=_half_of_full(self.ins[a], self.kinds[a], 1 - self.c), dst_ref=self.outs[a],
                send_sem=self.send_sems.at[a], recv_sem=self.recv_sems.at[a],
                device_id=(self.x, self.y, 1 - self.c), device_id_type=MESH)

    def start(self):
        for cp in self._copies():
            cp.start()

    def wait(self):
        for cp in self._copies():
            cp.wait()

    @staticmethod
    def sems(n_arr):
        return [pltpu.SemaphoreType.DMA((n_arr,)), pltpu.SemaphoreType.DMA((n_arr,))]

    @staticmethod
    def out_shapes(fulls, kinds):
        return [jax.ShapeDtypeStruct(_half_shape(a.shape, k), a.dtype) for a, k in zip(fulls, kinds)]


def _swap_halves(fulls_bf16, kinds, name):
    n_arr = len(fulls_bf16)

    def body(*refs):
        swap = _HalfSwap(refs[:n_arr], refs[n_arr:2 * n_arr], kinds, *refs[2 * n_arr:])
        swap.start()
        swap.wait()

    return pl.pallas_call(
        body, name=name, out_shape=_HalfSwap.out_shapes(fulls_bf16, kinds),
        in_specs=[ANY] * n_arr, out_specs=[ANY] * n_arr, scratch_shapes=_HalfSwap.sems(n_arr),
    )(*fulls_bf16)


def _add_halves(full, got, kind, name):
    hs = _half_shape(full.shape, kind)

    def body(c_ref, a_ref, b_ref, o_ref, ob_ref):
        p = a_ref[...] + b_ref[...].astype(F32)
        o_ref[...] = p
        ob_ref[...] = p.astype(BF16)

    if kind == "blk":
        nb, h, cc = hs
        own = pl.BlockSpec((1, h, cc), lambda b, c_ref: (b, c_ref[0], 0))
        other = pl.BlockSpec((1, h, cc), lambda b, c_ref: (b, 0, 0))
    else:
        h, cc = hs[0], hs[1] // N_CHIPS
        own = pl.BlockSpec((h, cc), lambda b, c_ref: (c_ref[0], b))
        other = pl.BlockSpec((h, cc), lambda b, c_ref: (0, b))
    return pl.pallas_call(
        body, name=name, out_shape=(jax.ShapeDtypeStruct(hs, F32), jax.ShapeDtypeStruct(hs, BF16)),
        grid_spec=pltpu.PrefetchScalarGridSpec(
            num_scalar_prefetch=1, grid=(N_CHIPS,), in_specs=[own, other], out_specs=(other, other)),
        compiler_params=_params(("arbitrary",)),
    )(_core_index_scalar(), full, got)


def _rx_shape(part_shape, kind):
    if kind == "blk":
        return (3, part_shape[1], part_shape[2])
    return (3, part_shape[0], part_shape[1] // N_CHIPS)


class _ChipExchange:
    def __init__(self, parts, rxs, kinds, send_sems, recv_sems):
        self.parts, self.rxs, self.kinds = parts, rxs, kinds
        self.send_sems, self.recv_sems = send_sems, recv_sems
        self.x, self.y, self.c = _my_pos()
        self.chips = _other_chips(self.x, self.y)

    def _copies(self):
        for a in range(len(self.parts)):
            for j, chip in enumerate(self.chips):
                cj = 2 * chip[0] + chip[1]
                if self.kinds[a] == "blk":
                    src = self.parts[a].at[cj]
                else:
                    cc = self.parts[a].shape[1] // N_CHIPS
                    src = self.parts[a].at[:, pl.ds(cj * cc, cc)]
                yield pltpu.make_async_remote_copy(
                    src_ref=src, dst_ref=self.rxs[a].at[j], send_sem=self.send_sems.at[a * 3 + j],
                    recv_sem=self.recv_sems.at[a * 3 + j], device_id=(*chip, self.c), device_id_type=MESH)

    def start(self):
        for cp in self._copies():
            cp.start()

    def wait(self):
        for cp in self._copies():
            cp.wait_recv()
        for cp in self._copies():
            cp.wait_send()

    @staticmethod
    def sems(n_arr):
        return [pltpu.SemaphoreType.DMA((n_arr * 3,)), pltpu.SemaphoreType.DMA((n_arr * 3,))]


def _exchange_chip_partials(parts, kinds, name):
    n_arr = len(parts)

    def body(*refs):
        exchange = _ChipExchange(refs[:n_arr], refs[n_arr:2 * n_arr], kinds, *refs[2 * n_arr:])
        exchange.start()
        exchange.wait()

    return pl.pallas_call(
        body, name=name,
        out_shape=[jax.ShapeDtypeStruct(_rx_shape(p.shape, k), BF16) for p, k in zip(parts, kinds)],
        in_specs=[ANY] * n_arr, out_specs=[ANY] * n_arr, scratch_shapes=_ChipExchange.sems(n_arr),
    )(*parts)


def _sum_chips(part, rx, kind, tr, name):
    _, h, cc = rx.shape
    flips = (2, 1, 3)

    def body(chip_ref, p_ref, rx_ref, o_ref):
        own = p_ref[...].reshape(tr, cc)
        for mc in range(N_CHIPS):
            @pl.when(chip_ref[0] == mc)
            def _():
                terms = sorted([(mc, None)] + [(mc ^ f, j) for j, f in enumerate(flips)])
                acc = None
                for _, j in terms:
                    t = own if j is None else rx_ref[j].astype(F32)
                    acc = t if acc is None else acc + t
                o_ref[...] = acc

    if kind == "blk":
        own_spec = pl.BlockSpec((1, tr, cc), lambda i, chip_ref: (chip_ref[0], i, 0))
    else:
        own_spec = pl.BlockSpec((tr, cc), lambda i, chip_ref: (i, chip_ref[0]))
    return pl.pallas_call(
        body, name=name, out_shape=jax.ShapeDtypeStruct((h, cc), F32),
        grid_spec=pltpu.PrefetchScalarGridSpec(
            num_scalar_prefetch=1, grid=(h // tr,),
            in_specs=[own_spec, pl.BlockSpec((3, tr, cc), lambda i, chip_ref: (0, i, 0))],
            out_specs=pl.BlockSpec((tr, cc), lambda i, chip_ref: (i, 0))),
        compiler_params=_params(("arbitrary",)),
    )(_chip_index_scalar(), part, rx)


def _share_halves(halves, name):
    n_arr = len(halves)

    def body(*refs):
        ins, outs = refs[:n_arr], refs[n_arr:2 * n_arr]
        send_sems, recv_sems = refs[2 * n_arr:]
        x, y, c = _my_pos()
        cps = []
        for a in range(n_arr):
            cp = pltpu.make_async_remote_copy(
                src_ref=ins[a], dst_ref=outs[a], send_sem=send_sems.at[a], recv_sem=recv_sems.at[a],
                device_id=(x, y, 1 - c), device_id_type=MESH)
            cp.start()
            cps.append(cp)
        for cp in cps:
            cp.wait()

    return pl.pallas_call(
        body, name=name, out_shape=[jax.ShapeDtypeStruct(h.shape, h.dtype) for h in halves],
        in_specs=[ANY] * n_arr, out_specs=[ANY] * n_arr,
        scratch_shapes=[pltpu.SemaphoreType.DMA((n_arr,)), pltpu.SemaphoreType.DMA((n_arr,))],
    )(*halves)


def _bucket_table():
    qi = jnp.arange(BLOCK)[:, None]
    si = jnp.arange(2 * BLOCK)[None, :]
    dist = qi + BLOCK - si
    max_exact = N_BUCKETS // 2
    n = jnp.maximum(dist, 0)
    nf = jnp.maximum(n, max_exact).astype(F32)
    large = max_exact + (jnp.log(nf / max_exact) / math.log(MAX_DISTANCE / max_exact)
                         * (N_BUCKETS - max_exact)).astype(jnp.int32)
    large = jnp.minimum(large, N_BUCKETS - 1)
    return jnp.where(n < max_exact, n, large).astype(F32)


def _prep_tables(bucket, rel_bias, w_s):
    def body(bucket_ref, rb_ref, ws_ref, bias_ref, wsm_ref):
        qi = lax.broadcasted_iota(jnp.int32, (BLOCK, 2 * BLOCK), 0)
        si = lax.broadcasted_iota(jnp.int32, (BLOCK, 2 * BLOCK), 1)
        dist = qi + BLOCK - si
        in_window = (dist >= 0) & (dist < BLOCK)
        bk = bucket_ref[...]
        for h in range(N_HEADS):
            acc = jnp.zeros((BLOCK, 2 * BLOCK), F32)
            for b in range(N_BUCKETS):
                acc = jnp.where(bk == float(b), rb_ref[b, h], acc)
            bias_ref[h] = jnp.where(in_window, acc, NEG_INF)
        ti = lax.broadcasted_iota(jnp.int32, (BLOCK, BLOCK), 0)
        ui = lax.broadcasted_iota(jnp.int32, (BLOCK, BLOCK), 1)
        for g in range(N_GROUPS):
            wsm_ref[g] = jnp.where(ti >= ui, ws_ref[g], 0.0).astype(BF16)

    return pl.pallas_call(
        body, name="prep_tables",
        out_shape=(jax.ShapeDtypeStruct((N_HEADS, BLOCK, 2 * BLOCK), F32),
                   jax.ShapeDtypeStruct((N_GROUPS, BLOCK, BLOCK), BF16)),
        grid=(1,),
        in_specs=[_const_spec((BLOCK, 2 * BLOCK)), pl.BlockSpec(memory_space=pltpu.SMEM),
                  _const_spec((N_GROUPS, BLOCK, BLOCK))],
        out_specs=(_const_spec((N_HEADS, BLOCK, 2 * BLOCK)), _const_spec((N_GROUPS, BLOCK, BLOCK))),
        compiler_params=_params(("arbitrary",)),
    )(bucket, rel_bias, w_s)


def _fwd_in(x, modr, w_in, b_in, tm, shards, kinds):
    s = x.shape[0]
    n_steps = s // tm
    fwd_step = (3 * n_steps) // 4
    n_w = len(shards)

    def body(x_ref, mod_ref, w_ref, b_ref, *rest):
        shard_refs = rest[:n_w]
        h1_ref, q_ref, kv_ref, gu_ref, gv_ref = rest[n_w:n_w + 5]
        gathered_refs = rest[n_w + 5:2 * n_w + 5]
        send_sems, recv_sems = rest[2 * n_w + 5:]
        i = pl.program_id(0)
        gather = _WeightGather(shard_refs, gathered_refs, kinds, send_sems, recv_sems)

        @pl.when(i == 0)
        def _():
            gather.start()

        h1 = (x_ref[...] * (1.0 + mod_ref[1:2, :]) + mod_ref[0:1, :]).astype(BF16)
        h1_ref[...] = h1
        proj = _dot(h1, w_ref[...]) + b_ref[...]
        q_ref[...] = (proj[:, :ATTN_W] * Q_SCALE).astype(BF16)
        kv_ref[...] = proj[:, ATTN_W:ATTN_W + 2 * KV_W].astype(BF16)
        gu_ref[...] = proj[:, ATTN_W + 2 * KV_W:ATTN_W + 2 * KV_W + GMLP_W]
        gv_ref[...] = proj[:, ATTN_W + 2 * KV_W + GMLP_W:]

        @pl.when(i == fwd_step)
        def _():
            gather.forward()

        @pl.when(i == n_steps - 1)
        def _():
            gather.finish()

    row = lambda w: pl.BlockSpec((tm, w), lambda i: (i, 0))
    outs = pl.pallas_call(
        body, name="fwd_in",
        out_shape=[jax.ShapeDtypeStruct((s, D_MODEL), BF16), jax.ShapeDtypeStruct((s, ATTN_W), BF16),
                   jax.ShapeDtypeStruct((s, 2 * KV_W), BF16), jax.ShapeDtypeStruct((s, GMLP_W), F32),
                   jax.ShapeDtypeStruct((s, GMLP_W), F32)]
        + [jax.ShapeDtypeStruct(_gathered_shape(sh, k), BF16) for sh, k in zip(shards, kinds)],
        grid=(n_steps,),
        in_specs=[row(D_MODEL), _const_spec((8, D_MODEL)), _const_spec((D_MODEL, IN_W)), _const_spec((1, IN_W))]
        + [ANY] * n_w,
        out_specs=[row(D_MODEL), row(ATTN_W), row(2 * KV_W), row(GMLP_W), row(GMLP_W)] + [ANY] * n_w,
        scratch_shapes=_WeightGather.sems(n_w),
        compiler_params=_params(("arbitrary",)),
    )(x, modr, w_in, b_in, *shards)
    return outs[:5], outs[5:]


def _kv_variants(kk):
    kf = kk.astype(F32)
    lane = lax.broadcasted_iota(jnp.int32, kf.shape, 1)
    low = lane < HEAD_DIM
    k0_lo = jnp.where(low, kf, 0.0)
    k1_hi = jnp.where(low, 0.0, kf)
    k0_hi = pltpu.roll(k0_lo, HEAD_DIM, 1)
    k1_lo = pltpu.roll(k1_hi, HEAD_DIM, 1)
    return ((k0_lo.astype(BF16), k0_hi.astype(BF16)), (k1_lo.astype(BF16), k1_hi.astype(BF16)))


def _head_kv(h):
    return h // (N_HEADS // N_KV), h % 2


MIX_GROUP = 2


def _interleave(*gens):
    results = [None] * len(gens)
    active = list(enumerate(gens))
    while active:
        still = []
        for i, g in active:
            try:
                next(g)
                still.append((i, g))
            except StopIteration as done:
                results[i] = done.value
        active = still
    return results


def _attn_block_fwd(q_blk, kk, vv, bias_ref, sinks_ref, first_mask):
    kvar = _kv_variants(kk)
    vvar = _kv_variants(vv)
    heads = range(N_HEADS)
    q_pairs = [q_blk[:, (h // 2) * LANES:(h // 2 + 1) * LANES] for h in heads]
    logits = [_dot_nt(q_pairs[h], kvar[_head_kv(h)[0]][_head_kv(h)[1]]) + bias_ref[h] for h in heads]
    if first_mask is not None:
        logits = [jnp.where(first_mask, NEG_INF, lg) for lg in logits]
    yield
    ms = [jnp.maximum(jnp.max(logits[h], axis=-1, keepdims=True), sinks_ref[h]) for h in heads]
    yield
    es = [jnp.exp(logits[h] - ms[h]) for h in heads]
    ess = [jnp.exp(sinks_ref[h] - ms[h]) for h in heads]
    yield
    invs = [1.0 / (jnp.sum(es[h], axis=-1, keepdims=True) + ess[h]) for h in heads]
    probs = [(es[h] * invs[h], ess[h] * invs[h]) for h in heads]
    yield
    outs = [_dot(probs[h][0].astype(BF16), vvar[_head_kv(h)[0]][_head_kv(h)[1]]) for h in heads]
    pairs = [outs[2 * i] + outs[2 * i + 1] for i in range(N_HEADS // 2)]
    return jnp.concatenate(pairs, axis=1), probs, kvar, vvar


def _gmlp_chunk_fwd(gu, gv, ln_g, ln_b, wsm_ref, bsx, amat):
    u, tu = _gelu(gu)
    a, ta = _gelu(gv)
    yield
    mean = _split_dot(a, amat)
    d = a - mean
    yield
    var = _split_dot(d * d, amat)
    yield
    rstd = lax.rsqrt(var + LN_EPS)
    xhat = d * rstd
    vb = (xhat * ln_g + ln_b).astype(BF16)
    yield
    lane = lax.broadcasted_iota(jnp.int32, (BLOCK, LANES), 1)
    low = lane < GROUP_DIM
    cols = []
    for pair in range(N_GROUPS // 2):
        vp = vb[:, pair * LANES:(pair + 1) * LANES]
        cols.append(jnp.where(low, _dot(wsm_ref[2 * pair], vp), _dot(wsm_ref[2 * pair + 1], vp)))
    mixedv = jnp.concatenate(cols, axis=1) + bsx
    return u * mixedv, (u, tu, ta, xhat, rstd, vb, mixedv)


def _rms(a, g):
    r = lax.rsqrt(jnp.mean(a * a, axis=-1, keepdims=True) + LN_EPS)
    return a * r * g, r


def _fwd_mix(q, kv, gu, gv, x, modr, bias, sinks, gln_g, gln_b, wsm, bsx, amat, aog, gog, w_out, ln1_g, ln1_b, tm,
             ffn_shards, ffn_kinds):
    s = x.shape[0]
    nb = tm // BLOCK
    n_steps = s // tm
    fwd_step = (3 * n_steps) // 4
    n_w = len(ffn_shards)

    def body(q_ref, kv_ref, kvp_ref, gu_ref, gv_ref, x_ref, mod_ref, bias_ref, sinks_ref, glng_ref, glnb_ref, wsm_ref,
             bsx_ref, amat_ref, aog_ref, gog_ref, wout_ref, ln1g_ref, ln1b_ref, *rest):
        shard_refs = rest[:n_w]
        x1_ref, y_ref, mixed_ref = rest[n_w:n_w + 3]
        gathered_refs = rest[n_w + 3:2 * n_w + 3]
        mix_scr, send_sems, recv_sems = rest[2 * n_w + 3:]
        i = pl.program_id(0)
        gather = _WeightGather(shard_refs, gathered_refs, ffn_kinds, send_sems, recv_sems)

        @pl.when(i == 0)
        def _():
            gather.start()

        col = lax.broadcasted_iota(jnp.int32, (BLOCK, 2 * BLOCK), 1)
        for b0 in range(0, nb, MIX_GROUP):
            gens = []
            for b in range(b0, min(b0 + MIX_GROUP, nb)):
                r0 = b * BLOCK
                if b == 0:
                    kvprev = kvp_ref[...]
                    first_mask = (col < BLOCK) & (i == 0)
                else:
                    kvprev = kv_ref[r0 - BLOCK:r0, :]
                    first_mask = None
                kvcur = kv_ref[r0:r0 + BLOCK, :]
                kk = jnp.concatenate([kvprev[:, :KV_W], kvcur[:, :KV_W]], axis=0)
                vv = jnp.concatenate([kvprev[:, KV_W:], kvcur[:, KV_W:]], axis=0)
                gens.append(_attn_block_fwd(q_ref[r0:r0 + BLOCK, :], kk, vv, bias_ref, sinks_ref, first_mask))
                gens.append(_gmlp_chunk_fwd(gu_ref[r0:r0 + BLOCK, :], gv_ref[r0:r0 + BLOCK, :], glng_ref[...],
                                            glnb_ref[...], wsm_ref, bsx_ref[...], amat_ref[...]))
            res = _interleave(*gens)
            for k, b in enumerate(range(b0, min(b0 + MIX_GROUP, nb))):
                r0 = b * BLOCK
                na, _ = _rms(res[2 * k][0], aog_ref[...])
                ng, _ = _rms(res[2 * k + 1][0], gog_ref[...])
                mix_scr[r0:r0 + BLOCK, :ATTN_W] = na.astype(BF16)
                mix_scr[r0:r0 + BLOCK, ATTN_W:] = ng.astype(BF16)
        mixed = mix_scr[...]
        mixed_ref[...] = mixed
        y = _dot(mixed, wout_ref[...])
        y_ref[...] = y
        z1 = ALPHA * x_ref[...] + mod_ref[2:3, :] * y
        xhat, _ = _ln_stats(z1)
        x1_ref[...] = xhat * ln1g_ref[...] + ln1b_ref[...]

        @pl.when(i == fwd_step)
        def _():
            gather.forward()

        @pl.when(i == n_steps - 1)
        def _():
            gather.finish()

    row = lambda w: pl.BlockSpec((tm, w), lambda i: (i, 0))
    prev = pl.BlockSpec((BLOCK, 2 * KV_W), lambda i: (jnp.maximum(i * nb - 1, 0), 0))
    outs = pl.pallas_call(
        body, name="fwd_mix",
        out_shape=[jax.ShapeDtypeStruct((s, D_MODEL), F32), jax.ShapeDtypeStruct((s, D_MODEL), F32),
                   jax.ShapeDtypeStruct((s, D_MODEL), BF16)]
        + [jax.ShapeDtypeStruct(_gathered_shape(sh, k), BF16) for sh, k in zip(ffn_shards, ffn_kinds)],
        grid=(n_steps,),
        in_specs=[row(ATTN_W), row(2 * KV_W), prev, row(GMLP_W), row(GMLP_W), row(D_MODEL), _const_spec((8, D_MODEL)),
                  _const_spec((N_HEADS, BLOCK, 2 * BLOCK)), pl.BlockSpec(memory_space=pltpu.SMEM),
                  _const_spec((1, GMLP_W)), _const_spec((1, GMLP_W)), _const_spec((N_GROUPS, BLOCK, BLOCK)),
                  _const_spec((BLOCK, GMLP_W)), _const_spec((GMLP_W, GMLP_W)), _const_spec((1, ATTN_W)),
                  _const_spec((1, GMLP_W)), _const_spec((D_MODEL, D_MODEL)), _const_spec((1, D_MODEL)),
                  _const_spec((1, D_MODEL))] + [ANY] * n_w,
        out_specs=[row(D_MODEL), row(D_MODEL), row(D_MODEL)] + [ANY] * n_w,
        scratch_shapes=[pltpu.VMEM((tm, D_MODEL), BF16)] + _WeightGather.sems(n_w),
        compiler_params=_params(("arbitrary",)),
    )(q, kv, kv, gu, gv, x, modr, bias, sinks, gln_g, gln_b, wsm, bsx, amat, aog, gog, w_out, ln1_g, ln1_b, *ffn_shards)
    return outs[0], outs[1], outs[2], outs[3:]


FF_BLOCKS = N_CHIPS // 2
FF_CHUNK = D_FF // FF_BLOCKS


def _sigmoid(x):
    return 1.0 / (1.0 + jnp.exp(-x))


def _fwd_ffn(x1, target, modr, ln2_g, ln2_b, w_gu, w_dn, tm):
    s = x1.shape[0]

    def body(x1_ref, t_ref, mod_ref, g_ref, b_ref, wgu_ref, wdn_ref, h2_ref, act_ref, dy2_ref, dx1a_ref, acc_ref):
        i = pl.program_id(0)

        @pl.when(i == 0)
        def _():
            acc_ref[...] = jnp.zeros_like(acc_ref)

        x1v = x1_ref[...]
        h2 = (x1v * (1.0 + mod_ref[4:5, :]) + mod_ref[3:4, :]).astype(BF16)
        h2_ref[...] = h2
        y2 = jnp.zeros((tm, D_MODEL), F32)
        for cc in range(D_FF // FF_CHUNK):
            c0 = cc * FF_CHUNK
            gate = _dot(h2, wgu_ref[cc])
            up = _dot(h2, wgu_ref[FF_BLOCKS + cc])
            act_ref[:, c0:c0 + FF_CHUNK] = gate.astype(BF16)
            act_ref[:, D_FF + c0:D_FF + c0 + FF_CHUNK] = up.astype(BF16)
            a = (gate * _sigmoid(gate) * up).astype(BF16)
            y2 = y2 + _dot(a, wdn_ref[c0:c0 + FF_CHUNK, :])
        g2 = mod_ref[5:6, :]
        z2 = ALPHA * x1v + g2 * y2
        xhat, rstd = _ln_stats(z2)
        gain = g_ref[...]
        diff = xhat * gain + b_ref[...] - t_ref[...]
        dx2 = diff * (1.0 / D_MODEL)
        dz2 = _ln_bwd(dx2 * gain, xhat, rstd)
        dx1a_ref[...] = ALPHA * dz2
        dy2_ref[...] = (g2 * dz2).astype(BF16)
        acc_ref[0:1, :] += _colsum(diff * diff)
        acc_ref[1:2, :] += _colsum(dx2 * xhat)
        acc_ref[2:3, :] += _colsum(dx2)
        acc_ref[3:4, :] += _colsum(dz2 * y2)

    row = lambda w: pl.BlockSpec((tm, w), lambda i: (i, 0))
    return pl.pallas_call(
        body, name="fwd_ffn",
        out_shape=(jax.ShapeDtypeStruct((s, D_MODEL), BF16), jax.ShapeDtypeStruct((s, 2 * D_FF), BF16),
                   jax.ShapeDtypeStruct((s, D_MODEL), BF16), jax.ShapeDtypeStruct((s, D_MODEL), F32),
                   jax.ShapeDtypeStruct((8, D_MODEL), F32)),
        grid=(s // tm,),
        in_specs=[row(D_MODEL), row(D_MODEL), _const_spec((8, D_MODEL)), _const_spec((1, D_MODEL)),
                  _const_spec((1, D_MODEL)), _const_spec((N_CHIPS, D_MODEL, FF_CHUNK), single=True),
                  _const_spec((D_FF, D_MODEL), single=True)],
        out_specs=(row(D_MODEL), row(2 * D_FF), row(D_MODEL), row(D_MODEL), _const_spec((8, D_MODEL))),
        compiler_params=_params(("arbitrary",)),
    )(x1, target, modr, ln2_g, ln2_b, w_gu, w_dn)


def _bwd_ffn(dy2, act, w_gu, w_dn, tm):
    s = dy2.shape[0]

    def body(dy2_ref, act_ref, wgu_ref, wdn_ref, a_ref, dgu_ref, dh2_ref):
        dy2v = dy2_ref[...]
        dh2 = jnp.zeros((tm, D_MODEL), F32)
        for cc in range(D_FF // FF_CHUNK):
            c0 = cc * FF_CHUNK
            da = _dot_nt(dy2v, wdn_ref[c0:c0 + FF_CHUNK, :])
            gate = act_ref[:, c0:c0 + FF_CHUNK].astype(F32)
            up = act_ref[:, D_FF + c0:D_FF + c0 + FF_CHUNK].astype(F32)
            sg = _sigmoid(gate)
            sl = gate * sg
            a_ref[:, c0:c0 + FF_CHUNK] = (sl * up).astype(BF16)
            dgate = (da * up * (sg * (1.0 + gate * (1.0 - sg)))).astype(BF16)
            dup = (da * sl).astype(BF16)
            dgu_ref[:, c0:c0 + FF_CHUNK] = dgate
            dgu_ref[:, D_FF + c0:D_FF + c0 + FF_CHUNK] = dup
            dh2 = dh2 + _dot_nt(dgate, wgu_ref[cc])
            dh2 = dh2 + _dot_nt(dup, wgu_ref[FF_BLOCKS + cc])
        dh2_ref[...] = dh2

    row = lambda w: pl.BlockSpec((tm, w), lambda i: (i, 0))
    return pl.pallas_call(
        body, name="bwd_ffn",
        out_shape=(jax.ShapeDtypeStruct((s, D_FF), BF16), jax.ShapeDtypeStruct((s, 2 * D_FF), BF16),
                   jax.ShapeDtypeStruct((s, D_MODEL), F32)),
        grid=(s // tm,),
        in_specs=[row(D_MODEL), row(2 * D_FF), _const_spec((N_CHIPS, D_MODEL, FF_CHUNK), single=True),
                  _const_spec((D_FF, D_MODEL), single=True)],
        out_specs=(row(D_FF), row(2 * D_FF), row(D_MODEL)),
        compiler_params=_params(("parallel",)),
    )(dy2, act, w_gu, w_dn)


def _bwd_mid(dh2, dx1a, x1, x, y, modr, ln1_g, w_out, tm, swap_fulls, swap_kinds):
    s = x.shape[0]
    n_steps = s // tm
    n_g = len(swap_fulls)

    def body(dh2_ref, dx1a_ref, x1_ref, x_ref, y_ref, mod_ref, g_ref, wout_ref, *rest):
        full_refs = rest[:n_g]
        dxa_ref, dy_ref, dmix_ref, acc_ref = rest[n_g:n_g + 4]
        got_refs = rest[n_g + 4:2 * n_g + 4]
        swap = _HalfSwap(full_refs, got_refs, swap_kinds, *rest[2 * n_g + 4:])
        i = pl.program_id(0)

        @pl.when(i == 0)
        def _():
            swap.start()
            acc_ref[...] = jnp.zeros_like(acc_ref)

        dh2 = dh2_ref[...]
        x1v = x1_ref[...]
        yv = y_ref[...]
        g1 = mod_ref[2:3, :]
        dx1 = dx1a_ref[...] + dh2 * (1.0 + mod_ref[4:5, :])
        z1 = ALPHA * x_ref[...] + g1 * yv
        xhat, rstd = _ln_stats(z1)
        dz1 = _ln_bwd(dx1 * g_ref[...], xhat, rstd)
        dxa_ref[...] = ALPHA * dz1
        dy = (g1 * dz1).astype(BF16)
        dy_ref[...] = dy
        dmix_ref[...] = _dot_nt(dy, wout_ref[...])
        acc_ref[0:1, :] += _colsum(dh2 * x1v)
        acc_ref[1:2, :] += _colsum(dh2)
        acc_ref[2:3, :] += _colsum(dx1 * xhat)
        acc_ref[3:4, :] += _colsum(dx1)
        acc_ref[4:5, :] += _colsum(dz1 * yv)

        @pl.when(i == n_steps - 1)
        def _():
            swap.wait()

    row = lambda w: pl.BlockSpec((tm, w), lambda i: (i, 0))
    outs = pl.pallas_call(
        body, name="bwd_mid",
        out_shape=[jax.ShapeDtypeStruct((s, D_MODEL), F32), jax.ShapeDtypeStruct((s, D_MODEL), BF16),
                   jax.ShapeDtypeStruct((s, D_MODEL), F32), jax.ShapeDtypeStruct((8, D_MODEL), F32)]
        + _HalfSwap.out_shapes(swap_fulls, swap_kinds),
        grid=(n_steps,),
        in_specs=[row(D_MODEL)] * 5 + [_const_spec((8, D_MODEL)), _const_spec((1, D_MODEL)),
                                       _const_spec((D_MODEL, D_MODEL))] + [ANY] * n_g,
        out_specs=[row(D_MODEL), row(D_MODEL), row(D_MODEL), _const_spec((8, D_MODEL))] + [ANY] * n_g,
        scratch_shapes=_HalfSwap.sems(n_g),
        compiler_params=_params(("arbitrary",)),
    )(dh2, dx1a, x1, x, y, modr, ln1_g, w_out, *swap_fulls)
    return outs[:4], outs[4:]


def _fold_kv(t0, t1):
    lane = lax.broadcasted_iota(jnp.int32, t0.shape, 1)
    f0 = t0 + pltpu.roll(t0, HEAD_DIM, 1)
    f1 = t1 + pltpu.roll(t1, HEAD_DIM, 1)
    return jnp.where(lane < HEAD_DIM, f0, f1)


def _bwd_mix(q, kv, gu, gv, dmix, bias, sinks, gln_g, gln_b, wsm, bsx, amat, aog, gog, grad_parts, grad_kinds):
    s = q.shape[0]
    tile = 2 * BLOCK
    n_steps = s // tile
    n_g = len(grad_parts)

    def body(q_ref, kv_ref, kvp_ref, gu_ref, gv_ref, dmix_ref, bias_ref, sinks_ref, glng_ref, glnb_ref, wsm_ref,
             bsx_ref, amat_ref, aog_ref, gog_ref, *rest):
        part_refs = rest[:n_g]
        dq_ref, dkv_ref, dgu_ref, dgv_ref, gbias_ref, dws_ref, dbs_ref, vec_ref, dsink_ref = rest[n_g:n_g + 9]
        rx_refs = rest[n_g + 9:2 * n_g + 9]
        carry, done, send_sems, recv_sems = rest[2 * n_g + 9:]
        n = pl.program_id(0)
        exchange = _ChipExchange(part_refs, rx_refs, grad_kinds, send_sems, recv_sems)

        @pl.when(n == 0)
        def _():
            exchange.start()
            carry[...] = jnp.zeros_like(carry)
            done[...] = jnp.zeros_like(done)
            gbias_ref[...] = jnp.zeros_like(gbias_ref)
            dws_ref[...] = jnp.zeros_like(dws_ref)
            dbs_ref[...] = jnp.zeros_like(dbs_ref)
            vec_ref[...] = jnp.zeros_like(vec_ref)
            dsink_ref[...] = jnp.zeros_like(dsink_ref)

        @pl.when(n == n_steps)
        def _():
            dkv_ref[:BLOCK, :] = done[...].astype(BF16)
            dkv_ref[BLOCK:, :] = carry[...].astype(BF16)
            exchange.wait()

        @pl.when(n < n_steps)
        def _():
            col = lax.broadcasted_iota(jnp.int32, (BLOCK, 2 * BLOCK), 1)
            lane = lax.broadcasted_iota(jnp.int32, (BLOCK, LANES), 1)
            low = lane < HEAD_DIM
            rows = [slice(0, BLOCK), slice(BLOCK, tile)]
            kv_blocks = [kvp_ref[...], kv_ref[rows[0], :], kv_ref[rows[1], :]]
            masks = [(col < BLOCK) & (n == 0), None]
            q_blks = [q_ref[r, :] for r in rows]
            fwd = []
            for b in range(2):
                kk = jnp.concatenate([kv_blocks[b][:, :KV_W], kv_blocks[b + 1][:, :KV_W]], axis=0)
                vv = jnp.concatenate([kv_blocks[b][:, KV_W:], kv_blocks[b + 1][:, KV_W:]], axis=0)
                fwd.append(_attn_block_fwd(q_blks[b], kk, vv, bias_ref, sinks_ref, masks[b]))
                fwd.append(_gmlp_chunk_fwd(gu_ref[rows[b], :], gv_ref[rows[b], :], glng_ref[...], glnb_ref[...],
                                           wsm_ref, bsx_ref[...], amat_ref[...]))
            res = _interleave(*fwd[:2]) + _interleave(*fwd[2:])

            def gating_bwd(b, d_gm, saved):
                u, tu, ta, xhat, rstd, vb, mixedv = saved
                dgu_ref[rows[b], :] = (d_gm * mixedv * _gelu_grad(gu_ref[rows[b], :], tu)).astype(BF16)
                dmx = d_gm * u
                dmxb = dmx.astype(BF16)
                yield
                dvn_cols, dws = [], []
                for pair in range(N_GROUPS // 2):
                    dp_ = dmxb[:, pair * LANES:(pair + 1) * LANES]
                    vp = vb[:, pair * LANES:(pair + 1) * LANES]
                    dvn_cols.append(
                        jnp.where(low, _dot_tn(wsm_ref[2 * pair], dp_), _dot_tn(wsm_ref[2 * pair + 1], dp_)))
                    zero = jnp.zeros_like(dp_)
                    dws.append(_dot_nt(jnp.where(low, dp_, zero), vp))
                    dws.append(_dot_nt(jnp.where(low, zero, dp_), vp))
                dvn = jnp.concatenate(dvn_cols, axis=1)
                yield
                dxh = dvn * glng_ref[...]
                am = amat_ref[...]
                m1 = _split_dot(dxh, am)
                m2 = _split_dot(dxh * xhat, am)
                yield
                da = rstd * (dxh - m1 - xhat * m2)
                dgv_ref[rows[b], :] = (da * _gelu_grad(gv_ref[rows[b], :], ta)).astype(BF16)
                return dmx, dws, _colsum(dvn * xhat), _colsum(dvn)

            def attention_bwd(b, d_attn, probs, kvar, vvar):
                heads = range(N_HEADS)
                sels = [low if h % 2 == 0 else jnp.logical_not(low) for h in heads]
                pair_of = lambda a, h: a[:, (h // 2) * LANES:(h // 2 + 1) * LANES]
                do_hs = [jnp.where(sels[h], pair_of(d_attn, h), 0.0).astype(BF16) for h in heads]
                q_hs = [jnp.where(sels[h], pair_of(q_blks[b], h), jnp.zeros((BLOCK, LANES), BF16)) for h in heads]
                dps = [_dot_nt(do_hs[h], vvar[_head_kv(h)[0]][_head_kv(h)[1]]) for h in heads]
                yield
                deltas = [jnp.sum(probs[h][0] * dps[h], axis=-1, keepdims=True) for h in heads]
                yield
                dss = [probs[h][0] * (dps[h] - deltas[h]) for h in heads]
                dsinks = [-(probs[h][1] * deltas[h]) for h in heads]
                dsbs = [ds.astype(BF16) for ds in dss]
                pbs = [probs[h][0].astype(BF16) for h in heads]
                yield
                dqs = [_dot(dsbs[h], kvar[_head_kv(h)[0]][_head_kv(h)[1]]) for h in heads]
                tks = [_dot_tn(dsbs[h], q_hs[h]) for h in heads]
                tvs = [_dot_tn(pbs[h], do_hs[h]) for h in heads]
                dq_cols = [dqs[2 * i] + dqs[2 * i + 1] for i in range(N_HEADS // 2)]
                dq_ref[rows[b], :] = (jnp.concatenate(dq_cols, axis=1) * Q_SCALE).astype(BF16)
                per_kv = N_HEADS // N_KV
                kv_sum = lambda ts, kvh: sum(ts[kvh * per_kv + 1:(kvh + 1) * per_kv], ts[kvh * per_kv])
                dkk = _fold_kv(kv_sum(tks, 0), kv_sum(tks, 1))
                dvv = _fold_kv(kv_sum(tvs, 0), kv_sum(tvs, 1))
                return jnp.concatenate([dkk, dvv], axis=1), dss, dsinks

            bwd, rms_g = [], []
            for b in range(2):
                attn, probs, kvar, vvar = res[2 * b]
                gm, saved = res[2 * b + 1]
                na_unit, r_a = _rms(attn, 1.0)
                ng_unit, r_g = _rms(gm, 1.0)
                dmix = dmix_ref[rows[b], :]
                dn_a = dmix[:, :ATTN_W]
                dn_g = dmix[:, ATTN_W:]
                rms_g.append((_colsum(dn_a * na_unit), _colsum(dn_g * ng_unit)))
                t_a = dn_a * aog_ref[...]
                d_attn = r_a * t_a - na_unit * (r_a * jnp.mean(t_a * na_unit, axis=-1, keepdims=True))
                t_g = dn_g * gog_ref[...]
                d_gm = r_g * t_g - ng_unit * (r_g * jnp.mean(t_g * ng_unit, axis=-1, keepdims=True))
                bwd.append(attention_bwd(b, d_attn, probs, kvar, vvar))
                bwd.append(gating_bwd(b, d_gm, saved))
            (dkv_a, dss_a, dsk_a), (dmx_a, dws_a, glg_a, glb_a) = _interleave(*bwd[:2])
            (dkv_b, dss_b, dsk_b), (dmx_b, dws_b, glg_b, glb_b) = _interleave(*bwd[2:])

            vec_ref[0:1, :] += rms_g[0][0] + rms_g[1][0]
            vec_ref[1:2, :] += rms_g[0][1] + rms_g[1][1]
            vec_ref[2:3, :] += glg_a + glg_b
            vec_ref[3:4, :] += glb_a + glb_b
            dbs_ref[...] += dmx_a + dmx_b
            for g in range(N_GROUPS):
                dws_ref[g] += dws_a[g] + dws_b[g]
            for h in range(N_HEADS):
                gbias_ref[h] += dss_a[h] + dss_b[h]
                dsink_ref[h] += dsk_a[h] + dsk_b[h]

            dkv_ref[:BLOCK, :] = done[...].astype(BF16)
            dkv_ref[BLOCK:, :] = (carry[...] + dkv_a[:BLOCK]).astype(BF16)
            done[...] = dkv_a[BLOCK:] + dkv_b[:BLOCK]
            carry[...] = dkv_b[BLOCK:]

    last = n_steps - 1
    cur = lambda w: pl.BlockSpec((tile, w), lambda n: (jnp.minimum(n, last), 0))
    late = lambda w: pl.BlockSpec((tile, w), lambda n: (jnp.clip(n - 1, 0, last), 0))
    before = pl.BlockSpec((BLOCK, 2 * KV_W), lambda n: (jnp.clip(2 * n - 1, 0, 2 * last + 1), 0))
    outs = pl.pallas_call(
        body, name="bwd_mix",
        out_shape=[jax.ShapeDtypeStruct((s, ATTN_W), BF16), jax.ShapeDtypeStruct((s, 2 * KV_W), BF16),
                   jax.ShapeDtypeStruct((s, GMLP_W), BF16), jax.ShapeDtypeStruct((s, GMLP_W), BF16),
                   jax.ShapeDtypeStruct((N_HEADS, BLOCK, 2 * BLOCK), F32),
                   jax.ShapeDtypeStruct((N_GROUPS, BLOCK, BLOCK), F32),
                   jax.ShapeDtypeStruct((BLOCK, GMLP_W), F32), jax.ShapeDtypeStruct((8, GMLP_W), F32),
                   jax.ShapeDtypeStruct((N_HEADS, BLOCK, 1), F32)]
        + [jax.ShapeDtypeStruct(_rx_shape(p.shape, k), BF16) for p, k in zip(grad_parts, grad_kinds)],
        grid=(n_steps + 1,),
        in_specs=[cur(ATTN_W), cur(2 * KV_W), before, cur(GMLP_W), cur(GMLP_W), cur(D_MODEL),
                  _const_spec((N_HEADS, BLOCK, 2 * BLOCK)), pl.BlockSpec(memory_space=pltpu.SMEM),
                  _const_spec((1, GMLP_W)), _const_spec((1, GMLP_W)), _const_spec((N_GROUPS, BLOCK, BLOCK)),
                  _const_spec((BLOCK, GMLP_W)), _const_spec((GMLP_W, GMLP_W)), _const_spec((1, ATTN_W)),
                  _const_spec((1, GMLP_W))] + [ANY] * n_g,
        out_specs=[cur(ATTN_W), late(2 * KV_W), cur(GMLP_W), cur(GMLP_W),
                   _const_spec((N_HEADS, BLOCK, 2 * BLOCK)), _const_spec((N_GROUPS, BLOCK, BLOCK)),
                   _const_spec((BLOCK, GMLP_W)), _const_spec((8, GMLP_W)), _const_spec((N_HEADS, BLOCK, 1))]
        + [ANY] * n_g,
        scratch_shapes=[pltpu.VMEM((BLOCK, 2 * KV_W), F32), pltpu.VMEM((BLOCK, 2 * KV_W), F32)]
        + _ChipExchange.sems(n_g),
        compiler_params=_params(("arbitrary",)),
    )(q, kv, kv, gu, gv, dmix, bias, sinks, gln_g, gln_b, wsm, bsx, amat, aog, gog, *grad_parts)
    return outs[:9], outs[9:]


def _mix_finalize(gbias, bucket, dws, dbs, dsink):
    def body(gb_ref, bucket_ref, dws_ref, dbs_ref, dsink_ref, tall_ref):
        bk = bucket_ref[...]
        lane = lax.broadcasted_iota(jnp.int32, (N_BUCKETS, LANES), 1)
        rowi = lax.broadcasted_iota(jnp.int32, (N_BUCKETS, LANES), 0)
        drb = jnp.zeros((N_BUCKETS, LANES), F32)
        dsk = jnp.zeros((8, LANES), F32)
        lane8 = lax.broadcasted_iota(jnp.int32, (8, LANES), 1)
        for h in range(N_HEADS):
            g = gb_ref[h]
            for b in range(N_BUCKETS):
                tot = jnp.sum(_colsum(jnp.where(bk == float(b), g, 0.0)), axis=1, keepdims=True)
                drb = jnp.where((lane == h) & (rowi == b), tot, drb)
            sk = jnp.sum(dsink_ref[h], axis=0, keepdims=True)
            dsk = jnp.where(lane8 == h, sk, dsk)
        tall_ref[TALL_RB:TALL_RB + N_BUCKETS, :] = drb
        tall_ref[TALL_SK:TALL_SK + 8, :] = dsk
        ti = lax.broadcasted_iota(jnp.int32, (BLOCK, BLOCK), 0)
        ui = lax.broadcasted_iota(jnp.int32, (BLOCK, BLOCK), 1)
        for g in range(N_GROUPS):
            tall_ref[g * BLOCK:(g + 1) * BLOCK, :] = jnp.where(ti >= ui, dws_ref[g], 0.0)
        gi = lax.broadcasted_iota(jnp.int32, (GMLP_W, LANES), 0) // GROUP_DIM
        li = lax.broadcasted_iota(jnp.int32, (GMLP_W, LANES), 1)
        ind = jnp.where(gi == li, 1.0, 0.0).astype(BF16)
        d = dbs_ref[...]
        hi = d.astype(BF16)
        r1 = d - hi.astype(F32)
        mid = r1.astype(BF16)
        lo = (r1 - mid.astype(F32)).astype(BF16)
        dbsg = _dot(hi, ind) + _dot(mid, ind) + _dot(lo, ind)
        tall_ref[TALL_BS:TALL_BS + N_GROUPS, :] = dbsg.T[:N_GROUPS, :]

    return pl.pallas_call(
        body, name="mix_finalize", out_shape=jax.ShapeDtypeStruct((TALL_ROWS, LANES), F32), grid=(1,),
        in_specs=[_const_spec((N_HEADS, BLOCK, 2 * BLOCK)), _const_spec((BLOCK, 2 * BLOCK)),
                  _const_spec((N_GROUPS, BLOCK, BLOCK)), _const_spec((BLOCK, GMLP_W)),
                  _const_spec((N_HEADS, BLOCK, 1))],
        out_specs=_const_spec((TALL_ROWS, LANES)),
        compiler_params=_params(("arbitrary",)),
    )(gbias, bucket, dws, dbs, dsink)


def _bwd_in(dq, dkv, dgu, dgv, dxa, x, modr, w_in, tm):
    s = x.shape[0]

    def body(dq_ref, dkv_ref, dgu_ref, dgv_ref, dxa_ref, x_ref, mod_ref, w_ref, gx_ref, acc_ref, db_ref):
        @pl.when(pl.program_id(0) == 0)
        def _():
            acc_ref[...] = jnp.zeros_like(acc_ref)
            db_ref[...] = jnp.zeros_like(db_ref)

        dproj = jnp.concatenate([dq_ref[...], dkv_ref[...], dgu_ref[...], dgv_ref[...]], axis=1)
        dh1 = _dot_nt(dproj, w_ref[...])
        gx_ref[...] = dxa_ref[...] + dh1 * (1.0 + mod_ref[1:2, :])
        acc_ref[0:1, :] += _colsum(dh1 * x_ref[...])
        acc_ref[1:2, :] += _colsum(dh1)
        db_ref[0:1, :] += _colsum(dproj.astype(F32))

    row = lambda w: pl.BlockSpec((tm, w), lambda i: (i, 0))
    return pl.pallas_call(
        body, name="bwd_in",
        out_shape=(jax.ShapeDtypeStruct((s, D_MODEL), F32), jax.ShapeDtypeStruct((8, D_MODEL), F32),
                   jax.ShapeDtypeStruct((8, IN_W), F32)),
        grid=(s // tm,),
        in_specs=[row(ATTN_W), row(2 * KV_W), row(GMLP_W), row(GMLP_W), row(D_MODEL), row(D_MODEL),
                  _const_spec((8, D_MODEL)), _const_spec((D_MODEL, IN_W))],
        out_specs=(row(D_MODEL), _const_spec((8, D_MODEL)), _const_spec((8, IN_W))),
        compiler_params=_params(("arbitrary",)),
    )(dq, dkv, dgu, dgv, dxa, x, modr, w_in)


def _wgrad(a, bs, tm, tk, name, owner_blocks=False, gather_vs=()):
    k_all, m = a.shape
    n = sum(b.shape[1] for b in bs)
    nk = k_all // tk
    nm = m // tm
    n_b = len(bs)
    n_v = len(gather_vs)
    wb = n // N_CHIPS

    def body(a_ref, *rest):
        b_refs, v_refs = rest[:n_b], rest[n_b:n_b + n_v]
        o_ref, ob_ref = rest[n_b + n_v:n_b + n_v + 2]
        vg_refs = rest[n_b + n_v + 2:n_b + 2 * n_v + 2]
        i, k = pl.program_id(0), pl.program_id(1)
        if n_v:
            gather = _Gather8(v_refs, vg_refs, *rest[n_b + 2 * n_v + 2:])

            @pl.when((i == 0) & (k == 0))
            def _():
                gather.start()

            @pl.when((i == nm - 1) & (k == 0))
            def _():
                gather.forward()

        @pl.when(k == 0)
        def _():
            o_ref[...] = jnp.zeros_like(o_ref)

        b = b_refs[0][...] if n_b == 1 else jnp.concatenate([r[...] for r in b_refs], axis=1)
        if owner_blocks:
            av = a_ref[...]
            for j in range(N_CHIPS):
                o_ref[j] += _dot_tn(av, b[:, j * wb:(j + 1) * wb])
        else:
            o_ref[...] += _dot_tn(a_ref[...], b)

        @pl.when(k == nk - 1)
        def _():
            ob_ref[...] = o_ref[...].astype(BF16)

        if n_v:
            @pl.when((i == nm - 1) & (k == nk - 1))
            def _():
                gather.finish()

    if owner_blocks:
        out_spec = pl.BlockSpec((N_CHIPS, tm, wb), lambda i, k: (0, i, 0))
        shape = (N_CHIPS, m, wb)
    else:
        out_spec = pl.BlockSpec((tm, n), lambda i, k: (i, 0))
        shape = (m, n)
    outs = pl.pallas_call(
        body, name=name,
        out_shape=[jax.ShapeDtypeStruct(shape, F32), jax.ShapeDtypeStruct(shape, BF16)] + _gathered8_shapes(gather_vs),
        grid=(nm, nk),
        in_specs=[pl.BlockSpec((tk, tm), lambda i, k: (k, i))]
        + [pl.BlockSpec((tk, b.shape[1]), lambda i, k: (k, 0)) for b in bs] + [ANY] * n_v,
        out_specs=[out_spec, out_spec] + [ANY] * n_v,
        scratch_shapes=_Gather8.sems(n_v) if n_v else [],
        compiler_params=_params(("arbitrary", "arbitrary") if n_v else ("parallel", "arbitrary")),
    )(a, *bs, *gather_vs)
    return outs[0], outs[1], outs[2:]


def _adam_math(w, g, m, v):
    m2 = ADAM_B1 * m + (1.0 - ADAM_B1) * g
    v2 = ADAM_B2 * v + (1.0 - ADAM_B2) * (g * g)
    m_hat = m2 / (1.0 - ADAM_B1 ** ADAM_STEP)
    v_hat = v2 / (1.0 - ADAM_B2 ** ADAM_STEP)
    delta = -ADAM_LR * (m_hat / (jnp.sqrt(v_hat) + ADAM_EPS) + ADAM_WD * w)
    return delta, m2, v2


def _adam_halves(w, mine, got, m, v, tr, name):
    r, cc = w.shape
    h = r // 2
    nt = h // tr

    def body(c_ref, w_ref, mine_ref, got_ref, m_ref, v_ref, g_ref, d_ref, m2_ref, v2_ref):
        g = jnp.where(pl.program_id(0) == c_ref[0], mine_ref[...], got_ref[...])
        g_ref[...] = g
        d, m2, v2 = _adam_math(w_ref[...], g, m_ref[...], v_ref[...])
        d_ref[...] = d
        m2_ref[...] = m2
        v2_ref[...] = v2

    full = pl.BlockSpec((tr, cc), lambda hh, i, c_ref: (hh * nt + i, 0))
    half = pl.BlockSpec((tr, cc), lambda hh, i, c_ref: (i, 0))
    shp = jax.ShapeDtypeStruct((r, cc), F32)
    return pl.pallas_call(
        body, name=name, out_shape=(shp, shp, shp, shp),
        grid_spec=pltpu.PrefetchScalarGridSpec(
            num_scalar_prefetch=1, grid=(2, nt), in_specs=[full, half, half, full, full],
            out_specs=(full, full, full, full)),
        compiler_params=_params(("arbitrary", "arbitrary")),
    )(_core_index_scalar(), w, mine, got, m, v)


def _adam_w_ada(sc_t, dmod_cols, w, m, v, tr):
    r, cc = w.shape

    def body(sct_ref, dm_ref, w_ref, m_ref, v_ref, g_ref, d_ref, m2_ref, v2_ref):
        g = sct_ref[:, 0:1] * dm_ref[0:1, :]
        for k in range(1, N_DEV):
            g = g + sct_ref[:, k:k + 1] * dm_ref[k:k + 1, :]
        g_ref[...] = g
        d, m2, v2 = _adam_math(w_ref[...], g, m_ref[...], v_ref[...])
        d_ref[...] = d
        m2_ref[...] = m2
        v2_ref[...] = v2

    spec = pl.BlockSpec((tr, cc), lambda i: (i, 0))
    shp = jax.ShapeDtypeStruct((r, cc), F32)
    return pl.pallas_call(
        body, name="adam_w_ada", out_shape=(shp, shp, shp, shp), grid=(r // tr,),
        in_specs=[pl.BlockSpec((tr, N_DEV), lambda i: (i, 0)), _const_spec((N_DEV, cc)), spec, spec, spec],
        out_specs=(spec, spec, spec, spec), compiler_params=_params(("parallel",)),
    )(sc_t, dmod_cols, w, m, v)


def _pack_wide(acc_i, acc_m, acc_f, db_in, vec):
    arrs = [acc_i, acc_m, acc_f, db_in, vec]
    i_, m_, f_, b_, v_ = range(5)
    src = {"b_in": (b_, 0), "ln1_g": (m_, 2), "ln1_b": (m_, 3), "ln2_g": (f_, 1), "ln2_b": (f_, 2),
           "gmlp_ln_g": (v_, 2), "gmlp_ln_b": (v_, 3), "attn_out_g": (v_, 0), "gmlp_out_g": (v_, 1), "loss": (f_, 0)}
    dmod = [(i_, 1), (i_, 0), (m_, 4), (m_, 1), (m_, 0), (f_, 3)]

    def body(*refs):
        ins, wide_ref = refs[:5], refs[5]
        wide_ref[...] = jnp.zeros_like(wide_ref)
        for k, (a, row) in enumerate(dmod):
            wide_ref[0:1, k * D_MODEL:(k + 1) * D_MODEL] = ins[a][row:row + 1, :]
        for name, (a, row) in src.items():
            r, off, n = WIDE_LAYOUT[name]
            wide_ref[r:r + 1, off:off + n] = ins[a][row:row + 1, :]

    return pl.pallas_call(
        body, name="pack_wide", out_shape=jax.ShapeDtypeStruct((8, WIDE_W), F32), grid=(1,),
        in_specs=[_const_spec(a.shape) for a in arrs], out_specs=_const_spec((8, WIDE_W)),
        compiler_params=_params(("arbitrary",)),
    )(*arrs)


def _adam_small(gw, gt, wide_wmv, w_s, b_s, rel_bias, sinks):
    names = list(WIDE_PARAMS)
    tall = [("gmlp_w_s", w_s), ("gmlp_b_s", b_s), ("rel_bias", rel_bias), ("attn_sinks", sinks)]
    ins = [gw, gt]
    for n in names:
        ins += list(wide_wmv[n])
    for _, t in tall:
        ins += list(t)
    n_in = len(ins)

    def body(*refs):
        gw_ref, gt_ref = refs[0], refs[1]
        wmv = refs[2:n_in]
        dmod_ref, loss_ref = refs[n_in], refs[n_in + 1]
        outs = refs[n_in + 2:]

        def tall_sum(r0, nr):
            g = gt_ref[r0:r0 + nr, :]
            for d in range(1, N_DEV):
                g = g + gt_ref[d * TALL_ROWS + r0:d * TALL_ROWS + r0 + nr, :]
            return g

        def emit(k, g, w_ref, m_ref, v_ref):
            d, m2, v2 = _adam_math(w_ref[...], g, m_ref[...], v_ref[...])
            outs[4 * k][...] = g
            outs[4 * k + 1][...] = d
            outs[4 * k + 2][...] = m2
            outs[4 * k + 3][...] = v2

        gsum = gw_ref[0:8, :]
        for d in range(1, N_DEV):
            gsum = gsum + gw_ref[8 * d:8 * d + 8, :]
        for d in range(N_DEV):
            dmod_ref[d:d + 1, :] = gw_ref[8 * d:8 * d + 1, :]
        for k, n in enumerate(names):
            r, off, sz = WIDE_LAYOUT[n]
            emit(k, gsum[r:r + 1, off:off + sz], *wmv[3 * k:3 * k + 3])
        r, off, sz = WIDE_LAYOUT["loss"]
        tot = jnp.sum(gsum[r:r + 1, off:off + sz], axis=1, keepdims=True)
        loss_ref[...] = jnp.broadcast_to(tot * (0.5 / D_MODEL), loss_ref.shape)

        k0 = len(names)
        ws_refs = wmv[3 * k0:3 * k0 + 3]
        for g in range(N_GROUPS):
            rows = slice(g * BLOCK, (g + 1) * BLOCK)
            gg = tall_sum(g * BLOCK, BLOCK)
            d, m2, v2 = _adam_math(ws_refs[0][rows, :], gg, ws_refs[1][rows, :], ws_refs[2][rows, :])
            outs[4 * k0][rows, :] = gg
            outs[4 * k0 + 1][rows, :] = d
            outs[4 * k0 + 2][rows, :] = m2
            outs[4 * k0 + 3][rows, :] = v2
        emit(k0 + 1, tall_sum(TALL_BS, N_GROUPS), *wmv[3 * (k0 + 1):3 * (k0 + 1) + 3])
        emit(k0 + 2, tall_sum(TALL_RB, N_BUCKETS)[:, :N_HEADS], *wmv[3 * (k0 + 2):3 * (k0 + 2) + 3])
        emit(k0 + 3, tall_sum(TALL_SK, 8)[0:1, :N_HEADS], *wmv[3 * (k0 + 3):3 * (k0 + 3) + 3])

    out_shapes = [jax.ShapeDtypeStruct((N_DEV, WIDE_W), F32), jax.ShapeDtypeStruct((8, LANES), F32)]
    for n in names:
        out_shapes += [jax.ShapeDtypeStruct(wide_wmv[n][0].shape, F32)] * 4
    for _, t in tall:
        out_shapes += [jax.ShapeDtypeStruct(t[0].shape, F32)] * 4
    res = pl.pallas_call(
        body, name="adam_small", out_shape=out_shapes, grid=(1,),
        in_specs=[_const_spec(a.shape) for a in ins], out_specs=[_const_spec(o.shape) for o in out_shapes],
        compiler_params=_params(("arbitrary",)),
    )(*ins)
    out = {}
    for k, n in enumerate(names + [t[0] for t in tall]):
        out[n] = tuple(res[2 + 4 * k:6 + 4 * k])
    return res[0], res[1], out


def kernel(x, c, rel_bias, w_ada, b_ada, w_in, b_in, attn_sinks, gmlp_ln_g, gmlp_ln_b, gmlp_w_s, gmlp_b_s, attn_out_g, gmlp_out_g, w_out, ln1_g, ln1_b, w_gate_up, w_down, ln2_g, ln2_b, loss_target, m_rel_bias, m_w_ada, m_b_ada, m_w_in, m_b_in, m_attn_sinks, m_gmlp_ln_g, m_gmlp_ln_b, m_gmlp_w_s, m_gmlp_b_s, m_attn_out_g, m_gmlp_out_g, m_w_out, m_ln1_g, m_ln1_b, m_w_gate_up, m_w_down, m_ln2_g, m_ln2_b, v_rel_bias, v_w_ada, v_b_ada, v_w_in, v_b_in, v_attn_sinks, v_gmlp_ln_g, v_gmlp_ln_b, v_gmlp_w_s, v_gmlp_b_s, v_attn_out_g, v_gmlp_out_g, v_w_out, v_ln1_g, v_ln1_b, v_w_gate_up, v_w_down, v_ln2_g, v_ln2_b):
    ix, iy, ic = _my_pos()
    chip = 2 * ix + iy
    dev = 4 * ix + 2 * iy + ic
    s = x.shape[1]
    xs = x[0]
    tgt = loss_target[0]
    tm_big = min(512, s)
    tm_ffn = min(256, s)
    n_ada = w_ada.shape[2]

    w_in_s, w_out_s = w_in[0].astype(BF16), w_out[0].astype(BF16)
    w_gu_s, w_dn_s = w_gate_up[0].astype(BF16), w_down[0].astype(BF16)
    sc_all, _, mod_rows, w_in_g = _prologue(
        jnp.pad(c, ((0, 7), (0, 0))), w_ada[0], lax.dynamic_slice_in_dim(b_ada, chip * n_ada, n_ada, axis=1), w_in_s)
    mod_all = mod_rows.reshape(N_DEV, N_DEV, -1)
    mod_row = lax.dynamic_index_in_dim(mod_all[0::2], dev, axis=1, keepdims=False)
    modr = jnp.pad(mod_row.reshape(6, D_MODEL), ((0, 2), (0, 0)))
    w_in_g = _insert_own(w_in_g, w_in_s, "blk", chip)
    w_in_f = jnp.transpose(w_in_g, (1, 0, 2)).reshape(D_MODEL, IN_W)

    bucket = _bucket_table()
    bias, wsm = _prep_tables(bucket, rel_bias, gmlp_w_s[0])
    bsx = jnp.repeat(gmlp_b_s[0].T, GROUP_DIM, axis=1)
    amat = _group_mean_matrix()
    sinks = attn_sinks[0]

    (h1, q, kv, gu, gv), (w_out_g, w_dn_g) = _fwd_in(xs, modr, w_in_f, b_in, tm_big, [w_out_s, w_dn_s], ["blk", "blk"])
    w_out_f = _insert_own(w_out_g, w_out_s, "blk", chip).reshape(D_MODEL, D_MODEL)
    x1, y, mixed, (w_gu_g,) = _fwd_mix(
        q, kv, gu, gv, xs, modr, bias, sinks, gmlp_ln_g, gmlp_ln_b, wsm, bsx, amat, attn_out_g, gmlp_out_g, w_out_f,
        ln1_g, ln1_b, tm_big, [w_gu_s], ["blk"])
    assert w_gate_up.shape[2] == FF_CHUNK
    w_gu_f = _insert_own(w_gu_g, w_gu_s, "blk", chip)
    w_dn_f = _insert_own(w_dn_g, w_dn_s, "blk", chip).reshape(D_FF, D_MODEL)
    h2, act, dy2, dx1a, acc_f = _fwd_ffn(x1, tgt, modr, ln2_g, ln2_b, w_gu_f, w_dn_f, tm_ffn)

    a_act, dgu_ff, dh2 = _bwd_ffn(dy2, act, w_gu_f, w_dn_f, tm_ffn)
    g_dn, g_dn_b, _ = _wgrad(a_act, [dy2], D_FF // 2, min(512, s), "wgrad_down")
    g_gu, g_gu_b, _ = _wgrad(h2, [dgu_ff], 512, min(512, s), "wgrad_gate_up")
    blk3 = lambda a, rows: a.reshape(N_CHIPS, rows, a.shape[1])
    (dxa, dy, dmix, acc_m), (got_dn, got_gu) = _bwd_mid(
        dh2, dx1a, x1, xs, y, modr, ln1_g, w_out_f, tm_big, [blk3(g_dn_b, D_FF // N_CHIPS), g_gu_b], ["blk", "cols"])
    g_out, g_out_b, _ = _wgrad(mixed, [dy], 512, min(512, s), "wgrad_out")
    (got_out,) = _swap_halves([blk3(g_out_b, D_MODEL // N_CHIPS)], ["blk"], "rs_swap_out")
    kinds_a = ["blk", "cols", "blk"]
    fulls_a = [blk3(g_dn, D_FF // N_CHIPS), g_gu, blk3(g_out, D_MODEL // N_CHIPS)]
    gots_a = [got_dn, got_gu, got_out]
    parts_a = [_add_halves(f, g, k, "rs_add_a%d" % i) for i, (f, g, k) in enumerate(zip(fulls_a, gots_a, kinds_a))]
    (dq, dkv, dgu, dgv, gbias, dws, dbs, vec, dsink), rxs_a = _bwd_mix(
        q, kv, gu, gv, dmix, bias, sinks, gmlp_ln_g, gmlp_ln_b, wsm, bsx, amat, attn_out_g, gmlp_out_g,
        [p[1] for p in parts_a], kinds_a)
    tall_g = _mix_finalize(gbias, bucket, dws, dbs, dsink)
    grad_x, acc_i, db_in = _bwd_in(dq, dkv, dgu, dgv, dxa, xs, modr, w_in_f, tm_big)

    wide_g = _pack_wide(acc_i, acc_m, acc_f, db_in, vec)
    full_in, full_in_b, (gw, gt) = _wgrad(h1, [dq, dkv, dgu, dgv], 512, min(512, s), "wgrad_in", owner_blocks=True,
                                          gather_vs=[wide_g, tall_g])
    (got_in,) = _swap_halves([full_in_b], ["blk"], "rs_swap_in")
    part_in = _add_halves(full_in, got_in, "blk", "rs_add_in")
    (rx_in,) = _exchange_chip_partials([part_in[1]], ["blk"], "rs_chips_in")
    wide_wmv = {"b_ada": (b_ada, m_b_ada, v_b_ada), "b_in": (b_in, m_b_in, v_b_in),
                "ln1_g": (ln1_g, m_ln1_g, v_ln1_g), "ln1_b": (ln1_b, m_ln1_b, v_ln1_b),
                "ln2_g": (ln2_g, m_ln2_g, v_ln2_g), "ln2_b": (ln2_b, m_ln2_b, v_ln2_b),
                "gmlp_ln_g": (gmlp_ln_g, m_gmlp_ln_g, v_gmlp_ln_g), "gmlp_ln_b": (gmlp_ln_b, m_gmlp_ln_b, v_gmlp_ln_b),
                "attn_out_g": (attn_out_g, m_attn_out_g, v_attn_out_g),
                "gmlp_out_g": (gmlp_out_g, m_gmlp_out_g, v_gmlp_out_g)}
    rows2 = lambda a: a.reshape(-1, a.shape[-1])
    dmod_all, loss_t, small = _adam_small(
        gw, gt, wide_wmv, tuple(rows2(a) for a in (gmlp_w_s, m_gmlp_w_s, v_gmlp_w_s)),
        tuple(rows2(a) for a in (gmlp_b_s, m_gmlp_b_s, v_gmlp_b_s)), (rel_bias, m_rel_bias, v_rel_bias),
        (attn_sinks, m_attn_sinks, v_attn_sinks))
    loss = loss_t[0, 0]

    dmod_cols = lax.dynamic_slice_in_dim(dmod_all, chip * n_ada, n_ada, axis=1)
    g_ada, d_ada, m_ada, v_ada = _adam_w_ada(sc_all.T, dmod_cols, w_ada[0], m_w_ada[0], v_w_ada[0], 256)

    sums = [(parts_a[0][0], rxs_a[0], "blk", 176), (parts_a[1][0], rxs_a[1], "cols", 256),
            (parts_a[2][0], rxs_a[2], "blk", 128), (part_in[0], rx_in, "blk", 256)]
    mine = [_sum_chips(p, rx, k, tr, "rs_sum_%d" % i) for i, (p, rx, k, tr) in enumerate(sums)]
    got = _share_halves(mine, "rs_share")

    gs_dn, d_dn, m_dn, v_dn = _adam_halves(w_down[0], mine[0], got[0], m_w_down[0], v_w_down[0], 176, "adam_w_down")
    gs_gu, d_gu, m_gu, v_gu = _adam_halves(w_gate_up[0], mine[1], got[1], m_w_gate_up[0], v_w_gate_up[0], 256,
                                           "adam_w_gate_up")
    gs_out, d_out, m_out, v_out = _adam_halves(w_out[0], mine[2], got[2], m_w_out[0], v_w_out[0], 128, "adam_w_out")
    gs_in, d_in, m_in, v_in = _adam_halves(w_in[0], mine[3], got[3], m_w_in[0], v_w_in[0], 256, "adam_w_in")

    big = {"w_ada": (g_ada, d_ada, m_ada, v_ada), "w_in": (gs_in, d_in, m_in, v_in), "w_out": (gs_out, d_out, m_out, v_out),
           "w_gate_up": (gs_gu, d_gu, m_gu, v_gu), "w_down": (gs_dn, d_dn, m_dn, v_dn)}
    order = ["rel_bias", "w_ada", "b_ada", "w_in", "b_in", "attn_sinks", "gmlp_ln_g", "gmlp_ln_b", "gmlp_w_s", "gmlp_b_s",
             "attn_out_g", "gmlp_out_g", "w_out", "ln1_g", "ln1_b", "w_gate_up", "w_down", "ln2_g", "ln2_b"]
    shapes = {"gmlp_w_s": gmlp_w_s.shape, "gmlp_b_s": gmlp_b_s.shape}
    outs = [loss, grad_x[None]]
    for k in range(4):
        for name in order:
            if name in big:
                outs.append(big[name][k][None])
            elif name in shapes:
                outs.append(small[name][k].reshape(shapes[name]))
            else:
                outs.append(small[name][k])
    return tuple(outs)
```

```python
import math

import numpy as np
import jax
import jax.numpy as jnp
from jax import lax
from jax.experimental import pallas as pl
from jax.experimental.pallas import tpu as pltpu

F32 = jnp.float32
BF16 = jnp.bfloat16
MESH = pl.DeviceIdType.MESH

D_MODEL = 1024
N_HEADS = 8
N_KV = 2
HEAD_DIM = 64
ATTN_W = N_HEADS * HEAD_DIM
KV_W = N_KV * HEAD_DIM
N_GROUPS = 8
GROUP_DIM = 64
GMLP_W = N_GROUPS * GROUP_DIM
IN_W = ATTN_W + 2 * KV_W + 2 * GMLP_W
BLOCK = 128
N_BUCKETS = 32
MAX_DISTANCE = 128
D_FF = 2816
ALPHA = 2.0 ** 0.25
LN_EPS = 1e-5
NEG_INF = -1e30
ADAM_LR, ADAM_B1, ADAM_B2, ADAM_EPS, ADAM_WD, ADAM_STEP = 0.001, 0.9, 0.999, 1e-8, 0.01, 10
N_CHIPS = 4
N_DEV = 8
LANES = 128
V7X_VMEM_LIMIT = 56 * 2 ** 20
GELU_C = math.sqrt(2.0 / math.pi)
Q_SCALE = HEAD_DIM ** -0.5
ANY = pl.BlockSpec(memory_space=pl.ANY)

TALL_BS = N_GROUPS * BLOCK
TALL_RB = TALL_BS + 8
TALL_SK = TALL_RB + N_BUCKETS
TALL_ROWS = TALL_SK + 8
WIDE_W = 6 * D_MODEL
WIDE_LAYOUT = {
    "b_ada": (0, 0, 6 * D_MODEL),
    "b_in": (1, 0, IN_W), "ln1_g": (1, IN_W, D_MODEL), "ln1_b": (1, IN_W + D_MODEL, D_MODEL),
    "ln2_g": (1, IN_W + 2 * D_MODEL, D_MODEL), "ln2_b": (1, IN_W + 3 * D_MODEL, D_MODEL),
    "gmlp_ln_g": (2, 0, GMLP_W), "gmlp_ln_b": (2, GMLP_W, GMLP_W), "attn_out_g": (2, 2 * GMLP_W, ATTN_W),
    "gmlp_out_g": (2, 2 * GMLP_W + ATTN_W, GMLP_W), "loss": (2, 3 * GMLP_W + ATTN_W, D_MODEL)}
WIDE_PARAMS = tuple(n for n in WIDE_LAYOUT if n != "loss")


def _params(sem=None):
    return pltpu.CompilerParams(dimension_semantics=sem, vmem_limit_bytes=V7X_VMEM_LIMIT)


def _const_spec(shape, single=False):
    nd = len(shape)
    if single:
        return pl.BlockSpec(shape, lambda *_: (0,) * nd, pipeline_mode=pl.Buffered(1))
    return pl.BlockSpec(shape, lambda *_: (0,) * nd)


def _dot(a, b):
    return jnp.dot(a, b, preferred_element_type=F32)


def _dot_nt(a, b):
    return lax.dot_general(a, b, (((1,), (1,)), ((), ())), preferred_element_type=F32)


def _dot_tn(a, b):
    return lax.dot_general(a, b, (((0,), (0,)), ((), ())), preferred_element_type=F32)


def _gelu(x):
    t = jnp.tanh(GELU_C * (x + 0.044715 * x * x * x))
    return 0.5 * x * (1.0 + t), t


def _gelu_grad(x, t):
    return 0.5 * (1.0 + t) + 0.5 * x * (1.0 - t * t) * GELU_C * (1.0 + 3.0 * 0.044715 * x * x)


def _split_dot(x, a):
    hi = x.astype(BF16)
    lo = (x - hi.astype(F32)).astype(BF16)
    return _dot(hi, a) + _dot(lo, a)


def _group_mean_matrix():
    g = np.arange(GMLP_W) // GROUP_DIM
    return jnp.asarray((g[:, None] == g[None, :]).astype(np.float32) / GROUP_DIM, dtype=BF16)


def _ln_stats(z):
    mu = jnp.mean(z, axis=-1, keepdims=True)
    d = z - mu
    var = jnp.mean(d * d, axis=-1, keepdims=True)
    rstd = lax.rsqrt(var + LN_EPS)
    return d * rstd, rstd


def _ln_bwd(dxhat, xhat, rstd):
    m1 = jnp.mean(dxhat, axis=-1, keepdims=True)
    m2 = jnp.mean(dxhat * xhat, axis=-1, keepdims=True)
    return rstd * (dxhat - m1 - xhat * m2)


def _colsum(x):
    return jnp.sum(x, axis=0, keepdims=True)


def _my_pos():
    return lax.axis_index("x"), lax.axis_index("y"), lax.axis_index("c")


def _other_chips(x, y):
    return [(1 - x, y), (x, 1 - y), (1 - x, 1 - y)]


def _chip_index_scalar():
    ix, iy, _ = _my_pos()
    return jnp.reshape(2 * ix + iy, (1,)).astype(jnp.int32)


def _core_index_scalar():
    return jnp.reshape(lax.axis_index("c"), (1,)).astype(jnp.int32)


class _Gather8:
    def __init__(self, x_refs, out_refs, send_sems, recv_sems, local_sems):
        self.x_refs, self.out_refs = x_refs, out_refs
        self.send_sems, self.recv_sems, self.local_sems = send_sems, recv_sems, local_sems
        self.x, self.y, self.c = _my_pos()
        self.me, self.sibling = (self.x, self.y, self.c), (self.x, self.y, 1 - self.c)
        self.chips = _other_chips(self.x, self.y)

    def _rows(self, a, px, py, pc):
        m_per = self.x_refs[a].shape[0]
        return self.out_refs[a].at[pl.ds((4 * px + 2 * py + pc) * m_per, m_per), :]

    def _copy(self, a, k, block, to, src=None):
        return pltpu.make_async_remote_copy(
            src_ref=self._rows(a, *block) if src is None else src, dst_ref=self._rows(a, *block),
            send_sem=self.send_sems.at[7 * a + k], recv_sem=self.recv_sems.at[7 * a + k], device_id=to,
            device_id_type=MESH)

    def _local(self, a):
        return pltpu.make_async_copy(self.x_refs[a], self._rows(a, *self.me), self.local_sems.at[a])

    def start(self):
        for a in range(len(self.x_refs)):
            self._local(a).start()
            self._copy(a, 0, self.me, self.sibling, src=self.x_refs[a]).start()
            for j, chip in enumerate(self.chips):
                self._copy(a, 1 + j, self.me, (*chip, self.c), src=self.x_refs[a]).start()

    def forward(self):
        for a in range(len(self.x_refs)):
            for j, chip in enumerate(self.chips):
                self._copy(a, 1 + j, (*chip, self.c), self.me).wait_recv()
                self._copy(a, 4 + j, (*chip, self.c), self.sibling).start()

    def finish(self):
        for a in range(len(self.x_refs)):
            self._copy(a, 0, self.sibling, self.me).wait_recv()
            for j, chip in enumerate(self.chips):
                self._copy(a, 4 + j, (*chip, 1 - self.c), self.me).wait_recv()
        for a in range(len(self.x_refs)):
            for k in range(7):
                self._copy(a, k, self.me, self.me).wait_send()
            self._local(a).wait()

    @staticmethod
    def sems(n_v):
        return [pltpu.SemaphoreType.DMA((7 * n_v,)), pltpu.SemaphoreType.DMA((7 * n_v,)),
                pltpu.SemaphoreType.DMA((n_v,))]


def _gathered8_shapes(vs):
    return [jax.ShapeDtypeStruct((N_DEV * v.shape[0], v.shape[1]), v.dtype) for v in vs]


VMEM_WHOLE = pl.BlockSpec(memory_space=pltpu.VMEM)


def _prologue(c_pad, w_ada_s, b_ada_s, w_in_s):
    n = w_ada_s.shape[1]

    def body(c_ref, w_ref, b_ref, win_ref, sc_ref, modc_ref, modg_ref, wing_ref, call_ref, *sems):
        weights = _WeightGather([win_ref], [wing_ref], ["blk"], sems[0], sems[1])
        gather_c = _Gather8([c_ref], [call_ref], sems[2], sems[3], sems[4])
        gather_mod = _Gather8([modc_ref], [modg_ref], sems[5], sems[6], sems[7])
        weights.start()
        gather_c.start()
        gather_c.forward()
        gather_c.finish()
        cv = call_ref[...]
        sc = cv * _sigmoid(cv)
        a_hi = sc.astype(BF16)
        a_lo = (sc - a_hi.astype(F32)).astype(BF16)
        w = w_ref[...]
        w_hi = w.astype(BF16)
        w_lo = (w - w_hi.astype(F32)).astype(BF16)
        mod = _dot(a_hi, w_hi) + _dot(a_hi, w_lo) + _dot(a_lo, w_hi) + b_ref[...]
        for d in range(N_DEV):
            sc_ref[d:d + 1, :] = sc[8 * d:8 * d + 1, :]
            modc_ref[d:d + 1, :] = mod[8 * d:8 * d + 1, :]
        gather_mod.start()
        gather_mod.forward()
        gather_mod.finish()
        weights.forward()
        weights.finish()

    return pl.pallas_call(
        body, name="prologue",
        out_shape=(jax.ShapeDtypeStruct((N_DEV, D_MODEL), F32), jax.ShapeDtypeStruct((N_DEV, n), F32),
                   jax.ShapeDtypeStruct((N_DEV * N_DEV, n), F32),
                   jax.ShapeDtypeStruct(_gathered_shape(w_in_s, "blk"), BF16)),
        in_specs=[VMEM_WHOLE, VMEM_WHOLE, VMEM_WHOLE, ANY],
        out_specs=(VMEM_WHOLE, VMEM_WHOLE, VMEM_WHOLE, ANY),
        scratch_shapes=[pltpu.VMEM((N_DEV * 8, D_MODEL), F32)] + _WeightGather.sems(1) + _Gather8.sems(1)
        + _Gather8.sems(1),
        compiler_params=pltpu.CompilerParams(vmem_limit_bytes=V7X_VMEM_LIMIT),
    )(c_pad, w_ada_s, b_ada_s, w_in_s)


def _gathered_shape(shard, kind):
    r, cc = shard.shape
    return (N_CHIPS, r, cc) if kind == "blk" else (r, N_CHIPS * cc)


class _WeightGather:
    def __init__(self, shards, gathered, kinds, send_sems, recv_sems):
        self.shards, self.gathered, self.kinds = shards, gathered, kinds
        self.send_sems, self.recv_sems = send_sems, recv_sems
        self.x, self.y, self.c = _my_pos()
        self.chips = _other_chips(self.x, self.y)

    def _dst(self, a, chip, pc):
        r, cc = self.shards[a].shape
        h = r // 2
        g = self.gathered[a]
        if self.kinds[a] == "blk":
            return g.at[chip, pl.ds(pc * h, h), :]
        return g.at[pl.ds(pc * h, h), pl.ds(chip * cc, cc)]

    def _copy(self, a, k, chip, pc, to, src=None):
        d = self._dst(a, chip, pc)
        return pltpu.make_async_remote_copy(
            src_ref=d if src is None else src, dst_ref=d, send_sem=self.send_sems.at[a * 6 + k],
            recv_sem=self.recv_sems.at[a * 6 + k], device_id=to, device_id_type=MESH)

    def _each(self):
        for a in range(len(self.shards)):
            for j, chip in enumerate(self.chips):
                yield a, j, chip, 2 * chip[0] + chip[1]

    def start(self):
        my_chip = 2 * self.x + self.y
        for a, j, chip, _ in self._each():
            h = self.shards[a].shape[0] // 2
            self._copy(a, j, my_chip, self.c, (*chip, self.c), src=self.shards[a].at[pl.ds(self.c * h, h), :]).start()

    def forward(self):
        me, sibling = (self.x, self.y, self.c), (self.x, self.y, 1 - self.c)
        for a, j, chip, cj in self._each():
            self._copy(a, j, cj, self.c, me).wait_recv()
            self._copy(a, 3 + j, cj, self.c, sibling).start()

    def finish(self):
        me = (self.x, self.y, self.c)
        for a, j, chip, cj in self._each():
            self._copy(a, 3 + j, cj, 1 - self.c, me).wait_recv()
        for a, j, chip, cj in self._each():
            self._copy(a, j, cj, self.c, me).wait_send()
            self._copy(a, 3 + j, cj, self.c, me).wait_send()

    @staticmethod
    def sems(n_arr):
        return [pltpu.SemaphoreType.DMA((n_arr * 6,)), pltpu.SemaphoreType.DMA((n_arr * 6,))]


def _insert_own(gathered, shard, kind, chip):
    if kind == "blk":
        return lax.dynamic_update_slice(gathered, shard[None], (chip, 0, 0))
    return lax.dynamic_update_slice(gathered, shard, (0, chip * shard.shape[1]))


def _half_of_full(ref, kind, pc):
    if kind == "blk":
        h = ref.shape[1] // 2
        return ref.at[:, pl.ds(pc * h, h), :]
    h = ref.shape[0] // 2
    return ref.at[pl.ds(pc * h, h), :]


def _half_shape(shape, kind):
    return (shape[0], shape[1] // 2, shape[2]) if kind == "blk" else (shape[0] // 2, shape[1])


class _HalfSwap:
    def __init__(self, ins, outs, kinds, send_sems, recv_sems):
        self.ins, self.outs, self.kinds = ins, outs, kinds
        self.send_sems, self.recv_sems = send_sems, recv_sems
        self.x, self.y, self.c = _my_pos()

    def _copies(self):
        for a in range(len(self.ins)):
            yield pltpu.make_async_remote_copy(
                src_ref=_half_of_full(self.ins[a], self.kinds[a], 1 - self.c), dst_ref=self.outs[a],
                send_sem=self.send_sems.at[a], recv_sem=self.recv_sems.at[a],
                device_id=(self.x, self.y, 1 - self.c), device_id_type=MESH)

    def start(self):
        for cp in self._copies():
            cp.start()

    def wait(self):
        for cp in self._copies():
            cp.wait()

    @staticmethod
    def sems(n_arr):
        return [pltpu.SemaphoreType.DMA((n_arr,)), pltpu.SemaphoreType.DMA((n_arr,))]

    @staticmethod
    def out_shapes(fulls, kinds):
        return [jax.ShapeDtypeStruct(_half_shape(a.shape, k), a.dtype) for a, k in zip(fulls, kinds)]


def _swap_halves(fulls_bf16, kinds, name):
    n_arr = len(fulls_bf16)

    def body(*refs):
        swap = _HalfSwap(refs[:n_arr], refs[n_arr:2 * n_arr], kinds, *refs[2 * n_arr:])
        swap.start()
        swap.wait()

    return pl.pallas_call(
        body, name=name, out_shape=_HalfSwap.out_shapes(fulls_bf16, kinds),
        in_specs=[ANY] * n_arr, out_specs=[ANY] * n_arr, scratch_shapes=_HalfSwap.sems(n_arr),
    )(*fulls_bf16)


def _add_halves(full, got, kind, name):
    hs = _half_shape(full.shape, kind)

    def body(c_ref, a_ref, b_ref, o_ref, ob_ref):
        p = a_ref[...] + b_ref[...].astype(F32)
        o_ref[...] = p
        ob_ref[...] = p.astype(BF16)

    if kind == "blk":
        nb, h, cc = hs
        own = pl.BlockSpec((1, h, cc), lambda b, c_ref: (b, c_ref[0], 0))
        other = pl.BlockSpec((1, h, cc), lambda b, c_ref: (b, 0, 0))
    else:
        h, cc = hs[0], hs[1] // N_CHIPS
        own = pl.BlockSpec((h, cc), lambda b, c_ref: (c_ref[0], b))
        other = pl.BlockSpec((h, cc), lambda b, c_ref: (0, b))
    return pl.pallas_call(
        body, name=name, out_shape=(jax.ShapeDtypeStruct(hs, F32), jax.ShapeDtypeStruct(hs, BF16)),
        grid_spec=pltpu.PrefetchScalarGridSpec(
            num_scalar_prefetch=1, grid=(N_CHIPS,), in_specs=[own, other], out_specs=(other, other)),
        compiler_params=_params(("arbitrary",)),
    )(_core_index_scalar(), full, got)


def _rx_shape(part_shape, kind):
    if kind == "blk":
        return (3, part_shape[1], part_shape[2])
    return (3, part_shape[0], part_shape[1] // N_CHIPS)


class _ChipExchange:
    def __init__(self, parts, rxs, kinds, send_sems, recv_sems):
        self.parts, self.rxs, self.kinds = parts, rxs, kinds
        self.send_sems, self.recv_sems = send_sems, recv_sems
        self.x, self.y, self.c = _my_pos()
        self.chips = _other_chips(self.x, self.y)

    def _copies(self):
        for a in range(len(self.parts)):
            for j, chip in enumerate(self.chips):
                cj = 2 * chip[0] + chip[1]
                if self.kinds[a] == "blk":
                    src = self.parts[a].at[cj]
                else:
                    cc = self.parts[a].shape[1] // N_CHIPS
                    src = self.parts[a].at[:, pl.ds(cj * cc, cc)]
                yield pltpu.make_async_remote_copy(
                    src_ref=src, dst_ref=self.rxs[a].at[j], send_sem=self.send_sems.at[a * 3 + j],
                    recv_sem=self.recv_sems.at[a * 3 + j], device_id=(*chip, self.c), device_id_type=MESH)

    def start(self):
        for cp in self._copies():
            cp.start()

    def wait(self):
        for cp in self._copies():
            cp.wait_recv()
        for cp in self._copies():
            cp.wait_send()

    @staticmethod
    def sems(n_arr):
        return [pltpu.SemaphoreType.DMA((n_arr * 3,)), pltpu.SemaphoreType.DMA((n_arr * 3,))]


def _exchange_chip_partials(parts, kinds, name):
    n_arr = len(parts)

    def body(*refs):
        exchange = _ChipExchange(refs[:n_arr], refs[n_arr:2 * n_arr], kinds, *refs[2 * n_arr:])
        exchange.start()
        exchange.wait()

    return pl.pallas_call(
        body, name=name,
        out_shape=[jax.ShapeDtypeStruct(_rx_shape(p.shape, k), BF16) for p, k in zip(parts, kinds)],
        in_specs=[ANY] * n_arr, out_specs=[ANY] * n_arr, scratch_shapes=_ChipExchange.sems(n_arr),
    )(*parts)


def _sum_chips(part, rx, kind, tr, name):
    _, h, cc = rx.shape
    flips = (2, 1, 3)

    def body(chip_ref, p_ref, rx_ref, o_ref):
        own = p_ref[...].reshape(tr, cc)
        for mc in range(N_CHIPS):
            @pl.when(chip_ref[0] == mc)
            def _():
                terms = sorted([(mc, None)] + [(mc ^ f, j) for j, f in enumerate(flips)])
                acc = None
                for _, j in terms:
                    t = own if j is None else rx_ref[j].astype(F32)
                    acc = t if acc is None else acc + t
                o_ref[...] = acc

    if kind == "blk":
        own_spec = pl.BlockSpec((1, tr, cc), lambda i, chip_ref: (chip_ref[0], i, 0))
    else:
        own_spec = pl.BlockSpec((tr, cc), lambda i, chip_ref: (i, chip_ref[0]))
    return pl.pallas_call(
        body, name=name, out_shape=jax.ShapeDtypeStruct((h, cc), F32),
        grid_spec=pltpu.PrefetchScalarGridSpec(
            num_scalar_prefetch=1, grid=(h // tr,),
            in_specs=[own_spec, pl.BlockSpec((3, tr, cc), lambda i, chip_ref: (0, i, 0))],
            out_specs=pl.BlockSpec((tr, cc), lambda i, chip_ref: (i, 0))),
        compiler_params=_params(("arbitrary",)),
    )(_chip_index_scalar(), part, rx)


def _share_halves(halves, name):
    n_arr = len(halves)

    def body(*refs):
        ins, outs = refs[:n_arr], refs[n_arr:2 * n_arr]
        send_sems, recv_sems = refs[2 * n_arr:]
        x, y, c = _my_pos()
        cps = []
        for a in range(n_arr):
            cp = pltpu.make_async_remote_copy(
                src_ref=ins[a], dst_ref=outs[a], send_sem=send_sems.at[a], recv_sem=recv_sems.at[a],
                device_id=(x, y, 1 - c), device_id_type=MESH)
            cp.start()
            cps.append(cp)
        for cp in cps:
            cp.wait()

    return pl.pallas_call(
        body, name=name, out_shape=[jax.ShapeDtypeStruct(h.shape, h.dtype) for h in halves],
        in_specs=[ANY] * n_arr, out_specs=[ANY] * n_arr,
        scratch_shapes=[pltpu.SemaphoreType.DMA((n_arr,)), pltpu.SemaphoreType.DMA((n_arr,))],
    )(*halves)


def _bucket_table():
    qi = jnp.arange(BLOCK)[:, None]
    si = jnp.arange(2 * BLOCK)[None, :]
    dist = qi + BLOCK - si
    max_exact = N_BUCKETS // 2
    n = jnp.maximum(dist, 0)
    nf = jnp.maximum(n, max_exact).astype(F32)
    large = max_exact + (jnp.log(nf / max_exact) / math.log(MAX_DISTANCE / max_exact)
                         * (N_BUCKETS - max_exact)).astype(jnp.int32)
    large = jnp.minimum(large, N_BUCKETS - 1)
    return jnp.where(n < max_exact, n, large).astype(F32)


def _prep_tables(bucket, rel_bias, w_s):
    def body(bucket_ref, rb_ref, ws_ref, bias_ref, wsm_ref):
        qi = lax.broadcasted_iota(jnp.int32, (BLOCK, 2 * BLOCK), 0)
        si = lax.broadcasted_iota(jnp.int32, (BLOCK, 2 * BLOCK), 1)
        dist = qi + BLOCK - si
        in_window = (dist >= 0) & (dist < BLOCK)
        bk = bucket_ref[...]
        for h in range(N_HEADS):
            acc = jnp.zeros((BLOCK, 2 * BLOCK), F32)
            for b in range(N_BUCKETS):
                acc = jnp.where(bk == float(b), rb_ref[b, h], acc)
            bias_ref[h] = jnp.where(in_window, acc, NEG_INF)
        ti = lax.broadcasted_iota(jnp.int32, (BLOCK, BLOCK), 0)
        ui = lax.broadcasted_iota(jnp.int32, (BLOCK, BLOCK), 1)
        for g in range(N_GROUPS):
            wsm_ref[g] = jnp.where(ti >= ui, ws_ref[g], 0.0).astype(BF16)

    return pl.pallas_call(
        body, name="prep_tables",
        out_shape=(jax.ShapeDtypeStruct((N_HEADS, BLOCK, 2 * BLOCK), F32),
                   jax.ShapeDtypeStruct((N_GROUPS, BLOCK, BLOCK), BF16)),
        grid=(1,),
        in_specs=[_const_spec((BLOCK, 2 * BLOCK)), pl.BlockSpec(memory_space=pltpu.SMEM),
                  _const_spec((N_GROUPS, BLOCK, BLOCK))],
        out_specs=(_const_spec((N_HEADS, BLOCK, 2 * BLOCK)), _const_spec((N_GROUPS, BLOCK, BLOCK))),
        compiler_params=_params(("arbitrary",)),
    )(bucket, rel_bias, w_s)


def _fwd_in(x, modr, w_in, b_in, tm, shards, kinds):
    s = x.shape[0]
    n_steps = s // tm
    fwd_step = (3 * n_steps) // 4
    n_w = len(shards)

    def body(x_ref, mod_ref, w_ref, b_ref, *rest):
        shard_refs = rest[:n_w]
        h1_ref, q_ref, kv_ref, gu_ref, gv_ref = rest[n_w:n_w + 5]
        gathered_refs = rest[n_w + 5:2 * n_w + 5]
        send_sems, recv_sems = rest[2 * n_w + 5:]
        i = pl.program_id(0)
        gather = _WeightGather(shard_refs, gathered_refs, kinds, send_sems, recv_sems)

        @pl.when(i == 0)
        def _():
            gather.start()

        h1 = (x_ref[...] * (1.0 + mod_ref[1:2, :]) + mod_ref[0:1, :]).astype(BF16)
        h1_ref[...] = h1
        proj = _dot(h1, w_ref[...]) + b_ref[...]
        q_ref[...] = (proj[:, :ATTN_W] * Q_SCALE).astype(BF16)
        kv_ref[...] = proj[:, ATTN_W:ATTN_W + 2 * KV_W].astype(BF16)
        gu_ref[...] = proj[:, ATTN_W + 2 * KV_W:ATTN_W + 2 * KV_W + GMLP_W]
        gv_ref[...] = proj[:, ATTN_W + 2 * KV_W + GMLP_W:]

        @pl.when(i == fwd_step)
        def _():
            gather.forward()

        @pl.when(i == n_steps - 1)
        def _():
            gather.finish()

    row = lambda w: pl.BlockSpec((tm, w), lambda i: (i, 0))
    outs = pl.pallas_call(
        body, name="fwd_in",
        out_shape=[jax.ShapeDtypeStruct((s, D_MODEL), BF16), jax.ShapeDtypeStruct((s, ATTN_W), BF16),
                   jax.ShapeDtypeStruct((s, 2 * KV_W), BF16), jax.ShapeDtypeStruct((s, GMLP_W), F32),
                   jax.ShapeDtypeStruct((s, GMLP_W), F32)]
        + [jax.ShapeDtypeStruct(_gathered_shape(sh, k), BF16) for sh, k in zip(shards, kinds)],
        grid=(n_steps,),
        in_specs=[row(D_MODEL), _const_spec((8, D_MODEL)), _const_spec((D_MODEL, IN_W)), _const_spec((1, IN_W))]
        + [ANY] * n_w,
        out_specs=[row(D_MODEL), row(ATTN_W), row(2 * KV_W), row(GMLP_W), row(GMLP_W)] + [ANY] * n_w,
        scratch_shapes=_WeightGather.sems(n_w),
        compiler_params=_params(("arbitrary",)),
    )(x, modr, w_in, b_in, *shards)
    return outs[:5], outs[5:]


def _kv_variants(kk):
    kf = kk.astype(F32)
    lane = lax.broadcasted_iota(jnp.int32, kf.shape, 1)
    low = lane < HEAD_DIM
    k0_lo = jnp.where(low, kf, 0.0)
    k1_hi = jnp.where(low, 0.0, kf)
    k0_hi = pltpu.roll(k0_lo, HEAD_DIM, 1)
    k1_lo = pltpu.roll(k1_hi, HEAD_DIM, 1)
    return ((k0_lo.astype(BF16), k0_hi.astype(BF16)), (k1_lo.astype(BF16), k1_hi.astype(BF16)))


def _head_kv(h):
    return h // (N_HEADS // N_KV), h % 2


MIX_GROUP = 2


def _interleave(*gens):
    results = [None] * len(gens)
    active = list(enumerate(gens))
    while active:
        still = []
        for i, g in active:
            try:
                next(g)
                still.append((i, g))
            except StopIteration as done:
                results[i] = done.value
        active = still
    return results


def _attn_block_fwd(q_blk, kk, vv, bias_ref, sinks_ref, first_mask):
    kvar = _kv_variants(kk)
    vvar = _kv_variants(vv)
    heads = range(N_HEADS)
    q_pairs = [q_blk[:, (h // 2) * LANES:(h // 2 + 1) * LANES] for h in heads]
    logits = [_dot_nt(q_pairs[h], kvar[_head_kv(h)[0]][_head_kv(h)[1]]) + bias_ref[h] for h in heads]
    if first_mask is not None:
        logits = [jnp.where(first_mask, NEG_INF, lg) for lg in logits]
    yield
    ms = [jnp.maximum(jnp.max(logits[h], axis=-1, keepdims=True), sinks_ref[h]) for h in heads]
    yield
    es = [jnp.exp(logits[h] - ms[h]) for h in heads]
    ess = [jnp.exp(sinks_ref[h] - ms[h]) for h in heads]
    yield
    invs = [1.0 / (jnp.sum(es[h], axis=-1, keepdims=True) + ess[h]) for h in heads]
    probs = [(es[h] * invs[h], ess[h] * invs[h]) for h in heads]
    yield
    outs = [_dot(probs[h][0].astype(BF16), vvar[_head_kv(h)[0]][_head_kv(h)[1]]) for h in heads]
    pairs = [outs[2 * i] + outs[2 * i + 1] for i in range(N_HEADS // 2)]
    return jnp.concatenate(pairs, axis=1), probs, kvar, vvar


def _gmlp_chunk_fwd(gu, gv, ln_g, ln_b, wsm_ref, bsx, amat):
    u, tu = _gelu(gu)
    a, ta = _gelu(gv)
    yield
    mean = _split_dot(a, amat)
    d = a - mean
    yield
    var = _split_dot(d * d, amat)
    yield
    rstd = lax.rsqrt(var + LN_EPS)
    xhat = d * rstd
    vb = (xhat * ln_g + ln_b).astype(BF16)
    yield
    lane = lax.broadcasted_iota(jnp.int32, (BLOCK, LANES), 1)
    low = lane < GROUP_DIM
    cols = []
    for pair in range(N_GROUPS // 2):
        vp = vb[:, pair * LANES:(pair + 1) * LANES]
        cols.append(jnp.where(low, _dot(wsm_ref[2 * pair], vp), _dot(wsm_ref[2 * pair + 1], vp)))
    mixedv = jnp.concatenate(cols, axis=1) + bsx
    return u * mixedv, (u, tu, ta, xhat, rstd, vb, mixedv)


def _rms(a, g):
    r = lax.rsqrt(jnp.mean(a * a, axis=-1, keepdims=True) + LN_EPS)
    return a * r * g, r


def _fwd_mix(q, kv, gu, gv, x, modr, bias, sinks, gln_g, gln_b, wsm, bsx, amat, aog, gog, w_out, ln1_g, ln1_b, tm,
             ffn_shards, ffn_kinds):
    s = x.shape[0]
    nb = tm // BLOCK
    n_steps = s // tm
    fwd_step = (3 * n_steps) // 4
    n_w = len(ffn_shards)

    def body(q_ref, kv_ref, kvp_ref, gu_ref, gv_ref, x_ref, mod_ref, bias_ref, sinks_ref, glng_ref, glnb_ref, wsm_ref,
             bsx_ref, amat_ref, aog_ref, gog_ref, wout_ref, ln1g_ref, ln1b_ref, *rest):
        shard_refs = rest[:n_w]
        x1_ref, y_ref, mixed_ref = rest[n_w:n_w + 3]
        gathered_refs = rest[n_w + 3:2 * n_w + 3]
        mix_scr, send_sems, recv_sems = rest[2 * n_w + 3:]
        i = pl.program_id(0)
        gather = _WeightGather(shard_refs, gathered_refs, ffn_kinds, send_sems, recv_sems)

        @pl.when(i == 0)
        def _():
            gather.start()

        col = lax.broadcasted_iota(jnp.int32, (BLOCK, 2 * BLOCK), 1)
        for b0 in range(0, nb, MIX_GROUP):
            gens = []
            for b in range(b0, min(b0 + MIX_GROUP, nb)):
                r0 = b * BLOCK
                if b == 0:
                    kvprev = kvp_ref[...]
                    first_mask = (col < BLOCK) & (i == 0)
                else:
                    kvprev = kv_ref[r0 - BLOCK:r0, :]
                    first_mask = None
                kvcur = kv_ref[r0:r0 + BLOCK, :]
                kk = jnp.concatenate([kvprev[:, :KV_W], kvcur[:, :KV_W]], axis=0)
                vv = jnp.concatenate([kvprev[:, KV_W:], kvcur[:, KV_W:]], axis=0)
                gens.append(_attn_block_fwd(q_ref[r0:r0 + BLOCK, :], kk, vv, bias_ref, sinks_ref, first_mask))
                gens.append(_gmlp_chunk_fwd(gu_ref[r0:r0 + BLOCK, :], gv_ref[r0:r0 + BLOCK, :], glng_ref[...],
                                            glnb_ref[...], wsm_ref, bsx_ref[...], amat_ref[...]))
            res = _interleave(*gens)
            for k, b in enumerate(range(b0, min(b0 + MIX_GROUP, nb))):
                r0 = b * BLOCK
                na, _ = _rms(res[2 * k][0], aog_ref[...])
                ng, _ = _rms(res[2 * k + 1][0], gog_ref[...])
                mix_scr[r0:r0 + BLOCK, :ATTN_W] = na.astype(BF16)
                mix_scr[r0:r0 + BLOCK, ATTN_W:] = ng.astype(BF16)
        mixed = mix_scr[...]
        mixed_ref[...] = mixed
        y = _dot(mixed, wout_ref[...])
        y_ref[...] = y
        z1 = ALPHA * x_ref[...] + mod_ref[2:3, :] * y
        xhat, _ = _ln_stats(z1)
        x1_ref[...] = xhat * ln1g_ref[...] + ln1b_ref[...]

        @pl.when(i == fwd_step)
        def _():
            gather.forward()

        @pl.when(i == n_steps - 1)
        def _():
            gather.finish()

    row = lambda w: pl.BlockSpec((tm, w), lambda i: (i, 0))
    prev = pl.BlockSpec((BLOCK, 2 * KV_W), lambda i: (jnp.maximum(i * nb - 1, 0), 0))
    outs = pl.pallas_call(
        body, name="fwd_mix",
        out_shape=[jax.ShapeDtypeStruct((s, D_MODEL), F32), jax.ShapeDtypeStruct((s, D_MODEL), F32),
                   jax.ShapeDtypeStruct((s, D_MODEL), BF16)]
        + [jax.ShapeDtypeStruct(_gathered_shape(sh, k), BF16) for sh, k in zip(ffn_shards, ffn_kinds)],
        grid=(n_steps,),
        in_specs=[row(ATTN_W), row(2 * KV_W), prev, row(GMLP_W), row(GMLP_W), row(D_MODEL), _const_spec((8, D_MODEL)),
                  _const_spec((N_HEADS, BLOCK, 2 * BLOCK)), pl.BlockSpec(memory_space=pltpu.SMEM),
                  _const_spec((1, GMLP_W)), _const_spec((1, GMLP_W)), _const_spec((N_GROUPS, BLOCK, BLOCK)),
                  _const_spec((BLOCK, GMLP_W)), _const_spec((GMLP_W, GMLP_W)), _const_spec((1, ATTN_W)),
                  _const_spec((1, GMLP_W)), _const_spec((D_MODEL, D_MODEL)), _const_spec((1, D_MODEL)),
                  _const_spec((1, D_MODEL))] + [ANY] * n_w,
        out_specs=[row(D_MODEL), row(D_MODEL), row(D_MODEL)] + [ANY] * n_w,
        scratch_shapes=[pltpu.VMEM((tm, D_MODEL), BF16)] + _WeightGather.sems(n_w),
        compiler_params=_params(("arbitrary",)),
    )(q, kv, kv, gu, gv, x, modr, bias, sinks, gln_g, gln_b, wsm, bsx, amat, aog, gog, w_out, ln1_g, ln1_b, *ffn_shards)
    return outs[0], outs[1], outs[2], outs[3:]


FF_BLOCKS = N_CHIPS // 2
FF_CHUNK = D_FF // FF_BLOCKS
FFN_SUB = 256


def _sigmoid(x):
    return 1.0 / (1.0 + jnp.exp(-x))


def _fwd_ffn(x1, target, modr, ln2_g, ln2_b, w_gu, w_dn, tm):
    s = x1.shape[0]
    n_tiles = s // tm

    def body(x1_ref, x1p_ref, tp_ref, mod_ref, g_ref, b_ref, wgu_ref, wdn_ref, h2_ref, act_ref, dy2_ref, dx1a_ref,
             acc_ref, y2_scr):
        i = pl.program_id(0)

        @pl.when(i == 0)
        def _():
            acc_ref[...] = jnp.zeros_like(acc_ref)

        def matmuls():
            h2 = (x1_ref[...] * (1.0 + mod_ref[4:5, :]) + mod_ref[3:4, :]).astype(BF16)
            h2_ref[...] = h2
            yield
            y2 = None
            for cc in range(FF_BLOCKS):
                c0 = cc * FF_CHUNK
                gate = _dot(h2, wgu_ref[cc])
                up = _dot(h2, wgu_ref[FF_BLOCKS + cc])
                act_ref[:, c0:c0 + FF_CHUNK] = gate.astype(BF16)
                act_ref[:, D_FF + c0:D_FF + c0 + FF_CHUNK] = up.astype(BF16)
                a = (gate * _sigmoid(gate) * up).astype(BF16)
                yield
                part = _dot(a, wdn_ref[c0:c0 + FF_CHUNK, :])
                y2 = part if y2 is None else y2 + part
                yield
            return y2

        def epilogue():
            y2 = y2_scr[...]
            g2 = mod_ref[5:6, :]
            z2 = ALPHA * x1p_ref[...] + g2 * y2
            xhat, rstd = _ln_stats(z2)
            yield
            gain = g_ref[...]
            diff = xhat * gain + b_ref[...] - tp_ref[...]
            dx2 = diff * (1.0 / D_MODEL)
            dxh = dx2 * gain
            yield
            dz2 = _ln_bwd(dxh, xhat, rstd)
            yield
            dx1a_ref[...] = ALPHA * dz2
            dy2_ref[...] = (g2 * dz2).astype(BF16)
            acc_ref[0:1, :] += _colsum(diff * diff)
            acc_ref[1:2, :] += _colsum(dx2 * xhat)
            acc_ref[2:3, :] += _colsum(dx2)
            acc_ref[3:4, :] += _colsum(dz2 * y2)

        @pl.when(i == 0)
        def _():
            (y2,) = _interleave(matmuls())
            y2_scr[...] = y2

        @pl.when((i > 0) & (i < n_tiles))
        def _():
            y2, _ = _interleave(matmuls(), epilogue())
            y2_scr[...] = y2

        @pl.when(i == n_tiles)
        def _():
            _interleave(epilogue())

    last = n_tiles - 1
    cur = lambda w: pl.BlockSpec((tm, w), lambda i: (jnp.minimum(i, last), 0))
    late = lambda w: pl.BlockSpec((tm, w), lambda i: (jnp.maximum(i - 1, 0), 0))
    return pl.pallas_call(
        body, name="fwd_ffn",
        out_shape=(jax.ShapeDtypeStruct((s, D_MODEL), BF16), jax.ShapeDtypeStruct((s, 2 * D_FF), BF16),
                   jax.ShapeDtypeStruct((s, D_MODEL), BF16), jax.ShapeDtypeStruct((s, D_MODEL), F32),
                   jax.ShapeDtypeStruct((8, D_MODEL), F32)),
        grid=(n_tiles + 1,),
        in_specs=[cur(D_MODEL), late(D_MODEL), late(D_MODEL), _const_spec((8, D_MODEL)), _const_spec((1, D_MODEL)),
                  _const_spec((1, D_MODEL)), _const_spec((N_CHIPS, D_MODEL, FF_CHUNK), single=True),
                  _const_spec((D_FF, D_MODEL), single=True)],
        out_specs=(cur(D_MODEL), cur(2 * D_FF), late(D_MODEL), late(D_MODEL), _const_spec((8, D_MODEL))),
        scratch_shapes=[pltpu.VMEM((tm, D_MODEL), F32)],
        compiler_params=_params(("arbitrary",)),
    )(x1, x1, target, modr, ln2_g, ln2_b, w_gu, w_dn)


def _bwd_ffn(dy2, act, w_gu, w_dn, tm):
    s = dy2.shape[0]

    def body(dy2_ref, act_ref, wgu_ref, wdn_ref, a_ref, dgu_ref, dh2_ref):
        dy2v = dy2_ref[...]
        dh2 = None
        for cc in range(FF_BLOCKS):
            c0 = cc * FF_CHUNK
            da = _dot_nt(dy2v, wdn_ref[c0:c0 + FF_CHUNK, :])
            gate = act_ref[:, c0:c0 + FF_CHUNK].astype(F32)
            up = act_ref[:, D_FF + c0:D_FF + c0 + FF_CHUNK].astype(F32)
            sg = _sigmoid(gate)
            sl = gate * sg
            a_ref[:, c0:c0 + FF_CHUNK] = (sl * up).astype(BF16)
            dgate = (da * up * (sg * (1.0 + gate * (1.0 - sg)))).astype(BF16)
            dup = (da * sl).astype(BF16)
            dgu_ref[:, c0:c0 + FF_CHUNK] = dgate
            dgu_ref[:, D_FF + c0:D_FF + c0 + FF_CHUNK] = dup
            part = _dot_nt(dgate, wgu_ref[cc]) + _dot_nt(dup, wgu_ref[FF_BLOCKS + cc])
            dh2 = part if dh2 is None else dh2 + part
        dh2_ref[...] = dh2

    row = lambda w: pl.BlockSpec((tm, w), lambda i: (i, 0))
    return pl.pallas_call(
        body, name="bwd_ffn",
        out_shape=(jax.ShapeDtypeStruct((s, D_FF), BF16), jax.ShapeDtypeStruct((s, 2 * D_FF), BF16),
                   jax.ShapeDtypeStruct((s, D_MODEL), F32)),
        grid=(s // tm,),
        in_specs=[row(D_MODEL), row(2 * D_FF), _const_spec((N_CHIPS, D_MODEL, FF_CHUNK), single=True),
                  _const_spec((D_FF, D_MODEL), single=True)],
        out_specs=(row(D_FF), row(2 * D_FF), row(D_MODEL)),
        compiler_params=_params(("parallel",)),
    )(dy2, act, w_gu, w_dn)


def _bwd_mid(dh2, dx1a, x1, x, y, modr, ln1_g, w_out, tm, swap_fulls, swap_kinds):
    s = x.shape[0]
    n_steps = s // tm
    n_g = len(swap_fulls)

    def body(dh2_ref, dx1a_ref, x1_ref, x_ref, y_ref, mod_ref, g_ref, wout_ref, *rest):
        full_refs = rest[:n_g]
        dxa_ref, dy_ref, dmix_ref, acc_ref = rest[n_g:n_g + 4]
        got_refs = rest[n_g + 4:2 * n_g + 4]
        swap = _HalfSwap(full_refs, got_refs, swap_kinds, *rest[2 * n_g + 4:])
        i = pl.program_id(0)

        @pl.when(i == 0)
        def _():
            swap.start()
            acc_ref[...] = jnp.zeros_like(acc_ref)

        dh2 = dh2_ref[...]
        x1v = x1_ref[...]
        yv = y_ref[...]
        g1 = mod_ref[2:3, :]
        dx1 = dx1a_ref[...] + dh2 * (1.0 + mod_ref[4:5, :])
        z1 = ALPHA * x_ref[...] + g1 * yv
        xhat, rstd = _ln_stats(z1)
        dz1 = _ln_bwd(dx1 * g_ref[...], xhat, rstd)
        dxa_ref[...] = ALPHA * dz1
        dy = (g1 * dz1).astype(BF16)
        dy_ref[...] = dy
        dmix_ref[...] = _dot_nt(dy, wout_ref[...])
        acc_ref[0:1, :] += _colsum(dh2 * x1v)
        acc_ref[1:2, :] += _colsum(dh2)
        acc_ref[2:3, :] += _colsum(dx1 * xhat)
        acc_ref[3:4, :] += _colsum(dx1)
        acc_ref[4:5, :] += _colsum(dz1 * yv)

        @pl.when(i == n_steps - 1)
        def _():
            swap.wait()

    row = lambda w: pl.BlockSpec((tm, w), lambda i: (i, 0))
    outs = pl.pallas_call(
        body, name="bwd_mid",
        out_shape=[jax.ShapeDtypeStruct((s, D_MODEL), F32), jax.ShapeDtypeStruct((s, D_MODEL), BF16),
                   jax.ShapeDtypeStruct((s, D_MODEL), F32), jax.ShapeDtypeStruct((8, D_MODEL), F32)]
        + _HalfSwap.out_shapes(swap_fulls, swap_kinds),
        grid=(n_steps,),
        in_specs=[row(D_MODEL)] * 5 + [_const_spec((8, D_MODEL)), _const_spec((1, D_MODEL)),
                                       _const_spec((D_MODEL, D_MODEL))] + [ANY] * n_g,
        out_specs=[row(D_MODEL), row(D_MODEL), row(D_MODEL), _const_spec((8, D_MODEL))] + [ANY] * n_g,
        scratch_shapes=_HalfSwap.sems(n_g),
        compiler_params=_params(("arbitrary",)),
    )(dh2, dx1a, x1, x, y, modr, ln1_g, w_out, *swap_fulls)
    return outs[:4], outs[4:]


def _fold_kv(t0, t1):
    lane = lax.broadcasted_iota(jnp.int32, t0.shape, 1)
    f0 = t0 + pltpu.roll(t0, HEAD_DIM, 1)
    f1 = t1 + pltpu.roll(t1, HEAD_DIM, 1)
    return jnp.where(lane < HEAD_DIM, f0, f1)


def _bwd_mix(q, kv, gu, gv, dmix, bias, sinks, gln_g, gln_b, wsm, bsx, amat, aog, gog, grad_parts, grad_kinds):
    s = q.shape[0]
    tile = 2 * BLOCK
    n_steps = s // tile
    n_g = len(grad_parts)

    def body(q_ref, kv_ref, kvp_ref, gu_ref, gv_ref, dmix_ref, bias_ref, sinks_ref, glng_ref, glnb_ref, wsm_ref,
             bsx_ref, amat_ref, aog_ref, gog_ref, *rest):
        part_refs = rest[:n_g]
        dq_ref, dkv_ref, dgu_ref, dgv_ref, gbias_ref, dws_ref, dbs_ref, vec_ref, dsink_ref = rest[n_g:n_g + 9]
        rx_refs = rest[n_g + 9:2 * n_g + 9]
        carry, done, send_sems, recv_sems = rest[2 * n_g + 9:]
        n = pl.program_id(0)
        exchange = _ChipExchange(part_refs, rx_refs, grad_kinds, send_sems, recv_sems)

        @pl.when(n == 0)
        def _():
            exchange.start()
            carry[...] = jnp.zeros_like(carry)
            done[...] = jnp.zeros_like(done)
            gbias_ref[...] = jnp.zeros_like(gbias_ref)
            dws_ref[...] = jnp.zeros_like(dws_ref)
            dbs_ref[...] = jnp.zeros_like(dbs_ref)
            vec_ref[...] = jnp.zeros_like(vec_ref)
            dsink_ref[...] = jnp.zeros_like(dsink_ref)

        @pl.when(n == n_steps)
        def _():
            dkv_ref[:BLOCK, :] = done[...].astype(BF16)
            dkv_ref[BLOCK:, :] = carry[...].astype(BF16)
            exchange.wait()

        @pl.when(n < n_steps)
        def _():
            col = lax.broadcasted_iota(jnp.int32, (BLOCK, 2 * BLOCK), 1)
            lane = lax.broadcasted_iota(jnp.int32, (BLOCK, LANES), 1)
            low = lane < HEAD_DIM
            rows = [slice(0, BLOCK), slice(BLOCK, tile)]
            kv_blocks = [kvp_ref[...], kv_ref[rows[0], :], kv_ref[rows[1], :]]
            masks = [(col < BLOCK) & (n == 0), None]
            q_blks = [q_ref[r, :] for r in rows]
            fwd = []
            for b in range(2):
                kk = jnp.concatenate([kv_blocks[b][:, :KV_W], kv_blocks[b + 1][:, :KV_W]], axis=0)
                vv = jnp.concatenate([kv_blocks[b][:, KV_W:], kv_blocks[b + 1][:, KV_W:]], axis=0)
                fwd.append(_attn_block_fwd(q_blks[b], kk, vv, bias_ref, sinks_ref, masks[b]))
                fwd.append(_gmlp_chunk_fwd(gu_ref[rows[b], :], gv_ref[rows[b], :], glng_ref[...], glnb_ref[...],
                                           wsm_ref, bsx_ref[...], amat_ref[...]))
            res = _interleave(*fwd[:2]) + _interleave(*fwd[2:])

            def gating_bwd(b, d_gm, saved):
                u, tu, ta, xhat, rstd, vb, mixedv = saved
                dgu_ref[rows[b], :] = (d_gm * mixedv * _gelu_grad(gu_ref[rows[b], :], tu)).astype(BF16)
                dmx = d_gm * u
                dmxb = dmx.astype(BF16)
                yield
                dvn_cols, dws = [], []
                for pair in range(N_GROUPS // 2):
                    dp_ = dmxb[:, pair * LANES:(pair + 1) * LANES]
                    vp = vb[:, pair * LANES:(pair + 1) * LANES]
                    dvn_cols.append(
                        jnp.where(low, _dot_tn(wsm_ref[2 * pair], dp_), _dot_tn(wsm_ref[2 * pair + 1], dp_)))
                    zero = jnp.zeros_like(dp_)
                    dws.append(_dot_nt(jnp.where(low, dp_, zero), vp))
                    dws.append(_dot_nt(jnp.where(low, zero, dp_), vp))
                dvn = jnp.concatenate(dvn_cols, axis=1)
                yield
                dxh = dvn * glng_ref[...]
                am = amat_ref[...]
                m1 = _split_dot(dxh, am)
                m2 = _split_dot(dxh * xhat, am)
                yield
                da = rstd * (dxh - m1 - xhat * m2)
                dgv_ref[rows[b], :] = (da * _gelu_grad(gv_ref[rows[b], :], ta)).astype(BF16)
                return dmx, dws, _colsum(dvn * xhat), _colsum(dvn)

            def attention_bwd(b, d_attn, probs, kvar, vvar):
                heads = range(N_HEADS)
                sels = [low if h % 2 == 0 else jnp.logical_not(low) for h in heads]
                pair_of = lambda a, h: a[:, (h // 2) * LANES:(h // 2 + 1) * LANES]
                do_hs = [jnp.where(sels[h], pair_of(d_attn, h), 0.0).astype(BF16) for h in heads]
                q_hs = [jnp.where(sels[h], pair_of(q_blks[b], h), jnp.zeros((BLOCK, LANES), BF16)) for h in heads]
                dps = [_dot_nt(do_hs[h], vvar[_head_kv(h)[0]][_head_kv(h)[1]]) for h in heads]
                yield
                deltas = [jnp.sum(probs[h][0] * dps[h], axis=-1, keepdims=True) for h in heads]
                yield
                dss = [probs[h][0] * (dps[h] - deltas[h]) for h in heads]
                dsinks = [-(probs[h][1] * deltas[h]) for h in heads]
                dsbs = [ds.astype(BF16) for ds in dss]
                pbs = [probs[h][0].astype(BF16) for h in heads]
                yield
                dqs = [_dot(dsbs[h], kvar[_head_kv(h)[0]][_head_kv(h)[1]]) for h in heads]
                tks = [_dot_tn(dsbs[h], q_hs[h]) for h in heads]
                tvs = [_dot_tn(pbs[h], do_hs[h]) for h in heads]
                dq_cols = [dqs[2 * i] + dqs[2 * i + 1] for i in range(N_HEADS // 2)]
                dq_ref[rows[b], :] = (jnp.concatenate(dq_cols, axis=1) * Q_SCALE).astype(BF16)
                per_kv = N_HEADS // N_KV
                kv_sum = lambda ts, kvh: sum(ts[kvh * per_kv + 1:(kvh + 1) * per_kv], ts[kvh * per_kv])
                dkk = _fold_kv(kv_sum(tks, 0), kv_sum(tks, 1))
                dvv = _fold_kv(kv_sum(tvs, 0), kv_sum(tvs, 1))
                return jnp.concatenate([dkk, dvv], axis=1), dss, dsinks

            bwd, rms_g = [], []
            for b in range(2):
                attn, probs, kvar, vvar = res[2 * b]
                gm, saved = res[2 * b + 1]
                na_unit, r_a = _rms(attn, 1.0)
                ng_unit, r_g = _rms(gm, 1.0)
                dmix = dmix_ref[rows[b], :]
                dn_a = dmix[:, :ATTN_W]
                dn_g = dmix[:, ATTN_W:]
                rms_g.append((_colsum(dn_a * na_unit), _colsum(dn_g * ng_unit)))
                t_a = dn_a * aog_ref[...]
                d_attn = r_a * t_a - na_unit * (r_a * jnp.mean(t_a * na_unit, axis=-1, keepdims=True))
                t_g = dn_g * gog_ref[...]
                d_gm = r_g * t_g - ng_unit * (r_g * jnp.mean(t_g * ng_unit, axis=-1, keepdims=True))
                bwd.append(attention_bwd(b, d_attn, probs, kvar, vvar))
                bwd.append(gating_bwd(b, d_gm, saved))
            (dkv_a, dss_a, dsk_a), (dmx_a, dws_a, glg_a, glb_a) = _interleave(*bwd[:2])
            (dkv_b, dss_b, dsk_b), (dmx_b, dws_b, glg_b, glb_b) = _interleave(*bwd[2:])

            vec_ref[0:1, :] += rms_g[0][0] + rms_g[1][0]
            vec_ref[1:2, :] += rms_g[0][1] + rms_g[1][1]
            vec_ref[2:3, :] += glg_a + glg_b
            vec_ref[3:4, :] += glb_a + glb_b
            dbs_ref[...] += dmx_a + dmx_b
            for g in range(N_GROUPS):
                dws_ref[g] += dws_a[g] + dws_b[g]
            for h in range(N_HEADS):
                gbias_ref[h] += dss_a[h] + dss_b[h]
                dsink_ref[h] += dsk_a[h] + dsk_b[h]

            dkv_ref[:BLOCK, :] = done[...].astype(BF16)
            dkv_ref[BLOCK:, :] = (carry[...] + dkv_a[:BLOCK]).astype(BF16)
            done[...] = dkv_a[BLOCK:] + dkv_b[:BLOCK]
            carry[...] = dkv_b[BLOCK:]

    last = n_steps - 1
    cur = lambda w: pl.BlockSpec((tile, w), lambda n: (jnp.minimum(n, last), 0))
    late = lambda w: pl.BlockSpec((tile, w), lambda n: (jnp.clip(n - 1, 0, last), 0))
    before = pl.BlockSpec((BLOCK, 2 * KV_W), lambda n: (jnp.clip(2 * n - 1, 0, 2 * last + 1), 0))
    outs = pl.pallas_call(
        body, name="bwd_mix",
        out_shape=[jax.ShapeDtypeStruct((s, ATTN_W), BF16), jax.ShapeDtypeStruct((s, 2 * KV_W), BF16),
                   jax.ShapeDtypeStruct((s, GMLP_W), BF16), jax.ShapeDtypeStruct((s, GMLP_W), BF16),
                   jax.ShapeDtypeStruct((N_HEADS, BLOCK, 2 * BLOCK), F32),
                   jax.ShapeDtypeStruct((N_GROUPS, BLOCK, BLOCK), F32),
                   jax.ShapeDtypeStruct((BLOCK, GMLP_W), F32), jax.ShapeDtypeStruct((8, GMLP_W), F32),
                   jax.ShapeDtypeStruct((N_HEADS, BLOCK, 1), F32)]
        + [jax.ShapeDtypeStruct(_rx_shape(p.shape, k), BF16) for p, k in zip(grad_parts, grad_kinds)],
        grid=(n_steps + 1,),
        in_specs=[cur(ATTN_W), cur(2 * KV_W), before, cur(GMLP_W), cur(GMLP_W), cur(D_MODEL),
                  _const_spec((N_HEADS, BLOCK, 2 * BLOCK)), pl.BlockSpec(memory_space=pltpu.SMEM),
                  _const_spec((1, GMLP_W)), _const_spec((1, GMLP_W)), _const_spec((N_GROUPS, BLOCK, BLOCK)),
                  _const_spec((BLOCK, GMLP_W)), _const_spec((GMLP_W, GMLP_W)), _const_spec((1, ATTN_W)),
                  _const_spec((1, GMLP_W))] + [ANY] * n_g,
        out_specs=[cur(ATTN_W), late(2 * KV_W), cur(GMLP_W), cur(GMLP_W),
                   _const_spec((N_HEADS, BLOCK, 2 * BLOCK)), _const_spec((N_GROUPS, BLOCK, BLOCK)),
                   _const_spec((BLOCK, GMLP_W)), _const_spec((8, GMLP_W)), _const_spec((N_HEADS, BLOCK, 1))]
        + [ANY] * n_g,
        scratch_shapes=[pltpu.VMEM((BLOCK, 2 * KV_W), F32), pltpu.VMEM((BLOCK, 2 * KV_W), F32)]
        + _ChipExchange.sems(n_g),
        compiler_params=_params(("arbitrary",)),
    )(q, kv, kv, gu, gv, dmix, bias, sinks, gln_g, gln_b, wsm, bsx, amat, aog, gog, *grad_parts)
    return outs[:9], outs[9:]


def _mix_finalize(gbias, bucket, dws, dbs, dsink):
    def body(gb_ref, bucket_ref, dws_ref, dbs_ref, dsink_ref, tall_ref):
        bk = bucket_ref[...]
        lane = lax.broadcasted_iota(jnp.int32, (N_BUCKETS, LANES), 1)
        rowi = lax.broadcasted_iota(jnp.int32, (N_BUCKETS, LANES), 0)
        drb = jnp.zeros((N_BUCKETS, LANES), F32)
        dsk = jnp.zeros((8, LANES), F32)
        lane8 = lax.broadcasted_iota(jnp.int32, (8, LANES), 1)
        for h in range(N_HEADS):
            g = gb_ref[h]
            for b in range(N_BUCKETS):
                tot = jnp.sum(_colsum(jnp.where(bk == float(b), g, 0.0)), axis=1, keepdims=True)
                drb = jnp.where((lane == h) & (rowi == b), tot, drb)
            sk = jnp.sum(dsink_ref[h], axis=0, keepdims=True)
            dsk = jnp.where(lane8 == h, sk, dsk)
        tall_ref[TALL_RB:TALL_RB + N_BUCKETS, :] = drb
        tall_ref[TALL_SK:TALL_SK + 8, :] = dsk
        ti = lax.broadcasted_iota(jnp.int32, (BLOCK, BLOCK), 0)
        ui = lax.broadcasted_iota(jnp.int32, (BLOCK, BLOCK), 1)
        for g in range(N_GROUPS):
            tall_ref[g * BLOCK:(g + 1) * BLOCK, :] = jnp.where(ti >= ui, dws_ref[g], 0.0)
        gi = lax.broadcasted_iota(jnp.int32, (GMLP_W, LANES), 0) // GROUP_DIM
        li = lax.broadcasted_iota(jnp.int32, (GMLP_W, LANES), 1)
        ind = jnp.where(gi == li, 1.0, 0.0).astype(BF16)
        d = dbs_ref[...]
        hi = d.astype(BF16)
        r1 = d - hi.astype(F32)
        mid = r1.astype(BF16)
        lo = (r1 - mid.astype(F32)).astype(BF16)
        dbsg = _dot(hi, ind) + _dot(mid, ind) + _dot(lo, ind)
        tall_ref[TALL_BS:TALL_BS + N_GROUPS, :] = dbsg.T[:N_GROUPS, :]

    return pl.pallas_call(
        body, name="mix_finalize", out_shape=jax.ShapeDtypeStruct((TALL_ROWS, LANES), F32), grid=(1,),
        in_specs=[_const_spec((N_HEADS, BLOCK, 2 * BLOCK)), _const_spec((BLOCK, 2 * BLOCK)),
                  _const_spec((N_GROUPS, BLOCK, BLOCK)), _const_spec((BLOCK, GMLP_W)),
                  _const_spec((N_HEADS, BLOCK, 1))],
        out_specs=_const_spec((TALL_ROWS, LANES)),
        compiler_params=_params(("arbitrary",)),
    )(gbias, bucket, dws, dbs, dsink)


def _bwd_in(dq, dkv, dgu, dgv, dxa, x, modr, w_in, tm):
    s = x.shape[0]

    def body(dq_ref, dkv_ref, dgu_ref, dgv_ref, dxa_ref, x_ref, mod_ref, w_ref, gx_ref, acc_ref, db_ref):
        @pl.when(pl.program_id(0) == 0)
        def _():
            acc_ref[...] = jnp.zeros_like(acc_ref)
            db_ref[...] = jnp.zeros_like(db_ref)

        dproj = jnp.concatenate([dq_ref[...], dkv_ref[...], dgu_ref[...], dgv_ref[...]], axis=1)
        dh1 = _dot_nt(dproj, w_ref[...])
        gx_ref[...] = dxa_ref[...] + dh1 * (1.0 + mod_ref[1:2, :])
        acc_ref[0:1, :] += _colsum(dh1 * x_ref[...])
        acc_ref[1:2, :] += _colsum(dh1)
        db_ref[0:1, :] += _colsum(dproj.astype(F32))

    row = lambda w: pl.BlockSpec((tm, w), lambda i: (i, 0))
    return pl.pallas_call(
        body, name="bwd_in",
        out_shape=(jax.ShapeDtypeStruct((s, D_MODEL), F32), jax.ShapeDtypeStruct((8, D_MODEL), F32),
                   jax.ShapeDtypeStruct((8, IN_W), F32)),
        grid=(s // tm,),
        in_specs=[row(ATTN_W), row(2 * KV_W), row(GMLP_W), row(GMLP_W), row(D_MODEL), row(D_MODEL),
                  _const_spec((8, D_MODEL)), _const_spec((D_MODEL, IN_W))],
        out_specs=(row(D_MODEL), _const_spec((8, D_MODEL)), _const_spec((8, IN_W))),
        compiler_params=_params(("arbitrary",)),
    )(dq, dkv, dgu, dgv, dxa, x, modr, w_in)


def _wgrad(a, bs, tm, tk, name, owner_blocks=False, gather_vs=()):
    k_all, m = a.shape
    n = sum(b.shape[1] for b in bs)
    nk = k_all // tk
    nm = m // tm
    n_b = len(bs)
    n_v = len(gather_vs)
    wb = n // N_CHIPS

    def body(a_ref, *rest):
        b_refs, v_refs = rest[:n_b], rest[n_b:n_b + n_v]
        o_ref, ob_ref = rest[n_b + n_v:n_b + n_v + 2]
        vg_refs = rest[n_b + n_v + 2:n_b + 2 * n_v + 2]
        i, k = pl.program_id(0), pl.program_id(1)
        if n_v:
            gather = _Gather8(v_refs, vg_refs, *rest[n_b + 2 * n_v + 2:])

            @pl.when((i == 0) & (k == 0))
            def _():
                gather.start()

            @pl.when((i == nm - 1) & (k == 0))
            def _():
                gather.forward()

        @pl.when(k == 0)
        def _():
            o_ref[...] = jnp.zeros_like(o_ref)

        b = b_refs[0][...] if n_b == 1 else jnp.concatenate([r[...] for r in b_refs], axis=1)
        if owner_blocks:
            av = a_ref[...]
            for j in range(N_CHIPS):
                o_ref[j] += _dot_tn(av, b[:, j * wb:(j + 1) * wb])
        else:
            o_ref[...] += _dot_tn(a_ref[...], b)

        @pl.when(k == nk - 1)
        def _():
            ob_ref[...] = o_ref[...].astype(BF16)

        if n_v:
            @pl.when((i == nm - 1) & (k == nk - 1))
            def _():
                gather.finish()

    if owner_blocks:
        out_spec = pl.BlockSpec((N_CHIPS, tm, wb), lambda i, k: (0, i, 0))
        shape = (N_CHIPS, m, wb)
    else:
        out_spec = pl.BlockSpec((tm, n), lambda i, k: (i, 0))
        shape = (m, n)
    outs = pl.pallas_call(
        body, name=name,
        out_shape=[jax.ShapeDtypeStruct(shape, F32), jax.ShapeDtypeStruct(shape, BF16)] + _gathered8_shapes(gather_vs),
        grid=(nm, nk),
        in_specs=[pl.BlockSpec((tk, tm), lambda i, k: (k, i))]
        + [pl.BlockSpec((tk, b.shape[1]), lambda i, k: (k, 0)) for b in bs] + [ANY] * n_v,
        out_specs=[out_spec, out_spec] + [ANY] * n_v,
        scratch_shapes=_Gather8.sems(n_v) if n_v else [],
        compiler_params=_params(("arbitrary", "arbitrary") if n_v else ("parallel", "arbitrary")),
    )(a, *bs, *gather_vs)
    return outs[0], outs[1], outs[2:]


def _adam_math(w, g, m, v):
    m2 = ADAM_B1 * m + (1.0 - ADAM_B1) * g
    v2 = ADAM_B2 * v + (1.0 - ADAM_B2) * (g * g)
    m_hat = m2 / (1.0 - ADAM_B1 ** ADAM_STEP)
    v_hat = v2 / (1.0 - ADAM_B2 ** ADAM_STEP)
    delta = -ADAM_LR * (m_hat / (jnp.sqrt(v_hat) + ADAM_EPS) + ADAM_WD * w)
    return delta, m2, v2


def _adam_halves(w, mine, got, m, v, tr, name):
    r, cc = w.shape
    h = r // 2
    nt = h // tr

    def body(c_ref, w_ref, mine_ref, got_ref, m_ref, v_ref, g_ref, d_ref, m2_ref, v2_ref):
        g = jnp.where(pl.program_id(0) == c_ref[0], mine_ref[...], got_ref[...])
        g_ref[...] = g
        d, m2, v2 = _adam_math(w_ref[...], g, m_ref[...], v_ref[...])
        d_ref[...] = d
        m2_ref[...] = m2
        v2_ref[...] = v2

    full = pl.BlockSpec((tr, cc), lambda hh, i, c_ref: (hh * nt + i, 0))
    half = pl.BlockSpec((tr, cc), lambda hh, i, c_ref: (i, 0))
    shp = jax.ShapeDtypeStruct((r, cc), F32)
    return pl.pallas_call(
        body, name=name, out_shape=(shp, shp, shp, shp),
        grid_spec=pltpu.PrefetchScalarGridSpec(
            num_scalar_prefetch=1, grid=(2, nt), in_specs=[full, half, half, full, full],
            out_specs=(full, full, full, full)),
        compiler_params=_params(("arbitrary", "arbitrary")),
    )(_core_index_scalar(), w, mine, got, m, v)


def _adam_w_ada(sc_t, dmod_cols, w, m, v, tr):
    r, cc = w.shape

    def body(sct_ref, dm_ref, w_ref, m_ref, v_ref, g_ref, d_ref, m2_ref, v2_ref):
        g = sct_ref[:, 0:1] * dm_ref[0:1, :]
        for k in range(1, N_DEV):
            g = g + sct_ref[:, k:k + 1] * dm_ref[k:k + 1, :]
        g_ref[...] = g
        d, m2, v2 = _adam_math(w_ref[...], g, m_ref[...], v_ref[...])
        d_ref[...] = d
        m2_ref[...] = m2
        v2_ref[...] = v2

    spec = pl.BlockSpec((tr, cc), lambda i: (i, 0))
    shp = jax.ShapeDtypeStruct((r, cc), F32)
    return pl.pallas_call(
        body, name="adam_w_ada", out_shape=(shp, shp, shp, shp), grid=(r // tr,),
        in_specs=[pl.BlockSpec((tr, N_DEV), lambda i: (i, 0)), _const_spec((N_DEV, cc)), spec, spec, spec],
        out_specs=(spec, spec, spec, spec), compiler_params=_params(("parallel",)),
    )(sc_t, dmod_cols, w, m, v)


def _pack_wide(acc_i, acc_m, acc_f, db_in, vec):
    arrs = [acc_i, acc_m, acc_f, db_in, vec]
    i_, m_, f_, b_, v_ = range(5)
    src = {"b_in": (b_, 0), "ln1_g": (m_, 2), "ln1_b": (m_, 3), "ln2_g": (f_, 1), "ln2_b": (f_, 2),
           "gmlp_ln_g": (v_, 2), "gmlp_ln_b": (v_, 3), "attn_out_g": (v_, 0), "gmlp_out_g": (v_, 1), "loss": (f_, 0)}
    dmod = [(i_, 1), (i_, 0), (m_, 4), (m_, 1), (m_, 0), (f_, 3)]

    def body(*refs):
        ins, wide_ref = refs[:5], refs[5]
        wide_ref[...] = jnp.zeros_like(wide_ref)
        for k, (a, row) in enumerate(dmod):
            wide_ref[0:1, k * D_MODEL:(k + 1) * D_MODEL] = ins[a][row:row + 1, :]
        for name, (a, row) in src.items():
            r, off, n = WIDE_LAYOUT[name]
            wide_ref[r:r + 1, off:off + n] = ins[a][row:row + 1, :]

    return pl.pallas_call(
        body, name="pack_wide", out_shape=jax.ShapeDtypeStruct((8, WIDE_W), F32), grid=(1,),
        in_specs=[_const_spec(a.shape) for a in arrs], out_specs=_const_spec((8, WIDE_W)),
        compiler_params=_params(("arbitrary",)),
    )(*arrs)


def _adam_small(gw, gt, wide_wmv, w_s, b_s, rel_bias, sinks):
    names = list(WIDE_PARAMS)
    tall = [("gmlp_w_s", w_s), ("gmlp_b_s", b_s), ("rel_bias", rel_bias), ("attn_sinks", sinks)]
    ins = [gw, gt]
    for n in names:
        ins += list(wide_wmv[n])
    for _, t in tall:
        ins += list(t)
    n_in = len(ins)

    def body(*refs):
        gw_ref, gt_ref = refs[0], refs[1]
        wmv = refs[2:n_in]
        dmod_ref, loss_ref = refs[n_in], refs[n_in + 1]
        outs = refs[n_in + 2:]

        def tall_sum(r0, nr):
            g = gt_ref[r0:r0 + nr, :]
            for d in range(1, N_DEV):
                g = g + gt_ref[d * TALL_ROWS + r0:d * TALL_ROWS + r0 + nr, :]
            return g

        def emit(k, g, w_ref, m_ref, v_ref):
            d, m2, v2 = _adam_math(w_ref[...], g, m_ref[...], v_ref[...])
            outs[4 * k][...] = g
            outs[4 * k + 1][...] = d
            outs[4 * k + 2][...] = m2
            outs[4 * k + 3][...] = v2

        gsum = gw_ref[0:8, :]
        for d in range(1, N_DEV):
            gsum = gsum + gw_ref[8 * d:8 * d + 8, :]
        for d in range(N_DEV):
            dmod_ref[d:d + 1, :] = gw_ref[8 * d:8 * d + 1, :]
        for k, n in enumerate(names):
            r, off, sz = WIDE_LAYOUT[n]
            emit(k, gsum[r:r + 1, off:off + sz], *wmv[3 * k:3 * k + 3])
        r, off, sz = WIDE_LAYOUT["loss"]
        tot = jnp.sum(gsum[r:r + 1, off:off + sz], axis=1, keepdims=True)
        loss_ref[...] = jnp.broadcast_to(tot * (0.5 / D_MODEL), loss_ref.shape)

        k0 = len(names)
        ws_refs = wmv[3 * k0:3 * k0 + 3]
        for g in range(N_GROUPS):
            rows = slice(g * BLOCK, (g + 1) * BLOCK)
            gg = tall_sum(g * BLOCK, BLOCK)
            d, m2, v2 = _adam_math(ws_refs[0][rows, :], gg, ws_refs[1][rows, :], ws_refs[2][rows, :])
            outs[4 * k0][rows, :] = gg
            outs[4 * k0 + 1][rows, :] = d
            outs[4 * k0 + 2][rows, :] = m2
            outs[4 * k0 + 3][rows, :] = v2
        emit(k0 + 1, tall_sum(TALL_BS, N_GROUPS), *wmv[3 * (k0 + 1):3 * (k0 + 1) + 3])
        emit(k0 + 2, tall_sum(TALL_RB, N_BUCKETS)[:, :N_HEADS], *wmv[3 * (k0 + 2):3 * (k0 + 2) + 3])
        emit(k0 + 3, tall_sum(TALL_SK, 8)[0:1, :N_HEADS], *wmv[3 * (k0 + 3):3 * (k0 + 3) + 3])

    out_shapes = [jax.ShapeDtypeStruct((N_DEV, WIDE_W), F32), jax.ShapeDtypeStruct((8, LANES), F32)]
    for n in names:
        out_shapes += [jax.ShapeDtypeStruct(wide_wmv[n][0].shape, F32)] * 4
    for _, t in tall:
        out_shapes += [jax.ShapeDtypeStruct(t[0].shape, F32)] * 4
    res = pl.pallas_call(
        body, name="adam_small", out_shape=out_shapes, grid=(1,),
        in_specs=[_const_spec(a.shape) for a in ins], out_specs=[_const_spec(o.shape) for o in out_shapes],
        compiler_params=_params(("arbitrary",)),
    )(*ins)
    out = {}
    for k, n in enumerate(names + [t[0] for t in tall]):
        out[n] = tuple(res[2 + 4 * k:6 + 4 * k])
    return res[0], res[1], out


def kernel(x, c, rel_bias, w_ada, b_ada, w_in, b_in, attn_sinks, gmlp_ln_g, gmlp_ln_b, gmlp_w_s, gmlp_b_s, attn_out_g, gmlp_out_g, w_out, ln1_g, ln1_b, w_gate_up, w_down, ln2_g, ln2_b, loss_target, m_rel_bias, m_w_ada, m_b_ada, m_w_in, m_b_in, m_attn_sinks, m_gmlp_ln_g, m_gmlp_ln_b, m_gmlp_w_s, m_gmlp_b_s, m_attn_out_g, m_gmlp_out_g, m_w_out, m_ln1_g, m_ln1_b, m_w_gate_up, m_w_down, m_ln2_g, m_ln2_b, v_rel_bias, v_w_ada, v_b_ada, v_w_in, v_b_in, v_attn_sinks, v_gmlp_ln_g, v_gmlp_ln_b, v_gmlp_w_s, v_gmlp_b_s, v_attn_out_g, v_gmlp_out_g, v_w_out, v_ln1_g, v_ln1_b, v_w_gate_up, v_w_down, v_ln2_g, v_ln2_b):
    ix, iy, ic = _my_pos()
    chip = 2 * ix + iy
    dev = 4 * ix + 2 * iy + ic
    s = x.shape[1]
    xs = x[0]
    tgt = loss_target[0]
    tm_big = min(512, s)
    tm_ffn = min(FFN_SUB, s)
    n_ada = w_ada.shape[2]

    w_in_s, w_out_s = w_in[0].astype(BF16), w_out[0].astype(BF16)
    w_gu_s, w_dn_s = w_gate_up[0].astype(BF16), w_down[0].astype(BF16)
    sc_all, _, mod_rows, w_in_g = _prologue(
        jnp.pad(c, ((0, 7), (0, 0))), w_ada[0], lax.dynamic_slice_in_dim(b_ada, chip * n_ada, n_ada, axis=1), w_in_s)
    mod_all = mod_rows.reshape(N_DEV, N_DEV, -1)
    mod_row = lax.dynamic_index_in_dim(mod_all[0::2], dev, axis=1, keepdims=False)
    modr = jnp.pad(mod_row.reshape(6, D_MODEL), ((0, 2), (0, 0)))
    w_in_g = _insert_own(w_in_g, w_in_s, "blk", chip)
    w_in_f = jnp.transpose(w_in_g, (1, 0, 2)).reshape(D_MODEL, IN_W)

    bucket = _bucket_table()
    bias, wsm = _prep_tables(bucket, rel_bias, gmlp_w_s[0])
    bsx = jnp.repeat(gmlp_b_s[0].T, GROUP_DIM, axis=1)
    amat = _group_mean_matrix()
    sinks = attn_sinks[0]

    (h1, q, kv, gu, gv), (w_out_g, w_dn_g) = _fwd_in(xs, modr, w_in_f, b_in, tm_big, [w_out_s, w_dn_s], ["blk", "blk"])
    w_out_f = _insert_own(w_out_g, w_out_s, "blk", chip).reshape(D_MODEL, D_MODEL)
    x1, y, mixed, (w_gu_g,) = _fwd_mix(
        q, kv, gu, gv, xs, modr, bias, sinks, gmlp_ln_g, gmlp_ln_b, wsm, bsx, amat, attn_out_g, gmlp_out_g, w_out_f,
        ln1_g, ln1_b, tm_big, [w_gu_s], ["blk"])
    assert w_gate_up.shape[2] == FF_CHUNK
    w_gu_f = _insert_own(w_gu_g, w_gu_s, "blk", chip)
    w_dn_f = _insert_own(w_dn_g, w_dn_s, "blk", chip).reshape(D_FF, D_MODEL)
    h2, act, dy2, dx1a, acc_f = _fwd_ffn(x1, tgt, modr, ln2_g, ln2_b, w_gu_f, w_dn_f, tm_ffn)

    a_act, dgu_ff, dh2 = _bwd_ffn(dy2, act, w_gu_f, w_dn_f, min(FFN_SUB, s))
    g_dn, g_dn_b, _ = _wgrad(a_act, [dy2], D_FF // 2, min(512, s), "wgrad_down")
    g_gu, g_gu_b, _ = _wgrad(h2, [dgu_ff], 512, min(512, s), "wgrad_gate_up")
    blk3 = lambda a, rows: a.reshape(N_CHIPS, rows, a.shape[1])
    (dxa, dy, dmix, acc_m), (got_dn, got_gu) = _bwd_mid(
        dh2, dx1a, x1, xs, y, modr, ln1_g, w_out_f, tm_big, [blk3(g_dn_b, D_FF // N_CHIPS), g_gu_b], ["blk", "cols"])
    g_out, g_out_b, _ = _wgrad(mixed, [dy], 512, min(512, s), "wgrad_out")
    (got_out,) = _swap_halves([blk3(g_out_b, D_MODEL // N_CHIPS)], ["blk"], "rs_swap_out")
    kinds_a = ["blk", "cols", "blk"]
    fulls_a = [blk3(g_dn, D_FF // N_CHIPS), g_gu, blk3(g_out, D_MODEL // N_CHIPS)]
    gots_a = [got_dn, got_gu, got_out]
    parts_a = [_add_halves(f, g, k, "rs_add_a%d" % i) for i, (f, g, k) in enumerate(zip(fulls_a, gots_a, kinds_a))]
    (dq, dkv, dgu, dgv, gbias, dws, dbs, vec, dsink), rxs_a = _bwd_mix(
        q, kv, gu, gv, dmix, bias, sinks, gmlp_ln_g, gmlp_ln_b, wsm, bsx, amat, attn_out_g, gmlp_out_g,
        [p[1] for p in parts_a], kinds_a)
    tall_g = _mix_finalize(gbias, bucket, dws, dbs, dsink)
    grad_x, acc_i, db_in = _bwd_in(dq, dkv, dgu, dgv, dxa, xs, modr, w_in_f, tm_big)

    wide_g = _pack_wide(acc_i, acc_m, acc_f, db_in, vec)
    full_in, full_in_b, (gw, gt) = _wgrad(h1, [dq, dkv, dgu, dgv], 512, min(512, s), "wgrad_in", owner_blocks=True,
                                          gather_vs=[wide_g, tall_g])
    (got_in,) = _swap_halves([full_in_b], ["blk"], "rs_swap_in")
    part_in = _add_halves(full_in, got_in, "blk", "rs_add_in")
    (rx_in,) = _exchange_chip_partials([part_in[1]], ["blk"], "rs_chips_in")
    wide_wmv = {"b_ada": (b_ada, m_b_ada, v_b_ada), "b_in": (b_in, m_b_in, v_b_in),
                "ln1_g": (ln1_g, m_ln1_g, v_ln1_g), "ln1_b": (ln1_b, m_ln1_b, v_ln1_b),
                "ln2_g": (ln2_g, m_ln2_g, v_ln2_g), "ln2_b": (ln2_b, m_ln2_b, v_ln2_b),
                "gmlp_ln_g": (gmlp_ln_g, m_gmlp_ln_g, v_gmlp_ln_g), "gmlp_ln_b": (gmlp_ln_b, m_gmlp_ln_b, v_gmlp_ln_b),
                "attn_out_g": (attn_out_g, m_attn_out_g, v_attn_out_g),
                "gmlp_out_g": (gmlp_out_g, m_gmlp_out_g, v_gmlp_out_g)}
    rows2 = lambda a: a.reshape(-1, a.shape[-1])
    dmod_all, loss_t, small = _adam_small(
        gw, gt, wide_wmv, tuple(rows2(a) for a in (gmlp_w_s, m_gmlp_w_s, v_gmlp_w_s)),
        tuple(rows2(a) for a in (gmlp_b_s, m_gmlp_b_s, v_gmlp_b_s)), (rel_bias, m_rel_bias, v_rel_bias),
        (attn_sinks, m_attn_sinks, v_attn_sinks))
    loss = loss_t[0, 0]

    dmod_cols = lax.dynamic_slice_in_dim(dmod_all, chip * n_ada, n_ada, axis=1)
    g_ada, d_ada, m_ada, v_ada = _adam_w_ada(sc_all.T, dmod_cols, w_ada[0], m_w_ada[0], v_w_ada[0], 256)

    sums = [(parts_a[0][0], rxs_a[0], "blk", 176), (parts_a[1][0], rxs_a[1], "cols", 256),
            (parts_a[2][0], rxs_a[2], "blk", 128), (part_in[0], rx_in, "blk", 256)]
    mine = [_sum_chips(p, rx, k, tr, "rs_sum_%d" % i) for i, (p, rx, k, tr) in enumerate(sums)]
    got = _share_halves(mine, "rs_share")

    gs_dn, d_dn, m_dn, v_dn = _adam_halves(w_down[0], mine[0], got[0], m_w_down[0], v_w_down[0], 176, "adam_w_down")
    gs_gu, d_gu, m_gu, v_gu = _adam_halves(w_gate_up[0], mine[1], got[1], m_w_gate_up[0], v_w_gate_up[0], 256,
                                           "adam_w_gate_up")
    gs_out, d_out, m_out, v_out = _adam_halves(w_out[0], mine[2], got[2], m_w_out[0], v_w_out[0], 128, "adam_w_out")
    gs_in, d_in, m_in, v_in = _adam_halves(w_in[0], mine[3], got[3], m_w_in[0], v_w_in[0], 256, "adam_w_in")

    big = {"w_ada": (g_ada, d_ada, m_ada, v_ada), "w_in": (gs_in, d_in, m_in, v_in), "w_out": (gs_out, d_out, m_out, v_out),
           "w_gate_up": (gs_gu, d_gu, m_gu, v_gu), "w_down": (gs_dn, d_dn, m_dn, v_dn)}
    order = ["rel_bias", "w_ada", "b_ada", "w_in", "b_in", "attn_sinks", "gmlp_ln_g", "gmlp_ln_b", "gmlp_w_s", "gmlp_b_s",
             "attn_out_g", "gmlp_out_g", "w_out", "ln1_g", "ln1_b", "w_gate_up", "w_down", "ln2_g", "ln2_b"]
    shapes = {"gmlp_w_s": gmlp_w_s.shape, "gmlp_b_s": gmlp_b_s.shape}
    outs = [loss, grad_x[None]]
    for k in range(4):
        for name in order:
            if name in big:
                outs.append(big[name][k][None])
            elif name in shapes:
                outs.append(small[name][k].reshape(shapes[name]))
            else:
                outs.append(small[name][k])
    return tuple(outs)
```

```python
import math

import numpy as np
import jax
import jax.numpy as jnp
from jax import lax
from jax.experimental import pallas as pl
from jax.experimental.pallas import tpu as pltpu

F32 = jnp.float32
BF16 = jnp.bfloat16
MESH = pl.DeviceIdType.MESH

D_MODEL = 1024
N_HEADS = 8
N_KV = 2
HEAD_DIM = 64
ATTN_W = N_HEADS * HEAD_DIM
KV_W = N_KV * HEAD_DIM
N_GROUPS = 8
GROUP_DIM = 64
GMLP_W = N_GROUPS * GROUP_DIM
IN_W = ATTN_W + 2 * KV_W + 2 * GMLP_W
BLOCK = 128
N_BUCKETS = 32
MAX_DISTANCE = 128
D_FF = 2816
ALPHA = 2.0 ** 0.25
LN_EPS = 1e-5
NEG_INF = -1e30
ADAM_LR, ADAM_B1, ADAM_B2, ADAM_EPS, ADAM_WD, ADAM_STEP = 0.001, 0.9, 0.999, 1e-8, 0.01, 10
N_CHIPS = 4
N_DEV = 8
LANES = 128
V7X_VMEM_LIMIT = 56 * 2 ** 20
GELU_C = math.sqrt(2.0 / math.pi)
Q_SCALE = HEAD_DIM ** -0.5
ANY = pl.BlockSpec(memory_space=pl.ANY)

TALL_BS = N_GROUPS * BLOCK
TALL_RB = TALL_BS + 8
TALL_SK = TALL_RB + N_BUCKETS
TALL_ROWS = TALL_SK + 8
WIDE_W = 6 * D_MODEL
WIDE_LAYOUT = {
    "b_ada": (0, 0, 6 * D_MODEL),
    "b_in": (1, 0, IN_W), "ln1_g": (1, IN_W, D_MODEL), "ln1_b": (1, IN_W + D_MODEL, D_MODEL),
    "ln2_g": (1, IN_W + 2 * D_MODEL, D_MODEL), "ln2_b": (1, IN_W + 3 * D_MODEL, D_MODEL),
    "gmlp_ln_g": (2, 0, GMLP_W), "gmlp_ln_b": (2, GMLP_W, GMLP_W), "attn_out_g": (2, 2 * GMLP_W, ATTN_W),
    "gmlp_out_g": (2, 2 * GMLP_W + ATTN_W, GMLP_W), "loss": (2, 3 * GMLP_W + ATTN_W, D_MODEL)}
WIDE_PARAMS = tuple(n for n in WIDE_LAYOUT if n != "loss")


def _params(sem=None):
    return pltpu.CompilerParams(dimension_semantics=sem, vmem_limit_bytes=V7X_VMEM_LIMIT)


def _const_spec(shape, single=False):
    nd = len(shape)
    if single:
        return pl.BlockSpec(shape, lambda *_: (0,) * nd, pipeline_mode=pl.Buffered(1))
    return pl.BlockSpec(shape, lambda *_: (0,) * nd)


def _dot(a, b):
    return jnp.dot(a, b, preferred_element_type=F32)


def _dot_nt(a, b):
    return lax.dot_general(a, b, (((1,), (1,)), ((), ())), preferred_element_type=F32)


def _dot_tn(a, b):
    return lax.dot_general(a, b, (((0,), (0,)), ((), ())), preferred_element_type=F32)


def _gelu(x):
    t = jnp.tanh(GELU_C * (x + 0.044715 * x * x * x))
    return 0.5 * x * (1.0 + t), t


def _gelu_grad(x, t):
    return 0.5 * (1.0 + t) + 0.5 * x * (1.0 - t * t) * GELU_C * (1.0 + 3.0 * 0.044715 * x * x)


def _split_dot(x, a):
    hi = x.astype(BF16)
    lo = (x - hi.astype(F32)).astype(BF16)
    return _dot(hi, a) + _dot(lo, a)


def _group_mean_matrix():
    g = np.arange(GMLP_W) // GROUP_DIM
    return jnp.asarray((g[:, None] == g[None, :]).astype(np.float32) / GROUP_DIM, dtype=BF16)


def _ln_stats(z):
    mu = jnp.mean(z, axis=-1, keepdims=True)
    d = z - mu
    var = jnp.mean(d * d, axis=-1, keepdims=True)
    rstd = lax.rsqrt(var + LN_EPS)
    return d * rstd, rstd


def _ln_bwd(dxhat, xhat, rstd):
    m1 = jnp.mean(dxhat, axis=-1, keepdims=True)
    m2 = jnp.mean(dxhat * xhat, axis=-1, keepdims=True)
    return rstd * (dxhat - m1 - xhat * m2)


def _colsum(x):
    return jnp.sum(x, axis=0, keepdims=True)


def _my_pos():
    return lax.axis_index("x"), lax.axis_index("y"), lax.axis_index("c")


def _other_chips(x, y):
    return [(1 - x, y), (x, 1 - y), (1 - x, 1 - y)]


def _chip_index_scalar():
    ix, iy, _ = _my_pos()
    return jnp.reshape(2 * ix + iy, (1,)).astype(jnp.int32)


def _core_index_scalar():
    return jnp.reshape(lax.axis_index("c"), (1,)).astype(jnp.int32)


class _Gather8:
    def __init__(self, x_refs, out_refs, send_sems, recv_sems, local_sems):
        self.x_refs, self.out_refs = x_refs, out_refs
        self.send_sems, self.recv_sems, self.local_sems = send_sems, recv_sems, local_sems
        self.x, self.y, self.c = _my_pos()
        self.me, self.sibling = (self.x, self.y, self.c), (self.x, self.y, 1 - self.c)
        self.chips = _other_chips(self.x, self.y)

    def _rows(self, a, px, py, pc):
        m_per = self.x_refs[a].shape[0]
        return self.out_refs[a].at[pl.ds((4 * px + 2 * py + pc) * m_per, m_per), :]

    def _copy(self, a, k, block, to, src=None):
        return pltpu.make_async_remote_copy(
            src_ref=self._rows(a, *block) if src is None else src, dst_ref=self._rows(a, *block),
            send_sem=self.send_sems.at[7 * a + k], recv_sem=self.recv_sems.at[7 * a + k], device_id=to,
            device_id_type=MESH)

    def _local(self, a):
        return pltpu.make_async_copy(self.x_refs[a], self._rows(a, *self.me), self.local_sems.at[a])

    def start(self):
        for a in range(len(self.x_refs)):
            self._local(a).start()
            self._copy(a, 0, self.me, self.sibling, src=self.x_refs[a]).start()
            for j, chip in enumerate(self.chips):
                self._copy(a, 1 + j, self.me, (*chip, self.c), src=self.x_refs[a]).start()

    def forward(self):
        for a in range(len(self.x_refs)):
            for j, chip in enumerate(self.chips):
                self._copy(a, 1 + j, (*chip, self.c), self.me).wait_recv()
                self._copy(a, 4 + j, (*chip, self.c), self.sibling).start()

    def finish(self):
        for a in range(len(self.x_refs)):
            self._copy(a, 0, self.sibling, self.me).wait_recv()
            for j, chip in enumerate(self.chips):
                self._copy(a, 4 + j, (*chip, 1 - self.c), self.me).wait_recv()
        for a in range(len(self.x_refs)):
            for k in range(7):
                self._copy(a, k, self.me, self.me).wait_send()
            self._local(a).wait()

    @staticmethod
    def sems(n_v):
        return [pltpu.SemaphoreType.DMA((7 * n_v,)), pltpu.SemaphoreType.DMA((7 * n_v,)),
                pltpu.SemaphoreType.DMA((n_v,))]


def _gathered8_shapes(vs):
    return [jax.ShapeDtypeStruct((N_DEV * v.shape[0], v.shape[1]), v.dtype) for v in vs]


VMEM_WHOLE = pl.BlockSpec(memory_space=pltpu.VMEM)


def _prologue(c_pad, w_ada_s, b_ada_s, w_in_s):
    n = w_ada_s.shape[1]

    def body(c_ref, w_ref, b_ref, win_ref, sc_ref, modc_ref, modg_ref, wing_ref, call_ref, *sems):
        weights = _WeightGather([win_ref], [wing_ref], ["blk"], sems[0], sems[1])
        gather_c = _Gather8([c_ref], [call_ref], sems[2], sems[3], sems[4])
        gather_mod = _Gather8([modc_ref], [modg_ref], sems[5], sems[6], sems[7])
        weights.start()
        gather_c.start()
        gather_c.forward()
        gather_c.finish()
        cv = call_ref[...]
        sc = cv * _sigmoid(cv)
        a_hi = sc.astype(BF16)
        a_lo = (sc - a_hi.astype(F32)).astype(BF16)
        w = w_ref[...]
        w_hi = w.astype(BF16)
        w_lo = (w - w_hi.astype(F32)).astype(BF16)
        mod = _dot(a_hi, w_hi) + _dot(a_hi, w_lo) + _dot(a_lo, w_hi) + b_ref[...]
        for d in range(N_DEV):
            sc_ref[d:d + 1, :] = sc[8 * d:8 * d + 1, :]
            modc_ref[d:d + 1, :] = mod[8 * d:8 * d + 1, :]
        gather_mod.start()
        gather_mod.forward()
        gather_mod.finish()
        weights.forward()
        weights.forward_diagonal()
        weights.finish()

    return pl.pallas_call(
        body, name="prologue",
        out_shape=(jax.ShapeDtypeStruct((N_DEV, D_MODEL), F32), jax.ShapeDtypeStruct((N_DEV, n), F32),
                   jax.ShapeDtypeStruct((N_DEV * N_DEV, n), F32),
                   jax.ShapeDtypeStruct(_gathered_shape(w_in_s, "blk"), BF16)),
        in_specs=[VMEM_WHOLE, VMEM_WHOLE, VMEM_WHOLE, ANY],
        out_specs=(VMEM_WHOLE, VMEM_WHOLE, VMEM_WHOLE, ANY),
        scratch_shapes=[pltpu.VMEM((N_DEV * 8, D_MODEL), F32)] + _WeightGather.sems(1) + _Gather8.sems(1)
        + _Gather8.sems(1),
        compiler_params=pltpu.CompilerParams(vmem_limit_bytes=V7X_VMEM_LIMIT),
    )(c_pad, w_ada_s, b_ada_s, w_in_s)


def _gathered_shape(shard, kind):
    r, cc = shard.shape
    return (N_CHIPS, r, cc) if kind == "blk" else (r, N_CHIPS * cc)


class _WeightGather:
    N_SEM = 8

    def __init__(self, shards, gathered, kinds, send_sems, recv_sems):
        self.shards, self.gathered, self.kinds = shards, gathered, kinds
        self.send_sems, self.recv_sems = send_sems, recv_sems
        self.x, self.y, self.c = _my_pos()
        self.me, self.sibling = (self.x, self.y, self.c), (self.x, self.y, 1 - self.c)
        self.nbr = ((1 - self.x, self.y), (self.x, 1 - self.y))
        self.diag = 2 * (1 - self.x) + (1 - self.y)

    def _dst(self, a, chip, pc, quarter=None):
        r, cc = self.shards[a].shape
        h = r // 2
        row0, rows = pc * h, h
        if quarter is not None:
            row0, rows = pc * h + quarter * (h // 2), h // 2
        g = self.gathered[a]
        if self.kinds[a] == "blk":
            return g.at[chip, pl.ds(row0, rows), :]
        return g.at[pl.ds(row0, rows), pl.ds(chip * cc, cc)]

    def _copy(self, a, k, region, to, src=None):
        return pltpu.make_async_remote_copy(
            src_ref=region if src is None else src, dst_ref=region, send_sem=self.send_sems.at[a * self.N_SEM + k],
            recv_sem=self.recv_sems.at[a * self.N_SEM + k], device_id=to, device_id_type=MESH)

    def _arrays(self):
        return range(len(self.shards))

    def start(self):
        my_chip = 2 * self.x + self.y
        for a in self._arrays():
            h = self.shards[a].shape[0] // 2
            mine = self.shards[a].at[pl.ds(self.c * h, h), :]
            for j, chip in enumerate(self.nbr):
                self._copy(a, j, self._dst(a, my_chip, self.c), (*chip, self.c), src=mine).start()

    def forward(self):
        for a in self._arrays():
            for j, chip in enumerate(self.nbr):
                cj = 2 * chip[0] + chip[1]
                half = self._dst(a, cj, self.c)
                self._copy(a, j, half, self.me).wait_recv()
                self._copy(a, 2 + j, half, self.sibling).start()
                other = self.nbr[1 - j]
                self._copy(a, 4 + j, self._dst(a, cj, self.c, quarter=j), (*other, self.c)).start()

    def forward_diagonal(self):
        for a in self._arrays():
            for j in range(2):
                quarter = self._dst(a, self.diag, self.c, quarter=j)
                self._copy(a, 4 + j, quarter, self.me).wait_recv()
                self._copy(a, 6 + j, quarter, self.sibling).start()

    def finish(self):
        for a in self._arrays():
            for j, chip in enumerate(self.nbr):
                self._copy(a, 2 + j, self._dst(a, 2 * chip[0] + chip[1], 1 - self.c), self.me).wait_recv()
                self._copy(a, 6 + j, self._dst(a, self.diag, 1 - self.c, quarter=j), self.me).wait_recv()
        for a in self._arrays():
            half = self._dst(a, self.diag, self.c)
            quarter = self._dst(a, self.diag, self.c, quarter=0)
            for k in range(self.N_SEM):
                self._copy(a, k, half if k < 4 else quarter, self.me).wait_send()

    @classmethod
    def sems(cls, n_arr):
        return [pltpu.SemaphoreType.DMA((n_arr * cls.N_SEM,)), pltpu.SemaphoreType.DMA((n_arr * cls.N_SEM,))]


def _insert_own(gathered, shard, kind, chip):
    if kind == "blk":
        return lax.dynamic_update_slice(gathered, shard[None], (chip, 0, 0))
    return lax.dynamic_update_slice(gathered, shard, (0, chip * shard.shape[1]))


def _half_of_full(ref, kind, pc):
    if kind == "blk":
        h = ref.shape[1] // 2
        return ref.at[:, pl.ds(pc * h, h), :]
    h = ref.shape[0] // 2
    return ref.at[pl.ds(pc * h, h), :]


def _half_shape(shape, kind):
    return (shape[0], shape[1] // 2, shape[2]) if kind == "blk" else (shape[0] // 2, shape[1])


class _HalfSwap:
    def __init__(self, ins, outs, kinds, send_sems, recv_sems):
        self.ins, self.outs, self.kinds = ins, outs, kinds
        self.send_sems, self.recv_sems = send_sems, recv_sems
        self.x, self.y, self.c = _my_pos()

    def _copies(self):
        for a in range(len(self.ins)):
            yield pltpu.make_async_remote_copy(
                src_ref=_half_of_full(self.ins[a], self.kinds[a], 1 - self.c), dst_ref=self.outs[a],
                send_sem=self.send_sems.at[a], recv_sem=self.recv_sems.at[a],
                device_id=(self.x, self.y, 1 - self.c), device_id_type=MESH)

    def start(self):
        for cp in self._copies():
            cp.start()

    def wait(self):
        for cp in self._copies():
            cp.wait()

    @staticmethod
    def sems(n_arr):
        return [pltpu.SemaphoreType.DMA((n_arr,)), pltpu.SemaphoreType.DMA((n_arr,))]

    @staticmethod
    def out_shapes(fulls, kinds):
        return [jax.ShapeDtypeStruct(_half_shape(a.shape, k), a.dtype) for a, k in zip(fulls, kinds)]


def _swap_halves(fulls_bf16, kinds, name):
    n_arr = len(fulls_bf16)

    def body(*refs):
        swap = _HalfSwap(refs[:n_arr], refs[n_arr:2 * n_arr], kinds, *refs[2 * n_arr:])
        swap.start()
        swap.wait()

    return pl.pallas_call(
        body, name=name, out_shape=_HalfSwap.out_shapes(fulls_bf16, kinds),
        in_specs=[ANY] * n_arr, out_specs=[ANY] * n_arr, scratch_shapes=_HalfSwap.sems(n_arr),
    )(*fulls_bf16)


def _add_halves(full, got, kind, name):
    hs = _half_shape(full.shape, kind)

    def body(c_ref, a_ref, b_ref, o_ref, ob_ref):
        p = a_ref[...] + b_ref[...].astype(F32)
        o_ref[...] = p
        ob_ref[...] = p.astype(BF16)

    if kind == "blk":
        nb, h, cc = hs
        own = pl.BlockSpec((1, h, cc), lambda b, c_ref: (b, c_ref[0], 0))
        other = pl.BlockSpec((1, h, cc), lambda b, c_ref: (b, 0, 0))
    else:
        h, cc = hs[0], hs[1] // N_CHIPS
        own = pl.BlockSpec((h, cc), lambda b, c_ref: (c_ref[0], b))
        other = pl.BlockSpec((h, cc), lambda b, c_ref: (0, b))
    return pl.pallas_call(
        body, name=name, out_shape=(jax.ShapeDtypeStruct(hs, F32), jax.ShapeDtypeStruct(hs, BF16)),
        grid_spec=pltpu.PrefetchScalarGridSpec(
            num_scalar_prefetch=1, grid=(N_CHIPS,), in_specs=[own, other], out_specs=(other, other)),
        compiler_params=_params(("arbitrary",)),
    )(_core_index_scalar(), full, got)


def _rx_shape(part_shape, kind):
    if kind == "blk":
        return (3, part_shape[1], part_shape[2])
    return (3, part_shape[0], part_shape[1] // N_CHIPS)


class _ChipExchange:
    def __init__(self, parts, rxs, kinds, send_sems, recv_sems):
        self.parts, self.rxs, self.kinds = parts, rxs, kinds
        self.send_sems, self.recv_sems = send_sems, recv_sems
        self.x, self.y, self.c = _my_pos()
        self.chips = _other_chips(self.x, self.y)

    def _copies(self):
        for a in range(len(self.parts)):
            for j, chip in enumerate(self.chips):
                cj = 2 * chip[0] + chip[1]
                if self.kinds[a] == "blk":
                    src = self.parts[a].at[cj]
                else:
                    cc = self.parts[a].shape[1] // N_CHIPS
                    src = self.parts[a].at[:, pl.ds(cj * cc, cc)]
                yield pltpu.make_async_remote_copy(
                    src_ref=src, dst_ref=self.rxs[a].at[j], send_sem=self.send_sems.at[a * 3 + j],
                    recv_sem=self.recv_sems.at[a * 3 + j], device_id=(*chip, self.c), device_id_type=MESH)

    def start(self):
        for cp in self._copies():
            cp.start()

    def wait(self):
        for cp in self._copies():
            cp.wait_recv()
        for cp in self._copies():
            cp.wait_send()

    @staticmethod
    def sems(n_arr):
        return [pltpu.SemaphoreType.DMA((n_arr * 3,)), pltpu.SemaphoreType.DMA((n_arr * 3,))]


def _exchange_chip_partials(parts, kinds, name):
    n_arr = len(parts)

    def body(*refs):
        exchange = _ChipExchange(refs[:n_arr], refs[n_arr:2 * n_arr], kinds, *refs[2 * n_arr:])
        exchange.start()
        exchange.wait()

    return pl.pallas_call(
        body, name=name,
        out_shape=[jax.ShapeDtypeStruct(_rx_shape(p.shape, k), BF16) for p, k in zip(parts, kinds)],
        in_specs=[ANY] * n_arr, out_specs=[ANY] * n_arr, scratch_shapes=_ChipExchange.sems(n_arr),
    )(*parts)


def _sum_chips(part, rx, kind, tr, name):
    _, h, cc = rx.shape
    flips = (2, 1, 3)

    def body(chip_ref, p_ref, rx_ref, o_ref):
        own = p_ref[...].reshape(tr, cc)
        for mc in range(N_CHIPS):
            @pl.when(chip_ref[0] == mc)
            def _():
                terms = sorted([(mc, None)] + [(mc ^ f, j) for j, f in enumerate(flips)])
                acc = None
                for _, j in terms:
                    t = own if j is None else rx_ref[j].astype(F32)
                    acc = t if acc is None else acc + t
                o_ref[...] = acc

    if kind == "blk":
        own_spec = pl.BlockSpec((1, tr, cc), lambda i, chip_ref: (chip_ref[0], i, 0))
    else:
        own_spec = pl.BlockSpec((tr, cc), lambda i, chip_ref: (i, chip_ref[0]))
    return pl.pallas_call(
        body, name=name, out_shape=jax.ShapeDtypeStruct((h, cc), F32),
        grid_spec=pltpu.PrefetchScalarGridSpec(
            num_scalar_prefetch=1, grid=(h // tr,),
            in_specs=[own_spec, pl.BlockSpec((3, tr, cc), lambda i, chip_ref: (0, i, 0))],
            out_specs=pl.BlockSpec((tr, cc), lambda i, chip_ref: (i, 0))),
        compiler_params=_params(("arbitrary",)),
    )(_chip_index_scalar(), part, rx)


def _share_halves(halves, name):
    n_arr = len(halves)

    def body(*refs):
        ins, outs = refs[:n_arr], refs[n_arr:2 * n_arr]
        send_sems, recv_sems = refs[2 * n_arr:]
        x, y, c = _my_pos()
        cps = []
        for a in range(n_arr):
            cp = pltpu.make_async_remote_copy(
                src_ref=ins[a], dst_ref=outs[a], send_sem=send_sems.at[a], recv_sem=recv_sems.at[a],
                device_id=(x, y, 1 - c), device_id_type=MESH)
            cp.start()
            cps.append(cp)
        for cp in cps:
            cp.wait()

    return pl.pallas_call(
        body, name=name, out_shape=[jax.ShapeDtypeStruct(h.shape, h.dtype) for h in halves],
        in_specs=[ANY] * n_arr, out_specs=[ANY] * n_arr,
        scratch_shapes=[pltpu.SemaphoreType.DMA((n_arr,)), pltpu.SemaphoreType.DMA((n_arr,))],
    )(*halves)


def _bucket_table():
    qi = jnp.arange(BLOCK)[:, None]
    si = jnp.arange(2 * BLOCK)[None, :]
    dist = qi + BLOCK - si
    max_exact = N_BUCKETS // 2
    n = jnp.maximum(dist, 0)
    nf = jnp.maximum(n, max_exact).astype(F32)
    large = max_exact + (jnp.log(nf / max_exact) / math.log(MAX_DISTANCE / max_exact)
                         * (N_BUCKETS - max_exact)).astype(jnp.int32)
    large = jnp.minimum(large, N_BUCKETS - 1)
    return jnp.where(n < max_exact, n, large).astype(F32)


def _prep_tables(bucket, rel_bias, w_s):
    def body(bucket_ref, rb_ref, ws_ref, bias_ref, wsm_ref):
        qi = lax.broadcasted_iota(jnp.int32, (BLOCK, 2 * BLOCK), 0)
        si = lax.broadcasted_iota(jnp.int32, (BLOCK, 2 * BLOCK), 1)
        dist = qi + BLOCK - si
        in_window = (dist >= 0) & (dist < BLOCK)
        bk = bucket_ref[...]
        for h in range(N_HEADS):
            acc = jnp.zeros((BLOCK, 2 * BLOCK), F32)
            for b in range(N_BUCKETS):
                acc = jnp.where(bk == float(b), rb_ref[b, h], acc)
            bias_ref[h] = jnp.where(in_window, acc, NEG_INF)
        ti = lax.broadcasted_iota(jnp.int32, (BLOCK, BLOCK), 0)
        ui = lax.broadcasted_iota(jnp.int32, (BLOCK, BLOCK), 1)
        for g in range(N_GROUPS):
            wsm_ref[g] = jnp.where(ti >= ui, ws_ref[g], 0.0).astype(BF16)

    return pl.pallas_call(
        body, name="prep_tables",
        out_shape=(jax.ShapeDtypeStruct((N_HEADS, BLOCK, 2 * BLOCK), F32),
                   jax.ShapeDtypeStruct((N_GROUPS, BLOCK, BLOCK), BF16)),
        grid=(1,),
        in_specs=[_const_spec((BLOCK, 2 * BLOCK)), pl.BlockSpec(memory_space=pltpu.SMEM),
                  _const_spec((N_GROUPS, BLOCK, BLOCK))],
        out_specs=(_const_spec((N_HEADS, BLOCK, 2 * BLOCK)), _const_spec((N_GROUPS, BLOCK, BLOCK))),
        compiler_params=_params(("arbitrary",)),
    )(bucket, rel_bias, w_s)


def _fwd_in(x, modr, w_in, b_in, tm, shards, kinds):
    s = x.shape[0]
    n_steps = s // tm
    fwd_step, diag_step = (11 * n_steps) // 16, (15 * n_steps) // 16
    n_w = len(shards)

    def body(x_ref, mod_ref, w_ref, b_ref, *rest):
        shard_refs = rest[:n_w]
        h1_ref, q_ref, kv_ref, gu_ref, gv_ref = rest[n_w:n_w + 5]
        gathered_refs = rest[n_w + 5:2 * n_w + 5]
        send_sems, recv_sems = rest[2 * n_w + 5:]
        i = pl.program_id(0)
        gather = _WeightGather(shard_refs, gathered_refs, kinds, send_sems, recv_sems)

        @pl.when(i == 0)
        def _():
            gather.start()

        h1 = (x_ref[...] * (1.0 + mod_ref[1:2, :]) + mod_ref[0:1, :]).astype(BF16)
        h1_ref[...] = h1
        proj = _dot(h1, w_ref[...]) + b_ref[...]
        q_ref[...] = (proj[:, :ATTN_W] * Q_SCALE).astype(BF16)
        kv_ref[...] = proj[:, ATTN_W:ATTN_W + 2 * KV_W].astype(BF16)
        gu_ref[...] = proj[:, ATTN_W + 2 * KV_W:ATTN_W + 2 * KV_W + GMLP_W]
        gv_ref[...] = proj[:, ATTN_W + 2 * KV_W + GMLP_W:]

        @pl.when(i == fwd_step)
        def _():
            gather.forward()

        @pl.when(i == diag_step)
        def _():
            gather.forward_diagonal()

        @pl.when(i == n_steps - 1)
        def _():
            gather.finish()

    row = lambda w: pl.BlockSpec((tm, w), lambda i: (i, 0))
    outs = pl.pallas_call(
        body, name="fwd_in",
        out_shape=[jax.ShapeDtypeStruct((s, D_MODEL), BF16), jax.ShapeDtypeStruct((s, ATTN_W), BF16),
                   jax.ShapeDtypeStruct((s, 2 * KV_W), BF16), jax.ShapeDtypeStruct((s, GMLP_W), F32),
                   jax.ShapeDtypeStruct((s, GMLP_W), F32)]
        + [jax.ShapeDtypeStruct(_gathered_shape(sh, k), BF16) for sh, k in zip(shards, kinds)],
        grid=(n_steps,),
        in_specs=[row(D_MODEL), _const_spec((8, D_MODEL)), _const_spec((D_MODEL, IN_W)), _const_spec((1, IN_W))]
        + [ANY] * n_w,
        out_specs=[row(D_MODEL), row(ATTN_W), row(2 * KV_W), row(GMLP_W), row(GMLP_W)] + [ANY] * n_w,
        scratch_shapes=_WeightGather.sems(n_w),
        compiler_params=_params(("arbitrary",)),
    )(x, modr, w_in, b_in, *shards)
    return outs[:5], outs[5:]


def _kv_variants(kk):
    kf = kk.astype(F32)
    lane = lax.broadcasted_iota(jnp.int32, kf.shape, 1)
    low = lane < HEAD_DIM
    k0_lo = jnp.where(low, kf, 0.0)
    k1_hi = jnp.where(low, 0.0, kf)
    k0_hi = pltpu.roll(k0_lo, HEAD_DIM, 1)
    k1_lo = pltpu.roll(k1_hi, HEAD_DIM, 1)
    return ((k0_lo.astype(BF16), k0_hi.astype(BF16)), (k1_lo.astype(BF16), k1_hi.astype(BF16)))


def _head_kv(h):
    return h // (N_HEADS // N_KV), h % 2


MIX_GROUP = 2


def _interleave(*gens):
    results = [None] * len(gens)
    active = list(enumerate(gens))
    while active:
        still = []
        for i, g in active:
            try:
                next(g)
                still.append((i, g))
            except StopIteration as done:
                results[i] = done.value
        active = still
    return results


def _attn_block_fwd(q_blk, kk, vv, bias_ref, sinks_ref, first_mask):
    kvar = _kv_variants(kk)
    vvar = _kv_variants(vv)
    heads = range(N_HEADS)
    q_pairs = [q_blk[:, (h // 2) * LANES:(h // 2 + 1) * LANES] for h in heads]
    logits = [_dot_nt(q_pairs[h], kvar[_head_kv(h)[0]][_head_kv(h)[1]]) + bias_ref[h] for h in heads]
    if first_mask is not None:
        logits = [jnp.where(first_mask, NEG_INF, lg) for lg in logits]
    yield
    ms = [jnp.maximum(jnp.max(logits[h], axis=-1, keepdims=True), sinks_ref[h]) for h in heads]
    yield
    es = [jnp.exp(logits[h] - ms[h]) for h in heads]
    ess = [jnp.exp(sinks_ref[h] - ms[h]) for h in heads]
    yield
    invs = [1.0 / (jnp.sum(es[h], axis=-1, keepdims=True) + ess[h]) for h in heads]
    probs = [(es[h] * invs[h], ess[h] * invs[h]) for h in heads]
    yield
    outs = [_dot(probs[h][0].astype(BF16), vvar[_head_kv(h)[0]][_head_kv(h)[1]]) for h in heads]
    pairs = [outs[2 * i] + outs[2 * i + 1] for i in range(N_HEADS // 2)]
    return jnp.concatenate(pairs, axis=1), probs, kvar, vvar


def _gmlp_chunk_fwd(gu, gv, ln_g, ln_b, wsm_ref, bsx, amat):
    u, tu = _gelu(gu)
    a, ta = _gelu(gv)
    yield
    mean = _split_dot(a, amat)
    d = a - mean
    yield
    var = _split_dot(d * d, amat)
    yield
    rstd = lax.rsqrt(var + LN_EPS)
    xhat = d * rstd
    vb = (xhat * ln_g + ln_b).astype(BF16)
    yield
    lane = lax.broadcasted_iota(jnp.int32, (BLOCK, LANES), 1)
    low = lane < GROUP_DIM
    cols = []
    for pair in range(N_GROUPS // 2):
        vp = vb[:, pair * LANES:(pair + 1) * LANES]
        cols.append(jnp.where(low, _dot(wsm_ref[2 * pair], vp), _dot(wsm_ref[2 * pair + 1], vp)))
    mixedv = jnp.concatenate(cols, axis=1) + bsx
    return u * mixedv, (u, tu, ta, xhat, rstd, vb, mixedv)


def _rms(a, g):
    r = lax.rsqrt(jnp.mean(a * a, axis=-1, keepdims=True) + LN_EPS)
    return a * r * g, r


def _fwd_mix(q, kv, gu, gv, x, modr, bias, sinks, gln_g, gln_b, wsm, bsx, amat, aog, gog, w_out, ln1_g, ln1_b, tm,
             ffn_shards, ffn_kinds):
    s = x.shape[0]
    nb = tm // BLOCK
    n_steps = s // tm
    fwd_step, diag_step = (11 * n_steps) // 16, (15 * n_steps) // 16
    n_w = len(ffn_shards)

    def body(q_ref, kv_ref, kvp_ref, gu_ref, gv_ref, x_ref, mod_ref, bias_ref, sinks_ref, glng_ref, glnb_ref, wsm_ref,
             bsx_ref, amat_ref, aog_ref, gog_ref, wout_ref, ln1g_ref, ln1b_ref, *rest):
        shard_refs = rest[:n_w]
        x1_ref, y_ref, mixed_ref = rest[n_w:n_w + 3]
        gathered_refs = rest[n_w + 3:2 * n_w + 3]
        mix_scr, send_sems, recv_sems = rest[2 * n_w + 3:]
        i = pl.program_id(0)
        gather = _WeightGather(shard_refs, gathered_refs, ffn_kinds, send_sems, recv_sems)

        @pl.when(i == 0)
        def _():
            gather.start()

        col = lax.broadcasted_iota(jnp.int32, (BLOCK, 2 * BLOCK), 1)
        for b0 in range(0, nb, MIX_GROUP):
            gens = []
            for b in range(b0, min(b0 + MIX_GROUP, nb)):
                r0 = b * BLOCK
                if b == 0:
                    kvprev = kvp_ref[...]
                    first_mask = (col < BLOCK) & (i == 0)
                else:
                    kvprev = kv_ref[r0 - BLOCK:r0, :]
                    first_mask = None
                kvcur = kv_ref[r0:r0 + BLOCK, :]
                kk = jnp.concatenate([kvprev[:, :KV_W], kvcur[:, :KV_W]], axis=0)
                vv = jnp.concatenate([kvprev[:, KV_W:], kvcur[:, KV_W:]], axis=0)
                gens.append(_attn_block_fwd(q_ref[r0:r0 + BLOCK, :], kk, vv, bias_ref, sinks_ref, first_mask))
                gens.append(_gmlp_chunk_fwd(gu_ref[r0:r0 + BLOCK, :], gv_ref[r0:r0 + BLOCK, :], glng_ref[...],
                                            glnb_ref[...], wsm_ref, bsx_ref[...], amat_ref[...]))
            res = _interleave(*gens)
            for k, b in enumerate(range(b0, min(b0 + MIX_GROUP, nb))):
                r0 = b * BLOCK
                na, _ = _rms(res[2 * k][0], aog_ref[...])
                ng, _ = _rms(res[2 * k + 1][0], gog_ref[...])
                mix_scr[r0:r0 + BLOCK, :ATTN_W] = na.astype(BF16)
                mix_scr[r0:r0 + BLOCK, ATTN_W:] = ng.astype(BF16)
        mixed = mix_scr[...]
        mixed_ref[...] = mixed
        y = _dot(mixed, wout_ref[...])
        y_ref[...] = y
        z1 = ALPHA * x_ref[...] + mod_ref[2:3, :] * y
        xhat, _ = _ln_stats(z1)
        x1_ref[...] = xhat * ln1g_ref[...] + ln1b_ref[...]

        @pl.when(i == fwd_step)
        def _():
            gather.forward()

        @pl.when(i == diag_step)
        def _():
            gather.forward_diagonal()

        @pl.when(i == n_steps - 1)
        def _():
            gather.finish()

    row = lambda w: pl.BlockSpec((tm, w), lambda i: (i, 0))
    prev = pl.BlockSpec((BLOCK, 2 * KV_W), lambda i: (jnp.maximum(i * nb - 1, 0), 0))
    outs = pl.pallas_call(
        body, name="fwd_mix",
        out_shape=[jax.ShapeDtypeStruct((s, D_MODEL), F32), jax.ShapeDtypeStruct((s, D_MODEL), F32),
                   jax.ShapeDtypeStruct((s, D_MODEL), BF16)]
        + [jax.ShapeDtypeStruct(_gathered_shape(sh, k), BF16) for sh, k in zip(ffn_shards, ffn_kinds)],
        grid=(n_steps,),
        in_specs=[row(ATTN_W), row(2 * KV_W), prev, row(GMLP_W), row(GMLP_W), row(D_MODEL), _const_spec((8, D_MODEL)),
                  _const_spec((N_HEADS, BLOCK, 2 * BLOCK)), pl.BlockSpec(memory_space=pltpu.SMEM),
                  _const_spec((1, GMLP_W)), _const_spec((1, GMLP_W)), _const_spec((N_GROUPS, BLOCK, BLOCK)),
                  _const_spec((BLOCK, GMLP_W)), _const_spec((GMLP_W, GMLP_W)), _const_spec((1, ATTN_W)),
                  _const_spec((1, GMLP_W)), _const_spec((D_MODEL, D_MODEL)), _const_spec((1, D_MODEL)),
                  _const_spec((1, D_MODEL))] + [ANY] * n_w,
        out_specs=[row(D_MODEL), row(D_MODEL), row(D_MODEL)] + [ANY] * n_w,
        scratch_shapes=[pltpu.VMEM((tm, D_MODEL), BF16)] + _WeightGather.sems(n_w),
        compiler_params=_params(("arbitrary",)),
    )(q, kv, kv, gu, gv, x, modr, bias, sinks, gln_g, gln_b, wsm, bsx, amat, aog, gog, w_out, ln1_g, ln1_b, *ffn_shards)
    return outs[0], outs[1], outs[2], outs[3:]


FF_BLOCKS = N_CHIPS // 2
FF_CHUNK = D_FF // FF_BLOCKS
FFN_SUB = 256


def _sigmoid(x):
    return 1.0 / (1.0 + jnp.exp(-x))


def _fwd_ffn(x1, target, modr, ln2_g, ln2_b, w_gu, w_dn, tm):
    s = x1.shape[0]

    def body(x1_ref, t_ref, mod_ref, g_ref, b_ref, wgu_ref, wdn_ref, h2_ref, act_ref, dy2_ref, dx1a_ref, acc_ref):
        @pl.when(pl.program_id(0) == 0)
        def _():
            acc_ref[...] = jnp.zeros_like(acc_ref)

        x1v = x1_ref[...]
        h2 = (x1v * (1.0 + mod_ref[4:5, :]) + mod_ref[3:4, :]).astype(BF16)
        h2_ref[...] = h2
        y2 = None
        for cc in range(FF_BLOCKS):
            c0 = cc * FF_CHUNK
            gate = _dot(h2, wgu_ref[cc])
            up = _dot(h2, wgu_ref[FF_BLOCKS + cc])
            act_ref[:, c0:c0 + FF_CHUNK] = gate.astype(BF16)
            act_ref[:, D_FF + c0:D_FF + c0 + FF_CHUNK] = up.astype(BF16)
            a = (gate * _sigmoid(gate) * up).astype(BF16)
            part = _dot(a, wdn_ref[c0:c0 + FF_CHUNK, :])
            y2 = part if y2 is None else y2 + part
        g2 = mod_ref[5:6, :]
        z2 = ALPHA * x1v + g2 * y2
        xhat, rstd = _ln_stats(z2)
        gain = g_ref[...]
        diff = xhat * gain + b_ref[...] - t_ref[...]
        dx2 = diff * (1.0 / D_MODEL)
        dz2 = _ln_bwd(dx2 * gain, xhat, rstd)
        dx1a_ref[...] = ALPHA * dz2
        dy2_ref[...] = (g2 * dz2).astype(BF16)
        acc_ref[0:1, :] += _colsum(diff * diff)
        acc_ref[1:2, :] += _colsum(dx2 * xhat)
        acc_ref[2:3, :] += _colsum(dx2)
        acc_ref[3:4, :] += _colsum(dz2 * y2)

    row = lambda w: pl.BlockSpec((tm, w), lambda i: (i, 0))
    return pl.pallas_call(
        body, name="fwd_ffn",
        out_shape=(jax.ShapeDtypeStruct((s, D_MODEL), BF16), jax.ShapeDtypeStruct((s, 2 * D_FF), BF16),
                   jax.ShapeDtypeStruct((s, D_MODEL), BF16), jax.ShapeDtypeStruct((s, D_MODEL), F32),
                   jax.ShapeDtypeStruct((8, D_MODEL), F32)),
        grid=(s // tm,),
        in_specs=[row(D_MODEL), row(D_MODEL), _const_spec((8, D_MODEL)), _const_spec((1, D_MODEL)),
                  _const_spec((1, D_MODEL)), _const_spec((N_CHIPS, D_MODEL, FF_CHUNK), single=True),
                  _const_spec((D_FF, D_MODEL), single=True)],
        out_specs=(row(D_MODEL), row(2 * D_FF), row(D_MODEL), row(D_MODEL), _const_spec((8, D_MODEL))),
        compiler_params=_params(("arbitrary",)),
    )(x1, target, modr, ln2_g, ln2_b, w_gu, w_dn)


def _bwd_ffn(dy2, act, w_gu, w_dn, tm):
    s = dy2.shape[0]

    def body(dy2_ref, act_ref, wgu_ref, wdn_ref, a_ref, dgu_ref, dh2_ref):
        dy2v = dy2_ref[...]
        dh2 = None
        for cc in range(FF_BLOCKS):
            c0 = cc * FF_CHUNK
            da = _dot_nt(dy2v, wdn_ref[c0:c0 + FF_CHUNK, :])
            gate = act_ref[:, c0:c0 + FF_CHUNK].astype(F32)
            up = act_ref[:, D_FF + c0:D_FF + c0 + FF_CHUNK].astype(F32)
            sg = _sigmoid(gate)
            sl = gate * sg
            a_ref[:, c0:c0 + FF_CHUNK] = (sl * up).astype(BF16)
            dgate = (da * up * (sg * (1.0 + gate * (1.0 - sg)))).astype(BF16)
            dup = (da * sl).astype(BF16)
            dgu_ref[:, c0:c0 + FF_CHUNK] = dgate
            dgu_ref[:, D_FF + c0:D_FF + c0 + FF_CHUNK] = dup
            part = _dot_nt(dgate, wgu_ref[cc]) + _dot_nt(dup, wgu_ref[FF_BLOCKS + cc])
            dh2 = part if dh2 is None else dh2 + part
        dh2_ref[...] = dh2

    row = lambda w: pl.BlockSpec((tm, w), lambda i: (i, 0))
    return pl.pallas_call(
        body, name="bwd_ffn",
        out_shape=(jax.ShapeDtypeStruct((s, D_FF), BF16), jax.ShapeDtypeStruct((s, 2 * D_FF), BF16),
                   jax.ShapeDtypeStruct((s, D_MODEL), F32)),
        grid=(s // tm,),
        in_specs=[row(D_MODEL), row(2 * D_FF), _const_spec((N_CHIPS, D_MODEL, FF_CHUNK), single=True),
                  _const_spec((D_FF, D_MODEL), single=True)],
        out_specs=(row(D_FF), row(2 * D_FF), row(D_MODEL)),
        compiler_params=_params(("parallel",)),
    )(dy2, act, w_gu, w_dn)


def _bwd_mid(dh2, dx1a, x1, x, y, modr, ln1_g, w_out, tm, swap_fulls, swap_kinds):
    s = x.shape[0]
    n_steps = s // tm
    n_g = len(swap_fulls)

    def body(dh2_ref, dx1a_ref, x1_ref, x_ref, y_ref, mod_ref, g_ref, wout_ref, *rest):
        full_refs = rest[:n_g]
        dxa_ref, dy_ref, dmix_ref, acc_ref = rest[n_g:n_g + 4]
        got_refs = rest[n_g + 4:2 * n_g + 4]
        swap = _HalfSwap(full_refs, got_refs, swap_kinds, *rest[2 * n_g + 4:])
        i = pl.program_id(0)

        @pl.when(i == 0)
        def _():
            swap.start()
            acc_ref[...] = jnp.zeros_like(acc_ref)

        dh2 = dh2_ref[...]
        x1v = x1_ref[...]
        yv = y_ref[...]
        g1 = mod_ref[2:3, :]
        dx1 = dx1a_ref[...] + dh2 * (1.0 + mod_ref[4:5, :])
        z1 = ALPHA * x_ref[...] + g1 * yv
        xhat, rstd = _ln_stats(z1)
        dz1 = _ln_bwd(dx1 * g_ref[...], xhat, rstd)
        dxa_ref[...] = ALPHA * dz1
        dy = (g1 * dz1).astype(BF16)
        dy_ref[...] = dy
        dmix_ref[...] = _dot_nt(dy, wout_ref[...])
        acc_ref[0:1, :] += _colsum(dh2 * x1v)
        acc_ref[1:2, :] += _colsum(dh2)
        acc_ref[2:3, :] += _colsum(dx1 * xhat)
        acc_ref[3:4, :] += _colsum(dx1)
        acc_ref[4:5, :] += _colsum(dz1 * yv)

        @pl.when(i == n_steps - 1)
        def _():
            swap.wait()

    row = lambda w: pl.BlockSpec((tm, w), lambda i: (i, 0))
    outs = pl.pallas_call(
        body, name="bwd_mid",
        out_shape=[jax.ShapeDtypeStruct((s, D_MODEL), F32), jax.ShapeDtypeStruct((s, D_MODEL), BF16),
                   jax.ShapeDtypeStruct((s, D_MODEL), F32), jax.ShapeDtypeStruct((8, D_MODEL), F32)]
        + _HalfSwap.out_shapes(swap_fulls, swap_kinds),
        grid=(n_steps,),
        in_specs=[row(D_MODEL)] * 5 + [_const_spec((8, D_MODEL)), _const_spec((1, D_MODEL)),
                                       _const_spec((D_MODEL, D_MODEL))] + [ANY] * n_g,
        out_specs=[row(D_MODEL), row(D_MODEL), row(D_MODEL), _const_spec((8, D_MODEL))] + [ANY] * n_g,
        scratch_shapes=_HalfSwap.sems(n_g),
        compiler_params=_params(("arbitrary",)),
    )(dh2, dx1a, x1, x, y, modr, ln1_g, w_out, *swap_fulls)
    return outs[:4], outs[4:]


def _fold_kv(t0, t1):
    lane = lax.broadcasted_iota(jnp.int32, t0.shape, 1)
    f0 = t0 + pltpu.roll(t0, HEAD_DIM, 1)
    f1 = t1 + pltpu.roll(t1, HEAD_DIM, 1)
    return jnp.where(lane < HEAD_DIM, f0, f1)


def _bwd_mix(q, kv, gu, gv, dmix, bias, sinks, gln_g, gln_b, wsm, bsx, amat, aog, gog, grad_parts, grad_kinds):
    s = q.shape[0]
    tile = 2 * BLOCK
    n_steps = s // tile
    n_g = len(grad_parts)

    def body(q_ref, kv_ref, kvp_ref, gu_ref, gv_ref, dmix_ref, bias_ref, sinks_ref, glng_ref, glnb_ref, wsm_ref,
             bsx_ref, amat_ref, aog_ref, gog_ref, *rest):
        part_refs = rest[:n_g]
        dq_ref, dkv_ref, dgu_ref, dgv_ref, gbias_ref, dws_ref, dbs_ref, vec_ref, dsink_ref = rest[n_g:n_g + 9]
        rx_refs = rest[n_g + 9:2 * n_g + 9]
        carry, done, send_sems, recv_sems = rest[2 * n_g + 9:]
        n = pl.program_id(0)
        exchange = _ChipExchange(part_refs, rx_refs, grad_kinds, send_sems, recv_sems)

        @pl.when(n == 0)
        def _():
            exchange.start()
            carry[...] = jnp.zeros_like(carry)
            done[...] = jnp.zeros_like(done)
            gbias_ref[...] = jnp.zeros_like(gbias_ref)
            dws_ref[...] = jnp.zeros_like(dws_ref)
            dbs_ref[...] = jnp.zeros_like(dbs_ref)
            vec_ref[...] = jnp.zeros_like(vec_ref)
            dsink_ref[...] = jnp.zeros_like(dsink_ref)

        @pl.when(n == n_steps)
        def _():
            dkv_ref[:BLOCK, :] = done[...].astype(BF16)
            dkv_ref[BLOCK:, :] = carry[...].astype(BF16)
            exchange.wait()

        @pl.when(n < n_steps)
        def _():
            col = lax.broadcasted_iota(jnp.int32, (BLOCK, 2 * BLOCK), 1)
            lane = lax.broadcasted_iota(jnp.int32, (BLOCK, LANES), 1)
            low = lane < HEAD_DIM
            rows = [slice(0, BLOCK), slice(BLOCK, tile)]
            kv_blocks = [kvp_ref[...], kv_ref[rows[0], :], kv_ref[rows[1], :]]
            masks = [(col < BLOCK) & (n == 0), None]
            q_blks = [q_ref[r, :] for r in rows]
            fwd = []
            for b in range(2):
                kk = jnp.concatenate([kv_blocks[b][:, :KV_W], kv_blocks[b + 1][:, :KV_W]], axis=0)
                vv = jnp.concatenate([kv_blocks[b][:, KV_W:], kv_blocks[b + 1][:, KV_W:]], axis=0)
                fwd.append(_attn_block_fwd(q_blks[b], kk, vv, bias_ref, sinks_ref, masks[b]))
                fwd.append(_gmlp_chunk_fwd(gu_ref[rows[b], :], gv_ref[rows[b], :], glng_ref[...], glnb_ref[...],
                                           wsm_ref, bsx_ref[...], amat_ref[...]))
            res = _interleave(*fwd[:2]) + _interleave(*fwd[2:])

            def gating_bwd(b, d_gm, saved):
                u, tu, ta, xhat, rstd, vb, mixedv = saved
                dgu_ref[rows[b], :] = (d_gm * mixedv * _gelu_grad(gu_ref[rows[b], :], tu)).astype(BF16)
                dmx = d_gm * u
                dmxb = dmx.astype(BF16)
                yield
                dvn_cols, dws = [], []
                for pair in range(N_GROUPS // 2):
                    dp_ = dmxb[:, pair * LANES:(pair + 1) * LANES]
                    vp = vb[:, pair * LANES:(pair + 1) * LANES]
                    dvn_cols.append(
                        jnp.where(low, _dot_tn(wsm_ref[2 * pair], dp_), _dot_tn(wsm_ref[2 * pair + 1], dp_)))
                    zero = jnp.zeros_like(dp_)
                    dws.append(_dot_nt(jnp.where(low, dp_, zero), vp))
                    dws.append(_dot_nt(jnp.where(low, zero, dp_), vp))
                dvn = jnp.concatenate(dvn_cols, axis=1)
                yield
                dxh = dvn * glng_ref[...]
                am = amat_ref[...]
                m1 = _split_dot(dxh, am)
                m2 = _split_dot(dxh * xhat, am)
                yield
                da = rstd * (dxh - m1 - xhat * m2)
                dgv_ref[rows[b], :] = (da * _gelu_grad(gv_ref[rows[b], :], ta)).astype(BF16)
                return dmx, dws, _colsum(dvn * xhat), _colsum(dvn)

            def attention_bwd(b, d_attn, probs, kvar, vvar):
                heads = range(N_HEADS)
                sels = [low if h % 2 == 0 else jnp.logical_not(low) for h in heads]
                pair_of = lambda a, h: a[:, (h // 2) * LANES:(h // 2 + 1) * LANES]
                do_hs = [jnp.where(sels[h], pair_of(d_attn, h), 0.0).astype(BF16) for h in heads]
                q_hs = [jnp.where(sels[h], pair_of(q_blks[b], h), jnp.zeros((BLOCK, LANES), BF16)) for h in heads]
                dps = [_dot_nt(do_hs[h], vvar[_head_kv(h)[0]][_head_kv(h)[1]]) for h in heads]
                yield
                deltas = [jnp.sum(probs[h][0] * dps[h], axis=-1, keepdims=True) for h in heads]
                yield
                dss = [probs[h][0] * (dps[h] - deltas[h]) for h in heads]
                dsinks = [-(probs[h][1] * deltas[h]) for h in heads]
                dsbs = [ds.astype(BF16) for ds in dss]
                pbs = [probs[h][0].astype(BF16) for h in heads]
                yield
                dqs = [_dot(dsbs[h], kvar[_head_kv(h)[0]][_head_kv(h)[1]]) for h in heads]
                tks = [_dot_tn(dsbs[h], q_hs[h]) for h in heads]
                tvs = [_dot_tn(pbs[h], do_hs[h]) for h in heads]
                dq_cols = [dqs[2 * i] + dqs[2 * i + 1] for i in range(N_HEADS // 2)]
                dq_ref[rows[b], :] = (jnp.concatenate(dq_cols, axis=1) * Q_SCALE).astype(BF16)
                per_kv = N_HEADS // N_KV
                kv_sum = lambda ts, kvh: sum(ts[kvh * per_kv + 1:(kvh + 1) * per_kv], ts[kvh * per_kv])
                dkk = _fold_kv(kv_sum(tks, 0), kv_sum(tks, 1))
                dvv = _fold_kv(kv_sum(tvs, 0), kv_sum(tvs, 1))
                return jnp.concatenate([dkk, dvv], axis=1), dss, dsinks

            bwd, rms_g = [], []
            for b in range(2):
                attn, probs, kvar, vvar = res[2 * b]
                gm, saved = res[2 * b + 1]
                na_unit, r_a = _rms(attn, 1.0)
                ng_unit, r_g = _rms(gm, 1.0)
                dmix = dmix_ref[rows[b], :]
                dn_a = dmix[:, :ATTN_W]
                dn_g = dmix[:, ATTN_W:]
                rms_g.append((_colsum(dn_a * na_unit), _colsum(dn_g * ng_unit)))
                t_a = dn_a * aog_ref[...]
                d_attn = r_a * t_a - na_unit * (r_a * jnp.mean(t_a * na_unit, axis=-1, keepdims=True))
                t_g = dn_g * gog_ref[...]
                d_gm = r_g * t_g - ng_unit * (r_g * jnp.mean(t_g * ng_unit, axis=-1, keepdims=True))
                bwd.append(attention_bwd(b, d_attn, probs, kvar, vvar))
                bwd.append(gating_bwd(b, d_gm, saved))
            (dkv_a, dss_a, dsk_a), (dmx_a, dws_a, glg_a, glb_a) = _interleave(*bwd[:2])
            (dkv_b, dss_b, dsk_b), (dmx_b, dws_b, glg_b, glb_b) = _interleave(*bwd[2:])

            vec_ref[0:1, :] += rms_g[0][0] + rms_g[1][0]
            vec_ref[1:2, :] += rms_g[0][1] + rms_g[1][1]
            vec_ref[2:3, :] += glg_a + glg_b
            vec_ref[3:4, :] += glb_a + glb_b
            dbs_ref[...] += dmx_a + dmx_b
            for g in range(N_GROUPS):
                dws_ref[g] += dws_a[g] + dws_b[g]
            for h in range(N_HEADS):
                gbias_ref[h] += dss_a[h] + dss_b[h]
                dsink_ref[h] += dsk_a[h] + dsk_b[h]

            dkv_ref[:BLOCK, :] = done[...].astype(BF16)
            dkv_ref[BLOCK:, :] = (carry[...] + dkv_a[:BLOCK]).astype(BF16)
            done[...] = dkv_a[BLOCK:] + dkv_b[:BLOCK]
            carry[...] = dkv_b[BLOCK:]

    last = n_steps - 1
    cur = lambda w: pl.BlockSpec((tile, w), lambda n: (jnp.minimum(n, last), 0))
    late = lambda w: pl.BlockSpec((tile, w), lambda n: (jnp.clip(n - 1, 0, last), 0))
    before = pl.BlockSpec((BLOCK, 2 * KV_W), lambda n: (jnp.clip(2 * n - 1, 0, 2 * last + 1), 0))
    outs = pl.pallas_call(
        body, name="bwd_mix",
        out_shape=[jax.ShapeDtypeStruct((s, ATTN_W), BF16), jax.ShapeDtypeStruct((s, 2 * KV_W), BF16),
                   jax.ShapeDtypeStruct((s, GMLP_W), BF16), jax.ShapeDtypeStruct((s, GMLP_W), BF16),
                   jax.ShapeDtypeStruct((N_HEADS, BLOCK, 2 * BLOCK), F32),
                   jax.ShapeDtypeStruct((N_GROUPS, BLOCK, BLOCK), F32),
                   jax.ShapeDtypeStruct((BLOCK, GMLP_W), F32), jax.ShapeDtypeStruct((8, GMLP_W), F32),
                   jax.ShapeDtypeStruct((N_HEADS, BLOCK, 1), F32)]
        + [jax.ShapeDtypeStruct(_rx_shape(p.shape, k), BF16) for p, k in zip(grad_parts, grad_kinds)],
        grid=(n_steps + 1,),
        in_specs=[cur(ATTN_W), cur(2 * KV_W), before, cur(GMLP_W), cur(GMLP_W), cur(D_MODEL),
                  _const_spec((N_HEADS, BLOCK, 2 * BLOCK)), pl.BlockSpec(memory_space=pltpu.SMEM),
                  _const_spec((1, GMLP_W)), _const_spec((1, GMLP_W)), _const_spec((N_GROUPS, BLOCK, BLOCK)),
                  _const_spec((BLOCK, GMLP_W)), _const_spec((GMLP_W, GMLP_W)), _const_spec((1, ATTN_W)),
                  _const_spec((1, GMLP_W))] + [ANY] * n_g,
        out_specs=[cur(ATTN_W), late(2 * KV_W), cur(GMLP_W), cur(GMLP_W),
                   _const_spec((N_HEADS, BLOCK, 2 * BLOCK)), _const_spec((N_GROUPS, BLOCK, BLOCK)),
                   _const_spec((BLOCK, GMLP_W)), _const_spec((8, GMLP_W)), _const_spec((N_HEADS, BLOCK, 1))]
        + [ANY] * n_g,
        scratch_shapes=[pltpu.VMEM((BLOCK, 2 * KV_W), F32), pltpu.VMEM((BLOCK, 2 * KV_W), F32)]
        + _ChipExchange.sems(n_g),
        compiler_params=_params(("arbitrary",)),
    )(q, kv, kv, gu, gv, dmix, bias, sinks, gln_g, gln_b, wsm, bsx, amat, aog, gog, *grad_parts)
    return outs[:9], outs[9:]


def _mix_finalize(gbias, bucket, dws, dbs, dsink):
    def body(gb_ref, bucket_ref, dws_ref, dbs_ref, dsink_ref, tall_ref):
        bk = bucket_ref[...]
        lane = lax.broadcasted_iota(jnp.int32, (N_BUCKETS, LANES), 1)
        rowi = lax.broadcasted_iota(jnp.int32, (N_BUCKETS, LANES), 0)
        drb = jnp.zeros((N_BUCKETS, LANES), F32)
        dsk = jnp.zeros((8, LANES), F32)
        lane8 = lax.broadcasted_iota(jnp.int32, (8, LANES), 1)
        for h in range(N_HEADS):
            g = gb_ref[h]
            for b in range(N_BUCKETS):
                tot = jnp.sum(_colsum(jnp.where(bk == float(b), g, 0.0)), axis=1, keepdims=True)
                drb = jnp.where((lane == h) & (rowi == b), tot, drb)
            sk = jnp.sum(dsink_ref[h], axis=0, keepdims=True)
            dsk = jnp.where(lane8 == h, sk, dsk)
        tall_ref[TALL_RB:TALL_RB + N_BUCKETS, :] = drb
        tall_ref[TALL_SK:TALL_SK + 8, :] = dsk
        ti = lax.broadcasted_iota(jnp.int32, (BLOCK, BLOCK), 0)
        ui = lax.broadcasted_iota(jnp.int32, (BLOCK, BLOCK), 1)
        for g in range(N_GROUPS):
            tall_ref[g * BLOCK:(g + 1) * BLOCK, :] = jnp.where(ti >= ui, dws_ref[g], 0.0)
        gi = lax.broadcasted_iota(jnp.int32, (GMLP_W, LANES), 0) // GROUP_DIM
        li = lax.broadcasted_iota(jnp.int32, (GMLP_W, LANES), 1)
        ind = jnp.where(gi == li, 1.0, 0.0).astype(BF16)
        d = dbs_ref[...]
        hi = d.astype(BF16)
        r1 = d - hi.astype(F32)
        mid = r1.astype(BF16)
        lo = (r1 - mid.astype(F32)).astype(BF16)
        dbsg = _dot(hi, ind) + _dot(mid, ind) + _dot(lo, ind)
        tall_ref[TALL_BS:TALL_BS + N_GROUPS, :] = dbsg.T[:N_GROUPS, :]

    return pl.pallas_call(
        body, name="mix_finalize", out_shape=jax.ShapeDtypeStruct((TALL_ROWS, LANES), F32), grid=(1,),
        in_specs=[_const_spec((N_HEADS, BLOCK, 2 * BLOCK)), _const_spec((BLOCK, 2 * BLOCK)),
                  _const_spec((N_GROUPS, BLOCK, BLOCK)), _const_spec((BLOCK, GMLP_W)),
                  _const_spec((N_HEADS, BLOCK, 1))],
        out_specs=_const_spec((TALL_ROWS, LANES)),
        compiler_params=_params(("arbitrary",)),
    )(gbias, bucket, dws, dbs, dsink)


def _bwd_in(dq, dkv, dgu, dgv, dxa, x, modr, w_in, tm):
    s = x.shape[0]

    def body(dq_ref, dkv_ref, dgu_ref, dgv_ref, dxa_ref, x_ref, mod_ref, w_ref, gx_ref, acc_ref, db_ref):
        @pl.when(pl.program_id(0) == 0)
        def _():
            acc_ref[...] = jnp.zeros_like(acc_ref)
            db_ref[...] = jnp.zeros_like(db_ref)

        dproj = jnp.concatenate([dq_ref[...], dkv_ref[...], dgu_ref[...], dgv_ref[...]], axis=1)
        dh1 = _dot_nt(dproj, w_ref[...])
        gx_ref[...] = dxa_ref[...] + dh1 * (1.0 + mod_ref[1:2, :])
        acc_ref[0:1, :] += _colsum(dh1 * x_ref[...])
        acc_ref[1:2, :] += _colsum(dh1)
        db_ref[0:1, :] += _colsum(dproj.astype(F32))

    row = lambda w: pl.BlockSpec((tm, w), lambda i: (i, 0))
    return pl.pallas_call(
        body, name="bwd_in",
        out_shape=(jax.ShapeDtypeStruct((s, D_MODEL), F32), jax.ShapeDtypeStruct((8, D_MODEL), F32),
                   jax.ShapeDtypeStruct((8, IN_W), F32)),
        grid=(s // tm,),
        in_specs=[row(ATTN_W), row(2 * KV_W), row(GMLP_W), row(GMLP_W), row(D_MODEL), row(D_MODEL),
                  _const_spec((8, D_MODEL)), _const_spec((D_MODEL, IN_W))],
        out_specs=(row(D_MODEL), _const_spec((8, D_MODEL)), _const_spec((8, IN_W))),
        compiler_params=_params(("arbitrary",)),
    )(dq, dkv, dgu, dgv, dxa, x, modr, w_in)


def _wgrad(a, bs, tm, tk, name, owner_blocks=False, gather_vs=()):
    k_all, m = a.shape
    n = sum(b.shape[1] for b in bs)
    nk = k_all // tk
    nm = m // tm
    n_b = len(bs)
    n_v = len(gather_vs)
    wb = n // N_CHIPS

    def body(a_ref, *rest):
        b_refs, v_refs = rest[:n_b], rest[n_b:n_b + n_v]
        o_ref, ob_ref = rest[n_b + n_v:n_b + n_v + 2]
        vg_refs = rest[n_b + n_v + 2:n_b + 2 * n_v + 2]
        i, k = pl.program_id(0), pl.program_id(1)
        if n_v:
            gather = _Gather8(v_refs, vg_refs, *rest[n_b + 2 * n_v + 2:])

            @pl.when((i == 0) & (k == 0))
            def _():
                gather.start()

            @pl.when((i == nm - 1) & (k == 0))
            def _():
                gather.forward()

        @pl.when(k == 0)
        def _():
            o_ref[...] = jnp.zeros_like(o_ref)

        b = b_refs[0][...] if n_b == 1 else jnp.concatenate([r[...] for r in b_refs], axis=1)
        if owner_blocks:
            av = a_ref[...]
            for j in range(N_CHIPS):
                o_ref[j] += _dot_tn(av, b[:, j * wb:(j + 1) * wb])
        else:
            o_ref[...] += _dot_tn(a_ref[...], b)

        @pl.when(k == nk - 1)
        def _():
            ob_ref[...] = o_ref[...].astype(BF16)

        if n_v:
            @pl.when((i == nm - 1) & (k == nk - 1))
            def _():
                gather.finish()

    if owner_blocks:
        out_spec = pl.BlockSpec((N_CHIPS, tm, wb), lambda i, k: (0, i, 0))
        shape = (N_CHIPS, m, wb)
    else:
        out_spec = pl.BlockSpec((tm, n), lambda i, k: (i, 0))
        shape = (m, n)
    outs = pl.pallas_call(
        body, name=name,
        out_shape=[jax.ShapeDtypeStruct(shape, F32), jax.ShapeDtypeStruct(shape, BF16)] + _gathered8_shapes(gather_vs),
        grid=(nm, nk),
        in_specs=[pl.BlockSpec((tk, tm), lambda i, k: (k, i))]
        + [pl.BlockSpec((tk, b.shape[1]), lambda i, k: (k, 0)) for b in bs] + [ANY] * n_v,
        out_specs=[out_spec, out_spec] + [ANY] * n_v,
        scratch_shapes=_Gather8.sems(n_v) if n_v else [],
        compiler_params=_params(("arbitrary", "arbitrary") if n_v else ("parallel", "arbitrary")),
    )(a, *bs, *gather_vs)
    return outs[0], outs[1], outs[2:]


def _adam_math(w, g, m, v):
    m2 = ADAM_B1 * m + (1.0 - ADAM_B1) * g
    v2 = ADAM_B2 * v + (1.0 - ADAM_B2) * (g * g)
    m_hat = m2 / (1.0 - ADAM_B1 ** ADAM_STEP)
    v_hat = v2 / (1.0 - ADAM_B2 ** ADAM_STEP)
    delta = -ADAM_LR * (m_hat / (jnp.sqrt(v_hat) + ADAM_EPS) + ADAM_WD * w)
    return delta, m2, v2


def _adam_halves(w, mine, got, m, v, tr, name):
    r, cc = w.shape
    h = r // 2
    nt = h // tr

    def body(c_ref, w_ref, mine_ref, got_ref, m_ref, v_ref, g_ref, d_ref, m2_ref, v2_ref):
        g = jnp.where(pl.program_id(0) == c_ref[0], mine_ref[...], got_ref[...])
        g_ref[...] = g
        d, m2, v2 = _adam_math(w_ref[...], g, m_ref[...], v_ref[...])
        d_ref[...] = d
        m2_ref[...] = m2
        v2_ref[...] = v2

    full = pl.BlockSpec((tr, cc), lambda hh, i, c_ref: (hh * nt + i, 0))
    half = pl.BlockSpec((tr, cc), lambda hh, i, c_ref: (i, 0))
    shp = jax.ShapeDtypeStruct((r, cc), F32)
    return pl.pallas_call(
        body, name=name, out_shape=(shp, shp, shp, shp),
        grid_spec=pltpu.PrefetchScalarGridSpec(
            num_scalar_prefetch=1, grid=(2, nt), in_specs=[full, half, half, full, full],
            out_specs=(full, full, full, full)),
        compiler_params=_params(("arbitrary", "arbitrary")),
    )(_core_index_scalar(), w, mine, got, m, v)


def _adam_w_ada(sc_t, dmod_cols, w, m, v, tr):
    r, cc = w.shape

    def body(sct_ref, dm_ref, w_ref, m_ref, v_ref, g_ref, d_ref, m2_ref, v2_ref):
        g = sct_ref[:, 0:1] * dm_ref[0:1, :]
        for k in range(1, N_DEV):
            g = g + sct_ref[:, k:k + 1] * dm_ref[k:k + 1, :]
        g_ref[...] = g
        d, m2, v2 = _adam_math(w_ref[...], g, m_ref[...], v_ref[...])
        d_ref[...] = d
        m2_ref[...] = m2
        v2_ref[...] = v2

    spec = pl.BlockSpec((tr, cc), lambda i: (i, 0))
    shp = jax.ShapeDtypeStruct((r, cc), F32)
    return pl.pallas_call(
        body, name="adam_w_ada", out_shape=(shp, shp, shp, shp), grid=(r // tr,),
        in_specs=[pl.BlockSpec((tr, N_DEV), lambda i: (i, 0)), _const_spec((N_DEV, cc)), spec, spec, spec],
        out_specs=(spec, spec, spec, spec), compiler_params=_params(("parallel",)),
    )(sc_t, dmod_cols, w, m, v)


def _pack_wide(acc_i, acc_m, acc_f, db_in, vec):
    arrs = [acc_i, acc_m, acc_f, db_in, vec]
    i_, m_, f_, b_, v_ = range(5)
    src = {"b_in": (b_, 0), "ln1_g": (m_, 2), "ln1_b": (m_, 3), "ln2_g": (f_, 1), "ln2_b": (f_, 2),
           "gmlp_ln_g": (v_, 2), "gmlp_ln_b": (v_, 3), "attn_out_g": (v_, 0), "gmlp_out_g": (v_, 1), "loss": (f_, 0)}
    dmod = [(i_, 1), (i_, 0), (m_, 4), (m_, 1), (m_, 0), (f_, 3)]

    def body(*refs):
        ins, wide_ref = refs[:5], refs[5]
        wide_ref[...] = jnp.zeros_like(wide_ref)
        for k, (a, row) in enumerate(dmod):
            wide_ref[0:1, k * D_MODEL:(k + 1) * D_MODEL] = ins[a][row:row + 1, :]
        for name, (a, row) in src.items():
            r, off, n = WIDE_LAYOUT[name]
            wide_ref[r:r + 1, off:off + n] = ins[a][row:row + 1, :]

    return pl.pallas_call(
        body, name="pack_wide", out_shape=jax.ShapeDtypeStruct((8, WIDE_W), F32), grid=(1,),
        in_specs=[_const_spec(a.shape) for a in arrs], out_specs=_const_spec((8, WIDE_W)),
        compiler_params=_params(("arbitrary",)),
    )(*arrs)


def _adam_small(gw, gt, wide_wmv, w_s, b_s, rel_bias, sinks):
    names = list(WIDE_PARAMS)
    tall = [("gmlp_w_s", w_s), ("gmlp_b_s", b_s), ("rel_bias", rel_bias), ("attn_sinks", sinks)]
    ins = [gw, gt]
    for n in names:
        ins += list(wide_wmv[n])
    for _, t in tall:
        ins += list(t)
    n_in = len(ins)

    def body(*refs):
        gw_ref, gt_ref = refs[0], refs[1]
        wmv = refs[2:n_in]
        dmod_ref, loss_ref = refs[n_in], refs[n_in + 1]
        outs = refs[n_in + 2:]

        def tall_sum(r0, nr):
            g = gt_ref[r0:r0 + nr, :]
            for d in range(1, N_DEV):
                g = g + gt_ref[d * TALL_ROWS + r0:d * TALL_ROWS + r0 + nr, :]
            return g

        def emit(k, g, w_ref, m_ref, v_ref):
            d, m2, v2 = _adam_math(w_ref[...], g, m_ref[...], v_ref[...])
            outs[4 * k][...] = g
            outs[4 * k + 1][...] = d
            outs[4 * k + 2][...] = m2
            outs[4 * k + 3][...] = v2

        gsum = gw_ref[0:8, :]
        for d in range(1, N_DEV):
            gsum = gsum + gw_ref[8 * d:8 * d + 8, :]
        for d in range(N_DEV):
            dmod_ref[d:d + 1, :] = gw_ref[8 * d:8 * d + 1, :]
        for k, n in enumerate(names):
            r, off, sz = WIDE_LAYOUT[n]
            emit(k, gsum[r:r + 1, off:off + sz], *wmv[3 * k:3 * k + 3])
        r, off, sz = WIDE_LAYOUT["loss"]
        tot = jnp.sum(gsum[r:r + 1, off:off + sz], axis=1, keepdims=True)
        loss_ref[...] = jnp.broadcast_to(tot * (0.5 / D_MODEL), loss_ref.shape)

        k0 = len(names)
        ws_refs = wmv[3 * k0:3 * k0 + 3]
        for g in range(N_GROUPS):
            rows = slice(g * BLOCK, (g + 1) * BLOCK)
            gg = tall_sum(g * BLOCK, BLOCK)
            d, m2, v2 = _adam_math(ws_refs[0][rows, :], gg, ws_refs[1][rows, :], ws_refs[2][rows, :])
            outs[4 * k0][rows, :] = gg
            outs[4 * k0 + 1][rows, :] = d
            outs[4 * k0 + 2][rows, :] = m2
            outs[4 * k0 + 3][rows, :] = v2
        emit(k0 + 1, tall_sum(TALL_BS, N_GROUPS), *wmv[3 * (k0 + 1):3 * (k0 + 1) + 3])
        emit(k0 + 2, tall_sum(TALL_RB, N_BUCKETS)[:, :N_HEADS], *wmv[3 * (k0 + 2):3 * (k0 + 2) + 3])
        emit(k0 + 3, tall_sum(TALL_SK, 8)[0:1, :N_HEADS], *wmv[3 * (k0 + 3):3 * (k0 + 3) + 3])

    out_shapes = [jax.ShapeDtypeStruct((N_DEV, WIDE_W), F32), jax.ShapeDtypeStruct((8, LANES), F32)]
    for n in names:
        out_shapes += [jax.ShapeDtypeStruct(wide_wmv[n][0].shape, F32)] * 4
    for _, t in tall:
        out_shapes += [jax.ShapeDtypeStruct(t[0].shape, F32)] * 4
    res = pl.pallas_call(
        body, name="adam_small", out_shape=out_shapes, grid=(1,),
        in_specs=[_const_spec(a.shape) for a in ins], out_specs=[_const_spec(o.shape) for o in out_shapes],
        compiler_params=_params(("arbitrary",)),
    )(*ins)
    out = {}
    for k, n in enumerate(names + [t[0] for t in tall]):
        out[n] = tuple(res[2 + 4 * k:6 + 4 * k])
    return res[0], res[1], out


def kernel(x, c, rel_bias, w_ada, b_ada, w_in, b_in, attn_sinks, gmlp_ln_g, gmlp_ln_b, gmlp_w_s, gmlp_b_s, attn_out_g, gmlp_out_g, w_out, ln1_g, ln1_b, w_gate_up, w_down, ln2_g, ln2_b, loss_target, m_rel_bias, m_w_ada, m_b_ada, m_w_in, m_b_in, m_attn_sinks, m_gmlp_ln_g, m_gmlp_ln_b, m_gmlp_w_s, m_gmlp_b_s, m_attn_out_g, m_gmlp_out_g, m_w_out, m_ln1_g, m_ln1_b, m_w_gate_up, m_w_down, m_ln2_g, m_ln2_b, v_rel_bias, v_w_ada, v_b_ada, v_w_in, v_b_in, v_attn_sinks, v_gmlp_ln_g, v_gmlp_ln_b, v_gmlp_w_s, v_gmlp_b_s, v_attn_out_g, v_gmlp_out_g, v_w_out, v_ln1_g, v_ln1_b, v_w_gate_up, v_w_down, v_ln2_g, v_ln2_b):
    ix, iy, ic = _my_pos()
    chip = 2 * ix + iy
    dev = 4 * ix + 2 * iy + ic
    s = x.shape[1]
    xs = x[0]
    tgt = loss_target[0]
    tm_big = min(512, s)
    tm_ffn = min(FFN_SUB, s)
    n_ada = w_ada.shape[2]

    w_in_s, w_out_s = w_in[0].astype(BF16), w_out[0].astype(BF16)
    w_gu_s, w_dn_s = w_gate_up[0].astype(BF16), w_down[0].astype(BF16)
    sc_all, _, mod_rows, w_in_g = _prologue(
        jnp.pad(c, ((0, 7), (0, 0))), w_ada[0], lax.dynamic_slice_in_dim(b_ada, chip * n_ada, n_ada, axis=1), w_in_s)
    mod_all = mod_rows.reshape(N_DEV, N_DEV, -1)
    mod_row = lax.dynamic_index_in_dim(mod_all[0::2], dev, axis=1, keepdims=False)
    modr = jnp.pad(mod_row.reshape(6, D_MODEL), ((0, 2), (0, 0)))
    w_in_g = _insert_own(w_in_g, w_in_s, "blk", chip)
    w_in_f = jnp.transpose(w_in_g, (1, 0, 2)).reshape(D_MODEL, IN_W)

    bucket = _bucket_table()
    bias, wsm = _prep_tables(bucket, rel_bias, gmlp_w_s[0])
    bsx = jnp.repeat(gmlp_b_s[0].T, GROUP_DIM, axis=1)
    amat = _group_mean_matrix()
    sinks = attn_sinks[0]

    (h1, q, kv, gu, gv), (w_out_g, w_dn_g) = _fwd_in(xs, modr, w_in_f, b_in, tm_big, [w_out_s, w_dn_s], ["blk", "blk"])
    w_out_f = _insert_own(w_out_g, w_out_s, "blk", chip).reshape(D_MODEL, D_MODEL)
    x1, y, mixed, (w_gu_g,) = _fwd_mix(
        q, kv, gu, gv, xs, modr, bias, sinks, gmlp_ln_g, gmlp_ln_b, wsm, bsx, amat, attn_out_g, gmlp_out_g, w_out_f,
        ln1_g, ln1_b, tm_big, [w_gu_s], ["blk"])
    assert w_gate_up.shape[2] == FF_CHUNK
    w_gu_f = _insert_own(w_gu_g, w_gu_s, "blk", chip)
    w_dn_f = _insert_own(w_dn_g, w_dn_s, "blk", chip).reshape(D_FF, D_MODEL)
    h2, act, dy2, dx1a, acc_f = _fwd_ffn(x1, tgt, modr, ln2_g, ln2_b, w_gu_f, w_dn_f, tm_ffn)

    a_act, dgu_ff, dh2 = _bwd_ffn(dy2, act, w_gu_f, w_dn_f, min(FFN_SUB, s))
    g_dn, g_dn_b, _ = _wgrad(a_act, [dy2], D_FF // 2, min(512, s), "wgrad_down")
    g_gu, g_gu_b, _ = _wgrad(h2, [dgu_ff], 512, min(512, s), "wgrad_gate_up")
    blk3 = lambda a, rows: a.reshape(N_CHIPS, rows, a.shape[1])
    (dxa, dy, dmix, acc_m), (got_dn, got_gu) = _bwd_mid(
        dh2, dx1a, x1, xs, y, modr, ln1_g, w_out_f, tm_big, [blk3(g_dn_b, D_FF // N_CHIPS), g_gu_b], ["blk", "cols"])
    g_out, g_out_b, _ = _wgrad(mixed, [dy], 512, min(512, s), "wgrad_out")
    (got_out,) = _swap_halves([blk3(g_out_b, D_MODEL // N_CHIPS)], ["blk"], "rs_swap_out")
    kinds_a = ["blk", "cols", "blk"]
    fulls_a = [blk3(g_dn, D_FF // N_CHIPS), g_gu, blk3(g_out, D_MODEL // N_CHIPS)]
    gots_a = [got_dn, got_gu, got_out]
    parts_a = [_add_halves(f, g, k, "rs_add_a%d" % i) for i, (f, g, k) in enumerate(zip(fulls_a, gots_a, kinds_a))]
    (dq, dkv, dgu, dgv, gbias, dws, dbs, vec, dsink), rxs_a = _bwd_mix(
        q, kv, gu, gv, dmix, bias, sinks, gmlp_ln_g, gmlp_ln_b, wsm, bsx, amat, attn_out_g, gmlp_out_g,
        [p[1] for p in parts_a], kinds_a)
    tall_g = _mix_finalize(gbias, bucket, dws, dbs, dsink)
    grad_x, acc_i, db_in = _bwd_in(dq, dkv, dgu, dgv, dxa, xs, modr, w_in_f, tm_big)

    wide_g = _pack_wide(acc_i, acc_m, acc_f, db_in, vec)
    full_in, full_in_b, (gw, gt) = _wgrad(h1, [dq, dkv, dgu, dgv], 512, min(512, s), "wgrad_in", owner_blocks=True,
                                          gather_vs=[wide_g, tall_g])
    (got_in,) = _swap_halves([full_in_b], ["blk"], "rs_swap_in")
    part_in = _add_halves(full_in, got_in, "blk", "rs_add_in")
    (rx_in,) = _exchange_chip_partials([part_in[1]], ["blk"], "rs_chips_in")
    wide_wmv = {"b_ada": (b_ada, m_b_ada, v_b_ada), "b_in": (b_in, m_b_in, v_b_in),
                "ln1_g": (ln1_g, m_ln1_g, v_ln1_g), "ln1_b": (ln1_b, m_ln1_b, v_ln1_b),
                "ln2_g": (ln2_g, m_ln2_g, v_ln2_g), "ln2_b": (ln2_b, m_ln2_b, v_ln2_b),
                "gmlp_ln_g": (gmlp_ln_g, m_gmlp_ln_g, v_gmlp_ln_g), "gmlp_ln_b": (gmlp_ln_b, m_gmlp_ln_b, v_gmlp_ln_b),
                "attn_out_g": (attn_out_g, m_attn_out_g, v_attn_out_g),
                "gmlp_out_g": (gmlp_out_g, m_gmlp_out_g, v_gmlp_out_g)}
    rows2 = lambda a: a.reshape(-1, a.shape[-1])
    dmod_all, loss_t, small = _adam_small(
        gw, gt, wide_wmv, tuple(rows2(a) for a in (gmlp_w_s, m_gmlp_w_s, v_gmlp_w_s)),
        tuple(rows2(a) for a in (gmlp_b_s, m_gmlp_b_s, v_gmlp_b_s)), (rel_bias, m_rel_bias, v_rel_bias),
        (attn_sinks, m_attn_sinks, v_attn_sinks))
    loss = loss_t[0, 0]

    dmod_cols = lax.dynamic_slice_in_dim(dmod_all, chip * n_ada, n_ada, axis=1)
    g_ada, d_ada, m_ada, v_ada = _adam_w_ada(sc_all.T, dmod_cols, w_ada[0], m_w_ada[0], v_w_ada[0], 256)

    sums = [(parts_a[0][0], rxs_a[0], "blk", 176), (parts_a[1][0], rxs_a[1], "cols", 256),
            (parts_a[2][0], rxs_a[2], "blk", 128), (part_in[0], rx_in, "blk", 256)]
    mine = [_sum_chips(p, rx, k, tr, "rs_sum_%d" % i) for i, (p, rx, k, tr) in enumerate(sums)]
    got = _share_halves(mine, "rs_share")

    gs_dn, d_dn, m_dn, v_dn = _adam_halves(w_down[0], mine[0], got[0], m_w_down[0], v_w_down[0], 176, "adam_w_down")
    gs_gu, d_gu, m_gu, v_gu = _adam_halves(w_gate_up[0], mine[1], got[1], m_w_gate_up[0], v_w_gate_up[0], 256,
                                           "adam_w_gate_up")
    gs_out, d_out, m_out, v_out = _adam_halves(w_out[0], mine[2], got[2], m_w_out[0], v_w_out[0], 128, "adam_w_out")
    gs_in, d_in, m_in, v_in = _adam_halves(w_in[0], mine[3], got[3], m_w_in[0], v_w_in[0], 256, "adam_w_in")

    big = {"w_ada": (g_ada, d_ada, m_ada, v_ada), "w_in": (gs_in, d_in, m_in, v_in), "w_out": (gs_out, d_out, m_out, v_out),
           "w_gate_up": (gs_gu, d_gu, m_gu, v_gu), "w_down": (gs_dn, d_dn, m_dn, v_dn)}
    order = ["rel_bias", "w_ada", "b_ada", "w_in", "b_in", "attn_sinks", "gmlp_ln_g", "gmlp_ln_b", "gmlp_w_s", "gmlp_b_s",
             "attn_out_g", "gmlp_out_g", "w_out", "ln1_g", "ln1_b", "w_gate_up", "w_down", "ln2_g", "ln2_b"]
    shapes = {"gmlp_w_s": gmlp_w_s.shape, "gmlp_b_s": gmlp_b_s.shape}
    outs = [loss, grad_x[None]]
    for k in range(4):
        for name in order:
            if name in big:
                outs.append(big[name][k][None])
            elif name in shapes:
                outs.append(small[name][k].reshape(shapes[name]))
            else:
                outs.append(small[name][k])
    return tuple(outs)
```

```python
import math

import numpy as np
import jax
import jax.numpy as jnp
from jax import lax
from jax.experimental import pallas as pl
from jax.experimental.pallas import tpu as pltpu

F32 = jnp.float32
BF16 = jnp.bfloat16
MESH = pl.DeviceIdType.MESH

D_MODEL = 1024
N_HEADS = 8
N_KV = 2
HEAD_DIM = 64
ATTN_W = N_HEADS * HEAD_DIM
KV_W = N_KV * HEAD_DIM
N_GROUPS = 8
GROUP_DIM = 64
GMLP_W = N_GROUPS * GROUP_DIM
IN_W = ATTN_W + 2 * KV_W + 2 * GMLP_W
BLOCK = 128
N_BUCKETS = 32
MAX_DISTANCE = 128
D_FF = 2816
ALPHA = 2.0 ** 0.25
LN_EPS = 1e-5
NEG_INF = -1e30
ADAM_LR, ADAM_B1, ADAM_B2, ADAM_EPS, ADAM_WD, ADAM_STEP = 0.001, 0.9, 0.999, 1e-8, 0.01, 10
N_CHIPS = 4
N_DEV = 8
LANES = 128
V7X_VMEM_LIMIT = 56 * 2 ** 20
GELU_C = math.sqrt(2.0 / math.pi)
Q_SCALE = HEAD_DIM ** -0.5
ANY = pl.BlockSpec(memory_space=pl.ANY)

TALL_BS = N_GROUPS * BLOCK
TALL_RB = TALL_BS + 8
TALL_SK = TALL_RB + N_BUCKETS
TALL_ROWS = TALL_SK + 8
WIDE_W = 6 * D_MODEL
WIDE_LAYOUT = {
    "b_ada": (0, 0, 6 * D_MODEL),
    "b_in": (1, 0, IN_W), "ln1_g": (1, IN_W, D_MODEL), "ln1_b": (1, IN_W + D_MODEL, D_MODEL),
    "ln2_g": (1, IN_W + 2 * D_MODEL, D_MODEL), "ln2_b": (1, IN_W + 3 * D_MODEL, D_MODEL),
    "gmlp_ln_g": (2, 0, GMLP_W), "gmlp_ln_b": (2, GMLP_W, GMLP_W), "attn_out_g": (2, 2 * GMLP_W, ATTN_W),
    "gmlp_out_g": (2, 2 * GMLP_W + ATTN_W, GMLP_W), "loss": (2, 3 * GMLP_W + ATTN_W, D_MODEL)}
WIDE_PARAMS = tuple(n for n in WIDE_LAYOUT if n != "loss")


def _params(sem=None):
    return pltpu.CompilerParams(dimension_semantics=sem, vmem_limit_bytes=V7X_VMEM_LIMIT)


def _const_spec(shape, single=False):
    nd = len(shape)
    if single:
        return pl.BlockSpec(shape, lambda *_: (0,) * nd, pipeline_mode=pl.Buffered(1))
    return pl.BlockSpec(shape, lambda *_: (0,) * nd)


def _dot(a, b):
    return jnp.dot(a, b, preferred_element_type=F32)


def _dot_nt(a, b):
    return lax.dot_general(a, b, (((1,), (1,)), ((), ())), preferred_element_type=F32)


def _dot_tn(a, b):
    return lax.dot_general(a, b, (((0,), (0,)), ((), ())), preferred_element_type=F32)


def _gelu(x):
    t = jnp.tanh(GELU_C * (x + 0.044715 * x * x * x))
    return 0.5 * x * (1.0 + t), t


def _gelu_grad(x, t):
    return 0.5 * (1.0 + t) + 0.5 * x * (1.0 - t * t) * GELU_C * (1.0 + 3.0 * 0.044715 * x * x)


def _split_dot(x, a):
    hi = x.astype(BF16)
    lo = (x - hi.astype(F32)).astype(BF16)
    return _dot(hi, a) + _dot(lo, a)


def _group_mean_matrix():
    g = np.arange(GMLP_W) // GROUP_DIM
    return jnp.asarray((g[:, None] == g[None, :]).astype(np.float32) / GROUP_DIM, dtype=BF16)


def _ln_stats(z):
    mu = jnp.mean(z, axis=-1, keepdims=True)
    d = z - mu
    var = jnp.mean(d * d, axis=-1, keepdims=True)
    rstd = lax.rsqrt(var + LN_EPS)
    return d * rstd, rstd


def _ln_bwd(dxhat, xhat, rstd):
    m1 = jnp.mean(dxhat, axis=-1, keepdims=True)
    m2 = jnp.mean(dxhat * xhat, axis=-1, keepdims=True)
    return rstd * (dxhat - m1 - xhat * m2)


def _colsum(x):
    return jnp.sum(x, axis=0, keepdims=True)


def _my_pos():
    return lax.axis_index("x"), lax.axis_index("y"), lax.axis_index("c")


def _other_chips(x, y):
    return [(1 - x, y), (x, 1 - y), (1 - x, 1 - y)]


def _chip_index_scalar():
    ix, iy, _ = _my_pos()
    return jnp.reshape(2 * ix + iy, (1,)).astype(jnp.int32)


def _core_index_scalar():
    return jnp.reshape(lax.axis_index("c"), (1,)).astype(jnp.int32)


class _Gather8:
    def __init__(self, x_refs, out_refs, send_sems, recv_sems, local_sems):
        self.x_refs, self.out_refs = x_refs, out_refs
        self.send_sems, self.recv_sems, self.local_sems = send_sems, recv_sems, local_sems
        self.x, self.y, self.c = _my_pos()
        self.me, self.sibling = (self.x, self.y, self.c), (self.x, self.y, 1 - self.c)
        self.chips = _other_chips(self.x, self.y)

    def _rows(self, a, px, py, pc):
        m_per = self.x_refs[a].shape[0]
        return self.out_refs[a].at[pl.ds((4 * px + 2 * py + pc) * m_per, m_per), :]

    def _copy(self, a, k, block, to, src=None):
        return pltpu.make_async_remote_copy(
            src_ref=self._rows(a, *block) if src is None else src, dst_ref=self._rows(a, *block),
            send_sem=self.send_sems.at[7 * a + k], recv_sem=self.recv_sems.at[7 * a + k], device_id=to,
            device_id_type=MESH)

    def _local(self, a):
        return pltpu.make_async_copy(self.x_refs[a], self._rows(a, *self.me), self.local_sems.at[a])

    def start(self):
        for a in range(len(self.x_refs)):
            self._local(a).start()
            self._copy(a, 0, self.me, self.sibling, src=self.x_refs[a]).start()
            for j, chip in enumerate(self.chips):
                self._copy(a, 1 + j, self.me, (*chip, self.c), src=self.x_refs[a]).start()

    def forward(self):
        for a in range(len(self.x_refs)):
            for j, chip in enumerate(self.chips):
                self._copy(a, 1 + j, (*chip, self.c), self.me).wait_recv()
                self._copy(a, 4 + j, (*chip, self.c), self.sibling).start()

    def finish(self):
        for a in range(len(self.x_refs)):
            self._copy(a, 0, self.sibling, self.me).wait_recv()
            for j, chip in enumerate(self.chips):
                self._copy(a, 4 + j, (*chip, 1 - self.c), self.me).wait_recv()
        for a in range(len(self.x_refs)):
            for k in range(7):
                self._copy(a, k, self.me, self.me).wait_send()
            self._local(a).wait()

    @staticmethod
    def sems(n_v):
        return [pltpu.SemaphoreType.DMA((7 * n_v,)), pltpu.SemaphoreType.DMA((7 * n_v,)),
                pltpu.SemaphoreType.DMA((n_v,))]


def _gathered8_shapes(vs):
    return [jax.ShapeDtypeStruct((N_DEV * v.shape[0], v.shape[1]), v.dtype) for v in vs]


VMEM_WHOLE = pl.BlockSpec(memory_space=pltpu.VMEM)


def _prologue(c_pad, w_ada_s, b_ada_s, w_in_s):
    n = w_ada_s.shape[1]

    def body(c_ref, w_ref, b_ref, win_ref, sc_ref, modc_ref, modg_ref, wing_ref, call_ref, *sems):
        weights = _WeightGather([win_ref], [wing_ref], ["blk"], sems[0], sems[1])
        gather_c = _Gather8([c_ref], [call_ref], sems[2], sems[3], sems[4])
        gather_mod = _Gather8([modc_ref], [modg_ref], sems[5], sems[6], sems[7])
        weights.start()
        gather_c.start()
        gather_c.forward()
        gather_c.finish()
        cv = call_ref[...]
        sc = cv * _sigmoid(cv)
        a_hi = sc.astype(BF16)
        a_lo = (sc - a_hi.astype(F32)).astype(BF16)
        w = w_ref[...]
        w_hi = w.astype(BF16)
        w_lo = (w - w_hi.astype(F32)).astype(BF16)
        mod = _dot(a_hi, w_hi) + _dot(a_hi, w_lo) + _dot(a_lo, w_hi) + b_ref[...]
        for d in range(N_DEV):
            sc_ref[d:d + 1, :] = sc[8 * d:8 * d + 1, :]
            modc_ref[d:d + 1, :] = mod[8 * d:8 * d + 1, :]
        gather_mod.start()
        gather_mod.forward()
        gather_mod.finish()
        weights.forward()
        weights.forward_diagonal()
        weights.finish()

    return pl.pallas_call(
        body, name="prologue",
        out_shape=(jax.ShapeDtypeStruct((N_DEV, D_MODEL), F32), jax.ShapeDtypeStruct((N_DEV, n), F32),
                   jax.ShapeDtypeStruct((N_DEV * N_DEV, n), F32),
                   jax.ShapeDtypeStruct(_gathered_shape(w_in_s, "blk"), BF16)),
        in_specs=[VMEM_WHOLE, VMEM_WHOLE, VMEM_WHOLE, ANY],
        out_specs=(VMEM_WHOLE, VMEM_WHOLE, VMEM_WHOLE, ANY),
        scratch_shapes=[pltpu.VMEM((N_DEV * 8, D_MODEL), F32)] + _WeightGather.sems(1) + _Gather8.sems(1)
        + _Gather8.sems(1),
        compiler_params=pltpu.CompilerParams(vmem_limit_bytes=V7X_VMEM_LIMIT),
    )(c_pad, w_ada_s, b_ada_s, w_in_s)


def _gathered_shape(shard, kind):
    r, cc = shard.shape
    return (N_CHIPS, r, cc) if kind == "blk" else (r, N_CHIPS * cc)


class _WeightGather:
    N_SEM = 8

    def __init__(self, shards, gathered, kinds, send_sems, recv_sems):
        self.shards, self.gathered, self.kinds = shards, gathered, kinds
        self.send_sems, self.recv_sems = send_sems, recv_sems
        self.x, self.y, self.c = _my_pos()
        self.me, self.sibling = (self.x, self.y, self.c), (self.x, self.y, 1 - self.c)
        self.nbr = ((1 - self.x, self.y), (self.x, 1 - self.y))
        self.diag = 2 * (1 - self.x) + (1 - self.y)

    def _dst(self, a, chip, pc, quarter=None):
        r, cc = self.shards[a].shape
        h = r // 2
        row0, rows = pc * h, h
        if quarter is not None:
            row0, rows = pc * h + quarter * (h // 2), h // 2
        g = self.gathered[a]
        if self.kinds[a] == "blk":
            return g.at[chip, pl.ds(row0, rows), :]
        return g.at[pl.ds(row0, rows), pl.ds(chip * cc, cc)]

    def _copy(self, a, k, region, to, src=None):
        return pltpu.make_async_remote_copy(
            src_ref=region if src is None else src, dst_ref=region, send_sem=self.send_sems.at[a * self.N_SEM + k],
            recv_sem=self.recv_sems.at[a * self.N_SEM + k], device_id=to, device_id_type=MESH)

    def _arrays(self):
        return range(len(self.shards))

    def start(self):
        my_chip = 2 * self.x + self.y
        for a in self._arrays():
            h = self.shards[a].shape[0] // 2
            mine = self.shards[a].at[pl.ds(self.c * h, h), :]
            for j, chip in enumerate(self.nbr):
                self._copy(a, j, self._dst(a, my_chip, self.c), (*chip, self.c), src=mine).start()

    def forward(self):
        for a in self._arrays():
            for j, chip in enumerate(self.nbr):
                cj = 2 * chip[0] + chip[1]
                half = self._dst(a, cj, self.c)
                self._copy(a, j, half, self.me).wait_recv()
                self._copy(a, 2 + j, half, self.sibling).start()
                other = self.nbr[1 - j]
                self._copy(a, 4 + j, self._dst(a, cj, self.c, quarter=j), (*other, self.c)).start()

    def forward_diagonal(self):
        for a in self._arrays():
            for j in range(2):
                quarter = self._dst(a, self.diag, self.c, quarter=j)
                self._copy(a, 4 + j, quarter, self.me).wait_recv()
                self._copy(a, 6 + j, quarter, self.sibling).start()

    def finish(self):
        for a in self._arrays():
            for j, chip in enumerate(self.nbr):
                self._copy(a, 2 + j, self._dst(a, 2 * chip[0] + chip[1], 1 - self.c), self.me).wait_recv()
                self._copy(a, 6 + j, self._dst(a, self.diag, 1 - self.c, quarter=j), self.me).wait_recv()
        for a in self._arrays():
            half = self._dst(a, self.diag, self.c)
            quarter = self._dst(a, self.diag, self.c, quarter=0)
            for k in range(self.N_SEM):
                self._copy(a, k, half if k < 4 else quarter, self.me).wait_send()

    @classmethod
    def sems(cls, n_arr):
        return [pltpu.SemaphoreType.DMA((n_arr * cls.N_SEM,)), pltpu.SemaphoreType.DMA((n_arr * cls.N_SEM,))]


def _insert_own(gathered, shard, kind, chip):
    if kind == "blk":
        return lax.dynamic_update_slice(gathered, shard[None], (chip, 0, 0))
    return lax.dynamic_update_slice(gathered, shard, (0, chip * shard.shape[1]))


def _half_of_full(ref, kind, pc):
    if kind == "blk":
        h = ref.shape[1] // 2
        return ref.at[:, pl.ds(pc * h, h), :]
    h = ref.shape[0] // 2
    return ref.at[pl.ds(pc * h, h), :]


def _half_shape(shape, kind):
    return (shape[0], shape[1] // 2, shape[2]) if kind == "blk" else (shape[0] // 2, shape[1])


class _HalfSwap:
    def __init__(self, ins, outs, kinds, send_sems, recv_sems):
        self.ins, self.outs, self.kinds = ins, outs, kinds
        self.send_sems, self.recv_sems = send_sems, recv_sems
        self.x, self.y, self.c = _my_pos()

    def _copies(self):
        for a in range(len(self.ins)):
            yield pltpu.make_async_remote_copy(
                src_ref=_half_of_full(self.ins[a], self.kinds[a], 1 - self.c), dst_ref=self.outs[a],
                send_sem=self.send_sems.at[a], recv_sem=self.recv_sems.at[a],
                device_id=(self.x, self.y, 1 - self.c), device_id_type=MESH)

    def start(self):
        for cp in self._copies():
            cp.start()

    def wait(self):
        for cp in self._copies():
            cp.wait()

    @staticmethod
    def sems(n_arr):
        return [pltpu.SemaphoreType.DMA((n_arr,)), pltpu.SemaphoreType.DMA((n_arr,))]

    @staticmethod
    def out_shapes(fulls, kinds):
        return [jax.ShapeDtypeStruct(_half_shape(a.shape, k), a.dtype) for a, k in zip(fulls, kinds)]


def _swap_halves(fulls_bf16, kinds, name):
    n_arr = len(fulls_bf16)

    def body(*refs):
        swap = _HalfSwap(refs[:n_arr], refs[n_arr:2 * n_arr], kinds, *refs[2 * n_arr:])
        swap.start()
        swap.wait()

    return pl.pallas_call(
        body, name=name, out_shape=_HalfSwap.out_shapes(fulls_bf16, kinds),
        in_specs=[ANY] * n_arr, out_specs=[ANY] * n_arr, scratch_shapes=_HalfSwap.sems(n_arr),
    )(*fulls_bf16)


def _add_halves(full, got, kind, name):
    hs = _half_shape(full.shape, kind)

    def body(c_ref, a_ref, b_ref, o_ref, ob_ref):
        p = a_ref[...] + b_ref[...].astype(F32)
        o_ref[...] = p
        ob_ref[...] = p.astype(BF16)

    if kind == "blk":
        nb, h, cc = hs
        own = pl.BlockSpec((1, h, cc), lambda b, c_ref: (b, c_ref[0], 0))
        other = pl.BlockSpec((1, h, cc), lambda b, c_ref: (b, 0, 0))
    else:
        h, cc = hs[0], hs[1] // N_CHIPS
        own = pl.BlockSpec((h, cc), lambda b, c_ref: (c_ref[0], b))
        other = pl.BlockSpec((h, cc), lambda b, c_ref: (0, b))
    return pl.pallas_call(
        body, name=name, out_shape=(jax.ShapeDtypeStruct(hs, F32), jax.ShapeDtypeStruct(hs, BF16)),
        grid_spec=pltpu.PrefetchScalarGridSpec(
            num_scalar_prefetch=1, grid=(N_CHIPS,), in_specs=[own, other], out_specs=(other, other)),
        compiler_params=_params(("arbitrary",)),
    )(_core_index_scalar(), full, got)


def _rx_shape(part_shape, kind):
    if kind == "blk":
        return (3, part_shape[1], part_shape[2])
    return (3, part_shape[0], part_shape[1] // N_CHIPS)


class _ChipExchange:
    def __init__(self, parts, rxs, kinds, send_sems, recv_sems):
        self.parts, self.rxs, self.kinds = parts, rxs, kinds
        self.send_sems, self.recv_sems = send_sems, recv_sems
        self.x, self.y, self.c = _my_pos()
        self.chips = _other_chips(self.x, self.y)

    def _copies(self):
        for a in range(len(self.parts)):
            for j, chip in enumerate(self.chips):
                cj = 2 * chip[0] + chip[1]
                if self.kinds[a] == "blk":
                    src = self.parts[a].at[cj]
                else:
                    cc = self.parts[a].shape[1] // N_CHIPS
                    src = self.parts[a].at[:, pl.ds(cj * cc, cc)]
                yield pltpu.make_async_remote_copy(
                    src_ref=src, dst_ref=self.rxs[a].at[j], send_sem=self.send_sems.at[a * 3 + j],
                    recv_sem=self.recv_sems.at[a * 3 + j], device_id=(*chip, self.c), device_id_type=MESH)

    def start(self):
        for cp in self._copies():
            cp.start()

    def wait(self):
        for cp in self._copies():
            cp.wait_recv()
        for cp in self._copies():
            cp.wait_send()

    @staticmethod
    def sems(n_arr):
        return [pltpu.SemaphoreType.DMA((n_arr * 3,)), pltpu.SemaphoreType.DMA((n_arr * 3,))]


def _exchange_chip_partials(parts, kinds, name):
    n_arr = len(parts)

    def body(*refs):
        exchange = _ChipExchange(refs[:n_arr], refs[n_arr:2 * n_arr], kinds, *refs[2 * n_arr:])
        exchange.start()
        exchange.wait()

    return pl.pallas_call(
        body, name=name,
        out_shape=[jax.ShapeDtypeStruct(_rx_shape(p.shape, k), BF16) for p, k in zip(parts, kinds)],
        in_specs=[ANY] * n_arr, out_specs=[ANY] * n_arr, scratch_shapes=_ChipExchange.sems(n_arr),
    )(*parts)


def _sum_chips(part, rx, kind, tr, name):
    _, h, cc = rx.shape
    flips = (2, 1, 3)

    def body(chip_ref, p_ref, rx_ref, o_ref):
        own = p_ref[...].reshape(tr, cc)
        for mc in range(N_CHIPS):
            @pl.when(chip_ref[0] == mc)
            def _():
                terms = sorted([(mc, None)] + [(mc ^ f, j) for j, f in enumerate(flips)])
                acc = None
                for _, j in terms:
                    t = own if j is None else rx_ref[j].astype(F32)
                    acc = t if acc is None else acc + t
                o_ref[...] = acc

    if kind == "blk":
        own_spec = pl.BlockSpec((1, tr, cc), lambda i, chip_ref: (chip_ref[0], i, 0))
    else:
        own_spec = pl.BlockSpec((tr, cc), lambda i, chip_ref: (i, chip_ref[0]))
    return pl.pallas_call(
        body, name=name, out_shape=jax.ShapeDtypeStruct((h, cc), F32),
        grid_spec=pltpu.PrefetchScalarGridSpec(
            num_scalar_prefetch=1, grid=(h // tr,),
            in_specs=[own_spec, pl.BlockSpec((3, tr, cc), lambda i, chip_ref: (0, i, 0))],
            out_specs=pl.BlockSpec((tr, cc), lambda i, chip_ref: (i, 0))),
        compiler_params=_params(("arbitrary",)),
    )(_chip_index_scalar(), part, rx)


def _share_halves(halves, name):
    n_arr = len(halves)

    def body(*refs):
        ins, outs = refs[:n_arr], refs[n_arr:2 * n_arr]
        send_sems, recv_sems = refs[2 * n_arr:]
        x, y, c = _my_pos()
        cps = []
        for a in range(n_arr):
            cp = pltpu.make_async_remote_copy(
                src_ref=ins[a], dst_ref=outs[a], send_sem=send_sems.at[a], recv_sem=recv_sems.at[a],
                device_id=(x, y, 1 - c), device_id_type=MESH)
            cp.start()
            cps.append(cp)
        for cp in cps:
            cp.wait()

    return pl.pallas_call(
        body, name=name, out_shape=[jax.ShapeDtypeStruct(h.shape, h.dtype) for h in halves],
        in_specs=[ANY] * n_arr, out_specs=[ANY] * n_arr,
        scratch_shapes=[pltpu.SemaphoreType.DMA((n_arr,)), pltpu.SemaphoreType.DMA((n_arr,))],
    )(*halves)


def _bucket_table():
    qi = jnp.arange(BLOCK)[:, None]
    si = jnp.arange(2 * BLOCK)[None, :]
    dist = qi + BLOCK - si
    max_exact = N_BUCKETS // 2
    n = jnp.maximum(dist, 0)
    nf = jnp.maximum(n, max_exact).astype(F32)
    large = max_exact + (jnp.log(nf / max_exact) / math.log(MAX_DISTANCE / max_exact)
                         * (N_BUCKETS - max_exact)).astype(jnp.int32)
    large = jnp.minimum(large, N_BUCKETS - 1)
    return jnp.where(n < max_exact, n, large).astype(F32)


def _prep_tables(bucket, rel_bias, w_s):
    def body(bucket_ref, rb_ref, ws_ref, bias_ref, wsm_ref):
        qi = lax.broadcasted_iota(jnp.int32, (BLOCK, 2 * BLOCK), 0)
        si = lax.broadcasted_iota(jnp.int32, (BLOCK, 2 * BLOCK), 1)
        dist = qi + BLOCK - si
        in_window = (dist >= 0) & (dist < BLOCK)
        bk = bucket_ref[...]
        for h in range(N_HEADS):
            acc = jnp.zeros((BLOCK, 2 * BLOCK), F32)
            for b in range(N_BUCKETS):
                acc = jnp.where(bk == float(b), rb_ref[b, h], acc)
            bias_ref[h] = jnp.where(in_window, acc, NEG_INF)
        ti = lax.broadcasted_iota(jnp.int32, (BLOCK, BLOCK), 0)
        ui = lax.broadcasted_iota(jnp.int32, (BLOCK, BLOCK), 1)
        for g in range(N_GROUPS):
            wsm_ref[g] = jnp.where(ti >= ui, ws_ref[g], 0.0).astype(BF16)

    return pl.pallas_call(
        body, name="prep_tables",
        out_shape=(jax.ShapeDtypeStruct((N_HEADS, BLOCK, 2 * BLOCK), F32),
                   jax.ShapeDtypeStruct((N_GROUPS, BLOCK, BLOCK), BF16)),
        grid=(1,),
        in_specs=[_const_spec((BLOCK, 2 * BLOCK)), pl.BlockSpec(memory_space=pltpu.SMEM),
                  _const_spec((N_GROUPS, BLOCK, BLOCK))],
        out_specs=(_const_spec((N_HEADS, BLOCK, 2 * BLOCK)), _const_spec((N_GROUPS, BLOCK, BLOCK))),
        compiler_params=_params(("arbitrary",)),
    )(bucket, rel_bias, w_s)


def _fwd_in(x, modr, w_in, b_in, tm, shards, kinds):
    s = x.shape[0]
    n_steps = s // tm
    fwd_step, diag_step = (11 * n_steps) // 16, (15 * n_steps) // 16
    n_w = len(shards)

    def body(x_ref, mod_ref, w_ref, b_ref, *rest):
        shard_refs = rest[:n_w]
        h1_ref, q_ref, kv_ref, gu_ref, gv_ref = rest[n_w:n_w + 5]
        gathered_refs = rest[n_w + 5:2 * n_w + 5]
        send_sems, recv_sems = rest[2 * n_w + 5:]
        i = pl.program_id(0)
        gather = _WeightGather(shard_refs, gathered_refs, kinds, send_sems, recv_sems)

        @pl.when(i == 0)
        def _():
            gather.start()

        h1 = (x_ref[...] * (1.0 + mod_ref[1:2, :]) + mod_ref[0:1, :]).astype(BF16)
        h1_ref[...] = h1
        proj = _dot(h1, w_ref[...]) + b_ref[...]
        q_ref[...] = (proj[:, :ATTN_W] * Q_SCALE).astype(BF16)
        kv_ref[...] = proj[:, ATTN_W:ATTN_W + 2 * KV_W].astype(BF16)
        gu_ref[...] = proj[:, ATTN_W + 2 * KV_W:ATTN_W + 2 * KV_W + GMLP_W]
        gv_ref[...] = proj[:, ATTN_W + 2 * KV_W + GMLP_W:]

        @pl.when(i == fwd_step)
        def _():
            gather.forward()

        @pl.when(i == diag_step)
        def _():
            gather.forward_diagonal()

        @pl.when(i == n_steps - 1)
        def _():
            gather.finish()

    row = lambda w: pl.BlockSpec((tm, w), lambda i: (i, 0))
    outs = pl.pallas_call(
        body, name="fwd_in",
        out_shape=[jax.ShapeDtypeStruct((s, D_MODEL), BF16), jax.ShapeDtypeStruct((s, ATTN_W), BF16),
                   jax.ShapeDtypeStruct((s, 2 * KV_W), BF16), jax.ShapeDtypeStruct((s, GMLP_W), F32),
                   jax.ShapeDtypeStruct((s, GMLP_W), F32)]
        + [jax.ShapeDtypeStruct(_gathered_shape(sh, k), BF16) for sh, k in zip(shards, kinds)],
        grid=(n_steps,),
        in_specs=[row(D_MODEL), _const_spec((8, D_MODEL)), _const_spec((D_MODEL, IN_W)), _const_spec((1, IN_W))]
        + [ANY] * n_w,
        out_specs=[row(D_MODEL), row(ATTN_W), row(2 * KV_W), row(GMLP_W), row(GMLP_W)] + [ANY] * n_w,
        scratch_shapes=_WeightGather.sems(n_w),
        compiler_params=_params(("arbitrary",)),
    )(x, modr, w_in, b_in, *shards)
    return outs[:5], outs[5:]


def _kv_variants(kk):
    kf = kk.astype(F32)
    lane = lax.broadcasted_iota(jnp.int32, kf.shape, 1)
    low = lane < HEAD_DIM
    k0_lo = jnp.where(low, kf, 0.0)
    k1_hi = jnp.where(low, 0.0, kf)
    k0_hi = pltpu.roll(k0_lo, HEAD_DIM, 1)
    k1_lo = pltpu.roll(k1_hi, HEAD_DIM, 1)
    return ((k0_lo.astype(BF16), k0_hi.astype(BF16)), (k1_lo.astype(BF16), k1_hi.astype(BF16)))


def _head_kv(h):
    return h // (N_HEADS // N_KV), h % 2


MIX_GROUP = 2


def _interleave(*gens):
    results = [None] * len(gens)
    active = list(enumerate(gens))
    while active:
        still = []
        for i, g in active:
            try:
                next(g)
                still.append((i, g))
            except StopIteration as done:
                results[i] = done.value
        active = still
    return results


def _attn_block_fwd(q_blk, kk, vv, bias_ref, sinks_ref, first_mask):
    kvar = _kv_variants(kk)
    vvar = _kv_variants(vv)
    heads = range(N_HEADS)
    q_pairs = [q_blk[:, (h // 2) * LANES:(h // 2 + 1) * LANES] for h in heads]
    logits = [_dot_nt(q_pairs[h], kvar[_head_kv(h)[0]][_head_kv(h)[1]]) + bias_ref[h] for h in heads]
    if first_mask is not None:
        logits = [jnp.where(first_mask, NEG_INF, lg) for lg in logits]
    yield
    ms = [jnp.maximum(jnp.max(logits[h], axis=-1, keepdims=True), sinks_ref[h]) for h in heads]
    yield
    es = [jnp.exp(logits[h] - ms[h]) for h in heads]
    ess = [jnp.exp(sinks_ref[h] - ms[h]) for h in heads]
    yield
    invs = [1.0 / (jnp.sum(es[h], axis=-1, keepdims=True) + ess[h]) for h in heads]
    probs = [(es[h] * invs[h], ess[h] * invs[h]) for h in heads]
    yield
    outs = [_dot(probs[h][0].astype(BF16), vvar[_head_kv(h)[0]][_head_kv(h)[1]]) for h in heads]
    pairs = [outs[2 * i] + outs[2 * i + 1] for i in range(N_HEADS // 2)]
    return jnp.concatenate(pairs, axis=1), probs, kvar, vvar


def _gmlp_chunk_fwd(gu, gv, ln_g, ln_b, wsm_ref, bsx, amat):
    u, tu = _gelu(gu)
    a, ta = _gelu(gv)
    yield
    mean = _split_dot(a, amat)
    d = a - mean
    yield
    var = _split_dot(d * d, amat)
    yield
    rstd = lax.rsqrt(var + LN_EPS)
    xhat = d * rstd
    vb = (xhat * ln_g + ln_b).astype(BF16)
    yield
    lane = lax.broadcasted_iota(jnp.int32, (BLOCK, LANES), 1)
    low = lane < GROUP_DIM
    cols = []
    for pair in range(N_GROUPS // 2):
        vp = vb[:, pair * LANES:(pair + 1) * LANES]
        cols.append(jnp.where(low, _dot(wsm_ref[2 * pair], vp), _dot(wsm_ref[2 * pair + 1], vp)))
    mixedv = jnp.concatenate(cols, axis=1) + bsx
    return u * mixedv, (u, tu, ta, xhat, rstd, vb, mixedv)


def _rms(a, g):
    r = lax.rsqrt(jnp.mean(a * a, axis=-1, keepdims=True) + LN_EPS)
    return a * r * g, r


def _fwd_mix(q, kv, gu, gv, x, modr, bias, sinks, gln_g, gln_b, wsm, bsx, amat, aog, gog, w_out, ln1_g, ln1_b, tm,
             ffn_shards, ffn_kinds):
    s = x.shape[0]
    nb = tm // BLOCK
    n_steps = s // tm
    fwd_step, diag_step = (7 * n_steps) // 16, (12 * n_steps) // 16
    n_w = len(ffn_shards)

    def body(q_ref, kv_ref, kvp_ref, gu_ref, gv_ref, x_ref, mod_ref, bias_ref, sinks_ref, glng_ref, glnb_ref, wsm_ref,
             bsx_ref, amat_ref, aog_ref, gog_ref, wout_ref, ln1g_ref, ln1b_ref, *rest):
        shard_refs = rest[:n_w]
        x1_ref, y_ref, mixed_ref = rest[n_w:n_w + 3]
        gathered_refs = rest[n_w + 3:2 * n_w + 3]
        mix_scr, send_sems, recv_sems = rest[2 * n_w + 3:]
        i = pl.program_id(0)
        gather = _WeightGather(shard_refs, gathered_refs, ffn_kinds, send_sems, recv_sems)

        @pl.when(i == 0)
        def _():
            gather.start()

        col = lax.broadcasted_iota(jnp.int32, (BLOCK, 2 * BLOCK), 1)
        for b0 in range(0, nb, MIX_GROUP):
            gens = []
            for b in range(b0, min(b0 + MIX_GROUP, nb)):
                r0 = b * BLOCK
                if b == 0:
                    kvprev = kvp_ref[...]
                    first_mask = (col < BLOCK) & (i == 0)
                else:
                    kvprev = kv_ref[r0 - BLOCK:r0, :]
                    first_mask = None
                kvcur = kv_ref[r0:r0 + BLOCK, :]
                kk = jnp.concatenate([kvprev[:, :KV_W], kvcur[:, :KV_W]], axis=0)
                vv = jnp.concatenate([kvprev[:, KV_W:], kvcur[:, KV_W:]], axis=0)
                gens.append(_attn_block_fwd(q_ref[r0:r0 + BLOCK, :], kk, vv, bias_ref, sinks_ref, first_mask))
                gens.append(_gmlp_chunk_fwd(gu_ref[r0:r0 + BLOCK, :], gv_ref[r0:r0 + BLOCK, :], glng_ref[...],
                                            glnb_ref[...], wsm_ref, bsx_ref[...], amat_ref[...]))
            res = _interleave(*gens)
            for k, b in enumerate(range(b0, min(b0 + MIX_GROUP, nb))):
                r0 = b * BLOCK
                na, _ = _rms(res[2 * k][0], aog_ref[...])
                ng, _ = _rms(res[2 * k + 1][0], gog_ref[...])
                mix_scr[r0:r0 + BLOCK, :ATTN_W] = na.astype(BF16)
                mix_scr[r0:r0 + BLOCK, ATTN_W:] = ng.astype(BF16)
        mixed = mix_scr[...]
        mixed_ref[...] = mixed
        y = _dot(mixed, wout_ref[...])
        y_ref[...] = y
        z1 = ALPHA * x_ref[...] + mod_ref[2:3, :] * y
        xhat, _ = _ln_stats(z1)
        x1_ref[...] = xhat * ln1g_ref[...] + ln1b_ref[...]

        @pl.when(i == fwd_step)
        def _():
            gather.forward()

        @pl.when(i == diag_step)
        def _():
            gather.forward_diagonal()

        @pl.when(i == n_steps - 1)
        def _():
            gather.finish()

    row = lambda w: pl.BlockSpec((tm, w), lambda i: (i, 0))
    prev = pl.BlockSpec((BLOCK, 2 * KV_W), lambda i: (jnp.maximum(i * nb - 1, 0), 0))
    outs = pl.pallas_call(
        body, name="fwd_mix",
        out_shape=[jax.ShapeDtypeStruct((s, D_MODEL), F32), jax.ShapeDtypeStruct((s, D_MODEL), F32),
                   jax.ShapeDtypeStruct((s, D_MODEL), BF16)]
        + [jax.ShapeDtypeStruct(_gathered_shape(sh, k), BF16) for sh, k in zip(ffn_shards, ffn_kinds)],
        grid=(n_steps,),
        in_specs=[row(ATTN_W), row(2 * KV_W), prev, row(GMLP_W), row(GMLP_W), row(D_MODEL), _const_spec((8, D_MODEL)),
                  _const_spec((N_HEADS, BLOCK, 2 * BLOCK)), pl.BlockSpec(memory_space=pltpu.SMEM),
                  _const_spec((1, GMLP_W)), _const_spec((1, GMLP_W)), _const_spec((N_GROUPS, BLOCK, BLOCK)),
                  _const_spec((BLOCK, GMLP_W)), _const_spec((GMLP_W, GMLP_W)), _const_spec((1, ATTN_W)),
                  _const_spec((1, GMLP_W)), _const_spec((D_MODEL, D_MODEL)), _const_spec((1, D_MODEL)),
                  _const_spec((1, D_MODEL))] + [ANY] * n_w,
        out_specs=[row(D_MODEL), row(D_MODEL), row(D_MODEL)] + [ANY] * n_w,
        scratch_shapes=[pltpu.VMEM((tm, D_MODEL), BF16)] + _WeightGather.sems(n_w),
        compiler_params=_params(("arbitrary",)),
    )(q, kv, kv, gu, gv, x, modr, bias, sinks, gln_g, gln_b, wsm, bsx, amat, aog, gog, w_out, ln1_g, ln1_b, *ffn_shards)
    return outs[0], outs[1], outs[2], outs[3:]


FF_BLOCKS = N_CHIPS // 2
FF_CHUNK = D_FF // FF_BLOCKS
FFN_SUB = 256


def _sigmoid(x):
    return 1.0 / (1.0 + jnp.exp(-x))


def _fwd_ffn(x1, target, modr, ln2_g, ln2_b, w_gu, w_dn, tm):
    s = x1.shape[0]

    def body(x1_ref, t_ref, mod_ref, g_ref, b_ref, wgu_ref, wdn_ref, h2_ref, act_ref, dy2_ref, dx1a_ref, acc_ref):
        @pl.when(pl.program_id(0) == 0)
        def _():
            acc_ref[...] = jnp.zeros_like(acc_ref)

        x1v = x1_ref[...]
        h2 = (x1v * (1.0 + mod_ref[4:5, :]) + mod_ref[3:4, :]).astype(BF16)
        h2_ref[...] = h2
        y2 = None
        for cc in range(FF_BLOCKS):
            c0 = cc * FF_CHUNK
            gate = _dot(h2, wgu_ref[cc])
            up = _dot(h2, wgu_ref[FF_BLOCKS + cc])
            act_ref[:, c0:c0 + FF_CHUNK] = gate.astype(BF16)
            act_ref[:, D_FF + c0:D_FF + c0 + FF_CHUNK] = up.astype(BF16)
            a = (gate * _sigmoid(gate) * up).astype(BF16)
            part = _dot(a, wdn_ref[c0:c0 + FF_CHUNK, :])
            y2 = part if y2 is None else y2 + part
        g2 = mod_ref[5:6, :]
        z2 = ALPHA * x1v + g2 * y2
        xhat, rstd = _ln_stats(z2)
        gain = g_ref[...]
        diff = xhat * gain + b_ref[...] - t_ref[...]
        dx2 = diff * (1.0 / D_MODEL)
        dz2 = _ln_bwd(dx2 * gain, xhat, rstd)
        dx1a_ref[...] = ALPHA * dz2
        dy2_ref[...] = (g2 * dz2).astype(BF16)
        acc_ref[0:1, :] += _colsum(diff * diff)
        acc_ref[1:2, :] += _colsum(dx2 * xhat)
        acc_ref[2:3, :] += _colsum(dx2)
        acc_ref[3:4, :] += _colsum(dz2 * y2)

    row = lambda w: pl.BlockSpec((tm, w), lambda i: (i, 0))
    return pl.pallas_call(
        body, name="fwd_ffn",
        out_shape=(jax.ShapeDtypeStruct((s, D_MODEL), BF16), jax.ShapeDtypeStruct((s, 2 * D_FF), BF16),
                   jax.ShapeDtypeStruct((s, D_MODEL), BF16), jax.ShapeDtypeStruct((s, D_MODEL), F32),
                   jax.ShapeDtypeStruct((8, D_MODEL), F32)),
        grid=(s // tm,),
        in_specs=[row(D_MODEL), row(D_MODEL), _const_spec((8, D_MODEL)), _const_spec((1, D_MODEL)),
                  _const_spec((1, D_MODEL)), _const_spec((N_CHIPS, D_MODEL, FF_CHUNK), single=True),
                  _const_spec((D_FF, D_MODEL), single=True)],
        out_specs=(row(D_MODEL), row(2 * D_FF), row(D_MODEL), row(D_MODEL), _const_spec((8, D_MODEL))),
        compiler_params=_params(("arbitrary",)),
    )(x1, target, modr, ln2_g, ln2_b, w_gu, w_dn)


def _bwd_ffn(dy2, act, w_gu, w_dn, tm):
    s = dy2.shape[0]

    def body(dy2_ref, act_ref, wgu_ref, wdn_ref, a_ref, dgu_ref, dh2_ref):
        dy2v = dy2_ref[...]
        dh2 = None
        for cc in range(FF_BLOCKS):
            c0 = cc * FF_CHUNK
            da = _dot_nt(dy2v, wdn_ref[c0:c0 + FF_CHUNK, :])
            gate = act_ref[:, c0:c0 + FF_CHUNK].astype(F32)
            up = act_ref[:, D_FF + c0:D_FF + c0 + FF_CHUNK].astype(F32)
            sg = _sigmoid(gate)
            sl = gate * sg
            a_ref[:, c0:c0 + FF_CHUNK] = (sl * up).astype(BF16)
            dgate = (da * up * (sg * (1.0 + gate * (1.0 - sg)))).astype(BF16)
            dup = (da * sl).astype(BF16)
            dgu_ref[:, c0:c0 + FF_CHUNK] = dgate
            dgu_ref[:, D_FF + c0:D_FF + c0 + FF_CHUNK] = dup
            part = _dot_nt(dgate, wgu_ref[cc]) + _dot_nt(dup, wgu_ref[FF_BLOCKS + cc])
            dh2 = part if dh2 is None else dh2 + part
        dh2_ref[...] = dh2

    row = lambda w: pl.BlockSpec((tm, w), lambda i: (i, 0))
    return pl.pallas_call(
        body, name="bwd_ffn",
        out_shape=(jax.ShapeDtypeStruct((s, D_FF), BF16), jax.ShapeDtypeStruct((s, 2 * D_FF), BF16),
                   jax.ShapeDtypeStruct((s, D_MODEL), F32)),
        grid=(s // tm,),
        in_specs=[row(D_MODEL), row(2 * D_FF), _const_spec((N_CHIPS, D_MODEL, FF_CHUNK), single=True),
                  _const_spec((D_FF, D_MODEL), single=True)],
        out_specs=(row(D_FF), row(2 * D_FF), row(D_MODEL)),
        compiler_params=_params(("parallel",)),
    )(dy2, act, w_gu, w_dn)


def _bwd_mid(dh2, dx1a, x1, x, y, modr, ln1_g, w_out, tm, swap_fulls, swap_kinds):
    s = x.shape[0]
    n_steps = s // tm
    n_g = len(swap_fulls)

    def body(dh2_ref, dx1a_ref, x1_ref, x_ref, y_ref, mod_ref, g_ref, wout_ref, *rest):
        full_refs = rest[:n_g]
        dxa_ref, dy_ref, dmix_ref, acc_ref = rest[n_g:n_g + 4]
        got_refs = rest[n_g + 4:2 * n_g + 4]
        swap = _HalfSwap(full_refs, got_refs, swap_kinds, *rest[2 * n_g + 4:])
        i = pl.program_id(0)

        @pl.when(i == 0)
        def _():
            swap.start()
            acc_ref[...] = jnp.zeros_like(acc_ref)

        dh2 = dh2_ref[...]
        x1v = x1_ref[...]
        yv = y_ref[...]
        g1 = mod_ref[2:3, :]
        dx1 = dx1a_ref[...] + dh2 * (1.0 + mod_ref[4:5, :])
        z1 = ALPHA * x_ref[...] + g1 * yv
        xhat, rstd = _ln_stats(z1)
        dz1 = _ln_bwd(dx1 * g_ref[...], xhat, rstd)
        dxa_ref[...] = ALPHA * dz1
        dy = (g1 * dz1).astype(BF16)
        dy_ref[...] = dy
        dmix_ref[...] = _dot_nt(dy, wout_ref[...])
        acc_ref[0:1, :] += _colsum(dh2 * x1v)
        acc_ref[1:2, :] += _colsum(dh2)
        acc_ref[2:3, :] += _colsum(dx1 * xhat)
        acc_ref[3:4, :] += _colsum(dx1)
        acc_ref[4:5, :] += _colsum(dz1 * yv)

        @pl.when(i == n_steps - 1)
        def _():
            swap.wait()

    row = lambda w: pl.BlockSpec((tm, w), lambda i: (i, 0))
    outs = pl.pallas_call(
        body, name="bwd_mid",
        out_shape=[jax.ShapeDtypeStruct((s, D_MODEL), F32), jax.ShapeDtypeStruct((s, D_MODEL), BF16),
                   jax.ShapeDtypeStruct((s, D_MODEL), F32), jax.ShapeDtypeStruct((8, D_MODEL), F32)]
        + _HalfSwap.out_shapes(swap_fulls, swap_kinds),
        grid=(n_steps,),
        in_specs=[row(D_MODEL)] * 5 + [_const_spec((8, D_MODEL)), _const_spec((1, D_MODEL)),
                                       _const_spec((D_MODEL, D_MODEL))] + [ANY] * n_g,
        out_specs=[row(D_MODEL), row(D_MODEL), row(D_MODEL), _const_spec((8, D_MODEL))] + [ANY] * n_g,
        scratch_shapes=_HalfSwap.sems(n_g),
        compiler_params=_params(("arbitrary",)),
    )(dh2, dx1a, x1, x, y, modr, ln1_g, w_out, *swap_fulls)
    return outs[:4], outs[4:]


def _fold_kv(t0, t1):
    lane = lax.broadcasted_iota(jnp.int32, t0.shape, 1)
    f0 = t0 + pltpu.roll(t0, HEAD_DIM, 1)
    f1 = t1 + pltpu.roll(t1, HEAD_DIM, 1)
    return jnp.where(lane < HEAD_DIM, f0, f1)


def _bwd_mix(q, kv, gu, gv, dmix, bias, sinks, gln_g, gln_b, wsm, bsx, amat, aog, gog, grad_parts, grad_kinds):
    s = q.shape[0]
    tile = 2 * BLOCK
    n_steps = s // tile
    n_g = len(grad_parts)

    def body(q_ref, kv_ref, kvp_ref, gu_ref, gv_ref, dmix_ref, bias_ref, sinks_ref, glng_ref, glnb_ref, wsm_ref,
             bsx_ref, amat_ref, aog_ref, gog_ref, *rest):
        part_refs = rest[:n_g]
        dq_ref, dkv_ref, dgu_ref, dgv_ref, gbias_ref, dws_ref, dbs_ref, vec_ref, dsink_ref = rest[n_g:n_g + 9]
        rx_refs = rest[n_g + 9:2 * n_g + 9]
        carry, done, send_sems, recv_sems = rest[2 * n_g + 9:]
        n = pl.program_id(0)
        exchange = _ChipExchange(part_refs, rx_refs, grad_kinds, send_sems, recv_sems)

        @pl.when(n == 0)
        def _():
            exchange.start()
            carry[...] = jnp.zeros_like(carry)
            done[...] = jnp.zeros_like(done)
            gbias_ref[...] = jnp.zeros_like(gbias_ref)
            dws_ref[...] = jnp.zeros_like(dws_ref)
            dbs_ref[...] = jnp.zeros_like(dbs_ref)
            vec_ref[...] = jnp.zeros_like(vec_ref)
            dsink_ref[...] = jnp.zeros_like(dsink_ref)

        @pl.when(n == n_steps)
        def _():
            dkv_ref[:BLOCK, :] = done[...].astype(BF16)
            dkv_ref[BLOCK:, :] = carry[...].astype(BF16)
            exchange.wait()

        @pl.when(n < n_steps)
        def _():
            col = lax.broadcasted_iota(jnp.int32, (BLOCK, 2 * BLOCK), 1)
            lane = lax.broadcasted_iota(jnp.int32, (BLOCK, LANES), 1)
            low = lane < HEAD_DIM
            rows = [slice(0, BLOCK), slice(BLOCK, tile)]
            kv_blocks = [kvp_ref[...], kv_ref[rows[0], :], kv_ref[rows[1], :]]
            masks = [(col < BLOCK) & (n == 0), None]
            q_blks = [q_ref[r, :] for r in rows]
            fwd = []
            for b in range(2):
                kk = jnp.concatenate([kv_blocks[b][:, :KV_W], kv_blocks[b + 1][:, :KV_W]], axis=0)
                vv = jnp.concatenate([kv_blocks[b][:, KV_W:], kv_blocks[b + 1][:, KV_W:]], axis=0)
                fwd.append(_attn_block_fwd(q_blks[b], kk, vv, bias_ref, sinks_ref, masks[b]))
                fwd.append(_gmlp_chunk_fwd(gu_ref[rows[b], :], gv_ref[rows[b], :], glng_ref[...], glnb_ref[...],
                                           wsm_ref, bsx_ref[...], amat_ref[...]))
            res = _interleave(*fwd[:2]) + _interleave(*fwd[2:])

            def gating_bwd(b, d_gm, saved):
                u, tu, ta, xhat, rstd, vb, mixedv = saved
                dgu_ref[rows[b], :] = (d_gm * mixedv * _gelu_grad(gu_ref[rows[b], :], tu)).astype(BF16)
                dmx = d_gm * u
                dmxb = dmx.astype(BF16)
                yield
                dvn_cols, dws = [], []
                for pair in range(N_GROUPS // 2):
                    dp_ = dmxb[:, pair * LANES:(pair + 1) * LANES]
                    vp = vb[:, pair * LANES:(pair + 1) * LANES]
                    dvn_cols.append(
                        jnp.where(low, _dot_tn(wsm_ref[2 * pair], dp_), _dot_tn(wsm_ref[2 * pair + 1], dp_)))
                    zero = jnp.zeros_like(dp_)
                    dws.append(_dot_nt(jnp.where(low, dp_, zero), vp))
                    dws.append(_dot_nt(jnp.where(low, zero, dp_), vp))
                dvn = jnp.concatenate(dvn_cols, axis=1)
                yield
                dxh = dvn * glng_ref[...]
                am = amat_ref[...]
                m1 = _split_dot(dxh, am)
                m2 = _split_dot(dxh * xhat, am)
                yield
                da = rstd * (dxh - m1 - xhat * m2)
                dgv_ref[rows[b], :] = (da * _gelu_grad(gv_ref[rows[b], :], ta)).astype(BF16)
                return dmx, dws, _colsum(dvn * xhat), _colsum(dvn)

            def attention_bwd(b, d_attn, probs, kvar, vvar):
                heads = range(N_HEADS)
                sels = [low if h % 2 == 0 else jnp.logical_not(low) for h in heads]
                pair_of = lambda a, h: a[:, (h // 2) * LANES:(h // 2 + 1) * LANES]
                do_hs = [jnp.where(sels[h], pair_of(d_attn, h), 0.0).astype(BF16) for h in heads]
                q_hs = [jnp.where(sels[h], pair_of(q_blks[b], h), jnp.zeros((BLOCK, LANES), BF16)) for h in heads]
                dps = [_dot_nt(do_hs[h], vvar[_head_kv(h)[0]][_head_kv(h)[1]]) for h in heads]
                yield
                deltas = [jnp.sum(probs[h][0] * dps[h], axis=-1, keepdims=True) for h in heads]
                yield
                dss = [probs[h][0] * (dps[h] - deltas[h]) for h in heads]
                dsinks = [-(probs[h][1] * deltas[h]) for h in heads]
                dsbs = [ds.astype(BF16) for ds in dss]
                pbs = [probs[h][0].astype(BF16) for h in heads]
                yield
                dqs = [_dot(dsbs[h], kvar[_head_kv(h)[0]][_head_kv(h)[1]]) for h in heads]
                tks = [_dot_tn(dsbs[h], q_hs[h]) for h in heads]
                tvs = [_dot_tn(pbs[h], do_hs[h]) for h in heads]
                dq_cols = [dqs[2 * i] + dqs[2 * i + 1] for i in range(N_HEADS // 2)]
                dq_ref[rows[b], :] = (jnp.concatenate(dq_cols, axis=1) * Q_SCALE).astype(BF16)
                per_kv = N_HEADS // N_KV
                kv_sum = lambda ts, kvh: sum(ts[kvh * per_kv + 1:(kvh + 1) * per_kv], ts[kvh * per_kv])
                dkk = _fold_kv(kv_sum(tks, 0), kv_sum(tks, 1))
                dvv = _fold_kv(kv_sum(tvs, 0), kv_sum(tvs, 1))
                return jnp.concatenate([dkk, dvv], axis=1), dss, dsinks

            bwd, rms_g = [], []
            for b in range(2):
                attn, probs, kvar, vvar = res[2 * b]
                gm, saved = res[2 * b + 1]
                na_unit, r_a = _rms(attn, 1.0)
                ng_unit, r_g = _rms(gm, 1.0)
                dmix = dmix_ref[rows[b], :]
                dn_a = dmix[:, :ATTN_W]
                dn_g = dmix[:, ATTN_W:]
                rms_g.append((_colsum(dn_a * na_unit), _colsum(dn_g * ng_unit)))
                t_a = dn_a * aog_ref[...]
                d_attn = r_a * t_a - na_unit * (r_a * jnp.mean(t_a * na_unit, axis=-1, keepdims=True))
                t_g = dn_g * gog_ref[...]
                d_gm = r_g * t_g - ng_unit * (r_g * jnp.mean(t_g * ng_unit, axis=-1, keepdims=True))
                bwd.append(attention_bwd(b, d_attn, probs, kvar, vvar))
                bwd.append(gating_bwd(b, d_gm, saved))
            (dkv_a, dss_a, dsk_a), (dmx_a, dws_a, glg_a, glb_a) = _interleave(*bwd[:2])
            (dkv_b, dss_b, dsk_b), (dmx_b, dws_b, glg_b, glb_b) = _interleave(*bwd[2:])

            vec_ref[0:1, :] += rms_g[0][0] + rms_g[1][0]
            vec_ref[1:2, :] += rms_g[0][1] + rms_g[1][1]
            vec_ref[2:3, :] += glg_a + glg_b
            vec_ref[3:4, :] += glb_a + glb_b
            dbs_ref[...] += dmx_a + dmx_b
            for g in range(N_GROUPS):
                dws_ref[g] += dws_a[g] + dws_b[g]
            for h in range(N_HEADS):
                gbias_ref[h] += dss_a[h] + dss_b[h]
                dsink_ref[h] += dsk_a[h] + dsk_b[h]

            dkv_ref[:BLOCK, :] = done[...].astype(BF16)
            dkv_ref[BLOCK:, :] = (carry[...] + dkv_a[:BLOCK]).astype(BF16)
            done[...] = dkv_a[BLOCK:] + dkv_b[:BLOCK]
            carry[...] = dkv_b[BLOCK:]

    last = n_steps - 1
    cur = lambda w: pl.BlockSpec((tile, w), lambda n: (jnp.minimum(n, last), 0))
    late = lambda w: pl.BlockSpec((tile, w), lambda n: (jnp.clip(n - 1, 0, last), 0))
    before = pl.BlockSpec((BLOCK, 2 * KV_W), lambda n: (jnp.clip(2 * n - 1, 0, 2 * last + 1), 0))
    outs = pl.pallas_call(
        body, name="bwd_mix",
        out_shape=[jax.ShapeDtypeStruct((s, ATTN_W), BF16), jax.ShapeDtypeStruct((s, 2 * KV_W), BF16),
                   jax.ShapeDtypeStruct((s, GMLP_W), BF16), jax.ShapeDtypeStruct((s, GMLP_W), BF16),
                   jax.ShapeDtypeStruct((N_HEADS, BLOCK, 2 * BLOCK), F32),
                   jax.ShapeDtypeStruct((N_GROUPS, BLOCK, BLOCK), F32),
                   jax.ShapeDtypeStruct((BLOCK, GMLP_W), F32), jax.ShapeDtypeStruct((8, GMLP_W), F32),
                   jax.ShapeDtypeStruct((N_HEADS, BLOCK, 1), F32)]
        + [jax.ShapeDtypeStruct(_rx_shape(p.shape, k), BF16) for p, k in zip(grad_parts, grad_kinds)],
        grid=(n_steps + 1,),
        in_specs=[cur(ATTN_W), cur(2 * KV_W), before, cur(GMLP_W), cur(GMLP_W), cur(D_MODEL),
                  _const_spec((N_HEADS, BLOCK, 2 * BLOCK)), pl.BlockSpec(memory_space=pltpu.SMEM),
                  _const_spec((1, GMLP_W)), _const_spec((1, GMLP_W)), _const_spec((N_GROUPS, BLOCK, BLOCK)),
                  _const_spec((BLOCK, GMLP_W)), _const_spec((GMLP_W, GMLP_W)), _const_spec((1, ATTN_W)),
                  _const_spec((1, GMLP_W))] + [ANY] * n_g,
        out_specs=[cur(ATTN_W), late(2 * KV_W), cur(GMLP_W), cur(GMLP_W),
                   _const_spec((N_HEADS, BLOCK, 2 * BLOCK)), _const_spec((N_GROUPS, BLOCK, BLOCK)),
                   _const_spec((BLOCK, GMLP_W)), _const_spec((8, GMLP_W)), _const_spec((N_HEADS, BLOCK, 1))]
        + [ANY] * n_g,
        scratch_shapes=[pltpu.VMEM((BLOCK, 2 * KV_W), F32), pltpu.VMEM((BLOCK, 2 * KV_W), F32)]
        + _ChipExchange.sems(n_g),
        compiler_params=_params(("arbitrary",)),
    )(q, kv, kv, gu, gv, dmix, bias, sinks, gln_g, gln_b, wsm, bsx, amat, aog, gog, *grad_parts)
    return outs[:9], outs[9:]


def _mix_finalize(gbias, bucket, dws, dbs, dsink):
    def body(gb_ref, bucket_ref, dws_ref, dbs_ref, dsink_ref, tall_ref):
        bk = bucket_ref[...]
        lane = lax.broadcasted_iota(jnp.int32, (N_BUCKETS, LANES), 1)
        rowi = lax.broadcasted_iota(jnp.int32, (N_BUCKETS, LANES), 0)
        drb = jnp.zeros((N_BUCKETS, LANES), F32)
        dsk = jnp.zeros((8, LANES), F32)
        lane8 = lax.broadcasted_iota(jnp.int32, (8, LANES), 1)
        for h in range(N_HEADS):
            g = gb_ref[h]
            for b in range(N_BUCKETS):
                tot = jnp.sum(_colsum(jnp.where(bk == float(b), g, 0.0)), axis=1, keepdims=True)
                drb = jnp.where((lane == h) & (rowi == b), tot, drb)
            sk = jnp.sum(dsink_ref[h], axis=0, keepdims=True)
            dsk = jnp.where(lane8 == h, sk, dsk)
        tall_ref[TALL_RB:TALL_RB + N_BUCKETS, :] = drb
        tall_ref[TALL_SK:TALL_SK + 8, :] = dsk
        ti = lax.broadcasted_iota(jnp.int32, (BLOCK, BLOCK), 0)
        ui = lax.broadcasted_iota(jnp.int32, (BLOCK, BLOCK), 1)
        for g in range(N_GROUPS):
            tall_ref[g * BLOCK:(g + 1) * BLOCK, :] = jnp.where(ti >= ui, dws_ref[g], 0.0)
        gi = lax.broadcasted_iota(jnp.int32, (GMLP_W, LANES), 0) // GROUP_DIM
        li = lax.broadcasted_iota(jnp.int32, (GMLP_W, LANES), 1)
        ind = jnp.where(gi == li, 1.0, 0.0).astype(BF16)
        d = dbs_ref[...]
        hi = d.astype(BF16)
        r1 = d - hi.astype(F32)
        mid = r1.astype(BF16)
        lo = (r1 - mid.astype(F32)).astype(BF16)
        dbsg = _dot(hi, ind) + _dot(mid, ind) + _dot(lo, ind)
        tall_ref[TALL_BS:TALL_BS + N_GROUPS, :] = dbsg.T[:N_GROUPS, :]

    return pl.pallas_call(
        body, name="mix_finalize", out_shape=jax.ShapeDtypeStruct((TALL_ROWS, LANES), F32), grid=(1,),
        in_specs=[_const_spec((N_HEADS, BLOCK, 2 * BLOCK)), _const_spec((BLOCK, 2 * BLOCK)),
                  _const_spec((N_GROUPS, BLOCK, BLOCK)), _const_spec((BLOCK, GMLP_W)),
                  _const_spec((N_HEADS, BLOCK, 1))],
        out_specs=_const_spec((TALL_ROWS, LANES)),
        compiler_params=_params(("arbitrary",)),
    )(gbias, bucket, dws, dbs, dsink)


def _bwd_in(dq, dkv, dgu, dgv, dxa, x, modr, w_in, tm):
    s = x.shape[0]

    def body(dq_ref, dkv_ref, dgu_ref, dgv_ref, dxa_ref, x_ref, mod_ref, w_ref, gx_ref, acc_ref, db_ref):
        @pl.when(pl.program_id(0) == 0)
        def _():
            acc_ref[...] = jnp.zeros_like(acc_ref)
            db_ref[...] = jnp.zeros_like(db_ref)

        dproj = jnp.concatenate([dq_ref[...], dkv_ref[...], dgu_ref[...], dgv_ref[...]], axis=1)
        dh1 = _dot_nt(dproj, w_ref[...])
        gx_ref[...] = dxa_ref[...] + dh1 * (1.0 + mod_ref[1:2, :])
        acc_ref[0:1, :] += _colsum(dh1 * x_ref[...])
        acc_ref[1:2, :] += _colsum(dh1)
        db_ref[0:1, :] += _colsum(dproj.astype(F32))

    row = lambda w: pl.BlockSpec((tm, w), lambda i: (i, 0))
    return pl.pallas_call(
        body, name="bwd_in",
        out_shape=(jax.ShapeDtypeStruct((s, D_MODEL), F32), jax.ShapeDtypeStruct((8, D_MODEL), F32),
                   jax.ShapeDtypeStruct((8, IN_W), F32)),
        grid=(s // tm,),
        in_specs=[row(ATTN_W), row(2 * KV_W), row(GMLP_W), row(GMLP_W), row(D_MODEL), row(D_MODEL),
                  _const_spec((8, D_MODEL)), _const_spec((D_MODEL, IN_W))],
        out_specs=(row(D_MODEL), _const_spec((8, D_MODEL)), _const_spec((8, IN_W))),
        compiler_params=_params(("arbitrary",)),
    )(dq, dkv, dgu, dgv, dxa, x, modr, w_in)


def _wgrad(a, bs, tm, tk, name, owner_blocks=False, gather_vs=()):
    k_all, m = a.shape
    n = sum(b.shape[1] for b in bs)
    nk = k_all // tk
    nm = m // tm
    n_b = len(bs)
    n_v = len(gather_vs)
    wb = n // N_CHIPS

    def body(a_ref, *rest):
        b_refs, v_refs = rest[:n_b], rest[n_b:n_b + n_v]
        o_ref, ob_ref = rest[n_b + n_v:n_b + n_v + 2]
        vg_refs = rest[n_b + n_v + 2:n_b + 2 * n_v + 2]
        i, k = pl.program_id(0), pl.program_id(1)
        if n_v:
            gather = _Gather8(v_refs, vg_refs, *rest[n_b + 2 * n_v + 2:])

            @pl.when((i == 0) & (k == 0))
            def _():
                gather.start()

            @pl.when((i == nm - 1) & (k == 0))
            def _():
                gather.forward()

        @pl.when(k == 0)
        def _():
            o_ref[...] = jnp.zeros_like(o_ref)

        b = b_refs[0][...] if n_b == 1 else jnp.concatenate([r[...] for r in b_refs], axis=1)
        if owner_blocks:
            av = a_ref[...]
            for j in range(N_CHIPS):
                o_ref[j] += _dot_tn(av, b[:, j * wb:(j + 1) * wb])
        else:
            o_ref[...] += _dot_tn(a_ref[...], b)

        @pl.when(k == nk - 1)
        def _():
            ob_ref[...] = o_ref[...].astype(BF16)

        if n_v:
            @pl.when((i == nm - 1) & (k == nk - 1))
            def _():
                gather.finish()

    if owner_blocks:
        out_spec = pl.BlockSpec((N_CHIPS, tm, wb), lambda i, k: (0, i, 0))
        shape = (N_CHIPS, m, wb)
    else:
        out_spec = pl.BlockSpec((tm, n), lambda i, k: (i, 0))
        shape = (m, n)
    outs = pl.pallas_call(
        body, name=name,
        out_shape=[jax.ShapeDtypeStruct(shape, F32), jax.ShapeDtypeStruct(shape, BF16)] + _gathered8_shapes(gather_vs),
        grid=(nm, nk),
        in_specs=[pl.BlockSpec((tk, tm), lambda i, k: (k, i))]
        + [pl.BlockSpec((tk, b.shape[1]), lambda i, k: (k, 0)) for b in bs] + [ANY] * n_v,
        out_specs=[out_spec, out_spec] + [ANY] * n_v,
        scratch_shapes=_Gather8.sems(n_v) if n_v else [],
        compiler_params=_params(("arbitrary", "arbitrary") if n_v else ("parallel", "arbitrary")),
    )(a, *bs, *gather_vs)
    return outs[0], outs[1], outs[2:]


def _adam_math(w, g, m, v):
    m2 = ADAM_B1 * m + (1.0 - ADAM_B1) * g
    v2 = ADAM_B2 * v + (1.0 - ADAM_B2) * (g * g)
    m_hat = m2 / (1.0 - ADAM_B1 ** ADAM_STEP)
    v_hat = v2 / (1.0 - ADAM_B2 ** ADAM_STEP)
    delta = -ADAM_LR * (m_hat / (jnp.sqrt(v_hat) + ADAM_EPS) + ADAM_WD * w)
    return delta, m2, v2


def _adam_halves(w, mine, got, m, v, tr, name):
    r, cc = w.shape
    h = r // 2
    nt = h // tr

    def body(c_ref, w_ref, mine_ref, got_ref, m_ref, v_ref, g_ref, d_ref, m2_ref, v2_ref):
        g = jnp.where(pl.program_id(0) == c_ref[0], mine_ref[...], got_ref[...])
        g_ref[...] = g
        d, m2, v2 = _adam_math(w_ref[...], g, m_ref[...], v_ref[...])
        d_ref[...] = d
        m2_ref[...] = m2
        v2_ref[...] = v2

    full = pl.BlockSpec((tr, cc), lambda hh, i, c_ref: (hh * nt + i, 0))
    half = pl.BlockSpec((tr, cc), lambda hh, i, c_ref: (i, 0))
    shp = jax.ShapeDtypeStruct((r, cc), F32)
    return pl.pallas_call(
        body, name=name, out_shape=(shp, shp, shp, shp),
        grid_spec=pltpu.PrefetchScalarGridSpec(
            num_scalar_prefetch=1, grid=(2, nt), in_specs=[full, half, half, full, full],
            out_specs=(full, full, full, full)),
        compiler_params=_params(("arbitrary", "arbitrary")),
    )(_core_index_scalar(), w, mine, got, m, v)


def _adam_w_ada(sc_t, dmod_cols, w, m, v, tr):
    r, cc = w.shape

    def body(sct_ref, dm_ref, w_ref, m_ref, v_ref, g_ref, d_ref, m2_ref, v2_ref):
        g = sct_ref[:, 0:1] * dm_ref[0:1, :]
        for k in range(1, N_DEV):
            g = g + sct_ref[:, k:k + 1] * dm_ref[k:k + 1, :]
        g_ref[...] = g
        d, m2, v2 = _adam_math(w_ref[...], g, m_ref[...], v_ref[...])
        d_ref[...] = d
        m2_ref[...] = m2
        v2_ref[...] = v2

    spec = pl.BlockSpec((tr, cc), lambda i: (i, 0))
    shp = jax.ShapeDtypeStruct((r, cc), F32)
    return pl.pallas_call(
        body, name="adam_w_ada", out_shape=(shp, shp, shp, shp), grid=(r // tr,),
        in_specs=[pl.BlockSpec((tr, N_DEV), lambda i: (i, 0)), _const_spec((N_DEV, cc)), spec, spec, spec],
        out_specs=(spec, spec, spec, spec), compiler_params=_params(("parallel",)),
    )(sc_t, dmod_cols, w, m, v)


def _pack_wide(acc_i, acc_m, acc_f, db_in, vec):
    arrs = [acc_i, acc_m, acc_f, db_in, vec]
    i_, m_, f_, b_, v_ = range(5)
    src = {"b_in": (b_, 0), "ln1_g": (m_, 2), "ln1_b": (m_, 3), "ln2_g": (f_, 1), "ln2_b": (f_, 2),
           "gmlp_ln_g": (v_, 2), "gmlp_ln_b": (v_, 3), "attn_out_g": (v_, 0), "gmlp_out_g": (v_, 1), "loss": (f_, 0)}
    dmod = [(i_, 1), (i_, 0), (m_, 4), (m_, 1), (m_, 0), (f_, 3)]

    def body(*refs):
        ins, wide_ref = refs[:5], refs[5]
        wide_ref[...] = jnp.zeros_like(wide_ref)
        for k, (a, row) in enumerate(dmod):
            wide_ref[0:1, k * D_MODEL:(k + 1) * D_MODEL] = ins[a][row:row + 1, :]
        for name, (a, row) in src.items():
            r, off, n = WIDE_LAYOUT[name]
            wide_ref[r:r + 1, off:off + n] = ins[a][row:row + 1, :]

    return pl.pallas_call(
        body, name="pack_wide", out_shape=jax.ShapeDtypeStruct((8, WIDE_W), F32), grid=(1,),
        in_specs=[_const_spec(a.shape) for a in arrs], out_specs=_const_spec((8, WIDE_W)),
        compiler_params=_params(("arbitrary",)),
    )(*arrs)


def _adam_small(gw, gt, wide_wmv, w_s, b_s, rel_bias, sinks):
    names = list(WIDE_PARAMS)
    tall = [("gmlp_w_s", w_s), ("gmlp_b_s", b_s), ("rel_bias", rel_bias), ("attn_sinks", sinks)]
    ins = [gw, gt]
    for n in names:
        ins += list(wide_wmv[n])
    for _, t in tall:
        ins += list(t)
    n_in = len(ins)

    def body(*refs):
        gw_ref, gt_ref = refs[0], refs[1]
        wmv = refs[2:n_in]
        dmod_ref, loss_ref = refs[n_in], refs[n_in + 1]
        outs = refs[n_in + 2:]

        def tall_sum(r0, nr):
            g = gt_ref[r0:r0 + nr, :]
            for d in range(1, N_DEV):
                g = g + gt_ref[d * TALL_ROWS + r0:d * TALL_ROWS + r0 + nr, :]
            return g

        def emit(k, g, w_ref, m_ref, v_ref):
            d, m2, v2 = _adam_math(w_ref[...], g, m_ref[...], v_ref[...])
            outs[4 * k][...] = g
            outs[4 * k + 1][...] = d
            outs[4 * k + 2][...] = m2
            outs[4 * k + 3][...] = v2

        gsum = gw_ref[0:8, :]
        for d in range(1, N_DEV):
            gsum = gsum + gw_ref[8 * d:8 * d + 8, :]
        for d in range(N_DEV):
            dmod_ref[d:d + 1, :] = gw_ref[8 * d:8 * d + 1, :]
        for k, n in enumerate(names):
            r, off, sz = WIDE_LAYOUT[n]
            emit(k, gsum[r:r + 1, off:off + sz], *wmv[3 * k:3 * k + 3])
        r, off, sz = WIDE_LAYOUT["loss"]
        tot = jnp.sum(gsum[r:r + 1, off:off + sz], axis=1, keepdims=True)
        loss_ref[...] = jnp.broadcast_to(tot * (0.5 / D_MODEL), loss_ref.shape)

        k0 = len(names)
        ws_refs = wmv[3 * k0:3 * k0 + 3]
        for g in range(N_GROUPS):
            rows = slice(g * BLOCK, (g + 1) * BLOCK)
            gg = tall_sum(g * BLOCK, BLOCK)
            d, m2, v2 = _adam_math(ws_refs[0][rows, :], gg, ws_refs[1][rows, :], ws_refs[2][rows, :])
            outs[4 * k0][rows, :] = gg
            outs[4 * k0 + 1][rows, :] = d
            outs[4 * k0 + 2][rows, :] = m2
            outs[4 * k0 + 3][rows, :] = v2
        emit(k0 + 1, tall_sum(TALL_BS, N_GROUPS), *wmv[3 * (k0 + 1):3 * (k0 + 1) + 3])
        emit(k0 + 2, tall_sum(TALL_RB, N_BUCKETS)[:, :N_HEADS], *wmv[3 * (k0 + 2):3 * (k0 + 2) + 3])
        emit(k0 + 3, tall_sum(TALL_SK, 8)[0:1, :N_HEADS], *wmv[3 * (k0 + 3):3 * (k0 + 3) + 3])

    out_shapes = [jax.ShapeDtypeStruct((N_DEV, WIDE_W), F32), jax.ShapeDtypeStruct((8, LANES), F32)]
    for n in names:
        out_shapes += [jax.ShapeDtypeStruct(wide_wmv[n][0].shape, F32)] * 4
    for _, t in tall:
        out_shapes += [jax.ShapeDtypeStruct(t[0].shape, F32)] * 4
    res = pl.pallas_call(
        body, name="adam_small", out_shape=out_shapes, grid=(1,),
        in_specs=[_const_spec(a.shape) for a in ins], out_specs=[_const_spec(o.shape) for o in out_shapes],
        compiler_params=_params(("arbitrary",)),
    )(*ins)
    out = {}
    for k, n in enumerate(names + [t[0] for t in tall]):
        out[n] = tuple(res[2 + 4 * k:6 + 4 * k])
    return res[0], res[1], out


def kernel(x, c, rel_bias, w_ada, b_ada, w_in, b_in, attn_sinks, gmlp_ln_g, gmlp_ln_b, gmlp_w_s, gmlp_b_s, attn_out_g, gmlp_out_g, w_out, ln1_g, ln1_b, w_gate_up, w_down, ln2_g, ln2_b, loss_target, m_rel_bias, m_w_ada, m_b_ada, m_w_in, m_b_in, m_attn_sinks, m_gmlp_ln_g, m_gmlp_ln_b, m_gmlp_w_s, m_gmlp_b_s, m_attn_out_g, m_gmlp_out_g, m_w_out, m_ln1_g, m_ln1_b, m_w_gate_up, m_w_down, m_ln2_g, m_ln2_b, v_rel_bias, v_w_ada, v_b_ada, v_w_in, v_b_in, v_attn_sinks, v_gmlp_ln_g, v_gmlp_ln_b, v_gmlp_w_s, v_gmlp_b_s, v_attn_out_g, v_gmlp_out_g, v_w_out, v_ln1_g, v_ln1_b, v_w_gate_up, v_w_down, v_ln2_g, v_ln2_b):
    ix, iy, ic = _my_pos()
    chip = 2 * ix + iy
    dev = 4 * ix + 2 * iy + ic
    s = x.shape[1]
    xs = x[0]
    tgt = loss_target[0]
    tm_big = min(512, s)
    tm_ffn = min(FFN_SUB, s)
    n_ada = w_ada.shape[2]

    w_in_s, w_out_s = w_in[0].astype(BF16), w_out[0].astype(BF16)
    w_gu_s, w_dn_s = w_gate_up[0].astype(BF16), w_down[0].astype(BF16)
    sc_all, _, mod_rows, w_in_g = _prologue(
        jnp.pad(c, ((0, 7), (0, 0))), w_ada[0], lax.dynamic_slice_in_dim(b_ada, chip * n_ada, n_ada, axis=1), w_in_s)
    mod_all = mod_rows.reshape(N_DEV, N_DEV, -1)
    mod_row = lax.dynamic_index_in_dim(mod_all[0::2], dev, axis=1, keepdims=False)
    modr = jnp.pad(mod_row.reshape(6, D_MODEL), ((0, 2), (0, 0)))
    w_in_g = _insert_own(w_in_g, w_in_s, "blk", chip)
    w_in_f = jnp.transpose(w_in_g, (1, 0, 2)).reshape(D_MODEL, IN_W)

    bucket = _bucket_table()
    bias, wsm = _prep_tables(bucket, rel_bias, gmlp_w_s[0])
    bsx = jnp.repeat(gmlp_b_s[0].T, GROUP_DIM, axis=1)
    amat = _group_mean_matrix()
    sinks = attn_sinks[0]

    (h1, q, kv, gu, gv), (w_out_g, w_dn_g) = _fwd_in(xs, modr, w_in_f, b_in, tm_big, [w_out_s, w_dn_s], ["blk", "blk"])
    w_out_f = _insert_own(w_out_g, w_out_s, "blk", chip).reshape(D_MODEL, D_MODEL)
    x1, y, mixed, (w_gu_g,) = _fwd_mix(
        q, kv, gu, gv, xs, modr, bias, sinks, gmlp_ln_g, gmlp_ln_b, wsm, bsx, amat, attn_out_g, gmlp_out_g, w_out_f,
        ln1_g, ln1_b, tm_big, [w_gu_s], ["blk"])
    assert w_gate_up.shape[2] == FF_CHUNK
    w_gu_f = _insert_own(w_gu_g, w_gu_s, "blk", chip)
    w_dn_f = _insert_own(w_dn_g, w_dn_s, "blk", chip).reshape(D_FF, D_MODEL)
    h2, act, dy2, dx1a, acc_f = _fwd_ffn(x1, tgt, modr, ln2_g, ln2_b, w_gu_f, w_dn_f, tm_ffn)

    a_act, dgu_ff, dh2 = _bwd_ffn(dy2, act, w_gu_f, w_dn_f, min(FFN_SUB, s))
    g_dn, g_dn_b, _ = _wgrad(a_act, [dy2], D_FF // 2, min(512, s), "wgrad_down")
    g_gu, g_gu_b, _ = _wgrad(h2, [dgu_ff], 512, min(512, s), "wgrad_gate_up")
    blk3 = lambda a, rows: a.reshape(N_CHIPS, rows, a.shape[1])
    (dxa, dy, dmix, acc_m), (got_dn, got_gu) = _bwd_mid(
        dh2, dx1a, x1, xs, y, modr, ln1_g, w_out_f, tm_big, [blk3(g_dn_b, D_FF // N_CHIPS), g_gu_b], ["blk", "cols"])
    g_out, g_out_b, _ = _wgrad(mixed, [dy], 512, min(512, s), "wgrad_out")
    (got_out,) = _swap_halves([blk3(g_out_b, D_MODEL // N_CHIPS)], ["blk"], "rs_swap_out")
    kinds_a = ["blk", "cols", "blk"]
    fulls_a = [blk3(g_dn, D_FF // N_CHIPS), g_gu, blk3(g_out, D_MODEL // N_CHIPS)]
    gots_a = [got_dn, got_gu, got_out]
    parts_a = [_add_halves(f, g, k, "rs_add_a%d" % i) for i, (f, g, k) in enumerate(zip(fulls_a, gots_a, kinds_a))]
    (dq, dkv, dgu, dgv, gbias, dws, dbs, vec, dsink), rxs_a = _bwd_mix(
        q, kv, gu, gv, dmix, bias, sinks, gmlp_ln_g, gmlp_ln_b, wsm, bsx, amat, attn_out_g, gmlp_out_g,
        [p[1] for p in parts_a], kinds_a)
    tall_g = _mix_finalize(gbias, bucket, dws, dbs, dsink)
    grad_x, acc_i, db_in = _bwd_in(dq, dkv, dgu, dgv, dxa, xs, modr, w_in_f, tm_big)

    wide_g = _pack_wide(acc_i, acc_m, acc_f, db_in, vec)
    full_in, full_in_b, (gw, gt) = _wgrad(h1, [dq, dkv, dgu, dgv], 512, min(512, s), "wgrad_in", owner_blocks=True,
                                          gather_vs=[wide_g, tall_g])
    (got_in,) = _swap_halves([full_in_b], ["blk"], "rs_swap_in")
    part_in = _add_halves(full_in, got_in, "blk", "rs_add_in")
    (rx_in,) = _exchange_chip_partials([part_in[1]], ["blk"], "rs_chips_in")
    wide_wmv = {"b_ada": (b_ada, m_b_ada, v_b_ada), "b_in": (b_in, m_b_in, v_b_in),
                "ln1_g": (ln1_g, m_ln1_g, v_ln1_g), "ln1_b": (ln1_b, m_ln1_b, v_ln1_b),
                "ln2_g": (ln2_g, m_ln2_g, v_ln2_g), "ln2_b": (ln2_b, m_ln2_b, v_ln2_b),
                "gmlp_ln_g": (gmlp_ln_g, m_gmlp_ln_g, v_gmlp_ln_g), "gmlp_ln_b": (gmlp_ln_b, m_gmlp_ln_b, v_gmlp_ln_b),
                "attn_out_g": (attn_out_g, m_attn_out_g, v_attn_out_g),
                "gmlp_out_g": (gmlp_out_g, m_gmlp_out_g, v_gmlp_out_g)}
    rows2 = lambda a: a.reshape(-1, a.shape[-1])
    dmod_all, loss_t, small = _adam_small(
        gw, gt, wide_wmv, tuple(rows2(a) for a in (gmlp_w_s, m_gmlp_w_s, v_gmlp_w_s)),
        tuple(rows2(a) for a in (gmlp_b_s, m_gmlp_b_s, v_gmlp_b_s)), (rel_bias, m_rel_bias, v_rel_bias),
        (attn_sinks, m_attn_sinks, v_attn_sinks))
    loss = loss_t[0, 0]

    dmod_cols = lax.dynamic_slice_in_dim(dmod_all, chip * n_ada, n_ada, axis=1)
    g_ada, d_ada, m_ada, v_ada = _adam_w_ada(sc_all.T, dmod_cols, w_ada[0], m_w_ada[0], v_w_ada[0], 256)

    sums = [(parts_a[0][0], rxs_a[0], "blk", 176), (parts_a[1][0], rxs_a[1], "cols", 256),
            (parts_a[2][0], rxs_a[2], "blk", 128), (part_in[0], rx_in, "blk", 256)]
    mine = [_sum_chips(p, rx, k, tr, "rs_sum_%d" % i) for i, (p, rx, k, tr) in enumerate(sums)]
    got = _share_halves(mine, "rs_share")

    gs_dn, d_dn, m_dn, v_dn = _adam_halves(w_down[0], mine[0], got[0], m_w_down[0], v_w_down[0], 176, "adam_w_down")
    gs_gu, d_gu, m_gu, v_gu = _adam_halves(w_gate_up[0], mine[1], got[1], m_w_gate_up[0], v_w_gate_up[0], 256,
                                           "adam_w_gate_up")
    gs_out, d_out, m_out, v_out = _adam_halves(w_out[0], mine[2], got[2], m_w_out[0], v_w_out[0], 128, "adam_w_out")
    gs_in, d_in, m_in, v_in = _adam_halves(w_in[0], mine[3], got[3], m_w_in[0], v_w_in[0], 256, "adam_w_in")

    big = {"w_ada": (g_ada, d_ada, m_ada, v_ada), "w_in": (gs_in, d_in, m_in, v_in), "w_out": (gs_out, d_out, m_out, v_out),
           "w_gate_up": (gs_gu, d_gu, m_gu, v_gu), "w_down": (gs_dn, d_dn, m_dn, v_dn)}
    order = ["rel_bias", "w_ada", "b_ada", "w_in", "b_in", "attn_sinks", "gmlp_ln_g", "gmlp_ln_b", "gmlp_w_s", "gmlp_b_s",
             "attn_out_g", "gmlp_out_g", "w_out", "ln1_g", "ln1_b", "w_gate_up", "w_down", "ln2_g", "ln2_b"]
    shapes = {"gmlp_w_s": gmlp_w_s.shape, "gmlp_b_s": gmlp_b_s.shape}
    outs = [loss, grad_x[None]]
    for k in range(4):
        for name in order:
            if name in big:
                outs.append(big[name][k][None])
            elif name in shapes:
                outs.append(small[name][k].reshape(shapes[name]))
            else:
                outs.append(small[name][k])
    return tuple(outs)
```

```python
import math

import numpy as np
import jax
import jax.numpy as jnp
from jax import lax
from jax.experimental import pallas as pl
from jax.experimental.pallas import tpu as pltpu

F32 = jnp.float32
BF16 = jnp.bfloat16
MESH = pl.DeviceIdType.MESH

D_MODEL = 1024
N_HEADS = 8
N_KV = 2
HEAD_DIM = 64
ATTN_W = N_HEADS * HEAD_DIM
KV_W = N_KV * HEAD_DIM
N_GROUPS = 8
GROUP_DIM = 64
GMLP_W = N_GROUPS * GROUP_DIM
IN_W = ATTN_W + 2 * KV_W + 2 * GMLP_W
BLOCK = 128
N_BUCKETS = 32
MAX_DISTANCE = 128
D_FF = 2816
ALPHA = 2.0 ** 0.25
LN_EPS = 1e-5
NEG_INF = -1e30
ADAM_LR, ADAM_B1, ADAM_B2, ADAM_EPS, ADAM_WD, ADAM_STEP = 0.001, 0.9, 0.999, 1e-8, 0.01, 10
N_CHIPS = 4
N_DEV = 8
LANES = 128
V7X_VMEM_LIMIT = 56 * 2 ** 20
GELU_C = math.sqrt(2.0 / math.pi)
Q_SCALE = HEAD_DIM ** -0.5
ANY = pl.BlockSpec(memory_space=pl.ANY)

TALL_BS = N_GROUPS * BLOCK
TALL_RB = TALL_BS + 8
TALL_SK = TALL_RB + N_BUCKETS
TALL_ROWS = TALL_SK + 8
WIDE_W = 6 * D_MODEL
WIDE_LAYOUT = {
    "b_ada": (0, 0, 6 * D_MODEL),
    "b_in": (1, 0, IN_W), "ln1_g": (1, IN_W, D_MODEL), "ln1_b": (1, IN_W + D_MODEL, D_MODEL),
    "ln2_g": (1, IN_W + 2 * D_MODEL, D_MODEL), "ln2_b": (1, IN_W + 3 * D_MODEL, D_MODEL),
    "gmlp_ln_g": (2, 0, GMLP_W), "gmlp_ln_b": (2, GMLP_W, GMLP_W), "attn_out_g": (2, 2 * GMLP_W, ATTN_W),
    "gmlp_out_g": (2, 2 * GMLP_W + ATTN_W, GMLP_W), "loss": (2, 3 * GMLP_W + ATTN_W, D_MODEL)}
WIDE_PARAMS = tuple(n for n in WIDE_LAYOUT if n != "loss")


def _params(sem=None):
    return pltpu.CompilerParams(dimension_semantics=sem, vmem_limit_bytes=V7X_VMEM_LIMIT)


def _const_spec(shape, single=False):
    nd = len(shape)
    if single:
        return pl.BlockSpec(shape, lambda *_: (0,) * nd, pipeline_mode=pl.Buffered(1))
    return pl.BlockSpec(shape, lambda *_: (0,) * nd)


def _dot(a, b):
    return jnp.dot(a, b, preferred_element_type=F32)


def _dot_nt(a, b):
    return lax.dot_general(a, b, (((1,), (1,)), ((), ())), preferred_element_type=F32)


def _dot_tn(a, b):
    return lax.dot_general(a, b, (((0,), (0,)), ((), ())), preferred_element_type=F32)


def _gelu(x):
    t = jnp.tanh(GELU_C * (x + 0.044715 * x * x * x))
    return 0.5 * x * (1.0 + t), t


def _gelu_grad(x, t):
    return 0.5 * (1.0 + t) + 0.5 * x * (1.0 - t * t) * GELU_C * (1.0 + 3.0 * 0.044715 * x * x)


def _split_dot(x, a):
    hi = x.astype(BF16)
    lo = (x - hi.astype(F32)).astype(BF16)
    return _dot(hi, a) + _dot(lo, a)


def _group_mean_matrix():
    g = np.arange(GMLP_W) // GROUP_DIM
    return jnp.asarray((g[:, None] == g[None, :]).astype(np.float32) / GROUP_DIM, dtype=BF16)


def _ln_stats(z):
    mu = jnp.mean(z, axis=-1, keepdims=True)
    d = z - mu
    var = jnp.mean(d * d, axis=-1, keepdims=True)
    rstd = lax.rsqrt(var + LN_EPS)
    return d * rstd, rstd


def _ln_bwd(dxhat, xhat, rstd):
    m1 = jnp.mean(dxhat, axis=-1, keepdims=True)
    m2 = jnp.mean(dxhat * xhat, axis=-1, keepdims=True)
    return rstd * (dxhat - m1 - xhat * m2)


def _colsum(x):
    return jnp.sum(x, axis=0, keepdims=True)


def _my_pos():
    return lax.axis_index("x"), lax.axis_index("y"), lax.axis_index("c")


def _other_chips(x, y):
    return [(1 - x, y), (x, 1 - y), (1 - x, 1 - y)]


def _chip_index_scalar():
    ix, iy, _ = _my_pos()
    return jnp.reshape(2 * ix + iy, (1,)).astype(jnp.int32)


def _core_index_scalar():
    return jnp.reshape(lax.axis_index("c"), (1,)).astype(jnp.int32)


class _Gather8:
    def __init__(self, x_refs, out_refs, send_sems, recv_sems, local_sems):
        self.x_refs, self.out_refs = x_refs, out_refs
        self.send_sems, self.recv_sems, self.local_sems = send_sems, recv_sems, local_sems
        self.x, self.y, self.c = _my_pos()
        self.me, self.sibling = (self.x, self.y, self.c), (self.x, self.y, 1 - self.c)
        self.chips = _other_chips(self.x, self.y)

    def _rows(self, a, px, py, pc):
        m_per = self.x_refs[a].shape[0]
        return self.out_refs[a].at[pl.ds((4 * px + 2 * py + pc) * m_per, m_per), :]

    def _copy(self, a, k, block, to, src=None):
        return pltpu.make_async_remote_copy(
            src_ref=self._rows(a, *block) if src is None else src, dst_ref=self._rows(a, *block),
            send_sem=self.send_sems.at[7 * a + k], recv_sem=self.recv_sems.at[7 * a + k], device_id=to,
            device_id_type=MESH)

    def _local(self, a):
        return pltpu.make_async_copy(self.x_refs[a], self._rows(a, *self.me), self.local_sems.at[a])

    def start(self):
        for a in range(len(self.x_refs)):
            self._local(a).start()
            self._copy(a, 0, self.me, self.sibling, src=self.x_refs[a]).start()
            for j, chip in enumerate(self.chips):
                self._copy(a, 1 + j, self.me, (*chip, self.c), src=self.x_refs[a]).start()

    def forward(self):
        for a in range(len(self.x_refs)):
            for j, chip in enumerate(self.chips):
                self._copy(a, 1 + j, (*chip, self.c), self.me).wait_recv()
                self._copy(a, 4 + j, (*chip, self.c), self.sibling).start()

    def finish(self):
        for a in range(len(self.x_refs)):
            self._copy(a, 0, self.sibling, self.me).wait_recv()
            for j, chip in enumerate(self.chips):
                self._copy(a, 4 + j, (*chip, 1 - self.c), self.me).wait_recv()
        for a in range(len(self.x_refs)):
            for k in range(7):
                self._copy(a, k, self.me, self.me).wait_send()
            self._local(a).wait()

    @staticmethod
    def sems(n_v):
        return [pltpu.SemaphoreType.DMA((7 * n_v,)), pltpu.SemaphoreType.DMA((7 * n_v,)),
                pltpu.SemaphoreType.DMA((n_v,))]


def _gathered8_shapes(vs):
    return [jax.ShapeDtypeStruct((N_DEV * v.shape[0], v.shape[1]), v.dtype) for v in vs]


VMEM_WHOLE = pl.BlockSpec(memory_space=pltpu.VMEM)


def _prologue(c_pad, w_ada_s, b_ada_s, w_in_s):
    n = w_ada_s.shape[1]

    def body(c_ref, w_ref, b_ref, win_ref, sc_ref, modc_ref, modg_ref, wing_ref, call_ref, *sems):
        weights = _WeightGather([win_ref], [wing_ref], ["blk"], sems[0], sems[1])
        gather_c = _Gather8([c_ref], [call_ref], sems[2], sems[3], sems[4])
        gather_mod = _Gather8([modc_ref], [modg_ref], sems[5], sems[6], sems[7])
        weights.start()
        gather_c.start()
        gather_c.forward()
        gather_c.finish()
        cv = call_ref[...]
        sc = cv * _sigmoid(cv)
        a_hi = sc.astype(BF16)
        a_lo = (sc - a_hi.astype(F32)).astype(BF16)
        w = w_ref[...]
        w_hi = w.astype(BF16)
        w_lo = (w - w_hi.astype(F32)).astype(BF16)
        mod = _dot(a_hi, w_hi) + _dot(a_hi, w_lo) + _dot(a_lo, w_hi) + b_ref[...]
        for d in range(N_DEV):
            sc_ref[d:d + 1, :] = sc[8 * d:8 * d + 1, :]
            modc_ref[d:d + 1, :] = mod[8 * d:8 * d + 1, :]
        gather_mod.start()
        gather_mod.forward()
        gather_mod.finish()
        weights.forward()
        weights.forward_diagonal()
        weights.finish()

    return pl.pallas_call(
        body, name="prologue",
        out_shape=(jax.ShapeDtypeStruct((N_DEV, D_MODEL), F32), jax.ShapeDtypeStruct((N_DEV, n), F32),
                   jax.ShapeDtypeStruct((N_DEV * N_DEV, n), F32),
                   jax.ShapeDtypeStruct(_gathered_shape(w_in_s, "blk"), BF16)),
        in_specs=[VMEM_WHOLE, VMEM_WHOLE, VMEM_WHOLE, ANY],
        out_specs=(VMEM_WHOLE, VMEM_WHOLE, VMEM_WHOLE, ANY),
        scratch_shapes=[pltpu.VMEM((N_DEV * 8, D_MODEL), F32)] + _WeightGather.sems(1) + _Gather8.sems(1)
        + _Gather8.sems(1),
        compiler_params=pltpu.CompilerParams(vmem_limit_bytes=V7X_VMEM_LIMIT),
    )(c_pad, w_ada_s, b_ada_s, w_in_s)


def _gathered_shape(shard, kind):
    r, cc = shard.shape
    return (N_CHIPS, r, cc) if kind == "blk" else (r, N_CHIPS * cc)


class _WeightGather:
    N_SEM = 8

    def __init__(self, shards, gathered, kinds, send_sems, recv_sems):
        self.shards, self.gathered, self.kinds = shards, gathered, kinds
        self.send_sems, self.recv_sems = send_sems, recv_sems
        self.x, self.y, self.c = _my_pos()
        self.me, self.sibling = (self.x, self.y, self.c), (self.x, self.y, 1 - self.c)
        self.nbr = ((1 - self.x, self.y), (self.x, 1 - self.y))
        self.diag = 2 * (1 - self.x) + (1 - self.y)

    def _dst(self, a, chip, pc, quarter=None):
        r, cc = self.shards[a].shape
        h = r // 2
        row0, rows = pc * h, h
        if quarter is not None:
            row0, rows = pc * h + quarter * (h // 2), h // 2
        g = self.gathered[a]
        if self.kinds[a] == "blk":
            return g.at[chip, pl.ds(row0, rows), :]
        return g.at[pl.ds(row0, rows), pl.ds(chip * cc, cc)]

    def _copy(self, a, k, region, to, src=None):
        return pltpu.make_async_remote_copy(
            src_ref=region if src is None else src, dst_ref=region, send_sem=self.send_sems.at[a * self.N_SEM + k],
            recv_sem=self.recv_sems.at[a * self.N_SEM + k], device_id=to, device_id_type=MESH)

    def _arrays(self):
        return range(len(self.shards))

    def start(self):
        my_chip = 2 * self.x + self.y
        for a in self._arrays():
            h = self.shards[a].shape[0] // 2
            mine = self.shards[a].at[pl.ds(self.c * h, h), :]
            for j, chip in enumerate(self.nbr):
                self._copy(a, j, self._dst(a, my_chip, self.c), (*chip, self.c), src=mine).start()

    def forward(self):
        for a in self._arrays():
            for j, chip in enumerate(self.nbr):
                cj = 2 * chip[0] + chip[1]
                half = self._dst(a, cj, self.c)
                self._copy(a, j, half, self.me).wait_recv()
                self._copy(a, 2 + j, half, self.sibling).start()
                other = self.nbr[1 - j]
                self._copy(a, 4 + j, self._dst(a, cj, self.c, quarter=j), (*other, self.c)).start()

    def forward_diagonal(self):
        for a in self._arrays():
            for j in range(2):
                quarter = self._dst(a, self.diag, self.c, quarter=j)
                self._copy(a, 4 + j, quarter, self.me).wait_recv()
                self._copy(a, 6 + j, quarter, self.sibling).start()

    def finish(self):
        for a in self._arrays():
            for j, chip in enumerate(self.nbr):
                self._copy(a, 2 + j, self._dst(a, 2 * chip[0] + chip[1], 1 - self.c), self.me).wait_recv()
                self._copy(a, 6 + j, self._dst(a, self.diag, 1 - self.c, quarter=j), self.me).wait_recv()
        for a in self._arrays():
            half = self._dst(a, self.diag, self.c)
            quarter = self._dst(a, self.diag, self.c, quarter=0)
            for k in range(self.N_SEM):
                self._copy(a, k, half if k < 4 else quarter, self.me).wait_send()

    @classmethod
    def sems(cls, n_arr):
        return [pltpu.SemaphoreType.DMA((n_arr * cls.N_SEM,)), pltpu.SemaphoreType.DMA((n_arr * cls.N_SEM,))]


def _insert_own(gathered, shard, kind, chip):
    if kind == "blk":
        return lax.dynamic_update_slice(gathered, shard[None], (chip, 0, 0))
    return lax.dynamic_update_slice(gathered, shard, (0, chip * shard.shape[1]))


def _half_of_full(ref, kind, pc):
    if kind == "blk":
        h = ref.shape[1] // 2
        return ref.at[:, pl.ds(pc * h, h), :]
    h = ref.shape[0] // 2
    return ref.at[pl.ds(pc * h, h), :]


def _half_shape(shape, kind):
    return (shape[0], shape[1] // 2, shape[2]) if kind == "blk" else (shape[0] // 2, shape[1])


class _HalfSwap:
    def __init__(self, ins, outs, kinds, send_sems, recv_sems):
        self.ins, self.outs, self.kinds = ins, outs, kinds
        self.send_sems, self.recv_sems = send_sems, recv_sems
        self.x, self.y, self.c = _my_pos()

    def _copies(self):
        for a in range(len(self.ins)):
            yield pltpu.make_async_remote_copy(
                src_ref=_half_of_full(self.ins[a], self.kinds[a], 1 - self.c), dst_ref=self.outs[a],
                send_sem=self.send_sems.at[a], recv_sem=self.recv_sems.at[a],
                device_id=(self.x, self.y, 1 - self.c), device_id_type=MESH)

    def start(self):
        for cp in self._copies():
            cp.start()

    def wait(self):
        for cp in self._copies():
            cp.wait()

    @staticmethod
    def sems(n_arr):
        return [pltpu.SemaphoreType.DMA((n_arr,)), pltpu.SemaphoreType.DMA((n_arr,))]

    @staticmethod
    def out_shapes(fulls, kinds):
        return [jax.ShapeDtypeStruct(_half_shape(a.shape, k), a.dtype) for a, k in zip(fulls, kinds)]


def _swap_halves(fulls_bf16, kinds, name):
    n_arr = len(fulls_bf16)

    def body(*refs):
        swap = _HalfSwap(refs[:n_arr], refs[n_arr:2 * n_arr], kinds, *refs[2 * n_arr:])
        swap.start()
        swap.wait()

    return pl.pallas_call(
        body, name=name, out_shape=_HalfSwap.out_shapes(fulls_bf16, kinds),
        in_specs=[ANY] * n_arr, out_specs=[ANY] * n_arr, scratch_shapes=_HalfSwap.sems(n_arr),
    )(*fulls_bf16)


def _add_halves(full, got, kind, name):
    hs = _half_shape(full.shape, kind)

    def body(pos_ref, a_ref, b_ref, o_ref, ob_ref):
        p = a_ref[...] + b_ref[...].astype(F32)
        ob_ref[...] = p.astype(BF16)

        @pl.when(pl.program_id(0) == pos_ref[1])
        def _():
            o_ref[...] = p.reshape(o_ref.shape)

    if kind == "blk":
        nb, h, cc = hs
        own = pl.BlockSpec((1, h, cc), lambda b, pos_ref: (b, pos_ref[0], 0))
        other = pl.BlockSpec((1, h, cc), lambda b, pos_ref: (b, 0, 0))
    else:
        h, cc = hs[0], hs[1] // N_CHIPS
        own = pl.BlockSpec((h, cc), lambda b, pos_ref: (pos_ref[0], b))
        other = pl.BlockSpec((h, cc), lambda b, pos_ref: (0, b))
    pos = jnp.concatenate([_core_index_scalar(), _chip_index_scalar()])
    return pl.pallas_call(
        body, name=name, out_shape=(jax.ShapeDtypeStruct((h, cc), F32), jax.ShapeDtypeStruct(hs, BF16)),
        grid_spec=pltpu.PrefetchScalarGridSpec(
            num_scalar_prefetch=1, grid=(N_CHIPS,), in_specs=[own, other],
            out_specs=(pl.BlockSpec((h, cc), lambda b, pos_ref: (0, 0)), other)),
        compiler_params=_params(("arbitrary",)),
    )(pos, full, got)


def _rx_shape(part_shape, kind):
    if kind == "blk":
        return (3, part_shape[1], part_shape[2])
    return (3, part_shape[0], part_shape[1] // N_CHIPS)


class _ChipExchange:
    def __init__(self, parts, rxs, kinds, send_sems, recv_sems):
        self.parts, self.rxs, self.kinds = parts, rxs, kinds
        self.send_sems, self.recv_sems = send_sems, recv_sems
        self.x, self.y, self.c = _my_pos()
        self.chips = _other_chips(self.x, self.y)

    def _copies(self):
        for a in range(len(self.parts)):
            for j, chip in enumerate(self.chips):
                cj = 2 * chip[0] + chip[1]
                if self.kinds[a] == "blk":
                    src = self.parts[a].at[cj]
                else:
                    cc = self.parts[a].shape[1] // N_CHIPS
                    src = self.parts[a].at[:, pl.ds(cj * cc, cc)]
                yield pltpu.make_async_remote_copy(
                    src_ref=src, dst_ref=self.rxs[a].at[j], send_sem=self.send_sems.at[a * 3 + j],
                    recv_sem=self.recv_sems.at[a * 3 + j], device_id=(*chip, self.c), device_id_type=MESH)

    def start(self):
        for cp in self._copies():
            cp.start()

    def wait(self):
        for cp in self._copies():
            cp.wait_recv()
        for cp in self._copies():
            cp.wait_send()

    @staticmethod
    def sems(n_arr):
        return [pltpu.SemaphoreType.DMA((n_arr * 3,)), pltpu.SemaphoreType.DMA((n_arr * 3,))]


def _exchange_chip_partials(parts, kinds, name):
    n_arr = len(parts)

    def body(*refs):
        exchange = _ChipExchange(refs[:n_arr], refs[n_arr:2 * n_arr], kinds, *refs[2 * n_arr:])
        exchange.start()
        exchange.wait()

    return pl.pallas_call(
        body, name=name,
        out_shape=[jax.ShapeDtypeStruct(_rx_shape(p.shape, k), BF16) for p, k in zip(parts, kinds)],
        in_specs=[ANY] * n_arr, out_specs=[ANY] * n_arr, scratch_shapes=_ChipExchange.sems(n_arr),
    )(*parts)


def _sum_chips(part, rx, tr, name):
    _, h, cc = rx.shape
    flips = (2, 1, 3)

    def body(chip_ref, p_ref, rx_ref, o_ref):
        own = p_ref[...]
        for mc in range(N_CHIPS):
            @pl.when(chip_ref[0] == mc)
            def _():
                terms = sorted([(mc, None)] + [(mc ^ f, j) for j, f in enumerate(flips)])
                acc = None
                for _, j in terms:
                    t = own if j is None else rx_ref[j].astype(F32)
                    acc = t if acc is None else acc + t
                o_ref[...] = acc

    return pl.pallas_call(
        body, name=name, out_shape=jax.ShapeDtypeStruct((h, cc), F32),
        grid_spec=pltpu.PrefetchScalarGridSpec(
            num_scalar_prefetch=1, grid=(h // tr,),
            in_specs=[pl.BlockSpec((tr, cc), lambda i, chip_ref: (i, 0)),
                      pl.BlockSpec((3, tr, cc), lambda i, chip_ref: (0, i, 0))],
            out_specs=pl.BlockSpec((tr, cc), lambda i, chip_ref: (i, 0))),
        compiler_params=_params(("arbitrary",)),
    )(_chip_index_scalar(), part, rx)


def _share_halves(halves, name):
    n_arr = len(halves)

    def body(*refs):
        ins, outs = refs[:n_arr], refs[n_arr:2 * n_arr]
        send_sems, recv_sems = refs[2 * n_arr:]
        x, y, c = _my_pos()
        cps = []
        for a in range(n_arr):
            cp = pltpu.make_async_remote_copy(
                src_ref=ins[a], dst_ref=outs[a], send_sem=send_sems.at[a], recv_sem=recv_sems.at[a],
                device_id=(x, y, 1 - c), device_id_type=MESH)
            cp.start()
            cps.append(cp)
        for cp in cps:
            cp.wait()

    return pl.pallas_call(
        body, name=name, out_shape=[jax.ShapeDtypeStruct(h.shape, h.dtype) for h in halves],
        in_specs=[ANY] * n_arr, out_specs=[ANY] * n_arr,
        scratch_shapes=[pltpu.SemaphoreType.DMA((n_arr,)), pltpu.SemaphoreType.DMA((n_arr,))],
    )(*halves)


def _bucket_table():
    qi = jnp.arange(BLOCK)[:, None]
    si = jnp.arange(2 * BLOCK)[None, :]
    dist = qi + BLOCK - si
    max_exact = N_BUCKETS // 2
    n = jnp.maximum(dist, 0)
    nf = jnp.maximum(n, max_exact).astype(F32)
    large = max_exact + (jnp.log(nf / max_exact) / math.log(MAX_DISTANCE / max_exact)
                         * (N_BUCKETS - max_exact)).astype(jnp.int32)
    large = jnp.minimum(large, N_BUCKETS - 1)
    return jnp.where(n < max_exact, n, large).astype(F32)


def _prep_tables(bucket, rel_bias, w_s):
    def body(bucket_ref, rb_ref, ws_ref, bias_ref, wsm_ref):
        qi = lax.broadcasted_iota(jnp.int32, (BLOCK, 2 * BLOCK), 0)
        si = lax.broadcasted_iota(jnp.int32, (BLOCK, 2 * BLOCK), 1)
        dist = qi + BLOCK - si
        in_window = (dist >= 0) & (dist < BLOCK)
        bk = bucket_ref[...]
        for h in range(N_HEADS):
            acc = jnp.zeros((BLOCK, 2 * BLOCK), F32)
            for b in range(N_BUCKETS):
                acc = jnp.where(bk == float(b), rb_ref[b, h], acc)
            bias_ref[h] = jnp.where(in_window, acc, NEG_INF)
        ti = lax.broadcasted_iota(jnp.int32, (BLOCK, BLOCK), 0)
        ui = lax.broadcasted_iota(jnp.int32, (BLOCK, BLOCK), 1)
        for g in range(N_GROUPS):
            wsm_ref[g] = jnp.where(ti >= ui, ws_ref[g], 0.0).astype(BF16)

    return pl.pallas_call(
        body, name="prep_tables",
        out_shape=(jax.ShapeDtypeStruct((N_HEADS, BLOCK, 2 * BLOCK), F32),
                   jax.ShapeDtypeStruct((N_GROUPS, BLOCK, BLOCK), BF16)),
        grid=(1,),
        in_specs=[_const_spec((BLOCK, 2 * BLOCK)), pl.BlockSpec(memory_space=pltpu.SMEM),
                  _const_spec((N_GROUPS, BLOCK, BLOCK))],
        out_specs=(_const_spec((N_HEADS, BLOCK, 2 * BLOCK)), _const_spec((N_GROUPS, BLOCK, BLOCK))),
        compiler_params=_params(("arbitrary",)),
    )(bucket, rel_bias, w_s)


def _fwd_in(x, modr, w_in, b_in, tm, shards, kinds):
    s = x.shape[0]
    n_steps = s // tm
    fwd_step, diag_step = (11 * n_steps) // 16, (15 * n_steps) // 16
    n_w = len(shards)

    def body(x_ref, mod_ref, w_ref, b_ref, *rest):
        shard_refs = rest[:n_w]
        h1_ref, q_ref, kv_ref, gu_ref, gv_ref = rest[n_w:n_w + 5]
        gathered_refs = rest[n_w + 5:2 * n_w + 5]
        send_sems, recv_sems = rest[2 * n_w + 5:]
        i = pl.program_id(0)
        gather = _WeightGather(shard_refs, gathered_refs, kinds, send_sems, recv_sems)

        @pl.when(i == 0)
        def _():
            gather.start()

        h1 = (x_ref[...] * (1.0 + mod_ref[1:2, :]) + mod_ref[0:1, :]).astype(BF16)
        h1_ref[...] = h1
        proj = _dot(h1, w_ref[...]) + b_ref[...]
        q_ref[...] = (proj[:, :ATTN_W] * Q_SCALE).astype(BF16)
        kv_ref[...] = proj[:, ATTN_W:ATTN_W + 2 * KV_W].astype(BF16)
        gu_ref[...] = proj[:, ATTN_W + 2 * KV_W:ATTN_W + 2 * KV_W + GMLP_W]
        gv_ref[...] = proj[:, ATTN_W + 2 * KV_W + GMLP_W:]

        @pl.when(i == fwd_step)
        def _():
            gather.forward()

        @pl.when(i == diag_step)
        def _():
            gather.forward_diagonal()

        @pl.when(i == n_steps - 1)
        def _():
            gather.finish()

    row = lambda w: pl.BlockSpec((tm, w), lambda i: (i, 0))
    outs = pl.pallas_call(
        body, name="fwd_in",
        out_shape=[jax.ShapeDtypeStruct((s, D_MODEL), BF16), jax.ShapeDtypeStruct((s, ATTN_W), BF16),
                   jax.ShapeDtypeStruct((s, 2 * KV_W), BF16), jax.ShapeDtypeStruct((s, GMLP_W), F32),
                   jax.ShapeDtypeStruct((s, GMLP_W), F32)]
        + [jax.ShapeDtypeStruct(_gathered_shape(sh, k), BF16) for sh, k in zip(shards, kinds)],
        grid=(n_steps,),
        in_specs=[row(D_MODEL), _const_spec((8, D_MODEL)), _const_spec((D_MODEL, IN_W)), _const_spec((1, IN_W))]
        + [ANY] * n_w,
        out_specs=[row(D_MODEL), row(ATTN_W), row(2 * KV_W), row(GMLP_W), row(GMLP_W)] + [ANY] * n_w,
        scratch_shapes=_WeightGather.sems(n_w),
        compiler_params=_params(("arbitrary",)),
    )(x, modr, w_in, b_in, *shards)
    return outs[:5], outs[5:]


def _kv_variants(kk):
    kf = kk.astype(F32)
    lane = lax.broadcasted_iota(jnp.int32, kf.shape, 1)
    low = lane < HEAD_DIM
    k0_lo = jnp.where(low, kf, 0.0)
    k1_hi = jnp.where(low, 0.0, kf)
    k0_hi = pltpu.roll(k0_lo, HEAD_DIM, 1)
    k1_lo = pltpu.roll(k1_hi, HEAD_DIM, 1)
    return ((k0_lo.astype(BF16), k0_hi.astype(BF16)), (k1_lo.astype(BF16), k1_hi.astype(BF16)))


def _head_kv(h):
    return h // (N_HEADS // N_KV), h % 2


MIX_GROUP = 2


def _interleave(*gens):
    results = [None] * len(gens)
    active = list(enumerate(gens))
    while active:
        still = []
        for i, g in active:
            try:
                next(g)
                still.append((i, g))
            except StopIteration as done:
                results[i] = done.value
        active = still
    return results


def _attn_block_fwd(q_blk, kk, vv, bias_ref, sinks_ref, first_mask):
    kvar = _kv_variants(kk)
    vvar = _kv_variants(vv)
    heads = range(N_HEADS)
    q_pairs = [q_blk[:, (h // 2) * LANES:(h // 2 + 1) * LANES] for h in heads]
    logits = [_dot_nt(q_pairs[h], kvar[_head_kv(h)[0]][_head_kv(h)[1]]) + bias_ref[h] for h in heads]
    if first_mask is not None:
        logits = [jnp.where(first_mask, NEG_INF, lg) for lg in logits]
    yield
    ms = [jnp.maximum(jnp.max(logits[h], axis=-1, keepdims=True), sinks_ref[h]) for h in heads]
    yield
    es = [jnp.exp(logits[h] - ms[h]) for h in heads]
    ess = [jnp.exp(sinks_ref[h] - ms[h]) for h in heads]
    yield
    invs = [1.0 / (jnp.sum(es[h], axis=-1, keepdims=True) + ess[h]) for h in heads]
    probs = [(es[h] * invs[h], ess[h] * invs[h]) for h in heads]
    yield
    outs = [_dot(probs[h][0].astype(BF16), vvar[_head_kv(h)[0]][_head_kv(h)[1]]) for h in heads]
    pairs = [outs[2 * i] + outs[2 * i + 1] for i in range(N_HEADS // 2)]
    return jnp.concatenate(pairs, axis=1), probs, kvar, vvar


def _gmlp_chunk_fwd(gu, gv, ln_g, ln_b, wsm_ref, bsx, amat):
    u, tu = _gelu(gu)
    a, ta = _gelu(gv)
    yield
    mean = _split_dot(a, amat)
    d = a - mean
    yield
    var = _split_dot(d * d, amat)
    yield
    rstd = lax.rsqrt(var + LN_EPS)
    xhat = d * rstd
    vb = (xhat * ln_g + ln_b).astype(BF16)
    yield
    lane = lax.broadcasted_iota(jnp.int32, (BLOCK, LANES), 1)
    low = lane < GROUP_DIM
    cols = []
    for pair in range(N_GROUPS // 2):
        vp = vb[:, pair * LANES:(pair + 1) * LANES]
        cols.append(jnp.where(low, _dot(wsm_ref[2 * pair], vp), _dot(wsm_ref[2 * pair + 1], vp)))
    mixedv = jnp.concatenate(cols, axis=1) + bsx
    return u * mixedv, (u, tu, ta, xhat, rstd, vb, mixedv)


def _rms(a, g):
    r = lax.rsqrt(jnp.mean(a * a, axis=-1, keepdims=True) + LN_EPS)
    return a * r * g, r


def _fwd_mix(q, kv, gu, gv, x, modr, bias, sinks, gln_g, gln_b, wsm, bsx, amat, aog, gog, w_out, ln1_g, ln1_b, tm,
             ffn_shards, ffn_kinds):
    s = x.shape[0]
    nb = tm // BLOCK
    n_steps = s // tm
    fwd_step, diag_step = (7 * n_steps) // 16, (12 * n_steps) // 16
    n_w = len(ffn_shards)

    def body(q_ref, kv_ref, kvp_ref, gu_ref, gv_ref, x_ref, mod_ref, bias_ref, sinks_ref, glng_ref, glnb_ref, wsm_ref,
             bsx_ref, amat_ref, aog_ref, gog_ref, wout_ref, ln1g_ref, ln1b_ref, *rest):
        shard_refs = rest[:n_w]
        x1_ref, y_ref, mixed_ref = rest[n_w:n_w + 3]
        gathered_refs = rest[n_w + 3:2 * n_w + 3]
        mix_scr, send_sems, recv_sems = rest[2 * n_w + 3:]
        i = pl.program_id(0)
        gather = _WeightGather(shard_refs, gathered_refs, ffn_kinds, send_sems, recv_sems)

        @pl.when(i == 0)
        def _():
            gather.start()

        col = lax.broadcasted_iota(jnp.int32, (BLOCK, 2 * BLOCK), 1)
        for b0 in range(0, nb, MIX_GROUP):
            gens = []
            for b in range(b0, min(b0 + MIX_GROUP, nb)):
                r0 = b * BLOCK
                if b == 0:
                    kvprev = kvp_ref[...]
                    first_mask = (col < BLOCK) & (i == 0)
                else:
                    kvprev = kv_ref[r0 - BLOCK:r0, :]
                    first_mask = None
                kvcur = kv_ref[r0:r0 + BLOCK, :]
                kk = jnp.concatenate([kvprev[:, :KV_W], kvcur[:, :KV_W]], axis=0)
                vv = jnp.concatenate([kvprev[:, KV_W:], kvcur[:, KV_W:]], axis=0)
                gens.append(_attn_block_fwd(q_ref[r0:r0 + BLOCK, :], kk, vv, bias_ref, sinks_ref, first_mask))
                gens.append(_gmlp_chunk_fwd(gu_ref[r0:r0 + BLOCK, :], gv_ref[r0:r0 + BLOCK, :], glng_ref[...],
                                            glnb_ref[...], wsm_ref, bsx_ref[...], amat_ref[...]))
            res = _interleave(*gens)
            for k, b in enumerate(range(b0, min(b0 + MIX_GROUP, nb))):
                r0 = b * BLOCK
                na, _ = _rms(res[2 * k][0], aog_ref[...])
                ng, _ = _rms(res[2 * k + 1][0], gog_ref[...])
                mix_scr[r0:r0 + BLOCK, :ATTN_W] = na.astype(BF16)
                mix_scr[r0:r0 + BLOCK, ATTN_W:] = ng.astype(BF16)
        mixed = mix_scr[...]
        mixed_ref[...] = mixed
        y = _dot(mixed, wout_ref[...])
        y_ref[...] = y.astype(BF16)
        z1 = ALPHA * x_ref[...] + mod_ref[2:3, :] * y
        xhat, _ = _ln_stats(z1)
        x1_ref[...] = xhat * ln1g_ref[...] + ln1b_ref[...]

        @pl.when(i == fwd_step)
        def _():
            gather.forward()

        @pl.when(i == diag_step)
        def _():
            gather.forward_diagonal()

        @pl.when(i == n_steps - 1)
        def _():
            gather.finish()

    row = lambda w: pl.BlockSpec((tm, w), lambda i: (i, 0))
    prev = pl.BlockSpec((BLOCK, 2 * KV_W), lambda i: (jnp.maximum(i * nb - 1, 0), 0))
    outs = pl.pallas_call(
        body, name="fwd_mix",
        out_shape=[jax.ShapeDtypeStruct((s, D_MODEL), F32), jax.ShapeDtypeStruct((s, D_MODEL), BF16),
                   jax.ShapeDtypeStruct((s, D_MODEL), BF16)]
        + [jax.ShapeDtypeStruct(_gathered_shape(sh, k), BF16) for sh, k in zip(ffn_shards, ffn_kinds)],
        grid=(n_steps,),
        in_specs=[row(ATTN_W), row(2 * KV_W), prev, row(GMLP_W), row(GMLP_W), row(D_MODEL), _const_spec((8, D_MODEL)),
                  _const_spec((N_HEADS, BLOCK, 2 * BLOCK)), pl.BlockSpec(memory_space=pltpu.SMEM),
                  _const_spec((1, GMLP_W)), _const_spec((1, GMLP_W)), _const_spec((N_GROUPS, BLOCK, BLOCK)),
                  _const_spec((BLOCK, GMLP_W)), _const_spec((GMLP_W, GMLP_W)), _const_spec((1, ATTN_W)),
                  _const_spec((1, GMLP_W)), _const_spec((D_MODEL, D_MODEL)), _const_spec((1, D_MODEL)),
                  _const_spec((1, D_MODEL))] + [ANY] * n_w,
        out_specs=[row(D_MODEL), row(D_MODEL), row(D_MODEL)] + [ANY] * n_w,
        scratch_shapes=[pltpu.VMEM((tm, D_MODEL), BF16)] + _WeightGather.sems(n_w),
        compiler_params=_params(("arbitrary",)),
    )(q, kv, kv, gu, gv, x, modr, bias, sinks, gln_g, gln_b, wsm, bsx, amat, aog, gog, w_out, ln1_g, ln1_b, *ffn_shards)
    return outs[0], outs[1], outs[2], outs[3:]


FF_BLOCKS = N_CHIPS // 2
FF_CHUNK = D_FF // FF_BLOCKS
FFN_SUB = 256


def _sigmoid(x):
    return 1.0 / (1.0 + jnp.exp(-x))


def _fwd_ffn(x1, target, modr, ln2_g, ln2_b, w_gu, w_dn, tm):
    s = x1.shape[0]

    def body(x1_ref, t_ref, mod_ref, g_ref, b_ref, wgu_ref, wdn_ref, h2_ref, act_ref, dy2_ref, dx1a_ref, acc_ref):
        @pl.when(pl.program_id(0) == 0)
        def _():
            acc_ref[...] = jnp.zeros_like(acc_ref)

        x1v = x1_ref[...]
        h2 = (x1v * (1.0 + mod_ref[4:5, :]) + mod_ref[3:4, :]).astype(BF16)
        h2_ref[...] = h2
        y2 = None
        for cc in range(FF_BLOCKS):
            c0 = cc * FF_CHUNK
            gate = _dot(h2, wgu_ref[cc])
            up = _dot(h2, wgu_ref[FF_BLOCKS + cc])
            act_ref[:, c0:c0 + FF_CHUNK] = gate.astype(BF16)
            act_ref[:, D_FF + c0:D_FF + c0 + FF_CHUNK] = up.astype(BF16)
            a = (gate * _sigmoid(gate) * up).astype(BF16)
            part = _dot(a, wdn_ref[c0:c0 + FF_CHUNK, :])
            y2 = part if y2 is None else y2 + part
        g2 = mod_ref[5:6, :]
        z2 = ALPHA * x1v + g2 * y2
        xhat, rstd = _ln_stats(z2)
        gain = g_ref[...]
        diff = xhat * gain + b_ref[...] - t_ref[...]
        dx2 = diff * (1.0 / D_MODEL)
        dz2 = _ln_bwd(dx2 * gain, xhat, rstd)
        dx1a_ref[...] = ALPHA * dz2
        dy2_ref[...] = (g2 * dz2).astype(BF16)
        acc_ref[0:1, :] += _colsum(diff * diff)
        acc_ref[1:2, :] += _colsum(dx2 * xhat)
        acc_ref[2:3, :] += _colsum(dx2)
        acc_ref[3:4, :] += _colsum(dz2 * y2)

    row = lambda w: pl.BlockSpec((tm, w), lambda i: (i, 0))
    return pl.pallas_call(
        body, name="fwd_ffn",
        out_shape=(jax.ShapeDtypeStruct((s, D_MODEL), BF16), jax.ShapeDtypeStruct((s, 2 * D_FF), BF16),
                   jax.ShapeDtypeStruct((s, D_MODEL), BF16), jax.ShapeDtypeStruct((s, D_MODEL), F32),
                   jax.ShapeDtypeStruct((8, D_MODEL), F32)),
        grid=(s // tm,),
        in_specs=[row(D_MODEL), row(D_MODEL), _const_spec((8, D_MODEL)), _const_spec((1, D_MODEL)),
                  _const_spec((1, D_MODEL)), _const_spec((N_CHIPS, D_MODEL, FF_CHUNK), single=True),
                  _const_spec((D_FF, D_MODEL), single=True)],
        out_specs=(row(D_MODEL), row(2 * D_FF), row(D_MODEL), row(D_MODEL), _const_spec((8, D_MODEL))),
        compiler_params=_params(("arbitrary",)),
    )(x1, target, modr, ln2_g, ln2_b, w_gu, w_dn)


def _bwd_ffn(dy2, act, w_gu, w_dn, tm):
    s = dy2.shape[0]

    def body(dy2_ref, act_ref, wgu_ref, wdn_ref, a_ref, dgu_ref, dh2_ref):
        dy2v = dy2_ref[...]
        dh2 = None
        for cc in range(FF_BLOCKS):
            c0 = cc * FF_CHUNK
            da = _dot_nt(dy2v, wdn_ref[c0:c0 + FF_CHUNK, :])
            gate = act_ref[:, c0:c0 + FF_CHUNK].astype(F32)
            up = act_ref[:, D_FF + c0:D_FF + c0 + FF_CHUNK].astype(F32)
            sg = _sigmoid(gate)
            sl = gate * sg
            a_ref[:, c0:c0 + FF_CHUNK] = (sl * up).astype(BF16)
            dgate = (da * up * (sg * (1.0 + gate * (1.0 - sg)))).astype(BF16)
            dup = (da * sl).astype(BF16)
            dgu_ref[:, c0:c0 + FF_CHUNK] = dgate
            dgu_ref[:, D_FF + c0:D_FF + c0 + FF_CHUNK] = dup
            part = _dot_nt(dgate, wgu_ref[cc]) + _dot_nt(dup, wgu_ref[FF_BLOCKS + cc])
            dh2 = part if dh2 is None else dh2 + part
        dh2_ref[...] = dh2.astype(BF16)

    row = lambda w: pl.BlockSpec((tm, w), lambda i: (i, 0))
    return pl.pallas_call(
        body, name="bwd_ffn",
        out_shape=(jax.ShapeDtypeStruct((s, D_FF), BF16), jax.ShapeDtypeStruct((s, 2 * D_FF), BF16),
                   jax.ShapeDtypeStruct((s, D_MODEL), BF16)),
        grid=(s // tm,),
        in_specs=[row(D_MODEL), row(2 * D_FF), _const_spec((N_CHIPS, D_MODEL, FF_CHUNK), single=True),
                  _const_spec((D_FF, D_MODEL), single=True)],
        out_specs=(row(D_FF), row(2 * D_FF), row(D_MODEL)),
        compiler_params=_params(("parallel",)),
    )(dy2, act, w_gu, w_dn)


def _bwd_mid(dh2, dx1a, x1, x, y, modr, ln1_g, w_out, tm, swap_fulls, swap_kinds):
    s = x.shape[0]
    n_steps = s // tm
    n_g = len(swap_fulls)

    def body(dh2_ref, dx1a_ref, x1_ref, x_ref, y_ref, mod_ref, g_ref, wout_ref, *rest):
        full_refs = rest[:n_g]
        dxa_ref, dy_ref, dmix_ref, acc_ref = rest[n_g:n_g + 4]
        got_refs = rest[n_g + 4:2 * n_g + 4]
        swap = _HalfSwap(full_refs, got_refs, swap_kinds, *rest[2 * n_g + 4:])
        i = pl.program_id(0)

        @pl.when(i == 0)
        def _():
            swap.start()
            acc_ref[...] = jnp.zeros_like(acc_ref)

        dh2 = dh2_ref[...].astype(F32)
        x1v = x1_ref[...]
        yv = y_ref[...].astype(F32)
        g1 = mod_ref[2:3, :]
        dx1 = dx1a_ref[...] + dh2 * (1.0 + mod_ref[4:5, :])
        z1 = ALPHA * x_ref[...] + g1 * yv
        xhat, rstd = _ln_stats(z1)
        dz1 = _ln_bwd(dx1 * g_ref[...], xhat, rstd)
        dxa_ref[...] = ALPHA * dz1
        dy = (g1 * dz1).astype(BF16)
        dy_ref[...] = dy
        dmix_ref[...] = _dot_nt(dy, wout_ref[...]).astype(BF16)
        acc_ref[0:1, :] += _colsum(dh2 * x1v)
        acc_ref[1:2, :] += _colsum(dh2)
        acc_ref[2:3, :] += _colsum(dx1 * xhat)
        acc_ref[3:4, :] += _colsum(dx1)
        acc_ref[4:5, :] += _colsum(dz1 * yv)

        @pl.when(i == n_steps - 1)
        def _():
            swap.wait()

    row = lambda w: pl.BlockSpec((tm, w), lambda i: (i, 0))
    outs = pl.pallas_call(
        body, name="bwd_mid",
        out_shape=[jax.ShapeDtypeStruct((s, D_MODEL), F32), jax.ShapeDtypeStruct((s, D_MODEL), BF16),
                   jax.ShapeDtypeStruct((s, D_MODEL), BF16), jax.ShapeDtypeStruct((8, D_MODEL), F32)]
        + _HalfSwap.out_shapes(swap_fulls, swap_kinds),
        grid=(n_steps,),
        in_specs=[row(D_MODEL)] * 5 + [_const_spec((8, D_MODEL)), _const_spec((1, D_MODEL)),
                                       _const_spec((D_MODEL, D_MODEL))] + [ANY] * n_g,
        out_specs=[row(D_MODEL), row(D_MODEL), row(D_MODEL), _const_spec((8, D_MODEL))] + [ANY] * n_g,
        scratch_shapes=_HalfSwap.sems(n_g),
        compiler_params=_params(("arbitrary",)),
    )(dh2, dx1a, x1, x, y, modr, ln1_g, w_out, *swap_fulls)
    return outs[:4], outs[4:]


def _fold_kv(t0, t1):
    lane = lax.broadcasted_iota(jnp.int32, t0.shape, 1)
    f0 = t0 + pltpu.roll(t0, HEAD_DIM, 1)
    f1 = t1 + pltpu.roll(t1, HEAD_DIM, 1)
    return jnp.where(lane < HEAD_DIM, f0, f1)


def _bwd_mix(q, kv, gu, gv, dmix, bias, sinks, gln_g, gln_b, wsm, bsx, amat, aog, gog, grad_parts, grad_kinds):
    s = q.shape[0]
    tile = 2 * BLOCK
    n_steps = s // tile
    n_g = len(grad_parts)

    def body(q_ref, kv_ref, kvp_ref, gu_ref, gv_ref, dmix_ref, bias_ref, sinks_ref, glng_ref, glnb_ref, wsm_ref,
             bsx_ref, amat_ref, aog_ref, gog_ref, *rest):
        part_refs = rest[:n_g]
        dq_ref, dkv_ref, dgu_ref, dgv_ref, gbias_ref, dws_ref, dbs_ref, vec_ref, dsink_ref = rest[n_g:n_g + 9]
        rx_refs = rest[n_g + 9:2 * n_g + 9]
        carry, done, send_sems, recv_sems = rest[2 * n_g + 9:]
        n = pl.program_id(0)
        exchange = _ChipExchange(part_refs, rx_refs, grad_kinds, send_sems, recv_sems)

        @pl.when(n == 0)
        def _():
            exchange.start()
            carry[...] = jnp.zeros_like(carry)
            done[...] = jnp.zeros_like(done)
            gbias_ref[...] = jnp.zeros_like(gbias_ref)
            dws_ref[...] = jnp.zeros_like(dws_ref)
            dbs_ref[...] = jnp.zeros_like(dbs_ref)
            vec_ref[...] = jnp.zeros_like(vec_ref)
            dsink_ref[...] = jnp.zeros_like(dsink_ref)

        @pl.when(n == n_steps)
        def _():
            dkv_ref[:BLOCK, :] = done[...].astype(BF16)
            dkv_ref[BLOCK:, :] = carry[...].astype(BF16)
            exchange.wait()

        @pl.when(n < n_steps)
        def _():
            col = lax.broadcasted_iota(jnp.int32, (BLOCK, 2 * BLOCK), 1)
            lane = lax.broadcasted_iota(jnp.int32, (BLOCK, LANES), 1)
            low = lane < HEAD_DIM
            rows = [slice(0, BLOCK), slice(BLOCK, tile)]
            kv_blocks = [kvp_ref[...], kv_ref[rows[0], :], kv_ref[rows[1], :]]
            masks = [(col < BLOCK) & (n == 0), None]
            q_blks = [q_ref[r, :] for r in rows]
            fwd = []
            for b in range(2):
                kk = jnp.concatenate([kv_blocks[b][:, :KV_W], kv_blocks[b + 1][:, :KV_W]], axis=0)
                vv = jnp.concatenate([kv_blocks[b][:, KV_W:], kv_blocks[b + 1][:, KV_W:]], axis=0)
                fwd.append(_attn_block_fwd(q_blks[b], kk, vv, bias_ref, sinks_ref, masks[b]))
                fwd.append(_gmlp_chunk_fwd(gu_ref[rows[b], :], gv_ref[rows[b], :], glng_ref[...], glnb_ref[...],
                                           wsm_ref, bsx_ref[...], amat_ref[...]))
            res = _interleave(*fwd[:2]) + _interleave(*fwd[2:])

            def gating_bwd(b, d_gm, saved):
                u, tu, ta, xhat, rstd, vb, mixedv = saved
                dgu_ref[rows[b], :] = (d_gm * mixedv * _gelu_grad(gu_ref[rows[b], :], tu)).astype(BF16)
                dmx = d_gm * u
                dmxb = dmx.astype(BF16)
                yield
                dvn_cols, dws = [], []
                for pair in range(N_GROUPS // 2):
                    dp_ = dmxb[:, pair * LANES:(pair + 1) * LANES]
                    vp = vb[:, pair * LANES:(pair + 1) * LANES]
                    dvn_cols.append(
                        jnp.where(low, _dot_tn(wsm_ref[2 * pair], dp_), _dot_tn(wsm_ref[2 * pair + 1], dp_)))
                    zero = jnp.zeros_like(dp_)
                    dws.append(_dot_nt(jnp.where(low, dp_, zero), vp))
                    dws.append(_dot_nt(jnp.where(low, zero, dp_), vp))
                dvn = jnp.concatenate(dvn_cols, axis=1)
                yield
                dxh = dvn * glng_ref[...]
                am = amat_ref[...]
                m1 = _split_dot(dxh, am)
                m2 = _split_dot(dxh * xhat, am)
                yield
                da = rstd * (dxh - m1 - xhat * m2)
                dgv_ref[rows[b], :] = (da * _gelu_grad(gv_ref[rows[b], :], ta)).astype(BF16)
                return dmx, dws, _colsum(dvn * xhat), _colsum(dvn)

            def attention_bwd(b, d_attn, probs, kvar, vvar):
                heads = range(N_HEADS)
                sels = [low if h % 2 == 0 else jnp.logical_not(low) for h in heads]
                pair_of = lambda a, h: a[:, (h // 2) * LANES:(h // 2 + 1) * LANES]
                do_hs = [jnp.where(sels[h], pair_of(d_attn, h), 0.0).astype(BF16) for h in heads]
                q_hs = [jnp.where(sels[h], pair_of(q_blks[b], h), jnp.zeros((BLOCK, LANES), BF16)) for h in heads]
                dps = [_dot_nt(do_hs[h], vvar[_head_kv(h)[0]][_head_kv(h)[1]]) for h in heads]
                yield
                deltas = [jnp.sum(probs[h][0] * dps[h], axis=-1, keepdims=True) for h in heads]
                yield
                dss = [probs[h][0] * (dps[h] - deltas[h]) for h in heads]
                dsinks = [-(probs[h][1] * deltas[h]) for h in heads]
                dsbs = [ds.astype(BF16) for ds in dss]
                pbs = [probs[h][0].astype(BF16) for h in heads]
                yield
                dqs = [_dot(dsbs[h], kvar[_head_kv(h)[0]][_head_kv(h)[1]]) for h in heads]
                tks = [_dot_tn(dsbs[h], q_hs[h]) for h in heads]
                tvs = [_dot_tn(pbs[h], do_hs[h]) for h in heads]
                dq_cols = [dqs[2 * i] + dqs[2 * i + 1] for i in range(N_HEADS // 2)]
                dq_ref[rows[b], :] = (jnp.concatenate(dq_cols, axis=1) * Q_SCALE).astype(BF16)
                per_kv = N_HEADS // N_KV
                kv_sum = lambda ts, kvh: sum(ts[kvh * per_kv + 1:(kvh + 1) * per_kv], ts[kvh * per_kv])
                dkk = _fold_kv(kv_sum(tks, 0), kv_sum(tks, 1))
                dvv = _fold_kv(kv_sum(tvs, 0), kv_sum(tvs, 1))
                return jnp.concatenate([dkk, dvv], axis=1), dss, dsinks

            bwd, rms_g = [], []
            for b in range(2):
                attn, probs, kvar, vvar = res[2 * b]
                gm, saved = res[2 * b + 1]
                na_unit, r_a = _rms(attn, 1.0)
                ng_unit, r_g = _rms(gm, 1.0)
                dmix = dmix_ref[rows[b], :].astype(F32)
                dn_a = dmix[:, :ATTN_W]
                dn_g = dmix[:, ATTN_W:]
                rms_g.append((_colsum(dn_a * na_unit), _colsum(dn_g * ng_unit)))
                t_a = dn_a * aog_ref[...]
                d_attn = r_a * t_a - na_unit * (r_a * jnp.mean(t_a * na_unit, axis=-1, keepdims=True))
                t_g = dn_g * gog_ref[...]
                d_gm = r_g * t_g - ng_unit * (r_g * jnp.mean(t_g * ng_unit, axis=-1, keepdims=True))
                bwd.append(attention_bwd(b, d_attn, probs, kvar, vvar))
                bwd.append(gating_bwd(b, d_gm, saved))
            (dkv_a, dss_a, dsk_a), (dmx_a, dws_a, glg_a, glb_a) = _interleave(*bwd[:2])
            (dkv_b, dss_b, dsk_b), (dmx_b, dws_b, glg_b, glb_b) = _interleave(*bwd[2:])

            vec_ref[0:1, :] += rms_g[0][0] + rms_g[1][0]
            vec_ref[1:2, :] += rms_g[0][1] + rms_g[1][1]
            vec_ref[2:3, :] += glg_a + glg_b
            vec_ref[3:4, :] += glb_a + glb_b
            dbs_ref[...] += dmx_a + dmx_b
            for g in range(N_GROUPS):
                dws_ref[g] += dws_a[g] + dws_b[g]
            for h in range(N_HEADS):
                gbias_ref[h] += dss_a[h] + dss_b[h]
                dsink_ref[h] += dsk_a[h] + dsk_b[h]

            dkv_ref[:BLOCK, :] = done[...].astype(BF16)
            dkv_ref[BLOCK:, :] = (carry[...] + dkv_a[:BLOCK]).astype(BF16)
            done[...] = dkv_a[BLOCK:] + dkv_b[:BLOCK]
            carry[...] = dkv_b[BLOCK:]

    last = n_steps - 1
    cur = lambda w: pl.BlockSpec((tile, w), lambda n: (jnp.minimum(n, last), 0))
    late = lambda w: pl.BlockSpec((tile, w), lambda n: (jnp.clip(n - 1, 0, last), 0))
    before = pl.BlockSpec((BLOCK, 2 * KV_W), lambda n: (jnp.clip(2 * n - 1, 0, 2 * last + 1), 0))
    outs = pl.pallas_call(
        body, name="bwd_mix",
        out_shape=[jax.ShapeDtypeStruct((s, ATTN_W), BF16), jax.ShapeDtypeStruct((s, 2 * KV_W), BF16),
                   jax.ShapeDtypeStruct((s, GMLP_W), BF16), jax.ShapeDtypeStruct((s, GMLP_W), BF16),
                   jax.ShapeDtypeStruct((N_HEADS, BLOCK, 2 * BLOCK), F32),
                   jax.ShapeDtypeStruct((N_GROUPS, BLOCK, BLOCK), F32),
                   jax.ShapeDtypeStruct((BLOCK, GMLP_W), F32), jax.ShapeDtypeStruct((8, GMLP_W), F32),
                   jax.ShapeDtypeStruct((N_HEADS, BLOCK, 1), F32)]
        + [jax.ShapeDtypeStruct(_rx_shape(p.shape, k), BF16) for p, k in zip(grad_parts, grad_kinds)],
        grid=(n_steps + 1,),
        in_specs=[cur(ATTN_W), cur(2 * KV_W), before, cur(GMLP_W), cur(GMLP_W), cur(D_MODEL),
                  _const_spec((N_HEADS, BLOCK, 2 * BLOCK)), pl.BlockSpec(memory_space=pltpu.SMEM),
                  _const_spec((1, GMLP_W)), _const_spec((1, GMLP_W)), _const_spec((N_GROUPS, BLOCK, BLOCK)),
                  _const_spec((BLOCK, GMLP_W)), _const_spec((GMLP_W, GMLP_W)), _const_spec((1, ATTN_W)),
                  _const_spec((1, GMLP_W))] + [ANY] * n_g,
        out_specs=[cur(ATTN_W), late(2 * KV_W), cur(GMLP_W), cur(GMLP_W),
                   _const_spec((N_HEADS, BLOCK, 2 * BLOCK)), _const_spec((N_GROUPS, BLOCK, BLOCK)),
                   _const_spec((BLOCK, GMLP_W)), _const_spec((8, GMLP_W)), _const_spec((N_HEADS, BLOCK, 1))]
        + [ANY] * n_g,
        scratch_shapes=[pltpu.VMEM((BLOCK, 2 * KV_W), F32), pltpu.VMEM((BLOCK, 2 * KV_W), F32)]
        + _ChipExchange.sems(n_g),
        compiler_params=_params(("arbitrary",)),
    )(q, kv, kv, gu, gv, dmix, bias, sinks, gln_g, gln_b, wsm, bsx, amat, aog, gog, *grad_parts)
    return outs[:9], outs[9:]


def _mix_finalize(gbias, bucket, dws, dbs, dsink):
    def body(gb_ref, bucket_ref, dws_ref, dbs_ref, dsink_ref, tall_ref):
        bk = bucket_ref[...]
        lane = lax.broadcasted_iota(jnp.int32, (N_BUCKETS, LANES), 1)
        rowi = lax.broadcasted_iota(jnp.int32, (N_BUCKETS, LANES), 0)
        drb = jnp.zeros((N_BUCKETS, LANES), F32)
        dsk = jnp.zeros((8, LANES), F32)
        lane8 = lax.broadcasted_iota(jnp.int32, (8, LANES), 1)
        for h in range(N_HEADS):
            g = gb_ref[h]
            for b in range(N_BUCKETS):
                tot = jnp.sum(_colsum(jnp.where(bk == float(b), g, 0.0)), axis=1, keepdims=True)
                drb = jnp.where((lane == h) & (rowi == b), tot, drb)
            sk = jnp.sum(dsink_ref[h], axis=0, keepdims=True)
            dsk = jnp.where(lane8 == h, sk, dsk)
        tall_ref[TALL_RB:TALL_RB + N_BUCKETS, :] = drb
        tall_ref[TALL_SK:TALL_SK + 8, :] = dsk
        ti = lax.broadcasted_iota(jnp.int32, (BLOCK, BLOCK), 0)
        ui = lax.broadcasted_iota(jnp.int32, (BLOCK, BLOCK), 1)
        for g in range(N_GROUPS):
            tall_ref[g * BLOCK:(g + 1) * BLOCK, :] = jnp.where(ti >= ui, dws_ref[g], 0.0)
        gi = lax.broadcasted_iota(jnp.int32, (GMLP_W, LANES), 0) // GROUP_DIM
        li = lax.broadcasted_iota(jnp.int32, (GMLP_W, LANES), 1)
        ind = jnp.where(gi == li, 1.0, 0.0).astype(BF16)
        d = dbs_ref[...]
        hi = d.astype(BF16)
        r1 = d - hi.astype(F32)
        mid = r1.astype(BF16)
        lo = (r1 - mid.astype(F32)).astype(BF16)
        dbsg = _dot(hi, ind) + _dot(mid, ind) + _dot(lo, ind)
        tall_ref[TALL_BS:TALL_BS + N_GROUPS, :] = dbsg.T[:N_GROUPS, :]

    return pl.pallas_call(
        body, name="mix_finalize", out_shape=jax.ShapeDtypeStruct((TALL_ROWS, LANES), F32), grid=(1,),
        in_specs=[_const_spec((N_HEADS, BLOCK, 2 * BLOCK)), _const_spec((BLOCK, 2 * BLOCK)),
                  _const_spec((N_GROUPS, BLOCK, BLOCK)), _const_spec((BLOCK, GMLP_W)),
                  _const_spec((N_HEADS, BLOCK, 1))],
        out_specs=_const_spec((TALL_ROWS, LANES)),
        compiler_params=_params(("arbitrary",)),
    )(gbias, bucket, dws, dbs, dsink)


def _bwd_in(dq, dkv, dgu, dgv, dxa, x, modr, w_in, tm):
    s = x.shape[0]

    def body(dq_ref, dkv_ref, dgu_ref, dgv_ref, dxa_ref, x_ref, mod_ref, w_ref, gx_ref, acc_ref, db_ref):
        @pl.when(pl.program_id(0) == 0)
        def _():
            acc_ref[...] = jnp.zeros_like(acc_ref)
            db_ref[...] = jnp.zeros_like(db_ref)

        dproj = jnp.concatenate([dq_ref[...], dkv_ref[...], dgu_ref[...], dgv_ref[...]], axis=1)
        dh1 = _dot_nt(dproj, w_ref[...])
        gx_ref[...] = dxa_ref[...] + dh1 * (1.0 + mod_ref[1:2, :])
        acc_ref[0:1, :] += _colsum(dh1 * x_ref[...])
        acc_ref[1:2, :] += _colsum(dh1)
        db_ref[0:1, :] += _colsum(dproj.astype(F32))

    row = lambda w: pl.BlockSpec((tm, w), lambda i: (i, 0))
    return pl.pallas_call(
        body, name="bwd_in",
        out_shape=(jax.ShapeDtypeStruct((s, D_MODEL), F32), jax.ShapeDtypeStruct((8, D_MODEL), F32),
                   jax.ShapeDtypeStruct((8, IN_W), F32)),
        grid=(s // tm,),
        in_specs=[row(ATTN_W), row(2 * KV_W), row(GMLP_W), row(GMLP_W), row(D_MODEL), row(D_MODEL),
                  _const_spec((8, D_MODEL)), _const_spec((D_MODEL, IN_W))],
        out_specs=(row(D_MODEL), _const_spec((8, D_MODEL)), _const_spec((8, IN_W))),
        compiler_params=_params(("arbitrary",)),
    )(dq, dkv, dgu, dgv, dxa, x, modr, w_in)


def _wgrad(a, bs, tm, tk, name, owner_blocks=False, gather_vs=()):
    k_all, m = a.shape
    n = sum(b.shape[1] for b in bs)
    nk = k_all // tk
    nm = m // tm
    n_b = len(bs)
    n_v = len(gather_vs)
    wb = n // N_CHIPS

    def body(a_ref, *rest):
        b_refs, v_refs = rest[:n_b], rest[n_b:n_b + n_v]
        o_ref, ob_ref = rest[n_b + n_v:n_b + n_v + 2]
        vg_refs = rest[n_b + n_v + 2:n_b + 2 * n_v + 2]
        i, k = pl.program_id(0), pl.program_id(1)
        if n_v:
            gather = _Gather8(v_refs, vg_refs, *rest[n_b + 2 * n_v + 2:])

            @pl.when((i == 0) & (k == 0))
            def _():
                gather.start()

            @pl.when((i == nm - 1) & (k == 0))
            def _():
                gather.forward()

        @pl.when(k == 0)
        def _():
            o_ref[...] = jnp.zeros_like(o_ref)

        b = b_refs[0][...] if n_b == 1 else jnp.concatenate([r[...] for r in b_refs], axis=1)
        if owner_blocks:
            av = a_ref[...]
            for j in range(N_CHIPS):
                o_ref[j] += _dot_tn(av, b[:, j * wb:(j + 1) * wb])
        else:
            o_ref[...] += _dot_tn(a_ref[...], b)

        @pl.when(k == nk - 1)
        def _():
            ob_ref[...] = o_ref[...].astype(BF16)

        if n_v:
            @pl.when((i == nm - 1) & (k == nk - 1))
            def _():
                gather.finish()

    if owner_blocks:
        out_spec = pl.BlockSpec((N_CHIPS, tm, wb), lambda i, k: (0, i, 0))
        shape = (N_CHIPS, m, wb)
    else:
        out_spec = pl.BlockSpec((tm, n), lambda i, k: (i, 0))
        shape = (m, n)
    outs = pl.pallas_call(
        body, name=name,
        out_shape=[jax.ShapeDtypeStruct(shape, F32), jax.ShapeDtypeStruct(shape, BF16)] + _gathered8_shapes(gather_vs),
        grid=(nm, nk),
        in_specs=[pl.BlockSpec((tk, tm), lambda i, k: (k, i))]
        + [pl.BlockSpec((tk, b.shape[1]), lambda i, k: (k, 0)) for b in bs] + [ANY] * n_v,
        out_specs=[out_spec, out_spec] + [ANY] * n_v,
        scratch_shapes=_Gather8.sems(n_v) if n_v else [],
        compiler_params=_params(("arbitrary", "arbitrary") if n_v else ("parallel", "arbitrary")),
    )(a, *bs, *gather_vs)
    return outs[0], outs[1], outs[2:]


def _adam_math(w, g, m, v):
    m2 = ADAM_B1 * m + (1.0 - ADAM_B1) * g
    v2 = ADAM_B2 * v + (1.0 - ADAM_B2) * (g * g)
    m_hat = m2 / (1.0 - ADAM_B1 ** ADAM_STEP)
    v_hat = v2 / (1.0 - ADAM_B2 ** ADAM_STEP)
    delta = -ADAM_LR * (m_hat / (jnp.sqrt(v_hat) + ADAM_EPS) + ADAM_WD * w)
    return delta, m2, v2


def _adam_halves(w, mine, got, m, v, tr, name):
    r, cc = w.shape
    h = r // 2
    nt = h // tr

    def body(c_ref, w_ref, mine_ref, got_ref, m_ref, v_ref, g_ref, d_ref, m2_ref, v2_ref):
        g = jnp.where(pl.program_id(0) == c_ref[0], mine_ref[...], got_ref[...])
        g_ref[...] = g
        d, m2, v2 = _adam_math(w_ref[...], g, m_ref[...], v_ref[...])
        d_ref[...] = d
        m2_ref[...] = m2
        v2_ref[...] = v2

    full = pl.BlockSpec((tr, cc), lambda hh, i, c_ref: (hh * nt + i, 0))
    half = pl.BlockSpec((tr, cc), lambda hh, i, c_ref: (i, 0))
    shp = jax.ShapeDtypeStruct((r, cc), F32)
    return pl.pallas_call(
        body, name=name, out_shape=(shp, shp, shp, shp),
        grid_spec=pltpu.PrefetchScalarGridSpec(
            num_scalar_prefetch=1, grid=(2, nt), in_specs=[full, half, half, full, full],
            out_specs=(full, full, full, full)),
        compiler_params=_params(("arbitrary", "arbitrary")),
    )(_core_index_scalar(), w, mine, got, m, v)


def _adam_w_ada(sc_t, dmod_cols, w, m, v, tr):
    r, cc = w.shape

    def body(sct_ref, dm_ref, w_ref, m_ref, v_ref, g_ref, d_ref, m2_ref, v2_ref):
        g = sct_ref[:, 0:1] * dm_ref[0:1, :]
        for k in range(1, N_DEV):
            g = g + sct_ref[:, k:k + 1] * dm_ref[k:k + 1, :]
        g_ref[...] = g
        d, m2, v2 = _adam_math(w_ref[...], g, m_ref[...], v_ref[...])
        d_ref[...] = d
        m2_ref[...] = m2
        v2_ref[...] = v2

    spec = pl.BlockSpec((tr, cc), lambda i: (i, 0))
    shp = jax.ShapeDtypeStruct((r, cc), F32)
    return pl.pallas_call(
        body, name="adam_w_ada", out_shape=(shp, shp, shp, shp), grid=(r // tr,),
        in_specs=[pl.BlockSpec((tr, N_DEV), lambda i: (i, 0)), _const_spec((N_DEV, cc)), spec, spec, spec],
        out_specs=(spec, spec, spec, spec), compiler_params=_params(("parallel",)),
    )(sc_t, dmod_cols, w, m, v)


def _pack_wide(acc_i, acc_m, acc_f, db_in, vec):
    arrs = [acc_i, acc_m, acc_f, db_in, vec]
    i_, m_, f_, b_, v_ = range(5)
    src = {"b_in": (b_, 0), "ln1_g": (m_, 2), "ln1_b": (m_, 3), "ln2_g": (f_, 1), "ln2_b": (f_, 2),
           "gmlp_ln_g": (v_, 2), "gmlp_ln_b": (v_, 3), "attn_out_g": (v_, 0), "gmlp_out_g": (v_, 1), "loss": (f_, 0)}
    dmod = [(i_, 1), (i_, 0), (m_, 4), (m_, 1), (m_, 0), (f_, 3)]

    def body(*refs):
        ins, wide_ref = refs[:5], refs[5]
        wide_ref[...] = jnp.zeros_like(wide_ref)
        for k, (a, row) in enumerate(dmod):
            wide_ref[0:1, k * D_MODEL:(k + 1) * D_MODEL] = ins[a][row:row + 1, :]
        for name, (a, row) in src.items():
            r, off, n = WIDE_LAYOUT[name]
            wide_ref[r:r + 1, off:off + n] = ins[a][row:row + 1, :]

    return pl.pallas_call(
        body, name="pack_wide", out_shape=jax.ShapeDtypeStruct((8, WIDE_W), F32), grid=(1,),
        in_specs=[_const_spec(a.shape) for a in arrs], out_specs=_const_spec((8, WIDE_W)),
        compiler_params=_params(("arbitrary",)),
    )(*arrs)


def _adam_small(gw, gt, wide_wmv, w_s, b_s, rel_bias, sinks):
    names = list(WIDE_PARAMS)
    tall = [("gmlp_w_s", w_s), ("gmlp_b_s", b_s), ("rel_bias", rel_bias), ("attn_sinks", sinks)]
    ins = [gw, gt]
    for n in names:
        ins += list(wide_wmv[n])
    for _, t in tall:
        ins += list(t)
    n_in = len(ins)

    def body(*refs):
        gw_ref, gt_ref = refs[0], refs[1]
        wmv = refs[2:n_in]
        dmod_ref, loss_ref = refs[n_in], refs[n_in + 1]
        outs = refs[n_in + 2:]

        def tall_sum(r0, nr):
            g = gt_ref[r0:r0 + nr, :]
            for d in range(1, N_DEV):
                g = g + gt_ref[d * TALL_ROWS + r0:d * TALL_ROWS + r0 + nr, :]
            return g

        def emit(k, g, w_ref, m_ref, v_ref):
            d, m2, v2 = _adam_math(w_ref[...], g, m_ref[...], v_ref[...])
            outs[4 * k][...] = g
            outs[4 * k + 1][...] = d
            outs[4 * k + 2][...] = m2
            outs[4 * k + 3][...] = v2

        gsum = gw_ref[0:8, :]
        for d in range(1, N_DEV):
            gsum = gsum + gw_ref[8 * d:8 * d + 8, :]
        for d in range(N_DEV):
            dmod_ref[d:d + 1, :] = gw_ref[8 * d:8 * d + 1, :]
        for k, n in enumerate(names):
            r, off, sz = WIDE_LAYOUT[n]
            emit(k, gsum[r:r + 1, off:off + sz], *wmv[3 * k:3 * k + 3])
        r, off, sz = WIDE_LAYOUT["loss"]
        tot = jnp.sum(gsum[r:r + 1, off:off + sz], axis=1, keepdims=True)
        loss_ref[...] = jnp.broadcast_to(tot * (0.5 / D_MODEL), loss_ref.shape)

        k0 = len(names)
        ws_refs = wmv[3 * k0:3 * k0 + 3]
        for g in range(N_GROUPS):
            rows = slice(g * BLOCK, (g + 1) * BLOCK)
            gg = tall_sum(g * BLOCK, BLOCK)
            d, m2, v2 = _adam_math(ws_refs[0][rows, :], gg, ws_refs[1][rows, :], ws_refs[2][rows, :])
            outs[4 * k0][rows, :] = gg
            outs[4 * k0 + 1][rows, :] = d
            outs[4 * k0 + 2][rows, :] = m2
            outs[4 * k0 + 3][rows, :] = v2
        emit(k0 + 1, tall_sum(TALL_BS, N_GROUPS), *wmv[3 * (k0 + 1):3 * (k0 + 1) + 3])
        emit(k0 + 2, tall_sum(TALL_RB, N_BUCKETS)[:, :N_HEADS], *wmv[3 * (k0 + 2):3 * (k0 + 2) + 3])
        emit(k0 + 3, tall_sum(TALL_SK, 8)[0:1, :N_HEADS], *wmv[3 * (k0 + 3):3 * (k0 + 3) + 3])

    out_shapes = [jax.ShapeDtypeStruct((N_DEV, WIDE_W), F32), jax.ShapeDtypeStruct((8, LANES), F32)]
    for n in names:
        out_shapes += [jax.ShapeDtypeStruct(wide_wmv[n][0].shape, F32)] * 4
    for _, t in tall:
        out_shapes += [jax.ShapeDtypeStruct(t[0].shape, F32)] * 4
    res = pl.pallas_call(
        body, name="adam_small", out_shape=out_shapes, grid=(1,),
        in_specs=[_const_spec(a.shape) for a in ins], out_specs=[_const_spec(o.shape) for o in out_shapes],
        compiler_params=_params(("arbitrary",)),
    )(*ins)
    out = {}
    for k, n in enumerate(names + [t[0] for t in tall]):
        out[n] = tuple(res[2 + 4 * k:6 + 4 * k])
    return res[0], res[1], out


def kernel(x, c, rel_bias, w_ada, b_ada, w_in, b_in, attn_sinks, gmlp_ln_g, gmlp_ln_b, gmlp_w_s, gmlp_b_s, attn_out_g, gmlp_out_g, w_out, ln1_g, ln1_b, w_gate_up, w_down, ln2_g, ln2_b, loss_target, m_rel_bias, m_w_ada, m_b_ada, m_w_in, m_b_in, m_attn_sinks, m_gmlp_ln_g, m_gmlp_ln_b, m_gmlp_w_s, m_gmlp_b_s, m_attn_out_g, m_gmlp_out_g, m_w_out, m_ln1_g, m_ln1_b, m_w_gate_up, m_w_down, m_ln2_g, m_ln2_b, v_rel_bias, v_w_ada, v_b_ada, v_w_in, v_b_in, v_attn_sinks, v_gmlp_ln_g, v_gmlp_ln_b, v_gmlp_w_s, v_gmlp_b_s, v_attn_out_g, v_gmlp_out_g, v_w_out, v_ln1_g, v_ln1_b, v_w_gate_up, v_w_down, v_ln2_g, v_ln2_b):
    ix, iy, ic = _my_pos()
    chip = 2 * ix + iy
    dev = 4 * ix + 2 * iy + ic
    s = x.shape[1]
    xs = x[0]
    tgt = loss_target[0]
    tm_big = min(512, s)
    tm_ffn = min(FFN_SUB, s)
    n_ada = w_ada.shape[2]

    w_in_s, w_out_s = w_in[0].astype(BF16), w_out[0].astype(BF16)
    w_gu_s, w_dn_s = w_gate_up[0].astype(BF16), w_down[0].astype(BF16)
    sc_all, _, mod_rows, w_in_g = _prologue(
        jnp.pad(c, ((0, 7), (0, 0))), w_ada[0], lax.dynamic_slice_in_dim(b_ada, chip * n_ada, n_ada, axis=1), w_in_s)
    mod_all = mod_rows.reshape(N_DEV, N_DEV, -1)
    mod_row = lax.dynamic_index_in_dim(mod_all[0::2], dev, axis=1, keepdims=False)
    modr = jnp.pad(mod_row.reshape(6, D_MODEL), ((0, 2), (0, 0)))
    w_in_g = _insert_own(w_in_g, w_in_s, "blk", chip)
    w_in_f = jnp.transpose(w_in_g, (1, 0, 2)).reshape(D_MODEL, IN_W)

    bucket = _bucket_table()
    bias, wsm = _prep_tables(bucket, rel_bias, gmlp_w_s[0])
    bsx = jnp.repeat(gmlp_b_s[0].T, GROUP_DIM, axis=1)
    amat = _group_mean_matrix()
    sinks = attn_sinks[0]

    (h1, q, kv, gu, gv), (w_out_g, w_dn_g) = _fwd_in(xs, modr, w_in_f, b_in, tm_big, [w_out_s, w_dn_s], ["blk", "blk"])
    w_out_f = _insert_own(w_out_g, w_out_s, "blk", chip).reshape(D_MODEL, D_MODEL)
    x1, y, mixed, (w_gu_g,) = _fwd_mix(
        q, kv, gu, gv, xs, modr, bias, sinks, gmlp_ln_g, gmlp_ln_b, wsm, bsx, amat, attn_out_g, gmlp_out_g, w_out_f,
        ln1_g, ln1_b, tm_big, [w_gu_s], ["blk"])
    assert w_gate_up.shape[2] == FF_CHUNK
    w_gu_f = _insert_own(w_gu_g, w_gu_s, "blk", chip)
    w_dn_f = _insert_own(w_dn_g, w_dn_s, "blk", chip).reshape(D_FF, D_MODEL)
    h2, act, dy2, dx1a, acc_f = _fwd_ffn(x1, tgt, modr, ln2_g, ln2_b, w_gu_f, w_dn_f, tm_ffn)

    a_act, dgu_ff, dh2 = _bwd_ffn(dy2, act, w_gu_f, w_dn_f, min(FFN_SUB, s))
    g_dn, g_dn_b, _ = _wgrad(a_act, [dy2], D_FF // 2, min(512, s), "wgrad_down")
    g_gu, g_gu_b, _ = _wgrad(h2, [dgu_ff], 512, min(512, s), "wgrad_gate_up")
    blk3 = lambda a, rows: a.reshape(N_CHIPS, rows, a.shape[1])
    (dxa, dy, dmix, acc_m), (got_dn, got_gu) = _bwd_mid(
        dh2, dx1a, x1, xs, y, modr, ln1_g, w_out_f, tm_big, [blk3(g_dn_b, D_FF // N_CHIPS), g_gu_b], ["blk", "cols"])
    g_out, g_out_b, _ = _wgrad(mixed, [dy], 512, min(512, s), "wgrad_out")
    (got_out,) = _swap_halves([blk3(g_out_b, D_MODEL // N_CHIPS)], ["blk"], "rs_swap_out")
    kinds_a = ["blk", "cols", "blk"]
    fulls_a = [blk3(g_dn, D_FF // N_CHIPS), g_gu, blk3(g_out, D_MODEL // N_CHIPS)]
    gots_a = [got_dn, got_gu, got_out]
    parts_a = [_add_halves(f, g, k, "rs_add_a%d" % i) for i, (f, g, k) in enumerate(zip(fulls_a, gots_a, kinds_a))]
    (dq, dkv, dgu, dgv, gbias, dws, dbs, vec, dsink), rxs_a = _bwd_mix(
        q, kv, gu, gv, dmix, bias, sinks, gmlp_ln_g, gmlp_ln_b, wsm, bsx, amat, attn_out_g, gmlp_out_g,
        [p[1] for p in parts_a], kinds_a)
    tall_g = _mix_finalize(gbias, bucket, dws, dbs, dsink)
    grad_x, acc_i, db_in = _bwd_in(dq, dkv, dgu, dgv, dxa, xs, modr, w_in_f, tm_big)

    wide_g = _pack_wide(acc_i, acc_m, acc_f, db_in, vec)
    full_in, full_in_b, (gw, gt) = _wgrad(h1, [dq, dkv, dgu, dgv], 512, min(512, s), "wgrad_in", owner_blocks=True,
                                          gather_vs=[wide_g, tall_g])
    (got_in,) = _swap_halves([full_in_b], ["blk"], "rs_swap_in")
    part_in = _add_halves(full_in, got_in, "blk", "rs_add_in")
    (rx_in,) = _exchange_chip_partials([part_in[1]], ["blk"], "rs_chips_in")
    wide_wmv = {"b_ada": (b_ada, m_b_ada, v_b_ada), "b_in": (b_in, m_b_in, v_b_in),
                "ln1_g": (ln1_g, m_ln1_g, v_ln1_g), "ln1_b": (ln1_b, m_ln1_b, v_ln1_b),
                "ln2_g": (ln2_g, m_ln2_g, v_ln2_g), "ln2_b": (ln2_b, m_ln2_b, v_ln2_b),
                "gmlp_ln_g": (gmlp_ln_g, m_gmlp_ln_g, v_gmlp_ln_g), "gmlp_ln_b": (gmlp_ln_b, m_gmlp_ln_b, v_gmlp_ln_b),
                "attn_out_g": (attn_out_g, m_attn_out_g, v_attn_out_g),
                "gmlp_out_g": (gmlp_out_g, m_gmlp_out_g, v_gmlp_out_g)}
    rows2 = lambda a: a.reshape(-1, a.shape[-1])
    dmod_all, loss_t, small = _adam_small(
        gw, gt, wide_wmv, tuple(rows2(a) for a in (gmlp_w_s, m_gmlp_w_s, v_gmlp_w_s)),
        tuple(rows2(a) for a in (gmlp_b_s, m_gmlp_b_s, v_gmlp_b_s)), (rel_bias, m_rel_bias, v_rel_bias),
        (attn_sinks, m_attn_sinks, v_attn_sinks))
    loss = loss_t[0, 0]

    dmod_cols = lax.dynamic_slice_in_dim(dmod_all, chip * n_ada, n_ada, axis=1)
    g_ada, d_ada, m_ada, v_ada = _adam_w_ada(sc_all.T, dmod_cols, w_ada[0], m_w_ada[0], v_w_ada[0], 256)

    sums = [(parts_a[0][0], rxs_a[0], 176), (parts_a[1][0], rxs_a[1], 256), (parts_a[2][0], rxs_a[2], 128),
            (part_in[0], rx_in, 256)]
    mine = [_sum_chips(p, rx, tr, "rs_sum_%d" % i) for i, (p, rx, tr) in enumerate(sums)]
    got = _share_halves(mine, "rs_share")

    gs_dn, d_dn, m_dn, v_dn = _adam_halves(w_down[0], mine[0], got[0], m_w_down[0], v_w_down[0], 176, "adam_w_down")
    gs_gu, d_gu, m_gu, v_gu = _adam_halves(w_gate_up[0], mine[1], got[1], m_w_gate_up[0], v_w_gate_up[0], 256,
                                           "adam_w_gate_up")
    gs_out, d_out, m_out, v_out = _adam_halves(w_out[0], mine[2], got[2], m_w_out[0], v_w_out[0], 128, "adam_w_out")
    gs_in, d_in, m_in, v_in = _adam_halves(w_in[0], mine[3], got[3], m_w_in[0], v_w_in[0], 256, "adam_w_in")

    big = {"w_ada": (g_ada, d_ada, m_ada, v_ada), "w_in": (gs_in, d_in, m_in, v_in), "w_out": (gs_out, d_out, m_out, v_out),
           "w_gate_up": (gs_gu, d_gu, m_gu, v_gu), "w_down": (gs_dn, d_dn, m_dn, v_dn)}
    order = ["rel_bias", "w_ada", "b_ada", "w_in", "b_in", "attn_sinks", "gmlp_ln_g", "gmlp_ln_b", "gmlp_w_s", "gmlp_b_s",
             "attn_out_g", "gmlp_out_g", "w_out", "ln1_g", "ln1_b", "w_gate_up", "w_down", "ln2_g", "ln2_b"]
    shapes = {"gmlp_w_s": gmlp_w_s.shape, "gmlp_b_s": gmlp_b_s.shape}
    outs = [loss, grad_x[None]]
    for k in range(4):
        for name in order:
            if name in big:
                outs.append(big[name][k][None])
            elif name in shapes:
                outs.append(small[name][k].reshape(shapes[name]))
            else:
                outs.append(small[name][k])
    return tuple(outs)
```

```python
import math

import numpy as np
import jax
import jax.numpy as jnp
from jax import lax
from jax.experimental import pallas as pl
from jax.experimental.pallas import tpu as pltpu

F32 = jnp.float32
BF16 = jnp.bfloat16
MESH = pl.DeviceIdType.MESH

D_MODEL = 1024
N_HEADS = 8
N_KV = 2
HEAD_DIM = 64
ATTN_W = N_HEADS * HEAD_DIM
KV_W = N_KV * HEAD_DIM
N_GROUPS = 8
GROUP_DIM = 64
GMLP_W = N_GROUPS * GROUP_DIM
IN_W = ATTN_W + 2 * KV_W + 2 * GMLP_W
BLOCK = 128
N_BUCKETS = 32
MAX_DISTANCE = 128
D_FF = 2816
ALPHA = 2.0 ** 0.25
LN_EPS = 1e-5
NEG_INF = -1e30
ADAM_LR, ADAM_B1, ADAM_B2, ADAM_EPS, ADAM_WD, ADAM_STEP = 0.001, 0.9, 0.999, 1e-8, 0.01, 10
N_CHIPS = 4
N_DEV = 8
LANES = 128
V7X_VMEM_LIMIT = 56 * 2 ** 20
GELU_C = math.sqrt(2.0 / math.pi)
Q_SCALE = HEAD_DIM ** -0.5
ANY = pl.BlockSpec(memory_space=pl.ANY)

TALL_BS = N_GROUPS * BLOCK
TALL_RB = TALL_BS + 8
TALL_SK = TALL_RB + N_BUCKETS
TALL_ROWS = TALL_SK + 8
WIDE_W = 6 * D_MODEL
WIDE_LAYOUT = {
    "b_ada": (0, 0, 6 * D_MODEL),
    "b_in": (1, 0, IN_W), "ln1_g": (1, IN_W, D_MODEL), "ln1_b": (1, IN_W + D_MODEL, D_MODEL),
    "ln2_g": (1, IN_W + 2 * D_MODEL, D_MODEL), "ln2_b": (1, IN_W + 3 * D_MODEL, D_MODEL),
    "gmlp_ln_g": (2, 0, GMLP_W), "gmlp_ln_b": (2, GMLP_W, GMLP_W), "attn_out_g": (2, 2 * GMLP_W, ATTN_W),
    "gmlp_out_g": (2, 2 * GMLP_W + ATTN_W, GMLP_W), "loss": (2, 3 * GMLP_W + ATTN_W, D_MODEL)}
WIDE_PARAMS = tuple(n for n in WIDE_LAYOUT if n != "loss")


def _params(sem=None):
    return pltpu.CompilerParams(dimension_semantics=sem, vmem_limit_bytes=V7X_VMEM_LIMIT)


def _const_spec(shape, single=False):
    nd = len(shape)
    if single:
        return pl.BlockSpec(shape, lambda *_: (0,) * nd, pipeline_mode=pl.Buffered(1))
    return pl.BlockSpec(shape, lambda *_: (0,) * nd)


def _dot(a, b):
    return jnp.dot(a, b, preferred_element_type=F32)


def _dot_nt(a, b):
    return lax.dot_general(a, b, (((1,), (1,)), ((), ())), preferred_element_type=F32)


def _dot_tn(a, b):
    return lax.dot_general(a, b, (((0,), (0,)), ((), ())), preferred_element_type=F32)


def _gelu(x):
    t = jnp.tanh(GELU_C * (x + 0.044715 * x * x * x))
    return 0.5 * x * (1.0 + t), t


def _gelu_grad(x, t):
    return 0.5 * (1.0 + t) + 0.5 * x * (1.0 - t * t) * GELU_C * (1.0 + 3.0 * 0.044715 * x * x)


def _split_dot(x, a):
    hi = x.astype(BF16)
    lo = (x - hi.astype(F32)).astype(BF16)
    return _dot(hi, a) + _dot(lo, a)


def _group_mean_matrix():
    g = np.arange(GMLP_W) // GROUP_DIM
    return jnp.asarray((g[:, None] == g[None, :]).astype(np.float32) / GROUP_DIM, dtype=BF16)


def _ln_stats(z):
    mu = jnp.mean(z, axis=-1, keepdims=True)
    d = z - mu
    var = jnp.mean(d * d, axis=-1, keepdims=True)
    rstd = lax.rsqrt(var + LN_EPS)
    return d * rstd, rstd


def _ln_bwd(dxhat, xhat, rstd):
    m1 = jnp.mean(dxhat, axis=-1, keepdims=True)
    m2 = jnp.mean(dxhat * xhat, axis=-1, keepdims=True)
    return rstd * (dxhat - m1 - xhat * m2)


def _colsum(x):
    return jnp.sum(x, axis=0, keepdims=True)


def _my_pos():
    return lax.axis_index("x"), lax.axis_index("y"), lax.axis_index("c")


def _other_chips(x, y):
    return [(1 - x, y), (x, 1 - y), (1 - x, 1 - y)]


def _chip_index_scalar():
    ix, iy, _ = _my_pos()
    return jnp.reshape(2 * ix + iy, (1,)).astype(jnp.int32)


def _core_index_scalar():
    return jnp.reshape(lax.axis_index("c"), (1,)).astype(jnp.int32)


class _Gather8:
    def __init__(self, x_refs, out_refs, send_sems, recv_sems, local_sems):
        self.x_refs, self.out_refs = x_refs, out_refs
        self.send_sems, self.recv_sems, self.local_sems = send_sems, recv_sems, local_sems
        self.x, self.y, self.c = _my_pos()
        self.me, self.sibling = (self.x, self.y, self.c), (self.x, self.y, 1 - self.c)
        self.chips = _other_chips(self.x, self.y)

    def _rows(self, a, px, py, pc):
        m_per = self.x_refs[a].shape[0]
        return self.out_refs[a].at[pl.ds((4 * px + 2 * py + pc) * m_per, m_per), :]

    def _copy(self, a, k, block, to, src=None):
        return pltpu.make_async_remote_copy(
            src_ref=self._rows(a, *block) if src is None else src, dst_ref=self._rows(a, *block),
            send_sem=self.send_sems.at[7 * a + k], recv_sem=self.recv_sems.at[7 * a + k], device_id=to,
            device_id_type=MESH)

    def _local(self, a):
        return pltpu.make_async_copy(self.x_refs[a], self._rows(a, *self.me), self.local_sems.at[a])

    def start(self):
        for a in range(len(self.x_refs)):
            self._local(a).start()
            self._copy(a, 0, self.me, self.sibling, src=self.x_refs[a]).start()
            for j, chip in enumerate(self.chips):
                self._copy(a, 1 + j, self.me, (*chip, self.c), src=self.x_refs[a]).start()

    def forward(self):
        for a in range(len(self.x_refs)):
            for j, chip in enumerate(self.chips):
                self._copy(a, 1 + j, (*chip, self.c), self.me).wait_recv()
                self._copy(a, 4 + j, (*chip, self.c), self.sibling).start()

    def finish(self):
        for a in range(len(self.x_refs)):
            self._copy(a, 0, self.sibling, self.me).wait_recv()
            for j, chip in enumerate(self.chips):
                self._copy(a, 4 + j, (*chip, 1 - self.c), self.me).wait_recv()
        for a in range(len(self.x_refs)):
            for k in range(7):
                self._copy(a, k, self.me, self.me).wait_send()
            self._local(a).wait()

    @staticmethod
    def sems(n_v):
        return [pltpu.SemaphoreType.DMA((7 * n_v,)), pltpu.SemaphoreType.DMA((7 * n_v,)),
                pltpu.SemaphoreType.DMA((n_v,))]


def _gathered8_shapes(vs):
    return [jax.ShapeDtypeStruct((N_DEV * v.shape[0], v.shape[1]), v.dtype) for v in vs]


VMEM_WHOLE = pl.BlockSpec(memory_space=pltpu.VMEM)


def _prologue(c_pad, w_ada_s, b_ada_s, shards):
    n = w_ada_s.shape[1]
    n_w = len(shards)

    def body(c_ref, w_ref, b_ref, *rest):
        shard_refs = rest[:n_w]
        sc_ref, modc_ref, modg_ref = rest[n_w:n_w + 3]
        gathered_refs = rest[n_w + 3:2 * n_w + 3]
        call_ref = rest[2 * n_w + 3]
        sems = rest[2 * n_w + 4:]
        weights = _WeightGather(shard_refs, gathered_refs, ["blk"] * n_w, sems[0], sems[1])
        gather_c = _Gather8([c_ref], [call_ref], sems[2], sems[3], sems[4])
        gather_mod = _Gather8([modc_ref], [modg_ref], sems[5], sems[6], sems[7])
        weights.start()
        gather_c.start()
        gather_c.forward()
        gather_c.finish()
        cv = call_ref[...]
        sc = cv * _sigmoid(cv)
        a_hi = sc.astype(BF16)
        a_lo = (sc - a_hi.astype(F32)).astype(BF16)
        w = w_ref[...]
        w_hi = w.astype(BF16)
        w_lo = (w - w_hi.astype(F32)).astype(BF16)
        mod = _dot(a_hi, w_hi) + _dot(a_hi, w_lo) + _dot(a_lo, w_hi) + b_ref[...]
        for d in range(N_DEV):
            sc_ref[d:d + 1, :] = sc[8 * d:8 * d + 1, :]
            modc_ref[d:d + 1, :] = mod[8 * d:8 * d + 1, :]
        gather_mod.start()
        weights.forward()
        gather_mod.forward()
        gather_mod.finish()
        weights.forward_diagonal()
        weights.finish()

    outs = pl.pallas_call(
        body, name="prologue",
        out_shape=[jax.ShapeDtypeStruct((N_DEV, D_MODEL), F32), jax.ShapeDtypeStruct((N_DEV, n), F32),
                   jax.ShapeDtypeStruct((N_DEV * N_DEV, n), F32)]
        + [jax.ShapeDtypeStruct(_gathered_shape(sh, "blk"), BF16) for sh in shards],
        in_specs=[VMEM_WHOLE, VMEM_WHOLE, VMEM_WHOLE] + [ANY] * n_w,
        out_specs=[VMEM_WHOLE, VMEM_WHOLE, VMEM_WHOLE] + [ANY] * n_w,
        scratch_shapes=[pltpu.VMEM((N_DEV * 8, D_MODEL), F32)] + _WeightGather.sems(n_w) + _Gather8.sems(1)
        + _Gather8.sems(1),
        compiler_params=pltpu.CompilerParams(vmem_limit_bytes=V7X_VMEM_LIMIT),
    )(c_pad, w_ada_s, b_ada_s, *shards)
    return outs[0], outs[2], outs[3:]


def _gathered_shape(shard, kind):
    r, cc = shard.shape
    return (N_CHIPS, r, cc) if kind == "blk" else (r, N_CHIPS * cc)


class _WeightGather:
    N_SEM = 8

    def __init__(self, shards, gathered, kinds, send_sems, recv_sems):
        self.shards, self.gathered, self.kinds = shards, gathered, kinds
        self.send_sems, self.recv_sems = send_sems, recv_sems
        self.x, self.y, self.c = _my_pos()
        self.me, self.sibling = (self.x, self.y, self.c), (self.x, self.y, 1 - self.c)
        self.nbr = ((1 - self.x, self.y), (self.x, 1 - self.y))
        self.diag = 2 * (1 - self.x) + (1 - self.y)

    def _dst(self, a, chip, pc, quarter=None):
        r, cc = self.shards[a].shape
        h = r // 2
        row0, rows = pc * h, h
        if quarter is not None:
            row0, rows = pc * h + quarter * (h // 2), h // 2
        g = self.gathered[a]
        if self.kinds[a] == "blk":
            return g.at[chip, pl.ds(row0, rows), :]
        return g.at[pl.ds(row0, rows), pl.ds(chip * cc, cc)]

    def _copy(self, a, k, region, to, src=None):
        return pltpu.make_async_remote_copy(
            src_ref=region if src is None else src, dst_ref=region, send_sem=self.send_sems.at[a * self.N_SEM + k],
            recv_sem=self.recv_sems.at[a * self.N_SEM + k], device_id=to, device_id_type=MESH)

    def _arrays(self):
        return range(len(self.shards))

    def start(self):
        my_chip = 2 * self.x + self.y
        for a in self._arrays():
            h = self.shards[a].shape[0] // 2
            mine = self.shards[a].at[pl.ds(self.c * h, h), :]
            for j, chip in enumerate(self.nbr):
                self._copy(a, j, self._dst(a, my_chip, self.c), (*chip, self.c), src=mine).start()

    def forward(self):
        for a in self._arrays():
            for j, chip in enumerate(self.nbr):
                cj = 2 * chip[0] + chip[1]
                half = self._dst(a, cj, self.c)
                self._copy(a, j, half, self.me).wait_recv()
                self._copy(a, 2 + j, half, self.sibling).start()
                other = self.nbr[1 - j]
                self._copy(a, 4 + j, self._dst(a, cj, self.c, quarter=j), (*other, self.c)).start()

    def forward_diagonal(self):
        for a in self._arrays():
            for j in range(2):
                quarter = self._dst(a, self.diag, self.c, quarter=j)
                self._copy(a, 4 + j, quarter, self.me).wait_recv()
                self._copy(a, 6 + j, quarter, self.sibling).start()

    def finish(self):
        for a in self._arrays():
            for j, chip in enumerate(self.nbr):
                self._copy(a, 2 + j, self._dst(a, 2 * chip[0] + chip[1], 1 - self.c), self.me).wait_recv()
                self._copy(a, 6 + j, self._dst(a, self.diag, 1 - self.c, quarter=j), self.me).wait_recv()
        for a in self._arrays():
            half = self._dst(a, self.diag, self.c)
            quarter = self._dst(a, self.diag, self.c, quarter=0)
            for k in range(self.N_SEM):
                self._copy(a, k, half if k < 4 else quarter, self.me).wait_send()

    @classmethod
    def sems(cls, n_arr):
        return [pltpu.SemaphoreType.DMA((n_arr * cls.N_SEM,)), pltpu.SemaphoreType.DMA((n_arr * cls.N_SEM,))]


def _insert_own(gathered, shard, kind, chip):
    if kind == "blk":
        return lax.dynamic_update_slice(gathered, shard[None], (chip, 0, 0))
    return lax.dynamic_update_slice(gathered, shard, (0, chip * shard.shape[1]))


def _half_of_full(ref, kind, pc):
    if kind == "blk":
        h = ref.shape[1] // 2
        return ref.at[:, pl.ds(pc * h, h), :]
    h = ref.shape[0] // 2
    return ref.at[pl.ds(pc * h, h), :]


def _half_shape(shape, kind):
    return (shape[0], shape[1] // 2, shape[2]) if kind == "blk" else (shape[0] // 2, shape[1])


class _HalfSwap:
    def __init__(self, ins, outs, kinds, send_sems, recv_sems):
        self.ins, self.outs, self.kinds = ins, outs, kinds
        self.send_sems, self.recv_sems = send_sems, recv_sems
        self.x, self.y, self.c = _my_pos()

    def _copies(self):
        for a in range(len(self.ins)):
            yield pltpu.make_async_remote_copy(
                src_ref=_half_of_full(self.ins[a], self.kinds[a], 1 - self.c), dst_ref=self.outs[a],
                send_sem=self.send_sems.at[a], recv_sem=self.recv_sems.at[a],
                device_id=(self.x, self.y, 1 - self.c), device_id_type=MESH)

    def start(self):
        for cp in self._copies():
            cp.start()

    def wait(self):
        for cp in self._copies():
            cp.wait()

    @staticmethod
    def sems(n_arr):
        return [pltpu.SemaphoreType.DMA((n_arr,)), pltpu.SemaphoreType.DMA((n_arr,))]

    @staticmethod
    def out_shapes(fulls, kinds):
        return [jax.ShapeDtypeStruct(_half_shape(a.shape, k), a.dtype) for a, k in zip(fulls, kinds)]


def _swap_halves(fulls_bf16, kinds, name):
    n_arr = len(fulls_bf16)

    def body(*refs):
        swap = _HalfSwap(refs[:n_arr], refs[n_arr:2 * n_arr], kinds, *refs[2 * n_arr:])
        swap.start()
        swap.wait()

    return pl.pallas_call(
        body, name=name, out_shape=_HalfSwap.out_shapes(fulls_bf16, kinds),
        in_specs=[ANY] * n_arr, out_specs=[ANY] * n_arr, scratch_shapes=_HalfSwap.sems(n_arr),
    )(*fulls_bf16)


def _add_halves(full, got, kind, name):
    hs = _half_shape(full.shape, kind)

    def body(pos_ref, a_ref, b_ref, o_ref, ob_ref):
        p = a_ref[...] + b_ref[...].astype(F32)
        ob_ref[...] = p.astype(BF16)

        @pl.when(pl.program_id(0) == pos_ref[1])
        def _():
            o_ref[...] = p.reshape(o_ref.shape)

    if kind == "blk":
        nb, h, cc = hs
        own = pl.BlockSpec((1, h, cc), lambda b, pos_ref: (b, pos_ref[0], 0))
        other = pl.BlockSpec((1, h, cc), lambda b, pos_ref: (b, 0, 0))
    else:
        h, cc = hs[0], hs[1] // N_CHIPS
        own = pl.BlockSpec((h, cc), lambda b, pos_ref: (pos_ref[0], b))
        other = pl.BlockSpec((h, cc), lambda b, pos_ref: (0, b))
    pos = jnp.concatenate([_core_index_scalar(), _chip_index_scalar()])
    return pl.pallas_call(
        body, name=name, out_shape=(jax.ShapeDtypeStruct((h, cc), F32), jax.ShapeDtypeStruct(hs, BF16)),
        grid_spec=pltpu.PrefetchScalarGridSpec(
            num_scalar_prefetch=1, grid=(N_CHIPS,), in_specs=[own, other],
            out_specs=(pl.BlockSpec((h, cc), lambda b, pos_ref: (0, 0)), other)),
        compiler_params=_params(("arbitrary",)),
    )(pos, full, got)


def _rx_shape(part_shape, kind):
    if kind == "blk":
        return (3, part_shape[1], part_shape[2])
    return (3, part_shape[0], part_shape[1] // N_CHIPS)


class _ChipExchange:
    def __init__(self, parts, rxs, kinds, send_sems, recv_sems):
        self.parts, self.rxs, self.kinds = parts, rxs, kinds
        self.send_sems, self.recv_sems = send_sems, recv_sems
        self.x, self.y, self.c = _my_pos()
        self.chips = _other_chips(self.x, self.y)

    def _copies(self):
        for a in range(len(self.parts)):
            for j, chip in enumerate(self.chips):
                cj = 2 * chip[0] + chip[1]
                if self.kinds[a] == "blk":
                    src = self.parts[a].at[cj]
                else:
                    cc = self.parts[a].shape[1] // N_CHIPS
                    src = self.parts[a].at[:, pl.ds(cj * cc, cc)]
                yield pltpu.make_async_remote_copy(
                    src_ref=src, dst_ref=self.rxs[a].at[j], send_sem=self.send_sems.at[a * 3 + j],
                    recv_sem=self.recv_sems.at[a * 3 + j], device_id=(*chip, self.c), device_id_type=MESH)

    def start(self):
        for cp in self._copies():
            cp.start()

    def wait(self):
        for cp in self._copies():
            cp.wait_recv()
        for cp in self._copies():
            cp.wait_send()

    @staticmethod
    def sems(n_arr):
        return [pltpu.SemaphoreType.DMA((n_arr * 3,)), pltpu.SemaphoreType.DMA((n_arr * 3,))]


def _exchange_chip_partials(parts, kinds, name):
    n_arr = len(parts)

    def body(*refs):
        exchange = _ChipExchange(refs[:n_arr], refs[n_arr:2 * n_arr], kinds, *refs[2 * n_arr:])
        exchange.start()
        exchange.wait()

    return pl.pallas_call(
        body, name=name,
        out_shape=[jax.ShapeDtypeStruct(_rx_shape(p.shape, k), BF16) for p, k in zip(parts, kinds)],
        in_specs=[ANY] * n_arr, out_specs=[ANY] * n_arr, scratch_shapes=_ChipExchange.sems(n_arr),
    )(*parts)


def _sum_chips(part, rx, tr, name):
    _, h, cc = rx.shape
    flips = (2, 1, 3)

    def body(chip_ref, p_ref, rx_ref, o_ref):
        own = p_ref[...]
        for mc in range(N_CHIPS):
            @pl.when(chip_ref[0] == mc)
            def _():
                terms = sorted([(mc, None)] + [(mc ^ f, j) for j, f in enumerate(flips)])
                acc = None
                for _, j in terms:
                    t = own if j is None else rx_ref[j].astype(F32)
                    acc = t if acc is None else acc + t
                o_ref[...] = acc

    return pl.pallas_call(
        body, name=name, out_shape=jax.ShapeDtypeStruct((h, cc), F32),
        grid_spec=pltpu.PrefetchScalarGridSpec(
            num_scalar_prefetch=1, grid=(h // tr,),
            in_specs=[pl.BlockSpec((tr, cc), lambda i, chip_ref: (i, 0)),
                      pl.BlockSpec((3, tr, cc), lambda i, chip_ref: (0, i, 0))],
            out_specs=pl.BlockSpec((tr, cc), lambda i, chip_ref: (i, 0))),
        compiler_params=_params(("arbitrary",)),
    )(_chip_index_scalar(), part, rx)


def _share_halves(halves, name):
    n_arr = len(halves)

    def body(*refs):
        ins, outs = refs[:n_arr], refs[n_arr:2 * n_arr]
        send_sems, recv_sems = refs[2 * n_arr:]
        x, y, c = _my_pos()
        cps = []
        for a in range(n_arr):
            cp = pltpu.make_async_remote_copy(
                src_ref=ins[a], dst_ref=outs[a], send_sem=send_sems.at[a], recv_sem=recv_sems.at[a],
                device_id=(x, y, 1 - c), device_id_type=MESH)
            cp.start()
            cps.append(cp)
        for cp in cps:
            cp.wait()

    return pl.pallas_call(
        body, name=name, out_shape=[jax.ShapeDtypeStruct(h.shape, h.dtype) for h in halves],
        in_specs=[ANY] * n_arr, out_specs=[ANY] * n_arr,
        scratch_shapes=[pltpu.SemaphoreType.DMA((n_arr,)), pltpu.SemaphoreType.DMA((n_arr,))],
    )(*halves)


def _bucket_table():
    qi = jnp.arange(BLOCK)[:, None]
    si = jnp.arange(2 * BLOCK)[None, :]
    dist = qi + BLOCK - si
    max_exact = N_BUCKETS // 2
    n = jnp.maximum(dist, 0)
    nf = jnp.maximum(n, max_exact).astype(F32)
    large = max_exact + (jnp.log(nf / max_exact) / math.log(MAX_DISTANCE / max_exact)
                         * (N_BUCKETS - max_exact)).astype(jnp.int32)
    large = jnp.minimum(large, N_BUCKETS - 1)
    return jnp.where(n < max_exact, n, large).astype(F32)


def _prep_tables(bucket, rel_bias, w_s):
    def body(bucket_ref, rb_ref, ws_ref, bias_ref, wsm_ref):
        qi = lax.broadcasted_iota(jnp.int32, (BLOCK, 2 * BLOCK), 0)
        si = lax.broadcasted_iota(jnp.int32, (BLOCK, 2 * BLOCK), 1)
        dist = qi + BLOCK - si
        in_window = (dist >= 0) & (dist < BLOCK)
        bk = bucket_ref[...]
        for h in range(N_HEADS):
            acc = jnp.zeros((BLOCK, 2 * BLOCK), F32)
            for b in range(N_BUCKETS):
                acc = jnp.where(bk == float(b), rb_ref[b, h], acc)
            bias_ref[h] = jnp.where(in_window, acc, NEG_INF)
        ti = lax.broadcasted_iota(jnp.int32, (BLOCK, BLOCK), 0)
        ui = lax.broadcasted_iota(jnp.int32, (BLOCK, BLOCK), 1)
        for g in range(N_GROUPS):
            wsm_ref[g] = jnp.where(ti >= ui, ws_ref[g], 0.0).astype(BF16)

    return pl.pallas_call(
        body, name="prep_tables",
        out_shape=(jax.ShapeDtypeStruct((N_HEADS, BLOCK, 2 * BLOCK), F32),
                   jax.ShapeDtypeStruct((N_GROUPS, BLOCK, BLOCK), BF16)),
        grid=(1,),
        in_specs=[_const_spec((BLOCK, 2 * BLOCK)), pl.BlockSpec(memory_space=pltpu.SMEM),
                  _const_spec((N_GROUPS, BLOCK, BLOCK))],
        out_specs=(_const_spec((N_HEADS, BLOCK, 2 * BLOCK)), _const_spec((N_GROUPS, BLOCK, BLOCK))),
        compiler_params=_params(("arbitrary",)),
    )(bucket, rel_bias, w_s)


def _fwd_in(x, modr, w_in, b_in, tm, shards, kinds):
    s = x.shape[0]
    n_steps = s // tm
    fwd_step, diag_step = (8 * n_steps) // 16, (13 * n_steps) // 16
    n_w = len(shards)

    def body(x_ref, mod_ref, w_ref, b_ref, *rest):
        shard_refs = rest[:n_w]
        h1_ref, q_ref, kv_ref, gu_ref, gv_ref = rest[n_w:n_w + 5]
        gathered_refs = rest[n_w + 5:2 * n_w + 5]
        send_sems, recv_sems = rest[2 * n_w + 5:]
        i = pl.program_id(0)
        gather = _WeightGather(shard_refs, gathered_refs, kinds, send_sems, recv_sems)

        @pl.when(i == 0)
        def _():
            gather.start()

        h1 = (x_ref[...] * (1.0 + mod_ref[1:2, :]) + mod_ref[0:1, :]).astype(BF16)
        h1_ref[...] = h1
        proj = _dot(h1, w_ref[...]) + b_ref[...]
        q_ref[...] = (proj[:, :ATTN_W] * Q_SCALE).astype(BF16)
        kv_ref[...] = proj[:, ATTN_W:ATTN_W + 2 * KV_W].astype(BF16)
        gu_ref[...] = proj[:, ATTN_W + 2 * KV_W:ATTN_W + 2 * KV_W + GMLP_W]
        gv_ref[...] = proj[:, ATTN_W + 2 * KV_W + GMLP_W:]

        @pl.when(i == fwd_step)
        def _():
            gather.forward()

        @pl.when(i == diag_step)
        def _():
            gather.forward_diagonal()

        @pl.when(i == n_steps - 1)
        def _():
            gather.finish()

    row = lambda w: pl.BlockSpec((tm, w), lambda i: (i, 0))
    outs = pl.pallas_call(
        body, name="fwd_in",
        out_shape=[jax.ShapeDtypeStruct((s, D_MODEL), BF16), jax.ShapeDtypeStruct((s, ATTN_W), BF16),
                   jax.ShapeDtypeStruct((s, 2 * KV_W), BF16), jax.ShapeDtypeStruct((s, GMLP_W), F32),
                   jax.ShapeDtypeStruct((s, GMLP_W), F32)]
        + [jax.ShapeDtypeStruct(_gathered_shape(sh, k), BF16) for sh, k in zip(shards, kinds)],
        grid=(n_steps,),
        in_specs=[row(D_MODEL), _const_spec((8, D_MODEL)), _const_spec((D_MODEL, IN_W)), _const_spec((1, IN_W))]
        + [ANY] * n_w,
        out_specs=[row(D_MODEL), row(ATTN_W), row(2 * KV_W), row(GMLP_W), row(GMLP_W)] + [ANY] * n_w,
        scratch_shapes=_WeightGather.sems(n_w),
        compiler_params=_params(("arbitrary",)),
    )(x, modr, w_in, b_in, *shards)
    return outs[:5], outs[5:]


def _kv_variants(kk):
    kf = kk.astype(F32)
    lane = lax.broadcasted_iota(jnp.int32, kf.shape, 1)
    low = lane < HEAD_DIM
    k0_lo = jnp.where(low, kf, 0.0)
    k1_hi = jnp.where(low, 0.0, kf)
    k0_hi = pltpu.roll(k0_lo, HEAD_DIM, 1)
    k1_lo = pltpu.roll(k1_hi, HEAD_DIM, 1)
    return ((k0_lo.astype(BF16), k0_hi.astype(BF16)), (k1_lo.astype(BF16), k1_hi.astype(BF16)))


def _head_kv(h):
    return h // (N_HEADS // N_KV), h % 2


MIX_GROUP = 2


def _interleave(*gens):
    results = [None] * len(gens)
    active = list(enumerate(gens))
    while active:
        still = []
        for i, g in active:
            try:
                next(g)
                still.append((i, g))
            except StopIteration as done:
                results[i] = done.value
        active = still
    return results


def _attn_block_fwd(q_blk, kk, vv, bias_ref, sinks_ref, first_mask):
    kvar = _kv_variants(kk)
    vvar = _kv_variants(vv)
    heads = range(N_HEADS)
    q_pairs = [q_blk[:, (h // 2) * LANES:(h // 2 + 1) * LANES] for h in heads]
    logits = [_dot_nt(q_pairs[h], kvar[_head_kv(h)[0]][_head_kv(h)[1]]) + bias_ref[h] for h in heads]
    if first_mask is not None:
        logits = [jnp.where(first_mask, NEG_INF, lg) for lg in logits]
    yield
    ms = [jnp.maximum(jnp.max(logits[h], axis=-1, keepdims=True), sinks_ref[h]) for h in heads]
    yield
    es = [jnp.exp(logits[h] - ms[h]) for h in heads]
    ess = [jnp.exp(sinks_ref[h] - ms[h]) for h in heads]
    yield
    invs = [1.0 / (jnp.sum(es[h], axis=-1, keepdims=True) + ess[h]) for h in heads]
    probs = [(es[h] * invs[h], ess[h] * invs[h]) for h in heads]
    yield
    outs = [_dot(probs[h][0].astype(BF16), vvar[_head_kv(h)[0]][_head_kv(h)[1]]) for h in heads]
    pairs = [outs[2 * i] + outs[2 * i + 1] for i in range(N_HEADS // 2)]
    return jnp.concatenate(pairs, axis=1), probs, kvar, vvar


def _gmlp_chunk_fwd(gu, gv, ln_g, ln_b, wsm_ref, bsx, amat):
    u, tu = _gelu(gu)
    a, ta = _gelu(gv)
    yield
    mean = _split_dot(a, amat)
    d = a - mean
    yield
    var = _split_dot(d * d, amat)
    yield
    rstd = lax.rsqrt(var + LN_EPS)
    xhat = d * rstd
    vb = (xhat * ln_g + ln_b).astype(BF16)
    yield
    lane = lax.broadcasted_iota(jnp.int32, (BLOCK, LANES), 1)
    low = lane < GROUP_DIM
    cols = []
    for pair in range(N_GROUPS // 2):
        vp = vb[:, pair * LANES:(pair + 1) * LANES]
        cols.append(jnp.where(low, _dot(wsm_ref[2 * pair], vp), _dot(wsm_ref[2 * pair + 1], vp)))
    mixedv = jnp.concatenate(cols, axis=1) + bsx
    return u * mixedv, (u, tu, ta, xhat, rstd, vb, mixedv)


def _rms(a, g):
    r = lax.rsqrt(jnp.mean(a * a, axis=-1, keepdims=True) + LN_EPS)
    return a * r * g, r


def _fwd_mix(q, kv, gu, gv, x, modr, bias, sinks, gln_g, gln_b, wsm, bsx, amat, aog, gog, w_out, ln1_g, ln1_b, tm,
             ffn_shards, ffn_kinds):
    s = x.shape[0]
    nb = tm // BLOCK
    n_steps = s // tm
    fwd_step, diag_step = (7 * n_steps) // 16, (12 * n_steps) // 16
    n_w = len(ffn_shards)

    def body(q_ref, kv_ref, kvp_ref, gu_ref, gv_ref, x_ref, mod_ref, bias_ref, sinks_ref, glng_ref, glnb_ref, wsm_ref,
             bsx_ref, amat_ref, aog_ref, gog_ref, wout_ref, ln1g_ref, ln1b_ref, *rest):
        shard_refs = rest[:n_w]
        x1_ref, x1b_ref, y_ref, mixed_ref = rest[n_w:n_w + 4]
        gathered_refs = rest[n_w + 4:2 * n_w + 4]
        mix_scr, send_sems, recv_sems = rest[2 * n_w + 4:]
        i = pl.program_id(0)
        gather = _WeightGather(shard_refs, gathered_refs, ffn_kinds, send_sems, recv_sems)

        @pl.when(i == 0)
        def _():
            gather.start()

        col = lax.broadcasted_iota(jnp.int32, (BLOCK, 2 * BLOCK), 1)
        for b0 in range(0, nb, MIX_GROUP):
            gens = []
            for b in range(b0, min(b0 + MIX_GROUP, nb)):
                r0 = b * BLOCK
                if b == 0:
                    kvprev = kvp_ref[...]
                    first_mask = (col < BLOCK) & (i == 0)
                else:
                    kvprev = kv_ref[r0 - BLOCK:r0, :]
                    first_mask = None
                kvcur = kv_ref[r0:r0 + BLOCK, :]
                kk = jnp.concatenate([kvprev[:, :KV_W], kvcur[:, :KV_W]], axis=0)
                vv = jnp.concatenate([kvprev[:, KV_W:], kvcur[:, KV_W:]], axis=0)
                gens.append(_attn_block_fwd(q_ref[r0:r0 + BLOCK, :], kk, vv, bias_ref, sinks_ref, first_mask))
                gens.append(_gmlp_chunk_fwd(gu_ref[r0:r0 + BLOCK, :], gv_ref[r0:r0 + BLOCK, :], glng_ref[...],
                                            glnb_ref[...], wsm_ref, bsx_ref[...], amat_ref[...]))
            res = _interleave(*gens)
            for k, b in enumerate(range(b0, min(b0 + MIX_GROUP, nb))):
                r0 = b * BLOCK
                na, _ = _rms(res[2 * k][0], aog_ref[...])
                ng, _ = _rms(res[2 * k + 1][0], gog_ref[...])
                mix_scr[r0:r0 + BLOCK, :ATTN_W] = na.astype(BF16)
                mix_scr[r0:r0 + BLOCK, ATTN_W:] = ng.astype(BF16)
        mixed = mix_scr[...]
        mixed_ref[...] = mixed
        y = _dot(mixed, wout_ref[...])
        y_ref[...] = y.astype(BF16)
        z1 = ALPHA * x_ref[...] + mod_ref[2:3, :] * y
        xhat, _ = _ln_stats(z1)
        x1 = xhat * ln1g_ref[...] + ln1b_ref[...]
        x1_ref[...] = x1
        x1b_ref[...] = x1.astype(BF16)

        @pl.when(i == fwd_step)
        def _():
            gather.forward()

        @pl.when(i == diag_step)
        def _():
            gather.forward_diagonal()

        @pl.when(i == n_steps - 1)
        def _():
            gather.finish()

    row = lambda w: pl.BlockSpec((tm, w), lambda i: (i, 0))
    prev = pl.BlockSpec((BLOCK, 2 * KV_W), lambda i: (jnp.maximum(i * nb - 1, 0), 0))
    outs = pl.pallas_call(
        body, name="fwd_mix",
        out_shape=[jax.ShapeDtypeStruct((s, D_MODEL), F32)] + [jax.ShapeDtypeStruct((s, D_MODEL), BF16)] * 3
        + [jax.ShapeDtypeStruct(_gathered_shape(sh, k), BF16) for sh, k in zip(ffn_shards, ffn_kinds)],
        grid=(n_steps,),
        in_specs=[row(ATTN_W), row(2 * KV_W), prev, row(GMLP_W), row(GMLP_W), row(D_MODEL), _const_spec((8, D_MODEL)),
                  _const_spec((N_HEADS, BLOCK, 2 * BLOCK)), pl.BlockSpec(memory_space=pltpu.SMEM),
                  _const_spec((1, GMLP_W)), _const_spec((1, GMLP_W)), _const_spec((N_GROUPS, BLOCK, BLOCK)),
                  _const_spec((BLOCK, GMLP_W)), _const_spec((GMLP_W, GMLP_W)), _const_spec((1, ATTN_W)),
                  _const_spec((1, GMLP_W)), _const_spec((D_MODEL, D_MODEL)), _const_spec((1, D_MODEL)),
                  _const_spec((1, D_MODEL))] + [ANY] * n_w,
        out_specs=[row(D_MODEL)] * 4 + [ANY] * n_w,
        scratch_shapes=[pltpu.VMEM((tm, D_MODEL), BF16)] + _WeightGather.sems(n_w),
        compiler_params=_params(("arbitrary",)),
    )(q, kv, kv, gu, gv, x, modr, bias, sinks, gln_g, gln_b, wsm, bsx, amat, aog, gog, w_out, ln1_g, ln1_b, *ffn_shards)
    return outs[:4], outs[4:]


FF_BLOCKS = N_CHIPS // 2
FF_CHUNK = D_FF // FF_BLOCKS
FFN_SUB = 256


def _sigmoid(x):
    return 1.0 / (1.0 + jnp.exp(-x))


def _fwd_ffn(x1, target, modr, ln2_g, ln2_b, w_gu, w_dn, tm):
    s = x1.shape[0]

    def body(x1_ref, t_ref, mod_ref, g_ref, b_ref, wgu_ref, wdn_ref, h2_ref, act_ref, dy2_ref, dx1a_ref, acc_ref):
        @pl.when(pl.program_id(0) == 0)
        def _():
            acc_ref[...] = jnp.zeros_like(acc_ref)

        x1v = x1_ref[...]
        h2 = (x1v * (1.0 + mod_ref[4:5, :]) + mod_ref[3:4, :]).astype(BF16)
        h2_ref[...] = h2
        y2 = None
        for cc in range(FF_BLOCKS):
            c0 = cc * FF_CHUNK
            gate = _dot(h2, wgu_ref[cc])
            up = _dot(h2, wgu_ref[FF_BLOCKS + cc])
            act_ref[:, c0:c0 + FF_CHUNK] = gate.astype(BF16)
            act_ref[:, D_FF + c0:D_FF + c0 + FF_CHUNK] = up.astype(BF16)
            a = (gate * _sigmoid(gate) * up).astype(BF16)
            part = _dot(a, wdn_ref[c0:c0 + FF_CHUNK, :])
            y2 = part if y2 is None else y2 + part
        g2 = mod_ref[5:6, :]
        z2 = ALPHA * x1v + g2 * y2
        xhat, rstd = _ln_stats(z2)
        gain = g_ref[...]
        diff = xhat * gain + b_ref[...] - t_ref[...]
        dx2 = diff * (1.0 / D_MODEL)
        dz2 = _ln_bwd(dx2 * gain, xhat, rstd)
        dx1a_ref[...] = ALPHA * dz2
        dy2_ref[...] = (g2 * dz2).astype(BF16)
        acc_ref[0:1, :] += _colsum(diff * diff)
        acc_ref[1:2, :] += _colsum(dx2 * xhat)
        acc_ref[2:3, :] += _colsum(dx2)
        acc_ref[3:4, :] += _colsum(dz2 * y2)

    row = lambda w: pl.BlockSpec((tm, w), lambda i: (i, 0))
    return pl.pallas_call(
        body, name="fwd_ffn",
        out_shape=(jax.ShapeDtypeStruct((s, D_MODEL), BF16), jax.ShapeDtypeStruct((s, 2 * D_FF), BF16),
                   jax.ShapeDtypeStruct((s, D_MODEL), BF16), jax.ShapeDtypeStruct((s, D_MODEL), F32),
                   jax.ShapeDtypeStruct((8, D_MODEL), F32)),
        grid=(s // tm,),
        in_specs=[row(D_MODEL), row(D_MODEL), _const_spec((8, D_MODEL)), _const_spec((1, D_MODEL)),
                  _const_spec((1, D_MODEL)), _const_spec((N_CHIPS, D_MODEL, FF_CHUNK), single=True),
                  _const_spec((D_FF, D_MODEL), single=True)],
        out_specs=(row(D_MODEL), row(2 * D_FF), row(D_MODEL), row(D_MODEL), _const_spec((8, D_MODEL))),
        compiler_params=_params(("arbitrary",)),
    )(x1, target, modr, ln2_g, ln2_b, w_gu, w_dn)


def _bwd_ffn(dy2, act, w_gu, w_dn, tm):
    s = dy2.shape[0]

    def body(dy2_ref, act_ref, wgu_ref, wdn_ref, a_ref, dgu_ref, dh2_ref):
        dy2v = dy2_ref[...]
        dh2 = None
        for cc in range(FF_BLOCKS):
            c0 = cc * FF_CHUNK
            da = _dot_nt(dy2v, wdn_ref[c0:c0 + FF_CHUNK, :])
            gate = act_ref[:, c0:c0 + FF_CHUNK].astype(F32)
            up = act_ref[:, D_FF + c0:D_FF + c0 + FF_CHUNK].astype(F32)
            sg = _sigmoid(gate)
            sl = gate * sg
            a_ref[:, c0:c0 + FF_CHUNK] = (sl * up).astype(BF16)
            dgate = (da * up * (sg * (1.0 + gate * (1.0 - sg)))).astype(BF16)
            dup = (da * sl).astype(BF16)
            dgu_ref[:, c0:c0 + FF_CHUNK] = dgate
            dgu_ref[:, D_FF + c0:D_FF + c0 + FF_CHUNK] = dup
            part = _dot_nt(dgate, wgu_ref[cc]) + _dot_nt(dup, wgu_ref[FF_BLOCKS + cc])
            dh2 = part if dh2 is None else dh2 + part
        dh2_ref[...] = dh2.astype(BF16)

    row = lambda w: pl.BlockSpec((tm, w), lambda i: (i, 0))
    return pl.pallas_call(
        body, name="bwd_ffn",
        out_shape=(jax.ShapeDtypeStruct((s, D_FF), BF16), jax.ShapeDtypeStruct((s, 2 * D_FF), BF16),
                   jax.ShapeDtypeStruct((s, D_MODEL), BF16)),
        grid=(s // tm,),
        in_specs=[row(D_MODEL), row(2 * D_FF), _const_spec((N_CHIPS, D_MODEL, FF_CHUNK), single=True),
                  _const_spec((D_FF, D_MODEL), single=True)],
        out_specs=(row(D_FF), row(2 * D_FF), row(D_MODEL)),
        compiler_params=_params(("parallel",)),
    )(dy2, act, w_gu, w_dn)


def _bwd_mid(dh2, dx1a, x1, x, y, modr, ln1_g, w_out, tm, swap_fulls, swap_kinds):
    s = x.shape[0]
    n_steps = s // tm
    n_g = len(swap_fulls)

    def body(dh2_ref, dx1a_ref, x1_ref, x_ref, y_ref, mod_ref, g_ref, wout_ref, *rest):
        full_refs = rest[:n_g]
        dxa_ref, dy_ref, dmix_ref, acc_ref = rest[n_g:n_g + 4]
        got_refs = rest[n_g + 4:2 * n_g + 4]
        swap = _HalfSwap(full_refs, got_refs, swap_kinds, *rest[2 * n_g + 4:])
        i = pl.program_id(0)

        @pl.when(i == 0)
        def _():
            swap.start()
            acc_ref[...] = jnp.zeros_like(acc_ref)

        dh2 = dh2_ref[...].astype(F32)
        x1v = x1_ref[...].astype(F32)
        yv = y_ref[...].astype(F32)
        g1 = mod_ref[2:3, :]
        dx1 = dx1a_ref[...] + dh2 * (1.0 + mod_ref[4:5, :])
        z1 = ALPHA * x_ref[...] + g1 * yv
        xhat, rstd = _ln_stats(z1)
        dz1 = _ln_bwd(dx1 * g_ref[...], xhat, rstd)
        dxa_ref[...] = ALPHA * dz1
        dy = (g1 * dz1).astype(BF16)
        dy_ref[...] = dy
        dmix_ref[...] = _dot_nt(dy, wout_ref[...]).astype(BF16)
        acc_ref[0:1, :] += _colsum(dh2 * x1v)
        acc_ref[1:2, :] += _colsum(dh2)
        acc_ref[2:3, :] += _colsum(dx1 * xhat)
        acc_ref[3:4, :] += _colsum(dx1)
        acc_ref[4:5, :] += _colsum(dz1 * yv)

        @pl.when(i == n_steps - 1)
        def _():
            swap.wait()

    row = lambda w: pl.BlockSpec((tm, w), lambda i: (i, 0))
    outs = pl.pallas_call(
        body, name="bwd_mid",
        out_shape=[jax.ShapeDtypeStruct((s, D_MODEL), F32), jax.ShapeDtypeStruct((s, D_MODEL), BF16),
                   jax.ShapeDtypeStruct((s, D_MODEL), BF16), jax.ShapeDtypeStruct((8, D_MODEL), F32)]
        + _HalfSwap.out_shapes(swap_fulls, swap_kinds),
        grid=(n_steps,),
        in_specs=[row(D_MODEL)] * 5 + [_const_spec((8, D_MODEL)), _const_spec((1, D_MODEL)),
                                       _const_spec((D_MODEL, D_MODEL))] + [ANY] * n_g,
        out_specs=[row(D_MODEL), row(D_MODEL), row(D_MODEL), _const_spec((8, D_MODEL))] + [ANY] * n_g,
        scratch_shapes=_HalfSwap.sems(n_g),
        compiler_params=_params(("arbitrary",)),
    )(dh2, dx1a, x1, x, y, modr, ln1_g, w_out, *swap_fulls)
    return outs[:4], outs[4:]


def _fold_kv(t0, t1):
    lane = lax.broadcasted_iota(jnp.int32, t0.shape, 1)
    f0 = t0 + pltpu.roll(t0, HEAD_DIM, 1)
    f1 = t1 + pltpu.roll(t1, HEAD_DIM, 1)
    return jnp.where(lane < HEAD_DIM, f0, f1)


def _bwd_mix(q, kv, gu, gv, dmix, bias, sinks, gln_g, gln_b, wsm, bsx, amat, aog, gog, grad_parts, grad_kinds):
    s = q.shape[0]
    tile = 2 * BLOCK
    n_steps = s // tile
    n_g = len(grad_parts)

    def body(q_ref, kv_ref, kvp_ref, gu_ref, gv_ref, dmix_ref, bias_ref, sinks_ref, glng_ref, glnb_ref, wsm_ref,
             bsx_ref, amat_ref, aog_ref, gog_ref, *rest):
        part_refs = rest[:n_g]
        dq_ref, dkv_ref, dgu_ref, dgv_ref, gbias_ref, dws_ref, dbs_ref, vec_ref, dsink_ref = rest[n_g:n_g + 9]
        rx_refs = rest[n_g + 9:2 * n_g + 9]
        carry, done, send_sems, recv_sems = rest[2 * n_g + 9:]
        n = pl.program_id(0)
        exchange = _ChipExchange(part_refs, rx_refs, grad_kinds, send_sems, recv_sems)

        @pl.when(n == 0)
        def _():
            exchange.start()
            carry[...] = jnp.zeros_like(carry)
            done[...] = jnp.zeros_like(done)
            gbias_ref[...] = jnp.zeros_like(gbias_ref)
            dws_ref[...] = jnp.zeros_like(dws_ref)
            dbs_ref[...] = jnp.zeros_like(dbs_ref)
            vec_ref[...] = jnp.zeros_like(vec_ref)
            dsink_ref[...] = jnp.zeros_like(dsink_ref)

        @pl.when(n == n_steps)
        def _():
            dkv_ref[:BLOCK, :] = done[...].astype(BF16)
            dkv_ref[BLOCK:, :] = carry[...].astype(BF16)
            exchange.wait()

        @pl.when(n < n_steps)
        def _():
            col = lax.broadcasted_iota(jnp.int32, (BLOCK, 2 * BLOCK), 1)
            lane = lax.broadcasted_iota(jnp.int32, (BLOCK, LANES), 1)
            low = lane < HEAD_DIM
            rows = [slice(0, BLOCK), slice(BLOCK, tile)]
            kv_blocks = [kvp_ref[...], kv_ref[rows[0], :], kv_ref[rows[1], :]]
            masks = [(col < BLOCK) & (n == 0), None]
            q_blks = [q_ref[r, :] for r in rows]
            fwd = []
            for b in range(2):
                kk = jnp.concatenate([kv_blocks[b][:, :KV_W], kv_blocks[b + 1][:, :KV_W]], axis=0)
                vv = jnp.concatenate([kv_blocks[b][:, KV_W:], kv_blocks[b + 1][:, KV_W:]], axis=0)
                fwd.append(_attn_block_fwd(q_blks[b], kk, vv, bias_ref, sinks_ref, masks[b]))
                fwd.append(_gmlp_chunk_fwd(gu_ref[rows[b], :], gv_ref[rows[b], :], glng_ref[...], glnb_ref[...],
                                           wsm_ref, bsx_ref[...], amat_ref[...]))
            res = _interleave(*fwd[:2]) + _interleave(*fwd[2:])

            def gating_bwd(b, d_gm, saved):
                u, tu, ta, xhat, rstd, vb, mixedv = saved
                dgu_ref[rows[b], :] = (d_gm * mixedv * _gelu_grad(gu_ref[rows[b], :], tu)).astype(BF16)
                dmx = d_gm * u
                dmxb = dmx.astype(BF16)
                yield
                dvn_cols, dws = [], []
                for pair in range(N_GROUPS // 2):
                    dp_ = dmxb[:, pair * LANES:(pair + 1) * LANES]
                    vp = vb[:, pair * LANES:(pair + 1) * LANES]
                    dvn_cols.append(
                        jnp.where(low, _dot_tn(wsm_ref[2 * pair], dp_), _dot_tn(wsm_ref[2 * pair + 1], dp_)))
                    zero = jnp.zeros_like(dp_)
                    dws.append(_dot_nt(jnp.where(low, dp_, zero), vp))
                    dws.append(_dot_nt(jnp.where(low, zero, dp_), vp))
                dvn = jnp.concatenate(dvn_cols, axis=1)
                yield
                dxh = dvn * glng_ref[...]
                am = amat_ref[...]
                m1 = _split_dot(dxh, am)
                m2 = _split_dot(dxh * xhat, am)
                yield
                da = rstd * (dxh - m1 - xhat * m2)
                dgv_ref[rows[b], :] = (da * _gelu_grad(gv_ref[rows[b], :], ta)).astype(BF16)
                return dmx, dws, _colsum(dvn * xhat), _colsum(dvn)

            def attention_bwd(b, d_attn, probs, kvar, vvar):
                heads = range(N_HEADS)
                sels = [low if h % 2 == 0 else jnp.logical_not(low) for h in heads]
                pair_of = lambda a, h: a[:, (h // 2) * LANES:(h // 2 + 1) * LANES]
                do_hs = [jnp.where(sels[h], pair_of(d_attn, h), 0.0).astype(BF16) for h in heads]
                q_hs = [jnp.where(sels[h], pair_of(q_blks[b], h), jnp.zeros((BLOCK, LANES), BF16)) for h in heads]
                dps = [_dot_nt(do_hs[h], vvar[_head_kv(h)[0]][_head_kv(h)[1]]) for h in heads]
                yield
                deltas = [jnp.sum(probs[h][0] * dps[h], axis=-1, keepdims=True) for h in heads]
                yield
                dss = [probs[h][0] * (dps[h] - deltas[h]) for h in heads]
                dsinks = [-(probs[h][1] * deltas[h]) for h in heads]
                dsbs = [ds.astype(BF16) for ds in dss]
                pbs = [probs[h][0].astype(BF16) for h in heads]
                yield
                dqs = [_dot(dsbs[h], kvar[_head_kv(h)[0]][_head_kv(h)[1]]) for h in heads]
                tks = [_dot_tn(dsbs[h], q_hs[h]) for h in heads]
                tvs = [_dot_tn(pbs[h], do_hs[h]) for h in heads]
                dq_cols = [dqs[2 * i] + dqs[2 * i + 1] for i in range(N_HEADS // 2)]
                dq_ref[rows[b], :] = (jnp.concatenate(dq_cols, axis=1) * Q_SCALE).astype(BF16)
                per_kv = N_HEADS // N_KV
                kv_sum = lambda ts, kvh: sum(ts[kvh * per_kv + 1:(kvh + 1) * per_kv], ts[kvh * per_kv])
                dkk = _fold_kv(kv_sum(tks, 0), kv_sum(tks, 1))
                dvv = _fold_kv(kv_sum(tvs, 0), kv_sum(tvs, 1))
                return jnp.concatenate([dkk, dvv], axis=1), dss, dsinks

            bwd, rms_g = [], []
            for b in range(2):
                attn, probs, kvar, vvar = res[2 * b]
                gm, saved = res[2 * b + 1]
                na_unit, r_a = _rms(attn, 1.0)
                ng_unit, r_g = _rms(gm, 1.0)
                dmix = dmix_ref[rows[b], :].astype(F32)
                dn_a = dmix[:, :ATTN_W]
                dn_g = dmix[:, ATTN_W:]
                rms_g.append((_colsum(dn_a * na_unit), _colsum(dn_g * ng_unit)))
                t_a = dn_a * aog_ref[...]
                d_attn = r_a * t_a - na_unit * (r_a * jnp.mean(t_a * na_unit, axis=-1, keepdims=True))
                t_g = dn_g * gog_ref[...]
                d_gm = r_g * t_g - ng_unit * (r_g * jnp.mean(t_g * ng_unit, axis=-1, keepdims=True))
                bwd.append(attention_bwd(b, d_attn, probs, kvar, vvar))
                bwd.append(gating_bwd(b, d_gm, saved))
            (dkv_a, dss_a, dsk_a), (dmx_a, dws_a, glg_a, glb_a) = _interleave(*bwd[:2])
            (dkv_b, dss_b, dsk_b), (dmx_b, dws_b, glg_b, glb_b) = _interleave(*bwd[2:])

            vec_ref[0:1, :] += rms_g[0][0] + rms_g[1][0]
            vec_ref[1:2, :] += rms_g[0][1] + rms_g[1][1]
            vec_ref[2:3, :] += glg_a + glg_b
            vec_ref[3:4, :] += glb_a + glb_b
            dbs_ref[...] += dmx_a + dmx_b
            for g in range(N_GROUPS):
                dws_ref[g] += dws_a[g] + dws_b[g]
            for h in range(N_HEADS):
                gbias_ref[h] += dss_a[h] + dss_b[h]
                dsink_ref[h] += dsk_a[h] + dsk_b[h]

            dkv_ref[:BLOCK, :] = done[...].astype(BF16)
            dkv_ref[BLOCK:, :] = (carry[...] + dkv_a[:BLOCK]).astype(BF16)
            done[...] = dkv_a[BLOCK:] + dkv_b[:BLOCK]
            carry[...] = dkv_b[BLOCK:]

    last = n_steps - 1
    cur = lambda w: pl.BlockSpec((tile, w), lambda n: (jnp.minimum(n, last), 0))
    late = lambda w: pl.BlockSpec((tile, w), lambda n: (jnp.clip(n - 1, 0, last), 0))
    before = pl.BlockSpec((BLOCK, 2 * KV_W), lambda n: (jnp.clip(2 * n - 1, 0, 2 * last + 1), 0))
    outs = pl.pallas_call(
        body, name="bwd_mix",
        out_shape=[jax.ShapeDtypeStruct((s, ATTN_W), BF16), jax.ShapeDtypeStruct((s, 2 * KV_W), BF16),
                   jax.ShapeDtypeStruct((s, GMLP_W), BF16), jax.ShapeDtypeStruct((s, GMLP_W), BF16),
                   jax.ShapeDtypeStruct((N_HEADS, BLOCK, 2 * BLOCK), F32),
                   jax.ShapeDtypeStruct((N_GROUPS, BLOCK, BLOCK), F32),
                   jax.ShapeDtypeStruct((BLOCK, GMLP_W), F32), jax.ShapeDtypeStruct((8, GMLP_W), F32),
                   jax.ShapeDtypeStruct((N_HEADS, BLOCK, 1), F32)]
        + [jax.ShapeDtypeStruct(_rx_shape(p.shape, k), BF16) for p, k in zip(grad_parts, grad_kinds)],
        grid=(n_steps + 1,),
        in_specs=[cur(ATTN_W), cur(2 * KV_W), before, cur(GMLP_W), cur(GMLP_W), cur(D_MODEL),
                  _const_spec((N_HEADS, BLOCK, 2 * BLOCK)), pl.BlockSpec(memory_space=pltpu.SMEM),
                  _const_spec((1, GMLP_W)), _const_spec((1, GMLP_W)), _const_spec((N_GROUPS, BLOCK, BLOCK)),
                  _const_spec((BLOCK, GMLP_W)), _const_spec((GMLP_W, GMLP_W)), _const_spec((1, ATTN_W)),
                  _const_spec((1, GMLP_W))] + [ANY] * n_g,
        out_specs=[cur(ATTN_W), late(2 * KV_W), cur(GMLP_W), cur(GMLP_W),
                   _const_spec((N_HEADS, BLOCK, 2 * BLOCK)), _const_spec((N_GROUPS, BLOCK, BLOCK)),
                   _const_spec((BLOCK, GMLP_W)), _const_spec((8, GMLP_W)), _const_spec((N_HEADS, BLOCK, 1))]
        + [ANY] * n_g,
        scratch_shapes=[pltpu.VMEM((BLOCK, 2 * KV_W), F32), pltpu.VMEM((BLOCK, 2 * KV_W), F32)]
        + _ChipExchange.sems(n_g),
        compiler_params=_params(("arbitrary",)),
    )(q, kv, kv, gu, gv, dmix, bias, sinks, gln_g, gln_b, wsm, bsx, amat, aog, gog, *grad_parts)
    return outs[:9], outs[9:]


def _mix_finalize(gbias, bucket, dws, dbs, dsink):
    def body(gb_ref, bucket_ref, dws_ref, dbs_ref, dsink_ref, tall_ref):
        bk = bucket_ref[...]
        lane = lax.broadcasted_iota(jnp.int32, (N_BUCKETS, LANES), 1)
        rowi = lax.broadcasted_iota(jnp.int32, (N_BUCKETS, LANES), 0)
        drb = jnp.zeros((N_BUCKETS, LANES), F32)
        dsk = jnp.zeros((8, LANES), F32)
        lane8 = lax.broadcasted_iota(jnp.int32, (8, LANES), 1)
        for h in range(N_HEADS):
            g = gb_ref[h]
            for b in range(N_BUCKETS):
                tot = jnp.sum(_colsum(jnp.where(bk == float(b), g, 0.0)), axis=1, keepdims=True)
                drb = jnp.where((lane == h) & (rowi == b), tot, drb)
            sk = jnp.sum(dsink_ref[h], axis=0, keepdims=True)
            dsk = jnp.where(lane8 == h, sk, dsk)
        tall_ref[TALL_RB:TALL_RB + N_BUCKETS, :] = drb
        tall_ref[TALL_SK:TALL_SK + 8, :] = dsk
        ti = lax.broadcasted_iota(jnp.int32, (BLOCK, BLOCK), 0)
        ui = lax.broadcasted_iota(jnp.int32, (BLOCK, BLOCK), 1)
        for g in range(N_GROUPS):
            tall_ref[g * BLOCK:(g + 1) * BLOCK, :] = jnp.where(ti >= ui, dws_ref[g], 0.0)
        gi = lax.broadcasted_iota(jnp.int32, (GMLP_W, LANES), 0) // GROUP_DIM
        li = lax.broadcasted_iota(jnp.int32, (GMLP_W, LANES), 1)
        ind = jnp.where(gi == li, 1.0, 0.0).astype(BF16)
        d = dbs_ref[...]
        hi = d.astype(BF16)
        r1 = d - hi.astype(F32)
        mid = r1.astype(BF16)
        lo = (r1 - mid.astype(F32)).astype(BF16)
        dbsg = _dot(hi, ind) + _dot(mid, ind) + _dot(lo, ind)
        tall_ref[TALL_BS:TALL_BS + N_GROUPS, :] = dbsg.T[:N_GROUPS, :]

    return pl.pallas_call(
        body, name="mix_finalize", out_shape=jax.ShapeDtypeStruct((TALL_ROWS, LANES), F32), grid=(1,),
        in_specs=[_const_spec((N_HEADS, BLOCK, 2 * BLOCK)), _const_spec((BLOCK, 2 * BLOCK)),
                  _const_spec((N_GROUPS, BLOCK, BLOCK)), _const_spec((BLOCK, GMLP_W)),
                  _const_spec((N_HEADS, BLOCK, 1))],
        out_specs=_const_spec((TALL_ROWS, LANES)),
        compiler_params=_params(("arbitrary",)),
    )(gbias, bucket, dws, dbs, dsink)


def _bwd_in(dq, dkv, dgu, dgv, dxa, x, modr, w_in, tm):
    s = x.shape[0]

    def body(dq_ref, dkv_ref, dgu_ref, dgv_ref, dxa_ref, x_ref, mod_ref, w_ref, gx_ref, acc_ref, db_ref):
        @pl.when(pl.program_id(0) == 0)
        def _():
            acc_ref[...] = jnp.zeros_like(acc_ref)
            db_ref[...] = jnp.zeros_like(db_ref)

        dproj = jnp.concatenate([dq_ref[...], dkv_ref[...], dgu_ref[...], dgv_ref[...]], axis=1)
        dh1 = _dot_nt(dproj, w_ref[...])
        gx_ref[...] = dxa_ref[...] + dh1 * (1.0 + mod_ref[1:2, :])
        acc_ref[0:1, :] += _colsum(dh1 * x_ref[...])
        acc_ref[1:2, :] += _colsum(dh1)
        db_ref[0:1, :] += _colsum(dproj.astype(F32))

    row = lambda w: pl.BlockSpec((tm, w), lambda i: (i, 0))
    return pl.pallas_call(
        body, name="bwd_in",
        out_shape=(jax.ShapeDtypeStruct((s, D_MODEL), F32), jax.ShapeDtypeStruct((8, D_MODEL), F32),
                   jax.ShapeDtypeStruct((8, IN_W), F32)),
        grid=(s // tm,),
        in_specs=[row(ATTN_W), row(2 * KV_W), row(GMLP_W), row(GMLP_W), row(D_MODEL), row(D_MODEL),
                  _const_spec((8, D_MODEL)), _const_spec((D_MODEL, IN_W))],
        out_specs=(row(D_MODEL), _const_spec((8, D_MODEL)), _const_spec((8, IN_W))),
        compiler_params=_params(("arbitrary",)),
    )(dq, dkv, dgu, dgv, dxa, x, modr, w_in)


def _wgrad(a, bs, tm, tk, name, owner_blocks=False, gather_vs=()):
    k_all, m = a.shape
    n = sum(b.shape[1] for b in bs)
    nk = k_all // tk
    nm = m // tm
    n_b = len(bs)
    n_v = len(gather_vs)
    wb = n // N_CHIPS

    def body(a_ref, *rest):
        b_refs, v_refs = rest[:n_b], rest[n_b:n_b + n_v]
        o_ref, ob_ref = rest[n_b + n_v:n_b + n_v + 2]
        vg_refs = rest[n_b + n_v + 2:n_b + 2 * n_v + 2]
        i, k = pl.program_id(0), pl.program_id(1)
        if n_v:
            gather = _Gather8(v_refs, vg_refs, *rest[n_b + 2 * n_v + 2:])

            @pl.when((i == 0) & (k == 0))
            def _():
                gather.start()

            @pl.when((i == nm - 1) & (k == 0))
            def _():
                gather.forward()

        @pl.when(k == 0)
        def _():
            o_ref[...] = jnp.zeros_like(o_ref)

        b = b_refs[0][...] if n_b == 1 else jnp.concatenate([r[...] for r in b_refs], axis=1)
        if owner_blocks:
            av = a_ref[...]
            for j in range(N_CHIPS):
                o_ref[j] += _dot_tn(av, b[:, j * wb:(j + 1) * wb])
        else:
            o_ref[...] += _dot_tn(a_ref[...], b)

        @pl.when(k == nk - 1)
        def _():
            ob_ref[...] = o_ref[...].astype(BF16)

        if n_v:
            @pl.when((i == nm - 1) & (k == nk - 1))
            def _():
                gather.finish()

    if owner_blocks:
        out_spec = pl.BlockSpec((N_CHIPS, tm, wb), lambda i, k: (0, i, 0))
        shape = (N_CHIPS, m, wb)
    else:
        out_spec = pl.BlockSpec((tm, n), lambda i, k: (i, 0))
        shape = (m, n)
    outs = pl.pallas_call(
        body, name=name,
        out_shape=[jax.ShapeDtypeStruct(shape, F32), jax.ShapeDtypeStruct(shape, BF16)] + _gathered8_shapes(gather_vs),
        grid=(nm, nk),
        in_specs=[pl.BlockSpec((tk, tm), lambda i, k: (k, i))]
        + [pl.BlockSpec((tk, b.shape[1]), lambda i, k: (k, 0)) for b in bs] + [ANY] * n_v,
        out_specs=[out_spec, out_spec] + [ANY] * n_v,
        scratch_shapes=_Gather8.sems(n_v) if n_v else [],
        compiler_params=_params(("arbitrary", "arbitrary") if n_v else ("parallel", "arbitrary")),
    )(a, *bs, *gather_vs)
    return outs[0], outs[1], outs[2:]


def _adam_math(w, g, m, v):
    m2 = ADAM_B1 * m + (1.0 - ADAM_B1) * g
    v2 = ADAM_B2 * v + (1.0 - ADAM_B2) * (g * g)
    m_hat = m2 / (1.0 - ADAM_B1 ** ADAM_STEP)
    v_hat = v2 / (1.0 - ADAM_B2 ** ADAM_STEP)
    delta = -ADAM_LR * (m_hat / (jnp.sqrt(v_hat) + ADAM_EPS) + ADAM_WD * w)
    return delta, m2, v2


def _adam_halves(w, mine, got, m, v, tr, name):
    r, cc = w.shape
    h = r // 2
    nt = h // tr

    def body(c_ref, w_ref, mine_ref, got_ref, m_ref, v_ref, g_ref, d_ref, m2_ref, v2_ref):
        g = jnp.where(pl.program_id(0) == c_ref[0], mine_ref[...], got_ref[...])
        g_ref[...] = g
        d, m2, v2 = _adam_math(w_ref[...], g, m_ref[...], v_ref[...])
        d_ref[...] = d
        m2_ref[...] = m2
        v2_ref[...] = v2

    full = pl.BlockSpec((tr, cc), lambda hh, i, c_ref: (hh * nt + i, 0))
    half = pl.BlockSpec((tr, cc), lambda hh, i, c_ref: (i, 0))
    shp = jax.ShapeDtypeStruct((r, cc), F32)
    return pl.pallas_call(
        body, name=name, out_shape=(shp, shp, shp, shp),
        grid_spec=pltpu.PrefetchScalarGridSpec(
            num_scalar_prefetch=1, grid=(2, nt), in_specs=[full, half, half, full, full],
            out_specs=(full, full, full, full)),
        compiler_params=_params(("arbitrary", "arbitrary")),
    )(_core_index_scalar(), w, mine, got, m, v)


def _adam_w_ada(sc_t, dmod_cols, w, m, v, tr):
    r, cc = w.shape

    def body(sct_ref, dm_ref, w_ref, m_ref, v_ref, g_ref, d_ref, m2_ref, v2_ref):
        g = sct_ref[:, 0:1] * dm_ref[0:1, :]
        for k in range(1, N_DEV):
            g = g + sct_ref[:, k:k + 1] * dm_ref[k:k + 1, :]
        g_ref[...] = g
        d, m2, v2 = _adam_math(w_ref[...], g, m_ref[...], v_ref[...])
        d_ref[...] = d
        m2_ref[...] = m2
        v2_ref[...] = v2

    spec = pl.BlockSpec((tr, cc), lambda i: (i, 0))
    shp = jax.ShapeDtypeStruct((r, cc), F32)
    return pl.pallas_call(
        body, name="adam_w_ada", out_shape=(shp, shp, shp, shp), grid=(r // tr,),
        in_specs=[pl.BlockSpec((tr, N_DEV), lambda i: (i, 0)), _const_spec((N_DEV, cc)), spec, spec, spec],
        out_specs=(spec, spec, spec, spec), compiler_params=_params(("parallel",)),
    )(sc_t, dmod_cols, w, m, v)


def _pack_wide(acc_i, acc_m, acc_f, db_in, vec):
    arrs = [acc_i, acc_m, acc_f, db_in, vec]
    i_, m_, f_, b_, v_ = range(5)
    src = {"b_in": (b_, 0), "ln1_g": (m_, 2), "ln1_b": (m_, 3), "ln2_g": (f_, 1), "ln2_b": (f_, 2),
           "gmlp_ln_g": (v_, 2), "gmlp_ln_b": (v_, 3), "attn_out_g": (v_, 0), "gmlp_out_g": (v_, 1), "loss": (f_, 0)}
    dmod = [(i_, 1), (i_, 0), (m_, 4), (m_, 1), (m_, 0), (f_, 3)]

    def body(*refs):
        ins, wide_ref = refs[:5], refs[5]
        wide_ref[...] = jnp.zeros_like(wide_ref)
        for k, (a, row) in enumerate(dmod):
            wide_ref[0:1, k * D_MODEL:(k + 1) * D_MODEL] = ins[a][row:row + 1, :]
        for name, (a, row) in src.items():
            r, off, n = WIDE_LAYOUT[name]
            wide_ref[r:r + 1, off:off + n] = ins[a][row:row + 1, :]

    return pl.pallas_call(
        body, name="pack_wide", out_shape=jax.ShapeDtypeStruct((8, WIDE_W), F32), grid=(1,),
        in_specs=[_const_spec(a.shape) for a in arrs], out_specs=_const_spec((8, WIDE_W)),
        compiler_params=_params(("arbitrary",)),
    )(*arrs)


def _adam_small(gw, gt, wide_wmv, w_s, b_s, rel_bias, sinks):
    names = list(WIDE_PARAMS)
    tall = [("gmlp_w_s", w_s), ("gmlp_b_s", b_s), ("rel_bias", rel_bias), ("attn_sinks", sinks)]
    ins = [gw, gt]
    for n in names:
        ins += list(wide_wmv[n])
    for _, t in tall:
        ins += list(t)
    n_in = len(ins)

    def body(*refs):
        gw_ref, gt_ref = refs[0], refs[1]
        wmv = refs[2:n_in]
        dmod_ref, loss_ref = refs[n_in], refs[n_in + 1]
        outs = refs[n_in + 2:]

        def tall_sum(r0, nr):
            g = gt_ref[r0:r0 + nr, :]
            for d in range(1, N_DEV):
                g = g + gt_ref[d * TALL_ROWS + r0:d * TALL_ROWS + r0 + nr, :]
            return g

        def emit(k, g, w_ref, m_ref, v_ref):
            d, m2, v2 = _adam_math(w_ref[...], g, m_ref[...], v_ref[...])
            outs[4 * k][...] = g
            outs[4 * k + 1][...] = d
            outs[4 * k + 2][...] = m2
            outs[4 * k + 3][...] = v2

        gsum = gw_ref[0:8, :]
        for d in range(1, N_DEV):
            gsum = gsum + gw_ref[8 * d:8 * d + 8, :]
        for d in range(N_DEV):
            dmod_ref[d:d + 1, :] = gw_ref[8 * d:8 * d + 1, :]
        for k, n in enumerate(names):
            r, off, sz = WIDE_LAYOUT[n]
            emit(k, gsum[r:r + 1, off:off + sz], *wmv[3 * k:3 * k + 3])
        r, off, sz = WIDE_LAYOUT["loss"]
        tot = jnp.sum(gsum[r:r + 1, off:off + sz], axis=1, keepdims=True)
        loss_ref[...] = jnp.broadcast_to(tot * (0.5 / D_MODEL), loss_ref.shape)

        k0 = len(names)
        ws_refs = wmv[3 * k0:3 * k0 + 3]
        for g in range(N_GROUPS):
            rows = slice(g * BLOCK, (g + 1) * BLOCK)
            gg = tall_sum(g * BLOCK, BLOCK)
            d, m2, v2 = _adam_math(ws_refs[0][rows, :], gg, ws_refs[1][rows, :], ws_refs[2][rows, :])
            outs[4 * k0][rows, :] = gg
            outs[4 * k0 + 1][rows, :] = d
            outs[4 * k0 + 2][rows, :] = m2
            outs[4 * k0 + 3][rows, :] = v2
        emit(k0 + 1, tall_sum(TALL_BS, N_GROUPS), *wmv[3 * (k0 + 1):3 * (k0 + 1) + 3])
        emit(k0 + 2, tall_sum(TALL_RB, N_BUCKETS)[:, :N_HEADS], *wmv[3 * (k0 + 2):3 * (k0 + 2) + 3])
        emit(k0 + 3, tall_sum(TALL_SK, 8)[0:1, :N_HEADS], *wmv[3 * (k0 + 3):3 * (k0 + 3) + 3])

    out_shapes = [jax.ShapeDtypeStruct((N_DEV, WIDE_W), F32), jax.ShapeDtypeStruct((8, LANES), F32)]
    for n in names:
        out_shapes += [jax.ShapeDtypeStruct(wide_wmv[n][0].shape, F32)] * 4
    for _, t in tall:
        out_shapes += [jax.ShapeDtypeStruct(t[0].shape, F32)] * 4
    res = pl.pallas_call(
        body, name="adam_small", out_shape=out_shapes, grid=(1,),
        in_specs=[_const_spec(a.shape) for a in ins], out_specs=[_const_spec(o.shape) for o in out_shapes],
        compiler_params=_params(("arbitrary",)),
    )(*ins)
    out = {}
    for k, n in enumerate(names + [t[0] for t in tall]):
        out[n] = tuple(res[2 + 4 * k:6 + 4 * k])
    return res[0], res[1], out


def kernel(x, c, rel_bias, w_ada, b_ada, w_in, b_in, attn_sinks, gmlp_ln_g, gmlp_ln_b, gmlp_w_s, gmlp_b_s, attn_out_g, gmlp_out_g, w_out, ln1_g, ln1_b, w_gate_up, w_down, ln2_g, ln2_b, loss_target, m_rel_bias, m_w_ada, m_b_ada, m_w_in, m_b_in, m_attn_sinks, m_gmlp_ln_g, m_gmlp_ln_b, m_gmlp_w_s, m_gmlp_b_s, m_attn_out_g, m_gmlp_out_g, m_w_out, m_ln1_g, m_ln1_b, m_w_gate_up, m_w_down, m_ln2_g, m_ln2_b, v_rel_bias, v_w_ada, v_b_ada, v_w_in, v_b_in, v_attn_sinks, v_gmlp_ln_g, v_gmlp_ln_b, v_gmlp_w_s, v_gmlp_b_s, v_attn_out_g, v_gmlp_out_g, v_w_out, v_ln1_g, v_ln1_b, v_w_gate_up, v_w_down, v_ln2_g, v_ln2_b):
    ix, iy, ic = _my_pos()
    chip = 2 * ix + iy
    dev = 4 * ix + 2 * iy + ic
    s = x.shape[1]
    xs = x[0]
    tgt = loss_target[0]
    tm_big = min(512, s)
    tm_ffn = min(FFN_SUB, s)
    n_ada = w_ada.shape[2]

    w_in_s, w_out_s = w_in[0].astype(BF16), w_out[0].astype(BF16)
    w_gu_s, w_dn_s = w_gate_up[0].astype(BF16), w_down[0].astype(BF16)
    sc_all, mod_rows, (w_in_g, w_out_g) = _prologue(
        jnp.pad(c, ((0, 7), (0, 0))), w_ada[0], lax.dynamic_slice_in_dim(b_ada, chip * n_ada, n_ada, axis=1),
        [w_in_s, w_out_s])
    mod_all = mod_rows.reshape(N_DEV, N_DEV, -1)
    mod_row = lax.dynamic_index_in_dim(mod_all[0::2], dev, axis=1, keepdims=False)
    modr = jnp.pad(mod_row.reshape(6, D_MODEL), ((0, 2), (0, 0)))
    w_in_g = _insert_own(w_in_g, w_in_s, "blk", chip)
    w_in_f = jnp.transpose(w_in_g, (1, 0, 2)).reshape(D_MODEL, IN_W)

    bucket = _bucket_table()
    bias, wsm = _prep_tables(bucket, rel_bias, gmlp_w_s[0])
    bsx = jnp.repeat(gmlp_b_s[0].T, GROUP_DIM, axis=1)
    amat = _group_mean_matrix()
    sinks = attn_sinks[0]

    (h1, q, kv, gu, gv), (w_dn_g,) = _fwd_in(xs, modr, w_in_f, b_in, tm_big, [w_dn_s], ["blk"])
    w_out_f = _insert_own(w_out_g, w_out_s, "blk", chip).reshape(D_MODEL, D_MODEL)
    (x1, x1b, y, mixed), (w_gu_g,) = _fwd_mix(
        q, kv, gu, gv, xs, modr, bias, sinks, gmlp_ln_g, gmlp_ln_b, wsm, bsx, amat, attn_out_g, gmlp_out_g, w_out_f,
        ln1_g, ln1_b, tm_big, [w_gu_s], ["blk"])
    assert w_gate_up.shape[2] == FF_CHUNK
    w_gu_f = _insert_own(w_gu_g, w_gu_s, "blk", chip)
    w_dn_f = _insert_own(w_dn_g, w_dn_s, "blk", chip).reshape(D_FF, D_MODEL)
    h2, act, dy2, dx1a, acc_f = _fwd_ffn(x1, tgt, modr, ln2_g, ln2_b, w_gu_f, w_dn_f, tm_ffn)

    a_act, dgu_ff, dh2 = _bwd_ffn(dy2, act, w_gu_f, w_dn_f, min(FFN_SUB, s))
    g_dn, g_dn_b, _ = _wgrad(a_act, [dy2], D_FF // 2, min(512, s), "wgrad_down")
    g_gu, g_gu_b, _ = _wgrad(h2, [dgu_ff], 512, min(512, s), "wgrad_gate_up")
    blk3 = lambda a, rows: a.reshape(N_CHIPS, rows, a.shape[1])
    (dxa, dy, dmix, acc_m), (got_dn, got_gu) = _bwd_mid(
        dh2, dx1a, x1b, xs, y, modr, ln1_g, w_out_f, tm_big, [blk3(g_dn_b, D_FF // N_CHIPS), g_gu_b], ["blk", "cols"])
    g_out, g_out_b, _ = _wgrad(mixed, [dy], 512, min(512, s), "wgrad_out")
    (got_out,) = _swap_halves([blk3(g_out_b, D_MODEL // N_CHIPS)], ["blk"], "rs_swap_out")
    kinds_a = ["blk", "cols", "blk"]
    fulls_a = [blk3(g_dn, D_FF // N_CHIPS), g_gu, blk3(g_out, D_MODEL // N_CHIPS)]
    gots_a = [got_dn, got_gu, got_out]
    parts_a = [_add_halves(f, g, k, "rs_add_a%d" % i) for i, (f, g, k) in enumerate(zip(fulls_a, gots_a, kinds_a))]
    (dq, dkv, dgu, dgv, gbias, dws, dbs, vec, dsink), rxs_a = _bwd_mix(
        q, kv, gu, gv, dmix, bias, sinks, gmlp_ln_g, gmlp_ln_b, wsm, bsx, amat, attn_out_g, gmlp_out_g,
        [p[1] for p in parts_a], kinds_a)
    tall_g = _mix_finalize(gbias, bucket, dws, dbs, dsink)
    grad_x, acc_i, db_in = _bwd_in(dq, dkv, dgu, dgv, dxa, xs, modr, w_in_f, tm_big)

    wide_g = _pack_wide(acc_i, acc_m, acc_f, db_in, vec)
    full_in, full_in_b, (gw, gt) = _wgrad(h1, [dq, dkv, dgu, dgv], 512, min(512, s), "wgrad_in", owner_blocks=True,
                                          gather_vs=[wide_g, tall_g])
    (got_in,) = _swap_halves([full_in_b], ["blk"], "rs_swap_in")
    part_in = _add_halves(full_in, got_in, "blk", "rs_add_in")
    (rx_in,) = _exchange_chip_partials([part_in[1]], ["blk"], "rs_chips_in")
    wide_wmv = {"b_ada": (b_ada, m_b_ada, v_b_ada), "b_in": (b_in, m_b_in, v_b_in),
                "ln1_g": (ln1_g, m_ln1_g, v_ln1_g), "ln1_b": (ln1_b, m_ln1_b, v_ln1_b),
                "ln2_g": (ln2_g, m_ln2_g, v_ln2_g), "ln2_b": (ln2_b, m_ln2_b, v_ln2_b),
                "gmlp_ln_g": (gmlp_ln_g, m_gmlp_ln_g, v_gmlp_ln_g), "gmlp_ln_b": (gmlp_ln_b, m_gmlp_ln_b, v_gmlp_ln_b),
                "attn_out_g": (attn_out_g, m_attn_out_g, v_attn_out_g),
                "gmlp_out_g": (gmlp_out_g, m_gmlp_out_g, v_gmlp_out_g)}
    rows2 = lambda a: a.reshape(-1, a.shape[-1])
    dmod_all, loss_t, small = _adam_small(
        gw, gt, wide_wmv, tuple(rows2(a) for a in (gmlp_w_s, m_gmlp_w_s, v_gmlp_w_s)),
        tuple(rows2(a) for a in (gmlp_b_s, m_gmlp_b_s, v_gmlp_b_s)), (rel_bias, m_rel_bias, v_rel_bias),
        (attn_sinks, m_attn_sinks, v_attn_sinks))
    loss = loss_t[0, 0]

    dmod_cols = lax.dynamic_slice_in_dim(dmod_all, chip * n_ada, n_ada, axis=1)
    g_ada, d_ada, m_ada, v_ada = _adam_w_ada(sc_all.T, dmod_cols, w_ada[0], m_w_ada[0], v_w_ada[0], 256)

    sums = [(parts_a[0][0], rxs_a[0], 176), (parts_a[1][0], rxs_a[1], 256), (parts_a[2][0], rxs_a[2], 128),
            (part_in[0], rx_in, 256)]
    mine = [_sum_chips(p, rx, tr, "rs_sum_%d" % i) for i, (p, rx, tr) in enumerate(sums)]
    got = _share_halves(mine, "rs_share")

    gs_dn, d_dn, m_dn, v_dn = _adam_halves(w_down[0], mine[0], got[0], m_w_down[0], v_w_down[0], 176, "adam_w_down")
    gs_gu, d_gu, m_gu, v_gu = _adam_halves(w_gate_up[0], mine[1], got[1], m_w_gate_up[0], v_w_gate_up[0], 256,
                                           "adam_w_gate_up")
    gs_out, d_out, m_out, v_out = _adam_halves(w_out[0], mine[2], got[2], m_w_out[0], v_w_out[0], 128, "adam_w_out")
    gs_in, d_in, m_in, v_in = _adam_halves(w_in[0], mine[3], got[3], m_w_in[0], v_w_in[0], 256, "adam_w_in")

    big = {"w_ada": (g_ada, d_ada, m_ada, v_ada), "w_in": (gs_in, d_in, m_in, v_in), "w_out": (gs_out, d_out, m_out, v_out),
           "w_gate_up": (gs_gu, d_gu, m_gu, v_gu), "w_down": (gs_dn, d_dn, m_dn, v_dn)}
    order = ["rel_bias", "w_ada", "b_ada", "w_in", "b_in", "attn_sinks", "gmlp_ln_g", "gmlp_ln_b", "gmlp_w_s", "gmlp_b_s",
             "attn_out_g", "gmlp_out_g", "w_out", "ln1_g", "ln1_b", "w_gate_up", "w_down", "ln2_g", "ln2_b"]
    shapes = {"gmlp_w_s": gmlp_w_s.shape, "gmlp_b_s": gmlp_b_s.shape}
    outs = [loss, grad_x[None]]
    for k in range(4):
        for name in order:
            if name in big:
                outs.append(big[name][k][None])
            elif name in shapes:
                outs.append(small[name][k].reshape(shapes[name]))
            else:
                outs.append(small[name][k])
    return tuple(outs)
```

```python
import math

import numpy as np
import jax
import jax.numpy as jnp
from jax import lax
from jax.experimental import pallas as pl
from jax.experimental.pallas import tpu as pltpu

F32 = jnp.float32
BF16 = jnp.bfloat16
MESH = pl.DeviceIdType.MESH

D_MODEL = 1024
N_HEADS = 8
N_KV = 2
HEAD_DIM = 64
ATTN_W = N_HEADS * HEAD_DIM
KV_W = N_KV * HEAD_DIM
N_GROUPS = 8
GROUP_DIM = 64
GMLP_W = N_GROUPS * GROUP_DIM
IN_W = ATTN_W + 2 * KV_W + 2 * GMLP_W
BLOCK = 128
N_BUCKETS = 32
MAX_DISTANCE = 128
D_FF = 2816
ALPHA = 2.0 ** 0.25
LN_EPS = 1e-5
NEG_INF = -1e30
ADAM_LR, ADAM_B1, ADAM_B2, ADAM_EPS, ADAM_WD, ADAM_STEP = 0.001, 0.9, 0.999, 1e-8, 0.01, 10
N_CHIPS = 4
N_DEV = 8
LANES = 128
V7X_VMEM_LIMIT = 56 * 2 ** 20
GELU_C = math.sqrt(2.0 / math.pi)
Q_SCALE = HEAD_DIM ** -0.5
ANY = pl.BlockSpec(memory_space=pl.ANY)

TALL_BS = N_GROUPS * BLOCK
TALL_RB = TALL_BS + 8
TALL_SK = TALL_RB + N_BUCKETS
TALL_ROWS = TALL_SK + 8
WIDE_W = 6 * D_MODEL
WIDE_LAYOUT = {
    "b_ada": (0, 0, 6 * D_MODEL),
    "b_in": (1, 0, IN_W), "ln1_g": (1, IN_W, D_MODEL), "ln1_b": (1, IN_W + D_MODEL, D_MODEL),
    "ln2_g": (1, IN_W + 2 * D_MODEL, D_MODEL), "ln2_b": (1, IN_W + 3 * D_MODEL, D_MODEL),
    "gmlp_ln_g": (2, 0, GMLP_W), "gmlp_ln_b": (2, GMLP_W, GMLP_W), "attn_out_g": (2, 2 * GMLP_W, ATTN_W),
    "gmlp_out_g": (2, 2 * GMLP_W + ATTN_W, GMLP_W), "loss": (2, 3 * GMLP_W + ATTN_W, D_MODEL)}
WIDE_PARAMS = tuple(n for n in WIDE_LAYOUT if n != "loss")


def _params(sem=None):
    return pltpu.CompilerParams(dimension_semantics=sem, vmem_limit_bytes=V7X_VMEM_LIMIT)


def _const_spec(shape, single=False):
    nd = len(shape)
    if single:
        return pl.BlockSpec(shape, lambda *_: (0,) * nd, pipeline_mode=pl.Buffered(1))
    return pl.BlockSpec(shape, lambda *_: (0,) * nd)


def _dot(a, b):
    return jnp.dot(a, b, preferred_element_type=F32)


def _dot_nt(a, b):
    return lax.dot_general(a, b, (((1,), (1,)), ((), ())), preferred_element_type=F32)


def _dot_tn(a, b):
    return lax.dot_general(a, b, (((0,), (0,)), ((), ())), preferred_element_type=F32)


def _gelu(x):
    t = jnp.tanh(GELU_C * (x + 0.044715 * x * x * x))
    return 0.5 * x * (1.0 + t), t


def _gelu_grad(x, t):
    return 0.5 * (1.0 + t) + 0.5 * x * (1.0 - t * t) * GELU_C * (1.0 + 3.0 * 0.044715 * x * x)


def _split_dot(x, a):
    hi = x.astype(BF16)
    lo = (x - hi.astype(F32)).astype(BF16)
    return _dot(hi, a) + _dot(lo, a)


def _group_mean_matrix():
    g = np.arange(GMLP_W) // GROUP_DIM
    return jnp.asarray((g[:, None] == g[None, :]).astype(np.float32) / GROUP_DIM, dtype=BF16)


def _ln_stats(z):
    mu = jnp.mean(z, axis=-1, keepdims=True)
    d = z - mu
    var = jnp.mean(d * d, axis=-1, keepdims=True)
    rstd = lax.rsqrt(var + LN_EPS)
    return d * rstd, rstd


def _ln_bwd(dxhat, xhat, rstd):
    m1 = jnp.mean(dxhat, axis=-1, keepdims=True)
    m2 = jnp.mean(dxhat * xhat, axis=-1, keepdims=True)
    return rstd * (dxhat - m1 - xhat * m2)


def _colsum(x):
    return jnp.sum(x, axis=0, keepdims=True)


def _my_pos():
    return lax.axis_index("x"), lax.axis_index("y"), lax.axis_index("c")


def _other_chips(x, y):
    return [(1 - x, y), (x, 1 - y), (1 - x, 1 - y)]


def _chip_index_scalar():
    ix, iy, _ = _my_pos()
    return jnp.reshape(2 * ix + iy, (1,)).astype(jnp.int32)


def _core_index_scalar():
    return jnp.reshape(lax.axis_index("c"), (1,)).astype(jnp.int32)


class _Gather8:
    def __init__(self, x_refs, out_refs, send_sems, recv_sems, local_sems):
        self.x_refs, self.out_refs = x_refs, out_refs
        self.send_sems, self.recv_sems, self.local_sems = send_sems, recv_sems, local_sems
        self.x, self.y, self.c = _my_pos()
        self.me, self.sibling = (self.x, self.y, self.c), (self.x, self.y, 1 - self.c)
        self.chips = _other_chips(self.x, self.y)

    def _rows(self, a, px, py, pc):
        m_per = self.x_refs[a].shape[0]
        return self.out_refs[a].at[pl.ds((4 * px + 2 * py + pc) * m_per, m_per), :]

    def _copy(self, a, k, block, to, src=None):
        return pltpu.make_async_remote_copy(
            src_ref=self._rows(a, *block) if src is None else src, dst_ref=self._rows(a, *block),
            send_sem=self.send_sems.at[7 * a + k], recv_sem=self.recv_sems.at[7 * a + k], device_id=to,
            device_id_type=MESH)

    def _local(self, a):
        return pltpu.make_async_copy(self.x_refs[a], self._rows(a, *self.me), self.local_sems.at[a])

    def start(self):
        for a in range(len(self.x_refs)):
            self._local(a).start()
            self._copy(a, 0, self.me, self.sibling, src=self.x_refs[a]).start()
            for j, chip in enumerate(self.chips):
                self._copy(a, 1 + j, self.me, (*chip, self.c), src=self.x_refs[a]).start()

    def forward(self):
        for a in range(len(self.x_refs)):
            for j, chip in enumerate(self.chips):
                self._copy(a, 1 + j, (*chip, self.c), self.me).wait_recv()
                self._copy(a, 4 + j, (*chip, self.c), self.sibling).start()

    def finish(self):
        for a in range(len(self.x_refs)):
            self._copy(a, 0, self.sibling, self.me).wait_recv()
            for j, chip in enumerate(self.chips):
                self._copy(a, 4 + j, (*chip, 1 - self.c), self.me).wait_recv()
        for a in range(len(self.x_refs)):
            for k in range(7):
                self._copy(a, k, self.me, self.me).wait_send()
            self._local(a).wait()

    @staticmethod
    def sems(n_v):
        return [pltpu.SemaphoreType.DMA((7 * n_v,)), pltpu.SemaphoreType.DMA((7 * n_v,)),
                pltpu.SemaphoreType.DMA((n_v,))]


def _gathered8_shapes(vs):
    return [jax.ShapeDtypeStruct((N_DEV * v.shape[0], v.shape[1]), v.dtype) for v in vs]


VMEM_WHOLE = pl.BlockSpec(memory_space=pltpu.VMEM)


def _prologue(c_pad, w_ada_s, b_ada_s, shards):
    n = w_ada_s.shape[1]
    n_w = len(shards)

    def body(c_ref, w_ref, b_ref, *rest):
        shard_refs = rest[:n_w]
        sc_ref, modc_ref, modg_ref = rest[n_w:n_w + 3]
        gathered_refs = rest[n_w + 3:2 * n_w + 3]
        call_ref = rest[2 * n_w + 3]
        sems = rest[2 * n_w + 4:]
        weights = _WeightGather(shard_refs, gathered_refs, ["blk"] * n_w, sems[0], sems[1])
        gather_c = _Gather8([c_ref], [call_ref], sems[2], sems[3], sems[4])
        gather_mod = _Gather8([modc_ref], [modg_ref], sems[5], sems[6], sems[7])
        weights.start()
        gather_c.start()
        gather_c.forward()
        gather_c.finish()
        cv = call_ref[...]
        sc = cv * _sigmoid(cv)
        a_hi = sc.astype(BF16)
        a_lo = (sc - a_hi.astype(F32)).astype(BF16)
        w = w_ref[...]
        w_hi = w.astype(BF16)
        w_lo = (w - w_hi.astype(F32)).astype(BF16)
        mod = _dot(a_hi, w_hi) + _dot(a_hi, w_lo) + _dot(a_lo, w_hi) + b_ref[...]
        for d in range(N_DEV):
            sc_ref[d:d + 1, :] = sc[8 * d:8 * d + 1, :]
            modc_ref[d:d + 1, :] = mod[8 * d:8 * d + 1, :]
        gather_mod.start()
        weights.forward()
        gather_mod.forward()
        gather_mod.finish()
        weights.forward_diagonal()
        weights.finish()

    outs = pl.pallas_call(
        body, name="prologue",
        out_shape=[jax.ShapeDtypeStruct((N_DEV, D_MODEL), F32), jax.ShapeDtypeStruct((N_DEV, n), F32),
                   jax.ShapeDtypeStruct((N_DEV * N_DEV, n), F32)]
        + [jax.ShapeDtypeStruct(_gathered_shape(sh, "blk"), BF16) for sh in shards],
        in_specs=[VMEM_WHOLE, VMEM_WHOLE, VMEM_WHOLE] + [ANY] * n_w,
        out_specs=[VMEM_WHOLE, VMEM_WHOLE, VMEM_WHOLE] + [ANY] * n_w,
        scratch_shapes=[pltpu.VMEM((N_DEV * 8, D_MODEL), F32)] + _WeightGather.sems(n_w) + _Gather8.sems(1)
        + _Gather8.sems(1),
        compiler_params=pltpu.CompilerParams(vmem_limit_bytes=V7X_VMEM_LIMIT),
    )(c_pad, w_ada_s, b_ada_s, *shards)
    return outs[0], outs[2], outs[3:]


def _gathered_shape(shard, kind):
    r, cc = shard.shape
    return (N_CHIPS, r, cc) if kind == "blk" else (r, N_CHIPS * cc)


class _WeightGather:
    N_SEM = 8

    def __init__(self, shards, gathered, kinds, send_sems, recv_sems):
        self.shards, self.gathered, self.kinds = shards, gathered, kinds
        self.send_sems, self.recv_sems = send_sems, recv_sems
        self.x, self.y, self.c = _my_pos()
        self.me, self.sibling = (self.x, self.y, self.c), (self.x, self.y, 1 - self.c)
        self.nbr = ((1 - self.x, self.y), (self.x, 1 - self.y))
        self.diag = 2 * (1 - self.x) + (1 - self.y)

    def _dst(self, a, chip, pc, quarter=None):
        r, cc = self.shards[a].shape
        h = r // 2
        row0, rows = pc * h, h
        if quarter is not None:
            row0, rows = pc * h + quarter * (h // 2), h // 2
        g = self.gathered[a]
        if self.kinds[a] == "blk":
            return g.at[chip, pl.ds(row0, rows), :]
        return g.at[pl.ds(row0, rows), pl.ds(chip * cc, cc)]

    def _copy(self, a, k, region, to, src=None):
        return pltpu.make_async_remote_copy(
            src_ref=region if src is None else src, dst_ref=region, send_sem=self.send_sems.at[a * self.N_SEM + k],
            recv_sem=self.recv_sems.at[a * self.N_SEM + k], device_id=to, device_id_type=MESH)

    def _arrays(self):
        return range(len(self.shards))

    def start(self):
        my_chip = 2 * self.x + self.y
        for a in self._arrays():
            h = self.shards[a].shape[0] // 2
            mine = self.shards[a].at[pl.ds(self.c * h, h), :]
            for j, chip in enumerate(self.nbr):
                self._copy(a, j, self._dst(a, my_chip, self.c), (*chip, self.c), src=mine).start()

    def forward(self):
        for a in self._arrays():
            for j, chip in enumerate(self.nbr):
                cj = 2 * chip[0] + chip[1]
                half = self._dst(a, cj, self.c)
                self._copy(a, j, half, self.me).wait_recv()
                self._copy(a, 2 + j, half, self.sibling).start()
                other = self.nbr[1 - j]
                self._copy(a, 4 + j, self._dst(a, cj, self.c, quarter=j), (*other, self.c)).start()

    def forward_diagonal(self):
        for a in self._arrays():
            for j in range(2):
                quarter = self._dst(a, self.diag, self.c, quarter=j)
                self._copy(a, 4 + j, quarter, self.me).wait_recv()
                self._copy(a, 6 + j, quarter, self.sibling).start()

    def finish(self):
        for a in self._arrays():
            for j, chip in enumerate(self.nbr):
                self._copy(a, 2 + j, self._dst(a, 2 * chip[0] + chip[1], 1 - self.c), self.me).wait_recv()
                self._copy(a, 6 + j, self._dst(a, self.diag, 1 - self.c, quarter=j), self.me).wait_recv()
        for a in self._arrays():
            half = self._dst(a, self.diag, self.c)
            quarter = self._dst(a, self.diag, self.c, quarter=0)
            for k in range(self.N_SEM):
                self._copy(a, k, half if k < 4 else quarter, self.me).wait_send()

    @classmethod
    def sems(cls, n_arr):
        return [pltpu.SemaphoreType.DMA((n_arr * cls.N_SEM,)), pltpu.SemaphoreType.DMA((n_arr * cls.N_SEM,))]


def _insert_own(gathered, shard, kind, chip):
    if kind == "blk":
        return lax.dynamic_update_slice(gathered, shard[None], (chip, 0, 0))
    return lax.dynamic_update_slice(gathered, shard, (0, chip * shard.shape[1]))


def _half_of_full(ref, kind, pc):
    if kind == "blk":
        h = ref.shape[1] // 2
        return ref.at[:, pl.ds(pc * h, h), :]
    h = ref.shape[0] // 2
    return ref.at[pl.ds(pc * h, h), :]


def _half_shape(shape, kind):
    return (shape[0], shape[1] // 2, shape[2]) if kind == "blk" else (shape[0] // 2, shape[1])


class _HalfSwap:
    def __init__(self, ins, outs, kinds, send_sems, recv_sems):
        self.ins, self.outs, self.kinds = ins, outs, kinds
        self.send_sems, self.recv_sems = send_sems, recv_sems
        self.x, self.y, self.c = _my_pos()

    def _copies(self):
        for a in range(len(self.ins)):
            yield pltpu.make_async_remote_copy(
                src_ref=_half_of_full(self.ins[a], self.kinds[a], 1 - self.c), dst_ref=self.outs[a],
                send_sem=self.send_sems.at[a], recv_sem=self.recv_sems.at[a],
                device_id=(self.x, self.y, 1 - self.c), device_id_type=MESH)

    def start(self):
        for cp in self._copies():
            cp.start()

    def wait(self):
        for cp in self._copies():
            cp.wait()

    @staticmethod
    def sems(n_arr):
        return [pltpu.SemaphoreType.DMA((n_arr,)), pltpu.SemaphoreType.DMA((n_arr,))]

    @staticmethod
    def out_shapes(fulls, kinds):
        return [jax.ShapeDtypeStruct(_half_shape(a.shape, k), a.dtype) for a, k in zip(fulls, kinds)]


def _swap_halves(fulls_bf16, kinds, name):
    n_arr = len(fulls_bf16)

    def body(*refs):
        swap = _HalfSwap(refs[:n_arr], refs[n_arr:2 * n_arr], kinds, *refs[2 * n_arr:])
        swap.start()
        swap.wait()

    return pl.pallas_call(
        body, name=name, out_shape=_HalfSwap.out_shapes(fulls_bf16, kinds),
        in_specs=[ANY] * n_arr, out_specs=[ANY] * n_arr, scratch_shapes=_HalfSwap.sems(n_arr),
    )(*fulls_bf16)


def _add_halves(full, got, kind, name):
    hs = _half_shape(full.shape, kind)

    def body(pos_ref, a_ref, b_ref, o_ref, ob_ref):
        p = a_ref[...] + b_ref[...].astype(F32)
        ob_ref[...] = p.astype(BF16)

        @pl.when(pl.program_id(0) == pos_ref[1])
        def _():
            o_ref[...] = p.reshape(o_ref.shape)

    if kind == "blk":
        nb, h, cc = hs
        own = pl.BlockSpec((1, h, cc), lambda b, pos_ref: (b, pos_ref[0], 0))
        other = pl.BlockSpec((1, h, cc), lambda b, pos_ref: (b, 0, 0))
    else:
        h, cc = hs[0], hs[1] // N_CHIPS
        own = pl.BlockSpec((h, cc), lambda b, pos_ref: (pos_ref[0], b))
        other = pl.BlockSpec((h, cc), lambda b, pos_ref: (0, b))
    pos = jnp.concatenate([_core_index_scalar(), _chip_index_scalar()])
    return pl.pallas_call(
        body, name=name, out_shape=(jax.ShapeDtypeStruct((h, cc), F32), jax.ShapeDtypeStruct(hs, BF16)),
        grid_spec=pltpu.PrefetchScalarGridSpec(
            num_scalar_prefetch=1, grid=(N_CHIPS,), in_specs=[own, other],
            out_specs=(pl.BlockSpec((h, cc), lambda b, pos_ref: (0, 0)), other)),
        compiler_params=_params(("arbitrary",)),
    )(pos, full, got)


def _rx_shape(part_shape, kind):
    if kind == "blk":
        return (3, part_shape[1], part_shape[2])
    return (3, part_shape[0], part_shape[1] // N_CHIPS)


class _ChipExchange:
    def __init__(self, parts, rxs, kinds, send_sems, recv_sems):
        self.parts, self.rxs, self.kinds = parts, rxs, kinds
        self.send_sems, self.recv_sems = send_sems, recv_sems
        self.x, self.y, self.c = _my_pos()
        self.chips = _other_chips(self.x, self.y)

    def _copies(self):
        for a in range(len(self.parts)):
            for j, chip in enumerate(self.chips):
                cj = 2 * chip[0] + chip[1]
                if self.kinds[a] == "blk":
                    src = self.parts[a].at[cj]
                else:
                    cc = self.parts[a].shape[1] // N_CHIPS
                    src = self.parts[a].at[:, pl.ds(cj * cc, cc)]
                yield pltpu.make_async_remote_copy(
                    src_ref=src, dst_ref=self.rxs[a].at[j], send_sem=self.send_sems.at[a * 3 + j],
                    recv_sem=self.recv_sems.at[a * 3 + j], device_id=(*chip, self.c), device_id_type=MESH)

    def start(self):
        for cp in self._copies():
            cp.start()

    def wait(self):
        for cp in self._copies():
            cp.wait_recv()
        for cp in self._copies():
            cp.wait_send()

    @staticmethod
    def sems(n_arr):
        return [pltpu.SemaphoreType.DMA((n_arr * 3,)), pltpu.SemaphoreType.DMA((n_arr * 3,))]


def _exchange_chip_partials(parts, kinds, name):
    n_arr = len(parts)

    def body(*refs):
        exchange = _ChipExchange(refs[:n_arr], refs[n_arr:2 * n_arr], kinds, *refs[2 * n_arr:])
        exchange.start()
        exchange.wait()

    return pl.pallas_call(
        body, name=name,
        out_shape=[jax.ShapeDtypeStruct(_rx_shape(p.shape, k), BF16) for p, k in zip(parts, kinds)],
        in_specs=[ANY] * n_arr, out_specs=[ANY] * n_arr, scratch_shapes=_ChipExchange.sems(n_arr),
    )(*parts)


def _sum_chips(part, rx, tr, name):
    _, h, cc = rx.shape
    flips = (2, 1, 3)

    def body(chip_ref, p_ref, rx_ref, o_ref):
        own = p_ref[...]
        for mc in range(N_CHIPS):
            @pl.when(chip_ref[0] == mc)
            def _():
                terms = sorted([(mc, None)] + [(mc ^ f, j) for j, f in enumerate(flips)])
                acc = None
                for _, j in terms:
                    t = own if j is None else rx_ref[j].astype(F32)
                    acc = t if acc is None else acc + t
                o_ref[...] = acc

    return pl.pallas_call(
        body, name=name, out_shape=jax.ShapeDtypeStruct((h, cc), F32),
        grid_spec=pltpu.PrefetchScalarGridSpec(
            num_scalar_prefetch=1, grid=(h // tr,),
            in_specs=[pl.BlockSpec((tr, cc), lambda i, chip_ref: (i, 0)),
                      pl.BlockSpec((3, tr, cc), lambda i, chip_ref: (0, i, 0))],
            out_specs=pl.BlockSpec((tr, cc), lambda i, chip_ref: (i, 0))),
        compiler_params=_params(("arbitrary",)),
    )(_chip_index_scalar(), part, rx)


def _share_halves(halves, name):
    n_arr = len(halves)

    def body(*refs):
        ins, outs = refs[:n_arr], refs[n_arr:2 * n_arr]
        send_sems, recv_sems = refs[2 * n_arr:]
        x, y, c = _my_pos()
        cps = []
        for a in range(n_arr):
            cp = pltpu.make_async_remote_copy(
                src_ref=ins[a], dst_ref=outs[a], send_sem=send_sems.at[a], recv_sem=recv_sems.at[a],
                device_id=(x, y, 1 - c), device_id_type=MESH)
            cp.start()
            cps.append(cp)
        for cp in cps:
            cp.wait()

    return pl.pallas_call(
        body, name=name, out_shape=[jax.ShapeDtypeStruct(h.shape, h.dtype) for h in halves],
        in_specs=[ANY] * n_arr, out_specs=[ANY] * n_arr,
        scratch_shapes=[pltpu.SemaphoreType.DMA((n_arr,)), pltpu.SemaphoreType.DMA((n_arr,))],
    )(*halves)


def _bucket_table():
    qi = jnp.arange(BLOCK)[:, None]
    si = jnp.arange(2 * BLOCK)[None, :]
    dist = qi + BLOCK - si
    max_exact = N_BUCKETS // 2
    n = jnp.maximum(dist, 0)
    nf = jnp.maximum(n, max_exact).astype(F32)
    large = max_exact + (jnp.log(nf / max_exact) / math.log(MAX_DISTANCE / max_exact)
                         * (N_BUCKETS - max_exact)).astype(jnp.int32)
    large = jnp.minimum(large, N_BUCKETS - 1)
    return jnp.where(n < max_exact, n, large).astype(F32)


def _prep_tables(bucket, rel_bias, w_s):
    def body(bucket_ref, rb_ref, ws_ref, bias_ref, wsm_ref):
        qi = lax.broadcasted_iota(jnp.int32, (BLOCK, 2 * BLOCK), 0)
        si = lax.broadcasted_iota(jnp.int32, (BLOCK, 2 * BLOCK), 1)
        dist = qi + BLOCK - si
        in_window = (dist >= 0) & (dist < BLOCK)
        bk = bucket_ref[...]
        for h in range(N_HEADS):
            acc = jnp.zeros((BLOCK, 2 * BLOCK), F32)
            for b in range(N_BUCKETS):
                acc = jnp.where(bk == float(b), rb_ref[b, h], acc)
            bias_ref[h] = jnp.where(in_window, acc, NEG_INF)
        ti = lax.broadcasted_iota(jnp.int32, (BLOCK, BLOCK), 0)
        ui = lax.broadcasted_iota(jnp.int32, (BLOCK, BLOCK), 1)
        for g in range(N_GROUPS):
            wsm_ref[g] = jnp.where(ti >= ui, ws_ref[g], 0.0).astype(BF16)

    return pl.pallas_call(
        body, name="prep_tables",
        out_shape=(jax.ShapeDtypeStruct((N_HEADS, BLOCK, 2 * BLOCK), F32),
                   jax.ShapeDtypeStruct((N_GROUPS, BLOCK, BLOCK), BF16)),
        grid=(1,),
        in_specs=[_const_spec((BLOCK, 2 * BLOCK)), pl.BlockSpec(memory_space=pltpu.SMEM),
                  _const_spec((N_GROUPS, BLOCK, BLOCK))],
        out_specs=(_const_spec((N_HEADS, BLOCK, 2 * BLOCK)), _const_spec((N_GROUPS, BLOCK, BLOCK))),
        compiler_params=_params(("arbitrary",)),
    )(bucket, rel_bias, w_s)


def _fwd_in(x, modr, w_in, b_in, tm, shards, kinds):
    s = x.shape[0]
    n_steps = s // tm
    fwd_step, diag_step = (8 * n_steps) // 16, (13 * n_steps) // 16
    n_w = len(shards)

    def body(x_ref, mod_ref, w_ref, b_ref, *rest):
        shard_refs = rest[:n_w]
        h1_ref, q_ref, kv_ref, gu_ref, gv_ref, xb_ref = rest[n_w:n_w + 6]
        gathered_refs = rest[n_w + 6:2 * n_w + 6]
        send_sems, recv_sems = rest[2 * n_w + 6:]
        i = pl.program_id(0)
        gather = _WeightGather(shard_refs, gathered_refs, kinds, send_sems, recv_sems)

        @pl.when(i == 0)
        def _():
            gather.start()

        xv = x_ref[...]
        xb_ref[...] = xv.astype(BF16)
        h1 = (xv * (1.0 + mod_ref[1:2, :]) + mod_ref[0:1, :]).astype(BF16)
        h1_ref[...] = h1
        proj = _dot(h1, w_ref[...]) + b_ref[...]
        q_ref[...] = (proj[:, :ATTN_W] * Q_SCALE).astype(BF16)
        kv_ref[...] = proj[:, ATTN_W:ATTN_W + 2 * KV_W].astype(BF16)
        gu_ref[...] = proj[:, ATTN_W + 2 * KV_W:ATTN_W + 2 * KV_W + GMLP_W]
        gv_ref[...] = proj[:, ATTN_W + 2 * KV_W + GMLP_W:]

        @pl.when(i == fwd_step)
        def _():
            gather.forward()

        @pl.when(i == diag_step)
        def _():
            gather.forward_diagonal()

        @pl.when(i == n_steps - 1)
        def _():
            gather.finish()

    row = lambda w: pl.BlockSpec((tm, w), lambda i: (i, 0))
    outs = pl.pallas_call(
        body, name="fwd_in",
        out_shape=[jax.ShapeDtypeStruct((s, D_MODEL), BF16), jax.ShapeDtypeStruct((s, ATTN_W), BF16),
                   jax.ShapeDtypeStruct((s, 2 * KV_W), BF16), jax.ShapeDtypeStruct((s, GMLP_W), F32),
                   jax.ShapeDtypeStruct((s, GMLP_W), F32), jax.ShapeDtypeStruct((s, D_MODEL), BF16)]
        + [jax.ShapeDtypeStruct(_gathered_shape(sh, k), BF16) for sh, k in zip(shards, kinds)],
        grid=(n_steps,),
        in_specs=[row(D_MODEL), _const_spec((8, D_MODEL)), _const_spec((D_MODEL, IN_W)), _const_spec((1, IN_W))]
        + [ANY] * n_w,
        out_specs=[row(D_MODEL), row(ATTN_W), row(2 * KV_W), row(GMLP_W), row(GMLP_W), row(D_MODEL)] + [ANY] * n_w,
        scratch_shapes=_WeightGather.sems(n_w),
        compiler_params=_params(("arbitrary",)),
    )(x, modr, w_in, b_in, *shards)
    return outs[:6], outs[6:]


def _kv_variants(kk):
    kf = kk.astype(F32)
    lane = lax.broadcasted_iota(jnp.int32, kf.shape, 1)
    low = lane < HEAD_DIM
    k0_lo = jnp.where(low, kf, 0.0)
    k1_hi = jnp.where(low, 0.0, kf)
    k0_hi = pltpu.roll(k0_lo, HEAD_DIM, 1)
    k1_lo = pltpu.roll(k1_hi, HEAD_DIM, 1)
    return ((k0_lo.astype(BF16), k0_hi.astype(BF16)), (k1_lo.astype(BF16), k1_hi.astype(BF16)))


def _head_kv(h):
    return h // (N_HEADS // N_KV), h % 2


MIX_GROUP = 2


def _interleave(*gens):
    results = [None] * len(gens)
    active = list(enumerate(gens))
    while active:
        still = []
        for i, g in active:
            try:
                next(g)
                still.append((i, g))
            except StopIteration as done:
                results[i] = done.value
        active = still
    return results


def _attn_block_fwd(q_blk, kk, vv, bias_ref, sinks_ref, first_mask):
    kvar = _kv_variants(kk)
    vvar = _kv_variants(vv)
    heads = range(N_HEADS)
    q_pairs = [q_blk[:, (h // 2) * LANES:(h // 2 + 1) * LANES] for h in heads]
    logits = [_dot_nt(q_pairs[h], kvar[_head_kv(h)[0]][_head_kv(h)[1]]) + bias_ref[h] for h in heads]
    if first_mask is not None:
        logits = [jnp.where(first_mask, NEG_INF, lg) for lg in logits]
    yield
    ms = [jnp.maximum(jnp.max(logits[h], axis=-1, keepdims=True), sinks_ref[h]) for h in heads]
    yield
    es = [jnp.exp(logits[h] - ms[h]) for h in heads]
    ess = [jnp.exp(sinks_ref[h] - ms[h]) for h in heads]
    yield
    invs = [1.0 / (jnp.sum(es[h], axis=-1, keepdims=True) + ess[h]) for h in heads]
    probs = [(es[h] * invs[h], ess[h] * invs[h]) for h in heads]
    yield
    outs = [_dot(probs[h][0].astype(BF16), vvar[_head_kv(h)[0]][_head_kv(h)[1]]) for h in heads]
    pairs = [outs[2 * i] + outs[2 * i + 1] for i in range(N_HEADS // 2)]
    return jnp.concatenate(pairs, axis=1), probs, kvar, vvar


def _gmlp_chunk_fwd(gu, gv, ln_g, ln_b, wsm_ref, bsx, amat):
    u, tu = _gelu(gu)
    a, ta = _gelu(gv)
    yield
    mean = _split_dot(a, amat)
    d = a - mean
    yield
    var = _split_dot(d * d, amat)
    yield
    rstd = lax.rsqrt(var + LN_EPS)
    xhat = d * rstd
    vb = (xhat * ln_g + ln_b).astype(BF16)
    yield
    lane = lax.broadcasted_iota(jnp.int32, (BLOCK, LANES), 1)
    low = lane < GROUP_DIM
    cols = []
    for pair in range(N_GROUPS // 2):
        vp = vb[:, pair * LANES:(pair + 1) * LANES]
        cols.append(jnp.where(low, _dot(wsm_ref[2 * pair], vp), _dot(wsm_ref[2 * pair + 1], vp)))
    mixedv = jnp.concatenate(cols, axis=1) + bsx
    return u * mixedv, (u, tu, ta, xhat, rstd, vb, mixedv)


def _rms(a, g):
    r = lax.rsqrt(jnp.mean(a * a, axis=-1, keepdims=True) + LN_EPS)
    return a * r * g, r


def _fwd_mix(q, kv, gu, gv, x, modr, bias, sinks, gln_g, gln_b, wsm, bsx, amat, aog, gog, w_out, ln1_g, ln1_b, tm,
             ffn_shards, ffn_kinds):
    s = x.shape[0]
    nb = tm // BLOCK
    n_steps = s // tm
    fwd_step, diag_step = (7 * n_steps) // 16, (12 * n_steps) // 16
    n_w = len(ffn_shards)

    def body(q_ref, kv_ref, kvp_ref, gu_ref, gv_ref, x_ref, mod_ref, bias_ref, sinks_ref, glng_ref, glnb_ref, wsm_ref,
             bsx_ref, amat_ref, aog_ref, gog_ref, wout_ref, ln1g_ref, ln1b_ref, *rest):
        shard_refs = rest[:n_w]
        x1_ref, x1b_ref, y_ref, mixed_ref = rest[n_w:n_w + 4]
        gathered_refs = rest[n_w + 4:2 * n_w + 4]
        mix_scr, send_sems, recv_sems = rest[2 * n_w + 4:]
        i = pl.program_id(0)
        gather = _WeightGather(shard_refs, gathered_refs, ffn_kinds, send_sems, recv_sems)

        @pl.when(i == 0)
        def _():
            gather.start()

        col = lax.broadcasted_iota(jnp.int32, (BLOCK, 2 * BLOCK), 1)
        for b0 in range(0, nb, MIX_GROUP):
            gens = []
            for b in range(b0, min(b0 + MIX_GROUP, nb)):
                r0 = b * BLOCK
                if b == 0:
                    kvprev = kvp_ref[...]
                    first_mask = (col < BLOCK) & (i == 0)
                else:
                    kvprev = kv_ref[r0 - BLOCK:r0, :]
                    first_mask = None
                kvcur = kv_ref[r0:r0 + BLOCK, :]
                kk = jnp.concatenate([kvprev[:, :KV_W], kvcur[:, :KV_W]], axis=0)
                vv = jnp.concatenate([kvprev[:, KV_W:], kvcur[:, KV_W:]], axis=0)
                gens.append(_attn_block_fwd(q_ref[r0:r0 + BLOCK, :], kk, vv, bias_ref, sinks_ref, first_mask))
                gens.append(_gmlp_chunk_fwd(gu_ref[r0:r0 + BLOCK, :], gv_ref[r0:r0 + BLOCK, :], glng_ref[...],
                                            glnb_ref[...], wsm_ref, bsx_ref[...], amat_ref[...]))
            res = _interleave(*gens)
            for k, b in enumerate(range(b0, min(b0 + MIX_GROUP, nb))):
                r0 = b * BLOCK
                na, _ = _rms(res[2 * k][0], aog_ref[...])
                ng, _ = _rms(res[2 * k + 1][0], gog_ref[...])
                mix_scr[r0:r0 + BLOCK, :ATTN_W] = na.astype(BF16)
                mix_scr[r0:r0 + BLOCK, ATTN_W:] = ng.astype(BF16)
        mixed = mix_scr[...]
        mixed_ref[...] = mixed
        y = _dot(mixed, wout_ref[...])
        y_ref[...] = y.astype(BF16)
        z1 = ALPHA * x_ref[...] + mod_ref[2:3, :] * y
        xhat, _ = _ln_stats(z1)
        x1 = xhat * ln1g_ref[...] + ln1b_ref[...]
        x1_ref[...] = x1
        x1b_ref[...] = x1.astype(BF16)

        @pl.when(i == fwd_step)
        def _():
            gather.forward()

        @pl.when(i == diag_step)
        def _():
            gather.forward_diagonal()

        @pl.when(i == n_steps - 1)
        def _():
            gather.finish()

    row = lambda w: pl.BlockSpec((tm, w), lambda i: (i, 0))
    prev = pl.BlockSpec((BLOCK, 2 * KV_W), lambda i: (jnp.maximum(i * nb - 1, 0), 0))
    outs = pl.pallas_call(
        body, name="fwd_mix",
        out_shape=[jax.ShapeDtypeStruct((s, D_MODEL), F32)] + [jax.ShapeDtypeStruct((s, D_MODEL), BF16)] * 3
        + [jax.ShapeDtypeStruct(_gathered_shape(sh, k), BF16) for sh, k in zip(ffn_shards, ffn_kinds)],
        grid=(n_steps,),
        in_specs=[row(ATTN_W), row(2 * KV_W), prev, row(GMLP_W), row(GMLP_W), row(D_MODEL), _const_spec((8, D_MODEL)),
                  _const_spec((N_HEADS, BLOCK, 2 * BLOCK)), pl.BlockSpec(memory_space=pltpu.SMEM),
                  _const_spec((1, GMLP_W)), _const_spec((1, GMLP_W)), _const_spec((N_GROUPS, BLOCK, BLOCK)),
                  _const_spec((BLOCK, GMLP_W)), _const_spec((GMLP_W, GMLP_W)), _const_spec((1, ATTN_W)),
                  _const_spec((1, GMLP_W)), _const_spec((D_MODEL, D_MODEL)), _const_spec((1, D_MODEL)),
                  _const_spec((1, D_MODEL))] + [ANY] * n_w,
        out_specs=[row(D_MODEL)] * 4 + [ANY] * n_w,
        scratch_shapes=[pltpu.VMEM((tm, D_MODEL), BF16)] + _WeightGather.sems(n_w),
        compiler_params=_params(("arbitrary",)),
    )(q, kv, kv, gu, gv, x, modr, bias, sinks, gln_g, gln_b, wsm, bsx, amat, aog, gog, w_out, ln1_g, ln1_b, *ffn_shards)
    return outs[:4], outs[4:]


FF_BLOCKS = N_CHIPS // 2
FF_CHUNK = D_FF // FF_BLOCKS
FFN_SUB = 256


def _sigmoid(x):
    return 1.0 / (1.0 + jnp.exp(-x))


def _fwd_ffn(x1, target, modr, ln2_g, ln2_b, w_gu, w_dn, tm):
    s = x1.shape[0]

    def body(x1_ref, t_ref, mod_ref, g_ref, b_ref, wgu_ref, wdn_ref, h2_ref, act_ref, dy2_ref, dx1a_ref, acc_ref):
        @pl.when(pl.program_id(0) == 0)
        def _():
            acc_ref[...] = jnp.zeros_like(acc_ref)

        x1v = x1_ref[...]
        h2 = (x1v * (1.0 + mod_ref[4:5, :]) + mod_ref[3:4, :]).astype(BF16)
        h2_ref[...] = h2
        y2 = None
        for cc in range(FF_BLOCKS):
            c0 = cc * FF_CHUNK
            gate = _dot(h2, wgu_ref[cc])
            up = _dot(h2, wgu_ref[FF_BLOCKS + cc])
            act_ref[:, c0:c0 + FF_CHUNK] = gate.astype(BF16)
            act_ref[:, D_FF + c0:D_FF + c0 + FF_CHUNK] = up.astype(BF16)
            a = (gate * _sigmoid(gate) * up).astype(BF16)
            part = _dot(a, wdn_ref[c0:c0 + FF_CHUNK, :])
            y2 = part if y2 is None else y2 + part
        g2 = mod_ref[5:6, :]
        z2 = ALPHA * x1v + g2 * y2
        xhat, rstd = _ln_stats(z2)
        gain = g_ref[...]
        diff = xhat * gain + b_ref[...] - t_ref[...]
        dx2 = diff * (1.0 / D_MODEL)
        dz2 = _ln_bwd(dx2 * gain, xhat, rstd)
        dx1a_ref[...] = ALPHA * dz2
        dy2_ref[...] = (g2 * dz2).astype(BF16)
        acc_ref[0:1, :] += _colsum(diff * diff)
        acc_ref[1:2, :] += _colsum(dx2 * xhat)
        acc_ref[2:3, :] += _colsum(dx2)
        acc_ref[3:4, :] += _colsum(dz2 * y2)

    row = lambda w: pl.BlockSpec((tm, w), lambda i: (i, 0))
    return pl.pallas_call(
        body, name="fwd_ffn",
        out_shape=(jax.ShapeDtypeStruct((s, D_MODEL), BF16), jax.ShapeDtypeStruct((s, 2 * D_FF), BF16),
                   jax.ShapeDtypeStruct((s, D_MODEL), BF16), jax.ShapeDtypeStruct((s, D_MODEL), F32),
                   jax.ShapeDtypeStruct((8, D_MODEL), F32)),
        grid=(s // tm,),
        in_specs=[row(D_MODEL), row(D_MODEL), _const_spec((8, D_MODEL)), _const_spec((1, D_MODEL)),
                  _const_spec((1, D_MODEL)), _const_spec((N_CHIPS, D_MODEL, FF_CHUNK), single=True),
                  _const_spec((D_FF, D_MODEL), single=True)],
        out_specs=(row(D_MODEL), row(2 * D_FF), row(D_MODEL), row(D_MODEL), _const_spec((8, D_MODEL))),
        compiler_params=_params(("arbitrary",)),
    )(x1, target, modr, ln2_g, ln2_b, w_gu, w_dn)


def _bwd_ffn(dy2, act, w_gu, w_dn, tm):
    s = dy2.shape[0]

    def body(dy2_ref, act_ref, wgu_ref, wdn_ref, a_ref, dgu_ref, dh2_ref):
        dy2v = dy2_ref[...]
        dh2 = None
        for cc in range(FF_BLOCKS):
            c0 = cc * FF_CHUNK
            da = _dot_nt(dy2v, wdn_ref[c0:c0 + FF_CHUNK, :])
            gate = act_ref[:, c0:c0 + FF_CHUNK].astype(F32)
            up = act_ref[:, D_FF + c0:D_FF + c0 + FF_CHUNK].astype(F32)
            sg = _sigmoid(gate)
            sl = gate * sg
            a_ref[:, c0:c0 + FF_CHUNK] = (sl * up).astype(BF16)
            dgate = (da * up * (sg * (1.0 + gate * (1.0 - sg)))).astype(BF16)
            dup = (da * sl).astype(BF16)
            dgu_ref[:, c0:c0 + FF_CHUNK] = dgate
            dgu_ref[:, D_FF + c0:D_FF + c0 + FF_CHUNK] = dup
            part = _dot_nt(dgate, wgu_ref[cc]) + _dot_nt(dup, wgu_ref[FF_BLOCKS + cc])
            dh2 = part if dh2 is None else dh2 + part
        dh2_ref[...] = dh2.astype(BF16)

    row = lambda w: pl.BlockSpec((tm, w), lambda i: (i, 0))
    return pl.pallas_call(
        body, name="bwd_ffn",
        out_shape=(jax.ShapeDtypeStruct((s, D_FF), BF16), jax.ShapeDtypeStruct((s, 2 * D_FF), BF16),
                   jax.ShapeDtypeStruct((s, D_MODEL), BF16)),
        grid=(s // tm,),
        in_specs=[row(D_MODEL), row(2 * D_FF), _const_spec((N_CHIPS, D_MODEL, FF_CHUNK), single=True),
                  _const_spec((D_FF, D_MODEL), single=True)],
        out_specs=(row(D_FF), row(2 * D_FF), row(D_MODEL)),
        compiler_params=_params(("parallel",)),
    )(dy2, act, w_gu, w_dn)


def _bwd_mid(dh2, dx1a, x1, x, y, modr, ln1_g, w_out, tm, swap_fulls, swap_kinds):
    s = x.shape[0]
    n_steps = s // tm
    n_g = len(swap_fulls)

    def body(dh2_ref, dx1a_ref, x1_ref, x_ref, y_ref, mod_ref, g_ref, wout_ref, *rest):
        full_refs = rest[:n_g]
        dxa_ref, dy_ref, dmix_ref, acc_ref = rest[n_g:n_g + 4]
        got_refs = rest[n_g + 4:2 * n_g + 4]
        swap = _HalfSwap(full_refs, got_refs, swap_kinds, *rest[2 * n_g + 4:])
        i = pl.program_id(0)

        @pl.when(i == 0)
        def _():
            swap.start()
            acc_ref[...] = jnp.zeros_like(acc_ref)

        dh2 = dh2_ref[...].astype(F32)
        x1v = x1_ref[...].astype(F32)
        yv = y_ref[...].astype(F32)
        g1 = mod_ref[2:3, :]
        dx1 = dx1a_ref[...] + dh2 * (1.0 + mod_ref[4:5, :])
        z1 = ALPHA * x_ref[...] + g1 * yv
        xhat, rstd = _ln_stats(z1)
        dz1 = _ln_bwd(dx1 * g_ref[...], xhat, rstd)
        dxa_ref[...] = (ALPHA * dz1).astype(BF16)
        dy = (g1 * dz1).astype(BF16)
        dy_ref[...] = dy
        dmix_ref[...] = _dot_nt(dy, wout_ref[...]).astype(BF16)
        acc_ref[0:1, :] += _colsum(dh2 * x1v)
        acc_ref[1:2, :] += _colsum(dh2)
        acc_ref[2:3, :] += _colsum(dx1 * xhat)
        acc_ref[3:4, :] += _colsum(dx1)
        acc_ref[4:5, :] += _colsum(dz1 * yv)

        @pl.when(i == n_steps - 1)
        def _():
            swap.wait()

    row = lambda w: pl.BlockSpec((tm, w), lambda i: (i, 0))
    outs = pl.pallas_call(
        body, name="bwd_mid",
        out_shape=[jax.ShapeDtypeStruct((s, D_MODEL), BF16), jax.ShapeDtypeStruct((s, D_MODEL), BF16),
                   jax.ShapeDtypeStruct((s, D_MODEL), BF16), jax.ShapeDtypeStruct((8, D_MODEL), F32)]
        + _HalfSwap.out_shapes(swap_fulls, swap_kinds),
        grid=(n_steps,),
        in_specs=[row(D_MODEL)] * 5 + [_const_spec((8, D_MODEL)), _const_spec((1, D_MODEL)),
                                       _const_spec((D_MODEL, D_MODEL))] + [ANY] * n_g,
        out_specs=[row(D_MODEL), row(D_MODEL), row(D_MODEL), _const_spec((8, D_MODEL))] + [ANY] * n_g,
        scratch_shapes=_HalfSwap.sems(n_g),
        compiler_params=_params(("arbitrary",)),
    )(dh2, dx1a, x1, x, y, modr, ln1_g, w_out, *swap_fulls)
    return outs[:4], outs[4:]


def _fold_kv(t0, t1):
    lane = lax.broadcasted_iota(jnp.int32, t0.shape, 1)
    f0 = t0 + pltpu.roll(t0, HEAD_DIM, 1)
    f1 = t1 + pltpu.roll(t1, HEAD_DIM, 1)
    return jnp.where(lane < HEAD_DIM, f0, f1)


def _bwd_mix(q, kv, gu, gv, dmix, bias, sinks, gln_g, gln_b, wsm, bsx, amat, aog, gog, grad_parts, grad_kinds):
    s = q.shape[0]
    tile = 2 * BLOCK
    n_steps = s // tile
    n_g = len(grad_parts)

    def body(q_ref, kv_ref, kvp_ref, gu_ref, gv_ref, dmix_ref, bias_ref, sinks_ref, glng_ref, glnb_ref, wsm_ref,
             bsx_ref, amat_ref, aog_ref, gog_ref, *rest):
        part_refs = rest[:n_g]
        dq_ref, dkv_ref, dgu_ref, dgv_ref, gbias_ref, dws_ref, dbs_ref, vec_ref, dsink_ref = rest[n_g:n_g + 9]
        rx_refs = rest[n_g + 9:2 * n_g + 9]
        carry, done, send_sems, recv_sems = rest[2 * n_g + 9:]
        n = pl.program_id(0)
        exchange = _ChipExchange(part_refs, rx_refs, grad_kinds, send_sems, recv_sems)

        @pl.when(n == 0)
        def _():
            exchange.start()
            carry[...] = jnp.zeros_like(carry)
            done[...] = jnp.zeros_like(done)
            gbias_ref[...] = jnp.zeros_like(gbias_ref)
            dws_ref[...] = jnp.zeros_like(dws_ref)
            dbs_ref[...] = jnp.zeros_like(dbs_ref)
            vec_ref[...] = jnp.zeros_like(vec_ref)
            dsink_ref[...] = jnp.zeros_like(dsink_ref)

        @pl.when(n == n_steps)
        def _():
            dkv_ref[:BLOCK, :] = done[...].astype(BF16)
            dkv_ref[BLOCK:, :] = carry[...].astype(BF16)
            exchange.wait()

        @pl.when(n < n_steps)
        def _():
            col = lax.broadcasted_iota(jnp.int32, (BLOCK, 2 * BLOCK), 1)
            lane = lax.broadcasted_iota(jnp.int32, (BLOCK, LANES), 1)
            low = lane < HEAD_DIM
            rows = [slice(0, BLOCK), slice(BLOCK, tile)]
            kv_blocks = [kvp_ref[...], kv_ref[rows[0], :], kv_ref[rows[1], :]]
            masks = [(col < BLOCK) & (n == 0), None]
            q_blks = [q_ref[r, :] for r in rows]
            fwd = []
            for b in range(2):
                kk = jnp.concatenate([kv_blocks[b][:, :KV_W], kv_blocks[b + 1][:, :KV_W]], axis=0)
                vv = jnp.concatenate([kv_blocks[b][:, KV_W:], kv_blocks[b + 1][:, KV_W:]], axis=0)
                fwd.append(_attn_block_fwd(q_blks[b], kk, vv, bias_ref, sinks_ref, masks[b]))
                fwd.append(_gmlp_chunk_fwd(gu_ref[rows[b], :], gv_ref[rows[b], :], glng_ref[...], glnb_ref[...],
                                           wsm_ref, bsx_ref[...], amat_ref[...]))
            res = _interleave(*fwd[:2]) + _interleave(*fwd[2:])

            def gating_bwd(b, d_gm, saved):
                u, tu, ta, xhat, rstd, vb, mixedv = saved
                dgu_ref[rows[b], :] = (d_gm * mixedv * _gelu_grad(gu_ref[rows[b], :], tu)).astype(BF16)
                dmx = d_gm * u
                dmxb = dmx.astype(BF16)
                yield
                dvn_cols, dws = [], []
                for pair in range(N_GROUPS // 2):
                    dp_ = dmxb[:, pair * LANES:(pair + 1) * LANES]
                    vp = vb[:, pair * LANES:(pair + 1) * LANES]
                    dvn_cols.append(
                        jnp.where(low, _dot_tn(wsm_ref[2 * pair], dp_), _dot_tn(wsm_ref[2 * pair + 1], dp_)))
                    zero = jnp.zeros_like(dp_)
                    dws.append(_dot_nt(jnp.where(low, dp_, zero), vp))
                    dws.append(_dot_nt(jnp.where(low, zero, dp_), vp))
                dvn = jnp.concatenate(dvn_cols, axis=1)
                yield
                dxh = dvn * glng_ref[...]
                am = amat_ref[...]
                m1 = _split_dot(dxh, am)
                m2 = _split_dot(dxh * xhat, am)
                yield
                da = rstd * (dxh - m1 - xhat * m2)
                dgv_ref[rows[b], :] = (da * _gelu_grad(gv_ref[rows[b], :], ta)).astype(BF16)
                return dmx, dws, _colsum(dvn * xhat), _colsum(dvn)

            def attention_bwd(b, d_attn, probs, kvar, vvar):
                heads = range(N_HEADS)
                sels = [low if h % 2 == 0 else jnp.logical_not(low) for h in heads]
                pair_of = lambda a, h: a[:, (h // 2) * LANES:(h // 2 + 1) * LANES]
                do_hs = [jnp.where(sels[h], pair_of(d_attn, h), 0.0).astype(BF16) for h in heads]
                q_hs = [jnp.where(sels[h], pair_of(q_blks[b], h), jnp.zeros((BLOCK, LANES), BF16)) for h in heads]
                dps = [_dot_nt(do_hs[h], vvar[_head_kv(h)[0]][_head_kv(h)[1]]) for h in heads]
                yield
                deltas = [jnp.sum(probs[h][0] * dps[h], axis=-1, keepdims=True) for h in heads]
                yield
                dss = [probs[h][0] * (dps[h] - deltas[h]) for h in heads]
                dsinks = [-(probs[h][1] * deltas[h]) for h in heads]
                dsbs = [ds.astype(BF16) for ds in dss]
                pbs = [probs[h][0].astype(BF16) for h in heads]
                yield
                dqs = [_dot(dsbs[h], kvar[_head_kv(h)[0]][_head_kv(h)[1]]) for h in heads]
                tks = [_dot_tn(dsbs[h], q_hs[h]) for h in heads]
                tvs = [_dot_tn(pbs[h], do_hs[h]) for h in heads]
                dq_cols = [dqs[2 * i] + dqs[2 * i + 1] for i in range(N_HEADS // 2)]
                dq_ref[rows[b], :] = (jnp.concatenate(dq_cols, axis=1) * Q_SCALE).astype(BF16)
                per_kv = N_HEADS // N_KV
                kv_sum = lambda ts, kvh: sum(ts[kvh * per_kv + 1:(kvh + 1) * per_kv], ts[kvh * per_kv])
                dkk = _fold_kv(kv_sum(tks, 0), kv_sum(tks, 1))
                dvv = _fold_kv(kv_sum(tvs, 0), kv_sum(tvs, 1))
                return jnp.concatenate([dkk, dvv], axis=1), dss, dsinks

            bwd, rms_g = [], []
            for b in range(2):
                attn, probs, kvar, vvar = res[2 * b]
                gm, saved = res[2 * b + 1]
                na_unit, r_a = _rms(attn, 1.0)
                ng_unit, r_g = _rms(gm, 1.0)
                dmix = dmix_ref[rows[b], :].astype(F32)
                dn_a = dmix[:, :ATTN_W]
                dn_g = dmix[:, ATTN_W:]
                rms_g.append((_colsum(dn_a * na_unit), _colsum(dn_g * ng_unit)))
                t_a = dn_a * aog_ref[...]
                d_attn = r_a * t_a - na_unit * (r_a * jnp.mean(t_a * na_unit, axis=-1, keepdims=True))
                t_g = dn_g * gog_ref[...]
                d_gm = r_g * t_g - ng_unit * (r_g * jnp.mean(t_g * ng_unit, axis=-1, keepdims=True))
                bwd.append(attention_bwd(b, d_attn, probs, kvar, vvar))
                bwd.append(gating_bwd(b, d_gm, saved))
            (dkv_a, dss_a, dsk_a), (dmx_a, dws_a, glg_a, glb_a) = _interleave(*bwd[:2])
            (dkv_b, dss_b, dsk_b), (dmx_b, dws_b, glg_b, glb_b) = _interleave(*bwd[2:])

            vec_ref[0:1, :] += rms_g[0][0] + rms_g[1][0]
            vec_ref[1:2, :] += rms_g[0][1] + rms_g[1][1]
            vec_ref[2:3, :] += glg_a + glg_b
            vec_ref[3:4, :] += glb_a + glb_b
            dbs_ref[...] += dmx_a + dmx_b
            for g in range(N_GROUPS):
                dws_ref[g] += dws_a[g] + dws_b[g]
            for h in range(N_HEADS):
                gbias_ref[h] += dss_a[h] + dss_b[h]
                dsink_ref[h] += dsk_a[h] + dsk_b[h]

            dkv_ref[:BLOCK, :] = done[...].astype(BF16)
            dkv_ref[BLOCK:, :] = (carry[...] + dkv_a[:BLOCK]).astype(BF16)
            done[...] = dkv_a[BLOCK:] + dkv_b[:BLOCK]
            carry[...] = dkv_b[BLOCK:]

    last = n_steps - 1
    cur = lambda w: pl.BlockSpec((tile, w), lambda n: (jnp.minimum(n, last), 0))
    late = lambda w: pl.BlockSpec((tile, w), lambda n: (jnp.clip(n - 1, 0, last), 0))
    before = pl.BlockSpec((BLOCK, 2 * KV_W), lambda n: (jnp.clip(2 * n - 1, 0, 2 * last + 1), 0))
    outs = pl.pallas_call(
        body, name="bwd_mix",
        out_shape=[jax.ShapeDtypeStruct((s, ATTN_W), BF16), jax.ShapeDtypeStruct((s, 2 * KV_W), BF16),
                   jax.ShapeDtypeStruct((s, GMLP_W), BF16), jax.ShapeDtypeStruct((s, GMLP_W), BF16),
                   jax.ShapeDtypeStruct((N_HEADS, BLOCK, 2 * BLOCK), F32),
                   jax.ShapeDtypeStruct((N_GROUPS, BLOCK, BLOCK), F32),
                   jax.ShapeDtypeStruct((BLOCK, GMLP_W), F32), jax.ShapeDtypeStruct((8, GMLP_W), F32),
                   jax.ShapeDtypeStruct((N_HEADS, BLOCK, 1), F32)]
        + [jax.ShapeDtypeStruct(_rx_shape(p.shape, k), BF16) for p, k in zip(grad_parts, grad_kinds)],
        grid=(n_steps + 1,),
        in_specs=[cur(ATTN_W), cur(2 * KV_W), before, cur(GMLP_W), cur(GMLP_W), cur(D_MODEL),
                  _const_spec((N_HEADS, BLOCK, 2 * BLOCK)), pl.BlockSpec(memory_space=pltpu.SMEM),
                  _const_spec((1, GMLP_W)), _const_spec((1, GMLP_W)), _const_spec((N_GROUPS, BLOCK, BLOCK)),
                  _const_spec((BLOCK, GMLP_W)), _const_spec((GMLP_W, GMLP_W)), _const_spec((1, ATTN_W)),
                  _const_spec((1, GMLP_W))] + [ANY] * n_g,
        out_specs=[cur(ATTN_W), late(2 * KV_W), cur(GMLP_W), cur(GMLP_W),
                   _const_spec((N_HEADS, BLOCK, 2 * BLOCK)), _const_spec((N_GROUPS, BLOCK, BLOCK)),
                   _const_spec((BLOCK, GMLP_W)), _const_spec((8, GMLP_W)), _const_spec((N_HEADS, BLOCK, 1))]
        + [ANY] * n_g,
        scratch_shapes=[pltpu.VMEM((BLOCK, 2 * KV_W), F32), pltpu.VMEM((BLOCK, 2 * KV_W), F32)]
        + _ChipExchange.sems(n_g),
        compiler_params=_params(("arbitrary",)),
    )(q, kv, kv, gu, gv, dmix, bias, sinks, gln_g, gln_b, wsm, bsx, amat, aog, gog, *grad_parts)
    return outs[:9], outs[9:]


def _mix_finalize(gbias, bucket, dws, dbs, dsink):
    def body(gb_ref, bucket_ref, dws_ref, dbs_ref, dsink_ref, tall_ref):
        bk = bucket_ref[...]
        lane = lax.broadcasted_iota(jnp.int32, (N_BUCKETS, LANES), 1)
        rowi = lax.broadcasted_iota(jnp.int32, (N_BUCKETS, LANES), 0)
        drb = jnp.zeros((N_BUCKETS, LANES), F32)
        dsk = jnp.zeros((8, LANES), F32)
        lane8 = lax.broadcasted_iota(jnp.int32, (8, LANES), 1)
        for h in range(N_HEADS):
            g = gb_ref[h]
            for b in range(N_BUCKETS):
                tot = jnp.sum(_colsum(jnp.where(bk == float(b), g, 0.0)), axis=1, keepdims=True)
                drb = jnp.where((lane == h) & (rowi == b), tot, drb)
            sk = jnp.sum(dsink_ref[h], axis=0, keepdims=True)
            dsk = jnp.where(lane8 == h, sk, dsk)
        tall_ref[TALL_RB:TALL_RB + N_BUCKETS, :] = drb
        tall_ref[TALL_SK:TALL_SK + 8, :] = dsk
        ti = lax.broadcasted_iota(jnp.int32, (BLOCK, BLOCK), 0)
        ui = lax.broadcasted_iota(jnp.int32, (BLOCK, BLOCK), 1)
        for g in range(N_GROUPS):
            tall_ref[g * BLOCK:(g + 1) * BLOCK, :] = jnp.where(ti >= ui, dws_ref[g], 0.0)
        gi = lax.broadcasted_iota(jnp.int32, (GMLP_W, LANES), 0) // GROUP_DIM
        li = lax.broadcasted_iota(jnp.int32, (GMLP_W, LANES), 1)
        ind = jnp.where(gi == li, 1.0, 0.0).astype(BF16)
        d = dbs_ref[...]
        hi = d.astype(BF16)
        r1 = d - hi.astype(F32)
        mid = r1.astype(BF16)
        lo = (r1 - mid.astype(F32)).astype(BF16)
        dbsg = _dot(hi, ind) + _dot(mid, ind) + _dot(lo, ind)
        tall_ref[TALL_BS:TALL_BS + N_GROUPS, :] = dbsg.T[:N_GROUPS, :]

    return pl.pallas_call(
        body, name="mix_finalize", out_shape=jax.ShapeDtypeStruct((TALL_ROWS, LANES), F32), grid=(1,),
        in_specs=[_const_spec((N_HEADS, BLOCK, 2 * BLOCK)), _const_spec((BLOCK, 2 * BLOCK)),
                  _const_spec((N_GROUPS, BLOCK, BLOCK)), _const_spec((BLOCK, GMLP_W)),
                  _const_spec((N_HEADS, BLOCK, 1))],
        out_specs=_const_spec((TALL_ROWS, LANES)),
        compiler_params=_params(("arbitrary",)),
    )(gbias, bucket, dws, dbs, dsink)


def _bwd_in(dq, dkv, dgu, dgv, dxa, x, modr, w_in, tm):
    s = x.shape[0]

    def body(dq_ref, dkv_ref, dgu_ref, dgv_ref, dxa_ref, x_ref, mod_ref, w_ref, gx_ref, acc_ref, db_ref):
        @pl.when(pl.program_id(0) == 0)
        def _():
            acc_ref[...] = jnp.zeros_like(acc_ref)
            db_ref[...] = jnp.zeros_like(db_ref)

        dproj = jnp.concatenate([dq_ref[...], dkv_ref[...], dgu_ref[...], dgv_ref[...]], axis=1)
        dh1 = _dot_nt(dproj, w_ref[...])
        gx_ref[...] = dxa_ref[...].astype(F32) + dh1 * (1.0 + mod_ref[1:2, :])
        acc_ref[0:1, :] += _colsum(dh1 * x_ref[...].astype(F32))
        acc_ref[1:2, :] += _colsum(dh1)
        db_ref[0:1, :] += _colsum(dproj.astype(F32))

    row = lambda w: pl.BlockSpec((tm, w), lambda i: (i, 0))
    return pl.pallas_call(
        body, name="bwd_in",
        out_shape=(jax.ShapeDtypeStruct((s, D_MODEL), F32), jax.ShapeDtypeStruct((8, D_MODEL), F32),
                   jax.ShapeDtypeStruct((8, IN_W), F32)),
        grid=(s // tm,),
        in_specs=[row(ATTN_W), row(2 * KV_W), row(GMLP_W), row(GMLP_W), row(D_MODEL), row(D_MODEL),
                  _const_spec((8, D_MODEL)), _const_spec((D_MODEL, IN_W))],
        out_specs=(row(D_MODEL), _const_spec((8, D_MODEL)), _const_spec((8, IN_W))),
        compiler_params=_params(("arbitrary",)),
    )(dq, dkv, dgu, dgv, dxa, x, modr, w_in)


def _wgrad(a, bs, tm, tk, name, owner_blocks=False, gather_vs=()):
    k_all, m = a.shape
    n = sum(b.shape[1] for b in bs)
    nk = k_all // tk
    nm = m // tm
    n_b = len(bs)
    n_v = len(gather_vs)
    wb = n // N_CHIPS

    def body(a_ref, *rest):
        b_refs, v_refs = rest[:n_b], rest[n_b:n_b + n_v]
        o_ref, ob_ref = rest[n_b + n_v:n_b + n_v + 2]
        vg_refs = rest[n_b + n_v + 2:n_b + 2 * n_v + 2]
        i, k = pl.program_id(0), pl.program_id(1)
        if n_v:
            gather = _Gather8(v_refs, vg_refs, *rest[n_b + 2 * n_v + 2:])

            @pl.when((i == 0) & (k == 0))
            def _():
                gather.start()

            @pl.when((i == nm - 1) & (k == 0))
            def _():
                gather.forward()

        @pl.when(k == 0)
        def _():
            o_ref[...] = jnp.zeros_like(o_ref)

        b = b_refs[0][...] if n_b == 1 else jnp.concatenate([r[...] for r in b_refs], axis=1)
        if owner_blocks:
            av = a_ref[...]
            for j in range(N_CHIPS):
                o_ref[j] += _dot_tn(av, b[:, j * wb:(j + 1) * wb])
        else:
            o_ref[...] += _dot_tn(a_ref[...], b)

        @pl.when(k == nk - 1)
        def _():
            ob_ref[...] = o_ref[...].astype(BF16)

        if n_v:
            @pl.when((i == nm - 1) & (k == nk - 1))
            def _():
                gather.finish()

    if owner_blocks:
        out_spec = pl.BlockSpec((N_CHIPS, tm, wb), lambda i, k: (0, i, 0))
        shape = (N_CHIPS, m, wb)
    else:
        out_spec = pl.BlockSpec((tm, n), lambda i, k: (i, 0))
        shape = (m, n)
    outs = pl.pallas_call(
        body, name=name,
        out_shape=[jax.ShapeDtypeStruct(shape, F32), jax.ShapeDtypeStruct(shape, BF16)] + _gathered8_shapes(gather_vs),
        grid=(nm, nk),
        in_specs=[pl.BlockSpec((tk, tm), lambda i, k: (k, i))]
        + [pl.BlockSpec((tk, b.shape[1]), lambda i, k: (k, 0)) for b in bs] + [ANY] * n_v,
        out_specs=[out_spec, out_spec] + [ANY] * n_v,
        scratch_shapes=_Gather8.sems(n_v) if n_v else [],
        compiler_params=_params(("arbitrary", "arbitrary") if n_v else ("parallel", "arbitrary")),
    )(a, *bs, *gather_vs)
    return outs[0], outs[1], outs[2:]


def _adam_math(w, g, m, v):
    m2 = ADAM_B1 * m + (1.0 - ADAM_B1) * g
    v2 = ADAM_B2 * v + (1.0 - ADAM_B2) * (g * g)
    m_hat = m2 / (1.0 - ADAM_B1 ** ADAM_STEP)
    v_hat = v2 / (1.0 - ADAM_B2 ** ADAM_STEP)
    delta = -ADAM_LR * (m_hat / (jnp.sqrt(v_hat) + ADAM_EPS) + ADAM_WD * w)
    return delta, m2, v2


def _adam_halves(w, mine, got, m, v, tr, name):
    r, cc = w.shape
    h = r // 2
    nt = h // tr

    def body(c_ref, w_ref, mine_ref, got_ref, m_ref, v_ref, g_ref, d_ref, m2_ref, v2_ref):
        g = jnp.where(pl.program_id(0) == c_ref[0], mine_ref[...], got_ref[...])
        g_ref[...] = g
        d, m2, v2 = _adam_math(w_ref[...], g, m_ref[...], v_ref[...])
        d_ref[...] = d
        m2_ref[...] = m2
        v2_ref[...] = v2

    full = pl.BlockSpec((tr, cc), lambda hh, i, c_ref: (hh * nt + i, 0))
    half = pl.BlockSpec((tr, cc), lambda hh, i, c_ref: (i, 0))
    shp = jax.ShapeDtypeStruct((r, cc), F32)
    return pl.pallas_call(
        body, name=name, out_shape=(shp, shp, shp, shp),
        grid_spec=pltpu.PrefetchScalarGridSpec(
            num_scalar_prefetch=1, grid=(2, nt), in_specs=[full, half, half, full, full],
            out_specs=(full, full, full, full)),
        compiler_params=_params(("arbitrary", "arbitrary")),
    )(_core_index_scalar(), w, mine, got, m, v)


def _adam_w_ada(sc_t, dmod_cols, w, m, v, tr):
    r, cc = w.shape

    def body(sct_ref, dm_ref, w_ref, m_ref, v_ref, g_ref, d_ref, m2_ref, v2_ref):
        g = sct_ref[:, 0:1] * dm_ref[0:1, :]
        for k in range(1, N_DEV):
            g = g + sct_ref[:, k:k + 1] * dm_ref[k:k + 1, :]
        g_ref[...] = g
        d, m2, v2 = _adam_math(w_ref[...], g, m_ref[...], v_ref[...])
        d_ref[...] = d
        m2_ref[...] = m2
        v2_ref[...] = v2

    spec = pl.BlockSpec((tr, cc), lambda i: (i, 0))
    shp = jax.ShapeDtypeStruct((r, cc), F32)
    return pl.pallas_call(
        body, name="adam_w_ada", out_shape=(shp, shp, shp, shp), grid=(r // tr,),
        in_specs=[pl.BlockSpec((tr, N_DEV), lambda i: (i, 0)), _const_spec((N_DEV, cc)), spec, spec, spec],
        out_specs=(spec, spec, spec, spec), compiler_params=_params(("parallel",)),
    )(sc_t, dmod_cols, w, m, v)


def _pack_wide(acc_i, acc_m, acc_f, db_in, vec):
    arrs = [acc_i, acc_m, acc_f, db_in, vec]
    i_, m_, f_, b_, v_ = range(5)
    src = {"b_in": (b_, 0), "ln1_g": (m_, 2), "ln1_b": (m_, 3), "ln2_g": (f_, 1), "ln2_b": (f_, 2),
           "gmlp_ln_g": (v_, 2), "gmlp_ln_b": (v_, 3), "attn_out_g": (v_, 0), "gmlp_out_g": (v_, 1), "loss": (f_, 0)}
    dmod = [(i_, 1), (i_, 0), (m_, 4), (m_, 1), (m_, 0), (f_, 3)]

    def body(*refs):
        ins, wide_ref = refs[:5], refs[5]
        wide_ref[...] = jnp.zeros_like(wide_ref)
        for k, (a, row) in enumerate(dmod):
            wide_ref[0:1, k * D_MODEL:(k + 1) * D_MODEL] = ins[a][row:row + 1, :]
        for name, (a, row) in src.items():
            r, off, n = WIDE_LAYOUT[name]
            wide_ref[r:r + 1, off:off + n] = ins[a][row:row + 1, :]

    return pl.pallas_call(
        body, name="pack_wide", out_shape=jax.ShapeDtypeStruct((8, WIDE_W), F32), grid=(1,),
        in_specs=[_const_spec(a.shape) for a in arrs], out_specs=_const_spec((8, WIDE_W)),
        compiler_params=_params(("arbitrary",)),
    )(*arrs)


def _adam_small(gw, gt, wide_wmv, w_s, b_s, rel_bias, sinks):
    names = list(WIDE_PARAMS)
    tall = [("gmlp_w_s", w_s), ("gmlp_b_s", b_s), ("rel_bias", rel_bias), ("attn_sinks", sinks)]
    ins = [gw, gt]
    for n in names:
        ins += list(wide_wmv[n])
    for _, t in tall:
        ins += list(t)
    n_in = len(ins)

    def body(*refs):
        gw_ref, gt_ref = refs[0], refs[1]
        wmv = refs[2:n_in]
        dmod_ref, loss_ref = refs[n_in], refs[n_in + 1]
        outs = refs[n_in + 2:]

        def tall_sum(r0, nr):
            g = gt_ref[r0:r0 + nr, :]
            for d in range(1, N_DEV):
                g = g + gt_ref[d * TALL_ROWS + r0:d * TALL_ROWS + r0 + nr, :]
            return g

        def emit(k, g, w_ref, m_ref, v_ref):
            d, m2, v2 = _adam_math(w_ref[...], g, m_ref[...], v_ref[...])
            outs[4 * k][...] = g
            outs[4 * k + 1][...] = d
            outs[4 * k + 2][...] = m2
            outs[4 * k + 3][...] = v2

        gsum = gw_ref[0:8, :]
        for d in range(1, N_DEV):
            gsum = gsum + gw_ref[8 * d:8 * d + 8, :]
        for d in range(N_DEV):
            dmod_ref[d:d + 1, :] = gw_ref[8 * d:8 * d + 1, :]
        for k, n in enumerate(names):
            r, off, sz = WIDE_LAYOUT[n]
            emit(k, gsum[r:r + 1, off:off + sz], *wmv[3 * k:3 * k + 3])
        r, off, sz = WIDE_LAYOUT["loss"]
        tot = jnp.sum(gsum[r:r + 1, off:off + sz], axis=1, keepdims=True)
        loss_ref[...] = jnp.broadcast_to(tot * (0.5 / D_MODEL), loss_ref.shape)

        k0 = len(names)
        ws_refs = wmv[3 * k0:3 * k0 + 3]
        for g in range(N_GROUPS):
            rows = slice(g * BLOCK, (g + 1) * BLOCK)
            gg = tall_sum(g * BLOCK, BLOCK)
            d, m2, v2 = _adam_math(ws_refs[0][rows, :], gg, ws_refs[1][rows, :], ws_refs[2][rows, :])
            outs[4 * k0][rows, :] = gg
            outs[4 * k0 + 1][rows, :] = d
            outs[4 * k0 + 2][rows, :] = m2
            outs[4 * k0 + 3][rows, :] = v2
        emit(k0 + 1, tall_sum(TALL_BS, N_GROUPS), *wmv[3 * (k0 + 1):3 * (k0 + 1) + 3])
        emit(k0 + 2, tall_sum(TALL_RB, N_BUCKETS)[:, :N_HEADS], *wmv[3 * (k0 + 2):3 * (k0 + 2) + 3])
        emit(k0 + 3, tall_sum(TALL_SK, 8)[0:1, :N_HEADS], *wmv[3 * (k0 + 3):3 * (k0 + 3) + 3])

    out_shapes = [jax.ShapeDtypeStruct((N_DEV, WIDE_W), F32), jax.ShapeDtypeStruct((8, LANES), F32)]
    for n in names:
        out_shapes += [jax.ShapeDtypeStruct(wide_wmv[n][0].shape, F32)] * 4
    for _, t in tall:
        out_shapes += [jax.ShapeDtypeStruct(t[0].shape, F32)] * 4
    res = pl.pallas_call(
        body, name="adam_small", out_shape=out_shapes, grid=(1,),
        in_specs=[_const_spec(a.shape) for a in ins], out_specs=[_const_spec(o.shape) for o in out_shapes],
        compiler_params=_params(("arbitrary",)),
    )(*ins)
    out = {}
    for k, n in enumerate(names + [t[0] for t in tall]):
        out[n] = tuple(res[2 + 4 * k:6 + 4 * k])
    return res[0], res[1], out


def kernel(x, c, rel_bias, w_ada, b_ada, w_in, b_in, attn_sinks, gmlp_ln_g, gmlp_ln_b, gmlp_w_s, gmlp_b_s, attn_out_g, gmlp_out_g, w_out, ln1_g, ln1_b, w_gate_up, w_down, ln2_g, ln2_b, loss_target, m_rel_bias, m_w_ada, m_b_ada, m_w_in, m_b_in, m_attn_sinks, m_gmlp_ln_g, m_gmlp_ln_b, m_gmlp_w_s, m_gmlp_b_s, m_attn_out_g, m_gmlp_out_g, m_w_out, m_ln1_g, m_ln1_b, m_w_gate_up, m_w_down, m_ln2_g, m_ln2_b, v_rel_bias, v_w_ada, v_b_ada, v_w_in, v_b_in, v_attn_sinks, v_gmlp_ln_g, v_gmlp_ln_b, v_gmlp_w_s, v_gmlp_b_s, v_attn_out_g, v_gmlp_out_g, v_w_out, v_ln1_g, v_ln1_b, v_w_gate_up, v_w_down, v_ln2_g, v_ln2_b):
    ix, iy, ic = _my_pos()
    chip = 2 * ix + iy
    dev = 4 * ix + 2 * iy + ic
    s = x.shape[1]
    xs = x[0]
    tgt = loss_target[0]
    tm_big = min(512, s)
    tm_ffn = min(FFN_SUB, s)
    n_ada = w_ada.shape[2]

    w_in_s, w_out_s = w_in[0].astype(BF16), w_out[0].astype(BF16)
    w_gu_s, w_dn_s = w_gate_up[0].astype(BF16), w_down[0].astype(BF16)
    sc_all, mod_rows, (w_in_g, w_out_g) = _prologue(
        jnp.pad(c, ((0, 7), (0, 0))), w_ada[0], lax.dynamic_slice_in_dim(b_ada, chip * n_ada, n_ada, axis=1),
        [w_in_s, w_out_s])
    mod_all = mod_rows.reshape(N_DEV, N_DEV, -1)
    mod_row = lax.dynamic_index_in_dim(mod_all[0::2], dev, axis=1, keepdims=False)
    modr = jnp.pad(mod_row.reshape(6, D_MODEL), ((0, 2), (0, 0)))
    w_in_g = _insert_own(w_in_g, w_in_s, "blk", chip)
    w_in_f = jnp.transpose(w_in_g, (1, 0, 2)).reshape(D_MODEL, IN_W)

    bucket = _bucket_table()
    bias, wsm = _prep_tables(bucket, rel_bias, gmlp_w_s[0])
    bsx = jnp.repeat(gmlp_b_s[0].T, GROUP_DIM, axis=1)
    amat = _group_mean_matrix()
    sinks = attn_sinks[0]

    (h1, q, kv, gu, gv, xb), (w_dn_g,) = _fwd_in(xs, modr, w_in_f, b_in, tm_big, [w_dn_s], ["blk"])
    w_out_f = _insert_own(w_out_g, w_out_s, "blk", chip).reshape(D_MODEL, D_MODEL)
    (x1, x1b, y, mixed), (w_gu_g,) = _fwd_mix(
        q, kv, gu, gv, xs, modr, bias, sinks, gmlp_ln_g, gmlp_ln_b, wsm, bsx, amat, attn_out_g, gmlp_out_g, w_out_f,
        ln1_g, ln1_b, tm_big, [w_gu_s], ["blk"])
    assert w_gate_up.shape[2] == FF_CHUNK
    w_gu_f = _insert_own(w_gu_g, w_gu_s, "blk", chip)
    w_dn_f = _insert_own(w_dn_g, w_dn_s, "blk", chip).reshape(D_FF, D_MODEL)
    h2, act, dy2, dx1a, acc_f = _fwd_ffn(x1, tgt, modr, ln2_g, ln2_b, w_gu_f, w_dn_f, tm_ffn)

    a_act, dgu_ff, dh2 = _bwd_ffn(dy2, act, w_gu_f, w_dn_f, min(FFN_SUB, s))
    g_dn, g_dn_b, _ = _wgrad(a_act, [dy2], D_FF // 2, min(512, s), "wgrad_down")
    g_gu, g_gu_b, _ = _wgrad(h2, [dgu_ff], 512, min(512, s), "wgrad_gate_up")
    blk3 = lambda a, rows: a.reshape(N_CHIPS, rows, a.shape[1])
    (dxa, dy, dmix, acc_m), (got_dn, got_gu) = _bwd_mid(
        dh2, dx1a, x1b, xs, y, modr, ln1_g, w_out_f, tm_big, [blk3(g_dn_b, D_FF // N_CHIPS), g_gu_b], ["blk", "cols"])
    g_out, g_out_b, _ = _wgrad(mixed, [dy], 512, min(512, s), "wgrad_out")
    (got_out,) = _swap_halves([blk3(g_out_b, D_MODEL // N_CHIPS)], ["blk"], "rs_swap_out")
    kinds_a = ["blk", "cols", "blk"]
    fulls_a = [blk3(g_dn, D_FF // N_CHIPS), g_gu, blk3(g_out, D_MODEL // N_CHIPS)]
    gots_a = [got_dn, got_gu, got_out]
    parts_a = [_add_halves(f, g, k, "rs_add_a%d" % i) for i, (f, g, k) in enumerate(zip(fulls_a, gots_a, kinds_a))]
    (dq, dkv, dgu, dgv, gbias, dws, dbs, vec, dsink), rxs_a = _bwd_mix(
        q, kv, gu, gv, dmix, bias, sinks, gmlp_ln_g, gmlp_ln_b, wsm, bsx, amat, attn_out_g, gmlp_out_g,
        [p[1] for p in parts_a], kinds_a)
    tall_g = _mix_finalize(gbias, bucket, dws, dbs, dsink)
    grad_x, acc_i, db_in = _bwd_in(dq, dkv, dgu, dgv, dxa, xb, modr, w_in_f, tm_big)

    wide_g = _pack_wide(acc_i, acc_m, acc_f, db_in, vec)
    full_in, full_in_b, (gw, gt) = _wgrad(h1, [dq, dkv, dgu, dgv], 512, min(512, s), "wgrad_in", owner_blocks=True,
                                          gather_vs=[wide_g, tall_g])
    (got_in,) = _swap_halves([full_in_b], ["blk"], "rs_swap_in")
    part_in = _add_halves(full_in, got_in, "blk", "rs_add_in")
    (rx_in,) = _exchange_chip_partials([part_in[1]], ["blk"], "rs_chips_in")
    wide_wmv = {"b_ada": (b_ada, m_b_ada, v_b_ada), "b_in": (b_in, m_b_in, v_b_in),
                "ln1_g": (ln1_g, m_ln1_g, v_ln1_g), "ln1_b": (ln1_b, m_ln1_b, v_ln1_b),
                "ln2_g": (ln2_g, m_ln2_g, v_ln2_g), "ln2_b": (ln2_b, m_ln2_b, v_ln2_b),
                "gmlp_ln_g": (gmlp_ln_g, m_gmlp_ln_g, v_gmlp_ln_g), "gmlp_ln_b": (gmlp_ln_b, m_gmlp_ln_b, v_gmlp_ln_b),
                "attn_out_g": (attn_out_g, m_attn_out_g, v_attn_out_g),
                "gmlp_out_g": (gmlp_out_g, m_gmlp_out_g, v_gmlp_out_g)}
    rows2 = lambda a: a.reshape(-1, a.shape[-1])
    dmod_all, loss_t, small = _adam_small(
        gw, gt, wide_wmv, tuple(rows2(a) for a in (gmlp_w_s, m_gmlp_w_s, v_gmlp_w_s)),
        tuple(rows2(a) for a in (gmlp_b_s, m_gmlp_b_s, v_gmlp_b_s)), (rel_bias, m_rel_bias, v_rel_bias),
        (attn_sinks, m_attn_sinks, v_attn_sinks))
    loss = loss_t[0, 0]

    dmod_cols = lax.dynamic_slice_in_dim(dmod_all, chip * n_ada, n_ada, axis=1)
    g_ada, d_ada, m_ada, v_ada = _adam_w_ada(sc_all.T, dmod_cols, w_ada[0], m_w_ada[0], v_w_ada[0], 256)

    sums = [(parts_a[0][0], rxs_a[0], 176), (parts_a[1][0], rxs_a[1], 256), (parts_a[2][0], rxs_a[2], 128),
            (part_in[0], rx_in, 256)]
    mine = [_sum_chips(p, rx, tr, "rs_sum_%d" % i) for i, (p, rx, tr) in enumerate(sums)]
    got = _share_halves(mine, "rs_share")

    gs_dn, d_dn, m_dn, v_dn = _adam_halves(w_down[0], mine[0], got[0], m_w_down[0], v_w_down[0], 176, "adam_w_down")
    gs_gu, d_gu, m_gu, v_gu = _adam_halves(w_gate_up[0], mine[1], got[1], m_w_gate_up[0], v_w_gate_up[0], 256,
                                           "adam_w_gate_up")
    gs_out, d_out, m_out, v_out = _adam_halves(w_out[0], mine[2], got[2], m_w_out[0], v_w_out[0], 128, "adam_w_out")
    gs_in, d_in, m_in, v_in = _adam_halves(w_in[0], mine[3], got[3], m_w_in[0], v_w_in[0], 256, "adam_w_in")

    big = {"w_ada": (g_ada, d_ada, m_ada, v_ada), "w_in": (gs_in, d_in, m_in, v_in), "w_out": (gs_out, d_out, m_out, v_out),
           "w_gate_up": (gs_gu, d_gu, m_gu, v_gu), "w_down": (gs_dn, d_dn, m_dn, v_dn)}
    order = ["rel_bias", "w_ada", "b_ada", "w_in", "b_in", "attn_sinks", "gmlp_ln_g", "gmlp_ln_b", "gmlp_w_s", "gmlp_b_s",
             "attn_out_g", "gmlp_out_g", "w_out", "ln1_g", "ln1_b", "w_gate_up", "w_down", "ln2_g", "ln2_b"]
    shapes = {"gmlp_w_s": gmlp_w_s.shape, "gmlp_b_s": gmlp_b_s.shape}
    outs = [loss, grad_x[None]]
    for k in range(4):
        for name in order:
            if name in big:
                outs.append(big[name][k][None])
            elif name in shapes:
                outs.append(small[name][k].reshape(shapes[name]))
            else:
                outs.append(small[name][k])
    return tuple(outs)
```

```python
import math

import numpy as np
import jax
import jax.numpy as jnp
from jax import lax
from jax.experimental import pallas as pl
from jax.experimental.pallas import tpu as pltpu

F32 = jnp.float32
BF16 = jnp.bfloat16
MESH = pl.DeviceIdType.MESH

D_MODEL = 1024
N_HEADS = 8
N_KV = 2
HEAD_DIM = 64
ATTN_W = N_HEADS * HEAD_DIM
KV_W = N_KV * HEAD_DIM
N_GROUPS = 8
GROUP_DIM = 64
GMLP_W = N_GROUPS * GROUP_DIM
IN_W = ATTN_W + 2 * KV_W + 2 * GMLP_W
BLOCK = 128
N_BUCKETS = 32
MAX_DISTANCE = 128
D_FF = 2816
ALPHA = 2.0 ** 0.25
LN_EPS = 1e-5
NEG_INF = -1e30
ADAM_LR, ADAM_B1, ADAM_B2, ADAM_EPS, ADAM_WD, ADAM_STEP = 0.001, 0.9, 0.999, 1e-8, 0.01, 10
N_CHIPS = 4
N_DEV = 8
LANES = 128
V7X_VMEM_LIMIT = 56 * 2 ** 20
GELU_C = math.sqrt(2.0 / math.pi)
Q_SCALE = HEAD_DIM ** -0.5
ANY = pl.BlockSpec(memory_space=pl.ANY)

TALL_BS = N_GROUPS * BLOCK
TALL_RB = TALL_BS + 8
TALL_SK = TALL_RB + N_BUCKETS
TALL_ROWS = TALL_SK + 8
WIDE_W = 6 * D_MODEL
WIDE_LAYOUT = {
    "b_ada": (0, 0, 6 * D_MODEL),
    "b_in": (1, 0, IN_W), "ln1_g": (1, IN_W, D_MODEL), "ln1_b": (1, IN_W + D_MODEL, D_MODEL),
    "ln2_g": (1, IN_W + 2 * D_MODEL, D_MODEL), "ln2_b": (1, IN_W + 3 * D_MODEL, D_MODEL),
    "gmlp_ln_g": (2, 0, GMLP_W), "gmlp_ln_b": (2, GMLP_W, GMLP_W), "attn_out_g": (2, 2 * GMLP_W, ATTN_W),
    "gmlp_out_g": (2, 2 * GMLP_W + ATTN_W, GMLP_W), "loss": (2, 3 * GMLP_W + ATTN_W, D_MODEL)}
WIDE_PARAMS = tuple(n for n in WIDE_LAYOUT if n != "loss")


def _params(sem=None):
    return pltpu.CompilerParams(dimension_semantics=sem, vmem_limit_bytes=V7X_VMEM_LIMIT)


def _const_spec(shape, single=False):
    nd = len(shape)
    if single:
        return pl.BlockSpec(shape, lambda *_: (0,) * nd, pipeline_mode=pl.Buffered(1))
    return pl.BlockSpec(shape, lambda *_: (0,) * nd)


def _dot(a, b):
    return jnp.dot(a, b, preferred_element_type=F32)


def _dot_nt(a, b):
    return lax.dot_general(a, b, (((1,), (1,)), ((), ())), preferred_element_type=F32)


def _dot_tn(a, b):
    return lax.dot_general(a, b, (((0,), (0,)), ((), ())), preferred_element_type=F32)


def _gelu(x):
    t = jnp.tanh(GELU_C * (x + 0.044715 * x * x * x))
    return 0.5 * x * (1.0 + t), t


def _gelu_grad(x, t):
    return 0.5 * (1.0 + t) + 0.5 * x * (1.0 - t * t) * GELU_C * (1.0 + 3.0 * 0.044715 * x * x)


def _split_dot(x, a):
    hi = x.astype(BF16)
    lo = (x - hi.astype(F32)).astype(BF16)
    return _dot(hi, a) + _dot(lo, a)


def _group_mean_matrix():
    g = np.arange(GMLP_W) // GROUP_DIM
    return jnp.asarray((g[:, None] == g[None, :]).astype(np.float32) / GROUP_DIM, dtype=BF16)


def _ln_stats(z):
    mu = jnp.mean(z, axis=-1, keepdims=True)
    d = z - mu
    var = jnp.mean(d * d, axis=-1, keepdims=True)
    rstd = lax.rsqrt(var + LN_EPS)
    return d * rstd, rstd


def _ln_bwd(dxhat, xhat, rstd):
    m1 = jnp.mean(dxhat, axis=-1, keepdims=True)
    m2 = jnp.mean(dxhat * xhat, axis=-1, keepdims=True)
    return rstd * (dxhat - m1 - xhat * m2)


def _colsum(x):
    return jnp.sum(x, axis=0, keepdims=True)


def _my_pos():
    return lax.axis_index("x"), lax.axis_index("y"), lax.axis_index("c")


def _other_chips(x, y):
    return [(1 - x, y), (x, 1 - y), (1 - x, 1 - y)]


def _chip_index_scalar():
    ix, iy, _ = _my_pos()
    return jnp.reshape(2 * ix + iy, (1,)).astype(jnp.int32)


def _core_index_scalar():
    return jnp.reshape(lax.axis_index("c"), (1,)).astype(jnp.int32)


class _Gather8:
    def __init__(self, x_refs, out_refs, send_sems, recv_sems, local_sems):
        self.x_refs, self.out_refs = x_refs, out_refs
        self.send_sems, self.recv_sems, self.local_sems = send_sems, recv_sems, local_sems
        self.x, self.y, self.c = _my_pos()
        self.me, self.sibling = (self.x, self.y, self.c), (self.x, self.y, 1 - self.c)
        self.chips = _other_chips(self.x, self.y)

    def _rows(self, a, px, py, pc):
        m_per = self.x_refs[a].shape[0]
        return self.out_refs[a].at[pl.ds((4 * px + 2 * py + pc) * m_per, m_per), :]

    def _copy(self, a, k, block, to, src=None):
        return pltpu.make_async_remote_copy(
            src_ref=self._rows(a, *block) if src is None else src, dst_ref=self._rows(a, *block),
            send_sem=self.send_sems.at[7 * a + k], recv_sem=self.recv_sems.at[7 * a + k], device_id=to,
            device_id_type=MESH)

    def _local(self, a):
        return pltpu.make_async_copy(self.x_refs[a], self._rows(a, *self.me), self.local_sems.at[a])

    def start(self):
        for a in range(len(self.x_refs)):
            self._local(a).start()
            self._copy(a, 0, self.me, self.sibling, src=self.x_refs[a]).start()
            for j, chip in enumerate(self.chips):
                self._copy(a, 1 + j, self.me, (*chip, self.c), src=self.x_refs[a]).start()

    def forward(self):
        for a in range(len(self.x_refs)):
            for j, chip in enumerate(self.chips):
                self._copy(a, 1 + j, (*chip, self.c), self.me).wait_recv()
                self._copy(a, 4 + j, (*chip, self.c), self.sibling).start()

    def finish(self):
        for a in range(len(self.x_refs)):
            self._copy(a, 0, self.sibling, self.me).wait_recv()
            for j, chip in enumerate(self.chips):
                self._copy(a, 4 + j, (*chip, 1 - self.c), self.me).wait_recv()
        for a in range(len(self.x_refs)):
            for k in range(7):
                self._copy(a, k, self.me, self.me).wait_send()
            self._local(a).wait()

    @staticmethod
    def sems(n_v):
        return [pltpu.SemaphoreType.DMA((7 * n_v,)), pltpu.SemaphoreType.DMA((7 * n_v,)),
                pltpu.SemaphoreType.DMA((n_v,))]


def _gathered8_shapes(vs):
    return [jax.ShapeDtypeStruct((N_DEV * v.shape[0], v.shape[1]), v.dtype) for v in vs]


VMEM_WHOLE = pl.BlockSpec(memory_space=pltpu.VMEM)


def _prologue(c_pad, w_ada_s, b_ada_s, shards):
    n = w_ada_s.shape[1]
    n_w = len(shards)

    def body(c_ref, w_ref, b_ref, *rest):
        shard_refs = rest[:n_w]
        sc_ref, modc_ref, modg_ref = rest[n_w:n_w + 3]
        gathered_refs = rest[n_w + 3:2 * n_w + 3]
        call_ref = rest[2 * n_w + 3]
        sems = rest[2 * n_w + 4:]
        weights = _WeightGather(shard_refs, gathered_refs, ["blk"] * n_w, sems[0], sems[1])
        gather_c = _Gather8([c_ref], [call_ref], sems[2], sems[3], sems[4])
        gather_mod = _Gather8([modc_ref], [modg_ref], sems[5], sems[6], sems[7])
        weights.start()
        gather_c.start()
        gather_c.forward()
        gather_c.finish()
        cv = call_ref[...]
        sc = cv * _sigmoid(cv)
        a_hi = sc.astype(BF16)
        a_lo = (sc - a_hi.astype(F32)).astype(BF16)
        w = w_ref[...]
        w_hi = w.astype(BF16)
        w_lo = (w - w_hi.astype(F32)).astype(BF16)
        mod = _dot(a_hi, w_hi) + _dot(a_hi, w_lo) + _dot(a_lo, w_hi) + b_ref[...]
        for d in range(N_DEV):
            sc_ref[d:d + 1, :] = sc[8 * d:8 * d + 1, :]
            modc_ref[d:d + 1, :] = mod[8 * d:8 * d + 1, :]
        gather_mod.start()
        weights.forward()
        gather_mod.forward()
        gather_mod.finish()
        weights.forward_diagonal()
        weights.finish()

    outs = pl.pallas_call(
        body, name="prologue",
        out_shape=[jax.ShapeDtypeStruct((N_DEV, D_MODEL), F32), jax.ShapeDtypeStruct((N_DEV, n), F32),
                   jax.ShapeDtypeStruct((N_DEV * N_DEV, n), F32)]
        + [jax.ShapeDtypeStruct(_gathered_shape(sh, "blk"), BF16) for sh in shards],
        in_specs=[VMEM_WHOLE, VMEM_WHOLE, VMEM_WHOLE] + [ANY] * n_w,
        out_specs=[VMEM_WHOLE, VMEM_WHOLE, VMEM_WHOLE] + [ANY] * n_w,
        scratch_shapes=[pltpu.VMEM((N_DEV * 8, D_MODEL), F32)] + _WeightGather.sems(n_w) + _Gather8.sems(1)
        + _Gather8.sems(1),
        compiler_params=pltpu.CompilerParams(vmem_limit_bytes=V7X_VMEM_LIMIT),
    )(c_pad, w_ada_s, b_ada_s, *shards)
    return outs[0], outs[2], outs[3:]


def _gathered_shape(shard, kind):
    r, cc = shard.shape
    return (N_CHIPS, r, cc) if kind == "blk" else (r, N_CHIPS * cc)


class _WeightGather:
    N_SEM = 8

    def __init__(self, shards, gathered, kinds, send_sems, recv_sems):
        self.shards, self.gathered, self.kinds = shards, gathered, kinds
        self.send_sems, self.recv_sems = send_sems, recv_sems
        self.x, self.y, self.c = _my_pos()
        self.me, self.sibling = (self.x, self.y, self.c), (self.x, self.y, 1 - self.c)
        self.nbr = ((1 - self.x, self.y), (self.x, 1 - self.y))
        self.diag = 2 * (1 - self.x) + (1 - self.y)

    def _dst(self, a, chip, pc, quarter=None):
        r, cc = self.shards[a].shape
        h = r // 2
        row0, rows = pc * h, h
        if quarter is not None:
            row0, rows = pc * h + quarter * (h // 2), h // 2
        g = self.gathered[a]
        if self.kinds[a] == "blk":
            return g.at[chip, pl.ds(row0, rows), :]
        return g.at[pl.ds(row0, rows), pl.ds(chip * cc, cc)]

    def _copy(self, a, k, region, to, src=None):
        return pltpu.make_async_remote_copy(
            src_ref=region if src is None else src, dst_ref=region, send_sem=self.send_sems.at[a * self.N_SEM + k],
            recv_sem=self.recv_sems.at[a * self.N_SEM + k], device_id=to, device_id_type=MESH)

    def _arrays(self):
        return range(len(self.shards))

    def start(self):
        my_chip = 2 * self.x + self.y
        for a in self._arrays():
            h = self.shards[a].shape[0] // 2
            mine = self.shards[a].at[pl.ds(self.c * h, h), :]
            for j, chip in enumerate(self.nbr):
                self._copy(a, j, self._dst(a, my_chip, self.c), (*chip, self.c), src=mine).start()

    def forward(self):
        for a in self._arrays():
            for j, chip in enumerate(self.nbr):
                cj = 2 * chip[0] + chip[1]
                half = self._dst(a, cj, self.c)
                self._copy(a, j, half, self.me).wait_recv()
                self._copy(a, 2 + j, half, self.sibling).start()
                other = self.nbr[1 - j]
                self._copy(a, 4 + j, self._dst(a, cj, self.c, quarter=j), (*other, self.c)).start()

    def forward_diagonal(self):
        for a in self._arrays():
            for j in range(2):
                quarter = self._dst(a, self.diag, self.c, quarter=j)
                self._copy(a, 4 + j, quarter, self.me).wait_recv()
                self._copy(a, 6 + j, quarter, self.sibling).start()

    def finish(self):
        for a in self._arrays():
            for j, chip in enumerate(self.nbr):
                self._copy(a, 2 + j, self._dst(a, 2 * chip[0] + chip[1], 1 - self.c), self.me).wait_recv()
                self._copy(a, 6 + j, self._dst(a, self.diag, 1 - self.c, quarter=j), self.me).wait_recv()
        for a in self._arrays():
            half = self._dst(a, self.diag, self.c)
            quarter = self._dst(a, self.diag, self.c, quarter=0)
            for k in range(self.N_SEM):
                self._copy(a, k, half if k < 4 else quarter, self.me).wait_send()

    @classmethod
    def sems(cls, n_arr):
        return [pltpu.SemaphoreType.DMA((n_arr * cls.N_SEM,)), pltpu.SemaphoreType.DMA((n_arr * cls.N_SEM,))]


def _insert_own(gathered, shard, kind, chip):
    if kind == "blk":
        return lax.dynamic_update_slice(gathered, shard[None], (chip, 0, 0))
    return lax.dynamic_update_slice(gathered, shard, (0, chip * shard.shape[1]))


def _half_of_full(ref, kind, pc):
    if kind == "blk":
        h = ref.shape[1] // 2
        return ref.at[:, pl.ds(pc * h, h), :]
    h = ref.shape[0] // 2
    return ref.at[pl.ds(pc * h, h), :]


def _half_shape(shape, kind):
    return (shape[0], shape[1] // 2, shape[2]) if kind == "blk" else (shape[0] // 2, shape[1])


class _HalfSwap:
    def __init__(self, ins, outs, kinds, send_sems, recv_sems):
        self.ins, self.outs, self.kinds = ins, outs, kinds
        self.send_sems, self.recv_sems = send_sems, recv_sems
        self.x, self.y, self.c = _my_pos()

    def _copies(self):
        for a in range(len(self.ins)):
            yield pltpu.make_async_remote_copy(
                src_ref=_half_of_full(self.ins[a], self.kinds[a], 1 - self.c), dst_ref=self.outs[a],
                send_sem=self.send_sems.at[a], recv_sem=self.recv_sems.at[a],
                device_id=(self.x, self.y, 1 - self.c), device_id_type=MESH)

    def start(self):
        for cp in self._copies():
            cp.start()

    def wait(self):
        for cp in self._copies():
            cp.wait()

    @staticmethod
    def sems(n_arr):
        return [pltpu.SemaphoreType.DMA((n_arr,)), pltpu.SemaphoreType.DMA((n_arr,))]

    @staticmethod
    def out_shapes(fulls, kinds):
        return [jax.ShapeDtypeStruct(_half_shape(a.shape, k), a.dtype) for a, k in zip(fulls, kinds)]


def _swap_halves(fulls_bf16, kinds, name):
    n_arr = len(fulls_bf16)

    def body(*refs):
        swap = _HalfSwap(refs[:n_arr], refs[n_arr:2 * n_arr], kinds, *refs[2 * n_arr:])
        swap.start()
        swap.wait()

    return pl.pallas_call(
        body, name=name, out_shape=_HalfSwap.out_shapes(fulls_bf16, kinds),
        in_specs=[ANY] * n_arr, out_specs=[ANY] * n_arr, scratch_shapes=_HalfSwap.sems(n_arr),
    )(*fulls_bf16)


def _add_halves(full, got, kind, name):
    hs = _half_shape(full.shape, kind)

    def body(pos_ref, a_ref, b_ref, o_ref, ob_ref):
        p = a_ref[...] + b_ref[...].astype(F32)
        ob_ref[...] = p.astype(BF16)

        @pl.when(pl.program_id(0) == pos_ref[1])
        def _():
            o_ref[...] = p.reshape(o_ref.shape)

    if kind == "blk":
        nb, h, cc = hs
        own = pl.BlockSpec((1, h, cc), lambda b, pos_ref: (b, pos_ref[0], 0))
        other = pl.BlockSpec((1, h, cc), lambda b, pos_ref: (b, 0, 0))
    else:
        h, cc = hs[0], hs[1] // N_CHIPS
        own = pl.BlockSpec((h, cc), lambda b, pos_ref: (pos_ref[0], b))
        other = pl.BlockSpec((h, cc), lambda b, pos_ref: (0, b))
    pos = jnp.concatenate([_core_index_scalar(), _chip_index_scalar()])
    return pl.pallas_call(
        body, name=name, out_shape=(jax.ShapeDtypeStruct((h, cc), F32), jax.ShapeDtypeStruct(hs, BF16)),
        grid_spec=pltpu.PrefetchScalarGridSpec(
            num_scalar_prefetch=1, grid=(N_CHIPS,), in_specs=[own, other],
            out_specs=(pl.BlockSpec((h, cc), lambda b, pos_ref: (0, 0)), other)),
        compiler_params=_params(("arbitrary",)),
    )(pos, full, got)


def _rx_shape(part_shape, kind):
    if kind == "blk":
        return (3, part_shape[1], part_shape[2])
    return (3, part_shape[0], part_shape[1] // N_CHIPS)


class _ChipExchange:
    def __init__(self, parts, rxs, kinds, send_sems, recv_sems):
        self.parts, self.rxs, self.kinds = parts, rxs, kinds
        self.send_sems, self.recv_sems = send_sems, recv_sems
        self.x, self.y, self.c = _my_pos()
        self.chips = _other_chips(self.x, self.y)

    def _copies(self):
        for a in range(len(self.parts)):
            for j, chip in enumerate(self.chips):
                cj = 2 * chip[0] + chip[1]
                if self.kinds[a] == "blk":
                    src = self.parts[a].at[cj]
                else:
                    cc = self.parts[a].shape[1] // N_CHIPS
                    src = self.parts[a].at[:, pl.ds(cj * cc, cc)]
                yield pltpu.make_async_remote_copy(
                    src_ref=src, dst_ref=self.rxs[a].at[j], send_sem=self.send_sems.at[a * 3 + j],
                    recv_sem=self.recv_sems.at[a * 3 + j], device_id=(*chip, self.c), device_id_type=MESH)

    def start(self):
        for cp in self._copies():
            cp.start()

    def wait(self):
        for cp in self._copies():
            cp.wait_recv()
        for cp in self._copies():
            cp.wait_send()

    @staticmethod
    def sems(n_arr):
        return [pltpu.SemaphoreType.DMA((n_arr * 3,)), pltpu.SemaphoreType.DMA((n_arr * 3,))]


def _exchange_chip_partials(parts, kinds, name):
    n_arr = len(parts)

    def body(*refs):
        exchange = _ChipExchange(refs[:n_arr], refs[n_arr:2 * n_arr], kinds, *refs[2 * n_arr:])
        exchange.start()
        exchange.wait()

    return pl.pallas_call(
        body, name=name,
        out_shape=[jax.ShapeDtypeStruct(_rx_shape(p.shape, k), BF16) for p, k in zip(parts, kinds)],
        in_specs=[ANY] * n_arr, out_specs=[ANY] * n_arr, scratch_shapes=_ChipExchange.sems(n_arr),
    )(*parts)


def _sum_chips(part, rx, tr, name):
    _, h, cc = rx.shape
    flips = (2, 1, 3)

    def body(chip_ref, p_ref, rx_ref, o_ref):
        own = p_ref[...]
        for mc in range(N_CHIPS):
            @pl.when(chip_ref[0] == mc)
            def _():
                terms = sorted([(mc, None)] + [(mc ^ f, j) for j, f in enumerate(flips)])
                acc = None
                for _, j in terms:
                    t = own if j is None else rx_ref[j].astype(F32)
                    acc = t if acc is None else acc + t
                o_ref[...] = acc

    return pl.pallas_call(
        body, name=name, out_shape=jax.ShapeDtypeStruct((h, cc), F32),
        grid_spec=pltpu.PrefetchScalarGridSpec(
            num_scalar_prefetch=1, grid=(h // tr,),
            in_specs=[pl.BlockSpec((tr, cc), lambda i, chip_ref: (i, 0)),
                      pl.BlockSpec((3, tr, cc), lambda i, chip_ref: (0, i, 0))],
            out_specs=pl.BlockSpec((tr, cc), lambda i, chip_ref: (i, 0))),
        compiler_params=_params(("arbitrary",)),
    )(_chip_index_scalar(), part, rx)


def _share_halves(halves, name):
    n_arr = len(halves)

    def body(*refs):
        ins, outs = refs[:n_arr], refs[n_arr:2 * n_arr]
        send_sems, recv_sems = refs[2 * n_arr:]
        x, y, c = _my_pos()
        cps = []
        for a in range(n_arr):
            cp = pltpu.make_async_remote_copy(
                src_ref=ins[a], dst_ref=outs[a], send_sem=send_sems.at[a], recv_sem=recv_sems.at[a],
                device_id=(x, y, 1 - c), device_id_type=MESH)
            cp.start()
            cps.append(cp)
        for cp in cps:
            cp.wait()

    return pl.pallas_call(
        body, name=name, out_shape=[jax.ShapeDtypeStruct(h.shape, h.dtype) for h in halves],
        in_specs=[ANY] * n_arr, out_specs=[ANY] * n_arr,
        scratch_shapes=[pltpu.SemaphoreType.DMA((n_arr,)), pltpu.SemaphoreType.DMA((n_arr,))],
    )(*halves)


def _bucket_table():
    qi = jnp.arange(BLOCK)[:, None]
    si = jnp.arange(2 * BLOCK)[None, :]
    dist = qi + BLOCK - si
    max_exact = N_BUCKETS // 2
    n = jnp.maximum(dist, 0)
    nf = jnp.maximum(n, max_exact).astype(F32)
    large = max_exact + (jnp.log(nf / max_exact) / math.log(MAX_DISTANCE / max_exact)
                         * (N_BUCKETS - max_exact)).astype(jnp.int32)
    large = jnp.minimum(large, N_BUCKETS - 1)
    return jnp.where(n < max_exact, n, large).astype(F32)


def _prep_tables(bucket, rel_bias, w_s):
    def body(bucket_ref, rb_ref, ws_ref, bias_ref, wsm_ref):
        qi = lax.broadcasted_iota(jnp.int32, (BLOCK, 2 * BLOCK), 0)
        si = lax.broadcasted_iota(jnp.int32, (BLOCK, 2 * BLOCK), 1)
        dist = qi + BLOCK - si
        in_window = (dist >= 0) & (dist < BLOCK)
        bk = bucket_ref[...]
        for h in range(N_HEADS):
            acc = jnp.zeros((BLOCK, 2 * BLOCK), F32)
            for b in range(N_BUCKETS):
                acc = jnp.where(bk == float(b), rb_ref[b, h], acc)
            bias_ref[h] = jnp.where(in_window, acc, NEG_INF)
        ti = lax.broadcasted_iota(jnp.int32, (BLOCK, BLOCK), 0)
        ui = lax.broadcasted_iota(jnp.int32, (BLOCK, BLOCK), 1)
        for g in range(N_GROUPS):
            wsm_ref[g] = jnp.where(ti >= ui, ws_ref[g], 0.0).astype(BF16)

    return pl.pallas_call(
        body, name="prep_tables",
        out_shape=(jax.ShapeDtypeStruct((N_HEADS, BLOCK, 2 * BLOCK), F32),
                   jax.ShapeDtypeStruct((N_GROUPS, BLOCK, BLOCK), BF16)),
        grid=(1,),
        in_specs=[_const_spec((BLOCK, 2 * BLOCK)), pl.BlockSpec(memory_space=pltpu.SMEM),
                  _const_spec((N_GROUPS, BLOCK, BLOCK))],
        out_specs=(_const_spec((N_HEADS, BLOCK, 2 * BLOCK)), _const_spec((N_GROUPS, BLOCK, BLOCK))),
        compiler_params=_params(("arbitrary",)),
    )(bucket, rel_bias, w_s)


def _fwd_in(x, modr, w_in, b_in, tm, shards, kinds):
    s = x.shape[0]
    n_steps = s // tm
    fwd_step, diag_step = (8 * n_steps) // 16, (13 * n_steps) // 16
    n_w = len(shards)

    def body(x_ref, mod_ref, w_ref, b_ref, *rest):
        shard_refs = rest[:n_w]
        h1_ref, q_ref, kv_ref, gu_ref, gv_ref, xb_ref = rest[n_w:n_w + 6]
        gathered_refs = rest[n_w + 6:2 * n_w + 6]
        send_sems, recv_sems = rest[2 * n_w + 6:]
        i = pl.program_id(0)
        gather = _WeightGather(shard_refs, gathered_refs, kinds, send_sems, recv_sems)

        @pl.when(i == 0)
        def _():
            gather.start()

        xv = x_ref[...]
        xb_ref[...] = xv.astype(BF16)
        h1 = (xv * (1.0 + mod_ref[1:2, :]) + mod_ref[0:1, :]).astype(BF16)
        h1_ref[...] = h1
        proj = _dot(h1, w_ref[...]) + b_ref[...]
        q_ref[...] = (proj[:, :ATTN_W] * Q_SCALE).astype(BF16)
        kv_ref[...] = proj[:, ATTN_W:ATTN_W + 2 * KV_W].astype(BF16)
        gu_ref[...] = proj[:, ATTN_W + 2 * KV_W:ATTN_W + 2 * KV_W + GMLP_W]
        gv_ref[...] = proj[:, ATTN_W + 2 * KV_W + GMLP_W:]

        @pl.when(i == fwd_step)
        def _():
            gather.forward()

        @pl.when(i == diag_step)
        def _():
            gather.forward_diagonal()

        @pl.when(i == n_steps - 1)
        def _():
            gather.finish()

    row = lambda w: pl.BlockSpec((tm, w), lambda i: (i, 0))
    outs = pl.pallas_call(
        body, name="fwd_in",
        out_shape=[jax.ShapeDtypeStruct((s, D_MODEL), BF16), jax.ShapeDtypeStruct((s, ATTN_W), BF16),
                   jax.ShapeDtypeStruct((s, 2 * KV_W), BF16), jax.ShapeDtypeStruct((s, GMLP_W), F32),
                   jax.ShapeDtypeStruct((s, GMLP_W), F32), jax.ShapeDtypeStruct((s, D_MODEL), BF16)]
        + [jax.ShapeDtypeStruct(_gathered_shape(sh, k), BF16) for sh, k in zip(shards, kinds)],
        grid=(n_steps,),
        in_specs=[row(D_MODEL), _const_spec((8, D_MODEL)), _const_spec((D_MODEL, IN_W)), _const_spec((1, IN_W))]
        + [ANY] * n_w,
        out_specs=[row(D_MODEL), row(ATTN_W), row(2 * KV_W), row(GMLP_W), row(GMLP_W), row(D_MODEL)] + [ANY] * n_w,
        scratch_shapes=_WeightGather.sems(n_w),
        compiler_params=_params(("arbitrary",)),
    )(x, modr, w_in, b_in, *shards)
    return outs[:6], outs[6:]


def _kv_variants(kk):
    kf = kk.astype(F32)
    lane = lax.broadcasted_iota(jnp.int32, kf.shape, 1)
    low = lane < HEAD_DIM
    k0_lo = jnp.where(low, kf, 0.0)
    k1_hi = jnp.where(low, 0.0, kf)
    k0_hi = pltpu.roll(k0_lo, HEAD_DIM, 1)
    k1_lo = pltpu.roll(k1_hi, HEAD_DIM, 1)
    return ((k0_lo.astype(BF16), k0_hi.astype(BF16)), (k1_lo.astype(BF16), k1_hi.astype(BF16)))


def _head_kv(h):
    return h // (N_HEADS // N_KV), h % 2


MIX_GROUP = 2


def _interleave(*gens):
    results = [None] * len(gens)
    active = list(enumerate(gens))
    while active:
        still = []
        for i, g in active:
            try:
                next(g)
                still.append((i, g))
            except StopIteration as done:
                results[i] = done.value
        active = still
    return results


def _attn_block_fwd(q_blk, kk, vv, bias_ref, sinks_ref, first_mask):
    kvar = _kv_variants(kk)
    vvar = _kv_variants(vv)
    heads = range(N_HEADS)
    q_pairs = [q_blk[:, (h // 2) * LANES:(h // 2 + 1) * LANES] for h in heads]
    logits = [_dot_nt(q_pairs[h], kvar[_head_kv(h)[0]][_head_kv(h)[1]]) + bias_ref[h] for h in heads]
    if first_mask is not None:
        logits = [jnp.where(first_mask, NEG_INF, lg) for lg in logits]
    yield
    ms = [jnp.maximum(jnp.max(logits[h], axis=-1, keepdims=True), sinks_ref[h]) for h in heads]
    yield
    es = [jnp.exp(logits[h] - ms[h]) for h in heads]
    ess = [jnp.exp(sinks_ref[h] - ms[h]) for h in heads]
    yield
    invs = [1.0 / (jnp.sum(es[h], axis=-1, keepdims=True) + ess[h]) for h in heads]
    probs = [(es[h] * invs[h], ess[h] * invs[h]) for h in heads]
    yield
    outs = [_dot(probs[h][0].astype(BF16), vvar[_head_kv(h)[0]][_head_kv(h)[1]]) for h in heads]
    pairs = [outs[2 * i] + outs[2 * i + 1] for i in range(N_HEADS // 2)]
    return jnp.concatenate(pairs, axis=1), probs, kvar, vvar


def _gmlp_chunk_fwd(gu, gv, ln_g, ln_b, wsm_ref, bsx, amat):
    u, tu = _gelu(gu)
    a, ta = _gelu(gv)
    yield
    mean = _split_dot(a, amat)
    d = a - mean
    yield
    var = _split_dot(d * d, amat)
    yield
    rstd = lax.rsqrt(var + LN_EPS)
    xhat = d * rstd
    vb = (xhat * ln_g + ln_b).astype(BF16)
    yield
    lane = lax.broadcasted_iota(jnp.int32, (BLOCK, LANES), 1)
    low = lane < GROUP_DIM
    cols = []
    for pair in range(N_GROUPS // 2):
        vp = vb[:, pair * LANES:(pair + 1) * LANES]
        cols.append(jnp.where(low, _dot(wsm_ref[2 * pair], vp), _dot(wsm_ref[2 * pair + 1], vp)))
    mixedv = jnp.concatenate(cols, axis=1) + bsx
    return u * mixedv, (u, tu, ta, xhat, rstd, vb, mixedv)


def _rms(a, g):
    r = lax.rsqrt(jnp.mean(a * a, axis=-1, keepdims=True) + LN_EPS)
    return a * r * g, r


def _fwd_mix(q, kv, gu, gv, x, modr, bias, sinks, gln_g, gln_b, wsm, bsx, amat, aog, gog, w_out, ln1_g, ln1_b, tm,
             ffn_shards, ffn_kinds):
    s = x.shape[0]
    nb = tm // BLOCK
    n_steps = s // tm
    fwd_step, diag_step = (7 * n_steps) // 16, (12 * n_steps) // 16
    n_w = len(ffn_shards)

    def body(q_ref, kv_ref, kvp_ref, gu_ref, gv_ref, x_ref, mod_ref, bias_ref, sinks_ref, glng_ref, glnb_ref, wsm_ref,
             bsx_ref, amat_ref, aog_ref, gog_ref, wout_ref, ln1g_ref, ln1b_ref, *rest):
        shard_refs = rest[:n_w]
        x1_ref, x1b_ref, y_ref, mixed_ref = rest[n_w:n_w + 4]
        gathered_refs = rest[n_w + 4:2 * n_w + 4]
        mix_scr, send_sems, recv_sems = rest[2 * n_w + 4:]
        i = pl.program_id(0)
        gather = _WeightGather(shard_refs, gathered_refs, ffn_kinds, send_sems, recv_sems)

        @pl.when(i == 0)
        def _():
            gather.start()

        col = lax.broadcasted_iota(jnp.int32, (BLOCK, 2 * BLOCK), 1)
        for b0 in range(0, nb, MIX_GROUP):
            gens = []
            for b in range(b0, min(b0 + MIX_GROUP, nb)):
                r0 = b * BLOCK
                if b == 0:
                    kvprev = kvp_ref[...]
                    first_mask = (col < BLOCK) & (i == 0)
                else:
                    kvprev = kv_ref[r0 - BLOCK:r0, :]
                    first_mask = None
                kvcur = kv_ref[r0:r0 + BLOCK, :]
                kk = jnp.concatenate([kvprev[:, :KV_W], kvcur[:, :KV_W]], axis=0)
                vv = jnp.concatenate([kvprev[:, KV_W:], kvcur[:, KV_W:]], axis=0)
                gens.append(_attn_block_fwd(q_ref[r0:r0 + BLOCK, :], kk, vv, bias_ref, sinks_ref, first_mask))
                gens.append(_gmlp_chunk_fwd(gu_ref[r0:r0 + BLOCK, :], gv_ref[r0:r0 + BLOCK, :], glng_ref[...],
                                            glnb_ref[...], wsm_ref, bsx_ref[...], amat_ref[...]))
            res = _interleave(*gens)
            for k, b in enumerate(range(b0, min(b0 + MIX_GROUP, nb))):
                r0 = b * BLOCK
                na, _ = _rms(res[2 * k][0], aog_ref[...])
                ng, _ = _rms(res[2 * k + 1][0], gog_ref[...])
                mix_scr[r0:r0 + BLOCK, :ATTN_W] = na.astype(BF16)
                mix_scr[r0:r0 + BLOCK, ATTN_W:] = ng.astype(BF16)
        mixed = mix_scr[...]
        mixed_ref[...] = mixed
        y = _dot(mixed, wout_ref[...])
        y_ref[...] = y.astype(BF16)
        z1 = ALPHA * x_ref[...] + mod_ref[2:3, :] * y
        xhat, _ = _ln_stats(z1)
        x1 = xhat * ln1g_ref[...] + ln1b_ref[...]
        x1_ref[...] = x1
        x1b_ref[...] = x1.astype(BF16)

        @pl.when(i == fwd_step)
        def _():
            gather.forward()

        @pl.when(i == diag_step)
        def _():
            gather.forward_diagonal()

        @pl.when(i == n_steps - 1)
        def _():
            gather.finish()

    row = lambda w: pl.BlockSpec((tm, w), lambda i: (i, 0))
    prev = pl.BlockSpec((BLOCK, 2 * KV_W), lambda i: (jnp.maximum(i * nb - 1, 0), 0))
    outs = pl.pallas_call(
        body, name="fwd_mix",
        out_shape=[jax.ShapeDtypeStruct((s, D_MODEL), F32)] + [jax.ShapeDtypeStruct((s, D_MODEL), BF16)] * 3
        + [jax.ShapeDtypeStruct(_gathered_shape(sh, k), BF16) for sh, k in zip(ffn_shards, ffn_kinds)],
        grid=(n_steps,),
        in_specs=[row(ATTN_W), row(2 * KV_W), prev, row(GMLP_W), row(GMLP_W), row(D_MODEL), _const_spec((8, D_MODEL)),
                  _const_spec((N_HEADS, BLOCK, 2 * BLOCK)), pl.BlockSpec(memory_space=pltpu.SMEM),
                  _const_spec((1, GMLP_W)), _const_spec((1, GMLP_W)), _const_spec((N_GROUPS, BLOCK, BLOCK)),
                  _const_spec((BLOCK, GMLP_W)), _const_spec((GMLP_W, GMLP_W)), _const_spec((1, ATTN_W)),
                  _const_spec((1, GMLP_W)), _const_spec((D_MODEL, D_MODEL)), _const_spec((1, D_MODEL)),
                  _const_spec((1, D_MODEL))] + [ANY] * n_w,
        out_specs=[row(D_MODEL)] * 4 + [ANY] * n_w,
        scratch_shapes=[pltpu.VMEM((tm, D_MODEL), BF16)] + _WeightGather.sems(n_w),
        compiler_params=_params(("arbitrary",)),
    )(q, kv, kv, gu, gv, x, modr, bias, sinks, gln_g, gln_b, wsm, bsx, amat, aog, gog, w_out, ln1_g, ln1_b, *ffn_shards)
    return outs[:4], outs[4:]


FF_BLOCKS = N_CHIPS // 2
FF_CHUNK = D_FF // FF_BLOCKS
FFN_SUB = 256


def _sigmoid(x):
    return 1.0 / (1.0 + jnp.exp(-x))


def _fwd_ffn(x1, target, modr, ln2_g, ln2_b, w_gu, w_dn, tm):
    s = x1.shape[0]

    def body(x1_ref, t_ref, mod_ref, g_ref, b_ref, wgu_ref, wdn_ref, h2_ref, act_ref, dy2_ref, dx1a_ref, acc_ref):
        @pl.when(pl.program_id(0) == 0)
        def _():
            acc_ref[...] = jnp.zeros_like(acc_ref)

        x1v = x1_ref[...]
        h2 = (x1v * (1.0 + mod_ref[4:5, :]) + mod_ref[3:4, :]).astype(BF16)
        h2_ref[...] = h2
        y2 = None
        for cc in range(FF_BLOCKS):
            c0 = cc * FF_CHUNK
            gate = _dot(h2, wgu_ref[cc])
            up = _dot(h2, wgu_ref[FF_BLOCKS + cc])
            act_ref[:, c0:c0 + FF_CHUNK] = gate.astype(BF16)
            act_ref[:, D_FF + c0:D_FF + c0 + FF_CHUNK] = up.astype(BF16)
            a = (gate * _sigmoid(gate) * up).astype(BF16)
            part = _dot(a, wdn_ref[c0:c0 + FF_CHUNK, :])
            y2 = part if y2 is None else y2 + part
        g2 = mod_ref[5:6, :]
        z2 = ALPHA * x1v + g2 * y2
        xhat, rstd = _ln_stats(z2)
        gain = g_ref[...]
        diff = xhat * gain + b_ref[...] - t_ref[...]
        dx2 = diff * (1.0 / D_MODEL)
        dz2 = _ln_bwd(dx2 * gain, xhat, rstd)
        dx1a_ref[...] = ALPHA * dz2
        dy2_ref[...] = (g2 * dz2).astype(BF16)
        acc_ref[0:1, :] += _colsum(diff * diff)
        acc_ref[1:2, :] += _colsum(dx2 * xhat)
        acc_ref[2:3, :] += _colsum(dx2)
        acc_ref[3:4, :] += _colsum(dz2 * y2)

    row = lambda w: pl.BlockSpec((tm, w), lambda i: (i, 0))
    return pl.pallas_call(
        body, name="fwd_ffn",
        out_shape=(jax.ShapeDtypeStruct((s, D_MODEL), BF16), jax.ShapeDtypeStruct((s, 2 * D_FF), BF16),
                   jax.ShapeDtypeStruct((s, D_MODEL), BF16), jax.ShapeDtypeStruct((s, D_MODEL), F32),
                   jax.ShapeDtypeStruct((8, D_MODEL), F32)),
        grid=(s // tm,),
        in_specs=[row(D_MODEL), row(D_MODEL), _const_spec((8, D_MODEL)), _const_spec((1, D_MODEL)),
                  _const_spec((1, D_MODEL)), _const_spec((N_CHIPS, D_MODEL, FF_CHUNK), single=True),
                  _const_spec((D_FF, D_MODEL), single=True)],
        out_specs=(row(D_MODEL), row(2 * D_FF), row(D_MODEL), row(D_MODEL), _const_spec((8, D_MODEL))),
        compiler_params=_params(("arbitrary",)),
    )(x1, target, modr, ln2_g, ln2_b, w_gu, w_dn)


def _bwd_ffn(dy2, act, w_gu, w_dn, tm):
    s = dy2.shape[0]

    def body(dy2_ref, act_ref, wgu_ref, wdn_ref, a_ref, dgu_ref, dh2_ref):
        dy2v = dy2_ref[...]
        dh2 = None
        for cc in range(FF_BLOCKS):
            c0 = cc * FF_CHUNK
            da = _dot_nt(dy2v, wdn_ref[c0:c0 + FF_CHUNK, :])
            gate = act_ref[:, c0:c0 + FF_CHUNK].astype(F32)
            up = act_ref[:, D_FF + c0:D_FF + c0 + FF_CHUNK].astype(F32)
            sg = _sigmoid(gate)
            sl = gate * sg
            a_ref[:, c0:c0 + FF_CHUNK] = (sl * up).astype(BF16)
            dgate = (da * up * (sg * (1.0 + gate * (1.0 - sg)))).astype(BF16)
            dup = (da * sl).astype(BF16)
            dgu_ref[:, c0:c0 + FF_CHUNK] = dgate
            dgu_ref[:, D_FF + c0:D_FF + c0 + FF_CHUNK] = dup
            part = _dot_nt(dgate, wgu_ref[cc]) + _dot_nt(dup, wgu_ref[FF_BLOCKS + cc])
            dh2 = part if dh2 is None else dh2 + part
        dh2_ref[...] = dh2.astype(BF16)

    row = lambda w: pl.BlockSpec((tm, w), lambda i: (i, 0))
    return pl.pallas_call(
        body, name="bwd_ffn",
        out_shape=(jax.ShapeDtypeStruct((s, D_FF), BF16), jax.ShapeDtypeStruct((s, 2 * D_FF), BF16),
                   jax.ShapeDtypeStruct((s, D_MODEL), BF16)),
        grid=(s // tm,),
        in_specs=[row(D_MODEL), row(2 * D_FF), _const_spec((N_CHIPS, D_MODEL, FF_CHUNK), single=True),
                  _const_spec((D_FF, D_MODEL), single=True)],
        out_specs=(row(D_FF), row(2 * D_FF), row(D_MODEL)),
        compiler_params=_params(("parallel",)),
    )(dy2, act, w_gu, w_dn)


def _bwd_mid(dh2, dx1a, x1, x, y, modr, ln1_g, w_out, tm, swap_fulls, swap_kinds):
    s = x.shape[0]
    n_steps = s // tm
    n_g = len(swap_fulls)

    def body(dh2_ref, dx1a_ref, x1_ref, x_ref, y_ref, mod_ref, g_ref, wout_ref, *rest):
        full_refs = rest[:n_g]
        dxa_ref, dy_ref, dmix_ref, acc_ref = rest[n_g:n_g + 4]
        got_refs = rest[n_g + 4:2 * n_g + 4]
        swap = _HalfSwap(full_refs, got_refs, swap_kinds, *rest[2 * n_g + 4:])
        i = pl.program_id(0)

        @pl.when(i == 0)
        def _():
            swap.start()
            acc_ref[...] = jnp.zeros_like(acc_ref)

        dh2 = dh2_ref[...].astype(F32)
        x1v = x1_ref[...].astype(F32)
        yv = y_ref[...].astype(F32)
        g1 = mod_ref[2:3, :]
        dx1 = dx1a_ref[...] + dh2 * (1.0 + mod_ref[4:5, :])
        z1 = ALPHA * x_ref[...] + g1 * yv
        xhat, rstd = _ln_stats(z1)
        dz1 = _ln_bwd(dx1 * g_ref[...], xhat, rstd)
        dxa_ref[...] = (ALPHA * dz1).astype(BF16)
        dy = (g1 * dz1).astype(BF16)
        dy_ref[...] = dy
        dmix_ref[...] = _dot_nt(dy, wout_ref[...]).astype(BF16)
        acc_ref[0:1, :] += _colsum(dh2 * x1v)
        acc_ref[1:2, :] += _colsum(dh2)
        acc_ref[2:3, :] += _colsum(dx1 * xhat)
        acc_ref[3:4, :] += _colsum(dx1)
        acc_ref[4:5, :] += _colsum(dz1 * yv)

        @pl.when(i == n_steps - 1)
        def _():
            swap.wait()

    row = lambda w: pl.BlockSpec((tm, w), lambda i: (i, 0))
    outs = pl.pallas_call(
        body, name="bwd_mid",
        out_shape=[jax.ShapeDtypeStruct((s, D_MODEL), BF16), jax.ShapeDtypeStruct((s, D_MODEL), BF16),
                   jax.ShapeDtypeStruct((s, D_MODEL), BF16), jax.ShapeDtypeStruct((8, D_MODEL), F32)]
        + _HalfSwap.out_shapes(swap_fulls, swap_kinds),
        grid=(n_steps,),
        in_specs=[row(D_MODEL)] * 5 + [_const_spec((8, D_MODEL)), _const_spec((1, D_MODEL)),
                                       _const_spec((D_MODEL, D_MODEL))] + [ANY] * n_g,
        out_specs=[row(D_MODEL), row(D_MODEL), row(D_MODEL), _const_spec((8, D_MODEL))] + [ANY] * n_g,
        scratch_shapes=_HalfSwap.sems(n_g),
        compiler_params=_params(("arbitrary",)),
    )(dh2, dx1a, x1, x, y, modr, ln1_g, w_out, *swap_fulls)
    return outs[:4], outs[4:]


def _fold_kv(t0, t1):
    lane = lax.broadcasted_iota(jnp.int32, t0.shape, 1)
    f0 = t0 + pltpu.roll(t0, HEAD_DIM, 1)
    f1 = t1 + pltpu.roll(t1, HEAD_DIM, 1)
    return jnp.where(lane < HEAD_DIM, f0, f1)


def _bwd_mix(q, kv, gu, gv, dmix, bias, sinks, gln_g, gln_b, wsm, bsx, amat, aog, gog, grad_parts, grad_kinds):
    s = q.shape[0]
    tile = 2 * BLOCK
    n_steps = s // tile
    n_g = len(grad_parts)

    def body(q_ref, kv_ref, kvp_ref, gu_ref, gv_ref, dmix_ref, bias_ref, sinks_ref, glng_ref, glnb_ref, wsm_ref,
             bsx_ref, amat_ref, aog_ref, gog_ref, *rest):
        part_refs = rest[:n_g]
        dq_ref, dkv_ref, dgu_ref, dgv_ref, gbias_ref, dws_ref, dbs_ref, vec_ref, dsink_ref = rest[n_g:n_g + 9]
        rx_refs = rest[n_g + 9:2 * n_g + 9]
        carry, done, send_sems, recv_sems = rest[2 * n_g + 9:]
        n = pl.program_id(0)
        exchange = _ChipExchange(part_refs, rx_refs, grad_kinds, send_sems, recv_sems)

        @pl.when(n == 0)
        def _():
            exchange.start()
            carry[...] = jnp.zeros_like(carry)
            done[...] = jnp.zeros_like(done)
            gbias_ref[...] = jnp.zeros_like(gbias_ref)
            dws_ref[...] = jnp.zeros_like(dws_ref)
            dbs_ref[...] = jnp.zeros_like(dbs_ref)
            vec_ref[...] = jnp.zeros_like(vec_ref)
            dsink_ref[...] = jnp.zeros_like(dsink_ref)

        @pl.when(n == n_steps)
        def _():
            dkv_ref[:BLOCK, :] = done[...].astype(BF16)
            dkv_ref[BLOCK:, :] = carry[...].astype(BF16)
            exchange.wait()

        @pl.when(n < n_steps)
        def _():
            col = lax.broadcasted_iota(jnp.int32, (BLOCK, 2 * BLOCK), 1)
            lane = lax.broadcasted_iota(jnp.int32, (BLOCK, LANES), 1)
            low = lane < HEAD_DIM
            rows = [slice(0, BLOCK), slice(BLOCK, tile)]
            kv_blocks = [kvp_ref[...], kv_ref[rows[0], :], kv_ref[rows[1], :]]
            masks = [(col < BLOCK) & (n == 0), None]
            q_blks = [q_ref[r, :] for r in rows]
            fwd = []
            for b in range(2):
                kk = jnp.concatenate([kv_blocks[b][:, :KV_W], kv_blocks[b + 1][:, :KV_W]], axis=0)
                vv = jnp.concatenate([kv_blocks[b][:, KV_W:], kv_blocks[b + 1][:, KV_W:]], axis=0)
                fwd.append(_attn_block_fwd(q_blks[b], kk, vv, bias_ref, sinks_ref, masks[b]))
                fwd.append(_gmlp_chunk_fwd(gu_ref[rows[b], :], gv_ref[rows[b], :], glng_ref[...], glnb_ref[...],
                                           wsm_ref, bsx_ref[...], amat_ref[...]))
            res = _interleave(*fwd[:2]) + _interleave(*fwd[2:])

            def gating_bwd(b, d_gm, saved):
                u, tu, ta, xhat, rstd, vb, mixedv = saved
                dgu_ref[rows[b], :] = (d_gm * mixedv * _gelu_grad(gu_ref[rows[b], :], tu)).astype(BF16)
                dmx = d_gm * u
                dmxb = dmx.astype(BF16)
                yield
                dvn_cols, dws = [], []
                for pair in range(N_GROUPS // 2):
                    dp_ = dmxb[:, pair * LANES:(pair + 1) * LANES]
                    vp = vb[:, pair * LANES:(pair + 1) * LANES]
                    dvn_cols.append(
                        jnp.where(low, _dot_tn(wsm_ref[2 * pair], dp_), _dot_tn(wsm_ref[2 * pair + 1], dp_)))
                    zero = jnp.zeros_like(dp_)
                    dws.append(_dot_nt(jnp.where(low, dp_, zero), vp))
                    dws.append(_dot_nt(jnp.where(low, zero, dp_), vp))
                dvn = jnp.concatenate(dvn_cols, axis=1)
                yield
                dxh = dvn * glng_ref[...]
                am = amat_ref[...]
                m1 = _split_dot(dxh, am)
                m2 = _split_dot(dxh * xhat, am)
                yield
                da = rstd * (dxh - m1 - xhat * m2)
                dgv_ref[rows[b], :] = (da * _gelu_grad(gv_ref[rows[b], :], ta)).astype(BF16)
                return dmx, dws, _colsum(dvn * xhat), _colsum(dvn)

            def attention_bwd(b, d_attn, probs, kvar, vvar):
                heads = range(N_HEADS)
                sels = [low if h % 2 == 0 else jnp.logical_not(low) for h in heads]
                pair_of = lambda a, h: a[:, (h // 2) * LANES:(h // 2 + 1) * LANES]
                do_hs = [jnp.where(sels[h], pair_of(d_attn, h), 0.0).astype(BF16) for h in heads]
                q_hs = [jnp.where(sels[h], pair_of(q_blks[b], h), jnp.zeros((BLOCK, LANES), BF16)) for h in heads]
                dps = [_dot_nt(do_hs[h], vvar[_head_kv(h)[0]][_head_kv(h)[1]]) for h in heads]
                yield
                deltas = [jnp.sum(probs[h][0] * dps[h], axis=-1, keepdims=True) for h in heads]
                yield
                dss = [probs[h][0] * (dps[h] - deltas[h]) for h in heads]
                dsinks = [-(probs[h][1] * deltas[h]) for h in heads]
                dsbs = [ds.astype(BF16) for ds in dss]
                pbs = [probs[h][0].astype(BF16) for h in heads]
                yield
                dqs = [_dot(dsbs[h], kvar[_head_kv(h)[0]][_head_kv(h)[1]]) for h in heads]
                tks = [_dot_tn(dsbs[h], q_hs[h]) for h in heads]
                tvs = [_dot_tn(pbs[h], do_hs[h]) for h in heads]
                dq_cols = [dqs[2 * i] + dqs[2 * i + 1] for i in range(N_HEADS // 2)]
                dq_ref[rows[b], :] = (jnp.concatenate(dq_cols, axis=1) * Q_SCALE).astype(BF16)
                per_kv = N_HEADS // N_KV
                kv_sum = lambda ts, kvh: sum(ts[kvh * per_kv + 1:(kvh + 1) * per_kv], ts[kvh * per_kv])
                dkk = _fold_kv(kv_sum(tks, 0), kv_sum(tks, 1))
                dvv = _fold_kv(kv_sum(tvs, 0), kv_sum(tvs, 1))
                return jnp.concatenate([dkk, dvv], axis=1), dss, dsinks

            bwd, rms_g = [], []
            for b in range(2):
                attn, probs, kvar, vvar = res[2 * b]
                gm, saved = res[2 * b + 1]
                na_unit, r_a = _rms(attn, 1.0)
                ng_unit, r_g = _rms(gm, 1.0)
                dmix = dmix_ref[rows[b], :].astype(F32)
                dn_a = dmix[:, :ATTN_W]
                dn_g = dmix[:, ATTN_W:]
                rms_g.append((_colsum(dn_a * na_unit), _colsum(dn_g * ng_unit)))
                t_a = dn_a * aog_ref[...]
                d_attn = r_a * t_a - na_unit * (r_a * jnp.mean(t_a * na_unit, axis=-1, keepdims=True))
                t_g = dn_g * gog_ref[...]
                d_gm = r_g * t_g - ng_unit * (r_g * jnp.mean(t_g * ng_unit, axis=-1, keepdims=True))
                bwd.append(attention_bwd(b, d_attn, probs, kvar, vvar))
                bwd.append(gating_bwd(b, d_gm, saved))
            (dkv_a, dss_a, dsk_a), (dmx_a, dws_a, glg_a, glb_a) = _interleave(*bwd[:2])
            (dkv_b, dss_b, dsk_b), (dmx_b, dws_b, glg_b, glb_b) = _interleave(*bwd[2:])

            vec_ref[0:1, :] += rms_g[0][0] + rms_g[1][0]
            vec_ref[1:2, :] += rms_g[0][1] + rms_g[1][1]
            vec_ref[2:3, :] += glg_a + glg_b
            vec_ref[3:4, :] += glb_a + glb_b
            dbs_ref[...] += dmx_a + dmx_b
            for g in range(N_GROUPS):
                dws_ref[g] += dws_a[g] + dws_b[g]
            for h in range(N_HEADS):
                gbias_ref[h] += dss_a[h] + dss_b[h]
                dsink_ref[h] += dsk_a[h] + dsk_b[h]

            dkv_ref[:BLOCK, :] = done[...].astype(BF16)
            dkv_ref[BLOCK:, :] = (carry[...] + dkv_a[:BLOCK]).astype(BF16)
            done[...] = dkv_a[BLOCK:] + dkv_b[:BLOCK]
            carry[...] = dkv_b[BLOCK:]

    last = n_steps - 1
    cur = lambda w: pl.BlockSpec((tile, w), lambda n: (jnp.minimum(n, last), 0))
    late = lambda w: pl.BlockSpec((tile, w), lambda n: (jnp.clip(n - 1, 0, last), 0))
    before = pl.BlockSpec((BLOCK, 2 * KV_W), lambda n: (jnp.clip(2 * n - 1, 0, 2 * last + 1), 0))
    outs = pl.pallas_call(
        body, name="bwd_mix",
        out_shape=[jax.ShapeDtypeStruct((s, ATTN_W), BF16), jax.ShapeDtypeStruct((s, 2 * KV_W), BF16),
                   jax.ShapeDtypeStruct((s, GMLP_W), BF16), jax.ShapeDtypeStruct((s, GMLP_W), BF16),
                   jax.ShapeDtypeStruct((N_HEADS, BLOCK, 2 * BLOCK), F32),
                   jax.ShapeDtypeStruct((N_GROUPS, BLOCK, BLOCK), F32),
                   jax.ShapeDtypeStruct((BLOCK, GMLP_W), F32), jax.ShapeDtypeStruct((8, GMLP_W), F32),
                   jax.ShapeDtypeStruct((N_HEADS, BLOCK, 1), F32)]
        + [jax.ShapeDtypeStruct(_rx_shape(p.shape, k), BF16) for p, k in zip(grad_parts, grad_kinds)],
        grid=(n_steps + 1,),
        in_specs=[cur(ATTN_W), cur(2 * KV_W), before, cur(GMLP_W), cur(GMLP_W), cur(D_MODEL),
                  _const_spec((N_HEADS, BLOCK, 2 * BLOCK)), pl.BlockSpec(memory_space=pltpu.SMEM),
                  _const_spec((1, GMLP_W)), _const_spec((1, GMLP_W)), _const_spec((N_GROUPS, BLOCK, BLOCK)),
                  _const_spec((BLOCK, GMLP_W)), _const_spec((GMLP_W, GMLP_W)), _const_spec((1, ATTN_W)),
                  _const_spec((1, GMLP_W))] + [ANY] * n_g,
        out_specs=[cur(ATTN_W), late(2 * KV_W), cur(GMLP_W), cur(GMLP_W),
                   _const_spec((N_HEADS, BLOCK, 2 * BLOCK)), _const_spec((N_GROUPS, BLOCK, BLOCK)),
                   _const_spec((BLOCK, GMLP_W)), _const_spec((8, GMLP_W)), _const_spec((N_HEADS, BLOCK, 1))]
        + [ANY] * n_g,
        scratch_shapes=[pltpu.VMEM((BLOCK, 2 * KV_W), F32), pltpu.VMEM((BLOCK, 2 * KV_W), F32)]
        + _ChipExchange.sems(n_g),
        compiler_params=_params(("arbitrary",)),
    )(q, kv, kv, gu, gv, dmix, bias, sinks, gln_g, gln_b, wsm, bsx, amat, aog, gog, *grad_parts)
    return outs[:9], outs[9:]


def _mix_finalize(gbias, bucket, dws, dbs, dsink):
    def body(gb_ref, bucket_ref, dws_ref, dbs_ref, dsink_ref, tall_ref):
        bk = bucket_ref[...]
        lane = lax.broadcasted_iota(jnp.int32, (N_BUCKETS, LANES), 1)
        rowi = lax.broadcasted_iota(jnp.int32, (N_BUCKETS, LANES), 0)
        drb = jnp.zeros((N_BUCKETS, LANES), F32)
        dsk = jnp.zeros((8, LANES), F32)
        lane8 = lax.broadcasted_iota(jnp.int32, (8, LANES), 1)
        for h in range(N_HEADS):
            g = gb_ref[h]
            for b in range(N_BUCKETS):
                tot = jnp.sum(_colsum(jnp.where(bk == float(b), g, 0.0)), axis=1, keepdims=True)
                drb = jnp.where((lane == h) & (rowi == b), tot, drb)
            sk = jnp.sum(dsink_ref[h], axis=0, keepdims=True)
            dsk = jnp.where(lane8 == h, sk, dsk)
        tall_ref[TALL_RB:TALL_RB + N_BUCKETS, :] = drb
        tall_ref[TALL_SK:TALL_SK + 8, :] = dsk
        ti = lax.broadcasted_iota(jnp.int32, (BLOCK, BLOCK), 0)
        ui = lax.broadcasted_iota(jnp.int32, (BLOCK, BLOCK), 1)
        for g in range(N_GROUPS):
            tall_ref[g * BLOCK:(g + 1) * BLOCK, :] = jnp.where(ti >= ui, dws_ref[g], 0.0)
        gi = lax.broadcasted_iota(jnp.int32, (GMLP_W, LANES), 0) // GROUP_DIM
        li = lax.broadcasted_iota(jnp.int32, (GMLP_W, LANES), 1)
        ind = jnp.where(gi == li, 1.0, 0.0).astype(BF16)
        d = dbs_ref[...]
        hi = d.astype(BF16)
        r1 = d - hi.astype(F32)
        mid = r1.astype(BF16)
        lo = (r1 - mid.astype(F32)).astype(BF16)
        dbsg = _dot(hi, ind) + _dot(mid, ind) + _dot(lo, ind)
        tall_ref[TALL_BS:TALL_BS + N_GROUPS, :] = dbsg.T[:N_GROUPS, :]

    return pl.pallas_call(
        body, name="mix_finalize", out_shape=jax.ShapeDtypeStruct((TALL_ROWS, LANES), F32), grid=(1,),
        in_specs=[_const_spec((N_HEADS, BLOCK, 2 * BLOCK)), _const_spec((BLOCK, 2 * BLOCK)),
                  _const_spec((N_GROUPS, BLOCK, BLOCK)), _const_spec((BLOCK, GMLP_W)),
                  _const_spec((N_HEADS, BLOCK, 1))],
        out_specs=_const_spec((TALL_ROWS, LANES)),
        compiler_params=_params(("arbitrary",)),
    )(gbias, bucket, dws, dbs, dsink)


def _bwd_in(dq, dkv, dgu, dgv, dxa, x, modr, w_in, tm):
    s = x.shape[0]

    def body(dq_ref, dkv_ref, dgu_ref, dgv_ref, dxa_ref, x_ref, mod_ref, w_ref, gx_ref, acc_ref, db_ref):
        @pl.when(pl.program_id(0) == 0)
        def _():
            acc_ref[...] = jnp.zeros_like(acc_ref)
            db_ref[...] = jnp.zeros_like(db_ref)

        dproj = jnp.concatenate([dq_ref[...], dkv_ref[...], dgu_ref[...], dgv_ref[...]], axis=1)
        dh1 = _dot_nt(dproj, w_ref[...])
        gx_ref[...] = dxa_ref[...].astype(F32) + dh1 * (1.0 + mod_ref[1:2, :])
        acc_ref[0:1, :] += _colsum(dh1 * x_ref[...].astype(F32))
        acc_ref[1:2, :] += _colsum(dh1)
        db_ref[0:1, :] += _colsum(dproj.astype(F32))

    row = lambda w: pl.BlockSpec((tm, w), lambda i: (i, 0))
    return pl.pallas_call(
        body, name="bwd_in",
        out_shape=(jax.ShapeDtypeStruct((s, D_MODEL), F32), jax.ShapeDtypeStruct((8, D_MODEL), F32),
                   jax.ShapeDtypeStruct((8, IN_W), F32)),
        grid=(s // tm,),
        in_specs=[row(ATTN_W), row(2 * KV_W), row(GMLP_W), row(GMLP_W), row(D_MODEL), row(D_MODEL),
                  _const_spec((8, D_MODEL)), _const_spec((D_MODEL, IN_W))],
        out_specs=(row(D_MODEL), _const_spec((8, D_MODEL)), _const_spec((8, IN_W))),
        compiler_params=_params(("arbitrary",)),
    )(dq, dkv, dgu, dgv, dxa, x, modr, w_in)


def _wgrad(a, bs, tm, tk, name, owner_blocks=False, gather_vs=()):
    k_all, m = a.shape
    n = sum(b.shape[1] for b in bs)
    nk = k_all // tk
    nm = m // tm
    n_b = len(bs)
    n_v = len(gather_vs)
    wb = n // N_CHIPS

    def body(a_ref, *rest):
        b_refs, v_refs = rest[:n_b], rest[n_b:n_b + n_v]
        o_ref, ob_ref = rest[n_b + n_v:n_b + n_v + 2]
        vg_refs = rest[n_b + n_v + 2:n_b + 2 * n_v + 2]
        i, k = pl.program_id(0), pl.program_id(1)
        if n_v:
            gather = _Gather8(v_refs, vg_refs, *rest[n_b + 2 * n_v + 2:])

            @pl.when((i == 0) & (k == 0))
            def _():
                gather.start()

            @pl.when((i == nm - 1) & (k == 0))
            def _():
                gather.forward()

        @pl.when(k == 0)
        def _():
            o_ref[...] = jnp.zeros_like(o_ref)

        b = b_refs[0][...] if n_b == 1 else jnp.concatenate([r[...] for r in b_refs], axis=1)
        if owner_blocks:
            av = a_ref[...]
            for j in range(N_CHIPS):
                o_ref[j] += _dot_tn(av, b[:, j * wb:(j + 1) * wb])
        else:
            o_ref[...] += _dot_tn(a_ref[...], b)

        @pl.when(k == nk - 1)
        def _():
            ob_ref[...] = o_ref[...].astype(BF16)

        if n_v:
            @pl.when((i == nm - 1) & (k == nk - 1))
            def _():
                gather.finish()

    if owner_blocks:
        out_spec = pl.BlockSpec((N_CHIPS, tm, wb), lambda i, k: (0, i, 0))
        shape = (N_CHIPS, m, wb)
    else:
        out_spec = pl.BlockSpec((tm, n), lambda i, k: (i, 0))
        shape = (m, n)
    outs = pl.pallas_call(
        body, name=name,
        out_shape=[jax.ShapeDtypeStruct(shape, F32), jax.ShapeDtypeStruct(shape, BF16)] + _gathered8_shapes(gather_vs),
        grid=(nm, nk),
        in_specs=[pl.BlockSpec((tk, tm), lambda i, k: (k, i))]
        + [pl.BlockSpec((tk, b.shape[1]), lambda i, k: (k, 0)) for b in bs] + [ANY] * n_v,
        out_specs=[out_spec, out_spec] + [ANY] * n_v,
        scratch_shapes=_Gather8.sems(n_v) if n_v else [],
        compiler_params=_params(("arbitrary", "arbitrary") if n_v else ("parallel", "arbitrary")),
    )(a, *bs, *gather_vs)
    return outs[0], outs[1], outs[2:]


def _adam_math(w, g, m, v):
    m2 = ADAM_B1 * m + (1.0 - ADAM_B1) * g
    v2 = ADAM_B2 * v + (1.0 - ADAM_B2) * (g * g)
    m_hat = m2 / (1.0 - ADAM_B1 ** ADAM_STEP)
    v_hat = v2 / (1.0 - ADAM_B2 ** ADAM_STEP)
    delta = -ADAM_LR * (m_hat / (jnp.sqrt(v_hat) + ADAM_EPS) + ADAM_WD * w)
    return delta, m2, v2


def _adam_halves(w, mine, got, m, v, tr, name):
    r, cc = w.shape
    h = r // 2
    nt = h // tr

    def body(c_ref, w_ref, mine_ref, got_ref, m_ref, v_ref, g_ref, d_ref, m2_ref, v2_ref):
        g = jnp.where(pl.program_id(0) == c_ref[0], mine_ref[...], got_ref[...])
        g_ref[...] = g
        d, m2, v2 = _adam_math(w_ref[...], g, m_ref[...], v_ref[...])
        d_ref[...] = d
        m2_ref[...] = m2
        v2_ref[...] = v2

    full = pl.BlockSpec((tr, cc), lambda hh, i, c_ref: (hh * nt + i, 0))
    half = pl.BlockSpec((tr, cc), lambda hh, i, c_ref: (i, 0))
    shp = jax.ShapeDtypeStruct((r, cc), F32)
    return pl.pallas_call(
        body, name=name, out_shape=(shp, shp, shp, shp),
        grid_spec=pltpu.PrefetchScalarGridSpec(
            num_scalar_prefetch=1, grid=(2, nt), in_specs=[full, half, half, full, full],
            out_specs=(full, full, full, full)),
        compiler_params=_params(("arbitrary", "arbitrary")),
    )(_core_index_scalar(), w, mine, got, m, v)


def _adam_w_ada(sc_t, dmod_cols, w, m, v, tr):
    r, cc = w.shape

    def body(sct_ref, dm_ref, w_ref, m_ref, v_ref, g_ref, d_ref, m2_ref, v2_ref):
        g = sct_ref[:, 0:1] * dm_ref[0:1, :]
        for k in range(1, N_DEV):
            g = g + sct_ref[:, k:k + 1] * dm_ref[k:k + 1, :]
        g_ref[...] = g
        d, m2, v2 = _adam_math(w_ref[...], g, m_ref[...], v_ref[...])
        d_ref[...] = d
        m2_ref[...] = m2
        v2_ref[...] = v2

    spec = pl.BlockSpec((tr, cc), lambda i: (i, 0))
    shp = jax.ShapeDtypeStruct((r, cc), F32)
    return pl.pallas_call(
        body, name="adam_w_ada", out_shape=(shp, shp, shp, shp), grid=(r // tr,),
        in_specs=[pl.BlockSpec((tr, N_DEV), lambda i: (i, 0)), _const_spec((N_DEV, cc)), spec, spec, spec],
        out_specs=(spec, spec, spec, spec), compiler_params=_params(("parallel",)),
    )(sc_t, dmod_cols, w, m, v)


def _pack_wide(acc_i, acc_m, acc_f, db_in, vec):
    arrs = [acc_i, acc_m, acc_f, db_in, vec]
    i_, m_, f_, b_, v_ = range(5)
    src = {"b_in": (b_, 0), "ln1_g": (m_, 2), "ln1_b": (m_, 3), "ln2_g": (f_, 1), "ln2_b": (f_, 2),
           "gmlp_ln_g": (v_, 2), "gmlp_ln_b": (v_, 3), "attn_out_g": (v_, 0), "gmlp_out_g": (v_, 1), "loss": (f_, 0)}
    dmod = [(i_, 1), (i_, 0), (m_, 4), (m_, 1), (m_, 0), (f_, 3)]

    def body(*refs):
        ins, wide_ref = refs[:5], refs[5]
        wide_ref[...] = jnp.zeros_like(wide_ref)
        for k, (a, row) in enumerate(dmod):
            wide_ref[0:1, k * D_MODEL:(k + 1) * D_MODEL] = ins[a][row:row + 1, :]
        for name, (a, row) in src.items():
            r, off, n = WIDE_LAYOUT[name]
            wide_ref[r:r + 1, off:off + n] = ins[a][row:row + 1, :]

    return pl.pallas_call(
        body, name="pack_wide", out_shape=jax.ShapeDtypeStruct((8, WIDE_W), F32), grid=(1,),
        in_specs=[_const_spec(a.shape) for a in arrs], out_specs=_const_spec((8, WIDE_W)),
        compiler_params=_params(("arbitrary",)),
    )(*arrs)


def _adam_small(gw, gt, wide_wmv, w_s, b_s, rel_bias, sinks):
    names = list(WIDE_PARAMS)
    tall = [("gmlp_w_s", w_s), ("gmlp_b_s", b_s), ("rel_bias", rel_bias), ("attn_sinks", sinks)]
    ins = [gw, gt]
    for n in names:
        ins += list(wide_wmv[n])
    for _, t in tall:
        ins += list(t)
    n_in = len(ins)

    def body(*refs):
        gw_ref, gt_ref = refs[0], refs[1]
        wmv = refs[2:n_in]
        dmod_ref, loss_ref = refs[n_in], refs[n_in + 1]
        outs = refs[n_in + 2:]

        def tall_sum(r0, nr):
            g = gt_ref[r0:r0 + nr, :]
            for d in range(1, N_DEV):
                g = g + gt_ref[d * TALL_ROWS + r0:d * TALL_ROWS + r0 + nr, :]
            return g

        def emit(k, g, w_ref, m_ref, v_ref):
            d, m2, v2 = _adam_math(w_ref[...], g, m_ref[...], v_ref[...])
            outs[4 * k][...] = g
            outs[4 * k + 1][...] = d
            outs[4 * k + 2][...] = m2
            outs[4 * k + 3][...] = v2

        gsum = gw_ref[0:8, :]
        for d in range(1, N_DEV):
            gsum = gsum + gw_ref[8 * d:8 * d + 8, :]
        for d in range(N_DEV):
            dmod_ref[d:d + 1, :] = gw_ref[8 * d:8 * d + 1, :]
        for k, n in enumerate(names):
            r, off, sz = WIDE_LAYOUT[n]
            emit(k, gsum[r:r + 1, off:off + sz], *wmv[3 * k:3 * k + 3])
        r, off, sz = WIDE_LAYOUT["loss"]
        tot = jnp.sum(gsum[r:r + 1, off:off + sz], axis=1, keepdims=True)
        loss_ref[...] = jnp.broadcast_to(tot * (0.5 / D_MODEL), loss_ref.shape)

        k0 = len(names)
        ws_refs = wmv[3 * k0:3 * k0 + 3]
        for g in range(N_GROUPS):
            rows = slice(g * BLOCK, (g + 1) * BLOCK)
            gg = tall_sum(g * BLOCK, BLOCK)
            d, m2, v2 = _adam_math(ws_refs[0][rows, :], gg, ws_refs[1][rows, :], ws_refs[2][rows, :])
            outs[4 * k0][rows, :] = gg
            outs[4 * k0 + 1][rows, :] = d
            outs[4 * k0 + 2][rows, :] = m2
            outs[4 * k0 + 3][rows, :] = v2
        emit(k0 + 1, tall_sum(TALL_BS, N_GROUPS), *wmv[3 * (k0 + 1):3 * (k0 + 1) + 3])
        emit(k0 + 2, tall_sum(TALL_RB, N_BUCKETS)[:, :N_HEADS], *wmv[3 * (k0 + 2):3 * (k0 + 2) + 3])
        emit(k0 + 3, tall_sum(TALL_SK, 8)[0:1, :N_HEADS], *wmv[3 * (k0 + 3):3 * (k0 + 3) + 3])

    out_shapes = [jax.ShapeDtypeStruct((N_DEV, WIDE_W), F32), jax.ShapeDtypeStruct((8, LANES), F32)]
    for n in names:
        out_shapes += [jax.ShapeDtypeStruct(wide_wmv[n][0].shape, F32)] * 4
    for _, t in tall:
        out_shapes += [jax.ShapeDtypeStruct(t[0].shape, F32)] * 4
    res = pl.pallas_call(
        body, name="adam_small", out_shape=out_shapes, grid=(1,),
        in_specs=[_const_spec(a.shape) for a in ins], out_specs=[_const_spec(o.shape) for o in out_shapes],
        compiler_params=_params(("arbitrary",)),
    )(*ins)
    out = {}
    for k, n in enumerate(names + [t[0] for t in tall]):
        out[n] = tuple(res[2 + 4 * k:6 + 4 * k])
    return res[0], res[1], out


def kernel(x, c, rel_bias, w_ada, b_ada, w_in, b_in, attn_sinks, gmlp_ln_g, gmlp_ln_b, gmlp_w_s, gmlp_b_s, attn_out_g, gmlp_out_g, w_out, ln1_g, ln1_b, w_gate_up, w_down, ln2_g, ln2_b, loss_target, m_rel_bias, m_w_ada, m_b_ada, m_w_in, m_b_in, m_attn_sinks, m_gmlp_ln_g, m_gmlp_ln_b, m_gmlp_w_s, m_gmlp_b_s, m_attn_out_g, m_gmlp_out_g, m_w_out, m_ln1_g, m_ln1_b, m_w_gate_up, m_w_down, m_ln2_g, m_ln2_b, v_rel_bias, v_w_ada, v_b_ada, v_w_in, v_b_in, v_attn_sinks, v_gmlp_ln_g, v_gmlp_ln_b, v_gmlp_w_s, v_gmlp_b_s, v_attn_out_g, v_gmlp_out_g, v_w_out, v_ln1_g, v_ln1_b, v_w_gate_up, v_w_down, v_ln2_g, v_ln2_b):
    ix, iy, ic = _my_pos()
    chip = 2 * ix + iy
    dev = 4 * ix + 2 * iy + ic
    s = x.shape[1]
    xs = x[0]
    tgt = loss_target[0]
    tm_big = min(512, s)
    tm_ffn = min(FFN_SUB, s)
    n_ada = w_ada.shape[2]

    w_in_s, w_out_s = w_in[0].astype(BF16), w_out[0].astype(BF16)
    w_gu_s, w_dn_s = w_gate_up[0].astype(BF16), w_down[0].astype(BF16)
    sc_all, mod_rows, (w_in_g, w_out_g) = _prologue(
        jnp.pad(c, ((0, 7), (0, 0))), w_ada[0], lax.dynamic_slice_in_dim(b_ada, chip * n_ada, n_ada, axis=1),
        [w_in_s, w_out_s])
    mod_all = mod_rows.reshape(N_DEV, N_DEV, -1)
    mod_row = lax.dynamic_index_in_dim(mod_all[0::2], dev, axis=1, keepdims=False)
    modr = jnp.pad(mod_row.reshape(6, D_MODEL), ((0, 2), (0, 0)))
    w_in_g = _insert_own(w_in_g, w_in_s, "blk", chip)
    w_in_f = jnp.transpose(w_in_g, (1, 0, 2)).reshape(D_MODEL, IN_W)

    bucket = _bucket_table()
    bias, wsm = _prep_tables(bucket, rel_bias, gmlp_w_s[0])
    bsx = jnp.repeat(gmlp_b_s[0].T, GROUP_DIM, axis=1)
    amat = _group_mean_matrix()
    sinks = attn_sinks[0]

    (h1, q, kv, gu, gv, xb), (w_dn_g,) = _fwd_in(xs, modr, w_in_f, b_in, tm_big, [w_dn_s], ["blk"])
    w_out_f = _insert_own(w_out_g, w_out_s, "blk", chip).reshape(D_MODEL, D_MODEL)
    (x1, x1b, y, mixed), (w_gu_g,) = _fwd_mix(
        q, kv, gu, gv, xs, modr, bias, sinks, gmlp_ln_g, gmlp_ln_b, wsm, bsx, amat, attn_out_g, gmlp_out_g, w_out_f,
        ln1_g, ln1_b, tm_big, [w_gu_s], ["blk"])
    assert w_gate_up.shape[2] == FF_CHUNK
    w_gu_f = _insert_own(w_gu_g, w_gu_s, "blk", chip)
    w_dn_f = _insert_own(w_dn_g, w_dn_s, "blk", chip).reshape(D_FF, D_MODEL)
    h2, act, dy2, dx1a, acc_f = _fwd_ffn(x1, tgt, modr, ln2_g, ln2_b, w_gu_f, w_dn_f, tm_ffn)

    a_act, dgu_ff, dh2 = _bwd_ffn(dy2, act, w_gu_f, w_dn_f, min(FFN_SUB, s))
    g_dn, g_dn_b, _ = _wgrad(a_act, [dy2], D_FF // 2, min(1024, s), "wgrad_down")
    g_gu, g_gu_b, _ = _wgrad(h2, [dgu_ff], 512, min(512, s), "wgrad_gate_up")
    blk3 = lambda a, rows: a.reshape(N_CHIPS, rows, a.shape[1])
    (dxa, dy, dmix, acc_m), (got_dn, got_gu) = _bwd_mid(
        dh2, dx1a, x1b, xs, y, modr, ln1_g, w_out_f, tm_big, [blk3(g_dn_b, D_FF // N_CHIPS), g_gu_b], ["blk", "cols"])
    g_out, g_out_b, _ = _wgrad(mixed, [dy], 512, min(2048, s), "wgrad_out")
    (got_out,) = _swap_halves([blk3(g_out_b, D_MODEL // N_CHIPS)], ["blk"], "rs_swap_out")
    kinds_a = ["blk", "cols", "blk"]
    fulls_a = [blk3(g_dn, D_FF // N_CHIPS), g_gu, blk3(g_out, D_MODEL // N_CHIPS)]
    gots_a = [got_dn, got_gu, got_out]
    parts_a = [_add_halves(f, g, k, "rs_add_a%d" % i) for i, (f, g, k) in enumerate(zip(fulls_a, gots_a, kinds_a))]
    (dq, dkv, dgu, dgv, gbias, dws, dbs, vec, dsink), rxs_a = _bwd_mix(
        q, kv, gu, gv, dmix, bias, sinks, gmlp_ln_g, gmlp_ln_b, wsm, bsx, amat, attn_out_g, gmlp_out_g,
        [p[1] for p in parts_a], kinds_a)
    tall_g = _mix_finalize(gbias, bucket, dws, dbs, dsink)
    grad_x, acc_i, db_in = _bwd_in(dq, dkv, dgu, dgv, dxa, xb, modr, w_in_f, tm_big)

    wide_g = _pack_wide(acc_i, acc_m, acc_f, db_in, vec)
    full_in, full_in_b, (gw, gt) = _wgrad(h1, [dq, dkv, dgu, dgv], 512, min(1024, s), "wgrad_in", owner_blocks=True,
                                          gather_vs=[wide_g, tall_g])
    (got_in,) = _swap_halves([full_in_b], ["blk"], "rs_swap_in")
    part_in = _add_halves(full_in, got_in, "blk", "rs_add_in")
    (rx_in,) = _exchange_chip_partials([part_in[1]], ["blk"], "rs_chips_in")
    wide_wmv = {"b_ada": (b_ada, m_b_ada, v_b_ada), "b_in": (b_in, m_b_in, v_b_in),
                "ln1_g": (ln1_g, m_ln1_g, v_ln1_g), "ln1_b": (ln1_b, m_ln1_b, v_ln1_b),
                "ln2_g": (ln2_g, m_ln2_g, v_ln2_g), "ln2_b": (ln2_b, m_ln2_b, v_ln2_b),
                "gmlp_ln_g": (gmlp_ln_g, m_gmlp_ln_g, v_gmlp_ln_g), "gmlp_ln_b": (gmlp_ln_b, m_gmlp_ln_b, v_gmlp_ln_b),
                "attn_out_g": (attn_out_g, m_attn_out_g, v_attn_out_g),
                "gmlp_out_g": (gmlp_out_g, m_gmlp_out_g, v_gmlp_out_g)}
    rows2 = lambda a: a.reshape(-1, a.shape[-1])
    dmod_all, loss_t, small = _adam_small(
        gw, gt, wide_wmv, tuple(rows2(a) for a in (gmlp_w_s, m_gmlp_w_s, v_gmlp_w_s)),
        tuple(rows2(a) for a in (gmlp_b_s, m_gmlp_b_s, v_gmlp_b_s)), (rel_bias, m_rel_bias, v_rel_bias),
        (attn_sinks, m_attn_sinks, v_attn_sinks))
    loss = loss_t[0, 0]

    dmod_cols = lax.dynamic_slice_in_dim(dmod_all, chip * n_ada, n_ada, axis=1)
    g_ada, d_ada, m_ada, v_ada = _adam_w_ada(sc_all.T, dmod_cols, w_ada[0], m_w_ada[0], v_w_ada[0], 256)

    sums = [(parts_a[0][0], rxs_a[0], 176), (parts_a[1][0], rxs_a[1], 256), (parts_a[2][0], rxs_a[2], 128),
            (part_in[0], rx_in, 256)]
    mine = [_sum_chips(p, rx, tr, "rs_sum_%d" % i) for i, (p, rx, tr) in enumerate(sums)]
    got = _share_halves(mine, "rs_share")

    gs_dn, d_dn, m_dn, v_dn = _adam_halves(w_down[0], mine[0], got[0], m_w_down[0], v_w_down[0], 176, "adam_w_down")
    gs_gu, d_gu, m_gu, v_gu = _adam_halves(w_gate_up[0], mine[1], got[1], m_w_gate_up[0], v_w_gate_up[0], 256,
                                           "adam_w_gate_up")
    gs_out, d_out, m_out, v_out = _adam_halves(w_out[0], mine[2], got[2], m_w_out[0], v_w_out[0], 128, "adam_w_out")
    gs_in, d_in, m_in, v_in = _adam_halves(w_in[0], mine[3], got[3], m_w_in[0], v_w_in[0], 256, "adam_w_in")

    big = {"w_ada": (g_ada, d_ada, m_ada, v_ada), "w_in": (gs_in, d_in, m_in, v_in), "w_out": (gs_out, d_out, m_out, v_out),
           "w_gate_up": (gs_gu, d_gu, m_gu, v_gu), "w_down": (gs_dn, d_dn, m_dn, v_dn)}
    order = ["rel_bias", "w_ada", "b_ada", "w_in", "b_in", "attn_sinks", "gmlp_ln_g", "gmlp_ln_b", "gmlp_w_s", "gmlp_b_s",
             "attn_out_g", "gmlp_out_g", "w_out", "ln1_g", "ln1_b", "w_gate_up", "w_down", "ln2_g", "ln2_b"]
    shapes = {"gmlp_w_s": gmlp_w_s.shape, "gmlp_b_s": gmlp_b_s.shape}
    outs = [loss, grad_x[None]]
    for k in range(4):
        for name in order:
            if name in big:
                outs.append(big[name][k][None])
            elif name in shapes:
                outs.append(small[name][k].reshape(shapes[name]))
            else:
                outs.append(small[name][k])
    return tuple(outs)
```

```python
import math

import numpy as np
import jax
import jax.numpy as jnp
from jax import lax
from jax.experimental import pallas as pl
from jax.experimental.pallas import tpu as pltpu

F32 = jnp.float32
BF16 = jnp.bfloat16
MESH = pl.DeviceIdType.MESH

D_MODEL = 1024
N_HEADS = 8
N_KV = 2
HEAD_DIM = 64
ATTN_W = N_HEADS * HEAD_DIM
KV_W = N_KV * HEAD_DIM
N_GROUPS = 8
GROUP_DIM = 64
GMLP_W = N_GROUPS * GROUP_DIM
IN_W = ATTN_W + 2 * KV_W + 2 * GMLP_W
BLOCK = 128
N_BUCKETS = 32
MAX_DISTANCE = 128
D_FF = 2816
ALPHA = 2.0 ** 0.25
LN_EPS = 1e-5
NEG_INF = -1e30
ADAM_LR, ADAM_B1, ADAM_B2, ADAM_EPS, ADAM_WD, ADAM_STEP = 0.001, 0.9, 0.999, 1e-8, 0.01, 10
N_CHIPS = 4
N_DEV = 8
LANES = 128
V7X_VMEM_LIMIT = 56 * 2 ** 20
GELU_C = math.sqrt(2.0 / math.pi)
Q_SCALE = HEAD_DIM ** -0.5
ANY = pl.BlockSpec(memory_space=pl.ANY)

TALL_BS = N_GROUPS * BLOCK
TALL_RB = TALL_BS + 8
TALL_SK = TALL_RB + N_BUCKETS
TALL_ROWS = TALL_SK + 8
WIDE_W = 6 * D_MODEL
WIDE_LAYOUT = {
    "b_ada": (0, 0, 6 * D_MODEL),
    "b_in": (1, 0, IN_W), "ln1_g": (1, IN_W, D_MODEL), "ln1_b": (1, IN_W + D_MODEL, D_MODEL),
    "ln2_g": (1, IN_W + 2 * D_MODEL, D_MODEL), "ln2_b": (1, IN_W + 3 * D_MODEL, D_MODEL),
    "gmlp_ln_g": (2, 0, GMLP_W), "gmlp_ln_b": (2, GMLP_W, GMLP_W), "attn_out_g": (2, 2 * GMLP_W, ATTN_W),
    "gmlp_out_g": (2, 2 * GMLP_W + ATTN_W, GMLP_W), "loss": (2, 3 * GMLP_W + ATTN_W, D_MODEL)}
WIDE_PARAMS = tuple(n for n in WIDE_LAYOUT if n != "loss")


def _params(sem=None):
    return pltpu.CompilerParams(dimension_semantics=sem, vmem_limit_bytes=V7X_VMEM_LIMIT)


def _const_spec(shape, single=False):
    nd = len(shape)
    if single:
        return pl.BlockSpec(shape, lambda *_: (0,) * nd, pipeline_mode=pl.Buffered(1))
    return pl.BlockSpec(shape, lambda *_: (0,) * nd)


def _dot(a, b):
    return jnp.dot(a, b, preferred_element_type=F32)


def _dot_nt(a, b):
    return lax.dot_general(a, b, (((1,), (1,)), ((), ())), preferred_element_type=F32)


def _dot_tn(a, b):
    return lax.dot_general(a, b, (((0,), (0,)), ((), ())), preferred_element_type=F32)


def _gelu(x):
    t = jnp.tanh(GELU_C * (x + 0.044715 * x * x * x))
    return 0.5 * x * (1.0 + t), t


def _gelu_grad(x, t):
    return 0.5 * (1.0 + t) + 0.5 * x * (1.0 - t * t) * GELU_C * (1.0 + 3.0 * 0.044715 * x * x)


def _split_dot(x, a):
    hi = x.astype(BF16)
    lo = (x - hi.astype(F32)).astype(BF16)
    return _dot(hi, a) + _dot(lo, a)


def _group_mean_matrix():
    g = np.arange(GMLP_W) // GROUP_DIM
    return jnp.asarray((g[:, None] == g[None, :]).astype(np.float32) / GROUP_DIM, dtype=BF16)


def _ln_stats(z):
    mu = jnp.mean(z, axis=-1, keepdims=True)
    d = z - mu
    var = jnp.mean(d * d, axis=-1, keepdims=True)
    rstd = lax.rsqrt(var + LN_EPS)
    return d * rstd, rstd


def _ln_bwd(dxhat, xhat, rstd):
    m1 = jnp.mean(dxhat, axis=-1, keepdims=True)
    m2 = jnp.mean(dxhat * xhat, axis=-1, keepdims=True)
    return rstd * (dxhat - m1 - xhat * m2)


def _colsum(x):
    return jnp.sum(x, axis=0, keepdims=True)


def _my_pos():
    return lax.axis_index("x"), lax.axis_index("y"), lax.axis_index("c")


def _other_chips(x, y):
    return [(1 - x, y), (x, 1 - y), (1 - x, 1 - y)]


def _chip_index_scalar():
    ix, iy, _ = _my_pos()
    return jnp.reshape(2 * ix + iy, (1,)).astype(jnp.int32)


def _core_index_scalar():
    return jnp.reshape(lax.axis_index("c"), (1,)).astype(jnp.int32)


class _Gather8:
    def __init__(self, x_refs, out_refs, send_sems, recv_sems, local_sems):
        self.x_refs, self.out_refs = x_refs, out_refs
        self.send_sems, self.recv_sems, self.local_sems = send_sems, recv_sems, local_sems
        self.x, self.y, self.c = _my_pos()
        self.me, self.sibling = (self.x, self.y, self.c), (self.x, self.y, 1 - self.c)
        self.chips = _other_chips(self.x, self.y)

    def _rows(self, a, px, py, pc):
        m_per = self.x_refs[a].shape[0]
        return self.out_refs[a].at[pl.ds((4 * px + 2 * py + pc) * m_per, m_per), :]

    def _copy(self, a, k, block, to, src=None):
        return pltpu.make_async_remote_copy(
            src_ref=self._rows(a, *block) if src is None else src, dst_ref=self._rows(a, *block),
            send_sem=self.send_sems.at[7 * a + k], recv_sem=self.recv_sems.at[7 * a + k], device_id=to,
            device_id_type=MESH)

    def _local(self, a):
        return pltpu.make_async_copy(self.x_refs[a], self._rows(a, *self.me), self.local_sems.at[a])

    def start(self):
        for a in range(len(self.x_refs)):
            self._local(a).start()
            self._copy(a, 0, self.me, self.sibling, src=self.x_refs[a]).start()
            for j, chip in enumerate(self.chips):
                self._copy(a, 1 + j, self.me, (*chip, self.c), src=self.x_refs[a]).start()

    def forward(self):
        for a in range(len(self.x_refs)):
            for j, chip in enumerate(self.chips):
                self._copy(a, 1 + j, (*chip, self.c), self.me).wait_recv()
                self._copy(a, 4 + j, (*chip, self.c), self.sibling).start()

    def finish(self):
        for a in range(len(self.x_refs)):
            self._copy(a, 0, self.sibling, self.me).wait_recv()
            for j, chip in enumerate(self.chips):
                self._copy(a, 4 + j, (*chip, 1 - self.c), self.me).wait_recv()
        for a in range(len(self.x_refs)):
            for k in range(7):
                self._copy(a, k, self.me, self.me).wait_send()
            self._local(a).wait()

    @staticmethod
    def sems(n_v):
        return [pltpu.SemaphoreType.DMA((7 * n_v,)), pltpu.SemaphoreType.DMA((7 * n_v,)),
                pltpu.SemaphoreType.DMA((n_v,))]


def _gathered8_shapes(vs):
    return [jax.ShapeDtypeStruct((N_DEV * v.shape[0], v.shape[1]), v.dtype) for v in vs]


VMEM_WHOLE = pl.BlockSpec(memory_space=pltpu.VMEM)


def _prologue(c_pad, w_ada_s, b_ada_s, shards):
    n = w_ada_s.shape[1]
    n_w = len(shards)

    def body(c_ref, w_ref, b_ref, *rest):
        shard_refs = rest[:n_w]
        sc_ref, modc_ref, modg_ref = rest[n_w:n_w + 3]
        gathered_refs = rest[n_w + 3:2 * n_w + 3]
        call_ref = rest[2 * n_w + 3]
        sems = rest[2 * n_w + 4:]
        weights = _WeightGather(shard_refs, gathered_refs, ["blk"] * n_w, sems[0], sems[1])
        gather_c = _Gather8([c_ref], [call_ref], sems[2], sems[3], sems[4])
        gather_mod = _Gather8([modc_ref], [modg_ref], sems[5], sems[6], sems[7])
        weights.start()
        gather_c.start()
        gather_c.forward()
        gather_c.finish()
        cv = call_ref[...]
        sc = cv * _sigmoid(cv)
        a_hi = sc.astype(BF16)
        a_lo = (sc - a_hi.astype(F32)).astype(BF16)
        w = w_ref[...]
        w_hi = w.astype(BF16)
        w_lo = (w - w_hi.astype(F32)).astype(BF16)
        mod = _dot(a_hi, w_hi) + _dot(a_hi, w_lo) + _dot(a_lo, w_hi) + b_ref[...]
        for d in range(N_DEV):
            sc_ref[d:d + 1, :] = sc[8 * d:8 * d + 1, :]
            modc_ref[d:d + 1, :] = mod[8 * d:8 * d + 1, :]
        gather_mod.start()
        weights.forward()
        gather_mod.forward()
        gather_mod.finish()
        weights.forward_diagonal()
        weights.finish()

    outs = pl.pallas_call(
        body, name="prologue",
        out_shape=[jax.ShapeDtypeStruct((N_DEV, D_MODEL), F32), jax.ShapeDtypeStruct((N_DEV, n), F32),
                   jax.ShapeDtypeStruct((N_DEV * N_DEV, n), F32)]
        + [jax.ShapeDtypeStruct(_gathered_shape(sh, "blk"), BF16) for sh in shards],
        in_specs=[VMEM_WHOLE, VMEM_WHOLE, VMEM_WHOLE] + [ANY] * n_w,
        out_specs=[VMEM_WHOLE, VMEM_WHOLE, VMEM_WHOLE] + [ANY] * n_w,
        scratch_shapes=[pltpu.VMEM((N_DEV * 8, D_MODEL), F32)] + _WeightGather.sems(n_w) + _Gather8.sems(1)
        + _Gather8.sems(1),
        compiler_params=pltpu.CompilerParams(vmem_limit_bytes=V7X_VMEM_LIMIT),
    )(c_pad, w_ada_s, b_ada_s, *shards)
    return outs[0], outs[2], outs[3:]


def _gathered_shape(shard, kind):
    r, cc = shard.shape
    return (N_CHIPS, r, cc) if kind == "blk" else (r, N_CHIPS * cc)


class _WeightGather:
    N_SEM = 8

    def __init__(self, shards, gathered, kinds, send_sems, recv_sems):
        self.shards, self.gathered, self.kinds = shards, gathered, kinds
        self.send_sems, self.recv_sems = send_sems, recv_sems
        self.x, self.y, self.c = _my_pos()
        self.me, self.sibling = (self.x, self.y, self.c), (self.x, self.y, 1 - self.c)
        self.nbr = ((1 - self.x, self.y), (self.x, 1 - self.y))
        self.diag = 2 * (1 - self.x) + (1 - self.y)

    def _dst(self, a, chip, pc, quarter=None):
        r, cc = self.shards[a].shape
        h = r // 2
        row0, rows = pc * h, h
        if quarter is not None:
            row0, rows = pc * h + quarter * (h // 2), h // 2
        g = self.gathered[a]
        if self.kinds[a] == "blk":
            return g.at[chip, pl.ds(row0, rows), :]
        return g.at[pl.ds(row0, rows), pl.ds(chip * cc, cc)]

    def _copy(self, a, k, region, to, src=None):
        return pltpu.make_async_remote_copy(
            src_ref=region if src is None else src, dst_ref=region, send_sem=self.send_sems.at[a * self.N_SEM + k],
            recv_sem=self.recv_sems.at[a * self.N_SEM + k], device_id=to, device_id_type=MESH)

    def _arrays(self):
        return range(len(self.shards))

    def start(self):
        my_chip = 2 * self.x + self.y
        for a in self._arrays():
            h = self.shards[a].shape[0] // 2
            mine = self.shards[a].at[pl.ds(self.c * h, h), :]
            for j, chip in enumerate(self.nbr):
                self._copy(a, j, self._dst(a, my_chip, self.c), (*chip, self.c), src=mine).start()

    def forward(self):
        for a in self._arrays():
            for j, chip in enumerate(self.nbr):
                cj = 2 * chip[0] + chip[1]
                half = self._dst(a, cj, self.c)
                self._copy(a, j, half, self.me).wait_recv()
                self._copy(a, 2 + j, half, self.sibling).start()
                other = self.nbr[1 - j]
                self._copy(a, 4 + j, self._dst(a, cj, self.c, quarter=j), (*other, self.c)).start()

    def forward_diagonal(self):
        for a in self._arrays():
            for j in range(2):
                quarter = self._dst(a, self.diag, self.c, quarter=j)
                self._copy(a, 4 + j, quarter, self.me).wait_recv()
                self._copy(a, 6 + j, quarter, self.sibling).start()

    def finish(self):
        for a in self._arrays():
            for j, chip in enumerate(self.nbr):
                self._copy(a, 2 + j, self._dst(a, 2 * chip[0] + chip[1], 1 - self.c), self.me).wait_recv()
                self._copy(a, 6 + j, self._dst(a, self.diag, 1 - self.c, quarter=j), self.me).wait_recv()
        for a in self._arrays():
            half = self._dst(a, self.diag, self.c)
            quarter = self._dst(a, self.diag, self.c, quarter=0)
            for k in range(self.N_SEM):
                self._copy(a, k, half if k < 4 else quarter, self.me).wait_send()

    @classmethod
    def sems(cls, n_arr):
        return [pltpu.SemaphoreType.DMA((n_arr * cls.N_SEM,)), pltpu.SemaphoreType.DMA((n_arr * cls.N_SEM,))]


def _insert_own(gathered, shard, kind, chip):
    if kind == "blk":
        return lax.dynamic_update_slice(gathered, shard[None], (chip, 0, 0))
    return lax.dynamic_update_slice(gathered, shard, (0, chip * shard.shape[1]))


def _half_of_full(ref, kind, pc):
    if kind == "blk":
        h = ref.shape[1] // 2
        return ref.at[:, pl.ds(pc * h, h), :]
    h = ref.shape[0] // 2
    return ref.at[pl.ds(pc * h, h), :]


def _half_shape(shape, kind):
    return (shape[0], shape[1] // 2, shape[2]) if kind == "blk" else (shape[0] // 2, shape[1])


class _HalfSwap:
    def __init__(self, ins, outs, kinds, send_sems, recv_sems):
        self.ins, self.outs, self.kinds = ins, outs, kinds
        self.send_sems, self.recv_sems = send_sems, recv_sems
        self.x, self.y, self.c = _my_pos()

    def _copies(self):
        for a in range(len(self.ins)):
            yield pltpu.make_async_remote_copy(
                src_ref=_half_of_full(self.ins[a], self.kinds[a], 1 - self.c), dst_ref=self.outs[a],
                send_sem=self.send_sems.at[a], recv_sem=self.recv_sems.at[a],
                device_id=(self.x, self.y, 1 - self.c), device_id_type=MESH)

    def start(self):
        for cp in self._copies():
            cp.start()

    def wait(self):
        for cp in self._copies():
            cp.wait()

    @staticmethod
    def sems(n_arr):
        return [pltpu.SemaphoreType.DMA((n_arr,)), pltpu.SemaphoreType.DMA((n_arr,))]

    @staticmethod
    def out_shapes(fulls, kinds):
        return [jax.ShapeDtypeStruct(_half_shape(a.shape, k), a.dtype) for a, k in zip(fulls, kinds)]


def _swap_halves(fulls_bf16, kinds, name):
    n_arr = len(fulls_bf16)

    def body(*refs):
        swap = _HalfSwap(refs[:n_arr], refs[n_arr:2 * n_arr], kinds, *refs[2 * n_arr:])
        swap.start()
        swap.wait()

    return pl.pallas_call(
        body, name=name, out_shape=_HalfSwap.out_shapes(fulls_bf16, kinds),
        in_specs=[ANY] * n_arr, out_specs=[ANY] * n_arr, scratch_shapes=_HalfSwap.sems(n_arr),
    )(*fulls_bf16)


def _add_halves(full, got, kind, name):
    hs = _half_shape(full.shape, kind)

    def body(pos_ref, a_ref, b_ref, o_ref, ob_ref):
        p = a_ref[...] + b_ref[...].astype(F32)
        ob_ref[...] = p.astype(BF16)

        @pl.when(pl.program_id(0) == pos_ref[1])
        def _():
            o_ref[...] = p.reshape(o_ref.shape)

    if kind == "blk":
        nb, h, cc = hs
        own = pl.BlockSpec((1, h, cc), lambda b, pos_ref: (b, pos_ref[0], 0))
        other = pl.BlockSpec((1, h, cc), lambda b, pos_ref: (b, 0, 0))
    else:
        h, cc = hs[0], hs[1] // N_CHIPS
        own = pl.BlockSpec((h, cc), lambda b, pos_ref: (pos_ref[0], b))
        other = pl.BlockSpec((h, cc), lambda b, pos_ref: (0, b))
    pos = jnp.concatenate([_core_index_scalar(), _chip_index_scalar()])
    return pl.pallas_call(
        body, name=name, out_shape=(jax.ShapeDtypeStruct((h, cc), F32), jax.ShapeDtypeStruct(hs, BF16)),
        grid_spec=pltpu.PrefetchScalarGridSpec(
            num_scalar_prefetch=1, grid=(N_CHIPS,), in_specs=[own, other],
            out_specs=(pl.BlockSpec((h, cc), lambda b, pos_ref: (0, 0)), other)),
        compiler_params=_params(("arbitrary",)),
    )(pos, full, got)


def _rx_shape(part_shape, kind):
    if kind == "blk":
        return (3, part_shape[1], part_shape[2])
    return (3, part_shape[0], part_shape[1] // N_CHIPS)


class _ChipExchange:
    def __init__(self, parts, rxs, kinds, send_sems, recv_sems):
        self.parts, self.rxs, self.kinds = parts, rxs, kinds
        self.send_sems, self.recv_sems = send_sems, recv_sems
        self.x, self.y, self.c = _my_pos()
        self.chips = _other_chips(self.x, self.y)

    def _copies(self):
        for a in range(len(self.parts)):
            for j, chip in enumerate(self.chips):
                cj = 2 * chip[0] + chip[1]
                if self.kinds[a] == "blk":
                    src = self.parts[a].at[cj]
                else:
                    cc = self.parts[a].shape[1] // N_CHIPS
                    src = self.parts[a].at[:, pl.ds(cj * cc, cc)]
                yield pltpu.make_async_remote_copy(
                    src_ref=src, dst_ref=self.rxs[a].at[j], send_sem=self.send_sems.at[a * 3 + j],
                    recv_sem=self.recv_sems.at[a * 3 + j], device_id=(*chip, self.c), device_id_type=MESH)

    def start(self):
        for cp in self._copies():
            cp.start()

    def wait(self):
        for cp in self._copies():
            cp.wait_recv()
        for cp in self._copies():
            cp.wait_send()

    @staticmethod
    def sems(n_arr):
        return [pltpu.SemaphoreType.DMA((n_arr * 3,)), pltpu.SemaphoreType.DMA((n_arr * 3,))]


def _exchange_chip_partials(parts, kinds, name):
    n_arr = len(parts)

    def body(*refs):
        exchange = _ChipExchange(refs[:n_arr], refs[n_arr:2 * n_arr], kinds, *refs[2 * n_arr:])
        exchange.start()
        exchange.wait()

    return pl.pallas_call(
        body, name=name,
        out_shape=[jax.ShapeDtypeStruct(_rx_shape(p.shape, k), BF16) for p, k in zip(parts, kinds)],
        in_specs=[ANY] * n_arr, out_specs=[ANY] * n_arr, scratch_shapes=_ChipExchange.sems(n_arr),
    )(*parts)


def _sum_chips(part, rx, tr, name):
    _, h, cc = rx.shape
    flips = (2, 1, 3)

    def body(chip_ref, p_ref, rx_ref, o_ref):
        own = p_ref[...]
        for mc in range(N_CHIPS):
            @pl.when(chip_ref[0] == mc)
            def _():
                terms = sorted([(mc, None)] + [(mc ^ f, j) for j, f in enumerate(flips)])
                acc = None
                for _, j in terms:
                    t = own if j is None else rx_ref[j].astype(F32)
                    acc = t if acc is None else acc + t
                o_ref[...] = acc

    return pl.pallas_call(
        body, name=name, out_shape=jax.ShapeDtypeStruct((h, cc), F32),
        grid_spec=pltpu.PrefetchScalarGridSpec(
            num_scalar_prefetch=1, grid=(h // tr,),
            in_specs=[pl.BlockSpec((tr, cc), lambda i, chip_ref: (i, 0)),
                      pl.BlockSpec((3, tr, cc), lambda i, chip_ref: (0, i, 0))],
            out_specs=pl.BlockSpec((tr, cc), lambda i, chip_ref: (i, 0))),
        compiler_params=_params(("arbitrary",)),
    )(_chip_index_scalar(), part, rx)


def _share_halves(halves, name):
    n_arr = len(halves)

    def body(*refs):
        ins, outs = refs[:n_arr], refs[n_arr:2 * n_arr]
        send_sems, recv_sems = refs[2 * n_arr:]
        x, y, c = _my_pos()
        cps = []
        for a in range(n_arr):
            cp = pltpu.make_async_remote_copy(
                src_ref=ins[a], dst_ref=outs[a], send_sem=send_sems.at[a], recv_sem=recv_sems.at[a],
                device_id=(x, y, 1 - c), device_id_type=MESH)
            cp.start()
            cps.append(cp)
        for cp in cps:
            cp.wait()

    return pl.pallas_call(
        body, name=name, out_shape=[jax.ShapeDtypeStruct(h.shape, h.dtype) for h in halves],
        in_specs=[ANY] * n_arr, out_specs=[ANY] * n_arr,
        scratch_shapes=[pltpu.SemaphoreType.DMA((n_arr,)), pltpu.SemaphoreType.DMA((n_arr,))],
    )(*halves)


def _bucket_table():
    qi = jnp.arange(BLOCK)[:, None]
    si = jnp.arange(2 * BLOCK)[None, :]
    dist = qi + BLOCK - si
    max_exact = N_BUCKETS // 2
    n = jnp.maximum(dist, 0)
    nf = jnp.maximum(n, max_exact).astype(F32)
    large = max_exact + (jnp.log(nf / max_exact) / math.log(MAX_DISTANCE / max_exact)
                         * (N_BUCKETS - max_exact)).astype(jnp.int32)
    large = jnp.minimum(large, N_BUCKETS - 1)
    return jnp.where(n < max_exact, n, large).astype(F32)


def _prep_tables(bucket, rel_bias, w_s):
    def body(bucket_ref, rb_ref, ws_ref, bias_ref, wsm_ref):
        qi = lax.broadcasted_iota(jnp.int32, (BLOCK, 2 * BLOCK), 0)
        si = lax.broadcasted_iota(jnp.int32, (BLOCK, 2 * BLOCK), 1)
        dist = qi + BLOCK - si
        in_window = (dist >= 0) & (dist < BLOCK)
        bk = bucket_ref[...]
        for h in range(N_HEADS):
            acc = jnp.zeros((BLOCK, 2 * BLOCK), F32)
            for b in range(N_BUCKETS):
                acc = jnp.where(bk == float(b), rb_ref[b, h], acc)
            bias_ref[h] = jnp.where(in_window, acc, NEG_INF)
        ti = lax.broadcasted_iota(jnp.int32, (BLOCK, BLOCK), 0)
        ui = lax.broadcasted_iota(jnp.int32, (BLOCK, BLOCK), 1)
        for g in range(N_GROUPS):
            wsm_ref[g] = jnp.where(ti >= ui, ws_ref[g], 0.0).astype(BF16)

    return pl.pallas_call(
        body, name="prep_tables",
        out_shape=(jax.ShapeDtypeStruct((N_HEADS, BLOCK, 2 * BLOCK), F32),
                   jax.ShapeDtypeStruct((N_GROUPS, BLOCK, BLOCK), BF16)),
        grid=(1,),
        in_specs=[_const_spec((BLOCK, 2 * BLOCK)), pl.BlockSpec(memory_space=pltpu.SMEM),
                  _const_spec((N_GROUPS, BLOCK, BLOCK))],
        out_specs=(_const_spec((N_HEADS, BLOCK, 2 * BLOCK)), _const_spec((N_GROUPS, BLOCK, BLOCK))),
        compiler_params=_params(("arbitrary",)),
    )(bucket, rel_bias, w_s)


def _fwd_in(x, modr, w_in, b_in, tm, shards, kinds):
    s = x.shape[0]
    n_steps = s // tm
    fwd_step, diag_step = (8 * n_steps) // 16, (13 * n_steps) // 16
    n_w = len(shards)

    def body(x_ref, mod_ref, w_ref, b_ref, *rest):
        shard_refs = rest[:n_w]
        h1_ref, q_ref, kv_ref, gu_ref, gv_ref, xb_ref = rest[n_w:n_w + 6]
        gathered_refs = rest[n_w + 6:2 * n_w + 6]
        send_sems, recv_sems = rest[2 * n_w + 6:]
        i = pl.program_id(0)
        gather = _WeightGather(shard_refs, gathered_refs, kinds, send_sems, recv_sems)

        @pl.when(i == 0)
        def _():
            gather.start()

        xv = x_ref[...]
        xb_ref[...] = xv.astype(BF16)
        h1 = (xv * (1.0 + mod_ref[1:2, :]) + mod_ref[0:1, :]).astype(BF16)
        h1_ref[...] = h1
        proj = _dot(h1, w_ref[...]) + b_ref[...]
        q_ref[...] = (proj[:, :ATTN_W] * Q_SCALE).astype(BF16)
        kv_ref[...] = proj[:, ATTN_W:ATTN_W + 2 * KV_W].astype(BF16)
        gu_ref[...] = proj[:, ATTN_W + 2 * KV_W:ATTN_W + 2 * KV_W + GMLP_W]
        gv_ref[...] = proj[:, ATTN_W + 2 * KV_W + GMLP_W:]

        @pl.when(i == fwd_step)
        def _():
            gather.forward()

        @pl.when(i == diag_step)
        def _():
            gather.forward_diagonal()

        @pl.when(i == n_steps - 1)
        def _():
            gather.finish()

    row = lambda w: pl.BlockSpec((tm, w), lambda i: (i, 0))
    outs = pl.pallas_call(
        body, name="fwd_in",
        out_shape=[jax.ShapeDtypeStruct((s, D_MODEL), BF16), jax.ShapeDtypeStruct((s, ATTN_W), BF16),
                   jax.ShapeDtypeStruct((s, 2 * KV_W), BF16), jax.ShapeDtypeStruct((s, GMLP_W), F32),
                   jax.ShapeDtypeStruct((s, GMLP_W), F32), jax.ShapeDtypeStruct((s, D_MODEL), BF16)]
        + [jax.ShapeDtypeStruct(_gathered_shape(sh, k), BF16) for sh, k in zip(shards, kinds)],
        grid=(n_steps,),
        in_specs=[row(D_MODEL), _const_spec((8, D_MODEL)), _const_spec((D_MODEL, IN_W)), _const_spec((1, IN_W))]
        + [ANY] * n_w,
        out_specs=[row(D_MODEL), row(ATTN_W), row(2 * KV_W), row(GMLP_W), row(GMLP_W), row(D_MODEL)] + [ANY] * n_w,
        scratch_shapes=_WeightGather.sems(n_w),
        compiler_params=_params(("arbitrary",)),
    )(x, modr, w_in, b_in, *shards)
    return outs[:6], outs[6:]


def _kv_variants(kk):
    kf = kk.astype(F32)
    lane = lax.broadcasted_iota(jnp.int32, kf.shape, 1)
    low = lane < HEAD_DIM
    k0_lo = jnp.where(low, kf, 0.0)
    k1_hi = jnp.where(low, 0.0, kf)
    k0_hi = pltpu.roll(k0_lo, HEAD_DIM, 1)
    k1_lo = pltpu.roll(k1_hi, HEAD_DIM, 1)
    return ((k0_lo.astype(BF16), k0_hi.astype(BF16)), (k1_lo.astype(BF16), k1_hi.astype(BF16)))


def _head_kv(h):
    return h // (N_HEADS // N_KV), h % 2


MIX_GROUP = 2


def _interleave(*gens):
    results = [None] * len(gens)
    active = list(enumerate(gens))
    while active:
        still = []
        for i, g in active:
            try:
                next(g)
                still.append((i, g))
            except StopIteration as done:
                results[i] = done.value
        active = still
    return results


def _attn_block_fwd(q_blk, kk, vv, bias_ref, sinks_ref, first_mask):
    kvar = _kv_variants(kk)
    vvar = _kv_variants(vv)
    heads = range(N_HEADS)
    q_pairs = [q_blk[:, (h // 2) * LANES:(h // 2 + 1) * LANES] for h in heads]
    logits = [_dot_nt(q_pairs[h], kvar[_head_kv(h)[0]][_head_kv(h)[1]]) + bias_ref[h] for h in heads]
    if first_mask is not None:
        logits = [jnp.where(first_mask, NEG_INF, lg) for lg in logits]
    yield
    ms = [jnp.maximum(jnp.max(logits[h], axis=-1, keepdims=True), sinks_ref[h]) for h in heads]
    yield
    es = [jnp.exp(logits[h] - ms[h]) for h in heads]
    ess = [jnp.exp(sinks_ref[h] - ms[h]) for h in heads]
    yield
    invs = [1.0 / (jnp.sum(es[h], axis=-1, keepdims=True) + ess[h]) for h in heads]
    probs = [(es[h] * invs[h], ess[h] * invs[h]) for h in heads]
    yield
    outs = [_dot(probs[h][0].astype(BF16), vvar[_head_kv(h)[0]][_head_kv(h)[1]]) for h in heads]
    pairs = [outs[2 * i] + outs[2 * i + 1] for i in range(N_HEADS // 2)]
    return jnp.concatenate(pairs, axis=1), probs, kvar, vvar


def _gmlp_chunk_fwd(gu, gv, ln_g, ln_b, wsm_ref, bsx, amat):
    u, tu = _gelu(gu)
    a, ta = _gelu(gv)
    yield
    mean = _split_dot(a, amat)
    d = a - mean
    yield
    var = _split_dot(d * d, amat)
    yield
    rstd = lax.rsqrt(var + LN_EPS)
    xhat = d * rstd
    vb = (xhat * ln_g + ln_b).astype(BF16)
    yield
    lane = lax.broadcasted_iota(jnp.int32, (BLOCK, LANES), 1)
    low = lane < GROUP_DIM
    cols = []
    for pair in range(N_GROUPS // 2):
        vp = vb[:, pair * LANES:(pair + 1) * LANES]
        cols.append(jnp.where(low, _dot(wsm_ref[2 * pair], vp), _dot(wsm_ref[2 * pair + 1], vp)))
    mixedv = jnp.concatenate(cols, axis=1) + bsx
    return u * mixedv, (u, tu, ta, xhat, rstd, vb, mixedv)


def _rms(a, g):
    r = lax.rsqrt(jnp.mean(a * a, axis=-1, keepdims=True) + LN_EPS)
    return a * r * g, r


def _fwd_mix(q, kv, gu, gv, x, modr, bias, sinks, gln_g, gln_b, wsm, bsx, amat, aog, gog, w_out, ln1_g, ln1_b, tm,
             ffn_shards, ffn_kinds):
    s = x.shape[0]
    nb = tm // BLOCK
    n_steps = s // tm
    fwd_step, diag_step = (7 * n_steps) // 16, (12 * n_steps) // 16
    n_w = len(ffn_shards)

    def body(q_ref, kv_ref, kvp_ref, gu_ref, gv_ref, x_ref, mod_ref, bias_ref, sinks_ref, glng_ref, glnb_ref, wsm_ref,
             bsx_ref, amat_ref, aog_ref, gog_ref, wout_ref, ln1g_ref, ln1b_ref, *rest):
        shard_refs = rest[:n_w]
        x1_ref, x1b_ref, y_ref, mixed_ref = rest[n_w:n_w + 4]
        gathered_refs = rest[n_w + 4:2 * n_w + 4]
        mix_scr, send_sems, recv_sems = rest[2 * n_w + 4:]
        i = pl.program_id(0)
        gather = _WeightGather(shard_refs, gathered_refs, ffn_kinds, send_sems, recv_sems)

        @pl.when(i == 0)
        def _():
            gather.start()

        col = lax.broadcasted_iota(jnp.int32, (BLOCK, 2 * BLOCK), 1)
        for b0 in range(0, nb, MIX_GROUP):
            gens = []
            for b in range(b0, min(b0 + MIX_GROUP, nb)):
                r0 = b * BLOCK
                if b == 0:
                    kvprev = kvp_ref[...]
                    first_mask = (col < BLOCK) & (i == 0)
                else:
                    kvprev = kv_ref[r0 - BLOCK:r0, :]
                    first_mask = None
                kvcur = kv_ref[r0:r0 + BLOCK, :]
                kk = jnp.concatenate([kvprev[:, :KV_W], kvcur[:, :KV_W]], axis=0)
                vv = jnp.concatenate([kvprev[:, KV_W:], kvcur[:, KV_W:]], axis=0)
                gens.append(_attn_block_fwd(q_ref[r0:r0 + BLOCK, :], kk, vv, bias_ref, sinks_ref, first_mask))
                gens.append(_gmlp_chunk_fwd(gu_ref[r0:r0 + BLOCK, :], gv_ref[r0:r0 + BLOCK, :], glng_ref[...],
                                            glnb_ref[...], wsm_ref, bsx_ref[...], amat_ref[...]))
            res = _interleave(*gens)
            for k, b in enumerate(range(b0, min(b0 + MIX_GROUP, nb))):
                r0 = b * BLOCK
                na, _ = _rms(res[2 * k][0], aog_ref[...])
                ng, _ = _rms(res[2 * k + 1][0], gog_ref[...])
                mix_scr[r0:r0 + BLOCK, :ATTN_W] = na.astype(BF16)
                mix_scr[r0:r0 + BLOCK, ATTN_W:] = ng.astype(BF16)
        mixed = mix_scr[...]
        mixed_ref[...] = mixed
        y = _dot(mixed, wout_ref[...])
        y_ref[...] = y.astype(BF16)
        z1 = ALPHA * x_ref[...] + mod_ref[2:3, :] * y
        xhat, _ = _ln_stats(z1)
        x1 = xhat * ln1g_ref[...] + ln1b_ref[...]
        x1_ref[...] = x1
        x1b_ref[...] = x1.astype(BF16)

        @pl.when(i == fwd_step)
        def _():
            gather.forward()

        @pl.when(i == diag_step)
        def _():
            gather.forward_diagonal()

        @pl.when(i == n_steps - 1)
        def _():
            gather.finish()

    row = lambda w: pl.BlockSpec((tm, w), lambda i: (i, 0))
    prev = pl.BlockSpec((BLOCK, 2 * KV_W), lambda i: (jnp.maximum(i * nb - 1, 0), 0))
    outs = pl.pallas_call(
        body, name="fwd_mix",
        out_shape=[jax.ShapeDtypeStruct((s, D_MODEL), F32)] + [jax.ShapeDtypeStruct((s, D_MODEL), BF16)] * 3
        + [jax.ShapeDtypeStruct(_gathered_shape(sh, k), BF16) for sh, k in zip(ffn_shards, ffn_kinds)],
        grid=(n_steps,),
        in_specs=[row(ATTN_W), row(2 * KV_W), prev, row(GMLP_W), row(GMLP_W), row(D_MODEL), _const_spec((8, D_MODEL)),
                  _const_spec((N_HEADS, BLOCK, 2 * BLOCK)), pl.BlockSpec(memory_space=pltpu.SMEM),
                  _const_spec((1, GMLP_W)), _const_spec((1, GMLP_W)), _const_spec((N_GROUPS, BLOCK, BLOCK)),
                  _const_spec((BLOCK, GMLP_W)), _const_spec((GMLP_W, GMLP_W)), _const_spec((1, ATTN_W)),
                  _const_spec((1, GMLP_W)), _const_spec((D_MODEL, D_MODEL)), _const_spec((1, D_MODEL)),
                  _const_spec((1, D_MODEL))] + [ANY] * n_w,
        out_specs=[row(D_MODEL)] * 4 + [ANY] * n_w,
        scratch_shapes=[pltpu.VMEM((tm, D_MODEL), BF16)] + _WeightGather.sems(n_w),
        compiler_params=_params(("arbitrary",)),
    )(q, kv, kv, gu, gv, x, modr, bias, sinks, gln_g, gln_b, wsm, bsx, amat, aog, gog, w_out, ln1_g, ln1_b, *ffn_shards)
    return outs[:4], outs[4:]


FF_BLOCKS = N_CHIPS // 2
FF_CHUNK = D_FF // FF_BLOCKS
FFN_SUB = 256


def _sigmoid(x):
    return 1.0 / (1.0 + jnp.exp(-x))


def _fwd_ffn(x1, target, modr, ln2_g, ln2_b, w_gu, w_dn, tm):
    s = x1.shape[0]

    def body(x1_ref, t_ref, mod_ref, g_ref, b_ref, wgu_ref, wdn_ref, h2_ref, act_ref, dy2_ref, dx1a_ref, acc_ref):
        @pl.when(pl.program_id(0) == 0)
        def _():
            acc_ref[...] = jnp.zeros_like(acc_ref)

        for k in range(tm // sub):
            r = slice(k * sub, (k + 1) * sub)
            x1v = x1_ref[r, :]
            h2 = (x1v * (1.0 + mod_ref[4:5, :]) + mod_ref[3:4, :]).astype(BF16)
            h2_ref[r, :] = h2
            y2 = None
            for cc in range(FF_BLOCKS):
                c0 = cc * FF_CHUNK
                gate = _dot(h2, wgu_ref[cc])
                up = _dot(h2, wgu_ref[FF_BLOCKS + cc])
                act_ref[r, c0:c0 + FF_CHUNK] = gate.astype(BF16)
                act_ref[r, D_FF + c0:D_FF + c0 + FF_CHUNK] = up.astype(BF16)
                a = (gate * _sigmoid(gate) * up).astype(BF16)
                part = _dot(a, wdn_ref[c0:c0 + FF_CHUNK, :])
                y2 = part if y2 is None else y2 + part
            g2 = mod_ref[5:6, :]
            z2 = ALPHA * x1v + g2 * y2
            xhat, rstd = _ln_stats(z2)
            gain = g_ref[...]
            diff = xhat * gain + b_ref[...] - t_ref[r, :]
            dx2 = diff * (1.0 / D_MODEL)
            dz2 = _ln_bwd(dx2 * gain, xhat, rstd)
            dx1a_ref[r, :] = ALPHA * dz2
            dy2_ref[r, :] = (g2 * dz2).astype(BF16)
            acc_ref[0:1, :] += _colsum(diff * diff)
            acc_ref[1:2, :] += _colsum(dx2 * xhat)
            acc_ref[2:3, :] += _colsum(dx2)
            acc_ref[3:4, :] += _colsum(dz2 * y2)

    sub = min(FFN_SUB, tm)
    row = lambda w: pl.BlockSpec((tm, w), lambda i: (i, 0))
    return pl.pallas_call(
        body, name="fwd_ffn",
        out_shape=(jax.ShapeDtypeStruct((s, D_MODEL), BF16), jax.ShapeDtypeStruct((s, 2 * D_FF), BF16),
                   jax.ShapeDtypeStruct((s, D_MODEL), BF16), jax.ShapeDtypeStruct((s, D_MODEL), F32),
                   jax.ShapeDtypeStruct((8, D_MODEL), F32)),
        grid=(s // tm,),
        in_specs=[row(D_MODEL), row(D_MODEL), _const_spec((8, D_MODEL)), _const_spec((1, D_MODEL)),
                  _const_spec((1, D_MODEL)), _const_spec((N_CHIPS, D_MODEL, FF_CHUNK), single=True),
                  _const_spec((D_FF, D_MODEL), single=True)],
        out_specs=(row(D_MODEL), row(2 * D_FF), row(D_MODEL), row(D_MODEL), _const_spec((8, D_MODEL))),
        compiler_params=_params(("arbitrary",)),
    )(x1, target, modr, ln2_g, ln2_b, w_gu, w_dn)


def _bwd_ffn(dy2, act, w_gu, w_dn, tm):
    s = dy2.shape[0]

    def body(dy2_ref, act_ref, wgu_ref, wdn_ref, a_ref, dgu_ref, dh2_ref):
        dy2v = dy2_ref[...]
        dh2 = None
        for cc in range(FF_BLOCKS):
            c0 = cc * FF_CHUNK
            da = _dot_nt(dy2v, wdn_ref[c0:c0 + FF_CHUNK, :])
            gate = act_ref[:, c0:c0 + FF_CHUNK].astype(F32)
            up = act_ref[:, D_FF + c0:D_FF + c0 + FF_CHUNK].astype(F32)
            sg = _sigmoid(gate)
            sl = gate * sg
            a_ref[:, c0:c0 + FF_CHUNK] = (sl * up).astype(BF16)
            dgate = (da * up * (sg * (1.0 + gate * (1.0 - sg)))).astype(BF16)
            dup = (da * sl).astype(BF16)
            dgu_ref[:, c0:c0 + FF_CHUNK] = dgate
            dgu_ref[:, D_FF + c0:D_FF + c0 + FF_CHUNK] = dup
            part = _dot_nt(dgate, wgu_ref[cc]) + _dot_nt(dup, wgu_ref[FF_BLOCKS + cc])
            dh2 = part if dh2 is None else dh2 + part
        dh2_ref[...] = dh2.astype(BF16)

    row = lambda w: pl.BlockSpec((tm, w), lambda i: (i, 0))
    return pl.pallas_call(
        body, name="bwd_ffn",
        out_shape=(jax.ShapeDtypeStruct((s, D_FF), BF16), jax.ShapeDtypeStruct((s, 2 * D_FF), BF16),
                   jax.ShapeDtypeStruct((s, D_MODEL), BF16)),
        grid=(s // tm,),
        in_specs=[row(D_MODEL), row(2 * D_FF), _const_spec((N_CHIPS, D_MODEL, FF_CHUNK), single=True),
                  _const_spec((D_FF, D_MODEL), single=True)],
        out_specs=(row(D_FF), row(2 * D_FF), row(D_MODEL)),
        compiler_params=_params(("parallel",)),
    )(dy2, act, w_gu, w_dn)


def _bwd_mid(dh2, dx1a, x1, x, y, modr, ln1_g, w_out, tm, swap_fulls, swap_kinds):
    s = x.shape[0]
    n_steps = s // tm
    n_g = len(swap_fulls)

    def body(dh2_ref, dx1a_ref, x1_ref, x_ref, y_ref, mod_ref, g_ref, wout_ref, *rest):
        full_refs = rest[:n_g]
        dxa_ref, dy_ref, dmix_ref, acc_ref = rest[n_g:n_g + 4]
        got_refs = rest[n_g + 4:2 * n_g + 4]
        swap = _HalfSwap(full_refs, got_refs, swap_kinds, *rest[2 * n_g + 4:])
        i = pl.program_id(0)

        @pl.when(i == 0)
        def _():
            swap.start()
            acc_ref[...] = jnp.zeros_like(acc_ref)

        dh2 = dh2_ref[...].astype(F32)
        x1v = x1_ref[...].astype(F32)
        yv = y_ref[...].astype(F32)
        g1 = mod_ref[2:3, :]
        dx1 = dx1a_ref[...] + dh2 * (1.0 + mod_ref[4:5, :])
        z1 = ALPHA * x_ref[...] + g1 * yv
        xhat, rstd = _ln_stats(z1)
        dz1 = _ln_bwd(dx1 * g_ref[...], xhat, rstd)
        dxa_ref[...] = (ALPHA * dz1).astype(BF16)
        dy = (g1 * dz1).astype(BF16)
        dy_ref[...] = dy
        dmix_ref[...] = _dot_nt(dy, wout_ref[...]).astype(BF16)
        acc_ref[0:1, :] += _colsum(dh2 * x1v)
        acc_ref[1:2, :] += _colsum(dh2)
        acc_ref[2:3, :] += _colsum(dx1 * xhat)
        acc_ref[3:4, :] += _colsum(dx1)
        acc_ref[4:5, :] += _colsum(dz1 * yv)

        @pl.when(i == n_steps - 1)
        def _():
            swap.wait()

    row = lambda w: pl.BlockSpec((tm, w), lambda i: (i, 0))
    outs = pl.pallas_call(
        body, name="bwd_mid",
        out_shape=[jax.ShapeDtypeStruct((s, D_MODEL), BF16), jax.ShapeDtypeStruct((s, D_MODEL), BF16),
                   jax.ShapeDtypeStruct((s, D_MODEL), BF16), jax.ShapeDtypeStruct((8, D_MODEL), F32)]
        + _HalfSwap.out_shapes(swap_fulls, swap_kinds),
        grid=(n_steps,),
        in_specs=[row(D_MODEL)] * 5 + [_const_spec((8, D_MODEL)), _const_spec((1, D_MODEL)),
                                       _const_spec((D_MODEL, D_MODEL))] + [ANY] * n_g,
        out_specs=[row(D_MODEL), row(D_MODEL), row(D_MODEL), _const_spec((8, D_MODEL))] + [ANY] * n_g,
        scratch_shapes=_HalfSwap.sems(n_g),
        compiler_params=_params(("arbitrary",)),
    )(dh2, dx1a, x1, x, y, modr, ln1_g, w_out, *swap_fulls)
    return outs[:4], outs[4:]


def _fold_kv(t0, t1):
    lane = lax.broadcasted_iota(jnp.int32, t0.shape, 1)
    f0 = t0 + pltpu.roll(t0, HEAD_DIM, 1)
    f1 = t1 + pltpu.roll(t1, HEAD_DIM, 1)
    return jnp.where(lane < HEAD_DIM, f0, f1)


def _bwd_mix(q, kv, gu, gv, dmix, bias, sinks, gln_g, gln_b, wsm, bsx, amat, aog, gog, grad_parts, grad_kinds):
    s = q.shape[0]
    tile = 2 * BLOCK
    n_steps = s // tile
    n_g = len(grad_parts)

    def body(q_ref, kv_ref, kvp_ref, gu_ref, gv_ref, dmix_ref, bias_ref, sinks_ref, glng_ref, glnb_ref, wsm_ref,
             bsx_ref, amat_ref, aog_ref, gog_ref, *rest):
        part_refs = rest[:n_g]
        dq_ref, dkv_ref, dgu_ref, dgv_ref, gbias_ref, dws_ref, dbs_ref, vec_ref, dsink_ref = rest[n_g:n_g + 9]
        rx_refs = rest[n_g + 9:2 * n_g + 9]
        carry, done, send_sems, recv_sems = rest[2 * n_g + 9:]
        n = pl.program_id(0)
        exchange = _ChipExchange(part_refs, rx_refs, grad_kinds, send_sems, recv_sems)

        @pl.when(n == 0)
        def _():
            exchange.start()
            carry[...] = jnp.zeros_like(carry)
            done[...] = jnp.zeros_like(done)
            gbias_ref[...] = jnp.zeros_like(gbias_ref)
            dws_ref[...] = jnp.zeros_like(dws_ref)
            dbs_ref[...] = jnp.zeros_like(dbs_ref)
            vec_ref[...] = jnp.zeros_like(vec_ref)
            dsink_ref[...] = jnp.zeros_like(dsink_ref)

        @pl.when(n == n_steps)
        def _():
            dkv_ref[:BLOCK, :] = done[...].astype(BF16)
            dkv_ref[BLOCK:, :] = carry[...].astype(BF16)
            exchange.wait()

        @pl.when(n < n_steps)
        def _():
            col = lax.broadcasted_iota(jnp.int32, (BLOCK, 2 * BLOCK), 1)
            lane = lax.broadcasted_iota(jnp.int32, (BLOCK, LANES), 1)
            low = lane < HEAD_DIM
            rows = [slice(0, BLOCK), slice(BLOCK, tile)]
            kv_blocks = [kvp_ref[...], kv_ref[rows[0], :], kv_ref[rows[1], :]]
            masks = [(col < BLOCK) & (n == 0), None]
            q_blks = [q_ref[r, :] for r in rows]
            fwd = []
            for b in range(2):
                kk = jnp.concatenate([kv_blocks[b][:, :KV_W], kv_blocks[b + 1][:, :KV_W]], axis=0)
                vv = jnp.concatenate([kv_blocks[b][:, KV_W:], kv_blocks[b + 1][:, KV_W:]], axis=0)
                fwd.append(_attn_block_fwd(q_blks[b], kk, vv, bias_ref, sinks_ref, masks[b]))
                fwd.append(_gmlp_chunk_fwd(gu_ref[rows[b], :], gv_ref[rows[b], :], glng_ref[...], glnb_ref[...],
                                           wsm_ref, bsx_ref[...], amat_ref[...]))
            res = _interleave(*fwd[:2]) + _interleave(*fwd[2:])

            def gating_bwd(b, d_gm, saved):
                u, tu, ta, xhat, rstd, vb, mixedv = saved
                dgu_ref[rows[b], :] = (d_gm * mixedv * _gelu_grad(gu_ref[rows[b], :], tu)).astype(BF16)
                dmx = d_gm * u
                dmxb = dmx.astype(BF16)
                yield
                dvn_cols, dws = [], []
                for pair in range(N_GROUPS // 2):
                    dp_ = dmxb[:, pair * LANES:(pair + 1) * LANES]
                    vp = vb[:, pair * LANES:(pair + 1) * LANES]
                    dvn_cols.append(
                        jnp.where(low, _dot_tn(wsm_ref[2 * pair], dp_), _dot_tn(wsm_ref[2 * pair + 1], dp_)))
                    zero = jnp.zeros_like(dp_)
                    dws.append(_dot_nt(jnp.where(low, dp_, zero), vp))
                    dws.append(_dot_nt(jnp.where(low, zero, dp_), vp))
                dvn = jnp.concatenate(dvn_cols, axis=1)
                yield
                dxh = dvn * glng_ref[...]
                am = amat_ref[...]
                m1 = _split_dot(dxh, am)
                m2 = _split_dot(dxh * xhat, am)
                yield
                da = rstd * (dxh - m1 - xhat * m2)
                dgv_ref[rows[b], :] = (da * _gelu_grad(gv_ref[rows[b], :], ta)).astype(BF16)
                return dmx, dws, _colsum(dvn * xhat), _colsum(dvn)

            def attention_bwd(b, d_attn, probs, kvar, vvar):
                heads = range(N_HEADS)
                sels = [low if h % 2 == 0 else jnp.logical_not(low) for h in heads]
                pair_of = lambda a, h: a[:, (h // 2) * LANES:(h // 2 + 1) * LANES]
                do_hs = [jnp.where(sels[h], pair_of(d_attn, h), 0.0).astype(BF16) for h in heads]
                q_hs = [jnp.where(sels[h], pair_of(q_blks[b], h), jnp.zeros((BLOCK, LANES), BF16)) for h in heads]
                dps = [_dot_nt(do_hs[h], vvar[_head_kv(h)[0]][_head_kv(h)[1]]) for h in heads]
                yield
                deltas = [jnp.sum(probs[h][0] * dps[h], axis=-1, keepdims=True) for h in heads]
                yield
                dss = [probs[h][0] * (dps[h] - deltas[h]) for h in heads]
                dsinks = [-(probs[h][1] * deltas[h]) for h in heads]
                dsbs = [ds.astype(BF16) for ds in dss]
                pbs = [probs[h][0].astype(BF16) for h in heads]
                yield
                dqs = [_dot(dsbs[h], kvar[_head_kv(h)[0]][_head_kv(h)[1]]) for h in heads]
                tks = [_dot_tn(dsbs[h], q_hs[h]) for h in heads]
                tvs = [_dot_tn(pbs[h], do_hs[h]) for h in heads]
                dq_cols = [dqs[2 * i] + dqs[2 * i + 1] for i in range(N_HEADS // 2)]
                dq_ref[rows[b], :] = (jnp.concatenate(dq_cols, axis=1) * Q_SCALE).astype(BF16)
                per_kv = N_HEADS // N_KV
                kv_sum = lambda ts, kvh: sum(ts[kvh * per_kv + 1:(kvh + 1) * per_kv], ts[kvh * per_kv])
                dkk = _fold_kv(kv_sum(tks, 0), kv_sum(tks, 1))
                dvv = _fold_kv(kv_sum(tvs, 0), kv_sum(tvs, 1))
                return jnp.concatenate([dkk, dvv], axis=1), dss, dsinks

            bwd, rms_g = [], []
            for b in range(2):
                attn, probs, kvar, vvar = res[2 * b]
                gm, saved = res[2 * b + 1]
                na_unit, r_a = _rms(attn, 1.0)
                ng_unit, r_g = _rms(gm, 1.0)
                dmix = dmix_ref[rows[b], :].astype(F32)
                dn_a = dmix[:, :ATTN_W]
                dn_g = dmix[:, ATTN_W:]
                rms_g.append((_colsum(dn_a * na_unit), _colsum(dn_g * ng_unit)))
                t_a = dn_a * aog_ref[...]
                d_attn = r_a * t_a - na_unit * (r_a * jnp.mean(t_a * na_unit, axis=-1, keepdims=True))
                t_g = dn_g * gog_ref[...]
                d_gm = r_g * t_g - ng_unit * (r_g * jnp.mean(t_g * ng_unit, axis=-1, keepdims=True))
                bwd.append(attention_bwd(b, d_attn, probs, kvar, vvar))
                bwd.append(gating_bwd(b, d_gm, saved))
            (dkv_a, dss_a, dsk_a), (dmx_a, dws_a, glg_a, glb_a) = _interleave(*bwd[:2])
            (dkv_b, dss_b, dsk_b), (dmx_b, dws_b, glg_b, glb_b) = _interleave(*bwd[2:])

            vec_ref[0:1, :] += rms_g[0][0] + rms_g[1][0]
            vec_ref[1:2, :] += rms_g[0][1] + rms_g[1][1]
            vec_ref[2:3, :] += glg_a + glg_b
            vec_ref[3:4, :] += glb_a + glb_b
            dbs_ref[...] += dmx_a + dmx_b
            for g in range(N_GROUPS):
                dws_ref[g] += dws_a[g] + dws_b[g]
            for h in range(N_HEADS):
                gbias_ref[h] += dss_a[h] + dss_b[h]
                dsink_ref[h] += dsk_a[h] + dsk_b[h]

            dkv_ref[:BLOCK, :] = done[...].astype(BF16)
            dkv_ref[BLOCK:, :] = (carry[...] + dkv_a[:BLOCK]).astype(BF16)
            done[...] = dkv_a[BLOCK:] + dkv_b[:BLOCK]
            carry[...] = dkv_b[BLOCK:]

    last = n_steps - 1
    cur = lambda w: pl.BlockSpec((tile, w), lambda n: (jnp.minimum(n, last), 0))
    late = lambda w: pl.BlockSpec((tile, w), lambda n: (jnp.clip(n - 1, 0, last), 0))
    before = pl.BlockSpec((BLOCK, 2 * KV_W), lambda n: (jnp.clip(2 * n - 1, 0, 2 * last + 1), 0))
    outs = pl.pallas_call(
        body, name="bwd_mix",
        out_shape=[jax.ShapeDtypeStruct((s, ATTN_W), BF16), jax.ShapeDtypeStruct((s, 2 * KV_W), BF16),
                   jax.ShapeDtypeStruct((s, GMLP_W), BF16), jax.ShapeDtypeStruct((s, GMLP_W), BF16),
                   jax.ShapeDtypeStruct((N_HEADS, BLOCK, 2 * BLOCK), F32),
                   jax.ShapeDtypeStruct((N_GROUPS, BLOCK, BLOCK), F32),
                   jax.ShapeDtypeStruct((BLOCK, GMLP_W), F32), jax.ShapeDtypeStruct((8, GMLP_W), F32),
                   jax.ShapeDtypeStruct((N_HEADS, BLOCK, 1), F32)]
        + [jax.ShapeDtypeStruct(_rx_shape(p.shape, k), BF16) for p, k in zip(grad_parts, grad_kinds)],
        grid=(n_steps + 1,),
        in_specs=[cur(ATTN_W), cur(2 * KV_W), before, cur(GMLP_W), cur(GMLP_W), cur(D_MODEL),
                  _const_spec((N_HEADS, BLOCK, 2 * BLOCK)), pl.BlockSpec(memory_space=pltpu.SMEM),
                  _const_spec((1, GMLP_W)), _const_spec((1, GMLP_W)), _const_spec((N_GROUPS, BLOCK, BLOCK)),
                  _const_spec((BLOCK, GMLP_W)), _const_spec((GMLP_W, GMLP_W)), _const_spec((1, ATTN_W)),
                  _const_spec((1, GMLP_W))] + [ANY] * n_g,
        out_specs=[cur(ATTN_W), late(2 * KV_W), cur(GMLP_W), cur(GMLP_W),
                   _const_spec((N_HEADS, BLOCK, 2 * BLOCK)), _const_spec((N_GROUPS, BLOCK, BLOCK)),
                   _const_spec((BLOCK, GMLP_W)), _const_spec((8, GMLP_W)), _const_spec((N_HEADS, BLOCK, 1))]
        + [ANY] * n_g,
        scratch_shapes=[pltpu.VMEM((BLOCK, 2 * KV_W), F32), pltpu.VMEM((BLOCK, 2 * KV_W), F32)]
        + _ChipExchange.sems(n_g),
        compiler_params=_params(("arbitrary",)),
    )(q, kv, kv, gu, gv, dmix, bias, sinks, gln_g, gln_b, wsm, bsx, amat, aog, gog, *grad_parts)
    return outs[:9], outs[9:]


def _mix_finalize(gbias, bucket, dws, dbs, dsink):
    def body(gb_ref, bucket_ref, dws_ref, dbs_ref, dsink_ref, tall_ref):
        bk = bucket_ref[...]
        lane = lax.broadcasted_iota(jnp.int32, (N_BUCKETS, LANES), 1)
        rowi = lax.broadcasted_iota(jnp.int32, (N_BUCKETS, LANES), 0)
        drb = jnp.zeros((N_BUCKETS, LANES), F32)
        dsk = jnp.zeros((8, LANES), F32)
        lane8 = lax.broadcasted_iota(jnp.int32, (8, LANES), 1)
        for h in range(N_HEADS):
            g = gb_ref[h]
            for b in range(N_BUCKETS):
                tot = jnp.sum(_colsum(jnp.where(bk == float(b), g, 0.0)), axis=1, keepdims=True)
                drb = jnp.where((lane == h) & (rowi == b), tot, drb)
            sk = jnp.sum(dsink_ref[h], axis=0, keepdims=True)
            dsk = jnp.where(lane8 == h, sk, dsk)
        tall_ref[TALL_RB:TALL_RB + N_BUCKETS, :] = drb
        tall_ref[TALL_SK:TALL_SK + 8, :] = dsk
        ti = lax.broadcasted_iota(jnp.int32, (BLOCK, BLOCK), 0)
        ui = lax.broadcasted_iota(jnp.int32, (BLOCK, BLOCK), 1)
        for g in range(N_GROUPS):
            tall_ref[g * BLOCK:(g + 1) * BLOCK, :] = jnp.where(ti >= ui, dws_ref[g], 0.0)
        gi = lax.broadcasted_iota(jnp.int32, (GMLP_W, LANES), 0) // GROUP_DIM
        li = lax.broadcasted_iota(jnp.int32, (GMLP_W, LANES), 1)
        ind = jnp.where(gi == li, 1.0, 0.0).astype(BF16)
        d = dbs_ref[...]
        hi = d.astype(BF16)
        r1 = d - hi.astype(F32)
        mid = r1.astype(BF16)
        lo = (r1 - mid.astype(F32)).astype(BF16)
        dbsg = _dot(hi, ind) + _dot(mid, ind) + _dot(lo, ind)
        tall_ref[TALL_BS:TALL_BS + N_GROUPS, :] = dbsg.T[:N_GROUPS, :]

    return pl.pallas_call(
        body, name="mix_finalize", out_shape=jax.ShapeDtypeStruct((TALL_ROWS, LANES), F32), grid=(1,),
        in_specs=[_const_spec((N_HEADS, BLOCK, 2 * BLOCK)), _const_spec((BLOCK, 2 * BLOCK)),
                  _const_spec((N_GROUPS, BLOCK, BLOCK)), _const_spec((BLOCK, GMLP_W)),
                  _const_spec((N_HEADS, BLOCK, 1))],
        out_specs=_const_spec((TALL_ROWS, LANES)),
        compiler_params=_params(("arbitrary",)),
    )(gbias, bucket, dws, dbs, dsink)


def _bwd_in(dq, dkv, dgu, dgv, dxa, x, modr, w_in, tm):
    s = x.shape[0]

    def body(dq_ref, dkv_ref, dgu_ref, dgv_ref, dxa_ref, x_ref, mod_ref, w_ref, gx_ref, acc_ref, db_ref):
        @pl.when(pl.program_id(0) == 0)
        def _():
            acc_ref[...] = jnp.zeros_like(acc_ref)
            db_ref[...] = jnp.zeros_like(db_ref)

        dproj = jnp.concatenate([dq_ref[...], dkv_ref[...], dgu_ref[...], dgv_ref[...]], axis=1)
        dh1 = _dot_nt(dproj, w_ref[...])
        gx_ref[...] = dxa_ref[...].astype(F32) + dh1 * (1.0 + mod_ref[1:2, :])
        acc_ref[0:1, :] += _colsum(dh1 * x_ref[...].astype(F32))
        acc_ref[1:2, :] += _colsum(dh1)
        db_ref[0:1, :] += _colsum(dproj.astype(F32))

    row = lambda w: pl.BlockSpec((tm, w), lambda i: (i, 0))
    return pl.pallas_call(
        body, name="bwd_in",
        out_shape=(jax.ShapeDtypeStruct((s, D_MODEL), F32), jax.ShapeDtypeStruct((8, D_MODEL), F32),
                   jax.ShapeDtypeStruct((8, IN_W), F32)),
        grid=(s // tm,),
        in_specs=[row(ATTN_W), row(2 * KV_W), row(GMLP_W), row(GMLP_W), row(D_MODEL), row(D_MODEL),
                  _const_spec((8, D_MODEL)), _const_spec((D_MODEL, IN_W))],
        out_specs=(row(D_MODEL), _const_spec((8, D_MODEL)), _const_spec((8, IN_W))),
        compiler_params=_params(("arbitrary",)),
    )(dq, dkv, dgu, dgv, dxa, x, modr, w_in)


def _wgrad(a, bs, tm, tk, name, owner_blocks=False, gather_vs=()):
    k_all, m = a.shape
    n = sum(b.shape[1] for b in bs)
    nk = k_all // tk
    nm = m // tm
    n_b = len(bs)
    n_v = len(gather_vs)
    wb = n // N_CHIPS

    def body(a_ref, *rest):
        b_refs, v_refs = rest[:n_b], rest[n_b:n_b + n_v]
        o_ref, ob_ref = rest[n_b + n_v:n_b + n_v + 2]
        vg_refs = rest[n_b + n_v + 2:n_b + 2 * n_v + 2]
        i, k = pl.program_id(0), pl.program_id(1)
        if n_v:
            gather = _Gather8(v_refs, vg_refs, *rest[n_b + 2 * n_v + 2:])

            @pl.when((i == 0) & (k == 0))
            def _():
                gather.start()

            @pl.when((i == nm - 1) & (k == 0))
            def _():
                gather.forward()

        @pl.when(k == 0)
        def _():
            o_ref[...] = jnp.zeros_like(o_ref)

        b = b_refs[0][...] if n_b == 1 else jnp.concatenate([r[...] for r in b_refs], axis=1)
        if owner_blocks:
            av = a_ref[...]
            for j in range(N_CHIPS):
                o_ref[j] += _dot_tn(av, b[:, j * wb:(j + 1) * wb])
        else:
            o_ref[...] += _dot_tn(a_ref[...], b)

        @pl.when(k == nk - 1)
        def _():
            ob_ref[...] = o_ref[...].astype(BF16)

        if n_v:
            @pl.when((i == nm - 1) & (k == nk - 1))
            def _():
                gather.finish()

    if owner_blocks:
        out_spec = pl.BlockSpec((N_CHIPS, tm, wb), lambda i, k: (0, i, 0))
        shape = (N_CHIPS, m, wb)
    else:
        out_spec = pl.BlockSpec((tm, n), lambda i, k: (i, 0))
        shape = (m, n)
    outs = pl.pallas_call(
        body, name=name,
        out_shape=[jax.ShapeDtypeStruct(shape, F32), jax.ShapeDtypeStruct(shape, BF16)] + _gathered8_shapes(gather_vs),
        grid=(nm, nk),
        in_specs=[pl.BlockSpec((tk, tm), lambda i, k: (k, i))]
        + [pl.BlockSpec((tk, b.shape[1]), lambda i, k: (k, 0)) for b in bs] + [ANY] * n_v,
        out_specs=[out_spec, out_spec] + [ANY] * n_v,
        scratch_shapes=_Gather8.sems(n_v) if n_v else [],
        compiler_params=_params(("arbitrary", "arbitrary") if n_v else ("parallel", "arbitrary")),
    )(a, *bs, *gather_vs)
    return outs[0], outs[1], outs[2:]


def _adam_math(w, g, m, v):
    m2 = ADAM_B1 * m + (1.0 - ADAM_B1) * g
    v2 = ADAM_B2 * v + (1.0 - ADAM_B2) * (g * g)
    m_hat = m2 / (1.0 - ADAM_B1 ** ADAM_STEP)
    v_hat = v2 / (1.0 - ADAM_B2 ** ADAM_STEP)
    delta = -ADAM_LR * (m_hat / (jnp.sqrt(v_hat) + ADAM_EPS) + ADAM_WD * w)
    return delta, m2, v2


def _adam_halves(w, mine, got, m, v, tr, name):
    r, cc = w.shape
    h = r // 2
    nt = h // tr

    def body(c_ref, w_ref, mine_ref, got_ref, m_ref, v_ref, g_ref, d_ref, m2_ref, v2_ref):
        g = jnp.where(pl.program_id(0) == c_ref[0], mine_ref[...], got_ref[...])
        g_ref[...] = g
        d, m2, v2 = _adam_math(w_ref[...], g, m_ref[...], v_ref[...])
        d_ref[...] = d
        m2_ref[...] = m2
        v2_ref[...] = v2

    full = pl.BlockSpec((tr, cc), lambda hh, i, c_ref: (hh * nt + i, 0))
    half = pl.BlockSpec((tr, cc), lambda hh, i, c_ref: (i, 0))
    shp = jax.ShapeDtypeStruct((r, cc), F32)
    return pl.pallas_call(
        body, name=name, out_shape=(shp, shp, shp, shp),
        grid_spec=pltpu.PrefetchScalarGridSpec(
            num_scalar_prefetch=1, grid=(2, nt), in_specs=[full, half, half, full, full],
            out_specs=(full, full, full, full)),
        compiler_params=_params(("arbitrary", "arbitrary")),
    )(_core_index_scalar(), w, mine, got, m, v)


def _adam_w_ada(sc_t, dmod_cols, w, m, v, tr):
    r, cc = w.shape

    def body(sct_ref, dm_ref, w_ref, m_ref, v_ref, g_ref, d_ref, m2_ref, v2_ref):
        g = sct_ref[:, 0:1] * dm_ref[0:1, :]
        for k in range(1, N_DEV):
            g = g + sct_ref[:, k:k + 1] * dm_ref[k:k + 1, :]
        g_ref[...] = g
        d, m2, v2 = _adam_math(w_ref[...], g, m_ref[...], v_ref[...])
        d_ref[...] = d
        m2_ref[...] = m2
        v2_ref[...] = v2

    spec = pl.BlockSpec((tr, cc), lambda i: (i, 0))
    shp = jax.ShapeDtypeStruct((r, cc), F32)
    return pl.pallas_call(
        body, name="adam_w_ada", out_shape=(shp, shp, shp, shp), grid=(r // tr,),
        in_specs=[pl.BlockSpec((tr, N_DEV), lambda i: (i, 0)), _const_spec((N_DEV, cc)), spec, spec, spec],
        out_specs=(spec, spec, spec, spec), compiler_params=_params(("parallel",)),
    )(sc_t, dmod_cols, w, m, v)


def _pack_wide(acc_i, acc_m, acc_f, db_in, vec):
    arrs = [acc_i, acc_m, acc_f, db_in, vec]
    i_, m_, f_, b_, v_ = range(5)
    src = {"b_in": (b_, 0), "ln1_g": (m_, 2), "ln1_b": (m_, 3), "ln2_g": (f_, 1), "ln2_b": (f_, 2),
           "gmlp_ln_g": (v_, 2), "gmlp_ln_b": (v_, 3), "attn_out_g": (v_, 0), "gmlp_out_g": (v_, 1), "loss": (f_, 0)}
    dmod = [(i_, 1), (i_, 0), (m_, 4), (m_, 1), (m_, 0), (f_, 3)]

    def body(*refs):
        ins, wide_ref = refs[:5], refs[5]
        wide_ref[...] = jnp.zeros_like(wide_ref)
        for k, (a, row) in enumerate(dmod):
            wide_ref[0:1, k * D_MODEL:(k + 1) * D_MODEL] = ins[a][row:row + 1, :]
        for name, (a, row) in src.items():
            r, off, n = WIDE_LAYOUT[name]
            wide_ref[r:r + 1, off:off + n] = ins[a][row:row + 1, :]

    return pl.pallas_call(
        body, name="pack_wide", out_shape=jax.ShapeDtypeStruct((8, WIDE_W), F32), grid=(1,),
        in_specs=[_const_spec(a.shape) for a in arrs], out_specs=_const_spec((8, WIDE_W)),
        compiler_params=_params(("arbitrary",)),
    )(*arrs)


def _adam_small(gw, gt, wide_wmv, w_s, b_s, rel_bias, sinks):
    names = list(WIDE_PARAMS)
    tall = [("gmlp_w_s", w_s), ("gmlp_b_s", b_s), ("rel_bias", rel_bias), ("attn_sinks", sinks)]
    ins = [gw, gt]
    for n in names:
        ins += list(wide_wmv[n])
    for _, t in tall:
        ins += list(t)
    n_in = len(ins)

    def body(*refs):
        gw_ref, gt_ref = refs[0], refs[1]
        wmv = refs[2:n_in]
        dmod_ref, loss_ref = refs[n_in], refs[n_in + 1]
        outs = refs[n_in + 2:]

        def tall_sum(r0, nr):
            g = gt_ref[r0:r0 + nr, :]
            for d in range(1, N_DEV):
                g = g + gt_ref[d * TALL_ROWS + r0:d * TALL_ROWS + r0 + nr, :]
            return g

        def emit(k, g, w_ref, m_ref, v_ref):
            d, m2, v2 = _adam_math(w_ref[...], g, m_ref[...], v_ref[...])
            outs[4 * k][...] = g
            outs[4 * k + 1][...] = d
            outs[4 * k + 2][...] = m2
            outs[4 * k + 3][...] = v2

        gsum = gw_ref[0:8, :]
        for d in range(1, N_DEV):
            gsum = gsum + gw_ref[8 * d:8 * d + 8, :]
        for d in range(N_DEV):
            dmod_ref[d:d + 1, :] = gw_ref[8 * d:8 * d + 1, :]
        for k, n in enumerate(names):
            r, off, sz = WIDE_LAYOUT[n]
            emit(k, gsum[r:r + 1, off:off + sz], *wmv[3 * k:3 * k + 3])
        r, off, sz = WIDE_LAYOUT["loss"]
        tot = jnp.sum(gsum[r:r + 1, off:off + sz], axis=1, keepdims=True)
        loss_ref[...] = jnp.broadcast_to(tot * (0.5 / D_MODEL), loss_ref.shape)

        k0 = len(names)
        ws_refs = wmv[3 * k0:3 * k0 + 3]
        for g in range(N_GROUPS):
            rows = slice(g * BLOCK, (g + 1) * BLOCK)
            gg = tall_sum(g * BLOCK, BLOCK)
            d, m2, v2 = _adam_math(ws_refs[0][rows, :], gg, ws_refs[1][rows, :], ws_refs[2][rows, :])
            outs[4 * k0][rows, :] = gg
            outs[4 * k0 + 1][rows, :] = d
            outs[4 * k0 + 2][rows, :] = m2
            outs[4 * k0 + 3][rows, :] = v2
        emit(k0 + 1, tall_sum(TALL_BS, N_GROUPS), *wmv[3 * (k0 + 1):3 * (k0 + 1) + 3])
        emit(k0 + 2, tall_sum(TALL_RB, N_BUCKETS)[:, :N_HEADS], *wmv[3 * (k0 + 2):3 * (k0 + 2) + 3])
        emit(k0 + 3, tall_sum(TALL_SK, 8)[0:1, :N_HEADS], *wmv[3 * (k0 + 3):3 * (k0 + 3) + 3])

    out_shapes = [jax.ShapeDtypeStruct((N_DEV, WIDE_W), F32), jax.ShapeDtypeStruct((8, LANES), F32)]
    for n in names:
        out_shapes += [jax.ShapeDtypeStruct(wide_wmv[n][0].shape, F32)] * 4
    for _, t in tall:
        out_shapes += [jax.ShapeDtypeStruct(t[0].shape, F32)] * 4
    res = pl.pallas_call(
        body, name="adam_small", out_shape=out_shapes, grid=(1,),
        in_specs=[_const_spec(a.shape) for a in ins], out_specs=[_const_spec(o.shape) for o in out_shapes],
        compiler_params=_params(("arbitrary",)),
    )(*ins)
    out = {}
    for k, n in enumerate(names + [t[0] for t in tall]):
        out[n] = tuple(res[2 + 4 * k:6 + 4 * k])
    return res[0], res[1], out


def kernel(x, c, rel_bias, w_ada, b_ada, w_in, b_in, attn_sinks, gmlp_ln_g, gmlp_ln_b, gmlp_w_s, gmlp_b_s, attn_out_g, gmlp_out_g, w_out, ln1_g, ln1_b, w_gate_up, w_down, ln2_g, ln2_b, loss_target, m_rel_bias, m_w_ada, m_b_ada, m_w_in, m_b_in, m_attn_sinks, m_gmlp_ln_g, m_gmlp_ln_b, m_gmlp_w_s, m_gmlp_b_s, m_attn_out_g, m_gmlp_out_g, m_w_out, m_ln1_g, m_ln1_b, m_w_gate_up, m_w_down, m_ln2_g, m_ln2_b, v_rel_bias, v_w_ada, v_b_ada, v_w_in, v_b_in, v_attn_sinks, v_gmlp_ln_g, v_gmlp_ln_b, v_gmlp_w_s, v_gmlp_b_s, v_attn_out_g, v_gmlp_out_g, v_w_out, v_ln1_g, v_ln1_b, v_w_gate_up, v_w_down, v_ln2_g, v_ln2_b):
    ix, iy, ic = _my_pos()
    chip = 2 * ix + iy
    dev = 4 * ix + 2 * iy + ic
    s = x.shape[1]
    xs = x[0]
    tgt = loss_target[0]
    tm_big = min(512, s)
    tm_ffn = min(2 * FFN_SUB, s)
    n_ada = w_ada.shape[2]

    w_in_s, w_out_s = w_in[0].astype(BF16), w_out[0].astype(BF16)
    w_gu_s, w_dn_s = w_gate_up[0].astype(BF16), w_down[0].astype(BF16)
    sc_all, mod_rows, (w_in_g, w_out_g) = _prologue(
        jnp.pad(c, ((0, 7), (0, 0))), w_ada[0], lax.dynamic_slice_in_dim(b_ada, chip * n_ada, n_ada, axis=1),
        [w_in_s, w_out_s])
    mod_all = mod_rows.reshape(N_DEV, N_DEV, -1)
    mod_row = lax.dynamic_index_in_dim(mod_all[0::2], dev, axis=1, keepdims=False)
    modr = jnp.pad(mod_row.reshape(6, D_MODEL), ((0, 2), (0, 0)))
    w_in_g = _insert_own(w_in_g, w_in_s, "blk", chip)
    w_in_f = jnp.transpose(w_in_g, (1, 0, 2)).reshape(D_MODEL, IN_W)

    bucket = _bucket_table()
    bias, wsm = _prep_tables(bucket, rel_bias, gmlp_w_s[0])
    bsx = jnp.repeat(gmlp_b_s[0].T, GROUP_DIM, axis=1)
    amat = _group_mean_matrix()
    sinks = attn_sinks[0]

    (h1, q, kv, gu, gv, xb), (w_dn_g,) = _fwd_in(xs, modr, w_in_f, b_in, tm_big, [w_dn_s], ["blk"])
    w_out_f = _insert_own(w_out_g, w_out_s, "blk", chip).reshape(D_MODEL, D_MODEL)
    (x1, x1b, y, mixed), (w_gu_g,) = _fwd_mix(
        q, kv, gu, gv, xs, modr, bias, sinks, gmlp_ln_g, gmlp_ln_b, wsm, bsx, amat, attn_out_g, gmlp_out_g, w_out_f,
        ln1_g, ln1_b, tm_big, [w_gu_s], ["blk"])
    assert w_gate_up.shape[2] == FF_CHUNK
    w_gu_f = _insert_own(w_gu_g, w_gu_s, "blk", chip)
    w_dn_f = _insert_own(w_dn_g, w_dn_s, "blk", chip).reshape(D_FF, D_MODEL)
    h2, act, dy2, dx1a, acc_f = _fwd_ffn(x1, tgt, modr, ln2_g, ln2_b, w_gu_f, w_dn_f, tm_ffn)

    a_act, dgu_ff, dh2 = _bwd_ffn(dy2, act, w_gu_f, w_dn_f, min(FFN_SUB, s))
    g_dn, g_dn_b, _ = _wgrad(a_act, [dy2], D_FF // 2, min(1024, s), "wgrad_down")
    g_gu, g_gu_b, _ = _wgrad(h2, [dgu_ff], 512, min(512, s), "wgrad_gate_up")
    blk3 = lambda a, rows: a.reshape(N_CHIPS, rows, a.shape[1])
    (dxa, dy, dmix, acc_m), (got_dn, got_gu) = _bwd_mid(
        dh2, dx1a, x1b, xs, y, modr, ln1_g, w_out_f, tm_big, [blk3(g_dn_b, D_FF // N_CHIPS), g_gu_b], ["blk", "cols"])
    g_out, g_out_b, _ = _wgrad(mixed, [dy], 512, min(2048, s), "wgrad_out")
    (got_out,) = _swap_halves([blk3(g_out_b, D_MODEL // N_CHIPS)], ["blk"], "rs_swap_out")
    kinds_a = ["blk", "cols", "blk"]
    fulls_a = [blk3(g_dn, D_FF // N_CHIPS), g_gu, blk3(g_out, D_MODEL // N_CHIPS)]
    gots_a = [got_dn, got_gu, got_out]
    parts_a = [_add_halves(f, g, k, "rs_add_a%d" % i) for i, (f, g, k) in enumerate(zip(fulls_a, gots_a, kinds_a))]
    (dq, dkv, dgu, dgv, gbias, dws, dbs, vec, dsink), rxs_a = _bwd_mix(
        q, kv, gu, gv, dmix, bias, sinks, gmlp_ln_g, gmlp_ln_b, wsm, bsx, amat, attn_out_g, gmlp_out_g,
        [p[1] for p in parts_a], kinds_a)
    tall_g = _mix_finalize(gbias, bucket, dws, dbs, dsink)
    grad_x, acc_i, db_in = _bwd_in(dq, dkv, dgu, dgv, dxa, xb, modr, w_in_f, tm_big)

    wide_g = _pack_wide(acc_i, acc_m, acc_f, db_in, vec)
    full_in, full_in_b, (gw, gt) = _wgrad(h1, [dq, dkv, dgu, dgv], 512, min(1024, s), "wgrad_in", owner_blocks=True,
                                          gather_vs=[wide_g, tall_g])
    (got_in,) = _swap_halves([full_in_b], ["blk"], "rs_swap_in")
    part_in = _add_halves(full_in, got_in, "blk", "rs_add_in")
    (rx_in,) = _exchange_chip_partials([part_in[1]], ["blk"], "rs_chips_in")
    wide_wmv = {"b_ada": (b_ada, m_b_ada, v_b_ada), "b_in": (b_in, m_b_in, v_b_in),
                "ln1_g": (ln1_g, m_ln1_g, v_ln1_g), "ln1_b": (ln1_b, m_ln1_b, v_ln1_b),
                "ln2_g": (ln2_g, m_ln2_g, v_ln2_g), "ln2_b": (ln2_b, m_ln2_b, v_ln2_b),
                "gmlp_ln_g": (gmlp_ln_g, m_gmlp_ln_g, v_gmlp_ln_g), "gmlp_ln_b": (gmlp_ln_b, m_gmlp_ln_b, v_gmlp_ln_b),
                "attn_out_g": (attn_out_g, m_attn_out_g, v_attn_out_g),
                "gmlp_out_g": (gmlp_out_g, m_gmlp_out_g, v_gmlp_out_g)}
    rows2 = lambda a: a.reshape(-1, a.shape[-1])
    dmod_all, loss_t, small = _adam_small(
        gw, gt, wide_wmv, tuple(rows2(a) for a in (gmlp_w_s, m_gmlp_w_s, v_gmlp_w_s)),
        tuple(rows2(a) for a in (gmlp_b_s, m_gmlp_b_s, v_gmlp_b_s)), (rel_bias, m_rel_bias, v_rel_bias),
        (attn_sinks, m_attn_sinks, v_attn_sinks))
    loss = loss_t[0, 0]

    dmod_cols = lax.dynamic_slice_in_dim(dmod_all, chip * n_ada, n_ada, axis=1)
    g_ada, d_ada, m_ada, v_ada = _adam_w_ada(sc_all.T, dmod_cols, w_ada[0], m_w_ada[0], v_w_ada[0], 256)

    sums = [(parts_a[0][0], rxs_a[0], 176), (parts_a[1][0], rxs_a[1], 256), (parts_a[2][0], rxs_a[2], 128),
            (part_in[0], rx_in, 256)]
    mine = [_sum_chips(p, rx, tr, "rs_sum_%d" % i) for i, (p, rx, tr) in enumerate(sums)]
    got = _share_halves(mine, "rs_share")

    gs_dn, d_dn, m_dn, v_dn = _adam_halves(w_down[0], mine[0], got[0], m_w_down[0], v_w_down[0], 176, "adam_w_down")
    gs_gu, d_gu, m_gu, v_gu = _adam_halves(w_gate_up[0], mine[1], got[1], m_w_gate_up[0], v_w_gate_up[0], 256,
                                           "adam_w_gate_up")
    gs_out, d_out, m_out, v_out = _adam_halves(w_out[0], mine[2], got[2], m_w_out[0], v_w_out[0], 128, "adam_w_out")
    gs_in, d_in, m_in, v_in = _adam_halves(w_in[0], mine[3], got[3], m_w_in[0], v_w_in[0], 256, "adam_w_in")

    big = {"w_ada": (g_ada, d_ada, m_ada, v_ada), "w_in": (gs_in, d_in, m_in, v_in), "w_out": (gs_out, d_out, m_out, v_out),
           "w_gate_up": (gs_gu, d_gu, m_gu, v_gu), "w_down": (gs_dn, d_dn, m_dn, v_dn)}
    order = ["rel_bias", "w_ada", "b_ada", "w_in", "b_in", "attn_sinks", "gmlp_ln_g", "gmlp_ln_b", "gmlp_w_s", "gmlp_b_s",
             "attn_out_g", "gmlp_out_g", "w_out", "ln1_g", "ln1_b", "w_gate_up", "w_down", "ln2_g", "ln2_b"]
    shapes = {"gmlp_w_s": gmlp_w_s.shape, "gmlp_b_s": gmlp_b_s.shape}
    outs = [loss, grad_x[None]]
    for k in range(4):
        for name in order:
            if name in big:
                outs.append(big[name][k][None])
            elif name in shapes:
                outs.append(small[name][k].reshape(shapes[name]))
            else:
                outs.append(small[name][k])
    return tuple(outs)
```

```python
import math

import numpy as np
import jax
import jax.numpy as jnp
from jax import lax
from jax.experimental import pallas as pl
from jax.experimental.pallas import tpu as pltpu

F32 = jnp.float32
BF16 = jnp.bfloat16
MESH = pl.DeviceIdType.MESH

D_MODEL = 1024
N_HEADS = 8
N_KV = 2
HEAD_DIM = 64
ATTN_W = N_HEADS * HEAD_DIM
KV_W = N_KV * HEAD_DIM
N_GROUPS = 8
GROUP_DIM = 64
GMLP_W = N_GROUPS * GROUP_DIM
IN_W = ATTN_W + 2 * KV_W + 2 * GMLP_W
BLOCK = 128
N_BUCKETS = 32
MAX_DISTANCE = 128
D_FF = 2816
ALPHA = 2.0 ** 0.25
LN_EPS = 1e-5
NEG_INF = -1e30
ADAM_LR, ADAM_B1, ADAM_B2, ADAM_EPS, ADAM_WD, ADAM_STEP = 0.001, 0.9, 0.999, 1e-8, 0.01, 10
N_CHIPS = 4
N_DEV = 8
LANES = 128
V7X_VMEM_LIMIT = 56 * 2 ** 20
GELU_C = math.sqrt(2.0 / math.pi)
Q_SCALE = HEAD_DIM ** -0.5
ANY = pl.BlockSpec(memory_space=pl.ANY)

TALL_BS = N_GROUPS * BLOCK
TALL_RB = TALL_BS + 8
TALL_SK = TALL_RB + N_BUCKETS
TALL_ROWS = TALL_SK + 8
WIDE_W = 6 * D_MODEL
WIDE_LAYOUT = {
    "b_ada": (0, 0, 6 * D_MODEL),
    "b_in": (1, 0, IN_W), "ln1_g": (1, IN_W, D_MODEL), "ln1_b": (1, IN_W + D_MODEL, D_MODEL),
    "ln2_g": (1, IN_W + 2 * D_MODEL, D_MODEL), "ln2_b": (1, IN_W + 3 * D_MODEL, D_MODEL),
    "gmlp_ln_g": (2, 0, GMLP_W), "gmlp_ln_b": (2, GMLP_W, GMLP_W), "attn_out_g": (2, 2 * GMLP_W, ATTN_W),
    "gmlp_out_g": (2, 2 * GMLP_W + ATTN_W, GMLP_W), "loss": (2, 3 * GMLP_W + ATTN_W, D_MODEL)}
WIDE_PARAMS = tuple(n for n in WIDE_LAYOUT if n != "loss")


def _params(sem=None):
    return pltpu.CompilerParams(dimension_semantics=sem, vmem_limit_bytes=V7X_VMEM_LIMIT)


def _const_spec(shape, single=False):
    nd = len(shape)
    if single:
        return pl.BlockSpec(shape, lambda *_: (0,) * nd, pipeline_mode=pl.Buffered(1))
    return pl.BlockSpec(shape, lambda *_: (0,) * nd)


def _dot(a, b):
    return jnp.dot(a, b, preferred_element_type=F32)


def _dot_nt(a, b):
    return lax.dot_general(a, b, (((1,), (1,)), ((), ())), preferred_element_type=F32)


def _dot_tn(a, b):
    return lax.dot_general(a, b, (((0,), (0,)), ((), ())), preferred_element_type=F32)


def _gelu(x):
    t = jnp.tanh(GELU_C * (x + 0.044715 * x * x * x))
    return 0.5 * x * (1.0 + t), t


def _gelu_grad(x, t):
    return 0.5 * (1.0 + t) + 0.5 * x * (1.0 - t * t) * GELU_C * (1.0 + 3.0 * 0.044715 * x * x)


def _split_dot(x, a):
    hi = x.astype(BF16)
    lo = (x - hi.astype(F32)).astype(BF16)
    return _dot(hi, a) + _dot(lo, a)


def _group_mean_matrix():
    g = np.arange(GMLP_W) // GROUP_DIM
    return jnp.asarray((g[:, None] == g[None, :]).astype(np.float32) / GROUP_DIM, dtype=BF16)


def _ln_stats(z):
    mu = jnp.mean(z, axis=-1, keepdims=True)
    d = z - mu
    var = jnp.mean(d * d, axis=-1, keepdims=True)
    rstd = lax.rsqrt(var + LN_EPS)
    return d * rstd, rstd


def _ln_bwd(dxhat, xhat, rstd):
    m1 = jnp.mean(dxhat, axis=-1, keepdims=True)
    m2 = jnp.mean(dxhat * xhat, axis=-1, keepdims=True)
    return rstd * (dxhat - m1 - xhat * m2)


def _colsum(x):
    return jnp.sum(x, axis=0, keepdims=True)


def _my_pos():
    return lax.axis_index("x"), lax.axis_index("y"), lax.axis_index("c")


def _other_chips(x, y):
    return [(1 - x, y), (x, 1 - y), (1 - x, 1 - y)]


def _chip_index_scalar():
    ix, iy, _ = _my_pos()
    return jnp.reshape(2 * ix + iy, (1,)).astype(jnp.int32)


def _core_index_scalar():
    return jnp.reshape(lax.axis_index("c"), (1,)).astype(jnp.int32)


class _Gather8:
    def __init__(self, x_refs, out_refs, send_sems, recv_sems, local_sems):
        self.x_refs, self.out_refs = x_refs, out_refs
        self.send_sems, self.recv_sems, self.local_sems = send_sems, recv_sems, local_sems
        self.x, self.y, self.c = _my_pos()
        self.me, self.sibling = (self.x, self.y, self.c), (self.x, self.y, 1 - self.c)
        self.chips = _other_chips(self.x, self.y)

    def _rows(self, a, px, py, pc):
        m_per = self.x_refs[a].shape[0]
        return self.out_refs[a].at[pl.ds((4 * px + 2 * py + pc) * m_per, m_per), :]

    def _copy(self, a, k, block, to, src=None):
        return pltpu.make_async_remote_copy(
            src_ref=self._rows(a, *block) if src is None else src, dst_ref=self._rows(a, *block),
            send_sem=self.send_sems.at[7 * a + k], recv_sem=self.recv_sems.at[7 * a + k], device_id=to,
            device_id_type=MESH)

    def _local(self, a):
        return pltpu.make_async_copy(self.x_refs[a], self._rows(a, *self.me), self.local_sems.at[a])

    def start(self):
        for a in range(len(self.x_refs)):
            self._local(a).start()
            self._copy(a, 0, self.me, self.sibling, src=self.x_refs[a]).start()
            for j, chip in enumerate(self.chips):
                self._copy(a, 1 + j, self.me, (*chip, self.c), src=self.x_refs[a]).start()

    def forward(self):
        for a in range(len(self.x_refs)):
            for j, chip in enumerate(self.chips):
                self._copy(a, 1 + j, (*chip, self.c), self.me).wait_recv()
                self._copy(a, 4 + j, (*chip, self.c), self.sibling).start()

    def finish(self):
        for a in range(len(self.x_refs)):
            self._copy(a, 0, self.sibling, self.me).wait_recv()
            for j, chip in enumerate(self.chips):
                self._copy(a, 4 + j, (*chip, 1 - self.c), self.me).wait_recv()
        for a in range(len(self.x_refs)):
            for k in range(7):
                self._copy(a, k, self.me, self.me).wait_send()
            self._local(a).wait()

    @staticmethod
    def sems(n_v):
        return [pltpu.SemaphoreType.DMA((7 * n_v,)), pltpu.SemaphoreType.DMA((7 * n_v,)),
                pltpu.SemaphoreType.DMA((n_v,))]


def _gathered8_shapes(vs):
    return [jax.ShapeDtypeStruct((N_DEV * v.shape[0], v.shape[1]), v.dtype) for v in vs]


VMEM_WHOLE = pl.BlockSpec(memory_space=pltpu.VMEM)


def _prologue(c_pad, w_ada_s, b_ada_s, shards):
    n = w_ada_s.shape[1]
    n_w = len(shards)

    def body(c_ref, w_ref, b_ref, *rest):
        shard_refs = rest[:n_w]
        sc_ref, modc_ref, modg_ref = rest[n_w:n_w + 3]
        gathered_refs = rest[n_w + 3:2 * n_w + 3]
        call_ref, w_vmem = rest[2 * n_w + 3:2 * n_w + 5]
        sems = rest[2 * n_w + 5:]
        weights = _WeightGather(shard_refs, gathered_refs, ["blk"] * n_w, sems[0], sems[1])
        gather_c = _Gather8([c_ref], [call_ref], sems[2], sems[3], sems[4])
        gather_mod = _Gather8([modc_ref], [modg_ref], sems[5], sems[6], sems[7])
        load_w = pltpu.make_async_copy(w_ref, w_vmem, sems[8])
        weights.start()
        gather_c.start()
        load_w.start()
        gather_c.forward()
        gather_c.finish()
        cv = call_ref[...]
        sc = cv * _sigmoid(cv)
        a_hi = sc.astype(BF16)
        a_lo = (sc - a_hi.astype(F32)).astype(BF16)
        load_w.wait()
        w = w_vmem[...]
        w_hi = w.astype(BF16)
        w_lo = (w - w_hi.astype(F32)).astype(BF16)
        mod = _dot(a_hi, w_hi) + _dot(a_hi, w_lo) + _dot(a_lo, w_hi) + b_ref[...]
        for d in range(N_DEV):
            sc_ref[d:d + 1, :] = sc[8 * d:8 * d + 1, :]
            modc_ref[d:d + 1, :] = mod[8 * d:8 * d + 1, :]
        gather_mod.start()
        weights.forward()
        gather_mod.forward()
        gather_mod.finish()
        weights.forward_diagonal()
        weights.finish()

    outs = pl.pallas_call(
        body, name="prologue",
        out_shape=[jax.ShapeDtypeStruct((N_DEV, D_MODEL), F32), jax.ShapeDtypeStruct((N_DEV, n), F32),
                   jax.ShapeDtypeStruct((N_DEV * N_DEV, n), F32)]
        + [jax.ShapeDtypeStruct(_gathered_shape(sh, "blk"), BF16) for sh in shards],
        in_specs=[VMEM_WHOLE, ANY, VMEM_WHOLE] + [ANY] * n_w,
        out_specs=[VMEM_WHOLE, VMEM_WHOLE, VMEM_WHOLE] + [ANY] * n_w,
        scratch_shapes=[pltpu.VMEM((N_DEV * 8, D_MODEL), F32), pltpu.VMEM(w_ada_s.shape, F32)]
        + _WeightGather.sems(n_w) + _Gather8.sems(1) + _Gather8.sems(1) + [pltpu.SemaphoreType.DMA],
        compiler_params=pltpu.CompilerParams(vmem_limit_bytes=V7X_VMEM_LIMIT),
    )(c_pad, w_ada_s, b_ada_s, *shards)
    return outs[0], outs[2], outs[3:]


def _gathered_shape(shard, kind):
    r, cc = shard.shape
    return (N_CHIPS, r, cc) if kind == "blk" else (r, N_CHIPS * cc)


class _WeightGather:
    N_SEM = 8

    def __init__(self, shards, gathered, kinds, send_sems, recv_sems):
        self.shards, self.gathered, self.kinds = shards, gathered, kinds
        self.send_sems, self.recv_sems = send_sems, recv_sems
        self.x, self.y, self.c = _my_pos()
        self.me, self.sibling = (self.x, self.y, self.c), (self.x, self.y, 1 - self.c)
        self.nbr = ((1 - self.x, self.y), (self.x, 1 - self.y))
        self.diag = 2 * (1 - self.x) + (1 - self.y)

    def _dst(self, a, chip, pc, quarter=None):
        r, cc = self.shards[a].shape
        h = r // 2
        row0, rows = pc * h, h
        if quarter is not None:
            row0, rows = pc * h + quarter * (h // 2), h // 2
        g = self.gathered[a]
        if self.kinds[a] == "blk":
            return g.at[chip, pl.ds(row0, rows), :]
        return g.at[pl.ds(row0, rows), pl.ds(chip * cc, cc)]

    def _copy(self, a, k, region, to, src=None):
        return pltpu.make_async_remote_copy(
            src_ref=region if src is None else src, dst_ref=region, send_sem=self.send_sems.at[a * self.N_SEM + k],
            recv_sem=self.recv_sems.at[a * self.N_SEM + k], device_id=to, device_id_type=MESH)

    def _arrays(self):
        return range(len(self.shards))

    def start(self):
        my_chip = 2 * self.x + self.y
        for a in self._arrays():
            h = self.shards[a].shape[0] // 2
            mine = self.shards[a].at[pl.ds(self.c * h, h), :]
            for j, chip in enumerate(self.nbr):
                self._copy(a, j, self._dst(a, my_chip, self.c), (*chip, self.c), src=mine).start()

    def forward(self):
        for a in self._arrays():
            for j, chip in enumerate(self.nbr):
                cj = 2 * chip[0] + chip[1]
                half = self._dst(a, cj, self.c)
                self._copy(a, j, half, self.me).wait_recv()
                self._copy(a, 2 + j, half, self.sibling).start()
                other = self.nbr[1 - j]
                self._copy(a, 4 + j, self._dst(a, cj, self.c, quarter=j), (*other, self.c)).start()

    def forward_diagonal(self):
        for a in self._arrays():
            for j in range(2):
                quarter = self._dst(a, self.diag, self.c, quarter=j)
                self._copy(a, 4 + j, quarter, self.me).wait_recv()
                self._copy(a, 6 + j, quarter, self.sibling).start()

    def finish(self):
        for a in self._arrays():
            for j, chip in enumerate(self.nbr):
                self._copy(a, 2 + j, self._dst(a, 2 * chip[0] + chip[1], 1 - self.c), self.me).wait_recv()
                self._copy(a, 6 + j, self._dst(a, self.diag, 1 - self.c, quarter=j), self.me).wait_recv()
        for a in self._arrays():
            half = self._dst(a, self.diag, self.c)
            quarter = self._dst(a, self.diag, self.c, quarter=0)
            for k in range(self.N_SEM):
                self._copy(a, k, half if k < 4 else quarter, self.me).wait_send()

    @classmethod
    def sems(cls, n_arr):
        return [pltpu.SemaphoreType.DMA((n_arr * cls.N_SEM,)), pltpu.SemaphoreType.DMA((n_arr * cls.N_SEM,))]


def _insert_own(gathered, shard, kind, chip):
    if kind == "blk":
        return lax.dynamic_update_slice(gathered, shard[None], (chip, 0, 0))
    return lax.dynamic_update_slice(gathered, shard, (0, chip * shard.shape[1]))


def _half_of_full(ref, kind, pc):
    if kind == "blk":
        h = ref.shape[1] // 2
        return ref.at[:, pl.ds(pc * h, h), :]
    h = ref.shape[0] // 2
    return ref.at[pl.ds(pc * h, h), :]


def _half_shape(shape, kind):
    return (shape[0], shape[1] // 2, shape[2]) if kind == "blk" else (shape[0] // 2, shape[1])


class _HalfSwap:
    def __init__(self, ins, outs, kinds, send_sems, recv_sems):
        self.ins, self.outs, self.kinds = ins, outs, kinds
        self.send_sems, self.recv_sems = send_sems, recv_sems
        self.x, self.y, self.c = _my_pos()

    def _copies(self):
        for a in range(len(self.ins)):
            yield pltpu.make_async_remote_copy(
                src_ref=_half_of_full(self.ins[a], self.kinds[a], 1 - self.c), dst_ref=self.outs[a],
                send_sem=self.send_sems.at[a], recv_sem=self.recv_sems.at[a],
                device_id=(self.x, self.y, 1 - self.c), device_id_type=MESH)

    def start(self):
        for cp in self._copies():
            cp.start()

    def wait(self):
        for cp in self._copies():
            cp.wait()

    @staticmethod
    def sems(n_arr):
        return [pltpu.SemaphoreType.DMA((n_arr,)), pltpu.SemaphoreType.DMA((n_arr,))]

    @staticmethod
    def out_shapes(fulls, kinds):
        return [jax.ShapeDtypeStruct(_half_shape(a.shape, k), a.dtype) for a, k in zip(fulls, kinds)]


def _swap_halves(fulls_bf16, kinds, name):
    n_arr = len(fulls_bf16)

    def body(*refs):
        swap = _HalfSwap(refs[:n_arr], refs[n_arr:2 * n_arr], kinds, *refs[2 * n_arr:])
        swap.start()
        swap.wait()

    return pl.pallas_call(
        body, name=name, out_shape=_HalfSwap.out_shapes(fulls_bf16, kinds),
        in_specs=[ANY] * n_arr, out_specs=[ANY] * n_arr, scratch_shapes=_HalfSwap.sems(n_arr),
    )(*fulls_bf16)


def _add_halves(full, got, kind, name):
    hs = _half_shape(full.shape, kind)

    def body(pos_ref, a_ref, b_ref, o_ref, ob_ref):
        p = a_ref[...] + b_ref[...].astype(F32)
        ob_ref[...] = p.astype(BF16)

        @pl.when(pl.program_id(0) == pos_ref[1])
        def _():
            o_ref[...] = p.reshape(o_ref.shape)

    if kind == "blk":
        nb, h, cc = hs
        own = pl.BlockSpec((1, h, cc), lambda b, pos_ref: (b, pos_ref[0], 0))
        other = pl.BlockSpec((1, h, cc), lambda b, pos_ref: (b, 0, 0))
    else:
        h, cc = hs[0], hs[1] // N_CHIPS
        own = pl.BlockSpec((h, cc), lambda b, pos_ref: (pos_ref[0], b))
        other = pl.BlockSpec((h, cc), lambda b, pos_ref: (0, b))
    pos = jnp.concatenate([_core_index_scalar(), _chip_index_scalar()])
    return pl.pallas_call(
        body, name=name, out_shape=(jax.ShapeDtypeStruct((h, cc), F32), jax.ShapeDtypeStruct(hs, BF16)),
        grid_spec=pltpu.PrefetchScalarGridSpec(
            num_scalar_prefetch=1, grid=(N_CHIPS,), in_specs=[own, other],
            out_specs=(pl.BlockSpec((h, cc), lambda b, pos_ref: (0, 0)), other)),
        compiler_params=_params(("arbitrary",)),
    )(pos, full, got)


def _rx_shape(part_shape, kind):
    if kind == "blk":
        return (3, part_shape[1], part_shape[2])
    return (3, part_shape[0], part_shape[1] // N_CHIPS)


class _ChipExchange:
    def __init__(self, parts, rxs, kinds, send_sems, recv_sems):
        self.parts, self.rxs, self.kinds = parts, rxs, kinds
        self.send_sems, self.recv_sems = send_sems, recv_sems
        self.x, self.y, self.c = _my_pos()
        self.chips = _other_chips(self.x, self.y)

    def _copies(self):
        for a in range(len(self.parts)):
            for j, chip in enumerate(self.chips):
                cj = 2 * chip[0] + chip[1]
                if self.kinds[a] == "blk":
                    src = self.parts[a].at[cj]
                else:
                    cc = self.parts[a].shape[1] // N_CHIPS
                    src = self.parts[a].at[:, pl.ds(cj * cc, cc)]
                yield pltpu.make_async_remote_copy(
                    src_ref=src, dst_ref=self.rxs[a].at[j], send_sem=self.send_sems.at[a * 3 + j],
                    recv_sem=self.recv_sems.at[a * 3 + j], device_id=(*chip, self.c), device_id_type=MESH)

    def start(self):
        for cp in self._copies():
            cp.start()

    def wait(self):
        for cp in self._copies():
            cp.wait_recv()
        for cp in self._copies():
            cp.wait_send()

    @staticmethod
    def sems(n_arr):
        return [pltpu.SemaphoreType.DMA((n_arr * 3,)), pltpu.SemaphoreType.DMA((n_arr * 3,))]


def _exchange_chip_partials(parts, kinds, name):
    n_arr = len(parts)

    def body(*refs):
        exchange = _ChipExchange(refs[:n_arr], refs[n_arr:2 * n_arr], kinds, *refs[2 * n_arr:])
        exchange.start()
        exchange.wait()

    return pl.pallas_call(
        body, name=name,
        out_shape=[jax.ShapeDtypeStruct(_rx_shape(p.shape, k), BF16) for p, k in zip(parts, kinds)],
        in_specs=[ANY] * n_arr, out_specs=[ANY] * n_arr, scratch_shapes=_ChipExchange.sems(n_arr),
    )(*parts)


def _sum_chips(part, rx, tr, name):
    _, h, cc = rx.shape
    flips = (2, 1, 3)

    def body(chip_ref, p_ref, rx_ref, o_ref):
        own = p_ref[...]
        for mc in range(N_CHIPS):
            @pl.when(chip_ref[0] == mc)
            def _():
                terms = sorted([(mc, None)] + [(mc ^ f, j) for j, f in enumerate(flips)])
                acc = None
                for _, j in terms:
                    t = own if j is None else rx_ref[j].astype(F32)
                    acc = t if acc is None else acc + t
                o_ref[...] = acc

    return pl.pallas_call(
        body, name=name, out_shape=jax.ShapeDtypeStruct((h, cc), F32),
        grid_spec=pltpu.PrefetchScalarGridSpec(
            num_scalar_prefetch=1, grid=(h // tr,),
            in_specs=[pl.BlockSpec((tr, cc), lambda i, chip_ref: (i, 0)),
                      pl.BlockSpec((3, tr, cc), lambda i, chip_ref: (0, i, 0))],
            out_specs=pl.BlockSpec((tr, cc), lambda i, chip_ref: (i, 0))),
        compiler_params=_params(("arbitrary",)),
    )(_chip_index_scalar(), part, rx)


def _share_halves(halves, name):
    n_arr = len(halves)

    def body(*refs):
        ins, outs = refs[:n_arr], refs[n_arr:2 * n_arr]
        send_sems, recv_sems = refs[2 * n_arr:]
        x, y, c = _my_pos()
        cps = []
        for a in range(n_arr):
            cp = pltpu.make_async_remote_copy(
                src_ref=ins[a], dst_ref=outs[a], send_sem=send_sems.at[a], recv_sem=recv_sems.at[a],
                device_id=(x, y, 1 - c), device_id_type=MESH)
            cp.start()
            cps.append(cp)
        for cp in cps:
            cp.wait()

    return pl.pallas_call(
        body, name=name, out_shape=[jax.ShapeDtypeStruct(h.shape, h.dtype) for h in halves],
        in_specs=[ANY] * n_arr, out_specs=[ANY] * n_arr,
        scratch_shapes=[pltpu.SemaphoreType.DMA((n_arr,)), pltpu.SemaphoreType.DMA((n_arr,))],
    )(*halves)


def _bucket_table():
    qi = jnp.arange(BLOCK)[:, None]
    si = jnp.arange(2 * BLOCK)[None, :]
    dist = qi + BLOCK - si
    max_exact = N_BUCKETS // 2
    n = jnp.maximum(dist, 0)
    nf = jnp.maximum(n, max_exact).astype(F32)
    large = max_exact + (jnp.log(nf / max_exact) / math.log(MAX_DISTANCE / max_exact)
                         * (N_BUCKETS - max_exact)).astype(jnp.int32)
    large = jnp.minimum(large, N_BUCKETS - 1)
    return jnp.where(n < max_exact, n, large).astype(F32)


def _prep_tables(bucket, rel_bias, w_s):
    def body(bucket_ref, rb_ref, ws_ref, bias_ref, wsm_ref):
        qi = lax.broadcasted_iota(jnp.int32, (BLOCK, 2 * BLOCK), 0)
        si = lax.broadcasted_iota(jnp.int32, (BLOCK, 2 * BLOCK), 1)
        dist = qi + BLOCK - si
        in_window = (dist >= 0) & (dist < BLOCK)
        bk = bucket_ref[...]
        for h in range(N_HEADS):
            acc = jnp.zeros((BLOCK, 2 * BLOCK), F32)
            for b in range(N_BUCKETS):
                acc = jnp.where(bk == float(b), rb_ref[b, h], acc)
            bias_ref[h] = jnp.where(in_window, acc, NEG_INF)
        ti = lax.broadcasted_iota(jnp.int32, (BLOCK, BLOCK), 0)
        ui = lax.broadcasted_iota(jnp.int32, (BLOCK, BLOCK), 1)
        for g in range(N_GROUPS):
            wsm_ref[g] = jnp.where(ti >= ui, ws_ref[g], 0.0).astype(BF16)

    return pl.pallas_call(
        body, name="prep_tables",
        out_shape=(jax.ShapeDtypeStruct((N_HEADS, BLOCK, 2 * BLOCK), F32),
                   jax.ShapeDtypeStruct((N_GROUPS, BLOCK, BLOCK), BF16)),
        grid=(1,),
        in_specs=[_const_spec((BLOCK, 2 * BLOCK)), pl.BlockSpec(memory_space=pltpu.SMEM),
                  _const_spec((N_GROUPS, BLOCK, BLOCK))],
        out_specs=(_const_spec((N_HEADS, BLOCK, 2 * BLOCK)), _const_spec((N_GROUPS, BLOCK, BLOCK))),
        compiler_params=_params(("arbitrary",)),
    )(bucket, rel_bias, w_s)


def _fwd_in(x, modr, w_in, b_in, tm, shards, kinds):
    s = x.shape[0]
    n_steps = s // tm
    fwd_step, diag_step = (8 * n_steps) // 16, (13 * n_steps) // 16
    n_w = len(shards)

    def body(x_ref, mod_ref, w_ref, b_ref, *rest):
        shard_refs = rest[:n_w]
        h1_ref, q_ref, kv_ref, gu_ref, gv_ref, xb_ref = rest[n_w:n_w + 6]
        gathered_refs = rest[n_w + 6:2 * n_w + 6]
        send_sems, recv_sems = rest[2 * n_w + 6:]
        i = pl.program_id(0)
        gather = _WeightGather(shard_refs, gathered_refs, kinds, send_sems, recv_sems)

        @pl.when(i == 0)
        def _():
            gather.start()

        xv = x_ref[...]
        xb_ref[...] = xv.astype(BF16)
        h1 = (xv * (1.0 + mod_ref[1:2, :]) + mod_ref[0:1, :]).astype(BF16)
        h1_ref[...] = h1
        proj = jnp.concatenate([_dot(h1, w_ref[j]) for j in range(N_CHIPS)], axis=1) + b_ref[...]
        q_ref[...] = (proj[:, :ATTN_W] * Q_SCALE).astype(BF16)
        kv_ref[...] = proj[:, ATTN_W:ATTN_W + 2 * KV_W].astype(BF16)
        gu_ref[...] = proj[:, ATTN_W + 2 * KV_W:ATTN_W + 2 * KV_W + GMLP_W]
        gv_ref[...] = proj[:, ATTN_W + 2 * KV_W + GMLP_W:]

        @pl.when(i == fwd_step)
        def _():
            gather.forward()

        @pl.when(i == diag_step)
        def _():
            gather.forward_diagonal()

        @pl.when(i == n_steps - 1)
        def _():
            gather.finish()

    row = lambda w: pl.BlockSpec((tm, w), lambda i: (i, 0))
    outs = pl.pallas_call(
        body, name="fwd_in",
        out_shape=[jax.ShapeDtypeStruct((s, D_MODEL), BF16), jax.ShapeDtypeStruct((s, ATTN_W), BF16),
                   jax.ShapeDtypeStruct((s, 2 * KV_W), BF16), jax.ShapeDtypeStruct((s, GMLP_W), F32),
                   jax.ShapeDtypeStruct((s, GMLP_W), F32), jax.ShapeDtypeStruct((s, D_MODEL), BF16)]
        + [jax.ShapeDtypeStruct(_gathered_shape(sh, k), BF16) for sh, k in zip(shards, kinds)],
        grid=(n_steps,),
        in_specs=[row(D_MODEL), _const_spec((8, D_MODEL)), _const_spec(w_in.shape), _const_spec((1, IN_W))]
        + [ANY] * n_w,
        out_specs=[row(D_MODEL), row(ATTN_W), row(2 * KV_W), row(GMLP_W), row(GMLP_W), row(D_MODEL)] + [ANY] * n_w,
        scratch_shapes=_WeightGather.sems(n_w),
        compiler_params=_params(("arbitrary",)),
    )(x, modr, w_in, b_in, *shards)
    return outs[:6], outs[6:]


def _kv_variants(kk):
    kf = kk.astype(F32)
    lane = lax.broadcasted_iota(jnp.int32, kf.shape, 1)
    low = lane < HEAD_DIM
    k0_lo = jnp.where(low, kf, 0.0)
    k1_hi = jnp.where(low, 0.0, kf)
    k0_hi = pltpu.roll(k0_lo, HEAD_DIM, 1)
    k1_lo = pltpu.roll(k1_hi, HEAD_DIM, 1)
    return ((k0_lo.astype(BF16), k0_hi.astype(BF16)), (k1_lo.astype(BF16), k1_hi.astype(BF16)))


def _head_kv(h):
    return h // (N_HEADS // N_KV), h % 2


MIX_GROUP = 2


def _interleave(*gens):
    results = [None] * len(gens)
    active = list(enumerate(gens))
    while active:
        still = []
        for i, g in active:
            try:
                next(g)
                still.append((i, g))
            except StopIteration as done:
                results[i] = done.value
        active = still
    return results


def _attn_block_fwd(q_blk, kk, vv, bias_ref, sinks_ref, first_mask):
    kvar = _kv_variants(kk)
    vvar = _kv_variants(vv)
    heads = range(N_HEADS)
    q_pairs = [q_blk[:, (h // 2) * LANES:(h // 2 + 1) * LANES] for h in heads]
    logits = [_dot_nt(q_pairs[h], kvar[_head_kv(h)[0]][_head_kv(h)[1]]) + bias_ref[h] for h in heads]
    if first_mask is not None:
        logits = [jnp.where(first_mask, NEG_INF, lg) for lg in logits]
    yield
    ms = [jnp.maximum(jnp.max(logits[h], axis=-1, keepdims=True), sinks_ref[h]) for h in heads]
    yield
    es = [jnp.exp(logits[h] - ms[h]) for h in heads]
    ess = [jnp.exp(sinks_ref[h] - ms[h]) for h in heads]
    yield
    invs = [1.0 / (jnp.sum(es[h], axis=-1, keepdims=True) + ess[h]) for h in heads]
    probs = [(es[h] * invs[h], ess[h] * invs[h]) for h in heads]
    yield
    outs = [_dot(probs[h][0].astype(BF16), vvar[_head_kv(h)[0]][_head_kv(h)[1]]) for h in heads]
    pairs = [outs[2 * i] + outs[2 * i + 1] for i in range(N_HEADS // 2)]
    return jnp.concatenate(pairs, axis=1), probs, kvar, vvar


def _gmlp_chunk_fwd(gu, gv, ln_g, ln_b, wsm_ref, bsx, amat):
    u, tu = _gelu(gu)
    a, ta = _gelu(gv)
    yield
    mean = _split_dot(a, amat)
    d = a - mean
    yield
    var = _split_dot(d * d, amat)
    yield
    rstd = lax.rsqrt(var + LN_EPS)
    xhat = d * rstd
    vb = (xhat * ln_g + ln_b).astype(BF16)
    yield
    lane = lax.broadcasted_iota(jnp.int32, (BLOCK, LANES), 1)
    low = lane < GROUP_DIM
    cols = []
    for pair in range(N_GROUPS // 2):
        vp = vb[:, pair * LANES:(pair + 1) * LANES]
        cols.append(jnp.where(low, _dot(wsm_ref[2 * pair], vp), _dot(wsm_ref[2 * pair + 1], vp)))
    mixedv = jnp.concatenate(cols, axis=1) + bsx
    return u * mixedv, (u, tu, ta, xhat, rstd, vb, mixedv)


def _rms(a, g):
    r = lax.rsqrt(jnp.mean(a * a, axis=-1, keepdims=True) + LN_EPS)
    return a * r * g, r


def _fwd_mix(q, kv, gu, gv, x, modr, bias, sinks, gln_g, gln_b, wsm, bsx, amat, aog, gog, w_out, ln1_g, ln1_b, tm,
             ffn_shards, ffn_kinds):
    s = x.shape[0]
    nb = tm // BLOCK
    n_steps = s // tm
    fwd_step, diag_step = (7 * n_steps) // 16, (12 * n_steps) // 16
    n_w = len(ffn_shards)

    def body(q_ref, kv_ref, kvp_ref, gu_ref, gv_ref, x_ref, mod_ref, bias_ref, sinks_ref, glng_ref, glnb_ref, wsm_ref,
             bsx_ref, amat_ref, aog_ref, gog_ref, wout_ref, ln1g_ref, ln1b_ref, *rest):
        shard_refs = rest[:n_w]
        x1_ref, x1b_ref, y_ref, mixed_ref = rest[n_w:n_w + 4]
        gathered_refs = rest[n_w + 4:2 * n_w + 4]
        mix_scr, send_sems, recv_sems = rest[2 * n_w + 4:]
        i = pl.program_id(0)
        gather = _WeightGather(shard_refs, gathered_refs, ffn_kinds, send_sems, recv_sems)

        @pl.when(i == 0)
        def _():
            gather.start()

        col = lax.broadcasted_iota(jnp.int32, (BLOCK, 2 * BLOCK), 1)
        for b0 in range(0, nb, MIX_GROUP):
            gens = []
            for b in range(b0, min(b0 + MIX_GROUP, nb)):
                r0 = b * BLOCK
                if b == 0:
                    kvprev = kvp_ref[...]
                    first_mask = (col < BLOCK) & (i == 0)
                else:
                    kvprev = kv_ref[r0 - BLOCK:r0, :]
                    first_mask = None
                kvcur = kv_ref[r0:r0 + BLOCK, :]
                kk = jnp.concatenate([kvprev[:, :KV_W], kvcur[:, :KV_W]], axis=0)
                vv = jnp.concatenate([kvprev[:, KV_W:], kvcur[:, KV_W:]], axis=0)
                gens.append(_attn_block_fwd(q_ref[r0:r0 + BLOCK, :], kk, vv, bias_ref, sinks_ref, first_mask))
                gens.append(_gmlp_chunk_fwd(gu_ref[r0:r0 + BLOCK, :], gv_ref[r0:r0 + BLOCK, :], glng_ref[...],
                                            glnb_ref[...], wsm_ref, bsx_ref[...], amat_ref[...]))
            res = _interleave(*gens)
            for k, b in enumerate(range(b0, min(b0 + MIX_GROUP, nb))):
                r0 = b * BLOCK
                na, _ = _rms(res[2 * k][0], aog_ref[...])
                ng, _ = _rms(res[2 * k + 1][0], gog_ref[...])
                mix_scr[r0:r0 + BLOCK, :ATTN_W] = na.astype(BF16)
                mix_scr[r0:r0 + BLOCK, ATTN_W:] = ng.astype(BF16)
        mixed = mix_scr[...]
        mixed_ref[...] = mixed
        y = _dot(mixed, wout_ref[...])
        y_ref[...] = y.astype(BF16)
        z1 = ALPHA * x_ref[...] + mod_ref[2:3, :] * y
        xhat, _ = _ln_stats(z1)
        x1 = xhat * ln1g_ref[...] + ln1b_ref[...]
        x1_ref[...] = x1
        x1b_ref[...] = x1.astype(BF16)

        @pl.when(i == fwd_step)
        def _():
            gather.forward()

        @pl.when(i == diag_step)
        def _():
            gather.forward_diagonal()

        @pl.when(i == n_steps - 1)
        def _():
            gather.finish()

    row = lambda w: pl.BlockSpec((tm, w), lambda i: (i, 0))
    prev = pl.BlockSpec((BLOCK, 2 * KV_W), lambda i: (jnp.maximum(i * nb - 1, 0), 0))
    outs = pl.pallas_call(
        body, name="fwd_mix",
        out_shape=[jax.ShapeDtypeStruct((s, D_MODEL), F32)] + [jax.ShapeDtypeStruct((s, D_MODEL), BF16)] * 3
        + [jax.ShapeDtypeStruct(_gathered_shape(sh, k), BF16) for sh, k in zip(ffn_shards, ffn_kinds)],
        grid=(n_steps,),
        in_specs=[row(ATTN_W), row(2 * KV_W), prev, row(GMLP_W), row(GMLP_W), row(D_MODEL), _const_spec((8, D_MODEL)),
                  _const_spec((N_HEADS, BLOCK, 2 * BLOCK)), pl.BlockSpec(memory_space=pltpu.SMEM),
                  _const_spec((1, GMLP_W)), _const_spec((1, GMLP_W)), _const_spec((N_GROUPS, BLOCK, BLOCK)),
                  _const_spec((BLOCK, GMLP_W)), _const_spec((GMLP_W, GMLP_W)), _const_spec((1, ATTN_W)),
                  _const_spec((1, GMLP_W)), _const_spec((D_MODEL, D_MODEL)), _const_spec((1, D_MODEL)),
                  _const_spec((1, D_MODEL))] + [ANY] * n_w,
        out_specs=[row(D_MODEL)] * 4 + [ANY] * n_w,
        scratch_shapes=[pltpu.VMEM((tm, D_MODEL), BF16)] + _WeightGather.sems(n_w),
        compiler_params=_params(("arbitrary",)),
    )(q, kv, kv, gu, gv, x, modr, bias, sinks, gln_g, gln_b, wsm, bsx, amat, aog, gog, w_out, ln1_g, ln1_b, *ffn_shards)
    return outs[:4], outs[4:]


FF_BLOCKS = N_CHIPS // 2
FF_CHUNK = D_FF // FF_BLOCKS
FFN_SUB = 256


def _sigmoid(x):
    return 1.0 / (1.0 + jnp.exp(-x))


def _fwd_ffn(x1, target, modr, ln2_g, ln2_b, w_gu, w_dn, tm):
    s = x1.shape[0]

    def body(x1_ref, t_ref, mod_ref, g_ref, b_ref, wgu_ref, wdn_ref, h2_ref, act_ref, dy2_ref, dx1a_ref, acc_ref):
        @pl.when(pl.program_id(0) == 0)
        def _():
            acc_ref[...] = jnp.zeros_like(acc_ref)

        x1v = x1_ref[...]
        h2 = (x1v * (1.0 + mod_ref[4:5, :]) + mod_ref[3:4, :]).astype(BF16)
        h2_ref[...] = h2
        y2 = None
        for cc in range(FF_BLOCKS):
            c0 = cc * FF_CHUNK
            gate = _dot(h2, wgu_ref[cc])
            up = _dot(h2, wgu_ref[FF_BLOCKS + cc])
            act_ref[:, c0:c0 + FF_CHUNK] = gate.astype(BF16)
            act_ref[:, D_FF + c0:D_FF + c0 + FF_CHUNK] = up.astype(BF16)
            a = (gate * _sigmoid(gate) * up).astype(BF16)
            part = _dot(a, wdn_ref[c0:c0 + FF_CHUNK, :])
            y2 = part if y2 is None else y2 + part
        g2 = mod_ref[5:6, :]
        z2 = ALPHA * x1v + g2 * y2
        xhat, rstd = _ln_stats(z2)
        gain = g_ref[...]
        diff = xhat * gain + b_ref[...] - t_ref[...]
        dx2 = diff * (1.0 / D_MODEL)
        dz2 = _ln_bwd(dx2 * gain, xhat, rstd)
        dx1a_ref[...] = ALPHA * dz2
        dy2_ref[...] = (g2 * dz2).astype(BF16)
        acc_ref[0:1, :] += _colsum(diff * diff)
        acc_ref[1:2, :] += _colsum(dx2 * xhat)
        acc_ref[2:3, :] += _colsum(dx2)
        acc_ref[3:4, :] += _colsum(dz2 * y2)

    row = lambda w: pl.BlockSpec((tm, w), lambda i: (i, 0))
    return pl.pallas_call(
        body, name="fwd_ffn",
        out_shape=(jax.ShapeDtypeStruct((s, D_MODEL), BF16), jax.ShapeDtypeStruct((s, 2 * D_FF), BF16),
                   jax.ShapeDtypeStruct((s, D_MODEL), BF16), jax.ShapeDtypeStruct((s, D_MODEL), F32),
                   jax.ShapeDtypeStruct((8, D_MODEL), F32)),
        grid=(s // tm,),
        in_specs=[row(D_MODEL), row(D_MODEL), _const_spec((8, D_MODEL)), _const_spec((1, D_MODEL)),
                  _const_spec((1, D_MODEL)), _const_spec((N_CHIPS, D_MODEL, FF_CHUNK), single=True),
                  _const_spec((D_FF, D_MODEL), single=True)],
        out_specs=(row(D_MODEL), row(2 * D_FF), row(D_MODEL), row(D_MODEL), _const_spec((8, D_MODEL))),
        compiler_params=_params(("arbitrary",)),
    )(x1, target, modr, ln2_g, ln2_b, w_gu, w_dn)


def _bwd_ffn(dy2, act, w_gu, w_dn, tm):
    s = dy2.shape[0]

    def body(dy2_ref, act_ref, wgu_ref, wdn_ref, a_ref, dgu_ref, dh2_ref):
        dy2v = dy2_ref[...]
        dh2 = None
        for cc in range(FF_BLOCKS):
            c0 = cc * FF_CHUNK
            da = _dot_nt(dy2v, wdn_ref[c0:c0 + FF_CHUNK, :])
            gate = act_ref[:, c0:c0 + FF_CHUNK].astype(F32)
            up = act_ref[:, D_FF + c0:D_FF + c0 + FF_CHUNK].astype(F32)
            sg = _sigmoid(gate)
            sl = gate * sg
            a_ref[:, c0:c0 + FF_CHUNK] = (sl * up).astype(BF16)
            dgate = (da * up * (sg * (1.0 + gate * (1.0 - sg)))).astype(BF16)
            dup = (da * sl).astype(BF16)
            dgu_ref[:, c0:c0 + FF_CHUNK] = dgate
            dgu_ref[:, D_FF + c0:D_FF + c0 + FF_CHUNK] = dup
            part = _dot_nt(dgate, wgu_ref[cc]) + _dot_nt(dup, wgu_ref[FF_BLOCKS + cc])
            dh2 = part if dh2 is None else dh2 + part
        dh2_ref[...] = dh2.astype(BF16)

    row = lambda w: pl.BlockSpec((tm, w), lambda i: (i, 0))
    return pl.pallas_call(
        body, name="bwd_ffn",
        out_shape=(jax.ShapeDtypeStruct((s, D_FF), BF16), jax.ShapeDtypeStruct((s, 2 * D_FF), BF16),
                   jax.ShapeDtypeStruct((s, D_MODEL), BF16)),
        grid=(s // tm,),
        in_specs=[row(D_MODEL), row(2 * D_FF), _const_spec((N_CHIPS, D_MODEL, FF_CHUNK), single=True),
                  _const_spec((D_FF, D_MODEL), single=True)],
        out_specs=(row(D_FF), row(2 * D_FF), row(D_MODEL)),
        compiler_params=_params(("parallel",)),
    )(dy2, act, w_gu, w_dn)


def _bwd_mid(dh2, dx1a, x1, x, y, modr, ln1_g, w_out, tm, swap_fulls, swap_kinds):
    s = x.shape[0]
    n_steps = s // tm
    n_g = len(swap_fulls)

    def body(dh2_ref, dx1a_ref, x1_ref, x_ref, y_ref, mod_ref, g_ref, wout_ref, *rest):
        full_refs = rest[:n_g]
        dxa_ref, dy_ref, dmix_ref, acc_ref = rest[n_g:n_g + 4]
        got_refs = rest[n_g + 4:2 * n_g + 4]
        swap = _HalfSwap(full_refs, got_refs, swap_kinds, *rest[2 * n_g + 4:])
        i = pl.program_id(0)

        @pl.when(i == 0)
        def _():
            swap.start()
            acc_ref[...] = jnp.zeros_like(acc_ref)

        dh2 = dh2_ref[...].astype(F32)
        x1v = x1_ref[...].astype(F32)
        yv = y_ref[...].astype(F32)
        g1 = mod_ref[2:3, :]
        dx1 = dx1a_ref[...] + dh2 * (1.0 + mod_ref[4:5, :])
        z1 = ALPHA * x_ref[...] + g1 * yv
        xhat, rstd = _ln_stats(z1)
        dz1 = _ln_bwd(dx1 * g_ref[...], xhat, rstd)
        dxa_ref[...] = (ALPHA * dz1).astype(BF16)
        dy = (g1 * dz1).astype(BF16)
        dy_ref[...] = dy
        dmix_ref[...] = _dot_nt(dy, wout_ref[...]).astype(BF16)
        acc_ref[0:1, :] += _colsum(dh2 * x1v)
        acc_ref[1:2, :] += _colsum(dh2)
        acc_ref[2:3, :] += _colsum(dx1 * xhat)
        acc_ref[3:4, :] += _colsum(dx1)
        acc_ref[4:5, :] += _colsum(dz1 * yv)

        @pl.when(i == n_steps - 1)
        def _():
            swap.wait()

    row = lambda w: pl.BlockSpec((tm, w), lambda i: (i, 0))
    outs = pl.pallas_call(
        body, name="bwd_mid",
        out_shape=[jax.ShapeDtypeStruct((s, D_MODEL), BF16), jax.ShapeDtypeStruct((s, D_MODEL), BF16),
                   jax.ShapeDtypeStruct((s, D_MODEL), BF16), jax.ShapeDtypeStruct((8, D_MODEL), F32)]
        + _HalfSwap.out_shapes(swap_fulls, swap_kinds),
        grid=(n_steps,),
        in_specs=[row(D_MODEL)] * 5 + [_const_spec((8, D_MODEL)), _const_spec((1, D_MODEL)),
                                       _const_spec((D_MODEL, D_MODEL))] + [ANY] * n_g,
        out_specs=[row(D_MODEL), row(D_MODEL), row(D_MODEL), _const_spec((8, D_MODEL))] + [ANY] * n_g,
        scratch_shapes=_HalfSwap.sems(n_g),
        compiler_params=_params(("arbitrary",)),
    )(dh2, dx1a, x1, x, y, modr, ln1_g, w_out, *swap_fulls)
    return outs[:4], outs[4:]


def _fold_kv(t0, t1):
    lane = lax.broadcasted_iota(jnp.int32, t0.shape, 1)
    f0 = t0 + pltpu.roll(t0, HEAD_DIM, 1)
    f1 = t1 + pltpu.roll(t1, HEAD_DIM, 1)
    return jnp.where(lane < HEAD_DIM, f0, f1)


def _bwd_mix(q, kv, gu, gv, dmix, bias, sinks, gln_g, gln_b, wsm, bsx, amat, aog, gog, grad_parts, grad_kinds):
    s = q.shape[0]
    tile = 2 * BLOCK
    n_steps = s // tile
    n_g = len(grad_parts)

    def body(q_ref, kv_ref, kvp_ref, gu_ref, gv_ref, dmix_ref, bias_ref, sinks_ref, glng_ref, glnb_ref, wsm_ref,
             bsx_ref, amat_ref, aog_ref, gog_ref, *rest):
        part_refs = rest[:n_g]
        dq_ref, dkv_ref, dgu_ref, dgv_ref, gbias_ref, dws_ref, dbs_ref, vec_ref, dsink_ref = rest[n_g:n_g + 9]
        rx_refs = rest[n_g + 9:2 * n_g + 9]
        carry, done, send_sems, recv_sems = rest[2 * n_g + 9:]
        n = pl.program_id(0)
        exchange = _ChipExchange(part_refs, rx_refs, grad_kinds, send_sems, recv_sems)

        @pl.when(n == 0)
        def _():
            exchange.start()
            carry[...] = jnp.zeros_like(carry)
            done[...] = jnp.zeros_like(done)
            gbias_ref[...] = jnp.zeros_like(gbias_ref)
            dws_ref[...] = jnp.zeros_like(dws_ref)
            dbs_ref[...] = jnp.zeros_like(dbs_ref)
            vec_ref[...] = jnp.zeros_like(vec_ref)
            dsink_ref[...] = jnp.zeros_like(dsink_ref)

        @pl.when(n == n_steps)
        def _():
            dkv_ref[:BLOCK, :] = done[...].astype(BF16)
            dkv_ref[BLOCK:, :] = carry[...].astype(BF16)
            exchange.wait()

        @pl.when(n < n_steps)
        def _():
            col = lax.broadcasted_iota(jnp.int32, (BLOCK, 2 * BLOCK), 1)
            lane = lax.broadcasted_iota(jnp.int32, (BLOCK, LANES), 1)
            low = lane < HEAD_DIM
            rows = [slice(0, BLOCK), slice(BLOCK, tile)]
            kv_blocks = [kvp_ref[...], kv_ref[rows[0], :], kv_ref[rows[1], :]]
            masks = [(col < BLOCK) & (n == 0), None]
            q_blks = [q_ref[r, :] for r in rows]
            fwd = []
            for b in range(2):
                kk = jnp.concatenate([kv_blocks[b][:, :KV_W], kv_blocks[b + 1][:, :KV_W]], axis=0)
                vv = jnp.concatenate([kv_blocks[b][:, KV_W:], kv_blocks[b + 1][:, KV_W:]], axis=0)
                fwd.append(_attn_block_fwd(q_blks[b], kk, vv, bias_ref, sinks_ref, masks[b]))
                fwd.append(_gmlp_chunk_fwd(gu_ref[rows[b], :], gv_ref[rows[b], :], glng_ref[...], glnb_ref[...],
                                           wsm_ref, bsx_ref[...], amat_ref[...]))
            res = _interleave(*fwd[:2]) + _interleave(*fwd[2:])

            def gating_bwd(b, d_gm, saved):
                u, tu, ta, xhat, rstd, vb, mixedv = saved
                dgu_ref[rows[b], :] = (d_gm * mixedv * _gelu_grad(gu_ref[rows[b], :], tu)).astype(BF16)
                dmx = d_gm * u
                dmxb = dmx.astype(BF16)
                yield
                dvn_cols, dws = [], []
                for pair in range(N_GROUPS // 2):
                    dp_ = dmxb[:, pair * LANES:(pair + 1) * LANES]
                    vp = vb[:, pair * LANES:(pair + 1) * LANES]
                    dvn_cols.append(
                        jnp.where(low, _dot_tn(wsm_ref[2 * pair], dp_), _dot_tn(wsm_ref[2 * pair + 1], dp_)))
                    zero = jnp.zeros_like(dp_)
                    dws.append(_dot_nt(jnp.where(low, dp_, zero), vp))
                    dws.append(_dot_nt(jnp.where(low, zero, dp_), vp))
                dvn = jnp.concatenate(dvn_cols, axis=1)
                yield
                dxh = dvn * glng_ref[...]
                am = amat_ref[...]
                m1 = _split_dot(dxh, am)
                m2 = _split_dot(dxh * xhat, am)
                yield
                da = rstd * (dxh - m1 - xhat * m2)
                dgv_ref[rows[b], :] = (da * _gelu_grad(gv_ref[rows[b], :], ta)).astype(BF16)
                return dmx, dws, _colsum(dvn * xhat), _colsum(dvn)

            def attention_bwd(b, d_attn, probs, kvar, vvar):
                heads = range(N_HEADS)
                sels = [low if h % 2 == 0 else jnp.logical_not(low) for h in heads]
                pair_of = lambda a, h: a[:, (h // 2) * LANES:(h // 2 + 1) * LANES]
                do_hs = [jnp.where(sels[h], pair_of(d_attn, h), 0.0).astype(BF16) for h in heads]
                q_hs = [jnp.where(sels[h], pair_of(q_blks[b], h), jnp.zeros((BLOCK, LANES), BF16)) for h in heads]
                dps = [_dot_nt(do_hs[h], vvar[_head_kv(h)[0]][_head_kv(h)[1]]) for h in heads]
                yield
                deltas = [jnp.sum(probs[h][0] * dps[h], axis=-1, keepdims=True) for h in heads]
                yield
                dss = [probs[h][0] * (dps[h] - deltas[h]) for h in heads]
                dsinks = [-(probs[h][1] * deltas[h]) for h in heads]
                dsbs = [ds.astype(BF16) for ds in dss]
                pbs = [probs[h][0].astype(BF16) for h in heads]
                yield
                dqs = [_dot(dsbs[h], kvar[_head_kv(h)[0]][_head_kv(h)[1]]) for h in heads]
                tks = [_dot_tn(dsbs[h], q_hs[h]) for h in heads]
                tvs = [_dot_tn(pbs[h], do_hs[h]) for h in heads]
                dq_cols = [dqs[2 * i] + dqs[2 * i + 1] for i in range(N_HEADS // 2)]
                dq_ref[rows[b], :] = (jnp.concatenate(dq_cols, axis=1) * Q_SCALE).astype(BF16)
                per_kv = N_HEADS // N_KV
                kv_sum = lambda ts, kvh: sum(ts[kvh * per_kv + 1:(kvh + 1) * per_kv], ts[kvh * per_kv])
                dkk = _fold_kv(kv_sum(tks, 0), kv_sum(tks, 1))
                dvv = _fold_kv(kv_sum(tvs, 0), kv_sum(tvs, 1))
                return jnp.concatenate([dkk, dvv], axis=1), dss, dsinks

            bwd, rms_g = [], []
            for b in range(2):
                attn, probs, kvar, vvar = res[2 * b]
                gm, saved = res[2 * b + 1]
                na_unit, r_a = _rms(attn, 1.0)
                ng_unit, r_g = _rms(gm, 1.0)
                dmix = dmix_ref[rows[b], :].astype(F32)
                dn_a = dmix[:, :ATTN_W]
                dn_g = dmix[:, ATTN_W:]
                rms_g.append((_colsum(dn_a * na_unit), _colsum(dn_g * ng_unit)))
                t_a = dn_a * aog_ref[...]
                d_attn = r_a * t_a - na_unit * (r_a * jnp.mean(t_a * na_unit, axis=-1, keepdims=True))
                t_g = dn_g * gog_ref[...]
                d_gm = r_g * t_g - ng_unit * (r_g * jnp.mean(t_g * ng_unit, axis=-1, keepdims=True))
                bwd.append(attention_bwd(b, d_attn, probs, kvar, vvar))
                bwd.append(gating_bwd(b, d_gm, saved))
            (dkv_a, dss_a, dsk_a), (dmx_a, dws_a, glg_a, glb_a) = _interleave(*bwd[:2])
            (dkv_b, dss_b, dsk_b), (dmx_b, dws_b, glg_b, glb_b) = _interleave(*bwd[2:])

            vec_ref[0:1, :] += rms_g[0][0] + rms_g[1][0]
            vec_ref[1:2, :] += rms_g[0][1] + rms_g[1][1]
            vec_ref[2:3, :] += glg_a + glg_b
            vec_ref[3:4, :] += glb_a + glb_b
            dbs_ref[...] += dmx_a + dmx_b
            for g in range(N_GROUPS):
                dws_ref[g] += dws_a[g] + dws_b[g]
            for h in range(N_HEADS):
                gbias_ref[h] += dss_a[h] + dss_b[h]
                dsink_ref[h] += dsk_a[h] + dsk_b[h]

            dkv_ref[:BLOCK, :] = done[...].astype(BF16)
            dkv_ref[BLOCK:, :] = (carry[...] + dkv_a[:BLOCK]).astype(BF16)
            done[...] = dkv_a[BLOCK:] + dkv_b[:BLOCK]
            carry[...] = dkv_b[BLOCK:]

    last = n_steps - 1
    cur = lambda w: pl.BlockSpec((tile, w), lambda n: (jnp.minimum(n, last), 0))
    late = lambda w: pl.BlockSpec((tile, w), lambda n: (jnp.clip(n - 1, 0, last), 0))
    before = pl.BlockSpec((BLOCK, 2 * KV_W), lambda n: (jnp.clip(2 * n - 1, 0, 2 * last + 1), 0))
    outs = pl.pallas_call(
        body, name="bwd_mix",
        out_shape=[jax.ShapeDtypeStruct((s, ATTN_W), BF16), jax.ShapeDtypeStruct((s, 2 * KV_W), BF16),
                   jax.ShapeDtypeStruct((s, GMLP_W), BF16), jax.ShapeDtypeStruct((s, GMLP_W), BF16),
                   jax.ShapeDtypeStruct((N_HEADS, BLOCK, 2 * BLOCK), F32),
                   jax.ShapeDtypeStruct((N_GROUPS, BLOCK, BLOCK), F32),
                   jax.ShapeDtypeStruct((BLOCK, GMLP_W), F32), jax.ShapeDtypeStruct((8, GMLP_W), F32),
                   jax.ShapeDtypeStruct((N_HEADS, BLOCK, 1), F32)]
        + [jax.ShapeDtypeStruct(_rx_shape(p.shape, k), BF16) for p, k in zip(grad_parts, grad_kinds)],
        grid=(n_steps + 1,),
        in_specs=[cur(ATTN_W), cur(2 * KV_W), before, cur(GMLP_W), cur(GMLP_W), cur(D_MODEL),
                  _const_spec((N_HEADS, BLOCK, 2 * BLOCK)), pl.BlockSpec(memory_space=pltpu.SMEM),
                  _const_spec((1, GMLP_W)), _const_spec((1, GMLP_W)), _const_spec((N_GROUPS, BLOCK, BLOCK)),
                  _const_spec((BLOCK, GMLP_W)), _const_spec((GMLP_W, GMLP_W)), _const_spec((1, ATTN_W)),
                  _const_spec((1, GMLP_W))] + [ANY] * n_g,
        out_specs=[cur(ATTN_W), late(2 * KV_W), cur(GMLP_W), cur(GMLP_W),
                   _const_spec((N_HEADS, BLOCK, 2 * BLOCK)), _const_spec((N_GROUPS, BLOCK, BLOCK)),
                   _const_spec((BLOCK, GMLP_W)), _const_spec((8, GMLP_W)), _const_spec((N_HEADS, BLOCK, 1))]
        + [ANY] * n_g,
        scratch_shapes=[pltpu.VMEM((BLOCK, 2 * KV_W), F32), pltpu.VMEM((BLOCK, 2 * KV_W), F32)]
        + _ChipExchange.sems(n_g),
        compiler_params=_params(("arbitrary",)),
    )(q, kv, kv, gu, gv, dmix, bias, sinks, gln_g, gln_b, wsm, bsx, amat, aog, gog, *grad_parts)
    return outs[:9], outs[9:]


def _mix_finalize(gbias, bucket, dws, dbs, dsink):
    def body(gb_ref, bucket_ref, dws_ref, dbs_ref, dsink_ref, tall_ref):
        bk = bucket_ref[...]
        lane = lax.broadcasted_iota(jnp.int32, (N_BUCKETS, LANES), 1)
        rowi = lax.broadcasted_iota(jnp.int32, (N_BUCKETS, LANES), 0)
        drb = jnp.zeros((N_BUCKETS, LANES), F32)
        dsk = jnp.zeros((8, LANES), F32)
        lane8 = lax.broadcasted_iota(jnp.int32, (8, LANES), 1)
        for h in range(N_HEADS):
            g = gb_ref[h]
            for b in range(N_BUCKETS):
                tot = jnp.sum(_colsum(jnp.where(bk == float(b), g, 0.0)), axis=1, keepdims=True)
                drb = jnp.where((lane == h) & (rowi == b), tot, drb)
            sk = jnp.sum(dsink_ref[h], axis=0, keepdims=True)
            dsk = jnp.where(lane8 == h, sk, dsk)
        tall_ref[TALL_RB:TALL_RB + N_BUCKETS, :] = drb
        tall_ref[TALL_SK:TALL_SK + 8, :] = dsk
        ti = lax.broadcasted_iota(jnp.int32, (BLOCK, BLOCK), 0)
        ui = lax.broadcasted_iota(jnp.int32, (BLOCK, BLOCK), 1)
        for g in range(N_GROUPS):
            tall_ref[g * BLOCK:(g + 1) * BLOCK, :] = jnp.where(ti >= ui, dws_ref[g], 0.0)
        gi = lax.broadcasted_iota(jnp.int32, (GMLP_W, LANES), 0) // GROUP_DIM
        li = lax.broadcasted_iota(jnp.int32, (GMLP_W, LANES), 1)
        ind = jnp.where(gi == li, 1.0, 0.0).astype(BF16)
        d = dbs_ref[...]
        hi = d.astype(BF16)
        r1 = d - hi.astype(F32)
        mid = r1.astype(BF16)
        lo = (r1 - mid.astype(F32)).astype(BF16)
        dbsg = _dot(hi, ind) + _dot(mid, ind) + _dot(lo, ind)
        tall_ref[TALL_BS:TALL_BS + N_GROUPS, :] = dbsg.T[:N_GROUPS, :]

    return pl.pallas_call(
        body, name="mix_finalize", out_shape=jax.ShapeDtypeStruct((TALL_ROWS, LANES), F32), grid=(1,),
        in_specs=[_const_spec((N_HEADS, BLOCK, 2 * BLOCK)), _const_spec((BLOCK, 2 * BLOCK)),
                  _const_spec((N_GROUPS, BLOCK, BLOCK)), _const_spec((BLOCK, GMLP_W)),
                  _const_spec((N_HEADS, BLOCK, 1))],
        out_specs=_const_spec((TALL_ROWS, LANES)),
        compiler_params=_params(("arbitrary",)),
    )(gbias, bucket, dws, dbs, dsink)


def _bwd_in(dq, dkv, dgu, dgv, dxa, x, modr, w_in, tm):
    s = x.shape[0]

    def body(dq_ref, dkv_ref, dgu_ref, dgv_ref, dxa_ref, x_ref, mod_ref, w_ref, gx_ref, acc_ref, db_ref):
        @pl.when(pl.program_id(0) == 0)
        def _():
            acc_ref[...] = jnp.zeros_like(acc_ref)
            db_ref[...] = jnp.zeros_like(db_ref)

        dproj = jnp.concatenate([dq_ref[...], dkv_ref[...], dgu_ref[...], dgv_ref[...]], axis=1)
        wb = IN_W // N_CHIPS
        dh1 = sum([_dot_nt(dproj[:, j * wb:(j + 1) * wb], w_ref[j]) for j in range(1, N_CHIPS)],
                  _dot_nt(dproj[:, :wb], w_ref[0]))
        gx_ref[...] = dxa_ref[...].astype(F32) + dh1 * (1.0 + mod_ref[1:2, :])
        acc_ref[0:1, :] += _colsum(dh1 * x_ref[...].astype(F32))
        acc_ref[1:2, :] += _colsum(dh1)
        db_ref[0:1, :] += _colsum(dproj.astype(F32))

    row = lambda w: pl.BlockSpec((tm, w), lambda i: (i, 0))
    return pl.pallas_call(
        body, name="bwd_in",
        out_shape=(jax.ShapeDtypeStruct((s, D_MODEL), F32), jax.ShapeDtypeStruct((8, D_MODEL), F32),
                   jax.ShapeDtypeStruct((8, IN_W), F32)),
        grid=(s // tm,),
        in_specs=[row(ATTN_W), row(2 * KV_W), row(GMLP_W), row(GMLP_W), row(D_MODEL), row(D_MODEL),
                  _const_spec((8, D_MODEL)), _const_spec(w_in.shape)],
        out_specs=(row(D_MODEL), _const_spec((8, D_MODEL)), _const_spec((8, IN_W))),
        compiler_params=_params(("arbitrary",)),
    )(dq, dkv, dgu, dgv, dxa, x, modr, w_in)


def _wgrad(a, bs, tm, tk, name, owner_blocks=False, gather_vs=()):
    k_all, m = a.shape
    n = sum(b.shape[1] for b in bs)
    nk = k_all // tk
    nm = m // tm
    n_b = len(bs)
    n_v = len(gather_vs)
    wb = n // N_CHIPS

    def body(a_ref, *rest):
        b_refs, v_refs = rest[:n_b], rest[n_b:n_b + n_v]
        o_ref, ob_ref = rest[n_b + n_v:n_b + n_v + 2]
        vg_refs = rest[n_b + n_v + 2:n_b + 2 * n_v + 2]
        i, k = pl.program_id(0), pl.program_id(1)
        if n_v:
            gather = _Gather8(v_refs, vg_refs, *rest[n_b + 2 * n_v + 2:])

            @pl.when((i == 0) & (k == 0))
            def _():
                gather.start()

            @pl.when((i == nm - 1) & (k == 0))
            def _():
                gather.forward()

        @pl.when(k == 0)
        def _():
            o_ref[...] = jnp.zeros_like(o_ref)

        b = b_refs[0][...] if n_b == 1 else jnp.concatenate([r[...] for r in b_refs], axis=1)
        if owner_blocks:
            av = a_ref[...]
            for j in range(N_CHIPS):
                o_ref[j] += _dot_tn(av, b[:, j * wb:(j + 1) * wb])
        else:
            o_ref[...] += _dot_tn(a_ref[...], b)

        @pl.when(k == nk - 1)
        def _():
            ob_ref[...] = o_ref[...].astype(BF16)

        if n_v:
            @pl.when((i == nm - 1) & (k == nk - 1))
            def _():
                gather.finish()

    if owner_blocks:
        out_spec = pl.BlockSpec((N_CHIPS, tm, wb), lambda i, k: (0, i, 0))
        shape = (N_CHIPS, m, wb)
    else:
        out_spec = pl.BlockSpec((tm, n), lambda i, k: (i, 0))
        shape = (m, n)
    outs = pl.pallas_call(
        body, name=name,
        out_shape=[jax.ShapeDtypeStruct(shape, F32), jax.ShapeDtypeStruct(shape, BF16)] + _gathered8_shapes(gather_vs),
        grid=(nm, nk),
        in_specs=[pl.BlockSpec((tk, tm), lambda i, k: (k, i))]
        + [pl.BlockSpec((tk, b.shape[1]), lambda i, k: (k, 0)) for b in bs] + [ANY] * n_v,
        out_specs=[out_spec, out_spec] + [ANY] * n_v,
        scratch_shapes=_Gather8.sems(n_v) if n_v else [],
        compiler_params=_params(("arbitrary", "arbitrary") if n_v else ("parallel", "arbitrary")),
    )(a, *bs, *gather_vs)
    return outs[0], outs[1], outs[2:]


def _adam_math(w, g, m, v):
    m2 = ADAM_B1 * m + (1.0 - ADAM_B1) * g
    v2 = ADAM_B2 * v + (1.0 - ADAM_B2) * (g * g)
    m_hat = m2 / (1.0 - ADAM_B1 ** ADAM_STEP)
    v_hat = v2 / (1.0 - ADAM_B2 ** ADAM_STEP)
    delta = -ADAM_LR * (m_hat / (jnp.sqrt(v_hat) + ADAM_EPS) + ADAM_WD * w)
    return delta, m2, v2


def _adam_halves(w, mine, got, m, v, tr, name):
    r, cc = w.shape
    h = r // 2
    nt = h // tr

    def body(c_ref, w_ref, mine_ref, got_ref, m_ref, v_ref, g_ref, d_ref, m2_ref, v2_ref):
        g = jnp.where(pl.program_id(0) == c_ref[0], mine_ref[...], got_ref[...])
        g_ref[...] = g
        d, m2, v2 = _adam_math(w_ref[...], g, m_ref[...], v_ref[...])
        d_ref[...] = d
        m2_ref[...] = m2
        v2_ref[...] = v2

    full = pl.BlockSpec((tr, cc), lambda hh, i, c_ref: (hh * nt + i, 0))
    half = pl.BlockSpec((tr, cc), lambda hh, i, c_ref: (i, 0))
    shp = jax.ShapeDtypeStruct((r, cc), F32)
    return pl.pallas_call(
        body, name=name, out_shape=(shp, shp, shp, shp),
        grid_spec=pltpu.PrefetchScalarGridSpec(
            num_scalar_prefetch=1, grid=(2, nt), in_specs=[full, half, half, full, full],
            out_specs=(full, full, full, full)),
        compiler_params=_params(("arbitrary", "arbitrary")),
    )(_core_index_scalar(), w, mine, got, m, v)


def _adam_w_ada(sc_t, dmod_cols, w, m, v, tr):
    r, cc = w.shape

    def body(sct_ref, dm_ref, w_ref, m_ref, v_ref, g_ref, d_ref, m2_ref, v2_ref):
        g = sct_ref[:, 0:1] * dm_ref[0:1, :]
        for k in range(1, N_DEV):
            g = g + sct_ref[:, k:k + 1] * dm_ref[k:k + 1, :]
        g_ref[...] = g
        d, m2, v2 = _adam_math(w_ref[...], g, m_ref[...], v_ref[...])
        d_ref[...] = d
        m2_ref[...] = m2
        v2_ref[...] = v2

    spec = pl.BlockSpec((tr, cc), lambda i: (i, 0))
    shp = jax.ShapeDtypeStruct((r, cc), F32)
    return pl.pallas_call(
        body, name="adam_w_ada", out_shape=(shp, shp, shp, shp), grid=(r // tr,),
        in_specs=[pl.BlockSpec((tr, N_DEV), lambda i: (i, 0)), _const_spec((N_DEV, cc)), spec, spec, spec],
        out_specs=(spec, spec, spec, spec), compiler_params=_params(("parallel",)),
    )(sc_t, dmod_cols, w, m, v)


def _pack_wide(acc_i, acc_m, acc_f, db_in, vec):
    arrs = [acc_i, acc_m, acc_f, db_in, vec]
    i_, m_, f_, b_, v_ = range(5)
    src = {"b_in": (b_, 0), "ln1_g": (m_, 2), "ln1_b": (m_, 3), "ln2_g": (f_, 1), "ln2_b": (f_, 2),
           "gmlp_ln_g": (v_, 2), "gmlp_ln_b": (v_, 3), "attn_out_g": (v_, 0), "gmlp_out_g": (v_, 1), "loss": (f_, 0)}
    dmod = [(i_, 1), (i_, 0), (m_, 4), (m_, 1), (m_, 0), (f_, 3)]

    def body(*refs):
        ins, wide_ref = refs[:5], refs[5]
        wide_ref[...] = jnp.zeros_like(wide_ref)
        for k, (a, row) in enumerate(dmod):
            wide_ref[0:1, k * D_MODEL:(k + 1) * D_MODEL] = ins[a][row:row + 1, :]
        for name, (a, row) in src.items():
            r, off, n = WIDE_LAYOUT[name]
            wide_ref[r:r + 1, off:off + n] = ins[a][row:row + 1, :]

    return pl.pallas_call(
        body, name="pack_wide", out_shape=jax.ShapeDtypeStruct((8, WIDE_W), F32), grid=(1,),
        in_specs=[_const_spec(a.shape) for a in arrs], out_specs=_const_spec((8, WIDE_W)),
        compiler_params=_params(("arbitrary",)),
    )(*arrs)


def _adam_small(gw, gt, wide_wmv, w_s, b_s, rel_bias, sinks):
    names = list(WIDE_PARAMS)
    tall = [("gmlp_w_s", w_s), ("gmlp_b_s", b_s), ("rel_bias", rel_bias), ("attn_sinks", sinks)]
    ins = [gw, gt]
    for n in names:
        ins += list(wide_wmv[n])
    for _, t in tall:
        ins += list(t)
    n_in = len(ins)

    def body(*refs):
        gw_ref, gt_ref = refs[0], refs[1]
        wmv = refs[2:n_in]
        dmod_ref, loss_ref = refs[n_in], refs[n_in + 1]
        outs = refs[n_in + 2:]

        def tall_sum(r0, nr):
            g = gt_ref[r0:r0 + nr, :]
            for d in range(1, N_DEV):
                g = g + gt_ref[d * TALL_ROWS + r0:d * TALL_ROWS + r0 + nr, :]
            return g

        def emit(k, g, w_ref, m_ref, v_ref):
            d, m2, v2 = _adam_math(w_ref[...], g, m_ref[...], v_ref[...])
            outs[4 * k][...] = g
            outs[4 * k + 1][...] = d
            outs[4 * k + 2][...] = m2
            outs[4 * k + 3][...] = v2

        gsum = gw_ref[0:8, :]
        for d in range(1, N_DEV):
            gsum = gsum + gw_ref[8 * d:8 * d + 8, :]
        for d in range(N_DEV):
            dmod_ref[d:d + 1, :] = gw_ref[8 * d:8 * d + 1, :]
        for k, n in enumerate(names):
            r, off, sz = WIDE_LAYOUT[n]
            emit(k, gsum[r:r + 1, off:off + sz], *wmv[3 * k:3 * k + 3])
        r, off, sz = WIDE_LAYOUT["loss"]
        tot = jnp.sum(gsum[r:r + 1, off:off + sz], axis=1, keepdims=True)
        loss_ref[...] = jnp.broadcast_to(tot * (0.5 / D_MODEL), loss_ref.shape)

        k0 = len(names)
        ws_refs = wmv[3 * k0:3 * k0 + 3]
        for g in range(N_GROUPS):
            rows = slice(g * BLOCK, (g + 1) * BLOCK)
            gg = tall_sum(g * BLOCK, BLOCK)
            d, m2, v2 = _adam_math(ws_refs[0][rows, :], gg, ws_refs[1][rows, :], ws_refs[2][rows, :])
            outs[4 * k0][rows, :] = gg
            outs[4 * k0 + 1][rows, :] = d
            outs[4 * k0 + 2][rows, :] = m2
            outs[4 * k0 + 3][rows, :] = v2
        emit(k0 + 1, tall_sum(TALL_BS, N_GROUPS), *wmv[3 * (k0 + 1):3 * (k0 + 1) + 3])
        emit(k0 + 2, tall_sum(TALL_RB, N_BUCKETS)[:, :N_HEADS], *wmv[3 * (k0 + 2):3 * (k0 + 2) + 3])
        emit(k0 + 3, tall_sum(TALL_SK, 8)[0:1, :N_HEADS], *wmv[3 * (k0 + 3):3 * (k0 + 3) + 3])

    out_shapes = [jax.ShapeDtypeStruct((N_DEV, WIDE_W), F32), jax.ShapeDtypeStruct((8, LANES), F32)]
    for n in names:
        out_shapes += [jax.ShapeDtypeStruct(wide_wmv[n][0].shape, F32)] * 4
    for _, t in tall:
        out_shapes += [jax.ShapeDtypeStruct(t[0].shape, F32)] * 4
    res = pl.pallas_call(
        body, name="adam_small", out_shape=out_shapes, grid=(1,),
        in_specs=[_const_spec(a.shape) for a in ins], out_specs=[_const_spec(o.shape) for o in out_shapes],
        compiler_params=_params(("arbitrary",)),
    )(*ins)
    out = {}
    for k, n in enumerate(names + [t[0] for t in tall]):
        out[n] = tuple(res[2 + 4 * k:6 + 4 * k])
    return res[0], res[1], out


def kernel(x, c, rel_bias, w_ada, b_ada, w_in, b_in, attn_sinks, gmlp_ln_g, gmlp_ln_b, gmlp_w_s, gmlp_b_s, attn_out_g, gmlp_out_g, w_out, ln1_g, ln1_b, w_gate_up, w_down, ln2_g, ln2_b, loss_target, m_rel_bias, m_w_ada, m_b_ada, m_w_in, m_b_in, m_attn_sinks, m_gmlp_ln_g, m_gmlp_ln_b, m_gmlp_w_s, m_gmlp_b_s, m_attn_out_g, m_gmlp_out_g, m_w_out, m_ln1_g, m_ln1_b, m_w_gate_up, m_w_down, m_ln2_g, m_ln2_b, v_rel_bias, v_w_ada, v_b_ada, v_w_in, v_b_in, v_attn_sinks, v_gmlp_ln_g, v_gmlp_ln_b, v_gmlp_w_s, v_gmlp_b_s, v_attn_out_g, v_gmlp_out_g, v_w_out, v_ln1_g, v_ln1_b, v_w_gate_up, v_w_down, v_ln2_g, v_ln2_b):
    ix, iy, ic = _my_pos()
    chip = 2 * ix + iy
    dev = 4 * ix + 2 * iy + ic
    s = x.shape[1]
    xs = x[0]
    tgt = loss_target[0]
    tm_big = min(512, s)
    tm_ffn = min(FFN_SUB, s)
    n_ada = w_ada.shape[2]

    w_in_s, w_out_s = w_in[0].astype(BF16), w_out[0].astype(BF16)
    w_gu_s, w_dn_s = w_gate_up[0].astype(BF16), w_down[0].astype(BF16)
    sc_all, mod_rows, (w_in_g, w_out_g) = _prologue(
        jnp.pad(c, ((0, 7), (0, 0))), w_ada[0], lax.dynamic_slice_in_dim(b_ada, chip * n_ada, n_ada, axis=1),
        [w_in_s, w_out_s])
    mod_all = mod_rows.reshape(N_DEV, N_DEV, -1)
    mod_row = lax.dynamic_index_in_dim(mod_all[0::2], dev, axis=1, keepdims=False)
    modr = jnp.pad(mod_row.reshape(6, D_MODEL), ((0, 2), (0, 0)))
    w_in_f = _insert_own(w_in_g, w_in_s, "blk", chip)

    bucket = _bucket_table()
    bias, wsm = _prep_tables(bucket, rel_bias, gmlp_w_s[0])
    bsx = jnp.repeat(gmlp_b_s[0].T, GROUP_DIM, axis=1)
    amat = _group_mean_matrix()
    sinks = attn_sinks[0]

    (h1, q, kv, gu, gv, xb), (w_dn_g,) = _fwd_in(xs, modr, w_in_f, b_in, tm_big, [w_dn_s], ["blk"])
    w_out_f = _insert_own(w_out_g, w_out_s, "blk", chip).reshape(D_MODEL, D_MODEL)
    (x1, x1b, y, mixed), (w_gu_g,) = _fwd_mix(
        q, kv, gu, gv, xs, modr, bias, sinks, gmlp_ln_g, gmlp_ln_b, wsm, bsx, amat, attn_out_g, gmlp_out_g, w_out_f,
        ln1_g, ln1_b, tm_big, [w_gu_s], ["blk"])
    assert w_gate_up.shape[2] == FF_CHUNK
    w_gu_f = _insert_own(w_gu_g, w_gu_s, "blk", chip)
    w_dn_f = _insert_own(w_dn_g, w_dn_s, "blk", chip).reshape(D_FF, D_MODEL)
    h2, act, dy2, dx1a, acc_f = _fwd_ffn(x1, tgt, modr, ln2_g, ln2_b, w_gu_f, w_dn_f, tm_ffn)

    a_act, dgu_ff, dh2 = _bwd_ffn(dy2, act, w_gu_f, w_dn_f, min(FFN_SUB, s))
    g_dn, g_dn_b, _ = _wgrad(a_act, [dy2], D_FF // 2, min(1024, s), "wgrad_down")
    g_gu, g_gu_b, _ = _wgrad(h2, [dgu_ff], 512, min(512, s), "wgrad_gate_up")
    blk3 = lambda a, rows: a.reshape(N_CHIPS, rows, a.shape[1])
    (dxa, dy, dmix, acc_m), (got_dn, got_gu) = _bwd_mid(
        dh2, dx1a, x1b, xs, y, modr, ln1_g, w_out_f, tm_big, [blk3(g_dn_b, D_FF // N_CHIPS), g_gu_b], ["blk", "cols"])
    g_out, g_out_b, _ = _wgrad(mixed, [dy], 512, min(2048, s), "wgrad_out")
    (got_out,) = _swap_halves([blk3(g_out_b, D_MODEL // N_CHIPS)], ["blk"], "rs_swap_out")
    kinds_a = ["blk", "cols", "blk"]
    fulls_a = [blk3(g_dn, D_FF // N_CHIPS), g_gu, blk3(g_out, D_MODEL // N_CHIPS)]
    gots_a = [got_dn, got_gu, got_out]
    parts_a = [_add_halves(f, g, k, "rs_add_a%d" % i) for i, (f, g, k) in enumerate(zip(fulls_a, gots_a, kinds_a))]
    (dq, dkv, dgu, dgv, gbias, dws, dbs, vec, dsink), rxs_a = _bwd_mix(
        q, kv, gu, gv, dmix, bias, sinks, gmlp_ln_g, gmlp_ln_b, wsm, bsx, amat, attn_out_g, gmlp_out_g,
        [p[1] for p in parts_a], kinds_a)
    tall_g = _mix_finalize(gbias, bucket, dws, dbs, dsink)
    grad_x, acc_i, db_in = _bwd_in(dq, dkv, dgu, dgv, dxa, xb, modr, w_in_f, tm_big)

    wide_g = _pack_wide(acc_i, acc_m, acc_f, db_in, vec)
    full_in, full_in_b, (gw, gt) = _wgrad(h1, [dq, dkv, dgu, dgv], 512, min(1024, s), "wgrad_in", owner_blocks=True,
                                          gather_vs=[wide_g, tall_g])
    (got_in,) = _swap_halves([full_in_b], ["blk"], "rs_swap_in")
    part_in = _add_halves(full_in, got_in, "blk", "rs_add_in")
    (rx_in,) = _exchange_chip_partials([part_in[1]], ["blk"], "rs_chips_in")
    wide_wmv = {"b_ada": (b_ada, m_b_ada, v_b_ada), "b_in": (b_in, m_b_in, v_b_in),
                "ln1_g": (ln1_g, m_ln1_g, v_ln1_g), "ln1_b": (ln1_b, m_ln1_b, v_ln1_b),
                "ln2_g": (ln2_g, m_ln2_g, v_ln2_g), "ln2_b": (ln2_b, m_ln2_b, v_ln2_b),
                "gmlp_ln_g": (gmlp_ln_g, m_gmlp_ln_g, v_gmlp_ln_g), "gmlp_ln_b": (gmlp_ln_b, m_gmlp_ln_b, v_gmlp_ln_b),
                "attn_out_g": (attn_out_g, m_attn_out_g, v_attn_out_g),
                "gmlp_out_g": (gmlp_out_g, m_gmlp_out_g, v_gmlp_out_g)}
    rows2 = lambda a: a.reshape(-1, a.shape[-1])
    dmod_all, loss_t, small = _adam_small(
        gw, gt, wide_wmv, tuple(rows2(a) for a in (gmlp_w_s, m_gmlp_w_s, v_gmlp_w_s)),
        tuple(rows2(a) for a in (gmlp_b_s, m_gmlp_b_s, v_gmlp_b_s)), (rel_bias, m_rel_bias, v_rel_bias),
        (attn_sinks, m_attn_sinks, v_attn_sinks))
    loss = loss_t[0, 0]

    dmod_cols = lax.dynamic_slice_in_dim(dmod_all, chip * n_ada, n_ada, axis=1)
    g_ada, d_ada, m_ada, v_ada = _adam_w_ada(sc_all.T, dmod_cols, w_ada[0], m_w_ada[0], v_w_ada[0], 256)

    sums = [(parts_a[0][0], rxs_a[0], 176), (parts_a[1][0], rxs_a[1], 256), (parts_a[2][0], rxs_a[2], 128),
            (part_in[0], rx_in, 256)]
    mine = [_sum_chips(p, rx, tr, "rs_sum_%d" % i) for i, (p, rx, tr) in enumerate(sums)]
    got = _share_halves(mine, "rs_share")

    gs_dn, d_dn, m_dn, v_dn = _adam_halves(w_down[0], mine[0], got[0], m_w_down[0], v_w_down[0], 176, "adam_w_down")
    gs_gu, d_gu, m_gu, v_gu = _adam_halves(w_gate_up[0], mine[1], got[1], m_w_gate_up[0], v_w_gate_up[0], 256,
                                           "adam_w_gate_up")
    gs_out, d_out, m_out, v_out = _adam_halves(w_out[0], mine[2], got[2], m_w_out[0], v_w_out[0], 128, "adam_w_out")
    gs_in, d_in, m_in, v_in = _adam_halves(w_in[0], mine[3], got[3], m_w_in[0], v_w_in[0], 256, "adam_w_in")

    big = {"w_ada": (g_ada, d_ada, m_ada, v_ada), "w_in": (gs_in, d_in, m_in, v_in), "w_out": (gs_out, d_out, m_out, v_out),
           "w_gate_up": (gs_gu, d_gu, m_gu, v_gu), "w_down": (gs_dn, d_dn, m_dn, v_dn)}
    order = ["rel_bias", "w_ada", "b_ada", "w_in", "b_in", "attn_sinks", "gmlp_ln_g", "gmlp_ln_b", "gmlp_w_s", "gmlp_b_s",
             "attn_out_g", "gmlp_out_g", "w_out", "ln1_g", "ln1_b", "w_gate_up", "w_down", "ln2_g", "ln2_b"]
    shapes = {"gmlp_w_s": gmlp_w_s.shape, "gmlp_b_s": gmlp_b_s.shape}
    outs = [loss, grad_x[None]]
    for k in range(4):
        for name in order:
            if name in big:
                outs.append(big[name][k][None])
            elif name in shapes:
                outs.append(small[name][k].reshape(shapes[name]))
            else:
                outs.append(small[name][k])
    return tuple(outs)
```

```python
import math

import numpy as np
import jax
import jax.numpy as jnp
from jax import lax
from jax.experimental import pallas as pl
from jax.experimental.pallas import tpu as pltpu

F32 = jnp.float32
BF16 = jnp.bfloat16
MESH = pl.DeviceIdType.MESH

D_MODEL = 1024
N_HEADS = 8
N_KV = 2
HEAD_DIM = 64
ATTN_W = N_HEADS * HEAD_DIM
KV_W = N_KV * HEAD_DIM
N_GROUPS = 8
GROUP_DIM = 64
GMLP_W = N_GROUPS * GROUP_DIM
IN_W = ATTN_W + 2 * KV_W + 2 * GMLP_W
BLOCK = 128
N_BUCKETS = 32
MAX_DISTANCE = 128
D_FF = 2816
ALPHA = 2.0 ** 0.25
LN_EPS = 1e-5
NEG_INF = -1e30
ADAM_LR, ADAM_B1, ADAM_B2, ADAM_EPS, ADAM_WD, ADAM_STEP = 0.001, 0.9, 0.999, 1e-8, 0.01, 10
N_CHIPS = 4
N_DEV = 8
LANES = 128
V7X_VMEM_LIMIT = 56 * 2 ** 20
GELU_C = math.sqrt(2.0 / math.pi)
Q_SCALE = HEAD_DIM ** -0.5
ANY = pl.BlockSpec(memory_space=pl.ANY)

TALL_BS = N_GROUPS * BLOCK
TALL_RB = TALL_BS + 8
TALL_SK = TALL_RB + N_BUCKETS
TALL_ROWS = TALL_SK + 8
WIDE_W = 6 * D_MODEL
WIDE_LAYOUT = {
    "b_ada": (0, 0, 6 * D_MODEL),
    "b_in": (1, 0, IN_W), "ln1_g": (1, IN_W, D_MODEL), "ln1_b": (1, IN_W + D_MODEL, D_MODEL),
    "ln2_g": (1, IN_W + 2 * D_MODEL, D_MODEL), "ln2_b": (1, IN_W + 3 * D_MODEL, D_MODEL),
    "gmlp_ln_g": (2, 0, GMLP_W), "gmlp_ln_b": (2, GMLP_W, GMLP_W), "attn_out_g": (2, 2 * GMLP_W, ATTN_W),
    "gmlp_out_g": (2, 2 * GMLP_W + ATTN_W, GMLP_W), "loss": (2, 3 * GMLP_W + ATTN_W, D_MODEL)}
WIDE_PARAMS = tuple(n for n in WIDE_LAYOUT if n != "loss")


def _params(sem=None):
    return pltpu.CompilerParams(dimension_semantics=sem, vmem_limit_bytes=V7X_VMEM_LIMIT)


def _const_spec(shape, single=False):
    nd = len(shape)
    if single:
        return pl.BlockSpec(shape, lambda *_: (0,) * nd, pipeline_mode=pl.Buffered(1))
    return pl.BlockSpec(shape, lambda *_: (0,) * nd)


def _dot(a, b):
    return jnp.dot(a, b, preferred_element_type=F32)


def _dot_nt(a, b):
    return lax.dot_general(a, b, (((1,), (1,)), ((), ())), preferred_element_type=F32)


def _dot_tn(a, b):
    return lax.dot_general(a, b, (((0,), (0,)), ((), ())), preferred_element_type=F32)


def _gelu(x):
    t = jnp.tanh(GELU_C * (x + 0.044715 * x * x * x))
    return 0.5 * x * (1.0 + t), t


def _gelu_grad(x, t):
    return 0.5 * (1.0 + t) + 0.5 * x * (1.0 - t * t) * GELU_C * (1.0 + 3.0 * 0.044715 * x * x)


def _split_dot(x, a):
    hi = x.astype(BF16)
    lo = (x - hi.astype(F32)).astype(BF16)
    return _dot(hi, a) + _dot(lo, a)


def _group_mean_matrix():
    g = np.arange(GMLP_W) // GROUP_DIM
    return jnp.asarray((g[:, None] == g[None, :]).astype(np.float32) / GROUP_DIM, dtype=BF16)


def _ln_stats(z):
    mu = jnp.mean(z, axis=-1, keepdims=True)
    d = z - mu
    var = jnp.mean(d * d, axis=-1, keepdims=True)
    rstd = lax.rsqrt(var + LN_EPS)
    return d * rstd, rstd


def _ln_bwd(dxhat, xhat, rstd):
    m1 = jnp.mean(dxhat, axis=-1, keepdims=True)
    m2 = jnp.mean(dxhat * xhat, axis=-1, keepdims=True)
    return rstd * (dxhat - m1 - xhat * m2)


def _colsum(x):
    return jnp.sum(x, axis=0, keepdims=True)


def _my_pos():
    return lax.axis_index("x"), lax.axis_index("y"), lax.axis_index("c")


def _other_chips(x, y):
    return [(1 - x, y), (x, 1 - y), (1 - x, 1 - y)]


def _chip_index_scalar():
    ix, iy, _ = _my_pos()
    return jnp.reshape(2 * ix + iy, (1,)).astype(jnp.int32)


def _core_index_scalar():
    return jnp.reshape(lax.axis_index("c"), (1,)).astype(jnp.int32)


class _Gather8:
    def __init__(self, x_refs, out_refs, send_sems, recv_sems, local_sems):
        self.x_refs, self.out_refs = x_refs, out_refs
        self.send_sems, self.recv_sems, self.local_sems = send_sems, recv_sems, local_sems
        self.x, self.y, self.c = _my_pos()
        self.me, self.sibling = (self.x, self.y, self.c), (self.x, self.y, 1 - self.c)
        self.chips = _other_chips(self.x, self.y)

    def _rows(self, a, px, py, pc):
        m_per = self.x_refs[a].shape[0]
        return self.out_refs[a].at[pl.ds((4 * px + 2 * py + pc) * m_per, m_per), :]

    def _copy(self, a, k, block, to, src=None):
        return pltpu.make_async_remote_copy(
            src_ref=self._rows(a, *block) if src is None else src, dst_ref=self._rows(a, *block),
            send_sem=self.send_sems.at[7 * a + k], recv_sem=self.recv_sems.at[7 * a + k], device_id=to,
            device_id_type=MESH)

    def _local(self, a):
        return pltpu.make_async_copy(self.x_refs[a], self._rows(a, *self.me), self.local_sems.at[a])

    def start(self):
        for a in range(len(self.x_refs)):
            self._local(a).start()
            self._copy(a, 0, self.me, self.sibling, src=self.x_refs[a]).start()
            for j, chip in enumerate(self.chips):
                self._copy(a, 1 + j, self.me, (*chip, self.c), src=self.x_refs[a]).start()

    def forward(self):
        for a in range(len(self.x_refs)):
            for j, chip in enumerate(self.chips):
                self._copy(a, 1 + j, (*chip, self.c), self.me).wait_recv()
                self._copy(a, 4 + j, (*chip, self.c), self.sibling).start()

    def finish(self):
        for a in range(len(self.x_refs)):
            self._copy(a, 0, self.sibling, self.me).wait_recv()
            for j, chip in enumerate(self.chips):
                self._copy(a, 4 + j, (*chip, 1 - self.c), self.me).wait_recv()
        for a in range(len(self.x_refs)):
            for k in range(7):
                self._copy(a, k, self.me, self.me).wait_send()
            self._local(a).wait()

    @staticmethod
    def sems(n_v):
        return [pltpu.SemaphoreType.DMA((7 * n_v,)), pltpu.SemaphoreType.DMA((7 * n_v,)),
                pltpu.SemaphoreType.DMA((n_v,))]


def _gathered8_shapes(vs):
    return [jax.ShapeDtypeStruct((N_DEV * v.shape[0], v.shape[1]), v.dtype) for v in vs]


VMEM_WHOLE = pl.BlockSpec(memory_space=pltpu.VMEM)


def _prologue(c_pad, w_ada_s, b_ada_s, shards):
    n = w_ada_s.shape[1]
    n_w = len(shards)

    def body(c_ref, w_ref, b_ref, *rest):
        shard_refs = rest[:n_w]
        sc_ref, modc_ref, modg_ref = rest[n_w:n_w + 3]
        gathered_refs = rest[n_w + 3:2 * n_w + 3]
        call_ref, w_vmem = rest[2 * n_w + 3:2 * n_w + 5]
        sems = rest[2 * n_w + 5:]
        weights = _WeightGather(shard_refs, gathered_refs, ["blk"] * n_w, sems[0], sems[1])
        gather_c = _Gather8([c_ref], [call_ref], sems[2], sems[3], sems[4])
        gather_mod = _Gather8([modc_ref], [modg_ref], sems[5], sems[6], sems[7])
        load_w = pltpu.make_async_copy(w_ref, w_vmem, sems[8])
        weights.start()
        gather_c.start()
        load_w.start()
        gather_c.forward()
        gather_c.finish()
        cv = call_ref[...]
        sc = cv * _sigmoid(cv)
        a_hi = sc.astype(BF16)
        a_lo = (sc - a_hi.astype(F32)).astype(BF16)
        load_w.wait()
        w = w_vmem[...]
        w_hi = w.astype(BF16)
        w_lo = (w - w_hi.astype(F32)).astype(BF16)
        mod = _dot(a_hi, w_hi) + _dot(a_hi, w_lo) + _dot(a_lo, w_hi) + b_ref[...]
        for d in range(N_DEV):
            sc_ref[d:d + 1, :] = sc[8 * d:8 * d + 1, :]
            modc_ref[d:d + 1, :] = mod[8 * d:8 * d + 1, :]
        gather_mod.start()
        weights.forward()
        gather_mod.forward()
        gather_mod.finish()
        weights.forward_diagonal()
        weights.finish()

    outs = pl.pallas_call(
        body, name="prologue",
        out_shape=[jax.ShapeDtypeStruct((N_DEV, D_MODEL), F32), jax.ShapeDtypeStruct((N_DEV, n), F32),
                   jax.ShapeDtypeStruct((N_DEV * N_DEV, n), F32)]
        + [jax.ShapeDtypeStruct(_gathered_shape(sh, "blk"), BF16) for sh in shards],
        in_specs=[VMEM_WHOLE, ANY, VMEM_WHOLE] + [ANY] * n_w,
        out_specs=[VMEM_WHOLE, VMEM_WHOLE, VMEM_WHOLE] + [ANY] * n_w,
        scratch_shapes=[pltpu.VMEM((N_DEV * 8, D_MODEL), F32), pltpu.VMEM(w_ada_s.shape, F32)]
        + _WeightGather.sems(n_w) + _Gather8.sems(1) + _Gather8.sems(1) + [pltpu.SemaphoreType.DMA],
        compiler_params=pltpu.CompilerParams(vmem_limit_bytes=V7X_VMEM_LIMIT),
    )(c_pad, w_ada_s, b_ada_s, *shards)
    return outs[0], outs[2], outs[3:]


def _gathered_shape(shard, kind):
    r, cc = shard.shape
    return (N_CHIPS, r, cc) if kind == "blk" else (r, N_CHIPS * cc)


class _WeightGather:
    N_SEM = 8

    def __init__(self, shards, gathered, kinds, send_sems, recv_sems):
        self.shards, self.gathered, self.kinds = shards, gathered, kinds
        self.send_sems, self.recv_sems = send_sems, recv_sems
        self.x, self.y, self.c = _my_pos()
        self.me, self.sibling = (self.x, self.y, self.c), (self.x, self.y, 1 - self.c)
        self.nbr = ((1 - self.x, self.y), (self.x, 1 - self.y))
        self.diag = 2 * (1 - self.x) + (1 - self.y)

    def _dst(self, a, chip, pc, quarter=None):
        r, cc = self.shards[a].shape
        h = r // 2
        row0, rows = pc * h, h
        if quarter is not None:
            row0, rows = pc * h + quarter * (h // 2), h // 2
        g = self.gathered[a]
        if self.kinds[a] == "blk":
            return g.at[chip, pl.ds(row0, rows), :]
        return g.at[pl.ds(row0, rows), pl.ds(chip * cc, cc)]

    def _copy(self, a, k, region, to, src=None):
        return pltpu.make_async_remote_copy(
            src_ref=region if src is None else src, dst_ref=region, send_sem=self.send_sems.at[a * self.N_SEM + k],
            recv_sem=self.recv_sems.at[a * self.N_SEM + k], device_id=to, device_id_type=MESH)

    def _arrays(self):
        return range(len(self.shards))

    def start(self):
        my_chip = 2 * self.x + self.y
        for a in self._arrays():
            h = self.shards[a].shape[0] // 2
            mine = self.shards[a].at[pl.ds(self.c * h, h), :]
            for j, chip in enumerate(self.nbr):
                self._copy(a, j, self._dst(a, my_chip, self.c), (*chip, self.c), src=mine).start()

    def forward(self):
        for a in self._arrays():
            for j, chip in enumerate(self.nbr):
                cj = 2 * chip[0] + chip[1]
                half = self._dst(a, cj, self.c)
                self._copy(a, j, half, self.me).wait_recv()
                self._copy(a, 2 + j, half, self.sibling).start()
                other = self.nbr[1 - j]
                self._copy(a, 4 + j, self._dst(a, cj, self.c, quarter=j), (*other, self.c)).start()

    def forward_diagonal(self):
        for a in self._arrays():
            for j in range(2):
                quarter = self._dst(a, self.diag, self.c, quarter=j)
                self._copy(a, 4 + j, quarter, self.me).wait_recv()
                self._copy(a, 6 + j, quarter, self.sibling).start()

    def finish(self):
        for a in self._arrays():
            for j, chip in enumerate(self.nbr):
                self._copy(a, 2 + j, self._dst(a, 2 * chip[0] + chip[1], 1 - self.c), self.me).wait_recv()
                self._copy(a, 6 + j, self._dst(a, self.diag, 1 - self.c, quarter=j), self.me).wait_recv()
        for a in self._arrays():
            half = self._dst(a, self.diag, self.c)
            quarter = self._dst(a, self.diag, self.c, quarter=0)
            for k in range(self.N_SEM):
                self._copy(a, k, half if k < 4 else quarter, self.me).wait_send()

    @classmethod
    def sems(cls, n_arr):
        return [pltpu.SemaphoreType.DMA((n_arr * cls.N_SEM,)), pltpu.SemaphoreType.DMA((n_arr * cls.N_SEM,))]


def _insert_own(gathered, shard, kind, chip):
    if kind == "blk":
        return lax.dynamic_update_slice(gathered, shard[None], (chip, 0, 0))
    return lax.dynamic_update_slice(gathered, shard, (0, chip * shard.shape[1]))


def _half_of_full(ref, kind, pc):
    if kind == "blk":
        h = ref.shape[1] // 2
        return ref.at[:, pl.ds(pc * h, h), :]
    h = ref.shape[0] // 2
    return ref.at[pl.ds(pc * h, h), :]


def _half_shape(shape, kind):
    return (shape[0], shape[1] // 2, shape[2]) if kind == "blk" else (shape[0] // 2, shape[1])


class _HalfSwap:
    def __init__(self, ins, outs, kinds, send_sems, recv_sems):
        self.ins, self.outs, self.kinds = ins, outs, kinds
        self.send_sems, self.recv_sems = send_sems, recv_sems
        self.x, self.y, self.c = _my_pos()

    def _copies(self):
        for a in range(len(self.ins)):
            yield pltpu.make_async_remote_copy(
                src_ref=_half_of_full(self.ins[a], self.kinds[a], 1 - self.c), dst_ref=self.outs[a],
                send_sem=self.send_sems.at[a], recv_sem=self.recv_sems.at[a],
                device_id=(self.x, self.y, 1 - self.c), device_id_type=MESH)

    def start(self):
        for cp in self._copies():
            cp.start()

    def wait(self):
        for cp in self._copies():
            cp.wait()

    @staticmethod
    def sems(n_arr):
        return [pltpu.SemaphoreType.DMA((n_arr,)), pltpu.SemaphoreType.DMA((n_arr,))]

    @staticmethod
    def out_shapes(fulls, kinds):
        return [jax.ShapeDtypeStruct(_half_shape(a.shape, k), a.dtype) for a, k in zip(fulls, kinds)]


def _swap_halves(fulls_bf16, kinds, name):
    n_arr = len(fulls_bf16)

    def body(*refs):
        swap = _HalfSwap(refs[:n_arr], refs[n_arr:2 * n_arr], kinds, *refs[2 * n_arr:])
        swap.start()
        swap.wait()

    return pl.pallas_call(
        body, name=name, out_shape=_HalfSwap.out_shapes(fulls_bf16, kinds),
        in_specs=[ANY] * n_arr, out_specs=[ANY] * n_arr, scratch_shapes=_HalfSwap.sems(n_arr),
    )(*fulls_bf16)


def _add_halves(full, got, kind, name):
    hs = _half_shape(full.shape, kind)

    def body(pos_ref, a_ref, b_ref, o_ref, ob_ref):
        p = a_ref[...] + b_ref[...].astype(F32)
        ob_ref[...] = p.astype(BF16)

        @pl.when(pl.program_id(0) == pos_ref[1])
        def _():
            o_ref[...] = p.reshape(o_ref.shape)

    if kind == "blk":
        nb, h, cc = hs
        own = pl.BlockSpec((1, h, cc), lambda b, pos_ref: (b, pos_ref[0], 0))
        other = pl.BlockSpec((1, h, cc), lambda b, pos_ref: (b, 0, 0))
    else:
        h, cc = hs[0], hs[1] // N_CHIPS
        own = pl.BlockSpec((h, cc), lambda b, pos_ref: (pos_ref[0], b))
        other = pl.BlockSpec((h, cc), lambda b, pos_ref: (0, b))
    pos = jnp.concatenate([_core_index_scalar(), _chip_index_scalar()])
    return pl.pallas_call(
        body, name=name, out_shape=(jax.ShapeDtypeStruct((h, cc), F32), jax.ShapeDtypeStruct(hs, BF16)),
        grid_spec=pltpu.PrefetchScalarGridSpec(
            num_scalar_prefetch=1, grid=(N_CHIPS,), in_specs=[own, other],
            out_specs=(pl.BlockSpec((h, cc), lambda b, pos_ref: (0, 0)), other)),
        compiler_params=_params(("arbitrary",)),
    )(pos, full, got)


def _rx_shape(part_shape, kind):
    if kind == "blk":
        return (3, part_shape[1], part_shape[2])
    return (3, part_shape[0], part_shape[1] // N_CHIPS)


class _ChipExchange:
    def __init__(self, parts, rxs, kinds, send_sems, recv_sems):
        self.parts, self.rxs, self.kinds = parts, rxs, kinds
        self.send_sems, self.recv_sems = send_sems, recv_sems
        self.x, self.y, self.c = _my_pos()
        self.chips = _other_chips(self.x, self.y)

    def _copies(self):
        for a in range(len(self.parts)):
            for j, chip in enumerate(self.chips):
                cj = 2 * chip[0] + chip[1]
                if self.kinds[a] == "blk":
                    src = self.parts[a].at[cj]
                else:
                    cc = self.parts[a].shape[1] // N_CHIPS
                    src = self.parts[a].at[:, pl.ds(cj * cc, cc)]
                yield pltpu.make_async_remote_copy(
                    src_ref=src, dst_ref=self.rxs[a].at[j], send_sem=self.send_sems.at[a * 3 + j],
                    recv_sem=self.recv_sems.at[a * 3 + j], device_id=(*chip, self.c), device_id_type=MESH)

    def start(self):
        for cp in self._copies():
            cp.start()

    def wait(self):
        for cp in self._copies():
            cp.wait_recv()
        for cp in self._copies():
            cp.wait_send()

    @staticmethod
    def sems(n_arr):
        return [pltpu.SemaphoreType.DMA((n_arr * 3,)), pltpu.SemaphoreType.DMA((n_arr * 3,))]


HBM_SPEC = pl.BlockSpec(memory_space=pltpu.HBM)
SEM_SPEC = pl.BlockSpec(memory_space=pltpu.SEMAPHORE)
DATAFLOW = pltpu.SideEffectType.DATAFLOW_SIDE_EFFECTING


def _exchange_start(part, name):
    rx_shape = _rx_shape(part.shape, "blk")

    def body(part_ref, rx_ref, send_sems, recv_sems, part_thru, rx_thru, token):
        _ChipExchange([part_ref], [rx_ref], ["blk"], send_sems, recv_sems).start()
        token[...] = jnp.zeros_like(token)

    return pl.pallas_call(
        body, name=name,
        out_shape=(pltpu.SemaphoreType.DMA((3,)), pltpu.SemaphoreType.DMA((3,)), pltpu.HBM(part.shape, part.dtype),
                   pltpu.HBM(rx_shape, BF16), jax.ShapeDtypeStruct((8, LANES), F32)),
        in_specs=(HBM_SPEC, HBM_SPEC), out_specs=(SEM_SPEC, SEM_SPEC, HBM_SPEC, HBM_SPEC, VMEM_WHOLE),
        input_output_aliases={0: 2, 1: 3}, compiler_params=pltpu.CompilerParams(has_side_effects=DATAFLOW),
    )(pltpu.with_memory_space_constraint(part, pltpu.HBM),
      pltpu.with_memory_space_constraint(lax.empty(rx_shape, BF16), pltpu.HBM))


def _exchange_wait(send_sems, recv_sems, part_thru, rx_thru, after, name):
    def body(part_ref, rx_ref, send_sems, recv_sems, after_ref, part_dead, rx_out):
        _ChipExchange([part_ref], [rx_ref], ["blk"], send_sems, recv_sems).wait()

    return pl.pallas_call(
        body, name=name,
        out_shape=(pltpu.HBM(part_thru.shape, part_thru.dtype), pltpu.HBM(rx_thru.shape, rx_thru.dtype)),
        in_specs=(HBM_SPEC, HBM_SPEC, SEM_SPEC, SEM_SPEC, ANY), out_specs=(HBM_SPEC, HBM_SPEC),
        input_output_aliases={0: 0, 1: 1}, compiler_params=pltpu.CompilerParams(has_side_effects=DATAFLOW),
    )(part_thru, rx_thru, send_sems, recv_sems, after)[1]


def _sum_chips(part, rx, tr, name, after):
    _, h, cc = rx.shape
    flips = (2, 1, 3)

    def body(chip_ref, p_ref, rx_ref, after_ref, o_ref):
        own = p_ref[...]
        for mc in range(N_CHIPS):
            @pl.when(chip_ref[0] == mc)
            def _():
                terms = sorted([(mc, None)] + [(mc ^ f, j) for j, f in enumerate(flips)])
                acc = None
                for _, j in terms:
                    t = own if j is None else rx_ref[j].astype(F32)
                    acc = t if acc is None else acc + t
                o_ref[...] = acc

    return pl.pallas_call(
        body, name=name, out_shape=jax.ShapeDtypeStruct((h, cc), F32),
        grid_spec=pltpu.PrefetchScalarGridSpec(
            num_scalar_prefetch=1, grid=(h // tr,),
            in_specs=[pl.BlockSpec((tr, cc), lambda i, chip_ref: (i, 0)),
                      pl.BlockSpec((3, tr, cc), lambda i, chip_ref: (0, i, 0)), ANY],
            out_specs=pl.BlockSpec((tr, cc), lambda i, chip_ref: (i, 0))),
        compiler_params=_params(("arbitrary",)),
    )(_chip_index_scalar(), part, rx, after)


def _share_halves(halves, name):
    n_arr = len(halves)

    def body(*refs):
        ins, outs = refs[:n_arr], refs[n_arr:2 * n_arr]
        send_sems, recv_sems = refs[2 * n_arr:]
        x, y, c = _my_pos()
        cps = []
        for a in range(n_arr):
            cp = pltpu.make_async_remote_copy(
                src_ref=ins[a], dst_ref=outs[a], send_sem=send_sems.at[a], recv_sem=recv_sems.at[a],
                device_id=(x, y, 1 - c), device_id_type=MESH)
            cp.start()
            cps.append(cp)
        for cp in cps:
            cp.wait()

    return pl.pallas_call(
        body, name=name, out_shape=[jax.ShapeDtypeStruct(h.shape, h.dtype) for h in halves],
        in_specs=[ANY] * n_arr, out_specs=[ANY] * n_arr,
        scratch_shapes=[pltpu.SemaphoreType.DMA((n_arr,)), pltpu.SemaphoreType.DMA((n_arr,))],
    )(*halves)


def _bucket_table():
    qi = jnp.arange(BLOCK)[:, None]
    si = jnp.arange(2 * BLOCK)[None, :]
    dist = qi + BLOCK - si
    max_exact = N_BUCKETS // 2
    n = jnp.maximum(dist, 0)
    nf = jnp.maximum(n, max_exact).astype(F32)
    large = max_exact + (jnp.log(nf / max_exact) / math.log(MAX_DISTANCE / max_exact)
                         * (N_BUCKETS - max_exact)).astype(jnp.int32)
    large = jnp.minimum(large, N_BUCKETS - 1)
    return jnp.where(n < max_exact, n, large).astype(F32)


def _prep_tables(bucket, rel_bias, w_s):
    def body(bucket_ref, rb_ref, ws_ref, bias_ref, wsm_ref):
        qi = lax.broadcasted_iota(jnp.int32, (BLOCK, 2 * BLOCK), 0)
        si = lax.broadcasted_iota(jnp.int32, (BLOCK, 2 * BLOCK), 1)
        dist = qi + BLOCK - si
        in_window = (dist >= 0) & (dist < BLOCK)
        bk = bucket_ref[...]
        for h in range(N_HEADS):
            acc = jnp.zeros((BLOCK, 2 * BLOCK), F32)
            for b in range(N_BUCKETS):
                acc = jnp.where(bk == float(b), rb_ref[b, h], acc)
            bias_ref[h] = jnp.where(in_window, acc, NEG_INF)
        ti = lax.broadcasted_iota(jnp.int32, (BLOCK, BLOCK), 0)
        ui = lax.broadcasted_iota(jnp.int32, (BLOCK, BLOCK), 1)
        for g in range(N_GROUPS):
            wsm_ref[g] = jnp.where(ti >= ui, ws_ref[g], 0.0).astype(BF16)

    return pl.pallas_call(
        body, name="prep_tables",
        out_shape=(jax.ShapeDtypeStruct((N_HEADS, BLOCK, 2 * BLOCK), F32),
                   jax.ShapeDtypeStruct((N_GROUPS, BLOCK, BLOCK), BF16)),
        grid=(1,),
        in_specs=[_const_spec((BLOCK, 2 * BLOCK)), pl.BlockSpec(memory_space=pltpu.SMEM),
                  _const_spec((N_GROUPS, BLOCK, BLOCK))],
        out_specs=(_const_spec((N_HEADS, BLOCK, 2 * BLOCK)), _const_spec((N_GROUPS, BLOCK, BLOCK))),
        compiler_params=_params(("arbitrary",)),
    )(bucket, rel_bias, w_s)


def _fwd_in(x, modr, w_in, b_in, tm, shards, kinds):
    s = x.shape[0]
    n_steps = s // tm
    fwd_step, diag_step = (8 * n_steps) // 16, (13 * n_steps) // 16
    n_w = len(shards)

    def body(x_ref, mod_ref, w_ref, b_ref, *rest):
        shard_refs = rest[:n_w]
        h1_ref, q_ref, kv_ref, gu_ref, gv_ref, xb_ref = rest[n_w:n_w + 6]
        gathered_refs = rest[n_w + 6:2 * n_w + 6]
        send_sems, recv_sems = rest[2 * n_w + 6:]
        i = pl.program_id(0)
        gather = _WeightGather(shard_refs, gathered_refs, kinds, send_sems, recv_sems)

        @pl.when(i == 0)
        def _():
            gather.start()

        xv = x_ref[...]
        xb_ref[...] = xv.astype(BF16)
        h1 = (xv * (1.0 + mod_ref[1:2, :]) + mod_ref[0:1, :]).astype(BF16)
        h1_ref[...] = h1
        proj = jnp.concatenate([_dot(h1, w_ref[j]) for j in range(N_CHIPS)], axis=1) + b_ref[...]
        q_ref[...] = (proj[:, :ATTN_W] * Q_SCALE).astype(BF16)
        kv_ref[...] = proj[:, ATTN_W:ATTN_W + 2 * KV_W].astype(BF16)
        gu_ref[...] = proj[:, ATTN_W + 2 * KV_W:ATTN_W + 2 * KV_W + GMLP_W]
        gv_ref[...] = proj[:, ATTN_W + 2 * KV_W + GMLP_W:]

        @pl.when(i == fwd_step)
        def _():
            gather.forward()

        @pl.when(i == diag_step)
        def _():
            gather.forward_diagonal()

        @pl.when(i == n_steps - 1)
        def _():
            gather.finish()

    row = lambda w: pl.BlockSpec((tm, w), lambda i: (i, 0))
    outs = pl.pallas_call(
        body, name="fwd_in",
        out_shape=[jax.ShapeDtypeStruct((s, D_MODEL), BF16), jax.ShapeDtypeStruct((s, ATTN_W), BF16),
                   jax.ShapeDtypeStruct((s, 2 * KV_W), BF16), jax.ShapeDtypeStruct((s, GMLP_W), F32),
                   jax.ShapeDtypeStruct((s, GMLP_W), F32), jax.ShapeDtypeStruct((s, D_MODEL), BF16)]
        + [jax.ShapeDtypeStruct(_gathered_shape(sh, k), BF16) for sh, k in zip(shards, kinds)],
        grid=(n_steps,),
        in_specs=[row(D_MODEL), _const_spec((8, D_MODEL)), _const_spec(w_in.shape), _const_spec((1, IN_W))]
        + [ANY] * n_w,
        out_specs=[row(D_MODEL), row(ATTN_W), row(2 * KV_W), row(GMLP_W), row(GMLP_W), row(D_MODEL)] + [ANY] * n_w,
        scratch_shapes=_WeightGather.sems(n_w),
        compiler_params=_params(("arbitrary",)),
    )(x, modr, w_in, b_in, *shards)
    return outs[:6], outs[6:]


def _kv_variants(kk):
    kf = kk.astype(F32)
    lane = lax.broadcasted_iota(jnp.int32, kf.shape, 1)
    low = lane < HEAD_DIM
    k0_lo = jnp.where(low, kf, 0.0)
    k1_hi = jnp.where(low, 0.0, kf)
    k0_hi = pltpu.roll(k0_lo, HEAD_DIM, 1)
    k1_lo = pltpu.roll(k1_hi, HEAD_DIM, 1)
    return ((k0_lo.astype(BF16), k0_hi.astype(BF16)), (k1_lo.astype(BF16), k1_hi.astype(BF16)))


def _head_kv(h):
    return h // (N_HEADS // N_KV), h % 2


MIX_GROUP = 2


def _interleave(*gens):
    results = [None] * len(gens)
    active = list(enumerate(gens))
    while active:
        still = []
        for i, g in active:
            try:
                next(g)
                still.append((i, g))
            except StopIteration as done:
                results[i] = done.value
        active = still
    return results


def _attn_block_fwd(q_blk, kk, vv, bias_ref, sinks_ref, first_mask):
    kvar = _kv_variants(kk)
    vvar = _kv_variants(vv)
    heads = range(N_HEADS)
    q_pairs = [q_blk[:, (h // 2) * LANES:(h // 2 + 1) * LANES] for h in heads]
    logits = [_dot_nt(q_pairs[h], kvar[_head_kv(h)[0]][_head_kv(h)[1]]) + bias_ref[h] for h in heads]
    if first_mask is not None:
        logits = [jnp.where(first_mask, NEG_INF, lg) for lg in logits]
    yield
    ms = [jnp.maximum(jnp.max(logits[h], axis=-1, keepdims=True), sinks_ref[h]) for h in heads]
    yield
    es = [jnp.exp(logits[h] - ms[h]) for h in heads]
    ess = [jnp.exp(sinks_ref[h] - ms[h]) for h in heads]
    yield
    invs = [1.0 / (jnp.sum(es[h], axis=-1, keepdims=True) + ess[h]) for h in heads]
    probs = [(es[h] * invs[h], ess[h] * invs[h]) for h in heads]
    yield
    outs = [_dot(probs[h][0].astype(BF16), vvar[_head_kv(h)[0]][_head_kv(h)[1]]) for h in heads]
    pairs = [outs[2 * i] + outs[2 * i + 1] for i in range(N_HEADS // 2)]
    return jnp.concatenate(pairs, axis=1), probs, kvar, vvar


def _gmlp_chunk_fwd(gu, gv, ln_g, ln_b, wsm_ref, bsx, amat):
    u, tu = _gelu(gu)
    a, ta = _gelu(gv)
    yield
    mean = _split_dot(a, amat)
    d = a - mean
    yield
    var = _split_dot(d * d, amat)
    yield
    rstd = lax.rsqrt(var + LN_EPS)
    xhat = d * rstd
    vb = (xhat * ln_g + ln_b).astype(BF16)
    yield
    lane = lax.broadcasted_iota(jnp.int32, (BLOCK, LANES), 1)
    low = lane < GROUP_DIM
    cols = []
    for pair in range(N_GROUPS // 2):
        vp = vb[:, pair * LANES:(pair + 1) * LANES]
        cols.append(jnp.where(low, _dot(wsm_ref[2 * pair], vp), _dot(wsm_ref[2 * pair + 1], vp)))
    mixedv = jnp.concatenate(cols, axis=1) + bsx
    return u * mixedv, (u, tu, ta, xhat, rstd, vb, mixedv)


def _rms(a, g):
    r = lax.rsqrt(jnp.mean(a * a, axis=-1, keepdims=True) + LN_EPS)
    return a * r * g, r


def _fwd_mix(q, kv, gu, gv, x, modr, bias, sinks, gln_g, gln_b, wsm, bsx, amat, aog, gog, w_out, ln1_g, ln1_b, tm,
             ffn_shards, ffn_kinds):
    s = x.shape[0]
    nb = tm // BLOCK
    n_steps = s // tm
    fwd_step, diag_step = (7 * n_steps) // 16, (12 * n_steps) // 16
    n_w = len(ffn_shards)

    def body(q_ref, kv_ref, kvp_ref, gu_ref, gv_ref, x_ref, mod_ref, bias_ref, sinks_ref, glng_ref, glnb_ref, wsm_ref,
             bsx_ref, amat_ref, aog_ref, gog_ref, wout_ref, ln1g_ref, ln1b_ref, *rest):
        shard_refs = rest[:n_w]
        x1_ref, x1b_ref, y_ref, mixed_ref = rest[n_w:n_w + 4]
        gathered_refs = rest[n_w + 4:2 * n_w + 4]
        mix_scr, send_sems, recv_sems = rest[2 * n_w + 4:]
        i = pl.program_id(0)
        gather = _WeightGather(shard_refs, gathered_refs, ffn_kinds, send_sems, recv_sems)

        @pl.when(i == 0)
        def _():
            gather.start()

        col = lax.broadcasted_iota(jnp.int32, (BLOCK, 2 * BLOCK), 1)
        for b0 in range(0, nb, MIX_GROUP):
            gens = []
            for b in range(b0, min(b0 + MIX_GROUP, nb)):
                r0 = b * BLOCK
                if b == 0:
                    kvprev = kvp_ref[...]
                    first_mask = (col < BLOCK) & (i == 0)
                else:
                    kvprev = kv_ref[r0 - BLOCK:r0, :]
                    first_mask = None
                kvcur = kv_ref[r0:r0 + BLOCK, :]
                kk = jnp.concatenate([kvprev[:, :KV_W], kvcur[:, :KV_W]], axis=0)
                vv = jnp.concatenate([kvprev[:, KV_W:], kvcur[:, KV_W:]], axis=0)
                gens.append(_attn_block_fwd(q_ref[r0:r0 + BLOCK, :], kk, vv, bias_ref, sinks_ref, first_mask))
                gens.append(_gmlp_chunk_fwd(gu_ref[r0:r0 + BLOCK, :], gv_ref[r0:r0 + BLOCK, :], glng_ref[...],
                                            glnb_ref[...], wsm_ref, bsx_ref[...], amat_ref[...]))
            res = _interleave(*gens)
            for k, b in enumerate(range(b0, min(b0 + MIX_GROUP, nb))):
                r0 = b * BLOCK
                na, _ = _rms(res[2 * k][0], aog_ref[...])
                ng, _ = _rms(res[2 * k + 1][0], gog_ref[...])
                mix_scr[r0:r0 + BLOCK, :ATTN_W] = na.astype(BF16)
                mix_scr[r0:r0 + BLOCK, ATTN_W:] = ng.astype(BF16)
        mixed = mix_scr[...]
        mixed_ref[...] = mixed
        y = _dot(mixed, wout_ref[...])
        y_ref[...] = y.astype(BF16)
        z1 = ALPHA * x_ref[...] + mod_ref[2:3, :] * y
        xhat, _ = _ln_stats(z1)
        x1 = xhat * ln1g_ref[...] + ln1b_ref[...]
        x1_ref[...] = x1
        x1b_ref[...] = x1.astype(BF16)

        @pl.when(i == fwd_step)
        def _():
            gather.forward()

        @pl.when(i == diag_step)
        def _():
            gather.forward_diagonal()

        @pl.when(i == n_steps - 1)
        def _():
            gather.finish()

    row = lambda w: pl.BlockSpec((tm, w), lambda i: (i, 0))
    prev = pl.BlockSpec((BLOCK, 2 * KV_W), lambda i: (jnp.maximum(i * nb - 1, 0), 0))
    outs = pl.pallas_call(
        body, name="fwd_mix",
        out_shape=[jax.ShapeDtypeStruct((s, D_MODEL), F32)] + [jax.ShapeDtypeStruct((s, D_MODEL), BF16)] * 3
        + [jax.ShapeDtypeStruct(_gathered_shape(sh, k), BF16) for sh, k in zip(ffn_shards, ffn_kinds)],
        grid=(n_steps,),
        in_specs=[row(ATTN_W), row(2 * KV_W), prev, row(GMLP_W), row(GMLP_W), row(D_MODEL), _const_spec((8, D_MODEL)),
                  _const_spec((N_HEADS, BLOCK, 2 * BLOCK)), pl.BlockSpec(memory_space=pltpu.SMEM),
                  _const_spec((1, GMLP_W)), _const_spec((1, GMLP_W)), _const_spec((N_GROUPS, BLOCK, BLOCK)),
                  _const_spec((BLOCK, GMLP_W)), _const_spec((GMLP_W, GMLP_W)), _const_spec((1, ATTN_W)),
                  _const_spec((1, GMLP_W)), _const_spec((D_MODEL, D_MODEL)), _const_spec((1, D_MODEL)),
                  _const_spec((1, D_MODEL))] + [ANY] * n_w,
        out_specs=[row(D_MODEL)] * 4 + [ANY] * n_w,
        scratch_shapes=[pltpu.VMEM((tm, D_MODEL), BF16)] + _WeightGather.sems(n_w),
        compiler_params=_params(("arbitrary",)),
    )(q, kv, kv, gu, gv, x, modr, bias, sinks, gln_g, gln_b, wsm, bsx, amat, aog, gog, w_out, ln1_g, ln1_b, *ffn_shards)
    return outs[:4], outs[4:]


FF_BLOCKS = N_CHIPS // 2
FF_CHUNK = D_FF // FF_BLOCKS
FFN_SUB = 256


def _sigmoid(x):
    return 1.0 / (1.0 + jnp.exp(-x))


def _fwd_ffn(x1, target, modr, ln2_g, ln2_b, w_gu, w_dn, tm):
    s = x1.shape[0]

    def body(x1_ref, t_ref, mod_ref, g_ref, b_ref, wgu_ref, wdn_ref, h2_ref, act_ref, dy2_ref, dx1a_ref, acc_ref):
        @pl.when(pl.program_id(0) == 0)
        def _():
            acc_ref[...] = jnp.zeros_like(acc_ref)

        x1v = x1_ref[...]
        h2 = (x1v * (1.0 + mod_ref[4:5, :]) + mod_ref[3:4, :]).astype(BF16)
        h2_ref[...] = h2
        y2 = None
        for cc in range(FF_BLOCKS):
            c0 = cc * FF_CHUNK
            gate = _dot(h2, wgu_ref[cc])
            up = _dot(h2, wgu_ref[FF_BLOCKS + cc])
            act_ref[:, c0:c0 + FF_CHUNK] = gate.astype(BF16)
            act_ref[:, D_FF + c0:D_FF + c0 + FF_CHUNK] = up.astype(BF16)
            a = (gate * _sigmoid(gate) * up).astype(BF16)
            part = _dot(a, wdn_ref[c0:c0 + FF_CHUNK, :])
            y2 = part if y2 is None else y2 + part
        g2 = mod_ref[5:6, :]
        z2 = ALPHA * x1v + g2 * y2
        xhat, rstd = _ln_stats(z2)
        gain = g_ref[...]
        diff = xhat * gain + b_ref[...] - t_ref[...]
        dx2 = diff * (1.0 / D_MODEL)
        dz2 = _ln_bwd(dx2 * gain, xhat, rstd)
        dx1a_ref[...] = ALPHA * dz2
        dy2_ref[...] = (g2 * dz2).astype(BF16)
        acc_ref[0:1, :] += _colsum(diff * diff)
        acc_ref[1:2, :] += _colsum(dx2 * xhat)
        acc_ref[2:3, :] += _colsum(dx2)
        acc_ref[3:4, :] += _colsum(dz2 * y2)

    row = lambda w: pl.BlockSpec((tm, w), lambda i: (i, 0))
    return pl.pallas_call(
        body, name="fwd_ffn",
        out_shape=(jax.ShapeDtypeStruct((s, D_MODEL), BF16), jax.ShapeDtypeStruct((s, 2 * D_FF), BF16),
                   jax.ShapeDtypeStruct((s, D_MODEL), BF16), jax.ShapeDtypeStruct((s, D_MODEL), F32),
                   jax.ShapeDtypeStruct((8, D_MODEL), F32)),
        grid=(s // tm,),
        in_specs=[row(D_MODEL), row(D_MODEL), _const_spec((8, D_MODEL)), _const_spec((1, D_MODEL)),
                  _const_spec((1, D_MODEL)), _const_spec((N_CHIPS, D_MODEL, FF_CHUNK), single=True),
                  _const_spec((D_FF, D_MODEL), single=True)],
        out_specs=(row(D_MODEL), row(2 * D_FF), row(D_MODEL), row(D_MODEL), _const_spec((8, D_MODEL))),
        compiler_params=_params(("arbitrary",)),
    )(x1, target, modr, ln2_g, ln2_b, w_gu, w_dn)


def _bwd_ffn(dy2, act, w_gu, w_dn, tm):
    s = dy2.shape[0]

    def body(dy2_ref, act_ref, wgu_ref, wdn_ref, a_ref, dgu_ref, dh2_ref):
        dy2v = dy2_ref[...]
        dh2 = None
        for cc in range(FF_BLOCKS):
            c0 = cc * FF_CHUNK
            da = _dot_nt(dy2v, wdn_ref[c0:c0 + FF_CHUNK, :])
            gate = act_ref[:, c0:c0 + FF_CHUNK].astype(F32)
            up = act_ref[:, D_FF + c0:D_FF + c0 + FF_CHUNK].astype(F32)
            sg = _sigmoid(gate)
            sl = gate * sg
            a_ref[:, c0:c0 + FF_CHUNK] = (sl * up).astype(BF16)
            dgate = (da * up * (sg * (1.0 + gate * (1.0 - sg)))).astype(BF16)
            dup = (da * sl).astype(BF16)
            dgu_ref[:, c0:c0 + FF_CHUNK] = dgate
            dgu_ref[:, D_FF + c0:D_FF + c0 + FF_CHUNK] = dup
            part = _dot_nt(dgate, wgu_ref[cc]) + _dot_nt(dup, wgu_ref[FF_BLOCKS + cc])
            dh2 = part if dh2 is None else dh2 + part
        dh2_ref[...] = dh2.astype(BF16)

    row = lambda w: pl.BlockSpec((tm, w), lambda i: (i, 0))
    return pl.pallas_call(
        body, name="bwd_ffn",
        out_shape=(jax.ShapeDtypeStruct((s, D_FF), BF16), jax.ShapeDtypeStruct((s, 2 * D_FF), BF16),
                   jax.ShapeDtypeStruct((s, D_MODEL), BF16)),
        grid=(s // tm,),
        in_specs=[row(D_MODEL), row(2 * D_FF), _const_spec((N_CHIPS, D_MODEL, FF_CHUNK), single=True),
                  _const_spec((D_FF, D_MODEL), single=True)],
        out_specs=(row(D_FF), row(2 * D_FF), row(D_MODEL)),
        compiler_params=_params(("parallel",)),
    )(dy2, act, w_gu, w_dn)


def _bwd_mid(dh2, dx1a, x1, x, y, modr, ln1_g, w_out, tm, swap_fulls, swap_kinds):
    s = x.shape[0]
    n_steps = s // tm
    n_g = len(swap_fulls)

    def body(dh2_ref, dx1a_ref, x1_ref, x_ref, y_ref, mod_ref, g_ref, wout_ref, *rest):
        full_refs = rest[:n_g]
        dxa_ref, dy_ref, dmix_ref, acc_ref = rest[n_g:n_g + 4]
        got_refs = rest[n_g + 4:2 * n_g + 4]
        swap = _HalfSwap(full_refs, got_refs, swap_kinds, *rest[2 * n_g + 4:])
        i = pl.program_id(0)

        @pl.when(i == 0)
        def _():
            swap.start()
            acc_ref[...] = jnp.zeros_like(acc_ref)

        dh2 = dh2_ref[...].astype(F32)
        x1v = x1_ref[...].astype(F32)
        yv = y_ref[...].astype(F32)
        g1 = mod_ref[2:3, :]
        dx1 = dx1a_ref[...] + dh2 * (1.0 + mod_ref[4:5, :])
        z1 = ALPHA * x_ref[...] + g1 * yv
        xhat, rstd = _ln_stats(z1)
        dz1 = _ln_bwd(dx1 * g_ref[...], xhat, rstd)
        dxa_ref[...] = (ALPHA * dz1).astype(BF16)
        dy = (g1 * dz1).astype(BF16)
        dy_ref[...] = dy
        dmix_ref[...] = _dot_nt(dy, wout_ref[...]).astype(BF16)
        acc_ref[0:1, :] += _colsum(dh2 * x1v)
        acc_ref[1:2, :] += _colsum(dh2)
        acc_ref[2:3, :] += _colsum(dx1 * xhat)
        acc_ref[3:4, :] += _colsum(dx1)
        acc_ref[4:5, :] += _colsum(dz1 * yv)

        @pl.when(i == n_steps - 1)
        def _():
            swap.wait()

    row = lambda w: pl.BlockSpec((tm, w), lambda i: (i, 0))
    outs = pl.pallas_call(
        body, name="bwd_mid",
        out_shape=[jax.ShapeDtypeStruct((s, D_MODEL), BF16), jax.ShapeDtypeStruct((s, D_MODEL), BF16),
                   jax.ShapeDtypeStruct((s, D_MODEL), BF16), jax.ShapeDtypeStruct((8, D_MODEL), F32)]
        + _HalfSwap.out_shapes(swap_fulls, swap_kinds),
        grid=(n_steps,),
        in_specs=[row(D_MODEL)] * 5 + [_const_spec((8, D_MODEL)), _const_spec((1, D_MODEL)),
                                       _const_spec((D_MODEL, D_MODEL))] + [ANY] * n_g,
        out_specs=[row(D_MODEL), row(D_MODEL), row(D_MODEL), _const_spec((8, D_MODEL))] + [ANY] * n_g,
        scratch_shapes=_HalfSwap.sems(n_g),
        compiler_params=_params(("arbitrary",)),
    )(dh2, dx1a, x1, x, y, modr, ln1_g, w_out, *swap_fulls)
    return outs[:4], outs[4:]


def _fold_kv(t0, t1):
    lane = lax.broadcasted_iota(jnp.int32, t0.shape, 1)
    f0 = t0 + pltpu.roll(t0, HEAD_DIM, 1)
    f1 = t1 + pltpu.roll(t1, HEAD_DIM, 1)
    return jnp.where(lane < HEAD_DIM, f0, f1)


def _bwd_mix(q, kv, gu, gv, dmix, bias, sinks, gln_g, gln_b, wsm, bsx, amat, aog, gog, grad_parts, grad_kinds):
    s = q.shape[0]
    tile = 2 * BLOCK
    n_steps = s // tile
    n_g = len(grad_parts)

    def body(q_ref, kv_ref, kvp_ref, gu_ref, gv_ref, dmix_ref, bias_ref, sinks_ref, glng_ref, glnb_ref, wsm_ref,
             bsx_ref, amat_ref, aog_ref, gog_ref, *rest):
        part_refs = rest[:n_g]
        dq_ref, dkv_ref, dgu_ref, dgv_ref, gbias_ref, dws_ref, dbs_ref, vec_ref, dsink_ref = rest[n_g:n_g + 9]
        rx_refs = rest[n_g + 9:2 * n_g + 9]
        carry, done, send_sems, recv_sems = rest[2 * n_g + 9:]
        n = pl.program_id(0)
        exchange = _ChipExchange(part_refs, rx_refs, grad_kinds, send_sems, recv_sems)

        @pl.when(n == 0)
        def _():
            exchange.start()
            carry[...] = jnp.zeros_like(carry)
            done[...] = jnp.zeros_like(done)
            gbias_ref[...] = jnp.zeros_like(gbias_ref)
            dws_ref[...] = jnp.zeros_like(dws_ref)
            dbs_ref[...] = jnp.zeros_like(dbs_ref)
            vec_ref[...] = jnp.zeros_like(vec_ref)
            dsink_ref[...] = jnp.zeros_like(dsink_ref)

        @pl.when(n == n_steps)
        def _():
            dkv_ref[:BLOCK, :] = done[...].astype(BF16)
            dkv_ref[BLOCK:, :] = carry[...].astype(BF16)
            exchange.wait()

        @pl.when(n < n_steps)
        def _():
            col = lax.broadcasted_iota(jnp.int32, (BLOCK, 2 * BLOCK), 1)
            lane = lax.broadcasted_iota(jnp.int32, (BLOCK, LANES), 1)
            low = lane < HEAD_DIM
            rows = [slice(0, BLOCK), slice(BLOCK, tile)]
            kv_blocks = [kvp_ref[...], kv_ref[rows[0], :], kv_ref[rows[1], :]]
            masks = [(col < BLOCK) & (n == 0), None]
            q_blks = [q_ref[r, :] for r in rows]
            fwd = []
            for b in range(2):
                kk = jnp.concatenate([kv_blocks[b][:, :KV_W], kv_blocks[b + 1][:, :KV_W]], axis=0)
                vv = jnp.concatenate([kv_blocks[b][:, KV_W:], kv_blocks[b + 1][:, KV_W:]], axis=0)
                fwd.append(_attn_block_fwd(q_blks[b], kk, vv, bias_ref, sinks_ref, masks[b]))
                fwd.append(_gmlp_chunk_fwd(gu_ref[rows[b], :], gv_ref[rows[b], :], glng_ref[...], glnb_ref[...],
                                           wsm_ref, bsx_ref[...], amat_ref[...]))
            res = _interleave(*fwd[:2]) + _interleave(*fwd[2:])

            def gating_bwd(b, d_gm, saved):
                u, tu, ta, xhat, rstd, vb, mixedv = saved
                dgu_ref[rows[b], :] = (d_gm * mixedv * _gelu_grad(gu_ref[rows[b], :], tu)).astype(BF16)
                dmx = d_gm * u
                dmxb = dmx.astype(BF16)
                yield
                dvn_cols, dws = [], []
                for pair in range(N_GROUPS // 2):
                    dp_ = dmxb[:, pair * LANES:(pair + 1) * LANES]
                    vp = vb[:, pair * LANES:(pair + 1) * LANES]
                    dvn_cols.append(
                        jnp.where(low, _dot_tn(wsm_ref[2 * pair], dp_), _dot_tn(wsm_ref[2 * pair + 1], dp_)))
                    zero = jnp.zeros_like(dp_)
                    dws.append(_dot_nt(jnp.where(low, dp_, zero), vp))
                    dws.append(_dot_nt(jnp.where(low, zero, dp_), vp))
                dvn = jnp.concatenate(dvn_cols, axis=1)
                yield
                dxh = dvn * glng_ref[...]
                am = amat_ref[...]
                m1 = _split_dot(dxh, am)
                m2 = _split_dot(dxh * xhat, am)
                yield
                da = rstd * (dxh - m1 - xhat * m2)
                dgv_ref[rows[b], :] = (da * _gelu_grad(gv_ref[rows[b], :], ta)).astype(BF16)
                return dmx, dws, _colsum(dvn * xhat), _colsum(dvn)

            def attention_bwd(b, d_attn, probs, kvar, vvar):
                heads = range(N_HEADS)
                sels = [low if h % 2 == 0 else jnp.logical_not(low) for h in heads]
                pair_of = lambda a, h: a[:, (h // 2) * LANES:(h // 2 + 1) * LANES]
                do_hs = [jnp.where(sels[h], pair_of(d_attn, h), 0.0).astype(BF16) for h in heads]
                q_hs = [jnp.where(sels[h], pair_of(q_blks[b], h), jnp.zeros((BLOCK, LANES), BF16)) for h in heads]
                dps = [_dot_nt(do_hs[h], vvar[_head_kv(h)[0]][_head_kv(h)[1]]) for h in heads]
                yield
                deltas = [jnp.sum(probs[h][0] * dps[h], axis=-1, keepdims=True) for h in heads]
                yield
                dss = [probs[h][0] * (dps[h] - deltas[h]) for h in heads]
                dsinks = [-(probs[h][1] * deltas[h]) for h in heads]
                dsbs = [ds.astype(BF16) for ds in dss]
                pbs = [probs[h][0].astype(BF16) for h in heads]
                yield
                dqs = [_dot(dsbs[h], kvar[_head_kv(h)[0]][_head_kv(h)[1]]) for h in heads]
                tks = [_dot_tn(dsbs[h], q_hs[h]) for h in heads]
                tvs = [_dot_tn(pbs[h], do_hs[h]) for h in heads]
                dq_cols = [dqs[2 * i] + dqs[2 * i + 1] for i in range(N_HEADS // 2)]
                dq_ref[rows[b], :] = (jnp.concatenate(dq_cols, axis=1) * Q_SCALE).astype(BF16)
                per_kv = N_HEADS // N_KV
                kv_sum = lambda ts, kvh: sum(ts[kvh * per_kv + 1:(kvh + 1) * per_kv], ts[kvh * per_kv])
                dkk = _fold_kv(kv_sum(tks, 0), kv_sum(tks, 1))
                dvv = _fold_kv(kv_sum(tvs, 0), kv_sum(tvs, 1))
                return jnp.concatenate([dkk, dvv], axis=1), dss, dsinks

            bwd, rms_g = [], []
            for b in range(2):
                attn, probs, kvar, vvar = res[2 * b]
                gm, saved = res[2 * b + 1]
                na_unit, r_a = _rms(attn, 1.0)
                ng_unit, r_g = _rms(gm, 1.0)
                dmix = dmix_ref[rows[b], :].astype(F32)
                dn_a = dmix[:, :ATTN_W]
                dn_g = dmix[:, ATTN_W:]
                rms_g.append((_colsum(dn_a * na_unit), _colsum(dn_g * ng_unit)))
                t_a = dn_a * aog_ref[...]
                d_attn = r_a * t_a - na_unit * (r_a * jnp.mean(t_a * na_unit, axis=-1, keepdims=True))
                t_g = dn_g * gog_ref[...]
                d_gm = r_g * t_g - ng_unit * (r_g * jnp.mean(t_g * ng_unit, axis=-1, keepdims=True))
                bwd.append(attention_bwd(b, d_attn, probs, kvar, vvar))
                bwd.append(gating_bwd(b, d_gm, saved))
            (dkv_a, dss_a, dsk_a), (dmx_a, dws_a, glg_a, glb_a) = _interleave(*bwd[:2])
            (dkv_b, dss_b, dsk_b), (dmx_b, dws_b, glg_b, glb_b) = _interleave(*bwd[2:])

            vec_ref[0:1, :] += rms_g[0][0] + rms_g[1][0]
            vec_ref[1:2, :] += rms_g[0][1] + rms_g[1][1]
            vec_ref[2:3, :] += glg_a + glg_b
            vec_ref[3:4, :] += glb_a + glb_b
            dbs_ref[...] += dmx_a + dmx_b
            for g in range(N_GROUPS):
                dws_ref[g] += dws_a[g] + dws_b[g]
            for h in range(N_HEADS):
                gbias_ref[h] += dss_a[h] + dss_b[h]
                dsink_ref[h] += dsk_a[h] + dsk_b[h]

            dkv_ref[:BLOCK, :] = done[...].astype(BF16)
            dkv_ref[BLOCK:, :] = (carry[...] + dkv_a[:BLOCK]).astype(BF16)
            done[...] = dkv_a[BLOCK:] + dkv_b[:BLOCK]
            carry[...] = dkv_b[BLOCK:]

    last = n_steps - 1
    cur = lambda w: pl.BlockSpec((tile, w), lambda n: (jnp.minimum(n, last), 0))
    late = lambda w: pl.BlockSpec((tile, w), lambda n: (jnp.clip(n - 1, 0, last), 0))
    before = pl.BlockSpec((BLOCK, 2 * KV_W), lambda n: (jnp.clip(2 * n - 1, 0, 2 * last + 1), 0))
    outs = pl.pallas_call(
        body, name="bwd_mix",
        out_shape=[jax.ShapeDtypeStruct((s, ATTN_W), BF16), jax.ShapeDtypeStruct((s, 2 * KV_W), BF16),
                   jax.ShapeDtypeStruct((s, GMLP_W), BF16), jax.ShapeDtypeStruct((s, GMLP_W), BF16),
                   jax.ShapeDtypeStruct((N_HEADS, BLOCK, 2 * BLOCK), F32),
                   jax.ShapeDtypeStruct((N_GROUPS, BLOCK, BLOCK), F32),
                   jax.ShapeDtypeStruct((BLOCK, GMLP_W), F32), jax.ShapeDtypeStruct((8, GMLP_W), F32),
                   jax.ShapeDtypeStruct((N_HEADS, BLOCK, 1), F32)]
        + [jax.ShapeDtypeStruct(_rx_shape(p.shape, k), BF16) for p, k in zip(grad_parts, grad_kinds)],
        grid=(n_steps + 1,),
        in_specs=[cur(ATTN_W), cur(2 * KV_W), before, cur(GMLP_W), cur(GMLP_W), cur(D_MODEL),
                  _const_spec((N_HEADS, BLOCK, 2 * BLOCK)), pl.BlockSpec(memory_space=pltpu.SMEM),
                  _const_spec((1, GMLP_W)), _const_spec((1, GMLP_W)), _const_spec((N_GROUPS, BLOCK, BLOCK)),
                  _const_spec((BLOCK, GMLP_W)), _const_spec((GMLP_W, GMLP_W)), _const_spec((1, ATTN_W)),
                  _const_spec((1, GMLP_W))] + [ANY] * n_g,
        out_specs=[cur(ATTN_W), late(2 * KV_W), cur(GMLP_W), cur(GMLP_W),
                   _const_spec((N_HEADS, BLOCK, 2 * BLOCK)), _const_spec((N_GROUPS, BLOCK, BLOCK)),
                   _const_spec((BLOCK, GMLP_W)), _const_spec((8, GMLP_W)), _const_spec((N_HEADS, BLOCK, 1))]
        + [ANY] * n_g,
        scratch_shapes=[pltpu.VMEM((BLOCK, 2 * KV_W), F32), pltpu.VMEM((BLOCK, 2 * KV_W), F32)]
        + _ChipExchange.sems(n_g),
        compiler_params=_params(("arbitrary",)),
    )(q, kv, kv, gu, gv, dmix, bias, sinks, gln_g, gln_b, wsm, bsx, amat, aog, gog, *grad_parts)
    return outs[:9], outs[9:]


def _mix_finalize(gbias, bucket, dws, dbs, dsink):
    def body(gb_ref, bucket_ref, dws_ref, dbs_ref, dsink_ref, tall_ref):
        bk = bucket_ref[...]
        lane = lax.broadcasted_iota(jnp.int32, (N_BUCKETS, LANES), 1)
        rowi = lax.broadcasted_iota(jnp.int32, (N_BUCKETS, LANES), 0)
        drb = jnp.zeros((N_BUCKETS, LANES), F32)
        dsk = jnp.zeros((8, LANES), F32)
        lane8 = lax.broadcasted_iota(jnp.int32, (8, LANES), 1)
        for h in range(N_HEADS):
            g = gb_ref[h]
            for b in range(N_BUCKETS):
                tot = jnp.sum(_colsum(jnp.where(bk == float(b), g, 0.0)), axis=1, keepdims=True)
                drb = jnp.where((lane == h) & (rowi == b), tot, drb)
            sk = jnp.sum(dsink_ref[h], axis=0, keepdims=True)
            dsk = jnp.where(lane8 == h, sk, dsk)
        tall_ref[TALL_RB:TALL_RB + N_BUCKETS, :] = drb
        tall_ref[TALL_SK:TALL_SK + 8, :] = dsk
        ti = lax.broadcasted_iota(jnp.int32, (BLOCK, BLOCK), 0)
        ui = lax.broadcasted_iota(jnp.int32, (BLOCK, BLOCK), 1)
        for g in range(N_GROUPS):
            tall_ref[g * BLOCK:(g + 1) * BLOCK, :] = jnp.where(ti >= ui, dws_ref[g], 0.0)
        gi = lax.broadcasted_iota(jnp.int32, (GMLP_W, LANES), 0) // GROUP_DIM
        li = lax.broadcasted_iota(jnp.int32, (GMLP_W, LANES), 1)
        ind = jnp.where(gi == li, 1.0, 0.0).astype(BF16)
        d = dbs_ref[...]
        hi = d.astype(BF16)
        r1 = d - hi.astype(F32)
        mid = r1.astype(BF16)
        lo = (r1 - mid.astype(F32)).astype(BF16)
        dbsg = _dot(hi, ind) + _dot(mid, ind) + _dot(lo, ind)
        tall_ref[TALL_BS:TALL_BS + N_GROUPS, :] = dbsg.T[:N_GROUPS, :]

    return pl.pallas_call(
        body, name="mix_finalize", out_shape=jax.ShapeDtypeStruct((TALL_ROWS, LANES), F32), grid=(1,),
        in_specs=[_const_spec((N_HEADS, BLOCK, 2 * BLOCK)), _const_spec((BLOCK, 2 * BLOCK)),
                  _const_spec((N_GROUPS, BLOCK, BLOCK)), _const_spec((BLOCK, GMLP_W)),
                  _const_spec((N_HEADS, BLOCK, 1))],
        out_specs=_const_spec((TALL_ROWS, LANES)),
        compiler_params=_params(("arbitrary",)),
    )(gbias, bucket, dws, dbs, dsink)


def _bwd_in(dq, dkv, dgu, dgv, dxa, x, modr, w_in, tm):
    s = x.shape[0]

    def body(dq_ref, dkv_ref, dgu_ref, dgv_ref, dxa_ref, x_ref, mod_ref, w_ref, gx_ref, acc_ref, db_ref):
        @pl.when(pl.program_id(0) == 0)
        def _():
            acc_ref[...] = jnp.zeros_like(acc_ref)
            db_ref[...] = jnp.zeros_like(db_ref)

        dproj = jnp.concatenate([dq_ref[...], dkv_ref[...], dgu_ref[...], dgv_ref[...]], axis=1)
        wb = IN_W // N_CHIPS
        dh1 = sum([_dot_nt(dproj[:, j * wb:(j + 1) * wb], w_ref[j]) for j in range(1, N_CHIPS)],
                  _dot_nt(dproj[:, :wb], w_ref[0]))
        gx_ref[...] = dxa_ref[...].astype(F32) + dh1 * (1.0 + mod_ref[1:2, :])
        acc_ref[0:1, :] += _colsum(dh1 * x_ref[...].astype(F32))
        acc_ref[1:2, :] += _colsum(dh1)
        db_ref[0:1, :] += _colsum(dproj.astype(F32))

    row = lambda w: pl.BlockSpec((tm, w), lambda i: (i, 0))
    return pl.pallas_call(
        body, name="bwd_in",
        out_shape=(jax.ShapeDtypeStruct((s, D_MODEL), F32), jax.ShapeDtypeStruct((8, D_MODEL), F32),
                   jax.ShapeDtypeStruct((8, IN_W), F32)),
        grid=(s // tm,),
        in_specs=[row(ATTN_W), row(2 * KV_W), row(GMLP_W), row(GMLP_W), row(D_MODEL), row(D_MODEL),
                  _const_spec((8, D_MODEL)), _const_spec(w_in.shape)],
        out_specs=(row(D_MODEL), _const_spec((8, D_MODEL)), _const_spec((8, IN_W))),
        compiler_params=_params(("arbitrary",)),
    )(dq, dkv, dgu, dgv, dxa, x, modr, w_in)


def _wgrad(a, bs, tm, tk, name, owner_blocks=False, gather_vs=()):
    k_all, m = a.shape
    n = sum(b.shape[1] for b in bs)
    nk = k_all // tk
    nm = m // tm
    n_b = len(bs)
    n_v = len(gather_vs)
    wb = n // N_CHIPS

    def body(a_ref, *rest):
        b_refs, v_refs = rest[:n_b], rest[n_b:n_b + n_v]
        o_ref, ob_ref = rest[n_b + n_v:n_b + n_v + 2]
        vg_refs = rest[n_b + n_v + 2:n_b + 2 * n_v + 2]
        i, k = pl.program_id(0), pl.program_id(1)
        if n_v:
            gather = _Gather8(v_refs, vg_refs, *rest[n_b + 2 * n_v + 2:])

            @pl.when((i == 0) & (k == 0))
            def _():
                gather.start()

            @pl.when((i == nm - 1) & (k == 0))
            def _():
                gather.forward()

        @pl.when(k == 0)
        def _():
            o_ref[...] = jnp.zeros_like(o_ref)

        b = b_refs[0][...] if n_b == 1 else jnp.concatenate([r[...] for r in b_refs], axis=1)
        if owner_blocks:
            av = a_ref[...]
            for j in range(N_CHIPS):
                o_ref[j] += _dot_tn(av, b[:, j * wb:(j + 1) * wb])
        else:
            o_ref[...] += _dot_tn(a_ref[...], b)

        @pl.when(k == nk - 1)
        def _():
            ob_ref[...] = o_ref[...].astype(BF16)

        if n_v:
            @pl.when((i == nm - 1) & (k == nk - 1))
            def _():
                gather.finish()

    if owner_blocks:
        out_spec = pl.BlockSpec((N_CHIPS, tm, wb), lambda i, k: (0, i, 0))
        shape = (N_CHIPS, m, wb)
    else:
        out_spec = pl.BlockSpec((tm, n), lambda i, k: (i, 0))
        shape = (m, n)
    outs = pl.pallas_call(
        body, name=name,
        out_shape=[jax.ShapeDtypeStruct(shape, F32), jax.ShapeDtypeStruct(shape, BF16)] + _gathered8_shapes(gather_vs),
        grid=(nm, nk),
        in_specs=[pl.BlockSpec((tk, tm), lambda i, k: (k, i))]
        + [pl.BlockSpec((tk, b.shape[1]), lambda i, k: (k, 0)) for b in bs] + [ANY] * n_v,
        out_specs=[out_spec, out_spec] + [ANY] * n_v,
        scratch_shapes=_Gather8.sems(n_v) if n_v else [],
        compiler_params=_params(("arbitrary", "arbitrary") if n_v else ("parallel", "arbitrary")),
    )(a, *bs, *gather_vs)
    return outs[0], outs[1], outs[2:]


def _adam_math(w, g, m, v):
    m2 = ADAM_B1 * m + (1.0 - ADAM_B1) * g
    v2 = ADAM_B2 * v + (1.0 - ADAM_B2) * (g * g)
    m_hat = m2 / (1.0 - ADAM_B1 ** ADAM_STEP)
    v_hat = v2 / (1.0 - ADAM_B2 ** ADAM_STEP)
    delta = -ADAM_LR * (m_hat / (jnp.sqrt(v_hat) + ADAM_EPS) + ADAM_WD * w)
    return delta, m2, v2


def _adam_halves(w, mine, got, m, v, tr, name):
    r, cc = w.shape
    h = r // 2
    nt = h // tr

    def body(c_ref, w_ref, mine_ref, got_ref, m_ref, v_ref, g_ref, d_ref, m2_ref, v2_ref):
        g = jnp.where(pl.program_id(0) == c_ref[0], mine_ref[...], got_ref[...])
        g_ref[...] = g
        d, m2, v2 = _adam_math(w_ref[...], g, m_ref[...], v_ref[...])
        d_ref[...] = d
        m2_ref[...] = m2
        v2_ref[...] = v2

    full = pl.BlockSpec((tr, cc), lambda hh, i, c_ref: (hh * nt + i, 0))
    half = pl.BlockSpec((tr, cc), lambda hh, i, c_ref: (i, 0))
    shp = jax.ShapeDtypeStruct((r, cc), F32)
    return pl.pallas_call(
        body, name=name, out_shape=(shp, shp, shp, shp),
        grid_spec=pltpu.PrefetchScalarGridSpec(
            num_scalar_prefetch=1, grid=(2, nt), in_specs=[full, half, half, full, full],
            out_specs=(full, full, full, full)),
        compiler_params=_params(("arbitrary", "arbitrary")),
    )(_core_index_scalar(), w, mine, got, m, v)


def _adam_w_ada(sc_t, dmod_cols, w, m, v, tr):
    r, cc = w.shape

    def body(sct_ref, dm_ref, w_ref, m_ref, v_ref, g_ref, d_ref, m2_ref, v2_ref):
        g = sct_ref[:, 0:1] * dm_ref[0:1, :]
        for k in range(1, N_DEV):
            g = g + sct_ref[:, k:k + 1] * dm_ref[k:k + 1, :]
        g_ref[...] = g
        d, m2, v2 = _adam_math(w_ref[...], g, m_ref[...], v_ref[...])
        d_ref[...] = d
        m2_ref[...] = m2
        v2_ref[...] = v2

    spec = pl.BlockSpec((tr, cc), lambda i: (i, 0))
    shp = jax.ShapeDtypeStruct((r, cc), F32)
    return pl.pallas_call(
        body, name="adam_w_ada", out_shape=(shp, shp, shp, shp), grid=(r // tr,),
        in_specs=[pl.BlockSpec((tr, N_DEV), lambda i: (i, 0)), _const_spec((N_DEV, cc)), spec, spec, spec],
        out_specs=(spec, spec, spec, spec), compiler_params=_params(("parallel",)),
    )(sc_t, dmod_cols, w, m, v)


def _pack_wide(acc_i, acc_m, acc_f, db_in, vec):
    arrs = [acc_i, acc_m, acc_f, db_in, vec]
    i_, m_, f_, b_, v_ = range(5)
    src = {"b_in": (b_, 0), "ln1_g": (m_, 2), "ln1_b": (m_, 3), "ln2_g": (f_, 1), "ln2_b": (f_, 2),
           "gmlp_ln_g": (v_, 2), "gmlp_ln_b": (v_, 3), "attn_out_g": (v_, 0), "gmlp_out_g": (v_, 1), "loss": (f_, 0)}
    dmod = [(i_, 1), (i_, 0), (m_, 4), (m_, 1), (m_, 0), (f_, 3)]

    def body(*refs):
        ins, wide_ref = refs[:5], refs[5]
        wide_ref[...] = jnp.zeros_like(wide_ref)
        for k, (a, row) in enumerate(dmod):
            wide_ref[0:1, k * D_MODEL:(k + 1) * D_MODEL] = ins[a][row:row + 1, :]
        for name, (a, row) in src.items():
            r, off, n = WIDE_LAYOUT[name]
            wide_ref[r:r + 1, off:off + n] = ins[a][row:row + 1, :]

    return pl.pallas_call(
        body, name="pack_wide", out_shape=jax.ShapeDtypeStruct((8, WIDE_W), F32), grid=(1,),
        in_specs=[_const_spec(a.shape) for a in arrs], out_specs=_const_spec((8, WIDE_W)),
        compiler_params=_params(("arbitrary",)),
    )(*arrs)


def _adam_small(gw, gt, wide_wmv, w_s, b_s, rel_bias, sinks, after):
    names = list(WIDE_PARAMS)
    tall = [("gmlp_w_s", w_s), ("gmlp_b_s", b_s), ("rel_bias", rel_bias), ("attn_sinks", sinks)]
    ins = [gw, gt]
    for n in names:
        ins += list(wide_wmv[n])
    for _, t in tall:
        ins += list(t)
    n_in = len(ins)

    def body(*refs):
        gw_ref, gt_ref = refs[0], refs[1]
        wmv = refs[2:n_in]
        dmod_ref, loss_ref = refs[n_in + 1], refs[n_in + 2]
        outs = refs[n_in + 3:]

        def tall_sum(r0, nr):
            g = gt_ref[r0:r0 + nr, :]
            for d in range(1, N_DEV):
                g = g + gt_ref[d * TALL_ROWS + r0:d * TALL_ROWS + r0 + nr, :]
            return g

        def emit(k, g, w_ref, m_ref, v_ref):
            d, m2, v2 = _adam_math(w_ref[...], g, m_ref[...], v_ref[...])
            outs[4 * k][...] = g
            outs[4 * k + 1][...] = d
            outs[4 * k + 2][...] = m2
            outs[4 * k + 3][...] = v2

        gsum = gw_ref[0:8, :]
        for d in range(1, N_DEV):
            gsum = gsum + gw_ref[8 * d:8 * d + 8, :]
        for d in range(N_DEV):
            dmod_ref[d:d + 1, :] = gw_ref[8 * d:8 * d + 1, :]
        for k, n in enumerate(names):
            r, off, sz = WIDE_LAYOUT[n]
            emit(k, gsum[r:r + 1, off:off + sz], *wmv[3 * k:3 * k + 3])
        r, off, sz = WIDE_LAYOUT["loss"]
        tot = jnp.sum(gsum[r:r + 1, off:off + sz], axis=1, keepdims=True)
        loss_ref[...] = jnp.broadcast_to(tot * (0.5 / D_MODEL), loss_ref.shape)

        k0 = len(names)
        ws_refs = wmv[3 * k0:3 * k0 + 3]
        for g in range(N_GROUPS):
            rows = slice(g * BLOCK, (g + 1) * BLOCK)
            gg = tall_sum(g * BLOCK, BLOCK)
            d, m2, v2 = _adam_math(ws_refs[0][rows, :], gg, ws_refs[1][rows, :], ws_refs[2][rows, :])
            outs[4 * k0][rows, :] = gg
            outs[4 * k0 + 1][rows, :] = d
            outs[4 * k0 + 2][rows, :] = m2
            outs[4 * k0 + 3][rows, :] = v2
        emit(k0 + 1, tall_sum(TALL_BS, N_GROUPS), *wmv[3 * (k0 + 1):3 * (k0 + 1) + 3])
        emit(k0 + 2, tall_sum(TALL_RB, N_BUCKETS)[:, :N_HEADS], *wmv[3 * (k0 + 2):3 * (k0 + 2) + 3])
        emit(k0 + 3, tall_sum(TALL_SK, 8)[0:1, :N_HEADS], *wmv[3 * (k0 + 3):3 * (k0 + 3) + 3])

    out_shapes = [jax.ShapeDtypeStruct((N_DEV, WIDE_W), F32), jax.ShapeDtypeStruct((8, LANES), F32)]
    for n in names:
        out_shapes += [jax.ShapeDtypeStruct(wide_wmv[n][0].shape, F32)] * 4
    for _, t in tall:
        out_shapes += [jax.ShapeDtypeStruct(t[0].shape, F32)] * 4
    res = pl.pallas_call(
        body, name="adam_small", out_shape=out_shapes, grid=(1,),
        in_specs=[_const_spec(a.shape) for a in ins] + [ANY], out_specs=[_const_spec(o.shape) for o in out_shapes],
        compiler_params=_params(("arbitrary",)),
    )(*ins, after)
    out = {}
    for k, n in enumerate(names + [t[0] for t in tall]):
        out[n] = tuple(res[2 + 4 * k:6 + 4 * k])
    return res[0], res[1], out


def kernel(x, c, rel_bias, w_ada, b_ada, w_in, b_in, attn_sinks, gmlp_ln_g, gmlp_ln_b, gmlp_w_s, gmlp_b_s, attn_out_g, gmlp_out_g, w_out, ln1_g, ln1_b, w_gate_up, w_down, ln2_g, ln2_b, loss_target, m_rel_bias, m_w_ada, m_b_ada, m_w_in, m_b_in, m_attn_sinks, m_gmlp_ln_g, m_gmlp_ln_b, m_gmlp_w_s, m_gmlp_b_s, m_attn_out_g, m_gmlp_out_g, m_w_out, m_ln1_g, m_ln1_b, m_w_gate_up, m_w_down, m_ln2_g, m_ln2_b, v_rel_bias, v_w_ada, v_b_ada, v_w_in, v_b_in, v_attn_sinks, v_gmlp_ln_g, v_gmlp_ln_b, v_gmlp_w_s, v_gmlp_b_s, v_attn_out_g, v_gmlp_out_g, v_w_out, v_ln1_g, v_ln1_b, v_w_gate_up, v_w_down, v_ln2_g, v_ln2_b):
    ix, iy, ic = _my_pos()
    chip = 2 * ix + iy
    dev = 4 * ix + 2 * iy + ic
    s = x.shape[1]
    xs = x[0]
    tgt = loss_target[0]
    tm_big = min(512, s)
    tm_ffn = min(FFN_SUB, s)
    n_ada = w_ada.shape[2]

    w_in_s, w_out_s = w_in[0].astype(BF16), w_out[0].astype(BF16)
    w_gu_s, w_dn_s = w_gate_up[0].astype(BF16), w_down[0].astype(BF16)
    sc_all, mod_rows, (w_in_g, w_out_g) = _prologue(
        jnp.pad(c, ((0, 7), (0, 0))), w_ada[0], lax.dynamic_slice_in_dim(b_ada, chip * n_ada, n_ada, axis=1),
        [w_in_s, w_out_s])
    mod_all = mod_rows.reshape(N_DEV, N_DEV, -1)
    mod_row = lax.dynamic_index_in_dim(mod_all[0::2], dev, axis=1, keepdims=False)
    modr = jnp.pad(mod_row.reshape(6, D_MODEL), ((0, 2), (0, 0)))
    w_in_f = _insert_own(w_in_g, w_in_s, "blk", chip)

    bucket = _bucket_table()
    bias, wsm = _prep_tables(bucket, rel_bias, gmlp_w_s[0])
    bsx = jnp.repeat(gmlp_b_s[0].T, GROUP_DIM, axis=1)
    amat = _group_mean_matrix()
    sinks = attn_sinks[0]

    (h1, q, kv, gu, gv, xb), (w_dn_g,) = _fwd_in(xs, modr, w_in_f, b_in, tm_big, [w_dn_s], ["blk"])
    w_out_f = _insert_own(w_out_g, w_out_s, "blk", chip).reshape(D_MODEL, D_MODEL)
    (x1, x1b, y, mixed), (w_gu_g,) = _fwd_mix(
        q, kv, gu, gv, xs, modr, bias, sinks, gmlp_ln_g, gmlp_ln_b, wsm, bsx, amat, attn_out_g, gmlp_out_g, w_out_f,
        ln1_g, ln1_b, tm_big, [w_gu_s], ["blk"])
    assert w_gate_up.shape[2] == FF_CHUNK
    w_gu_f = _insert_own(w_gu_g, w_gu_s, "blk", chip)
    w_dn_f = _insert_own(w_dn_g, w_dn_s, "blk", chip).reshape(D_FF, D_MODEL)
    h2, act, dy2, dx1a, acc_f = _fwd_ffn(x1, tgt, modr, ln2_g, ln2_b, w_gu_f, w_dn_f, tm_ffn)

    a_act, dgu_ff, dh2 = _bwd_ffn(dy2, act, w_gu_f, w_dn_f, min(FFN_SUB, s))
    g_dn, g_dn_b, _ = _wgrad(a_act, [dy2], D_FF // 2, min(1024, s), "wgrad_down")
    g_gu, g_gu_b, _ = _wgrad(h2, [dgu_ff], 512, min(512, s), "wgrad_gate_up")
    blk3 = lambda a, rows: a.reshape(N_CHIPS, rows, a.shape[1])
    (dxa, dy, dmix, acc_m), (got_dn, got_gu) = _bwd_mid(
        dh2, dx1a, x1b, xs, y, modr, ln1_g, w_out_f, tm_big, [blk3(g_dn_b, D_FF // N_CHIPS), g_gu_b], ["blk", "cols"])
    g_out, g_out_b, _ = _wgrad(mixed, [dy], 512, min(2048, s), "wgrad_out")
    (got_out,) = _swap_halves([blk3(g_out_b, D_MODEL // N_CHIPS)], ["blk"], "rs_swap_out")
    kinds_a = ["blk", "cols", "blk"]
    fulls_a = [blk3(g_dn, D_FF // N_CHIPS), g_gu, blk3(g_out, D_MODEL // N_CHIPS)]
    gots_a = [got_dn, got_gu, got_out]
    parts_a = [_add_halves(f, g, k, "rs_add_a%d" % i) for i, (f, g, k) in enumerate(zip(fulls_a, gots_a, kinds_a))]
    (dq, dkv, dgu, dgv, gbias, dws, dbs, vec, dsink), rxs_a = _bwd_mix(
        q, kv, gu, gv, dmix, bias, sinks, gmlp_ln_g, gmlp_ln_b, wsm, bsx, amat, attn_out_g, gmlp_out_g,
        [p[1] for p in parts_a], kinds_a)
    tall_g = _mix_finalize(gbias, bucket, dws, dbs, dsink)
    grad_x, acc_i, db_in = _bwd_in(dq, dkv, dgu, dgv, dxa, xb, modr, w_in_f, tm_big)

    wide_g = _pack_wide(acc_i, acc_m, acc_f, db_in, vec)
    full_in, full_in_b, (gw, gt) = _wgrad(h1, [dq, dkv, dgu, dgv], 512, min(1024, s), "wgrad_in", owner_blocks=True,
                                          gather_vs=[wide_g, tall_g])
    (got_in,) = _swap_halves([full_in_b], ["blk"], "rs_swap_in")
    part_in = _add_halves(full_in, got_in, "blk", "rs_add_in")
    in_send, in_recv, in_part, in_rx, token = _exchange_start(part_in[1], "rs_chips_in_start")
    wide_wmv ={"b_ada": (b_ada, m_b_ada, v_b_ada), "b_in": (b_in, m_b_in, v_b_in),
                "ln1_g": (ln1_g, m_ln1_g, v_ln1_g), "ln1_b": (ln1_b, m_ln1_b, v_ln1_b),
                "ln2_g": (ln2_g, m_ln2_g, v_ln2_g), "ln2_b": (ln2_b, m_ln2_b, v_ln2_b),
                "gmlp_ln_g": (gmlp_ln_g, m_gmlp_ln_g, v_gmlp_ln_g), "gmlp_ln_b": (gmlp_ln_b, m_gmlp_ln_b, v_gmlp_ln_b),
                "attn_out_g": (attn_out_g, m_attn_out_g, v_attn_out_g),
                "gmlp_out_g": (gmlp_out_g, m_gmlp_out_g, v_gmlp_out_g)}
    rows2 = lambda a: a.reshape(-1, a.shape[-1])
    dmod_all, loss_t, small = _adam_small(
        gw, gt, wide_wmv, tuple(rows2(a) for a in (gmlp_w_s, m_gmlp_w_s, v_gmlp_w_s)),
        tuple(rows2(a) for a in (gmlp_b_s, m_gmlp_b_s, v_gmlp_b_s)), (rel_bias, m_rel_bias, v_rel_bias),
        (attn_sinks, m_attn_sinks, v_attn_sinks), token)
    loss = loss_t[0, 0]

    dmod_cols = lax.dynamic_slice_in_dim(dmod_all, chip * n_ada, n_ada, axis=1)
    g_ada, d_ada, m_ada, v_ada = _adam_w_ada(sc_all.T, dmod_cols, w_ada[0], m_w_ada[0], v_w_ada[0], 256)

    sums = [(parts_a[0][0], rxs_a[0], 176), (parts_a[1][0], rxs_a[1], 256), (parts_a[2][0], rxs_a[2], 128)]
    mine = [_sum_chips(p, rx, tr, "rs_sum_%d" % i, loss_t) for i, (p, rx, tr) in enumerate(sums)]
    got = _share_halves(mine, "rs_share")
    gs_dn, d_dn, m_dn, v_dn = _adam_halves(w_down[0], mine[0], got[0], m_w_down[0], v_w_down[0], 176, "adam_w_down")
    gs_gu, d_gu, m_gu, v_gu = _adam_halves(w_gate_up[0], mine[1], got[1], m_w_gate_up[0], v_w_gate_up[0], 256,
                                           "adam_w_gate_up")
    gs_out, d_out, m_out, v_out = _adam_halves(w_out[0], mine[2], got[2], m_w_out[0], v_w_out[0], 128, "adam_w_out")

    rx_in = _exchange_wait(in_send, in_recv, in_part, in_rx, d_gu, "rs_chips_in_wait")
    mine_in = _sum_chips(part_in[0], rx_in, 256, "rs_sum_in", rx_in)
    (got_in_half,) = _share_halves([mine_in], "rs_share_in")
    gs_in, d_in, m_in, v_in = _adam_halves(w_in[0], mine_in, got_in_half, m_w_in[0], v_w_in[0], 256, "adam_w_in")

    big = {"w_ada": (g_ada, d_ada, m_ada, v_ada), "w_in": (gs_in, d_in, m_in, v_in), "w_out": (gs_out, d_out, m_out, v_out),
           "w_gate_up": (gs_gu, d_gu, m_gu, v_gu), "w_down": (gs_dn, d_dn, m_dn, v_dn)}
    order = ["rel_bias", "w_ada", "b_ada", "w_in", "b_in", "attn_sinks", "gmlp_ln_g", "gmlp_ln_b", "gmlp_w_s", "gmlp_b_s",
             "attn_out_g", "gmlp_out_g", "w_out", "ln1_g", "ln1_b", "w_gate_up", "w_down", "ln2_g", "ln2_b"]
    shapes = {"gmlp_w_s": gmlp_w_s.shape, "gmlp_b_s": gmlp_b_s.shape}
    outs = [loss, grad_x[None]]
    for k in range(4):
        for name in order:
            if name in big:
                outs.append(big[name][k][None])
            elif name in shapes:
                outs.append(small[name][k].reshape(shapes[name]))
            else:
                outs.append(small[name][k])
    return tuple(outs)
```

```python
import math

import numpy as np
import jax
import jax.numpy as jnp
from jax import lax
from jax.experimental import pallas as pl
from jax.experimental.pallas import tpu as pltpu

F32 = jnp.float32
BF16 = jnp.bfloat16
MESH = pl.DeviceIdType.MESH

D_MODEL = 1024
N_HEADS = 8
N_KV = 2
HEAD_DIM = 64
ATTN_W = N_HEADS * HEAD_DIM
KV_W = N_KV * HEAD_DIM
N_GROUPS = 8
GROUP_DIM = 64
GMLP_W = N_GROUPS * GROUP_DIM
IN_W = ATTN_W + 2 * KV_W + 2 * GMLP_W
BLOCK = 128
N_BUCKETS = 32
MAX_DISTANCE = 128
D_FF = 2816
ALPHA = 2.0 ** 0.25
LN_EPS = 1e-5
NEG_INF = -1e30
ADAM_LR, ADAM_B1, ADAM_B2, ADAM_EPS, ADAM_WD, ADAM_STEP = 0.001, 0.9, 0.999, 1e-8, 0.01, 10
N_CHIPS = 4
N_DEV = 8
LANES = 128
V7X_VMEM_LIMIT = 56 * 2 ** 20
GELU_C = math.sqrt(2.0 / math.pi)
Q_SCALE = HEAD_DIM ** -0.5
ANY = pl.BlockSpec(memory_space=pl.ANY)

TALL_BS = N_GROUPS * BLOCK
TALL_RB = TALL_BS + 8
TALL_SK = TALL_RB + N_BUCKETS
TALL_ROWS = TALL_SK + 8
WIDE_W = 6 * D_MODEL
WIDE_LAYOUT = {
    "b_ada": (0, 0, 6 * D_MODEL),
    "b_in": (1, 0, IN_W), "ln1_g": (1, IN_W, D_MODEL), "ln1_b": (1, IN_W + D_MODEL, D_MODEL),
    "ln2_g": (1, IN_W + 2 * D_MODEL, D_MODEL), "ln2_b": (1, IN_W + 3 * D_MODEL, D_MODEL),
    "gmlp_ln_g": (2, 0, GMLP_W), "gmlp_ln_b": (2, GMLP_W, GMLP_W), "attn_out_g": (2, 2 * GMLP_W, ATTN_W),
    "gmlp_out_g": (2, 2 * GMLP_W + ATTN_W, GMLP_W), "loss": (2, 3 * GMLP_W + ATTN_W, D_MODEL)}
WIDE_PARAMS = tuple(n for n in WIDE_LAYOUT if n != "loss")


def _params(sem=None):
    return pltpu.CompilerParams(dimension_semantics=sem, vmem_limit_bytes=V7X_VMEM_LIMIT)


def _const_spec(shape, single=False):
    nd = len(shape)
    if single:
        return pl.BlockSpec(shape, lambda *_: (0,) * nd, pipeline_mode=pl.Buffered(1))
    return pl.BlockSpec(shape, lambda *_: (0,) * nd)


def _dot(a, b):
    return jnp.dot(a, b, preferred_element_type=F32)


def _dot_nt(a, b):
    return lax.dot_general(a, b, (((1,), (1,)), ((), ())), preferred_element_type=F32)


def _dot_tn(a, b):
    return lax.dot_general(a, b, (((0,), (0,)), ((), ())), preferred_element_type=F32)


def _gelu(x):
    t = jnp.tanh(GELU_C * (x + 0.044715 * x * x * x))
    return 0.5 * x * (1.0 + t), t


def _gelu_grad(x, t):
    return 0.5 * (1.0 + t) + 0.5 * x * (1.0 - t * t) * GELU_C * (1.0 + 3.0 * 0.044715 * x * x)


def _split_dot(x, a):
    hi = x.astype(BF16)
    lo = (x - hi.astype(F32)).astype(BF16)
    return _dot(hi, a) + _dot(lo, a)


def _group_mean_matrix():
    g = np.arange(GMLP_W) // GROUP_DIM
    return jnp.asarray((g[:, None] == g[None, :]).astype(np.float32) / GROUP_DIM, dtype=BF16)


def _ln_stats(z):
    mu = jnp.mean(z, axis=-1, keepdims=True)
    d = z - mu
    var = jnp.mean(d * d, axis=-1, keepdims=True)
    rstd = lax.rsqrt(var + LN_EPS)
    return d * rstd, rstd


def _ln_bwd(dxhat, xhat, rstd):
    m1 = jnp.mean(dxhat, axis=-1, keepdims=True)
    m2 = jnp.mean(dxhat * xhat, axis=-1, keepdims=True)
    return rstd * (dxhat - m1 - xhat * m2)


def _colsum(x):
    return jnp.sum(x, axis=0, keepdims=True)


def _my_pos():
    return lax.axis_index("x"), lax.axis_index("y"), lax.axis_index("c")


def _other_chips(x, y):
    return [(1 - x, y), (x, 1 - y), (1 - x, 1 - y)]


def _chip_index_scalar():
    ix, iy, _ = _my_pos()
    return jnp.reshape(2 * ix + iy, (1,)).astype(jnp.int32)


def _core_index_scalar():
    return jnp.reshape(lax.axis_index("c"), (1,)).astype(jnp.int32)


class _Gather8:
    def __init__(self, x_refs, out_refs, send_sems, recv_sems, local_sems):
        self.x_refs, self.out_refs = x_refs, out_refs
        self.send_sems, self.recv_sems, self.local_sems = send_sems, recv_sems, local_sems
        self.x, self.y, self.c = _my_pos()
        self.me, self.sibling = (self.x, self.y, self.c), (self.x, self.y, 1 - self.c)
        self.chips = _other_chips(self.x, self.y)

    def _rows(self, a, px, py, pc):
        m_per = self.x_refs[a].shape[0]
        return self.out_refs[a].at[pl.ds((4 * px + 2 * py + pc) * m_per, m_per), :]

    def _copy(self, a, k, block, to, src=None):
        return pltpu.make_async_remote_copy(
            src_ref=self._rows(a, *block) if src is None else src, dst_ref=self._rows(a, *block),
            send_sem=self.send_sems.at[7 * a + k], recv_sem=self.recv_sems.at[7 * a + k], device_id=to,
            device_id_type=MESH)

    def _local(self, a):
        return pltpu.make_async_copy(self.x_refs[a], self._rows(a, *self.me), self.local_sems.at[a])

    def start(self):
        for a in range(len(self.x_refs)):
            self._local(a).start()
            self._copy(a, 0, self.me, self.sibling, src=self.x_refs[a]).start()
            for j, chip in enumerate(self.chips):
                self._copy(a, 1 + j, self.me, (*chip, self.c), src=self.x_refs[a]).start()

    def forward(self):
        for a in range(len(self.x_refs)):
            for j, chip in enumerate(self.chips):
                self._copy(a, 1 + j, (*chip, self.c), self.me).wait_recv()
                self._copy(a, 4 + j, (*chip, self.c), self.sibling).start()

    def finish(self):
        for a in range(len(self.x_refs)):
            self._copy(a, 0, self.sibling, self.me).wait_recv()
            for j, chip in enumerate(self.chips):
                self._copy(a, 4 + j, (*chip, 1 - self.c), self.me).wait_recv()
        for a in range(len(self.x_refs)):
            for k in range(7):
                self._copy(a, k, self.me, self.me).wait_send()
            self._local(a).wait()

    @staticmethod
    def sems(n_v):
        return [pltpu.SemaphoreType.DMA((7 * n_v,)), pltpu.SemaphoreType.DMA((7 * n_v,)),
                pltpu.SemaphoreType.DMA((n_v,))]


def _gathered8_shapes(vs):
    return [jax.ShapeDtypeStruct((N_DEV * v.shape[0], v.shape[1]), v.dtype) for v in vs]


VMEM_WHOLE = pl.BlockSpec(memory_space=pltpu.VMEM)


def _prologue(c_pad, w_ada_s, b_ada_s, shards):
    n = w_ada_s.shape[1]
    n_w = len(shards)

    def body(c_ref, w_ref, b_ref, *rest):
        shard_refs = rest[:n_w]
        sc_ref, modc_ref, modg_ref = rest[n_w:n_w + 3]
        gathered_refs = rest[n_w + 3:2 * n_w + 3]
        call_ref, w_vmem = rest[2 * n_w + 3:2 * n_w + 5]
        sems = rest[2 * n_w + 5:]
        weights = _WeightGather(shard_refs, gathered_refs, ["blk"] * n_w, sems[0], sems[1])
        gather_c = _Gather8([c_ref], [call_ref], sems[2], sems[3], sems[4])
        gather_mod = _Gather8([modc_ref], [modg_ref], sems[5], sems[6], sems[7])
        load_w = pltpu.make_async_copy(w_ref, w_vmem, sems[8])
        weights.start()
        gather_c.start()
        load_w.start()
        gather_c.forward()
        gather_c.finish()
        cv = call_ref[...]
        sc = cv * _sigmoid(cv)
        a_hi = sc.astype(BF16)
        a_lo = (sc - a_hi.astype(F32)).astype(BF16)
        load_w.wait()
        w = w_vmem[...]
        w_hi = w.astype(BF16)
        w_lo = (w - w_hi.astype(F32)).astype(BF16)
        mod = _dot(a_hi, w_hi) + _dot(a_hi, w_lo) + _dot(a_lo, w_hi) + b_ref[...]
        for d in range(N_DEV):
            sc_ref[d:d + 1, :] = sc[8 * d:8 * d + 1, :]
            modc_ref[d:d + 1, :] = mod[8 * d:8 * d + 1, :]
        gather_mod.start()
        weights.forward()
        gather_mod.forward()
        gather_mod.finish()
        weights.forward_diagonal()
        weights.finish()

    outs = pl.pallas_call(
        body, name="prologue",
        out_shape=[jax.ShapeDtypeStruct((N_DEV, D_MODEL), F32), jax.ShapeDtypeStruct((N_DEV, n), F32),
                   jax.ShapeDtypeStruct((N_DEV * N_DEV, n), F32)]
        + [jax.ShapeDtypeStruct(_gathered_shape(sh, "blk"), BF16) for sh in shards],
        in_specs=[VMEM_WHOLE, ANY, VMEM_WHOLE] + [ANY] * n_w,
        out_specs=[VMEM_WHOLE, VMEM_WHOLE, VMEM_WHOLE] + [ANY] * n_w,
        scratch_shapes=[pltpu.VMEM((N_DEV * 8, D_MODEL), F32), pltpu.VMEM(w_ada_s.shape, F32)]
        + _WeightGather.sems(n_w) + _Gather8.sems(1) + _Gather8.sems(1) + [pltpu.SemaphoreType.DMA],
        compiler_params=pltpu.CompilerParams(vmem_limit_bytes=V7X_VMEM_LIMIT),
    )(c_pad, w_ada_s, b_ada_s, *shards)
    return outs[0], outs[2], outs[3:]


def _gathered_shape(shard, kind):
    r, cc = shard.shape
    return (N_CHIPS, r, cc) if kind == "blk" else (r, N_CHIPS * cc)


class _WeightGather:
    N_SEM = 8

    def __init__(self, shards, gathered, kinds, send_sems, recv_sems):
        self.shards, self.gathered, self.kinds = shards, gathered, kinds
        self.send_sems, self.recv_sems = send_sems, recv_sems
        self.x, self.y, self.c = _my_pos()
        self.me, self.sibling = (self.x, self.y, self.c), (self.x, self.y, 1 - self.c)
        self.nbr = ((1 - self.x, self.y), (self.x, 1 - self.y))
        self.diag = 2 * (1 - self.x) + (1 - self.y)

    def _dst(self, a, chip, pc, quarter=None):
        r, cc = self.shards[a].shape
        h = r // 2
        row0, rows = pc * h, h
        if quarter is not None:
            row0, rows = pc * h + quarter * (h // 2), h // 2
        g = self.gathered[a]
        if self.kinds[a] == "blk":
            return g.at[chip, pl.ds(row0, rows), :]
        return g.at[pl.ds(row0, rows), pl.ds(chip * cc, cc)]

    def _copy(self, a, k, region, to, src=None):
        return pltpu.make_async_remote_copy(
            src_ref=region if src is None else src, dst_ref=region, send_sem=self.send_sems.at[a * self.N_SEM + k],
            recv_sem=self.recv_sems.at[a * self.N_SEM + k], device_id=to, device_id_type=MESH)

    def _arrays(self):
        return range(len(self.shards))

    def start(self):
        my_chip = 2 * self.x + self.y
        for a in self._arrays():
            h = self.shards[a].shape[0] // 2
            mine = self.shards[a].at[pl.ds(self.c * h, h), :]
            for j, chip in enumerate(self.nbr):
                self._copy(a, j, self._dst(a, my_chip, self.c), (*chip, self.c), src=mine).start()

    def forward(self):
        for a in self._arrays():
            for j, chip in enumerate(self.nbr):
                cj = 2 * chip[0] + chip[1]
                half = self._dst(a, cj, self.c)
                self._copy(a, j, half, self.me).wait_recv()
                self._copy(a, 2 + j, half, self.sibling).start()
                other = self.nbr[1 - j]
                self._copy(a, 4 + j, self._dst(a, cj, self.c, quarter=j), (*other, self.c)).start()

    def forward_diagonal(self):
        for a in self._arrays():
            for j in range(2):
                quarter = self._dst(a, self.diag, self.c, quarter=j)
                self._copy(a, 4 + j, quarter, self.me).wait_recv()
                self._copy(a, 6 + j, quarter, self.sibling).start()

    def finish(self):
        for a in self._arrays():
            for j, chip in enumerate(self.nbr):
                self._copy(a, 2 + j, self._dst(a, 2 * chip[0] + chip[1], 1 - self.c), self.me).wait_recv()
                self._copy(a, 6 + j, self._dst(a, self.diag, 1 - self.c, quarter=j), self.me).wait_recv()
        for a in self._arrays():
            half = self._dst(a, self.diag, self.c)
            quarter = self._dst(a, self.diag, self.c, quarter=0)
            for k in range(self.N_SEM):
                self._copy(a, k, half if k < 4 else quarter, self.me).wait_send()

    @classmethod
    def sems(cls, n_arr):
        return [pltpu.SemaphoreType.DMA((n_arr * cls.N_SEM,)), pltpu.SemaphoreType.DMA((n_arr * cls.N_SEM,))]


def _insert_own(gathered, shard, kind, chip):
    if kind == "blk":
        return lax.dynamic_update_slice(gathered, shard[None], (chip, 0, 0))
    return lax.dynamic_update_slice(gathered, shard, (0, chip * shard.shape[1]))


def _half_of_full(ref, kind, pc):
    if kind == "blk":
        h = ref.shape[1] // 2
        return ref.at[:, pl.ds(pc * h, h), :]
    h = ref.shape[0] // 2
    return ref.at[pl.ds(pc * h, h), :]


def _half_shape(shape, kind):
    return (shape[0], shape[1] // 2, shape[2]) if kind == "blk" else (shape[0] // 2, shape[1])


class _HalfSwap:
    def __init__(self, ins, outs, kinds, send_sems, recv_sems):
        self.ins, self.outs, self.kinds = ins, outs, kinds
        self.send_sems, self.recv_sems = send_sems, recv_sems
        self.x, self.y, self.c = _my_pos()

    def _copies(self):
        for a in range(len(self.ins)):
            yield pltpu.make_async_remote_copy(
                src_ref=_half_of_full(self.ins[a], self.kinds[a], 1 - self.c), dst_ref=self.outs[a],
                send_sem=self.send_sems.at[a], recv_sem=self.recv_sems.at[a],
                device_id=(self.x, self.y, 1 - self.c), device_id_type=MESH)

    def start(self):
        for cp in self._copies():
            cp.start()

    def wait(self):
        for cp in self._copies():
            cp.wait()

    @staticmethod
    def sems(n_arr):
        return [pltpu.SemaphoreType.DMA((n_arr,)), pltpu.SemaphoreType.DMA((n_arr,))]

    @staticmethod
    def out_shapes(fulls, kinds):
        return [jax.ShapeDtypeStruct(_half_shape(a.shape, k), a.dtype) for a, k in zip(fulls, kinds)]


def _swap_halves(fulls_bf16, kinds, name):
    n_arr = len(fulls_bf16)

    def body(*refs):
        swap = _HalfSwap(refs[:n_arr], refs[n_arr:2 * n_arr], kinds, *refs[2 * n_arr:])
        swap.start()
        swap.wait()

    return pl.pallas_call(
        body, name=name, out_shape=_HalfSwap.out_shapes(fulls_bf16, kinds),
        in_specs=[ANY] * n_arr, out_specs=[ANY] * n_arr, scratch_shapes=_HalfSwap.sems(n_arr),
    )(*fulls_bf16)


def _add_halves(full, got, kind, name):
    hs = _half_shape(full.shape, kind)

    def body(pos_ref, a_ref, b_ref, o_ref, ob_ref):
        p = a_ref[...] + b_ref[...].astype(F32)
        ob_ref[...] = p.astype(BF16)

        @pl.when(pl.program_id(0) == pos_ref[1])
        def _():
            o_ref[...] = p.reshape(o_ref.shape)

    if kind == "blk":
        nb, h, cc = hs
        own = pl.BlockSpec((1, h, cc), lambda b, pos_ref: (b, pos_ref[0], 0))
        other = pl.BlockSpec((1, h, cc), lambda b, pos_ref: (b, 0, 0))
    else:
        h, cc = hs[0], hs[1] // N_CHIPS
        own = pl.BlockSpec((h, cc), lambda b, pos_ref: (pos_ref[0], b))
        other = pl.BlockSpec((h, cc), lambda b, pos_ref: (0, b))
    pos = jnp.concatenate([_core_index_scalar(), _chip_index_scalar()])
    return pl.pallas_call(
        body, name=name, out_shape=(jax.ShapeDtypeStruct((h, cc), F32), jax.ShapeDtypeStruct(hs, BF16)),
        grid_spec=pltpu.PrefetchScalarGridSpec(
            num_scalar_prefetch=1, grid=(N_CHIPS,), in_specs=[own, other],
            out_specs=(pl.BlockSpec((h, cc), lambda b, pos_ref: (0, 0)), other)),
        compiler_params=_params(("arbitrary",)),
    )(pos, full, got)


def _rx_shape(part_shape, kind):
    if kind == "blk":
        return (3, part_shape[1], part_shape[2])
    return (3, part_shape[0], part_shape[1] // N_CHIPS)


class _ChipExchange:
    def __init__(self, parts, rxs, kinds, send_sems, recv_sems):
        self.parts, self.rxs, self.kinds = parts, rxs, kinds
        self.send_sems, self.recv_sems = send_sems, recv_sems
        self.x, self.y, self.c = _my_pos()
        self.chips = _other_chips(self.x, self.y)

    def _copies(self):
        for a in range(len(self.parts)):
            for j, chip in enumerate(self.chips):
                cj = 2 * chip[0] + chip[1]
                if self.kinds[a] == "blk":
                    src = self.parts[a].at[cj]
                else:
                    cc = self.parts[a].shape[1] // N_CHIPS
                    src = self.parts[a].at[:, pl.ds(cj * cc, cc)]
                yield pltpu.make_async_remote_copy(
                    src_ref=src, dst_ref=self.rxs[a].at[j], send_sem=self.send_sems.at[a * 3 + j],
                    recv_sem=self.recv_sems.at[a * 3 + j], device_id=(*chip, self.c), device_id_type=MESH)

    def start(self):
        for cp in self._copies():
            cp.start()

    def wait(self):
        for cp in self._copies():
            cp.wait_recv()
        for cp in self._copies():
            cp.wait_send()

    @staticmethod
    def sems(n_arr):
        return [pltpu.SemaphoreType.DMA((n_arr * 3,)), pltpu.SemaphoreType.DMA((n_arr * 3,))]


HBM_SPEC = pl.BlockSpec(memory_space=pltpu.HBM)
SEM_SPEC = pl.BlockSpec(memory_space=pltpu.SEMAPHORE)
DATAFLOW = pltpu.SideEffectType.DATAFLOW_SIDE_EFFECTING


class _SplitCopy:
    def __init__(self, copier_of, srcs, lands, n_sems, name):
        self.copier_of, self.name = copier_of, name
        n_s, n_l = len(srcs), len(lands)
        self.n_s, self.n_l = n_s, n_l

        def body(*refs):
            copier_of(refs[:n_s], refs[n_s:n_s + n_l], refs[n_s + n_l], refs[n_s + n_l + 1]).start()
            refs[-1][...] = jnp.zeros_like(refs[-1])

        sems = pltpu.SemaphoreType.DMA((n_sems,))
        outs = pl.pallas_call(
            body, name=name + "_start",
            out_shape=[sems, sems] + [pltpu.HBM(a.shape, a.dtype) for a in srcs]
            + [pltpu.HBM(l.shape, l.dtype) for l in lands] + [jax.ShapeDtypeStruct((8, LANES), F32)],
            in_specs=[HBM_SPEC] * (n_s + n_l), out_specs=[SEM_SPEC, SEM_SPEC] + [HBM_SPEC] * (n_s + n_l) + [VMEM_WHOLE],
            input_output_aliases={i: 2 + i for i in range(n_s + n_l)},
            compiler_params=pltpu.CompilerParams(has_side_effects=DATAFLOW),
        )(*[pltpu.with_memory_space_constraint(a, pltpu.HBM) for a in srcs],
          *[pltpu.with_memory_space_constraint(lax.empty(l.shape, l.dtype), pltpu.HBM) for l in lands])
        self.sems, self.thru, self.token = outs[:2], outs[2:-1], outs[-1]

    def wait(self, after):
        n_s, n_l, copier_of = self.n_s, self.n_l, self.copier_of

        def body(*refs):
            copier_of(refs[:n_s], refs[n_s:n_s + n_l], refs[n_s + n_l], refs[n_s + n_l + 1]).wait()

        outs = pl.pallas_call(
            body, name=self.name + "_wait", out_shape=[pltpu.HBM(t.shape, t.dtype) for t in self.thru],
            in_specs=[HBM_SPEC] * (n_s + n_l) + [SEM_SPEC, SEM_SPEC, ANY], out_specs=[HBM_SPEC] * (n_s + n_l),
            input_output_aliases={i: i for i in range(n_s + n_l)},
            compiler_params=pltpu.CompilerParams(has_side_effects=DATAFLOW),
        )(*self.thru, *self.sems, after)
        return outs[n_s:]


class _SiblingCopy:
    def __init__(self, srcs, lands, send_sems, recv_sems):
        self.srcs, self.lands, self.send_sems, self.recv_sems = srcs, lands, send_sems, recv_sems
        self.x, self.y, self.c = _my_pos()

    def _copies(self):
        for a in range(len(self.srcs)):
            yield pltpu.make_async_remote_copy(
                src_ref=self.srcs[a], dst_ref=self.lands[a], send_sem=self.send_sems.at[a],
                recv_sem=self.recv_sems.at[a], device_id=(self.x, self.y, 1 - self.c), device_id_type=MESH)

    def start(self):
        for cp in self._copies():
            cp.start()

    def wait(self):
        for cp in self._copies():
            cp.wait()


def _sum_chips(part, rx, tr, name, after):
    _, h, cc = rx.shape
    flips = (2, 1, 3)

    def body(chip_ref, p_ref, rx_ref, after_ref, o_ref):
        own = p_ref[...]
        for mc in range(N_CHIPS):
            @pl.when(chip_ref[0] == mc)
            def _():
                terms = sorted([(mc, None)] + [(mc ^ f, j) for j, f in enumerate(flips)])
                acc = None
                for _, j in terms:
                    t = own if j is None else rx_ref[j].astype(F32)
                    acc = t if acc is None else acc + t
                o_ref[...] = acc

    return pl.pallas_call(
        body, name=name, out_shape=jax.ShapeDtypeStruct((h, cc), F32),
        grid_spec=pltpu.PrefetchScalarGridSpec(
            num_scalar_prefetch=1, grid=(h // tr,),
            in_specs=[pl.BlockSpec((tr, cc), lambda i, chip_ref: (i, 0)),
                      pl.BlockSpec((3, tr, cc), lambda i, chip_ref: (0, i, 0)), ANY],
            out_specs=pl.BlockSpec((tr, cc), lambda i, chip_ref: (i, 0))),
        compiler_params=_params(("arbitrary",)),
    )(_chip_index_scalar(), part, rx, after)


def _share_halves(halves, name):
    n_arr = len(halves)

    def body(*refs):
        ins, outs = refs[:n_arr], refs[n_arr:2 * n_arr]
        send_sems, recv_sems = refs[2 * n_arr:]
        x, y, c = _my_pos()
        cps = []
        for a in range(n_arr):
            cp = pltpu.make_async_remote_copy(
                src_ref=ins[a], dst_ref=outs[a], send_sem=send_sems.at[a], recv_sem=recv_sems.at[a],
                device_id=(x, y, 1 - c), device_id_type=MESH)
            cp.start()
            cps.append(cp)
        for cp in cps:
            cp.wait()

    return pl.pallas_call(
        body, name=name, out_shape=[jax.ShapeDtypeStruct(h.shape, h.dtype) for h in halves],
        in_specs=[ANY] * n_arr, out_specs=[ANY] * n_arr,
        scratch_shapes=[pltpu.SemaphoreType.DMA((n_arr,)), pltpu.SemaphoreType.DMA((n_arr,))],
    )(*halves)


def _bucket_table():
    qi = jnp.arange(BLOCK)[:, None]
    si = jnp.arange(2 * BLOCK)[None, :]
    dist = qi + BLOCK - si
    max_exact = N_BUCKETS // 2
    n = jnp.maximum(dist, 0)
    nf = jnp.maximum(n, max_exact).astype(F32)
    large = max_exact + (jnp.log(nf / max_exact) / math.log(MAX_DISTANCE / max_exact)
                         * (N_BUCKETS - max_exact)).astype(jnp.int32)
    large = jnp.minimum(large, N_BUCKETS - 1)
    return jnp.where(n < max_exact, n, large).astype(F32)


def _prep_tables(bucket, rel_bias, w_s):
    def body(bucket_ref, rb_ref, ws_ref, bias_ref, wsm_ref):
        qi = lax.broadcasted_iota(jnp.int32, (BLOCK, 2 * BLOCK), 0)
        si = lax.broadcasted_iota(jnp.int32, (BLOCK, 2 * BLOCK), 1)
        dist = qi + BLOCK - si
        in_window = (dist >= 0) & (dist < BLOCK)
        bk = bucket_ref[...]
        for h in range(N_HEADS):
            acc = jnp.zeros((BLOCK, 2 * BLOCK), F32)
            for b in range(N_BUCKETS):
                acc = jnp.where(bk == float(b), rb_ref[b, h], acc)
            bias_ref[h] = jnp.where(in_window, acc, NEG_INF)
        ti = lax.broadcasted_iota(jnp.int32, (BLOCK, BLOCK), 0)
        ui = lax.broadcasted_iota(jnp.int32, (BLOCK, BLOCK), 1)
        for g in range(N_GROUPS):
            wsm_ref[g] = jnp.where(ti >= ui, ws_ref[g], 0.0).astype(BF16)

    return pl.pallas_call(
        body, name="prep_tables",
        out_shape=(jax.ShapeDtypeStruct((N_HEADS, BLOCK, 2 * BLOCK), F32),
                   jax.ShapeDtypeStruct((N_GROUPS, BLOCK, BLOCK), BF16)),
        grid=(1,),
        in_specs=[_const_spec((BLOCK, 2 * BLOCK)), pl.BlockSpec(memory_space=pltpu.SMEM),
                  _const_spec((N_GROUPS, BLOCK, BLOCK))],
        out_specs=(_const_spec((N_HEADS, BLOCK, 2 * BLOCK)), _const_spec((N_GROUPS, BLOCK, BLOCK))),
        compiler_params=_params(("arbitrary",)),
    )(bucket, rel_bias, w_s)


def _fwd_in(x, modr, w_in, b_in, tm, shards, kinds):
    s = x.shape[0]
    n_steps = s // tm
    fwd_step, diag_step = (8 * n_steps) // 16, (13 * n_steps) // 16
    n_w = len(shards)

    def body(x_ref, mod_ref, w_ref, b_ref, *rest):
        shard_refs = rest[:n_w]
        h1_ref, q_ref, kv_ref, gu_ref, gv_ref, xb_ref = rest[n_w:n_w + 6]
        gathered_refs = rest[n_w + 6:2 * n_w + 6]
        send_sems, recv_sems = rest[2 * n_w + 6:]
        i = pl.program_id(0)
        gather = _WeightGather(shard_refs, gathered_refs, kinds, send_sems, recv_sems)

        @pl.when(i == 0)
        def _():
            gather.start()

        xv = x_ref[...]
        xb_ref[...] = xv.astype(BF16)
        h1 = (xv * (1.0 + mod_ref[1:2, :]) + mod_ref[0:1, :]).astype(BF16)
        h1_ref[...] = h1
        proj = jnp.concatenate([_dot(h1, w_ref[j]) for j in range(N_CHIPS)], axis=1) + b_ref[...]
        q_ref[...] = (proj[:, :ATTN_W] * Q_SCALE).astype(BF16)
        kv_ref[...] = proj[:, ATTN_W:ATTN_W + 2 * KV_W].astype(BF16)
        gu_ref[...] = proj[:, ATTN_W + 2 * KV_W:ATTN_W + 2 * KV_W + GMLP_W]
        gv_ref[...] = proj[:, ATTN_W + 2 * KV_W + GMLP_W:]

        @pl.when(i == fwd_step)
        def _():
            gather.forward()

        @pl.when(i == diag_step)
        def _():
            gather.forward_diagonal()

        @pl.when(i == n_steps - 1)
        def _():
            gather.finish()

    row = lambda w: pl.BlockSpec((tm, w), lambda i: (i, 0))
    outs = pl.pallas_call(
        body, name="fwd_in",
        out_shape=[jax.ShapeDtypeStruct((s, D_MODEL), BF16), jax.ShapeDtypeStruct((s, ATTN_W), BF16),
                   jax.ShapeDtypeStruct((s, 2 * KV_W), BF16), jax.ShapeDtypeStruct((s, GMLP_W), F32),
                   jax.ShapeDtypeStruct((s, GMLP_W), F32), jax.ShapeDtypeStruct((s, D_MODEL), BF16)]
        + [jax.ShapeDtypeStruct(_gathered_shape(sh, k), BF16) for sh, k in zip(shards, kinds)],
        grid=(n_steps,),
        in_specs=[row(D_MODEL), _const_spec((8, D_MODEL)), _const_spec(w_in.shape), _const_spec((1, IN_W))]
        + [ANY] * n_w,
        out_specs=[row(D_MODEL), row(ATTN_W), row(2 * KV_W), row(GMLP_W), row(GMLP_W), row(D_MODEL)] + [ANY] * n_w,
        scratch_shapes=_WeightGather.sems(n_w),
        compiler_params=_params(("arbitrary",)),
    )(x, modr, w_in, b_in, *shards)
    return outs[:6], outs[6:]


def _kv_variants(kk):
    kf = kk.astype(F32)
    lane = lax.broadcasted_iota(jnp.int32, kf.shape, 1)
    low = lane < HEAD_DIM
    k0_lo = jnp.where(low, kf, 0.0)
    k1_hi = jnp.where(low, 0.0, kf)
    k0_hi = pltpu.roll(k0_lo, HEAD_DIM, 1)
    k1_lo = pltpu.roll(k1_hi, HEAD_DIM, 1)
    return ((k0_lo.astype(BF16), k0_hi.astype(BF16)), (k1_lo.astype(BF16), k1_hi.astype(BF16)))


def _head_kv(h):
    return h // (N_HEADS // N_KV), h % 2


MIX_GROUP = 2


def _interleave(*gens):
    results = [None] * len(gens)
    active = list(enumerate(gens))
    while active:
        still = []
        for i, g in active:
            try:
                next(g)
                still.append((i, g))
            except StopIteration as done:
                results[i] = done.value
        active = still
    return results


def _attn_block_fwd(q_blk, kk, vv, bias_ref, sinks_ref, first_mask):
    kvar = _kv_variants(kk)
    vvar = _kv_variants(vv)
    heads = range(N_HEADS)
    q_pairs = [q_blk[:, (h // 2) * LANES:(h // 2 + 1) * LANES] for h in heads]
    logits = [_dot_nt(q_pairs[h], kvar[_head_kv(h)[0]][_head_kv(h)[1]]) + bias_ref[h] for h in heads]
    if first_mask is not None:
        logits = [jnp.where(first_mask, NEG_INF, lg) for lg in logits]
    yield
    ms = [jnp.maximum(jnp.max(logits[h], axis=-1, keepdims=True), sinks_ref[h]) for h in heads]
    yield
    es = [jnp.exp(logits[h] - ms[h]) for h in heads]
    ess = [jnp.exp(sinks_ref[h] - ms[h]) for h in heads]
    yield
    invs = [1.0 / (jnp.sum(es[h], axis=-1, keepdims=True) + ess[h]) for h in heads]
    probs = [(es[h] * invs[h], ess[h] * invs[h]) for h in heads]
    yield
    outs = [_dot(probs[h][0].astype(BF16), vvar[_head_kv(h)[0]][_head_kv(h)[1]]) for h in heads]
    pairs = [outs[2 * i] + outs[2 * i + 1] for i in range(N_HEADS // 2)]
    return jnp.concatenate(pairs, axis=1), probs, kvar, vvar


def _gmlp_chunk_fwd(gu, gv, ln_g, ln_b, wsm_ref, bsx, amat):
    u, tu = _gelu(gu)
    a, ta = _gelu(gv)
    yield
    mean = _split_dot(a, amat)
    d = a - mean
    yield
    var = _split_dot(d * d, amat)
    yield
    rstd = lax.rsqrt(var + LN_EPS)
    xhat = d * rstd
    vb = (xhat * ln_g + ln_b).astype(BF16)
    yield
    lane = lax.broadcasted_iota(jnp.int32, (BLOCK, LANES), 1)
    low = lane < GROUP_DIM
    cols = []
    for pair in range(N_GROUPS // 2):
        vp = vb[:, pair * LANES:(pair + 1) * LANES]
        cols.append(jnp.where(low, _dot(wsm_ref[2 * pair], vp), _dot(wsm_ref[2 * pair + 1], vp)))
    mixedv = jnp.concatenate(cols, axis=1) + bsx
    return u * mixedv, (u, tu, ta, xhat, rstd, vb, mixedv)


def _rms(a, g):
    r = lax.rsqrt(jnp.mean(a * a, axis=-1, keepdims=True) + LN_EPS)
    return a * r * g, r


def _fwd_mix(q, kv, gu, gv, x, modr, bias, sinks, gln_g, gln_b, wsm, bsx, amat, aog, gog, w_out, ln1_g, ln1_b, tm,
             ffn_shards, ffn_kinds):
    s = x.shape[0]
    nb = tm // BLOCK
    n_steps = s // tm
    fwd_step, diag_step = (7 * n_steps) // 16, (12 * n_steps) // 16
    n_w = len(ffn_shards)

    def body(q_ref, kv_ref, kvp_ref, gu_ref, gv_ref, x_ref, mod_ref, bias_ref, sinks_ref, glng_ref, glnb_ref, wsm_ref,
             bsx_ref, amat_ref, aog_ref, gog_ref, wout_ref, ln1g_ref, ln1b_ref, *rest):
        shard_refs = rest[:n_w]
        x1_ref, x1b_ref, y_ref, mixed_ref = rest[n_w:n_w + 4]
        gathered_refs = rest[n_w + 4:2 * n_w + 4]
        mix_scr, send_sems, recv_sems = rest[2 * n_w + 4:]
        i = pl.program_id(0)
        gather = _WeightGather(shard_refs, gathered_refs, ffn_kinds, send_sems, recv_sems)

        @pl.when(i == 0)
        def _():
            gather.start()

        col = lax.broadcasted_iota(jnp.int32, (BLOCK, 2 * BLOCK), 1)
        for b0 in range(0, nb, MIX_GROUP):
            gens = []
            for b in range(b0, min(b0 + MIX_GROUP, nb)):
                r0 = b * BLOCK
                if b == 0:
                    kvprev = kvp_ref[...]
                    first_mask = (col < BLOCK) & (i == 0)
                else:
                    kvprev = kv_ref[r0 - BLOCK:r0, :]
                    first_mask = None
                kvcur = kv_ref[r0:r0 + BLOCK, :]
                kk = jnp.concatenate([kvprev[:, :KV_W], kvcur[:, :KV_W]], axis=0)
                vv = jnp.concatenate([kvprev[:, KV_W:], kvcur[:, KV_W:]], axis=0)
                gens.append(_attn_block_fwd(q_ref[r0:r0 + BLOCK, :], kk, vv, bias_ref, sinks_ref, first_mask))
                gens.append(_gmlp_chunk_fwd(gu_ref[r0:r0 + BLOCK, :], gv_ref[r0:r0 + BLOCK, :], glng_ref[...],
                                            glnb_ref[...], wsm_ref, bsx_ref[...], amat_ref[...]))
            res = _interleave(*gens)
            for k, b in enumerate(range(b0, min(b0 + MIX_GROUP, nb))):
                r0 = b * BLOCK
                na, _ = _rms(res[2 * k][0], aog_ref[...])
                ng, _ = _rms(res[2 * k + 1][0], gog_ref[...])
                mix_scr[r0:r0 + BLOCK, :ATTN_W] = na.astype(BF16)
                mix_scr[r0:r0 + BLOCK, ATTN_W:] = ng.astype(BF16)
        mixed = mix_scr[...]
        mixed_ref[...] = mixed
        y = _dot(mixed, wout_ref[...])
        y_ref[...] = y.astype(BF16)
        z1 = ALPHA * x_ref[...] + mod_ref[2:3, :] * y
        xhat, _ = _ln_stats(z1)
        x1 = xhat * ln1g_ref[...] + ln1b_ref[...]
        x1_ref[...] = x1
        x1b_ref[...] = x1.astype(BF16)

        @pl.when(i == fwd_step)
        def _():
            gather.forward()

        @pl.when(i == diag_step)
        def _():
            gather.forward_diagonal()

        @pl.when(i == n_steps - 1)
        def _():
            gather.finish()

    row = lambda w: pl.BlockSpec((tm, w), lambda i: (i, 0))
    prev = pl.BlockSpec((BLOCK, 2 * KV_W), lambda i: (jnp.maximum(i * nb - 1, 0), 0))
    outs = pl.pallas_call(
        body, name="fwd_mix",
        out_shape=[jax.ShapeDtypeStruct((s, D_MODEL), F32)] + [jax.ShapeDtypeStruct((s, D_MODEL), BF16)] * 3
        + [jax.ShapeDtypeStruct(_gathered_shape(sh, k), BF16) for sh, k in zip(ffn_shards, ffn_kinds)],
        grid=(n_steps,),
        in_specs=[row(ATTN_W), row(2 * KV_W), prev, row(GMLP_W), row(GMLP_W), row(D_MODEL), _const_spec((8, D_MODEL)),
                  _const_spec((N_HEADS, BLOCK, 2 * BLOCK)), pl.BlockSpec(memory_space=pltpu.SMEM),
                  _const_spec((1, GMLP_W)), _const_spec((1, GMLP_W)), _const_spec((N_GROUPS, BLOCK, BLOCK)),
                  _const_spec((BLOCK, GMLP_W)), _const_spec((GMLP_W, GMLP_W)), _const_spec((1, ATTN_W)),
                  _const_spec((1, GMLP_W)), _const_spec((D_MODEL, D_MODEL)), _const_spec((1, D_MODEL)),
                  _const_spec((1, D_MODEL))] + [ANY] * n_w,
        out_specs=[row(D_MODEL)] * 4 + [ANY] * n_w,
        scratch_shapes=[pltpu.VMEM((tm, D_MODEL), BF16)] + _WeightGather.sems(n_w),
        compiler_params=_params(("arbitrary",)),
    )(q, kv, kv, gu, gv, x, modr, bias, sinks, gln_g, gln_b, wsm, bsx, amat, aog, gog, w_out, ln1_g, ln1_b, *ffn_shards)
    return outs[:4], outs[4:]


FF_BLOCKS = N_CHIPS // 2
FF_CHUNK = D_FF // FF_BLOCKS
FFN_SUB = 256


def _sigmoid(x):
    return 1.0 / (1.0 + jnp.exp(-x))


def _fwd_ffn(x1, target, modr, ln2_g, ln2_b, w_gu, w_dn, tm):
    s = x1.shape[0]

    def body(x1_ref, t_ref, mod_ref, g_ref, b_ref, wgu_ref, wdn_ref, h2_ref, act_ref, dy2_ref, dx1a_ref, acc_ref):
        @pl.when(pl.program_id(0) == 0)
        def _():
            acc_ref[...] = jnp.zeros_like(acc_ref)

        x1v = x1_ref[...]
        h2 = (x1v * (1.0 + mod_ref[4:5, :]) + mod_ref[3:4, :]).astype(BF16)
        h2_ref[...] = h2
        y2 = None
        for cc in range(FF_BLOCKS):
            c0 = cc * FF_CHUNK
            gate = _dot(h2, wgu_ref[cc])
            up = _dot(h2, wgu_ref[FF_BLOCKS + cc])
            act_ref[:, c0:c0 + FF_CHUNK] = gate.astype(BF16)
            act_ref[:, D_FF + c0:D_FF + c0 + FF_CHUNK] = up.astype(BF16)
            a = (gate * _sigmoid(gate) * up).astype(BF16)
            part = _dot(a, wdn_ref[c0:c0 + FF_CHUNK, :])
            y2 = part if y2 is None else y2 + part
        g2 = mod_ref[5:6, :]
        z2 = ALPHA * x1v + g2 * y2
        xhat, rstd = _ln_stats(z2)
        gain = g_ref[...]
        diff = xhat * gain + b_ref[...] - t_ref[...]
        dx2 = diff * (1.0 / D_MODEL)
        dz2 = _ln_bwd(dx2 * gain, xhat, rstd)
        dx1a_ref[...] = ALPHA * dz2
        dy2_ref[...] = (g2 * dz2).astype(BF16)
        acc_ref[0:1, :] += _colsum(diff * diff)
        acc_ref[1:2, :] += _colsum(dx2 * xhat)
        acc_ref[2:3, :] += _colsum(dx2)
        acc_ref[3:4, :] += _colsum(dz2 * y2)

    row = lambda w: pl.BlockSpec((tm, w), lambda i: (i, 0))
    return pl.pallas_call(
        body, name="fwd_ffn",
        out_shape=(jax.ShapeDtypeStruct((s, D_MODEL), BF16), jax.ShapeDtypeStruct((s, 2 * D_FF), BF16),
                   jax.ShapeDtypeStruct((s, D_MODEL), BF16), jax.ShapeDtypeStruct((s, D_MODEL), F32),
                   jax.ShapeDtypeStruct((8, D_MODEL), F32)),
        grid=(s // tm,),
        in_specs=[row(D_MODEL), row(D_MODEL), _const_spec((8, D_MODEL)), _const_spec((1, D_MODEL)),
                  _const_spec((1, D_MODEL)), _const_spec((N_CHIPS, D_MODEL, FF_CHUNK), single=True),
                  _const_spec((D_FF, D_MODEL), single=True)],
        out_specs=(row(D_MODEL), row(2 * D_FF), row(D_MODEL), row(D_MODEL), _const_spec((8, D_MODEL))),
        compiler_params=_params(("arbitrary",)),
    )(x1, target, modr, ln2_g, ln2_b, w_gu, w_dn)


def _bwd_ffn(dy2, act, w_gu, w_dn, tm):
    s = dy2.shape[0]

    def body(dy2_ref, act_ref, wgu_ref, wdn_ref, a_ref, dgu_ref, dh2_ref):
        dy2v = dy2_ref[...]
        dh2 = None
        for cc in range(FF_BLOCKS):
            c0 = cc * FF_CHUNK
            da = _dot_nt(dy2v, wdn_ref[c0:c0 + FF_CHUNK, :])
            gate = act_ref[:, c0:c0 + FF_CHUNK].astype(F32)
            up = act_ref[:, D_FF + c0:D_FF + c0 + FF_CHUNK].astype(F32)
            sg = _sigmoid(gate)
            sl = gate * sg
            a_ref[:, c0:c0 + FF_CHUNK] = (sl * up).astype(BF16)
            dgate = (da * up * (sg * (1.0 + gate * (1.0 - sg)))).astype(BF16)
            dup = (da * sl).astype(BF16)
            dgu_ref[:, c0:c0 + FF_CHUNK] = dgate
            dgu_ref[:, D_FF + c0:D_FF + c0 + FF_CHUNK] = dup
            part = _dot_nt(dgate, wgu_ref[cc]) + _dot_nt(dup, wgu_ref[FF_BLOCKS + cc])
            dh2 = part if dh2 is None else dh2 + part
        dh2_ref[...] = dh2.astype(BF16)

    row = lambda w: pl.BlockSpec((tm, w), lambda i: (i, 0))
    return pl.pallas_call(
        body, name="bwd_ffn",
        out_shape=(jax.ShapeDtypeStruct((s, D_FF), BF16), jax.ShapeDtypeStruct((s, 2 * D_FF), BF16),
                   jax.ShapeDtypeStruct((s, D_MODEL), BF16)),
        grid=(s // tm,),
        in_specs=[row(D_MODEL), row(2 * D_FF), _const_spec((N_CHIPS, D_MODEL, FF_CHUNK), single=True),
                  _const_spec((D_FF, D_MODEL), single=True)],
        out_specs=(row(D_FF), row(2 * D_FF), row(D_MODEL)),
        compiler_params=_params(("parallel",)),
    )(dy2, act, w_gu, w_dn)


def _bwd_mid(dh2, dx1a, x1, x, y, modr, ln1_g, w_out, tm, swap_fulls, swap_kinds):
    s = x.shape[0]
    n_steps = s // tm
    n_g = len(swap_fulls)

    def body(dh2_ref, dx1a_ref, x1_ref, x_ref, y_ref, mod_ref, g_ref, wout_ref, *rest):
        full_refs = rest[:n_g]
        dxa_ref, dy_ref, dmix_ref, acc_ref = rest[n_g:n_g + 4]
        got_refs = rest[n_g + 4:2 * n_g + 4]
        swap = _HalfSwap(full_refs, got_refs, swap_kinds, *rest[2 * n_g + 4:])
        i = pl.program_id(0)

        @pl.when(i == 0)
        def _():
            swap.start()
            acc_ref[...] = jnp.zeros_like(acc_ref)

        dh2 = dh2_ref[...].astype(F32)
        x1v = x1_ref[...].astype(F32)
        yv = y_ref[...].astype(F32)
        g1 = mod_ref[2:3, :]
        dx1 = dx1a_ref[...] + dh2 * (1.0 + mod_ref[4:5, :])
        z1 = ALPHA * x_ref[...] + g1 * yv
        xhat, rstd = _ln_stats(z1)
        dz1 = _ln_bwd(dx1 * g_ref[...], xhat, rstd)
        dxa_ref[...] = (ALPHA * dz1).astype(BF16)
        dy = (g1 * dz1).astype(BF16)
        dy_ref[...] = dy
        dmix_ref[...] = _dot_nt(dy, wout_ref[...]).astype(BF16)
        acc_ref[0:1, :] += _colsum(dh2 * x1v)
        acc_ref[1:2, :] += _colsum(dh2)
        acc_ref[2:3, :] += _colsum(dx1 * xhat)
        acc_ref[3:4, :] += _colsum(dx1)
        acc_ref[4:5, :] += _colsum(dz1 * yv)

        @pl.when(i == n_steps - 1)
        def _():
            swap.wait()

    row = lambda w: pl.BlockSpec((tm, w), lambda i: (i, 0))
    outs = pl.pallas_call(
        body, name="bwd_mid",
        out_shape=[jax.ShapeDtypeStruct((s, D_MODEL), BF16), jax.ShapeDtypeStruct((s, D_MODEL), BF16),
                   jax.ShapeDtypeStruct((s, D_MODEL), BF16), jax.ShapeDtypeStruct((8, D_MODEL), F32)]
        + _HalfSwap.out_shapes(swap_fulls, swap_kinds),
        grid=(n_steps,),
        in_specs=[row(D_MODEL)] * 5 + [_const_spec((8, D_MODEL)), _const_spec((1, D_MODEL)),
                                       _const_spec((D_MODEL, D_MODEL))] + [ANY] * n_g,
        out_specs=[row(D_MODEL), row(D_MODEL), row(D_MODEL), _const_spec((8, D_MODEL))] + [ANY] * n_g,
        scratch_shapes=_HalfSwap.sems(n_g),
        compiler_params=_params(("arbitrary",)),
    )(dh2, dx1a, x1, x, y, modr, ln1_g, w_out, *swap_fulls)
    return outs[:4], outs[4:]


def _fold_kv(t0, t1):
    lane = lax.broadcasted_iota(jnp.int32, t0.shape, 1)
    f0 = t0 + pltpu.roll(t0, HEAD_DIM, 1)
    f1 = t1 + pltpu.roll(t1, HEAD_DIM, 1)
    return jnp.where(lane < HEAD_DIM, f0, f1)


def _bwd_mix(q, kv, gu, gv, dmix, bias, sinks, gln_g, gln_b, wsm, bsx, amat, aog, gog, after):
    s = q.shape[0]
    tile = 2 * BLOCK
    n_steps = s // tile

    def body(q_ref, kv_ref, kvp_ref, gu_ref, gv_ref, dmix_ref, bias_ref, sinks_ref, glng_ref, glnb_ref, wsm_ref,
             bsx_ref, amat_ref, aog_ref, gog_ref, after_ref, dq_ref, dkv_ref, dgu_ref, dgv_ref, gbias_ref, dws_ref,
             dbs_ref, vec_ref, dsink_ref, carry, done):
        n = pl.program_id(0)

        @pl.when(n == 0)
        def _():
            carry[...] = jnp.zeros_like(carry)
            done[...] = jnp.zeros_like(done)
            gbias_ref[...] = jnp.zeros_like(gbias_ref)
            dws_ref[...] = jnp.zeros_like(dws_ref)
            dbs_ref[...] = jnp.zeros_like(dbs_ref)
            vec_ref[...] = jnp.zeros_like(vec_ref)
            dsink_ref[...] = jnp.zeros_like(dsink_ref)

        @pl.when(n == n_steps)
        def _():
            dkv_ref[:BLOCK, :] = done[...].astype(BF16)
            dkv_ref[BLOCK:, :] = carry[...].astype(BF16)

        @pl.when(n < n_steps)
        def _():
            col = lax.broadcasted_iota(jnp.int32, (BLOCK, 2 * BLOCK), 1)
            lane = lax.broadcasted_iota(jnp.int32, (BLOCK, LANES), 1)
            low = lane < HEAD_DIM
            rows = [slice(0, BLOCK), slice(BLOCK, tile)]
            kv_blocks = [kvp_ref[...], kv_ref[rows[0], :], kv_ref[rows[1], :]]
            masks = [(col < BLOCK) & (n == 0), None]
            q_blks = [q_ref[r, :] for r in rows]
            fwd = []
            for b in range(2):
                kk = jnp.concatenate([kv_blocks[b][:, :KV_W], kv_blocks[b + 1][:, :KV_W]], axis=0)
                vv = jnp.concatenate([kv_blocks[b][:, KV_W:], kv_blocks[b + 1][:, KV_W:]], axis=0)
                fwd.append(_attn_block_fwd(q_blks[b], kk, vv, bias_ref, sinks_ref, masks[b]))
                fwd.append(_gmlp_chunk_fwd(gu_ref[rows[b], :], gv_ref[rows[b], :], glng_ref[...], glnb_ref[...],
                                           wsm_ref, bsx_ref[...], amat_ref[...]))
            res = _interleave(*fwd[:2]) + _interleave(*fwd[2:])

            def gating_bwd(b, d_gm, saved):
                u, tu, ta, xhat, rstd, vb, mixedv = saved
                dgu_ref[rows[b], :] = (d_gm * mixedv * _gelu_grad(gu_ref[rows[b], :], tu)).astype(BF16)
                dmx = d_gm * u
                dmxb = dmx.astype(BF16)
                yield
                dvn_cols, dws = [], []
                for pair in range(N_GROUPS // 2):
                    dp_ = dmxb[:, pair * LANES:(pair + 1) * LANES]
                    vp = vb[:, pair * LANES:(pair + 1) * LANES]
                    dvn_cols.append(
                        jnp.where(low, _dot_tn(wsm_ref[2 * pair], dp_), _dot_tn(wsm_ref[2 * pair + 1], dp_)))
                    zero = jnp.zeros_like(dp_)
                    dws.append(_dot_nt(jnp.where(low, dp_, zero), vp))
                    dws.append(_dot_nt(jnp.where(low, zero, dp_), vp))
                dvn = jnp.concatenate(dvn_cols, axis=1)
                yield
                dxh = dvn * glng_ref[...]
                am = amat_ref[...]
                m1 = _split_dot(dxh, am)
                m2 = _split_dot(dxh * xhat, am)
                yield
                da = rstd * (dxh - m1 - xhat * m2)
                dgv_ref[rows[b], :] = (da * _gelu_grad(gv_ref[rows[b], :], ta)).astype(BF16)
                return dmx, dws, _colsum(dvn * xhat), _colsum(dvn)

            def attention_bwd(b, d_attn, probs, kvar, vvar):
                heads = range(N_HEADS)
                sels = [low if h % 2 == 0 else jnp.logical_not(low) for h in heads]
                pair_of = lambda a, h: a[:, (h // 2) * LANES:(h // 2 + 1) * LANES]
                do_hs = [jnp.where(sels[h], pair_of(d_attn, h), 0.0).astype(BF16) for h in heads]
                q_hs = [jnp.where(sels[h], pair_of(q_blks[b], h), jnp.zeros((BLOCK, LANES), BF16)) for h in heads]
                dps = [_dot_nt(do_hs[h], vvar[_head_kv(h)[0]][_head_kv(h)[1]]) for h in heads]
                yield
                deltas = [jnp.sum(probs[h][0] * dps[h], axis=-1, keepdims=True) for h in heads]
                yield
                dss = [probs[h][0] * (dps[h] - deltas[h]) for h in heads]
                dsinks = [-(probs[h][1] * deltas[h]) for h in heads]
                dsbs = [ds.astype(BF16) for ds in dss]
                pbs = [probs[h][0].astype(BF16) for h in heads]
                yield
                dqs = [_dot(dsbs[h], kvar[_head_kv(h)[0]][_head_kv(h)[1]]) for h in heads]
                tks = [_dot_tn(dsbs[h], q_hs[h]) for h in heads]
                tvs = [_dot_tn(pbs[h], do_hs[h]) for h in heads]
                dq_cols = [dqs[2 * i] + dqs[2 * i + 1] for i in range(N_HEADS // 2)]
                dq_ref[rows[b], :] = (jnp.concatenate(dq_cols, axis=1) * Q_SCALE).astype(BF16)
                per_kv = N_HEADS // N_KV
                kv_sum = lambda ts, kvh: sum(ts[kvh * per_kv + 1:(kvh + 1) * per_kv], ts[kvh * per_kv])
                dkk = _fold_kv(kv_sum(tks, 0), kv_sum(tks, 1))
                dvv = _fold_kv(kv_sum(tvs, 0), kv_sum(tvs, 1))
                return jnp.concatenate([dkk, dvv], axis=1), dss, dsinks

            bwd, rms_g = [], []
            for b in range(2):
                attn, probs, kvar, vvar = res[2 * b]
                gm, saved = res[2 * b + 1]
                na_unit, r_a = _rms(attn, 1.0)
                ng_unit, r_g = _rms(gm, 1.0)
                dmix = dmix_ref[rows[b], :].astype(F32)
                dn_a = dmix[:, :ATTN_W]
                dn_g = dmix[:, ATTN_W:]
                rms_g.append((_colsum(dn_a * na_unit), _colsum(dn_g * ng_unit)))
                t_a = dn_a * aog_ref[...]
                d_attn = r_a * t_a - na_unit * (r_a * jnp.mean(t_a * na_unit, axis=-1, keepdims=True))
                t_g = dn_g * gog_ref[...]
                d_gm = r_g * t_g - ng_unit * (r_g * jnp.mean(t_g * ng_unit, axis=-1, keepdims=True))
                bwd.append(attention_bwd(b, d_attn, probs, kvar, vvar))
                bwd.append(gating_bwd(b, d_gm, saved))
            (dkv_a, dss_a, dsk_a), (dmx_a, dws_a, glg_a, glb_a) = _interleave(*bwd[:2])
            (dkv_b, dss_b, dsk_b), (dmx_b, dws_b, glg_b, glb_b) = _interleave(*bwd[2:])

            vec_ref[0:1, :] += rms_g[0][0] + rms_g[1][0]
            vec_ref[1:2, :] += rms_g[0][1] + rms_g[1][1]
            vec_ref[2:3, :] += glg_a + glg_b
            vec_ref[3:4, :] += glb_a + glb_b
            dbs_ref[...] += dmx_a + dmx_b
            for g in range(N_GROUPS):
                dws_ref[g] += dws_a[g] + dws_b[g]
            for h in range(N_HEADS):
                gbias_ref[h] += dss_a[h] + dss_b[h]
                dsink_ref[h] += dsk_a[h] + dsk_b[h]

            dkv_ref[:BLOCK, :] = done[...].astype(BF16)
            dkv_ref[BLOCK:, :] = (carry[...] + dkv_a[:BLOCK]).astype(BF16)
            done[...] = dkv_a[BLOCK:] + dkv_b[:BLOCK]
            carry[...] = dkv_b[BLOCK:]

    last = n_steps - 1
    cur = lambda w: pl.BlockSpec((tile, w), lambda n: (jnp.minimum(n, last), 0))
    late = lambda w: pl.BlockSpec((tile, w), lambda n: (jnp.clip(n - 1, 0, last), 0))
    before = pl.BlockSpec((BLOCK, 2 * KV_W), lambda n: (jnp.clip(2 * n - 1, 0, 2 * last + 1), 0))
    outs = pl.pallas_call(
        body, name="bwd_mix",
        out_shape=[jax.ShapeDtypeStruct((s, ATTN_W), BF16), jax.ShapeDtypeStruct((s, 2 * KV_W), BF16),
                   jax.ShapeDtypeStruct((s, GMLP_W), BF16), jax.ShapeDtypeStruct((s, GMLP_W), BF16),
                   jax.ShapeDtypeStruct((N_HEADS, BLOCK, 2 * BLOCK), F32),
                   jax.ShapeDtypeStruct((N_GROUPS, BLOCK, BLOCK), F32),
                   jax.ShapeDtypeStruct((BLOCK, GMLP_W), F32), jax.ShapeDtypeStruct((8, GMLP_W), F32),
                   jax.ShapeDtypeStruct((N_HEADS, BLOCK, 1), F32)],
        grid=(n_steps + 1,),
        in_specs=[cur(ATTN_W), cur(2 * KV_W), before, cur(GMLP_W), cur(GMLP_W), cur(D_MODEL),
                  _const_spec((N_HEADS, BLOCK, 2 * BLOCK)), pl.BlockSpec(memory_space=pltpu.SMEM),
                  _const_spec((1, GMLP_W)), _const_spec((1, GMLP_W)), _const_spec((N_GROUPS, BLOCK, BLOCK)),
                  _const_spec((BLOCK, GMLP_W)), _const_spec((GMLP_W, GMLP_W)), _const_spec((1, ATTN_W)),
                  _const_spec((1, GMLP_W)), ANY],
        out_specs=[cur(ATTN_W), late(2 * KV_W), cur(GMLP_W), cur(GMLP_W),
                   _const_spec((N_HEADS, BLOCK, 2 * BLOCK)), _const_spec((N_GROUPS, BLOCK, BLOCK)),
                   _const_spec((BLOCK, GMLP_W)), _const_spec((8, GMLP_W)), _const_spec((N_HEADS, BLOCK, 1))],
        scratch_shapes=[pltpu.VMEM((BLOCK, 2 * KV_W), F32), pltpu.VMEM((BLOCK, 2 * KV_W), F32)],
        compiler_params=_params(("arbitrary",)),
    )(q, kv, kv, gu, gv, dmix, bias, sinks, gln_g, gln_b, wsm, bsx, amat, aog, gog, after)
    return outs


def _mix_finalize(gbias, bucket, dws, dbs, dsink):
    def body(gb_ref, bucket_ref, dws_ref, dbs_ref, dsink_ref, tall_ref):
        bk = bucket_ref[...]
        lane = lax.broadcasted_iota(jnp.int32, (N_BUCKETS, LANES), 1)
        rowi = lax.broadcasted_iota(jnp.int32, (N_BUCKETS, LANES), 0)
        drb = jnp.zeros((N_BUCKETS, LANES), F32)
        dsk = jnp.zeros((8, LANES), F32)
        lane8 = lax.broadcasted_iota(jnp.int32, (8, LANES), 1)
        for h in range(N_HEADS):
            g = gb_ref[h]
            for b in range(N_BUCKETS):
                tot = jnp.sum(_colsum(jnp.where(bk == float(b), g, 0.0)), axis=1, keepdims=True)
                drb = jnp.where((lane == h) & (rowi == b), tot, drb)
            sk = jnp.sum(dsink_ref[h], axis=0, keepdims=True)
            dsk = jnp.where(lane8 == h, sk, dsk)
        tall_ref[TALL_RB:TALL_RB + N_BUCKETS, :] = drb
        tall_ref[TALL_SK:TALL_SK + 8, :] = dsk
        ti = lax.broadcasted_iota(jnp.int32, (BLOCK, BLOCK), 0)
        ui = lax.broadcasted_iota(jnp.int32, (BLOCK, BLOCK), 1)
        for g in range(N_GROUPS):
            tall_ref[g * BLOCK:(g + 1) * BLOCK, :] = jnp.where(ti >= ui, dws_ref[g], 0.0)
        gi = lax.broadcasted_iota(jnp.int32, (GMLP_W, LANES), 0) // GROUP_DIM
        li = lax.broadcasted_iota(jnp.int32, (GMLP_W, LANES), 1)
        ind = jnp.where(gi == li, 1.0, 0.0).astype(BF16)
        d = dbs_ref[...]
        hi = d.astype(BF16)
        r1 = d - hi.astype(F32)
        mid = r1.astype(BF16)
        lo = (r1 - mid.astype(F32)).astype(BF16)
        dbsg = _dot(hi, ind) + _dot(mid, ind) + _dot(lo, ind)
        tall_ref[TALL_BS:TALL_BS + N_GROUPS, :] = dbsg.T[:N_GROUPS, :]

    return pl.pallas_call(
        body, name="mix_finalize", out_shape=jax.ShapeDtypeStruct((TALL_ROWS, LANES), F32), grid=(1,),
        in_specs=[_const_spec((N_HEADS, BLOCK, 2 * BLOCK)), _const_spec((BLOCK, 2 * BLOCK)),
                  _const_spec((N_GROUPS, BLOCK, BLOCK)), _const_spec((BLOCK, GMLP_W)),
                  _const_spec((N_HEADS, BLOCK, 1))],
        out_specs=_const_spec((TALL_ROWS, LANES)),
        compiler_params=_params(("arbitrary",)),
    )(gbias, bucket, dws, dbs, dsink)


def _bwd_in(dq, dkv, dgu, dgv, dxa, x, modr, w_in, tm):
    s = x.shape[0]

    def body(dq_ref, dkv_ref, dgu_ref, dgv_ref, dxa_ref, x_ref, mod_ref, w_ref, gx_ref, acc_ref, db_ref):
        @pl.when(pl.program_id(0) == 0)
        def _():
            acc_ref[...] = jnp.zeros_like(acc_ref)
            db_ref[...] = jnp.zeros_like(db_ref)

        dproj = jnp.concatenate([dq_ref[...], dkv_ref[...], dgu_ref[...], dgv_ref[...]], axis=1)
        wb = IN_W // N_CHIPS
        dh1 = sum([_dot_nt(dproj[:, j * wb:(j + 1) * wb], w_ref[j]) for j in range(1, N_CHIPS)],
                  _dot_nt(dproj[:, :wb], w_ref[0]))
        gx_ref[...] = dxa_ref[...].astype(F32) + dh1 * (1.0 + mod_ref[1:2, :])
        acc_ref[0:1, :] += _colsum(dh1 * x_ref[...].astype(F32))
        acc_ref[1:2, :] += _colsum(dh1)
        db_ref[0:1, :] += _colsum(dproj.astype(F32))

    row = lambda w: pl.BlockSpec((tm, w), lambda i: (i, 0))
    return pl.pallas_call(
        body, name="bwd_in",
        out_shape=(jax.ShapeDtypeStruct((s, D_MODEL), F32), jax.ShapeDtypeStruct((8, D_MODEL), F32),
                   jax.ShapeDtypeStruct((8, IN_W), F32)),
        grid=(s // tm,),
        in_specs=[row(ATTN_W), row(2 * KV_W), row(GMLP_W), row(GMLP_W), row(D_MODEL), row(D_MODEL),
                  _const_spec((8, D_MODEL)), _const_spec(w_in.shape)],
        out_specs=(row(D_MODEL), _const_spec((8, D_MODEL)), _const_spec((8, IN_W))),
        compiler_params=_params(("arbitrary",)),
    )(dq, dkv, dgu, dgv, dxa, x, modr, w_in)


def _wgrad(a, bs, tm, tk, name, owner_blocks=False, gather_vs=()):
    k_all, m = a.shape
    n = sum(b.shape[1] for b in bs)
    nk = k_all // tk
    nm = m // tm
    n_b = len(bs)
    n_v = len(gather_vs)
    wb = n // N_CHIPS

    def body(a_ref, *rest):
        b_refs, v_refs = rest[:n_b], rest[n_b:n_b + n_v]
        o_ref, ob_ref = rest[n_b + n_v:n_b + n_v + 2]
        vg_refs = rest[n_b + n_v + 2:n_b + 2 * n_v + 2]
        i, k = pl.program_id(0), pl.program_id(1)
        if n_v:
            gather = _Gather8(v_refs, vg_refs, *rest[n_b + 2 * n_v + 2:])

            @pl.when((i == 0) & (k == 0))
            def _():
                gather.start()

            @pl.when((i == nm - 1) & (k == 0))
            def _():
                gather.forward()

        @pl.when(k == 0)
        def _():
            o_ref[...] = jnp.zeros_like(o_ref)

        b = b_refs[0][...] if n_b == 1 else jnp.concatenate([r[...] for r in b_refs], axis=1)
        if owner_blocks:
            av = a_ref[...]
            for j in range(N_CHIPS):
                o_ref[j] += _dot_tn(av, b[:, j * wb:(j + 1) * wb])
        else:
            o_ref[...] += _dot_tn(a_ref[...], b)

        @pl.when(k == nk - 1)
        def _():
            ob_ref[...] = o_ref[...].astype(BF16)

        if n_v:
            @pl.when((i == nm - 1) & (k == nk - 1))
            def _():
                gather.finish()

    if owner_blocks:
        out_spec = pl.BlockSpec((N_CHIPS, tm, wb), lambda i, k: (0, i, 0))
        shape = (N_CHIPS, m, wb)
    else:
        out_spec = pl.BlockSpec((tm, n), lambda i, k: (i, 0))
        shape = (m, n)
    outs = pl.pallas_call(
        body, name=name,
        out_shape=[jax.ShapeDtypeStruct(shape, F32), jax.ShapeDtypeStruct(shape, BF16)] + _gathered8_shapes(gather_vs),
        grid=(nm, nk),
        in_specs=[pl.BlockSpec((tk, tm), lambda i, k: (k, i))]
        + [pl.BlockSpec((tk, b.shape[1]), lambda i, k: (k, 0)) for b in bs] + [ANY] * n_v,
        out_specs=[out_spec, out_spec] + [ANY] * n_v,
        scratch_shapes=_Gather8.sems(n_v) if n_v else [],
        compiler_params=_params(("arbitrary", "arbitrary") if n_v else ("parallel", "arbitrary")),
    )(a, *bs, *gather_vs)
    return outs[0], outs[1], outs[2:]


def _adam_math(w, g, m, v):
    m2 = ADAM_B1 * m + (1.0 - ADAM_B1) * g
    v2 = ADAM_B2 * v + (1.0 - ADAM_B2) * (g * g)
    m_hat = m2 / (1.0 - ADAM_B1 ** ADAM_STEP)
    v_hat = v2 / (1.0 - ADAM_B2 ** ADAM_STEP)
    delta = -ADAM_LR * (m_hat / (jnp.sqrt(v_hat) + ADAM_EPS) + ADAM_WD * w)
    return delta, m2, v2


def _adam_halves(w, mine, got, m, v, tr, name):
    r, cc = w.shape
    h = r // 2
    nt = h // tr

    def body(c_ref, w_ref, mine_ref, got_ref, m_ref, v_ref, g_ref, d_ref, m2_ref, v2_ref):
        g = jnp.where(pl.program_id(0) == c_ref[0], mine_ref[...], got_ref[...])
        g_ref[...] = g
        d, m2, v2 = _adam_math(w_ref[...], g, m_ref[...], v_ref[...])
        d_ref[...] = d
        m2_ref[...] = m2
        v2_ref[...] = v2

    full = pl.BlockSpec((tr, cc), lambda hh, i, c_ref: (hh * nt + i, 0))
    half = pl.BlockSpec((tr, cc), lambda hh, i, c_ref: (i, 0))
    shp = jax.ShapeDtypeStruct((r, cc), F32)
    return pl.pallas_call(
        body, name=name, out_shape=(shp, shp, shp, shp),
        grid_spec=pltpu.PrefetchScalarGridSpec(
            num_scalar_prefetch=1, grid=(2, nt), in_specs=[full, half, half, full, full],
            out_specs=(full, full, full, full)),
        compiler_params=_params(("arbitrary", "arbitrary")),
    )(_core_index_scalar(), w, mine, got, m, v)


def _adam_w_ada(sc_t, dmod_cols, w, m, v, tr, after):
    r, cc = w.shape

    def body(sct_ref, dm_ref, w_ref, m_ref, v_ref, after_ref, g_ref, d_ref, m2_ref, v2_ref):
        g = sct_ref[:, 0:1] * dm_ref[0:1, :]
        for k in range(1, N_DEV):
            g = g + sct_ref[:, k:k + 1] * dm_ref[k:k + 1, :]
        g_ref[...] = g
        d, m2, v2 = _adam_math(w_ref[...], g, m_ref[...], v_ref[...])
        d_ref[...] = d
        m2_ref[...] = m2
        v2_ref[...] = v2

    spec = pl.BlockSpec((tr, cc), lambda i: (i, 0))
    shp = jax.ShapeDtypeStruct((r, cc), F32)
    return pl.pallas_call(
        body, name="adam_w_ada", out_shape=(shp, shp, shp, shp), grid=(r // tr,),
        in_specs=[pl.BlockSpec((tr, N_DEV), lambda i: (i, 0)), _const_spec((N_DEV, cc)), spec, spec, spec, ANY],
        out_specs=(spec, spec, spec, spec), compiler_params=_params(("parallel",)),
    )(sc_t, dmod_cols, w, m, v, after)


def _pack_wide(acc_i, acc_m, acc_f, db_in, vec):
    arrs = [acc_i, acc_m, acc_f, db_in, vec]
    i_, m_, f_, b_, v_ = range(5)
    src = {"b_in": (b_, 0), "ln1_g": (m_, 2), "ln1_b": (m_, 3), "ln2_g": (f_, 1), "ln2_b": (f_, 2),
           "gmlp_ln_g": (v_, 2), "gmlp_ln_b": (v_, 3), "attn_out_g": (v_, 0), "gmlp_out_g": (v_, 1), "loss": (f_, 0)}
    dmod = [(i_, 1), (i_, 0), (m_, 4), (m_, 1), (m_, 0), (f_, 3)]

    def body(*refs):
        ins, wide_ref = refs[:5], refs[5]
        wide_ref[...] = jnp.zeros_like(wide_ref)
        for k, (a, row) in enumerate(dmod):
            wide_ref[0:1, k * D_MODEL:(k + 1) * D_MODEL] = ins[a][row:row + 1, :]
        for name, (a, row) in src.items():
            r, off, n = WIDE_LAYOUT[name]
            wide_ref[r:r + 1, off:off + n] = ins[a][row:row + 1, :]

    return pl.pallas_call(
        body, name="pack_wide", out_shape=jax.ShapeDtypeStruct((8, WIDE_W), F32), grid=(1,),
        in_specs=[_const_spec(a.shape) for a in arrs], out_specs=_const_spec((8, WIDE_W)),
        compiler_params=_params(("arbitrary",)),
    )(*arrs)


def _adam_small(gw, gt, wide_wmv, w_s, b_s, rel_bias, sinks, after):
    names = list(WIDE_PARAMS)
    tall = [("gmlp_w_s", w_s), ("gmlp_b_s", b_s), ("rel_bias", rel_bias), ("attn_sinks", sinks)]
    ins = [gw, gt]
    for n in names:
        ins += list(wide_wmv[n])
    for _, t in tall:
        ins += list(t)
    n_in = len(ins)

    def body(*refs):
        gw_ref, gt_ref = refs[0], refs[1]
        wmv = refs[2:n_in]
        dmod_ref, loss_ref = refs[n_in + 1], refs[n_in + 2]
        outs = refs[n_in + 3:]

        def tall_sum(r0, nr):
            g = gt_ref[r0:r0 + nr, :]
            for d in range(1, N_DEV):
                g = g + gt_ref[d * TALL_ROWS + r0:d * TALL_ROWS + r0 + nr, :]
            return g

        def emit(k, g, w_ref, m_ref, v_ref):
            d, m2, v2 = _adam_math(w_ref[...], g, m_ref[...], v_ref[...])
            outs[4 * k][...] = g
            outs[4 * k + 1][...] = d
            outs[4 * k + 2][...] = m2
            outs[4 * k + 3][...] = v2

        gsum = gw_ref[0:8, :]
        for d in range(1, N_DEV):
            gsum = gsum + gw_ref[8 * d:8 * d + 8, :]
        for d in range(N_DEV):
            dmod_ref[d:d + 1, :] = gw_ref[8 * d:8 * d + 1, :]
        for k, n in enumerate(names):
            r, off, sz = WIDE_LAYOUT[n]
            emit(k, gsum[r:r + 1, off:off + sz], *wmv[3 * k:3 * k + 3])
        r, off, sz = WIDE_LAYOUT["loss"]
        tot = jnp.sum(gsum[r:r + 1, off:off + sz], axis=1, keepdims=True)
        loss_ref[...] = jnp.broadcast_to(tot * (0.5 / D_MODEL), loss_ref.shape)

        k0 = len(names)
        ws_refs = wmv[3 * k0:3 * k0 + 3]
        for g in range(N_GROUPS):
            rows = slice(g * BLOCK, (g + 1) * BLOCK)
            gg = tall_sum(g * BLOCK, BLOCK)
            d, m2, v2 = _adam_math(ws_refs[0][rows, :], gg, ws_refs[1][rows, :], ws_refs[2][rows, :])
            outs[4 * k0][rows, :] = gg
            outs[4 * k0 + 1][rows, :] = d
            outs[4 * k0 + 2][rows, :] = m2
            outs[4 * k0 + 3][rows, :] = v2
        emit(k0 + 1, tall_sum(TALL_BS, N_GROUPS), *wmv[3 * (k0 + 1):3 * (k0 + 1) + 3])
        emit(k0 + 2, tall_sum(TALL_RB, N_BUCKETS)[:, :N_HEADS], *wmv[3 * (k0 + 2):3 * (k0 + 2) + 3])
        emit(k0 + 3, tall_sum(TALL_SK, 8)[0:1, :N_HEADS], *wmv[3 * (k0 + 3):3 * (k0 + 3) + 3])

    out_shapes = [jax.ShapeDtypeStruct((N_DEV, WIDE_W), F32), jax.ShapeDtypeStruct((8, LANES), F32)]
    for n in names:
        out_shapes += [jax.ShapeDtypeStruct(wide_wmv[n][0].shape, F32)] * 4
    for _, t in tall:
        out_shapes += [jax.ShapeDtypeStruct(t[0].shape, F32)] * 4
    res = pl.pallas_call(
        body, name="adam_small", out_shape=out_shapes, grid=(1,),
        in_specs=[_const_spec(a.shape) for a in ins] + [ANY], out_specs=[_const_spec(o.shape) for o in out_shapes],
        compiler_params=_params(("arbitrary",)),
    )(*ins, after)
    out = {}
    for k, n in enumerate(names + [t[0] for t in tall]):
        out[n] = tuple(res[2 + 4 * k:6 + 4 * k])
    return res[0], res[1], out


def kernel(x, c, rel_bias, w_ada, b_ada, w_in, b_in, attn_sinks, gmlp_ln_g, gmlp_ln_b, gmlp_w_s, gmlp_b_s, attn_out_g, gmlp_out_g, w_out, ln1_g, ln1_b, w_gate_up, w_down, ln2_g, ln2_b, loss_target, m_rel_bias, m_w_ada, m_b_ada, m_w_in, m_b_in, m_attn_sinks, m_gmlp_ln_g, m_gmlp_ln_b, m_gmlp_w_s, m_gmlp_b_s, m_attn_out_g, m_gmlp_out_g, m_w_out, m_ln1_g, m_ln1_b, m_w_gate_up, m_w_down, m_ln2_g, m_ln2_b, v_rel_bias, v_w_ada, v_b_ada, v_w_in, v_b_in, v_attn_sinks, v_gmlp_ln_g, v_gmlp_ln_b, v_gmlp_w_s, v_gmlp_b_s, v_attn_out_g, v_gmlp_out_g, v_w_out, v_ln1_g, v_ln1_b, v_w_gate_up, v_w_down, v_ln2_g, v_ln2_b):
    ix, iy, ic = _my_pos()
    chip = 2 * ix + iy
    dev = 4 * ix + 2 * iy + ic
    s = x.shape[1]
    xs = x[0]
    tgt = loss_target[0]
    tm_big = min(512, s)
    tm_ffn = min(FFN_SUB, s)
    n_ada = w_ada.shape[2]

    w_in_s, w_out_s = w_in[0].astype(BF16), w_out[0].astype(BF16)
    w_gu_s, w_dn_s = w_gate_up[0].astype(BF16), w_down[0].astype(BF16)
    sc_all, mod_rows, (w_in_g, w_out_g) = _prologue(
        jnp.pad(c, ((0, 7), (0, 0))), w_ada[0], lax.dynamic_slice_in_dim(b_ada, chip * n_ada, n_ada, axis=1),
        [w_in_s, w_out_s])
    mod_all = mod_rows.reshape(N_DEV, N_DEV, -1)
    mod_row = lax.dynamic_index_in_dim(mod_all[0::2], dev, axis=1, keepdims=False)
    modr = jnp.pad(mod_row.reshape(6, D_MODEL), ((0, 2), (0, 0)))
    w_in_f = _insert_own(w_in_g, w_in_s, "blk", chip)

    bucket = _bucket_table()
    bias, wsm = _prep_tables(bucket, rel_bias, gmlp_w_s[0])
    bsx = jnp.repeat(gmlp_b_s[0].T, GROUP_DIM, axis=1)
    amat = _group_mean_matrix()
    sinks = attn_sinks[0]

    (h1, q, kv, gu, gv, xb), (w_dn_g,) = _fwd_in(xs, modr, w_in_f, b_in, tm_big, [w_dn_s], ["blk"])
    w_out_f = _insert_own(w_out_g, w_out_s, "blk", chip).reshape(D_MODEL, D_MODEL)
    (x1, x1b, y, mixed), (w_gu_g,) = _fwd_mix(
        q, kv, gu, gv, xs, modr, bias, sinks, gmlp_ln_g, gmlp_ln_b, wsm, bsx, amat, attn_out_g, gmlp_out_g, w_out_f,
        ln1_g, ln1_b, tm_big, [w_gu_s], ["blk"])
    assert w_gate_up.shape[2] == FF_CHUNK
    w_gu_f = _insert_own(w_gu_g, w_gu_s, "blk", chip)
    w_dn_f = _insert_own(w_dn_g, w_dn_s, "blk", chip).reshape(D_FF, D_MODEL)
    h2, act, dy2, dx1a, acc_f = _fwd_ffn(x1, tgt, modr, ln2_g, ln2_b, w_gu_f, w_dn_f, tm_ffn)

    a_act, dgu_ff, dh2 = _bwd_ffn(dy2, act, w_gu_f, w_dn_f, min(FFN_SUB, s))
    g_dn, g_dn_b, _ = _wgrad(a_act, [dy2], D_FF // 2, min(1024, s), "wgrad_down")
    g_gu, g_gu_b, _ = _wgrad(h2, [dgu_ff], 512, min(512, s), "wgrad_gate_up")
    blk3 = lambda a, rows: a.reshape(N_CHIPS, rows, a.shape[1])
    (dxa, dy, dmix, acc_m), (got_dn, got_gu) = _bwd_mid(
        dh2, dx1a, x1b, xs, y, modr, ln1_g, w_out_f, tm_big, [blk3(g_dn_b, D_FF // N_CHIPS), g_gu_b], ["blk", "cols"])
    g_out, g_out_b, _ = _wgrad(mixed, [dy], 512, min(2048, s), "wgrad_out")
    (got_out,) = _swap_halves([blk3(g_out_b, D_MODEL // N_CHIPS)], ["blk"], "rs_swap_out")
    kinds_a = ["blk", "cols", "blk"]
    fulls_a = [blk3(g_dn, D_FF // N_CHIPS), g_gu, blk3(g_out, D_MODEL // N_CHIPS)]
    gots_a = [got_dn, got_gu, got_out]
    parts_a = [_add_halves(f, g, k, "rs_add_a%d" % i) for i, (f, g, k) in enumerate(zip(fulls_a, gots_a, kinds_a))]
    exchange_a = _SplitCopy(lambda s_, l_, ss, rs: _ChipExchange(s_, l_, kinds_a, ss, rs), [p[1] for p in parts_a],
                            [jax.ShapeDtypeStruct(_rx_shape(p[1].shape, k), BF16) for p, k in zip(parts_a, kinds_a)],
                            3 * len(parts_a), "rs_chips_a")
    dq, dkv, dgu, dgv, gbias, dws, dbs, vec, dsink = _bwd_mix(
        q, kv, gu, gv, dmix, bias, sinks, gmlp_ln_g, gmlp_ln_b, wsm, bsx, amat, attn_out_g, gmlp_out_g,
        exchange_a.token)
    rxs_a = exchange_a.wait(dq)
    tall_g = _mix_finalize(gbias, bucket, dws, dbs, dsink)
    grad_x, acc_i, db_in = _bwd_in(dq, dkv, dgu, dgv, dxa, xb, modr, w_in_f, tm_big)

    wide_g = _pack_wide(acc_i, acc_m, acc_f, db_in, vec)
    full_in, full_in_b, (gw, gt) = _wgrad(h1, [dq, dkv, dgu, dgv], 512, min(1024, s), "wgrad_in", owner_blocks=True,
                                          gather_vs=[wide_g, tall_g])
    (got_in,) = _swap_halves([full_in_b], ["blk"], "rs_swap_in")
    part_in = _add_halves(full_in, got_in, "blk", "rs_add_in")
    exchange_in = _SplitCopy(lambda s_, l_, ss, rs: _ChipExchange(s_, l_, ["blk"], ss, rs), [part_in[1]],
                             [jax.ShapeDtypeStruct(_rx_shape(part_in[1].shape, "blk"), BF16)], 3, "rs_chips_in")
    wide_wmv ={"b_ada": (b_ada, m_b_ada, v_b_ada), "b_in": (b_in, m_b_in, v_b_in),
                "ln1_g": (ln1_g, m_ln1_g, v_ln1_g), "ln1_b": (ln1_b, m_ln1_b, v_ln1_b),
                "ln2_g": (ln2_g, m_ln2_g, v_ln2_g), "ln2_b": (ln2_b, m_ln2_b, v_ln2_b),
                "gmlp_ln_g": (gmlp_ln_g, m_gmlp_ln_g, v_gmlp_ln_g), "gmlp_ln_b": (gmlp_ln_b, m_gmlp_ln_b, v_gmlp_ln_b),
                "attn_out_g": (attn_out_g, m_attn_out_g, v_attn_out_g),
                "gmlp_out_g": (gmlp_out_g, m_gmlp_out_g, v_gmlp_out_g)}
    rows2 = lambda a: a.reshape(-1, a.shape[-1])
    dmod_all, loss_t, small = _adam_small(
        gw, gt, wide_wmv, tuple(rows2(a) for a in (gmlp_w_s, m_gmlp_w_s, v_gmlp_w_s)),
        tuple(rows2(a) for a in (gmlp_b_s, m_gmlp_b_s, v_gmlp_b_s)), (rel_bias, m_rel_bias, v_rel_bias),
        (attn_sinks, m_attn_sinks, v_attn_sinks), exchange_in.token)
    loss = loss_t[0, 0]

    sums = [(parts_a[0][0], rxs_a[0], 176), (parts_a[1][0], rxs_a[1], 256), (parts_a[2][0], rxs_a[2], 128)]
    mine = [_sum_chips(p, rx, tr, "rs_sum_%d" % i, loss_t) for i, (p, rx, tr) in enumerate(sums)]
    share_a = _SplitCopy(_SiblingCopy, mine, mine, len(mine), "rs_share_a")
    dmod_cols = lax.dynamic_slice_in_dim(dmod_all, chip * n_ada, n_ada, axis=1)
    g_ada, d_ada, m_ada, v_ada = _adam_w_ada(sc_all.T, dmod_cols, w_ada[0], m_w_ada[0], v_w_ada[0], 256, share_a.token)
    got = share_a.wait(d_ada)
    gs_dn, d_dn, m_dn, v_dn = _adam_halves(w_down[0], mine[0], got[0], m_w_down[0], v_w_down[0], 176, "adam_w_down")
    gs_gu, d_gu, m_gu, v_gu = _adam_halves(w_gate_up[0], mine[1], got[1], m_w_gate_up[0], v_w_gate_up[0], 256,
                                           "adam_w_gate_up")
    gs_out, d_out, m_out, v_out = _adam_halves(w_out[0], mine[2], got[2], m_w_out[0], v_w_out[0], 128, "adam_w_out")

    (rx_in,) = exchange_in.wait(d_gu)
    mine_in = _sum_chips(part_in[0], rx_in, 256, "rs_sum_in", rx_in)
    (got_in_half,) = _share_halves([mine_in], "rs_share_in")
    gs_in, d_in, m_in, v_in = _adam_halves(w_in[0], mine_in, got_in_half, m_w_in[0], v_w_in[0], 256, "adam_w_in")

    big = {"w_ada": (g_ada, d_ada, m_ada, v_ada), "w_in": (gs_in, d_in, m_in, v_in), "w_out": (gs_out, d_out, m_out, v_out),
           "w_gate_up": (gs_gu, d_gu, m_gu, v_gu), "w_down": (gs_dn, d_dn, m_dn, v_dn)}
    order = ["rel_bias", "w_ada", "b_ada", "w_in", "b_in", "attn_sinks", "gmlp_ln_g", "gmlp_ln_b", "gmlp_w_s", "gmlp_b_s",
             "attn_out_g", "gmlp_out_g", "w_out", "ln1_g", "ln1_b", "w_gate_up", "w_down", "ln2_g", "ln2_b"]
    shapes = {"gmlp_w_s": gmlp_w_s.shape, "gmlp_b_s": gmlp_b_s.shape}
    outs = [loss, grad_x[None]]
    for k in range(4):
        for name in order:
            if name in big:
                outs.append(big[name][k][None])
            elif name in shapes:
                outs.append(small[name][k].reshape(shapes[name]))
            else:
                outs.append(small[name][k])
    return tuple(outs)
```

```python
import math

import numpy as np
import jax
import jax.numpy as jnp
from jax import lax
from jax.experimental import pallas as pl
from jax.experimental.pallas import tpu as pltpu

F32 = jnp.float32
BF16 = jnp.bfloat16
MESH = pl.DeviceIdType.MESH

D_MODEL = 1024
N_HEADS = 8
N_KV = 2
HEAD_DIM = 64
ATTN_W = N_HEADS * HEAD_DIM
KV_W = N_KV * HEAD_DIM
N_GROUPS = 8
GROUP_DIM = 64
GMLP_W = N_GROUPS * GROUP_DIM
IN_W = ATTN_W + 2 * KV_W + 2 * GMLP_W
BLOCK = 128
N_BUCKETS = 32
MAX_DISTANCE = 128
D_FF = 2816
ALPHA = 2.0 ** 0.25
LN_EPS = 1e-5
NEG_INF = -1e30
ADAM_LR, ADAM_B1, ADAM_B2, ADAM_EPS, ADAM_WD, ADAM_STEP = 0.001, 0.9, 0.999, 1e-8, 0.01, 10
N_CHIPS = 4
N_DEV = 8
LANES = 128
V7X_VMEM_LIMIT = 56 * 2 ** 20
GELU_C = math.sqrt(2.0 / math.pi)
Q_SCALE = HEAD_DIM ** -0.5
ANY = pl.BlockSpec(memory_space=pl.ANY)

TALL_BS = N_GROUPS * BLOCK
TALL_RB = TALL_BS + 8
TALL_SK = TALL_RB + N_BUCKETS
TALL_ROWS = TALL_SK + 8
WIDE_W = 6 * D_MODEL
WIDE_LAYOUT = {
    "b_ada": (0, 0, 6 * D_MODEL),
    "b_in": (1, 0, IN_W), "ln1_g": (1, IN_W, D_MODEL), "ln1_b": (1, IN_W + D_MODEL, D_MODEL),
    "ln2_g": (1, IN_W + 2 * D_MODEL, D_MODEL), "ln2_b": (1, IN_W + 3 * D_MODEL, D_MODEL),
    "gmlp_ln_g": (2, 0, GMLP_W), "gmlp_ln_b": (2, GMLP_W, GMLP_W), "attn_out_g": (2, 2 * GMLP_W, ATTN_W),
    "gmlp_out_g": (2, 2 * GMLP_W + ATTN_W, GMLP_W), "loss": (2, 3 * GMLP_W + ATTN_W, D_MODEL)}
WIDE_PARAMS = tuple(n for n in WIDE_LAYOUT if n != "loss")


def _params(sem=None):
    return pltpu.CompilerParams(dimension_semantics=sem, vmem_limit_bytes=V7X_VMEM_LIMIT)


def _const_spec(shape, single=False):
    nd = len(shape)
    if single:
        return pl.BlockSpec(shape, lambda *_: (0,) * nd, pipeline_mode=pl.Buffered(1))
    return pl.BlockSpec(shape, lambda *_: (0,) * nd)


def _dot(a, b):
    return jnp.dot(a, b, preferred_element_type=F32)


def _dot_nt(a, b):
    return lax.dot_general(a, b, (((1,), (1,)), ((), ())), preferred_element_type=F32)


def _dot_tn(a, b):
    return lax.dot_general(a, b, (((0,), (0,)), ((), ())), preferred_element_type=F32)


def _gelu(x):
    t = jnp.tanh(GELU_C * (x + 0.044715 * x * x * x))
    return 0.5 * x * (1.0 + t), t


def _gelu_grad(x, t):
    return 0.5 * (1.0 + t) + 0.5 * x * (1.0 - t * t) * GELU_C * (1.0 + 3.0 * 0.044715 * x * x)


def _split_dot(x, a):
    hi = x.astype(BF16)
    lo = (x - hi.astype(F32)).astype(BF16)
    return _dot(hi, a) + _dot(lo, a)


def _group_mean_matrix():
    g = np.arange(GMLP_W) // GROUP_DIM
    return jnp.asarray((g[:, None] == g[None, :]).astype(np.float32) / GROUP_DIM, dtype=BF16)


def _ln_stats(z):
    mu = jnp.mean(z, axis=-1, keepdims=True)
    d = z - mu
    var = jnp.mean(d * d, axis=-1, keepdims=True)
    rstd = lax.rsqrt(var + LN_EPS)
    return d * rstd, rstd


def _ln_bwd(dxhat, xhat, rstd):
    m1 = jnp.mean(dxhat, axis=-1, keepdims=True)
    m2 = jnp.mean(dxhat * xhat, axis=-1, keepdims=True)
    return rstd * (dxhat - m1 - xhat * m2)


def _colsum(x):
    return jnp.sum(x, axis=0, keepdims=True)


def _my_pos():
    return lax.axis_index("x"), lax.axis_index("y"), lax.axis_index("c")


def _other_chips(x, y):
    return [(1 - x, y), (x, 1 - y), (1 - x, 1 - y)]


def _chip_index_scalar():
    ix, iy, _ = _my_pos()
    return jnp.reshape(2 * ix + iy, (1,)).astype(jnp.int32)


def _core_index_scalar():
    return jnp.reshape(lax.axis_index("c"), (1,)).astype(jnp.int32)


class _Gather8:
    def __init__(self, x_refs, out_refs, send_sems, recv_sems, local_sems):
        self.x_refs, self.out_refs = x_refs, out_refs
        self.send_sems, self.recv_sems, self.local_sems = send_sems, recv_sems, local_sems
        self.x, self.y, self.c = _my_pos()
        self.me, self.sibling = (self.x, self.y, self.c), (self.x, self.y, 1 - self.c)
        self.chips = _other_chips(self.x, self.y)

    def _rows(self, a, px, py, pc):
        m_per = self.x_refs[a].shape[0]
        return self.out_refs[a].at[pl.ds((4 * px + 2 * py + pc) * m_per, m_per), :]

    def _copy(self, a, k, block, to, src=None):
        return pltpu.make_async_remote_copy(
            src_ref=self._rows(a, *block) if src is None else src, dst_ref=self._rows(a, *block),
            send_sem=self.send_sems.at[7 * a + k], recv_sem=self.recv_sems.at[7 * a + k], device_id=to,
            device_id_type=MESH)

    def _local(self, a):
        return pltpu.make_async_copy(self.x_refs[a], self._rows(a, *self.me), self.local_sems.at[a])

    def start(self):
        for a in range(len(self.x_refs)):
            self._local(a).start()
            self._copy(a, 0, self.me, self.sibling, src=self.x_refs[a]).start()
            for j, chip in enumerate(self.chips):
                self._copy(a, 1 + j, self.me, (*chip, self.c), src=self.x_refs[a]).start()

    def forward(self):
        for a in range(len(self.x_refs)):
            for j, chip in enumerate(self.chips):
                self._copy(a, 1 + j, (*chip, self.c), self.me).wait_recv()
                self._copy(a, 4 + j, (*chip, self.c), self.sibling).start()

    def finish(self):
        for a in range(len(self.x_refs)):
            self._copy(a, 0, self.sibling, self.me).wait_recv()
            for j, chip in enumerate(self.chips):
                self._copy(a, 4 + j, (*chip, 1 - self.c), self.me).wait_recv()
        for a in range(len(self.x_refs)):
            for k in range(7):
                self._copy(a, k, self.me, self.me).wait_send()
            self._local(a).wait()

    @staticmethod
    def sems(n_v):
        return [pltpu.SemaphoreType.DMA((7 * n_v,)), pltpu.SemaphoreType.DMA((7 * n_v,)),
                pltpu.SemaphoreType.DMA((n_v,))]


def _gathered8_shapes(vs):
    return [jax.ShapeDtypeStruct((N_DEV * v.shape[0], v.shape[1]), v.dtype) for v in vs]


VMEM_WHOLE = pl.BlockSpec(memory_space=pltpu.VMEM)


def _prologue(c_pad, w_ada_s, b_ada_s, shards):
    n = w_ada_s.shape[1]
    n_w = len(shards)

    def body(c_ref, w_ref, b_ref, *rest):
        shard_refs = rest[:n_w]
        sc_ref, modc_ref, modg_ref = rest[n_w:n_w + 3]
        gathered_refs = rest[n_w + 3:2 * n_w + 3]
        call_ref, w_vmem = rest[2 * n_w + 3:2 * n_w + 5]
        sems = rest[2 * n_w + 5:]
        weights = _WeightGather(shard_refs, gathered_refs, ["blk"] * n_w, sems[0], sems[1])
        gather_c = _Gather8([c_ref], [call_ref], sems[2], sems[3], sems[4])
        gather_mod = _Gather8([modc_ref], [modg_ref], sems[5], sems[6], sems[7])
        load_w = pltpu.make_async_copy(w_ref, w_vmem, sems[8])
        weights.start()
        gather_c.start()
        load_w.start()
        gather_c.forward()
        gather_c.finish()
        cv = call_ref[...]
        sc = cv * _sigmoid(cv)
        a_hi = sc.astype(BF16)
        a_lo = (sc - a_hi.astype(F32)).astype(BF16)
        load_w.wait()
        w = w_vmem[...]
        w_hi = w.astype(BF16)
        w_lo = (w - w_hi.astype(F32)).astype(BF16)
        mod = _dot(a_hi, w_hi) + _dot(a_hi, w_lo) + _dot(a_lo, w_hi) + b_ref[...]
        for d in range(N_DEV):
            sc_ref[d:d + 1, :] = sc[8 * d:8 * d + 1, :]
            modc_ref[d:d + 1, :] = mod[8 * d:8 * d + 1, :]
        gather_mod.start()
        weights.forward()
        gather_mod.forward()
        gather_mod.finish()
        weights.forward_diagonal()
        weights.finish()

    outs = pl.pallas_call(
        body, name="prologue",
        out_shape=[jax.ShapeDtypeStruct((N_DEV, D_MODEL), F32), jax.ShapeDtypeStruct((N_DEV, n), F32),
                   jax.ShapeDtypeStruct((N_DEV * N_DEV, n), F32)]
        + [jax.ShapeDtypeStruct(_gathered_shape(sh, "blk"), BF16) for sh in shards],
        in_specs=[VMEM_WHOLE, ANY, VMEM_WHOLE] + [ANY] * n_w,
        out_specs=[VMEM_WHOLE, VMEM_WHOLE, VMEM_WHOLE] + [ANY] * n_w,
        scratch_shapes=[pltpu.VMEM((N_DEV * 8, D_MODEL), F32), pltpu.VMEM(w_ada_s.shape, F32)]
        + _WeightGather.sems(n_w) + _Gather8.sems(1) + _Gather8.sems(1) + [pltpu.SemaphoreType.DMA],
        compiler_params=pltpu.CompilerParams(vmem_limit_bytes=V7X_VMEM_LIMIT),
    )(c_pad, w_ada_s, b_ada_s, *shards)
    return outs[0], outs[2], outs[3:]


def _gathered_shape(shard, kind):
    r, cc = shard.shape
    return (N_CHIPS, r, cc) if kind == "blk" else (r, N_CHIPS * cc)


class _WeightGather:
    N_SEM = 8

    def __init__(self, shards, gathered, kinds, send_sems, recv_sems):
        self.shards, self.gathered, self.kinds = shards, gathered, kinds
        self.send_sems, self.recv_sems = send_sems, recv_sems
        self.x, self.y, self.c = _my_pos()
        self.me, self.sibling = (self.x, self.y, self.c), (self.x, self.y, 1 - self.c)
        self.nbr = ((1 - self.x, self.y), (self.x, 1 - self.y))
        self.diag = 2 * (1 - self.x) + (1 - self.y)

    def _dst(self, a, chip, pc, quarter=None):
        r, cc = self.shards[a].shape
        h = r // 2
        row0, rows = pc * h, h
        if quarter is not None:
            row0, rows = pc * h + quarter * (h // 2), h // 2
        g = self.gathered[a]
        if self.kinds[a] == "blk":
            return g.at[chip, pl.ds(row0, rows), :]
        return g.at[pl.ds(row0, rows), pl.ds(chip * cc, cc)]

    def _copy(self, a, k, region, to, src=None):
        return pltpu.make_async_remote_copy(
            src_ref=region if src is None else src, dst_ref=region, send_sem=self.send_sems.at[a * self.N_SEM + k],
            recv_sem=self.recv_sems.at[a * self.N_SEM + k], device_id=to, device_id_type=MESH)

    def _arrays(self):
        return range(len(self.shards))

    def start(self):
        my_chip = 2 * self.x + self.y
        for a in self._arrays():
            h = self.shards[a].shape[0] // 2
            mine = self.shards[a].at[pl.ds(self.c * h, h), :]
            for j, chip in enumerate(self.nbr):
                self._copy(a, j, self._dst(a, my_chip, self.c), (*chip, self.c), src=mine).start()

    def forward(self):
        for a in self._arrays():
            for j, chip in enumerate(self.nbr):
                cj = 2 * chip[0] + chip[1]
                half = self._dst(a, cj, self.c)
                self._copy(a, j, half, self.me).wait_recv()
                self._copy(a, 2 + j, half, self.sibling).start()
                other = self.nbr[1 - j]
                self._copy(a, 4 + j, self._dst(a, cj, self.c, quarter=j), (*other, self.c)).start()

    def forward_diagonal(self):
        for a in self._arrays():
            for j in range(2):
                quarter = self._dst(a, self.diag, self.c, quarter=j)
                self._copy(a, 4 + j, quarter, self.me).wait_recv()
                self._copy(a, 6 + j, quarter, self.sibling).start()

    def finish(self):
        for a in self._arrays():
            for j, chip in enumerate(self.nbr):
                self._copy(a, 2 + j, self._dst(a, 2 * chip[0] + chip[1], 1 - self.c), self.me).wait_recv()
                self._copy(a, 6 + j, self._dst(a, self.diag, 1 - self.c, quarter=j), self.me).wait_recv()
        for a in self._arrays():
            half = self._dst(a, self.diag, self.c)
            quarter = self._dst(a, self.diag, self.c, quarter=0)
            for k in range(self.N_SEM):
                self._copy(a, k, half if k < 4 else quarter, self.me).wait_send()

    @classmethod
    def sems(cls, n_arr):
        return [pltpu.SemaphoreType.DMA((n_arr * cls.N_SEM,)), pltpu.SemaphoreType.DMA((n_arr * cls.N_SEM,))]


def _insert_own(gathered, shard, kind, chip):
    if kind == "blk":
        return lax.dynamic_update_slice(gathered, shard[None], (chip, 0, 0))
    return lax.dynamic_update_slice(gathered, shard, (0, chip * shard.shape[1]))


def _half_of_full(ref, kind, pc):
    if kind == "blk":
        h = ref.shape[1] // 2
        return ref.at[:, pl.ds(pc * h, h), :]
    h = ref.shape[0] // 2
    return ref.at[pl.ds(pc * h, h), :]


def _half_shape(shape, kind):
    return (shape[0], shape[1] // 2, shape[2]) if kind == "blk" else (shape[0] // 2, shape[1])


class _HalfSwap:
    def __init__(self, ins, outs, kinds, send_sems, recv_sems):
        self.ins, self.outs, self.kinds = ins, outs, kinds
        self.send_sems, self.recv_sems = send_sems, recv_sems
        self.x, self.y, self.c = _my_pos()

    def _copies(self):
        for a in range(len(self.ins)):
            yield pltpu.make_async_remote_copy(
                src_ref=_half_of_full(self.ins[a], self.kinds[a], 1 - self.c), dst_ref=self.outs[a],
                send_sem=self.send_sems.at[a], recv_sem=self.recv_sems.at[a],
                device_id=(self.x, self.y, 1 - self.c), device_id_type=MESH)

    def start(self):
        for cp in self._copies():
            cp.start()

    def wait(self):
        for cp in self._copies():
            cp.wait()

    @staticmethod
    def sems(n_arr):
        return [pltpu.SemaphoreType.DMA((n_arr,)), pltpu.SemaphoreType.DMA((n_arr,))]

    @staticmethod
    def out_shapes(fulls, kinds):
        return [jax.ShapeDtypeStruct(_half_shape(a.shape, k), a.dtype) for a, k in zip(fulls, kinds)]


def _swap_halves(fulls_bf16, kinds, name):
    n_arr = len(fulls_bf16)

    def body(*refs):
        swap = _HalfSwap(refs[:n_arr], refs[n_arr:2 * n_arr], kinds, *refs[2 * n_arr:])
        swap.start()
        swap.wait()

    return pl.pallas_call(
        body, name=name, out_shape=_HalfSwap.out_shapes(fulls_bf16, kinds),
        in_specs=[ANY] * n_arr, out_specs=[ANY] * n_arr, scratch_shapes=_HalfSwap.sems(n_arr),
    )(*fulls_bf16)


def _add_halves(full, got, kind, name):
    hs = _half_shape(full.shape, kind)

    def body(pos_ref, a_ref, b_ref, o_ref, ob_ref):
        p = a_ref[...] + b_ref[...].astype(F32)
        ob_ref[...] = p.astype(BF16)

        @pl.when(pl.program_id(0) == pos_ref[1])
        def _():
            o_ref[...] = p.reshape(o_ref.shape)

    if kind == "blk":
        nb, h, cc = hs
        own = pl.BlockSpec((1, h, cc), lambda b, pos_ref: (b, pos_ref[0], 0))
        other = pl.BlockSpec((1, h, cc), lambda b, pos_ref: (b, 0, 0))
    else:
        h, cc = hs[0], hs[1] // N_CHIPS
        own = pl.BlockSpec((h, cc), lambda b, pos_ref: (pos_ref[0], b))
        other = pl.BlockSpec((h, cc), lambda b, pos_ref: (0, b))
    pos = jnp.concatenate([_core_index_scalar(), _chip_index_scalar()])
    return pl.pallas_call(
        body, name=name, out_shape=(jax.ShapeDtypeStruct((h, cc), F32), jax.ShapeDtypeStruct(hs, BF16)),
        grid_spec=pltpu.PrefetchScalarGridSpec(
            num_scalar_prefetch=1, grid=(N_CHIPS,), in_specs=[own, other],
            out_specs=(pl.BlockSpec((h, cc), lambda b, pos_ref: (0, 0)), other)),
        compiler_params=_params(("arbitrary",)),
    )(pos, full, got)


def _rx_shape(part_shape, kind):
    if kind == "blk":
        return (3, part_shape[1], part_shape[2])
    return (3, part_shape[0], part_shape[1] // N_CHIPS)


class _ChipExchange:
    def __init__(self, parts, rxs, kinds, send_sems, recv_sems):
        self.parts, self.rxs, self.kinds = parts, rxs, kinds
        self.send_sems, self.recv_sems = send_sems, recv_sems
        self.x, self.y, self.c = _my_pos()
        self.chips = _other_chips(self.x, self.y)

    def _copies(self):
        for a in range(len(self.parts)):
            for j, chip in enumerate(self.chips):
                cj = 2 * chip[0] + chip[1]
                if self.kinds[a] == "blk":
                    src = self.parts[a].at[cj]
                else:
                    cc = self.parts[a].shape[1] // N_CHIPS
                    src = self.parts[a].at[:, pl.ds(cj * cc, cc)]
                yield pltpu.make_async_remote_copy(
                    src_ref=src, dst_ref=self.rxs[a].at[j], send_sem=self.send_sems.at[a * 3 + j],
                    recv_sem=self.recv_sems.at[a * 3 + j], device_id=(*chip, self.c), device_id_type=MESH)

    def start(self):
        for cp in self._copies():
            cp.start()

    def wait(self):
        for cp in self._copies():
            cp.wait_recv()
        for cp in self._copies():
            cp.wait_send()

    @staticmethod
    def sems(n_arr):
        return [pltpu.SemaphoreType.DMA((n_arr * 3,)), pltpu.SemaphoreType.DMA((n_arr * 3,))]


HBM_SPEC = pl.BlockSpec(memory_space=pltpu.HBM)
SEM_SPEC = pl.BlockSpec(memory_space=pltpu.SEMAPHORE)
DATAFLOW = pltpu.SideEffectType.DATAFLOW_SIDE_EFFECTING


class _SplitCopy:
    def __init__(self, copier_of, srcs, lands, n_sems, name):
        self.copier_of, self.name = copier_of, name
        n_s, n_l = len(srcs), len(lands)
        self.n_s, self.n_l = n_s, n_l

        def body(*refs):
            copier_of(refs[:n_s], refs[n_s:n_s + n_l], refs[n_s + n_l], refs[n_s + n_l + 1]).start()
            refs[-1][...] = jnp.zeros_like(refs[-1])

        sems = pltpu.SemaphoreType.DMA((n_sems,))
        outs = pl.pallas_call(
            body, name=name + "_start",
            out_shape=[sems, sems] + [pltpu.HBM(a.shape, a.dtype) for a in srcs]
            + [pltpu.HBM(l.shape, l.dtype) for l in lands] + [jax.ShapeDtypeStruct((8, LANES), F32)],
            in_specs=[HBM_SPEC] * (n_s + n_l), out_specs=[SEM_SPEC, SEM_SPEC] + [HBM_SPEC] * (n_s + n_l) + [VMEM_WHOLE],
            input_output_aliases={i: 2 + i for i in range(n_s + n_l)},
            compiler_params=pltpu.CompilerParams(has_side_effects=DATAFLOW),
        )(*[pltpu.with_memory_space_constraint(a, pltpu.HBM) for a in srcs],
          *[pltpu.with_memory_space_constraint(lax.empty(l.shape, l.dtype), pltpu.HBM) for l in lands])
        self.sems, self.thru, self.token = outs[:2], outs[2:-1], outs[-1]

    def wait(self, after):
        n_s, n_l, copier_of = self.n_s, self.n_l, self.copier_of

        def body(*refs):
            copier_of(refs[:n_s], refs[n_s:n_s + n_l], refs[n_s + n_l], refs[n_s + n_l + 1]).wait()

        outs = pl.pallas_call(
            body, name=self.name + "_wait", out_shape=[pltpu.HBM(t.shape, t.dtype) for t in self.thru],
            in_specs=[HBM_SPEC] * (n_s + n_l) + [SEM_SPEC, SEM_SPEC, ANY], out_specs=[HBM_SPEC] * (n_s + n_l),
            input_output_aliases={i: i for i in range(n_s + n_l)},
            compiler_params=pltpu.CompilerParams(has_side_effects=DATAFLOW),
        )(*self.thru, *self.sems, after)
        return outs[n_s:]


class _SiblingCopy:
    def __init__(self, srcs, lands, send_sems, recv_sems):
        self.srcs, self.lands, self.send_sems, self.recv_sems = srcs, lands, send_sems, recv_sems
        self.x, self.y, self.c = _my_pos()

    def _copies(self):
        for a in range(len(self.srcs)):
            yield pltpu.make_async_remote_copy(
                src_ref=self.srcs[a], dst_ref=self.lands[a], send_sem=self.send_sems.at[a],
                recv_sem=self.recv_sems.at[a], device_id=(self.x, self.y, 1 - self.c), device_id_type=MESH)

    def start(self):
        for cp in self._copies():
            cp.start()

    def wait(self):
        for cp in self._copies():
            cp.wait()


def _sum_chips(part, rx, tr, name, after):
    _, h, cc = rx.shape
    flips = (2, 1, 3)

    def body(chip_ref, p_ref, rx_ref, after_ref, o_ref):
        own = p_ref[...]
        for mc in range(N_CHIPS):
            @pl.when(chip_ref[0] == mc)
            def _():
                terms = sorted([(mc, None)] + [(mc ^ f, j) for j, f in enumerate(flips)])
                acc = None
                for _, j in terms:
                    t = own if j is None else rx_ref[j].astype(F32)
                    acc = t if acc is None else acc + t
                o_ref[...] = acc

    return pl.pallas_call(
        body, name=name, out_shape=jax.ShapeDtypeStruct((h, cc), F32),
        grid_spec=pltpu.PrefetchScalarGridSpec(
            num_scalar_prefetch=1, grid=(h // tr,),
            in_specs=[pl.BlockSpec((tr, cc), lambda i, chip_ref: (i, 0)),
                      pl.BlockSpec((3, tr, cc), lambda i, chip_ref: (0, i, 0)), ANY],
            out_specs=pl.BlockSpec((tr, cc), lambda i, chip_ref: (i, 0))),
        compiler_params=_params(("arbitrary",)),
    )(_chip_index_scalar(), part, rx, after)


def _share_halves(halves, name):
    n_arr = len(halves)

    def body(*refs):
        ins, outs = refs[:n_arr], refs[n_arr:2 * n_arr]
        send_sems, recv_sems = refs[2 * n_arr:]
        x, y, c = _my_pos()
        cps = []
        for a in range(n_arr):
            cp = pltpu.make_async_remote_copy(
                src_ref=ins[a], dst_ref=outs[a], send_sem=send_sems.at[a], recv_sem=recv_sems.at[a],
                device_id=(x, y, 1 - c), device_id_type=MESH)
            cp.start()
            cps.append(cp)
        for cp in cps:
            cp.wait()

    return pl.pallas_call(
        body, name=name, out_shape=[jax.ShapeDtypeStruct(h.shape, h.dtype) for h in halves],
        in_specs=[ANY] * n_arr, out_specs=[ANY] * n_arr,
        scratch_shapes=[pltpu.SemaphoreType.DMA((n_arr,)), pltpu.SemaphoreType.DMA((n_arr,))],
    )(*halves)


def _bucket_table():
    qi = jnp.arange(BLOCK)[:, None]
    si = jnp.arange(2 * BLOCK)[None, :]
    dist = qi + BLOCK - si
    max_exact = N_BUCKETS // 2
    n = jnp.maximum(dist, 0)
    nf = jnp.maximum(n, max_exact).astype(F32)
    large = max_exact + (jnp.log(nf / max_exact) / math.log(MAX_DISTANCE / max_exact)
                         * (N_BUCKETS - max_exact)).astype(jnp.int32)
    large = jnp.minimum(large, N_BUCKETS - 1)
    return jnp.where(n < max_exact, n, large).astype(F32)


def _prep_tables(bucket, rel_bias, w_s):
    def body(bucket_ref, rb_ref, ws_ref, bias_ref, wsm_ref):
        qi = lax.broadcasted_iota(jnp.int32, (BLOCK, 2 * BLOCK), 0)
        si = lax.broadcasted_iota(jnp.int32, (BLOCK, 2 * BLOCK), 1)
        dist = qi + BLOCK - si
        in_window = (dist >= 0) & (dist < BLOCK)
        bk = bucket_ref[...]
        for h in range(N_HEADS):
            acc = jnp.zeros((BLOCK, 2 * BLOCK), F32)
            for b in range(N_BUCKETS):
                acc = jnp.where(bk == float(b), rb_ref[b, h], acc)
            bias_ref[h] = jnp.where(in_window, acc, NEG_INF)
        ti = lax.broadcasted_iota(jnp.int32, (BLOCK, BLOCK), 0)
        ui = lax.broadcasted_iota(jnp.int32, (BLOCK, BLOCK), 1)
        for g in range(N_GROUPS):
            wsm_ref[g] = jnp.where(ti >= ui, ws_ref[g], 0.0).astype(BF16)

    return pl.pallas_call(
        body, name="prep_tables",
        out_shape=(jax.ShapeDtypeStruct((N_HEADS, BLOCK, 2 * BLOCK), F32),
                   jax.ShapeDtypeStruct((N_GROUPS, BLOCK, BLOCK), BF16)),
        grid=(1,),
        in_specs=[_const_spec((BLOCK, 2 * BLOCK)), pl.BlockSpec(memory_space=pltpu.SMEM),
                  _const_spec((N_GROUPS, BLOCK, BLOCK))],
        out_specs=(_const_spec((N_HEADS, BLOCK, 2 * BLOCK)), _const_spec((N_GROUPS, BLOCK, BLOCK))),
        compiler_params=_params(("arbitrary",)),
    )(bucket, rel_bias, w_s)


def _fwd_in(x, modr, w_in, b_in, tm, shards, kinds):
    s = x.shape[0]
    n_steps = s // tm
    fwd_step, diag_step = (10 * n_steps) // 16, (14 * n_steps) // 16
    n_w = len(shards)

    def body(x_ref, mod_ref, w_ref, b_ref, *rest):
        shard_refs = rest[:n_w]
        h1_ref, q_ref, kv_ref, gu_ref, gv_ref, xb_ref = rest[n_w:n_w + 6]
        gathered_refs = rest[n_w + 6:2 * n_w + 6]
        send_sems, recv_sems = rest[2 * n_w + 6:]
        i = pl.program_id(0)
        gather = _WeightGather(shard_refs, gathered_refs, kinds, send_sems, recv_sems)

        @pl.when(i == 0)
        def _():
            gather.start()

        xv = x_ref[...]
        xb_ref[...] = xv.astype(BF16)
        h1 = (xv * (1.0 + mod_ref[1:2, :]) + mod_ref[0:1, :]).astype(BF16)
        h1_ref[...] = h1
        proj = jnp.concatenate([_dot(h1, w_ref[j]) for j in range(N_CHIPS)], axis=1) + b_ref[...]
        q_ref[...] = (proj[:, :ATTN_W] * Q_SCALE).astype(BF16)
        kv_ref[...] = proj[:, ATTN_W:ATTN_W + 2 * KV_W].astype(BF16)
        gu_ref[...] = proj[:, ATTN_W + 2 * KV_W:ATTN_W + 2 * KV_W + GMLP_W]
        gv_ref[...] = proj[:, ATTN_W + 2 * KV_W + GMLP_W:]

        @pl.when(i == fwd_step)
        def _():
            gather.forward()

        @pl.when(i == diag_step)
        def _():
            gather.forward_diagonal()

        @pl.when(i == n_steps - 1)
        def _():
            gather.finish()

    row = lambda w: pl.BlockSpec((tm, w), lambda i: (i, 0))
    outs = pl.pallas_call(
        body, name="fwd_in",
        out_shape=[jax.ShapeDtypeStruct((s, D_MODEL), BF16), jax.ShapeDtypeStruct((s, ATTN_W), BF16),
                   jax.ShapeDtypeStruct((s, 2 * KV_W), BF16), jax.ShapeDtypeStruct((s, GMLP_W), F32),
                   jax.ShapeDtypeStruct((s, GMLP_W), F32), jax.ShapeDtypeStruct((s, D_MODEL), BF16)]
        + [jax.ShapeDtypeStruct(_gathered_shape(sh, k), BF16) for sh, k in zip(shards, kinds)],
        grid=(n_steps,),
        in_specs=[row(D_MODEL), _const_spec((8, D_MODEL)), _const_spec(w_in.shape), _const_spec((1, IN_W))]
        + [ANY] * n_w,
        out_specs=[row(D_MODEL), row(ATTN_W), row(2 * KV_W), row(GMLP_W), row(GMLP_W), row(D_MODEL)] + [ANY] * n_w,
        scratch_shapes=_WeightGather.sems(n_w),
        compiler_params=_params(("arbitrary",)),
    )(x, modr, w_in, b_in, *shards)
    return outs[:6], outs[6:]


def _kv_variants(kk):
    kf = kk.astype(F32)
    lane = lax.broadcasted_iota(jnp.int32, kf.shape, 1)
    low = lane < HEAD_DIM
    k0_lo = jnp.where(low, kf, 0.0)
    k1_hi = jnp.where(low, 0.0, kf)
    k0_hi = pltpu.roll(k0_lo, HEAD_DIM, 1)
    k1_lo = pltpu.roll(k1_hi, HEAD_DIM, 1)
    return ((k0_lo.astype(BF16), k0_hi.astype(BF16)), (k1_lo.astype(BF16), k1_hi.astype(BF16)))


def _head_kv(h):
    return h // (N_HEADS // N_KV), h % 2


MIX_GROUP = 2


def _interleave(*gens):
    results = [None] * len(gens)
    active = list(enumerate(gens))
    while active:
        still = []
        for i, g in active:
            try:
                next(g)
                still.append((i, g))
            except StopIteration as done:
                results[i] = done.value
        active = still
    return results


def _attn_block_fwd(q_blk, kk, vv, bias_ref, sinks_ref, first_mask):
    kvar = _kv_variants(kk)
    vvar = _kv_variants(vv)
    heads = range(N_HEADS)
    q_pairs = [q_blk[:, (h // 2) * LANES:(h // 2 + 1) * LANES] for h in heads]
    logits = [_dot_nt(q_pairs[h], kvar[_head_kv(h)[0]][_head_kv(h)[1]]) + bias_ref[h] for h in heads]
    if first_mask is not None:
        logits = [jnp.where(first_mask, NEG_INF, lg) for lg in logits]
    yield
    ms = [jnp.maximum(jnp.max(logits[h], axis=-1, keepdims=True), sinks_ref[h]) for h in heads]
    yield
    es = [jnp.exp(logits[h] - ms[h]) for h in heads]
    ess = [jnp.exp(sinks_ref[h] - ms[h]) for h in heads]
    yield
    invs = [1.0 / (jnp.sum(es[h], axis=-1, keepdims=True) + ess[h]) for h in heads]
    probs = [(es[h] * invs[h], ess[h] * invs[h]) for h in heads]
    yield
    outs = [_dot(probs[h][0].astype(BF16), vvar[_head_kv(h)[0]][_head_kv(h)[1]]) for h in heads]
    pairs = [outs[2 * i] + outs[2 * i + 1] for i in range(N_HEADS // 2)]
    return jnp.concatenate(pairs, axis=1), probs, kvar, vvar


def _gmlp_chunk_fwd(gu, gv, ln_g, ln_b, wsm_ref, bsx, amat):
    u, tu = _gelu(gu)
    a, ta = _gelu(gv)
    yield
    mean = _split_dot(a, amat)
    d = a - mean
    yield
    var = _split_dot(d * d, amat)
    yield
    rstd = lax.rsqrt(var + LN_EPS)
    xhat = d * rstd
    vb = (xhat * ln_g + ln_b).astype(BF16)
    yield
    lane = lax.broadcasted_iota(jnp.int32, (BLOCK, LANES), 1)
    low = lane < GROUP_DIM
    cols = []
    for pair in range(N_GROUPS // 2):
        vp = vb[:, pair * LANES:(pair + 1) * LANES]
        cols.append(jnp.where(low, _dot(wsm_ref[2 * pair], vp), _dot(wsm_ref[2 * pair + 1], vp)))
    mixedv = jnp.concatenate(cols, axis=1) + bsx
    return u * mixedv, (u, tu, ta, xhat, rstd, vb, mixedv)


def _rms(a, g):
    r = lax.rsqrt(jnp.mean(a * a, axis=-1, keepdims=True) + LN_EPS)
    return a * r * g, r


def _fwd_mix(q, kv, gu, gv, x, modr, bias, sinks, gln_g, gln_b, wsm, bsx, amat, aog, gog, w_out, ln1_g, ln1_b, tm,
             ffn_shards, ffn_kinds):
    s = x.shape[0]
    nb = tm // BLOCK
    n_steps = s // tm
    fwd_step, diag_step = (9 * n_steps) // 16, (13 * n_steps) // 16
    n_w = len(ffn_shards)

    def body(q_ref, kv_ref, kvp_ref, gu_ref, gv_ref, x_ref, mod_ref, bias_ref, sinks_ref, glng_ref, glnb_ref, wsm_ref,
             bsx_ref, amat_ref, aog_ref, gog_ref, wout_ref, ln1g_ref, ln1b_ref, *rest):
        shard_refs = rest[:n_w]
        x1_ref, x1b_ref, y_ref, mixed_ref = rest[n_w:n_w + 4]
        gathered_refs = rest[n_w + 4:2 * n_w + 4]
        mix_scr, send_sems, recv_sems = rest[2 * n_w + 4:]
        i = pl.program_id(0)
        gather = _WeightGather(shard_refs, gathered_refs, ffn_kinds, send_sems, recv_sems)

        @pl.when(i == 0)
        def _():
            gather.start()

        col = lax.broadcasted_iota(jnp.int32, (BLOCK, 2 * BLOCK), 1)
        for b0 in range(0, nb, MIX_GROUP):
            gens = []
            for b in range(b0, min(b0 + MIX_GROUP, nb)):
                r0 = b * BLOCK
                if b == 0:
                    kvprev = kvp_ref[...]
                    first_mask = (col < BLOCK) & (i == 0)
                else:
                    kvprev = kv_ref[r0 - BLOCK:r0, :]
                    first_mask = None
                kvcur = kv_ref[r0:r0 + BLOCK, :]
                kk = jnp.concatenate([kvprev[:, :KV_W], kvcur[:, :KV_W]], axis=0)
                vv = jnp.concatenate([kvprev[:, KV_W:], kvcur[:, KV_W:]], axis=0)
                gens.append(_attn_block_fwd(q_ref[r0:r0 + BLOCK, :], kk, vv, bias_ref, sinks_ref, first_mask))
                gens.append(_gmlp_chunk_fwd(gu_ref[r0:r0 + BLOCK, :], gv_ref[r0:r0 + BLOCK, :], glng_ref[...],
                                            glnb_ref[...], wsm_ref, bsx_ref[...], amat_ref[...]))
            res = _interleave(*gens)
            for k, b in enumerate(range(b0, min(b0 + MIX_GROUP, nb))):
                r0 = b * BLOCK
                na, _ = _rms(res[2 * k][0], aog_ref[...])
                ng, _ = _rms(res[2 * k + 1][0], gog_ref[...])
                mix_scr[r0:r0 + BLOCK, :ATTN_W] = na.astype(BF16)
                mix_scr[r0:r0 + BLOCK, ATTN_W:] = ng.astype(BF16)
        mixed = mix_scr[...]
        mixed_ref[...] = mixed
        y = _dot(mixed, wout_ref[...])
        y_ref[...] = y.astype(BF16)
        z1 = ALPHA * x_ref[...] + mod_ref[2:3, :] * y
        xhat, _ = _ln_stats(z1)
        x1 = xhat * ln1g_ref[...] + ln1b_ref[...]
        x1_ref[...] = x1
        x1b_ref[...] = x1.astype(BF16)

        @pl.when(i == fwd_step)
        def _():
            gather.forward()

        @pl.when(i == diag_step)
        def _():
            gather.forward_diagonal()

        @pl.when(i == n_steps - 1)
        def _():
            gather.finish()

    row = lambda w: pl.BlockSpec((tm, w), lambda i: (i, 0))
    prev = pl.BlockSpec((BLOCK, 2 * KV_W), lambda i: (jnp.maximum(i * nb - 1, 0), 0))
    outs = pl.pallas_call(
        body, name="fwd_mix",
        out_shape=[jax.ShapeDtypeStruct((s, D_MODEL), F32)] + [jax.ShapeDtypeStruct((s, D_MODEL), BF16)] * 3
        + [jax.ShapeDtypeStruct(_gathered_shape(sh, k), BF16) for sh, k in zip(ffn_shards, ffn_kinds)],
        grid=(n_steps,),
        in_specs=[row(ATTN_W), row(2 * KV_W), prev, row(GMLP_W), row(GMLP_W), row(D_MODEL), _const_spec((8, D_MODEL)),
                  _const_spec((N_HEADS, BLOCK, 2 * BLOCK)), pl.BlockSpec(memory_space=pltpu.SMEM),
                  _const_spec((1, GMLP_W)), _const_spec((1, GMLP_W)), _const_spec((N_GROUPS, BLOCK, BLOCK)),
                  _const_spec((BLOCK, GMLP_W)), _const_spec((GMLP_W, GMLP_W)), _const_spec((1, ATTN_W)),
                  _const_spec((1, GMLP_W)), _const_spec((D_MODEL, D_MODEL)), _const_spec((1, D_MODEL)),
                  _const_spec((1, D_MODEL))] + [ANY] * n_w,
        out_specs=[row(D_MODEL)] * 4 + [ANY] * n_w,
        scratch_shapes=[pltpu.VMEM((tm, D_MODEL), BF16)] + _WeightGather.sems(n_w),
        compiler_params=_params(("arbitrary",)),
    )(q, kv, kv, gu, gv, x, modr, bias, sinks, gln_g, gln_b, wsm, bsx, amat, aog, gog, w_out, ln1_g, ln1_b, *ffn_shards)
    return outs[:4], outs[4:]


FF_BLOCKS = N_CHIPS // 2
FF_CHUNK = D_FF // FF_BLOCKS
FFN_SUB = 256


def _sigmoid(x):
    return 1.0 / (1.0 + jnp.exp(-x))


def _fwd_ffn(x1, target, modr, ln2_g, ln2_b, w_gu, w_dn, tm):
    s = x1.shape[0]

    def body(x1_ref, t_ref, mod_ref, g_ref, b_ref, wgu_ref, wdn_ref, h2_ref, act_ref, dy2_ref, dx1a_ref, acc_ref):
        @pl.when(pl.program_id(0) == 0)
        def _():
            acc_ref[...] = jnp.zeros_like(acc_ref)

        x1v = x1_ref[...]
        h2 = (x1v * (1.0 + mod_ref[4:5, :]) + mod_ref[3:4, :]).astype(BF16)
        h2_ref[...] = h2
        y2 = None
        for cc in range(FF_BLOCKS):
            c0 = cc * FF_CHUNK
            gate = _dot(h2, wgu_ref[cc])
            up = _dot(h2, wgu_ref[FF_BLOCKS + cc])
            act_ref[:, c0:c0 + FF_CHUNK] = gate.astype(BF16)
            act_ref[:, D_FF + c0:D_FF + c0 + FF_CHUNK] = up.astype(BF16)
            a = (gate * _sigmoid(gate) * up).astype(BF16)
            part = _dot(a, wdn_ref[c0:c0 + FF_CHUNK, :])
            y2 = part if y2 is None else y2 + part
        g2 = mod_ref[5:6, :]
        z2 = ALPHA * x1v + g2 * y2
        xhat, rstd = _ln_stats(z2)
        gain = g_ref[...]
        diff = xhat * gain + b_ref[...] - t_ref[...]
        dx2 = diff * (1.0 / D_MODEL)
        dz2 = _ln_bwd(dx2 * gain, xhat, rstd)
        dx1a_ref[...] = ALPHA * dz2
        dy2_ref[...] = (g2 * dz2).astype(BF16)
        acc_ref[0:1, :] += _colsum(diff * diff)
        acc_ref[1:2, :] += _colsum(dx2 * xhat)
        acc_ref[2:3, :] += _colsum(dx2)
        acc_ref[3:4, :] += _colsum(dz2 * y2)

    row = lambda w: pl.BlockSpec((tm, w), lambda i: (i, 0))
    return pl.pallas_call(
        body, name="fwd_ffn",
        out_shape=(jax.ShapeDtypeStruct((s, D_MODEL), BF16), jax.ShapeDtypeStruct((s, 2 * D_FF), BF16),
                   jax.ShapeDtypeStruct((s, D_MODEL), BF16), jax.ShapeDtypeStruct((s, D_MODEL), F32),
                   jax.ShapeDtypeStruct((8, D_MODEL), F32)),
        grid=(s // tm,),
        in_specs=[row(D_MODEL), row(D_MODEL), _const_spec((8, D_MODEL)), _const_spec((1, D_MODEL)),
                  _const_spec((1, D_MODEL)), _const_spec((N_CHIPS, D_MODEL, FF_CHUNK), single=True),
                  _const_spec((D_FF, D_MODEL), single=True)],
        out_specs=(row(D_MODEL), row(2 * D_FF), row(D_MODEL), row(D_MODEL), _const_spec((8, D_MODEL))),
        compiler_params=_params(("arbitrary",)),
    )(x1, target, modr, ln2_g, ln2_b, w_gu, w_dn)


def _bwd_ffn(dy2, act, w_gu, w_dn, tm):
    s = dy2.shape[0]

    def body(dy2_ref, act_ref, wgu_ref, wdn_ref, a_ref, dgu_ref, dh2_ref):
        dy2v = dy2_ref[...]
        dh2 = None
        for cc in range(FF_BLOCKS):
            c0 = cc * FF_CHUNK
            da = _dot_nt(dy2v, wdn_ref[c0:c0 + FF_CHUNK, :])
            gate = act_ref[:, c0:c0 + FF_CHUNK].astype(F32)
            up = act_ref[:, D_FF + c0:D_FF + c0 + FF_CHUNK].astype(F32)
            sg = _sigmoid(gate)
            sl = gate * sg
            a_ref[:, c0:c0 + FF_CHUNK] = (sl * up).astype(BF16)
            dgate = (da * up * (sg * (1.0 + gate * (1.0 - sg)))).astype(BF16)
            dup = (da * sl).astype(BF16)
            dgu_ref[:, c0:c0 + FF_CHUNK] = dgate
            dgu_ref[:, D_FF + c0:D_FF + c0 + FF_CHUNK] = dup
            part = _dot_nt(dgate, wgu_ref[cc]) + _dot_nt(dup, wgu_ref[FF_BLOCKS + cc])
            dh2 = part if dh2 is None else dh2 + part
        dh2_ref[...] = dh2.astype(BF16)

    row = lambda w: pl.BlockSpec((tm, w), lambda i: (i, 0))
    return pl.pallas_call(
        body, name="bwd_ffn",
        out_shape=(jax.ShapeDtypeStruct((s, D_FF), BF16), jax.ShapeDtypeStruct((s, 2 * D_FF), BF16),
                   jax.ShapeDtypeStruct((s, D_MODEL), BF16)),
        grid=(s // tm,),
        in_specs=[row(D_MODEL), row(2 * D_FF), _const_spec((N_CHIPS, D_MODEL, FF_CHUNK), single=True),
                  _const_spec((D_FF, D_MODEL), single=True)],
        out_specs=(row(D_FF), row(2 * D_FF), row(D_MODEL)),
        compiler_params=_params(("parallel",)),
    )(dy2, act, w_gu, w_dn)


def _bwd_mid(dh2, dx1a, x1, x, y, modr, ln1_g, w_out, tm, swap_fulls, swap_kinds):
    s = x.shape[0]
    n_steps = s // tm
    n_g = len(swap_fulls)

    def body(dh2_ref, dx1a_ref, x1_ref, x_ref, y_ref, mod_ref, g_ref, wout_ref, *rest):
        full_refs = rest[:n_g]
        dxa_ref, dy_ref, dmix_ref, acc_ref = rest[n_g:n_g + 4]
        got_refs = rest[n_g + 4:2 * n_g + 4]
        swap = _HalfSwap(full_refs, got_refs, swap_kinds, *rest[2 * n_g + 4:])
        i = pl.program_id(0)

        @pl.when(i == 0)
        def _():
            swap.start()
            acc_ref[...] = jnp.zeros_like(acc_ref)

        dh2 = dh2_ref[...].astype(F32)
        x1v = x1_ref[...].astype(F32)
        yv = y_ref[...].astype(F32)
        g1 = mod_ref[2:3, :]
        dx1 = dx1a_ref[...] + dh2 * (1.0 + mod_ref[4:5, :])
        z1 = ALPHA * x_ref[...] + g1 * yv
        xhat, rstd = _ln_stats(z1)
        dz1 = _ln_bwd(dx1 * g_ref[...], xhat, rstd)
        dxa_ref[...] = (ALPHA * dz1).astype(BF16)
        dy = (g1 * dz1).astype(BF16)
        dy_ref[...] = dy
        dmix_ref[...] = _dot_nt(dy, wout_ref[...]).astype(BF16)
        acc_ref[0:1, :] += _colsum(dh2 * x1v)
        acc_ref[1:2, :] += _colsum(dh2)
        acc_ref[2:3, :] += _colsum(dx1 * xhat)
        acc_ref[3:4, :] += _colsum(dx1)
        acc_ref[4:5, :] += _colsum(dz1 * yv)

        @pl.when(i == n_steps - 1)
        def _():
            swap.wait()

    row = lambda w: pl.BlockSpec((tm, w), lambda i: (i, 0))
    outs = pl.pallas_call(
        body, name="bwd_mid",
        out_shape=[jax.ShapeDtypeStruct((s, D_MODEL), BF16), jax.ShapeDtypeStruct((s, D_MODEL), BF16),
                   jax.ShapeDtypeStruct((s, D_MODEL), BF16), jax.ShapeDtypeStruct((8, D_MODEL), F32)]
        + _HalfSwap.out_shapes(swap_fulls, swap_kinds),
        grid=(n_steps,),
        in_specs=[row(D_MODEL)] * 5 + [_const_spec((8, D_MODEL)), _const_spec((1, D_MODEL)),
                                       _const_spec((D_MODEL, D_MODEL))] + [ANY] * n_g,
        out_specs=[row(D_MODEL), row(D_MODEL), row(D_MODEL), _const_spec((8, D_MODEL))] + [ANY] * n_g,
        scratch_shapes=_HalfSwap.sems(n_g),
        compiler_params=_params(("arbitrary",)),
    )(dh2, dx1a, x1, x, y, modr, ln1_g, w_out, *swap_fulls)
    return outs[:4], outs[4:]


def _fold_kv(t0, t1):
    lane = lax.broadcasted_iota(jnp.int32, t0.shape, 1)
    f0 = t0 + pltpu.roll(t0, HEAD_DIM, 1)
    f1 = t1 + pltpu.roll(t1, HEAD_DIM, 1)
    return jnp.where(lane < HEAD_DIM, f0, f1)


def _bwd_mix(q, kv, gu, gv, dmix, bias, sinks, gln_g, gln_b, wsm, bsx, amat, aog, gog, after):
    s = q.shape[0]
    tile = 2 * BLOCK
    n_steps = s // tile

    def body(q_ref, kv_ref, kvp_ref, gu_ref, gv_ref, dmix_ref, bias_ref, sinks_ref, glng_ref, glnb_ref, wsm_ref,
             bsx_ref, amat_ref, aog_ref, gog_ref, after_ref, dq_ref, dkv_ref, dgu_ref, dgv_ref, gbias_ref, dws_ref,
             dbs_ref, vec_ref, dsink_ref, carry, done):
        n = pl.program_id(0)

        @pl.when(n == 0)
        def _():
            carry[...] = jnp.zeros_like(carry)
            done[...] = jnp.zeros_like(done)
            gbias_ref[...] = jnp.zeros_like(gbias_ref)
            dws_ref[...] = jnp.zeros_like(dws_ref)
            dbs_ref[...] = jnp.zeros_like(dbs_ref)
            vec_ref[...] = jnp.zeros_like(vec_ref)
            dsink_ref[...] = jnp.zeros_like(dsink_ref)

        @pl.when(n == n_steps)
        def _():
            dkv_ref[:BLOCK, :] = done[...].astype(BF16)
            dkv_ref[BLOCK:, :] = carry[...].astype(BF16)

        @pl.when(n < n_steps)
        def _():
            col = lax.broadcasted_iota(jnp.int32, (BLOCK, 2 * BLOCK), 1)
            lane = lax.broadcasted_iota(jnp.int32, (BLOCK, LANES), 1)
            low = lane < HEAD_DIM
            rows = [slice(0, BLOCK), slice(BLOCK, tile)]
            kv_blocks = [kvp_ref[...], kv_ref[rows[0], :], kv_ref[rows[1], :]]
            masks = [(col < BLOCK) & (n == 0), None]
            q_blks = [q_ref[r, :] for r in rows]
            fwd = []
            for b in range(2):
                kk = jnp.concatenate([kv_blocks[b][:, :KV_W], kv_blocks[b + 1][:, :KV_W]], axis=0)
                vv = jnp.concatenate([kv_blocks[b][:, KV_W:], kv_blocks[b + 1][:, KV_W:]], axis=0)
                fwd.append(_attn_block_fwd(q_blks[b], kk, vv, bias_ref, sinks_ref, masks[b]))
                fwd.append(_gmlp_chunk_fwd(gu_ref[rows[b], :], gv_ref[rows[b], :], glng_ref[...], glnb_ref[...],
                                           wsm_ref, bsx_ref[...], amat_ref[...]))
            res = _interleave(*fwd[:2]) + _interleave(*fwd[2:])

            def gating_bwd(b, d_gm, saved):
                u, tu, ta, xhat, rstd, vb, mixedv = saved
                dgu_ref[rows[b], :] = (d_gm * mixedv * _gelu_grad(gu_ref[rows[b], :], tu)).astype(BF16)
                dmx = d_gm * u
                dmxb = dmx.astype(BF16)
                yield
                dvn_cols, dws = [], []
                for pair in range(N_GROUPS // 2):
                    dp_ = dmxb[:, pair * LANES:(pair + 1) * LANES]
                    vp = vb[:, pair * LANES:(pair + 1) * LANES]
                    dvn_cols.append(
                        jnp.where(low, _dot_tn(wsm_ref[2 * pair], dp_), _dot_tn(wsm_ref[2 * pair + 1], dp_)))
                    zero = jnp.zeros_like(dp_)
                    dws.append(_dot_nt(jnp.where(low, dp_, zero), vp))
                    dws.append(_dot_nt(jnp.where(low, zero, dp_), vp))
                dvn = jnp.concatenate(dvn_cols, axis=1)
                yield
                dxh = dvn * glng_ref[...]
                am = amat_ref[...]
                m1 = _split_dot(dxh, am)
                m2 = _split_dot(dxh * xhat, am)
                yield
                da = rstd * (dxh - m1 - xhat * m2)
                dgv_ref[rows[b], :] = (da * _gelu_grad(gv_ref[rows[b], :], ta)).astype(BF16)
                return dmx, dws, _colsum(dvn * xhat), _colsum(dvn)

            def attention_bwd(b, d_attn, probs, kvar, vvar):
                heads = range(N_HEADS)
                sels = [low if h % 2 == 0 else jnp.logical_not(low) for h in heads]
                pair_of = lambda a, h: a[:, (h // 2) * LANES:(h // 2 + 1) * LANES]
                do_hs = [jnp.where(sels[h], pair_of(d_attn, h), 0.0).astype(BF16) for h in heads]
                q_hs = [jnp.where(sels[h], pair_of(q_blks[b], h), jnp.zeros((BLOCK, LANES), BF16)) for h in heads]
                dps = [_dot_nt(do_hs[h], vvar[_head_kv(h)[0]][_head_kv(h)[1]]) for h in heads]
                yield
                deltas = [jnp.sum(probs[h][0] * dps[h], axis=-1, keepdims=True) for h in heads]
                yield
                dss = [probs[h][0] * (dps[h] - deltas[h]) for h in heads]
                dsinks = [-(probs[h][1] * deltas[h]) for h in heads]
                dsbs = [ds.astype(BF16) for ds in dss]
                pbs = [probs[h][0].astype(BF16) for h in heads]
                yield
                dqs = [_dot(dsbs[h], kvar[_head_kv(h)[0]][_head_kv(h)[1]]) for h in heads]
                tks = [_dot_tn(dsbs[h], q_hs[h]) for h in heads]
                tvs = [_dot_tn(pbs[h], do_hs[h]) for h in heads]
                dq_cols = [dqs[2 * i] + dqs[2 * i + 1] for i in range(N_HEADS // 2)]
                dq_ref[rows[b], :] = (jnp.concatenate(dq_cols, axis=1) * Q_SCALE).astype(BF16)
                per_kv = N_HEADS // N_KV
                kv_sum = lambda ts, kvh: sum(ts[kvh * per_kv + 1:(kvh + 1) * per_kv], ts[kvh * per_kv])
                dkk = _fold_kv(kv_sum(tks, 0), kv_sum(tks, 1))
                dvv = _fold_kv(kv_sum(tvs, 0), kv_sum(tvs, 1))
                return jnp.concatenate([dkk, dvv], axis=1), dss, dsinks

            bwd, rms_g = [], []
            for b in range(2):
                attn, probs, kvar, vvar = res[2 * b]
                gm, saved = res[2 * b + 1]
                na_unit, r_a = _rms(attn, 1.0)
                ng_unit, r_g = _rms(gm, 1.0)
                dmix = dmix_ref[rows[b], :].astype(F32)
                dn_a = dmix[:, :ATTN_W]
                dn_g = dmix[:, ATTN_W:]
                rms_g.append((_colsum(dn_a * na_unit), _colsum(dn_g * ng_unit)))
                t_a = dn_a * aog_ref[...]
                d_attn = r_a * t_a - na_unit * (r_a * jnp.mean(t_a * na_unit, axis=-1, keepdims=True))
                t_g = dn_g * gog_ref[...]
                d_gm = r_g * t_g - ng_unit * (r_g * jnp.mean(t_g * ng_unit, axis=-1, keepdims=True))
                bwd.append(attention_bwd(b, d_attn, probs, kvar, vvar))
                bwd.append(gating_bwd(b, d_gm, saved))
            (dkv_a, dss_a, dsk_a), (dmx_a, dws_a, glg_a, glb_a) = _interleave(*bwd[:2])
            (dkv_b, dss_b, dsk_b), (dmx_b, dws_b, glg_b, glb_b) = _interleave(*bwd[2:])

            vec_ref[0:1, :] += rms_g[0][0] + rms_g[1][0]
            vec_ref[1:2, :] += rms_g[0][1] + rms_g[1][1]
            vec_ref[2:3, :] += glg_a + glg_b
            vec_ref[3:4, :] += glb_a + glb_b
            dbs_ref[...] += dmx_a + dmx_b
            for g in range(N_GROUPS):
                dws_ref[g] += dws_a[g] + dws_b[g]
            for h in range(N_HEADS):
                gbias_ref[h] += dss_a[h] + dss_b[h]
                dsink_ref[h] += dsk_a[h] + dsk_b[h]

            dkv_ref[:BLOCK, :] = done[...].astype(BF16)
            dkv_ref[BLOCK:, :] = (carry[...] + dkv_a[:BLOCK]).astype(BF16)
            done[...] = dkv_a[BLOCK:] + dkv_b[:BLOCK]
            carry[...] = dkv_b[BLOCK:]

    last = n_steps - 1
    cur = lambda w: pl.BlockSpec((tile, w), lambda n: (jnp.minimum(n, last), 0))
    late = lambda w: pl.BlockSpec((tile, w), lambda n: (jnp.clip(n - 1, 0, last), 0))
    before = pl.BlockSpec((BLOCK, 2 * KV_W), lambda n: (jnp.clip(2 * n - 1, 0, 2 * last + 1), 0))
    outs = pl.pallas_call(
        body, name="bwd_mix",
        out_shape=[jax.ShapeDtypeStruct((s, ATTN_W), BF16), jax.ShapeDtypeStruct((s, 2 * KV_W), BF16),
                   jax.ShapeDtypeStruct((s, GMLP_W), BF16), jax.ShapeDtypeStruct((s, GMLP_W), BF16),
                   jax.ShapeDtypeStruct((N_HEADS, BLOCK, 2 * BLOCK), F32),
                   jax.ShapeDtypeStruct((N_GROUPS, BLOCK, BLOCK), F32),
                   jax.ShapeDtypeStruct((BLOCK, GMLP_W), F32), jax.ShapeDtypeStruct((8, GMLP_W), F32),
                   jax.ShapeDtypeStruct((N_HEADS, BLOCK, 1), F32)],
        grid=(n_steps + 1,),
        in_specs=[cur(ATTN_W), cur(2 * KV_W), before, cur(GMLP_W), cur(GMLP_W), cur(D_MODEL),
                  _const_spec((N_HEADS, BLOCK, 2 * BLOCK)), pl.BlockSpec(memory_space=pltpu.SMEM),
                  _const_spec((1, GMLP_W)), _const_spec((1, GMLP_W)), _const_spec((N_GROUPS, BLOCK, BLOCK)),
                  _const_spec((BLOCK, GMLP_W)), _const_spec((GMLP_W, GMLP_W)), _const_spec((1, ATTN_W)),
                  _const_spec((1, GMLP_W)), ANY],
        out_specs=[cur(ATTN_W), late(2 * KV_W), cur(GMLP_W), cur(GMLP_W),
                   _const_spec((N_HEADS, BLOCK, 2 * BLOCK)), _const_spec((N_GROUPS, BLOCK, BLOCK)),
                   _const_spec((BLOCK, GMLP_W)), _const_spec((8, GMLP_W)), _const_spec((N_HEADS, BLOCK, 1))],
        scratch_shapes=[pltpu.VMEM((BLOCK, 2 * KV_W), F32), pltpu.VMEM((BLOCK, 2 * KV_W), F32)],
        compiler_params=_params(("arbitrary",)),
    )(q, kv, kv, gu, gv, dmix, bias, sinks, gln_g, gln_b, wsm, bsx, amat, aog, gog, after)
    return outs


def _mix_finalize(gbias, bucket, dws, dbs, dsink):
    def body(gb_ref, bucket_ref, dws_ref, dbs_ref, dsink_ref, tall_ref):
        bk = bucket_ref[...]
        lane = lax.broadcasted_iota(jnp.int32, (N_BUCKETS, LANES), 1)
        rowi = lax.broadcasted_iota(jnp.int32, (N_BUCKETS, LANES), 0)
        drb = jnp.zeros((N_BUCKETS, LANES), F32)
        dsk = jnp.zeros((8, LANES), F32)
        lane8 = lax.broadcasted_iota(jnp.int32, (8, LANES), 1)
        for h in range(N_HEADS):
            g = gb_ref[h]
            for b in range(N_BUCKETS):
                tot = jnp.sum(_colsum(jnp.where(bk == float(b), g, 0.0)), axis=1, keepdims=True)
                drb = jnp.where((lane == h) & (rowi == b), tot, drb)
            sk = jnp.sum(dsink_ref[h], axis=0, keepdims=True)
            dsk = jnp.where(lane8 == h, sk, dsk)
        tall_ref[TALL_RB:TALL_RB + N_BUCKETS, :] = drb
        tall_ref[TALL_SK:TALL_SK + 8, :] = dsk
        ti = lax.broadcasted_iota(jnp.int32, (BLOCK, BLOCK), 0)
        ui = lax.broadcasted_iota(jnp.int32, (BLOCK, BLOCK), 1)
        for g in range(N_GROUPS):
            tall_ref[g * BLOCK:(g + 1) * BLOCK, :] = jnp.where(ti >= ui, dws_ref[g], 0.0)
        gi = lax.broadcasted_iota(jnp.int32, (GMLP_W, LANES), 0) // GROUP_DIM
        li = lax.broadcasted_iota(jnp.int32, (GMLP_W, LANES), 1)
        ind = jnp.where(gi == li, 1.0, 0.0).astype(BF16)
        d = dbs_ref[...]
        hi = d.astype(BF16)
        r1 = d - hi.astype(F32)
        mid = r1.astype(BF16)
        lo = (r1 - mid.astype(F32)).astype(BF16)
        dbsg = _dot(hi, ind) + _dot(mid, ind) + _dot(lo, ind)
        tall_ref[TALL_BS:TALL_BS + N_GROUPS, :] = dbsg.T[:N_GROUPS, :]

    return pl.pallas_call(
        body, name="mix_finalize", out_shape=jax.ShapeDtypeStruct((TALL_ROWS, LANES), F32), grid=(1,),
        in_specs=[_const_spec((N_HEADS, BLOCK, 2 * BLOCK)), _const_spec((BLOCK, 2 * BLOCK)),
                  _const_spec((N_GROUPS, BLOCK, BLOCK)), _const_spec((BLOCK, GMLP_W)),
                  _const_spec((N_HEADS, BLOCK, 1))],
        out_specs=_const_spec((TALL_ROWS, LANES)),
        compiler_params=_params(("arbitrary",)),
    )(gbias, bucket, dws, dbs, dsink)


def _bwd_in(dq, dkv, dgu, dgv, dxa, x, modr, w_in, tm):
    s = x.shape[0]

    def body(dq_ref, dkv_ref, dgu_ref, dgv_ref, dxa_ref, x_ref, mod_ref, w_ref, gx_ref, acc_ref, db_ref):
        @pl.when(pl.program_id(0) == 0)
        def _():
            acc_ref[...] = jnp.zeros_like(acc_ref)
            db_ref[...] = jnp.zeros_like(db_ref)

        dproj = jnp.concatenate([dq_ref[...], dkv_ref[...], dgu_ref[...], dgv_ref[...]], axis=1)
        wb = IN_W // N_CHIPS
        dh1 = sum([_dot_nt(dproj[:, j * wb:(j + 1) * wb], w_ref[j]) for j in range(1, N_CHIPS)],
                  _dot_nt(dproj[:, :wb], w_ref[0]))
        gx_ref[...] = dxa_ref[...].astype(F32) + dh1 * (1.0 + mod_ref[1:2, :])
        acc_ref[0:1, :] += _colsum(dh1 * x_ref[...].astype(F32))
        acc_ref[1:2, :] += _colsum(dh1)
        db_ref[0:1, :] += _colsum(dproj.astype(F32))

    row = lambda w: pl.BlockSpec((tm, w), lambda i: (i, 0))
    return pl.pallas_call(
        body, name="bwd_in",
        out_shape=(jax.ShapeDtypeStruct((s, D_MODEL), F32), jax.ShapeDtypeStruct((8, D_MODEL), F32),
                   jax.ShapeDtypeStruct((8, IN_W), F32)),
        grid=(s // tm,),
        in_specs=[row(ATTN_W), row(2 * KV_W), row(GMLP_W), row(GMLP_W), row(D_MODEL), row(D_MODEL),
                  _const_spec((8, D_MODEL)), _const_spec(w_in.shape)],
        out_specs=(row(D_MODEL), _const_spec((8, D_MODEL)), _const_spec((8, IN_W))),
        compiler_params=_params(("arbitrary",)),
    )(dq, dkv, dgu, dgv, dxa, x, modr, w_in)


def _wgrad(a, bs, tm, tk, name, owner_blocks=False, gather_vs=()):
    k_all, m = a.shape
    n = sum(b.shape[1] for b in bs)
    nk = k_all // tk
    nm = m // tm
    n_b = len(bs)
    n_v = len(gather_vs)
    wb = n // N_CHIPS

    def body(a_ref, *rest):
        b_refs, v_refs = rest[:n_b], rest[n_b:n_b + n_v]
        o_ref, ob_ref = rest[n_b + n_v:n_b + n_v + 2]
        vg_refs = rest[n_b + n_v + 2:n_b + 2 * n_v + 2]
        i, k = pl.program_id(0), pl.program_id(1)
        if n_v:
            gather = _Gather8(v_refs, vg_refs, *rest[n_b + 2 * n_v + 2:])

            @pl.when((i == 0) & (k == 0))
            def _():
                gather.start()

            @pl.when((i == nm - 1) & (k == 0))
            def _():
                gather.forward()

        @pl.when(k == 0)
        def _():
            o_ref[...] = jnp.zeros_like(o_ref)

        b = b_refs[0][...] if n_b == 1 else jnp.concatenate([r[...] for r in b_refs], axis=1)
        if owner_blocks:
            av = a_ref[...]
            for j in range(N_CHIPS):
                o_ref[j] += _dot_tn(av, b[:, j * wb:(j + 1) * wb])
        else:
            o_ref[...] += _dot_tn(a_ref[...], b)

        @pl.when(k == nk - 1)
        def _():
            ob_ref[...] = o_ref[...].astype(BF16)

        if n_v:
            @pl.when((i == nm - 1) & (k == nk - 1))
            def _():
                gather.finish()

    if owner_blocks:
        out_spec = pl.BlockSpec((N_CHIPS, tm, wb), lambda i, k: (0, i, 0))
        shape = (N_CHIPS, m, wb)
    else:
        out_spec = pl.BlockSpec((tm, n), lambda i, k: (i, 0))
        shape = (m, n)
    outs = pl.pallas_call(
        body, name=name,
        out_shape=[jax.ShapeDtypeStruct(shape, F32), jax.ShapeDtypeStruct(shape, BF16)] + _gathered8_shapes(gather_vs),
        grid=(nm, nk),
        in_specs=[pl.BlockSpec((tk, tm), lambda i, k: (k, i))]
        + [pl.BlockSpec((tk, b.shape[1]), lambda i, k: (k, 0)) for b in bs] + [ANY] * n_v,
        out_specs=[out_spec, out_spec] + [ANY] * n_v,
        scratch_shapes=_Gather8.sems(n_v) if n_v else [],
        compiler_params=_params(("arbitrary", "arbitrary") if n_v else ("parallel", "arbitrary")),
    )(a, *bs, *gather_vs)
    return outs[0], outs[1], outs[2:]


def _adam_math(w, g, m, v):
    m2 = ADAM_B1 * m + (1.0 - ADAM_B1) * g
    v2 = ADAM_B2 * v + (1.0 - ADAM_B2) * (g * g)
    m_hat = m2 / (1.0 - ADAM_B1 ** ADAM_STEP)
    v_hat = v2 / (1.0 - ADAM_B2 ** ADAM_STEP)
    delta = -ADAM_LR * (m_hat / (jnp.sqrt(v_hat) + ADAM_EPS) + ADAM_WD * w)
    return delta, m2, v2


def _adam_halves(w, mine, got, m, v, tr, name):
    r, cc = w.shape
    h = r // 2
    nt = h // tr

    def body(c_ref, w_ref, mine_ref, got_ref, m_ref, v_ref, g_ref, d_ref, m2_ref, v2_ref):
        g = jnp.where(pl.program_id(0) == c_ref[0], mine_ref[...], got_ref[...])
        g_ref[...] = g
        d, m2, v2 = _adam_math(w_ref[...], g, m_ref[...], v_ref[...])
        d_ref[...] = d
        m2_ref[...] = m2
        v2_ref[...] = v2

    full = pl.BlockSpec((tr, cc), lambda hh, i, c_ref: (hh * nt + i, 0))
    half = pl.BlockSpec((tr, cc), lambda hh, i, c_ref: (i, 0))
    shp = jax.ShapeDtypeStruct((r, cc), F32)
    return pl.pallas_call(
        body, name=name, out_shape=(shp, shp, shp, shp),
        grid_spec=pltpu.PrefetchScalarGridSpec(
            num_scalar_prefetch=1, grid=(2, nt), in_specs=[full, half, half, full, full],
            out_specs=(full, full, full, full)),
        compiler_params=_params(("arbitrary", "arbitrary")),
    )(_core_index_scalar(), w, mine, got, m, v)


def _adam_w_ada(sc_t, dmod_cols, w, m, v, tr, after):
    r, cc = w.shape

    def body(sct_ref, dm_ref, w_ref, m_ref, v_ref, after_ref, g_ref, d_ref, m2_ref, v2_ref):
        g = sct_ref[:, 0:1] * dm_ref[0:1, :]
        for k in range(1, N_DEV):
            g = g + sct_ref[:, k:k + 1] * dm_ref[k:k + 1, :]
        g_ref[...] = g
        d, m2, v2 = _adam_math(w_ref[...], g, m_ref[...], v_ref[...])
        d_ref[...] = d
        m2_ref[...] = m2
        v2_ref[...] = v2

    spec = pl.BlockSpec((tr, cc), lambda i: (i, 0))
    shp = jax.ShapeDtypeStruct((r, cc), F32)
    return pl.pallas_call(
        body, name="adam_w_ada", out_shape=(shp, shp, shp, shp), grid=(r // tr,),
        in_specs=[pl.BlockSpec((tr, N_DEV), lambda i: (i, 0)), _const_spec((N_DEV, cc)), spec, spec, spec, ANY],
        out_specs=(spec, spec, spec, spec), compiler_params=_params(("parallel",)),
    )(sc_t, dmod_cols, w, m, v, after)


def _pack_wide(acc_i, acc_m, acc_f, db_in, vec):
    arrs = [acc_i, acc_m, acc_f, db_in, vec]
    i_, m_, f_, b_, v_ = range(5)
    src = {"b_in": (b_, 0), "ln1_g": (m_, 2), "ln1_b": (m_, 3), "ln2_g": (f_, 1), "ln2_b": (f_, 2),
           "gmlp_ln_g": (v_, 2), "gmlp_ln_b": (v_, 3), "attn_out_g": (v_, 0), "gmlp_out_g": (v_, 1), "loss": (f_, 0)}
    dmod = [(i_, 1), (i_, 0), (m_, 4), (m_, 1), (m_, 0), (f_, 3)]

    def body(*refs):
        ins, wide_ref = refs[:5], refs[5]
        wide_ref[...] = jnp.zeros_like(wide_ref)
        for k, (a, row) in enumerate(dmod):
            wide_ref[0:1, k * D_MODEL:(k + 1) * D_MODEL] = ins[a][row:row + 1, :]
        for name, (a, row) in src.items():
            r, off, n = WIDE_LAYOUT[name]
            wide_ref[r:r + 1, off:off + n] = ins[a][row:row + 1, :]

    return pl.pallas_call(
        body, name="pack_wide", out_shape=jax.ShapeDtypeStruct((8, WIDE_W), F32), grid=(1,),
        in_specs=[_const_spec(a.shape) for a in arrs], out_specs=_const_spec((8, WIDE_W)),
        compiler_params=_params(("arbitrary",)),
    )(*arrs)


def _adam_small(gw, gt, wide_wmv, w_s, b_s, rel_bias, sinks, after):
    names = list(WIDE_PARAMS)
    tall = [("gmlp_w_s", w_s), ("gmlp_b_s", b_s), ("rel_bias", rel_bias), ("attn_sinks", sinks)]
    ins = [gw, gt]
    for n in names:
        ins += list(wide_wmv[n])
    for _, t in tall:
        ins += list(t)
    n_in = len(ins)

    def body(*refs):
        gw_ref, gt_ref = refs[0], refs[1]
        wmv = refs[2:n_in]
        dmod_ref, loss_ref = refs[n_in + 1], refs[n_in + 2]
        outs = refs[n_in + 3:]

        def tall_sum(r0, nr):
            g = gt_ref[r0:r0 + nr, :]
            for d in range(1, N_DEV):
                g = g + gt_ref[d * TALL_ROWS + r0:d * TALL_ROWS + r0 + nr, :]
            return g

        def emit(k, g, w_ref, m_ref, v_ref):
            d, m2, v2 = _adam_math(w_ref[...], g, m_ref[...], v_ref[...])
            outs[4 * k][...] = g
            outs[4 * k + 1][...] = d
            outs[4 * k + 2][...] = m2
            outs[4 * k + 3][...] = v2

        gsum = gw_ref[0:8, :]
        for d in range(1, N_DEV):
            gsum = gsum + gw_ref[8 * d:8 * d + 8, :]
        for d in range(N_DEV):
            dmod_ref[d:d + 1, :] = gw_ref[8 * d:8 * d + 1, :]
        for k, n in enumerate(names):
            r, off, sz = WIDE_LAYOUT[n]
            emit(k, gsum[r:r + 1, off:off + sz], *wmv[3 * k:3 * k + 3])
        r, off, sz = WIDE_LAYOUT["loss"]
        tot = jnp.sum(gsum[r:r + 1, off:off + sz], axis=1, keepdims=True)
        loss_ref[...] = jnp.broadcast_to(tot * (0.5 / D_MODEL), loss_ref.shape)

        k0 = len(names)
        ws_refs = wmv[3 * k0:3 * k0 + 3]
        for g in range(N_GROUPS):
            rows = slice(g * BLOCK, (g + 1) * BLOCK)
            gg = tall_sum(g * BLOCK, BLOCK)
            d, m2, v2 = _adam_math(ws_refs[0][rows, :], gg, ws_refs[1][rows, :], ws_refs[2][rows, :])
            outs[4 * k0][rows, :] = gg
            outs[4 * k0 + 1][rows, :] = d
            outs[4 * k0 + 2][rows, :] = m2
            outs[4 * k0 + 3][rows, :] = v2
        emit(k0 + 1, tall_sum(TALL_BS, N_GROUPS), *wmv[3 * (k0 + 1):3 * (k0 + 1) + 3])
        emit(k0 + 2, tall_sum(TALL_RB, N_BUCKETS)[:, :N_HEADS], *wmv[3 * (k0 + 2):3 * (k0 + 2) + 3])
        emit(k0 + 3, tall_sum(TALL_SK, 8)[0:1, :N_HEADS], *wmv[3 * (k0 + 3):3 * (k0 + 3) + 3])

    out_shapes = [jax.ShapeDtypeStruct((N_DEV, WIDE_W), F32), jax.ShapeDtypeStruct((8, LANES), F32)]
    for n in names:
        out_shapes += [jax.ShapeDtypeStruct(wide_wmv[n][0].shape, F32)] * 4
    for _, t in tall:
        out_shapes += [jax.ShapeDtypeStruct(t[0].shape, F32)] * 4
    res = pl.pallas_call(
        body, name="adam_small", out_shape=out_shapes, grid=(1,),
        in_specs=[_const_spec(a.shape) for a in ins] + [ANY], out_specs=[_const_spec(o.shape) for o in out_shapes],
        compiler_params=_params(("arbitrary",)),
    )(*ins, after)
    out = {}
    for k, n in enumerate(names + [t[0] for t in tall]):
        out[n] = tuple(res[2 + 4 * k:6 + 4 * k])
    return res[0], res[1], out


def kernel(x, c, rel_bias, w_ada, b_ada, w_in, b_in, attn_sinks, gmlp_ln_g, gmlp_ln_b, gmlp_w_s, gmlp_b_s, attn_out_g, gmlp_out_g, w_out, ln1_g, ln1_b, w_gate_up, w_down, ln2_g, ln2_b, loss_target, m_rel_bias, m_w_ada, m_b_ada, m_w_in, m_b_in, m_attn_sinks, m_gmlp_ln_g, m_gmlp_ln_b, m_gmlp_w_s, m_gmlp_b_s, m_attn_out_g, m_gmlp_out_g, m_w_out, m_ln1_g, m_ln1_b, m_w_gate_up, m_w_down, m_ln2_g, m_ln2_b, v_rel_bias, v_w_ada, v_b_ada, v_w_in, v_b_in, v_attn_sinks, v_gmlp_ln_g, v_gmlp_ln_b, v_gmlp_w_s, v_gmlp_b_s, v_attn_out_g, v_gmlp_out_g, v_w_out, v_ln1_g, v_ln1_b, v_w_gate_up, v_w_down, v_ln2_g, v_ln2_b):
    ix, iy, ic = _my_pos()
    chip = 2 * ix + iy
    dev = 4 * ix + 2 * iy + ic
    s = x.shape[1]
    xs = x[0]
    tgt = loss_target[0]
    tm_big = min(512, s)
    tm_ffn = min(FFN_SUB, s)
    n_ada = w_ada.shape[2]

    w_in_s, w_out_s = w_in[0].astype(BF16), w_out[0].astype(BF16)
    w_gu_s, w_dn_s = w_gate_up[0].astype(BF16), w_down[0].astype(BF16)
    sc_all, mod_rows, (w_in_g, w_out_g) = _prologue(
        jnp.pad(c, ((0, 7), (0, 0))), w_ada[0], lax.dynamic_slice_in_dim(b_ada, chip * n_ada, n_ada, axis=1),
        [w_in_s, w_out_s])
    mod_all = mod_rows.reshape(N_DEV, N_DEV, -1)
    mod_row = lax.dynamic_index_in_dim(mod_all[0::2], dev, axis=1, keepdims=False)
    modr = jnp.pad(mod_row.reshape(6, D_MODEL), ((0, 2), (0, 0)))
    w_in_f = _insert_own(w_in_g, w_in_s, "blk", chip)

    bucket = _bucket_table()
    bias, wsm = _prep_tables(bucket, rel_bias, gmlp_w_s[0])
    bsx = jnp.repeat(gmlp_b_s[0].T, GROUP_DIM, axis=1)
    amat = _group_mean_matrix()
    sinks = attn_sinks[0]

    (h1, q, kv, gu, gv, xb), (w_dn_g,) = _fwd_in(xs, modr, w_in_f, b_in, tm_big, [w_dn_s], ["blk"])
    w_out_f = _insert_own(w_out_g, w_out_s, "blk", chip).reshape(D_MODEL, D_MODEL)
    (x1, x1b, y, mixed), (w_gu_g,) = _fwd_mix(
        q, kv, gu, gv, xs, modr, bias, sinks, gmlp_ln_g, gmlp_ln_b, wsm, bsx, amat, attn_out_g, gmlp_out_g, w_out_f,
        ln1_g, ln1_b, tm_big, [w_gu_s], ["blk"])
    assert w_gate_up.shape[2] == FF_CHUNK
    w_gu_f = _insert_own(w_gu_g, w_gu_s, "blk", chip)
    w_dn_f = _insert_own(w_dn_g, w_dn_s, "blk", chip).reshape(D_FF, D_MODEL)
    h2, act, dy2, dx1a, acc_f = _fwd_ffn(x1, tgt, modr, ln2_g, ln2_b, w_gu_f, w_dn_f, tm_ffn)

    a_act, dgu_ff, dh2 = _bwd_ffn(dy2, act, w_gu_f, w_dn_f, min(FFN_SUB, s))
    g_dn, g_dn_b, _ = _wgrad(a_act, [dy2], D_FF // 2, min(1024, s), "wgrad_down")
    g_gu, g_gu_b, _ = _wgrad(h2, [dgu_ff], 512, min(512, s), "wgrad_gate_up")
    blk3 = lambda a, rows: a.reshape(N_CHIPS, rows, a.shape[1])
    (dxa, dy, dmix, acc_m), (got_dn, got_gu) = _bwd_mid(
        dh2, dx1a, x1b, xs, y, modr, ln1_g, w_out_f, tm_big, [blk3(g_dn_b, D_FF // N_CHIPS), g_gu_b], ["blk", "cols"])
    g_out, g_out_b, _ = _wgrad(mixed, [dy], 512, min(2048, s), "wgrad_out")
    (got_out,) = _swap_halves([blk3(g_out_b, D_MODEL // N_CHIPS)], ["blk"], "rs_swap_out")
    kinds_a = ["blk", "cols", "blk"]
    fulls_a = [blk3(g_dn, D_FF // N_CHIPS), g_gu, blk3(g_out, D_MODEL // N_CHIPS)]
    gots_a = [got_dn, got_gu, got_out]
    parts_a = [_add_halves(f, g, k, "rs_add_a%d" % i) for i, (f, g, k) in enumerate(zip(fulls_a, gots_a, kinds_a))]
    exchange_a = _SplitCopy(lambda s_, l_, ss, rs: _ChipExchange(s_, l_, kinds_a, ss, rs), [p[1] for p in parts_a],
                            [jax.ShapeDtypeStruct(_rx_shape(p[1].shape, k), BF16) for p, k in zip(parts_a, kinds_a)],
                            3 * len(parts_a), "rs_chips_a")
    dq, dkv, dgu, dgv, gbias, dws, dbs, vec, dsink = _bwd_mix(
        q, kv, gu, gv, dmix, bias, sinks, gmlp_ln_g, gmlp_ln_b, wsm, bsx, amat, attn_out_g, gmlp_out_g,
        exchange_a.token)
    rxs_a = exchange_a.wait(dq)
    tall_g = _mix_finalize(gbias, bucket, dws, dbs, dsink)
    grad_x, acc_i, db_in = _bwd_in(dq, dkv, dgu, dgv, dxa, xb, modr, w_in_f, tm_big)

    wide_g = _pack_wide(acc_i, acc_m, acc_f, db_in, vec)
    full_in, full_in_b, (gw, gt) = _wgrad(h1, [dq, dkv, dgu, dgv], 512, min(1024, s), "wgrad_in", owner_blocks=True,
                                          gather_vs=[wide_g, tall_g])
    (got_in,) = _swap_halves([full_in_b], ["blk"], "rs_swap_in")
    part_in = _add_halves(full_in, got_in, "blk", "rs_add_in")
    exchange_in = _SplitCopy(lambda s_, l_, ss, rs: _ChipExchange(s_, l_, ["blk"], ss, rs), [part_in[1]],
                             [jax.ShapeDtypeStruct(_rx_shape(part_in[1].shape, "blk"), BF16)], 3, "rs_chips_in")
    wide_wmv ={"b_ada": (b_ada, m_b_ada, v_b_ada), "b_in": (b_in, m_b_in, v_b_in),
                "ln1_g": (ln1_g, m_ln1_g, v_ln1_g), "ln1_b": (ln1_b, m_ln1_b, v_ln1_b),
                "ln2_g": (ln2_g, m_ln2_g, v_ln2_g), "ln2_b": (ln2_b, m_ln2_b, v_ln2_b),
                "gmlp_ln_g": (gmlp_ln_g, m_gmlp_ln_g, v_gmlp_ln_g), "gmlp_ln_b": (gmlp_ln_b, m_gmlp_ln_b, v_gmlp_ln_b),
                "attn_out_g": (attn_out_g, m_attn_out_g, v_attn_out_g),
                "gmlp_out_g": (gmlp_out_g, m_gmlp_out_g, v_gmlp_out_g)}
    rows2 = lambda a: a.reshape(-1, a.shape[-1])
    dmod_all, loss_t, small = _adam_small(
        gw, gt, wide_wmv, tuple(rows2(a) for a in (gmlp_w_s, m_gmlp_w_s, v_gmlp_w_s)),
        tuple(rows2(a) for a in (gmlp_b_s, m_gmlp_b_s, v_gmlp_b_s)), (rel_bias, m_rel_bias, v_rel_bias),
        (attn_sinks, m_attn_sinks, v_attn_sinks), exchange_in.token)
    loss = loss_t[0, 0]

    sums = [(parts_a[0][0], rxs_a[0], 176), (parts_a[1][0], rxs_a[1], 256), (parts_a[2][0], rxs_a[2], 128)]
    mine = [_sum_chips(p, rx, tr, "rs_sum_%d" % i, loss_t) for i, (p, rx, tr) in enumerate(sums)]
    share_a = _SplitCopy(_SiblingCopy, mine, mine, len(mine), "rs_share_a")
    dmod_cols = lax.dynamic_slice_in_dim(dmod_all, chip * n_ada, n_ada, axis=1)
    g_ada, d_ada, m_ada, v_ada = _adam_w_ada(sc_all.T, dmod_cols, w_ada[0], m_w_ada[0], v_w_ada[0], 256, share_a.token)
    got = share_a.wait(d_ada)
    gs_dn, d_dn, m_dn, v_dn = _adam_halves(w_down[0], mine[0], got[0], m_w_down[0], v_w_down[0], 176, "adam_w_down")
    gs_gu, d_gu, m_gu, v_gu = _adam_halves(w_gate_up[0], mine[1], got[1], m_w_gate_up[0], v_w_gate_up[0], 256,
                                           "adam_w_gate_up")
    gs_out, d_out, m_out, v_out = _adam_halves(w_out[0], mine[2], got[2], m_w_out[0], v_w_out[0], 128, "adam_w_out")

    (rx_in,) = exchange_in.wait(d_gu)
    mine_in = _sum_chips(part_in[0], rx_in, 256, "rs_sum_in", rx_in)
    (got_in_half,) = _share_halves([mine_in], "rs_share_in")
    gs_in, d_in, m_in, v_in = _adam_halves(w_in[0], mine_in, got_in_half, m_w_in[0], v_w_in[0], 256, "adam_w_in")

    big = {"w_ada": (g_ada, d_ada, m_ada, v_ada), "w_in": (gs_in, d_in, m_in, v_in), "w_out": (gs_out, d_out, m_out, v_out),
           "w_gate_up": (gs_gu, d_gu, m_gu, v_gu), "w_down": (gs_dn, d_dn, m_dn, v_dn)}
    order = ["rel_bias", "w_ada", "b_ada", "w_in", "b_in", "attn_sinks", "gmlp_ln_g", "gmlp_ln_b", "gmlp_w_s", "gmlp_b_s",
             "attn_out_g", "gmlp_out_g", "w_out", "ln1_g", "ln1_b", "w_gate_up", "w_down", "ln2_g", "ln2_b"]
    shapes = {"gmlp_w_s": gmlp_w_s.shape, "gmlp_b_s": gmlp_b_s.shape}
    outs = [loss, grad_x[None]]
    for k in range(4):
        for name in order:
            if name in big:
                outs.append(big[name][k][None])
            elif name in shapes:
                outs.append(small[name][k].reshape(shapes[name]))
            else:
                outs.append(small[name][k])
    return tuple(outs)
```

```python
import math

import numpy as np
import jax
import jax.numpy as jnp
from jax import lax
from jax.experimental import pallas as pl
from jax.experimental.pallas import tpu as pltpu

F32 = jnp.float32
BF16 = jnp.bfloat16
MESH = pl.DeviceIdType.MESH

D_MODEL = 1024
N_HEADS = 8
N_KV = 2
HEAD_DIM = 64
ATTN_W = N_HEADS * HEAD_DIM
KV_W = N_KV * HEAD_DIM
N_GROUPS = 8
GROUP_DIM = 64
GMLP_W = N_GROUPS * GROUP_DIM
IN_W = ATTN_W + 2 * KV_W + 2 * GMLP_W
BLOCK = 128
N_BUCKETS = 32
MAX_DISTANCE = 128
D_FF = 2816
ALPHA = 2.0 ** 0.25
LN_EPS = 1e-5
NEG_INF = -1e30
ADAM_LR, ADAM_B1, ADAM_B2, ADAM_EPS, ADAM_WD, ADAM_STEP = 0.001, 0.9, 0.999, 1e-8, 0.01, 10
N_CHIPS = 4
N_DEV = 8
LANES = 128
V7X_VMEM_LIMIT = 56 * 2 ** 20
GELU_C = math.sqrt(2.0 / math.pi)
Q_SCALE = HEAD_DIM ** -0.5
ANY = pl.BlockSpec(memory_space=pl.ANY)

TALL_BS = N_GROUPS * BLOCK
TALL_RB = TALL_BS + 8
TALL_SK = TALL_RB + N_BUCKETS
TALL_ROWS = TALL_SK + 8
WIDE_W = 6 * D_MODEL
WIDE_LAYOUT = {
    "b_ada": (0, 0, 6 * D_MODEL),
    "b_in": (1, 0, IN_W), "ln1_g": (1, IN_W, D_MODEL), "ln1_b": (1, IN_W + D_MODEL, D_MODEL),
    "ln2_g": (1, IN_W + 2 * D_MODEL, D_MODEL), "ln2_b": (1, IN_W + 3 * D_MODEL, D_MODEL),
    "gmlp_ln_g": (2, 0, GMLP_W), "gmlp_ln_b": (2, GMLP_W, GMLP_W), "attn_out_g": (2, 2 * GMLP_W, ATTN_W),
    "gmlp_out_g": (2, 2 * GMLP_W + ATTN_W, GMLP_W), "loss": (2, 3 * GMLP_W + ATTN_W, D_MODEL)}
WIDE_PARAMS = tuple(n for n in WIDE_LAYOUT if n != "loss")


def _params(sem=None):
    return pltpu.CompilerParams(dimension_semantics=sem, vmem_limit_bytes=V7X_VMEM_LIMIT)


def _const_spec(shape, single=False):
    nd = len(shape)
    if single:
        return pl.BlockSpec(shape, lambda *_: (0,) * nd, pipeline_mode=pl.Buffered(1))
    return pl.BlockSpec(shape, lambda *_: (0,) * nd)


def _dot(a, b):
    return jnp.dot(a, b, preferred_element_type=F32)


def _dot_nt(a, b):
    return lax.dot_general(a, b, (((1,), (1,)), ((), ())), preferred_element_type=F32)


def _dot_tn(a, b):
    return lax.dot_general(a, b, (((0,), (0,)), ((), ())), preferred_element_type=F32)


def _gelu(x):
    t = jnp.tanh(GELU_C * (x + 0.044715 * x * x * x))
    return 0.5 * x * (1.0 + t), t


def _gelu_grad(x, t):
    return 0.5 * (1.0 + t) + 0.5 * x * (1.0 - t * t) * GELU_C * (1.0 + 3.0 * 0.044715 * x * x)


def _split_dot(x, a):
    hi = x.astype(BF16)
    lo = (x - hi.astype(F32)).astype(BF16)
    return _dot(hi, a) + _dot(lo, a)


def _group_mean_matrix():
    g = np.arange(GMLP_W) // GROUP_DIM
    return jnp.asarray((g[:, None] == g[None, :]).astype(np.float32) / GROUP_DIM, dtype=BF16)


def _ln_stats(z):
    mu = jnp.mean(z, axis=-1, keepdims=True)
    d = z - mu
    var = jnp.mean(d * d, axis=-1, keepdims=True)
    rstd = lax.rsqrt(var + LN_EPS)
    return d * rstd, rstd


def _ln_bwd(dxhat, xhat, rstd):
    m1 = jnp.mean(dxhat, axis=-1, keepdims=True)
    m2 = jnp.mean(dxhat * xhat, axis=-1, keepdims=True)
    return rstd * (dxhat - m1 - xhat * m2)


def _colsum(x):
    return jnp.sum(x, axis=0, keepdims=True)


def _my_pos():
    return lax.axis_index("x"), lax.axis_index("y"), lax.axis_index("c")


def _other_chips(x, y):
    return [(1 - x, y), (x, 1 - y), (1 - x, 1 - y)]


def _chip_index_scalar():
    ix, iy, _ = _my_pos()
    return jnp.reshape(2 * ix + iy, (1,)).astype(jnp.int32)


def _core_index_scalar():
    return jnp.reshape(lax.axis_index("c"), (1,)).astype(jnp.int32)


class _Gather8:
    def __init__(self, x_refs, out_refs, send_sems, recv_sems, local_sems):
        self.x_refs, self.out_refs = x_refs, out_refs
        self.send_sems, self.recv_sems, self.local_sems = send_sems, recv_sems, local_sems
        self.x, self.y, self.c = _my_pos()
        self.me, self.sibling = (self.x, self.y, self.c), (self.x, self.y, 1 - self.c)
        self.chips = _other_chips(self.x, self.y)

    def _rows(self, a, px, py, pc):
        m_per = self.x_refs[a].shape[0]
        return self.out_refs[a].at[pl.ds((4 * px + 2 * py + pc) * m_per, m_per), :]

    def _copy(self, a, k, block, to, src=None):
        return pltpu.make_async_remote_copy(
            src_ref=self._rows(a, *block) if src is None else src, dst_ref=self._rows(a, *block),
            send_sem=self.send_sems.at[7 * a + k], recv_sem=self.recv_sems.at[7 * a + k], device_id=to,
            device_id_type=MESH)

    def _local(self, a):
        return pltpu.make_async_copy(self.x_refs[a], self._rows(a, *self.me), self.local_sems.at[a])

    def start(self):
        for a in range(len(self.x_refs)):
            self._local(a).start()
            self._copy(a, 0, self.me, self.sibling, src=self.x_refs[a]).start()
            for j, chip in enumerate(self.chips):
                self._copy(a, 1 + j, self.me, (*chip, self.c), src=self.x_refs[a]).start()

    def forward(self):
        for a in range(len(self.x_refs)):
            for j, chip in enumerate(self.chips):
                self._copy(a, 1 + j, (*chip, self.c), self.me).wait_recv()
                self._copy(a, 4 + j, (*chip, self.c), self.sibling).start()

    def finish(self):
        for a in range(len(self.x_refs)):
            self._copy(a, 0, self.sibling, self.me).wait_recv()
            for j, chip in enumerate(self.chips):
                self._copy(a, 4 + j, (*chip, 1 - self.c), self.me).wait_recv()
        for a in range(len(self.x_refs)):
            for k in range(7):
                self._copy(a, k, self.me, self.me).wait_send()
            self._local(a).wait()

    @staticmethod
    def sems(n_v):
        return [pltpu.SemaphoreType.DMA((7 * n_v,)), pltpu.SemaphoreType.DMA((7 * n_v,)),
                pltpu.SemaphoreType.DMA((n_v,))]


def _gathered8_shapes(vs):
    return [jax.ShapeDtypeStruct((N_DEV * v.shape[0], v.shape[1]), v.dtype) for v in vs]


VMEM_WHOLE = pl.BlockSpec(memory_space=pltpu.VMEM)


def _prologue(c_pad, w_ada_s, b_ada_s, shards):
    n = w_ada_s.shape[1]
    n_w = len(shards)

    def body(c_ref, w_ref, b_ref, *rest):
        shard_refs = rest[:n_w]
        sc_ref, modc_ref, modg_ref = rest[n_w:n_w + 3]
        gathered_refs = rest[n_w + 3:2 * n_w + 3]
        call_ref, w_vmem = rest[2 * n_w + 3:2 * n_w + 5]
        sems = rest[2 * n_w + 5:]
        weights = _WeightGather(shard_refs, gathered_refs, ["blk"] * n_w, sems[0], sems[1])
        gather_c = _Gather8([c_ref], [call_ref], sems[2], sems[3], sems[4])
        gather_mod = _Gather8([modc_ref], [modg_ref], sems[5], sems[6], sems[7])
        load_w = pltpu.make_async_copy(w_ref, w_vmem, sems[8])
        weights.start()
        gather_c.start()
        load_w.start()
        gather_c.forward()
        gather_c.finish()
        cv = call_ref[...]
        sc = cv * _sigmoid(cv)
        a_hi = sc.astype(BF16)
        a_lo = (sc - a_hi.astype(F32)).astype(BF16)
        load_w.wait()
        w = w_vmem[...]
        w_hi = w.astype(BF16)
        w_lo = (w - w_hi.astype(F32)).astype(BF16)
        mod = _dot(a_hi, w_hi) + _dot(a_hi, w_lo) + _dot(a_lo, w_hi) + b_ref[...]
        for d in range(N_DEV):
            sc_ref[d:d + 1, :] = sc[8 * d:8 * d + 1, :]
            modc_ref[d:d + 1, :] = mod[8 * d:8 * d + 1, :]
        gather_mod.start()
        weights.forward()
        gather_mod.forward()
        gather_mod.finish()
        weights.forward_diagonal()
        weights.finish()

    outs = pl.pallas_call(
        body, name="prologue",
        out_shape=[jax.ShapeDtypeStruct((N_DEV, D_MODEL), F32), jax.ShapeDtypeStruct((N_DEV, n), F32),
                   jax.ShapeDtypeStruct((N_DEV * N_DEV, n), F32)]
        + [jax.ShapeDtypeStruct(_gathered_shape(sh, "blk"), BF16) for sh in shards],
        in_specs=[VMEM_WHOLE, ANY, VMEM_WHOLE] + [ANY] * n_w,
        out_specs=[VMEM_WHOLE, VMEM_WHOLE, VMEM_WHOLE] + [ANY] * n_w,
        scratch_shapes=[pltpu.VMEM((N_DEV * 8, D_MODEL), F32), pltpu.VMEM(w_ada_s.shape, F32)]
        + _WeightGather.sems(n_w) + _Gather8.sems(1) + _Gather8.sems(1) + [pltpu.SemaphoreType.DMA],
        compiler_params=pltpu.CompilerParams(vmem_limit_bytes=V7X_VMEM_LIMIT),
    )(c_pad, w_ada_s, b_ada_s, *shards)
    return outs[0], outs[2], outs[3:]


def _gathered_shape(shard, kind):
    r, cc = shard.shape
    return (N_CHIPS, r, cc) if kind == "blk" else (r, N_CHIPS * cc)


class _WeightGather:
    N_SEM = 8

    def __init__(self, shards, gathered, kinds, send_sems, recv_sems):
        self.shards, self.gathered, self.kinds = shards, gathered, kinds
        self.send_sems, self.recv_sems = send_sems, recv_sems
        self.x, self.y, self.c = _my_pos()
        self.me, self.sibling = (self.x, self.y, self.c), (self.x, self.y, 1 - self.c)
        self.nbr = ((1 - self.x, self.y), (self.x, 1 - self.y))
        self.diag = 2 * (1 - self.x) + (1 - self.y)

    def _dst(self, a, chip, pc, quarter=None):
        r, cc = self.shards[a].shape
        h = r // 2
        row0, rows = pc * h, h
        if quarter is not None:
            row0, rows = pc * h + quarter * (h // 2), h // 2
        g = self.gathered[a]
        if self.kinds[a] == "blk":
            return g.at[chip, pl.ds(row0, rows), :]
        return g.at[pl.ds(row0, rows), pl.ds(chip * cc, cc)]

    def _copy(self, a, k, region, to, src=None):
        return pltpu.make_async_remote_copy(
            src_ref=region if src is None else src, dst_ref=region, send_sem=self.send_sems.at[a * self.N_SEM + k],
            recv_sem=self.recv_sems.at[a * self.N_SEM + k], device_id=to, device_id_type=MESH)

    def _arrays(self):
        return range(len(self.shards))

    def start(self):
        my_chip = 2 * self.x + self.y
        for a in self._arrays():
            h = self.shards[a].shape[0] // 2
            mine = self.shards[a].at[pl.ds(self.c * h, h), :]
            for j, chip in enumerate(self.nbr):
                self._copy(a, j, self._dst(a, my_chip, self.c), (*chip, self.c), src=mine).start()

    def forward(self):
        for a in self._arrays():
            for j, chip in enumerate(self.nbr):
                cj = 2 * chip[0] + chip[1]
                half = self._dst(a, cj, self.c)
                self._copy(a, j, half, self.me).wait_recv()
                self._copy(a, 2 + j, half, self.sibling).start()
                other = self.nbr[1 - j]
                self._copy(a, 4 + j, self._dst(a, cj, self.c, quarter=j), (*other, self.c)).start()

    def forward_diagonal(self):
        for a in self._arrays():
            for j in range(2):
                quarter = self._dst(a, self.diag, self.c, quarter=j)
                self._copy(a, 4 + j, quarter, self.me).wait_recv()
                self._copy(a, 6 + j, quarter, self.sibling).start()

    def finish(self):
        for a in self._arrays():
            for j, chip in enumerate(self.nbr):
                self._copy(a, 2 + j, self._dst(a, 2 * chip[0] + chip[1], 1 - self.c), self.me).wait_recv()
                self._copy(a, 6 + j, self._dst(a, self.diag, 1 - self.c, quarter=j), self.me).wait_recv()
        for a in self._arrays():
            half = self._dst(a, self.diag, self.c)
            quarter = self._dst(a, self.diag, self.c, quarter=0)
            for k in range(self.N_SEM):
                self._copy(a, k, half if k < 4 else quarter, self.me).wait_send()

    @classmethod
    def sems(cls, n_arr):
        return [pltpu.SemaphoreType.DMA((n_arr * cls.N_SEM,)), pltpu.SemaphoreType.DMA((n_arr * cls.N_SEM,))]


def _insert_own(gathered, shard, kind, chip):
    if kind == "blk":
        return lax.dynamic_update_slice(gathered, shard[None], (chip, 0, 0))
    return lax.dynamic_update_slice(gathered, shard, (0, chip * shard.shape[1]))


def _half_of_full(ref, kind, pc):
    if kind == "blk":
        h = ref.shape[1] // 2
        return ref.at[:, pl.ds(pc * h, h), :]
    h = ref.shape[0] // 2
    return ref.at[pl.ds(pc * h, h), :]


def _half_shape(shape, kind):
    return (shape[0], shape[1] // 2, shape[2]) if kind == "blk" else (shape[0] // 2, shape[1])


class _HalfSwap:
    def __init__(self, ins, outs, kinds, send_sems, recv_sems):
        self.ins, self.outs, self.kinds = ins, outs, kinds
        self.send_sems, self.recv_sems = send_sems, recv_sems
        self.x, self.y, self.c = _my_pos()

    def _copies(self):
        for a in range(len(self.ins)):
            yield pltpu.make_async_remote_copy(
                src_ref=_half_of_full(self.ins[a], self.kinds[a], 1 - self.c), dst_ref=self.outs[a],
                send_sem=self.send_sems.at[a], recv_sem=self.recv_sems.at[a],
                device_id=(self.x, self.y, 1 - self.c), device_id_type=MESH)

    def start(self):
        for cp in self._copies():
            cp.start()

    def wait(self):
        for cp in self._copies():
            cp.wait()

    @staticmethod
    def sems(n_arr):
        return [pltpu.SemaphoreType.DMA((n_arr,)), pltpu.SemaphoreType.DMA((n_arr,))]

    @staticmethod
    def out_shapes(fulls, kinds):
        return [jax.ShapeDtypeStruct(_half_shape(a.shape, k), a.dtype) for a, k in zip(fulls, kinds)]


def _swap_halves(fulls_bf16, kinds, name):
    n_arr = len(fulls_bf16)

    def body(*refs):
        swap = _HalfSwap(refs[:n_arr], refs[n_arr:2 * n_arr], kinds, *refs[2 * n_arr:])
        swap.start()
        swap.wait()

    return pl.pallas_call(
        body, name=name, out_shape=_HalfSwap.out_shapes(fulls_bf16, kinds),
        in_specs=[ANY] * n_arr, out_specs=[ANY] * n_arr, scratch_shapes=_HalfSwap.sems(n_arr),
    )(*fulls_bf16)


def _add_halves(full, got, kind, name):
    hs = _half_shape(full.shape, kind)

    def body(pos_ref, a_ref, b_ref, o_ref, ob_ref):
        p = a_ref[...] + b_ref[...].astype(F32)
        ob_ref[...] = p.astype(BF16)

        @pl.when(pl.program_id(0) == pos_ref[1])
        def _():
            o_ref[...] = p.reshape(o_ref.shape)

    if kind == "blk":
        nb, h, cc = hs
        own = pl.BlockSpec((1, h, cc), lambda b, pos_ref: (b, pos_ref[0], 0))
        other = pl.BlockSpec((1, h, cc), lambda b, pos_ref: (b, 0, 0))
    else:
        h, cc = hs[0], hs[1] // N_CHIPS
        own = pl.BlockSpec((h, cc), lambda b, pos_ref: (pos_ref[0], b))
        other = pl.BlockSpec((h, cc), lambda b, pos_ref: (0, b))
    pos = jnp.concatenate([_core_index_scalar(), _chip_index_scalar()])
    return pl.pallas_call(
        body, name=name, out_shape=(jax.ShapeDtypeStruct((h, cc), F32), jax.ShapeDtypeStruct(hs, BF16)),
        grid_spec=pltpu.PrefetchScalarGridSpec(
            num_scalar_prefetch=1, grid=(N_CHIPS,), in_specs=[own, other],
            out_specs=(pl.BlockSpec((h, cc), lambda b, pos_ref: (0, 0)), other)),
        compiler_params=_params(("arbitrary",)),
    )(pos, full, got)


def _rx_shape(part_shape, kind):
    if kind == "blk":
        return (3, part_shape[1], part_shape[2])
    return (3, part_shape[0], part_shape[1] // N_CHIPS)


class _ChipExchange:
    def __init__(self, parts, rxs, kinds, send_sems, recv_sems):
        self.parts, self.rxs, self.kinds = parts, rxs, kinds
        self.send_sems, self.recv_sems = send_sems, recv_sems
        self.x, self.y, self.c = _my_pos()
        self.chips = _other_chips(self.x, self.y)

    def _copies(self):
        for a in range(len(self.parts)):
            for j, chip in enumerate(self.chips):
                cj = 2 * chip[0] + chip[1]
                if self.kinds[a] == "blk":
                    src = self.parts[a].at[cj]
                else:
                    cc = self.parts[a].shape[1] // N_CHIPS
                    src = self.parts[a].at[:, pl.ds(cj * cc, cc)]
                yield pltpu.make_async_remote_copy(
                    src_ref=src, dst_ref=self.rxs[a].at[j], send_sem=self.send_sems.at[a * 3 + j],
                    recv_sem=self.recv_sems.at[a * 3 + j], device_id=(*chip, self.c), device_id_type=MESH)

    def start(self):
        for cp in self._copies():
            cp.start()

    def wait(self):
        for cp in self._copies():
            cp.wait_recv()
        for cp in self._copies():
            cp.wait_send()

    @staticmethod
    def sems(n_arr):
        return [pltpu.SemaphoreType.DMA((n_arr * 3,)), pltpu.SemaphoreType.DMA((n_arr * 3,))]


HBM_SPEC = pl.BlockSpec(memory_space=pltpu.HBM)
SEM_SPEC = pl.BlockSpec(memory_space=pltpu.SEMAPHORE)
DATAFLOW = pltpu.SideEffectType.DATAFLOW_SIDE_EFFECTING


def _exchange_start(part, name):
    rx_shape = _rx_shape(part.shape, "blk")

    def body(part_ref, rx_ref, send_sems, recv_sems, part_thru, rx_thru, token):
        _ChipExchange([part_ref], [rx_ref], ["blk"], send_sems, recv_sems).start()
        token[...] = jnp.zeros_like(token)

    return pl.pallas_call(
        body, name=name,
        out_shape=(pltpu.SemaphoreType.DMA((3,)), pltpu.SemaphoreType.DMA((3,)), pltpu.HBM(part.shape, part.dtype),
                   pltpu.HBM(rx_shape, BF16), jax.ShapeDtypeStruct((8, LANES), F32)),
        in_specs=(HBM_SPEC, HBM_SPEC), out_specs=(SEM_SPEC, SEM_SPEC, HBM_SPEC, HBM_SPEC, VMEM_WHOLE),
        input_output_aliases={0: 2, 1: 3}, compiler_params=pltpu.CompilerParams(has_side_effects=DATAFLOW),
    )(pltpu.with_memory_space_constraint(part, pltpu.HBM),
      pltpu.with_memory_space_constraint(lax.empty(rx_shape, BF16), pltpu.HBM))


def _exchange_wait(send_sems, recv_sems, part_thru, rx_thru, after, name):
    def body(part_ref, rx_ref, send_sems, recv_sems, after_ref, part_dead, rx_out):
        _ChipExchange([part_ref], [rx_ref], ["blk"], send_sems, recv_sems).wait()

    return pl.pallas_call(
        body, name=name,
        out_shape=(pltpu.HBM(part_thru.shape, part_thru.dtype), pltpu.HBM(rx_thru.shape, rx_thru.dtype)),
        in_specs=(HBM_SPEC, HBM_SPEC, SEM_SPEC, SEM_SPEC, ANY), out_specs=(HBM_SPEC, HBM_SPEC),
        input_output_aliases={0: 0, 1: 1}, compiler_params=pltpu.CompilerParams(has_side_effects=DATAFLOW),
    )(part_thru, rx_thru, send_sems, recv_sems, after)[1]


def _sum_chips(part, rx, tr, name, after):
    _, h, cc = rx.shape
    flips = (2, 1, 3)

    def body(chip_ref, p_ref, rx_ref, after_ref, o_ref):
        own = p_ref[...]
        for mc in range(N_CHIPS):
            @pl.when(chip_ref[0] == mc)
            def _():
                terms = sorted([(mc, None)] + [(mc ^ f, j) for j, f in enumerate(flips)])
                acc = None
                for _, j in terms:
                    t = own if j is None else rx_ref[j].astype(F32)
                    acc = t if acc is None else acc + t
                o_ref[...] = acc

    return pl.pallas_call(
        body, name=name, out_shape=jax.ShapeDtypeStruct((h, cc), F32),
        grid_spec=pltpu.PrefetchScalarGridSpec(
            num_scalar_prefetch=1, grid=(h // tr,),
            in_specs=[pl.BlockSpec((tr, cc), lambda i, chip_ref: (i, 0)),
                      pl.BlockSpec((3, tr, cc), lambda i, chip_ref: (0, i, 0)), ANY],
            out_specs=pl.BlockSpec((tr, cc), lambda i, chip_ref: (i, 0))),
        compiler_params=_params(("arbitrary",)),
    )(_chip_index_scalar(), part, rx, after)


def _share_halves(halves, name):
    n_arr = len(halves)

    def body(*refs):
        ins, outs = refs[:n_arr], refs[n_arr:2 * n_arr]
        send_sems, recv_sems = refs[2 * n_arr:]
        x, y, c = _my_pos()
        cps = []
        for a in range(n_arr):
            cp = pltpu.make_async_remote_copy(
                src_ref=ins[a], dst_ref=outs[a], send_sem=send_sems.at[a], recv_sem=recv_sems.at[a],
                device_id=(x, y, 1 - c), device_id_type=MESH)
            cp.start()
            cps.append(cp)
        for cp in cps:
            cp.wait()

    return pl.pallas_call(
        body, name=name, out_shape=[jax.ShapeDtypeStruct(h.shape, h.dtype) for h in halves],
        in_specs=[ANY] * n_arr, out_specs=[ANY] * n_arr,
        scratch_shapes=[pltpu.SemaphoreType.DMA((n_arr,)), pltpu.SemaphoreType.DMA((n_arr,))],
    )(*halves)


def _bucket_table():
    qi = jnp.arange(BLOCK)[:, None]
    si = jnp.arange(2 * BLOCK)[None, :]
    dist = qi + BLOCK - si
    max_exact = N_BUCKETS // 2
    n = jnp.maximum(dist, 0)
    nf = jnp.maximum(n, max_exact).astype(F32)
    large = max_exact + (jnp.log(nf / max_exact) / math.log(MAX_DISTANCE / max_exact)
                         * (N_BUCKETS - max_exact)).astype(jnp.int32)
    large = jnp.minimum(large, N_BUCKETS - 1)
    return jnp.where(n < max_exact, n, large).astype(F32)


def _prep_tables(bucket, rel_bias, w_s):
    def body(bucket_ref, rb_ref, ws_ref, bias_ref, wsm_ref):
        qi = lax.broadcasted_iota(jnp.int32, (BLOCK, 2 * BLOCK), 0)
        si = lax.broadcasted_iota(jnp.int32, (BLOCK, 2 * BLOCK), 1)
        dist = qi + BLOCK - si
        in_window = (dist >= 0) & (dist < BLOCK)
        bk = bucket_ref[...]
        for h in range(N_HEADS):
            acc = jnp.zeros((BLOCK, 2 * BLOCK), F32)
            for b in range(N_BUCKETS):
                acc = jnp.where(bk == float(b), rb_ref[b, h], acc)
            bias_ref[h] = jnp.where(in_window, acc, NEG_INF)
        ti = lax.broadcasted_iota(jnp.int32, (BLOCK, BLOCK), 0)
        ui = lax.broadcasted_iota(jnp.int32, (BLOCK, BLOCK), 1)
        for g in range(N_GROUPS):
            wsm_ref[g] = jnp.where(ti >= ui, ws_ref[g], 0.0).astype(BF16)

    return pl.pallas_call(
        body, name="prep_tables",
        out_shape=(jax.ShapeDtypeStruct((N_HEADS, BLOCK, 2 * BLOCK), F32),
                   jax.ShapeDtypeStruct((N_GROUPS, BLOCK, BLOCK), BF16)),
        grid=(1,),
        in_specs=[_const_spec((BLOCK, 2 * BLOCK)), pl.BlockSpec(memory_space=pltpu.SMEM),
                  _const_spec((N_GROUPS, BLOCK, BLOCK))],
        out_specs=(_const_spec((N_HEADS, BLOCK, 2 * BLOCK)), _const_spec((N_GROUPS, BLOCK, BLOCK))),
        compiler_params=_params(("arbitrary",)),
    )(bucket, rel_bias, w_s)


def _fwd_in(x, modr, w_in, b_in, tm, shards, kinds):
    s = x.shape[0]
    n_steps = s // tm
    fwd_step, diag_step = (8 * n_steps) // 16, (13 * n_steps) // 16
    n_w = len(shards)

    def body(x_ref, mod_ref, w_ref, b_ref, *rest):
        shard_refs = rest[:n_w]
        h1_ref, q_ref, kv_ref, gu_ref, gv_ref, xb_ref = rest[n_w:n_w + 6]
        gathered_refs = rest[n_w + 6:2 * n_w + 6]
        send_sems, recv_sems = rest[2 * n_w + 6:]
        i = pl.program_id(0)
        gather = _WeightGather(shard_refs, gathered_refs, kinds, send_sems, recv_sems)

        @pl.when(i == 0)
        def _():
            gather.start()

        xv = x_ref[...]
        xb_ref[...] = xv.astype(BF16)
        h1 = (xv * (1.0 + mod_ref[1:2, :]) + mod_ref[0:1, :]).astype(BF16)
        h1_ref[...] = h1
        proj = jnp.concatenate([_dot(h1, w_ref[j]) for j in range(N_CHIPS)], axis=1) + b_ref[...]
        q_ref[...] = (proj[:, :ATTN_W] * Q_SCALE).astype(BF16)
        kv_ref[...] = proj[:, ATTN_W:ATTN_W + 2 * KV_W].astype(BF16)
        gu_ref[...] = proj[:, ATTN_W + 2 * KV_W:ATTN_W + 2 * KV_W + GMLP_W]
        gv_ref[...] = proj[:, ATTN_W + 2 * KV_W + GMLP_W:]

        @pl.when(i == fwd_step)
        def _():
            gather.forward()

        @pl.when(i == diag_step)
        def _():
            gather.forward_diagonal()

        @pl.when(i == n_steps - 1)
        def _():
            gather.finish()

    row = lambda w: pl.BlockSpec((tm, w), lambda i: (i, 0))
    outs = pl.pallas_call(
        body, name="fwd_in",
        out_shape=[jax.ShapeDtypeStruct((s, D_MODEL), BF16), jax.ShapeDtypeStruct((s, ATTN_W), BF16),
                   jax.ShapeDtypeStruct((s, 2 * KV_W), BF16), jax.ShapeDtypeStruct((s, GMLP_W), F32),
                   jax.ShapeDtypeStruct((s, GMLP_W), F32), jax.ShapeDtypeStruct((s, D_MODEL), BF16)]
        + [jax.ShapeDtypeStruct(_gathered_shape(sh, k), BF16) for sh, k in zip(shards, kinds)],
        grid=(n_steps,),
        in_specs=[row(D_MODEL), _const_spec((8, D_MODEL)), _const_spec(w_in.shape), _const_spec((1, IN_W))]
        + [ANY] * n_w,
        out_specs=[row(D_MODEL), row(ATTN_W), row(2 * KV_W), row(GMLP_W), row(GMLP_W), row(D_MODEL)] + [ANY] * n_w,
        scratch_shapes=_WeightGather.sems(n_w),
        compiler_params=_params(("arbitrary",)),
    )(x, modr, w_in, b_in, *shards)
    return outs[:6], outs[6:]


def _kv_variants(kk):
    kf = kk.astype(F32)
    lane = lax.broadcasted_iota(jnp.int32, kf.shape, 1)
    low = lane < HEAD_DIM
    k0_lo = jnp.where(low, kf, 0.0)
    k1_hi = jnp.where(low, 0.0, kf)
    k0_hi = pltpu.roll(k0_lo, HEAD_DIM, 1)
    k1_lo = pltpu.roll(k1_hi, HEAD_DIM, 1)
    return ((k0_lo.astype(BF16), k0_hi.astype(BF16)), (k1_lo.astype(BF16), k1_hi.astype(BF16)))


def _head_kv(h):
    return h // (N_HEADS // N_KV), h % 2


MIX_GROUP = 2


def _interleave(*gens):
    results = [None] * len(gens)
    active = list(enumerate(gens))
    while active:
        still = []
        for i, g in active:
            try:
                next(g)
                still.append((i, g))
            except StopIteration as done:
                results[i] = done.value
        active = still
    return results


def _attn_block_fwd(q_blk, kk, vv, bias_ref, sinks_ref, first_mask):
    kvar = _kv_variants(kk)
    vvar = _kv_variants(vv)
    heads = range(N_HEADS)
    q_pairs = [q_blk[:, (h // 2) * LANES:(h // 2 + 1) * LANES] for h in heads]
    logits = [_dot_nt(q_pairs[h], kvar[_head_kv(h)[0]][_head_kv(h)[1]]) + bias_ref[h] for h in heads]
    if first_mask is not None:
        logits = [jnp.where(first_mask, NEG_INF, lg) for lg in logits]
    yield
    ms = [jnp.maximum(jnp.max(logits[h], axis=-1, keepdims=True), sinks_ref[h]) for h in heads]
    yield
    es = [jnp.exp(logits[h] - ms[h]) for h in heads]
    ess = [jnp.exp(sinks_ref[h] - ms[h]) for h in heads]
    yield
    invs = [1.0 / (jnp.sum(es[h], axis=-1, keepdims=True) + ess[h]) for h in heads]
    probs = [(es[h] * invs[h], ess[h] * invs[h]) for h in heads]
    yield
    outs = [_dot(probs[h][0].astype(BF16), vvar[_head_kv(h)[0]][_head_kv(h)[1]]) for h in heads]
    pairs = [outs[2 * i] + outs[2 * i + 1] for i in range(N_HEADS // 2)]
    return jnp.concatenate(pairs, axis=1), probs, kvar, vvar


def _gmlp_chunk_fwd(gu, gv, ln_g, ln_b, wsm_ref, bsx, amat):
    u, tu = _gelu(gu)
    a, ta = _gelu(gv)
    yield
    mean = _split_dot(a, amat)
    d = a - mean
    yield
    var = _split_dot(d * d, amat)
    yield
    rstd = lax.rsqrt(var + LN_EPS)
    xhat = d * rstd
    vb = (xhat * ln_g + ln_b).astype(BF16)
    yield
    lane = lax.broadcasted_iota(jnp.int32, (BLOCK, LANES), 1)
    low = lane < GROUP_DIM
    cols = []
    for pair in range(N_GROUPS // 2):
        vp = vb[:, pair * LANES:(pair + 1) * LANES]
        cols.append(jnp.where(low, _dot(wsm_ref[2 * pair], vp), _dot(wsm_ref[2 * pair + 1], vp)))
    mixedv = jnp.concatenate(cols, axis=1) + bsx
    return u * mixedv, (u, tu, ta, xhat, rstd, vb, mixedv)


def _rms(a, g):
    r = lax.rsqrt(jnp.mean(a * a, axis=-1, keepdims=True) + LN_EPS)
    return a * r * g, r


def _fwd_mix(q, kv, gu, gv, x, modr, bias, sinks, gln_g, gln_b, wsm, bsx, amat, aog, gog, w_out, ln1_g, ln1_b, tm,
             ffn_shards, ffn_kinds):
    s = x.shape[0]
    nb = tm // BLOCK
    n_steps = s // tm
    fwd_step, diag_step = (7 * n_steps) // 16, (12 * n_steps) // 16
    n_w = len(ffn_shards)

    def body(q_ref, kv_ref, kvp_ref, gu_ref, gv_ref, x_ref, mod_ref, bias_ref, sinks_ref, glng_ref, glnb_ref, wsm_ref,
             bsx_ref, amat_ref, aog_ref, gog_ref, wout_ref, ln1g_ref, ln1b_ref, *rest):
        shard_refs = rest[:n_w]
        x1_ref, x1b_ref, y_ref, mixed_ref = rest[n_w:n_w + 4]
        gathered_refs = rest[n_w + 4:2 * n_w + 4]
        mix_scr, send_sems, recv_sems = rest[2 * n_w + 4:]
        i = pl.program_id(0)
        gather = _WeightGather(shard_refs, gathered_refs, ffn_kinds, send_sems, recv_sems)

        @pl.when(i == 0)
        def _():
            gather.start()

        col = lax.broadcasted_iota(jnp.int32, (BLOCK, 2 * BLOCK), 1)
        for b0 in range(0, nb, MIX_GROUP):
            gens = []
            for b in range(b0, min(b0 + MIX_GROUP, nb)):
                r0 = b * BLOCK
                if b == 0:
                    kvprev = kvp_ref[...]
                    first_mask = (col < BLOCK) & (i == 0)
                else:
                    kvprev = kv_ref[r0 - BLOCK:r0, :]
                    first_mask = None
                kvcur = kv_ref[r0:r0 + BLOCK, :]
                kk = jnp.concatenate([kvprev[:, :KV_W], kvcur[:, :KV_W]], axis=0)
                vv = jnp.concatenate([kvprev[:, KV_W:], kvcur[:, KV_W:]], axis=0)
                gens.append(_attn_block_fwd(q_ref[r0:r0 + BLOCK, :], kk, vv, bias_ref, sinks_ref, first_mask))
                gens.append(_gmlp_chunk_fwd(gu_ref[r0:r0 + BLOCK, :], gv_ref[r0:r0 + BLOCK, :], glng_ref[...],
                                            glnb_ref[...], wsm_ref, bsx_ref[...], amat_ref[...]))
            res = _interleave(*gens)
            for k, b in enumerate(range(b0, min(b0 + MIX_GROUP, nb))):
                r0 = b * BLOCK
                na, _ = _rms(res[2 * k][0], aog_ref[...])
                ng, _ = _rms(res[2 * k + 1][0], gog_ref[...])
                mix_scr[r0:r0 + BLOCK, :ATTN_W] = na.astype(BF16)
                mix_scr[r0:r0 + BLOCK, ATTN_W:] = ng.astype(BF16)
        mixed = mix_scr[...]
        mixed_ref[...] = mixed
        y = _dot(mixed, wout_ref[...])
        y_ref[...] = y.astype(BF16)
        z1 = ALPHA * x_ref[...] + mod_ref[2:3, :] * y
        xhat, _ = _ln_stats(z1)
        x1 = xhat * ln1g_ref[...] + ln1b_ref[...]
        x1_ref[...] = x1
        x1b_ref[...] = x1.astype(BF16)

        @pl.when(i == fwd_step)
        def _():
            gather.forward()

        @pl.when(i == diag_step)
        def _():
            gather.forward_diagonal()

        @pl.when(i == n_steps - 1)
        def _():
            gather.finish()

    row = lambda w: pl.BlockSpec((tm, w), lambda i: (i, 0))
    prev = pl.BlockSpec((BLOCK, 2 * KV_W), lambda i: (jnp.maximum(i * nb - 1, 0), 0))
    outs = pl.pallas_call(
        body, name="fwd_mix",
        out_shape=[jax.ShapeDtypeStruct((s, D_MODEL), F32)] + [jax.ShapeDtypeStruct((s, D_MODEL), BF16)] * 3
        + [jax.ShapeDtypeStruct(_gathered_shape(sh, k), BF16) for sh, k in zip(ffn_shards, ffn_kinds)],
        grid=(n_steps,),
        in_specs=[row(ATTN_W), row(2 * KV_W), prev, row(GMLP_W), row(GMLP_W), row(D_MODEL), _const_spec((8, D_MODEL)),
                  _const_spec((N_HEADS, BLOCK, 2 * BLOCK)), pl.BlockSpec(memory_space=pltpu.SMEM),
                  _const_spec((1, GMLP_W)), _const_spec((1, GMLP_W)), _const_spec((N_GROUPS, BLOCK, BLOCK)),
                  _const_spec((BLOCK, GMLP_W)), _const_spec((GMLP_W, GMLP_W)), _const_spec((1, ATTN_W)),
                  _const_spec((1, GMLP_W)), _const_spec((D_MODEL, D_MODEL)), _const_spec((1, D_MODEL)),
                  _const_spec((1, D_MODEL))] + [ANY] * n_w,
        out_specs=[row(D_MODEL)] * 4 + [ANY] * n_w,
        scratch_shapes=[pltpu.VMEM((tm, D_MODEL), BF16)] + _WeightGather.sems(n_w),
        compiler_params=_params(("arbitrary",)),
    )(q, kv, kv, gu, gv, x, modr, bias, sinks, gln_g, gln_b, wsm, bsx, amat, aog, gog, w_out, ln1_g, ln1_b, *ffn_shards)
    return outs[:4], outs[4:]


FF_BLOCKS = N_CHIPS // 2
FF_CHUNK = D_FF // FF_BLOCKS
FFN_SUB = 256


def _sigmoid(x):
    return 1.0 / (1.0 + jnp.exp(-x))


def _fwd_ffn(x1, target, modr, ln2_g, ln2_b, w_gu, w_dn, tm):
    s = x1.shape[0]

    def body(x1_ref, t_ref, mod_ref, g_ref, b_ref, wgu_ref, wdn_ref, h2_ref, act_ref, dy2_ref, dx1a_ref, acc_ref):
        @pl.when(pl.program_id(0) == 0)
        def _():
            acc_ref[...] = jnp.zeros_like(acc_ref)

        x1v = x1_ref[...]
        h2 = (x1v * (1.0 + mod_ref[4:5, :]) + mod_ref[3:4, :]).astype(BF16)
        h2_ref[...] = h2
        y2 = None
        for cc in range(FF_BLOCKS):
            c0 = cc * FF_CHUNK
            gate = _dot(h2, wgu_ref[cc])
            up = _dot(h2, wgu_ref[FF_BLOCKS + cc])
            act_ref[:, c0:c0 + FF_CHUNK] = gate.astype(BF16)
            act_ref[:, D_FF + c0:D_FF + c0 + FF_CHUNK] = up.astype(BF16)
            a = (gate * _sigmoid(gate) * up).astype(BF16)
            part = _dot(a, wdn_ref[c0:c0 + FF_CHUNK, :])
            y2 = part if y2 is None else y2 + part
        g2 = mod_ref[5:6, :]
        z2 = ALPHA * x1v + g2 * y2
        xhat, rstd = _ln_stats(z2)
        gain = g_ref[...]
        diff = xhat * gain + b_ref[...] - t_ref[...]
        dx2 = diff * (1.0 / D_MODEL)
        dz2 = _ln_bwd(dx2 * gain, xhat, rstd)
        dx1a_ref[...] = ALPHA * dz2
        dy2_ref[...] = (g2 * dz2).astype(BF16)
        acc_ref[0:1, :] += _colsum(diff * diff)
        acc_ref[1:2, :] += _colsum(dx2 * xhat)
        acc_ref[2:3, :] += _colsum(dx2)
        acc_ref[3:4, :] += _colsum(dz2 * y2)

    row = lambda w: pl.BlockSpec((tm, w), lambda i: (i, 0))
    return pl.pallas_call(
        body, name="fwd_ffn",
        out_shape=(jax.ShapeDtypeStruct((s, D_MODEL), BF16), jax.ShapeDtypeStruct((s, 2 * D_FF), BF16),
                   jax.ShapeDtypeStruct((s, D_MODEL), BF16), jax.ShapeDtypeStruct((s, D_MODEL), F32),
                   jax.ShapeDtypeStruct((8, D_MODEL), F32)),
        grid=(s // tm,),
        in_specs=[row(D_MODEL), row(D_MODEL), _const_spec((8, D_MODEL)), _const_spec((1, D_MODEL)),
                  _const_spec((1, D_MODEL)), _const_spec((N_CHIPS, D_MODEL, FF_CHUNK), single=True),
                  _const_spec((D_FF, D_MODEL), single=True)],
        out_specs=(row(D_MODEL), row(2 * D_FF), row(D_MODEL), row(D_MODEL), _const_spec((8, D_MODEL))),
        compiler_params=_params(("arbitrary",)),
    )(x1, target, modr, ln2_g, ln2_b, w_gu, w_dn)


def _bwd_ffn(dy2, act, w_gu, w_dn, tm):
    s = dy2.shape[0]

    def body(dy2_ref, act_ref, wgu_ref, wdn_ref, a_ref, dgu_ref, dh2_ref):
        dy2v = dy2_ref[...]
        dh2 = None
        for cc in range(FF_BLOCKS):
            c0 = cc * FF_CHUNK
            da = _dot_nt(dy2v, wdn_ref[c0:c0 + FF_CHUNK, :])
            gate = act_ref[:, c0:c0 + FF_CHUNK].astype(F32)
            up = act_ref[:, D_FF + c0:D_FF + c0 + FF_CHUNK].astype(F32)
            sg = _sigmoid(gate)
            sl = gate * sg
            a_ref[:, c0:c0 + FF_CHUNK] = (sl * up).astype(BF16)
            dgate = (da * up * (sg * (1.0 + gate * (1.0 - sg)))).astype(BF16)
            dup = (da * sl).astype(BF16)
            dgu_ref[:, c0:c0 + FF_CHUNK] = dgate
            dgu_ref[:, D_FF + c0:D_FF + c0 + FF_CHUNK] = dup
            part = _dot_nt(dgate, wgu_ref[cc]) + _dot_nt(dup, wgu_ref[FF_BLOCKS + cc])
            dh2 = part if dh2 is None else dh2 + part
        dh2_ref[...] = dh2.astype(BF16)

    row = lambda w: pl.BlockSpec((tm, w), lambda i: (i, 0))
    return pl.pallas_call(
        body, name="bwd_ffn",
        out_shape=(jax.ShapeDtypeStruct((s, D_FF), BF16), jax.ShapeDtypeStruct((s, 2 * D_FF), BF16),
                   jax.ShapeDtypeStruct((s, D_MODEL), BF16)),
        grid=(s // tm,),
        in_specs=[row(D_MODEL), row(2 * D_FF), _const_spec((N_CHIPS, D_MODEL, FF_CHUNK), single=True),
                  _const_spec((D_FF, D_MODEL), single=True)],
        out_specs=(row(D_FF), row(2 * D_FF), row(D_MODEL)),
        compiler_params=_params(("parallel",)),
    )(dy2, act, w_gu, w_dn)


def _bwd_mid(dh2, dx1a, x1, x, y, modr, ln1_g, w_out, tm, swap_fulls, swap_kinds):
    s = x.shape[0]
    n_steps = s // tm
    n_g = len(swap_fulls)

    def body(dh2_ref, dx1a_ref, x1_ref, x_ref, y_ref, mod_ref, g_ref, wout_ref, *rest):
        full_refs = rest[:n_g]
        dxa_ref, dy_ref, dmix_ref, acc_ref = rest[n_g:n_g + 4]
        got_refs = rest[n_g + 4:2 * n_g + 4]
        swap = _HalfSwap(full_refs, got_refs, swap_kinds, *rest[2 * n_g + 4:])
        i = pl.program_id(0)

        @pl.when(i == 0)
        def _():
            swap.start()
            acc_ref[...] = jnp.zeros_like(acc_ref)

        dh2 = dh2_ref[...].astype(F32)
        x1v = x1_ref[...].astype(F32)
        yv = y_ref[...].astype(F32)
        g1 = mod_ref[2:3, :]
        dx1 = dx1a_ref[...] + dh2 * (1.0 + mod_ref[4:5, :])
        z1 = ALPHA * x_ref[...] + g1 * yv
        xhat, rstd = _ln_stats(z1)
        dz1 = _ln_bwd(dx1 * g_ref[...], xhat, rstd)
        dxa_ref[...] = (ALPHA * dz1).astype(BF16)
        dy = (g1 * dz1).astype(BF16)
        dy_ref[...] = dy
        dmix_ref[...] = _dot_nt(dy, wout_ref[...]).astype(BF16)
        acc_ref[0:1, :] += _colsum(dh2 * x1v)
        acc_ref[1:2, :] += _colsum(dh2)
        acc_ref[2:3, :] += _colsum(dx1 * xhat)
        acc_ref[3:4, :] += _colsum(dx1)
        acc_ref[4:5, :] += _colsum(dz1 * yv)

        @pl.when(i == n_steps - 1)
        def _():
            swap.wait()

    row = lambda w: pl.BlockSpec((tm, w), lambda i: (i, 0))
    outs = pl.pallas_call(
        body, name="bwd_mid",
        out_shape=[jax.ShapeDtypeStruct((s, D_MODEL), BF16), jax.ShapeDtypeStruct((s, D_MODEL), BF16),
                   jax.ShapeDtypeStruct((s, D_MODEL), BF16), jax.ShapeDtypeStruct((8, D_MODEL), F32)]
        + _HalfSwap.out_shapes(swap_fulls, swap_kinds),
        grid=(n_steps,),
        in_specs=[row(D_MODEL)] * 5 + [_const_spec((8, D_MODEL)), _const_spec((1, D_MODEL)),
                                       _const_spec((D_MODEL, D_MODEL))] + [ANY] * n_g,
        out_specs=[row(D_MODEL), row(D_MODEL), row(D_MODEL), _const_spec((8, D_MODEL))] + [ANY] * n_g,
        scratch_shapes=_HalfSwap.sems(n_g),
        compiler_params=_params(("arbitrary",)),
    )(dh2, dx1a, x1, x, y, modr, ln1_g, w_out, *swap_fulls)
    return outs[:4], outs[4:]


def _fold_kv(t0, t1):
    lane = lax.broadcasted_iota(jnp.int32, t0.shape, 1)
    f0 = t0 + pltpu.roll(t0, HEAD_DIM, 1)
    f1 = t1 + pltpu.roll(t1, HEAD_DIM, 1)
    return jnp.where(lane < HEAD_DIM, f0, f1)


def _bwd_mix(q, kv, gu, gv, dmix, bias, sinks, gln_g, gln_b, wsm, bsx, amat, aog, gog, grad_parts, grad_kinds):
    s = q.shape[0]
    tile = 2 * BLOCK
    n_steps = s // tile
    n_g = len(grad_parts)

    def body(q_ref, kv_ref, kvp_ref, gu_ref, gv_ref, dmix_ref, bias_ref, sinks_ref, glng_ref, glnb_ref, wsm_ref,
             bsx_ref, amat_ref, aog_ref, gog_ref, *rest):
        part_refs = rest[:n_g]
        dq_ref, dkv_ref, dgu_ref, dgv_ref, gbias_ref, dws_ref, dbs_ref, vec_ref, dsink_ref = rest[n_g:n_g + 9]
        rx_refs = rest[n_g + 9:2 * n_g + 9]
        carry, done, send_sems, recv_sems = rest[2 * n_g + 9:]
        n = pl.program_id(0)
        exchange = _ChipExchange(part_refs, rx_refs, grad_kinds, send_sems, recv_sems)

        @pl.when(n == 0)
        def _():
            exchange.start()
            carry[...] = jnp.zeros_like(carry)
            done[...] = jnp.zeros_like(done)
            gbias_ref[...] = jnp.zeros_like(gbias_ref)
            dws_ref[...] = jnp.zeros_like(dws_ref)
            dbs_ref[...] = jnp.zeros_like(dbs_ref)
            vec_ref[...] = jnp.zeros_like(vec_ref)
            dsink_ref[...] = jnp.zeros_like(dsink_ref)

        @pl.when(n == n_steps)
        def _():
            dkv_ref[:BLOCK, :] = done[...].astype(BF16)
            dkv_ref[BLOCK:, :] = carry[...].astype(BF16)
            exchange.wait()

        @pl.when(n < n_steps)
        def _():
            col = lax.broadcasted_iota(jnp.int32, (BLOCK, 2 * BLOCK), 1)
            lane = lax.broadcasted_iota(jnp.int32, (BLOCK, LANES), 1)
            low = lane < HEAD_DIM
            rows = [slice(0, BLOCK), slice(BLOCK, tile)]
            kv_blocks = [kvp_ref[...], kv_ref[rows[0], :], kv_ref[rows[1], :]]
            masks = [(col < BLOCK) & (n == 0), None]
            q_blks = [q_ref[r, :] for r in rows]
            fwd = []
            for b in range(2):
                kk = jnp.concatenate([kv_blocks[b][:, :KV_W], kv_blocks[b + 1][:, :KV_W]], axis=0)
                vv = jnp.concatenate([kv_blocks[b][:, KV_W:], kv_blocks[b + 1][:, KV_W:]], axis=0)
                fwd.append(_attn_block_fwd(q_blks[b], kk, vv, bias_ref, sinks_ref, masks[b]))
                fwd.append(_gmlp_chunk_fwd(gu_ref[rows[b], :], gv_ref[rows[b], :], glng_ref[...], glnb_ref[...],
                                           wsm_ref, bsx_ref[...], amat_ref[...]))
            res = _interleave(*fwd[:2]) + _interleave(*fwd[2:])

            def gating_bwd(b, d_gm, saved):
                u, tu, ta, xhat, rstd, vb, mixedv = saved
                dgu_ref[rows[b], :] = (d_gm * mixedv * _gelu_grad(gu_ref[rows[b], :], tu)).astype(BF16)
                dmx = d_gm * u
                dmxb = dmx.astype(BF16)
                yield
                dvn_cols, dws = [], []
                for pair in range(N_GROUPS // 2):
                    dp_ = dmxb[:, pair * LANES:(pair + 1) * LANES]
                    vp = vb[:, pair * LANES:(pair + 1) * LANES]
                    dvn_cols.append(
                        jnp.where(low, _dot_tn(wsm_ref[2 * pair], dp_), _dot_tn(wsm_ref[2 * pair + 1], dp_)))
                    zero = jnp.zeros_like(dp_)
                    dws.append(_dot_nt(jnp.where(low, dp_, zero), vp))
                    dws.append(_dot_nt(jnp.where(low, zero, dp_), vp))
                dvn = jnp.concatenate(dvn_cols, axis=1)
                yield
                dxh = dvn * glng_ref[...]
                am = amat_ref[...]
                m1 = _split_dot(dxh, am)
                m2 = _split_dot(dxh * xhat, am)
                yield
                da = rstd * (dxh - m1 - xhat * m2)
                dgv_ref[rows[b], :] = (da * _gelu_grad(gv_ref[rows[b], :], ta)).astype(BF16)
                return dmx, dws, _colsum(dvn * xhat), _colsum(dvn)

            def attention_bwd(b, d_attn, probs, kvar, vvar):
                heads = range(N_HEADS)
                sels = [low if h % 2 == 0 else jnp.logical_not(low) for h in heads]
                pair_of = lambda a, h: a[:, (h // 2) * LANES:(h // 2 + 1) * LANES]
                do_hs = [jnp.where(sels[h], pair_of(d_attn, h), 0.0).astype(BF16) for h in heads]
                q_hs = [jnp.where(sels[h], pair_of(q_blks[b], h), jnp.zeros((BLOCK, LANES), BF16)) for h in heads]
                dps = [_dot_nt(do_hs[h], vvar[_head_kv(h)[0]][_head_kv(h)[1]]) for h in heads]
                yield
                deltas = [jnp.sum(probs[h][0] * dps[h], axis=-1, keepdims=True) for h in heads]
                yield
                dss = [probs[h][0] * (dps[h] - deltas[h]) for h in heads]
                dsinks = [-(probs[h][1] * deltas[h]) for h in heads]
                dsbs = [ds.astype(BF16) for ds in dss]
                pbs = [probs[h][0].astype(BF16) for h in heads]
                yield
                dqs = [_dot(dsbs[h], kvar[_head_kv(h)[0]][_head_kv(h)[1]]) for h in heads]
                tks = [_dot_tn(dsbs[h], q_hs[h]) for h in heads]
                tvs = [_dot_tn(pbs[h], do_hs[h]) for h in heads]
                dq_cols = [dqs[2 * i] + dqs[2 * i + 1] for i in range(N_HEADS // 2)]
                dq_ref[rows[b], :] = (jnp.concatenate(dq_cols, axis=1) * Q_SCALE).astype(BF16)
                per_kv = N_HEADS // N_KV
                kv_sum = lambda ts, kvh: sum(ts[kvh * per_kv + 1:(kvh + 1) * per_kv], ts[kvh * per_kv])
                dkk = _fold_kv(kv_sum(tks, 0), kv_sum(tks, 1))
                dvv = _fold_kv(kv_sum(tvs, 0), kv_sum(tvs, 1))
                return jnp.concatenate([dkk, dvv], axis=1), dss, dsinks

            bwd, rms_g = [], []
            for b in range(2):
                attn, probs, kvar, vvar = res[2 * b]
                gm, saved = res[2 * b + 1]
                na_unit, r_a = _rms(attn, 1.0)
                ng_unit, r_g = _rms(gm, 1.0)
                dmix = dmix_ref[rows[b], :].astype(F32)
                dn_a = dmix[:, :ATTN_W]
                dn_g = dmix[:, ATTN_W:]
                rms_g.append((_colsum(dn_a * na_unit), _colsum(dn_g * ng_unit)))
                t_a = dn_a * aog_ref[...]
                d_attn = r_a * t_a - na_unit * (r_a * jnp.mean(t_a * na_unit, axis=-1, keepdims=True))
                t_g = dn_g * gog_ref[...]
                d_gm = r_g * t_g - ng_unit * (r_g * jnp.mean(t_g * ng_unit, axis=-1, keepdims=True))
                bwd.append(attention_bwd(b, d_attn, probs, kvar, vvar))
                bwd.append(gating_bwd(b, d_gm, saved))
            (dkv_a, dss_a, dsk_a), (dmx_a, dws_a, glg_a, glb_a) = _interleave(*bwd[:2])
            (dkv_b, dss_b, dsk_b), (dmx_b, dws_b, glg_b, glb_b) = _interleave(*bwd[2:])

            vec_ref[0:1, :] += rms_g[0][0] + rms_g[1][0]
            vec_ref[1:2, :] += rms_g[0][1] + rms_g[1][1]
            vec_ref[2:3, :] += glg_a + glg_b
            vec_ref[3:4, :] += glb_a + glb_b
            dbs_ref[...] += dmx_a + dmx_b
            for g in range(N_GROUPS):
                dws_ref[g] += dws_a[g] + dws_b[g]
            for h in range(N_HEADS):
                gbias_ref[h] += dss_a[h] + dss_b[h]
                dsink_ref[h] += dsk_a[h] + dsk_b[h]

            dkv_ref[:BLOCK, :] = done[...].astype(BF16)
            dkv_ref[BLOCK:, :] = (carry[...] + dkv_a[:BLOCK]).astype(BF16)
            done[...] = dkv_a[BLOCK:] + dkv_b[:BLOCK]
            carry[...] = dkv_b[BLOCK:]

    last = n_steps - 1
    cur = lambda w: pl.BlockSpec((tile, w), lambda n: (jnp.minimum(n, last), 0))
    late = lambda w: pl.BlockSpec((tile, w), lambda n: (jnp.clip(n - 1, 0, last), 0))
    before = pl.BlockSpec((BLOCK, 2 * KV_W), lambda n: (jnp.clip(2 * n - 1, 0, 2 * last + 1), 0))
    outs = pl.pallas_call(
        body, name="bwd_mix",
        out_shape=[jax.ShapeDtypeStruct((s, ATTN_W), BF16), jax.ShapeDtypeStruct((s, 2 * KV_W), BF16),
                   jax.ShapeDtypeStruct((s, GMLP_W), BF16), jax.ShapeDtypeStruct((s, GMLP_W), BF16),
                   jax.ShapeDtypeStruct((N_HEADS, BLOCK, 2 * BLOCK), F32),
                   jax.ShapeDtypeStruct((N_GROUPS, BLOCK, BLOCK), F32),
                   jax.ShapeDtypeStruct((BLOCK, GMLP_W), F32), jax.ShapeDtypeStruct((8, GMLP_W), F32),
                   jax.ShapeDtypeStruct((N_HEADS, BLOCK, 1), F32)]
        + [jax.ShapeDtypeStruct(_rx_shape(p.shape, k), BF16) for p, k in zip(grad_parts, grad_kinds)],
        grid=(n_steps + 1,),
        in_specs=[cur(ATTN_W), cur(2 * KV_W), before, cur(GMLP_W), cur(GMLP_W), cur(D_MODEL),
                  _const_spec((N_HEADS, BLOCK, 2 * BLOCK)), pl.BlockSpec(memory_space=pltpu.SMEM),
                  _const_spec((1, GMLP_W)), _const_spec((1, GMLP_W)), _const_spec((N_GROUPS, BLOCK, BLOCK)),
                  _const_spec((BLOCK, GMLP_W)), _const_spec((GMLP_W, GMLP_W)), _const_spec((1, ATTN_W)),
                  _const_spec((1, GMLP_W))] + [ANY] * n_g,
        out_specs=[cur(ATTN_W), late(2 * KV_W), cur(GMLP_W), cur(GMLP_W),
                   _const_spec((N_HEADS, BLOCK, 2 * BLOCK)), _const_spec((N_GROUPS, BLOCK, BLOCK)),
                   _const_spec((BLOCK, GMLP_W)), _const_spec((8, GMLP_W)), _const_spec((N_HEADS, BLOCK, 1))]
        + [ANY] * n_g,
        scratch_shapes=[pltpu.VMEM((BLOCK, 2 * KV_W), F32), pltpu.VMEM((BLOCK, 2 * KV_W), F32)]
        + _ChipExchange.sems(n_g),
        compiler_params=_params(("arbitrary",)),
    )(q, kv, kv, gu, gv, dmix, bias, sinks, gln_g, gln_b, wsm, bsx, amat, aog, gog, *grad_parts)
    return outs[:9], outs[9:]


def _mix_finalize(gbias, bucket, dws, dbs, dsink):
    def body(gb_ref, bucket_ref, dws_ref, dbs_ref, dsink_ref, tall_ref):
        bk = bucket_ref[...]
        lane = lax.broadcasted_iota(jnp.int32, (N_BUCKETS, LANES), 1)
        rowi = lax.broadcasted_iota(jnp.int32, (N_BUCKETS, LANES), 0)
        drb = jnp.zeros((N_BUCKETS, LANES), F32)
        dsk = jnp.zeros((8, LANES), F32)
        lane8 = lax.broadcasted_iota(jnp.int32, (8, LANES), 1)
        for h in range(N_HEADS):
            g = gb_ref[h]
            for b in range(N_BUCKETS):
                tot = jnp.sum(_colsum(jnp.where(bk == float(b), g, 0.0)), axis=1, keepdims=True)
                drb = jnp.where((lane == h) & (rowi == b), tot, drb)
            sk = jnp.sum(dsink_ref[h], axis=0, keepdims=True)
            dsk = jnp.where(lane8 == h, sk, dsk)
        tall_ref[TALL_RB:TALL_RB + N_BUCKETS, :] = drb
        tall_ref[TALL_SK:TALL_SK + 8, :] = dsk
        ti = lax.broadcasted_iota(jnp.int32, (BLOCK, BLOCK), 0)
        ui = lax.broadcasted_iota(jnp.int32, (BLOCK, BLOCK), 1)
        for g in range(N_GROUPS):
            tall_ref[g * BLOCK:(g + 1) * BLOCK, :] = jnp.where(ti >= ui, dws_ref[g], 0.0)
        gi = lax.broadcasted_iota(jnp.int32, (GMLP_W, LANES), 0) // GROUP_DIM
        li = lax.broadcasted_iota(jnp.int32, (GMLP_W, LANES), 1)
        ind = jnp.where(gi == li, 1.0, 0.0).astype(BF16)
        d = dbs_ref[...]
        hi = d.astype(BF16)
        r1 = d - hi.astype(F32)
        mid = r1.astype(BF16)
        lo = (r1 - mid.astype(F32)).astype(BF16)
        dbsg = _dot(hi, ind) + _dot(mid, ind) + _dot(lo, ind)
        tall_ref[TALL_BS:TALL_BS + N_GROUPS, :] = dbsg.T[:N_GROUPS, :]

    return pl.pallas_call(
        body, name="mix_finalize", out_shape=jax.ShapeDtypeStruct((TALL_ROWS, LANES), F32), grid=(1,),
        in_specs=[_const_spec((N_HEADS, BLOCK, 2 * BLOCK)), _const_spec((BLOCK, 2 * BLOCK)),
                  _const_spec((N_GROUPS, BLOCK, BLOCK)), _const_spec((BLOCK, GMLP_W)),
                  _const_spec((N_HEADS, BLOCK, 1))],
        out_specs=_const_spec((TALL_ROWS, LANES)),
        compiler_params=_params(("arbitrary",)),
    )(gbias, bucket, dws, dbs, dsink)


def _bwd_in(dq, dkv, dgu, dgv, dxa, x, modr, w_in, tm):
    s = x.shape[0]

    def body(dq_ref, dkv_ref, dgu_ref, dgv_ref, dxa_ref, x_ref, mod_ref, w_ref, gx_ref, acc_ref, db_ref):
        @pl.when(pl.program_id(0) == 0)
        def _():
            acc_ref[...] = jnp.zeros_like(acc_ref)
            db_ref[...] = jnp.zeros_like(db_ref)

        dproj = jnp.concatenate([dq_ref[...], dkv_ref[...], dgu_ref[...], dgv_ref[...]], axis=1)
        wb = IN_W // N_CHIPS
        dh1 = sum([_dot_nt(dproj[:, j * wb:(j + 1) * wb], w_ref[j]) for j in range(1, N_CHIPS)],
                  _dot_nt(dproj[:, :wb], w_ref[0]))
        gx_ref[...] = dxa_ref[...].astype(F32) + dh1 * (1.0 + mod_ref[1:2, :])
        acc_ref[0:1, :] += _colsum(dh1 * x_ref[...].astype(F32))
        acc_ref[1:2, :] += _colsum(dh1)
        db_ref[0:1, :] += _colsum(dproj.astype(F32))

    row = lambda w: pl.BlockSpec((tm, w), lambda i: (i, 0))
    return pl.pallas_call(
        body, name="bwd_in",
        out_shape=(jax.ShapeDtypeStruct((s, D_MODEL), F32), jax.ShapeDtypeStruct((8, D_MODEL), F32),
                   jax.ShapeDtypeStruct((8, IN_W), F32)),
        grid=(s // tm,),
        in_specs=[row(ATTN_W), row(2 * KV_W), row(GMLP_W), row(GMLP_W), row(D_MODEL), row(D_MODEL),
                  _const_spec((8, D_MODEL)), _const_spec(w_in.shape)],
        out_specs=(row(D_MODEL), _const_spec((8, D_MODEL)), _const_spec((8, IN_W))),
        compiler_params=_params(("arbitrary",)),
    )(dq, dkv, dgu, dgv, dxa, x, modr, w_in)


def _wgrad(a, bs, tm, tk, name, owner_blocks=False, gather_vs=()):
    k_all, m = a.shape
    n = sum(b.shape[1] for b in bs)
    nk = k_all // tk
    nm = m // tm
    n_b = len(bs)
    n_v = len(gather_vs)
    wb = n // N_CHIPS

    def body(a_ref, *rest):
        b_refs, v_refs = rest[:n_b], rest[n_b:n_b + n_v]
        o_ref, ob_ref = rest[n_b + n_v:n_b + n_v + 2]
        vg_refs = rest[n_b + n_v + 2:n_b + 2 * n_v + 2]
        i, k = pl.program_id(0), pl.program_id(1)
        if n_v:
            gather = _Gather8(v_refs, vg_refs, *rest[n_b + 2 * n_v + 2:])

            @pl.when((i == 0) & (k == 0))
            def _():
                gather.start()

            @pl.when((i == nm - 1) & (k == 0))
            def _():
                gather.forward()

        @pl.when(k == 0)
        def _():
            o_ref[...] = jnp.zeros_like(o_ref)

        b = b_refs[0][...] if n_b == 1 else jnp.concatenate([r[...] for r in b_refs], axis=1)
        if owner_blocks:
            av = a_ref[...]
            for j in range(N_CHIPS):
                o_ref[j] += _dot_tn(av, b[:, j * wb:(j + 1) * wb])
        else:
            o_ref[...] += _dot_tn(a_ref[...], b)

        @pl.when(k == nk - 1)
        def _():
            ob_ref[...] = o_ref[...].astype(BF16)

        if n_v:
            @pl.when((i == nm - 1) & (k == nk - 1))
            def _():
                gather.finish()

    if owner_blocks:
        out_spec = pl.BlockSpec((N_CHIPS, tm, wb), lambda i, k: (0, i, 0))
        shape = (N_CHIPS, m, wb)
    else:
        out_spec = pl.BlockSpec((tm, n), lambda i, k: (i, 0))
        shape = (m, n)
    outs = pl.pallas_call(
        body, name=name,
        out_shape=[jax.ShapeDtypeStruct(shape, F32), jax.ShapeDtypeStruct(shape, BF16)] + _gathered8_shapes(gather_vs),
        grid=(nm, nk),
        in_specs=[pl.BlockSpec((tk, tm), lambda i, k: (k, i))]
        + [pl.BlockSpec((tk, b.shape[1]), lambda i, k: (k, 0)) for b in bs] + [ANY] * n_v,
        out_specs=[out_spec, out_spec] + [ANY] * n_v,
        scratch_shapes=_Gather8.sems(n_v) if n_v else [],
        compiler_params=_params(("arbitrary", "arbitrary") if n_v else ("parallel", "arbitrary")),
    )(a, *bs, *gather_vs)
    return outs[0], outs[1], outs[2:]


def _adam_math(w, g, m, v):
    m2 = ADAM_B1 * m + (1.0 - ADAM_B1) * g
    v2 = ADAM_B2 * v + (1.0 - ADAM_B2) * (g * g)
    m_hat = m2 / (1.0 - ADAM_B1 ** ADAM_STEP)
    v_hat = v2 / (1.0 - ADAM_B2 ** ADAM_STEP)
    delta = -ADAM_LR * (m_hat / (jnp.sqrt(v_hat) + ADAM_EPS) + ADAM_WD * w)
    return delta, m2, v2


def _transposed(g):
    r, c = g.shape
    pieces = []
    for lo in range(0, c, LANES):
        width = min(LANES, c - lo)
        piece = g[:, lo:lo + width]
        if width < LANES:
            piece = jnp.concatenate([piece, jnp.zeros((r, LANES - width), g.dtype)], axis=1)
        pieces.append(piece.T[:width])
    return jnp.concatenate(pieces, axis=0)


def _adam_halves(w, mine, got, m, v, tr, name, transposed=False):
    r, cc = w.shape[::-1] if transposed else w.shape
    h = r // 2
    nt = h // tr

    def body(c_ref, w_ref, mine_ref, got_ref, m_ref, v_ref, g_ref, d_ref, m2_ref, v2_ref):
        g = jnp.where(pl.program_id(0) == c_ref[0], mine_ref[...], got_ref[...])
        if transposed:
            g = _transposed(g)
        g_ref[...] = g
        d, m2, v2 = _adam_math(w_ref[...], g, m_ref[...], v_ref[...])
        d_ref[...] = d
        m2_ref[...] = m2
        v2_ref[...] = v2

    if transposed:
        full = pl.BlockSpec((cc, tr), lambda hh, i, c_ref: (0, hh * nt + i))
    else:
        full = pl.BlockSpec((tr, cc), lambda hh, i, c_ref: (hh * nt + i, 0))
    half = pl.BlockSpec((tr, cc), lambda hh, i, c_ref: (i, 0))
    shp = jax.ShapeDtypeStruct(w.shape, F32)
    return pl.pallas_call(
        body, name=name, out_shape=(shp, shp, shp, shp),
        grid_spec=pltpu.PrefetchScalarGridSpec(
            num_scalar_prefetch=1, grid=(2, nt), in_specs=[full, half, half, full, full],
            out_specs=(full, full, full, full)),
        compiler_params=_params(("arbitrary", "arbitrary")),
    )(_core_index_scalar(), w, mine, got, m, v)


def _adam_w_ada(sc_t, dmod_cols, w, m, v, tr):
    r, cc = w.shape

    def body(sct_ref, dm_ref, w_ref, m_ref, v_ref, g_ref, d_ref, m2_ref, v2_ref):
        g = sct_ref[:, 0:1] * dm_ref[0:1, :]
        for k in range(1, N_DEV):
            g = g + sct_ref[:, k:k + 1] * dm_ref[k:k + 1, :]
        g_ref[...] = g
        d, m2, v2 = _adam_math(w_ref[...], g, m_ref[...], v_ref[...])
        d_ref[...] = d
        m2_ref[...] = m2
        v2_ref[...] = v2

    spec = pl.BlockSpec((tr, cc), lambda i: (i, 0))
    shp = jax.ShapeDtypeStruct((r, cc), F32)
    return pl.pallas_call(
        body, name="adam_w_ada", out_shape=(shp, shp, shp, shp), grid=(r // tr,),
        in_specs=[pl.BlockSpec((tr, N_DEV), lambda i: (i, 0)), _const_spec((N_DEV, cc)), spec, spec, spec],
        out_specs=(spec, spec, spec, spec), compiler_params=_params(("parallel",)),
    )(sc_t, dmod_cols, w, m, v)


def _pack_wide(acc_i, acc_m, acc_f, db_in, vec):
    arrs = [acc_i, acc_m, acc_f, db_in, vec]
    i_, m_, f_, b_, v_ = range(5)
    src = {"b_in": (b_, 0), "ln1_g": (m_, 2), "ln1_b": (m_, 3), "ln2_g": (f_, 1), "ln2_b": (f_, 2),
           "gmlp_ln_g": (v_, 2), "gmlp_ln_b": (v_, 3), "attn_out_g": (v_, 0), "gmlp_out_g": (v_, 1), "loss": (f_, 0)}
    dmod = [(i_, 1), (i_, 0), (m_, 4), (m_, 1), (m_, 0), (f_, 3)]

    def body(*refs):
        ins, wide_ref = refs[:5], refs[5]
        wide_ref[...] = jnp.zeros_like(wide_ref)
        for k, (a, row) in enumerate(dmod):
            wide_ref[0:1, k * D_MODEL:(k + 1) * D_MODEL] = ins[a][row:row + 1, :]
        for name, (a, row) in src.items():
            r, off, n = WIDE_LAYOUT[name]
            wide_ref[r:r + 1, off:off + n] = ins[a][row:row + 1, :]

    return pl.pallas_call(
        body, name="pack_wide", out_shape=jax.ShapeDtypeStruct((8, WIDE_W), F32), grid=(1,),
        in_specs=[_const_spec(a.shape) for a in arrs], out_specs=_const_spec((8, WIDE_W)),
        compiler_params=_params(("arbitrary",)),
    )(*arrs)


def _adam_small(gw, gt, wide_wmv, w_s, b_s, rel_bias, sinks, after):
    names = list(WIDE_PARAMS)
    tall = [("gmlp_w_s", w_s), ("gmlp_b_s", b_s), ("rel_bias", rel_bias), ("attn_sinks", sinks)]
    ins = [gw, gt]
    for n in names:
        ins += list(wide_wmv[n])
    for _, t in tall:
        ins += list(t)
    n_in = len(ins)

    def body(*refs):
        gw_ref, gt_ref = refs[0], refs[1]
        wmv = refs[2:n_in]
        dmod_ref, loss_ref = refs[n_in + 1], refs[n_in + 2]
        outs = refs[n_in + 3:]

        def tall_sum(r0, nr):
            g = gt_ref[r0:r0 + nr, :]
            for d in range(1, N_DEV):
                g = g + gt_ref[d * TALL_ROWS + r0:d * TALL_ROWS + r0 + nr, :]
            return g

        def emit(k, g, w_ref, m_ref, v_ref):
            d, m2, v2 = _adam_math(w_ref[...], g, m_ref[...], v_ref[...])
            outs[4 * k][...] = g
            outs[4 * k + 1][...] = d
            outs[4 * k + 2][...] = m2
            outs[4 * k + 3][...] = v2

        gsum = gw_ref[0:8, :]
        for d in range(1, N_DEV):
            gsum = gsum + gw_ref[8 * d:8 * d + 8, :]
        for d in range(N_DEV):
            dmod_ref[d:d + 1, :] = gw_ref[8 * d:8 * d + 1, :]
        for k, n in enumerate(names):
            r, off, sz = WIDE_LAYOUT[n]
            emit(k, gsum[r:r + 1, off:off + sz], *wmv[3 * k:3 * k + 3])
        r, off, sz = WIDE_LAYOUT["loss"]
        tot = jnp.sum(gsum[r:r + 1, off:off + sz], axis=1, keepdims=True)
        loss_ref[...] = jnp.broadcast_to(tot * (0.5 / D_MODEL), loss_ref.shape)

        k0 = len(names)
        ws_refs = wmv[3 * k0:3 * k0 + 3]
        for g in range(N_GROUPS):
            rows = slice(g * BLOCK, (g + 1) * BLOCK)
            gg = tall_sum(g * BLOCK, BLOCK)
            d, m2, v2 = _adam_math(ws_refs[0][rows, :], gg, ws_refs[1][rows, :], ws_refs[2][rows, :])
            outs[4 * k0][rows, :] = gg
            outs[4 * k0 + 1][rows, :] = d
            outs[4 * k0 + 2][rows, :] = m2
            outs[4 * k0 + 3][rows, :] = v2
        emit(k0 + 1, tall_sum(TALL_BS, N_GROUPS), *wmv[3 * (k0 + 1):3 * (k0 + 1) + 3])
        emit(k0 + 2, tall_sum(TALL_RB, N_BUCKETS)[:, :N_HEADS], *wmv[3 * (k0 + 2):3 * (k0 + 2) + 3])
        emit(k0 + 3, tall_sum(TALL_SK, 8)[0:1, :N_HEADS], *wmv[3 * (k0 + 3):3 * (k0 + 3) + 3])

    out_shapes = [jax.ShapeDtypeStruct((N_DEV, WIDE_W), F32), jax.ShapeDtypeStruct((8, LANES), F32)]
    for n in names:
        out_shapes += [jax.ShapeDtypeStruct(wide_wmv[n][0].shape, F32)] * 4
    for _, t in tall:
        out_shapes += [jax.ShapeDtypeStruct(t[0].shape, F32)] * 4
    res = pl.pallas_call(
        body, name="adam_small", out_shape=out_shapes, grid=(1,),
        in_specs=[_const_spec(a.shape) for a in ins] + [ANY], out_specs=[_const_spec(o.shape) for o in out_shapes],
        compiler_params=_params(("arbitrary",)),
    )(*ins, after)
    out = {}
    for k, n in enumerate(names + [t[0] for t in tall]):
        out[n] = tuple(res[2 + 4 * k:6 + 4 * k])
    return res[0], res[1], out


def kernel(x, c, rel_bias, w_ada, b_ada, w_in, b_in, attn_sinks, gmlp_ln_g, gmlp_ln_b, gmlp_w_s, gmlp_b_s, attn_out_g, gmlp_out_g, w_out, ln1_g, ln1_b, w_gate_up, w_down, ln2_g, ln2_b, loss_target, m_rel_bias, m_w_ada, m_b_ada, m_w_in, m_b_in, m_attn_sinks, m_gmlp_ln_g, m_gmlp_ln_b, m_gmlp_w_s, m_gmlp_b_s, m_attn_out_g, m_gmlp_out_g, m_w_out, m_ln1_g, m_ln1_b, m_w_gate_up, m_w_down, m_ln2_g, m_ln2_b, v_rel_bias, v_w_ada, v_b_ada, v_w_in, v_b_in, v_attn_sinks, v_gmlp_ln_g, v_gmlp_ln_b, v_gmlp_w_s, v_gmlp_b_s, v_attn_out_g, v_gmlp_out_g, v_w_out, v_ln1_g, v_ln1_b, v_w_gate_up, v_w_down, v_ln2_g, v_ln2_b):
    ix, iy, ic = _my_pos()
    chip = 2 * ix + iy
    dev = 4 * ix + 2 * iy + ic
    s = x.shape[1]
    xs = x[0]
    tgt = loss_target[0]
    tm_big = min(512, s)
    tm_ffn = min(FFN_SUB, s)
    n_ada = w_ada.shape[2]

    w_in_s, w_out_s = w_in[0].astype(BF16), w_out[0].astype(BF16)
    w_gu_s, w_dn_s = w_gate_up[0].astype(BF16), w_down[0].astype(BF16)
    sc_all, mod_rows, (w_in_g, w_out_g) = _prologue(
        jnp.pad(c, ((0, 7), (0, 0))), w_ada[0], lax.dynamic_slice_in_dim(b_ada, chip * n_ada, n_ada, axis=1),
        [w_in_s, w_out_s])
    mod_all = mod_rows.reshape(N_DEV, N_DEV, -1)
    mod_row = lax.dynamic_index_in_dim(mod_all[0::2], dev, axis=1, keepdims=False)
    modr = jnp.pad(mod_row.reshape(6, D_MODEL), ((0, 2), (0, 0)))
    w_in_f = _insert_own(w_in_g, w_in_s, "blk", chip)

    bucket = _bucket_table()
    bias, wsm = _prep_tables(bucket, rel_bias, gmlp_w_s[0])
    bsx = jnp.repeat(gmlp_b_s[0].T, GROUP_DIM, axis=1)
    amat = _group_mean_matrix()
    sinks = attn_sinks[0]

    (h1, q, kv, gu, gv, xb), (w_dn_g,) = _fwd_in(xs, modr, w_in_f, b_in, tm_big, [w_dn_s], ["blk"])
    w_out_f = _insert_own(w_out_g, w_out_s, "blk", chip).reshape(D_MODEL, D_MODEL)
    (x1, x1b, y, mixed), (w_gu_g,) = _fwd_mix(
        q, kv, gu, gv, xs, modr, bias, sinks, gmlp_ln_g, gmlp_ln_b, wsm, bsx, amat, attn_out_g, gmlp_out_g, w_out_f,
        ln1_g, ln1_b, tm_big, [w_gu_s], ["blk"])
    assert w_gate_up.shape[2] == FF_CHUNK
    w_gu_f = _insert_own(w_gu_g, w_gu_s, "blk", chip)
    w_dn_f = _insert_own(w_dn_g, w_dn_s, "blk", chip).reshape(D_FF, D_MODEL)
    h2, act, dy2, dx1a, acc_f = _fwd_ffn(x1, tgt, modr, ln2_g, ln2_b, w_gu_f, w_dn_f, tm_ffn)

    a_act, dgu_ff, dh2 = _bwd_ffn(dy2, act, w_gu_f, w_dn_f, min(FFN_SUB, s))
    g_dn, g_dn_b, _ = _wgrad(a_act, [dy2], D_FF // 2, min(1024, s), "wgrad_down")
    g_gu, g_gu_b, _ = _wgrad(h2, [dgu_ff], 512, min(512, s), "wgrad_gate_up")
    blk3 = lambda a, rows: a.reshape(N_CHIPS, rows, a.shape[1])
    (dxa, dy, dmix, acc_m), (got_dn, got_gu) = _bwd_mid(
        dh2, dx1a, x1b, xs, y, modr, ln1_g, w_out_f, tm_big, [blk3(g_dn_b, D_FF // N_CHIPS), g_gu_b], ["blk", "cols"])
    g_out, g_out_b, _ = _wgrad(mixed, [dy], 512, min(2048, s), "wgrad_out")
    (got_out,) = _swap_halves([blk3(g_out_b, D_MODEL // N_CHIPS)], ["blk"], "rs_swap_out")
    kinds_a = ["blk", "cols", "blk"]
    fulls_a = [blk3(g_dn, D_FF // N_CHIPS), g_gu, blk3(g_out, D_MODEL // N_CHIPS)]
    gots_a = [got_dn, got_gu, got_out]
    parts_a = [_add_halves(f, g, k, "rs_add_a%d" % i) for i, (f, g, k) in enumerate(zip(fulls_a, gots_a, kinds_a))]
    (dq, dkv, dgu, dgv, gbias, dws, dbs, vec, dsink), rxs_a = _bwd_mix(
        q, kv, gu, gv, dmix, bias, sinks, gmlp_ln_g, gmlp_ln_b, wsm, bsx, amat, attn_out_g, gmlp_out_g,
        [p[1] for p in parts_a], kinds_a)
    tall_g = _mix_finalize(gbias, bucket, dws, dbs, dsink)
    grad_x, acc_i, db_in = _bwd_in(dq, dkv, dgu, dgv, dxa, xb, modr, w_in_f, tm_big)

    wide_g = _pack_wide(acc_i, acc_m, acc_f, db_in, vec)
    full_in, full_in_b, (gw, gt) = _wgrad(h1, [dq, dkv, dgu, dgv], 512, min(1024, s), "wgrad_in", owner_blocks=True,
                                          gather_vs=[wide_g, tall_g])
    (got_in,) = _swap_halves([full_in_b], ["blk"], "rs_swap_in")
    part_in = _add_halves(full_in, got_in, "blk", "rs_add_in")
    in_send, in_recv, in_part, in_rx, token = _exchange_start(part_in[1], "rs_chips_in_start")
    wide_wmv ={"b_ada": (b_ada, m_b_ada, v_b_ada), "b_in": (b_in, m_b_in, v_b_in),
                "ln1_g": (ln1_g, m_ln1_g, v_ln1_g), "ln1_b": (ln1_b, m_ln1_b, v_ln1_b),
                "ln2_g": (ln2_g, m_ln2_g, v_ln2_g), "ln2_b": (ln2_b, m_ln2_b, v_ln2_b),
                "gmlp_ln_g": (gmlp_ln_g, m_gmlp_ln_g, v_gmlp_ln_g), "gmlp_ln_b": (gmlp_ln_b, m_gmlp_ln_b, v_gmlp_ln_b),
                "attn_out_g": (attn_out_g, m_attn_out_g, v_attn_out_g),
                "gmlp_out_g": (gmlp_out_g, m_gmlp_out_g, v_gmlp_out_g)}
    rows2 = lambda a: a.reshape(-1, a.shape[-1])
    dmod_all, loss_t, small = _adam_small(
        gw, gt, wide_wmv, tuple(rows2(a) for a in (gmlp_w_s, m_gmlp_w_s, v_gmlp_w_s)),
        tuple(rows2(a) for a in (gmlp_b_s, m_gmlp_b_s, v_gmlp_b_s)), (rel_bias, m_rel_bias, v_rel_bias),
        (attn_sinks, m_attn_sinks, v_attn_sinks), token)
    loss = loss_t[0, 0]

    dmod_cols = lax.dynamic_slice_in_dim(dmod_all, chip * n_ada, n_ada, axis=1)
    g_ada, d_ada, m_ada, v_ada = _adam_w_ada(sc_all.T, dmod_cols, w_ada[0], m_w_ada[0], v_w_ada[0], 256)

    sums = [(parts_a[0][0], rxs_a[0], 176), (parts_a[1][0], rxs_a[1], 256), (parts_a[2][0], rxs_a[2], 128)]
    mine = [_sum_chips(p, rx, tr, "rs_sum_%d" % i, loss_t) for i, (p, rx, tr) in enumerate(sums)]
    got = _share_halves(mine, "rs_share")
    gs_dn, d_dn, m_dn, v_dn = _adam_halves(w_down[0], mine[0], got[0], m_w_down[0], v_w_down[0], 176, "adam_w_down")
    gs_gu, d_gu, m_gu, v_gu = _adam_halves(w_gate_up[0], mine[1], got[1], m_w_gate_up[0], v_w_gate_up[0], 256,
                                           "adam_w_gate_up")
    gs_out, d_out, m_out, v_out = _adam_halves(w_out[0], mine[2], got[2], m_w_out[0], v_w_out[0], 128, "adam_w_out")

    rx_in = _exchange_wait(in_send, in_recv, in_part, in_rx, d_gu, "rs_chips_in_wait")
    mine_in = _sum_chips(part_in[0], rx_in, 256, "rs_sum_in", rx_in)
    (got_in_half,) = _share_halves([mine_in], "rs_share_in")
    in_t = _adam_halves(w_in[0].T, mine_in, got_in_half, m_w_in[0].T, v_w_in[0].T, 256, "adam_w_in", transposed=True)
    gs_in, d_in, m_in, v_in = (a.T for a in in_t)

    big = {"w_ada": (g_ada, d_ada, m_ada, v_ada), "w_in": (gs_in, d_in, m_in, v_in), "w_out": (gs_out, d_out, m_out, v_out),
           "w_gate_up": (gs_gu, d_gu, m_gu, v_gu), "w_down": (gs_dn, d_dn, m_dn, v_dn)}
    order = ["rel_bias", "w_ada", "b_ada", "w_in", "b_in", "attn_sinks", "gmlp_ln_g", "gmlp_ln_b", "gmlp_w_s", "gmlp_b_s",
             "attn_out_g", "gmlp_out_g", "w_out", "ln1_g", "ln1_b", "w_gate_up", "w_down", "ln2_g", "ln2_b"]
    shapes = {"gmlp_w_s": gmlp_w_s.shape, "gmlp_b_s": gmlp_b_s.shape}
    outs = [loss, grad_x[None]]
    for k in range(4):
        for name in order:
            if name in big:
                outs.append(big[name][k][None])
            elif name in shapes:
                outs.append(small[name][k].reshape(shapes[name]))
            else:
                outs.append(small[name][k])
    return tuple(outs)
```

```python
import math

import numpy as np
import jax
import jax.numpy as jnp
from jax import lax
from jax.experimental import pallas as pl
from jax.experimental.pallas import tpu as pltpu

F32 = jnp.float32
BF16 = jnp.bfloat16
MESH = pl.DeviceIdType.MESH

D_MODEL = 1024
N_HEADS = 8
N_KV = 2
HEAD_DIM = 64
ATTN_W = N_HEADS * HEAD_DIM
KV_W = N_KV * HEAD_DIM
N_GROUPS = 8
GROUP_DIM = 64
GMLP_W = N_GROUPS * GROUP_DIM
IN_W = ATTN_W + 2 * KV_W + 2 * GMLP_W
BLOCK = 128
N_BUCKETS = 32
MAX_DISTANCE = 128
D_FF = 2816
ALPHA = 2.0 ** 0.25
LN_EPS = 1e-5
NEG_INF = -1e30
ADAM_LR, ADAM_B1, ADAM_B2, ADAM_EPS, ADAM_WD, ADAM_STEP = 0.001, 0.9, 0.999, 1e-8, 0.01, 10
N_CHIPS = 4
N_DEV = 8
LANES = 128
V7X_VMEM_LIMIT = 56 * 2 ** 20
GELU_C = math.sqrt(2.0 / math.pi)
Q_SCALE = HEAD_DIM ** -0.5
ANY = pl.BlockSpec(memory_space=pl.ANY)

TALL_BS = N_GROUPS * BLOCK
TALL_RB = TALL_BS + 8
TALL_SK = TALL_RB + N_BUCKETS
TALL_ROWS = TALL_SK + 8
WIDE_W = 6 * D_MODEL
WIDE_LAYOUT = {
    "b_ada": (0, 0, 6 * D_MODEL),
    "b_in": (1, 0, IN_W), "ln1_g": (1, IN_W, D_MODEL), "ln1_b": (1, IN_W + D_MODEL, D_MODEL),
    "ln2_g": (1, IN_W + 2 * D_MODEL, D_MODEL), "ln2_b": (1, IN_W + 3 * D_MODEL, D_MODEL),
    "gmlp_ln_g": (2, 0, GMLP_W), "gmlp_ln_b": (2, GMLP_W, GMLP_W), "attn_out_g": (2, 2 * GMLP_W, ATTN_W),
    "gmlp_out_g": (2, 2 * GMLP_W + ATTN_W, GMLP_W), "loss": (2, 3 * GMLP_W + ATTN_W, D_MODEL)}
WIDE_PARAMS = tuple(n for n in WIDE_LAYOUT if n != "loss")


def _params(sem=None):
    return pltpu.CompilerParams(dimension_semantics=sem, vmem_limit_bytes=V7X_VMEM_LIMIT)


def _const_spec(shape, single=False):
    nd = len(shape)
    if single:
        return pl.BlockSpec(shape, lambda *_: (0,) * nd, pipeline_mode=pl.Buffered(1))
    return pl.BlockSpec(shape, lambda *_: (0,) * nd)


def _dot(a, b):
    return jnp.dot(a, b, preferred_element_type=F32)


def _dot_nt(a, b):
    return lax.dot_general(a, b, (((1,), (1,)), ((), ())), preferred_element_type=F32)


def _dot_tn(a, b):
    return lax.dot_general(a, b, (((0,), (0,)), ((), ())), preferred_element_type=F32)


def _gelu(x):
    t = jnp.tanh(GELU_C * (x + 0.044715 * x * x * x))
    return 0.5 * x * (1.0 + t), t


def _gelu_grad(x, t):
    return 0.5 * (1.0 + t) + 0.5 * x * (1.0 - t * t) * GELU_C * (1.0 + 3.0 * 0.044715 * x * x)


def _split_dot(x, a):
    hi = x.astype(BF16)
    lo = (x - hi.astype(F32)).astype(BF16)
    return _dot(hi, a) + _dot(lo, a)


def _group_mean_matrix():
    g = np.arange(GMLP_W) // GROUP_DIM
    return jnp.asarray((g[:, None] == g[None, :]).astype(np.float32) / GROUP_DIM, dtype=BF16)


def _ln_stats(z):
    mu = jnp.mean(z, axis=-1, keepdims=True)
    d = z - mu
    var = jnp.mean(d * d, axis=-1, keepdims=True)
    rstd = lax.rsqrt(var + LN_EPS)
    return d * rstd, rstd


def _ln_bwd(dxhat, xhat, rstd):
    m1 = jnp.mean(dxhat, axis=-1, keepdims=True)
    m2 = jnp.mean(dxhat * xhat, axis=-1, keepdims=True)
    return rstd * (dxhat - m1 - xhat * m2)


def _colsum(x):
    return jnp.sum(x, axis=0, keepdims=True)


def _my_pos():
    return lax.axis_index("x"), lax.axis_index("y"), lax.axis_index("c")


def _other_chips(x, y):
    return [(1 - x, y), (x, 1 - y), (1 - x, 1 - y)]


def _chip_index_scalar():
    ix, iy, _ = _my_pos()
    return jnp.reshape(2 * ix + iy, (1,)).astype(jnp.int32)


def _core_index_scalar():
    return jnp.reshape(lax.axis_index("c"), (1,)).astype(jnp.int32)


class _Gather8:
    def __init__(self, x_refs, out_refs, send_sems, recv_sems, local_sems):
        self.x_refs, self.out_refs = x_refs, out_refs
        self.send_sems, self.recv_sems, self.local_sems = send_sems, recv_sems, local_sems
        self.x, self.y, self.c = _my_pos()
        self.me, self.sibling = (self.x, self.y, self.c), (self.x, self.y, 1 - self.c)
        self.chips = _other_chips(self.x, self.y)

    def _rows(self, a, px, py, pc):
        m_per = self.x_refs[a].shape[0]
        return self.out_refs[a].at[pl.ds((4 * px + 2 * py + pc) * m_per, m_per), :]

    def _copy(self, a, k, block, to, src=None):
        return pltpu.make_async_remote_copy(
            src_ref=self._rows(a, *block) if src is None else src, dst_ref=self._rows(a, *block),
            send_sem=self.send_sems.at[7 * a + k], recv_sem=self.recv_sems.at[7 * a + k], device_id=to,
            device_id_type=MESH)

    def _local(self, a):
        return pltpu.make_async_copy(self.x_refs[a], self._rows(a, *self.me), self.local_sems.at[a])

    def start(self):
        for a in range(len(self.x_refs)):
            self._local(a).start()
            self._copy(a, 0, self.me, self.sibling, src=self.x_refs[a]).start()
            for j, chip in enumerate(self.chips):
                self._copy(a, 1 + j, self.me, (*chip, self.c), src=self.x_refs[a]).start()

    def forward(self):
        for a in range(len(self.x_refs)):
            for j, chip in enumerate(self.chips):
                self._copy(a, 1 + j, (*chip, self.c), self.me).wait_recv()
                self._copy(a, 4 + j, (*chip, self.c), self.sibling).start()

    def finish(self):
        for a in range(len(self.x_refs)):
            self._copy(a, 0, self.sibling, self.me).wait_recv()
            for j, chip in enumerate(self.chips):
                self._copy(a, 4 + j, (*chip, 1 - self.c), self.me).wait_recv()
        for a in range(len(self.x_refs)):
            for k in range(7):
                self._copy(a, k, self.me, self.me).wait_send()
            self._local(a).wait()

    @staticmethod
    def sems(n_v):
        return [pltpu.SemaphoreType.DMA((7 * n_v,)), pltpu.SemaphoreType.DMA((7 * n_v,)),
                pltpu.SemaphoreType.DMA((n_v,))]


def _gathered8_shapes(vs):
    return [jax.ShapeDtypeStruct((N_DEV * v.shape[0], v.shape[1]), v.dtype) for v in vs]


VMEM_WHOLE = pl.BlockSpec(memory_space=pltpu.VMEM)


def _prologue(c_pad, w_ada_s, b_ada_s, shards):
    n = w_ada_s.shape[1]
    n_w = len(shards)

    def body(c_ref, w_ref, b_ref, *rest):
        shard_refs = rest[:n_w]
        sc_ref, modc_ref, modg_ref = rest[n_w:n_w + 3]
        gathered_refs = rest[n_w + 3:2 * n_w + 3]
        call_ref, w_vmem = rest[2 * n_w + 3:2 * n_w + 5]
        sems = rest[2 * n_w + 5:]
        weights = _WeightGather(shard_refs, gathered_refs, ["blk"] * n_w, sems[0], sems[1])
        gather_c = _Gather8([c_ref], [call_ref], sems[2], sems[3], sems[4])
        gather_mod = _Gather8([modc_ref], [modg_ref], sems[5], sems[6], sems[7])
        load_w = pltpu.make_async_copy(w_ref, w_vmem, sems[8])
        weights.start()
        gather_c.start()
        load_w.start()
        gather_c.forward()
        gather_c.finish()
        cv = call_ref[...]
        sc = cv * _sigmoid(cv)
        a_hi = sc.astype(BF16)
        a_lo = (sc - a_hi.astype(F32)).astype(BF16)
        load_w.wait()
        w = w_vmem[...]
        w_hi = w.astype(BF16)
        w_lo = (w - w_hi.astype(F32)).astype(BF16)
        mod = _dot(a_hi, w_hi) + _dot(a_hi, w_lo) + _dot(a_lo, w_hi) + b_ref[...]
        for d in range(N_DEV):
            sc_ref[d:d + 1, :] = sc[8 * d:8 * d + 1, :]
            modc_ref[d:d + 1, :] = mod[8 * d:8 * d + 1, :]
        gather_mod.start()
        weights.forward()
        gather_mod.forward()
        gather_mod.finish()
        weights.forward_diagonal()
        weights.finish()

    outs = pl.pallas_call(
        body, name="prologue",
        out_shape=[jax.ShapeDtypeStruct((N_DEV, D_MODEL), F32), jax.ShapeDtypeStruct((N_DEV, n), F32),
                   jax.ShapeDtypeStruct((N_DEV * N_DEV, n), F32)]
        + [jax.ShapeDtypeStruct(_gathered_shape(sh, "blk"), BF16) for sh in shards],
        in_specs=[VMEM_WHOLE, ANY, VMEM_WHOLE] + [ANY] * n_w,
        out_specs=[VMEM_WHOLE, VMEM_WHOLE, VMEM_WHOLE] + [ANY] * n_w,
        scratch_shapes=[pltpu.VMEM((N_DEV * 8, D_MODEL), F32), pltpu.VMEM(w_ada_s.shape, F32)]
        + _WeightGather.sems(n_w) + _Gather8.sems(1) + _Gather8.sems(1) + [pltpu.SemaphoreType.DMA],
        compiler_params=pltpu.CompilerParams(vmem_limit_bytes=V7X_VMEM_LIMIT),
    )(c_pad, w_ada_s, b_ada_s, *shards)
    return outs[0], outs[2], outs[3:]


def _gathered_shape(shard, kind):
    r, cc = shard.shape
    return (N_CHIPS, r, cc) if kind == "blk" else (r, N_CHIPS * cc)


class _WeightGather:
    N_SEM = 8

    def __init__(self, shards, gathered, kinds, send_sems, recv_sems):
        self.shards, self.gathered, self.kinds = shards, gathered, kinds
        self.send_sems, self.recv_sems = send_sems, recv_sems
        self.x, self.y, self.c = _my_pos()
        self.me, self.sibling = (self.x, self.y, self.c), (self.x, self.y, 1 - self.c)
        self.nbr = ((1 - self.x, self.y), (self.x, 1 - self.y))
        self.diag = 2 * (1 - self.x) + (1 - self.y)

    def _dst(self, a, chip, pc, quarter=None):
        r, cc = self.shards[a].shape
        h = r // 2
        row0, rows = pc * h, h
        if quarter is not None:
            row0, rows = pc * h + quarter * (h // 2), h // 2
        g = self.gathered[a]
        if self.kinds[a] == "blk":
            return g.at[chip, pl.ds(row0, rows), :]
        return g.at[pl.ds(row0, rows), pl.ds(chip * cc, cc)]

    def _copy(self, a, k, region, to, src=None):
        return pltpu.make_async_remote_copy(
            src_ref=region if src is None else src, dst_ref=region, send_sem=self.send_sems.at[a * self.N_SEM + k],
            recv_sem=self.recv_sems.at[a * self.N_SEM + k], device_id=to, device_id_type=MESH)

    def _arrays(self):
        return range(len(self.shards))

    def start(self):
        my_chip = 2 * self.x + self.y
        for a in self._arrays():
            h = self.shards[a].shape[0] // 2
            mine = self.shards[a].at[pl.ds(self.c * h, h), :]
            for j, chip in enumerate(self.nbr):
                self._copy(a, j, self._dst(a, my_chip, self.c), (*chip, self.c), src=mine).start()

    def forward(self):
        for a in self._arrays():
            for j, chip in enumerate(self.nbr):
                cj = 2 * chip[0] + chip[1]
                half = self._dst(a, cj, self.c)
                self._copy(a, j, half, self.me).wait_recv()
                self._copy(a, 2 + j, half, self.sibling).start()
                other = self.nbr[1 - j]
                self._copy(a, 4 + j, self._dst(a, cj, self.c, quarter=j), (*other, self.c)).start()

    def forward_diagonal(self):
        for a in self._arrays():
            for j in range(2):
                quarter = self._dst(a, self.diag, self.c, quarter=j)
                self._copy(a, 4 + j, quarter, self.me).wait_recv()
                self._copy(a, 6 + j, quarter, self.sibling).start()

    def finish(self):
        for a in self._arrays():
            for j, chip in enumerate(self.nbr):
                self._copy(a, 2 + j, self._dst(a, 2 * chip[0] + chip[1], 1 - self.c), self.me).wait_recv()
                self._copy(a, 6 + j, self._dst(a, self.diag, 1 - self.c, quarter=j), self.me).wait_recv()
        for a in self._arrays():
            half = self._dst(a, self.diag, self.c)
            quarter = self._dst(a, self.diag, self.c, quarter=0)
            for k in range(self.N_SEM):
                self._copy(a, k, half if k < 4 else quarter, self.me).wait_send()

    @classmethod
    def sems(cls, n_arr):
        return [pltpu.SemaphoreType.DMA((n_arr * cls.N_SEM,)), pltpu.SemaphoreType.DMA((n_arr * cls.N_SEM,))]


def _insert_own(gathered, shard, kind, chip):
    if kind == "blk":
        return lax.dynamic_update_slice(gathered, shard[None], (chip, 0, 0))
    return lax.dynamic_update_slice(gathered, shard, (0, chip * shard.shape[1]))


def _half_of_full(ref, kind, pc):
    if kind == "blk":
        h = ref.shape[1] // 2
        return ref.at[:, pl.ds(pc * h, h), :]
    h = ref.shape[0] // 2
    return ref.at[pl.ds(pc * h, h), :]


def _half_shape(shape, kind):
    return (shape[0], shape[1] // 2, shape[2]) if kind == "blk" else (shape[0] // 2, shape[1])


class _HalfSwap:
    def __init__(self, ins, outs, kinds, send_sems, recv_sems):
        self.ins, self.outs, self.kinds = ins, outs, kinds
        self.send_sems, self.recv_sems = send_sems, recv_sems
        self.x, self.y, self.c = _my_pos()

    def _copies(self):
        for a in range(len(self.ins)):
            yield pltpu.make_async_remote_copy(
                src_ref=_half_of_full(self.ins[a], self.kinds[a], 1 - self.c), dst_ref=self.outs[a],
                send_sem=self.send_sems.at[a], recv_sem=self.recv_sems.at[a],
                device_id=(self.x, self.y, 1 - self.c), device_id_type=MESH)

    def start(self):
        for cp in self._copies():
            cp.start()

    def wait(self):
        for cp in self._copies():
            cp.wait()

    @staticmethod
    def sems(n_arr):
        return [pltpu.SemaphoreType.DMA((n_arr,)), pltpu.SemaphoreType.DMA((n_arr,))]

    @staticmethod
    def out_shapes(fulls, kinds):
        return [jax.ShapeDtypeStruct(_half_shape(a.shape, k), a.dtype) for a, k in zip(fulls, kinds)]


def _swap_halves(fulls_bf16, kinds, name):
    n_arr = len(fulls_bf16)

    def body(*refs):
        swap = _HalfSwap(refs[:n_arr], refs[n_arr:2 * n_arr], kinds, *refs[2 * n_arr:])
        swap.start()
        swap.wait()

    return pl.pallas_call(
        body, name=name, out_shape=_HalfSwap.out_shapes(fulls_bf16, kinds),
        in_specs=[ANY] * n_arr, out_specs=[ANY] * n_arr, scratch_shapes=_HalfSwap.sems(n_arr),
    )(*fulls_bf16)


def _add_halves(full, got, kind, name):
    hs = _half_shape(full.shape, kind)

    def body(pos_ref, a_ref, b_ref, o_ref, ob_ref):
        p = a_ref[...] + b_ref[...].astype(F32)
        ob_ref[...] = p.astype(BF16)

        @pl.when(pl.program_id(0) == pos_ref[1])
        def _():
            o_ref[...] = p.reshape(o_ref.shape)

    if kind == "blk":
        nb, h, cc = hs
        own = pl.BlockSpec((1, h, cc), lambda b, pos_ref: (b, pos_ref[0], 0))
        other = pl.BlockSpec((1, h, cc), lambda b, pos_ref: (b, 0, 0))
    else:
        h, cc = hs[0], hs[1] // N_CHIPS
        own = pl.BlockSpec((h, cc), lambda b, pos_ref: (pos_ref[0], b))
        other = pl.BlockSpec((h, cc), lambda b, pos_ref: (0, b))
    pos = jnp.concatenate([_core_index_scalar(), _chip_index_scalar()])
    return pl.pallas_call(
        body, name=name, out_shape=(jax.ShapeDtypeStruct((h, cc), F32), jax.ShapeDtypeStruct(hs, BF16)),
        grid_spec=pltpu.PrefetchScalarGridSpec(
            num_scalar_prefetch=1, grid=(N_CHIPS,), in_specs=[own, other],
            out_specs=(pl.BlockSpec((h, cc), lambda b, pos_ref: (0, 0)), other)),
        compiler_params=_params(("arbitrary",)),
    )(pos, full, got)


def _rx_shape(part_shape, kind):
    if kind == "blk":
        return (3, part_shape[1], part_shape[2])
    return (3, part_shape[0], part_shape[1] // N_CHIPS)


class _ChipExchange:
    def __init__(self, parts, rxs, kinds, send_sems, recv_sems):
        self.parts, self.rxs, self.kinds = parts, rxs, kinds
        self.send_sems, self.recv_sems = send_sems, recv_sems
        self.x, self.y, self.c = _my_pos()
        self.chips = _other_chips(self.x, self.y)

    def _copies(self):
        for a in range(len(self.parts)):
            for j, chip in enumerate(self.chips):
                cj = 2 * chip[0] + chip[1]
                if self.kinds[a] == "blk":
                    src = self.parts[a].at[cj]
                else:
                    cc = self.parts[a].shape[1] // N_CHIPS
                    src = self.parts[a].at[:, pl.ds(cj * cc, cc)]
                yield pltpu.make_async_remote_copy(
                    src_ref=src, dst_ref=self.rxs[a].at[j], send_sem=self.send_sems.at[a * 3 + j],
                    recv_sem=self.recv_sems.at[a * 3 + j], device_id=(*chip, self.c), device_id_type=MESH)

    def start(self):
        for cp in self._copies():
            cp.start()

    def wait(self):
        for cp in self._copies():
            cp.wait_recv()
        for cp in self._copies():
            cp.wait_send()

    @staticmethod
    def sems(n_arr):
        return [pltpu.SemaphoreType.DMA((n_arr * 3,)), pltpu.SemaphoreType.DMA((n_arr * 3,))]


HBM_SPEC = pl.BlockSpec(memory_space=pltpu.HBM)
SEM_SPEC = pl.BlockSpec(memory_space=pltpu.SEMAPHORE)
DATAFLOW = pltpu.SideEffectType.DATAFLOW_SIDE_EFFECTING


def _exchange_start(part, name):
    rx_shape = _rx_shape(part.shape, "blk")

    def body(part_ref, rx_ref, send_sems, recv_sems, part_thru, rx_thru, token):
        _ChipExchange([part_ref], [rx_ref], ["blk"], send_sems, recv_sems).start()
        token[...] = jnp.zeros_like(token)

    return pl.pallas_call(
        body, name=name,
        out_shape=(pltpu.SemaphoreType.DMA((3,)), pltpu.SemaphoreType.DMA((3,)), pltpu.HBM(part.shape, part.dtype),
                   pltpu.HBM(rx_shape, BF16), jax.ShapeDtypeStruct((8, LANES), F32)),
        in_specs=(HBM_SPEC, HBM_SPEC), out_specs=(SEM_SPEC, SEM_SPEC, HBM_SPEC, HBM_SPEC, VMEM_WHOLE),
        input_output_aliases={0: 2, 1: 3}, compiler_params=pltpu.CompilerParams(has_side_effects=DATAFLOW),
    )(pltpu.with_memory_space_constraint(part, pltpu.HBM),
      pltpu.with_memory_space_constraint(lax.empty(rx_shape, BF16), pltpu.HBM))


def _exchange_wait(send_sems, recv_sems, part_thru, rx_thru, after, name):
    def body(part_ref, rx_ref, send_sems, recv_sems, after_ref, part_dead, rx_out):
        _ChipExchange([part_ref], [rx_ref], ["blk"], send_sems, recv_sems).wait()

    return pl.pallas_call(
        body, name=name,
        out_shape=(pltpu.HBM(part_thru.shape, part_thru.dtype), pltpu.HBM(rx_thru.shape, rx_thru.dtype)),
        in_specs=(HBM_SPEC, HBM_SPEC, SEM_SPEC, SEM_SPEC, ANY), out_specs=(HBM_SPEC, HBM_SPEC),
        input_output_aliases={0: 0, 1: 1}, compiler_params=pltpu.CompilerParams(has_side_effects=DATAFLOW),
    )(part_thru, rx_thru, send_sems, recv_sems, after)[1]


def _sum_chips(part, rx, tr, name, after):
    _, h, cc = rx.shape
    flips = (2, 1, 3)

    def body(chip_ref, p_ref, rx_ref, after_ref, o_ref):
        own = p_ref[...]
        for mc in range(N_CHIPS):
            @pl.when(chip_ref[0] == mc)
            def _():
                terms = sorted([(mc, None)] + [(mc ^ f, j) for j, f in enumerate(flips)])
                acc = None
                for _, j in terms:
                    t = own if j is None else rx_ref[j].astype(F32)
                    acc = t if acc is None else acc + t
                o_ref[...] = acc

    return pl.pallas_call(
        body, name=name, out_shape=jax.ShapeDtypeStruct((h, cc), F32),
        grid_spec=pltpu.PrefetchScalarGridSpec(
            num_scalar_prefetch=1, grid=(h // tr,),
            in_specs=[pl.BlockSpec((tr, cc), lambda i, chip_ref: (i, 0)),
                      pl.BlockSpec((3, tr, cc), lambda i, chip_ref: (0, i, 0)), ANY],
            out_specs=pl.BlockSpec((tr, cc), lambda i, chip_ref: (i, 0))),
        compiler_params=_params(("arbitrary",)),
    )(_chip_index_scalar(), part, rx, after)


def _share_halves(halves, name):
    n_arr = len(halves)

    def body(*refs):
        ins, outs = refs[:n_arr], refs[n_arr:2 * n_arr]
        send_sems, recv_sems = refs[2 * n_arr:]
        x, y, c = _my_pos()
        cps = []
        for a in range(n_arr):
            cp = pltpu.make_async_remote_copy(
                src_ref=ins[a], dst_ref=outs[a], send_sem=send_sems.at[a], recv_sem=recv_sems.at[a],
                device_id=(x, y, 1 - c), device_id_type=MESH)
            cp.start()
            cps.append(cp)
        for cp in cps:
            cp.wait()

    return pl.pallas_call(
        body, name=name, out_shape=[jax.ShapeDtypeStruct(h.shape, h.dtype) for h in halves],
        in_specs=[ANY] * n_arr, out_specs=[ANY] * n_arr,
        scratch_shapes=[pltpu.SemaphoreType.DMA((n_arr,)), pltpu.SemaphoreType.DMA((n_arr,))],
    )(*halves)


def _bucket_table():
    qi = jnp.arange(BLOCK)[:, None]
    si = jnp.arange(2 * BLOCK)[None, :]
    dist = qi + BLOCK - si
    max_exact = N_BUCKETS // 2
    n = jnp.maximum(dist, 0)
    nf = jnp.maximum(n, max_exact).astype(F32)
    large = max_exact + (jnp.log(nf / max_exact) / math.log(MAX_DISTANCE / max_exact)
                         * (N_BUCKETS - max_exact)).astype(jnp.int32)
    large = jnp.minimum(large, N_BUCKETS - 1)
    return jnp.where(n < max_exact, n, large).astype(F32)


def _prep_tables(bucket, rel_bias_t, w_s):
    def body(bucket_ref, rb_ref, ws_ref, bias_ref, wsm_ref):
        qi = lax.broadcasted_iota(jnp.int32, (BLOCK, 2 * BLOCK), 0)
        si = lax.broadcasted_iota(jnp.int32, (BLOCK, 2 * BLOCK), 1)
        dist = qi + BLOCK - si
        in_window = (dist >= 0) & (dist < BLOCK)
        bk = bucket_ref[...]
        for h in range(N_HEADS):
            acc = jnp.zeros((BLOCK, 2 * BLOCK), F32)
            for b in range(N_BUCKETS):
                acc = jnp.where(bk == float(b), rb_ref[h, b], acc)
            bias_ref[h] = jnp.where(in_window, acc, NEG_INF)
        ti = lax.broadcasted_iota(jnp.int32, (BLOCK, BLOCK), 0)
        ui = lax.broadcasted_iota(jnp.int32, (BLOCK, BLOCK), 1)
        for g in range(N_GROUPS):
            wsm_ref[g] = jnp.where(ti >= ui, ws_ref[g], 0.0).astype(BF16)

    return pl.pallas_call(
        body, name="prep_tables",
        out_shape=(jax.ShapeDtypeStruct((N_HEADS, BLOCK, 2 * BLOCK), F32),
                   jax.ShapeDtypeStruct((N_GROUPS, BLOCK, BLOCK), BF16)),
        grid=(1,),
        in_specs=[_const_spec((BLOCK, 2 * BLOCK)), pl.BlockSpec(memory_space=pltpu.SMEM),
                  _const_spec((N_GROUPS, BLOCK, BLOCK))],
        out_specs=(_const_spec((N_HEADS, BLOCK, 2 * BLOCK)), _const_spec((N_GROUPS, BLOCK, BLOCK))),
        compiler_params=_params(("arbitrary",)),
    )(bucket, rel_bias_t, w_s)


def _fwd_in(x, modr, w_in, b_in, tm, shards, kinds):
    s = x.shape[0]
    n_steps = s // tm
    fwd_step, diag_step = (8 * n_steps) // 16, (13 * n_steps) // 16
    n_w = len(shards)

    def body(x_ref, mod_ref, w_ref, b_ref, *rest):
        shard_refs = rest[:n_w]
        h1_ref, q_ref, kv_ref, gu_ref, gv_ref, xb_ref = rest[n_w:n_w + 6]
        gathered_refs = rest[n_w + 6:2 * n_w + 6]
        send_sems, recv_sems = rest[2 * n_w + 6:]
        i = pl.program_id(0)
        gather = _WeightGather(shard_refs, gathered_refs, kinds, send_sems, recv_sems)

        @pl.when(i == 0)
        def _():
            gather.start()

        xv = x_ref[...]
        xb_ref[...] = xv.astype(BF16)
        h1 = (xv * (1.0 + mod_ref[1:2, :]) + mod_ref[0:1, :]).astype(BF16)
        h1_ref[...] = h1
        proj = jnp.concatenate([_dot(h1, w_ref[j]) for j in range(N_CHIPS)], axis=1) + b_ref[...]
        q_ref[...] = (proj[:, :ATTN_W] * Q_SCALE).astype(BF16)
        kv_ref[...] = proj[:, ATTN_W:ATTN_W + 2 * KV_W].astype(BF16)
        gu_ref[...] = proj[:, ATTN_W + 2 * KV_W:ATTN_W + 2 * KV_W + GMLP_W]
        gv_ref[...] = proj[:, ATTN_W + 2 * KV_W + GMLP_W:]

        @pl.when(i == fwd_step)
        def _():
            gather.forward()

        @pl.when(i == diag_step)
        def _():
            gather.forward_diagonal()

        @pl.when(i == n_steps - 1)
        def _():
            gather.finish()

    row = lambda w: pl.BlockSpec((tm, w), lambda i: (i, 0))
    outs = pl.pallas_call(
        body, name="fwd_in",
        out_shape=[jax.ShapeDtypeStruct((s, D_MODEL), BF16), jax.ShapeDtypeStruct((s, ATTN_W), BF16),
                   jax.ShapeDtypeStruct((s, 2 * KV_W), BF16), jax.ShapeDtypeStruct((s, GMLP_W), F32),
                   jax.ShapeDtypeStruct((s, GMLP_W), F32), jax.ShapeDtypeStruct((s, D_MODEL), BF16)]
        + [jax.ShapeDtypeStruct(_gathered_shape(sh, k), BF16) for sh, k in zip(shards, kinds)],
        grid=(n_steps,),
        in_specs=[row(D_MODEL), _const_spec((8, D_MODEL)), _const_spec(w_in.shape), _const_spec((1, IN_W))]
        + [ANY] * n_w,
        out_specs=[row(D_MODEL), row(ATTN_W), row(2 * KV_W), row(GMLP_W), row(GMLP_W), row(D_MODEL)] + [ANY] * n_w,
        scratch_shapes=_WeightGather.sems(n_w),
        compiler_params=_params(("arbitrary",)),
    )(x, modr, w_in, b_in, *shards)
    return outs[:6], outs[6:]


def _kv_variants(kk):
    kf = kk.astype(F32)
    lane = lax.broadcasted_iota(jnp.int32, kf.shape, 1)
    low = lane < HEAD_DIM
    k0_lo = jnp.where(low, kf, 0.0)
    k1_hi = jnp.where(low, 0.0, kf)
    k0_hi = pltpu.roll(k0_lo, HEAD_DIM, 1)
    k1_lo = pltpu.roll(k1_hi, HEAD_DIM, 1)
    return ((k0_lo.astype(BF16), k0_hi.astype(BF16)), (k1_lo.astype(BF16), k1_hi.astype(BF16)))


def _head_kv(h):
    return h // (N_HEADS // N_KV), h % 2


MIX_GROUP = 2


def _interleave(*gens):
    results = [None] * len(gens)
    active = list(enumerate(gens))
    while active:
        still = []
        for i, g in active:
            try:
                next(g)
                still.append((i, g))
            except StopIteration as done:
                results[i] = done.value
        active = still
    return results


def _attn_block_fwd(q_blk, kk, vv, bias_ref, sinks_ref, first_mask):
    kvar = _kv_variants(kk)
    vvar = _kv_variants(vv)
    heads = range(N_HEADS)
    q_pairs = [q_blk[:, (h // 2) * LANES:(h // 2 + 1) * LANES] for h in heads]
    logits = [_dot_nt(q_pairs[h], kvar[_head_kv(h)[0]][_head_kv(h)[1]]) + bias_ref[h] for h in heads]
    if first_mask is not None:
        logits = [jnp.where(first_mask, NEG_INF, lg) for lg in logits]
    yield
    ms = [jnp.maximum(jnp.max(logits[h], axis=-1, keepdims=True), sinks_ref[h]) for h in heads]
    yield
    es = [jnp.exp(logits[h] - ms[h]) for h in heads]
    ess = [jnp.exp(sinks_ref[h] - ms[h]) for h in heads]
    yield
    invs = [1.0 / (jnp.sum(es[h], axis=-1, keepdims=True) + ess[h]) for h in heads]
    probs = [(es[h] * invs[h], ess[h] * invs[h]) for h in heads]
    yield
    outs = [_dot(probs[h][0].astype(BF16), vvar[_head_kv(h)[0]][_head_kv(h)[1]]) for h in heads]
    pairs = [outs[2 * i] + outs[2 * i + 1] for i in range(N_HEADS // 2)]
    return jnp.concatenate(pairs, axis=1), probs, kvar, vvar


def _gmlp_chunk_fwd(gu, gv, ln_g, ln_b, wsm_ref, bsx, amat):
    u, tu = _gelu(gu)
    a, ta = _gelu(gv)
    yield
    mean = _split_dot(a, amat)
    d = a - mean
    yield
    var = _split_dot(d * d, amat)
    yield
    rstd = lax.rsqrt(var + LN_EPS)
    xhat = d * rstd
    vb = (xhat * ln_g + ln_b).astype(BF16)
    yield
    lane = lax.broadcasted_iota(jnp.int32, (BLOCK, LANES), 1)
    low = lane < GROUP_DIM
    cols = []
    for pair in range(N_GROUPS // 2):
        vp = vb[:, pair * LANES:(pair + 1) * LANES]
        cols.append(jnp.where(low, _dot(wsm_ref[2 * pair], vp), _dot(wsm_ref[2 * pair + 1], vp)))
    mixedv = jnp.concatenate(cols, axis=1) + bsx
    return u * mixedv, (u, tu, ta, xhat, rstd, vb, mixedv)


def _rms(a, g):
    r = lax.rsqrt(jnp.mean(a * a, axis=-1, keepdims=True) + LN_EPS)
    return a * r * g, r


def _fwd_mix(q, kv, gu, gv, x, modr, bias, sinks, gln_g, gln_b, wsm, bsx, amat, aog, gog, w_out, ln1_g, ln1_b, tm,
             ffn_shards, ffn_kinds):
    s = x.shape[0]
    nb = tm // BLOCK
    n_steps = s // tm
    fwd_step, diag_step = (7 * n_steps) // 16, (12 * n_steps) // 16
    n_w = len(ffn_shards)

    def body(q_ref, kv_ref, kvp_ref, gu_ref, gv_ref, x_ref, mod_ref, bias_ref, sinks_ref, glng_ref, glnb_ref, wsm_ref,
             bsx_ref, amat_ref, aog_ref, gog_ref, wout_ref, ln1g_ref, ln1b_ref, *rest):
        shard_refs = rest[:n_w]
        x1_ref, x1b_ref, y_ref, mixed_ref = rest[n_w:n_w + 4]
        gathered_refs = rest[n_w + 4:2 * n_w + 4]
        mix_scr, send_sems, recv_sems = rest[2 * n_w + 4:]
        i = pl.program_id(0)
        gather = _WeightGather(shard_refs, gathered_refs, ffn_kinds, send_sems, recv_sems)

        @pl.when(i == 0)
        def _():
            gather.start()

        col = lax.broadcasted_iota(jnp.int32, (BLOCK, 2 * BLOCK), 1)
        for b0 in range(0, nb, MIX_GROUP):
            gens = []
            for b in range(b0, min(b0 + MIX_GROUP, nb)):
                r0 = b * BLOCK
                if b == 0:
                    kvprev = kvp_ref[...]
                    first_mask = (col < BLOCK) & (i == 0)
                else:
                    kvprev = kv_ref[r0 - BLOCK:r0, :]
                    first_mask = None
                kvcur = kv_ref[r0:r0 + BLOCK, :]
                kk = jnp.concatenate([kvprev[:, :KV_W], kvcur[:, :KV_W]], axis=0)
                vv = jnp.concatenate([kvprev[:, KV_W:], kvcur[:, KV_W:]], axis=0)
                gens.append(_attn_block_fwd(q_ref[r0:r0 + BLOCK, :], kk, vv, bias_ref, sinks_ref, first_mask))
                gens.append(_gmlp_chunk_fwd(gu_ref[r0:r0 + BLOCK, :], gv_ref[r0:r0 + BLOCK, :], glng_ref[...],
                                            glnb_ref[...], wsm_ref, bsx_ref[...], amat_ref[...]))
            res = _interleave(*gens)
            for k, b in enumerate(range(b0, min(b0 + MIX_GROUP, nb))):
                r0 = b * BLOCK
                na, _ = _rms(res[2 * k][0], aog_ref[...])
                ng, _ = _rms(res[2 * k + 1][0], gog_ref[...])
                mix_scr[r0:r0 + BLOCK, :ATTN_W] = na.astype(BF16)
                mix_scr[r0:r0 + BLOCK, ATTN_W:] = ng.astype(BF16)
        mixed = mix_scr[...]
        mixed_ref[...] = mixed
        y = _dot(mixed, wout_ref[...])
        y_ref[...] = y.astype(BF16)
        z1 = ALPHA * x_ref[...] + mod_ref[2:3, :] * y
        xhat, _ = _ln_stats(z1)
        x1 = xhat * ln1g_ref[...] + ln1b_ref[...]
        x1_ref[...] = x1
        x1b_ref[...] = x1.astype(BF16)

        @pl.when(i == fwd_step)
        def _():
            gather.forward()

        @pl.when(i == diag_step)
        def _():
            gather.forward_diagonal()

        @pl.when(i == n_steps - 1)
        def _():
            gather.finish()

    row = lambda w: pl.BlockSpec((tm, w), lambda i: (i, 0))
    prev = pl.BlockSpec((BLOCK, 2 * KV_W), lambda i: (jnp.maximum(i * nb - 1, 0), 0))
    outs = pl.pallas_call(
        body, name="fwd_mix",
        out_shape=[jax.ShapeDtypeStruct((s, D_MODEL), F32)] + [jax.ShapeDtypeStruct((s, D_MODEL), BF16)] * 3
        + [jax.ShapeDtypeStruct(_gathered_shape(sh, k), BF16) for sh, k in zip(ffn_shards, ffn_kinds)],
        grid=(n_steps,),
        in_specs=[row(ATTN_W), row(2 * KV_W), prev, row(GMLP_W), row(GMLP_W), row(D_MODEL), _const_spec((8, D_MODEL)),
                  _const_spec((N_HEADS, BLOCK, 2 * BLOCK)), pl.BlockSpec(memory_space=pltpu.SMEM),
                  _const_spec((1, GMLP_W)), _const_spec((1, GMLP_W)), _const_spec((N_GROUPS, BLOCK, BLOCK)),
                  _const_spec((BLOCK, GMLP_W)), _const_spec((GMLP_W, GMLP_W)), _const_spec((1, ATTN_W)),
                  _const_spec((1, GMLP_W)), _const_spec((D_MODEL, D_MODEL)), _const_spec((1, D_MODEL)),
                  _const_spec((1, D_MODEL))] + [ANY] * n_w,
        out_specs=[row(D_MODEL)] * 4 + [ANY] * n_w,
        scratch_shapes=[pltpu.VMEM((tm, D_MODEL), BF16)] + _WeightGather.sems(n_w),
        compiler_params=_params(("arbitrary",)),
    )(q, kv, kv, gu, gv, x, modr, bias, sinks, gln_g, gln_b, wsm, bsx, amat, aog, gog, w_out, ln1_g, ln1_b, *ffn_shards)
    return outs[:4], outs[4:]


FF_BLOCKS = N_CHIPS // 2
FF_CHUNK = D_FF // FF_BLOCKS
FFN_SUB = 256


def _sigmoid(x):
    return 1.0 / (1.0 + jnp.exp(-x))


def _fwd_ffn(x1, target, modr, ln2_g, ln2_b, w_gu, w_dn, tm):
    s = x1.shape[0]

    def body(x1_ref, t_ref, mod_ref, g_ref, b_ref, wgu_ref, wdn_ref, h2_ref, act_ref, dy2_ref, dx1a_ref, acc_ref):
        @pl.when(pl.program_id(0) == 0)
        def _():
            acc_ref[...] = jnp.zeros_like(acc_ref)

        x1v = x1_ref[...]
        h2 = (x1v * (1.0 + mod_ref[4:5, :]) + mod_ref[3:4, :]).astype(BF16)
        h2_ref[...] = h2
        y2 = None
        for cc in range(FF_BLOCKS):
            c0 = cc * FF_CHUNK
            gate = _dot(h2, wgu_ref[cc])
            up = _dot(h2, wgu_ref[FF_BLOCKS + cc])
            act_ref[:, c0:c0 + FF_CHUNK] = gate.astype(BF16)
            act_ref[:, D_FF + c0:D_FF + c0 + FF_CHUNK] = up.astype(BF16)
            a = (gate * _sigmoid(gate) * up).astype(BF16)
            part = _dot(a, wdn_ref[c0:c0 + FF_CHUNK, :])
            y2 = part if y2 is None else y2 + part
        g2 = mod_ref[5:6, :]
        z2 = ALPHA * x1v + g2 * y2
        xhat, rstd = _ln_stats(z2)
        gain = g_ref[...]
        diff = xhat * gain + b_ref[...] - t_ref[...]
        dx2 = diff * (1.0 / D_MODEL)
        dz2 = _ln_bwd(dx2 * gain, xhat, rstd)
        dx1a_ref[...] = ALPHA * dz2
        dy2_ref[...] = (g2 * dz2).astype(BF16)
        acc_ref[0:1, :] += _colsum(diff * diff)
        acc_ref[1:2, :] += _colsum(dx2 * xhat)
        acc_ref[2:3, :] += _colsum(dx2)
        acc_ref[3:4, :] += _colsum(dz2 * y2)

    row = lambda w: pl.BlockSpec((tm, w), lambda i: (i, 0))
    return pl.pallas_call(
        body, name="fwd_ffn",
        out_shape=(jax.ShapeDtypeStruct((s, D_MODEL), BF16), jax.ShapeDtypeStruct((s, 2 * D_FF), BF16),
                   jax.ShapeDtypeStruct((s, D_MODEL), BF16), jax.ShapeDtypeStruct((s, D_MODEL), F32),
                   jax.ShapeDtypeStruct((8, D_MODEL), F32)),
        grid=(s // tm,),
        in_specs=[row(D_MODEL), row(D_MODEL), _const_spec((8, D_MODEL)), _const_spec((1, D_MODEL)),
                  _const_spec((1, D_MODEL)), _const_spec((N_CHIPS, D_MODEL, FF_CHUNK), single=True),
                  _const_spec((D_FF, D_MODEL), single=True)],
        out_specs=(row(D_MODEL), row(2 * D_FF), row(D_MODEL), row(D_MODEL), _const_spec((8, D_MODEL))),
        compiler_params=_params(("arbitrary",)),
    )(x1, target, modr, ln2_g, ln2_b, w_gu, w_dn)


def _bwd_ffn(dy2, act, w_gu, w_dn, tm):
    s = dy2.shape[0]

    def body(dy2_ref, act_ref, wgu_ref, wdn_ref, a_ref, dgu_ref, dh2_ref):
        dy2v = dy2_ref[...]
        dh2 = None
        for cc in range(FF_BLOCKS):
            c0 = cc * FF_CHUNK
            da = _dot_nt(dy2v, wdn_ref[c0:c0 + FF_CHUNK, :])
            gate = act_ref[:, c0:c0 + FF_CHUNK].astype(F32)
            up = act_ref[:, D_FF + c0:D_FF + c0 + FF_CHUNK].astype(F32)
            sg = _sigmoid(gate)
            sl = gate * sg
            a_ref[:, c0:c0 + FF_CHUNK] = (sl * up).astype(BF16)
            dgate = (da * up * (sg * (1.0 + gate * (1.0 - sg)))).astype(BF16)
            dup = (da * sl).astype(BF16)
            dgu_ref[:, c0:c0 + FF_CHUNK] = dgate
            dgu_ref[:, D_FF + c0:D_FF + c0 + FF_CHUNK] = dup
            part = _dot_nt(dgate, wgu_ref[cc]) + _dot_nt(dup, wgu_ref[FF_BLOCKS + cc])
            dh2 = part if dh2 is None else dh2 + part
        dh2_ref[...] = dh2.astype(BF16)

    row = lambda w: pl.BlockSpec((tm, w), lambda i: (i, 0))
    return pl.pallas_call(
        body, name="bwd_ffn",
        out_shape=(jax.ShapeDtypeStruct((s, D_FF), BF16), jax.ShapeDtypeStruct((s, 2 * D_FF), BF16),
                   jax.ShapeDtypeStruct((s, D_MODEL), BF16)),
        grid=(s // tm,),
        in_specs=[row(D_MODEL), row(2 * D_FF), _const_spec((N_CHIPS, D_MODEL, FF_CHUNK), single=True),
                  _const_spec((D_FF, D_MODEL), single=True)],
        out_specs=(row(D_FF), row(2 * D_FF), row(D_MODEL)),
        compiler_params=_params(("parallel",)),
    )(dy2, act, w_gu, w_dn)


def _bwd_mid(dh2, dx1a, x1, x, y, modr, ln1_g, w_out, tm, swap_fulls, swap_kinds):
    s = x.shape[0]
    n_steps = s // tm
    n_g = len(swap_fulls)

    def body(dh2_ref, dx1a_ref, x1_ref, x_ref, y_ref, mod_ref, g_ref, wout_ref, *rest):
        full_refs = rest[:n_g]
        dxa_ref, dy_ref, dmix_ref, acc_ref = rest[n_g:n_g + 4]
        got_refs = rest[n_g + 4:2 * n_g + 4]
        swap = _HalfSwap(full_refs, got_refs, swap_kinds, *rest[2 * n_g + 4:])
        i = pl.program_id(0)

        @pl.when(i == 0)
        def _():
            swap.start()
            acc_ref[...] = jnp.zeros_like(acc_ref)

        dh2 = dh2_ref[...].astype(F32)
        x1v = x1_ref[...].astype(F32)
        yv = y_ref[...].astype(F32)
        g1 = mod_ref[2:3, :]
        dx1 = dx1a_ref[...] + dh2 * (1.0 + mod_ref[4:5, :])
        z1 = ALPHA * x_ref[...] + g1 * yv
        xhat, rstd = _ln_stats(z1)
        dz1 = _ln_bwd(dx1 * g_ref[...], xhat, rstd)
        dxa_ref[...] = (ALPHA * dz1).astype(BF16)
        dy = (g1 * dz1).astype(BF16)
        dy_ref[...] = dy
        dmix_ref[...] = _dot_nt(dy, wout_ref[...]).astype(BF16)
        acc_ref[0:1, :] += _colsum(dh2 * x1v)
        acc_ref[1:2, :] += _colsum(dh2)
        acc_ref[2:3, :] += _colsum(dx1 * xhat)
        acc_ref[3:4, :] += _colsum(dx1)
        acc_ref[4:5, :] += _colsum(dz1 * yv)

        @pl.when(i == n_steps - 1)
        def _():
            swap.wait()

    row = lambda w: pl.BlockSpec((tm, w), lambda i: (i, 0))
    outs = pl.pallas_call(
        body, name="bwd_mid",
        out_shape=[jax.ShapeDtypeStruct((s, D_MODEL), BF16), jax.ShapeDtypeStruct((s, D_MODEL), BF16),
                   jax.ShapeDtypeStruct((s, D_MODEL), BF16), jax.ShapeDtypeStruct((8, D_MODEL), F32)]
        + _HalfSwap.out_shapes(swap_fulls, swap_kinds),
        grid=(n_steps,),
        in_specs=[row(D_MODEL)] * 5 + [_const_spec((8, D_MODEL)), _const_spec((1, D_MODEL)),
                                       _const_spec((D_MODEL, D_MODEL))] + [ANY] * n_g,
        out_specs=[row(D_MODEL), row(D_MODEL), row(D_MODEL), _const_spec((8, D_MODEL))] + [ANY] * n_g,
        scratch_shapes=_HalfSwap.sems(n_g),
        compiler_params=_params(("arbitrary",)),
    )(dh2, dx1a, x1, x, y, modr, ln1_g, w_out, *swap_fulls)
    return outs[:4], outs[4:]


def _fold_kv(t0, t1):
    lane = lax.broadcasted_iota(jnp.int32, t0.shape, 1)
    f0 = t0 + pltpu.roll(t0, HEAD_DIM, 1)
    f1 = t1 + pltpu.roll(t1, HEAD_DIM, 1)
    return jnp.where(lane < HEAD_DIM, f0, f1)


def _bwd_mix(q, kv, gu, gv, dmix, bias, sinks, gln_g, gln_b, wsm, bsx, amat, aog, gog, grad_parts, grad_kinds):
    s = q.shape[0]
    tile = 2 * BLOCK
    n_steps = s // tile
    n_g = len(grad_parts)

    def body(q_ref, kv_ref, kvp_ref, gu_ref, gv_ref, dmix_ref, bias_ref, sinks_ref, glng_ref, glnb_ref, wsm_ref,
             bsx_ref, amat_ref, aog_ref, gog_ref, *rest):
        part_refs = rest[:n_g]
        dq_ref, dkv_ref, dgu_ref, dgv_ref, gbias_ref, dws_ref, dbs_ref, vec_ref, dsink_ref = rest[n_g:n_g + 9]
        rx_refs = rest[n_g + 9:2 * n_g + 9]
        carry, done, send_sems, recv_sems = rest[2 * n_g + 9:]
        n = pl.program_id(0)
        exchange = _ChipExchange(part_refs, rx_refs, grad_kinds, send_sems, recv_sems)

        @pl.when(n == 0)
        def _():
            exchange.start()
            carry[...] = jnp.zeros_like(carry)
            done[...] = jnp.zeros_like(done)
            gbias_ref[...] = jnp.zeros_like(gbias_ref)
            dws_ref[...] = jnp.zeros_like(dws_ref)
            dbs_ref[...] = jnp.zeros_like(dbs_ref)
            vec_ref[...] = jnp.zeros_like(vec_ref)
            dsink_ref[...] = jnp.zeros_like(dsink_ref)

        @pl.when(n == n_steps)
        def _():
            dkv_ref[:BLOCK, :] = done[...].astype(BF16)
            dkv_ref[BLOCK:, :] = carry[...].astype(BF16)
            exchange.wait()

        @pl.when(n < n_steps)
        def _():
            col = lax.broadcasted_iota(jnp.int32, (BLOCK, 2 * BLOCK), 1)
            lane = lax.broadcasted_iota(jnp.int32, (BLOCK, LANES), 1)
            low = lane < HEAD_DIM
            rows = [slice(0, BLOCK), slice(BLOCK, tile)]
            kv_blocks = [kvp_ref[...], kv_ref[rows[0], :], kv_ref[rows[1], :]]
            masks = [(col < BLOCK) & (n == 0), None]
            q_blks = [q_ref[r, :] for r in rows]
            fwd = []
            for b in range(2):
                kk = jnp.concatenate([kv_blocks[b][:, :KV_W], kv_blocks[b + 1][:, :KV_W]], axis=0)
                vv = jnp.concatenate([kv_blocks[b][:, KV_W:], kv_blocks[b + 1][:, KV_W:]], axis=0)
                fwd.append(_attn_block_fwd(q_blks[b], kk, vv, bias_ref, sinks_ref, masks[b]))
                fwd.append(_gmlp_chunk_fwd(gu_ref[rows[b], :], gv_ref[rows[b], :], glng_ref[...], glnb_ref[...],
                                           wsm_ref, bsx_ref[...], amat_ref[...]))
            res = _interleave(*fwd[:2]) + _interleave(*fwd[2:])

            def gating_bwd(b, d_gm, saved):
                u, tu, ta, xhat, rstd, vb, mixedv = saved
                dgu_ref[rows[b], :] = (d_gm * mixedv * _gelu_grad(gu_ref[rows[b], :], tu)).astype(BF16)
                dmx = d_gm * u
                dmxb = dmx.astype(BF16)
                yield
                dvn_cols, dws = [], []
                for pair in range(N_GROUPS // 2):
                    dp_ = dmxb[:, pair * LANES:(pair + 1) * LANES]
                    vp = vb[:, pair * LANES:(pair + 1) * LANES]
                    dvn_cols.append(
                        jnp.where(low, _dot_tn(wsm_ref[2 * pair], dp_), _dot_tn(wsm_ref[2 * pair + 1], dp_)))
                    zero = jnp.zeros_like(dp_)
                    dws.append(_dot_nt(jnp.where(low, dp_, zero), vp))
                    dws.append(_dot_nt(jnp.where(low, zero, dp_), vp))
                dvn = jnp.concatenate(dvn_cols, axis=1)
                yield
                dxh = dvn * glng_ref[...]
                am = amat_ref[...]
                m1 = _split_dot(dxh, am)
                m2 = _split_dot(dxh * xhat, am)
                yield
                da = rstd * (dxh - m1 - xhat * m2)
                dgv_ref[rows[b], :] = (da * _gelu_grad(gv_ref[rows[b], :], ta)).astype(BF16)
                return dmx, dws, _colsum(dvn * xhat), _colsum(dvn)

            def attention_bwd(b, d_attn, probs, kvar, vvar):
                heads = range(N_HEADS)
                sels = [low if h % 2 == 0 else jnp.logical_not(low) for h in heads]
                pair_of = lambda a, h: a[:, (h // 2) * LANES:(h // 2 + 1) * LANES]
                do_hs = [jnp.where(sels[h], pair_of(d_attn, h), 0.0).astype(BF16) for h in heads]
                q_hs = [jnp.where(sels[h], pair_of(q_blks[b], h), jnp.zeros((BLOCK, LANES), BF16)) for h in heads]
                dps = [_dot_nt(do_hs[h], vvar[_head_kv(h)[0]][_head_kv(h)[1]]) for h in heads]
                yield
                deltas = [jnp.sum(probs[h][0] * dps[h], axis=-1, keepdims=True) for h in heads]
                yield
                dss = [probs[h][0] * (dps[h] - deltas[h]) for h in heads]
                dsinks = [-(probs[h][1] * deltas[h]) for h in heads]
                dsbs = [ds.astype(BF16) for ds in dss]
                pbs = [probs[h][0].astype(BF16) for h in heads]
                yield
                dqs = [_dot(dsbs[h], kvar[_head_kv(h)[0]][_head_kv(h)[1]]) for h in heads]
                tks = [_dot_tn(dsbs[h], q_hs[h]) for h in heads]
                tvs = [_dot_tn(pbs[h], do_hs[h]) for h in heads]
                dq_cols = [dqs[2 * i] + dqs[2 * i + 1] for i in range(N_HEADS // 2)]
                dq_ref[rows[b], :] = (jnp.concatenate(dq_cols, axis=1) * Q_SCALE).astype(BF16)
                per_kv = N_HEADS // N_KV
                kv_sum = lambda ts, kvh: sum(ts[kvh * per_kv + 1:(kvh + 1) * per_kv], ts[kvh * per_kv])
                dkk = _fold_kv(kv_sum(tks, 0), kv_sum(tks, 1))
                dvv = _fold_kv(kv_sum(tvs, 0), kv_sum(tvs, 1))
                return jnp.concatenate([dkk, dvv], axis=1), dss, dsinks

            bwd, rms_g = [], []
            for b in range(2):
                attn, probs, kvar, vvar = res[2 * b]
                gm, saved = res[2 * b + 1]
                na_unit, r_a = _rms(attn, 1.0)
                ng_unit, r_g = _rms(gm, 1.0)
                dmix = dmix_ref[rows[b], :].astype(F32)
                dn_a = dmix[:, :ATTN_W]
                dn_g = dmix[:, ATTN_W:]
                rms_g.append((_colsum(dn_a * na_unit), _colsum(dn_g * ng_unit)))
                t_a = dn_a * aog_ref[...]
                d_attn = r_a * t_a - na_unit * (r_a * jnp.mean(t_a * na_unit, axis=-1, keepdims=True))
                t_g = dn_g * gog_ref[...]
                d_gm = r_g * t_g - ng_unit * (r_g * jnp.mean(t_g * ng_unit, axis=-1, keepdims=True))
                bwd.append(attention_bwd(b, d_attn, probs, kvar, vvar))
                bwd.append(gating_bwd(b, d_gm, saved))
            (dkv_a, dss_a, dsk_a), (dmx_a, dws_a, glg_a, glb_a) = _interleave(*bwd[:2])
            (dkv_b, dss_b, dsk_b), (dmx_b, dws_b, glg_b, glb_b) = _interleave(*bwd[2:])

            vec_ref[0:1, :] += rms_g[0][0] + rms_g[1][0]
            vec_ref[1:2, :] += rms_g[0][1] + rms_g[1][1]
            vec_ref[2:3, :] += glg_a + glg_b
            vec_ref[3:4, :] += glb_a + glb_b
            dbs_ref[...] += dmx_a + dmx_b
            for g in range(N_GROUPS):
                dws_ref[g] += dws_a[g] + dws_b[g]
            for h in range(N_HEADS):
                gbias_ref[h] += dss_a[h] + dss_b[h]
                dsink_ref[h] += dsk_a[h] + dsk_b[h]

            dkv_ref[:BLOCK, :] = done[...].astype(BF16)
            dkv_ref[BLOCK:, :] = (carry[...] + dkv_a[:BLOCK]).astype(BF16)
            done[...] = dkv_a[BLOCK:] + dkv_b[:BLOCK]
            carry[...] = dkv_b[BLOCK:]

    last = n_steps - 1
    cur = lambda w: pl.BlockSpec((tile, w), lambda n: (jnp.minimum(n, last), 0))
    late = lambda w: pl.BlockSpec((tile, w), lambda n: (jnp.clip(n - 1, 0, last), 0))
    before = pl.BlockSpec((BLOCK, 2 * KV_W), lambda n: (jnp.clip(2 * n - 1, 0, 2 * last + 1), 0))
    outs = pl.pallas_call(
        body, name="bwd_mix",
        out_shape=[jax.ShapeDtypeStruct((s, ATTN_W), BF16), jax.ShapeDtypeStruct((s, 2 * KV_W), BF16),
                   jax.ShapeDtypeStruct((s, GMLP_W), BF16), jax.ShapeDtypeStruct((s, GMLP_W), BF16),
                   jax.ShapeDtypeStruct((N_HEADS, BLOCK, 2 * BLOCK), F32),
                   jax.ShapeDtypeStruct((N_GROUPS, BLOCK, BLOCK), F32),
                   jax.ShapeDtypeStruct((BLOCK, GMLP_W), F32), jax.ShapeDtypeStruct((8, GMLP_W), F32),
                   jax.ShapeDtypeStruct((N_HEADS, BLOCK, 1), F32)]
        + [jax.ShapeDtypeStruct(_rx_shape(p.shape, k), BF16) for p, k in zip(grad_parts, grad_kinds)],
        grid=(n_steps + 1,),
        in_specs=[cur(ATTN_W), cur(2 * KV_W), before, cur(GMLP_W), cur(GMLP_W), cur(D_MODEL),
                  _const_spec((N_HEADS, BLOCK, 2 * BLOCK)), pl.BlockSpec(memory_space=pltpu.SMEM),
                  _const_spec((1, GMLP_W)), _const_spec((1, GMLP_W)), _const_spec((N_GROUPS, BLOCK, BLOCK)),
                  _const_spec((BLOCK, GMLP_W)), _const_spec((GMLP_W, GMLP_W)), _const_spec((1, ATTN_W)),
                  _const_spec((1, GMLP_W))] + [ANY] * n_g,
        out_specs=[cur(ATTN_W), late(2 * KV_W), cur(GMLP_W), cur(GMLP_W),
                   _const_spec((N_HEADS, BLOCK, 2 * BLOCK)), _const_spec((N_GROUPS, BLOCK, BLOCK)),
                   _const_spec((BLOCK, GMLP_W)), _const_spec((8, GMLP_W)), _const_spec((N_HEADS, BLOCK, 1))]
        + [ANY] * n_g,
        scratch_shapes=[pltpu.VMEM((BLOCK, 2 * KV_W), F32), pltpu.VMEM((BLOCK, 2 * KV_W), F32)]
        + _ChipExchange.sems(n_g),
        compiler_params=_params(("arbitrary",)),
    )(q, kv, kv, gu, gv, dmix, bias, sinks, gln_g, gln_b, wsm, bsx, amat, aog, gog, *grad_parts)
    return outs[:9], outs[9:]


def _mix_finalize(gbias, bucket, dws, dbs, dsink):
    def body(gb_ref, bucket_ref, dws_ref, dbs_ref, dsink_ref, tall_ref):
        bk = bucket_ref[...]
        lane = lax.broadcasted_iota(jnp.int32, (N_BUCKETS, LANES), 1)
        rowi = lax.broadcasted_iota(jnp.int32, (N_BUCKETS, LANES), 0)
        drb = jnp.zeros((N_BUCKETS, LANES), F32)
        dsk = jnp.zeros((8, LANES), F32)
        lane8 = lax.broadcasted_iota(jnp.int32, (8, LANES), 1)
        for h in range(N_HEADS):
            g = gb_ref[h]
            for b in range(N_BUCKETS):
                tot = jnp.sum(_colsum(jnp.where(bk == float(b), g, 0.0)), axis=1, keepdims=True)
                drb = jnp.where((rowi == h) & (lane == b), tot, drb)
            sk = jnp.sum(dsink_ref[h], axis=0, keepdims=True)
            dsk = jnp.where(lane8 == h, sk, dsk)
        tall_ref[TALL_RB:TALL_RB + N_BUCKETS, :] = drb
        tall_ref[TALL_SK:TALL_SK + 8, :] = dsk
        ti = lax.broadcasted_iota(jnp.int32, (BLOCK, BLOCK), 0)
        ui = lax.broadcasted_iota(jnp.int32, (BLOCK, BLOCK), 1)
        for g in range(N_GROUPS):
            tall_ref[g * BLOCK:(g + 1) * BLOCK, :] = jnp.where(ti >= ui, dws_ref[g], 0.0)
        gi = lax.broadcasted_iota(jnp.int32, (GMLP_W, LANES), 0) // GROUP_DIM
        li = lax.broadcasted_iota(jnp.int32, (GMLP_W, LANES), 1)
        ind = jnp.where(gi == li, 1.0, 0.0).astype(BF16)
        d = dbs_ref[...]
        hi = d.astype(BF16)
        r1 = d - hi.astype(F32)
        mid = r1.astype(BF16)
        lo = (r1 - mid.astype(F32)).astype(BF16)
        dbsg = _dot(hi, ind) + _dot(mid, ind) + _dot(lo, ind)
        tall_ref[TALL_BS:TALL_BS + N_GROUPS, :] = dbsg.T[:N_GROUPS, :]

    return pl.pallas_call(
        body, name="mix_finalize", out_shape=jax.ShapeDtypeStruct((TALL_ROWS, LANES), F32), grid=(1,),
        in_specs=[_const_spec((N_HEADS, BLOCK, 2 * BLOCK)), _const_spec((BLOCK, 2 * BLOCK)),
                  _const_spec((N_GROUPS, BLOCK, BLOCK)), _const_spec((BLOCK, GMLP_W)),
                  _const_spec((N_HEADS, BLOCK, 1))],
        out_specs=_const_spec((TALL_ROWS, LANES)),
        compiler_params=_params(("arbitrary",)),
    )(gbias, bucket, dws, dbs, dsink)


def _bwd_in(dq, dkv, dgu, dgv, dxa, x, modr, w_in, tm):
    s = x.shape[0]

    def body(dq_ref, dkv_ref, dgu_ref, dgv_ref, dxa_ref, x_ref, mod_ref, w_ref, gx_ref, acc_ref, db_ref):
        @pl.when(pl.program_id(0) == 0)
        def _():
            acc_ref[...] = jnp.zeros_like(acc_ref)
            db_ref[...] = jnp.zeros_like(db_ref)

        dproj = jnp.concatenate([dq_ref[...], dkv_ref[...], dgu_ref[...], dgv_ref[...]], axis=1)
        wb = IN_W // N_CHIPS
        dh1 = sum([_dot_nt(dproj[:, j * wb:(j + 1) * wb], w_ref[j]) for j in range(1, N_CHIPS)],
                  _dot_nt(dproj[:, :wb], w_ref[0]))
        gx_ref[...] = dxa_ref[...].astype(F32) + dh1 * (1.0 + mod_ref[1:2, :])
        acc_ref[0:1, :] += _colsum(dh1 * x_ref[...].astype(F32))
        acc_ref[1:2, :] += _colsum(dh1)
        db_ref[0:1, :] += _colsum(dproj.astype(F32))

    row = lambda w: pl.BlockSpec((tm, w), lambda i: (i, 0))
    return pl.pallas_call(
        body, name="bwd_in",
        out_shape=(jax.ShapeDtypeStruct((s, D_MODEL), F32), jax.ShapeDtypeStruct((8, D_MODEL), F32),
                   jax.ShapeDtypeStruct((8, IN_W), F32)),
        grid=(s // tm,),
        in_specs=[row(ATTN_W), row(2 * KV_W), row(GMLP_W), row(GMLP_W), row(D_MODEL), row(D_MODEL),
                  _const_spec((8, D_MODEL)), _const_spec(w_in.shape)],
        out_specs=(row(D_MODEL), _const_spec((8, D_MODEL)), _const_spec((8, IN_W))),
        compiler_params=_params(("arbitrary",)),
    )(dq, dkv, dgu, dgv, dxa, x, modr, w_in)


def _wgrad(a, bs, tm, tk, name, owner_blocks=False, gather_vs=()):
    k_all, m = a.shape
    n = sum(b.shape[1] for b in bs)
    nk = k_all // tk
    nm = m // tm
    n_b = len(bs)
    n_v = len(gather_vs)
    wb = n // N_CHIPS

    def body(a_ref, *rest):
        b_refs, v_refs = rest[:n_b], rest[n_b:n_b + n_v]
        o_ref, ob_ref = rest[n_b + n_v:n_b + n_v + 2]
        vg_refs = rest[n_b + n_v + 2:n_b + 2 * n_v + 2]
        i, k = pl.program_id(0), pl.program_id(1)
        if n_v:
            gather = _Gather8(v_refs, vg_refs, *rest[n_b + 2 * n_v + 2:])

            @pl.when((i == 0) & (k == 0))
            def _():
                gather.start()

            @pl.when((i == nm - 1) & (k == 0))
            def _():
                gather.forward()

        @pl.when(k == 0)
        def _():
            o_ref[...] = jnp.zeros_like(o_ref)

        b = b_refs[0][...] if n_b == 1 else jnp.concatenate([r[...] for r in b_refs], axis=1)
        if owner_blocks:
            av = a_ref[...]
            for j in range(N_CHIPS):
                o_ref[j] += _dot_tn(av, b[:, j * wb:(j + 1) * wb])
        else:
            o_ref[...] += _dot_tn(a_ref[...], b)

        @pl.when(k == nk - 1)
        def _():
            ob_ref[...] = o_ref[...].astype(BF16)

        if n_v:
            @pl.when((i == nm - 1) & (k == nk - 1))
            def _():
                gather.finish()

    if owner_blocks:
        out_spec = pl.BlockSpec((N_CHIPS, tm, wb), lambda i, k: (0, i, 0))
        shape = (N_CHIPS, m, wb)
    else:
        out_spec = pl.BlockSpec((tm, n), lambda i, k: (i, 0))
        shape = (m, n)
    outs = pl.pallas_call(
        body, name=name,
        out_shape=[jax.ShapeDtypeStruct(shape, F32), jax.ShapeDtypeStruct(shape, BF16)] + _gathered8_shapes(gather_vs),
        grid=(nm, nk),
        in_specs=[pl.BlockSpec((tk, tm), lambda i, k: (k, i))]
        + [pl.BlockSpec((tk, b.shape[1]), lambda i, k: (k, 0)) for b in bs] + [ANY] * n_v,
        out_specs=[out_spec, out_spec] + [ANY] * n_v,
        scratch_shapes=_Gather8.sems(n_v) if n_v else [],
        compiler_params=_params(("arbitrary", "arbitrary") if n_v else ("parallel", "arbitrary")),
    )(a, *bs, *gather_vs)
    return outs[0], outs[1], outs[2:]


def _adam_math(w, g, m, v):
    m2 = ADAM_B1 * m + (1.0 - ADAM_B1) * g
    v2 = ADAM_B2 * v + (1.0 - ADAM_B2) * (g * g)
    m_hat = m2 / (1.0 - ADAM_B1 ** ADAM_STEP)
    v_hat = v2 / (1.0 - ADAM_B2 ** ADAM_STEP)
    delta = -ADAM_LR * (m_hat / (jnp.sqrt(v_hat) + ADAM_EPS) + ADAM_WD * w)
    return delta, m2, v2


def _transposed(g):
    r, c = g.shape
    pieces = []
    for lo in range(0, c, LANES):
        width = min(LANES, c - lo)
        piece = g[:, lo:lo + width]
        if width < LANES:
            piece = jnp.concatenate([piece, jnp.zeros((r, LANES - width), g.dtype)], axis=1)
        pieces.append(piece.T[:width])
    return jnp.concatenate(pieces, axis=0)


def _adam_halves(w, mine, got, m, v, tr, name, transposed=False):
    r, cc = w.shape[::-1] if transposed else w.shape
    h = r // 2
    nt = h // tr

    def body(c_ref, w_ref, mine_ref, got_ref, m_ref, v_ref, g_ref, d_ref, m2_ref, v2_ref):
        g = jnp.where(pl.program_id(0) == c_ref[0], mine_ref[...], got_ref[...])
        if transposed:
            g = _transposed(g)
        g_ref[...] = g
        d, m2, v2 = _adam_math(w_ref[...], g, m_ref[...], v_ref[...])
        d_ref[...] = d
        m2_ref[...] = m2
        v2_ref[...] = v2

    if transposed:
        full = pl.BlockSpec((cc, tr), lambda hh, i, c_ref: (0, hh * nt + i))
    else:
        full = pl.BlockSpec((tr, cc), lambda hh, i, c_ref: (hh * nt + i, 0))
    half = pl.BlockSpec((tr, cc), lambda hh, i, c_ref: (i, 0))
    shp = jax.ShapeDtypeStruct(w.shape, F32)
    return pl.pallas_call(
        body, name=name, out_shape=(shp, shp, shp, shp),
        grid_spec=pltpu.PrefetchScalarGridSpec(
            num_scalar_prefetch=1, grid=(2, nt), in_specs=[full, half, half, full, full],
            out_specs=(full, full, full, full)),
        compiler_params=_params(("arbitrary", "arbitrary")),
    )(_core_index_scalar(), w, mine, got, m, v)


def _adam_w_ada(sc_t, dmod_all, w, m, v, tr):
    r, cc = w.shape

    def body(chip_ref, sct_ref, dm_ref, w_ref, m_ref, v_ref, g_ref, d_ref, m2_ref, v2_ref):
        g = sct_ref[:, 0:1] * dm_ref[0:1, :]
        for k in range(1, N_DEV):
            g = g + sct_ref[:, k:k + 1] * dm_ref[k:k + 1, :]
        g_ref[...] = g
        d, m2, v2 = _adam_math(w_ref[...], g, m_ref[...], v_ref[...])
        d_ref[...] = d
        m2_ref[...] = m2
        v2_ref[...] = v2

    spec = pl.BlockSpec((tr, cc), lambda i, chip_ref: (i, 0))
    shp = jax.ShapeDtypeStruct((r, cc), F32)
    return pl.pallas_call(
        body, name="adam_w_ada", out_shape=(shp, shp, shp, shp),
        grid_spec=pltpu.PrefetchScalarGridSpec(
            num_scalar_prefetch=1, grid=(r // tr,),
            in_specs=[pl.BlockSpec((tr, N_DEV), lambda i, chip_ref: (i, 0)),
                      pl.BlockSpec((N_DEV, cc), lambda i, chip_ref: (0, chip_ref[0])), spec, spec, spec],
            out_specs=(spec, spec, spec, spec)),
        compiler_params=_params(("parallel",)),
    )(_chip_index_scalar(), sc_t, dmod_all, w, m, v)


def _pack_wide(acc_i, acc_m, acc_f, db_in, vec):
    arrs = [acc_i, acc_m, acc_f, db_in, vec]
    i_, m_, f_, b_, v_ = range(5)
    src = {"b_in": (b_, 0), "ln1_g": (m_, 2), "ln1_b": (m_, 3), "ln2_g": (f_, 1), "ln2_b": (f_, 2),
           "gmlp_ln_g": (v_, 2), "gmlp_ln_b": (v_, 3), "attn_out_g": (v_, 0), "gmlp_out_g": (v_, 1), "loss": (f_, 0)}
    dmod = [(i_, 1), (i_, 0), (m_, 4), (m_, 1), (m_, 0), (f_, 3)]

    def body(*refs):
        ins, wide_ref = refs[:5], refs[5]
        wide_ref[...] = jnp.zeros_like(wide_ref)
        for k, (a, row) in enumerate(dmod):
            wide_ref[0:1, k * D_MODEL:(k + 1) * D_MODEL] = ins[a][row:row + 1, :]
        for name, (a, row) in src.items():
            r, off, n = WIDE_LAYOUT[name]
            wide_ref[r:r + 1, off:off + n] = ins[a][row:row + 1, :]

    return pl.pallas_call(
        body, name="pack_wide", out_shape=jax.ShapeDtypeStruct((8, WIDE_W), F32), grid=(1,),
        in_specs=[_const_spec(a.shape) for a in arrs], out_specs=_const_spec((8, WIDE_W)),
        compiler_params=_params(("arbitrary",)),
    )(*arrs)


def _adam_small(gw, gt, wide_wmv, w_s, b_s, rel_bias, sinks, after):
    names = list(WIDE_PARAMS)
    tall = [("gmlp_w_s", w_s), ("gmlp_b_s", b_s), ("rel_bias", rel_bias), ("attn_sinks", sinks)]
    ins = [gw, gt]
    for n in names:
        ins += list(wide_wmv[n])
    for _, t in tall:
        ins += list(t)
    n_in = len(ins)

    def body(*refs):
        gw_ref, gt_ref = refs[0], refs[1]
        wmv = refs[2:n_in]
        dmod_ref, loss_ref, loss1_ref = refs[n_in + 1:n_in + 4]
        outs = refs[n_in + 4:]

        def tall_sum(r0, nr):
            g = gt_ref[r0:r0 + nr, :]
            for d in range(1, N_DEV):
                g = g + gt_ref[d * TALL_ROWS + r0:d * TALL_ROWS + r0 + nr, :]
            return g

        def emit(k, g, w_ref, m_ref, v_ref):
            d, m2, v2 = _adam_math(w_ref[...], g, m_ref[...], v_ref[...])
            outs[4 * k][...] = g
            outs[4 * k + 1][...] = d
            outs[4 * k + 2][...] = m2
            outs[4 * k + 3][...] = v2

        gsum = gw_ref[0:8, :]
        for d in range(1, N_DEV):
            gsum = gsum + gw_ref[8 * d:8 * d + 8, :]
        for d in range(N_DEV):
            dmod_ref[d:d + 1, :] = gw_ref[8 * d:8 * d + 1, :]
        for k, n in enumerate(names):
            r, off, sz = WIDE_LAYOUT[n]
            emit(k, gsum[r:r + 1, off:off + sz], *wmv[3 * k:3 * k + 3])
        r, off, sz = WIDE_LAYOUT["loss"]
        tot = jnp.sum(gsum[r:r + 1, off:off + sz], axis=1, keepdims=True)
        loss_ref[...] = jnp.broadcast_to(tot * (0.5 / D_MODEL), loss_ref.shape)
        loss1_ref[...] = tot * (0.5 / D_MODEL)

        k0 = len(names)
        ws_refs = wmv[3 * k0:3 * k0 + 3]
        for g in range(N_GROUPS):
            rows = slice(g * BLOCK, (g + 1) * BLOCK)
            gg = tall_sum(g * BLOCK, BLOCK)
            d, m2, v2 = _adam_math(ws_refs[0][rows, :], gg, ws_refs[1][rows, :], ws_refs[2][rows, :])
            outs[4 * k0][rows, :] = gg
            outs[4 * k0 + 1][rows, :] = d
            outs[4 * k0 + 2][rows, :] = m2
            outs[4 * k0 + 3][rows, :] = v2
        emit(k0 + 1, tall_sum(TALL_BS, N_GROUPS), *wmv[3 * (k0 + 1):3 * (k0 + 1) + 3])
        emit(k0 + 2, tall_sum(TALL_RB, N_HEADS)[:, :N_BUCKETS], *wmv[3 * (k0 + 2):3 * (k0 + 2) + 3])
        emit(k0 + 3, tall_sum(TALL_SK, 8)[0:1, :N_HEADS], *wmv[3 * (k0 + 3):3 * (k0 + 3) + 3])

    out_shapes = [jax.ShapeDtypeStruct((N_DEV, WIDE_W), F32), jax.ShapeDtypeStruct((8, LANES), F32),
                  jax.ShapeDtypeStruct((1, 1), F32)]
    for n in names:
        out_shapes += [jax.ShapeDtypeStruct(wide_wmv[n][0].shape, F32)] * 4
    for _, t in tall:
        out_shapes += [jax.ShapeDtypeStruct(t[0].shape, F32)] * 4
    res = pl.pallas_call(
        body, name="adam_small", out_shape=out_shapes, grid=(1,),
        in_specs=[_const_spec(a.shape) for a in ins] + [ANY], out_specs=[_const_spec(o.shape) for o in out_shapes],
        compiler_params=_params(("arbitrary",)),
    )(*ins, after)
    out = {}
    for k, n in enumerate(names + [t[0] for t in tall]):
        out[n] = tuple(res[3 + 4 * k:7 + 4 * k])
    return res[0], res[1], res[2], out


def kernel(x, c, rel_bias, w_ada, b_ada, w_in, b_in, attn_sinks, gmlp_ln_g, gmlp_ln_b, gmlp_w_s, gmlp_b_s, attn_out_g, gmlp_out_g, w_out, ln1_g, ln1_b, w_gate_up, w_down, ln2_g, ln2_b, loss_target, m_rel_bias, m_w_ada, m_b_ada, m_w_in, m_b_in, m_attn_sinks, m_gmlp_ln_g, m_gmlp_ln_b, m_gmlp_w_s, m_gmlp_b_s, m_attn_out_g, m_gmlp_out_g, m_w_out, m_ln1_g, m_ln1_b, m_w_gate_up, m_w_down, m_ln2_g, m_ln2_b, v_rel_bias, v_w_ada, v_b_ada, v_w_in, v_b_in, v_attn_sinks, v_gmlp_ln_g, v_gmlp_ln_b, v_gmlp_w_s, v_gmlp_b_s, v_attn_out_g, v_gmlp_out_g, v_w_out, v_ln1_g, v_ln1_b, v_w_gate_up, v_w_down, v_ln2_g, v_ln2_b):
    ix, iy, ic = _my_pos()
    chip = 2 * ix + iy
    dev = 4 * ix + 2 * iy + ic
    s = x.shape[1]
    xs = x[0]
    tgt = loss_target[0]
    tm_big = min(512, s)
    tm_ffn = min(FFN_SUB, s)
    n_ada = w_ada.shape[2]

    w_in_s, w_out_s = w_in[0].astype(BF16), w_out[0].astype(BF16)
    w_gu_s, w_dn_s = w_gate_up[0].astype(BF16), w_down[0].astype(BF16)
    sc_all, mod_rows, (w_in_g, w_out_g) = _prologue(
        jnp.pad(c, ((0, 7), (0, 0))), w_ada[0], lax.dynamic_slice_in_dim(b_ada, chip * n_ada, n_ada, axis=1),
        [w_in_s, w_out_s])
    mod_all = mod_rows.reshape(N_DEV, N_DEV, -1)
    mod_row = lax.dynamic_index_in_dim(mod_all[0::2], dev, axis=1, keepdims=False)
    modr = jnp.pad(mod_row.reshape(6, D_MODEL), ((0, 2), (0, 0)))
    w_in_f = _insert_own(w_in_g, w_in_s, "blk", chip)

    bucket = _bucket_table()
    bias, wsm = _prep_tables(bucket, rel_bias.T, gmlp_w_s[0])
    bsx = jnp.repeat(gmlp_b_s[0].T, GROUP_DIM, axis=1)
    amat = _group_mean_matrix()
    sinks = attn_sinks[0]

    (h1, q, kv, gu, gv, xb), (w_dn_g,) = _fwd_in(xs, modr, w_in_f, b_in, tm_big, [w_dn_s], ["blk"])
    w_out_f = _insert_own(w_out_g, w_out_s, "blk", chip).reshape(D_MODEL, D_MODEL)
    (x1, x1b, y, mixed), (w_gu_g,) = _fwd_mix(
        q, kv, gu, gv, xs, modr, bias, sinks, gmlp_ln_g, gmlp_ln_b, wsm, bsx, amat, attn_out_g, gmlp_out_g, w_out_f,
        ln1_g, ln1_b, tm_big, [w_gu_s], ["blk"])
    assert w_gate_up.shape[2] == FF_CHUNK
    w_gu_f = _insert_own(w_gu_g, w_gu_s, "blk", chip)
    w_dn_f = _insert_own(w_dn_g, w_dn_s, "blk", chip).reshape(D_FF, D_MODEL)
    h2, act, dy2, dx1a, acc_f = _fwd_ffn(x1, tgt, modr, ln2_g, ln2_b, w_gu_f, w_dn_f, tm_ffn)

    a_act, dgu_ff, dh2 = _bwd_ffn(dy2, act, w_gu_f, w_dn_f, min(FFN_SUB, s))
    g_dn, g_dn_b, _ = _wgrad(a_act, [dy2], D_FF // 2, min(1024, s), "wgrad_down")
    g_gu, g_gu_b, _ = _wgrad(h2, [dgu_ff], 512, min(512, s), "wgrad_gate_up")
    blk3 = lambda a, rows: a.reshape(N_CHIPS, rows, a.shape[1])
    (dxa, dy, dmix, acc_m), (got_dn, got_gu) = _bwd_mid(
        dh2, dx1a, x1b, xs, y, modr, ln1_g, w_out_f, tm_big, [blk3(g_dn_b, D_FF // N_CHIPS), g_gu_b], ["blk", "cols"])
    g_out, g_out_b, _ = _wgrad(mixed, [dy], 512, min(2048, s), "wgrad_out")
    (got_out,) = _swap_halves([blk3(g_out_b, D_MODEL // N_CHIPS)], ["blk"], "rs_swap_out")
    kinds_a = ["blk", "cols", "blk"]
    fulls_a = [blk3(g_dn, D_FF // N_CHIPS), g_gu, blk3(g_out, D_MODEL // N_CHIPS)]
    gots_a = [got_dn, got_gu, got_out]
    parts_a = [_add_halves(f, g, k, "rs_add_a%d" % i) for i, (f, g, k) in enumerate(zip(fulls_a, gots_a, kinds_a))]
    (dq, dkv, dgu, dgv, gbias, dws, dbs, vec, dsink), rxs_a = _bwd_mix(
        q, kv, gu, gv, dmix, bias, sinks, gmlp_ln_g, gmlp_ln_b, wsm, bsx, amat, attn_out_g, gmlp_out_g,
        [p[1] for p in parts_a], kinds_a)
    tall_g = _mix_finalize(gbias, bucket, dws, dbs, dsink)
    grad_x, acc_i, db_in = _bwd_in(dq, dkv, dgu, dgv, dxa, xb, modr, w_in_f, tm_big)

    wide_g = _pack_wide(acc_i, acc_m, acc_f, db_in, vec)
    full_in, full_in_b, (gw, gt) = _wgrad(h1, [dq, dkv, dgu, dgv], 512, min(1024, s), "wgrad_in", owner_blocks=True,
                                          gather_vs=[wide_g, tall_g])
    (got_in,) = _swap_halves([full_in_b], ["blk"], "rs_swap_in")
    part_in = _add_halves(full_in, got_in, "blk", "rs_add_in")
    in_send, in_recv, in_part, in_rx, token = _exchange_start(part_in[1], "rs_chips_in_start")
    wide_wmv ={"b_ada": (b_ada, m_b_ada, v_b_ada), "b_in": (b_in, m_b_in, v_b_in),
                "ln1_g": (ln1_g, m_ln1_g, v_ln1_g), "ln1_b": (ln1_b, m_ln1_b, v_ln1_b),
                "ln2_g": (ln2_g, m_ln2_g, v_ln2_g), "ln2_b": (ln2_b, m_ln2_b, v_ln2_b),
                "gmlp_ln_g": (gmlp_ln_g, m_gmlp_ln_g, v_gmlp_ln_g), "gmlp_ln_b": (gmlp_ln_b, m_gmlp_ln_b, v_gmlp_ln_b),
                "attn_out_g": (attn_out_g, m_attn_out_g, v_attn_out_g),
                "gmlp_out_g": (gmlp_out_g, m_gmlp_out_g, v_gmlp_out_g)}
    rows2 = lambda a: a.reshape(-1, a.shape[-1])
    dmod_all, loss_t, loss1, small = _adam_small(
        gw, gt, wide_wmv, tuple(rows2(a) for a in (gmlp_w_s, m_gmlp_w_s, v_gmlp_w_s)),
        tuple(rows2(a) for a in (gmlp_b_s, m_gmlp_b_s, v_gmlp_b_s)), (rel_bias.T, m_rel_bias.T, v_rel_bias.T),
        (attn_sinks, m_attn_sinks, v_attn_sinks), token)
    small["rel_bias"] = tuple(a.T for a in small["rel_bias"])
    loss = loss1.reshape(())

    g_ada, d_ada, m_ada, v_ada = _adam_w_ada(sc_all.T, dmod_all, w_ada[0], m_w_ada[0], v_w_ada[0], 256)

    sums = [(parts_a[0][0], rxs_a[0], 176), (parts_a[1][0], rxs_a[1], 256), (parts_a[2][0], rxs_a[2], 128)]
    mine = [_sum_chips(p, rx, tr, "rs_sum_%d" % i, loss_t) for i, (p, rx, tr) in enumerate(sums)]
    got = _share_halves(mine, "rs_share")
    gs_dn, d_dn, m_dn, v_dn = _adam_halves(w_down[0], mine[0], got[0], m_w_down[0], v_w_down[0], 176, "adam_w_down")
    gs_gu, d_gu, m_gu, v_gu = _adam_halves(w_gate_up[0], mine[1], got[1], m_w_gate_up[0], v_w_gate_up[0], 256,
                                           "adam_w_gate_up")
    gs_out, d_out, m_out, v_out = _adam_halves(w_out[0], mine[2], got[2], m_w_out[0], v_w_out[0], 128, "adam_w_out")

    rx_in = _exchange_wait(in_send, in_recv, in_part, in_rx, d_gu, "rs_chips_in_wait")
    mine_in = _sum_chips(part_in[0], rx_in, 256, "rs_sum_in", rx_in)
    (got_in_half,) = _share_halves([mine_in], "rs_share_in")
    in_t = _adam_halves(w_in[0].T, mine_in, got_in_half, m_w_in[0].T, v_w_in[0].T, 256, "adam_w_in", transposed=True)
    gs_in, d_in, m_in, v_in = (a.T for a in in_t)

    big = {"w_ada": (g_ada, d_ada, m_ada, v_ada), "w_in": (gs_in, d_in, m_in, v_in), "w_out": (gs_out, d_out, m_out, v_out),
           "w_gate_up": (gs_gu, d_gu, m_gu, v_gu), "w_down": (gs_dn, d_dn, m_dn, v_dn)}
    order = ["rel_bias", "w_ada", "b_ada", "w_in", "b_in", "attn_sinks", "gmlp_ln_g", "gmlp_ln_b", "gmlp_w_s", "gmlp_b_s",
             "attn_out_g", "gmlp_out_g", "w_out", "ln1_g", "ln1_b", "w_gate_up", "w_down", "ln2_g", "ln2_b"]
    shapes = {"gmlp_w_s": gmlp_w_s.shape, "gmlp_b_s": gmlp_b_s.shape}
    outs = [loss, grad_x[None]]
    for k in range(4):
        for name in order:
            if name in big:
                outs.append(big[name][k][None])
            elif name in shapes:
                outs.append(small[name][k].reshape(shapes[name]))
            else:
                outs.append(small[name][k])
    return tuple(outs)
```

```python
import math

import numpy as np
import jax
import jax.numpy as jnp
from jax import lax
from jax.experimental import pallas as pl
from jax.experimental.pallas import tpu as pltpu

F32 = jnp.float32
BF16 = jnp.bfloat16
MESH = pl.DeviceIdType.MESH

D_MODEL = 1024
N_HEADS = 8
N_KV = 2
HEAD_DIM = 64
ATTN_W = N_HEADS * HEAD_DIM
KV_W = N_KV * HEAD_DIM
N_GROUPS = 8
GROUP_DIM = 64
GMLP_W = N_GROUPS * GROUP_DIM
IN_W = ATTN_W + 2 * KV_W + 2 * GMLP_W
BLOCK = 128
N_BUCKETS = 32
MAX_DISTANCE = 128
D_FF = 2816
ALPHA = 2.0 ** 0.25
LN_EPS = 1e-5
NEG_INF = -1e30
ADAM_LR, ADAM_B1, ADAM_B2, ADAM_EPS, ADAM_WD, ADAM_STEP = 0.001, 0.9, 0.999, 1e-8, 0.01, 10
N_CHIPS = 4
N_DEV = 8
LANES = 128
V7X_VMEM_LIMIT = 56 * 2 ** 20
GELU_C = math.sqrt(2.0 / math.pi)
Q_SCALE = HEAD_DIM ** -0.5
ANY = pl.BlockSpec(memory_space=pl.ANY)

TALL_BS = N_GROUPS * BLOCK
TALL_RB = TALL_BS + 8
TALL_SK = TALL_RB + N_BUCKETS
TALL_ROWS = TALL_SK + 8
WIDE_W = 6 * D_MODEL
WIDE_LAYOUT = {
    "b_ada": (0, 0, 6 * D_MODEL),
    "b_in": (1, 0, IN_W), "ln1_g": (1, IN_W, D_MODEL), "ln1_b": (1, IN_W + D_MODEL, D_MODEL),
    "ln2_g": (1, IN_W + 2 * D_MODEL, D_MODEL), "ln2_b": (1, IN_W + 3 * D_MODEL, D_MODEL),
    "gmlp_ln_g": (2, 0, GMLP_W), "gmlp_ln_b": (2, GMLP_W, GMLP_W), "attn_out_g": (2, 2 * GMLP_W, ATTN_W),
    "gmlp_out_g": (2, 2 * GMLP_W + ATTN_W, GMLP_W), "loss": (2, 3 * GMLP_W + ATTN_W, D_MODEL)}
WIDE_PARAMS = tuple(n for n in WIDE_LAYOUT if n != "loss")


def _params(sem=None):
    return pltpu.CompilerParams(dimension_semantics=sem, vmem_limit_bytes=V7X_VMEM_LIMIT)


def _const_spec(shape, single=False):
    nd = len(shape)
    if single:
        return pl.BlockSpec(shape, lambda *_: (0,) * nd, pipeline_mode=pl.Buffered(1))
    return pl.BlockSpec(shape, lambda *_: (0,) * nd)


def _dot(a, b):
    return jnp.dot(a, b, preferred_element_type=F32)


def _dot_nt(a, b):
    return lax.dot_general(a, b, (((1,), (1,)), ((), ())), preferred_element_type=F32)


def _dot_tn(a, b):
    return lax.dot_general(a, b, (((0,), (0,)), ((), ())), preferred_element_type=F32)


def _gelu(x):
    t = jnp.tanh(GELU_C * (x + 0.044715 * x * x * x))
    return 0.5 * x * (1.0 + t), t


def _gelu_grad(x, t):
    return 0.5 * (1.0 + t) + 0.5 * x * (1.0 - t * t) * GELU_C * (1.0 + 3.0 * 0.044715 * x * x)


def _split_dot(x, a):
    hi = x.astype(BF16)
    lo = (x - hi.astype(F32)).astype(BF16)
    return _dot(hi, a) + _dot(lo, a)


def _group_mean_matrix():
    g = np.arange(GMLP_W) // GROUP_DIM
    return jnp.asarray((g[:, None] == g[None, :]).astype(np.float32) / GROUP_DIM, dtype=BF16)


def _ln_stats(z):
    mu = jnp.mean(z, axis=-1, keepdims=True)
    d = z - mu
    var = jnp.mean(d * d, axis=-1, keepdims=True)
    rstd = lax.rsqrt(var + LN_EPS)
    return d * rstd, rstd


def _ln_bwd(dxhat, xhat, rstd):
    m1 = jnp.mean(dxhat, axis=-1, keepdims=True)
    m2 = jnp.mean(dxhat * xhat, axis=-1, keepdims=True)
    return rstd * (dxhat - m1 - xhat * m2)


def _colsum(x):
    return jnp.sum(x, axis=0, keepdims=True)


def _my_pos():
    return lax.axis_index("x"), lax.axis_index("y"), lax.axis_index("c")


def _other_chips(x, y):
    return [(1 - x, y), (x, 1 - y), (1 - x, 1 - y)]


def _chip_index_scalar():
    ix, iy, _ = _my_pos()
    return jnp.reshape(2 * ix + iy, (1,)).astype(jnp.int32)


def _core_index_scalar():
    return jnp.reshape(lax.axis_index("c"), (1,)).astype(jnp.int32)


class _Gather8:
    def __init__(self, x_refs, out_refs, send_sems, recv_sems, local_sems):
        self.x_refs, self.out_refs = x_refs, out_refs
        self.send_sems, self.recv_sems, self.local_sems = send_sems, recv_sems, local_sems
        self.x, self.y, self.c = _my_pos()
        self.me, self.sibling = (self.x, self.y, self.c), (self.x, self.y, 1 - self.c)
        self.chips = _other_chips(self.x, self.y)

    def _rows(self, a, px, py, pc):
        m_per = self.x_refs[a].shape[0]
        return self.out_refs[a].at[pl.ds((4 * px + 2 * py + pc) * m_per, m_per), :]

    def _copy(self, a, k, block, to, src=None):
        return pltpu.make_async_remote_copy(
            src_ref=self._rows(a, *block) if src is None else src, dst_ref=self._rows(a, *block),
            send_sem=self.send_sems.at[7 * a + k], recv_sem=self.recv_sems.at[7 * a + k], device_id=to,
            device_id_type=MESH)

    def _local(self, a):
        return pltpu.make_async_copy(self.x_refs[a], self._rows(a, *self.me), self.local_sems.at[a])

    def start(self):
        for a in range(len(self.x_refs)):
            self._local(a).start()
            self._copy(a, 0, self.me, self.sibling, src=self.x_refs[a]).start()
            for j, chip in enumerate(self.chips):
                self._copy(a, 1 + j, self.me, (*chip, self.c), src=self.x_refs[a]).start()

    def forward(self):
        for a in range(len(self.x_refs)):
            for j, chip in enumerate(self.chips):
                self._copy(a, 1 + j, (*chip, self.c), self.me).wait_recv()
                self._copy(a, 4 + j, (*chip, self.c), self.sibling).start()

    def finish(self):
        for a in range(len(self.x_refs)):
            self._copy(a, 0, self.sibling, self.me).wait_recv()
            for j, chip in enumerate(self.chips):
                self._copy(a, 4 + j, (*chip, 1 - self.c), self.me).wait_recv()
        for a in range(len(self.x_refs)):
            for k in range(7):
                self._copy(a, k, self.me, self.me).wait_send()
            self._local(a).wait()

    @staticmethod
    def sems(n_v):
        return [pltpu.SemaphoreType.DMA((7 * n_v,)), pltpu.SemaphoreType.DMA((7 * n_v,)),
                pltpu.SemaphoreType.DMA((n_v,))]


def _gathered8_shapes(vs):
    return [jax.ShapeDtypeStruct((N_DEV * v.shape[0], v.shape[1]), v.dtype) for v in vs]


VMEM_WHOLE = pl.BlockSpec(memory_space=pltpu.VMEM)


def _prologue(c_pad, w_ada_s, b_ada, shards):
    n = w_ada_s.shape[1]
    n_w = len(shards)
    assert N_CHIPS * n == 6 * D_MODEL and n % LANES == 0

    def body(c_ref, w_ref, b_ref, *rest):
        shard_refs = rest[:n_w]
        sc_ref, modc_ref, modg_ref, modr_ref = rest[n_w:n_w + 4]
        gathered_refs = rest[n_w + 4:2 * n_w + 4]
        call_ref, w_vmem = rest[2 * n_w + 4:2 * n_w + 6]
        sems = rest[2 * n_w + 6:]
        ix, iy, ic = _my_pos()
        chip = 2 * ix + iy
        weights = _WeightGather(shard_refs, gathered_refs, ["blk"] * n_w, sems[0], sems[1])
        gather_c = _Gather8([c_ref], [call_ref], sems[2], sems[3], sems[4])
        gather_mod = _Gather8([modc_ref], [modg_ref], sems[5], sems[6], sems[7])
        load_w = pltpu.make_async_copy(w_ref, w_vmem, sems[8])
        weights.start()
        gather_c.start()
        load_w.start()
        gather_c.forward()
        gather_c.finish()
        cv = call_ref[...]
        sc = cv * _sigmoid(cv)
        a_hi = sc.astype(BF16)
        a_lo = (sc - a_hi.astype(F32)).astype(BF16)
        load_w.wait()
        w = w_vmem[...]
        w_hi = w.astype(BF16)
        w_lo = (w - w_hi.astype(F32)).astype(BF16)
        b = b_ref[:, 0:n]
        for k in range(1, N_CHIPS):
            b = jnp.where(chip == k, b_ref[:, k * n:(k + 1) * n], b)
        mod = _dot(a_hi, w_hi) + _dot(a_hi, w_lo) + _dot(a_lo, w_hi) + b
        for d in range(N_DEV):
            sc_ref[d:d + 1, :] = sc[8 * d:8 * d + 1, :]
            modc_ref[d:d + 1, :] = mod[8 * d:8 * d + 1, :]
        gather_mod.start()
        weights.forward()
        gather_mod.forward()
        gather_mod.finish()
        dev = 2 * chip + ic
        mine = jnp.concatenate([modg_ref[pl.ds(2 * 8 * k + dev, 1), :] for k in range(N_CHIPS)], axis=1)
        modr_ref[...] = jnp.zeros_like(modr_ref)
        for r in range(6):
            modr_ref[r:r + 1, :] = mine[:, r * D_MODEL:(r + 1) * D_MODEL]
        weights.forward_diagonal()
        weights.finish()

    outs = pl.pallas_call(
        body, name="prologue",
        out_shape=[jax.ShapeDtypeStruct((N_DEV, D_MODEL), F32), jax.ShapeDtypeStruct((N_DEV, n), F32),
                   jax.ShapeDtypeStruct((N_DEV * N_DEV, n), F32), jax.ShapeDtypeStruct((8, D_MODEL), F32)]
        + [jax.ShapeDtypeStruct(_gathered_shape(sh, "blk"), BF16) for sh in shards],
        in_specs=[VMEM_WHOLE, ANY, VMEM_WHOLE] + [ANY] * n_w,
        out_specs=[VMEM_WHOLE, VMEM_WHOLE, VMEM_WHOLE, VMEM_WHOLE] + [ANY] * n_w,
        scratch_shapes=[pltpu.VMEM((N_DEV * 8, D_MODEL), F32), pltpu.VMEM(w_ada_s.shape, F32)]
        + _WeightGather.sems(n_w) + _Gather8.sems(1) + _Gather8.sems(1) + [pltpu.SemaphoreType.DMA],
        compiler_params=pltpu.CompilerParams(vmem_limit_bytes=V7X_VMEM_LIMIT),
    )(c_pad, w_ada_s, b_ada, *shards)
    return outs[0], outs[3], outs[4:]


def _gathered_shape(shard, kind):
    r, cc = shard.shape
    return (N_CHIPS, r, cc) if kind == "blk" else (r, N_CHIPS * cc)


class _WeightGather:
    N_SEM = 8

    def __init__(self, shards, gathered, kinds, send_sems, recv_sems):
        self.shards, self.gathered, self.kinds = shards, gathered, kinds
        self.send_sems, self.recv_sems = send_sems, recv_sems
        self.x, self.y, self.c = _my_pos()
        self.me, self.sibling = (self.x, self.y, self.c), (self.x, self.y, 1 - self.c)
        self.nbr = ((1 - self.x, self.y), (self.x, 1 - self.y))
        self.diag = 2 * (1 - self.x) + (1 - self.y)

    def _dst(self, a, chip, pc, quarter=None):
        r, cc = self.shards[a].shape
        h = r // 2
        row0, rows = pc * h, h
        if quarter is not None:
            row0, rows = pc * h + quarter * (h // 2), h // 2
        g = self.gathered[a]
        if self.kinds[a] == "blk":
            return g.at[chip, pl.ds(row0, rows), :]
        return g.at[pl.ds(row0, rows), pl.ds(chip * cc, cc)]

    def _copy(self, a, k, region, to, src=None):
        return pltpu.make_async_remote_copy(
            src_ref=region if src is None else src, dst_ref=region, send_sem=self.send_sems.at[a * self.N_SEM + k],
            recv_sem=self.recv_sems.at[a * self.N_SEM + k], device_id=to, device_id_type=MESH)

    def _arrays(self):
        return range(len(self.shards))

    def start(self):
        my_chip = 2 * self.x + self.y
        for a in self._arrays():
            h = self.shards[a].shape[0] // 2
            mine = self.shards[a].at[pl.ds(self.c * h, h), :]
            for j, chip in enumerate(self.nbr):
                self._copy(a, j, self._dst(a, my_chip, self.c), (*chip, self.c), src=mine).start()

    def forward(self):
        for a in self._arrays():
            for j, chip in enumerate(self.nbr):
                cj = 2 * chip[0] + chip[1]
                half = self._dst(a, cj, self.c)
                self._copy(a, j, half, self.me).wait_recv()
                self._copy(a, 2 + j, half, self.sibling).start()
                other = self.nbr[1 - j]
                self._copy(a, 4 + j, self._dst(a, cj, self.c, quarter=j), (*other, self.c)).start()

    def forward_diagonal(self):
        for a in self._arrays():
            for j in range(2):
                quarter = self._dst(a, self.diag, self.c, quarter=j)
                self._copy(a, 4 + j, quarter, self.me).wait_recv()
                self._copy(a, 6 + j, quarter, self.sibling).start()

    def finish(self):
        for a in self._arrays():
            for j, chip in enumerate(self.nbr):
                self._copy(a, 2 + j, self._dst(a, 2 * chip[0] + chip[1], 1 - self.c), self.me).wait_recv()
                self._copy(a, 6 + j, self._dst(a, self.diag, 1 - self.c, quarter=j), self.me).wait_recv()
        for a in self._arrays():
            half = self._dst(a, self.diag, self.c)
            quarter = self._dst(a, self.diag, self.c, quarter=0)
            for k in range(self.N_SEM):
                self._copy(a, k, half if k < 4 else quarter, self.me).wait_send()

    @classmethod
    def sems(cls, n_arr):
        return [pltpu.SemaphoreType.DMA((n_arr * cls.N_SEM,)), pltpu.SemaphoreType.DMA((n_arr * cls.N_SEM,))]


def _insert_own(gathered, shard, kind, chip):
    if kind == "blk":
        return lax.dynamic_update_slice(gathered, shard[None], (chip, 0, 0))
    return lax.dynamic_update_slice(gathered, shard, (0, chip * shard.shape[1]))


def _half_of_full(ref, kind, pc):
    if kind == "blk":
        h = ref.shape[1] // 2
        return ref.at[:, pl.ds(pc * h, h), :]
    h = ref.shape[0] // 2
    return ref.at[pl.ds(pc * h, h), :]


def _half_shape(shape, kind):
    return (shape[0], shape[1] // 2, shape[2]) if kind == "blk" else (shape[0] // 2, shape[1])


class _HalfSwap:
    def __init__(self, ins, outs, kinds, send_sems, recv_sems):
        self.ins, self.outs, self.kinds = ins, outs, kinds
        self.send_sems, self.recv_sems = send_sems, recv_sems
        self.x, self.y, self.c = _my_pos()

    def _copies(self):
        for a in range(len(self.ins)):
            yield pltpu.make_async_remote_copy(
                src_ref=_half_of_full(self.ins[a], self.kinds[a], 1 - self.c), dst_ref=self.outs[a],
                send_sem=self.send_sems.at[a], recv_sem=self.recv_sems.at[a],
                device_id=(self.x, self.y, 1 - self.c), device_id_type=MESH)

    def start(self):
        for cp in self._copies():
            cp.start()

    def wait(self):
        for cp in self._copies():
            cp.wait()

    @staticmethod
    def sems(n_arr):
        return [pltpu.SemaphoreType.DMA((n_arr,)), pltpu.SemaphoreType.DMA((n_arr,))]

    @staticmethod
    def out_shapes(fulls, kinds):
        return [jax.ShapeDtypeStruct(_half_shape(a.shape, k), a.dtype) for a, k in zip(fulls, kinds)]


def _swap_halves(fulls_bf16, kinds, name):
    n_arr = len(fulls_bf16)

    def body(*refs):
        swap = _HalfSwap(refs[:n_arr], refs[n_arr:2 * n_arr], kinds, *refs[2 * n_arr:])
        swap.start()
        swap.wait()

    return pl.pallas_call(
        body, name=name, out_shape=_HalfSwap.out_shapes(fulls_bf16, kinds),
        in_specs=[ANY] * n_arr, out_specs=[ANY] * n_arr, scratch_shapes=_HalfSwap.sems(n_arr),
    )(*fulls_bf16)


def _add_halves(full, got, kind, name):
    hs = _half_shape(full.shape, kind)

    def body(pos_ref, a_ref, b_ref, o_ref, ob_ref):
        p = a_ref[...] + b_ref[...].astype(F32)
        ob_ref[...] = p.astype(BF16)

        @pl.when(pl.program_id(0) == pos_ref[1])
        def _():
            o_ref[...] = p.reshape(o_ref.shape)

    if kind == "blk":
        nb, h, cc = hs
        own = pl.BlockSpec((1, h, cc), lambda b, pos_ref: (b, pos_ref[0], 0))
        other = pl.BlockSpec((1, h, cc), lambda b, pos_ref: (b, 0, 0))
    else:
        h, cc = hs[0], hs[1] // N_CHIPS
        own = pl.BlockSpec((h, cc), lambda b, pos_ref: (pos_ref[0], b))
        other = pl.BlockSpec((h, cc), lambda b, pos_ref: (0, b))
    pos = jnp.concatenate([_core_index_scalar(), _chip_index_scalar()])
    return pl.pallas_call(
        body, name=name, out_shape=(jax.ShapeDtypeStruct((h, cc), F32), jax.ShapeDtypeStruct(hs, BF16)),
        grid_spec=pltpu.PrefetchScalarGridSpec(
            num_scalar_prefetch=1, grid=(N_CHIPS,), in_specs=[own, other],
            out_specs=(pl.BlockSpec((h, cc), lambda b, pos_ref: (0, 0)), other)),
        compiler_params=_params(("arbitrary",)),
    )(pos, full, got)


def _rx_shape(part_shape, kind):
    if kind == "blk":
        return (3, part_shape[1], part_shape[2])
    return (3, part_shape[0], part_shape[1] // N_CHIPS)


class _ChipExchange:
    def __init__(self, parts, rxs, kinds, send_sems, recv_sems):
        self.parts, self.rxs, self.kinds = parts, rxs, kinds
        self.send_sems, self.recv_sems = send_sems, recv_sems
        self.x, self.y, self.c = _my_pos()
        self.chips = _other_chips(self.x, self.y)

    def _copies(self):
        for a in range(len(self.parts)):
            for j, chip in enumerate(self.chips):
                cj = 2 * chip[0] + chip[1]
                if self.kinds[a] == "blk":
                    src = self.parts[a].at[cj]
                else:
                    cc = self.parts[a].shape[1] // N_CHIPS
                    src = self.parts[a].at[:, pl.ds(cj * cc, cc)]
                yield pltpu.make_async_remote_copy(
                    src_ref=src, dst_ref=self.rxs[a].at[j], send_sem=self.send_sems.at[a * 3 + j],
                    recv_sem=self.recv_sems.at[a * 3 + j], device_id=(*chip, self.c), device_id_type=MESH)

    def start(self):
        for cp in self._copies():
            cp.start()

    def wait(self):
        for cp in self._copies():
            cp.wait_recv()
        for cp in self._copies():
            cp.wait_send()

    @staticmethod
    def sems(n_arr):
        return [pltpu.SemaphoreType.DMA((n_arr * 3,)), pltpu.SemaphoreType.DMA((n_arr * 3,))]


HBM_SPEC = pl.BlockSpec(memory_space=pltpu.HBM)
SEM_SPEC = pl.BlockSpec(memory_space=pltpu.SEMAPHORE)
DATAFLOW = pltpu.SideEffectType.DATAFLOW_SIDE_EFFECTING


def _exchange_start(part, name):
    rx_shape = _rx_shape(part.shape, "blk")

    def body(part_ref, rx_ref, send_sems, recv_sems, part_thru, rx_thru, token):
        _ChipExchange([part_ref], [rx_ref], ["blk"], send_sems, recv_sems).start()
        token[...] = jnp.zeros_like(token)

    return pl.pallas_call(
        body, name=name,
        out_shape=(pltpu.SemaphoreType.DMA((3,)), pltpu.SemaphoreType.DMA((3,)), pltpu.HBM(part.shape, part.dtype),
                   pltpu.HBM(rx_shape, BF16), jax.ShapeDtypeStruct((8, LANES), F32)),
        in_specs=(HBM_SPEC, HBM_SPEC), out_specs=(SEM_SPEC, SEM_SPEC, HBM_SPEC, HBM_SPEC, VMEM_WHOLE),
        input_output_aliases={0: 2, 1: 3}, compiler_params=pltpu.CompilerParams(has_side_effects=DATAFLOW),
    )(pltpu.with_memory_space_constraint(part, pltpu.HBM),
      pltpu.with_memory_space_constraint(lax.empty(rx_shape, BF16), pltpu.HBM))


def _exchange_wait(send_sems, recv_sems, part_thru, rx_thru, after, name):
    def body(part_ref, rx_ref, send_sems, recv_sems, after_ref, part_dead, rx_out):
        _ChipExchange([part_ref], [rx_ref], ["blk"], send_sems, recv_sems).wait()

    return pl.pallas_call(
        body, name=name,
        out_shape=(pltpu.HBM(part_thru.shape, part_thru.dtype), pltpu.HBM(rx_thru.shape, rx_thru.dtype)),
        in_specs=(HBM_SPEC, HBM_SPEC, SEM_SPEC, SEM_SPEC, ANY), out_specs=(HBM_SPEC, HBM_SPEC),
        input_output_aliases={0: 0, 1: 1}, compiler_params=pltpu.CompilerParams(has_side_effects=DATAFLOW),
    )(part_thru, rx_thru, send_sems, recv_sems, after)[1]


def _sum_chips(part, rx, tr, name, after):
    _, h, cc = rx.shape
    flips = (2, 1, 3)

    def body(chip_ref, p_ref, rx_ref, after_ref, o_ref):
        own = p_ref[...]
        for mc in range(N_CHIPS):
            @pl.when(chip_ref[0] == mc)
            def _():
                terms = sorted([(mc, None)] + [(mc ^ f, j) for j, f in enumerate(flips)])
                acc = None
                for _, j in terms:
                    t = own if j is None else rx_ref[j].astype(F32)
                    acc = t if acc is None else acc + t
                o_ref[...] = acc

    return pl.pallas_call(
        body, name=name, out_shape=jax.ShapeDtypeStruct((h, cc), F32),
        grid_spec=pltpu.PrefetchScalarGridSpec(
            num_scalar_prefetch=1, grid=(h // tr,),
            in_specs=[pl.BlockSpec((tr, cc), lambda i, chip_ref: (i, 0)),
                      pl.BlockSpec((3, tr, cc), lambda i, chip_ref: (0, i, 0)), ANY],
            out_specs=pl.BlockSpec((tr, cc), lambda i, chip_ref: (i, 0))),
        compiler_params=_params(("arbitrary",)),
    )(_chip_index_scalar(), part, rx, after)


def _share_halves(halves, name):
    n_arr = len(halves)

    def body(*refs):
        ins, outs = refs[:n_arr], refs[n_arr:2 * n_arr]
        send_sems, recv_sems = refs[2 * n_arr:]
        x, y, c = _my_pos()
        cps = []
        for a in range(n_arr):
            cp = pltpu.make_async_remote_copy(
                src_ref=ins[a], dst_ref=outs[a], send_sem=send_sems.at[a], recv_sem=recv_sems.at[a],
                device_id=(x, y, 1 - c), device_id_type=MESH)
            cp.start()
            cps.append(cp)
        for cp in cps:
            cp.wait()

    return pl.pallas_call(
        body, name=name, out_shape=[jax.ShapeDtypeStruct(h.shape, h.dtype) for h in halves],
        in_specs=[ANY] * n_arr, out_specs=[ANY] * n_arr,
        scratch_shapes=[pltpu.SemaphoreType.DMA((n_arr,)), pltpu.SemaphoreType.DMA((n_arr,))],
    )(*halves)


def _bucket_table():
    qi = jnp.arange(BLOCK)[:, None]
    si = jnp.arange(2 * BLOCK)[None, :]
    dist = qi + BLOCK - si
    max_exact = N_BUCKETS // 2
    n = jnp.maximum(dist, 0)
    nf = jnp.maximum(n, max_exact).astype(F32)
    large = max_exact + (jnp.log(nf / max_exact) / math.log(MAX_DISTANCE / max_exact)
                         * (N_BUCKETS - max_exact)).astype(jnp.int32)
    large = jnp.minimum(large, N_BUCKETS - 1)
    return jnp.where(n < max_exact, n, large).astype(F32)


def _prep_tables(bucket, rel_bias_t, w_s):
    def body(bucket_ref, rb_ref, ws_ref, bias_ref, wsm_ref):
        qi = lax.broadcasted_iota(jnp.int32, (BLOCK, 2 * BLOCK), 0)
        si = lax.broadcasted_iota(jnp.int32, (BLOCK, 2 * BLOCK), 1)
        dist = qi + BLOCK - si
        in_window = (dist >= 0) & (dist < BLOCK)
        bk = bucket_ref[...]
        for h in range(N_HEADS):
            acc = jnp.zeros((BLOCK, 2 * BLOCK), F32)
            for b in range(N_BUCKETS):
                acc = jnp.where(bk == float(b), rb_ref[h, b], acc)
            bias_ref[h] = jnp.where(in_window, acc, NEG_INF)
        ti = lax.broadcasted_iota(jnp.int32, (BLOCK, BLOCK), 0)
        ui = lax.broadcasted_iota(jnp.int32, (BLOCK, BLOCK), 1)
        for g in range(N_GROUPS):
            wsm_ref[g] = jnp.where(ti >= ui, ws_ref[g], 0.0).astype(BF16)

    return pl.pallas_call(
        body, name="prep_tables",
        out_shape=(jax.ShapeDtypeStruct((N_HEADS, BLOCK, 2 * BLOCK), F32),
                   jax.ShapeDtypeStruct((N_GROUPS, BLOCK, BLOCK), BF16)),
        grid=(1,),
        in_specs=[_const_spec((BLOCK, 2 * BLOCK)), pl.BlockSpec(memory_space=pltpu.SMEM),
                  _const_spec((N_GROUPS, BLOCK, BLOCK))],
        out_specs=(_const_spec((N_HEADS, BLOCK, 2 * BLOCK)), _const_spec((N_GROUPS, BLOCK, BLOCK))),
        compiler_params=_params(("arbitrary",)),
    )(bucket, rel_bias_t, w_s)


def _fwd_in(x, modr, w_in, b_in, tm, shards, kinds):
    s = x.shape[0]
    n_steps = s // tm
    fwd_step, diag_step = (8 * n_steps) // 16, (13 * n_steps) // 16
    n_w = len(shards)

    def body(x_ref, mod_ref, w_ref, b_ref, *rest):
        shard_refs = rest[:n_w]
        h1_ref, q_ref, kv_ref, gu_ref, gv_ref, xb_ref = rest[n_w:n_w + 6]
        gathered_refs = rest[n_w + 6:2 * n_w + 6]
        send_sems, recv_sems = rest[2 * n_w + 6:]
        i = pl.program_id(0)
        gather = _WeightGather(shard_refs, gathered_refs, kinds, send_sems, recv_sems)

        @pl.when(i == 0)
        def _():
            gather.start()

        xv = x_ref[...]
        xb_ref[...] = xv.astype(BF16)
        h1 = (xv * (1.0 + mod_ref[1:2, :]) + mod_ref[0:1, :]).astype(BF16)
        h1_ref[...] = h1
        proj = _dot_nt(h1, w_ref[...]) + b_ref[...]
        q_ref[...] = (proj[:, :ATTN_W] * Q_SCALE).astype(BF16)
        kv_ref[...] = proj[:, ATTN_W:ATTN_W + 2 * KV_W].astype(BF16)
        gu_ref[...] = proj[:, ATTN_W + 2 * KV_W:ATTN_W + 2 * KV_W + GMLP_W]
        gv_ref[...] = proj[:, ATTN_W + 2 * KV_W + GMLP_W:]

        @pl.when(i == fwd_step)
        def _():
            gather.forward()

        @pl.when(i == diag_step)
        def _():
            gather.forward_diagonal()

        @pl.when(i == n_steps - 1)
        def _():
            gather.finish()

    row = lambda w: pl.BlockSpec((tm, w), lambda i: (i, 0))
    outs = pl.pallas_call(
        body, name="fwd_in",
        out_shape=[jax.ShapeDtypeStruct((s, D_MODEL), BF16), jax.ShapeDtypeStruct((s, ATTN_W), BF16),
                   jax.ShapeDtypeStruct((s, 2 * KV_W), BF16), jax.ShapeDtypeStruct((s, GMLP_W), F32),
                   jax.ShapeDtypeStruct((s, GMLP_W), F32), jax.ShapeDtypeStruct((s, D_MODEL), BF16)]
        + [jax.ShapeDtypeStruct(_gathered_shape(sh, k), BF16) for sh, k in zip(shards, kinds)],
        grid=(n_steps,),
        in_specs=[row(D_MODEL), _const_spec((8, D_MODEL)), _const_spec(w_in.shape), _const_spec((1, IN_W))]
        + [ANY] * n_w,
        out_specs=[row(D_MODEL), row(ATTN_W), row(2 * KV_W), row(GMLP_W), row(GMLP_W), row(D_MODEL)] + [ANY] * n_w,
        scratch_shapes=_WeightGather.sems(n_w),
        compiler_params=_params(("arbitrary",)),
    )(x, modr, w_in, b_in, *shards)
    return outs[:6], outs[6:]


def _kv_variants(kk):
    kf = kk.astype(F32)
    lane = lax.broadcasted_iota(jnp.int32, kf.shape, 1)
    low = lane < HEAD_DIM
    k0_lo = jnp.where(low, kf, 0.0)
    k1_hi = jnp.where(low, 0.0, kf)
    k0_hi = pltpu.roll(k0_lo, HEAD_DIM, 1)
    k1_lo = pltpu.roll(k1_hi, HEAD_DIM, 1)
    return ((k0_lo.astype(BF16), k0_hi.astype(BF16)), (k1_lo.astype(BF16), k1_hi.astype(BF16)))


def _head_kv(h):
    return h // (N_HEADS // N_KV), h % 2


MIX_GROUP = 2


def _interleave(*gens):
    results = [None] * len(gens)
    active = list(enumerate(gens))
    while active:
        still = []
        for i, g in active:
            try:
                next(g)
                still.append((i, g))
            except StopIteration as done:
                results[i] = done.value
        active = still
    return results


def _attn_block_fwd(q_blk, kk, vv, bias_ref, sinks_ref, first_mask):
    kvar = _kv_variants(kk)
    vvar = _kv_variants(vv)
    heads = range(N_HEADS)
    q_pairs = [q_blk[:, (h // 2) * LANES:(h // 2 + 1) * LANES] for h in heads]
    logits = [_dot_nt(q_pairs[h], kvar[_head_kv(h)[0]][_head_kv(h)[1]]) + bias_ref[h] for h in heads]
    if first_mask is not None:
        logits = [jnp.where(first_mask, NEG_INF, lg) for lg in logits]
    yield
    ms = [jnp.maximum(jnp.max(logits[h], axis=-1, keepdims=True), sinks_ref[h]) for h in heads]
    yield
    es = [jnp.exp(logits[h] - ms[h]) for h in heads]
    ess = [jnp.exp(sinks_ref[h] - ms[h]) for h in heads]
    yield
    invs = [1.0 / (jnp.sum(es[h], axis=-1, keepdims=True) + ess[h]) for h in heads]
    probs = [(es[h] * invs[h], ess[h] * invs[h]) for h in heads]
    yield
    outs = [_dot(probs[h][0].astype(BF16), vvar[_head_kv(h)[0]][_head_kv(h)[1]]) for h in heads]
    pairs = [outs[2 * i] + outs[2 * i + 1] for i in range(N_HEADS // 2)]
    return jnp.concatenate(pairs, axis=1), probs, kvar, vvar


def _gmlp_chunk_fwd(gu, gv, ln_g, ln_b, wsm_ref, bsx, amat):
    u, tu = _gelu(gu)
    a, ta = _gelu(gv)
    yield
    mean = _split_dot(a, amat)
    d = a - mean
    yield
    var = _split_dot(d * d, amat)
    yield
    rstd = lax.rsqrt(var + LN_EPS)
    xhat = d * rstd
    vb = (xhat * ln_g + ln_b).astype(BF16)
    yield
    lane = lax.broadcasted_iota(jnp.int32, (BLOCK, LANES), 1)
    low = lane < GROUP_DIM
    cols = []
    for pair in range(N_GROUPS // 2):
        vp = vb[:, pair * LANES:(pair + 1) * LANES]
        cols.append(jnp.where(low, _dot(wsm_ref[2 * pair], vp), _dot(wsm_ref[2 * pair + 1], vp)))
    mixedv = jnp.concatenate(cols, axis=1) + bsx
    return u * mixedv, (u, tu, ta, xhat, rstd, vb, mixedv)


def _rms(a, g):
    r = lax.rsqrt(jnp.mean(a * a, axis=-1, keepdims=True) + LN_EPS)
    return a * r * g, r


def _fwd_mix(q, kv, gu, gv, x, modr, bias, sinks, gln_g, gln_b, wsm, bsx, amat, aog, gog, w_out, ln1_g, ln1_b, tm,
             ffn_shards, ffn_kinds):
    s = x.shape[0]
    nb = tm // BLOCK
    n_steps = s // tm
    fwd_step, diag_step = (7 * n_steps) // 16, (12 * n_steps) // 16
    n_w = len(ffn_shards)

    def body(q_ref, kv_ref, kvp_ref, gu_ref, gv_ref, x_ref, mod_ref, bias_ref, sinks_ref, glng_ref, glnb_ref, wsm_ref,
             bsx_ref, amat_ref, aog_ref, gog_ref, wout_ref, ln1g_ref, ln1b_ref, *rest):
        shard_refs = rest[:n_w]
        x1_ref, x1b_ref, y_ref, mixed_ref = rest[n_w:n_w + 4]
        gathered_refs = rest[n_w + 4:2 * n_w + 4]
        mix_scr, send_sems, recv_sems = rest[2 * n_w + 4:]
        i = pl.program_id(0)
        gather = _WeightGather(shard_refs, gathered_refs, ffn_kinds, send_sems, recv_sems)

        @pl.when(i == 0)
        def _():
            gather.start()

        col = lax.broadcasted_iota(jnp.int32, (BLOCK, 2 * BLOCK), 1)
        for b0 in range(0, nb, MIX_GROUP):
            gens = []
            for b in range(b0, min(b0 + MIX_GROUP, nb)):
                r0 = b * BLOCK
                if b == 0:
                    kvprev = kvp_ref[...]
                    first_mask = (col < BLOCK) & (i == 0)
                else:
                    kvprev = kv_ref[r0 - BLOCK:r0, :]
                    first_mask = None
                kvcur = kv_ref[r0:r0 + BLOCK, :]
                kk = jnp.concatenate([kvprev[:, :KV_W], kvcur[:, :KV_W]], axis=0)
                vv = jnp.concatenate([kvprev[:, KV_W:], kvcur[:, KV_W:]], axis=0)
                gens.append(_attn_block_fwd(q_ref[r0:r0 + BLOCK, :], kk, vv, bias_ref, sinks_ref, first_mask))
                gens.append(_gmlp_chunk_fwd(gu_ref[r0:r0 + BLOCK, :], gv_ref[r0:r0 + BLOCK, :], glng_ref[...],
                                            glnb_ref[...], wsm_ref, bsx_ref[...], amat_ref[...]))
            res = _interleave(*gens)
            for k, b in enumerate(range(b0, min(b0 + MIX_GROUP, nb))):
                r0 = b * BLOCK
                na, _ = _rms(res[2 * k][0], aog_ref[...])
                ng, _ = _rms(res[2 * k + 1][0], gog_ref[...])
                mix_scr[r0:r0 + BLOCK, :ATTN_W] = na.astype(BF16)
                mix_scr[r0:r0 + BLOCK, ATTN_W:] = ng.astype(BF16)
        mixed = mix_scr[...]
        mixed_ref[...] = mixed
        y = _dot(mixed, wout_ref[...])
        y_ref[...] = y.astype(BF16)
        z1 = ALPHA * x_ref[...] + mod_ref[2:3, :] * y
        xhat, _ = _ln_stats(z1)
        x1 = xhat * ln1g_ref[...] + ln1b_ref[...]
        x1_ref[...] = x1
        x1b_ref[...] = x1.astype(BF16)

        @pl.when(i == fwd_step)
        def _():
            gather.forward()

        @pl.when(i == diag_step)
        def _():
            gather.forward_diagonal()

        @pl.when(i == n_steps - 1)
        def _():
            gather.finish()

    row = lambda w: pl.BlockSpec((tm, w), lambda i: (i, 0))
    prev = pl.BlockSpec((BLOCK, 2 * KV_W), lambda i: (jnp.maximum(i * nb - 1, 0), 0))
    outs = pl.pallas_call(
        body, name="fwd_mix",
        out_shape=[jax.ShapeDtypeStruct((s, D_MODEL), F32)] + [jax.ShapeDtypeStruct((s, D_MODEL), BF16)] * 3
        + [jax.ShapeDtypeStruct(_gathered_shape(sh, k), BF16) for sh, k in zip(ffn_shards, ffn_kinds)],
        grid=(n_steps,),
        in_specs=[row(ATTN_W), row(2 * KV_W), prev, row(GMLP_W), row(GMLP_W), row(D_MODEL), _const_spec((8, D_MODEL)),
                  _const_spec((N_HEADS, BLOCK, 2 * BLOCK)), pl.BlockSpec(memory_space=pltpu.SMEM),
                  _const_spec((1, GMLP_W)), _const_spec((1, GMLP_W)), _const_spec((N_GROUPS, BLOCK, BLOCK)),
                  _const_spec((BLOCK, GMLP_W)), _const_spec((GMLP_W, GMLP_W)), _const_spec((1, ATTN_W)),
                  _const_spec((1, GMLP_W)), _const_spec((D_MODEL, D_MODEL)), _const_spec((1, D_MODEL)),
                  _const_spec((1, D_MODEL))] + [ANY] * n_w,
        out_specs=[row(D_MODEL)] * 4 + [ANY] * n_w,
        scratch_shapes=[pltpu.VMEM((tm, D_MODEL), BF16)] + _WeightGather.sems(n_w),
        compiler_params=_params(("arbitrary",)),
    )(q, kv, kv, gu, gv, x, modr, bias, sinks, gln_g, gln_b, wsm, bsx, amat, aog, gog, w_out, ln1_g, ln1_b, *ffn_shards)
    return outs[:4], outs[4:]


FF_BLOCKS = N_CHIPS // 2
FF_CHUNK = D_FF // FF_BLOCKS
FFN_SUB = 256


def _sigmoid(x):
    return 1.0 / (1.0 + jnp.exp(-x))


def _fwd_ffn(x1, target, modr, ln2_g, ln2_b, w_gu, w_dn, tm):
    s = x1.shape[0]

    def body(x1_ref, t_ref, mod_ref, g_ref, b_ref, wgu_ref, wdn_ref, h2_ref, act_ref, dy2_ref, dx1a_ref, acc_ref):
        @pl.when(pl.program_id(0) == 0)
        def _():
            acc_ref[...] = jnp.zeros_like(acc_ref)

        x1v = x1_ref[...]
        h2 = (x1v * (1.0 + mod_ref[4:5, :]) + mod_ref[3:4, :]).astype(BF16)
        h2_ref[...] = h2
        y2 = None
        for cc in range(FF_BLOCKS):
            c0 = cc * FF_CHUNK
            gate = _dot(h2, wgu_ref[cc])
            up = _dot(h2, wgu_ref[FF_BLOCKS + cc])
            act_ref[:, c0:c0 + FF_CHUNK] = gate.astype(BF16)
            act_ref[:, D_FF + c0:D_FF + c0 + FF_CHUNK] = up.astype(BF16)
            a = (gate * _sigmoid(gate) * up).astype(BF16)
            part = _dot(a, wdn_ref[c0:c0 + FF_CHUNK, :])
            y2 = part if y2 is None else y2 + part
        g2 = mod_ref[5:6, :]
        z2 = ALPHA * x1v + g2 * y2
        xhat, rstd = _ln_stats(z2)
        gain = g_ref[...]
        diff = xhat * gain + b_ref[...] - t_ref[...]
        dx2 = diff * (1.0 / D_MODEL)
        dz2 = _ln_bwd(dx2 * gain, xhat, rstd)
        dx1a_ref[...] = ALPHA * dz2
        dy2_ref[...] = (g2 * dz2).astype(BF16)
        acc_ref[0:1, :] += _colsum(diff * diff)
        acc_ref[1:2, :] += _colsum(dx2 * xhat)
        acc_ref[2:3, :] += _colsum(dx2)
        acc_ref[3:4, :] += _colsum(dz2 * y2)

    row = lambda w: pl.BlockSpec((tm, w), lambda i: (i, 0))
    return pl.pallas_call(
        body, name="fwd_ffn",
        out_shape=(jax.ShapeDtypeStruct((s, D_MODEL), BF16), jax.ShapeDtypeStruct((s, 2 * D_FF), BF16),
                   jax.ShapeDtypeStruct((s, D_MODEL), BF16), jax.ShapeDtypeStruct((s, D_MODEL), F32),
                   jax.ShapeDtypeStruct((8, D_MODEL), F32)),
        grid=(s // tm,),
        in_specs=[row(D_MODEL), row(D_MODEL), _const_spec((8, D_MODEL)), _const_spec((1, D_MODEL)),
                  _const_spec((1, D_MODEL)), _const_spec((N_CHIPS, D_MODEL, FF_CHUNK), single=True),
                  _const_spec((D_FF, D_MODEL), single=True)],
        out_specs=(row(D_MODEL), row(2 * D_FF), row(D_MODEL), row(D_MODEL), _const_spec((8, D_MODEL))),
        compiler_params=_params(("arbitrary",)),
    )(x1, target, modr, ln2_g, ln2_b, w_gu, w_dn)


def _bwd_ffn(dy2, act, w_gu, w_dn, tm):
    s = dy2.shape[0]

    def body(dy2_ref, act_ref, wgu_ref, wdn_ref, a_ref, dgu_ref, dh2_ref):
        dy2v = dy2_ref[...]
        dh2 = None
        for cc in range(FF_BLOCKS):
            c0 = cc * FF_CHUNK
            da = _dot_nt(dy2v, wdn_ref[c0:c0 + FF_CHUNK, :])
            gate = act_ref[:, c0:c0 + FF_CHUNK].astype(F32)
            up = act_ref[:, D_FF + c0:D_FF + c0 + FF_CHUNK].astype(F32)
            sg = _sigmoid(gate)
            sl = gate * sg
            a_ref[:, c0:c0 + FF_CHUNK] = (sl * up).astype(BF16)
            dgate = (da * up * (sg * (1.0 + gate * (1.0 - sg)))).astype(BF16)
            dup = (da * sl).astype(BF16)
            dgu_ref[:, c0:c0 + FF_CHUNK] = dgate
            dgu_ref[:, D_FF + c0:D_FF + c0 + FF_CHUNK] = dup
            part = _dot_nt(dgate, wgu_ref[cc]) + _dot_nt(dup, wgu_ref[FF_BLOCKS + cc])
            dh2 = part if dh2 is None else dh2 + part
        dh2_ref[...] = dh2.astype(BF16)

    row = lambda w: pl.BlockSpec((tm, w), lambda i: (i, 0))
    return pl.pallas_call(
        body, name="bwd_ffn",
        out_shape=(jax.ShapeDtypeStruct((s, D_FF), BF16), jax.ShapeDtypeStruct((s, 2 * D_FF), BF16),
                   jax.ShapeDtypeStruct((s, D_MODEL), BF16)),
        grid=(s // tm,),
        in_specs=[row(D_MODEL), row(2 * D_FF), _const_spec((N_CHIPS, D_MODEL, FF_CHUNK), single=True),
                  _const_spec((D_FF, D_MODEL), single=True)],
        out_specs=(row(D_FF), row(2 * D_FF), row(D_MODEL)),
        compiler_params=_params(("parallel",)),
    )(dy2, act, w_gu, w_dn)


def _bwd_mid(dh2, dx1a, x1, x, y, modr, ln1_g, w_out, tm, swap_fulls, swap_kinds):
    s = x.shape[0]
    n_steps = s // tm
    n_g = len(swap_fulls)

    def body(dh2_ref, dx1a_ref, x1_ref, x_ref, y_ref, mod_ref, g_ref, wout_ref, *rest):
        full_refs = rest[:n_g]
        dxa_ref, dy_ref, dmix_ref, acc_ref = rest[n_g:n_g + 4]
        got_refs = rest[n_g + 4:2 * n_g + 4]
        swap = _HalfSwap(full_refs, got_refs, swap_kinds, *rest[2 * n_g + 4:])
        i = pl.program_id(0)

        @pl.when(i == 0)
        def _():
            swap.start()
            acc_ref[...] = jnp.zeros_like(acc_ref)

        dh2 = dh2_ref[...].astype(F32)
        x1v = x1_ref[...].astype(F32)
        yv = y_ref[...].astype(F32)
        g1 = mod_ref[2:3, :]
        dx1 = dx1a_ref[...] + dh2 * (1.0 + mod_ref[4:5, :])
        z1 = ALPHA * x_ref[...] + g1 * yv
        xhat, rstd = _ln_stats(z1)
        dz1 = _ln_bwd(dx1 * g_ref[...], xhat, rstd)
        dxa_ref[...] = (ALPHA * dz1).astype(BF16)
        dy = (g1 * dz1).astype(BF16)
        dy_ref[...] = dy
        dmix_ref[...] = _dot_nt(dy, wout_ref[...]).astype(BF16)
        acc_ref[0:1, :] += _colsum(dh2 * x1v)
        acc_ref[1:2, :] += _colsum(dh2)
        acc_ref[2:3, :] += _colsum(dx1 * xhat)
        acc_ref[3:4, :] += _colsum(dx1)
        acc_ref[4:5, :] += _colsum(dz1 * yv)

        @pl.when(i == n_steps - 1)
        def _():
            swap.wait()

    row = lambda w: pl.BlockSpec((tm, w), lambda i: (i, 0))
    outs = pl.pallas_call(
        body, name="bwd_mid",
        out_shape=[jax.ShapeDtypeStruct((s, D_MODEL), BF16), jax.ShapeDtypeStruct((s, D_MODEL), BF16),
                   jax.ShapeDtypeStruct((s, D_MODEL), BF16), jax.ShapeDtypeStruct((8, D_MODEL), F32)]
        + _HalfSwap.out_shapes(swap_fulls, swap_kinds),
        grid=(n_steps,),
        in_specs=[row(D_MODEL)] * 5 + [_const_spec((8, D_MODEL)), _const_spec((1, D_MODEL)),
                                       _const_spec((D_MODEL, D_MODEL))] + [ANY] * n_g,
        out_specs=[row(D_MODEL), row(D_MODEL), row(D_MODEL), _const_spec((8, D_MODEL))] + [ANY] * n_g,
        scratch_shapes=_HalfSwap.sems(n_g),
        compiler_params=_params(("arbitrary",)),
    )(dh2, dx1a, x1, x, y, modr, ln1_g, w_out, *swap_fulls)
    return outs[:4], outs[4:]


def _fold_kv(t0, t1):
    lane = lax.broadcasted_iota(jnp.int32, t0.shape, 1)
    f0 = t0 + pltpu.roll(t0, HEAD_DIM, 1)
    f1 = t1 + pltpu.roll(t1, HEAD_DIM, 1)
    return jnp.where(lane < HEAD_DIM, f0, f1)


def _bwd_mix(q, kv, gu, gv, dmix, bias, sinks, gln_g, gln_b, wsm, bsx, amat, aog, gog, grad_parts, grad_kinds):
    s = q.shape[0]
    tile = 2 * BLOCK
    n_steps = s // tile
    n_g = len(grad_parts)

    def body(q_ref, kv_ref, kvp_ref, gu_ref, gv_ref, dmix_ref, bias_ref, sinks_ref, glng_ref, glnb_ref, wsm_ref,
             bsx_ref, amat_ref, aog_ref, gog_ref, *rest):
        part_refs = rest[:n_g]
        dq_ref, dkv_ref, dgu_ref, dgv_ref, gbias_ref, dws_ref, dbs_ref, vec_ref, dsink_ref = rest[n_g:n_g + 9]
        rx_refs = rest[n_g + 9:2 * n_g + 9]
        carry, done, send_sems, recv_sems = rest[2 * n_g + 9:]
        n = pl.program_id(0)
        exchange = _ChipExchange(part_refs, rx_refs, grad_kinds, send_sems, recv_sems)

        @pl.when(n == 0)
        def _():
            exchange.start()
            carry[...] = jnp.zeros_like(carry)
            done[...] = jnp.zeros_like(done)
            gbias_ref[...] = jnp.zeros_like(gbias_ref)
            dws_ref[...] = jnp.zeros_like(dws_ref)
            dbs_ref[...] = jnp.zeros_like(dbs_ref)
            vec_ref[...] = jnp.zeros_like(vec_ref)
            dsink_ref[...] = jnp.zeros_like(dsink_ref)

        @pl.when(n == n_steps)
        def _():
            dkv_ref[:BLOCK, :] = done[...].astype(BF16)
            dkv_ref[BLOCK:, :] = carry[...].astype(BF16)
            exchange.wait()

        @pl.when(n < n_steps)
        def _():
            col = lax.broadcasted_iota(jnp.int32, (BLOCK, 2 * BLOCK), 1)
            lane = lax.broadcasted_iota(jnp.int32, (BLOCK, LANES), 1)
            low = lane < HEAD_DIM
            rows = [slice(0, BLOCK), slice(BLOCK, tile)]
            kv_blocks = [kvp_ref[...], kv_ref[rows[0], :], kv_ref[rows[1], :]]
            masks = [(col < BLOCK) & (n == 0), None]
            q_blks = [q_ref[r, :] for r in rows]
            fwd = []
            for b in range(2):
                kk = jnp.concatenate([kv_blocks[b][:, :KV_W], kv_blocks[b + 1][:, :KV_W]], axis=0)
                vv = jnp.concatenate([kv_blocks[b][:, KV_W:], kv_blocks[b + 1][:, KV_W:]], axis=0)
                fwd.append(_attn_block_fwd(q_blks[b], kk, vv, bias_ref, sinks_ref, masks[b]))
                fwd.append(_gmlp_chunk_fwd(gu_ref[rows[b], :], gv_ref[rows[b], :], glng_ref[...], glnb_ref[...],
                                           wsm_ref, bsx_ref[...], amat_ref[...]))
            res = _interleave(*fwd[:2]) + _interleave(*fwd[2:])

            def gating_bwd(b, d_gm, saved):
                u, tu, ta, xhat, rstd, vb, mixedv = saved
                dgu_ref[rows[b], :] = (d_gm * mixedv * _gelu_grad(gu_ref[rows[b], :], tu)).astype(BF16)
                dmx = d_gm * u
                dmxb = dmx.astype(BF16)
                yield
                dvn_cols, dws = [], []
                for pair in range(N_GROUPS // 2):
                    dp_ = dmxb[:, pair * LANES:(pair + 1) * LANES]
                    vp = vb[:, pair * LANES:(pair + 1) * LANES]
                    dvn_cols.append(
                        jnp.where(low, _dot_tn(wsm_ref[2 * pair], dp_), _dot_tn(wsm_ref[2 * pair + 1], dp_)))
                    zero = jnp.zeros_like(dp_)
                    dws.append(_dot_nt(jnp.where(low, dp_, zero), vp))
                    dws.append(_dot_nt(jnp.where(low, zero, dp_), vp))
                dvn = jnp.concatenate(dvn_cols, axis=1)
                yield
                dxh = dvn * glng_ref[...]
                am = amat_ref[...]
                m1 = _split_dot(dxh, am)
                m2 = _split_dot(dxh * xhat, am)
                yield
                da = rstd * (dxh - m1 - xhat * m2)
                dgv_ref[rows[b], :] = (da * _gelu_grad(gv_ref[rows[b], :], ta)).astype(BF16)
                return dmx, dws, _colsum(dvn * xhat), _colsum(dvn)

            def attention_bwd(b, d_attn, probs, kvar, vvar):
                heads = range(N_HEADS)
                sels = [low if h % 2 == 0 else jnp.logical_not(low) for h in heads]
                pair_of = lambda a, h: a[:, (h // 2) * LANES:(h // 2 + 1) * LANES]
                do_hs = [jnp.where(sels[h], pair_of(d_attn, h), 0.0).astype(BF16) for h in heads]
                q_hs = [jnp.where(sels[h], pair_of(q_blks[b], h), jnp.zeros((BLOCK, LANES), BF16)) for h in heads]
                dps = [_dot_nt(do_hs[h], vvar[_head_kv(h)[0]][_head_kv(h)[1]]) for h in heads]
                yield
                deltas = [jnp.sum(probs[h][0] * dps[h], axis=-1, keepdims=True) for h in heads]
                yield
                dss = [probs[h][0] * (dps[h] - deltas[h]) for h in heads]
                dsinks = [-(probs[h][1] * deltas[h]) for h in heads]
                dsbs = [ds.astype(BF16) for ds in dss]
                pbs = [probs[h][0].astype(BF16) for h in heads]
                yield
                dqs = [_dot(dsbs[h], kvar[_head_kv(h)[0]][_head_kv(h)[1]]) for h in heads]
                tks = [_dot_tn(dsbs[h], q_hs[h]) for h in heads]
                tvs = [_dot_tn(pbs[h], do_hs[h]) for h in heads]
                dq_cols = [dqs[2 * i] + dqs[2 * i + 1] for i in range(N_HEADS // 2)]
                dq_ref[rows[b], :] = (jnp.concatenate(dq_cols, axis=1) * Q_SCALE).astype(BF16)
                per_kv = N_HEADS // N_KV
                kv_sum = lambda ts, kvh: sum(ts[kvh * per_kv + 1:(kvh + 1) * per_kv], ts[kvh * per_kv])
                dkk = _fold_kv(kv_sum(tks, 0), kv_sum(tks, 1))
                dvv = _fold_kv(kv_sum(tvs, 0), kv_sum(tvs, 1))
                return jnp.concatenate([dkk, dvv], axis=1), dss, dsinks

            bwd, rms_g = [], []
            for b in range(2):
                attn, probs, kvar, vvar = res[2 * b]
                gm, saved = res[2 * b + 1]
                na_unit, r_a = _rms(attn, 1.0)
                ng_unit, r_g = _rms(gm, 1.0)
                dmix = dmix_ref[rows[b], :].astype(F32)
                dn_a = dmix[:, :ATTN_W]
                dn_g = dmix[:, ATTN_W:]
                rms_g.append((_colsum(dn_a * na_unit), _colsum(dn_g * ng_unit)))
                t_a = dn_a * aog_ref[...]
                d_attn = r_a * t_a - na_unit * (r_a * jnp.mean(t_a * na_unit, axis=-1, keepdims=True))
                t_g = dn_g * gog_ref[...]
                d_gm = r_g * t_g - ng_unit * (r_g * jnp.mean(t_g * ng_unit, axis=-1, keepdims=True))
                bwd.append(attention_bwd(b, d_attn, probs, kvar, vvar))
                bwd.append(gating_bwd(b, d_gm, saved))
            (dkv_a, dss_a, dsk_a), (dmx_a, dws_a, glg_a, glb_a) = _interleave(*bwd[:2])
            (dkv_b, dss_b, dsk_b), (dmx_b, dws_b, glg_b, glb_b) = _interleave(*bwd[2:])

            vec_ref[0:1, :] += rms_g[0][0] + rms_g[1][0]
            vec_ref[1:2, :] += rms_g[0][1] + rms_g[1][1]
            vec_ref[2:3, :] += glg_a + glg_b
            vec_ref[3:4, :] += glb_a + glb_b
            dbs_ref[...] += dmx_a + dmx_b
            for g in range(N_GROUPS):
                dws_ref[g] += dws_a[g] + dws_b[g]
            for h in range(N_HEADS):
                gbias_ref[h] += dss_a[h] + dss_b[h]
                dsink_ref[h] += dsk_a[h] + dsk_b[h]

            dkv_ref[:BLOCK, :] = done[...].astype(BF16)
            dkv_ref[BLOCK:, :] = (carry[...] + dkv_a[:BLOCK]).astype(BF16)
            done[...] = dkv_a[BLOCK:] + dkv_b[:BLOCK]
            carry[...] = dkv_b[BLOCK:]

    last = n_steps - 1
    cur = lambda w: pl.BlockSpec((tile, w), lambda n: (jnp.minimum(n, last), 0))
    late = lambda w: pl.BlockSpec((tile, w), lambda n: (jnp.clip(n - 1, 0, last), 0))
    before = pl.BlockSpec((BLOCK, 2 * KV_W), lambda n: (jnp.clip(2 * n - 1, 0, 2 * last + 1), 0))
    outs = pl.pallas_call(
        body, name="bwd_mix",
        out_shape=[jax.ShapeDtypeStruct((s, ATTN_W), BF16), jax.ShapeDtypeStruct((s, 2 * KV_W), BF16),
                   jax.ShapeDtypeStruct((s, GMLP_W), BF16), jax.ShapeDtypeStruct((s, GMLP_W), BF16),
                   jax.ShapeDtypeStruct((N_HEADS, BLOCK, 2 * BLOCK), F32),
                   jax.ShapeDtypeStruct((N_GROUPS, BLOCK, BLOCK), F32),
                   jax.ShapeDtypeStruct((BLOCK, GMLP_W), F32), jax.ShapeDtypeStruct((8, GMLP_W), F32),
                   jax.ShapeDtypeStruct((N_HEADS, BLOCK, 1), F32)]
        + [jax.ShapeDtypeStruct(_rx_shape(p.shape, k), BF16) for p, k in zip(grad_parts, grad_kinds)],
        grid=(n_steps + 1,),
        in_specs=[cur(ATTN_W), cur(2 * KV_W), before, cur(GMLP_W), cur(GMLP_W), cur(D_MODEL),
                  _const_spec((N_HEADS, BLOCK, 2 * BLOCK)), pl.BlockSpec(memory_space=pltpu.SMEM),
                  _const_spec((1, GMLP_W)), _const_spec((1, GMLP_W)), _const_spec((N_GROUPS, BLOCK, BLOCK)),
                  _const_spec((BLOCK, GMLP_W)), _const_spec((GMLP_W, GMLP_W)), _const_spec((1, ATTN_W)),
                  _const_spec((1, GMLP_W))] + [ANY] * n_g,
        out_specs=[cur(ATTN_W), late(2 * KV_W), cur(GMLP_W), cur(GMLP_W),
                   _const_spec((N_HEADS, BLOCK, 2 * BLOCK)), _const_spec((N_GROUPS, BLOCK, BLOCK)),
                   _const_spec((BLOCK, GMLP_W)), _const_spec((8, GMLP_W)), _const_spec((N_HEADS, BLOCK, 1))]
        + [ANY] * n_g,
        scratch_shapes=[pltpu.VMEM((BLOCK, 2 * KV_W), F32), pltpu.VMEM((BLOCK, 2 * KV_W), F32)]
        + _ChipExchange.sems(n_g),
        compiler_params=_params(("arbitrary",)),
    )(q, kv, kv, gu, gv, dmix, bias, sinks, gln_g, gln_b, wsm, bsx, amat, aog, gog, *grad_parts)
    return outs[:9], outs[9:]


def _mix_finalize(gbias, bucket, dws, dbs, dsink):
    def body(gb_ref, bucket_ref, dws_ref, dbs_ref, dsink_ref, tall_ref):
        bk = bucket_ref[...]
        lane = lax.broadcasted_iota(jnp.int32, (N_BUCKETS, LANES), 1)
        rowi = lax.broadcasted_iota(jnp.int32, (N_BUCKETS, LANES), 0)
        drb = jnp.zeros((N_BUCKETS, LANES), F32)
        dsk = jnp.zeros((8, LANES), F32)
        lane8 = lax.broadcasted_iota(jnp.int32, (8, LANES), 1)
        for h in range(N_HEADS):
            g = gb_ref[h]
            for b in range(N_BUCKETS):
                tot = jnp.sum(_colsum(jnp.where(bk == float(b), g, 0.0)), axis=1, keepdims=True)
                drb = jnp.where((rowi == h) & (lane == b), tot, drb)
            sk = jnp.sum(dsink_ref[h], axis=0, keepdims=True)
            dsk = jnp.where(lane8 == h, sk, dsk)
        tall_ref[TALL_RB:TALL_RB + N_BUCKETS, :] = drb
        tall_ref[TALL_SK:TALL_SK + 8, :] = dsk
        ti = lax.broadcasted_iota(jnp.int32, (BLOCK, BLOCK), 0)
        ui = lax.broadcasted_iota(jnp.int32, (BLOCK, BLOCK), 1)
        for g in range(N_GROUPS):
            tall_ref[g * BLOCK:(g + 1) * BLOCK, :] = jnp.where(ti >= ui, dws_ref[g], 0.0)
        gi = lax.broadcasted_iota(jnp.int32, (GMLP_W, LANES), 0) // GROUP_DIM
        li = lax.broadcasted_iota(jnp.int32, (GMLP_W, LANES), 1)
        ind = jnp.where(gi == li, 1.0, 0.0).astype(BF16)
        d = dbs_ref[...]
        hi = d.astype(BF16)
        r1 = d - hi.astype(F32)
        mid = r1.astype(BF16)
        lo = (r1 - mid.astype(F32)).astype(BF16)
        dbsg = _dot(hi, ind) + _dot(mid, ind) + _dot(lo, ind)
        tall_ref[TALL_BS:TALL_BS + N_GROUPS, :] = dbsg.T[:N_GROUPS, :]

    return pl.pallas_call(
        body, name="mix_finalize", out_shape=jax.ShapeDtypeStruct((TALL_ROWS, LANES), F32), grid=(1,),
        in_specs=[_const_spec((N_HEADS, BLOCK, 2 * BLOCK)), _const_spec((BLOCK, 2 * BLOCK)),
                  _const_spec((N_GROUPS, BLOCK, BLOCK)), _const_spec((BLOCK, GMLP_W)),
                  _const_spec((N_HEADS, BLOCK, 1))],
        out_specs=_const_spec((TALL_ROWS, LANES)),
        compiler_params=_params(("arbitrary",)),
    )(gbias, bucket, dws, dbs, dsink)


def _bwd_in(dq, dkv, dgu, dgv, dxa, x, modr, w_in, tm):
    s = x.shape[0]

    def body(dq_ref, dkv_ref, dgu_ref, dgv_ref, dxa_ref, x_ref, mod_ref, w_ref, gx_ref, acc_ref, db_ref):
        @pl.when(pl.program_id(0) == 0)
        def _():
            acc_ref[...] = jnp.zeros_like(acc_ref)
            db_ref[...] = jnp.zeros_like(db_ref)

        dproj = jnp.concatenate([dq_ref[...], dkv_ref[...], dgu_ref[...], dgv_ref[...]], axis=1)
        dh1 = _dot(dproj, w_ref[...])
        gx_ref[...] = dxa_ref[...].astype(F32) + dh1 * (1.0 + mod_ref[1:2, :])
        acc_ref[0:1, :] += _colsum(dh1 * x_ref[...].astype(F32))
        acc_ref[1:2, :] += _colsum(dh1)
        db_ref[0:1, :] += _colsum(dproj.astype(F32))

    row = lambda w: pl.BlockSpec((tm, w), lambda i: (i, 0))
    return pl.pallas_call(
        body, name="bwd_in",
        out_shape=(jax.ShapeDtypeStruct((s, D_MODEL), F32), jax.ShapeDtypeStruct((8, D_MODEL), F32),
                   jax.ShapeDtypeStruct((8, IN_W), F32)),
        grid=(s // tm,),
        in_specs=[row(ATTN_W), row(2 * KV_W), row(GMLP_W), row(GMLP_W), row(D_MODEL), row(D_MODEL),
                  _const_spec((8, D_MODEL)), _const_spec(w_in.shape)],
        out_specs=(row(D_MODEL), _const_spec((8, D_MODEL)), _const_spec((8, IN_W))),
        compiler_params=_params(("arbitrary",)),
    )(dq, dkv, dgu, dgv, dxa, x, modr, w_in)


def _wgrad(a, bs, tm, tk, name, owner_blocks=False, gather_vs=()):
    k_all, m = a.shape
    n = sum(b.shape[1] for b in bs)
    nk = k_all // tk
    nm = m // tm
    n_b = len(bs)
    n_v = len(gather_vs)
    wb = n // N_CHIPS

    def body(a_ref, *rest):
        b_refs, v_refs = rest[:n_b], rest[n_b:n_b + n_v]
        o_ref, ob_ref = rest[n_b + n_v:n_b + n_v + 2]
        vg_refs = rest[n_b + n_v + 2:n_b + 2 * n_v + 2]
        i, k = pl.program_id(0), pl.program_id(1)
        if n_v:
            gather = _Gather8(v_refs, vg_refs, *rest[n_b + 2 * n_v + 2:])

            @pl.when((i == 0) & (k == 0))
            def _():
                gather.start()

            @pl.when((i == nm - 1) & (k == 0))
            def _():
                gather.forward()

        @pl.when(k == 0)
        def _():
            o_ref[...] = jnp.zeros_like(o_ref)

        b = b_refs[0][...] if n_b == 1 else jnp.concatenate([r[...] for r in b_refs], axis=1)
        if owner_blocks:
            av = a_ref[...]
            for j in range(N_CHIPS):
                o_ref[j] += _dot_tn(av, b[:, j * wb:(j + 1) * wb])
        else:
            o_ref[...] += _dot_tn(a_ref[...], b)

        @pl.when(k == nk - 1)
        def _():
            ob_ref[...] = o_ref[...].astype(BF16)

        if n_v:
            @pl.when((i == nm - 1) & (k == nk - 1))
            def _():
                gather.finish()

    if owner_blocks:
        out_spec = pl.BlockSpec((N_CHIPS, tm, wb), lambda i, k: (0, i, 0))
        shape = (N_CHIPS, m, wb)
    else:
        out_spec = pl.BlockSpec((tm, n), lambda i, k: (i, 0))
        shape = (m, n)
    outs = pl.pallas_call(
        body, name=name,
        out_shape=[jax.ShapeDtypeStruct(shape, F32), jax.ShapeDtypeStruct(shape, BF16)] + _gathered8_shapes(gather_vs),
        grid=(nm, nk),
        in_specs=[pl.BlockSpec((tk, tm), lambda i, k: (k, i))]
        + [pl.BlockSpec((tk, b.shape[1]), lambda i, k: (k, 0)) for b in bs] + [ANY] * n_v,
        out_specs=[out_spec, out_spec] + [ANY] * n_v,
        scratch_shapes=_Gather8.sems(n_v) if n_v else [],
        compiler_params=_params(("arbitrary", "arbitrary") if n_v else ("parallel", "arbitrary")),
    )(a, *bs, *gather_vs)
    return outs[0], outs[1], outs[2:]


def _adam_math(w, g, m, v):
    m2 = ADAM_B1 * m + (1.0 - ADAM_B1) * g
    v2 = ADAM_B2 * v + (1.0 - ADAM_B2) * (g * g)
    m_hat = m2 / (1.0 - ADAM_B1 ** ADAM_STEP)
    v_hat = v2 / (1.0 - ADAM_B2 ** ADAM_STEP)
    delta = -ADAM_LR * (m_hat / (jnp.sqrt(v_hat) + ADAM_EPS) + ADAM_WD * w)
    return delta, m2, v2


def _transposed(g):
    r, c = g.shape
    pieces = []
    for lo in range(0, c, LANES):
        width = min(LANES, c - lo)
        piece = g[:, lo:lo + width]
        if width < LANES:
            piece = jnp.concatenate([piece, jnp.zeros((r, LANES - width), g.dtype)], axis=1)
        pieces.append(piece.T[:width])
    return jnp.concatenate(pieces, axis=0)


def _adam_halves(w, mine, got, m, v, tr, name, transposed=False):
    r, cc = w.shape[::-1] if transposed else w.shape
    h = r // 2
    nt = h // tr

    def body(c_ref, w_ref, mine_ref, got_ref, m_ref, v_ref, g_ref, d_ref, m2_ref, v2_ref):
        g = jnp.where(pl.program_id(0) == c_ref[0], mine_ref[...], got_ref[...])
        if transposed:
            g = _transposed(g)
        g_ref[...] = g
        d, m2, v2 = _adam_math(w_ref[...], g, m_ref[...], v_ref[...])
        d_ref[...] = d
        m2_ref[...] = m2
        v2_ref[...] = v2

    if transposed:
        full = pl.BlockSpec((cc, tr), lambda hh, i, c_ref: (0, hh * nt + i))
    else:
        full = pl.BlockSpec((tr, cc), lambda hh, i, c_ref: (hh * nt + i, 0))
    half = pl.BlockSpec((tr, cc), lambda hh, i, c_ref: (i, 0))
    shp = jax.ShapeDtypeStruct(w.shape, F32)
    return pl.pallas_call(
        body, name=name, out_shape=(shp, shp, shp, shp),
        grid_spec=pltpu.PrefetchScalarGridSpec(
            num_scalar_prefetch=1, grid=(2, nt), in_specs=[full, half, half, full, full],
            out_specs=(full, full, full, full)),
        compiler_params=_params(("arbitrary", "arbitrary")),
    )(_core_index_scalar(), w, mine, got, m, v)


def _adam_w_ada(sc_t, dmod_all, w, m, v, tr):
    r, cc = w.shape

    def body(chip_ref, sct_ref, dm_ref, w_ref, m_ref, v_ref, g_ref, d_ref, m2_ref, v2_ref):
        g = sct_ref[:, 0:1] * dm_ref[0:1, :]
        for k in range(1, N_DEV):
            g = g + sct_ref[:, k:k + 1] * dm_ref[k:k + 1, :]
        g_ref[...] = g
        d, m2, v2 = _adam_math(w_ref[...], g, m_ref[...], v_ref[...])
        d_ref[...] = d
        m2_ref[...] = m2
        v2_ref[...] = v2

    spec = pl.BlockSpec((tr, cc), lambda i, chip_ref: (i, 0))
    shp = jax.ShapeDtypeStruct((r, cc), F32)
    return pl.pallas_call(
        body, name="adam_w_ada", out_shape=(shp, shp, shp, shp),
        grid_spec=pltpu.PrefetchScalarGridSpec(
            num_scalar_prefetch=1, grid=(r // tr,),
            in_specs=[pl.BlockSpec((tr, N_DEV), lambda i, chip_ref: (i, 0)),
                      pl.BlockSpec((N_DEV, cc), lambda i, chip_ref: (0, chip_ref[0])), spec, spec, spec],
            out_specs=(spec, spec, spec, spec)),
        compiler_params=_params(("parallel",)),
    )(_chip_index_scalar(), sc_t, dmod_all, w, m, v)


def _pack_wide(acc_i, acc_m, acc_f, db_in, vec):
    arrs = [acc_i, acc_m, acc_f, db_in, vec]
    i_, m_, f_, b_, v_ = range(5)
    src = {"b_in": (b_, 0), "ln1_g": (m_, 2), "ln1_b": (m_, 3), "ln2_g": (f_, 1), "ln2_b": (f_, 2),
           "gmlp_ln_g": (v_, 2), "gmlp_ln_b": (v_, 3), "attn_out_g": (v_, 0), "gmlp_out_g": (v_, 1), "loss": (f_, 0)}
    dmod = [(i_, 1), (i_, 0), (m_, 4), (m_, 1), (m_, 0), (f_, 3)]

    def body(*refs):
        ins, wide_ref = refs[:5], refs[5]
        wide_ref[...] = jnp.zeros_like(wide_ref)
        for k, (a, row) in enumerate(dmod):
            wide_ref[0:1, k * D_MODEL:(k + 1) * D_MODEL] = ins[a][row:row + 1, :]
        for name, (a, row) in src.items():
            r, off, n = WIDE_LAYOUT[name]
            wide_ref[r:r + 1, off:off + n] = ins[a][row:row + 1, :]

    return pl.pallas_call(
        body, name="pack_wide", out_shape=jax.ShapeDtypeStruct((8, WIDE_W), F32), grid=(1,),
        in_specs=[_const_spec(a.shape) for a in arrs], out_specs=_const_spec((8, WIDE_W)),
        compiler_params=_params(("arbitrary",)),
    )(*arrs)


def _adam_small(gw, gt, wide_wmv, w_s, b_s, rel_bias, sinks, after):
    names = list(WIDE_PARAMS)
    tall = [("gmlp_w_s", w_s), ("gmlp_b_s", b_s), ("rel_bias", rel_bias), ("attn_sinks", sinks)]
    ins = [gw, gt]
    for n in names:
        ins += list(wide_wmv[n])
    for _, t in tall:
        ins += list(t)
    n_in = len(ins)

    def body(*refs):
        gw_ref, gt_ref = refs[0], refs[1]
        wmv = refs[2:n_in]
        dmod_ref, loss_ref, loss1_ref = refs[n_in + 1:n_in + 4]
        outs = refs[n_in + 4:]

        def tall_sum(r0, nr):
            g = gt_ref[r0:r0 + nr, :]
            for d in range(1, N_DEV):
                g = g + gt_ref[d * TALL_ROWS + r0:d * TALL_ROWS + r0 + nr, :]
            return g

        def emit(k, g, w_ref, m_ref, v_ref):
            d, m2, v2 = _adam_math(w_ref[...], g, m_ref[...], v_ref[...])
            outs[4 * k][...] = g
            outs[4 * k + 1][...] = d
            outs[4 * k + 2][...] = m2
            outs[4 * k + 3][...] = v2

        gsum = gw_ref[0:8, :]
        for d in range(1, N_DEV):
            gsum = gsum + gw_ref[8 * d:8 * d + 8, :]
        for d in range(N_DEV):
            dmod_ref[d:d + 1, :] = gw_ref[8 * d:8 * d + 1, :]
        for k, n in enumerate(names):
            r, off, sz = WIDE_LAYOUT[n]
            emit(k, gsum[r:r + 1, off:off + sz], *wmv[3 * k:3 * k + 3])
        r, off, sz = WIDE_LAYOUT["loss"]
        tot = jnp.sum(gsum[r:r + 1, off:off + sz], axis=1, keepdims=True)
        loss_ref[...] = jnp.broadcast_to(tot * (0.5 / D_MODEL), loss_ref.shape)
        loss1_ref[...] = tot * (0.5 / D_MODEL)

        k0 = len(names)
        ws_refs = wmv[3 * k0:3 * k0 + 3]
        for g in range(N_GROUPS):
            rows = slice(g * BLOCK, (g + 1) * BLOCK)
            gg = tall_sum(g * BLOCK, BLOCK)
            d, m2, v2 = _adam_math(ws_refs[0][rows, :], gg, ws_refs[1][rows, :], ws_refs[2][rows, :])
            outs[4 * k0][rows, :] = gg
            outs[4 * k0 + 1][rows, :] = d
            outs[4 * k0 + 2][rows, :] = m2
            outs[4 * k0 + 3][rows, :] = v2
        emit(k0 + 1, tall_sum(TALL_BS, N_GROUPS), *wmv[3 * (k0 + 1):3 * (k0 + 1) + 3])
        emit(k0 + 2, tall_sum(TALL_RB, N_HEADS)[:, :N_BUCKETS], *wmv[3 * (k0 + 2):3 * (k0 + 2) + 3])
        emit(k0 + 3, tall_sum(TALL_SK, 8)[0:1, :N_HEADS], *wmv[3 * (k0 + 3):3 * (k0 + 3) + 3])

    out_shapes = [jax.ShapeDtypeStruct((N_DEV, WIDE_W), F32), jax.ShapeDtypeStruct((8, LANES), F32),
                  jax.ShapeDtypeStruct((1, 1), F32)]
    for n in names:
        out_shapes += [jax.ShapeDtypeStruct(wide_wmv[n][0].shape, F32)] * 4
    for _, t in tall:
        out_shapes += [jax.ShapeDtypeStruct(t[0].shape, F32)] * 4
    res = pl.pallas_call(
        body, name="adam_small", out_shape=out_shapes, grid=(1,),
        in_specs=[_const_spec(a.shape) for a in ins] + [ANY], out_specs=[_const_spec(o.shape) for o in out_shapes],
        compiler_params=_params(("arbitrary",)),
    )(*ins, after)
    out = {}
    for k, n in enumerate(names + [t[0] for t in tall]):
        out[n] = tuple(res[3 + 4 * k:7 + 4 * k])
    return res[0], res[1], res[2], out


def kernel(x, c, rel_bias, w_ada, b_ada, w_in, b_in, attn_sinks, gmlp_ln_g, gmlp_ln_b, gmlp_w_s, gmlp_b_s, attn_out_g, gmlp_out_g, w_out, ln1_g, ln1_b, w_gate_up, w_down, ln2_g, ln2_b, loss_target, m_rel_bias, m_w_ada, m_b_ada, m_w_in, m_b_in, m_attn_sinks, m_gmlp_ln_g, m_gmlp_ln_b, m_gmlp_w_s, m_gmlp_b_s, m_attn_out_g, m_gmlp_out_g, m_w_out, m_ln1_g, m_ln1_b, m_w_gate_up, m_w_down, m_ln2_g, m_ln2_b, v_rel_bias, v_w_ada, v_b_ada, v_w_in, v_b_in, v_attn_sinks, v_gmlp_ln_g, v_gmlp_ln_b, v_gmlp_w_s, v_gmlp_b_s, v_attn_out_g, v_gmlp_out_g, v_w_out, v_ln1_g, v_ln1_b, v_w_gate_up, v_w_down, v_ln2_g, v_ln2_b):
    ix, iy, _ = _my_pos()
    chip = 2 * ix + iy
    s = x.shape[1]
    xs = x[0]
    tgt = loss_target[0]
    tm_big = min(512, s)
    tm_ffn = min(FFN_SUB, s)

    w_in_s, w_out_s = w_in[0].T.astype(BF16), w_out[0].astype(BF16)
    w_gu_s, w_dn_s = w_gate_up[0].astype(BF16), w_down[0].astype(BF16)
    sc_all, modr, (w_in_g, w_out_g) = _prologue(jnp.pad(c, ((0, 7), (0, 0))), w_ada[0], b_ada, [w_in_s, w_out_s])
    w_in_f = _insert_own(w_in_g, w_in_s, "blk", chip).reshape(IN_W, D_MODEL)

    bucket = _bucket_table()
    bias, wsm = _prep_tables(bucket, rel_bias.T, gmlp_w_s[0])
    bsx = jnp.repeat(gmlp_b_s[0].T, GROUP_DIM, axis=1)
    amat = _group_mean_matrix()
    sinks = attn_sinks[0]

    (h1, q, kv, gu, gv, xb), (w_dn_g,) = _fwd_in(xs, modr, w_in_f, b_in, tm_big, [w_dn_s], ["blk"])
    w_out_f = _insert_own(w_out_g, w_out_s, "blk", chip).reshape(D_MODEL, D_MODEL)
    (x1, x1b, y, mixed), (w_gu_g,) = _fwd_mix(
        q, kv, gu, gv, xs, modr, bias, sinks, gmlp_ln_g, gmlp_ln_b, wsm, bsx, amat, attn_out_g, gmlp_out_g, w_out_f,
        ln1_g, ln1_b, tm_big, [w_gu_s], ["blk"])
    assert w_gate_up.shape[2] == FF_CHUNK
    w_gu_f = _insert_own(w_gu_g, w_gu_s, "blk", chip)
    w_dn_f = _insert_own(w_dn_g, w_dn_s, "blk", chip).reshape(D_FF, D_MODEL)
    h2, act, dy2, dx1a, acc_f = _fwd_ffn(x1, tgt, modr, ln2_g, ln2_b, w_gu_f, w_dn_f, tm_ffn)

    a_act, dgu_ff, dh2 = _bwd_ffn(dy2, act, w_gu_f, w_dn_f, min(FFN_SUB, s))
    g_dn, g_dn_b, _ = _wgrad(a_act, [dy2], D_FF // 2, min(1024, s), "wgrad_down")
    g_gu, g_gu_b, _ = _wgrad(h2, [dgu_ff], 512, min(512, s), "wgrad_gate_up")
    blk3 = lambda a, rows: a.reshape(N_CHIPS, rows, a.shape[1])
    (dxa, dy, dmix, acc_m), (got_dn, got_gu) = _bwd_mid(
        dh2, dx1a, x1b, xs, y, modr, ln1_g, w_out_f, tm_big, [blk3(g_dn_b, D_FF // N_CHIPS), g_gu_b], ["blk", "cols"])
    g_out, g_out_b, _ = _wgrad(mixed, [dy], 512, min(2048, s), "wgrad_out")
    (got_out,) = _swap_halves([blk3(g_out_b, D_MODEL // N_CHIPS)], ["blk"], "rs_swap_out")
    kinds_a = ["blk", "cols", "blk"]
    fulls_a = [blk3(g_dn, D_FF // N_CHIPS), g_gu, blk3(g_out, D_MODEL // N_CHIPS)]
    gots_a = [got_dn, got_gu, got_out]
    parts_a = [_add_halves(f, g, k, "rs_add_a%d" % i) for i, (f, g, k) in enumerate(zip(fulls_a, gots_a, kinds_a))]
    (dq, dkv, dgu, dgv, gbias, dws, dbs, vec, dsink), rxs_a = _bwd_mix(
        q, kv, gu, gv, dmix, bias, sinks, gmlp_ln_g, gmlp_ln_b, wsm, bsx, amat, attn_out_g, gmlp_out_g,
        [p[1] for p in parts_a], kinds_a)
    tall_g = _mix_finalize(gbias, bucket, dws, dbs, dsink)
    grad_x, acc_i, db_in = _bwd_in(dq, dkv, dgu, dgv, dxa, xb, modr, w_in_f, tm_big)

    wide_g = _pack_wide(acc_i, acc_m, acc_f, db_in, vec)
    full_in, full_in_b, (gw, gt) = _wgrad(h1, [dq, dkv, dgu, dgv], 512, min(1024, s), "wgrad_in", owner_blocks=True,
                                          gather_vs=[wide_g, tall_g])
    (got_in,) = _swap_halves([full_in_b], ["blk"], "rs_swap_in")
    part_in = _add_halves(full_in, got_in, "blk", "rs_add_in")
    in_send, in_recv, in_part, in_rx, token = _exchange_start(part_in[1], "rs_chips_in_start")
    wide_wmv ={"b_ada": (b_ada, m_b_ada, v_b_ada), "b_in": (b_in, m_b_in, v_b_in),
                "ln1_g": (ln1_g, m_ln1_g, v_ln1_g), "ln1_b": (ln1_b, m_ln1_b, v_ln1_b),
                "ln2_g": (ln2_g, m_ln2_g, v_ln2_g), "ln2_b": (ln2_b, m_ln2_b, v_ln2_b),
                "gmlp_ln_g": (gmlp_ln_g, m_gmlp_ln_g, v_gmlp_ln_g), "gmlp_ln_b": (gmlp_ln_b, m_gmlp_ln_b, v_gmlp_ln_b),
                "attn_out_g": (attn_out_g, m_attn_out_g, v_attn_out_g),
                "gmlp_out_g": (gmlp_out_g, m_gmlp_out_g, v_gmlp_out_g)}
    rows2 = lambda a: a.reshape(-1, a.shape[-1])
    dmod_all, loss_t, loss1, small = _adam_small(
        gw, gt, wide_wmv, tuple(rows2(a) for a in (gmlp_w_s, m_gmlp_w_s, v_gmlp_w_s)),
        tuple(rows2(a) for a in (gmlp_b_s, m_gmlp_b_s, v_gmlp_b_s)), (rel_bias.T, m_rel_bias.T, v_rel_bias.T),
        (attn_sinks, m_attn_sinks, v_attn_sinks), token)
    small["rel_bias"] = tuple(a.T for a in small["rel_bias"])
    loss = loss1.reshape(())

    g_ada, d_ada, m_ada, v_ada = _adam_w_ada(sc_all.T, dmod_all, w_ada[0], m_w_ada[0], v_w_ada[0], 256)

    sums = [(parts_a[0][0], rxs_a[0], 176), (parts_a[1][0], rxs_a[1], 256), (parts_a[2][0], rxs_a[2], 128)]
    mine = [_sum_chips(p, rx, tr, "rs_sum_%d" % i, loss_t) for i, (p, rx, tr) in enumerate(sums)]
    got = _share_halves(mine, "rs_share")
    gs_dn, d_dn, m_dn, v_dn = _adam_halves(w_down[0], mine[0], got[0], m_w_down[0], v_w_down[0], 176, "adam_w_down")
    gs_gu, d_gu, m_gu, v_gu = _adam_halves(w_gate_up[0], mine[1], got[1], m_w_gate_up[0], v_w_gate_up[0], 256,
                                           "adam_w_gate_up")
    gs_out, d_out, m_out, v_out = _adam_halves(w_out[0], mine[2], got[2], m_w_out[0], v_w_out[0], 128, "adam_w_out")

    rx_in = _exchange_wait(in_send, in_recv, in_part, in_rx, d_gu, "rs_chips_in_wait")
    mine_in = _sum_chips(part_in[0], rx_in, 256, "rs_sum_in", rx_in)
    (got_in_half,) = _share_halves([mine_in], "rs_share_in")
    in_t = _adam_halves(w_in[0].T, mine_in, got_in_half, m_w_in[0].T, v_w_in[0].T, 256, "adam_w_in", transposed=True)
    gs_in, d_in, m_in, v_in = (a.T for a in in_t)

    big = {"w_ada": (g_ada, d_ada, m_ada, v_ada), "w_in": (gs_in, d_in, m_in, v_in), "w_out": (gs_out, d_out, m_out, v_out),
           "w_gate_up": (gs_gu, d_gu, m_gu, v_gu), "w_down": (gs_dn, d_dn, m_dn, v_dn)}
    order = ["rel_bias", "w_ada", "b_ada", "w_in", "b_in", "attn_sinks", "gmlp_ln_g", "gmlp_ln_b", "gmlp_w_s", "gmlp_b_s",
             "attn_out_g", "gmlp_out_g", "w_out", "ln1_g", "ln1_b", "w_gate_up", "w_down", "ln2_g", "ln2_b"]
    shapes = {"gmlp_w_s": gmlp_w_s.shape, "gmlp_b_s": gmlp_b_s.shape}
    outs = [loss, grad_x[None]]
    for k in range(4):
        for name in order:
            if name in big:
                outs.append(big[name][k][None])
            elif name in shapes:
                outs.append(small[name][k].reshape(shapes[name]))
            else:
                outs.append(small[name][k])
    return tuple(outs)
```

```python
import math

import numpy as np
import jax
import jax.numpy as jnp
from jax import lax
from jax.experimental import pallas as pl
from jax.experimental.pallas import tpu as pltpu

F32 = jnp.float32
BF16 = jnp.bfloat16
MESH = pl.DeviceIdType.MESH

D_MODEL = 1024
N_HEADS = 8
N_KV = 2
HEAD_DIM = 64
ATTN_W = N_HEADS * HEAD_DIM
KV_W = N_KV * HEAD_DIM
N_GROUPS = 8
GROUP_DIM = 64
GMLP_W = N_GROUPS * GROUP_DIM
IN_W = ATTN_W + 2 * KV_W + 2 * GMLP_W
BLOCK = 128
N_BUCKETS = 32
MAX_DISTANCE = 128
D_FF = 2816
ALPHA = 2.0 ** 0.25
LN_EPS = 1e-5
NEG_INF = -1e30
ADAM_LR, ADAM_B1, ADAM_B2, ADAM_EPS, ADAM_WD, ADAM_STEP = 0.001, 0.9, 0.999, 1e-8, 0.01, 10
N_CHIPS = 4
N_DEV = 8
LANES = 128
V7X_VMEM_LIMIT = 56 * 2 ** 20
GELU_C = math.sqrt(2.0 / math.pi)
Q_SCALE = HEAD_DIM ** -0.5
ANY = pl.BlockSpec(memory_space=pl.ANY)

TALL_BS = N_GROUPS * BLOCK
TALL_RB = TALL_BS + 8
TALL_SK = TALL_RB + N_BUCKETS
TALL_ROWS = TALL_SK + 8
WIDE_W = 6 * D_MODEL
WIDE_LAYOUT = {
    "b_ada": (0, 0, 6 * D_MODEL),
    "b_in": (1, 0, IN_W), "ln1_g": (1, IN_W, D_MODEL), "ln1_b": (1, IN_W + D_MODEL, D_MODEL),
    "ln2_g": (1, IN_W + 2 * D_MODEL, D_MODEL), "ln2_b": (1, IN_W + 3 * D_MODEL, D_MODEL),
    "gmlp_ln_g": (2, 0, GMLP_W), "gmlp_ln_b": (2, GMLP_W, GMLP_W), "attn_out_g": (2, 2 * GMLP_W, ATTN_W),
    "gmlp_out_g": (2, 2 * GMLP_W + ATTN_W, GMLP_W), "loss": (2, 3 * GMLP_W + ATTN_W, D_MODEL)}
WIDE_PARAMS = tuple(n for n in WIDE_LAYOUT if n != "loss")


def _params(sem=None):
    return pltpu.CompilerParams(dimension_semantics=sem, vmem_limit_bytes=V7X_VMEM_LIMIT)


def _const_spec(shape, single=False):
    nd = len(shape)
    if single:
        return pl.BlockSpec(shape, lambda *_: (0,) * nd, pipeline_mode=pl.Buffered(1))
    return pl.BlockSpec(shape, lambda *_: (0,) * nd)


def _dot(a, b):
    return jnp.dot(a, b, preferred_element_type=F32)


def _dot_nt(a, b):
    return lax.dot_general(a, b, (((1,), (1,)), ((), ())), preferred_element_type=F32)


def _dot_tn(a, b):
    return lax.dot_general(a, b, (((0,), (0,)), ((), ())), preferred_element_type=F32)


def _gelu(x):
    t = jnp.tanh(GELU_C * (x + 0.044715 * x * x * x))
    return 0.5 * x * (1.0 + t), t


def _gelu_grad(x, t):
    return 0.5 * (1.0 + t) + 0.5 * x * (1.0 - t * t) * GELU_C * (1.0 + 3.0 * 0.044715 * x * x)


def _split_dot(x, a):
    hi = x.astype(BF16)
    lo = (x - hi.astype(F32)).astype(BF16)
    return _dot(hi, a) + _dot(lo, a)


def _group_mean_matrix():
    g = np.arange(GMLP_W) // GROUP_DIM
    return jnp.asarray((g[:, None] == g[None, :]).astype(np.float32) / GROUP_DIM, dtype=BF16)


def _ln_stats(z):
    mu = jnp.mean(z, axis=-1, keepdims=True)
    d = z - mu
    var = jnp.mean(d * d, axis=-1, keepdims=True)
    rstd = lax.rsqrt(var + LN_EPS)
    return d * rstd, rstd


def _ln_bwd(dxhat, xhat, rstd):
    m1 = jnp.mean(dxhat, axis=-1, keepdims=True)
    m2 = jnp.mean(dxhat * xhat, axis=-1, keepdims=True)
    return rstd * (dxhat - m1 - xhat * m2)


def _colsum(x):
    return jnp.sum(x, axis=0, keepdims=True)


def _my_pos():
    return lax.axis_index("x"), lax.axis_index("y"), lax.axis_index("c")


def _other_chips(x, y):
    return [(1 - x, y), (x, 1 - y), (1 - x, 1 - y)]


def _chip_index_scalar():
    ix, iy, _ = _my_pos()
    return jnp.reshape(2 * ix + iy, (1,)).astype(jnp.int32)


def _core_index_scalar():
    return jnp.reshape(lax.axis_index("c"), (1,)).astype(jnp.int32)


class _Gather8:
    def __init__(self, x_refs, out_refs, send_sems, recv_sems, local_sems):
        self.x_refs, self.out_refs = x_refs, out_refs
        self.send_sems, self.recv_sems, self.local_sems = send_sems, recv_sems, local_sems
        self.x, self.y, self.c = _my_pos()
        self.me, self.sibling = (self.x, self.y, self.c), (self.x, self.y, 1 - self.c)
        self.chips = _other_chips(self.x, self.y)

    def _rows(self, a, px, py, pc):
        m_per = self.x_refs[a].shape[0]
        return self.out_refs[a].at[pl.ds((4 * px + 2 * py + pc) * m_per, m_per), :]

    def _copy(self, a, k, block, to, src=None):
        return pltpu.make_async_remote_copy(
            src_ref=self._rows(a, *block) if src is None else src, dst_ref=self._rows(a, *block),
            send_sem=self.send_sems.at[7 * a + k], recv_sem=self.recv_sems.at[7 * a + k], device_id=to,
            device_id_type=MESH)

    def _local(self, a):
        return pltpu.make_async_copy(self.x_refs[a], self._rows(a, *self.me), self.local_sems.at[a])

    def start(self):
        for a in range(len(self.x_refs)):
            self._local(a).start()
            self._copy(a, 0, self.me, self.sibling, src=self.x_refs[a]).start()
            for j, chip in enumerate(self.chips):
                self._copy(a, 1 + j, self.me, (*chip, self.c), src=self.x_refs[a]).start()

    def forward(self):
        for a in range(len(self.x_refs)):
            for j, chip in enumerate(self.chips):
                self._copy(a, 1 + j, (*chip, self.c), self.me).wait_recv()
                self._copy(a, 4 + j, (*chip, self.c), self.sibling).start()

    def finish(self):
        for a in range(len(self.x_refs)):
            self._copy(a, 0, self.sibling, self.me).wait_recv()
            for j, chip in enumerate(self.chips):
                self._copy(a, 4 + j, (*chip, 1 - self.c), self.me).wait_recv()
        for a in range(len(self.x_refs)):
            for k in range(7):
                self._copy(a, k, self.me, self.me).wait_send()
            self._local(a).wait()

    @staticmethod
    def sems(n_v):
        return [pltpu.SemaphoreType.DMA((7 * n_v,)), pltpu.SemaphoreType.DMA((7 * n_v,)),
                pltpu.SemaphoreType.DMA((n_v,))]


def _gathered8_shapes(vs):
    return [jax.ShapeDtypeStruct((N_DEV * v.shape[0], v.shape[1]), v.dtype) for v in vs]


VMEM_WHOLE = pl.BlockSpec(memory_space=pltpu.VMEM)


def _prologue(c_pad, w_ada_s, b_ada, shards):
    n = w_ada_s.shape[1]
    n_w = len(shards)
    assert N_CHIPS * n == 6 * D_MODEL and n % LANES == 0

    def body(c_ref, w_ref, b_ref, *rest):
        shard_refs = rest[:n_w]
        sc_ref, modc_ref, modg_ref, modr_ref = rest[n_w:n_w + 4]
        gathered_refs = rest[n_w + 4:2 * n_w + 4]
        call_ref, w_vmem = rest[2 * n_w + 4:2 * n_w + 6]
        sems = rest[2 * n_w + 6:]
        ix, iy, ic = _my_pos()
        chip = 2 * ix + iy
        weights = _WeightGather(shard_refs, gathered_refs, ["blk"] * n_w, sems[0], sems[1])
        gather_c = _Gather8([c_ref], [call_ref], sems[2], sems[3], sems[4])
        gather_mod = _Gather8([modc_ref], [modg_ref], sems[5], sems[6], sems[7])
        load_w = pltpu.make_async_copy(w_ref, w_vmem, sems[8])
        weights.start()
        gather_c.start()
        load_w.start()
        gather_c.forward()
        gather_c.finish()
        cv = call_ref[...]
        sc = cv * _sigmoid(cv)
        a_hi = sc.astype(BF16)
        a_lo = (sc - a_hi.astype(F32)).astype(BF16)
        load_w.wait()
        w = w_vmem[...]
        w_hi = w.astype(BF16)
        w_lo = (w - w_hi.astype(F32)).astype(BF16)
        b = b_ref[:, 0:n]
        for k in range(1, N_CHIPS):
            b = jnp.where(chip == k, b_ref[:, k * n:(k + 1) * n], b)
        mod = _dot(a_hi, w_hi) + _dot(a_hi, w_lo) + _dot(a_lo, w_hi) + b
        for d in range(N_DEV):
            sc_ref[d:d + 1, :] = sc[8 * d:8 * d + 1, :]
            modc_ref[d:d + 1, :] = mod[8 * d:8 * d + 1, :]
        gather_mod.start()
        weights.forward()
        gather_mod.forward()
        gather_mod.finish()
        dev = 2 * chip + ic
        mine = jnp.concatenate([modg_ref[pl.ds(2 * 8 * k + dev, 1), :] for k in range(N_CHIPS)], axis=1)
        modr_ref[...] = jnp.zeros_like(modr_ref)
        for r in range(6):
            modr_ref[r:r + 1, :] = mine[:, r * D_MODEL:(r + 1) * D_MODEL]
        weights.forward_diagonal()
        weights.finish()

    outs = pl.pallas_call(
        body, name="prologue",
        out_shape=[jax.ShapeDtypeStruct((N_DEV, D_MODEL), F32), jax.ShapeDtypeStruct((N_DEV, n), F32),
                   jax.ShapeDtypeStruct((N_DEV * N_DEV, n), F32), jax.ShapeDtypeStruct((8, D_MODEL), F32)]
        + [jax.ShapeDtypeStruct(_gathered_shape(sh, "blk"), BF16) for sh in shards],
        in_specs=[VMEM_WHOLE, ANY, VMEM_WHOLE] + [ANY] * n_w,
        out_specs=[VMEM_WHOLE, VMEM_WHOLE, VMEM_WHOLE, VMEM_WHOLE] + [ANY] * n_w,
        scratch_shapes=[pltpu.VMEM((N_DEV * 8, D_MODEL), F32), pltpu.VMEM(w_ada_s.shape, F32)]
        + _WeightGather.sems(n_w) + _Gather8.sems(1) + _Gather8.sems(1) + [pltpu.SemaphoreType.DMA],
        compiler_params=pltpu.CompilerParams(vmem_limit_bytes=V7X_VMEM_LIMIT),
    )(c_pad, w_ada_s, b_ada, *shards)
    return outs[0], outs[3], outs[4:]


def _gathered_shape(shard, kind):
    r, cc = shard.shape
    return (N_CHIPS, r, cc) if kind == "blk" else (r, N_CHIPS * cc)


class _WeightGather:
    N_SEM = 8

    def __init__(self, shards, gathered, kinds, send_sems, recv_sems):
        self.shards, self.gathered, self.kinds = shards, gathered, kinds
        self.send_sems, self.recv_sems = send_sems, recv_sems
        self.x, self.y, self.c = _my_pos()
        self.me, self.sibling = (self.x, self.y, self.c), (self.x, self.y, 1 - self.c)
        self.nbr = ((1 - self.x, self.y), (self.x, 1 - self.y))
        self.diag = 2 * (1 - self.x) + (1 - self.y)

    def _dst(self, a, chip, pc, quarter=None):
        r, cc = self.shards[a].shape
        h = r // 2
        row0, rows = pc * h, h
        if quarter is not None:
            row0, rows = pc * h + quarter * (h // 2), h // 2
        g = self.gathered[a]
        if self.kinds[a] == "blk":
            return g.at[chip, pl.ds(row0, rows), :]
        return g.at[pl.ds(row0, rows), pl.ds(chip * cc, cc)]

    def _copy(self, a, k, region, to, src=None):
        return pltpu.make_async_remote_copy(
            src_ref=region if src is None else src, dst_ref=region, send_sem=self.send_sems.at[a * self.N_SEM + k],
            recv_sem=self.recv_sems.at[a * self.N_SEM + k], device_id=to, device_id_type=MESH)

    def _arrays(self):
        return range(len(self.shards))

    def start(self):
        my_chip = 2 * self.x + self.y
        for a in self._arrays():
            h = self.shards[a].shape[0] // 2
            mine = self.shards[a].at[pl.ds(self.c * h, h), :]
            for j, chip in enumerate(self.nbr):
                self._copy(a, j, self._dst(a, my_chip, self.c), (*chip, self.c), src=mine).start()

    def forward(self):
        for a in self._arrays():
            for j, chip in enumerate(self.nbr):
                cj = 2 * chip[0] + chip[1]
                half = self._dst(a, cj, self.c)
                self._copy(a, j, half, self.me).wait_recv()
                self._copy(a, 2 + j, half, self.sibling).start()
                other = self.nbr[1 - j]
                self._copy(a, 4 + j, self._dst(a, cj, self.c, quarter=j), (*other, self.c)).start()

    def forward_diagonal(self):
        for a in self._arrays():
            for j in range(2):
                quarter = self._dst(a, self.diag, self.c, quarter=j)
                self._copy(a, 4 + j, quarter, self.me).wait_recv()
                self._copy(a, 6 + j, quarter, self.sibling).start()

    def finish(self):
        for a in self._arrays():
            for j, chip in enumerate(self.nbr):
                self._copy(a, 2 + j, self._dst(a, 2 * chip[0] + chip[1], 1 - self.c), self.me).wait_recv()
                self._copy(a, 6 + j, self._dst(a, self.diag, 1 - self.c, quarter=j), self.me).wait_recv()
        for a in self._arrays():
            half = self._dst(a, self.diag, self.c)
            quarter = self._dst(a, self.diag, self.c, quarter=0)
            for k in range(self.N_SEM):
                self._copy(a, k, half if k < 4 else quarter, self.me).wait_send()

    @classmethod
    def sems(cls, n_arr):
        return [pltpu.SemaphoreType.DMA((n_arr * cls.N_SEM,)), pltpu.SemaphoreType.DMA((n_arr * cls.N_SEM,))]


def _insert_own(gathered, shard, kind, chip):
    if kind == "blk":
        return lax.dynamic_update_slice(gathered, shard[None], (chip, 0, 0))
    return lax.dynamic_update_slice(gathered, shard, (0, chip * shard.shape[1]))


def _half_of_full(ref, kind, pc):
    if kind == "blk":
        h = ref.shape[1] // 2
        return ref.at[:, pl.ds(pc * h, h), :]
    h = ref.shape[0] // 2
    return ref.at[pl.ds(pc * h, h), :]


def _half_shape(shape, kind):
    return (shape[0], shape[1] // 2, shape[2]) if kind == "blk" else (shape[0] // 2, shape[1])


class _HalfSwap:
    def __init__(self, ins, outs, kinds, send_sems, recv_sems):
        self.ins, self.outs, self.kinds = ins, outs, kinds
        self.send_sems, self.recv_sems = send_sems, recv_sems
        self.x, self.y, self.c = _my_pos()

    def _copies(self):
        for a in range(len(self.ins)):
            yield pltpu.make_async_remote_copy(
                src_ref=_half_of_full(self.ins[a], self.kinds[a], 1 - self.c), dst_ref=self.outs[a],
                send_sem=self.send_sems.at[a], recv_sem=self.recv_sems.at[a],
                device_id=(self.x, self.y, 1 - self.c), device_id_type=MESH)

    def start(self):
        for cp in self._copies():
            cp.start()

    def wait(self):
        for cp in self._copies():
            cp.wait()

    @staticmethod
    def sems(n_arr):
        return [pltpu.SemaphoreType.DMA((n_arr,)), pltpu.SemaphoreType.DMA((n_arr,))]

    @staticmethod
    def out_shapes(fulls, kinds):
        return [jax.ShapeDtypeStruct(_half_shape(a.shape, k), a.dtype) for a, k in zip(fulls, kinds)]


def _swap_halves(fulls_bf16, kinds, name):
    n_arr = len(fulls_bf16)

    def body(*refs):
        swap = _HalfSwap(refs[:n_arr], refs[n_arr:2 * n_arr], kinds, *refs[2 * n_arr:])
        swap.start()
        swap.wait()

    return pl.pallas_call(
        body, name=name, out_shape=_HalfSwap.out_shapes(fulls_bf16, kinds),
        in_specs=[ANY] * n_arr, out_specs=[ANY] * n_arr, scratch_shapes=_HalfSwap.sems(n_arr),
    )(*fulls_bf16)


def _add_halves(full, got, kind, name):
    hs = _half_shape(full.shape, kind)

    def body(pos_ref, a_ref, b_ref, o_ref, ob_ref):
        p = a_ref[...] + b_ref[...].astype(F32)
        ob_ref[...] = p.astype(BF16)

        @pl.when(pl.program_id(0) == pos_ref[1])
        def _():
            o_ref[...] = p.reshape(o_ref.shape)

    if kind == "blk":
        nb, h, cc = hs
        own = pl.BlockSpec((1, h, cc), lambda b, pos_ref: (b, pos_ref[0], 0))
        other = pl.BlockSpec((1, h, cc), lambda b, pos_ref: (b, 0, 0))
    else:
        h, cc = hs[0], hs[1] // N_CHIPS
        own = pl.BlockSpec((h, cc), lambda b, pos_ref: (pos_ref[0], b))
        other = pl.BlockSpec((h, cc), lambda b, pos_ref: (0, b))
    pos = jnp.concatenate([_core_index_scalar(), _chip_index_scalar()])
    return pl.pallas_call(
        body, name=name, out_shape=(jax.ShapeDtypeStruct((h, cc), F32), jax.ShapeDtypeStruct(hs, BF16)),
        grid_spec=pltpu.PrefetchScalarGridSpec(
            num_scalar_prefetch=1, grid=(N_CHIPS,), in_specs=[own, other],
            out_specs=(pl.BlockSpec((h, cc), lambda b, pos_ref: (0, 0)), other)),
        compiler_params=_params(("arbitrary",)),
    )(pos, full, got)


def _rx_shape(part_shape, kind):
    if kind == "blk":
        return (3, part_shape[1], part_shape[2])
    return (3, part_shape[0], part_shape[1] // N_CHIPS)


class _ChipExchange:
    def __init__(self, parts, rxs, kinds, send_sems, recv_sems):
        self.parts, self.rxs, self.kinds = parts, rxs, kinds
        self.send_sems, self.recv_sems = send_sems, recv_sems
        self.x, self.y, self.c = _my_pos()
        self.chips = _other_chips(self.x, self.y)

    def _copies(self):
        for a in range(len(self.parts)):
            for j, chip in enumerate(self.chips):
                cj = 2 * chip[0] + chip[1]
                if self.kinds[a] == "blk":
                    src = self.parts[a].at[cj]
                else:
                    cc = self.parts[a].shape[1] // N_CHIPS
                    src = self.parts[a].at[:, pl.ds(cj * cc, cc)]
                yield pltpu.make_async_remote_copy(
                    src_ref=src, dst_ref=self.rxs[a].at[j], send_sem=self.send_sems.at[a * 3 + j],
                    recv_sem=self.recv_sems.at[a * 3 + j], device_id=(*chip, self.c), device_id_type=MESH)

    def start(self):
        for cp in self._copies():
            cp.start()

    def wait(self):
        for cp in self._copies():
            cp.wait_recv()
        for cp in self._copies():
            cp.wait_send()

    @staticmethod
    def sems(n_arr):
        return [pltpu.SemaphoreType.DMA((n_arr * 3,)), pltpu.SemaphoreType.DMA((n_arr * 3,))]


HBM_SPEC = pl.BlockSpec(memory_space=pltpu.HBM)
SEM_SPEC = pl.BlockSpec(memory_space=pltpu.SEMAPHORE)
DATAFLOW = pltpu.SideEffectType.DATAFLOW_SIDE_EFFECTING


def _exchange_start(part, name):
    rx_shape = _rx_shape(part.shape, "blk")

    def body(part_ref, rx_ref, send_sems, recv_sems, part_thru, rx_thru, token):
        _ChipExchange([part_ref], [rx_ref], ["blk"], send_sems, recv_sems).start()
        token[...] = jnp.zeros_like(token)

    return pl.pallas_call(
        body, name=name,
        out_shape=(pltpu.SemaphoreType.DMA((3,)), pltpu.SemaphoreType.DMA((3,)), pltpu.HBM(part.shape, part.dtype),
                   pltpu.HBM(rx_shape, BF16), jax.ShapeDtypeStruct((8, LANES), F32)),
        in_specs=(HBM_SPEC, HBM_SPEC), out_specs=(SEM_SPEC, SEM_SPEC, HBM_SPEC, HBM_SPEC, VMEM_WHOLE),
        input_output_aliases={0: 2, 1: 3}, compiler_params=pltpu.CompilerParams(has_side_effects=DATAFLOW),
    )(pltpu.with_memory_space_constraint(part, pltpu.HBM),
      pltpu.with_memory_space_constraint(lax.empty(rx_shape, BF16), pltpu.HBM))


def _exchange_wait(send_sems, recv_sems, part_thru, rx_thru, after, name):
    def body(part_ref, rx_ref, send_sems, recv_sems, after_ref, part_dead, rx_out):
        _ChipExchange([part_ref], [rx_ref], ["blk"], send_sems, recv_sems).wait()

    return pl.pallas_call(
        body, name=name,
        out_shape=(pltpu.HBM(part_thru.shape, part_thru.dtype), pltpu.HBM(rx_thru.shape, rx_thru.dtype)),
        in_specs=(HBM_SPEC, HBM_SPEC, SEM_SPEC, SEM_SPEC, ANY), out_specs=(HBM_SPEC, HBM_SPEC),
        input_output_aliases={0: 0, 1: 1}, compiler_params=pltpu.CompilerParams(has_side_effects=DATAFLOW),
    )(part_thru, rx_thru, send_sems, recv_sems, after)[1]


def _sum_chips(part, rx, tr, name, after):
    _, h, cc = rx.shape
    flips = (2, 1, 3)

    def body(chip_ref, p_ref, rx_ref, after_ref, o_ref):
        own = p_ref[...]
        for mc in range(N_CHIPS):
            @pl.when(chip_ref[0] == mc)
            def _():
                terms = sorted([(mc, None)] + [(mc ^ f, j) for j, f in enumerate(flips)])
                acc = None
                for _, j in terms:
                    t = own if j is None else rx_ref[j].astype(F32)
                    acc = t if acc is None else acc + t
                o_ref[...] = acc

    return pl.pallas_call(
        body, name=name, out_shape=jax.ShapeDtypeStruct((h, cc), F32),
        grid_spec=pltpu.PrefetchScalarGridSpec(
            num_scalar_prefetch=1, grid=(h // tr,),
            in_specs=[pl.BlockSpec((tr, cc), lambda i, chip_ref: (i, 0)),
                      pl.BlockSpec((3, tr, cc), lambda i, chip_ref: (0, i, 0)), ANY],
            out_specs=pl.BlockSpec((tr, cc), lambda i, chip_ref: (i, 0))),
        compiler_params=_params(("arbitrary",)),
    )(_chip_index_scalar(), part, rx, after)


def _share_halves(halves, name):
    n_arr = len(halves)

    def body(*refs):
        ins, outs = refs[:n_arr], refs[n_arr:2 * n_arr]
        send_sems, recv_sems = refs[2 * n_arr:]
        x, y, c = _my_pos()
        cps = []
        for a in range(n_arr):
            cp = pltpu.make_async_remote_copy(
                src_ref=ins[a], dst_ref=outs[a], send_sem=send_sems.at[a], recv_sem=recv_sems.at[a],
                device_id=(x, y, 1 - c), device_id_type=MESH)
            cp.start()
            cps.append(cp)
        for cp in cps:
            cp.wait()

    return pl.pallas_call(
        body, name=name, out_shape=[jax.ShapeDtypeStruct(h.shape, h.dtype) for h in halves],
        in_specs=[ANY] * n_arr, out_specs=[ANY] * n_arr,
        scratch_shapes=[pltpu.SemaphoreType.DMA((n_arr,)), pltpu.SemaphoreType.DMA((n_arr,))],
    )(*halves)


def _bucket_table():
    qi = jnp.arange(BLOCK)[:, None]
    si = jnp.arange(2 * BLOCK)[None, :]
    dist = qi + BLOCK - si
    max_exact = N_BUCKETS // 2
    n = jnp.maximum(dist, 0)
    nf = jnp.maximum(n, max_exact).astype(F32)
    large = max_exact + (jnp.log(nf / max_exact) / math.log(MAX_DISTANCE / max_exact)
                         * (N_BUCKETS - max_exact)).astype(jnp.int32)
    large = jnp.minimum(large, N_BUCKETS - 1)
    return jnp.where(n < max_exact, n, large).astype(F32)


def _prep_tables(bucket, rel_bias_t, w_s):
    def body(bucket_ref, rb_ref, ws_ref, bias_ref, wsm_ref):
        qi = lax.broadcasted_iota(jnp.int32, (BLOCK, 2 * BLOCK), 0)
        si = lax.broadcasted_iota(jnp.int32, (BLOCK, 2 * BLOCK), 1)
        dist = qi + BLOCK - si
        in_window = (dist >= 0) & (dist < BLOCK)
        bk = bucket_ref[...]
        for h in range(N_HEADS):
            acc = jnp.zeros((BLOCK, 2 * BLOCK), F32)
            for b in range(N_BUCKETS):
                acc = jnp.where(bk == float(b), rb_ref[h, b], acc)
            bias_ref[h] = jnp.where(in_window, acc, NEG_INF)
        ti = lax.broadcasted_iota(jnp.int32, (BLOCK, BLOCK), 0)
        ui = lax.broadcasted_iota(jnp.int32, (BLOCK, BLOCK), 1)
        for g in range(N_GROUPS):
            wsm_ref[g] = jnp.where(ti >= ui, ws_ref[g], 0.0).astype(BF16)

    return pl.pallas_call(
        body, name="prep_tables",
        out_shape=(jax.ShapeDtypeStruct((N_HEADS, BLOCK, 2 * BLOCK), F32),
                   jax.ShapeDtypeStruct((N_GROUPS, BLOCK, BLOCK), BF16)),
        grid=(1,),
        in_specs=[_const_spec((BLOCK, 2 * BLOCK)), pl.BlockSpec(memory_space=pltpu.SMEM),
                  _const_spec((N_GROUPS, BLOCK, BLOCK))],
        out_specs=(_const_spec((N_HEADS, BLOCK, 2 * BLOCK)), _const_spec((N_GROUPS, BLOCK, BLOCK))),
        compiler_params=_params(("arbitrary",)),
    )(bucket, rel_bias_t, w_s)


def _fwd_in(x, modr, w_in, b_in, tm, shards, kinds):
    s = x.shape[0]
    n_steps = s // tm
    fwd_step, diag_step = (8 * n_steps) // 16, (13 * n_steps) // 16
    n_w = len(shards)

    def body(x_ref, mod_ref, w_ref, b_ref, *rest):
        shard_refs = rest[:n_w]
        h1_ref, q_ref, kv_ref, gu_ref, gv_ref, xb_ref = rest[n_w:n_w + 6]
        gathered_refs = rest[n_w + 6:2 * n_w + 6]
        send_sems, recv_sems = rest[2 * n_w + 6:]
        i = pl.program_id(0)
        gather = _WeightGather(shard_refs, gathered_refs, kinds, send_sems, recv_sems)

        @pl.when(i == 0)
        def _():
            gather.start()

        xv = x_ref[...]
        xb_ref[...] = xv.astype(BF16)
        h1 = (xv * (1.0 + mod_ref[1:2, :]) + mod_ref[0:1, :]).astype(BF16)
        h1_ref[...] = h1
        proj = _dot_nt(h1, w_ref[...]) + b_ref[...]
        q_ref[...] = (proj[:, :ATTN_W] * Q_SCALE).astype(BF16)
        kv_ref[...] = proj[:, ATTN_W:ATTN_W + 2 * KV_W].astype(BF16)
        gu_ref[...] = proj[:, ATTN_W + 2 * KV_W:ATTN_W + 2 * KV_W + GMLP_W]
        gv_ref[...] = proj[:, ATTN_W + 2 * KV_W + GMLP_W:]

        @pl.when(i == fwd_step)
        def _():
            gather.forward()

        @pl.when(i == diag_step)
        def _():
            gather.forward_diagonal()

        @pl.when(i == n_steps - 1)
        def _():
            gather.finish()

    row = lambda w: pl.BlockSpec((tm, w), lambda i: (i, 0))
    outs = pl.pallas_call(
        body, name="fwd_in",
        out_shape=[jax.ShapeDtypeStruct((s, D_MODEL), BF16), jax.ShapeDtypeStruct((s, ATTN_W), BF16),
                   jax.ShapeDtypeStruct((s, 2 * KV_W), BF16), jax.ShapeDtypeStruct((s, GMLP_W), F32),
                   jax.ShapeDtypeStruct((s, GMLP_W), F32), jax.ShapeDtypeStruct((s, D_MODEL), BF16)]
        + [jax.ShapeDtypeStruct(_gathered_shape(sh, k), BF16) for sh, k in zip(shards, kinds)],
        grid=(n_steps,),
        in_specs=[row(D_MODEL), _const_spec((8, D_MODEL)), _const_spec(w_in.shape), _const_spec((1, IN_W))]
        + [ANY] * n_w,
        out_specs=[row(D_MODEL), row(ATTN_W), row(2 * KV_W), row(GMLP_W), row(GMLP_W), row(D_MODEL)] + [ANY] * n_w,
        scratch_shapes=_WeightGather.sems(n_w),
        compiler_params=_params(("arbitrary",)),
    )(x, modr, w_in, b_in, *shards)
    return outs[:6], outs[6:]


def _kv_variants(kk):
    kf = kk.astype(F32)
    lane = lax.broadcasted_iota(jnp.int32, kf.shape, 1)
    low = lane < HEAD_DIM
    k0_lo = jnp.where(low, kf, 0.0)
    k1_hi = jnp.where(low, 0.0, kf)
    k0_hi = pltpu.roll(k0_lo, HEAD_DIM, 1)
    k1_lo = pltpu.roll(k1_hi, HEAD_DIM, 1)
    return ((k0_lo.astype(BF16), k0_hi.astype(BF16)), (k1_lo.astype(BF16), k1_hi.astype(BF16)))


def _head_kv(h):
    return h // (N_HEADS // N_KV), h % 2


MIX_GROUP = 2


def _interleave(*gens):
    results = [None] * len(gens)
    active = list(enumerate(gens))
    while active:
        still = []
        for i, g in active:
            try:
                next(g)
                still.append((i, g))
            except StopIteration as done:
                results[i] = done.value
        active = still
    return results


def _attn_block_fwd(q_blk, kk, vv, bias_ref, sinks_ref, first_mask):
    kvar = _kv_variants(kk)
    vvar = _kv_variants(vv)
    heads = range(N_HEADS)
    q_pairs = [q_blk[:, (h // 2) * LANES:(h // 2 + 1) * LANES] for h in heads]
    logits = [_dot_nt(q_pairs[h], kvar[_head_kv(h)[0]][_head_kv(h)[1]]) + bias_ref[h] for h in heads]
    if first_mask is not None:
        logits = [jnp.where(first_mask, NEG_INF, lg) for lg in logits]
    yield
    ms = [jnp.maximum(jnp.max(logits[h], axis=-1, keepdims=True), sinks_ref[h]) for h in heads]
    yield
    es = [jnp.exp(logits[h] - ms[h]) for h in heads]
    ess = [jnp.exp(sinks_ref[h] - ms[h]) for h in heads]
    yield
    invs = [1.0 / (jnp.sum(es[h], axis=-1, keepdims=True) + ess[h]) for h in heads]
    probs = [(es[h] * invs[h], ess[h] * invs[h]) for h in heads]
    yield
    outs = [_dot(probs[h][0].astype(BF16), vvar[_head_kv(h)[0]][_head_kv(h)[1]]) for h in heads]
    pairs = [outs[2 * i] + outs[2 * i + 1] for i in range(N_HEADS // 2)]
    return jnp.concatenate(pairs, axis=1), probs, kvar, vvar


def _gmlp_chunk_fwd(gu, gv, ln_g, ln_b, wsm_ref, bsx, amat):
    u, tu = _gelu(gu)
    a, ta = _gelu(gv)
    yield
    mean = _split_dot(a, amat)
    d = a - mean
    yield
    var = _split_dot(d * d, amat)
    yield
    rstd = lax.rsqrt(var + LN_EPS)
    xhat = d * rstd
    vb = (xhat * ln_g + ln_b).astype(BF16)
    yield
    lane = lax.broadcasted_iota(jnp.int32, (BLOCK, LANES), 1)
    low = lane < GROUP_DIM
    cols = []
    for pair in range(N_GROUPS // 2):
        vp = vb[:, pair * LANES:(pair + 1) * LANES]
        cols.append(jnp.where(low, _dot(wsm_ref[2 * pair], vp), _dot(wsm_ref[2 * pair + 1], vp)))
    mixedv = jnp.concatenate(cols, axis=1) + bsx
    return u * mixedv, (u, tu, ta, xhat, rstd, vb, mixedv)


def _rms(a, g):
    r = lax.rsqrt(jnp.mean(a * a, axis=-1, keepdims=True) + LN_EPS)
    return a * r * g, r


def _fwd_mix(q, kv, gu, gv, x, modr, bias, sinks, gln_g, gln_b, wsm, bsx, amat, aog, gog, w_out, ln1_g, ln1_b, tm,
             ffn_shards, ffn_kinds):
    s = x.shape[0]
    nb = tm // BLOCK
    n_steps = s // tm
    fwd_step, diag_step = (7 * n_steps) // 16, (12 * n_steps) // 16
    n_w = len(ffn_shards)

    def body(q_ref, kv_ref, kvp_ref, gu_ref, gv_ref, x_ref, mod_ref, bias_ref, sinks_ref, glng_ref, glnb_ref, wsm_ref,
             bsx_ref, amat_ref, aog_ref, gog_ref, wout_ref, ln1g_ref, ln1b_ref, *rest):
        shard_refs = rest[:n_w]
        x1_ref, x1b_ref, y_ref, mixed_ref = rest[n_w:n_w + 4]
        gathered_refs = rest[n_w + 4:2 * n_w + 4]
        mix_scr, send_sems, recv_sems = rest[2 * n_w + 4:]
        i = pl.program_id(0)
        gather = _WeightGather(shard_refs, gathered_refs, ffn_kinds, send_sems, recv_sems)

        @pl.when(i == 0)
        def _():
            gather.start()

        col = lax.broadcasted_iota(jnp.int32, (BLOCK, 2 * BLOCK), 1)
        for b0 in range(0, nb, MIX_GROUP):
            gens = []
            for b in range(b0, min(b0 + MIX_GROUP, nb)):
                r0 = b * BLOCK
                if b == 0:
                    kvprev = kvp_ref[...]
                    first_mask = (col < BLOCK) & (i == 0)
                else:
                    kvprev = kv_ref[r0 - BLOCK:r0, :]
                    first_mask = None
                kvcur = kv_ref[r0:r0 + BLOCK, :]
                kk = jnp.concatenate([kvprev[:, :KV_W], kvcur[:, :KV_W]], axis=0)
                vv = jnp.concatenate([kvprev[:, KV_W:], kvcur[:, KV_W:]], axis=0)
                gens.append(_attn_block_fwd(q_ref[r0:r0 + BLOCK, :], kk, vv, bias_ref, sinks_ref, first_mask))
                gens.append(_gmlp_chunk_fwd(gu_ref[r0:r0 + BLOCK, :], gv_ref[r0:r0 + BLOCK, :], glng_ref[...],
                                            glnb_ref[...], wsm_ref, bsx_ref[...], amat_ref[...]))
            res = _interleave(*gens)
            for k, b in enumerate(range(b0, min(b0 + MIX_GROUP, nb))):
                r0 = b * BLOCK
                na, _ = _rms(res[2 * k][0], aog_ref[...])
                ng, _ = _rms(res[2 * k + 1][0], gog_ref[...])
                mix_scr[r0:r0 + BLOCK, :ATTN_W] = na.astype(BF16)
                mix_scr[r0:r0 + BLOCK, ATTN_W:] = ng.astype(BF16)
        mixed = mix_scr[...]
        mixed_ref[...] = mixed
        y = _dot(mixed, wout_ref[...])
        y_ref[...] = y.astype(BF16)
        z1 = ALPHA * x_ref[...] + mod_ref[2:3, :] * y
        xhat, _ = _ln_stats(z1)
        x1 = xhat * ln1g_ref[...] + ln1b_ref[...]
        x1_ref[...] = x1
        x1b_ref[...] = x1.astype(BF16)

        @pl.when(i == fwd_step)
        def _():
            gather.forward()

        @pl.when(i == diag_step)
        def _():
            gather.forward_diagonal()

        @pl.when(i == n_steps - 1)
        def _():
            gather.finish()

    row = lambda w: pl.BlockSpec((tm, w), lambda i: (i, 0))
    prev = pl.BlockSpec((BLOCK, 2 * KV_W), lambda i: (jnp.maximum(i * nb - 1, 0), 0))
    outs = pl.pallas_call(
        body, name="fwd_mix",
        out_shape=[jax.ShapeDtypeStruct((s, D_MODEL), F32)] + [jax.ShapeDtypeStruct((s, D_MODEL), BF16)] * 3
        + [jax.ShapeDtypeStruct(_gathered_shape(sh, k), BF16) for sh, k in zip(ffn_shards, ffn_kinds)],
        grid=(n_steps,),
        in_specs=[row(ATTN_W), row(2 * KV_W), prev, row(GMLP_W), row(GMLP_W), row(D_MODEL), _const_spec((8, D_MODEL)),
                  _const_spec((N_HEADS, BLOCK, 2 * BLOCK)), pl.BlockSpec(memory_space=pltpu.SMEM),
                  _const_spec((1, GMLP_W)), _const_spec((1, GMLP_W)), _const_spec((N_GROUPS, BLOCK, BLOCK)),
                  _const_spec((BLOCK, GMLP_W)), _const_spec((GMLP_W, GMLP_W)), _const_spec((1, ATTN_W)),
                  _const_spec((1, GMLP_W)), _const_spec((D_MODEL, D_MODEL)), _const_spec((1, D_MODEL)),
                  _const_spec((1, D_MODEL))] + [ANY] * n_w,
        out_specs=[row(D_MODEL)] * 4 + [ANY] * n_w,
        scratch_shapes=[pltpu.VMEM((tm, D_MODEL), BF16)] + _WeightGather.sems(n_w),
        compiler_params=_params(("arbitrary",)),
    )(q, kv, kv, gu, gv, x, modr, bias, sinks, gln_g, gln_b, wsm, bsx, amat, aog, gog, w_out, ln1_g, ln1_b, *ffn_shards)
    return outs[:4], outs[4:]


FF_BLOCKS = N_CHIPS // 2
FF_CHUNK = D_FF // FF_BLOCKS
FFN_SUB = 256


def _sigmoid(x):
    return 1.0 / (1.0 + jnp.exp(-x))


def _fwd_ffn(x1, target, modr, ln2_g, ln2_b, w_gu, w_dn, tm):
    s = x1.shape[0]

    def body(x1_ref, t_ref, mod_ref, g_ref, b_ref, wgu_ref, wdn_ref, h2_ref, act_ref, dy2_ref, dx1a_ref, acc_ref):
        @pl.when(pl.program_id(0) == 0)
        def _():
            acc_ref[...] = jnp.zeros_like(acc_ref)

        x1v = x1_ref[...]
        h2 = (x1v * (1.0 + mod_ref[4:5, :]) + mod_ref[3:4, :]).astype(BF16)
        h2_ref[...] = h2
        y2 = None
        for cc in range(FF_BLOCKS):
            c0 = cc * FF_CHUNK
            gate = _dot(h2, wgu_ref[cc])
            up = _dot(h2, wgu_ref[FF_BLOCKS + cc])
            act_ref[:, c0:c0 + FF_CHUNK] = gate.astype(BF16)
            act_ref[:, D_FF + c0:D_FF + c0 + FF_CHUNK] = up.astype(BF16)
            a = (gate * _sigmoid(gate) * up).astype(BF16)
            part = _dot(a, wdn_ref[c0:c0 + FF_CHUNK, :])
            y2 = part if y2 is None else y2 + part
        g2 = mod_ref[5:6, :]
        z2 = ALPHA * x1v + g2 * y2
        xhat, rstd = _ln_stats(z2)
        gain = g_ref[...]
        diff = xhat * gain + b_ref[...] - t_ref[...]
        dx2 = diff * (1.0 / D_MODEL)
        dz2 = _ln_bwd(dx2 * gain, xhat, rstd)
        dx1a_ref[...] = ALPHA * dz2
        dy2_ref[...] = (g2 * dz2).astype(BF16)
        acc_ref[0:1, :] += _colsum(diff * diff)
        acc_ref[1:2, :] += _colsum(dx2 * xhat)
        acc_ref[2:3, :] += _colsum(dx2)
        acc_ref[3:4, :] += _colsum(dz2 * y2)

    row = lambda w: pl.BlockSpec((tm, w), lambda i: (i, 0))
    return pl.pallas_call(
        body, name="fwd_ffn",
        out_shape=(jax.ShapeDtypeStruct((s, D_MODEL), BF16), jax.ShapeDtypeStruct((s, 2 * D_FF), BF16),
                   jax.ShapeDtypeStruct((s, D_MODEL), BF16), jax.ShapeDtypeStruct((s, D_MODEL), F32),
                   jax.ShapeDtypeStruct((8, D_MODEL), F32)),
        grid=(s // tm,),
        in_specs=[row(D_MODEL), row(D_MODEL), _const_spec((8, D_MODEL)), _const_spec((1, D_MODEL)),
                  _const_spec((1, D_MODEL)), _const_spec((N_CHIPS, D_MODEL, FF_CHUNK), single=True),
                  _const_spec((D_FF, D_MODEL), single=True)],
        out_specs=(row(D_MODEL), row(2 * D_FF), row(D_MODEL), row(D_MODEL), _const_spec((8, D_MODEL))),
        compiler_params=_params(("arbitrary",)),
    )(x1, target, modr, ln2_g, ln2_b, w_gu, w_dn)


def _bwd_ffn(dy2, act, w_gu, w_dn, tm):
    s = dy2.shape[0]

    def body(dy2_ref, act_ref, wgu_ref, wdn_ref, a_ref, dgu_ref, dh2_ref):
        dy2v = dy2_ref[...]
        dh2 = None
        for cc in range(FF_BLOCKS):
            c0 = cc * FF_CHUNK
            da = _dot_nt(dy2v, wdn_ref[c0:c0 + FF_CHUNK, :])
            gate = act_ref[:, c0:c0 + FF_CHUNK].astype(F32)
            up = act_ref[:, D_FF + c0:D_FF + c0 + FF_CHUNK].astype(F32)
            sg = _sigmoid(gate)
            sl = gate * sg
            a_ref[:, c0:c0 + FF_CHUNK] = (sl * up).astype(BF16)
            dgate = (da * up * (sg * (1.0 + gate * (1.0 - sg)))).astype(BF16)
            dup = (da * sl).astype(BF16)
            dgu_ref[:, c0:c0 + FF_CHUNK] = dgate
            dgu_ref[:, D_FF + c0:D_FF + c0 + FF_CHUNK] = dup
            part = _dot_nt(dgate, wgu_ref[cc]) + _dot_nt(dup, wgu_ref[FF_BLOCKS + cc])
            dh2 = part if dh2 is None else dh2 + part
        dh2_ref[...] = dh2.astype(BF16)

    row = lambda w: pl.BlockSpec((tm, w), lambda i: (i, 0))
    return pl.pallas_call(
        body, name="bwd_ffn",
        out_shape=(jax.ShapeDtypeStruct((s, D_FF), BF16), jax.ShapeDtypeStruct((s, 2 * D_FF), BF16),
                   jax.ShapeDtypeStruct((s, D_MODEL), BF16)),
        grid=(s // tm,),
        in_specs=[row(D_MODEL), row(2 * D_FF), _const_spec((N_CHIPS, D_MODEL, FF_CHUNK), single=True),
                  _const_spec((D_FF, D_MODEL), single=True)],
        out_specs=(row(D_FF), row(2 * D_FF), row(D_MODEL)),
        compiler_params=_params(("parallel",)),
    )(dy2, act, w_gu, w_dn)


def _bwd_mid(dh2, dx1a, x1, x, y, modr, ln1_g, w_out, tm, swap_fulls, swap_kinds):
    s = x.shape[0]
    n_steps = s // tm
    n_g = len(swap_fulls)

    def body(dh2_ref, dx1a_ref, x1_ref, x_ref, y_ref, mod_ref, g_ref, wout_ref, *rest):
        full_refs = rest[:n_g]
        dxa_ref, dy_ref, dmix_ref, acc_ref = rest[n_g:n_g + 4]
        got_refs = rest[n_g + 4:2 * n_g + 4]
        swap = _HalfSwap(full_refs, got_refs, swap_kinds, *rest[2 * n_g + 4:])
        i = pl.program_id(0)

        @pl.when(i == 0)
        def _():
            swap.start()
            acc_ref[...] = jnp.zeros_like(acc_ref)

        dh2 = dh2_ref[...].astype(F32)
        x1v = x1_ref[...].astype(F32)
        yv = y_ref[...].astype(F32)
        g1 = mod_ref[2:3, :]
        dx1 = dx1a_ref[...] + dh2 * (1.0 + mod_ref[4:5, :])
        z1 = ALPHA * x_ref[...] + g1 * yv
        xhat, rstd = _ln_stats(z1)
        dz1 = _ln_bwd(dx1 * g_ref[...], xhat, rstd)
        dxa_ref[...] = (ALPHA * dz1).astype(BF16)
        dy = (g1 * dz1).astype(BF16)
        dy_ref[...] = dy
        dmix_ref[...] = _dot_nt(dy, wout_ref[...]).astype(BF16)
        acc_ref[0:1, :] += _colsum(dh2 * x1v)
        acc_ref[1:2, :] += _colsum(dh2)
        acc_ref[2:3, :] += _colsum(dx1 * xhat)
        acc_ref[3:4, :] += _colsum(dx1)
        acc_ref[4:5, :] += _colsum(dz1 * yv)

        @pl.when(i == n_steps - 1)
        def _():
            swap.wait()

    row = lambda w: pl.BlockSpec((tm, w), lambda i: (i, 0))
    outs = pl.pallas_call(
        body, name="bwd_mid",
        out_shape=[jax.ShapeDtypeStruct((s, D_MODEL), BF16), jax.ShapeDtypeStruct((s, D_MODEL), BF16),
                   jax.ShapeDtypeStruct((s, D_MODEL), BF16), jax.ShapeDtypeStruct((8, D_MODEL), F32)]
        + _HalfSwap.out_shapes(swap_fulls, swap_kinds),
        grid=(n_steps,),
        in_specs=[row(D_MODEL)] * 5 + [_const_spec((8, D_MODEL)), _const_spec((1, D_MODEL)),
                                       _const_spec((D_MODEL, D_MODEL))] + [ANY] * n_g,
        out_specs=[row(D_MODEL), row(D_MODEL), row(D_MODEL), _const_spec((8, D_MODEL))] + [ANY] * n_g,
        scratch_shapes=_HalfSwap.sems(n_g),
        compiler_params=_params(("arbitrary",)),
    )(dh2, dx1a, x1, x, y, modr, ln1_g, w_out, *swap_fulls)
    return outs[:4], outs[4:]


def _fold_kv(t0, t1):
    lane = lax.broadcasted_iota(jnp.int32, t0.shape, 1)
    f0 = t0 + pltpu.roll(t0, HEAD_DIM, 1)
    f1 = t1 + pltpu.roll(t1, HEAD_DIM, 1)
    return jnp.where(lane < HEAD_DIM, f0, f1)


def _bwd_mix(q, kv, gu, gv, dmix, bias, sinks, gln_g, gln_b, wsm, bsx, amat, aog, gog, grad_parts, grad_kinds):
    s = q.shape[0]
    tile = 2 * BLOCK
    n_steps = s // tile
    n_g = len(grad_parts)

    def body(q_ref, kv_ref, kvp_ref, gu_ref, gv_ref, dmix_ref, bias_ref, sinks_ref, glng_ref, glnb_ref, wsm_ref,
             bsx_ref, amat_ref, aog_ref, gog_ref, *rest):
        part_refs = rest[:n_g]
        dq_ref, dkv_ref, dgu_ref, dgv_ref, gbias_ref, dws_ref, dbs_ref, vec_ref, dsink_ref = rest[n_g:n_g + 9]
        rx_refs = rest[n_g + 9:2 * n_g + 9]
        carry, done, send_sems, recv_sems = rest[2 * n_g + 9:]
        n = pl.program_id(0)
        exchange = _ChipExchange(part_refs, rx_refs, grad_kinds, send_sems, recv_sems)

        @pl.when(n == 0)
        def _():
            exchange.start()
            carry[...] = jnp.zeros_like(carry)
            done[...] = jnp.zeros_like(done)
            gbias_ref[...] = jnp.zeros_like(gbias_ref)
            dws_ref[...] = jnp.zeros_like(dws_ref)
            dbs_ref[...] = jnp.zeros_like(dbs_ref)
            vec_ref[...] = jnp.zeros_like(vec_ref)
            dsink_ref[...] = jnp.zeros_like(dsink_ref)

        @pl.when(n == n_steps)
        def _():
            dkv_ref[:BLOCK, :] = done[...].astype(BF16)
            dkv_ref[BLOCK:, :] = carry[...].astype(BF16)
            exchange.wait()

        @pl.when(n < n_steps)
        def _():
            col = lax.broadcasted_iota(jnp.int32, (BLOCK, 2 * BLOCK), 1)
            lane = lax.broadcasted_iota(jnp.int32, (BLOCK, LANES), 1)
            low = lane < HEAD_DIM
            rows = [slice(0, BLOCK), slice(BLOCK, tile)]
            kv_blocks = [kvp_ref[...], kv_ref[rows[0], :], kv_ref[rows[1], :]]
            masks = [(col < BLOCK) & (n == 0), None]
            q_blks = [q_ref[r, :] for r in rows]
            fwd = []
            for b in range(2):
                kk = jnp.concatenate([kv_blocks[b][:, :KV_W], kv_blocks[b + 1][:, :KV_W]], axis=0)
                vv = jnp.concatenate([kv_blocks[b][:, KV_W:], kv_blocks[b + 1][:, KV_W:]], axis=0)
                fwd.append(_attn_block_fwd(q_blks[b], kk, vv, bias_ref, sinks_ref, masks[b]))
                fwd.append(_gmlp_chunk_fwd(gu_ref[rows[b], :], gv_ref[rows[b], :], glng_ref[...], glnb_ref[...],
                                           wsm_ref, bsx_ref[...], amat_ref[...]))
            res = _interleave(*fwd[:2]) + _interleave(*fwd[2:])

            def gating_bwd(b, d_gm, saved):
                u, tu, ta, xhat, rstd, vb, mixedv = saved
                dgu_ref[rows[b], :] = (d_gm * mixedv * _gelu_grad(gu_ref[rows[b], :], tu)).astype(BF16)
                dmx = d_gm * u
                dmxb = dmx.astype(BF16)
                yield
                dvn_cols, dws = [], []
                for pair in range(N_GROUPS // 2):
                    dp_ = dmxb[:, pair * LANES:(pair + 1) * LANES]
                    vp = vb[:, pair * LANES:(pair + 1) * LANES]
                    dvn_cols.append(
                        jnp.where(low, _dot_tn(wsm_ref[2 * pair], dp_), _dot_tn(wsm_ref[2 * pair + 1], dp_)))
                    zero = jnp.zeros_like(dp_)
                    dws.append(_dot_nt(jnp.where(low, dp_, zero), vp))
                    dws.append(_dot_nt(jnp.where(low, zero, dp_), vp))
                dvn = jnp.concatenate(dvn_cols, axis=1)
                yield
                dxh = dvn * glng_ref[...]
                am = amat_ref[...]
                m1 = _split_dot(dxh, am)
                m2 = _split_dot(dxh * xhat, am)
                yield
                da = rstd * (dxh - m1 - xhat * m2)
                dgv_ref[rows[b], :] = (da * _gelu_grad(gv_ref[rows[b], :], ta)).astype(BF16)
                return dmx, dws, _colsum(dvn * xhat), _colsum(dvn)

            def attention_bwd(b, d_attn, probs, kvar, vvar):
                heads = range(N_HEADS)
                sels = [low if h % 2 == 0 else jnp.logical_not(low) for h in heads]
                pair_of = lambda a, h: a[:, (h // 2) * LANES:(h // 2 + 1) * LANES]
                do_hs = [jnp.where(sels[h], pair_of(d_attn, h), 0.0).astype(BF16) for h in heads]
                q_hs = [jnp.where(sels[h], pair_of(q_blks[b], h), jnp.zeros((BLOCK, LANES), BF16)) for h in heads]
                dps = [_dot_nt(do_hs[h], vvar[_head_kv(h)[0]][_head_kv(h)[1]]) for h in heads]
                yield
                deltas = [jnp.sum(probs[h][0] * dps[h], axis=-1, keepdims=True) for h in heads]
                yield
                dss = [probs[h][0] * (dps[h] - deltas[h]) for h in heads]
                dsinks = [-(probs[h][1] * deltas[h]) for h in heads]
                dsbs = [ds.astype(BF16) for ds in dss]
                pbs = [probs[h][0].astype(BF16) for h in heads]
                yield
                dqs = [_dot(dsbs[h], kvar[_head_kv(h)[0]][_head_kv(h)[1]]) for h in heads]
                tks = [_dot_tn(dsbs[h], q_hs[h]) for h in heads]
                tvs = [_dot_tn(pbs[h], do_hs[h]) for h in heads]
                dq_cols = [dqs[2 * i] + dqs[2 * i + 1] for i in range(N_HEADS // 2)]
                dq_ref[rows[b], :] = (jnp.concatenate(dq_cols, axis=1) * Q_SCALE).astype(BF16)
                per_kv = N_HEADS // N_KV
                kv_sum = lambda ts, kvh: sum(ts[kvh * per_kv + 1:(kvh + 1) * per_kv], ts[kvh * per_kv])
                dkk = _fold_kv(kv_sum(tks, 0), kv_sum(tks, 1))
                dvv = _fold_kv(kv_sum(tvs, 0), kv_sum(tvs, 1))
                return jnp.concatenate([dkk, dvv], axis=1), dss, dsinks

            bwd, rms_g = [], []
            for b in range(2):
                attn, probs, kvar, vvar = res[2 * b]
                gm, saved = res[2 * b + 1]
                na_unit, r_a = _rms(attn, 1.0)
                ng_unit, r_g = _rms(gm, 1.0)
                dmix = dmix_ref[rows[b], :].astype(F32)
                dn_a = dmix[:, :ATTN_W]
                dn_g = dmix[:, ATTN_W:]
                rms_g.append((_colsum(dn_a * na_unit), _colsum(dn_g * ng_unit)))
                t_a = dn_a * aog_ref[...]
                d_attn = r_a * t_a - na_unit * (r_a * jnp.mean(t_a * na_unit, axis=-1, keepdims=True))
                t_g = dn_g * gog_ref[...]
                d_gm = r_g * t_g - ng_unit * (r_g * jnp.mean(t_g * ng_unit, axis=-1, keepdims=True))
                bwd.append(attention_bwd(b, d_attn, probs, kvar, vvar))
                bwd.append(gating_bwd(b, d_gm, saved))
            (dkv_a, dss_a, dsk_a), (dmx_a, dws_a, glg_a, glb_a) = _interleave(*bwd[:2])
            (dkv_b, dss_b, dsk_b), (dmx_b, dws_b, glg_b, glb_b) = _interleave(*bwd[2:])

            vec_ref[0:1, :] += rms_g[0][0] + rms_g[1][0]
            vec_ref[1:2, :] += rms_g[0][1] + rms_g[1][1]
            vec_ref[2:3, :] += glg_a + glg_b
            vec_ref[3:4, :] += glb_a + glb_b
            dbs_ref[...] += dmx_a + dmx_b
            for g in range(N_GROUPS):
                dws_ref[g] += dws_a[g] + dws_b[g]
            for h in range(N_HEADS):
                gbias_ref[h] += dss_a[h] + dss_b[h]
                dsink_ref[h] += dsk_a[h] + dsk_b[h]

            dkv_ref[:BLOCK, :] = done[...].astype(BF16)
            dkv_ref[BLOCK:, :] = (carry[...] + dkv_a[:BLOCK]).astype(BF16)
            done[...] = dkv_a[BLOCK:] + dkv_b[:BLOCK]
            carry[...] = dkv_b[BLOCK:]

    last = n_steps - 1
    cur = lambda w: pl.BlockSpec((tile, w), lambda n: (jnp.minimum(n, last), 0))
    late = lambda w: pl.BlockSpec((tile, w), lambda n: (jnp.clip(n - 1, 0, last), 0))
    before = pl.BlockSpec((BLOCK, 2 * KV_W), lambda n: (jnp.clip(2 * n - 1, 0, 2 * last + 1), 0))
    outs = pl.pallas_call(
        body, name="bwd_mix",
        out_shape=[jax.ShapeDtypeStruct((s, ATTN_W), BF16), jax.ShapeDtypeStruct((s, 2 * KV_W), BF16),
                   jax.ShapeDtypeStruct((s, GMLP_W), BF16), jax.ShapeDtypeStruct((s, GMLP_W), BF16),
                   jax.ShapeDtypeStruct((N_HEADS, BLOCK, 2 * BLOCK), F32),
                   jax.ShapeDtypeStruct((N_GROUPS, BLOCK, BLOCK), F32),
                   jax.ShapeDtypeStruct((BLOCK, GMLP_W), F32), jax.ShapeDtypeStruct((8, GMLP_W), F32),
                   jax.ShapeDtypeStruct((N_HEADS, BLOCK, 1), F32)]
        + [jax.ShapeDtypeStruct(_rx_shape(p.shape, k), BF16) for p, k in zip(grad_parts, grad_kinds)],
        grid=(n_steps + 1,),
        in_specs=[cur(ATTN_W), cur(2 * KV_W), before, cur(GMLP_W), cur(GMLP_W), cur(D_MODEL),
                  _const_spec((N_HEADS, BLOCK, 2 * BLOCK)), pl.BlockSpec(memory_space=pltpu.SMEM),
                  _const_spec((1, GMLP_W)), _const_spec((1, GMLP_W)), _const_spec((N_GROUPS, BLOCK, BLOCK)),
                  _const_spec((BLOCK, GMLP_W)), _const_spec((GMLP_W, GMLP_W)), _const_spec((1, ATTN_W)),
                  _const_spec((1, GMLP_W))] + [ANY] * n_g,
        out_specs=[cur(ATTN_W), late(2 * KV_W), cur(GMLP_W), cur(GMLP_W),
                   _const_spec((N_HEADS, BLOCK, 2 * BLOCK)), _const_spec((N_GROUPS, BLOCK, BLOCK)),
                   _const_spec((BLOCK, GMLP_W)), _const_spec((8, GMLP_W)), _const_spec((N_HEADS, BLOCK, 1))]
        + [ANY] * n_g,
        scratch_shapes=[pltpu.VMEM((BLOCK, 2 * KV_W), F32), pltpu.VMEM((BLOCK, 2 * KV_W), F32)]
        + _ChipExchange.sems(n_g),
        compiler_params=_params(("arbitrary",)),
    )(q, kv, kv, gu, gv, dmix, bias, sinks, gln_g, gln_b, wsm, bsx, amat, aog, gog, *grad_parts)
    return outs[:9], outs[9:]


def _mix_finalize(gbias, bucket, dws, dbs, dsink):
    def body(gb_ref, bucket_ref, dws_ref, dbs_ref, dsink_ref, tall_ref):
        bk = bucket_ref[...]
        lane = lax.broadcasted_iota(jnp.int32, (N_BUCKETS, LANES), 1)
        rowi = lax.broadcasted_iota(jnp.int32, (N_BUCKETS, LANES), 0)
        drb = jnp.zeros((N_BUCKETS, LANES), F32)
        dsk = jnp.zeros((8, LANES), F32)
        lane8 = lax.broadcasted_iota(jnp.int32, (8, LANES), 1)
        for h in range(N_HEADS):
            g = gb_ref[h]
            for b in range(N_BUCKETS):
                tot = jnp.sum(_colsum(jnp.where(bk == float(b), g, 0.0)), axis=1, keepdims=True)
                drb = jnp.where((rowi == h) & (lane == b), tot, drb)
            sk = jnp.sum(dsink_ref[h], axis=0, keepdims=True)
            dsk = jnp.where(lane8 == h, sk, dsk)
        tall_ref[TALL_RB:TALL_RB + N_BUCKETS, :] = drb
        tall_ref[TALL_SK:TALL_SK + 8, :] = dsk
        ti = lax.broadcasted_iota(jnp.int32, (BLOCK, BLOCK), 0)
        ui = lax.broadcasted_iota(jnp.int32, (BLOCK, BLOCK), 1)
        for g in range(N_GROUPS):
            tall_ref[g * BLOCK:(g + 1) * BLOCK, :] = jnp.where(ti >= ui, dws_ref[g], 0.0)
        gi = lax.broadcasted_iota(jnp.int32, (GMLP_W, LANES), 0) // GROUP_DIM
        li = lax.broadcasted_iota(jnp.int32, (GMLP_W, LANES), 1)
        ind = jnp.where(gi == li, 1.0, 0.0).astype(BF16)
        d = dbs_ref[...]
        hi = d.astype(BF16)
        r1 = d - hi.astype(F32)
        mid = r1.astype(BF16)
        lo = (r1 - mid.astype(F32)).astype(BF16)
        dbsg = _dot(hi, ind) + _dot(mid, ind) + _dot(lo, ind)
        tall_ref[TALL_BS:TALL_BS + N_GROUPS, :] = dbsg.T[:N_GROUPS, :]

    return pl.pallas_call(
        body, name="mix_finalize", out_shape=jax.ShapeDtypeStruct((TALL_ROWS, LANES), F32), grid=(1,),
        in_specs=[_const_spec((N_HEADS, BLOCK, 2 * BLOCK)), _const_spec((BLOCK, 2 * BLOCK)),
                  _const_spec((N_GROUPS, BLOCK, BLOCK)), _const_spec((BLOCK, GMLP_W)),
                  _const_spec((N_HEADS, BLOCK, 1))],
        out_specs=_const_spec((TALL_ROWS, LANES)),
        compiler_params=_params(("arbitrary",)),
    )(gbias, bucket, dws, dbs, dsink)


def _bwd_in(dq, dkv, dgu, dgv, dxa, x, modr, w_in, tm):
    s = x.shape[0]

    def body(dq_ref, dkv_ref, dgu_ref, dgv_ref, dxa_ref, x_ref, mod_ref, w_ref, gx_ref, acc_ref, db_ref):
        @pl.when(pl.program_id(0) == 0)
        def _():
            acc_ref[...] = jnp.zeros_like(acc_ref)
            db_ref[...] = jnp.zeros_like(db_ref)

        dproj = jnp.concatenate([dq_ref[...], dkv_ref[...], dgu_ref[...], dgv_ref[...]], axis=1)
        dh1 = _dot(dproj, w_ref[...])
        gx_ref[...] = dxa_ref[...].astype(F32) + dh1 * (1.0 + mod_ref[1:2, :])
        acc_ref[0:1, :] += _colsum(dh1 * x_ref[...].astype(F32))
        acc_ref[1:2, :] += _colsum(dh1)
        db_ref[0:1, :] += _colsum(dproj.astype(F32))

    row = lambda w: pl.BlockSpec((tm, w), lambda i: (i, 0))
    return pl.pallas_call(
        body, name="bwd_in",
        out_shape=(jax.ShapeDtypeStruct((s, D_MODEL), F32), jax.ShapeDtypeStruct((8, D_MODEL), F32),
                   jax.ShapeDtypeStruct((8, IN_W), F32)),
        grid=(s // tm,),
        in_specs=[row(ATTN_W), row(2 * KV_W), row(GMLP_W), row(GMLP_W), row(D_MODEL), row(D_MODEL),
                  _const_spec((8, D_MODEL)), _const_spec(w_in.shape)],
        out_specs=(row(D_MODEL), _const_spec((8, D_MODEL)), _const_spec((8, IN_W))),
        compiler_params=_params(("arbitrary",)),
    )(dq, dkv, dgu, dgv, dxa, x, modr, w_in)


def _wgrad(a, bs, tm, tk, name, owner_blocks=False, gather_vs=()):
    k_all, m = a.shape
    n = sum(b.shape[1] for b in bs)
    nk = k_all // tk
    nm = m // tm
    n_b = len(bs)
    n_v = len(gather_vs)
    wb = n // N_CHIPS

    def body(a_ref, *rest):
        b_refs, v_refs = rest[:n_b], rest[n_b:n_b + n_v]
        o_ref, ob_ref = rest[n_b + n_v:n_b + n_v + 2]
        vg_refs = rest[n_b + n_v + 2:n_b + 2 * n_v + 2]
        i, k = pl.program_id(0), pl.program_id(1)
        if n_v:
            gather = _Gather8(v_refs, vg_refs, *rest[n_b + 2 * n_v + 2:])

            @pl.when((i == 0) & (k == 0))
            def _():
                gather.start()

            @pl.when((i == nm - 1) & (k == 0))
            def _():
                gather.forward()

        @pl.when(k == 0)
        def _():
            o_ref[...] = jnp.zeros_like(o_ref)

        b = b_refs[0][...] if n_b == 1 else jnp.concatenate([r[...] for r in b_refs], axis=1)
        if owner_blocks:
            av = a_ref[...]
            for j in range(N_CHIPS):
                o_ref[j] += _dot_tn(av, b[:, j * wb:(j + 1) * wb])
        else:
            o_ref[...] += _dot_tn(a_ref[...], b)

        @pl.when(k == nk - 1)
        def _():
            ob_ref[...] = o_ref[...].astype(BF16)

        if n_v:
            @pl.when((i == nm - 1) & (k == nk - 1))
            def _():
                gather.finish()

    if owner_blocks:
        out_spec = pl.BlockSpec((N_CHIPS, tm, wb), lambda i, k: (0, i, 0))
        shape = (N_CHIPS, m, wb)
    else:
        out_spec = pl.BlockSpec((tm, n), lambda i, k: (i, 0))
        shape = (m, n)
    outs = pl.pallas_call(
        body, name=name,
        out_shape=[jax.ShapeDtypeStruct(shape, F32), jax.ShapeDtypeStruct(shape, BF16)] + _gathered8_shapes(gather_vs),
        grid=(nm, nk),
        in_specs=[pl.BlockSpec((tk, tm), lambda i, k: (k, i))]
        + [pl.BlockSpec((tk, b.shape[1]), lambda i, k: (k, 0)) for b in bs] + [ANY] * n_v,
        out_specs=[out_spec, out_spec] + [ANY] * n_v,
        scratch_shapes=_Gather8.sems(n_v) if n_v else [],
        compiler_params=_params(("arbitrary", "arbitrary") if n_v else ("parallel", "arbitrary")),
    )(a, *bs, *gather_vs)
    return outs[0], outs[1], outs[2:]


def _adam_math(w, g, m, v):
    m2 = ADAM_B1 * m + (1.0 - ADAM_B1) * g
    v2 = ADAM_B2 * v + (1.0 - ADAM_B2) * (g * g)
    m_hat = m2 / (1.0 - ADAM_B1 ** ADAM_STEP)
    v_hat = v2 / (1.0 - ADAM_B2 ** ADAM_STEP)
    delta = -ADAM_LR * (m_hat / (jnp.sqrt(v_hat) + ADAM_EPS) + ADAM_WD * w)
    return delta, m2, v2


def _transposed(g):
    r, c = g.shape
    pieces = []
    for lo in range(0, c, LANES):
        width = min(LANES, c - lo)
        piece = g[:, lo:lo + width]
        if width < LANES:
            piece = jnp.concatenate([piece, jnp.zeros((r, LANES - width), g.dtype)], axis=1)
        pieces.append(piece.T[:width])
    return jnp.concatenate(pieces, axis=0)


def _adam_halves(w, mine, got, m, v, tr, name, transposed=False):
    r, cc = w.shape[::-1] if transposed else w.shape
    h = r // 2
    nt = h // tr

    def body(c_ref, w_ref, mine_ref, got_ref, m_ref, v_ref, g_ref, d_ref, m2_ref, v2_ref):
        g = jnp.where(pl.program_id(0) == c_ref[0], mine_ref[...], got_ref[...])
        if transposed:
            g = _transposed(g)
        g_ref[...] = g
        d, m2, v2 = _adam_math(w_ref[...], g, m_ref[...], v_ref[...])
        d_ref[...] = d
        m2_ref[...] = m2
        v2_ref[...] = v2

    if transposed:
        full = pl.BlockSpec((cc, tr), lambda hh, i, c_ref: (0, hh * nt + i))
    else:
        full = pl.BlockSpec((tr, cc), lambda hh, i, c_ref: (hh * nt + i, 0))
    half = pl.BlockSpec((tr, cc), lambda hh, i, c_ref: (i, 0))
    shp = jax.ShapeDtypeStruct(w.shape, F32)
    return pl.pallas_call(
        body, name=name, out_shape=(shp, shp, shp, shp),
        grid_spec=pltpu.PrefetchScalarGridSpec(
            num_scalar_prefetch=1, grid=(2, nt), in_specs=[full, half, half, full, full],
            out_specs=(full, full, full, full)),
        compiler_params=_params(("arbitrary", "arbitrary")),
    )(_core_index_scalar(), w, mine, got, m, v)


def _adam_w_ada(sc_t, dmod_all, w, m, v, tr):
    r, cc = w.shape

    def body(chip_ref, sct_ref, dm_ref, w_ref, m_ref, v_ref, g_ref, d_ref, m2_ref, v2_ref):
        g = sct_ref[:, 0:1] * dm_ref[0:1, :]
        for k in range(1, N_DEV):
            g = g + sct_ref[:, k:k + 1] * dm_ref[k:k + 1, :]
        g_ref[...] = g
        d, m2, v2 = _adam_math(w_ref[...], g, m_ref[...], v_ref[...])
        d_ref[...] = d
        m2_ref[...] = m2
        v2_ref[...] = v2

    spec = pl.BlockSpec((tr, cc), lambda i, chip_ref: (i, 0))
    shp = jax.ShapeDtypeStruct((r, cc), F32)
    return pl.pallas_call(
        body, name="adam_w_ada", out_shape=(shp, shp, shp, shp),
        grid_spec=pltpu.PrefetchScalarGridSpec(
            num_scalar_prefetch=1, grid=(r // tr,),
            in_specs=[pl.BlockSpec((tr, N_DEV), lambda i, chip_ref: (i, 0)),
                      pl.BlockSpec((N_DEV, cc), lambda i, chip_ref: (0, chip_ref[0])), spec, spec, spec],
            out_specs=(spec, spec, spec, spec)),
        compiler_params=_params(("parallel",)),
    )(_chip_index_scalar(), sc_t, dmod_all, w, m, v)


def _pack_wide(acc_i, acc_m, acc_f, db_in, vec):
    arrs = [acc_i, acc_m, acc_f, db_in, vec]
    i_, m_, f_, b_, v_ = range(5)
    src = {"b_in": (b_, 0), "ln1_g": (m_, 2), "ln1_b": (m_, 3), "ln2_g": (f_, 1), "ln2_b": (f_, 2),
           "gmlp_ln_g": (v_, 2), "gmlp_ln_b": (v_, 3), "attn_out_g": (v_, 0), "gmlp_out_g": (v_, 1), "loss": (f_, 0)}
    dmod = [(i_, 1), (i_, 0), (m_, 4), (m_, 1), (m_, 0), (f_, 3)]

    def body(*refs):
        ins, wide_ref = refs[:5], refs[5]
        wide_ref[...] = jnp.zeros_like(wide_ref)
        for k, (a, row) in enumerate(dmod):
            wide_ref[0:1, k * D_MODEL:(k + 1) * D_MODEL] = ins[a][row:row + 1, :]
        for name, (a, row) in src.items():
            r, off, n = WIDE_LAYOUT[name]
            wide_ref[r:r + 1, off:off + n] = ins[a][row:row + 1, :]

    return pl.pallas_call(
        body, name="pack_wide", out_shape=jax.ShapeDtypeStruct((8, WIDE_W), F32), grid=(1,),
        in_specs=[_const_spec(a.shape) for a in arrs], out_specs=_const_spec((8, WIDE_W)),
        compiler_params=_params(("arbitrary",)),
    )(*arrs)


def _adam_small(gw, gt, wide_wmv, w_s, b_s, rel_bias, sinks, after):
    names = list(WIDE_PARAMS)
    tall = [("gmlp_w_s", w_s), ("gmlp_b_s", b_s), ("rel_bias", rel_bias), ("attn_sinks", sinks)]
    ins = [gw, gt]
    for n in names:
        ins += list(wide_wmv[n])
    for _, t in tall:
        ins += list(t)
    n_in = len(ins)

    def body(*refs):
        gw_ref, gt_ref = refs[0], refs[1]
        wmv = refs[2:n_in]
        dmod_ref, loss_ref, loss1_ref = refs[n_in + 1:n_in + 4]
        outs = refs[n_in + 4:]

        def tall_sum(r0, nr):
            g = gt_ref[r0:r0 + nr, :]
            for d in range(1, N_DEV):
                g = g + gt_ref[d * TALL_ROWS + r0:d * TALL_ROWS + r0 + nr, :]
            return g

        def emit(k, g, w_ref, m_ref, v_ref):
            d, m2, v2 = _adam_math(w_ref[...], g, m_ref[...], v_ref[...])
            outs[4 * k][...] = g
            outs[4 * k + 1][...] = d
            outs[4 * k + 2][...] = m2
            outs[4 * k + 3][...] = v2

        gsum = gw_ref[0:8, :]
        for d in range(1, N_DEV):
            gsum = gsum + gw_ref[8 * d:8 * d + 8, :]
        for d in range(N_DEV):
            dmod_ref[d:d + 1, :] = gw_ref[8 * d:8 * d + 1, :]
        for k, n in enumerate(names):
            r, off, sz = WIDE_LAYOUT[n]
            emit(k, gsum[r:r + 1, off:off + sz], *wmv[3 * k:3 * k + 3])
        r, off, sz = WIDE_LAYOUT["loss"]
        tot = jnp.sum(gsum[r:r + 1, off:off + sz], axis=1, keepdims=True)
        loss_ref[...] = jnp.broadcast_to(tot * (0.5 / D_MODEL), loss_ref.shape)
        loss1_ref[...] = tot * (0.5 / D_MODEL)

        k0 = len(names)
        ws_refs = wmv[3 * k0:3 * k0 + 3]
        for g in range(N_GROUPS):
            rows = slice(g * BLOCK, (g + 1) * BLOCK)
            gg = tall_sum(g * BLOCK, BLOCK)
            d, m2, v2 = _adam_math(ws_refs[0][rows, :], gg, ws_refs[1][rows, :], ws_refs[2][rows, :])
            outs[4 * k0][rows, :] = gg
            outs[4 * k0 + 1][rows, :] = d
            outs[4 * k0 + 2][rows, :] = m2
            outs[4 * k0 + 3][rows, :] = v2
        emit(k0 + 1, tall_sum(TALL_BS, N_GROUPS), *wmv[3 * (k0 + 1):3 * (k0 + 1) + 3])
        emit(k0 + 2, tall_sum(TALL_RB, N_HEADS)[:, :N_BUCKETS], *wmv[3 * (k0 + 2):3 * (k0 + 2) + 3])
        emit(k0 + 3, tall_sum(TALL_SK, 8)[0:1, :N_HEADS], *wmv[3 * (k0 + 3):3 * (k0 + 3) + 3])

    out_shapes = [jax.ShapeDtypeStruct((N_DEV, WIDE_W), F32), jax.ShapeDtypeStruct((8, LANES), F32),
                  jax.ShapeDtypeStruct((1, 1), F32)]
    for n in names:
        out_shapes += [jax.ShapeDtypeStruct(wide_wmv[n][0].shape, F32)] * 4
    for _, t in tall:
        out_shapes += [jax.ShapeDtypeStruct(t[0].shape, F32)] * 4
    res = pl.pallas_call(
        body, name="adam_small", out_shape=out_shapes, grid=(1,),
        in_specs=[_const_spec(a.shape) for a in ins] + [ANY], out_specs=[_const_spec(o.shape) for o in out_shapes],
        compiler_params=_params(("arbitrary",)),
    )(*ins, after)
    out = {}
    for k, n in enumerate(names + [t[0] for t in tall]):
        out[n] = tuple(res[3 + 4 * k:7 + 4 * k])
    return res[0], res[1], res[2], out


def kernel(x, c, rel_bias, w_ada, b_ada, w_in, b_in, attn_sinks, gmlp_ln_g, gmlp_ln_b, gmlp_w_s, gmlp_b_s, attn_out_g, gmlp_out_g, w_out, ln1_g, ln1_b, w_gate_up, w_down, ln2_g, ln2_b, loss_target, m_rel_bias, m_w_ada, m_b_ada, m_w_in, m_b_in, m_attn_sinks, m_gmlp_ln_g, m_gmlp_ln_b, m_gmlp_w_s, m_gmlp_b_s, m_attn_out_g, m_gmlp_out_g, m_w_out, m_ln1_g, m_ln1_b, m_w_gate_up, m_w_down, m_ln2_g, m_ln2_b, v_rel_bias, v_w_ada, v_b_ada, v_w_in, v_b_in, v_attn_sinks, v_gmlp_ln_g, v_gmlp_ln_b, v_gmlp_w_s, v_gmlp_b_s, v_attn_out_g, v_gmlp_out_g, v_w_out, v_ln1_g, v_ln1_b, v_w_gate_up, v_w_down, v_ln2_g, v_ln2_b):
    ix, iy, _ = _my_pos()
    chip = 2 * ix + iy
    s = x.shape[1]
    xs = x[0]
    tgt = loss_target[0]
    tm_big = min(512, s)
    tm_ffn = min(FFN_SUB, s)

    w_in_s, w_out_s = w_in[0].T.astype(BF16), w_out[0].astype(BF16)
    w_gu_s, w_dn_s = w_gate_up[0].astype(BF16), w_down[0].astype(BF16)
    sc_all, modr, (w_in_g, w_out_g) = _prologue(jnp.pad(c, ((0, 7), (0, 0))), w_ada[0], b_ada, [w_in_s, w_out_s])
    w_in_f = _insert_own(w_in_g, w_in_s, "blk", chip).reshape(IN_W, D_MODEL)

    bucket = _bucket_table()
    bias, wsm = _prep_tables(bucket, rel_bias.T, gmlp_w_s[0])
    bsx = jnp.repeat(gmlp_b_s[0].T, GROUP_DIM, axis=1)
    amat = _group_mean_matrix()
    sinks = attn_sinks[0]

    (h1, q, kv, gu, gv, xb), (w_dn_g,) = _fwd_in(xs, modr, w_in_f, b_in, tm_big, [w_dn_s], ["blk"])
    w_out_f = _insert_own(w_out_g, w_out_s, "blk", chip).reshape(D_MODEL, D_MODEL)
    (x1, x1b, y, mixed), (w_gu_g,) = _fwd_mix(
        q, kv, gu, gv, xs, modr, bias, sinks, gmlp_ln_g, gmlp_ln_b, wsm, bsx, amat, attn_out_g, gmlp_out_g, w_out_f,
        ln1_g, ln1_b, tm_big, [w_gu_s], ["blk"])
    assert w_gate_up.shape[2] == FF_CHUNK
    w_gu_f = _insert_own(w_gu_g, w_gu_s, "blk", chip)
    w_dn_f = _insert_own(w_dn_g, w_dn_s, "blk", chip).reshape(D_FF, D_MODEL)
    h2, act, dy2, dx1a, acc_f = _fwd_ffn(x1, tgt, modr, ln2_g, ln2_b, w_gu_f, w_dn_f, min(2 * FFN_SUB, s))

    a_act, dgu_ff, dh2 = _bwd_ffn(dy2, act, w_gu_f, w_dn_f, min(FFN_SUB, s))
    g_dn, g_dn_b, _ = _wgrad(a_act, [dy2], D_FF // 2, min(1024, s), "wgrad_down")
    g_gu, g_gu_b, _ = _wgrad(h2, [dgu_ff], 512, min(512, s), "wgrad_gate_up")
    blk3 = lambda a, rows: a.reshape(N_CHIPS, rows, a.shape[1])
    (dxa, dy, dmix, acc_m), (got_dn, got_gu) = _bwd_mid(
        dh2, dx1a, x1b, xs, y, modr, ln1_g, w_out_f, tm_big, [blk3(g_dn_b, D_FF // N_CHIPS), g_gu_b], ["blk", "cols"])
    g_out, g_out_b, _ = _wgrad(mixed, [dy], 512, min(2048, s), "wgrad_out")
    (got_out,) = _swap_halves([blk3(g_out_b, D_MODEL // N_CHIPS)], ["blk"], "rs_swap_out")
    kinds_a = ["blk", "cols", "blk"]
    fulls_a = [blk3(g_dn, D_FF // N_CHIPS), g_gu, blk3(g_out, D_MODEL // N_CHIPS)]
    gots_a = [got_dn, got_gu, got_out]
    parts_a = [_add_halves(f, g, k, "rs_add_a%d" % i) for i, (f, g, k) in enumerate(zip(fulls_a, gots_a, kinds_a))]
    (dq, dkv, dgu, dgv, gbias, dws, dbs, vec, dsink), rxs_a = _bwd_mix(
        q, kv, gu, gv, dmix, bias, sinks, gmlp_ln_g, gmlp_ln_b, wsm, bsx, amat, attn_out_g, gmlp_out_g,
        [p[1] for p in parts_a], kinds_a)
    tall_g = _mix_finalize(gbias, bucket, dws, dbs, dsink)
    grad_x, acc_i, db_in = _bwd_in(dq, dkv, dgu, dgv, dxa, xb, modr, w_in_f, tm_big)

    wide_g = _pack_wide(acc_i, acc_m, acc_f, db_in, vec)
    full_in, full_in_b, (gw, gt) = _wgrad(h1, [dq, dkv, dgu, dgv], 512, min(1024, s), "wgrad_in", owner_blocks=True,
                                          gather_vs=[wide_g, tall_g])
    (got_in,) = _swap_halves([full_in_b], ["blk"], "rs_swap_in")
    part_in = _add_halves(full_in, got_in, "blk", "rs_add_in")
    in_send, in_recv, in_part, in_rx, token = _exchange_start(part_in[1], "rs_chips_in_start")
    wide_wmv ={"b_ada": (b_ada, m_b_ada, v_b_ada), "b_in": (b_in, m_b_in, v_b_in),
                "ln1_g": (ln1_g, m_ln1_g, v_ln1_g), "ln1_b": (ln1_b, m_ln1_b, v_ln1_b),
                "ln2_g": (ln2_g, m_ln2_g, v_ln2_g), "ln2_b": (ln2_b, m_ln2_b, v_ln2_b),
                "gmlp_ln_g": (gmlp_ln_g, m_gmlp_ln_g, v_gmlp_ln_g), "gmlp_ln_b": (gmlp_ln_b, m_gmlp_ln_b, v_gmlp_ln_b),
                "attn_out_g": (attn_out_g, m_attn_out_g, v_attn_out_g),
                "gmlp_out_g": (gmlp_out_g, m_gmlp_out_g, v_gmlp_out_g)}
    rows2 = lambda a: a.reshape(-1, a.shape[-1])
    dmod_all, loss_t, loss1, small = _adam_small(
        gw, gt, wide_wmv, tuple(rows2(a) for a in (gmlp_w_s, m_gmlp_w_s, v_gmlp_w_s)),
        tuple(rows2(a) for a in (gmlp_b_s, m_gmlp_b_s, v_gmlp_b_s)), (rel_bias.T, m_rel_bias.T, v_rel_bias.T),
        (attn_sinks, m_attn_sinks, v_attn_sinks), token)
    small["rel_bias"] = tuple(a.T for a in small["rel_bias"])
    loss = loss1.reshape(())

    g_ada, d_ada, m_ada, v_ada = _adam_w_ada(sc_all.T, dmod_all, w_ada[0], m_w_ada[0], v_w_ada[0], 256)

    sums = [(parts_a[0][0], rxs_a[0], 176), (parts_a[1][0], rxs_a[1], 256), (parts_a[2][0], rxs_a[2], 128)]
    mine = [_sum_chips(p, rx, tr, "rs_sum_%d" % i, loss_t) for i, (p, rx, tr) in enumerate(sums)]
    got = _share_halves(mine, "rs_share")
    gs_dn, d_dn, m_dn, v_dn = _adam_halves(w_down[0], mine[0], got[0], m_w_down[0], v_w_down[0], 176, "adam_w_down")
    gs_gu, d_gu, m_gu, v_gu = _adam_halves(w_gate_up[0], mine[1], got[1], m_w_gate_up[0], v_w_gate_up[0], 256,
                                           "adam_w_gate_up")
    gs_out, d_out, m_out, v_out = _adam_halves(w_out[0], mine[2], got[2], m_w_out[0], v_w_out[0], 128, "adam_w_out")

    rx_in = _exchange_wait(in_send, in_recv, in_part, in_rx, d_gu, "rs_chips_in_wait")
    mine_in = _sum_chips(part_in[0], rx_in, 256, "rs_sum_in", rx_in)
    (got_in_half,) = _share_halves([mine_in], "rs_share_in")
    in_t = _adam_halves(w_in[0].T, mine_in, got_in_half, m_w_in[0].T, v_w_in[0].T, 256, "adam_w_in", transposed=True)
    gs_in, d_in, m_in, v_in = (a.T for a in in_t)

    big = {"w_ada": (g_ada, d_ada, m_ada, v_ada), "w_in": (gs_in, d_in, m_in, v_in), "w_out": (gs_out, d_out, m_out, v_out),
           "w_gate_up": (gs_gu, d_gu, m_gu, v_gu), "w_down": (gs_dn, d_dn, m_dn, v_dn)}
    order = ["rel_bias", "w_ada", "b_ada", "w_in", "b_in", "attn_sinks", "gmlp_ln_g", "gmlp_ln_b", "gmlp_w_s", "gmlp_b_s",
             "attn_out_g", "gmlp_out_g", "w_out", "ln1_g", "ln1_b", "w_gate_up", "w_down", "ln2_g", "ln2_b"]
    shapes = {"gmlp_w_s": gmlp_w_s.shape, "gmlp_b_s": gmlp_b_s.shape}
    outs = [loss, grad_x[None]]
    for k in range(4):
        for name in order:
            if name in big:
                outs.append(big[name][k][None])
            elif name in shapes:
                outs.append(small[name][k].reshape(shapes[name]))
            else:
                outs.append(small[name][k])
    return tuple(outs)
```

```python
import math

import numpy as np
import jax
import jax.numpy as jnp
from jax import lax
from jax.experimental import pallas as pl
from jax.experimental.pallas import tpu as pltpu

F32 = jnp.float32
BF16 = jnp.bfloat16
MESH = pl.DeviceIdType.MESH

D_MODEL = 1024
N_HEADS = 8
N_KV = 2
HEAD_DIM = 64
ATTN_W = N_HEADS * HEAD_DIM
KV_W = N_KV * HEAD_DIM
N_GROUPS = 8
GROUP_DIM = 64
GMLP_W = N_GROUPS * GROUP_DIM
IN_W = ATTN_W + 2 * KV_W + 2 * GMLP_W
BLOCK = 128
N_BUCKETS = 32
MAX_DISTANCE = 128
D_FF = 2816
ALPHA = 2.0 ** 0.25
LN_EPS = 1e-5
NEG_INF = -1e30
ADAM_LR, ADAM_B1, ADAM_B2, ADAM_EPS, ADAM_WD, ADAM_STEP = 0.001, 0.9, 0.999, 1e-8, 0.01, 10
N_CHIPS = 4
N_DEV = 8
LANES = 128
V7X_VMEM_LIMIT = 56 * 2 ** 20
GELU_C = math.sqrt(2.0 / math.pi)
Q_SCALE = HEAD_DIM ** -0.5
ANY = pl.BlockSpec(memory_space=pl.ANY)

TALL_BS = N_GROUPS * BLOCK
TALL_RB = TALL_BS + 8
TALL_SK = TALL_RB + N_BUCKETS
TALL_ROWS = TALL_SK + 8
WIDE_W = 6 * D_MODEL
WIDE_LAYOUT = {
    "b_ada": (0, 0, 6 * D_MODEL),
    "b_in": (1, 0, IN_W), "ln1_g": (1, IN_W, D_MODEL), "ln1_b": (1, IN_W + D_MODEL, D_MODEL),
    "ln2_g": (1, IN_W + 2 * D_MODEL, D_MODEL), "ln2_b": (1, IN_W + 3 * D_MODEL, D_MODEL),
    "gmlp_ln_g": (2, 0, GMLP_W), "gmlp_ln_b": (2, GMLP_W, GMLP_W), "attn_out_g": (2, 2 * GMLP_W, ATTN_W),
    "gmlp_out_g": (2, 2 * GMLP_W + ATTN_W, GMLP_W), "loss": (2, 3 * GMLP_W + ATTN_W, D_MODEL)}
WIDE_PARAMS = tuple(n for n in WIDE_LAYOUT if n != "loss")


def _params(sem=None):
    return pltpu.CompilerParams(dimension_semantics=sem, vmem_limit_bytes=V7X_VMEM_LIMIT)


def _const_spec(shape, single=False):
    nd = len(shape)
    if single:
        return pl.BlockSpec(shape, lambda *_: (0,) * nd, pipeline_mode=pl.Buffered(1))
    return pl.BlockSpec(shape, lambda *_: (0,) * nd)


def _dot(a, b):
    return jnp.dot(a, b, preferred_element_type=F32)


def _dot_nt(a, b):
    return lax.dot_general(a, b, (((1,), (1,)), ((), ())), preferred_element_type=F32)


def _dot_tn(a, b):
    return lax.dot_general(a, b, (((0,), (0,)), ((), ())), preferred_element_type=F32)


def _gelu(x):
    t = jnp.tanh(GELU_C * (x + 0.044715 * x * x * x))
    return 0.5 * x * (1.0 + t), t


def _gelu_grad(x, t):
    return 0.5 * (1.0 + t) + 0.5 * x * (1.0 - t * t) * GELU_C * (1.0 + 3.0 * 0.044715 * x * x)


def _split_dot(x, a):
    hi = x.astype(BF16)
    lo = (x - hi.astype(F32)).astype(BF16)
    return _dot(hi, a) + _dot(lo, a)


def _group_mean_matrix():
    g = np.arange(GMLP_W) // GROUP_DIM
    return jnp.asarray((g[:, None] == g[None, :]).astype(np.float32) / GROUP_DIM, dtype=BF16)


def _ln_stats(z):
    mu = jnp.mean(z, axis=-1, keepdims=True)
    d = z - mu
    var = jnp.mean(d * d, axis=-1, keepdims=True)
    rstd = lax.rsqrt(var + LN_EPS)
    return d * rstd, rstd


def _ln_bwd(dxhat, xhat, rstd):
    m1 = jnp.mean(dxhat, axis=-1, keepdims=True)
    m2 = jnp.mean(dxhat * xhat, axis=-1, keepdims=True)
    return rstd * (dxhat - m1 - xhat * m2)


def _colsum(x):
    return jnp.sum(x, axis=0, keepdims=True)


def _my_pos():
    return lax.axis_index("x"), lax.axis_index("y"), lax.axis_index("c")


def _other_chips(x, y):
    return [(1 - x, y), (x, 1 - y), (1 - x, 1 - y)]


def _chip_index_scalar():
    ix, iy, _ = _my_pos()
    return jnp.reshape(2 * ix + iy, (1,)).astype(jnp.int32)


def _core_index_scalar():
    return jnp.reshape(lax.axis_index("c"), (1,)).astype(jnp.int32)


class _Gather8:
    def __init__(self, x_refs, out_refs, send_sems, recv_sems, local_sems):
        self.x_refs, self.out_refs = x_refs, out_refs
        self.send_sems, self.recv_sems, self.local_sems = send_sems, recv_sems, local_sems
        self.x, self.y, self.c = _my_pos()
        self.me, self.sibling = (self.x, self.y, self.c), (self.x, self.y, 1 - self.c)
        self.chips = _other_chips(self.x, self.y)

    def _rows(self, a, px, py, pc):
        m_per = self.x_refs[a].shape[0]
        return self.out_refs[a].at[pl.ds((4 * px + 2 * py + pc) * m_per, m_per), :]

    def _copy(self, a, k, block, to, src=None):
        return pltpu.make_async_remote_copy(
            src_ref=self._rows(a, *block) if src is None else src, dst_ref=self._rows(a, *block),
            send_sem=self.send_sems.at[7 * a + k], recv_sem=self.recv_sems.at[7 * a + k], device_id=to,
            device_id_type=MESH)

    def _local(self, a):
        return pltpu.make_async_copy(self.x_refs[a], self._rows(a, *self.me), self.local_sems.at[a])

    def start(self):
        for a in range(len(self.x_refs)):
            self._local(a).start()
            self._copy(a, 0, self.me, self.sibling, src=self.x_refs[a]).start()
            for j, chip in enumerate(self.chips):
                self._copy(a, 1 + j, self.me, (*chip, self.c), src=self.x_refs[a]).start()

    def forward(self):
        for a in range(len(self.x_refs)):
            for j, chip in enumerate(self.chips):
                self._copy(a, 1 + j, (*chip, self.c), self.me).wait_recv()
                self._copy(a, 4 + j, (*chip, self.c), self.sibling).start()

    def finish(self):
        for a in range(len(self.x_refs)):
            self._copy(a, 0, self.sibling, self.me).wait_recv()
            for j, chip in enumerate(self.chips):
                self._copy(a, 4 + j, (*chip, 1 - self.c), self.me).wait_recv()
        for a in range(len(self.x_refs)):
            for k in range(7):
                self._copy(a, k, self.me, self.me).wait_send()
            self._local(a).wait()

    @staticmethod
    def sems(n_v):
        return [pltpu.SemaphoreType.DMA((7 * n_v,)), pltpu.SemaphoreType.DMA((7 * n_v,)),
                pltpu.SemaphoreType.DMA((n_v,))]


def _gathered8_shapes(vs):
    return [jax.ShapeDtypeStruct((N_DEV * v.shape[0], v.shape[1]), v.dtype) for v in vs]


VMEM_WHOLE = pl.BlockSpec(memory_space=pltpu.VMEM)


def _prologue(c_pad, w_ada_s, b_ada, shards):
    n = w_ada_s.shape[1]
    n_w = len(shards)
    assert N_CHIPS * n == 6 * D_MODEL and n % LANES == 0

    def body(c_ref, w_ref, b_ref, *rest):
        shard_refs = rest[:n_w]
        sc_ref, modc_ref, modg_ref, modr_ref = rest[n_w:n_w + 4]
        gathered_refs = rest[n_w + 4:2 * n_w + 4]
        call_ref, w_vmem = rest[2 * n_w + 4:2 * n_w + 6]
        sems = rest[2 * n_w + 6:]
        ix, iy, ic = _my_pos()
        chip = 2 * ix + iy
        weights = _WeightGather(shard_refs, gathered_refs, ["blk"] * n_w, sems[0], sems[1])
        gather_c = _Gather8([c_ref], [call_ref], sems[2], sems[3], sems[4])
        gather_mod = _Gather8([modc_ref], [modg_ref], sems[5], sems[6], sems[7])
        load_w = pltpu.make_async_copy(w_ref, w_vmem, sems[8])
        weights.start()
        gather_c.start()
        load_w.start()
        gather_c.forward()
        gather_c.finish()
        cv = call_ref[...]
        sc = cv * _sigmoid(cv)
        a_hi = sc.astype(BF16)
        a_lo = (sc - a_hi.astype(F32)).astype(BF16)
        load_w.wait()
        w = w_vmem[...]
        w_hi = w.astype(BF16)
        w_lo = (w - w_hi.astype(F32)).astype(BF16)
        b = b_ref[:, 0:n]
        for k in range(1, N_CHIPS):
            b = jnp.where(chip == k, b_ref[:, k * n:(k + 1) * n], b)
        mod = _dot(a_hi, w_hi) + _dot(a_hi, w_lo) + _dot(a_lo, w_hi) + b
        for d in range(N_DEV):
            sc_ref[d:d + 1, :] = sc[8 * d:8 * d + 1, :]
            modc_ref[d:d + 1, :] = mod[8 * d:8 * d + 1, :]
        gather_mod.start()
        weights.forward()
        gather_mod.forward()
        gather_mod.finish()
        dev = 2 * chip + ic
        mine = jnp.concatenate([modg_ref[pl.ds(2 * 8 * k + dev, 1), :] for k in range(N_CHIPS)], axis=1)
        modr_ref[...] = jnp.zeros_like(modr_ref)
        for r in range(6):
            modr_ref[r:r + 1, :] = mine[:, r * D_MODEL:(r + 1) * D_MODEL]
        weights.forward_diagonal()
        weights.finish()

    outs = pl.pallas_call(
        body, name="prologue",
        out_shape=[jax.ShapeDtypeStruct((N_DEV, D_MODEL), F32), jax.ShapeDtypeStruct((N_DEV, n), F32),
                   jax.ShapeDtypeStruct((N_DEV * N_DEV, n), F32), jax.ShapeDtypeStruct((8, D_MODEL), F32)]
        + [jax.ShapeDtypeStruct(_gathered_shape(sh, "blk"), BF16) for sh in shards],
        in_specs=[VMEM_WHOLE, ANY, VMEM_WHOLE] + [ANY] * n_w,
        out_specs=[VMEM_WHOLE, VMEM_WHOLE, VMEM_WHOLE, VMEM_WHOLE] + [ANY] * n_w,
        scratch_shapes=[pltpu.VMEM((N_DEV * 8, D_MODEL), F32), pltpu.VMEM(w_ada_s.shape, F32)]
        + _WeightGather.sems(n_w) + _Gather8.sems(1) + _Gather8.sems(1) + [pltpu.SemaphoreType.DMA],
        compiler_params=pltpu.CompilerParams(vmem_limit_bytes=V7X_VMEM_LIMIT),
    )(c_pad, w_ada_s, b_ada, *shards)
    return outs[0], outs[3], outs[4:]


def _gathered_shape(shard, kind):
    r, cc = shard.shape
    return (N_CHIPS, r, cc) if kind == "blk" else (r, N_CHIPS * cc)


class _WeightGather:
    N_SEM = 8

    def __init__(self, shards, gathered, kinds, send_sems, recv_sems):
        self.shards, self.gathered, self.kinds = shards, gathered, kinds
        self.send_sems, self.recv_sems = send_sems, recv_sems
        self.x, self.y, self.c = _my_pos()
        self.me, self.sibling = (self.x, self.y, self.c), (self.x, self.y, 1 - self.c)
        self.nbr = ((1 - self.x, self.y), (self.x, 1 - self.y))
        self.diag = 2 * (1 - self.x) + (1 - self.y)

    def _dst(self, a, chip, pc, quarter=None):
        r, cc = self.shards[a].shape
        h = r // 2
        row0, rows = pc * h, h
        if quarter is not None:
            row0, rows = pc * h + quarter * (h // 2), h // 2
        g = self.gathered[a]
        if self.kinds[a] == "blk":
            return g.at[chip, pl.ds(row0, rows), :]
        return g.at[pl.ds(row0, rows), pl.ds(chip * cc, cc)]

    def _copy(self, a, k, region, to, src=None):
        return pltpu.make_async_remote_copy(
            src_ref=region if src is None else src, dst_ref=region, send_sem=self.send_sems.at[a * self.N_SEM + k],
            recv_sem=self.recv_sems.at[a * self.N_SEM + k], device_id=to, device_id_type=MESH)

    def _arrays(self):
        return range(len(self.shards))

    def start(self):
        my_chip = 2 * self.x + self.y
        for a in self._arrays():
            h = self.shards[a].shape[0] // 2
            mine = self.shards[a].at[pl.ds(self.c * h, h), :]
            for j, chip in enumerate(self.nbr):
                self._copy(a, j, self._dst(a, my_chip, self.c), (*chip, self.c), src=mine).start()

    def forward(self):
        for a in self._arrays():
            for j, chip in enumerate(self.nbr):
                cj = 2 * chip[0] + chip[1]
                half = self._dst(a, cj, self.c)
                self._copy(a, j, half, self.me).wait_recv()
                self._copy(a, 2 + j, half, self.sibling).start()
                other = self.nbr[1 - j]
                self._copy(a, 4 + j, self._dst(a, cj, self.c, quarter=j), (*other, self.c)).start()

    def forward_diagonal(self):
        for a in self._arrays():
            for j in range(2):
                quarter = self._dst(a, self.diag, self.c, quarter=j)
                self._copy(a, 4 + j, quarter, self.me).wait_recv()
                self._copy(a, 6 + j, quarter, self.sibling).start()

    def finish(self):
        for a in self._arrays():
            for j, chip in enumerate(self.nbr):
                self._copy(a, 2 + j, self._dst(a, 2 * chip[0] + chip[1], 1 - self.c), self.me).wait_recv()
                self._copy(a, 6 + j, self._dst(a, self.diag, 1 - self.c, quarter=j), self.me).wait_recv()
        for a in self._arrays():
            half = self._dst(a, self.diag, self.c)
            quarter = self._dst(a, self.diag, self.c, quarter=0)
            for k in range(self.N_SEM):
                self._copy(a, k, half if k < 4 else quarter, self.me).wait_send()

    @classmethod
    def sems(cls, n_arr):
        return [pltpu.SemaphoreType.DMA((n_arr * cls.N_SEM,)), pltpu.SemaphoreType.DMA((n_arr * cls.N_SEM,))]


def _insert_own(gathered, shard, kind, chip):
    if kind == "blk":
        return lax.dynamic_update_slice(gathered, shard[None], (chip, 0, 0))
    return lax.dynamic_update_slice(gathered, shard, (0, chip * shard.shape[1]))


def _half_of_full(ref, kind, pc):
    if kind == "blk":
        h = ref.shape[1] // 2
        return ref.at[:, pl.ds(pc * h, h), :]
    h = ref.shape[0] // 2
    return ref.at[pl.ds(pc * h, h), :]


def _half_shape(shape, kind):
    return (shape[0], shape[1] // 2, shape[2]) if kind == "blk" else (shape[0] // 2, shape[1])


class _HalfSwap:
    def __init__(self, ins, outs, kinds, send_sems, recv_sems):
        self.ins, self.outs, self.kinds = ins, outs, kinds
        self.send_sems, self.recv_sems = send_sems, recv_sems
        self.x, self.y, self.c = _my_pos()

    def _copies(self):
        for a in range(len(self.ins)):
            yield pltpu.make_async_remote_copy(
                src_ref=_half_of_full(self.ins[a], self.kinds[a], 1 - self.c), dst_ref=self.outs[a],
                send_sem=self.send_sems.at[a], recv_sem=self.recv_sems.at[a],
                device_id=(self.x, self.y, 1 - self.c), device_id_type=MESH)

    def start(self):
        for cp in self._copies():
            cp.start()

    def wait(self):
        for cp in self._copies():
            cp.wait()

    @staticmethod
    def sems(n_arr):
        return [pltpu.SemaphoreType.DMA((n_arr,)), pltpu.SemaphoreType.DMA((n_arr,))]

    @staticmethod
    def out_shapes(fulls, kinds):
        return [jax.ShapeDtypeStruct(_half_shape(a.shape, k), a.dtype) for a, k in zip(fulls, kinds)]


def _swap_halves(fulls_bf16, kinds, name):
    n_arr = len(fulls_bf16)

    def body(*refs):
        swap = _HalfSwap(refs[:n_arr], refs[n_arr:2 * n_arr], kinds, *refs[2 * n_arr:])
        swap.start()
        swap.wait()

    return pl.pallas_call(
        body, name=name, out_shape=_HalfSwap.out_shapes(fulls_bf16, kinds),
        in_specs=[ANY] * n_arr, out_specs=[ANY] * n_arr, scratch_shapes=_HalfSwap.sems(n_arr),
    )(*fulls_bf16)


def _add_halves(full, got, kind, name):
    hs = _half_shape(full.shape, kind)

    def body(pos_ref, a_ref, b_ref, o_ref, ob_ref):
        p = a_ref[...] + b_ref[...].astype(F32)
        ob_ref[...] = p.astype(BF16)

        @pl.when(pl.program_id(0) == pos_ref[1])
        def _():
            o_ref[...] = p.reshape(o_ref.shape)

    if kind == "blk":
        nb, h, cc = hs
        own = pl.BlockSpec((1, h, cc), lambda b, pos_ref: (b, pos_ref[0], 0))
        other = pl.BlockSpec((1, h, cc), lambda b, pos_ref: (b, 0, 0))
    else:
        h, cc = hs[0], hs[1] // N_CHIPS
        own = pl.BlockSpec((h, cc), lambda b, pos_ref: (pos_ref[0], b))
        other = pl.BlockSpec((h, cc), lambda b, pos_ref: (0, b))
    pos = jnp.concatenate([_core_index_scalar(), _chip_index_scalar()])
    return pl.pallas_call(
        body, name=name, out_shape=(jax.ShapeDtypeStruct((h, cc), F32), jax.ShapeDtypeStruct(hs, BF16)),
        grid_spec=pltpu.PrefetchScalarGridSpec(
            num_scalar_prefetch=1, grid=(N_CHIPS,), in_specs=[own, other],
            out_specs=(pl.BlockSpec((h, cc), lambda b, pos_ref: (0, 0)), other)),
        compiler_params=_params(("arbitrary",)),
    )(pos, full, got)


def _rx_shape(part_shape, kind):
    if kind == "blk":
        return (3, part_shape[1], part_shape[2])
    return (3, part_shape[0], part_shape[1] // N_CHIPS)


class _ChipExchange:
    def __init__(self, parts, rxs, kinds, send_sems, recv_sems):
        self.parts, self.rxs, self.kinds = parts, rxs, kinds
        self.send_sems, self.recv_sems = send_sems, recv_sems
        self.x, self.y, self.c = _my_pos()
        self.chips = _other_chips(self.x, self.y)

    def _copies(self):
        for a in range(len(self.parts)):
            for j, chip in enumerate(self.chips):
                cj = 2 * chip[0] + chip[1]
                if self.kinds[a] == "blk":
                    src = self.parts[a].at[cj]
                else:
                    cc = self.parts[a].shape[1] // N_CHIPS
                    src = self.parts[a].at[:, pl.ds(cj * cc, cc)]
                yield pltpu.make_async_remote_copy(
                    src_ref=src, dst_ref=self.rxs[a].at[j], send_sem=self.send_sems.at[a * 3 + j],
                    recv_sem=self.recv_sems.at[a * 3 + j], device_id=(*chip, self.c), device_id_type=MESH)

    def start(self):
        for cp in self._copies():
            cp.start()

    def wait(self):
        for cp in self._copies():
            cp.wait_recv()
        for cp in self._copies():
            cp.wait_send()

    @staticmethod
    def sems(n_arr):
        return [pltpu.SemaphoreType.DMA((n_arr * 3,)), pltpu.SemaphoreType.DMA((n_arr * 3,))]


HBM_SPEC = pl.BlockSpec(memory_space=pltpu.HBM)
SEM_SPEC = pl.BlockSpec(memory_space=pltpu.SEMAPHORE)
DATAFLOW = pltpu.SideEffectType.DATAFLOW_SIDE_EFFECTING


def _exchange_start(part, name):
    rx_shape = _rx_shape(part.shape, "blk")

    def body(part_ref, rx_ref, send_sems, recv_sems, part_thru, rx_thru, token):
        _ChipExchange([part_ref], [rx_ref], ["blk"], send_sems, recv_sems).start()
        token[...] = jnp.zeros_like(token)

    return pl.pallas_call(
        body, name=name,
        out_shape=(pltpu.SemaphoreType.DMA((3,)), pltpu.SemaphoreType.DMA((3,)), pltpu.HBM(part.shape, part.dtype),
                   pltpu.HBM(rx_shape, BF16), jax.ShapeDtypeStruct((8, LANES), F32)),
        in_specs=(HBM_SPEC, HBM_SPEC), out_specs=(SEM_SPEC, SEM_SPEC, HBM_SPEC, HBM_SPEC, VMEM_WHOLE),
        input_output_aliases={0: 2, 1: 3}, compiler_params=pltpu.CompilerParams(has_side_effects=DATAFLOW),
    )(pltpu.with_memory_space_constraint(part, pltpu.HBM),
      pltpu.with_memory_space_constraint(lax.empty(rx_shape, BF16), pltpu.HBM))


def _exchange_wait(send_sems, recv_sems, part_thru, rx_thru, after, name):
    def body(part_ref, rx_ref, send_sems, recv_sems, after_ref, part_dead, rx_out):
        _ChipExchange([part_ref], [rx_ref], ["blk"], send_sems, recv_sems).wait()

    return pl.pallas_call(
        body, name=name,
        out_shape=(pltpu.HBM(part_thru.shape, part_thru.dtype), pltpu.HBM(rx_thru.shape, rx_thru.dtype)),
        in_specs=(HBM_SPEC, HBM_SPEC, SEM_SPEC, SEM_SPEC, ANY), out_specs=(HBM_SPEC, HBM_SPEC),
        input_output_aliases={0: 0, 1: 1}, compiler_params=pltpu.CompilerParams(has_side_effects=DATAFLOW),
    )(part_thru, rx_thru, send_sems, recv_sems, after)[1]


def _sum_chips(part, rx, tr, name, after):
    _, h, cc = rx.shape
    flips = (2, 1, 3)

    def body(chip_ref, p_ref, rx_ref, after_ref, o_ref):
        own = p_ref[...]
        for mc in range(N_CHIPS):
            @pl.when(chip_ref[0] == mc)
            def _():
                terms = sorted([(mc, None)] + [(mc ^ f, j) for j, f in enumerate(flips)])
                acc = None
                for _, j in terms:
                    t = own if j is None else rx_ref[j].astype(F32)
                    acc = t if acc is None else acc + t
                o_ref[...] = acc

    return pl.pallas_call(
        body, name=name, out_shape=jax.ShapeDtypeStruct((h, cc), F32),
        grid_spec=pltpu.PrefetchScalarGridSpec(
            num_scalar_prefetch=1, grid=(h // tr,),
            in_specs=[pl.BlockSpec((tr, cc), lambda i, chip_ref: (i, 0)),
                      pl.BlockSpec((3, tr, cc), lambda i, chip_ref: (0, i, 0)), ANY],
            out_specs=pl.BlockSpec((tr, cc), lambda i, chip_ref: (i, 0))),
        compiler_params=_params(("arbitrary",)),
    )(_chip_index_scalar(), part, rx, after)


def _share_halves(halves, name):
    n_arr = len(halves)

    def body(*refs):
        ins, outs = refs[:n_arr], refs[n_arr:2 * n_arr]
        send_sems, recv_sems = refs[2 * n_arr:]
        x, y, c = _my_pos()
        cps = []
        for a in range(n_arr):
            cp = pltpu.make_async_remote_copy(
                src_ref=ins[a], dst_ref=outs[a], send_sem=send_sems.at[a], recv_sem=recv_sems.at[a],
                device_id=(x, y, 1 - c), device_id_type=MESH)
            cp.start()
            cps.append(cp)
        for cp in cps:
            cp.wait()

    return pl.pallas_call(
        body, name=name, out_shape=[jax.ShapeDtypeStruct(h.shape, h.dtype) for h in halves],
        in_specs=[ANY] * n_arr, out_specs=[ANY] * n_arr,
        scratch_shapes=[pltpu.SemaphoreType.DMA((n_arr,)), pltpu.SemaphoreType.DMA((n_arr,))],
    )(*halves)


def _bucket_table():
    qi = jnp.arange(BLOCK)[:, None]
    si = jnp.arange(2 * BLOCK)[None, :]
    dist = qi + BLOCK - si
    max_exact = N_BUCKETS // 2
    n = jnp.maximum(dist, 0)
    nf = jnp.maximum(n, max_exact).astype(F32)
    large = max_exact + (jnp.log(nf / max_exact) / math.log(MAX_DISTANCE / max_exact)
                         * (N_BUCKETS - max_exact)).astype(jnp.int32)
    large = jnp.minimum(large, N_BUCKETS - 1)
    return jnp.where(n < max_exact, n, large).astype(F32)


def _prep_tables(bucket, rel_bias_t, w_s):
    def body(bucket_ref, rb_ref, ws_ref, bias_ref, wsm_ref):
        qi = lax.broadcasted_iota(jnp.int32, (BLOCK, 2 * BLOCK), 0)
        si = lax.broadcasted_iota(jnp.int32, (BLOCK, 2 * BLOCK), 1)
        dist = qi + BLOCK - si
        in_window = (dist >= 0) & (dist < BLOCK)
        bk = bucket_ref[...]
        for h in range(N_HEADS):
            acc = jnp.zeros((BLOCK, 2 * BLOCK), F32)
            for b in range(N_BUCKETS):
                acc = jnp.where(bk == float(b), rb_ref[h, b], acc)
            bias_ref[h] = jnp.where(in_window, acc, NEG_INF)
        ti = lax.broadcasted_iota(jnp.int32, (BLOCK, BLOCK), 0)
        ui = lax.broadcasted_iota(jnp.int32, (BLOCK, BLOCK), 1)
        for g in range(N_GROUPS):
            wsm_ref[g] = jnp.where(ti >= ui, ws_ref[g], 0.0).astype(BF16)

    return pl.pallas_call(
        body, name="prep_tables",
        out_shape=(jax.ShapeDtypeStruct((N_HEADS, BLOCK, 2 * BLOCK), F32),
                   jax.ShapeDtypeStruct((N_GROUPS, BLOCK, BLOCK), BF16)),
        grid=(1,),
        in_specs=[_const_spec((BLOCK, 2 * BLOCK)), pl.BlockSpec(memory_space=pltpu.SMEM),
                  _const_spec((N_GROUPS, BLOCK, BLOCK))],
        out_specs=(_const_spec((N_HEADS, BLOCK, 2 * BLOCK)), _const_spec((N_GROUPS, BLOCK, BLOCK))),
        compiler_params=_params(("arbitrary",)),
    )(bucket, rel_bias_t, w_s)


def _fwd_in(x, modr, w_in, b_in, tm, shards, kinds):
    s = x.shape[0]
    n_steps = s // tm
    fwd_step, diag_step = (8 * n_steps) // 16, (13 * n_steps) // 16
    n_w = len(shards)

    def body(x_ref, mod_ref, w_ref, b_ref, *rest):
        shard_refs = rest[:n_w]
        h1_ref, q_ref, kv_ref, gu_ref, gv_ref, xb_ref = rest[n_w:n_w + 6]
        gathered_refs = rest[n_w + 6:2 * n_w + 6]
        send_sems, recv_sems = rest[2 * n_w + 6:]
        i = pl.program_id(0)
        gather = _WeightGather(shard_refs, gathered_refs, kinds, send_sems, recv_sems)

        @pl.when(i == 0)
        def _():
            gather.start()

        xv = x_ref[...]
        xb_ref[...] = xv.astype(BF16)
        h1 = (xv * (1.0 + mod_ref[1:2, :]) + mod_ref[0:1, :]).astype(BF16)
        h1_ref[...] = h1
        proj = _dot_nt(h1, w_ref[...]) + b_ref[...]
        q_ref[...] = (proj[:, :ATTN_W] * Q_SCALE).astype(BF16)
        kv_ref[...] = proj[:, ATTN_W:ATTN_W + 2 * KV_W].astype(BF16)
        gu_ref[...] = proj[:, ATTN_W + 2 * KV_W:ATTN_W + 2 * KV_W + GMLP_W]
        gv_ref[...] = proj[:, ATTN_W + 2 * KV_W + GMLP_W:]

        @pl.when(i == fwd_step)
        def _():
            gather.forward()

        @pl.when(i == diag_step)
        def _():
            gather.forward_diagonal()

        @pl.when(i == n_steps - 1)
        def _():
            gather.finish()

    row = lambda w: pl.BlockSpec((tm, w), lambda i: (i, 0))
    outs = pl.pallas_call(
        body, name="fwd_in",
        out_shape=[jax.ShapeDtypeStruct((s, D_MODEL), BF16), jax.ShapeDtypeStruct((s, ATTN_W), BF16),
                   jax.ShapeDtypeStruct((s, 2 * KV_W), BF16), jax.ShapeDtypeStruct((s, GMLP_W), F32),
                   jax.ShapeDtypeStruct((s, GMLP_W), F32), jax.ShapeDtypeStruct((s, D_MODEL), BF16)]
        + [jax.ShapeDtypeStruct(_gathered_shape(sh, k), BF16) for sh, k in zip(shards, kinds)],
        grid=(n_steps,),
        in_specs=[row(D_MODEL), _const_spec((8, D_MODEL)), _const_spec(w_in.shape), _const_spec((1, IN_W))]
        + [ANY] * n_w,
        out_specs=[row(D_MODEL), row(ATTN_W), row(2 * KV_W), row(GMLP_W), row(GMLP_W), row(D_MODEL)] + [ANY] * n_w,
        scratch_shapes=_WeightGather.sems(n_w),
        compiler_params=_params(("arbitrary",)),
    )(x, modr, w_in, b_in, *shards)
    return outs[:6], outs[6:]


def _kv_variants(kk):
    kf = kk.astype(F32)
    lane = lax.broadcasted_iota(jnp.int32, kf.shape, 1)
    low = lane < HEAD_DIM
    k0_lo = jnp.where(low, kf, 0.0)
    k1_hi = jnp.where(low, 0.0, kf)
    k0_hi = pltpu.roll(k0_lo, HEAD_DIM, 1)
    k1_lo = pltpu.roll(k1_hi, HEAD_DIM, 1)
    return ((k0_lo.astype(BF16), k0_hi.astype(BF16)), (k1_lo.astype(BF16), k1_hi.astype(BF16)))


def _head_kv(h):
    return h // (N_HEADS // N_KV), h % 2


MIX_GROUP = 2


def _interleave(*gens):
    results = [None] * len(gens)
    active = list(enumerate(gens))
    while active:
        still = []
        for i, g in active:
            try:
                next(g)
                still.append((i, g))
            except StopIteration as done:
                results[i] = done.value
        active = still
    return results


def _attn_block_fwd(q_blk, kk, vv, bias_ref, sinks_ref, first_mask):
    kvar = _kv_variants(kk)
    vvar = _kv_variants(vv)
    heads = range(N_HEADS)
    q_pairs = [q_blk[:, (h // 2) * LANES:(h // 2 + 1) * LANES] for h in heads]
    logits = [_dot_nt(q_pairs[h], kvar[_head_kv(h)[0]][_head_kv(h)[1]]) + bias_ref[h] for h in heads]
    if first_mask is not None:
        logits = [jnp.where(first_mask, NEG_INF, lg) for lg in logits]
    yield
    ms = [jnp.maximum(jnp.max(logits[h], axis=-1, keepdims=True), sinks_ref[h]) for h in heads]
    yield
    es = [jnp.exp(logits[h] - ms[h]) for h in heads]
    ess = [jnp.exp(sinks_ref[h] - ms[h]) for h in heads]
    yield
    invs = [1.0 / (jnp.sum(es[h], axis=-1, keepdims=True) + ess[h]) for h in heads]
    probs = [(es[h] * invs[h], ess[h] * invs[h]) for h in heads]
    yield
    outs = [_dot(probs[h][0].astype(BF16), vvar[_head_kv(h)[0]][_head_kv(h)[1]]) for h in heads]
    pairs = [outs[2 * i] + outs[2 * i + 1] for i in range(N_HEADS // 2)]
    return jnp.concatenate(pairs, axis=1), probs, kvar, vvar


def _gmlp_chunk_fwd(gu, gv, ln_g, ln_b, wsm_ref, bsx, amat):
    u, tu = _gelu(gu)
    a, ta = _gelu(gv)
    yield
    mean = _split_dot(a, amat)
    d = a - mean
    yield
    var = _split_dot(d * d, amat)
    yield
    rstd = lax.rsqrt(var + LN_EPS)
    xhat = d * rstd
    vb = (xhat * ln_g + ln_b).astype(BF16)
    yield
    lane = lax.broadcasted_iota(jnp.int32, (BLOCK, LANES), 1)
    low = lane < GROUP_DIM
    cols = []
    for pair in range(N_GROUPS // 2):
        vp = vb[:, pair * LANES:(pair + 1) * LANES]
        cols.append(jnp.where(low, _dot(wsm_ref[2 * pair], vp), _dot(wsm_ref[2 * pair + 1], vp)))
    mixedv = jnp.concatenate(cols, axis=1) + bsx
    return u * mixedv, (u, tu, ta, xhat, rstd, vb, mixedv)


def _rms(a, g):
    r = lax.rsqrt(jnp.mean(a * a, axis=-1, keepdims=True) + LN_EPS)
    return a * r * g, r


def _fwd_mix(q, kv, gu, gv, x, modr, bias, sinks, gln_g, gln_b, wsm, bsx, amat, aog, gog, w_out, ln1_g, ln1_b, tm,
             ffn_shards, ffn_kinds):
    s = x.shape[0]
    nb = tm // BLOCK
    n_steps = s // tm
    fwd_step, diag_step = (7 * n_steps) // 16, (12 * n_steps) // 16
    n_w = len(ffn_shards)

    def body(q_ref, kv_ref, kvp_ref, gu_ref, gv_ref, x_ref, mod_ref, bias_ref, sinks_ref, glng_ref, glnb_ref, wsm_ref,
             bsx_ref, amat_ref, aog_ref, gog_ref, wout_ref, ln1g_ref, ln1b_ref, *rest):
        shard_refs = rest[:n_w]
        x1_ref, x1b_ref, y_ref, mixed_ref = rest[n_w:n_w + 4]
        gathered_refs = rest[n_w + 4:2 * n_w + 4]
        mix_scr, send_sems, recv_sems = rest[2 * n_w + 4:]
        i = pl.program_id(0)
        gather = _WeightGather(shard_refs, gathered_refs, ffn_kinds, send_sems, recv_sems)

        @pl.when(i == 0)
        def _():
            gather.start()

        col = lax.broadcasted_iota(jnp.int32, (BLOCK, 2 * BLOCK), 1)
        for b0 in range(0, nb, MIX_GROUP):
            gens = []
            for b in range(b0, min(b0 + MIX_GROUP, nb)):
                r0 = b * BLOCK
                if b == 0:
                    kvprev = kvp_ref[...]
                    first_mask = (col < BLOCK) & (i == 0)
                else:
                    kvprev = kv_ref[r0 - BLOCK:r0, :]
                    first_mask = None
                kvcur = kv_ref[r0:r0 + BLOCK, :]
                kk = jnp.concatenate([kvprev[:, :KV_W], kvcur[:, :KV_W]], axis=0)
                vv = jnp.concatenate([kvprev[:, KV_W:], kvcur[:, KV_W:]], axis=0)
                gens.append(_attn_block_fwd(q_ref[r0:r0 + BLOCK, :], kk, vv, bias_ref, sinks_ref, first_mask))
                gens.append(_gmlp_chunk_fwd(gu_ref[r0:r0 + BLOCK, :], gv_ref[r0:r0 + BLOCK, :], glng_ref[...],
                                            glnb_ref[...], wsm_ref, bsx_ref[...], amat_ref[...]))
            res = _interleave(*gens)
            for k, b in enumerate(range(b0, min(b0 + MIX_GROUP, nb))):
                r0 = b * BLOCK
                na, _ = _rms(res[2 * k][0], aog_ref[...])
                ng, _ = _rms(res[2 * k + 1][0], gog_ref[...])
                mix_scr[r0:r0 + BLOCK, :ATTN_W] = na.astype(BF16)
                mix_scr[r0:r0 + BLOCK, ATTN_W:] = ng.astype(BF16)
        mixed = mix_scr[...]
        mixed_ref[...] = mixed
        y = _dot(mixed, wout_ref[...])
        y_ref[...] = y.astype(BF16)
        z1 = ALPHA * x_ref[...] + mod_ref[2:3, :] * y
        xhat, _ = _ln_stats(z1)
        x1 = xhat * ln1g_ref[...] + ln1b_ref[...]
        x1_ref[...] = x1
        x1b_ref[...] = x1.astype(BF16)

        @pl.when(i == fwd_step)
        def _():
            gather.forward()

        @pl.when(i == diag_step)
        def _():
            gather.forward_diagonal()

        @pl.when(i == n_steps - 1)
        def _():
            gather.finish()

    row = lambda w: pl.BlockSpec((tm, w), lambda i: (i, 0))
    prev = pl.BlockSpec((BLOCK, 2 * KV_W), lambda i: (jnp.maximum(i * nb - 1, 0), 0))
    outs = pl.pallas_call(
        body, name="fwd_mix",
        out_shape=[jax.ShapeDtypeStruct((s, D_MODEL), F32)] + [jax.ShapeDtypeStruct((s, D_MODEL), BF16)] * 3
        + [jax.ShapeDtypeStruct(_gathered_shape(sh, k), BF16) for sh, k in zip(ffn_shards, ffn_kinds)],
        grid=(n_steps,),
        in_specs=[row(ATTN_W), row(2 * KV_W), prev, row(GMLP_W), row(GMLP_W), row(D_MODEL), _const_spec((8, D_MODEL)),
                  _const_spec((N_HEADS, BLOCK, 2 * BLOCK)), pl.BlockSpec(memory_space=pltpu.SMEM),
                  _const_spec((1, GMLP_W)), _const_spec((1, GMLP_W)), _const_spec((N_GROUPS, BLOCK, BLOCK)),
                  _const_spec((BLOCK, GMLP_W)), _const_spec((GMLP_W, GMLP_W)), _const_spec((1, ATTN_W)),
                  _const_spec((1, GMLP_W)), _const_spec((D_MODEL, D_MODEL)), _const_spec((1, D_MODEL)),
                  _const_spec((1, D_MODEL))] + [ANY] * n_w,
        out_specs=[row(D_MODEL)] * 4 + [ANY] * n_w,
        scratch_shapes=[pltpu.VMEM((tm, D_MODEL), BF16)] + _WeightGather.sems(n_w),
        compiler_params=_params(("arbitrary",)),
    )(q, kv, kv, gu, gv, x, modr, bias, sinks, gln_g, gln_b, wsm, bsx, amat, aog, gog, w_out, ln1_g, ln1_b, *ffn_shards)
    return outs[:4], outs[4:]


FF_BLOCKS = N_CHIPS // 2
FF_CHUNK = D_FF // FF_BLOCKS
FFN_SUB = 256


def _sigmoid(x):
    return 1.0 / (1.0 + jnp.exp(-x))


def _fwd_ffn(x1, target, modr, ln2_g, ln2_b, w_gu, w_dn, tm):
    s = x1.shape[0]

    def body(x1_ref, t_ref, mod_ref, g_ref, b_ref, wgu_ref, wdn_ref, h2_ref, act_ref, dy2_ref, dx1a_ref, acc_ref):
        @pl.when(pl.program_id(0) == 0)
        def _():
            acc_ref[...] = jnp.zeros_like(acc_ref)

        x1v = x1_ref[...]
        h2 = (x1v * (1.0 + mod_ref[4:5, :]) + mod_ref[3:4, :]).astype(BF16)
        h2_ref[...] = h2
        y2 = None
        for cc in range(FF_BLOCKS):
            c0 = cc * FF_CHUNK
            gate = _dot(h2, wgu_ref[cc])
            up = _dot(h2, wgu_ref[FF_BLOCKS + cc])
            act_ref[:, c0:c0 + FF_CHUNK] = gate.astype(BF16)
            act_ref[:, D_FF + c0:D_FF + c0 + FF_CHUNK] = up.astype(BF16)
            a = (gate * _sigmoid(gate) * up).astype(BF16)
            part = _dot(a, wdn_ref[c0:c0 + FF_CHUNK, :])
            y2 = part if y2 is None else y2 + part
        g2 = mod_ref[5:6, :]
        z2 = ALPHA * x1v + g2 * y2
        xhat, rstd = _ln_stats(z2)
        gain = g_ref[...]
        diff = xhat * gain + b_ref[...] - t_ref[...]
        dx2 = diff * (1.0 / D_MODEL)
        dz2 = _ln_bwd(dx2 * gain, xhat, rstd)
        dx1a_ref[...] = ALPHA * dz2
        dy2_ref[...] = (g2 * dz2).astype(BF16)
        acc_ref[0:1, :] += _colsum(diff * diff)
        acc_ref[1:2, :] += _colsum(dx2 * xhat)
        acc_ref[2:3, :] += _colsum(dx2)
        acc_ref[3:4, :] += _colsum(dz2 * y2)

    row = lambda w: pl.BlockSpec((tm, w), lambda i: (i, 0))
    return pl.pallas_call(
        body, name="fwd_ffn",
        out_shape=(jax.ShapeDtypeStruct((s, D_MODEL), BF16), jax.ShapeDtypeStruct((s, 2 * D_FF), BF16),
                   jax.ShapeDtypeStruct((s, D_MODEL), BF16), jax.ShapeDtypeStruct((s, D_MODEL), F32),
                   jax.ShapeDtypeStruct((8, D_MODEL), F32)),
        grid=(s // tm,),
        in_specs=[row(D_MODEL), row(D_MODEL), _const_spec((8, D_MODEL)), _const_spec((1, D_MODEL)),
                  _const_spec((1, D_MODEL)), _const_spec((N_CHIPS, D_MODEL, FF_CHUNK), single=True),
                  _const_spec((D_FF, D_MODEL), single=True)],
        out_specs=(row(D_MODEL), row(2 * D_FF), row(D_MODEL), row(D_MODEL), _const_spec((8, D_MODEL))),
        compiler_params=_params(("arbitrary",)),
    )(x1, target, modr, ln2_g, ln2_b, w_gu, w_dn)


def _bwd_ffn(dy2, act, w_gu, w_dn, tm):
    s = dy2.shape[0]

    def body(dy2_ref, act_ref, wgu_ref, wdn_ref, a_ref, dgu_ref, dh2_ref):
        dy2v = dy2_ref[...]
        dh2 = None
        for cc in range(FF_BLOCKS):
            c0 = cc * FF_CHUNK
            da = _dot_nt(dy2v, wdn_ref[c0:c0 + FF_CHUNK, :])
            gate = act_ref[:, c0:c0 + FF_CHUNK].astype(F32)
            up = act_ref[:, D_FF + c0:D_FF + c0 + FF_CHUNK].astype(F32)
            sg = _sigmoid(gate)
            sl = gate * sg
            a_ref[:, c0:c0 + FF_CHUNK] = (sl * up).astype(BF16)
            dgate = (da * up * (sg * (1.0 + gate * (1.0 - sg)))).astype(BF16)
            dup = (da * sl).astype(BF16)
            dgu_ref[:, c0:c0 + FF_CHUNK] = dgate
            dgu_ref[:, D_FF + c0:D_FF + c0 + FF_CHUNK] = dup
            part = _dot_nt(dgate, wgu_ref[cc]) + _dot_nt(dup, wgu_ref[FF_BLOCKS + cc])
            dh2 = part if dh2 is None else dh2 + part
        dh2_ref[...] = dh2.astype(BF16)

    row = lambda w: pl.BlockSpec((tm, w), lambda i: (i, 0))
    return pl.pallas_call(
        body, name="bwd_ffn",
        out_shape=(jax.ShapeDtypeStruct((s, D_FF), BF16), jax.ShapeDtypeStruct((s, 2 * D_FF), BF16),
                   jax.ShapeDtypeStruct((s, D_MODEL), BF16)),
        grid=(s // tm,),
        in_specs=[row(D_MODEL), row(2 * D_FF), _const_spec((N_CHIPS, D_MODEL, FF_CHUNK), single=True),
                  _const_spec((D_FF, D_MODEL), single=True)],
        out_specs=(row(D_FF), row(2 * D_FF), row(D_MODEL)),
        compiler_params=_params(("parallel",)),
    )(dy2, act, w_gu, w_dn)


def _bwd_mid(dh2, dx1a, x1, x, y, modr, ln1_g, w_out, tm, swap_fulls, swap_kinds):
    s = x.shape[0]
    n_steps = s // tm
    n_g = len(swap_fulls)

    def body(dh2_ref, dx1a_ref, x1_ref, x_ref, y_ref, mod_ref, g_ref, wout_ref, *rest):
        full_refs = rest[:n_g]
        dxa_ref, dy_ref, dmix_ref, acc_ref = rest[n_g:n_g + 4]
        got_refs = rest[n_g + 4:2 * n_g + 4]
        swap = _HalfSwap(full_refs, got_refs, swap_kinds, *rest[2 * n_g + 4:])
        i = pl.program_id(0)

        @pl.when(i == 0)
        def _():
            swap.start()
            acc_ref[...] = jnp.zeros_like(acc_ref)

        dh2 = dh2_ref[...].astype(F32)
        x1v = x1_ref[...].astype(F32)
        yv = y_ref[...].astype(F32)
        g1 = mod_ref[2:3, :]
        dx1 = dx1a_ref[...] + dh2 * (1.0 + mod_ref[4:5, :])
        z1 = ALPHA * x_ref[...] + g1 * yv
        xhat, rstd = _ln_stats(z1)
        dz1 = _ln_bwd(dx1 * g_ref[...], xhat, rstd)
        dxa_ref[...] = (ALPHA * dz1).astype(BF16)
        dy = (g1 * dz1).astype(BF16)
        dy_ref[...] = dy
        dmix_ref[...] = _dot_nt(dy, wout_ref[...]).astype(BF16)
        acc_ref[0:1, :] += _colsum(dh2 * x1v)
        acc_ref[1:2, :] += _colsum(dh2)
        acc_ref[2:3, :] += _colsum(dx1 * xhat)
        acc_ref[3:4, :] += _colsum(dx1)
        acc_ref[4:5, :] += _colsum(dz1 * yv)

        @pl.when(i == n_steps - 1)
        def _():
            swap.wait()

    row = lambda w: pl.BlockSpec((tm, w), lambda i: (i, 0))
    outs = pl.pallas_call(
        body, name="bwd_mid",
        out_shape=[jax.ShapeDtypeStruct((s, D_MODEL), BF16), jax.ShapeDtypeStruct((s, D_MODEL), BF16),
                   jax.ShapeDtypeStruct((s, D_MODEL), BF16), jax.ShapeDtypeStruct((8, D_MODEL), F32)]
        + _HalfSwap.out_shapes(swap_fulls, swap_kinds),
        grid=(n_steps,),
        in_specs=[row(D_MODEL)] * 5 + [_const_spec((8, D_MODEL)), _const_spec((1, D_MODEL)),
                                       _const_spec((D_MODEL, D_MODEL))] + [ANY] * n_g,
        out_specs=[row(D_MODEL), row(D_MODEL), row(D_MODEL), _const_spec((8, D_MODEL))] + [ANY] * n_g,
        scratch_shapes=_HalfSwap.sems(n_g),
        compiler_params=_params(("arbitrary",)),
    )(dh2, dx1a, x1, x, y, modr, ln1_g, w_out, *swap_fulls)
    return outs[:4], outs[4:]


def _fold_kv(t0, t1):
    lane = lax.broadcasted_iota(jnp.int32, t0.shape, 1)
    f0 = t0 + pltpu.roll(t0, HEAD_DIM, 1)
    f1 = t1 + pltpu.roll(t1, HEAD_DIM, 1)
    return jnp.where(lane < HEAD_DIM, f0, f1)


def _bwd_mix(q, kv, gu, gv, dmix, bias, sinks, gln_g, gln_b, wsm, bsx, amat, aog, gog, grad_parts, grad_kinds):
    s = q.shape[0]
    tile = 2 * BLOCK
    n_steps = s // tile
    n_g = len(grad_parts)

    def body(q_ref, kv_ref, kvp_ref, gu_ref, gv_ref, dmix_ref, bias_ref, sinks_ref, glng_ref, glnb_ref, wsm_ref,
             bsx_ref, amat_ref, aog_ref, gog_ref, *rest):
        part_refs = rest[:n_g]
        dq_ref, dkv_ref, dgu_ref, dgv_ref, gbias_ref, dws_ref, dbs_ref, vec_ref, dsink_ref = rest[n_g:n_g + 9]
        rx_refs = rest[n_g + 9:2 * n_g + 9]
        carry, done, send_sems, recv_sems = rest[2 * n_g + 9:]
        n = pl.program_id(0)
        exchange = _ChipExchange(part_refs, rx_refs, grad_kinds, send_sems, recv_sems)

        @pl.when(n == 0)
        def _():
            exchange.start()
            carry[...] = jnp.zeros_like(carry)
            done[...] = jnp.zeros_like(done)
            gbias_ref[...] = jnp.zeros_like(gbias_ref)
            dws_ref[...] = jnp.zeros_like(dws_ref)
            dbs_ref[...] = jnp.zeros_like(dbs_ref)
            vec_ref[...] = jnp.zeros_like(vec_ref)
            dsink_ref[...] = jnp.zeros_like(dsink_ref)

        @pl.when(n == n_steps)
        def _():
            dkv_ref[:BLOCK, :] = done[...].astype(BF16)
            dkv_ref[BLOCK:, :] = carry[...].astype(BF16)
            exchange.wait()

        @pl.when(n < n_steps)
        def _():
            col = lax.broadcasted_iota(jnp.int32, (BLOCK, 2 * BLOCK), 1)
            lane = lax.broadcasted_iota(jnp.int32, (BLOCK, LANES), 1)
            low = lane < HEAD_DIM
            rows = [slice(0, BLOCK), slice(BLOCK, tile)]
            kv_blocks = [kvp_ref[...], kv_ref[rows[0], :], kv_ref[rows[1], :]]
            masks = [(col < BLOCK) & (n == 0), None]
            q_blks = [q_ref[r, :] for r in rows]
            fwd = []
            for b in range(2):
                kk = jnp.concatenate([kv_blocks[b][:, :KV_W], kv_blocks[b + 1][:, :KV_W]], axis=0)
                vv = jnp.concatenate([kv_blocks[b][:, KV_W:], kv_blocks[b + 1][:, KV_W:]], axis=0)
                fwd.append(_attn_block_fwd(q_blks[b], kk, vv, bias_ref, sinks_ref, masks[b]))
                fwd.append(_gmlp_chunk_fwd(gu_ref[rows[b], :], gv_ref[rows[b], :], glng_ref[...], glnb_ref[...],
                                           wsm_ref, bsx_ref[...], amat_ref[...]))
            res = _interleave(*fwd[:2]) + _interleave(*fwd[2:])

            def gating_bwd(b, d_gm, saved):
                u, tu, ta, xhat, rstd, vb, mixedv = saved
                dgu_ref[rows[b], :] = (d_gm * mixedv * _gelu_grad(gu_ref[rows[b], :], tu)).astype(BF16)
                dmx = d_gm * u
                dmxb = dmx.astype(BF16)
                yield
                dvn_cols, dws = [], []
                for pair in range(N_GROUPS // 2):
                    dp_ = dmxb[:, pair * LANES:(pair + 1) * LANES]
                    vp = vb[:, pair * LANES:(pair + 1) * LANES]
                    dvn_cols.append(
                        jnp.where(low, _dot_tn(wsm_ref[2 * pair], dp_), _dot_tn(wsm_ref[2 * pair + 1], dp_)))
                    zero = jnp.zeros_like(dp_)
                    dws.append(_dot_nt(jnp.where(low, dp_, zero), vp))
                    dws.append(_dot_nt(jnp.where(low, zero, dp_), vp))
                dvn = jnp.concatenate(dvn_cols, axis=1)
                yield
                dxh = dvn * glng_ref[...]
                am = amat_ref[...]
                m1 = _split_dot(dxh, am)
                m2 = _split_dot(dxh * xhat, am)
                yield
                da = rstd * (dxh - m1 - xhat * m2)
                dgv_ref[rows[b], :] = (da * _gelu_grad(gv_ref[rows[b], :], ta)).astype(BF16)
                return dmx, dws, _colsum(dvn * xhat), _colsum(dvn)

            def attention_bwd(b, d_attn, probs, kvar, vvar):
                heads = range(N_HEADS)
                sels = [low if h % 2 == 0 else jnp.logical_not(low) for h in heads]
                pair_of = lambda a, h: a[:, (h // 2) * LANES:(h // 2 + 1) * LANES]
                do_hs = [jnp.where(sels[h], pair_of(d_attn, h), 0.0).astype(BF16) for h in heads]
                q_hs = [jnp.where(sels[h], pair_of(q_blks[b], h), jnp.zeros((BLOCK, LANES), BF16)) for h in heads]
                dps = [_dot_nt(do_hs[h], vvar[_head_kv(h)[0]][_head_kv(h)[1]]) for h in heads]
                yield
                deltas = [jnp.sum(probs[h][0] * dps[h], axis=-1, keepdims=True) for h in heads]
                yield
                dss = [probs[h][0] * (dps[h] - deltas[h]) for h in heads]
                dsinks = [-(probs[h][1] * deltas[h]) for h in heads]
                dsbs = [ds.astype(BF16) for ds in dss]
                pbs = [probs[h][0].astype(BF16) for h in heads]
                yield
                dqs = [_dot(dsbs[h], kvar[_head_kv(h)[0]][_head_kv(h)[1]]) for h in heads]
                tks = [_dot_tn(dsbs[h], q_hs[h]) for h in heads]
                tvs = [_dot_tn(pbs[h], do_hs[h]) for h in heads]
                dq_cols = [dqs[2 * i] + dqs[2 * i + 1] for i in range(N_HEADS // 2)]
                dq_ref[rows[b], :] = (jnp.concatenate(dq_cols, axis=1) * Q_SCALE).astype(BF16)
                per_kv = N_HEADS // N_KV
                kv_sum = lambda ts, kvh: sum(ts[kvh * per_kv + 1:(kvh + 1) * per_kv], ts[kvh * per_kv])
                dkk = _fold_kv(kv_sum(tks, 0), kv_sum(tks, 1))
                dvv = _fold_kv(kv_sum(tvs, 0), kv_sum(tvs, 1))
                return jnp.concatenate([dkk, dvv], axis=1), dss, dsinks

            bwd, rms_g = [], []
            for b in range(2):
                attn, probs, kvar, vvar = res[2 * b]
                gm, saved = res[2 * b + 1]
                na_unit, r_a = _rms(attn, 1.0)
                ng_unit, r_g = _rms(gm, 1.0)
                dmix = dmix_ref[rows[b], :].astype(F32)
                dn_a = dmix[:, :ATTN_W]
                dn_g = dmix[:, ATTN_W:]
                rms_g.append((_colsum(dn_a * na_unit), _colsum(dn_g * ng_unit)))
                t_a = dn_a * aog_ref[...]
                d_attn = r_a * t_a - na_unit * (r_a * jnp.mean(t_a * na_unit, axis=-1, keepdims=True))
                t_g = dn_g * gog_ref[...]
                d_gm = r_g * t_g - ng_unit * (r_g * jnp.mean(t_g * ng_unit, axis=-1, keepdims=True))
                bwd.append(attention_bwd(b, d_attn, probs, kvar, vvar))
                bwd.append(gating_bwd(b, d_gm, saved))
            (dkv_a, dss_a, dsk_a), (dmx_a, dws_a, glg_a, glb_a) = _interleave(*bwd[:2])
            (dkv_b, dss_b, dsk_b), (dmx_b, dws_b, glg_b, glb_b) = _interleave(*bwd[2:])

            vec_ref[0:1, :] += rms_g[0][0] + rms_g[1][0]
            vec_ref[1:2, :] += rms_g[0][1] + rms_g[1][1]
            vec_ref[2:3, :] += glg_a + glg_b
            vec_ref[3:4, :] += glb_a + glb_b
            dbs_ref[...] += dmx_a + dmx_b
            for g in range(N_GROUPS):
                dws_ref[g] += dws_a[g] + dws_b[g]
            for h in range(N_HEADS):
                gbias_ref[h] += dss_a[h] + dss_b[h]
                dsink_ref[h] += dsk_a[h] + dsk_b[h]

            dkv_ref[:BLOCK, :] = done[...].astype(BF16)
            dkv_ref[BLOCK:, :] = (carry[...] + dkv_a[:BLOCK]).astype(BF16)
            done[...] = dkv_a[BLOCK:] + dkv_b[:BLOCK]
            carry[...] = dkv_b[BLOCK:]

    last = n_steps - 1
    cur = lambda w: pl.BlockSpec((tile, w), lambda n: (jnp.minimum(n, last), 0))
    late = lambda w: pl.BlockSpec((tile, w), lambda n: (jnp.clip(n - 1, 0, last), 0))
    before = pl.BlockSpec((BLOCK, 2 * KV_W), lambda n: (jnp.clip(2 * n - 1, 0, 2 * last + 1), 0))
    outs = pl.pallas_call(
        body, name="bwd_mix",
        out_shape=[jax.ShapeDtypeStruct((s, ATTN_W), BF16), jax.ShapeDtypeStruct((s, 2 * KV_W), BF16),
                   jax.ShapeDtypeStruct((s, GMLP_W), BF16), jax.ShapeDtypeStruct((s, GMLP_W), BF16),
                   jax.ShapeDtypeStruct((N_HEADS, BLOCK, 2 * BLOCK), F32),
                   jax.ShapeDtypeStruct((N_GROUPS, BLOCK, BLOCK), F32),
                   jax.ShapeDtypeStruct((BLOCK, GMLP_W), F32), jax.ShapeDtypeStruct((8, GMLP_W), F32),
                   jax.ShapeDtypeStruct((N_HEADS, BLOCK, 1), F32)]
        + [jax.ShapeDtypeStruct(_rx_shape(p.shape, k), BF16) for p, k in zip(grad_parts, grad_kinds)],
        grid=(n_steps + 1,),
        in_specs=[cur(ATTN_W), cur(2 * KV_W), before, cur(GMLP_W), cur(GMLP_W), cur(D_MODEL),
                  _const_spec((N_HEADS, BLOCK, 2 * BLOCK)), pl.BlockSpec(memory_space=pltpu.SMEM),
                  _const_spec((1, GMLP_W)), _const_spec((1, GMLP_W)), _const_spec((N_GROUPS, BLOCK, BLOCK)),
                  _const_spec((BLOCK, GMLP_W)), _const_spec((GMLP_W, GMLP_W)), _const_spec((1, ATTN_W)),
                  _const_spec((1, GMLP_W))] + [ANY] * n_g,
        out_specs=[cur(ATTN_W), late(2 * KV_W), cur(GMLP_W), cur(GMLP_W),
                   _const_spec((N_HEADS, BLOCK, 2 * BLOCK)), _const_spec((N_GROUPS, BLOCK, BLOCK)),
                   _const_spec((BLOCK, GMLP_W)), _const_spec((8, GMLP_W)), _const_spec((N_HEADS, BLOCK, 1))]
        + [ANY] * n_g,
        scratch_shapes=[pltpu.VMEM((BLOCK, 2 * KV_W), F32), pltpu.VMEM((BLOCK, 2 * KV_W), F32)]
        + _ChipExchange.sems(n_g),
        compiler_params=_params(("arbitrary",)),
    )(q, kv, kv, gu, gv, dmix, bias, sinks, gln_g, gln_b, wsm, bsx, amat, aog, gog, *grad_parts)
    return outs[:9], outs[9:]


def _mix_finalize(gbias, bucket, dws, dbs, dsink):
    def body(gb_ref, bucket_ref, dws_ref, dbs_ref, dsink_ref, tall_ref):
        bk = bucket_ref[...]
        lane = lax.broadcasted_iota(jnp.int32, (N_BUCKETS, LANES), 1)
        rowi = lax.broadcasted_iota(jnp.int32, (N_BUCKETS, LANES), 0)
        drb = jnp.zeros((N_BUCKETS, LANES), F32)
        dsk = jnp.zeros((8, LANES), F32)
        lane8 = lax.broadcasted_iota(jnp.int32, (8, LANES), 1)
        for h in range(N_HEADS):
            g = gb_ref[h]
            for b in range(N_BUCKETS):
                tot = jnp.sum(_colsum(jnp.where(bk == float(b), g, 0.0)), axis=1, keepdims=True)
                drb = jnp.where((rowi == h) & (lane == b), tot, drb)
            sk = jnp.sum(dsink_ref[h], axis=0, keepdims=True)
            dsk = jnp.where(lane8 == h, sk, dsk)
        tall_ref[TALL_RB:TALL_RB + N_BUCKETS, :] = drb
        tall_ref[TALL_SK:TALL_SK + 8, :] = dsk
        ti = lax.broadcasted_iota(jnp.int32, (BLOCK, BLOCK), 0)
        ui = lax.broadcasted_iota(jnp.int32, (BLOCK, BLOCK), 1)
        for g in range(N_GROUPS):
            tall_ref[g * BLOCK:(g + 1) * BLOCK, :] = jnp.where(ti >= ui, dws_ref[g], 0.0)
        gi = lax.broadcasted_iota(jnp.int32, (GMLP_W, LANES), 0) // GROUP_DIM
        li = lax.broadcasted_iota(jnp.int32, (GMLP_W, LANES), 1)
        ind = jnp.where(gi == li, 1.0, 0.0).astype(BF16)
        d = dbs_ref[...]
        hi = d.astype(BF16)
        r1 = d - hi.astype(F32)
        mid = r1.astype(BF16)
        lo = (r1 - mid.astype(F32)).astype(BF16)
        dbsg = _dot(hi, ind) + _dot(mid, ind) + _dot(lo, ind)
        tall_ref[TALL_BS:TALL_BS + N_GROUPS, :] = dbsg.T[:N_GROUPS, :]

    return pl.pallas_call(
        body, name="mix_finalize", out_shape=jax.ShapeDtypeStruct((TALL_ROWS, LANES), F32), grid=(1,),
        in_specs=[_const_spec((N_HEADS, BLOCK, 2 * BLOCK)), _const_spec((BLOCK, 2 * BLOCK)),
                  _const_spec((N_GROUPS, BLOCK, BLOCK)), _const_spec((BLOCK, GMLP_W)),
                  _const_spec((N_HEADS, BLOCK, 1))],
        out_specs=_const_spec((TALL_ROWS, LANES)),
        compiler_params=_params(("arbitrary",)),
    )(gbias, bucket, dws, dbs, dsink)


def _bwd_in(dq, dkv, dgu, dgv, dxa, x, modr, w_in, tm):
    s = x.shape[0]

    def body(dq_ref, dkv_ref, dgu_ref, dgv_ref, dxa_ref, x_ref, mod_ref, w_ref, gx_ref, acc_ref, db_ref):
        @pl.when(pl.program_id(0) == 0)
        def _():
            acc_ref[...] = jnp.zeros_like(acc_ref)
            db_ref[...] = jnp.zeros_like(db_ref)

        dproj = jnp.concatenate([dq_ref[...], dkv_ref[...], dgu_ref[...], dgv_ref[...]], axis=1)
        dh1 = _dot(dproj, w_ref[...])
        gx_ref[...] = dxa_ref[...].astype(F32) + dh1 * (1.0 + mod_ref[1:2, :])
        acc_ref[0:1, :] += _colsum(dh1 * x_ref[...].astype(F32))
        acc_ref[1:2, :] += _colsum(dh1)
        db_ref[0:1, :] += _colsum(dproj.astype(F32))

    row = lambda w: pl.BlockSpec((tm, w), lambda i: (i, 0))
    return pl.pallas_call(
        body, name="bwd_in",
        out_shape=(jax.ShapeDtypeStruct((s, D_MODEL), F32), jax.ShapeDtypeStruct((8, D_MODEL), F32),
                   jax.ShapeDtypeStruct((8, IN_W), F32)),
        grid=(s // tm,),
        in_specs=[row(ATTN_W), row(2 * KV_W), row(GMLP_W), row(GMLP_W), row(D_MODEL), row(D_MODEL),
                  _const_spec((8, D_MODEL)), _const_spec(w_in.shape)],
        out_specs=(row(D_MODEL), _const_spec((8, D_MODEL)), _const_spec((8, IN_W))),
        compiler_params=_params(("arbitrary",)),
    )(dq, dkv, dgu, dgv, dxa, x, modr, w_in)


def _wgrad(a, bs, tm, tk, name, transposed=False, gather_vs=()):
    k_all, m = a.shape
    n = sum(b.shape[1] for b in bs)
    nk = k_all // tk
    nm = m // tm
    n_b = len(bs)
    n_v = len(gather_vs)

    def body(a_ref, *rest):
        b_refs, v_refs = rest[:n_b], rest[n_b:n_b + n_v]
        o_ref, ob_ref = rest[n_b + n_v:n_b + n_v + 2]
        vg_refs = rest[n_b + n_v + 2:n_b + 2 * n_v + 2]
        i, k = pl.program_id(0), pl.program_id(1)
        if n_v:
            gather = _Gather8(v_refs, vg_refs, *rest[n_b + 2 * n_v + 2:])

            @pl.when((i == 0) & (k == 0))
            def _():
                gather.start()

            @pl.when((i == nm - 1) & (k == 0))
            def _():
                gather.forward()

        @pl.when(k == 0)
        def _():
            o_ref[...] = jnp.zeros_like(o_ref)

        b = b_refs[0][...] if n_b == 1 else jnp.concatenate([r[...] for r in b_refs], axis=1)
        if transposed:
            o_ref[...] += _dot_tn(b, a_ref[...])
        else:
            o_ref[...] += _dot_tn(a_ref[...], b)

        @pl.when(k == nk - 1)
        def _():
            ob_ref[...] = o_ref[...].astype(BF16)

        if n_v:
            @pl.when((i == nm - 1) & (k == nk - 1))
            def _():
                gather.finish()

    if transposed:
        out_spec = pl.BlockSpec((n, tm), lambda i, k: (0, i))
        shape = (n, m)
    else:
        out_spec = pl.BlockSpec((tm, n), lambda i, k: (i, 0))
        shape = (m, n)
    outs = pl.pallas_call(
        body, name=name,
        out_shape=[jax.ShapeDtypeStruct(shape, F32), jax.ShapeDtypeStruct(shape, BF16)] + _gathered8_shapes(gather_vs),
        grid=(nm, nk),
        in_specs=[pl.BlockSpec((tk, tm), lambda i, k: (k, i))]
        + [pl.BlockSpec((tk, b.shape[1]), lambda i, k: (k, 0)) for b in bs] + [ANY] * n_v,
        out_specs=[out_spec, out_spec] + [ANY] * n_v,
        scratch_shapes=_Gather8.sems(n_v) if n_v else [],
        compiler_params=_params(("arbitrary", "arbitrary") if n_v else ("parallel", "arbitrary")),
    )(a, *bs, *gather_vs)
    return outs[0], outs[1], outs[2:]


def _adam_math(w, g, m, v):
    m2 = ADAM_B1 * m + (1.0 - ADAM_B1) * g
    v2 = ADAM_B2 * v + (1.0 - ADAM_B2) * (g * g)
    m_hat = m2 / (1.0 - ADAM_B1 ** ADAM_STEP)
    v_hat = v2 / (1.0 - ADAM_B2 ** ADAM_STEP)
    delta = -ADAM_LR * (m_hat / (jnp.sqrt(v_hat) + ADAM_EPS) + ADAM_WD * w)
    return delta, m2, v2


def _adam_halves(w, mine, got, m, v, tr, name):
    r, cc = w.shape
    h = r // 2
    nt = h // tr

    def body(c_ref, w_ref, mine_ref, got_ref, m_ref, v_ref, g_ref, d_ref, m2_ref, v2_ref):
        g = jnp.where(pl.program_id(0) == c_ref[0], mine_ref[...], got_ref[...])
        g_ref[...] = g
        d, m2, v2 = _adam_math(w_ref[...], g, m_ref[...], v_ref[...])
        d_ref[...] = d
        m2_ref[...] = m2
        v2_ref[...] = v2

    full = pl.BlockSpec((tr, cc), lambda hh, i, c_ref: (hh * nt + i, 0))
    half = pl.BlockSpec((tr, cc), lambda hh, i, c_ref: (i, 0))
    shp = jax.ShapeDtypeStruct((r, cc), F32)
    return pl.pallas_call(
        body, name=name, out_shape=(shp, shp, shp, shp),
        grid_spec=pltpu.PrefetchScalarGridSpec(
            num_scalar_prefetch=1, grid=(2, nt), in_specs=[full, half, half, full, full],
            out_specs=(full, full, full, full)),
        compiler_params=_params(("arbitrary", "arbitrary")),
    )(_core_index_scalar(), w, mine, got, m, v)


def _adam_w_ada(sc_t, dmod_all, w, m, v, tr):
    r, cc = w.shape

    def body(chip_ref, sct_ref, dm_ref, w_ref, m_ref, v_ref, g_ref, d_ref, m2_ref, v2_ref):
        g = sct_ref[:, 0:1] * dm_ref[0:1, :]
        for k in range(1, N_DEV):
            g = g + sct_ref[:, k:k + 1] * dm_ref[k:k + 1, :]
        g_ref[...] = g
        d, m2, v2 = _adam_math(w_ref[...], g, m_ref[...], v_ref[...])
        d_ref[...] = d
        m2_ref[...] = m2
        v2_ref[...] = v2

    spec = pl.BlockSpec((tr, cc), lambda i, chip_ref: (i, 0))
    shp = jax.ShapeDtypeStruct((r, cc), F32)
    return pl.pallas_call(
        body, name="adam_w_ada", out_shape=(shp, shp, shp, shp),
        grid_spec=pltpu.PrefetchScalarGridSpec(
            num_scalar_prefetch=1, grid=(r // tr,),
            in_specs=[pl.BlockSpec((tr, N_DEV), lambda i, chip_ref: (i, 0)),
                      pl.BlockSpec((N_DEV, cc), lambda i, chip_ref: (0, chip_ref[0])), spec, spec, spec],
            out_specs=(spec, spec, spec, spec)),
        compiler_params=_params(("parallel",)),
    )(_chip_index_scalar(), sc_t, dmod_all, w, m, v)


def _pack_wide(acc_i, acc_m, acc_f, db_in, vec):
    arrs = [acc_i, acc_m, acc_f, db_in, vec]
    i_, m_, f_, b_, v_ = range(5)
    src = {"b_in": (b_, 0), "ln1_g": (m_, 2), "ln1_b": (m_, 3), "ln2_g": (f_, 1), "ln2_b": (f_, 2),
           "gmlp_ln_g": (v_, 2), "gmlp_ln_b": (v_, 3), "attn_out_g": (v_, 0), "gmlp_out_g": (v_, 1), "loss": (f_, 0)}
    dmod = [(i_, 1), (i_, 0), (m_, 4), (m_, 1), (m_, 0), (f_, 3)]

    def body(*refs):
        ins, wide_ref = refs[:5], refs[5]
        wide_ref[...] = jnp.zeros_like(wide_ref)
        for k, (a, row) in enumerate(dmod):
            wide_ref[0:1, k * D_MODEL:(k + 1) * D_MODEL] = ins[a][row:row + 1, :]
        for name, (a, row) in src.items():
            r, off, n = WIDE_LAYOUT[name]
            wide_ref[r:r + 1, off:off + n] = ins[a][row:row + 1, :]

    return pl.pallas_call(
        body, name="pack_wide", out_shape=jax.ShapeDtypeStruct((8, WIDE_W), F32), grid=(1,),
        in_specs=[_const_spec(a.shape) for a in arrs], out_specs=_const_spec((8, WIDE_W)),
        compiler_params=_params(("arbitrary",)),
    )(*arrs)


def _adam_small(gw, gt, wide_wmv, w_s, b_s, rel_bias, sinks, after):
    names = list(WIDE_PARAMS)
    tall = [("gmlp_w_s", w_s), ("gmlp_b_s", b_s), ("rel_bias", rel_bias), ("attn_sinks", sinks)]
    ins = [gw, gt]
    for n in names:
        ins += list(wide_wmv[n])
    for _, t in tall:
        ins += list(t)
    n_in = len(ins)

    def body(*refs):
        gw_ref, gt_ref = refs[0], refs[1]
        wmv = refs[2:n_in]
        dmod_ref, loss_ref, loss1_ref = refs[n_in + 1:n_in + 4]
        outs = refs[n_in + 4:]

        def tall_sum(r0, nr):
            g = gt_ref[r0:r0 + nr, :]
            for d in range(1, N_DEV):
                g = g + gt_ref[d * TALL_ROWS + r0:d * TALL_ROWS + r0 + nr, :]
            return g

        def emit(k, g, w_ref, m_ref, v_ref):
            d, m2, v2 = _adam_math(w_ref[...], g, m_ref[...], v_ref[...])
            outs[4 * k][...] = g
            outs[4 * k + 1][...] = d
            outs[4 * k + 2][...] = m2
            outs[4 * k + 3][...] = v2

        gsum = gw_ref[0:8, :]
        for d in range(1, N_DEV):
            gsum = gsum + gw_ref[8 * d:8 * d + 8, :]
        for d in range(N_DEV):
            dmod_ref[d:d + 1, :] = gw_ref[8 * d:8 * d + 1, :]
        for k, n in enumerate(names):
            r, off, sz = WIDE_LAYOUT[n]
            emit(k, gsum[r:r + 1, off:off + sz], *wmv[3 * k:3 * k + 3])
        r, off, sz = WIDE_LAYOUT["loss"]
        tot = jnp.sum(gsum[r:r + 1, off:off + sz], axis=1, keepdims=True)
        loss_ref[...] = jnp.broadcast_to(tot * (0.5 / D_MODEL), loss_ref.shape)
        loss1_ref[...] = tot * (0.5 / D_MODEL)

        k0 = len(names)
        ws_refs = wmv[3 * k0:3 * k0 + 3]
        for g in range(N_GROUPS):
            rows = slice(g * BLOCK, (g + 1) * BLOCK)
            gg = tall_sum(g * BLOCK, BLOCK)
            d, m2, v2 = _adam_math(ws_refs[0][rows, :], gg, ws_refs[1][rows, :], ws_refs[2][rows, :])
            outs[4 * k0][rows, :] = gg
            outs[4 * k0 + 1][rows, :] = d
            outs[4 * k0 + 2][rows, :] = m2
            outs[4 * k0 + 3][rows, :] = v2
        emit(k0 + 1, tall_sum(TALL_BS, N_GROUPS), *wmv[3 * (k0 + 1):3 * (k0 + 1) + 3])
        emit(k0 + 2, tall_sum(TALL_RB, N_HEADS)[:, :N_BUCKETS], *wmv[3 * (k0 + 2):3 * (k0 + 2) + 3])
        emit(k0 + 3, tall_sum(TALL_SK, 8)[0:1, :N_HEADS], *wmv[3 * (k0 + 3):3 * (k0 + 3) + 3])

    out_shapes = [jax.ShapeDtypeStruct((N_DEV, WIDE_W), F32), jax.ShapeDtypeStruct((8, LANES), F32),
                  jax.ShapeDtypeStruct((1, 1), F32)]
    for n in names:
        out_shapes += [jax.ShapeDtypeStruct(wide_wmv[n][0].shape, F32)] * 4
    for _, t in tall:
        out_shapes += [jax.ShapeDtypeStruct(t[0].shape, F32)] * 4
    res = pl.pallas_call(
        body, name="adam_small", out_shape=out_shapes, grid=(1,),
        in_specs=[_const_spec(a.shape) for a in ins] + [ANY], out_specs=[_const_spec(o.shape) for o in out_shapes],
        compiler_params=_params(("arbitrary",)),
    )(*ins, after)
    out = {}
    for k, n in enumerate(names + [t[0] for t in tall]):
        out[n] = tuple(res[3 + 4 * k:7 + 4 * k])
    return res[0], res[1], res[2], out


def kernel(x, c, rel_bias, w_ada, b_ada, w_in, b_in, attn_sinks, gmlp_ln_g, gmlp_ln_b, gmlp_w_s, gmlp_b_s, attn_out_g, gmlp_out_g, w_out, ln1_g, ln1_b, w_gate_up, w_down, ln2_g, ln2_b, loss_target, m_rel_bias, m_w_ada, m_b_ada, m_w_in, m_b_in, m_attn_sinks, m_gmlp_ln_g, m_gmlp_ln_b, m_gmlp_w_s, m_gmlp_b_s, m_attn_out_g, m_gmlp_out_g, m_w_out, m_ln1_g, m_ln1_b, m_w_gate_up, m_w_down, m_ln2_g, m_ln2_b, v_rel_bias, v_w_ada, v_b_ada, v_w_in, v_b_in, v_attn_sinks, v_gmlp_ln_g, v_gmlp_ln_b, v_gmlp_w_s, v_gmlp_b_s, v_attn_out_g, v_gmlp_out_g, v_w_out, v_ln1_g, v_ln1_b, v_w_gate_up, v_w_down, v_ln2_g, v_ln2_b):
    ix, iy, _ = _my_pos()
    chip = 2 * ix + iy
    s = x.shape[1]
    xs = x[0]
    tgt = loss_target[0]
    tm_big = min(512, s)
    tm_ffn = min(FFN_SUB, s)

    w_in_s, w_out_s = w_in[0].T.astype(BF16), w_out[0].astype(BF16)
    w_gu_s, w_dn_s = w_gate_up[0].astype(BF16), w_down[0].astype(BF16)
    sc_all, modr, (w_in_g, w_out_g) = _prologue(jnp.pad(c, ((0, 7), (0, 0))), w_ada[0], b_ada, [w_in_s, w_out_s])
    w_in_f = _insert_own(w_in_g, w_in_s, "blk", chip).reshape(IN_W, D_MODEL)

    bucket = _bucket_table()
    bias, wsm = _prep_tables(bucket, rel_bias.T, gmlp_w_s[0])
    bsx = jnp.repeat(gmlp_b_s[0].T, GROUP_DIM, axis=1)
    amat = _group_mean_matrix()
    sinks = attn_sinks[0]

    (h1, q, kv, gu, gv, xb), (w_dn_g,) = _fwd_in(xs, modr, w_in_f, b_in, tm_big, [w_dn_s], ["blk"])
    w_out_f = _insert_own(w_out_g, w_out_s, "blk", chip).reshape(D_MODEL, D_MODEL)
    (x1, x1b, y, mixed), (w_gu_g,) = _fwd_mix(
        q, kv, gu, gv, xs, modr, bias, sinks, gmlp_ln_g, gmlp_ln_b, wsm, bsx, amat, attn_out_g, gmlp_out_g, w_out_f,
        ln1_g, ln1_b, tm_big, [w_gu_s], ["blk"])
    assert w_gate_up.shape[2] == FF_CHUNK
    w_gu_f = _insert_own(w_gu_g, w_gu_s, "blk", chip)
    w_dn_f = _insert_own(w_dn_g, w_dn_s, "blk", chip).reshape(D_FF, D_MODEL)
    h2, act, dy2, dx1a, acc_f = _fwd_ffn(x1, tgt, modr, ln2_g, ln2_b, w_gu_f, w_dn_f, min(2 * FFN_SUB, s))

    a_act, dgu_ff, dh2 = _bwd_ffn(dy2, act, w_gu_f, w_dn_f, min(FFN_SUB, s))
    g_dn, g_dn_b, _ = _wgrad(a_act, [dy2], D_FF // 2, min(1024, s), "wgrad_down")
    g_gu, g_gu_b, _ = _wgrad(h2, [dgu_ff], 512, min(512, s), "wgrad_gate_up")
    blk3 = lambda a, rows: a.reshape(N_CHIPS, rows, a.shape[1])
    (dxa, dy, dmix, acc_m), (got_dn, got_gu) = _bwd_mid(
        dh2, dx1a, x1b, xs, y, modr, ln1_g, w_out_f, tm_big, [blk3(g_dn_b, D_FF // N_CHIPS), g_gu_b], ["blk", "cols"])
    g_out, g_out_b, _ = _wgrad(mixed, [dy], 512, min(2048, s), "wgrad_out")
    (got_out,) = _swap_halves([blk3(g_out_b, D_MODEL // N_CHIPS)], ["blk"], "rs_swap_out")
    kinds_a = ["blk", "cols", "blk"]
    fulls_a = [blk3(g_dn, D_FF // N_CHIPS), g_gu, blk3(g_out, D_MODEL // N_CHIPS)]
    gots_a = [got_dn, got_gu, got_out]
    parts_a = [_add_halves(f, g, k, "rs_add_a%d" % i) for i, (f, g, k) in enumerate(zip(fulls_a, gots_a, kinds_a))]
    (dq, dkv, dgu, dgv, gbias, dws, dbs, vec, dsink), rxs_a = _bwd_mix(
        q, kv, gu, gv, dmix, bias, sinks, gmlp_ln_g, gmlp_ln_b, wsm, bsx, amat, attn_out_g, gmlp_out_g,
        [p[1] for p in parts_a], kinds_a)
    tall_g = _mix_finalize(gbias, bucket, dws, dbs, dsink)
    grad_x, acc_i, db_in = _bwd_in(dq, dkv, dgu, dgv, dxa, xb, modr, w_in_f, tm_big)

    wide_g = _pack_wide(acc_i, acc_m, acc_f, db_in, vec)
    full_in, full_in_b, (gw, gt) = _wgrad(h1, [dq, dkv, dgu, dgv], 512, min(1024, s), "wgrad_in", transposed=True,
                                          gather_vs=[wide_g, tall_g])
    (got_in,) = _swap_halves([blk3(full_in_b, IN_W // N_CHIPS)], ["blk"], "rs_swap_in")
    part_in = _add_halves(blk3(full_in, IN_W // N_CHIPS), got_in, "blk", "rs_add_in")
    in_send, in_recv, in_part, in_rx, token = _exchange_start(part_in[1], "rs_chips_in_start")
    wide_wmv ={"b_ada": (b_ada, m_b_ada, v_b_ada), "b_in": (b_in, m_b_in, v_b_in),
                "ln1_g": (ln1_g, m_ln1_g, v_ln1_g), "ln1_b": (ln1_b, m_ln1_b, v_ln1_b),
                "ln2_g": (ln2_g, m_ln2_g, v_ln2_g), "ln2_b": (ln2_b, m_ln2_b, v_ln2_b),
                "gmlp_ln_g": (gmlp_ln_g, m_gmlp_ln_g, v_gmlp_ln_g), "gmlp_ln_b": (gmlp_ln_b, m_gmlp_ln_b, v_gmlp_ln_b),
                "attn_out_g": (attn_out_g, m_attn_out_g, v_attn_out_g),
                "gmlp_out_g": (gmlp_out_g, m_gmlp_out_g, v_gmlp_out_g)}
    rows2 = lambda a: a.reshape(-1, a.shape[-1])
    dmod_all, loss_t, loss1, small = _adam_small(
        gw, gt, wide_wmv, tuple(rows2(a) for a in (gmlp_w_s, m_gmlp_w_s, v_gmlp_w_s)),
        tuple(rows2(a) for a in (gmlp_b_s, m_gmlp_b_s, v_gmlp_b_s)), (rel_bias.T, m_rel_bias.T, v_rel_bias.T),
        (attn_sinks, m_attn_sinks, v_attn_sinks), token)
    small["rel_bias"] = tuple(a.T for a in small["rel_bias"])
    loss = loss1.reshape(())

    g_ada, d_ada, m_ada, v_ada = _adam_w_ada(sc_all.T, dmod_all, w_ada[0], m_w_ada[0], v_w_ada[0], 256)

    sums = [(parts_a[0][0], rxs_a[0], 176), (parts_a[1][0], rxs_a[1], 256), (parts_a[2][0], rxs_a[2], 128)]
    mine = [_sum_chips(p, rx, tr, "rs_sum_%d" % i, loss_t) for i, (p, rx, tr) in enumerate(sums)]
    got = _share_halves(mine, "rs_share")
    gs_dn, d_dn, m_dn, v_dn = _adam_halves(w_down[0], mine[0], got[0], m_w_down[0], v_w_down[0], 176, "adam_w_down")
    gs_gu, d_gu, m_gu, v_gu = _adam_halves(w_gate_up[0], mine[1], got[1], m_w_gate_up[0], v_w_gate_up[0], 256,
                                           "adam_w_gate_up")
    gs_out, d_out, m_out, v_out = _adam_halves(w_out[0], mine[2], got[2], m_w_out[0], v_w_out[0], 128, "adam_w_out")

    rx_in = _exchange_wait(in_send, in_recv, in_part, in_rx, d_gu, "rs_chips_in_wait")
    mine_in = _sum_chips(part_in[0], rx_in, 112, "rs_sum_in", rx_in)
    (got_in_half,) = _share_halves([mine_in], "rs_share_in")
    in_t = _adam_halves(w_in[0].T, mine_in, got_in_half, m_w_in[0].T, v_w_in[0].T, 112, "adam_w_in")
    gs_in, d_in, m_in, v_in = (a.T for a in in_t)

    big = {"w_ada": (g_ada, d_ada, m_ada, v_ada), "w_in": (gs_in, d_in, m_in, v_in), "w_out": (gs_out, d_out, m_out, v_out),
           "w_gate_up": (gs_gu, d_gu, m_gu, v_gu), "w_down": (gs_dn, d_dn, m_dn, v_dn)}
    order = ["rel_bias", "w_ada", "b_ada", "w_in", "b_in", "attn_sinks", "gmlp_ln_g", "gmlp_ln_b", "gmlp_w_s", "gmlp_b_s",
             "attn_out_g", "gmlp_out_g", "w_out", "ln1_g", "ln1_b", "w_gate_up", "w_down", "ln2_g", "ln2_b"]
    shapes = {"gmlp_w_s": gmlp_w_s.shape, "gmlp_b_s": gmlp_b_s.shape}
    outs = [loss, grad_x[None]]
    for k in range(4):
        for name in order:
            if name in big:
                outs.append(big[name][k][None])
            elif name in shapes:
                outs.append(small[name][k].reshape(shapes[name]))
            else:
                outs.append(small[name][k])
    return tuple(outs)
```

```python
import math

import numpy as np
import jax
import jax.numpy as jnp
from jax import lax
from jax.experimental import pallas as pl
from jax.experimental.pallas import tpu as pltpu

F32 = jnp.float32
BF16 = jnp.bfloat16
MESH = pl.DeviceIdType.MESH

D_MODEL = 1024
N_HEADS = 8
N_KV = 2
HEAD_DIM = 64
ATTN_W = N_HEADS * HEAD_DIM
KV_W = N_KV * HEAD_DIM
N_GROUPS = 8
GROUP_DIM = 64
GMLP_W = N_GROUPS * GROUP_DIM
IN_W = ATTN_W + 2 * KV_W + 2 * GMLP_W
BLOCK = 128
N_BUCKETS = 32
MAX_DISTANCE = 128
D_FF = 2816
ALPHA = 2.0 ** 0.25
LN_EPS = 1e-5
NEG_INF = -1e30
ADAM_LR, ADAM_B1, ADAM_B2, ADAM_EPS, ADAM_WD, ADAM_STEP = 0.001, 0.9, 0.999, 1e-8, 0.01, 10
N_CHIPS = 4
N_DEV = 8
LANES = 128
V7X_VMEM_LIMIT = 56 * 2 ** 20
GELU_C = math.sqrt(2.0 / math.pi)
Q_SCALE = HEAD_DIM ** -0.5
ANY = pl.BlockSpec(memory_space=pl.ANY)

TALL_BS = N_GROUPS * BLOCK
TALL_RB = TALL_BS + 8
TALL_SK = TALL_RB + N_BUCKETS
TALL_ROWS = TALL_SK + 8
WIDE_W = 6 * D_MODEL
WIDE_LAYOUT = {
    "b_ada": (0, 0, 6 * D_MODEL),
    "b_in": (1, 0, IN_W), "ln1_g": (1, IN_W, D_MODEL), "ln1_b": (1, IN_W + D_MODEL, D_MODEL),
    "ln2_g": (1, IN_W + 2 * D_MODEL, D_MODEL), "ln2_b": (1, IN_W + 3 * D_MODEL, D_MODEL),
    "gmlp_ln_g": (2, 0, GMLP_W), "gmlp_ln_b": (2, GMLP_W, GMLP_W), "attn_out_g": (2, 2 * GMLP_W, ATTN_W),
    "gmlp_out_g": (2, 2 * GMLP_W + ATTN_W, GMLP_W), "loss": (2, 3 * GMLP_W + ATTN_W, D_MODEL)}
WIDE_PARAMS = tuple(n for n in WIDE_LAYOUT if n != "loss")


def _params(sem=None):
    return pltpu.CompilerParams(dimension_semantics=sem, vmem_limit_bytes=V7X_VMEM_LIMIT)


def _const_spec(shape, single=False):
    nd = len(shape)
    if single:
        return pl.BlockSpec(shape, lambda *_: (0,) * nd, pipeline_mode=pl.Buffered(1))
    return pl.BlockSpec(shape, lambda *_: (0,) * nd)


def _dot(a, b):
    return jnp.dot(a, b, preferred_element_type=F32)


def _dot_nt(a, b):
    return lax.dot_general(a, b, (((1,), (1,)), ((), ())), preferred_element_type=F32)


def _dot_tn(a, b):
    return lax.dot_general(a, b, (((0,), (0,)), ((), ())), preferred_element_type=F32)


def _gelu(x):
    t = jnp.tanh(GELU_C * (x + 0.044715 * x * x * x))
    return 0.5 * x * (1.0 + t), t


def _gelu_grad(x, t):
    return 0.5 * (1.0 + t) + 0.5 * x * (1.0 - t * t) * GELU_C * (1.0 + 3.0 * 0.044715 * x * x)


def _split_dot(x, a):
    hi = x.astype(BF16)
    lo = (x - hi.astype(F32)).astype(BF16)
    return _dot(hi, a) + _dot(lo, a)


def _group_mean_matrix():
    g = np.arange(GMLP_W) // GROUP_DIM
    return jnp.asarray((g[:, None] == g[None, :]).astype(np.float32) / GROUP_DIM, dtype=BF16)


def _ln_stats(z):
    mu = jnp.mean(z, axis=-1, keepdims=True)
    d = z - mu
    var = jnp.mean(d * d, axis=-1, keepdims=True)
    rstd = lax.rsqrt(var + LN_EPS)
    return d * rstd, rstd


def _ln_bwd(dxhat, xhat, rstd):
    m1 = jnp.mean(dxhat, axis=-1, keepdims=True)
    m2 = jnp.mean(dxhat * xhat, axis=-1, keepdims=True)
    return rstd * (dxhat - m1 - xhat * m2)


def _colsum(x):
    return jnp.sum(x, axis=0, keepdims=True)


def _my_pos():
    return lax.axis_index("x"), lax.axis_index("y"), lax.axis_index("c")


def _other_chips(x, y):
    return [(1 - x, y), (x, 1 - y), (1 - x, 1 - y)]


def _chip_index_scalar():
    ix, iy, _ = _my_pos()
    return jnp.reshape(2 * ix + iy, (1,)).astype(jnp.int32)


def _core_index_scalar():
    return jnp.reshape(lax.axis_index("c"), (1,)).astype(jnp.int32)


class _Gather8:
    def __init__(self, x_refs, out_refs, send_sems, recv_sems, local_sems):
        self.x_refs, self.out_refs = x_refs, out_refs
        self.send_sems, self.recv_sems, self.local_sems = send_sems, recv_sems, local_sems
        self.x, self.y, self.c = _my_pos()
        self.me, self.sibling = (self.x, self.y, self.c), (self.x, self.y, 1 - self.c)
        self.chips = _other_chips(self.x, self.y)

    def _rows(self, a, px, py, pc):
        m_per = self.x_refs[a].shape[0]
        return self.out_refs[a].at[pl.ds((4 * px + 2 * py + pc) * m_per, m_per), :]

    def _copy(self, a, k, block, to, src=None):
        return pltpu.make_async_remote_copy(
            src_ref=self._rows(a, *block) if src is None else src, dst_ref=self._rows(a, *block),
            send_sem=self.send_sems.at[7 * a + k], recv_sem=self.recv_sems.at[7 * a + k], device_id=to,
            device_id_type=MESH)

    def _local(self, a):
        return pltpu.make_async_copy(self.x_refs[a], self._rows(a, *self.me), self.local_sems.at[a])

    def start(self):
        for a in range(len(self.x_refs)):
            self._local(a).start()
            self._copy(a, 0, self.me, self.sibling, src=self.x_refs[a]).start()
            for j, chip in enumerate(self.chips):
                self._copy(a, 1 + j, self.me, (*chip, self.c), src=self.x_refs[a]).start()

    def forward(self):
        for a in range(len(self.x_refs)):
            for j, chip in enumerate(self.chips):
                self._copy(a, 1 + j, (*chip, self.c), self.me).wait_recv()
                self._copy(a, 4 + j, (*chip, self.c), self.sibling).start()

    def finish(self):
        for a in range(len(self.x_refs)):
            self._copy(a, 0, self.sibling, self.me).wait_recv()
            for j, chip in enumerate(self.chips):
                self._copy(a, 4 + j, (*chip, 1 - self.c), self.me).wait_recv()
        for a in range(len(self.x_refs)):
            for k in range(7):
                self._copy(a, k, self.me, self.me).wait_send()
            self._local(a).wait()

    @staticmethod
    def sems(n_v):
        return [pltpu.SemaphoreType.DMA((7 * n_v,)), pltpu.SemaphoreType.DMA((7 * n_v,)),
                pltpu.SemaphoreType.DMA((n_v,))]


def _gathered8_shapes(vs):
    return [jax.ShapeDtypeStruct((N_DEV * v.shape[0], v.shape[1]), v.dtype) for v in vs]


VMEM_WHOLE = pl.BlockSpec(memory_space=pltpu.VMEM)


def _prologue(c_pad, w_ada_s, b_ada, shards):
    n = w_ada_s.shape[1]
    n_w = len(shards)
    assert N_CHIPS * n == 6 * D_MODEL and n % LANES == 0

    def body(c_ref, w_ref, b_ref, *rest):
        shard_refs = rest[:n_w]
        sc_ref, modc_ref, modg_ref, modr_ref = rest[n_w:n_w + 4]
        gathered_refs = rest[n_w + 4:2 * n_w + 4]
        call_ref, w_vmem = rest[2 * n_w + 4:2 * n_w + 6]
        sems = rest[2 * n_w + 6:]
        ix, iy, ic = _my_pos()
        chip = 2 * ix + iy
        weights = _WeightGather(shard_refs, gathered_refs, ["blk"] * n_w, sems[0], sems[1])
        gather_c = _Gather8([c_ref], [call_ref], sems[2], sems[3], sems[4])
        gather_mod = _Gather8([modc_ref], [modg_ref], sems[5], sems[6], sems[7])
        load_w = pltpu.make_async_copy(w_ref, w_vmem, sems[8])
        weights.start()
        gather_c.start()
        load_w.start()
        gather_c.forward()
        gather_c.finish()
        cv = call_ref[...]
        sc = cv * _sigmoid(cv)
        a_hi = sc.astype(BF16)
        a_lo = (sc - a_hi.astype(F32)).astype(BF16)
        load_w.wait()
        w = w_vmem[...]
        w_hi = w.astype(BF16)
        w_lo = (w - w_hi.astype(F32)).astype(BF16)
        b = b_ref[:, 0:n]
        for k in range(1, N_CHIPS):
            b = jnp.where(chip == k, b_ref[:, k * n:(k + 1) * n], b)
        mod = _dot(a_hi, w_hi) + _dot(a_hi, w_lo) + _dot(a_lo, w_hi) + b
        for d in range(N_DEV):
            sc_ref[d:d + 1, :] = sc[8 * d:8 * d + 1, :]
            modc_ref[d:d + 1, :] = mod[8 * d:8 * d + 1, :]
        gather_mod.start()
        weights.forward()
        gather_mod.forward()
        gather_mod.finish()
        dev = 2 * chip + ic
        mine = jnp.concatenate([modg_ref[pl.ds(2 * 8 * k + dev, 1), :] for k in range(N_CHIPS)], axis=1)
        modr_ref[...] = jnp.zeros_like(modr_ref)
        for r in range(6):
            modr_ref[r:r + 1, :] = mine[:, r * D_MODEL:(r + 1) * D_MODEL]
        weights.forward_diagonal()
        weights.finish()

    outs = pl.pallas_call(
        body, name="prologue",
        out_shape=[jax.ShapeDtypeStruct((N_DEV, D_MODEL), F32), jax.ShapeDtypeStruct((N_DEV, n), F32),
                   jax.ShapeDtypeStruct((N_DEV * N_DEV, n), F32), jax.ShapeDtypeStruct((8, D_MODEL), F32)]
        + [jax.ShapeDtypeStruct(_gathered_shape(sh, "blk"), BF16) for sh in shards],
        in_specs=[VMEM_WHOLE, ANY, VMEM_WHOLE] + [ANY] * n_w,
        out_specs=[VMEM_WHOLE, VMEM_WHOLE, VMEM_WHOLE, VMEM_WHOLE] + [ANY] * n_w,
        scratch_shapes=[pltpu.VMEM((N_DEV * 8, D_MODEL), F32), pltpu.VMEM(w_ada_s.shape, F32)]
        + _WeightGather.sems(n_w) + _Gather8.sems(1) + _Gather8.sems(1) + [pltpu.SemaphoreType.DMA],
        compiler_params=pltpu.CompilerParams(vmem_limit_bytes=V7X_VMEM_LIMIT),
    )(c_pad, w_ada_s, b_ada, *shards)
    return outs[0], outs[3], outs[4:]


def _gathered_shape(shard, kind):
    r, cc = shard.shape
    return (N_CHIPS, r, cc) if kind == "blk" else (r, N_CHIPS * cc)


class _WeightGather:
    N_SEM = 8

    def __init__(self, shards, gathered, kinds, send_sems, recv_sems):
        self.shards, self.gathered, self.kinds = shards, gathered, kinds
        self.send_sems, self.recv_sems = send_sems, recv_sems
        self.x, self.y, self.c = _my_pos()
        self.me, self.sibling = (self.x, self.y, self.c), (self.x, self.y, 1 - self.c)
        self.nbr = ((1 - self.x, self.y), (self.x, 1 - self.y))
        self.diag = 2 * (1 - self.x) + (1 - self.y)

    def _dst(self, a, chip, pc, quarter=None):
        r, cc = self.shards[a].shape
        h = r // 2
        row0, rows = pc * h, h
        if quarter is not None:
            row0, rows = pc * h + quarter * (h // 2), h // 2
        g = self.gathered[a]
        if self.kinds[a] == "blk":
            return g.at[chip, pl.ds(row0, rows), :]
        return g.at[pl.ds(row0, rows), pl.ds(chip * cc, cc)]

    def _copy(self, a, k, region, to, src=None):
        return pltpu.make_async_remote_copy(
            src_ref=region if src is None else src, dst_ref=region, send_sem=self.send_sems.at[a * self.N_SEM + k],
            recv_sem=self.recv_sems.at[a * self.N_SEM + k], device_id=to, device_id_type=MESH)

    def _arrays(self):
        return range(len(self.shards))

    def start(self):
        my_chip = 2 * self.x + self.y
        for a in self._arrays():
            h = self.shards[a].shape[0] // 2
            mine = self.shards[a].at[pl.ds(self.c * h, h), :]
            for j, chip in enumerate(self.nbr):
                self._copy(a, j, self._dst(a, my_chip, self.c), (*chip, self.c), src=mine).start()

    def forward(self):
        for a in self._arrays():
            for j, chip in enumerate(self.nbr):
                cj = 2 * chip[0] + chip[1]
                half = self._dst(a, cj, self.c)
                self._copy(a, j, half, self.me).wait_recv()
                self._copy(a, 2 + j, half, self.sibling).start()
                other = self.nbr[1 - j]
                self._copy(a, 4 + j, self._dst(a, cj, self.c, quarter=j), (*other, self.c)).start()

    def forward_diagonal(self):
        for a in self._arrays():
            for j in range(2):
                quarter = self._dst(a, self.diag, self.c, quarter=j)
                self._copy(a, 4 + j, quarter, self.me).wait_recv()
                self._copy(a, 6 + j, quarter, self.sibling).start()

    def finish(self):
        for a in self._arrays():
            for j, chip in enumerate(self.nbr):
                self._copy(a, 2 + j, self._dst(a, 2 * chip[0] + chip[1], 1 - self.c), self.me).wait_recv()
                self._copy(a, 6 + j, self._dst(a, self.diag, 1 - self.c, quarter=j), self.me).wait_recv()
        for a in self._arrays():
            half = self._dst(a, self.diag, self.c)
            quarter = self._dst(a, self.diag, self.c, quarter=0)
            for k in range(self.N_SEM):
                self._copy(a, k, half if k < 4 else quarter, self.me).wait_send()

    @classmethod
    def sems(cls, n_arr):
        return [pltpu.SemaphoreType.DMA((n_arr * cls.N_SEM,)), pltpu.SemaphoreType.DMA((n_arr * cls.N_SEM,))]


def _insert_own(gathered, shard, kind, chip):
    if kind == "blk":
        return lax.dynamic_update_slice(gathered, shard[None], (chip, 0, 0))
    return lax.dynamic_update_slice(gathered, shard, (0, chip * shard.shape[1]))


def _half_of_full(ref, kind, pc):
    if kind == "blk":
        h = ref.shape[1] // 2
        return ref.at[:, pl.ds(pc * h, h), :]
    h = ref.shape[0] // 2
    return ref.at[pl.ds(pc * h, h), :]


def _half_shape(shape, kind):
    return (shape[0], shape[1] // 2, shape[2]) if kind == "blk" else (shape[0] // 2, shape[1])


class _HalfSwap:
    def __init__(self, ins, outs, kinds, send_sems, recv_sems):
        self.ins, self.outs, self.kinds = ins, outs, kinds
        self.send_sems, self.recv_sems = send_sems, recv_sems
        self.x, self.y, self.c = _my_pos()

    def _copies(self):
        for a in range(len(self.ins)):
            yield pltpu.make_async_remote_copy(
                src_ref=_half_of_full(self.ins[a], self.kinds[a], 1 - self.c), dst_ref=self.outs[a],
                send_sem=self.send_sems.at[a], recv_sem=self.recv_sems.at[a],
                device_id=(self.x, self.y, 1 - self.c), device_id_type=MESH)

    def start(self):
        for cp in self._copies():
            cp.start()

    def wait(self):
        for cp in self._copies():
            cp.wait()

    @staticmethod
    def sems(n_arr):
        return [pltpu.SemaphoreType.DMA((n_arr,)), pltpu.SemaphoreType.DMA((n_arr,))]

    @staticmethod
    def out_shapes(fulls, kinds):
        return [jax.ShapeDtypeStruct(_half_shape(a.shape, k), a.dtype) for a, k in zip(fulls, kinds)]


def _swap_halves(fulls_bf16, kinds, name):
    n_arr = len(fulls_bf16)

    def body(*refs):
        swap = _HalfSwap(refs[:n_arr], refs[n_arr:2 * n_arr], kinds, *refs[2 * n_arr:])
        swap.start()
        swap.wait()

    return pl.pallas_call(
        body, name=name, out_shape=_HalfSwap.out_shapes(fulls_bf16, kinds),
        in_specs=[ANY] * n_arr, out_specs=[ANY] * n_arr, scratch_shapes=_HalfSwap.sems(n_arr),
    )(*fulls_bf16)


def _add_halves(full, got, kind, name):
    hs = _half_shape(full.shape, kind)

    def body(pos_ref, a_ref, b_ref, o_ref, ob_ref):
        p = a_ref[...] + b_ref[...].astype(F32)
        ob_ref[...] = p.astype(BF16)

        @pl.when(pl.program_id(0) == pos_ref[1])
        def _():
            o_ref[...] = p.reshape(o_ref.shape)

    if kind == "blk":
        nb, h, cc = hs
        own = pl.BlockSpec((1, h, cc), lambda b, pos_ref: (b, pos_ref[0], 0))
        other = pl.BlockSpec((1, h, cc), lambda b, pos_ref: (b, 0, 0))
    else:
        h, cc = hs[0], hs[1] // N_CHIPS
        own = pl.BlockSpec((h, cc), lambda b, pos_ref: (pos_ref[0], b))
        other = pl.BlockSpec((h, cc), lambda b, pos_ref: (0, b))
    pos = jnp.concatenate([_core_index_scalar(), _chip_index_scalar()])
    return pl.pallas_call(
        body, name=name, out_shape=(jax.ShapeDtypeStruct((h, cc), F32), jax.ShapeDtypeStruct(hs, BF16)),
        grid_spec=pltpu.PrefetchScalarGridSpec(
            num_scalar_prefetch=1, grid=(N_CHIPS,), in_specs=[own, other],
            out_specs=(pl.BlockSpec((h, cc), lambda b, pos_ref: (0, 0)), other)),
        compiler_params=_params(("arbitrary",)),
    )(pos, full, got)


def _rx_shape(part_shape, kind):
    if kind == "blk":
        return (3, part_shape[1], part_shape[2])
    return (3, part_shape[0], part_shape[1] // N_CHIPS)


class _ChipExchange:
    def __init__(self, parts, rxs, kinds, send_sems, recv_sems):
        self.parts, self.rxs, self.kinds = parts, rxs, kinds
        self.send_sems, self.recv_sems = send_sems, recv_sems
        self.x, self.y, self.c = _my_pos()
        self.chips = _other_chips(self.x, self.y)

    def _copies(self):
        for a in range(len(self.parts)):
            for j, chip in enumerate(self.chips):
                cj = 2 * chip[0] + chip[1]
                if self.kinds[a] == "blk":
                    src = self.parts[a].at[cj]
                else:
                    cc = self.parts[a].shape[1] // N_CHIPS
                    src = self.parts[a].at[:, pl.ds(cj * cc, cc)]
                yield pltpu.make_async_remote_copy(
                    src_ref=src, dst_ref=self.rxs[a].at[j], send_sem=self.send_sems.at[a * 3 + j],
                    recv_sem=self.recv_sems.at[a * 3 + j], device_id=(*chip, self.c), device_id_type=MESH)

    def start(self):
        for cp in self._copies():
            cp.start()

    def wait(self):
        for cp in self._copies():
            cp.wait_recv()
        for cp in self._copies():
            cp.wait_send()

    @staticmethod
    def sems(n_arr):
        return [pltpu.SemaphoreType.DMA((n_arr * 3,)), pltpu.SemaphoreType.DMA((n_arr * 3,))]


HBM_SPEC = pl.BlockSpec(memory_space=pltpu.HBM)
SEM_SPEC = pl.BlockSpec(memory_space=pltpu.SEMAPHORE)
DATAFLOW = pltpu.SideEffectType.DATAFLOW_SIDE_EFFECTING


def _exchange_start(part, name):
    rx_shape = _rx_shape(part.shape, "blk")

    def body(part_ref, rx_ref, send_sems, recv_sems, part_thru, rx_thru, token):
        _ChipExchange([part_ref], [rx_ref], ["blk"], send_sems, recv_sems).start()
        token[...] = jnp.zeros_like(token)

    return pl.pallas_call(
        body, name=name,
        out_shape=(pltpu.SemaphoreType.DMA((3,)), pltpu.SemaphoreType.DMA((3,)), pltpu.HBM(part.shape, part.dtype),
                   pltpu.HBM(rx_shape, BF16), jax.ShapeDtypeStruct((8, LANES), F32)),
        in_specs=(HBM_SPEC, HBM_SPEC), out_specs=(SEM_SPEC, SEM_SPEC, HBM_SPEC, HBM_SPEC, VMEM_WHOLE),
        input_output_aliases={0: 2, 1: 3}, compiler_params=pltpu.CompilerParams(has_side_effects=DATAFLOW),
    )(pltpu.with_memory_space_constraint(part, pltpu.HBM),
      pltpu.with_memory_space_constraint(lax.empty(rx_shape, BF16), pltpu.HBM))


def _exchange_wait(send_sems, recv_sems, part_thru, rx_thru, after, name):
    def body(part_ref, rx_ref, send_sems, recv_sems, after_ref, part_dead, rx_out):
        _ChipExchange([part_ref], [rx_ref], ["blk"], send_sems, recv_sems).wait()

    return pl.pallas_call(
        body, name=name,
        out_shape=(pltpu.HBM(part_thru.shape, part_thru.dtype), pltpu.HBM(rx_thru.shape, rx_thru.dtype)),
        in_specs=(HBM_SPEC, HBM_SPEC, SEM_SPEC, SEM_SPEC, ANY), out_specs=(HBM_SPEC, HBM_SPEC),
        input_output_aliases={0: 0, 1: 1}, compiler_params=pltpu.CompilerParams(has_side_effects=DATAFLOW),
    )(part_thru, rx_thru, send_sems, recv_sems, after)[1]


def _sum_chips(part, rx, tr, name, after):
    _, h, cc = rx.shape
    flips = (2, 1, 3)

    def body(chip_ref, p_ref, rx_ref, after_ref, o_ref):
        own = p_ref[...]
        for mc in range(N_CHIPS):
            @pl.when(chip_ref[0] == mc)
            def _():
                terms = sorted([(mc, None)] + [(mc ^ f, j) for j, f in enumerate(flips)])
                acc = None
                for _, j in terms:
                    t = own if j is None else rx_ref[j].astype(F32)
                    acc = t if acc is None else acc + t
                o_ref[...] = acc

    return pl.pallas_call(
        body, name=name, out_shape=jax.ShapeDtypeStruct((h, cc), F32),
        grid_spec=pltpu.PrefetchScalarGridSpec(
            num_scalar_prefetch=1, grid=(h // tr,),
            in_specs=[pl.BlockSpec((tr, cc), lambda i, chip_ref: (i, 0)),
                      pl.BlockSpec((3, tr, cc), lambda i, chip_ref: (0, i, 0)), ANY],
            out_specs=pl.BlockSpec((tr, cc), lambda i, chip_ref: (i, 0))),
        compiler_params=_params(("arbitrary",)),
    )(_chip_index_scalar(), part, rx, after)


def _share_halves(halves, name):
    n_arr = len(halves)

    def body(*refs):
        ins, outs = refs[:n_arr], refs[n_arr:2 * n_arr]
        send_sems, recv_sems = refs[2 * n_arr:]
        x, y, c = _my_pos()
        cps = []
        for a in range(n_arr):
            cp = pltpu.make_async_remote_copy(
                src_ref=ins[a], dst_ref=outs[a], send_sem=send_sems.at[a], recv_sem=recv_sems.at[a],
                device_id=(x, y, 1 - c), device_id_type=MESH)
            cp.start()
            cps.append(cp)
        for cp in cps:
            cp.wait()

    return pl.pallas_call(
        body, name=name, out_shape=[jax.ShapeDtypeStruct(h.shape, h.dtype) for h in halves],
        in_specs=[ANY] * n_arr, out_specs=[ANY] * n_arr,
        scratch_shapes=[pltpu.SemaphoreType.DMA((n_arr,)), pltpu.SemaphoreType.DMA((n_arr,))],
    )(*halves)


def _bucket_table():
    qi = jnp.arange(BLOCK)[:, None]
    si = jnp.arange(2 * BLOCK)[None, :]
    dist = qi + BLOCK - si
    max_exact = N_BUCKETS // 2
    n = jnp.maximum(dist, 0)
    nf = jnp.maximum(n, max_exact).astype(F32)
    large = max_exact + (jnp.log(nf / max_exact) / math.log(MAX_DISTANCE / max_exact)
                         * (N_BUCKETS - max_exact)).astype(jnp.int32)
    large = jnp.minimum(large, N_BUCKETS - 1)
    return jnp.where(n < max_exact, n, large).astype(F32)


def _prep_tables(bucket, rel_bias_t, w_s):
    def body(bucket_ref, rb_ref, ws_ref, bias_ref, wsm_ref):
        qi = lax.broadcasted_iota(jnp.int32, (BLOCK, 2 * BLOCK), 0)
        si = lax.broadcasted_iota(jnp.int32, (BLOCK, 2 * BLOCK), 1)
        dist = qi + BLOCK - si
        in_window = (dist >= 0) & (dist < BLOCK)
        bk = bucket_ref[...]
        for h in range(N_HEADS):
            acc = jnp.zeros((BLOCK, 2 * BLOCK), F32)
            for b in range(N_BUCKETS):
                acc = jnp.where(bk == float(b), rb_ref[h, b], acc)
            bias_ref[h] = jnp.where(in_window, acc, NEG_INF)
        ti = lax.broadcasted_iota(jnp.int32, (BLOCK, BLOCK), 0)
        ui = lax.broadcasted_iota(jnp.int32, (BLOCK, BLOCK), 1)
        for g in range(N_GROUPS):
            wsm_ref[g] = jnp.where(ti >= ui, ws_ref[g], 0.0).astype(BF16)

    return pl.pallas_call(
        body, name="prep_tables",
        out_shape=(jax.ShapeDtypeStruct((N_HEADS, BLOCK, 2 * BLOCK), F32),
                   jax.ShapeDtypeStruct((N_GROUPS, BLOCK, BLOCK), BF16)),
        grid=(1,),
        in_specs=[_const_spec((BLOCK, 2 * BLOCK)), pl.BlockSpec(memory_space=pltpu.SMEM),
                  _const_spec((N_GROUPS, BLOCK, BLOCK))],
        out_specs=(_const_spec((N_HEADS, BLOCK, 2 * BLOCK)), _const_spec((N_GROUPS, BLOCK, BLOCK))),
        compiler_params=_params(("arbitrary",)),
    )(bucket, rel_bias_t, w_s)


def _fwd_in(x, modr, w_in, b_in, tm, shards, kinds):
    s = x.shape[0]
    n_steps = s // tm
    fwd_step, diag_step = (8 * n_steps) // 16, (13 * n_steps) // 16
    n_w = len(shards)

    def body(x_ref, mod_ref, w_ref, b_ref, *rest):
        shard_refs = rest[:n_w]
        h1_ref, q_ref, kv_ref, gu_ref, gv_ref, xb_ref = rest[n_w:n_w + 6]
        gathered_refs = rest[n_w + 6:2 * n_w + 6]
        send_sems, recv_sems = rest[2 * n_w + 6:]
        i = pl.program_id(0)
        gather = _WeightGather(shard_refs, gathered_refs, kinds, send_sems, recv_sems)

        @pl.when(i == 0)
        def _():
            gather.start()

        xv = x_ref[...]
        xb_ref[...] = xv.astype(BF16)
        h1 = (xv * (1.0 + mod_ref[1:2, :]) + mod_ref[0:1, :]).astype(BF16)
        h1_ref[...] = h1
        proj = _dot_nt(h1, w_ref[...]) + b_ref[...]
        q_ref[...] = (proj[:, :ATTN_W] * Q_SCALE).astype(BF16)
        kv_ref[...] = proj[:, ATTN_W:ATTN_W + 2 * KV_W].astype(BF16)
        gu_ref[...] = proj[:, ATTN_W + 2 * KV_W:ATTN_W + 2 * KV_W + GMLP_W]
        gv_ref[...] = proj[:, ATTN_W + 2 * KV_W + GMLP_W:]

        @pl.when(i == fwd_step)
        def _():
            gather.forward()

        @pl.when(i == diag_step)
        def _():
            gather.forward_diagonal()

        @pl.when(i == n_steps - 1)
        def _():
            gather.finish()

    row = lambda w: pl.BlockSpec((tm, w), lambda i: (i, 0))
    outs = pl.pallas_call(
        body, name="fwd_in",
        out_shape=[jax.ShapeDtypeStruct((s, D_MODEL), BF16), jax.ShapeDtypeStruct((s, ATTN_W), BF16),
                   jax.ShapeDtypeStruct((s, 2 * KV_W), BF16), jax.ShapeDtypeStruct((s, GMLP_W), F32),
                   jax.ShapeDtypeStruct((s, GMLP_W), F32), jax.ShapeDtypeStruct((s, D_MODEL), BF16)]
        + [jax.ShapeDtypeStruct(_gathered_shape(sh, k), BF16) for sh, k in zip(shards, kinds)],
        grid=(n_steps,),
        in_specs=[row(D_MODEL), _const_spec((8, D_MODEL)), _const_spec(w_in.shape), _const_spec((1, IN_W))]
        + [ANY] * n_w,
        out_specs=[row(D_MODEL), row(ATTN_W), row(2 * KV_W), row(GMLP_W), row(GMLP_W), row(D_MODEL)] + [ANY] * n_w,
        scratch_shapes=_WeightGather.sems(n_w),
        compiler_params=_params(("arbitrary",)),
    )(x, modr, w_in, b_in, *shards)
    return outs[:6], outs[6:]


def _kv_variants(kk):
    kf = kk.astype(F32)
    lane = lax.broadcasted_iota(jnp.int32, kf.shape, 1)
    low = lane < HEAD_DIM
    k0_lo = jnp.where(low, kf, 0.0)
    k1_hi = jnp.where(low, 0.0, kf)
    k0_hi = pltpu.roll(k0_lo, HEAD_DIM, 1)
    k1_lo = pltpu.roll(k1_hi, HEAD_DIM, 1)
    return ((k0_lo.astype(BF16), k0_hi.astype(BF16)), (k1_lo.astype(BF16), k1_hi.astype(BF16)))


def _head_kv(h):
    return h // (N_HEADS // N_KV), h % 2


MIX_GROUP = 2


def _interleave(*gens):
    results = [None] * len(gens)
    active = list(enumerate(gens))
    while active:
        still = []
        for i, g in active:
            try:
                next(g)
                still.append((i, g))
            except StopIteration as done:
                results[i] = done.value
        active = still
    return results


def _attn_block_fwd(q_blk, kk, vv, bias_ref, sinks_ref, first_mask):
    kvar = _kv_variants(kk)
    vvar = _kv_variants(vv)
    heads = range(N_HEADS)
    q_pairs = [q_blk[:, (h // 2) * LANES:(h // 2 + 1) * LANES] for h in heads]
    logits = [_dot_nt(q_pairs[h], kvar[_head_kv(h)[0]][_head_kv(h)[1]]) + bias_ref[h] for h in heads]
    if first_mask is not None:
        logits = [jnp.where(first_mask, NEG_INF, lg) for lg in logits]
    yield
    ms = [jnp.maximum(jnp.max(logits[h], axis=-1, keepdims=True), sinks_ref[h]) for h in heads]
    yield
    es = [jnp.exp(logits[h] - ms[h]) for h in heads]
    ess = [jnp.exp(sinks_ref[h] - ms[h]) for h in heads]
    yield
    invs = [1.0 / (jnp.sum(es[h], axis=-1, keepdims=True) + ess[h]) for h in heads]
    probs = [(es[h] * invs[h], ess[h] * invs[h]) for h in heads]
    yield
    outs = [_dot(probs[h][0].astype(BF16), vvar[_head_kv(h)[0]][_head_kv(h)[1]]) for h in heads]
    pairs = [outs[2 * i] + outs[2 * i + 1] for i in range(N_HEADS // 2)]
    return jnp.concatenate(pairs, axis=1), probs, kvar, vvar


def _gmlp_chunk_fwd(gu, gv, ln_g, ln_b, wsm_ref, bsx, amat):
    u, tu = _gelu(gu)
    a, ta = _gelu(gv)
    yield
    mean = _split_dot(a, amat)
    d = a - mean
    yield
    var = _split_dot(d * d, amat)
    yield
    rstd = lax.rsqrt(var + LN_EPS)
    xhat = d * rstd
    vb = (xhat * ln_g + ln_b).astype(BF16)
    yield
    lane = lax.broadcasted_iota(jnp.int32, (BLOCK, LANES), 1)
    low = lane < GROUP_DIM
    cols = []
    for pair in range(N_GROUPS // 2):
        vp = vb[:, pair * LANES:(pair + 1) * LANES]
        cols.append(jnp.where(low, _dot(wsm_ref[2 * pair], vp), _dot(wsm_ref[2 * pair + 1], vp)))
    mixedv = jnp.concatenate(cols, axis=1) + bsx
    return u * mixedv, (u, tu, ta, xhat, rstd, vb, mixedv)


def _rms(a, g):
    r = lax.rsqrt(jnp.mean(a * a, axis=-1, keepdims=True) + LN_EPS)
    return a * r * g, r


def _fwd_mix(q, kv, gu, gv, x, modr, bias, sinks, gln_g, gln_b, wsm, bsx, amat, aog, gog, w_out, ln1_g, ln1_b, tm,
             ffn_shards, ffn_kinds):
    s = x.shape[0]
    nb = tm // BLOCK
    n_steps = s // tm
    fwd_step, diag_step = (7 * n_steps) // 16, (12 * n_steps) // 16
    n_w = len(ffn_shards)

    def body(q_ref, kv_ref, kvp_ref, gu_ref, gv_ref, x_ref, mod_ref, bias_ref, sinks_ref, glng_ref, glnb_ref, wsm_ref,
             bsx_ref, amat_ref, aog_ref, gog_ref, wout_ref, ln1g_ref, ln1b_ref, *rest):
        shard_refs = rest[:n_w]
        x1_ref, x1b_ref, y_ref, mixed_ref = rest[n_w:n_w + 4]
        gathered_refs = rest[n_w + 4:2 * n_w + 4]
        mix_scr, send_sems, recv_sems = rest[2 * n_w + 4:]
        i = pl.program_id(0)
        gather = _WeightGather(shard_refs, gathered_refs, ffn_kinds, send_sems, recv_sems)

        @pl.when(i == 0)
        def _():
            gather.start()

        col = lax.broadcasted_iota(jnp.int32, (BLOCK, 2 * BLOCK), 1)
        for b0 in range(0, nb, MIX_GROUP):
            gens = []
            for b in range(b0, min(b0 + MIX_GROUP, nb)):
                r0 = b * BLOCK
                if b == 0:
                    kvprev = kvp_ref[...]
                    first_mask = (col < BLOCK) & (i == 0)
                else:
                    kvprev = kv_ref[r0 - BLOCK:r0, :]
                    first_mask = None
                kvcur = kv_ref[r0:r0 + BLOCK, :]
                kk = jnp.concatenate([kvprev[:, :KV_W], kvcur[:, :KV_W]], axis=0)
                vv = jnp.concatenate([kvprev[:, KV_W:], kvcur[:, KV_W:]], axis=0)
                gens.append(_attn_block_fwd(q_ref[r0:r0 + BLOCK, :], kk, vv, bias_ref, sinks_ref, first_mask))
                gens.append(_gmlp_chunk_fwd(gu_ref[r0:r0 + BLOCK, :], gv_ref[r0:r0 + BLOCK, :], glng_ref[...],
                                            glnb_ref[...], wsm_ref, bsx_ref[...], amat_ref[...]))
            res = _interleave(*gens)
            for k, b in enumerate(range(b0, min(b0 + MIX_GROUP, nb))):
                r0 = b * BLOCK
                na, _ = _rms(res[2 * k][0], aog_ref[...])
                ng, _ = _rms(res[2 * k + 1][0], gog_ref[...])
                mix_scr[r0:r0 + BLOCK, :ATTN_W] = na.astype(BF16)
                mix_scr[r0:r0 + BLOCK, ATTN_W:] = ng.astype(BF16)
        mixed = mix_scr[...]
        mixed_ref[...] = mixed
        y = _dot(mixed, wout_ref[...])
        y_ref[...] = y.astype(BF16)
        z1 = ALPHA * x_ref[...] + mod_ref[2:3, :] * y
        xhat, _ = _ln_stats(z1)
        x1 = xhat * ln1g_ref[...] + ln1b_ref[...]
        x1_ref[...] = x1
        x1b_ref[...] = x1.astype(BF16)

        @pl.when(i == fwd_step)
        def _():
            gather.forward()

        @pl.when(i == diag_step)
        def _():
            gather.forward_diagonal()

        @pl.when(i == n_steps - 1)
        def _():
            gather.finish()

    row = lambda w: pl.BlockSpec((tm, w), lambda i: (i, 0))
    prev = pl.BlockSpec((BLOCK, 2 * KV_W), lambda i: (jnp.maximum(i * nb - 1, 0), 0))
    outs = pl.pallas_call(
        body, name="fwd_mix",
        out_shape=[jax.ShapeDtypeStruct((s, D_MODEL), F32)] + [jax.ShapeDtypeStruct((s, D_MODEL), BF16)] * 3
        + [jax.ShapeDtypeStruct(_gathered_shape(sh, k), BF16) for sh, k in zip(ffn_shards, ffn_kinds)],
        grid=(n_steps,),
        in_specs=[row(ATTN_W), row(2 * KV_W), prev, row(GMLP_W), row(GMLP_W), row(D_MODEL), _const_spec((8, D_MODEL)),
                  _const_spec((N_HEADS, BLOCK, 2 * BLOCK)), pl.BlockSpec(memory_space=pltpu.SMEM),
                  _const_spec((1, GMLP_W)), _const_spec((1, GMLP_W)), _const_spec((N_GROUPS, BLOCK, BLOCK)),
                  _const_spec((BLOCK, GMLP_W)), _const_spec((GMLP_W, GMLP_W)), _const_spec((1, ATTN_W)),
                  _const_spec((1, GMLP_W)), _const_spec((D_MODEL, D_MODEL)), _const_spec((1, D_MODEL)),
                  _const_spec((1, D_MODEL))] + [ANY] * n_w,
        out_specs=[row(D_MODEL)] * 4 + [ANY] * n_w,
        scratch_shapes=[pltpu.VMEM((tm, D_MODEL), BF16)] + _WeightGather.sems(n_w),
        compiler_params=_params(("arbitrary",)),
    )(q, kv, kv, gu, gv, x, modr, bias, sinks, gln_g, gln_b, wsm, bsx, amat, aog, gog, w_out, ln1_g, ln1_b, *ffn_shards)
    return outs[:4], outs[4:]


FF_BLOCKS = N_CHIPS // 2
FF_CHUNK = D_FF // FF_BLOCKS
FFN_SUB = 256


def _sigmoid(x):
    return 1.0 / (1.0 + jnp.exp(-x))


def _fwd_ffn(x1, target, modr, ln2_g, ln2_b, w_gu, w_dn, tm):
    s = x1.shape[0]

    def body(x1_ref, t_ref, mod_ref, g_ref, b_ref, wgu_ref, wdn_ref, h2_ref, act_ref, dy2_ref, dx1a_ref, acc_ref):
        @pl.when(pl.program_id(0) == 0)
        def _():
            acc_ref[...] = jnp.zeros_like(acc_ref)

        x1v = x1_ref[...]
        h2 = (x1v * (1.0 + mod_ref[4:5, :]) + mod_ref[3:4, :]).astype(BF16)
        h2_ref[...] = h2
        y2 = None
        for cc in range(FF_BLOCKS):
            c0 = cc * FF_CHUNK
            gate = _dot(h2, wgu_ref[cc])
            up = _dot(h2, wgu_ref[FF_BLOCKS + cc])
            act_ref[:, c0:c0 + FF_CHUNK] = gate.astype(BF16)
            act_ref[:, D_FF + c0:D_FF + c0 + FF_CHUNK] = up.astype(BF16)
            a = (gate * _sigmoid(gate) * up).astype(BF16)
            part = _dot(a, wdn_ref[c0:c0 + FF_CHUNK, :])
            y2 = part if y2 is None else y2 + part
        g2 = mod_ref[5:6, :]
        z2 = ALPHA * x1v + g2 * y2
        xhat, rstd = _ln_stats(z2)
        gain = g_ref[...]
        diff = xhat * gain + b_ref[...] - t_ref[...]
        dx2 = diff * (1.0 / D_MODEL)
        dz2 = _ln_bwd(dx2 * gain, xhat, rstd)
        dx1a_ref[...] = ALPHA * dz2
        dy2_ref[...] = (g2 * dz2).astype(BF16)
        acc_ref[0:1, :] += _colsum(diff * diff)
        acc_ref[1:2, :] += _colsum(dx2 * xhat)
        acc_ref[2:3, :] += _colsum(dx2)
        acc_ref[3:4, :] += _colsum(dz2 * y2)

    row = lambda w: pl.BlockSpec((tm, w), lambda i: (i, 0))
    return pl.pallas_call(
        body, name="fwd_ffn",
        out_shape=(jax.ShapeDtypeStruct((s, D_MODEL), BF16), jax.ShapeDtypeStruct((s, 2 * D_FF), BF16),
                   jax.ShapeDtypeStruct((s, D_MODEL), BF16), jax.ShapeDtypeStruct((s, D_MODEL), F32),
                   jax.ShapeDtypeStruct((8, D_MODEL), F32)),
        grid=(s // tm,),
        in_specs=[row(D_MODEL), row(D_MODEL), _const_spec((8, D_MODEL)), _const_spec((1, D_MODEL)),
                  _const_spec((1, D_MODEL)), _const_spec((N_CHIPS, D_MODEL, FF_CHUNK), single=True),
                  _const_spec((D_FF, D_MODEL), single=True)],
        out_specs=(row(D_MODEL), row(2 * D_FF), row(D_MODEL), row(D_MODEL), _const_spec((8, D_MODEL))),
        compiler_params=_params(("arbitrary",)),
    )(x1, target, modr, ln2_g, ln2_b, w_gu, w_dn)


def _bwd_ffn(dy2, act, w_gu, w_dn, tm):
    s = dy2.shape[0]

    def body(dy2_ref, act_ref, wgu_ref, wdn_ref, a_ref, dgu_ref, dh2_ref):
        dy2v = dy2_ref[...]
        dh2 = None
        for cc in range(FF_BLOCKS):
            c0 = cc * FF_CHUNK
            da = _dot_nt(dy2v, wdn_ref[c0:c0 + FF_CHUNK, :])
            gate = act_ref[:, c0:c0 + FF_CHUNK].astype(F32)
            up = act_ref[:, D_FF + c0:D_FF + c0 + FF_CHUNK].astype(F32)
            sg = _sigmoid(gate)
            sl = gate * sg
            a_ref[:, c0:c0 + FF_CHUNK] = (sl * up).astype(BF16)
            dgate = (da * up * (sg * (1.0 + gate * (1.0 - sg)))).astype(BF16)
            dup = (da * sl).astype(BF16)
            dgu_ref[:, c0:c0 + FF_CHUNK] = dgate
            dgu_ref[:, D_FF + c0:D_FF + c0 + FF_CHUNK] = dup
            part = _dot_nt(dgate, wgu_ref[cc]) + _dot_nt(dup, wgu_ref[FF_BLOCKS + cc])
            dh2 = part if dh2 is None else dh2 + part
        dh2_ref[...] = dh2.astype(BF16)

    row = lambda w: pl.BlockSpec((tm, w), lambda i: (i, 0))
    return pl.pallas_call(
        body, name="bwd_ffn",
        out_shape=(jax.ShapeDtypeStruct((s, D_FF), BF16), jax.ShapeDtypeStruct((s, 2 * D_FF), BF16),
                   jax.ShapeDtypeStruct((s, D_MODEL), BF16)),
        grid=(s // tm,),
        in_specs=[row(D_MODEL), row(2 * D_FF), _const_spec((N_CHIPS, D_MODEL, FF_CHUNK), single=True),
                  _const_spec((D_FF, D_MODEL), single=True)],
        out_specs=(row(D_FF), row(2 * D_FF), row(D_MODEL)),
        compiler_params=_params(("parallel",)),
    )(dy2, act, w_gu, w_dn)


def _bwd_mid(dh2, dx1a, x1, x, y, modr, ln1_g, w_out, tm, swap_fulls, swap_kinds):
    s = x.shape[0]
    n_steps = s // tm
    n_g = len(swap_fulls)

    def body(dh2_ref, dx1a_ref, x1_ref, x_ref, y_ref, mod_ref, g_ref, wout_ref, *rest):
        full_refs = rest[:n_g]
        dxa_ref, dy_ref, dmix_ref, acc_ref = rest[n_g:n_g + 4]
        got_refs = rest[n_g + 4:2 * n_g + 4]
        swap = _HalfSwap(full_refs, got_refs, swap_kinds, *rest[2 * n_g + 4:])
        i = pl.program_id(0)

        @pl.when(i == 0)
        def _():
            swap.start()
            acc_ref[...] = jnp.zeros_like(acc_ref)

        dh2 = dh2_ref[...].astype(F32)
        x1v = x1_ref[...].astype(F32)
        yv = y_ref[...].astype(F32)
        g1 = mod_ref[2:3, :]
        dx1 = dx1a_ref[...] + dh2 * (1.0 + mod_ref[4:5, :])
        z1 = ALPHA * x_ref[...] + g1 * yv
        xhat, rstd = _ln_stats(z1)
        dz1 = _ln_bwd(dx1 * g_ref[...], xhat, rstd)
        dxa_ref[...] = (ALPHA * dz1).astype(BF16)
        dy = (g1 * dz1).astype(BF16)
        dy_ref[...] = dy
        dmix_ref[...] = _dot_nt(dy, wout_ref[...]).astype(BF16)
        acc_ref[0:1, :] += _colsum(dh2 * x1v)
        acc_ref[1:2, :] += _colsum(dh2)
        acc_ref[2:3, :] += _colsum(dx1 * xhat)
        acc_ref[3:4, :] += _colsum(dx1)
        acc_ref[4:5, :] += _colsum(dz1 * yv)

        @pl.when(i == n_steps - 1)
        def _():
            swap.wait()

    row = lambda w: pl.BlockSpec((tm, w), lambda i: (i, 0))
    outs = pl.pallas_call(
        body, name="bwd_mid",
        out_shape=[jax.ShapeDtypeStruct((s, D_MODEL), BF16), jax.ShapeDtypeStruct((s, D_MODEL), BF16),
                   jax.ShapeDtypeStruct((s, D_MODEL), BF16), jax.ShapeDtypeStruct((8, D_MODEL), F32)]
        + _HalfSwap.out_shapes(swap_fulls, swap_kinds),
        grid=(n_steps,),
        in_specs=[row(D_MODEL)] * 5 + [_const_spec((8, D_MODEL)), _const_spec((1, D_MODEL)),
                                       _const_spec((D_MODEL, D_MODEL))] + [ANY] * n_g,
        out_specs=[row(D_MODEL), row(D_MODEL), row(D_MODEL), _const_spec((8, D_MODEL))] + [ANY] * n_g,
        scratch_shapes=_HalfSwap.sems(n_g),
        compiler_params=_params(("arbitrary",)),
    )(dh2, dx1a, x1, x, y, modr, ln1_g, w_out, *swap_fulls)
    return outs[:4], outs[4:]


def _fold_kv(t0, t1):
    lane = lax.broadcasted_iota(jnp.int32, t0.shape, 1)
    f0 = t0 + pltpu.roll(t0, HEAD_DIM, 1)
    f1 = t1 + pltpu.roll(t1, HEAD_DIM, 1)
    return jnp.where(lane < HEAD_DIM, f0, f1)


def _bwd_mix(q, kv, gu, gv, dmix, bias, sinks, gln_g, gln_b, wsm, bsx, amat, aog, gog, grad_parts, grad_kinds):
    s = q.shape[0]
    tile = 2 * BLOCK
    n_steps = s // tile
    n_g = len(grad_parts)

    def body(q_ref, kv_ref, kvp_ref, gu_ref, gv_ref, dmix_ref, bias_ref, sinks_ref, glng_ref, glnb_ref, wsm_ref,
             bsx_ref, amat_ref, aog_ref, gog_ref, *rest):
        part_refs = rest[:n_g]
        dq_ref, dkv_ref, dgu_ref, dgv_ref, gbias_ref, dws_ref, dbs_ref, vec_ref, dsink_ref = rest[n_g:n_g + 9]
        rx_refs = rest[n_g + 9:2 * n_g + 9]
        carry, done, send_sems, recv_sems = rest[2 * n_g + 9:]
        n = pl.program_id(0)
        exchange = _ChipExchange(part_refs, rx_refs, grad_kinds, send_sems, recv_sems)

        @pl.when(n == 0)
        def _():
            exchange.start()
            carry[...] = jnp.zeros_like(carry)
            done[...] = jnp.zeros_like(done)
            gbias_ref[...] = jnp.zeros_like(gbias_ref)
            dws_ref[...] = jnp.zeros_like(dws_ref)
            dbs_ref[...] = jnp.zeros_like(dbs_ref)
            vec_ref[...] = jnp.zeros_like(vec_ref)
            dsink_ref[...] = jnp.zeros_like(dsink_ref)

        @pl.when(n == n_steps)
        def _():
            dkv_ref[:BLOCK, :] = done[...].astype(BF16)
            dkv_ref[BLOCK:, :] = carry[...].astype(BF16)
            exchange.wait()

        @pl.when(n < n_steps)
        def _():
            col = lax.broadcasted_iota(jnp.int32, (BLOCK, 2 * BLOCK), 1)
            lane = lax.broadcasted_iota(jnp.int32, (BLOCK, LANES), 1)
            low = lane < HEAD_DIM
            rows = [slice(0, BLOCK), slice(BLOCK, tile)]
            kv_blocks = [kvp_ref[...], kv_ref[rows[0], :], kv_ref[rows[1], :]]
            masks = [(col < BLOCK) & (n == 0), None]
            q_blks = [q_ref[r, :] for r in rows]
            fwd = []
            for b in range(2):
                kk = jnp.concatenate([kv_blocks[b][:, :KV_W], kv_blocks[b + 1][:, :KV_W]], axis=0)
                vv = jnp.concatenate([kv_blocks[b][:, KV_W:], kv_blocks[b + 1][:, KV_W:]], axis=0)
                fwd.append(_attn_block_fwd(q_blks[b], kk, vv, bias_ref, sinks_ref, masks[b]))
                fwd.append(_gmlp_chunk_fwd(gu_ref[rows[b], :], gv_ref[rows[b], :], glng_ref[...], glnb_ref[...],
                                           wsm_ref, bsx_ref[...], amat_ref[...]))
            res = _interleave(*fwd[:2]) + _interleave(*fwd[2:])

            def gating_bwd(b, d_gm, saved):
                u, tu, ta, xhat, rstd, vb, mixedv = saved
                dgu_ref[rows[b], :] = (d_gm * mixedv * _gelu_grad(gu_ref[rows[b], :], tu)).astype(BF16)
                dmx = d_gm * u
                dmxb = dmx.astype(BF16)
                yield
                dvn_cols, dws = [], []
                for pair in range(N_GROUPS // 2):
                    dp_ = dmxb[:, pair * LANES:(pair + 1) * LANES]
                    vp = vb[:, pair * LANES:(pair + 1) * LANES]
                    dvn_cols.append(
                        jnp.where(low, _dot_tn(wsm_ref[2 * pair], dp_), _dot_tn(wsm_ref[2 * pair + 1], dp_)))
                    zero = jnp.zeros_like(dp_)
                    dws.append(_dot_nt(jnp.where(low, dp_, zero), vp))
                    dws.append(_dot_nt(jnp.where(low, zero, dp_), vp))
                dvn = jnp.concatenate(dvn_cols, axis=1)
                yield
                dxh = dvn * glng_ref[...]
                am = amat_ref[...]
                m1 = _split_dot(dxh, am)
                m2 = _split_dot(dxh * xhat, am)
                yield
                da = rstd * (dxh - m1 - xhat * m2)
                dgv_ref[rows[b], :] = (da * _gelu_grad(gv_ref[rows[b], :], ta)).astype(BF16)
                return dmx, dws, _colsum(dvn * xhat), _colsum(dvn)

            def attention_bwd(b, d_attn, probs, kvar, vvar):
                heads = range(N_HEADS)
                sels = [low if h % 2 == 0 else jnp.logical_not(low) for h in heads]
                pair_of = lambda a, h: a[:, (h // 2) * LANES:(h // 2 + 1) * LANES]
                do_hs = [jnp.where(sels[h], pair_of(d_attn, h), 0.0).astype(BF16) for h in heads]
                q_hs = [jnp.where(sels[h], pair_of(q_blks[b], h), jnp.zeros((BLOCK, LANES), BF16)) for h in heads]
                dps = [_dot_nt(do_hs[h], vvar[_head_kv(h)[0]][_head_kv(h)[1]]) for h in heads]
                yield
                deltas = [jnp.sum(probs[h][0] * dps[h], axis=-1, keepdims=True) for h in heads]
                yield
                dss = [probs[h][0] * (dps[h] - deltas[h]) for h in heads]
                dsinks = [-(probs[h][1] * deltas[h]) for h in heads]
                dsbs = [ds.astype(BF16) for ds in dss]
                pbs = [probs[h][0].astype(BF16) for h in heads]
                yield
                dqs = [_dot(dsbs[h], kvar[_head_kv(h)[0]][_head_kv(h)[1]]) for h in heads]
                tks = [_dot_tn(dsbs[h], q_hs[h]) for h in heads]
                tvs = [_dot_tn(pbs[h], do_hs[h]) for h in heads]
                dq_cols = [dqs[2 * i] + dqs[2 * i + 1] for i in range(N_HEADS // 2)]
                dq_ref[rows[b], :] = (jnp.concatenate(dq_cols, axis=1) * Q_SCALE).astype(BF16)
                per_kv = N_HEADS // N_KV
                kv_sum = lambda ts, kvh: sum(ts[kvh * per_kv + 1:(kvh + 1) * per_kv], ts[kvh * per_kv])
                dkk = _fold_kv(kv_sum(tks, 0), kv_sum(tks, 1))
                dvv = _fold_kv(kv_sum(tvs, 0), kv_sum(tvs, 1))
                return jnp.concatenate([dkk, dvv], axis=1), dss, dsinks

            bwd, rms_g = [], []
            for b in range(2):
                attn, probs, kvar, vvar = res[2 * b]
                gm, saved = res[2 * b + 1]
                na_unit, r_a = _rms(attn, 1.0)
                ng_unit, r_g = _rms(gm, 1.0)
                dmix = dmix_ref[rows[b], :].astype(F32)
                dn_a = dmix[:, :ATTN_W]
                dn_g = dmix[:, ATTN_W:]
                rms_g.append((_colsum(dn_a * na_unit), _colsum(dn_g * ng_unit)))
                t_a = dn_a * aog_ref[...]
                d_attn = r_a * t_a - na_unit * (r_a * jnp.mean(t_a * na_unit, axis=-1, keepdims=True))
                t_g = dn_g * gog_ref[...]
                d_gm = r_g * t_g - ng_unit * (r_g * jnp.mean(t_g * ng_unit, axis=-1, keepdims=True))
                bwd.append(attention_bwd(b, d_attn, probs, kvar, vvar))
                bwd.append(gating_bwd(b, d_gm, saved))
            (dkv_a, dss_a, dsk_a), (dmx_a, dws_a, glg_a, glb_a) = _interleave(*bwd[:2])
            (dkv_b, dss_b, dsk_b), (dmx_b, dws_b, glg_b, glb_b) = _interleave(*bwd[2:])

            vec_ref[0:1, :] += rms_g[0][0] + rms_g[1][0]
            vec_ref[1:2, :] += rms_g[0][1] + rms_g[1][1]
            vec_ref[2:3, :] += glg_a + glg_b
            vec_ref[3:4, :] += glb_a + glb_b
            dbs_ref[...] += dmx_a + dmx_b
            for g in range(N_GROUPS):
                dws_ref[g] += dws_a[g] + dws_b[g]
            for h in range(N_HEADS):
                gbias_ref[h] += dss_a[h] + dss_b[h]
                dsink_ref[h] += dsk_a[h] + dsk_b[h]

            dkv_ref[:BLOCK, :] = done[...].astype(BF16)
            dkv_ref[BLOCK:, :] = (carry[...] + dkv_a[:BLOCK]).astype(BF16)
            done[...] = dkv_a[BLOCK:] + dkv_b[:BLOCK]
            carry[...] = dkv_b[BLOCK:]

    last = n_steps - 1
    cur = lambda w: pl.BlockSpec((tile, w), lambda n: (jnp.minimum(n, last), 0))
    late = lambda w: pl.BlockSpec((tile, w), lambda n: (jnp.clip(n - 1, 0, last), 0))
    before = pl.BlockSpec((BLOCK, 2 * KV_W), lambda n: (jnp.clip(2 * n - 1, 0, 2 * last + 1), 0))
    outs = pl.pallas_call(
        body, name="bwd_mix",
        out_shape=[jax.ShapeDtypeStruct((s, ATTN_W), BF16), jax.ShapeDtypeStruct((s, 2 * KV_W), BF16),
                   jax.ShapeDtypeStruct((s, GMLP_W), BF16), jax.ShapeDtypeStruct((s, GMLP_W), BF16),
                   jax.ShapeDtypeStruct((N_HEADS, BLOCK, 2 * BLOCK), F32),
                   jax.ShapeDtypeStruct((N_GROUPS, BLOCK, BLOCK), F32),
                   jax.ShapeDtypeStruct((BLOCK, GMLP_W), F32), jax.ShapeDtypeStruct((8, GMLP_W), F32),
                   jax.ShapeDtypeStruct((N_HEADS, BLOCK, 1), F32)]
        + [jax.ShapeDtypeStruct(_rx_shape(p.shape, k), BF16) for p, k in zip(grad_parts, grad_kinds)],
        grid=(n_steps + 1,),
        in_specs=[cur(ATTN_W), cur(2 * KV_W), before, cur(GMLP_W), cur(GMLP_W), cur(D_MODEL),
                  _const_spec((N_HEADS, BLOCK, 2 * BLOCK)), pl.BlockSpec(memory_space=pltpu.SMEM),
                  _const_spec((1, GMLP_W)), _const_spec((1, GMLP_W)), _const_spec((N_GROUPS, BLOCK, BLOCK)),
                  _const_spec((BLOCK, GMLP_W)), _const_spec((GMLP_W, GMLP_W)), _const_spec((1, ATTN_W)),
                  _const_spec((1, GMLP_W))] + [ANY] * n_g,
        out_specs=[cur(ATTN_W), late(2 * KV_W), cur(GMLP_W), cur(GMLP_W),
                   _const_spec((N_HEADS, BLOCK, 2 * BLOCK)), _const_spec((N_GROUPS, BLOCK, BLOCK)),
                   _const_spec((BLOCK, GMLP_W)), _const_spec((8, GMLP_W)), _const_spec((N_HEADS, BLOCK, 1))]
        + [ANY] * n_g,
        scratch_shapes=[pltpu.VMEM((BLOCK, 2 * KV_W), F32), pltpu.VMEM((BLOCK, 2 * KV_W), F32)]
        + _ChipExchange.sems(n_g),
        compiler_params=_params(("arbitrary",)),
    )(q, kv, kv, gu, gv, dmix, bias, sinks, gln_g, gln_b, wsm, bsx, amat, aog, gog, *grad_parts)
    return outs[:9], outs[9:]


def _mix_finalize(gbias, bucket, dws, dbs, dsink):
    def body(gb_ref, bucket_ref, dws_ref, dbs_ref, dsink_ref, tall_ref):
        bk = bucket_ref[...]
        lane = lax.broadcasted_iota(jnp.int32, (N_BUCKETS, LANES), 1)
        rowi = lax.broadcasted_iota(jnp.int32, (N_BUCKETS, LANES), 0)
        drb = jnp.zeros((N_BUCKETS, LANES), F32)
        dsk = jnp.zeros((8, LANES), F32)
        lane8 = lax.broadcasted_iota(jnp.int32, (8, LANES), 1)
        for h in range(N_HEADS):
            g = gb_ref[h]
            for b in range(N_BUCKETS):
                tot = jnp.sum(_colsum(jnp.where(bk == float(b), g, 0.0)), axis=1, keepdims=True)
                drb = jnp.where((rowi == h) & (lane == b), tot, drb)
            sk = jnp.sum(dsink_ref[h], axis=0, keepdims=True)
            dsk = jnp.where(lane8 == h, sk, dsk)
        tall_ref[TALL_RB:TALL_RB + N_BUCKETS, :] = drb
        tall_ref[TALL_SK:TALL_SK + 8, :] = dsk
        ti = lax.broadcasted_iota(jnp.int32, (BLOCK, BLOCK), 0)
        ui = lax.broadcasted_iota(jnp.int32, (BLOCK, BLOCK), 1)
        for g in range(N_GROUPS):
            tall_ref[g * BLOCK:(g + 1) * BLOCK, :] = jnp.where(ti >= ui, dws_ref[g], 0.0)
        gi = lax.broadcasted_iota(jnp.int32, (GMLP_W, LANES), 0) // GROUP_DIM
        li = lax.broadcasted_iota(jnp.int32, (GMLP_W, LANES), 1)
        ind = jnp.where(gi == li, 1.0, 0.0).astype(BF16)
        d = dbs_ref[...]
        hi = d.astype(BF16)
        r1 = d - hi.astype(F32)
        mid = r1.astype(BF16)
        lo = (r1 - mid.astype(F32)).astype(BF16)
        dbsg = _dot(hi, ind) + _dot(mid, ind) + _dot(lo, ind)
        tall_ref[TALL_BS:TALL_BS + N_GROUPS, :] = dbsg.T[:N_GROUPS, :]

    return pl.pallas_call(
        body, name="mix_finalize", out_shape=jax.ShapeDtypeStruct((TALL_ROWS, LANES), F32), grid=(1,),
        in_specs=[_const_spec((N_HEADS, BLOCK, 2 * BLOCK)), _const_spec((BLOCK, 2 * BLOCK)),
                  _const_spec((N_GROUPS, BLOCK, BLOCK)), _const_spec((BLOCK, GMLP_W)),
                  _const_spec((N_HEADS, BLOCK, 1))],
        out_specs=_const_spec((TALL_ROWS, LANES)),
        compiler_params=_params(("arbitrary",)),
    )(gbias, bucket, dws, dbs, dsink)


def _bwd_in(dq, dkv, dgu, dgv, dxa, x, modr, w_in, tm):
    s = x.shape[0]

    def body(dq_ref, dkv_ref, dgu_ref, dgv_ref, dxa_ref, x_ref, mod_ref, w_ref, gx_ref, acc_ref, db_ref):
        @pl.when(pl.program_id(0) == 0)
        def _():
            acc_ref[...] = jnp.zeros_like(acc_ref)
            db_ref[...] = jnp.zeros_like(db_ref)

        dproj = jnp.concatenate([dq_ref[...], dkv_ref[...], dgu_ref[...], dgv_ref[...]], axis=1)
        dh1 = _dot(dproj, w_ref[...])
        gx_ref[...] = dxa_ref[...].astype(F32) + dh1 * (1.0 + mod_ref[1:2, :])
        acc_ref[0:1, :] += _colsum(dh1 * x_ref[...].astype(F32))
        acc_ref[1:2, :] += _colsum(dh1)
        db_ref[0:1, :] += _colsum(dproj.astype(F32))

    row = lambda w: pl.BlockSpec((tm, w), lambda i: (i, 0))
    return pl.pallas_call(
        body, name="bwd_in",
        out_shape=(jax.ShapeDtypeStruct((s, D_MODEL), F32), jax.ShapeDtypeStruct((8, D_MODEL), F32),
                   jax.ShapeDtypeStruct((8, IN_W), F32)),
        grid=(s // tm,),
        in_specs=[row(ATTN_W), row(2 * KV_W), row(GMLP_W), row(GMLP_W), row(D_MODEL), row(D_MODEL),
                  _const_spec((8, D_MODEL)), _const_spec(w_in.shape)],
        out_specs=(row(D_MODEL), _const_spec((8, D_MODEL)), _const_spec((8, IN_W))),
        compiler_params=_params(("arbitrary",)),
    )(dq, dkv, dgu, dgv, dxa, x, modr, w_in)


def _wgrad(a, bs, tm, tk, name, transposed=False, gather_vs=()):
    k_all, m = a.shape
    n = sum(b.shape[1] for b in bs)
    nk = k_all // tk
    nm = m // tm
    n_b = len(bs)
    n_v = len(gather_vs)

    def body(a_ref, *rest):
        b_refs, v_refs = rest[:n_b], rest[n_b:n_b + n_v]
        o_ref, ob_ref = rest[n_b + n_v:n_b + n_v + 2]
        vg_refs = rest[n_b + n_v + 2:n_b + 2 * n_v + 2]
        i, k = pl.program_id(0), pl.program_id(1)
        if n_v:
            gather = _Gather8(v_refs, vg_refs, *rest[n_b + 2 * n_v + 2:])

            @pl.when((i == 0) & (k == 0))
            def _():
                gather.start()

            @pl.when(i * nk + k == (nm * nk) // 2)
            def _():
                gather.forward()

        @pl.when(k == 0)
        def _():
            o_ref[...] = jnp.zeros_like(o_ref)

        b = b_refs[0][...] if n_b == 1 else jnp.concatenate([r[...] for r in b_refs], axis=1)
        if transposed:
            o_ref[...] += _dot_tn(b, a_ref[...])
        else:
            o_ref[...] += _dot_tn(a_ref[...], b)

        @pl.when(k == nk - 1)
        def _():
            ob_ref[...] = o_ref[...].astype(BF16)

        if n_v:
            @pl.when((i == nm - 1) & (k == nk - 1))
            def _():
                gather.finish()

    if transposed:
        out_spec = pl.BlockSpec((n, tm), lambda i, k: (0, i))
        shape = (n, m)
    else:
        out_spec = pl.BlockSpec((tm, n), lambda i, k: (i, 0))
        shape = (m, n)
    outs = pl.pallas_call(
        body, name=name,
        out_shape=[jax.ShapeDtypeStruct(shape, F32), jax.ShapeDtypeStruct(shape, BF16)] + _gathered8_shapes(gather_vs),
        grid=(nm, nk),
        in_specs=[pl.BlockSpec((tk, tm), lambda i, k: (k, i))]
        + [pl.BlockSpec((tk, b.shape[1]), lambda i, k: (k, 0)) for b in bs] + [ANY] * n_v,
        out_specs=[out_spec, out_spec] + [ANY] * n_v,
        scratch_shapes=_Gather8.sems(n_v) if n_v else [],
        compiler_params=_params(("arbitrary", "arbitrary") if n_v else ("parallel", "arbitrary")),
    )(a, *bs, *gather_vs)
    return outs[0], outs[1], outs[2:]


def _adam_math(w, g, m, v):
    m2 = ADAM_B1 * m + (1.0 - ADAM_B1) * g
    v2 = ADAM_B2 * v + (1.0 - ADAM_B2) * (g * g)
    m_hat = m2 / (1.0 - ADAM_B1 ** ADAM_STEP)
    v_hat = v2 / (1.0 - ADAM_B2 ** ADAM_STEP)
    delta = -ADAM_LR * (m_hat / (jnp.sqrt(v_hat) + ADAM_EPS) + ADAM_WD * w)
    return delta, m2, v2


def _adam_halves(w, mine, got, m, v, tr, name):
    r, cc = w.shape
    h = r // 2
    nt = h // tr

    def body(c_ref, w_ref, mine_ref, got_ref, m_ref, v_ref, g_ref, d_ref, m2_ref, v2_ref):
        g = jnp.where(pl.program_id(0) == c_ref[0], mine_ref[...], got_ref[...])
        g_ref[...] = g
        d, m2, v2 = _adam_math(w_ref[...], g, m_ref[...], v_ref[...])
        d_ref[...] = d
        m2_ref[...] = m2
        v2_ref[...] = v2

    full = pl.BlockSpec((tr, cc), lambda hh, i, c_ref: (hh * nt + i, 0))
    half = pl.BlockSpec((tr, cc), lambda hh, i, c_ref: (i, 0))
    shp = jax.ShapeDtypeStruct((r, cc), F32)
    return pl.pallas_call(
        body, name=name, out_shape=(shp, shp, shp, shp),
        grid_spec=pltpu.PrefetchScalarGridSpec(
            num_scalar_prefetch=1, grid=(2, nt), in_specs=[full, half, half, full, full],
            out_specs=(full, full, full, full)),
        compiler_params=_params(("arbitrary", "arbitrary")),
    )(_core_index_scalar(), w, mine, got, m, v)


def _adam_w_ada(sc_t, dmod_all, w, m, v, tr):
    r, cc = w.shape

    def body(chip_ref, sct_ref, dm_ref, w_ref, m_ref, v_ref, g_ref, d_ref, m2_ref, v2_ref):
        g = sct_ref[:, 0:1] * dm_ref[0:1, :]
        for k in range(1, N_DEV):
            g = g + sct_ref[:, k:k + 1] * dm_ref[k:k + 1, :]
        g_ref[...] = g
        d, m2, v2 = _adam_math(w_ref[...], g, m_ref[...], v_ref[...])
        d_ref[...] = d
        m2_ref[...] = m2
        v2_ref[...] = v2

    spec = pl.BlockSpec((tr, cc), lambda i, chip_ref: (i, 0))
    shp = jax.ShapeDtypeStruct((r, cc), F32)
    return pl.pallas_call(
        body, name="adam_w_ada", out_shape=(shp, shp, shp, shp),
        grid_spec=pltpu.PrefetchScalarGridSpec(
            num_scalar_prefetch=1, grid=(r // tr,),
            in_specs=[pl.BlockSpec((tr, N_DEV), lambda i, chip_ref: (i, 0)),
                      pl.BlockSpec((N_DEV, cc), lambda i, chip_ref: (0, chip_ref[0])), spec, spec, spec],
            out_specs=(spec, spec, spec, spec)),
        compiler_params=_params(("parallel",)),
    )(_chip_index_scalar(), sc_t, dmod_all, w, m, v)


def _pack_wide(acc_i, acc_m, acc_f, db_in, vec):
    arrs = [acc_i, acc_m, acc_f, db_in, vec]
    i_, m_, f_, b_, v_ = range(5)
    src = {"b_in": (b_, 0), "ln1_g": (m_, 2), "ln1_b": (m_, 3), "ln2_g": (f_, 1), "ln2_b": (f_, 2),
           "gmlp_ln_g": (v_, 2), "gmlp_ln_b": (v_, 3), "attn_out_g": (v_, 0), "gmlp_out_g": (v_, 1), "loss": (f_, 0)}
    dmod = [(i_, 1), (i_, 0), (m_, 4), (m_, 1), (m_, 0), (f_, 3)]

    def body(*refs):
        ins, wide_ref = refs[:5], refs[5]
        wide_ref[...] = jnp.zeros_like(wide_ref)
        for k, (a, row) in enumerate(dmod):
            wide_ref[0:1, k * D_MODEL:(k + 1) * D_MODEL] = ins[a][row:row + 1, :]
        for name, (a, row) in src.items():
            r, off, n = WIDE_LAYOUT[name]
            wide_ref[r:r + 1, off:off + n] = ins[a][row:row + 1, :]

    return pl.pallas_call(
        body, name="pack_wide", out_shape=jax.ShapeDtypeStruct((8, WIDE_W), F32), grid=(1,),
        in_specs=[_const_spec(a.shape) for a in arrs], out_specs=_const_spec((8, WIDE_W)),
        compiler_params=_params(("arbitrary",)),
    )(*arrs)


def _adam_small(gw, gt, wide_wmv, w_s, b_s, rel_bias, sinks, after):
    names = list(WIDE_PARAMS)
    tall = [("gmlp_w_s", w_s), ("gmlp_b_s", b_s), ("rel_bias", rel_bias), ("attn_sinks", sinks)]
    ins = [gw, gt]
    for n in names:
        ins += list(wide_wmv[n])
    for _, t in tall:
        ins += list(t)
    n_in = len(ins)

    def body(*refs):
        gw_ref, gt_ref = refs[0], refs[1]
        wmv = refs[2:n_in]
        dmod_ref, loss_ref, loss1_ref = refs[n_in + 1:n_in + 4]
        outs = refs[n_in + 4:]

        def tall_sum(r0, nr):
            g = gt_ref[r0:r0 + nr, :]
            for d in range(1, N_DEV):
                g = g + gt_ref[d * TALL_ROWS + r0:d * TALL_ROWS + r0 + nr, :]
            return g

        def emit(k, g, w_ref, m_ref, v_ref):
            d, m2, v2 = _adam_math(w_ref[...], g, m_ref[...], v_ref[...])
            outs[4 * k][...] = g
            outs[4 * k + 1][...] = d
            outs[4 * k + 2][...] = m2
            outs[4 * k + 3][...] = v2

        gsum = gw_ref[0:8, :]
        for d in range(1, N_DEV):
            gsum = gsum + gw_ref[8 * d:8 * d + 8, :]
        for d in range(N_DEV):
            dmod_ref[d:d + 1, :] = gw_ref[8 * d:8 * d + 1, :]
        for k, n in enumerate(names):
            r, off, sz = WIDE_LAYOUT[n]
            emit(k, gsum[r:r + 1, off:off + sz], *wmv[3 * k:3 * k + 3])
        r, off, sz = WIDE_LAYOUT["loss"]
        tot = jnp.sum(gsum[r:r + 1, off:off + sz], axis=1, keepdims=True)
        loss_ref[...] = jnp.broadcast_to(tot * (0.5 / D_MODEL), loss_ref.shape)
        loss1_ref[...] = tot * (0.5 / D_MODEL)

        k0 = len(names)
        ws_refs = wmv[3 * k0:3 * k0 + 3]
        for g in range(N_GROUPS):
            rows = slice(g * BLOCK, (g + 1) * BLOCK)
            gg = tall_sum(g * BLOCK, BLOCK)
            d, m2, v2 = _adam_math(ws_refs[0][rows, :], gg, ws_refs[1][rows, :], ws_refs[2][rows, :])
            outs[4 * k0][rows, :] = gg
            outs[4 * k0 + 1][rows, :] = d
            outs[4 * k0 + 2][rows, :] = m2
            outs[4 * k0 + 3][rows, :] = v2
        emit(k0 + 1, tall_sum(TALL_BS, N_GROUPS), *wmv[3 * (k0 + 1):3 * (k0 + 1) + 3])
        emit(k0 + 2, tall_sum(TALL_RB, N_HEADS)[:, :N_BUCKETS], *wmv[3 * (k0 + 2):3 * (k0 + 2) + 3])
        emit(k0 + 3, tall_sum(TALL_SK, 8)[0:1, :N_HEADS], *wmv[3 * (k0 + 3):3 * (k0 + 3) + 3])

    out_shapes = [jax.ShapeDtypeStruct((N_DEV, WIDE_W), F32), jax.ShapeDtypeStruct((8, LANES), F32),
                  jax.ShapeDtypeStruct((1, 1), F32)]
    for n in names:
        out_shapes += [jax.ShapeDtypeStruct(wide_wmv[n][0].shape, F32)] * 4
    for _, t in tall:
        out_shapes += [jax.ShapeDtypeStruct(t[0].shape, F32)] * 4
    res = pl.pallas_call(
        body, name="adam_small", out_shape=out_shapes, grid=(1,),
        in_specs=[_const_spec(a.shape) for a in ins] + [ANY], out_specs=[_const_spec(o.shape) for o in out_shapes],
        compiler_params=_params(("arbitrary",)),
    )(*ins, after)
    out = {}
    for k, n in enumerate(names + [t[0] for t in tall]):
        out[n] = tuple(res[3 + 4 * k:7 + 4 * k])
    return res[0], res[1], res[2], out


def kernel(x, c, rel_bias, w_ada, b_ada, w_in, b_in, attn_sinks, gmlp_ln_g, gmlp_ln_b, gmlp_w_s, gmlp_b_s, attn_out_g, gmlp_out_g, w_out, ln1_g, ln1_b, w_gate_up, w_down, ln2_g, ln2_b, loss_target, m_rel_bias, m_w_ada, m_b_ada, m_w_in, m_b_in, m_attn_sinks, m_gmlp_ln_g, m_gmlp_ln_b, m_gmlp_w_s, m_gmlp_b_s, m_attn_out_g, m_gmlp_out_g, m_w_out, m_ln1_g, m_ln1_b, m_w_gate_up, m_w_down, m_ln2_g, m_ln2_b, v_rel_bias, v_w_ada, v_b_ada, v_w_in, v_b_in, v_attn_sinks, v_gmlp_ln_g, v_gmlp_ln_b, v_gmlp_w_s, v_gmlp_b_s, v_attn_out_g, v_gmlp_out_g, v_w_out, v_ln1_g, v_ln1_b, v_w_gate_up, v_w_down, v_ln2_g, v_ln2_b):
    ix, iy, _ = _my_pos()
    chip = 2 * ix + iy
    s = x.shape[1]
    xs = x[0]
    tgt = loss_target[0]
    tm_big = min(512, s)
    tm_ffn = min(FFN_SUB, s)

    w_in_s, w_out_s = w_in[0].T.astype(BF16), w_out[0].astype(BF16)
    w_gu_s, w_dn_s = w_gate_up[0].astype(BF16), w_down[0].astype(BF16)
    sc_all, modr, (w_in_g, w_out_g) = _prologue(jnp.pad(c, ((0, 7), (0, 0))), w_ada[0], b_ada, [w_in_s, w_out_s])
    w_in_f = _insert_own(w_in_g, w_in_s, "blk", chip).reshape(IN_W, D_MODEL)

    bucket = _bucket_table()
    bias, wsm = _prep_tables(bucket, rel_bias.T, gmlp_w_s[0])
    bsx = jnp.repeat(gmlp_b_s[0].T, GROUP_DIM, axis=1)
    amat = _group_mean_matrix()
    sinks = attn_sinks[0]

    (h1, q, kv, gu, gv, xb), (w_dn_g,) = _fwd_in(xs, modr, w_in_f, b_in, tm_big, [w_dn_s], ["blk"])
    w_out_f = _insert_own(w_out_g, w_out_s, "blk", chip).reshape(D_MODEL, D_MODEL)
    (x1, x1b, y, mixed), (w_gu_g,) = _fwd_mix(
        q, kv, gu, gv, xs, modr, bias, sinks, gmlp_ln_g, gmlp_ln_b, wsm, bsx, amat, attn_out_g, gmlp_out_g, w_out_f,
        ln1_g, ln1_b, tm_big, [w_gu_s], ["blk"])
    assert w_gate_up.shape[2] == FF_CHUNK
    w_gu_f = _insert_own(w_gu_g, w_gu_s, "blk", chip)
    w_dn_f = _insert_own(w_dn_g, w_dn_s, "blk", chip).reshape(D_FF, D_MODEL)
    h2, act, dy2, dx1a, acc_f = _fwd_ffn(x1, tgt, modr, ln2_g, ln2_b, w_gu_f, w_dn_f, min(2 * FFN_SUB, s))

    a_act, dgu_ff, dh2 = _bwd_ffn(dy2, act, w_gu_f, w_dn_f, min(FFN_SUB, s))
    g_dn, g_dn_b, _ = _wgrad(a_act, [dy2], D_FF // 2, min(1024, s), "wgrad_down")
    g_gu, g_gu_b, _ = _wgrad(h2, [dgu_ff], 512, min(512, s), "wgrad_gate_up")
    blk3 = lambda a, rows: a.reshape(N_CHIPS, rows, a.shape[1])
    (dxa, dy, dmix, acc_m), (got_dn, got_gu) = _bwd_mid(
        dh2, dx1a, x1b, xs, y, modr, ln1_g, w_out_f, tm_big, [blk3(g_dn_b, D_FF // N_CHIPS), g_gu_b], ["blk", "cols"])
    g_out, g_out_b, _ = _wgrad(mixed, [dy], 512, min(2048, s), "wgrad_out")
    (got_out,) = _swap_halves([blk3(g_out_b, D_MODEL // N_CHIPS)], ["blk"], "rs_swap_out")
    kinds_a = ["blk", "cols", "blk"]
    fulls_a = [blk3(g_dn, D_FF // N_CHIPS), g_gu, blk3(g_out, D_MODEL // N_CHIPS)]
    gots_a = [got_dn, got_gu, got_out]
    parts_a = [_add_halves(f, g, k, "rs_add_a%d" % i) for i, (f, g, k) in enumerate(zip(fulls_a, gots_a, kinds_a))]
    (dq, dkv, dgu, dgv, gbias, dws, dbs, vec, dsink), rxs_a = _bwd_mix(
        q, kv, gu, gv, dmix, bias, sinks, gmlp_ln_g, gmlp_ln_b, wsm, bsx, amat, attn_out_g, gmlp_out_g,
        [p[1] for p in parts_a], kinds_a)
    tall_g = _mix_finalize(gbias, bucket, dws, dbs, dsink)
    grad_x, acc_i, db_in = _bwd_in(dq, dkv, dgu, dgv, dxa, xb, modr, w_in_f, tm_big)

    wide_g = _pack_wide(acc_i, acc_m, acc_f, db_in, vec)
    full_in, full_in_b, (gw, gt) = _wgrad(h1, [dq, dkv, dgu, dgv], 1024, min(1024, s), "wgrad_in", transposed=True,
                                          gather_vs=[wide_g, tall_g])
    (got_in,) = _swap_halves([blk3(full_in_b, IN_W // N_CHIPS)], ["blk"], "rs_swap_in")
    part_in = _add_halves(blk3(full_in, IN_W // N_CHIPS), got_in, "blk", "rs_add_in")
    in_send, in_recv, in_part, in_rx, token = _exchange_start(part_in[1], "rs_chips_in_start")
    wide_wmv ={"b_ada": (b_ada, m_b_ada, v_b_ada), "b_in": (b_in, m_b_in, v_b_in),
                "ln1_g": (ln1_g, m_ln1_g, v_ln1_g), "ln1_b": (ln1_b, m_ln1_b, v_ln1_b),
                "ln2_g": (ln2_g, m_ln2_g, v_ln2_g), "ln2_b": (ln2_b, m_ln2_b, v_ln2_b),
                "gmlp_ln_g": (gmlp_ln_g, m_gmlp_ln_g, v_gmlp_ln_g), "gmlp_ln_b": (gmlp_ln_b, m_gmlp_ln_b, v_gmlp_ln_b),
                "attn_out_g": (attn_out_g, m_attn_out_g, v_attn_out_g),
                "gmlp_out_g": (gmlp_out_g, m_gmlp_out_g, v_gmlp_out_g)}
    rows2 = lambda a: a.reshape(-1, a.shape[-1])
    dmod_all, loss_t, loss1, small = _adam_small(
        gw, gt, wide_wmv, tuple(rows2(a) for a in (gmlp_w_s, m_gmlp_w_s, v_gmlp_w_s)),
        tuple(rows2(a) for a in (gmlp_b_s, m_gmlp_b_s, v_gmlp_b_s)), (rel_bias.T, m_rel_bias.T, v_rel_bias.T),
        (attn_sinks, m_attn_sinks, v_attn_sinks), token)
    small["rel_bias"] = tuple(a.T for a in small["rel_bias"])
    loss = loss1.reshape(())

    g_ada, d_ada, m_ada, v_ada = _adam_w_ada(sc_all.T, dmod_all, w_ada[0], m_w_ada[0], v_w_ada[0], 256)

    sums = [(parts_a[0][0], rxs_a[0], 176), (parts_a[1][0], rxs_a[1], 256), (parts_a[2][0], rxs_a[2], 128)]
    mine = [_sum_chips(p, rx, tr, "rs_sum_%d" % i, loss_t) for i, (p, rx, tr) in enumerate(sums)]
    got = _share_halves(mine, "rs_share")
    gs_dn, d_dn, m_dn, v_dn = _adam_halves(w_down[0], mine[0], got[0], m_w_down[0], v_w_down[0], 176, "adam_w_down")
    gs_gu, d_gu, m_gu, v_gu = _adam_halves(w_gate_up[0], mine[1], got[1], m_w_gate_up[0], v_w_gate_up[0], 256,
                                           "adam_w_gate_up")
    gs_out, d_out, m_out, v_out = _adam_halves(w_out[0], mine[2], got[2], m_w_out[0], v_w_out[0], 128, "adam_w_out")

    rx_in = _exchange_wait(in_send, in_recv, in_part, in_rx, d_gu, "rs_chips_in_wait")
    mine_in = _sum_chips(part_in[0], rx_in, 112, "rs_sum_in", rx_in)
    (got_in_half,) = _share_halves([mine_in], "rs_share_in")
    in_t = _adam_halves(w_in[0].T, mine_in, got_in_half, m_w_in[0].T, v_w_in[0].T, 112, "adam_w_in")
    gs_in, d_in, m_in, v_in = (a.T for a in in_t)

    big = {"w_ada": (g_ada, d_ada, m_ada, v_ada), "w_in": (gs_in, d_in, m_in, v_in), "w_out": (gs_out, d_out, m_out, v_out),
           "w_gate_up": (gs_gu, d_gu, m_gu, v_gu), "w_down": (gs_dn, d_dn, m_dn, v_dn)}
    order = ["rel_bias", "w_ada", "b_ada", "w_in", "b_in", "attn_sinks", "gmlp_ln_g", "gmlp_ln_b", "gmlp_w_s", "gmlp_b_s",
             "attn_out_g", "gmlp_out_g", "w_out", "ln1_g", "ln1_b", "w_gate_up", "w_down", "ln2_g", "ln2_b"]
    shapes = {"gmlp_w_s": gmlp_w_s.shape, "gmlp_b_s": gmlp_b_s.shape}
    outs = [loss, grad_x[None]]
    for k in range(4):
        for name in order:
            if name in big:
                outs.append(big[name][k][None])
            elif name in shapes:
                outs.append(small[name][k].reshape(shapes[name]))
            else:
                outs.append(small[name][k])
    return tuple(outs)
```

```python
import math

import numpy as np
import jax
import jax.numpy as jnp
from jax import lax
from jax.experimental import pallas as pl
from jax.experimental.pallas import tpu as pltpu

F32 = jnp.float32
BF16 = jnp.bfloat16
MESH = pl.DeviceIdType.MESH

D_MODEL = 1024
N_HEADS = 8
N_KV = 2
HEAD_DIM = 64
ATTN_W = N_HEADS * HEAD_DIM
KV_W = N_KV * HEAD_DIM
N_GROUPS = 8
GROUP_DIM = 64
GMLP_W = N_GROUPS * GROUP_DIM
IN_W = ATTN_W + 2 * KV_W + 2 * GMLP_W
BLOCK = 128
N_BUCKETS = 32
MAX_DISTANCE = 128
D_FF = 2816
ALPHA = 2.0 ** 0.25
LN_EPS = 1e-5
NEG_INF = -1e30
ADAM_LR, ADAM_B1, ADAM_B2, ADAM_EPS, ADAM_WD, ADAM_STEP = 0.001, 0.9, 0.999, 1e-8, 0.01, 10
N_CHIPS = 4
N_DEV = 8
LANES = 128
V7X_VMEM_LIMIT = 56 * 2 ** 20
GELU_C = math.sqrt(2.0 / math.pi)
Q_SCALE = HEAD_DIM ** -0.5
ANY = pl.BlockSpec(memory_space=pl.ANY)

TALL_BS = N_GROUPS * BLOCK
TALL_RB = TALL_BS + 8
TALL_SK = TALL_RB + N_BUCKETS
TALL_ROWS = TALL_SK + 8
WIDE_W = 6 * D_MODEL
WIDE_LAYOUT = {
    "b_ada": (0, 0, 6 * D_MODEL),
    "b_in": (1, 0, IN_W), "ln1_g": (1, IN_W, D_MODEL), "ln1_b": (1, IN_W + D_MODEL, D_MODEL),
    "ln2_g": (1, IN_W + 2 * D_MODEL, D_MODEL), "ln2_b": (1, IN_W + 3 * D_MODEL, D_MODEL),
    "gmlp_ln_g": (2, 0, GMLP_W), "gmlp_ln_b": (2, GMLP_W, GMLP_W), "attn_out_g": (2, 2 * GMLP_W, ATTN_W),
    "gmlp_out_g": (2, 2 * GMLP_W + ATTN_W, GMLP_W), "loss": (2, 3 * GMLP_W + ATTN_W, D_MODEL)}
WIDE_PARAMS = tuple(n for n in WIDE_LAYOUT if n != "loss")


def _params(sem=None):
    return pltpu.CompilerParams(dimension_semantics=sem, vmem_limit_bytes=V7X_VMEM_LIMIT)


def _const_spec(shape, single=False):
    nd = len(shape)
    if single:
        return pl.BlockSpec(shape, lambda *_: (0,) * nd, pipeline_mode=pl.Buffered(1))
    return pl.BlockSpec(shape, lambda *_: (0,) * nd)


def _dot(a, b):
    return jnp.dot(a, b, preferred_element_type=F32)


def _dot_nt(a, b):
    return lax.dot_general(a, b, (((1,), (1,)), ((), ())), preferred_element_type=F32)


def _dot_tn(a, b):
    return lax.dot_general(a, b, (((0,), (0,)), ((), ())), preferred_element_type=F32)


def _gelu(x):
    t = jnp.tanh(GELU_C * (x + 0.044715 * x * x * x))
    return 0.5 * x * (1.0 + t), t


def _gelu_grad(x, t):
    return 0.5 * (1.0 + t) + 0.5 * x * (1.0 - t * t) * GELU_C * (1.0 + 3.0 * 0.044715 * x * x)


def _split_dot(x, a):
    hi = x.astype(BF16)
    lo = (x - hi.astype(F32)).astype(BF16)
    return _dot(hi, a) + _dot(lo, a)


def _group_mean_matrix():
    g = np.arange(GMLP_W) // GROUP_DIM
    return jnp.asarray((g[:, None] == g[None, :]).astype(np.float32) / GROUP_DIM, dtype=BF16)


def _ln_stats(z):
    mu = jnp.mean(z, axis=-1, keepdims=True)
    d = z - mu
    var = jnp.mean(d * d, axis=-1, keepdims=True)
    rstd = lax.rsqrt(var + LN_EPS)
    return d * rstd, rstd


def _ln_bwd(dxhat, xhat, rstd):
    m1 = jnp.mean(dxhat, axis=-1, keepdims=True)
    m2 = jnp.mean(dxhat * xhat, axis=-1, keepdims=True)
    return rstd * (dxhat - m1 - xhat * m2)


def _colsum(x):
    return jnp.sum(x, axis=0, keepdims=True)


def _my_pos():
    return lax.axis_index("x"), lax.axis_index("y"), lax.axis_index("c")


def _other_chips(x, y):
    return [(1 - x, y), (x, 1 - y), (1 - x, 1 - y)]


def _chip_index_scalar():
    ix, iy, _ = _my_pos()
    return jnp.reshape(2 * ix + iy, (1,)).astype(jnp.int32)


def _core_index_scalar():
    return jnp.reshape(lax.axis_index("c"), (1,)).astype(jnp.int32)


class _Gather8:
    def __init__(self, x_refs, out_refs, send_sems, recv_sems, local_sems):
        self.x_refs, self.out_refs = x_refs, out_refs
        self.send_sems, self.recv_sems, self.local_sems = send_sems, recv_sems, local_sems
        self.x, self.y, self.c = _my_pos()
        self.me, self.sibling = (self.x, self.y, self.c), (self.x, self.y, 1 - self.c)
        self.chips = _other_chips(self.x, self.y)

    def _rows(self, a, px, py, pc):
        m_per = self.x_refs[a].shape[0]
        return self.out_refs[a].at[pl.ds((4 * px + 2 * py + pc) * m_per, m_per), :]

    def _copy(self, a, k, block, to, src=None):
        return pltpu.make_async_remote_copy(
            src_ref=self._rows(a, *block) if src is None else src, dst_ref=self._rows(a, *block),
            send_sem=self.send_sems.at[7 * a + k], recv_sem=self.recv_sems.at[7 * a + k], device_id=to,
            device_id_type=MESH)

    def _local(self, a):
        return pltpu.make_async_copy(self.x_refs[a], self._rows(a, *self.me), self.local_sems.at[a])

    def start(self):
        for a in range(len(self.x_refs)):
            self._local(a).start()
            self._copy(a, 0, self.me, self.sibling, src=self.x_refs[a]).start()
            for j, chip in enumerate(self.chips):
                self._copy(a, 1 + j, self.me, (*chip, self.c), src=self.x_refs[a]).start()

    def forward(self):
        for a in range(len(self.x_refs)):
            for j, chip in enumerate(self.chips):
                self._copy(a, 1 + j, (*chip, self.c), self.me).wait_recv()
                self._copy(a, 4 + j, (*chip, self.c), self.sibling).start()

    def finish(self):
        for a in range(len(self.x_refs)):
            self._copy(a, 0, self.sibling, self.me).wait_recv()
            for j, chip in enumerate(self.chips):
                self._copy(a, 4 + j, (*chip, 1 - self.c), self.me).wait_recv()
        for a in range(len(self.x_refs)):
            for k in range(7):
                self._copy(a, k, self.me, self.me).wait_send()
            self._local(a).wait()

    @staticmethod
    def sems(n_v):
        return [pltpu.SemaphoreType.DMA((7 * n_v,)), pltpu.SemaphoreType.DMA((7 * n_v,)),
                pltpu.SemaphoreType.DMA((n_v,))]


def _gathered8_shapes(vs):
    return [jax.ShapeDtypeStruct((N_DEV * v.shape[0], v.shape[1]), v.dtype) for v in vs]


VMEM_WHOLE = pl.BlockSpec(memory_space=pltpu.VMEM)


def _prologue(c_pad, w_ada_s, b_ada, shards):
    n = w_ada_s.shape[1]
    n_w = len(shards)
    assert N_CHIPS * n == 6 * D_MODEL and n % LANES == 0

    def body(c_ref, w_ref, b_ref, *rest):
        sc_ref, modc_ref, modg_ref, modr_ref = rest[n_w:n_w + 4]
        gathered_refs = rest[n_w + 4:2 * n_w + 4]
        call_ref, w_vmem = rest[2 * n_w + 4:2 * n_w + 6]
        sems = rest[2 * n_w + 6:]
        ix, iy, ic = _my_pos()
        chip = 2 * ix + iy
        weights = _WeightGather(gathered_refs, ["blk"] * n_w, sems[0], sems[1])
        gather_c = _Gather8([c_ref], [call_ref], sems[2], sems[3], sems[4])
        gather_mod = _Gather8([modc_ref], [modg_ref], sems[5], sems[6], sems[7])
        load_w = pltpu.make_async_copy(w_ref, w_vmem, sems[8])
        weights.start()
        gather_c.start()
        load_w.start()
        gather_c.forward()
        gather_c.finish()
        cv = call_ref[...]
        sc = cv * _sigmoid(cv)
        a_hi = sc.astype(BF16)
        a_lo = (sc - a_hi.astype(F32)).astype(BF16)
        load_w.wait()
        w = w_vmem[...]
        w_hi = w.astype(BF16)
        w_lo = (w - w_hi.astype(F32)).astype(BF16)
        b = b_ref[:, 0:n]
        for k in range(1, N_CHIPS):
            b = jnp.where(chip == k, b_ref[:, k * n:(k + 1) * n], b)
        mod = _dot(a_hi, w_hi) + _dot(a_hi, w_lo) + _dot(a_lo, w_hi) + b
        for d in range(N_DEV):
            sc_ref[d:d + 1, :] = sc[8 * d:8 * d + 1, :]
            modc_ref[d:d + 1, :] = mod[8 * d:8 * d + 1, :]
        gather_mod.start()
        weights.forward()
        gather_mod.forward()
        gather_mod.finish()
        dev = 2 * chip + ic
        mine = jnp.concatenate([modg_ref[pl.ds(2 * 8 * k + dev, 1), :] for k in range(N_CHIPS)], axis=1)
        modr_ref[...] = jnp.zeros_like(modr_ref)
        for r in range(6):
            modr_ref[r:r + 1, :] = mine[:, r * D_MODEL:(r + 1) * D_MODEL]
        weights.forward_diagonal()
        weights.finish()

    outs = pl.pallas_call(
        body, name="prologue",
        out_shape=[jax.ShapeDtypeStruct((N_DEV, D_MODEL), F32), jax.ShapeDtypeStruct((N_DEV, n), F32),
                   jax.ShapeDtypeStruct((N_DEV * N_DEV, n), F32), jax.ShapeDtypeStruct((8, D_MODEL), F32)]
        + [jax.ShapeDtypeStruct(sh.shape, BF16) for sh in shards],
        in_specs=[VMEM_WHOLE, ANY, VMEM_WHOLE] + [ANY] * n_w,
        out_specs=[VMEM_WHOLE, VMEM_WHOLE, VMEM_WHOLE, VMEM_WHOLE] + [ANY] * n_w,
        input_output_aliases={3 + a: 4 + a for a in range(n_w)},
        scratch_shapes=[pltpu.VMEM((N_DEV * 8, D_MODEL), F32), pltpu.VMEM(w_ada_s.shape, F32)]
        + _WeightGather.sems(n_w) + _Gather8.sems(1) + _Gather8.sems(1) + [pltpu.SemaphoreType.DMA],
        compiler_params=pltpu.CompilerParams(vmem_limit_bytes=V7X_VMEM_LIMIT),
    )(c_pad, w_ada_s, b_ada, *shards)
    return outs[0], outs[3], outs[4:]


def _with_own_block(shard, chip):
    empty = lax.empty((N_CHIPS,) + shard.shape, BF16)
    return lax.dynamic_update_slice(empty, shard.astype(BF16)[None], (chip, 0, 0))


class _WeightGather:
    N_SEM = 8

    def __init__(self, gathered, kinds, send_sems, recv_sems):
        self.gathered, self.kinds = gathered, kinds
        self.send_sems, self.recv_sems = send_sems, recv_sems
        self.x, self.y, self.c = _my_pos()
        self.me, self.sibling = (self.x, self.y, self.c), (self.x, self.y, 1 - self.c)
        self.nbr = ((1 - self.x, self.y), (self.x, 1 - self.y))
        self.diag = 2 * (1 - self.x) + (1 - self.y)

    def _dst(self, a, chip, pc, quarter=None):
        r, cc = self._shard_shape(a)
        h = r // 2
        row0, rows = pc * h, h
        if quarter is not None:
            row0, rows = pc * h + quarter * (h // 2), h // 2
        g = self.gathered[a]
        if self.kinds[a] == "blk":
            return g.at[chip, pl.ds(row0, rows), :]
        return g.at[pl.ds(row0, rows), pl.ds(chip * cc, cc)]

    def _copy(self, a, k, region, to):
        return pltpu.make_async_remote_copy(
            src_ref=region, dst_ref=region, send_sem=self.send_sems.at[a * self.N_SEM + k],
            recv_sem=self.recv_sems.at[a * self.N_SEM + k], device_id=to, device_id_type=MESH)

    def _arrays(self):
        return range(len(self.gathered))

    def _shard_shape(self, a):
        shape = self.gathered[a].shape
        return shape[1:] if self.kinds[a] == "blk" else (shape[0], shape[1] // N_CHIPS)

    def start(self):
        my_chip = 2 * self.x + self.y
        for a in self._arrays():
            for j, chip in enumerate(self.nbr):
                self._copy(a, j, self._dst(a, my_chip, self.c), (*chip, self.c)).start()

    def forward(self):
        for a in self._arrays():
            for j, chip in enumerate(self.nbr):
                cj = 2 * chip[0] + chip[1]
                half = self._dst(a, cj, self.c)
                self._copy(a, j, half, self.me).wait_recv()
                self._copy(a, 2 + j, half, self.sibling).start()
                other = self.nbr[1 - j]
                self._copy(a, 4 + j, self._dst(a, cj, self.c, quarter=j), (*other, self.c)).start()

    def forward_diagonal(self):
        for a in self._arrays():
            for j in range(2):
                quarter = self._dst(a, self.diag, self.c, quarter=j)
                self._copy(a, 4 + j, quarter, self.me).wait_recv()
                self._copy(a, 6 + j, quarter, self.sibling).start()

    def finish(self):
        for a in self._arrays():
            for j, chip in enumerate(self.nbr):
                self._copy(a, 2 + j, self._dst(a, 2 * chip[0] + chip[1], 1 - self.c), self.me).wait_recv()
                self._copy(a, 6 + j, self._dst(a, self.diag, 1 - self.c, quarter=j), self.me).wait_recv()
        for a in self._arrays():
            half = self._dst(a, self.diag, self.c)
            quarter = self._dst(a, self.diag, self.c, quarter=0)
            for k in range(self.N_SEM):
                self._copy(a, k, half if k < 4 else quarter, self.me).wait_send()

    @classmethod
    def sems(cls, n_arr):
        return [pltpu.SemaphoreType.DMA((n_arr * cls.N_SEM,)), pltpu.SemaphoreType.DMA((n_arr * cls.N_SEM,))]


def _half_of_full(ref, kind, pc):
    if kind == "blk":
        h = ref.shape[1] // 2
        return ref.at[:, pl.ds(pc * h, h), :]
    h = ref.shape[0] // 2
    return ref.at[pl.ds(pc * h, h), :]


def _half_shape(shape, kind):
    return (shape[0], shape[1] // 2, shape[2]) if kind == "blk" else (shape[0] // 2, shape[1])


class _HalfSwap:
    def __init__(self, ins, outs, kinds, send_sems, recv_sems):
        self.ins, self.outs, self.kinds = ins, outs, kinds
        self.send_sems, self.recv_sems = send_sems, recv_sems
        self.x, self.y, self.c = _my_pos()

    def _copies(self):
        for a in range(len(self.ins)):
            yield pltpu.make_async_remote_copy(
                src_ref=_half_of_full(self.ins[a], self.kinds[a], 1 - self.c), dst_ref=self.outs[a],
                send_sem=self.send_sems.at[a], recv_sem=self.recv_sems.at[a],
                device_id=(self.x, self.y, 1 - self.c), device_id_type=MESH)

    def start(self):
        for cp in self._copies():
            cp.start()

    def wait(self):
        for cp in self._copies():
            cp.wait()

    @staticmethod
    def sems(n_arr):
        return [pltpu.SemaphoreType.DMA((n_arr,)), pltpu.SemaphoreType.DMA((n_arr,))]

    @staticmethod
    def out_shapes(fulls, kinds):
        return [jax.ShapeDtypeStruct(_half_shape(a.shape, k), a.dtype) for a, k in zip(fulls, kinds)]


def _swap_halves(fulls_bf16, kinds, name):
    n_arr = len(fulls_bf16)

    def body(*refs):
        swap = _HalfSwap(refs[:n_arr], refs[n_arr:2 * n_arr], kinds, *refs[2 * n_arr:])
        swap.start()
        swap.wait()

    return pl.pallas_call(
        body, name=name, out_shape=_HalfSwap.out_shapes(fulls_bf16, kinds),
        in_specs=[ANY] * n_arr, out_specs=[ANY] * n_arr, scratch_shapes=_HalfSwap.sems(n_arr),
    )(*fulls_bf16)


def _add_halves(full, got, kind, name):
    hs = _half_shape(full.shape, kind)

    def body(pos_ref, a_ref, b_ref, o_ref, ob_ref):
        p = a_ref[...] + b_ref[...].astype(F32)
        ob_ref[...] = p.astype(BF16)

        @pl.when(pl.program_id(0) == pos_ref[1])
        def _():
            o_ref[...] = p.reshape(o_ref.shape)

    if kind == "blk":
        nb, h, cc = hs
        own = pl.BlockSpec((1, h, cc), lambda b, pos_ref: (b, pos_ref[0], 0))
        other = pl.BlockSpec((1, h, cc), lambda b, pos_ref: (b, 0, 0))
    else:
        h, cc = hs[0], hs[1] // N_CHIPS
        own = pl.BlockSpec((h, cc), lambda b, pos_ref: (pos_ref[0], b))
        other = pl.BlockSpec((h, cc), lambda b, pos_ref: (0, b))
    pos = jnp.concatenate([_core_index_scalar(), _chip_index_scalar()])
    return pl.pallas_call(
        body, name=name, out_shape=(jax.ShapeDtypeStruct((h, cc), F32), jax.ShapeDtypeStruct(hs, BF16)),
        grid_spec=pltpu.PrefetchScalarGridSpec(
            num_scalar_prefetch=1, grid=(N_CHIPS,), in_specs=[own, other],
            out_specs=(pl.BlockSpec((h, cc), lambda b, pos_ref: (0, 0)), other)),
        compiler_params=_params(("arbitrary",)),
    )(pos, full, got)


def _rx_shape(part_shape, kind):
    if kind == "blk":
        return (3, part_shape[1], part_shape[2])
    return (3, part_shape[0], part_shape[1] // N_CHIPS)


class _ChipExchange:
    def __init__(self, parts, rxs, kinds, send_sems, recv_sems):
        self.parts, self.rxs, self.kinds = parts, rxs, kinds
        self.send_sems, self.recv_sems = send_sems, recv_sems
        self.x, self.y, self.c = _my_pos()
        self.chips = _other_chips(self.x, self.y)

    def _copies(self):
        for a in range(len(self.parts)):
            for j, chip in enumerate(self.chips):
                cj = 2 * chip[0] + chip[1]
                if self.kinds[a] == "blk":
                    src = self.parts[a].at[cj]
                else:
                    cc = self.parts[a].shape[1] // N_CHIPS
                    src = self.parts[a].at[:, pl.ds(cj * cc, cc)]
                yield pltpu.make_async_remote_copy(
                    src_ref=src, dst_ref=self.rxs[a].at[j], send_sem=self.send_sems.at[a * 3 + j],
                    recv_sem=self.recv_sems.at[a * 3 + j], device_id=(*chip, self.c), device_id_type=MESH)

    def start(self):
        for cp in self._copies():
            cp.start()

    def wait(self):
        for cp in self._copies():
            cp.wait_recv()
        for cp in self._copies():
            cp.wait_send()

    @staticmethod
    def sems(n_arr):
        return [pltpu.SemaphoreType.DMA((n_arr * 3,)), pltpu.SemaphoreType.DMA((n_arr * 3,))]


HBM_SPEC = pl.BlockSpec(memory_space=pltpu.HBM)
SEM_SPEC = pl.BlockSpec(memory_space=pltpu.SEMAPHORE)
DATAFLOW = pltpu.SideEffectType.DATAFLOW_SIDE_EFFECTING


def _exchange_start(part, name):
    rx_shape = _rx_shape(part.shape, "blk")

    def body(part_ref, rx_ref, send_sems, recv_sems, part_thru, rx_thru, token):
        _ChipExchange([part_ref], [rx_ref], ["blk"], send_sems, recv_sems).start()
        token[...] = jnp.zeros_like(token)

    return pl.pallas_call(
        body, name=name,
        out_shape=(pltpu.SemaphoreType.DMA((3,)), pltpu.SemaphoreType.DMA((3,)), pltpu.HBM(part.shape, part.dtype),
                   pltpu.HBM(rx_shape, BF16), jax.ShapeDtypeStruct((8, LANES), F32)),
        in_specs=(HBM_SPEC, HBM_SPEC), out_specs=(SEM_SPEC, SEM_SPEC, HBM_SPEC, HBM_SPEC, VMEM_WHOLE),
        input_output_aliases={0: 2, 1: 3}, compiler_params=pltpu.CompilerParams(has_side_effects=DATAFLOW),
    )(pltpu.with_memory_space_constraint(part, pltpu.HBM),
      pltpu.with_memory_space_constraint(lax.empty(rx_shape, BF16), pltpu.HBM))


def _exchange_wait(send_sems, recv_sems, part_thru, rx_thru, after, name):
    def body(part_ref, rx_ref, send_sems, recv_sems, after_ref, part_dead, rx_out):
        _ChipExchange([part_ref], [rx_ref], ["blk"], send_sems, recv_sems).wait()

    return pl.pallas_call(
        body, name=name,
        out_shape=(pltpu.HBM(part_thru.shape, part_thru.dtype), pltpu.HBM(rx_thru.shape, rx_thru.dtype)),
        in_specs=(HBM_SPEC, HBM_SPEC, SEM_SPEC, SEM_SPEC, ANY), out_specs=(HBM_SPEC, HBM_SPEC),
        input_output_aliases={0: 0, 1: 1}, compiler_params=pltpu.CompilerParams(has_side_effects=DATAFLOW),
    )(part_thru, rx_thru, send_sems, recv_sems, after)[1]


def _sum_chips(part, rx, tr, name, after):
    _, h, cc = rx.shape
    flips = (2, 1, 3)

    def body(chip_ref, p_ref, rx_ref, after_ref, o_ref):
        own = p_ref[...]
        for mc in range(N_CHIPS):
            @pl.when(chip_ref[0] == mc)
            def _():
                terms = sorted([(mc, None)] + [(mc ^ f, j) for j, f in enumerate(flips)])
                acc = None
                for _, j in terms:
                    t = own if j is None else rx_ref[j].astype(F32)
                    acc = t if acc is None else acc + t
                o_ref[...] = acc

    return pl.pallas_call(
        body, name=name, out_shape=jax.ShapeDtypeStruct((h, cc), F32),
        grid_spec=pltpu.PrefetchScalarGridSpec(
            num_scalar_prefetch=1, grid=(h // tr,),
            in_specs=[pl.BlockSpec((tr, cc), lambda i, chip_ref: (i, 0)),
                      pl.BlockSpec((3, tr, cc), lambda i, chip_ref: (0, i, 0)), ANY],
            out_specs=pl.BlockSpec((tr, cc), lambda i, chip_ref: (i, 0))),
        compiler_params=_params(("arbitrary",)),
    )(_chip_index_scalar(), part, rx, after)


def _share_halves(halves, name):
    n_arr = len(halves)

    def body(*refs):
        ins, outs = refs[:n_arr], refs[n_arr:2 * n_arr]
        send_sems, recv_sems = refs[2 * n_arr:]
        x, y, c = _my_pos()
        cps = []
        for a in range(n_arr):
            cp = pltpu.make_async_remote_copy(
                src_ref=ins[a], dst_ref=outs[a], send_sem=send_sems.at[a], recv_sem=recv_sems.at[a],
                device_id=(x, y, 1 - c), device_id_type=MESH)
            cp.start()
            cps.append(cp)
        for cp in cps:
            cp.wait()

    return pl.pallas_call(
        body, name=name, out_shape=[jax.ShapeDtypeStruct(h.shape, h.dtype) for h in halves],
        in_specs=[ANY] * n_arr, out_specs=[ANY] * n_arr,
        scratch_shapes=[pltpu.SemaphoreType.DMA((n_arr,)), pltpu.SemaphoreType.DMA((n_arr,))],
    )(*halves)


def _bucket_table():
    qi = jnp.arange(BLOCK)[:, None]
    si = jnp.arange(2 * BLOCK)[None, :]
    dist = qi + BLOCK - si
    max_exact = N_BUCKETS // 2
    n = jnp.maximum(dist, 0)
    nf = jnp.maximum(n, max_exact).astype(F32)
    large = max_exact + (jnp.log(nf / max_exact) / math.log(MAX_DISTANCE / max_exact)
                         * (N_BUCKETS - max_exact)).astype(jnp.int32)
    large = jnp.minimum(large, N_BUCKETS - 1)
    return jnp.where(n < max_exact, n, large).astype(F32)


def _prep_tables(bucket, rel_bias_t, w_s):
    def body(bucket_ref, rb_ref, ws_ref, bias_ref, wsm_ref):
        qi = lax.broadcasted_iota(jnp.int32, (BLOCK, 2 * BLOCK), 0)
        si = lax.broadcasted_iota(jnp.int32, (BLOCK, 2 * BLOCK), 1)
        dist = qi + BLOCK - si
        in_window = (dist >= 0) & (dist < BLOCK)
        bk = bucket_ref[...]
        for h in range(N_HEADS):
            acc = jnp.zeros((BLOCK, 2 * BLOCK), F32)
            for b in range(N_BUCKETS):
                acc = jnp.where(bk == float(b), rb_ref[h, b], acc)
            bias_ref[h] = jnp.where(in_window, acc, NEG_INF)
        ti = lax.broadcasted_iota(jnp.int32, (BLOCK, BLOCK), 0)
        ui = lax.broadcasted_iota(jnp.int32, (BLOCK, BLOCK), 1)
        for g in range(N_GROUPS):
            wsm_ref[g] = jnp.where(ti >= ui, ws_ref[g], 0.0).astype(BF16)

    return pl.pallas_call(
        body, name="prep_tables",
        out_shape=(jax.ShapeDtypeStruct((N_HEADS, BLOCK, 2 * BLOCK), F32),
                   jax.ShapeDtypeStruct((N_GROUPS, BLOCK, BLOCK), BF16)),
        grid=(1,),
        in_specs=[_const_spec((BLOCK, 2 * BLOCK)), pl.BlockSpec(memory_space=pltpu.SMEM),
                  _const_spec((N_GROUPS, BLOCK, BLOCK))],
        out_specs=(_const_spec((N_HEADS, BLOCK, 2 * BLOCK)), _const_spec((N_GROUPS, BLOCK, BLOCK))),
        compiler_params=_params(("arbitrary",)),
    )(bucket, rel_bias_t, w_s)


def _fwd_in(x, modr, w_in, b_in, tm, shards, kinds):
    s = x.shape[0]
    n_steps = s // tm
    fwd_step, diag_step = (8 * n_steps) // 16, (13 * n_steps) // 16
    n_w = len(shards)

    def body(x_ref, mod_ref, w_ref, b_ref, *rest):
        h1_ref, q_ref, kv_ref, gu_ref, gv_ref, xb_ref = rest[n_w:n_w + 6]
        gathered_refs = rest[n_w + 6:2 * n_w + 6]
        send_sems, recv_sems = rest[2 * n_w + 6:]
        i = pl.program_id(0)
        gather = _WeightGather(gathered_refs, kinds, send_sems, recv_sems)

        @pl.when(i == 0)
        def _():
            gather.start()

        xv = x_ref[...]
        xb_ref[...] = xv.astype(BF16)
        h1 = (xv * (1.0 + mod_ref[1:2, :]) + mod_ref[0:1, :]).astype(BF16)
        h1_ref[...] = h1
        proj = _dot_nt(h1, w_ref[...]) + b_ref[...]
        q_ref[...] = (proj[:, :ATTN_W] * Q_SCALE).astype(BF16)
        kv_ref[...] = proj[:, ATTN_W:ATTN_W + 2 * KV_W].astype(BF16)
        gu_ref[...] = proj[:, ATTN_W + 2 * KV_W:ATTN_W + 2 * KV_W + GMLP_W]
        gv_ref[...] = proj[:, ATTN_W + 2 * KV_W + GMLP_W:]

        @pl.when(i == fwd_step)
        def _():
            gather.forward()

        @pl.when(i == diag_step)
        def _():
            gather.forward_diagonal()

        @pl.when(i == n_steps - 1)
        def _():
            gather.finish()

    row = lambda w: pl.BlockSpec((tm, w), lambda i: (i, 0))
    outs = pl.pallas_call(
        body, name="fwd_in",
        out_shape=[jax.ShapeDtypeStruct((s, D_MODEL), BF16), jax.ShapeDtypeStruct((s, ATTN_W), BF16),
                   jax.ShapeDtypeStruct((s, 2 * KV_W), BF16), jax.ShapeDtypeStruct((s, GMLP_W), F32),
                   jax.ShapeDtypeStruct((s, GMLP_W), F32), jax.ShapeDtypeStruct((s, D_MODEL), BF16)]
        + [jax.ShapeDtypeStruct(sh.shape, BF16) for sh in shards],
        grid=(n_steps,),
        in_specs=[row(D_MODEL), _const_spec((8, D_MODEL)), _const_spec(w_in.shape), _const_spec((1, IN_W))]
        + [ANY] * n_w,
        out_specs=[row(D_MODEL), row(ATTN_W), row(2 * KV_W), row(GMLP_W), row(GMLP_W), row(D_MODEL)] + [ANY] * n_w,
        input_output_aliases={4 + a: 6 + a for a in range(n_w)},
        scratch_shapes=_WeightGather.sems(n_w),
        compiler_params=_params(("arbitrary",)),
    )(x, modr, w_in, b_in, *shards)
    return outs[:6], outs[6:]


def _kv_variants(kk):
    kf = kk.astype(F32)
    lane = lax.broadcasted_iota(jnp.int32, kf.shape, 1)
    low = lane < HEAD_DIM
    k0_lo = jnp.where(low, kf, 0.0)
    k1_hi = jnp.where(low, 0.0, kf)
    k0_hi = pltpu.roll(k0_lo, HEAD_DIM, 1)
    k1_lo = pltpu.roll(k1_hi, HEAD_DIM, 1)
    return ((k0_lo.astype(BF16), k0_hi.astype(BF16)), (k1_lo.astype(BF16), k1_hi.astype(BF16)))


def _head_kv(h):
    return h // (N_HEADS // N_KV), h % 2


MIX_GROUP = 2


def _interleave(*gens):
    results = [None] * len(gens)
    active = list(enumerate(gens))
    while active:
        still = []
        for i, g in active:
            try:
                next(g)
                still.append((i, g))
            except StopIteration as done:
                results[i] = done.value
        active = still
    return results


def _attn_block_fwd(q_blk, kk, vv, bias_ref, sinks_ref, first_mask):
    kvar = _kv_variants(kk)
    vvar = _kv_variants(vv)
    heads = range(N_HEADS)
    q_pairs = [q_blk[:, (h // 2) * LANES:(h // 2 + 1) * LANES] for h in heads]
    logits = [_dot_nt(q_pairs[h], kvar[_head_kv(h)[0]][_head_kv(h)[1]]) + bias_ref[h] for h in heads]
    if first_mask is not None:
        logits = [jnp.where(first_mask, NEG_INF, lg) for lg in logits]
    yield
    ms = [jnp.maximum(jnp.max(logits[h], axis=-1, keepdims=True), sinks_ref[h]) for h in heads]
    yield
    es = [jnp.exp(logits[h] - ms[h]) for h in heads]
    ess = [jnp.exp(sinks_ref[h] - ms[h]) for h in heads]
    yield
    invs = [1.0 / (jnp.sum(es[h], axis=-1, keepdims=True) + ess[h]) for h in heads]
    probs = [(es[h] * invs[h], ess[h] * invs[h]) for h in heads]
    yield
    outs = [_dot(probs[h][0].astype(BF16), vvar[_head_kv(h)[0]][_head_kv(h)[1]]) for h in heads]
    pairs = [outs[2 * i] + outs[2 * i + 1] for i in range(N_HEADS // 2)]
    return jnp.concatenate(pairs, axis=1), probs, kvar, vvar


def _gmlp_chunk_fwd(gu, gv, ln_g, ln_b, wsm_ref, bsx, amat):
    u, tu = _gelu(gu)
    a, ta = _gelu(gv)
    yield
    mean = _split_dot(a, amat)
    d = a - mean
    yield
    var = _split_dot(d * d, amat)
    yield
    rstd = lax.rsqrt(var + LN_EPS)
    xhat = d * rstd
    vb = (xhat * ln_g + ln_b).astype(BF16)
    yield
    lane = lax.broadcasted_iota(jnp.int32, (BLOCK, LANES), 1)
    low = lane < GROUP_DIM
    cols = []
    for pair in range(N_GROUPS // 2):
        vp = vb[:, pair * LANES:(pair + 1) * LANES]
        cols.append(jnp.where(low, _dot(wsm_ref[2 * pair], vp), _dot(wsm_ref[2 * pair + 1], vp)))
    mixedv = jnp.concatenate(cols, axis=1) + bsx
    return u * mixedv, (u, tu, ta, xhat, rstd, vb, mixedv)


def _rms(a, g):
    r = lax.rsqrt(jnp.mean(a * a, axis=-1, keepdims=True) + LN_EPS)
    return a * r * g, r


def _fwd_mix(q, kv, gu, gv, x, modr, bias, sinks, gln_g, gln_b, wsm, bsx, amat, aog, gog, w_out, ln1_g, ln1_b, tm,
             ffn_shards, ffn_kinds):
    s = x.shape[0]
    nb = tm // BLOCK
    n_steps = s // tm
    fwd_step, diag_step = (7 * n_steps) // 16, (12 * n_steps) // 16
    n_w = len(ffn_shards)

    def body(q_ref, kv_ref, kvp_ref, gu_ref, gv_ref, x_ref, mod_ref, bias_ref, sinks_ref, glng_ref, glnb_ref, wsm_ref,
             bsx_ref, amat_ref, aog_ref, gog_ref, wout_ref, ln1g_ref, ln1b_ref, *rest):
        x1_ref, x1b_ref, y_ref, mixed_ref = rest[n_w:n_w + 4]
        gathered_refs = rest[n_w + 4:2 * n_w + 4]
        mix_scr, send_sems, recv_sems = rest[2 * n_w + 4:]
        i = pl.program_id(0)
        gather = _WeightGather(gathered_refs, ffn_kinds, send_sems, recv_sems)

        @pl.when(i == 0)
        def _():
            gather.start()

        col = lax.broadcasted_iota(jnp.int32, (BLOCK, 2 * BLOCK), 1)
        for b0 in range(0, nb, MIX_GROUP):
            gens = []
            for b in range(b0, min(b0 + MIX_GROUP, nb)):
                r0 = b * BLOCK
                if b == 0:
                    kvprev = kvp_ref[...]
                    first_mask = (col < BLOCK) & (i == 0)
                else:
                    kvprev = kv_ref[r0 - BLOCK:r0, :]
                    first_mask = None
                kvcur = kv_ref[r0:r0 + BLOCK, :]
                kk = jnp.concatenate([kvprev[:, :KV_W], kvcur[:, :KV_W]], axis=0)
                vv = jnp.concatenate([kvprev[:, KV_W:], kvcur[:, KV_W:]], axis=0)
                gens.append(_attn_block_fwd(q_ref[r0:r0 + BLOCK, :], kk, vv, bias_ref, sinks_ref, first_mask))
                gens.append(_gmlp_chunk_fwd(gu_ref[r0:r0 + BLOCK, :], gv_ref[r0:r0 + BLOCK, :], glng_ref[...],
                                            glnb_ref[...], wsm_ref, bsx_ref[...], amat_ref[...]))
            res = _interleave(*gens)
            for k, b in enumerate(range(b0, min(b0 + MIX_GROUP, nb))):
                r0 = b * BLOCK
                na, _ = _rms(res[2 * k][0], aog_ref[...])
                ng, _ = _rms(res[2 * k + 1][0], gog_ref[...])
                mix_scr[r0:r0 + BLOCK, :ATTN_W] = na.astype(BF16)
                mix_scr[r0:r0 + BLOCK, ATTN_W:] = ng.astype(BF16)
        mixed = mix_scr[...]
        mixed_ref[...] = mixed
        y = _dot(mixed, wout_ref[...])
        y_ref[...] = y.astype(BF16)
        z1 = ALPHA * x_ref[...] + mod_ref[2:3, :] * y
        xhat, _ = _ln_stats(z1)
        x1 = xhat * ln1g_ref[...] + ln1b_ref[...]
        x1_ref[...] = x1
        x1b_ref[...] = x1.astype(BF16)

        @pl.when(i == fwd_step)
        def _():
            gather.forward()

        @pl.when(i == diag_step)
        def _():
            gather.forward_diagonal()

        @pl.when(i == n_steps - 1)
        def _():
            gather.finish()

    row = lambda w: pl.BlockSpec((tm, w), lambda i: (i, 0))
    prev = pl.BlockSpec((BLOCK, 2 * KV_W), lambda i: (jnp.maximum(i * nb - 1, 0), 0))
    outs = pl.pallas_call(
        body, name="fwd_mix",
        out_shape=[jax.ShapeDtypeStruct((s, D_MODEL), F32)] + [jax.ShapeDtypeStruct((s, D_MODEL), BF16)] * 3
        + [jax.ShapeDtypeStruct(sh.shape, BF16) for sh in ffn_shards],
        grid=(n_steps,),
        in_specs=[row(ATTN_W), row(2 * KV_W), prev, row(GMLP_W), row(GMLP_W), row(D_MODEL), _const_spec((8, D_MODEL)),
                  _const_spec((N_HEADS, BLOCK, 2 * BLOCK)), pl.BlockSpec(memory_space=pltpu.SMEM),
                  _const_spec((1, GMLP_W)), _const_spec((1, GMLP_W)), _const_spec((N_GROUPS, BLOCK, BLOCK)),
                  _const_spec((BLOCK, GMLP_W)), _const_spec((GMLP_W, GMLP_W)), _const_spec((1, ATTN_W)),
                  _const_spec((1, GMLP_W)), _const_spec((D_MODEL, D_MODEL)), _const_spec((1, D_MODEL)),
                  _const_spec((1, D_MODEL))] + [ANY] * n_w,
        out_specs=[row(D_MODEL)] * 4 + [ANY] * n_w,
        input_output_aliases={19 + a: 4 + a for a in range(n_w)},
        scratch_shapes=[pltpu.VMEM((tm, D_MODEL), BF16)] + _WeightGather.sems(n_w),
        compiler_params=_params(("arbitrary",)),
    )(q, kv, kv, gu, gv, x, modr, bias, sinks, gln_g, gln_b, wsm, bsx, amat, aog, gog, w_out, ln1_g, ln1_b, *ffn_shards)
    return outs[:4], outs[4:]


FF_BLOCKS = N_CHIPS // 2
FF_CHUNK = D_FF // FF_BLOCKS
FFN_SUB = 256


def _sigmoid(x):
    return 1.0 / (1.0 + jnp.exp(-x))


def _fwd_ffn(x1, target, modr, ln2_g, ln2_b, w_gu, w_dn, tm):
    s = x1.shape[0]

    def body(x1_ref, t_ref, mod_ref, g_ref, b_ref, wgu_ref, wdn_ref, h2_ref, act_ref, dy2_ref, dx1a_ref, acc_ref):
        @pl.when(pl.program_id(0) == 0)
        def _():
            acc_ref[...] = jnp.zeros_like(acc_ref)

        x1v = x1_ref[...]
        h2 = (x1v * (1.0 + mod_ref[4:5, :]) + mod_ref[3:4, :]).astype(BF16)
        h2_ref[...] = h2
        y2 = None
        for cc in range(FF_BLOCKS):
            c0 = cc * FF_CHUNK
            gate = _dot(h2, wgu_ref[cc])
            up = _dot(h2, wgu_ref[FF_BLOCKS + cc])
            act_ref[:, c0:c0 + FF_CHUNK] = gate.astype(BF16)
            act_ref[:, D_FF + c0:D_FF + c0 + FF_CHUNK] = up.astype(BF16)
            a = (gate * _sigmoid(gate) * up).astype(BF16)
            part = _dot(a, wdn_ref[c0:c0 + FF_CHUNK, :])
            y2 = part if y2 is None else y2 + part
        g2 = mod_ref[5:6, :]
        z2 = ALPHA * x1v + g2 * y2
        xhat, rstd = _ln_stats(z2)
        gain = g_ref[...]
        diff = xhat * gain + b_ref[...] - t_ref[...]
        dx2 = diff * (1.0 / D_MODEL)
        dz2 = _ln_bwd(dx2 * gain, xhat, rstd)
        dx1a_ref[...] = ALPHA * dz2
        dy2_ref[...] = (g2 * dz2).astype(BF16)
        acc_ref[0:1, :] += _colsum(diff * diff)
        acc_ref[1:2, :] += _colsum(dx2 * xhat)
        acc_ref[2:3, :] += _colsum(dx2)
        acc_ref[3:4, :] += _colsum(dz2 * y2)

    row = lambda w: pl.BlockSpec((tm, w), lambda i: (i, 0))
    return pl.pallas_call(
        body, name="fwd_ffn",
        out_shape=(jax.ShapeDtypeStruct((s, D_MODEL), BF16), jax.ShapeDtypeStruct((s, 2 * D_FF), BF16),
                   jax.ShapeDtypeStruct((s, D_MODEL), BF16), jax.ShapeDtypeStruct((s, D_MODEL), F32),
                   jax.ShapeDtypeStruct((8, D_MODEL), F32)),
        grid=(s // tm,),
        in_specs=[row(D_MODEL), row(D_MODEL), _const_spec((8, D_MODEL)), _const_spec((1, D_MODEL)),
                  _const_spec((1, D_MODEL)), _const_spec((N_CHIPS, D_MODEL, FF_CHUNK), single=True),
                  _const_spec((D_FF, D_MODEL), single=True)],
        out_specs=(row(D_MODEL), row(2 * D_FF), row(D_MODEL), row(D_MODEL), _const_spec((8, D_MODEL))),
        compiler_params=_params(("arbitrary",)),
    )(x1, target, modr, ln2_g, ln2_b, w_gu, w_dn)


def _bwd_ffn(dy2, act, w_gu, w_dn, tm):
    s = dy2.shape[0]

    def body(dy2_ref, act_ref, wgu_ref, wdn_ref, a_ref, dgu_ref, dh2_ref):
        dy2v = dy2_ref[...]
        dh2 = None
        for cc in range(FF_BLOCKS):
            c0 = cc * FF_CHUNK
            da = _dot_nt(dy2v, wdn_ref[c0:c0 + FF_CHUNK, :])
            gate = act_ref[:, c0:c0 + FF_CHUNK].astype(F32)
            up = act_ref[:, D_FF + c0:D_FF + c0 + FF_CHUNK].astype(F32)
            sg = _sigmoid(gate)
            sl = gate * sg
            a_ref[:, c0:c0 + FF_CHUNK] = (sl * up).astype(BF16)
            dgate = (da * up * (sg * (1.0 + gate * (1.0 - sg)))).astype(BF16)
            dup = (da * sl).astype(BF16)
            dgu_ref[:, c0:c0 + FF_CHUNK] = dgate
            dgu_ref[:, D_FF + c0:D_FF + c0 + FF_CHUNK] = dup
            part = _dot_nt(dgate, wgu_ref[cc]) + _dot_nt(dup, wgu_ref[FF_BLOCKS + cc])
            dh2 = part if dh2 is None else dh2 + part
        dh2_ref[...] = dh2.astype(BF16)

    row = lambda w: pl.BlockSpec((tm, w), lambda i: (i, 0))
    return pl.pallas_call(
        body, name="bwd_ffn",
        out_shape=(jax.ShapeDtypeStruct((s, D_FF), BF16), jax.ShapeDtypeStruct((s, 2 * D_FF), BF16),
                   jax.ShapeDtypeStruct((s, D_MODEL), BF16)),
        grid=(s // tm,),
        in_specs=[row(D_MODEL), row(2 * D_FF), _const_spec((N_CHIPS, D_MODEL, FF_CHUNK), single=True),
                  _const_spec((D_FF, D_MODEL), single=True)],
        out_specs=(row(D_FF), row(2 * D_FF), row(D_MODEL)),
        compiler_params=_params(("parallel",)),
    )(dy2, act, w_gu, w_dn)


def _bwd_mid(dh2, dx1a, x1, x, y, modr, ln1_g, w_out, tm, swap_fulls, swap_kinds):
    s = x.shape[0]
    n_steps = s // tm
    n_g = len(swap_fulls)

    def body(dh2_ref, dx1a_ref, x1_ref, x_ref, y_ref, mod_ref, g_ref, wout_ref, *rest):
        full_refs = rest[:n_g]
        dxa_ref, dy_ref, dmix_ref, acc_ref = rest[n_g:n_g + 4]
        got_refs = rest[n_g + 4:2 * n_g + 4]
        swap = _HalfSwap(full_refs, got_refs, swap_kinds, *rest[2 * n_g + 4:])
        i = pl.program_id(0)

        @pl.when(i == 0)
        def _():
            swap.start()
            acc_ref[...] = jnp.zeros_like(acc_ref)

        dh2 = dh2_ref[...].astype(F32)
        x1v = x1_ref[...].astype(F32)
        yv = y_ref[...].astype(F32)
        g1 = mod_ref[2:3, :]
        dx1 = dx1a_ref[...] + dh2 * (1.0 + mod_ref[4:5, :])
        z1 = ALPHA * x_ref[...] + g1 * yv
        xhat, rstd = _ln_stats(z1)
        dz1 = _ln_bwd(dx1 * g_ref[...], xhat, rstd)
        dxa_ref[...] = (ALPHA * dz1).astype(BF16)
        dy = (g1 * dz1).astype(BF16)
        dy_ref[...] = dy
        dmix_ref[...] = _dot_nt(dy, wout_ref[...]).astype(BF16)
        acc_ref[0:1, :] += _colsum(dh2 * x1v)
        acc_ref[1:2, :] += _colsum(dh2)
        acc_ref[2:3, :] += _colsum(dx1 * xhat)
        acc_ref[3:4, :] += _colsum(dx1)
        acc_ref[4:5, :] += _colsum(dz1 * yv)

        @pl.when(i == n_steps - 1)
        def _():
            swap.wait()

    row = lambda w: pl.BlockSpec((tm, w), lambda i: (i, 0))
    outs = pl.pallas_call(
        body, name="bwd_mid",
        out_shape=[jax.ShapeDtypeStruct((s, D_MODEL), BF16), jax.ShapeDtypeStruct((s, D_MODEL), BF16),
                   jax.ShapeDtypeStruct((s, D_MODEL), BF16), jax.ShapeDtypeStruct((8, D_MODEL), F32)]
        + _HalfSwap.out_shapes(swap_fulls, swap_kinds),
        grid=(n_steps,),
        in_specs=[row(D_MODEL)] * 5 + [_const_spec((8, D_MODEL)), _const_spec((1, D_MODEL)),
                                       _const_spec((D_MODEL, D_MODEL))] + [ANY] * n_g,
        out_specs=[row(D_MODEL), row(D_MODEL), row(D_MODEL), _const_spec((8, D_MODEL))] + [ANY] * n_g,
        scratch_shapes=_HalfSwap.sems(n_g),
        compiler_params=_params(("arbitrary",)),
    )(dh2, dx1a, x1, x, y, modr, ln1_g, w_out, *swap_fulls)
    return outs[:4], outs[4:]


def _fold_kv(t0, t1):
    lane = lax.broadcasted_iota(jnp.int32, t0.shape, 1)
    f0 = t0 + pltpu.roll(t0, HEAD_DIM, 1)
    f1 = t1 + pltpu.roll(t1, HEAD_DIM, 1)
    return jnp.where(lane < HEAD_DIM, f0, f1)


def _bwd_mix(q, kv, gu, gv, dmix, bias, sinks, gln_g, gln_b, wsm, bsx, amat, aog, gog, grad_parts, grad_kinds):
    s = q.shape[0]
    tile = 2 * BLOCK
    n_steps = s // tile
    n_g = len(grad_parts)

    def body(q_ref, kv_ref, kvp_ref, gu_ref, gv_ref, dmix_ref, bias_ref, sinks_ref, glng_ref, glnb_ref, wsm_ref,
             bsx_ref, amat_ref, aog_ref, gog_ref, *rest):
        part_refs = rest[:n_g]
        dq_ref, dkv_ref, dgu_ref, dgv_ref, gbias_ref, dws_ref, dbs_ref, vec_ref, dsink_ref = rest[n_g:n_g + 9]
        rx_refs = rest[n_g + 9:2 * n_g + 9]
        carry, done, send_sems, recv_sems = rest[2 * n_g + 9:]
        n = pl.program_id(0)
        exchange = _ChipExchange(part_refs, rx_refs, grad_kinds, send_sems, recv_sems)

        @pl.when(n == 0)
        def _():
            exchange.start()
            carry[...] = jnp.zeros_like(carry)
            done[...] = jnp.zeros_like(done)
            gbias_ref[...] = jnp.zeros_like(gbias_ref)
            dws_ref[...] = jnp.zeros_like(dws_ref)
            dbs_ref[...] = jnp.zeros_like(dbs_ref)
            vec_ref[...] = jnp.zeros_like(vec_ref)
            dsink_ref[...] = jnp.zeros_like(dsink_ref)

        @pl.when(n == n_steps)
        def _():
            dkv_ref[:BLOCK, :] = done[...].astype(BF16)
            dkv_ref[BLOCK:, :] = carry[...].astype(BF16)
            exchange.wait()

        @pl.when(n < n_steps)
        def _():
            col = lax.broadcasted_iota(jnp.int32, (BLOCK, 2 * BLOCK), 1)
            lane = lax.broadcasted_iota(jnp.int32, (BLOCK, LANES), 1)
            low = lane < HEAD_DIM
            rows = [slice(0, BLOCK), slice(BLOCK, tile)]
            kv_blocks = [kvp_ref[...], kv_ref[rows[0], :], kv_ref[rows[1], :]]
            masks = [(col < BLOCK) & (n == 0), None]
            q_blks = [q_ref[r, :] for r in rows]
            fwd = []
            for b in range(2):
                kk = jnp.concatenate([kv_blocks[b][:, :KV_W], kv_blocks[b + 1][:, :KV_W]], axis=0)
                vv = jnp.concatenate([kv_blocks[b][:, KV_W:], kv_blocks[b + 1][:, KV_W:]], axis=0)
                fwd.append(_attn_block_fwd(q_blks[b], kk, vv, bias_ref, sinks_ref, masks[b]))
                fwd.append(_gmlp_chunk_fwd(gu_ref[rows[b], :], gv_ref[rows[b], :], glng_ref[...], glnb_ref[...],
                                           wsm_ref, bsx_ref[...], amat_ref[...]))
            res = _interleave(*fwd[:2]) + _interleave(*fwd[2:])

            def gating_bwd(b, d_gm, saved):
                u, tu, ta, xhat, rstd, vb, mixedv = saved
                dgu_ref[rows[b], :] = (d_gm * mixedv * _gelu_grad(gu_ref[rows[b], :], tu)).astype(BF16)
                dmx = d_gm * u
                dmxb = dmx.astype(BF16)
                yield
                dvn_cols, dws = [], []
                for pair in range(N_GROUPS // 2):
                    dp_ = dmxb[:, pair * LANES:(pair + 1) * LANES]
                    vp = vb[:, pair * LANES:(pair + 1) * LANES]
                    dvn_cols.append(
                        jnp.where(low, _dot_tn(wsm_ref[2 * pair], dp_), _dot_tn(wsm_ref[2 * pair + 1], dp_)))
                    zero = jnp.zeros_like(dp_)
                    dws.append(_dot_nt(jnp.where(low, dp_, zero), vp))
                    dws.append(_dot_nt(jnp.where(low, zero, dp_), vp))
                dvn = jnp.concatenate(dvn_cols, axis=1)
                yield
                dxh = dvn * glng_ref[...]
                am = amat_ref[...]
                m1 = _split_dot(dxh, am)
                m2 = _split_dot(dxh * xhat, am)
                yield
                da = rstd * (dxh - m1 - xhat * m2)
                dgv_ref[rows[b], :] = (da * _gelu_grad(gv_ref[rows[b], :], ta)).astype(BF16)
                return dmx, dws, _colsum(dvn * xhat), _colsum(dvn)

            def attention_bwd(b, d_attn, probs, kvar, vvar):
                heads = range(N_HEADS)
                sels = [low if h % 2 == 0 else jnp.logical_not(low) for h in heads]
                pair_of = lambda a, h: a[:, (h // 2) * LANES:(h // 2 + 1) * LANES]
                do_hs = [jnp.where(sels[h], pair_of(d_attn, h), 0.0).astype(BF16) for h in heads]
                q_hs = [jnp.where(sels[h], pair_of(q_blks[b], h), jnp.zeros((BLOCK, LANES), BF16)) for h in heads]
                dps = [_dot_nt(do_hs[h], vvar[_head_kv(h)[0]][_head_kv(h)[1]]) for h in heads]
                yield
                deltas = [jnp.sum(probs[h][0] * dps[h], axis=-1, keepdims=True) for h in heads]
                yield
                dss = [probs[h][0] * (dps[h] - deltas[h]) for h in heads]
                dsinks = [-(probs[h][1] * deltas[h]) for h in heads]
                dsbs = [ds.astype(BF16) for ds in dss]
                pbs = [probs[h][0].astype(BF16) for h in heads]
                yield
                dqs = [_dot(dsbs[h], kvar[_head_kv(h)[0]][_head_kv(h)[1]]) for h in heads]
                tks = [_dot_tn(dsbs[h], q_hs[h]) for h in heads]
                tvs = [_dot_tn(pbs[h], do_hs[h]) for h in heads]
                dq_cols = [dqs[2 * i] + dqs[2 * i + 1] for i in range(N_HEADS // 2)]
                dq_ref[rows[b], :] = (jnp.concatenate(dq_cols, axis=1) * Q_SCALE).astype(BF16)
                per_kv = N_HEADS // N_KV
                kv_sum = lambda ts, kvh: sum(ts[kvh * per_kv + 1:(kvh + 1) * per_kv], ts[kvh * per_kv])
                dkk = _fold_kv(kv_sum(tks, 0), kv_sum(tks, 1))
                dvv = _fold_kv(kv_sum(tvs, 0), kv_sum(tvs, 1))
                return jnp.concatenate([dkk, dvv], axis=1), dss, dsinks

            bwd, rms_g = [], []
            for b in range(2):
                attn, probs, kvar, vvar = res[2 * b]
                gm, saved = res[2 * b + 1]
                na_unit, r_a = _rms(attn, 1.0)
                ng_unit, r_g = _rms(gm, 1.0)
                dmix = dmix_ref[rows[b], :].astype(F32)
                dn_a = dmix[:, :ATTN_W]
                dn_g = dmix[:, ATTN_W:]
                rms_g.append((_colsum(dn_a * na_unit), _colsum(dn_g * ng_unit)))
                t_a = dn_a * aog_ref[...]
                d_attn = r_a * t_a - na_unit * (r_a * jnp.mean(t_a * na_unit, axis=-1, keepdims=True))
                t_g = dn_g * gog_ref[...]
                d_gm = r_g * t_g - ng_unit * (r_g * jnp.mean(t_g * ng_unit, axis=-1, keepdims=True))
                bwd.append(attention_bwd(b, d_attn, probs, kvar, vvar))
                bwd.append(gating_bwd(b, d_gm, saved))
            (dkv_a, dss_a, dsk_a), (dmx_a, dws_a, glg_a, glb_a) = _interleave(*bwd[:2])
            (dkv_b, dss_b, dsk_b), (dmx_b, dws_b, glg_b, glb_b) = _interleave(*bwd[2:])

            vec_ref[0:1, :] += rms_g[0][0] + rms_g[1][0]
            vec_ref[1:2, :] += rms_g[0][1] + rms_g[1][1]
            vec_ref[2:3, :] += glg_a + glg_b
            vec_ref[3:4, :] += glb_a + glb_b
            dbs_ref[...] += dmx_a + dmx_b
            for g in range(N_GROUPS):
                dws_ref[g] += dws_a[g] + dws_b[g]
            for h in range(N_HEADS):
                gbias_ref[h] += dss_a[h] + dss_b[h]
                dsink_ref[h] += dsk_a[h] + dsk_b[h]

            dkv_ref[:BLOCK, :] = done[...].astype(BF16)
            dkv_ref[BLOCK:, :] = (carry[...] + dkv_a[:BLOCK]).astype(BF16)
            done[...] = dkv_a[BLOCK:] + dkv_b[:BLOCK]
            carry[...] = dkv_b[BLOCK:]

    last = n_steps - 1
    cur = lambda w: pl.BlockSpec((tile, w), lambda n: (jnp.minimum(n, last), 0))
    late = lambda w: pl.BlockSpec((tile, w), lambda n: (jnp.clip(n - 1, 0, last), 0))
    before = pl.BlockSpec((BLOCK, 2 * KV_W), lambda n: (jnp.clip(2 * n - 1, 0, 2 * last + 1), 0))
    outs = pl.pallas_call(
        body, name="bwd_mix",
        out_shape=[jax.ShapeDtypeStruct((s, ATTN_W), BF16), jax.ShapeDtypeStruct((s, 2 * KV_W), BF16),
                   jax.ShapeDtypeStruct((s, GMLP_W), BF16), jax.ShapeDtypeStruct((s, GMLP_W), BF16),
                   jax.ShapeDtypeStruct((N_HEADS, BLOCK, 2 * BLOCK), F32),
                   jax.ShapeDtypeStruct((N_GROUPS, BLOCK, BLOCK), F32),
                   jax.ShapeDtypeStruct((BLOCK, GMLP_W), F32), jax.ShapeDtypeStruct((8, GMLP_W), F32),
                   jax.ShapeDtypeStruct((N_HEADS, BLOCK, 1), F32)]
        + [jax.ShapeDtypeStruct(_rx_shape(p.shape, k), BF16) for p, k in zip(grad_parts, grad_kinds)],
        grid=(n_steps + 1,),
        in_specs=[cur(ATTN_W), cur(2 * KV_W), before, cur(GMLP_W), cur(GMLP_W), cur(D_MODEL),
                  _const_spec((N_HEADS, BLOCK, 2 * BLOCK)), pl.BlockSpec(memory_space=pltpu.SMEM),
                  _const_spec((1, GMLP_W)), _const_spec((1, GMLP_W)), _const_spec((N_GROUPS, BLOCK, BLOCK)),
                  _const_spec((BLOCK, GMLP_W)), _const_spec((GMLP_W, GMLP_W)), _const_spec((1, ATTN_W)),
                  _const_spec((1, GMLP_W))] + [ANY] * n_g,
        out_specs=[cur(ATTN_W), late(2 * KV_W), cur(GMLP_W), cur(GMLP_W),
                   _const_spec((N_HEADS, BLOCK, 2 * BLOCK)), _const_spec((N_GROUPS, BLOCK, BLOCK)),
                   _const_spec((BLOCK, GMLP_W)), _const_spec((8, GMLP_W)), _const_spec((N_HEADS, BLOCK, 1))]
        + [ANY] * n_g,
        scratch_shapes=[pltpu.VMEM((BLOCK, 2 * KV_W), F32), pltpu.VMEM((BLOCK, 2 * KV_W), F32)]
        + _ChipExchange.sems(n_g),
        compiler_params=_params(("arbitrary",)),
    )(q, kv, kv, gu, gv, dmix, bias, sinks, gln_g, gln_b, wsm, bsx, amat, aog, gog, *grad_parts)
    return outs[:9], outs[9:]


def _mix_finalize(gbias, bucket, dws, dbs, dsink):
    def body(gb_ref, bucket_ref, dws_ref, dbs_ref, dsink_ref, tall_ref):
        bk = bucket_ref[...]
        lane = lax.broadcasted_iota(jnp.int32, (N_BUCKETS, LANES), 1)
        rowi = lax.broadcasted_iota(jnp.int32, (N_BUCKETS, LANES), 0)
        drb = jnp.zeros((N_BUCKETS, LANES), F32)
        dsk = jnp.zeros((8, LANES), F32)
        lane8 = lax.broadcasted_iota(jnp.int32, (8, LANES), 1)
        for h in range(N_HEADS):
            g = gb_ref[h]
            for b in range(N_BUCKETS):
                tot = jnp.sum(_colsum(jnp.where(bk == float(b), g, 0.0)), axis=1, keepdims=True)
                drb = jnp.where((rowi == h) & (lane == b), tot, drb)
            sk = jnp.sum(dsink_ref[h], axis=0, keepdims=True)
            dsk = jnp.where(lane8 == h, sk, dsk)
        tall_ref[TALL_RB:TALL_RB + N_BUCKETS, :] = drb
        tall_ref[TALL_SK:TALL_SK + 8, :] = dsk
        ti = lax.broadcasted_iota(jnp.int32, (BLOCK, BLOCK), 0)
        ui = lax.broadcasted_iota(jnp.int32, (BLOCK, BLOCK), 1)
        for g in range(N_GROUPS):
            tall_ref[g * BLOCK:(g + 1) * BLOCK, :] = jnp.where(ti >= ui, dws_ref[g], 0.0)
        gi = lax.broadcasted_iota(jnp.int32, (GMLP_W, LANES), 0) // GROUP_DIM
        li = lax.broadcasted_iota(jnp.int32, (GMLP_W, LANES), 1)
        ind = jnp.where(gi == li, 1.0, 0.0).astype(BF16)
        d = dbs_ref[...]
        hi = d.astype(BF16)
        r1 = d - hi.astype(F32)
        mid = r1.astype(BF16)
        lo = (r1 - mid.astype(F32)).astype(BF16)
        dbsg = _dot(hi, ind) + _dot(mid, ind) + _dot(lo, ind)
        tall_ref[TALL_BS:TALL_BS + N_GROUPS, :] = dbsg.T[:N_GROUPS, :]

    return pl.pallas_call(
        body, name="mix_finalize", out_shape=jax.ShapeDtypeStruct((TALL_ROWS, LANES), F32), grid=(1,),
        in_specs=[_const_spec((N_HEADS, BLOCK, 2 * BLOCK)), _const_spec((BLOCK, 2 * BLOCK)),
                  _const_spec((N_GROUPS, BLOCK, BLOCK)), _const_spec((BLOCK, GMLP_W)),
                  _const_spec((N_HEADS, BLOCK, 1))],
        out_specs=_const_spec((TALL_ROWS, LANES)),
        compiler_params=_params(("arbitrary",)),
    )(gbias, bucket, dws, dbs, dsink)


def _bwd_in(dq, dkv, dgu, dgv, dxa, x, modr, w_in, tm):
    s = x.shape[0]

    def body(dq_ref, dkv_ref, dgu_ref, dgv_ref, dxa_ref, x_ref, mod_ref, w_ref, gx_ref, acc_ref, db_ref):
        @pl.when(pl.program_id(0) == 0)
        def _():
            acc_ref[...] = jnp.zeros_like(acc_ref)
            db_ref[...] = jnp.zeros_like(db_ref)

        dproj = jnp.concatenate([dq_ref[...], dkv_ref[...], dgu_ref[...], dgv_ref[...]], axis=1)
        dh1 = _dot(dproj, w_ref[...])
        gx_ref[...] = dxa_ref[...].astype(F32) + dh1 * (1.0 + mod_ref[1:2, :])
        acc_ref[0:1, :] += _colsum(dh1 * x_ref[...].astype(F32))
        acc_ref[1:2, :] += _colsum(dh1)
        db_ref[0:1, :] += _colsum(dproj.astype(F32))

    row = lambda w: pl.BlockSpec((tm, w), lambda i: (i, 0))
    return pl.pallas_call(
        body, name="bwd_in",
        out_shape=(jax.ShapeDtypeStruct((s, D_MODEL), F32), jax.ShapeDtypeStruct((8, D_MODEL), F32),
                   jax.ShapeDtypeStruct((8, IN_W), F32)),
        grid=(s // tm,),
        in_specs=[row(ATTN_W), row(2 * KV_W), row(GMLP_W), row(GMLP_W), row(D_MODEL), row(D_MODEL),
                  _const_spec((8, D_MODEL)), _const_spec(w_in.shape)],
        out_specs=(row(D_MODEL), _const_spec((8, D_MODEL)), _const_spec((8, IN_W))),
        compiler_params=_params(("arbitrary",)),
    )(dq, dkv, dgu, dgv, dxa, x, modr, w_in)


def _wgrad(a, bs, tm, tk, name, transposed=False, gather_vs=()):
    k_all, m = a.shape
    n = sum(b.shape[1] for b in bs)
    nk = k_all // tk
    nm = m // tm
    n_b = len(bs)
    n_v = len(gather_vs)

    def body(a_ref, *rest):
        b_refs, v_refs = rest[:n_b], rest[n_b:n_b + n_v]
        o_ref, ob_ref = rest[n_b + n_v:n_b + n_v + 2]
        vg_refs = rest[n_b + n_v + 2:n_b + 2 * n_v + 2]
        i, k = pl.program_id(0), pl.program_id(1)
        if n_v:
            gather = _Gather8(v_refs, vg_refs, *rest[n_b + 2 * n_v + 2:])

            @pl.when((i == 0) & (k == 0))
            def _():
                gather.start()

            @pl.when((i == nm - 1) & (k == 0))
            def _():
                gather.forward()

        @pl.when(k == 0)
        def _():
            o_ref[...] = jnp.zeros_like(o_ref)

        b = b_refs[0][...] if n_b == 1 else jnp.concatenate([r[...] for r in b_refs], axis=1)
        if transposed:
            o_ref[...] += _dot_tn(b, a_ref[...])
        else:
            o_ref[...] += _dot_tn(a_ref[...], b)

        @pl.when(k == nk - 1)
        def _():
            ob_ref[...] = o_ref[...].astype(BF16)

        if n_v:
            @pl.when((i == nm - 1) & (k == nk - 1))
            def _():
                gather.finish()

    if transposed:
        out_spec = pl.BlockSpec((n, tm), lambda i, k: (0, i))
        shape = (n, m)
    else:
        out_spec = pl.BlockSpec((tm, n), lambda i, k: (i, 0))
        shape = (m, n)
    outs = pl.pallas_call(
        body, name=name,
        out_shape=[jax.ShapeDtypeStruct(shape, F32), jax.ShapeDtypeStruct(shape, BF16)] + _gathered8_shapes(gather_vs),
        grid=(nm, nk),
        in_specs=[pl.BlockSpec((tk, tm), lambda i, k: (k, i))]
        + [pl.BlockSpec((tk, b.shape[1]), lambda i, k: (k, 0)) for b in bs] + [ANY] * n_v,
        out_specs=[out_spec, out_spec] + [ANY] * n_v,
        scratch_shapes=_Gather8.sems(n_v) if n_v else [],
        compiler_params=_params(("arbitrary", "arbitrary") if n_v else ("parallel", "arbitrary")),
    )(a, *bs, *gather_vs)
    return outs[0], outs[1], outs[2:]


def _adam_math(w, g, m, v):
    m2 = ADAM_B1 * m + (1.0 - ADAM_B1) * g
    v2 = ADAM_B2 * v + (1.0 - ADAM_B2) * (g * g)
    m_hat = m2 / (1.0 - ADAM_B1 ** ADAM_STEP)
    v_hat = v2 / (1.0 - ADAM_B2 ** ADAM_STEP)
    delta = -ADAM_LR * (m_hat / (jnp.sqrt(v_hat) + ADAM_EPS) + ADAM_WD * w)
    return delta, m2, v2


def _adam_halves(w, mine, got, m, v, tr, name):
    r, cc = w.shape
    h = r // 2
    nt = h // tr

    def body(c_ref, w_ref, mine_ref, got_ref, m_ref, v_ref, g_ref, d_ref, m2_ref, v2_ref):
        g = jnp.where(pl.program_id(0) == c_ref[0], mine_ref[...], got_ref[...])
        g_ref[...] = g
        d, m2, v2 = _adam_math(w_ref[...], g, m_ref[...], v_ref[...])
        d_ref[...] = d
        m2_ref[...] = m2
        v2_ref[...] = v2

    full = pl.BlockSpec((tr, cc), lambda hh, i, c_ref: (hh * nt + i, 0))
    half = pl.BlockSpec((tr, cc), lambda hh, i, c_ref: (i, 0))
    shp = jax.ShapeDtypeStruct((r, cc), F32)
    return pl.pallas_call(
        body, name=name, out_shape=(shp, shp, shp, shp),
        grid_spec=pltpu.PrefetchScalarGridSpec(
            num_scalar_prefetch=1, grid=(2, nt), in_specs=[full, half, half, full, full],
            out_specs=(full, full, full, full)),
        compiler_params=_params(("arbitrary", "arbitrary")),
    )(_core_index_scalar(), w, mine, got, m, v)


def _adam_w_ada(sc_t, dmod_all, w, m, v, tr):
    r, cc = w.shape

    def body(chip_ref, sct_ref, dm_ref, w_ref, m_ref, v_ref, g_ref, d_ref, m2_ref, v2_ref):
        g = sct_ref[:, 0:1] * dm_ref[0:1, :]
        for k in range(1, N_DEV):
            g = g + sct_ref[:, k:k + 1] * dm_ref[k:k + 1, :]
        g_ref[...] = g
        d, m2, v2 = _adam_math(w_ref[...], g, m_ref[...], v_ref[...])
        d_ref[...] = d
        m2_ref[...] = m2
        v2_ref[...] = v2

    spec = pl.BlockSpec((tr, cc), lambda i, chip_ref: (i, 0))
    shp = jax.ShapeDtypeStruct((r, cc), F32)
    return pl.pallas_call(
        body, name="adam_w_ada", out_shape=(shp, shp, shp, shp),
        grid_spec=pltpu.PrefetchScalarGridSpec(
            num_scalar_prefetch=1, grid=(r // tr,),
            in_specs=[pl.BlockSpec((tr, N_DEV), lambda i, chip_ref: (i, 0)),
                      pl.BlockSpec((N_DEV, cc), lambda i, chip_ref: (0, chip_ref[0])), spec, spec, spec],
            out_specs=(spec, spec, spec, spec)),
        compiler_params=_params(("parallel",)),
    )(_chip_index_scalar(), sc_t, dmod_all, w, m, v)


def _pack_wide(acc_i, acc_m, acc_f, db_in, vec):
    arrs = [acc_i, acc_m, acc_f, db_in, vec]
    i_, m_, f_, b_, v_ = range(5)
    src = {"b_in": (b_, 0), "ln1_g": (m_, 2), "ln1_b": (m_, 3), "ln2_g": (f_, 1), "ln2_b": (f_, 2),
           "gmlp_ln_g": (v_, 2), "gmlp_ln_b": (v_, 3), "attn_out_g": (v_, 0), "gmlp_out_g": (v_, 1), "loss": (f_, 0)}
    dmod = [(i_, 1), (i_, 0), (m_, 4), (m_, 1), (m_, 0), (f_, 3)]

    def body(*refs):
        ins, wide_ref = refs[:5], refs[5]
        wide_ref[...] = jnp.zeros_like(wide_ref)
        for k, (a, row) in enumerate(dmod):
            wide_ref[0:1, k * D_MODEL:(k + 1) * D_MODEL] = ins[a][row:row + 1, :]
        for name, (a, row) in src.items():
            r, off, n = WIDE_LAYOUT[name]
            wide_ref[r:r + 1, off:off + n] = ins[a][row:row + 1, :]

    return pl.pallas_call(
        body, name="pack_wide", out_shape=jax.ShapeDtypeStruct((8, WIDE_W), F32), grid=(1,),
        in_specs=[_const_spec(a.shape) for a in arrs], out_specs=_const_spec((8, WIDE_W)),
        compiler_params=_params(("arbitrary",)),
    )(*arrs)


def _adam_small(gw, gt, wide_wmv, w_s, b_s, rel_bias, sinks, after):
    names = list(WIDE_PARAMS)
    tall = [("gmlp_w_s", w_s), ("gmlp_b_s", b_s), ("rel_bias", rel_bias), ("attn_sinks", sinks)]
    ins = [gw, gt]
    for n in names:
        ins += list(wide_wmv[n])
    for _, t in tall:
        ins += list(t)
    n_in = len(ins)

    def body(*refs):
        gw_ref, gt_ref = refs[0], refs[1]
        wmv = refs[2:n_in]
        dmod_ref, loss_ref, loss1_ref = refs[n_in + 1:n_in + 4]
        outs = refs[n_in + 4:]

        def tall_sum(r0, nr):
            g = gt_ref[r0:r0 + nr, :]
            for d in range(1, N_DEV):
                g = g + gt_ref[d * TALL_ROWS + r0:d * TALL_ROWS + r0 + nr, :]
            return g

        def emit(k, g, w_ref, m_ref, v_ref):
            d, m2, v2 = _adam_math(w_ref[...], g, m_ref[...], v_ref[...])
            outs[4 * k][...] = g
            outs[4 * k + 1][...] = d
            outs[4 * k + 2][...] = m2
            outs[4 * k + 3][...] = v2

        gsum = gw_ref[0:8, :]
        for d in range(1, N_DEV):
            gsum = gsum + gw_ref[8 * d:8 * d + 8, :]
        for d in range(N_DEV):
            dmod_ref[d:d + 1, :] = gw_ref[8 * d:8 * d + 1, :]
        for k, n in enumerate(names):
            r, off, sz = WIDE_LAYOUT[n]
            emit(k, gsum[r:r + 1, off:off + sz], *wmv[3 * k:3 * k + 3])
        r, off, sz = WIDE_LAYOUT["loss"]
        tot = jnp.sum(gsum[r:r + 1, off:off + sz], axis=1, keepdims=True)
        loss_ref[...] = jnp.broadcast_to(tot * (0.5 / D_MODEL), loss_ref.shape)
        loss1_ref[...] = tot * (0.5 / D_MODEL)

        k0 = len(names)
        ws_refs = wmv[3 * k0:3 * k0 + 3]
        for g in range(N_GROUPS):
            rows = slice(g * BLOCK, (g + 1) * BLOCK)
            gg = tall_sum(g * BLOCK, BLOCK)
            d, m2, v2 = _adam_math(ws_refs[0][rows, :], gg, ws_refs[1][rows, :], ws_refs[2][rows, :])
            outs[4 * k0][rows, :] = gg
            outs[4 * k0 + 1][rows, :] = d
            outs[4 * k0 + 2][rows, :] = m2
            outs[4 * k0 + 3][rows, :] = v2
        emit(k0 + 1, tall_sum(TALL_BS, N_GROUPS), *wmv[3 * (k0 + 1):3 * (k0 + 1) + 3])
        emit(k0 + 2, tall_sum(TALL_RB, N_HEADS)[:, :N_BUCKETS], *wmv[3 * (k0 + 2):3 * (k0 + 2) + 3])
        emit(k0 + 3, tall_sum(TALL_SK, 8)[0:1, :N_HEADS], *wmv[3 * (k0 + 3):3 * (k0 + 3) + 3])

    out_shapes = [jax.ShapeDtypeStruct((N_DEV, WIDE_W), F32), jax.ShapeDtypeStruct((8, LANES), F32),
                  jax.ShapeDtypeStruct((1, 1), F32)]
    for n in names:
        out_shapes += [jax.ShapeDtypeStruct(wide_wmv[n][0].shape, F32)] * 4
    for _, t in tall:
        out_shapes += [jax.ShapeDtypeStruct(t[0].shape, F32)] * 4
    res = pl.pallas_call(
        body, name="adam_small", out_shape=out_shapes, grid=(1,),
        in_specs=[_const_spec(a.shape) for a in ins] + [ANY], out_specs=[_const_spec(o.shape) for o in out_shapes],
        compiler_params=_params(("arbitrary",)),
    )(*ins, after)
    out = {}
    for k, n in enumerate(names + [t[0] for t in tall]):
        out[n] = tuple(res[3 + 4 * k:7 + 4 * k])
    return res[0], res[1], res[2], out


def kernel(x, c, rel_bias, w_ada, b_ada, w_in, b_in, attn_sinks, gmlp_ln_g, gmlp_ln_b, gmlp_w_s, gmlp_b_s, attn_out_g, gmlp_out_g, w_out, ln1_g, ln1_b, w_gate_up, w_down, ln2_g, ln2_b, loss_target, m_rel_bias, m_w_ada, m_b_ada, m_w_in, m_b_in, m_attn_sinks, m_gmlp_ln_g, m_gmlp_ln_b, m_gmlp_w_s, m_gmlp_b_s, m_attn_out_g, m_gmlp_out_g, m_w_out, m_ln1_g, m_ln1_b, m_w_gate_up, m_w_down, m_ln2_g, m_ln2_b, v_rel_bias, v_w_ada, v_b_ada, v_w_in, v_b_in, v_attn_sinks, v_gmlp_ln_g, v_gmlp_ln_b, v_gmlp_w_s, v_gmlp_b_s, v_attn_out_g, v_gmlp_out_g, v_w_out, v_ln1_g, v_ln1_b, v_w_gate_up, v_w_down, v_ln2_g, v_ln2_b):
    ix, iy, _ = _my_pos()
    chip = 2 * ix + iy
    s = x.shape[1]
    xs = x[0]
    tgt = loss_target[0]
    tm_big = min(512, s)
    tm_ffn = min(FFN_SUB, s)

    sc_all, modr, (w_in_g, w_out_g) = _prologue(
        jnp.pad(c, ((0, 7), (0, 0))), w_ada[0], b_ada, [_with_own_block(w_in[0].T, chip), _with_own_block(w_out[0], chip)])
    w_in_f = w_in_g.reshape(IN_W, D_MODEL)

    bucket = _bucket_table()
    bias, wsm = _prep_tables(bucket, rel_bias.T, gmlp_w_s[0])
    bsx = jnp.repeat(gmlp_b_s[0].T, GROUP_DIM, axis=1)
    amat = _group_mean_matrix()
    sinks = attn_sinks[0]

    (h1, q, kv, gu, gv, xb), (w_dn_g,) = _fwd_in(xs, modr, w_in_f, b_in, tm_big, [_with_own_block(w_down[0], chip)],
                                                 ["blk"])
    w_out_f = w_out_g.reshape(D_MODEL, D_MODEL)
    (x1, x1b, y, mixed), (w_gu_f,) = _fwd_mix(
        q, kv, gu, gv, xs, modr, bias, sinks, gmlp_ln_g, gmlp_ln_b, wsm, bsx, amat, attn_out_g, gmlp_out_g, w_out_f,
        ln1_g, ln1_b, tm_big, [_with_own_block(w_gate_up[0], chip)], ["blk"])
    assert w_gate_up.shape[2] == FF_CHUNK
    w_dn_f = w_dn_g.reshape(D_FF, D_MODEL)
    h2, act, dy2, dx1a, acc_f = _fwd_ffn(x1, tgt, modr, ln2_g, ln2_b, w_gu_f, w_dn_f, min(2 * FFN_SUB, s))

    a_act, dgu_ff, dh2 = _bwd_ffn(dy2, act, w_gu_f, w_dn_f, min(FFN_SUB, s))
    g_dn, g_dn_b, _ = _wgrad(a_act, [dy2], D_FF // 2, min(1024, s), "wgrad_down")
    g_gu, g_gu_b, _ = _wgrad(h2, [dgu_ff], 512, min(512, s), "wgrad_gate_up")
    blk3 = lambda a, rows: a.reshape(N_CHIPS, rows, a.shape[1])
    (dxa, dy, dmix, acc_m), (got_dn, got_gu) = _bwd_mid(
        dh2, dx1a, x1b, xs, y, modr, ln1_g, w_out_f, tm_big, [blk3(g_dn_b, D_FF // N_CHIPS), g_gu_b], ["blk", "cols"])
    g_out, g_out_b, _ = _wgrad(mixed, [dy], 512, min(2048, s), "wgrad_out")
    (got_out,) = _swap_halves([blk3(g_out_b, D_MODEL // N_CHIPS)], ["blk"], "rs_swap_out")
    kinds_a = ["blk", "cols", "blk"]
    fulls_a = [blk3(g_dn, D_FF // N_CHIPS), g_gu, blk3(g_out, D_MODEL // N_CHIPS)]
    gots_a = [got_dn, got_gu, got_out]
    parts_a = [_add_halves(f, g, k, "rs_add_a%d" % i) for i, (f, g, k) in enumerate(zip(fulls_a, gots_a, kinds_a))]
    (dq, dkv, dgu, dgv, gbias, dws, dbs, vec, dsink), rxs_a = _bwd_mix(
        q, kv, gu, gv, dmix, bias, sinks, gmlp_ln_g, gmlp_ln_b, wsm, bsx, amat, attn_out_g, gmlp_out_g,
        [p[1] for p in parts_a], kinds_a)
    tall_g = _mix_finalize(gbias, bucket, dws, dbs, dsink)
    grad_x, acc_i, db_in = _bwd_in(dq, dkv, dgu, dgv, dxa, xb, modr, w_in_f, tm_big)

    wide_g = _pack_wide(acc_i, acc_m, acc_f, db_in, vec)
    full_in, full_in_b, (gw, gt) = _wgrad(h1, [dq, dkv, dgu, dgv], 512, min(1024, s), "wgrad_in", transposed=True,
                                          gather_vs=[wide_g, tall_g])
    (got_in,) = _swap_halves([blk3(full_in_b, IN_W // N_CHIPS)], ["blk"], "rs_swap_in")
    part_in = _add_halves(blk3(full_in, IN_W // N_CHIPS), got_in, "blk", "rs_add_in")
    in_send, in_recv, in_part, in_rx, token = _exchange_start(part_in[1], "rs_chips_in_start")
    wide_wmv ={"b_ada": (b_ada, m_b_ada, v_b_ada), "b_in": (b_in, m_b_in, v_b_in),
                "ln1_g": (ln1_g, m_ln1_g, v_ln1_g), "ln1_b": (ln1_b, m_ln1_b, v_ln1_b),
                "ln2_g": (ln2_g, m_ln2_g, v_ln2_g), "ln2_b": (ln2_b, m_ln2_b, v_ln2_b),
                "gmlp_ln_g": (gmlp_ln_g, m_gmlp_ln_g, v_gmlp_ln_g), "gmlp_ln_b": (gmlp_ln_b, m_gmlp_ln_b, v_gmlp_ln_b),
                "attn_out_g": (attn_out_g, m_attn_out_g, v_attn_out_g),
                "gmlp_out_g": (gmlp_out_g, m_gmlp_out_g, v_gmlp_out_g)}
    rows2 = lambda a: a.reshape(-1, a.shape[-1])
    dmod_all, loss_t, loss1, small = _adam_small(
        gw, gt, wide_wmv, tuple(rows2(a) for a in (gmlp_w_s, m_gmlp_w_s, v_gmlp_w_s)),
        tuple(rows2(a) for a in (gmlp_b_s, m_gmlp_b_s, v_gmlp_b_s)), (rel_bias.T, m_rel_bias.T, v_rel_bias.T),
        (attn_sinks, m_attn_sinks, v_attn_sinks), token)
    small["rel_bias"] = tuple(a.T for a in small["rel_bias"])
    loss = loss1.reshape(())

    g_ada, d_ada, m_ada, v_ada = _adam_w_ada(sc_all.T, dmod_all, w_ada[0], m_w_ada[0], v_w_ada[0], 256)

    sums = [(parts_a[0][0], rxs_a[0], 176), (parts_a[1][0], rxs_a[1], 256), (parts_a[2][0], rxs_a[2], 128)]
    mine = [_sum_chips(p, rx, tr, "rs_sum_%d" % i, loss_t) for i, (p, rx, tr) in enumerate(sums)]
    got = _share_halves(mine, "rs_share")
    gs_dn, d_dn, m_dn, v_dn = _adam_halves(w_down[0], mine[0], got[0], m_w_down[0], v_w_down[0], 176, "adam_w_down")
    gs_gu, d_gu, m_gu, v_gu = _adam_halves(w_gate_up[0], mine[1], got[1], m_w_gate_up[0], v_w_gate_up[0], 256,
                                           "adam_w_gate_up")
    gs_out, d_out, m_out, v_out = _adam_halves(w_out[0], mine[2], got[2], m_w_out[0], v_w_out[0], 128, "adam_w_out")

    rx_in = _exchange_wait(in_send, in_recv, in_part, in_rx, d_gu, "rs_chips_in_wait")
    mine_in = _sum_chips(part_in[0], rx_in, 112, "rs_sum_in", rx_in)
    (got_in_half,) = _share_halves([mine_in], "rs_share_in")
    in_t = _adam_halves(w_in[0].T, mine_in, got_in_half, m_w_in[0].T, v_w_in[0].T, 112, "adam_w_in")
    gs_in, d_in, m_in, v_in = (a.T for a in in_t)

    big = {"w_ada": (g_ada, d_ada, m_ada, v_ada), "w_in": (gs_in, d_in, m_in, v_in), "w_out": (gs_out, d_out, m_out, v_out),
           "w_gate_up": (gs_gu, d_gu, m_gu, v_gu), "w_down": (gs_dn, d_dn, m_dn, v_dn)}
    order = ["rel_bias", "w_ada", "b_ada", "w_in", "b_in", "attn_sinks", "gmlp_ln_g", "gmlp_ln_b", "gmlp_w_s", "gmlp_b_s",
             "attn_out_g", "gmlp_out_g", "w_out", "ln1_g", "ln1_b", "w_gate_up", "w_down", "ln2_g", "ln2_b"]
    shapes = {"gmlp_w_s": gmlp_w_s.shape, "gmlp_b_s": gmlp_b_s.shape}
    outs = [loss, grad_x[None]]
    for k in range(4):
        for name in order:
            if name in big:
                outs.append(big[name][k][None])
            elif name in shapes:
                outs.append(small[name][k].reshape(shapes[name]))
            else:
                outs.append(small[name][k])
    return tuple(outs)
```

```python
import math

import numpy as np
import jax
import jax.numpy as jnp
from jax import lax
from jax.experimental import pallas as pl
from jax.experimental.pallas import tpu as pltpu

F32 = jnp.float32
BF16 = jnp.bfloat16
MESH = pl.DeviceIdType.MESH

D_MODEL = 1024
N_HEADS = 8
N_KV = 2
HEAD_DIM = 64
ATTN_W = N_HEADS * HEAD_DIM
KV_W = N_KV * HEAD_DIM
N_GROUPS = 8
GROUP_DIM = 64
GMLP_W = N_GROUPS * GROUP_DIM
IN_W = ATTN_W + 2 * KV_W + 2 * GMLP_W
BLOCK = 128
N_BUCKETS = 32
MAX_DISTANCE = 128
D_FF = 2816
ALPHA = 2.0 ** 0.25
LN_EPS = 1e-5
NEG_INF = -1e30
ADAM_LR, ADAM_B1, ADAM_B2, ADAM_EPS, ADAM_WD, ADAM_STEP = 0.001, 0.9, 0.999, 1e-8, 0.01, 10
N_CHIPS = 4
N_DEV = 8
LANES = 128
V7X_VMEM_LIMIT = 56 * 2 ** 20
GELU_C = math.sqrt(2.0 / math.pi)
Q_SCALE = HEAD_DIM ** -0.5
ANY = pl.BlockSpec(memory_space=pl.ANY)

TALL_BS = N_GROUPS * BLOCK
TALL_RB = TALL_BS + 8
TALL_SK = TALL_RB + N_BUCKETS
TALL_ROWS = TALL_SK + 8
WIDE_W = 6 * D_MODEL
WIDE_LAYOUT = {
    "b_ada": (0, 0, 6 * D_MODEL),
    "b_in": (1, 0, IN_W), "ln1_g": (1, IN_W, D_MODEL), "ln1_b": (1, IN_W + D_MODEL, D_MODEL),
    "ln2_g": (1, IN_W + 2 * D_MODEL, D_MODEL), "ln2_b": (1, IN_W + 3 * D_MODEL, D_MODEL),
    "gmlp_ln_g": (2, 0, GMLP_W), "gmlp_ln_b": (2, GMLP_W, GMLP_W), "attn_out_g": (2, 2 * GMLP_W, ATTN_W),
    "gmlp_out_g": (2, 2 * GMLP_W + ATTN_W, GMLP_W), "loss": (2, 3 * GMLP_W + ATTN_W, D_MODEL)}
WIDE_PARAMS = tuple(n for n in WIDE_LAYOUT if n != "loss")


def _params(sem=None):
    return pltpu.CompilerParams(dimension_semantics=sem, vmem_limit_bytes=V7X_VMEM_LIMIT)


def _const_spec(shape, single=False):
    nd = len(shape)
    if single:
        return pl.BlockSpec(shape, lambda *_: (0,) * nd, pipeline_mode=pl.Buffered(1))
    return pl.BlockSpec(shape, lambda *_: (0,) * nd)


def _dot(a, b):
    return jnp.dot(a, b, preferred_element_type=F32)


def _dot_nt(a, b):
    return lax.dot_general(a, b, (((1,), (1,)), ((), ())), preferred_element_type=F32)


def _dot_tn(a, b):
    return lax.dot_general(a, b, (((0,), (0,)), ((), ())), preferred_element_type=F32)


def _gelu(x):
    t = jnp.tanh(GELU_C * (x + 0.044715 * x * x * x))
    return 0.5 * x * (1.0 + t), t


def _gelu_grad(x, t):
    return 0.5 * (1.0 + t) + 0.5 * x * (1.0 - t * t) * GELU_C * (1.0 + 3.0 * 0.044715 * x * x)


def _split_dot(x, a):
    hi = x.astype(BF16)
    lo = (x - hi.astype(F32)).astype(BF16)
    return _dot(hi, a) + _dot(lo, a)


def _group_mean_matrix():
    g = np.arange(GMLP_W) // GROUP_DIM
    return jnp.asarray((g[:, None] == g[None, :]).astype(np.float32) / GROUP_DIM, dtype=BF16)


def _ln_stats(z):
    mu = jnp.mean(z, axis=-1, keepdims=True)
    d = z - mu
    var = jnp.mean(d * d, axis=-1, keepdims=True)
    rstd = lax.rsqrt(var + LN_EPS)
    return d * rstd, rstd


def _ln_bwd(dxhat, xhat, rstd):
    m1 = jnp.mean(dxhat, axis=-1, keepdims=True)
    m2 = jnp.mean(dxhat * xhat, axis=-1, keepdims=True)
    return rstd * (dxhat - m1 - xhat * m2)


def _colsum(x):
    return jnp.sum(x, axis=0, keepdims=True)


def _my_pos():
    return lax.axis_index("x"), lax.axis_index("y"), lax.axis_index("c")


def _other_chips(x, y):
    return [(1 - x, y), (x, 1 - y), (1 - x, 1 - y)]


def _chip_index_scalar():
    ix, iy, _ = _my_pos()
    return jnp.reshape(2 * ix + iy, (1,)).astype(jnp.int32)


def _core_index_scalar():
    return jnp.reshape(lax.axis_index("c"), (1,)).astype(jnp.int32)


class _Gather8:
    def __init__(self, x_refs, out_refs, send_sems, recv_sems, local_sems):
        self.x_refs, self.out_refs = x_refs, out_refs
        self.send_sems, self.recv_sems, self.local_sems = send_sems, recv_sems, local_sems
        self.x, self.y, self.c = _my_pos()
        self.me, self.sibling = (self.x, self.y, self.c), (self.x, self.y, 1 - self.c)
        self.chips = _other_chips(self.x, self.y)

    def _rows(self, a, px, py, pc):
        m_per = self.x_refs[a].shape[0]
        return self.out_refs[a].at[pl.ds((4 * px + 2 * py + pc) * m_per, m_per), :]

    def _copy(self, a, k, block, to, src=None):
        return pltpu.make_async_remote_copy(
            src_ref=self._rows(a, *block) if src is None else src, dst_ref=self._rows(a, *block),
            send_sem=self.send_sems.at[7 * a + k], recv_sem=self.recv_sems.at[7 * a + k], device_id=to,
            device_id_type=MESH)

    def _local(self, a):
        return pltpu.make_async_copy(self.x_refs[a], self._rows(a, *self.me), self.local_sems.at[a])

    def start(self):
        for a in range(len(self.x_refs)):
            self._local(a).start()
            self._copy(a, 0, self.me, self.sibling, src=self.x_refs[a]).start()
            for j, chip in enumerate(self.chips):
                self._copy(a, 1 + j, self.me, (*chip, self.c), src=self.x_refs[a]).start()

    def forward(self):
        for a in range(len(self.x_refs)):
            for j, chip in enumerate(self.chips):
                self._copy(a, 1 + j, (*chip, self.c), self.me).wait_recv()
                self._copy(a, 4 + j, (*chip, self.c), self.sibling).start()

    def finish(self):
        for a in range(len(self.x_refs)):
            self._copy(a, 0, self.sibling, self.me).wait_recv()
            for j, chip in enumerate(self.chips):
                self._copy(a, 4 + j, (*chip, 1 - self.c), self.me).wait_recv()
        for a in range(len(self.x_refs)):
            for k in range(7):
                self._copy(a, k, self.me, self.me).wait_send()
            self._local(a).wait()

    @staticmethod
    def sems(n_v):
        return [pltpu.SemaphoreType.DMA((7 * n_v,)), pltpu.SemaphoreType.DMA((7 * n_v,)),
                pltpu.SemaphoreType.DMA((n_v,))]


def _gathered8_shapes(vs):
    return [jax.ShapeDtypeStruct((N_DEV * v.shape[0], v.shape[1]), v.dtype) for v in vs]


VMEM_WHOLE = pl.BlockSpec(memory_space=pltpu.VMEM)


def _prologue(c_pad, w_ada_s, b_ada, shards):
    n = w_ada_s.shape[1]
    n_w = len(shards)
    assert N_CHIPS * n == 6 * D_MODEL and n % LANES == 0

    def body(c_ref, w_ref, b_ref, *rest):
        sc_ref, modc_ref, modg_ref, modr_ref = rest[n_w:n_w + 4]
        gathered_refs = rest[n_w + 4:2 * n_w + 4]
        call_ref, w_vmem = rest[2 * n_w + 4:2 * n_w + 6]
        sems = rest[2 * n_w + 6:]
        ix, iy, ic = _my_pos()
        chip = 2 * ix + iy
        weights = _WeightGather(gathered_refs, ["blk"] * n_w, sems[0], sems[1])
        gather_c = _Gather8([c_ref], [call_ref], sems[2], sems[3], sems[4])
        gather_mod = _Gather8([modc_ref], [modg_ref], sems[5], sems[6], sems[7])
        load_w = pltpu.make_async_copy(w_ref, w_vmem, sems[8])
        weights.start()
        gather_c.start()
        load_w.start()
        gather_c.forward()
        gather_c.finish()
        cv = call_ref[...]
        sc = cv * _sigmoid(cv)
        a_hi = sc.astype(BF16)
        a_lo = (sc - a_hi.astype(F32)).astype(BF16)
        load_w.wait()
        w = w_vmem[...]
        w_hi = w.astype(BF16)
        w_lo = (w - w_hi.astype(F32)).astype(BF16)
        b = b_ref[:, 0:n]
        for k in range(1, N_CHIPS):
            b = jnp.where(chip == k, b_ref[:, k * n:(k + 1) * n], b)
        mod = _dot(a_hi, w_hi) + _dot(a_hi, w_lo) + _dot(a_lo, w_hi) + b
        for d in range(N_DEV):
            sc_ref[d:d + 1, :] = sc[8 * d:8 * d + 1, :]
            modc_ref[d:d + 1, :] = mod[8 * d:8 * d + 1, :]
        gather_mod.start()
        weights.forward()
        gather_mod.forward()
        gather_mod.finish()
        dev = 2 * chip + ic
        mine = jnp.concatenate([modg_ref[pl.ds(2 * 8 * k + dev, 1), :] for k in range(N_CHIPS)], axis=1)
        modr_ref[...] = jnp.zeros_like(modr_ref)
        for r in range(6):
            modr_ref[r:r + 1, :] = mine[:, r * D_MODEL:(r + 1) * D_MODEL]
        weights.forward_diagonal()
        weights.finish()

    outs = pl.pallas_call(
        body, name="prologue",
        out_shape=[jax.ShapeDtypeStruct((N_DEV, D_MODEL), F32), jax.ShapeDtypeStruct((N_DEV, n), F32),
                   jax.ShapeDtypeStruct((N_DEV * N_DEV, n), F32), jax.ShapeDtypeStruct((8, D_MODEL), F32)]
        + [jax.ShapeDtypeStruct(sh.shape, BF16) for sh in shards],
        in_specs=[VMEM_WHOLE, ANY, VMEM_WHOLE] + [ANY] * n_w,
        out_specs=[VMEM_WHOLE, VMEM_WHOLE, VMEM_WHOLE, VMEM_WHOLE] + [ANY] * n_w,
        input_output_aliases={3 + a: 4 + a for a in range(n_w)},
        scratch_shapes=[pltpu.VMEM((N_DEV * 8, D_MODEL), F32), pltpu.VMEM(w_ada_s.shape, F32)]
        + _WeightGather.sems(n_w) + _Gather8.sems(1) + _Gather8.sems(1) + [pltpu.SemaphoreType.DMA],
        compiler_params=pltpu.CompilerParams(vmem_limit_bytes=V7X_VMEM_LIMIT),
    )(c_pad, w_ada_s, b_ada, *shards)
    return outs[0], outs[3], outs[4:]


def _with_own_block(shard, chip):
    empty = lax.empty((N_CHIPS,) + shard.shape, BF16)
    return lax.dynamic_update_slice(empty, shard.astype(BF16)[None], (chip, 0, 0))


class _WeightGather:
    N_SEM = 8

    def __init__(self, gathered, kinds, send_sems, recv_sems):
        self.gathered, self.kinds = gathered, kinds
        self.send_sems, self.recv_sems = send_sems, recv_sems
        self.x, self.y, self.c = _my_pos()
        self.me, self.sibling = (self.x, self.y, self.c), (self.x, self.y, 1 - self.c)
        self.nbr = ((1 - self.x, self.y), (self.x, 1 - self.y))
        self.diag = 2 * (1 - self.x) + (1 - self.y)

    def _dst(self, a, chip, pc, quarter=None):
        r, cc = self._shard_shape(a)
        h = r // 2
        row0, rows = pc * h, h
        if quarter is not None:
            row0, rows = pc * h + quarter * (h // 2), h // 2
        g = self.gathered[a]
        if self.kinds[a] == "blk":
            return g.at[chip, pl.ds(row0, rows), :]
        return g.at[pl.ds(row0, rows), pl.ds(chip * cc, cc)]

    def _copy(self, a, k, region, to):
        return pltpu.make_async_remote_copy(
            src_ref=region, dst_ref=region, send_sem=self.send_sems.at[a * self.N_SEM + k],
            recv_sem=self.recv_sems.at[a * self.N_SEM + k], device_id=to, device_id_type=MESH)

    def _arrays(self):
        return range(len(self.gathered))

    def _shard_shape(self, a):
        shape = self.gathered[a].shape
        return shape[1:] if self.kinds[a] == "blk" else (shape[0], shape[1] // N_CHIPS)

    def start(self):
        my_chip = 2 * self.x + self.y
        for a in self._arrays():
            for j, chip in enumerate(self.nbr):
                self._copy(a, j, self._dst(a, my_chip, self.c), (*chip, self.c)).start()

    def forward(self):
        for a in self._arrays():
            for j, chip in enumerate(self.nbr):
                cj = 2 * chip[0] + chip[1]
                half = self._dst(a, cj, self.c)
                self._copy(a, j, half, self.me).wait_recv()
                self._copy(a, 2 + j, half, self.sibling).start()
                other = self.nbr[1 - j]
                self._copy(a, 4 + j, self._dst(a, cj, self.c, quarter=j), (*other, self.c)).start()

    def forward_diagonal(self):
        for a in self._arrays():
            for j in range(2):
                quarter = self._dst(a, self.diag, self.c, quarter=j)
                self._copy(a, 4 + j, quarter, self.me).wait_recv()
                self._copy(a, 6 + j, quarter, self.sibling).start()

    def finish(self):
        for a in self._arrays():
            for j, chip in enumerate(self.nbr):
                self._copy(a, 2 + j, self._dst(a, 2 * chip[0] + chip[1], 1 - self.c), self.me).wait_recv()
                self._copy(a, 6 + j, self._dst(a, self.diag, 1 - self.c, quarter=j), self.me).wait_recv()
        for a in self._arrays():
            half = self._dst(a, self.diag, self.c)
            quarter = self._dst(a, self.diag, self.c, quarter=0)
            for k in range(self.N_SEM):
                self._copy(a, k, half if k < 4 else quarter, self.me).wait_send()

    @classmethod
    def sems(cls, n_arr):
        return [pltpu.SemaphoreType.DMA((n_arr * cls.N_SEM,)), pltpu.SemaphoreType.DMA((n_arr * cls.N_SEM,))]


def _half_of_full(ref, kind, pc):
    if kind == "blk":
        h = ref.shape[1] // 2
        return ref.at[:, pl.ds(pc * h, h), :]
    h = ref.shape[0] // 2
    return ref.at[pl.ds(pc * h, h), :]


def _half_shape(shape, kind):
    return (shape[0], shape[1] // 2, shape[2]) if kind == "blk" else (shape[0] // 2, shape[1])


class _HalfSwap:
    def __init__(self, ins, outs, kinds, send_sems, recv_sems):
        self.ins, self.outs, self.kinds = ins, outs, kinds
        self.send_sems, self.recv_sems = send_sems, recv_sems
        self.x, self.y, self.c = _my_pos()

    def _copies(self):
        for a in range(len(self.ins)):
            yield pltpu.make_async_remote_copy(
                src_ref=_half_of_full(self.ins[a], self.kinds[a], 1 - self.c), dst_ref=self.outs[a],
                send_sem=self.send_sems.at[a], recv_sem=self.recv_sems.at[a],
                device_id=(self.x, self.y, 1 - self.c), device_id_type=MESH)

    def start(self):
        for cp in self._copies():
            cp.start()

    def wait(self):
        for cp in self._copies():
            cp.wait()

    @staticmethod
    def sems(n_arr):
        return [pltpu.SemaphoreType.DMA((n_arr,)), pltpu.SemaphoreType.DMA((n_arr,))]

    @staticmethod
    def out_shapes(fulls, kinds):
        return [jax.ShapeDtypeStruct(_half_shape(a.shape, k), a.dtype) for a, k in zip(fulls, kinds)]


def _swap_halves(fulls_bf16, kinds, name):
    n_arr = len(fulls_bf16)

    def body(*refs):
        swap = _HalfSwap(refs[:n_arr], refs[n_arr:2 * n_arr], kinds, *refs[2 * n_arr:])
        swap.start()
        swap.wait()

    return pl.pallas_call(
        body, name=name, out_shape=_HalfSwap.out_shapes(fulls_bf16, kinds),
        in_specs=[ANY] * n_arr, out_specs=[ANY] * n_arr, scratch_shapes=_HalfSwap.sems(n_arr),
    )(*fulls_bf16)


def _add_halves(full, got, kind, name):
    hs = _half_shape(full.shape, kind)

    def body(pos_ref, a_ref, b_ref, o_ref, ob_ref):
        p = a_ref[...] + b_ref[...].astype(F32)
        ob_ref[...] = p.astype(BF16)

        @pl.when(pl.program_id(0) == pos_ref[1])
        def _():
            o_ref[...] = p.reshape(o_ref.shape)

    if kind == "blk":
        nb, h, cc = hs
        own = pl.BlockSpec((1, h, cc), lambda b, pos_ref: (b, pos_ref[0], 0))
        other = pl.BlockSpec((1, h, cc), lambda b, pos_ref: (b, 0, 0))
    else:
        h, cc = hs[0], hs[1] // N_CHIPS
        own = pl.BlockSpec((h, cc), lambda b, pos_ref: (pos_ref[0], b))
        other = pl.BlockSpec((h, cc), lambda b, pos_ref: (0, b))
    pos = jnp.concatenate([_core_index_scalar(), _chip_index_scalar()])
    return pl.pallas_call(
        body, name=name, out_shape=(jax.ShapeDtypeStruct((h, cc), F32), jax.ShapeDtypeStruct(hs, BF16)),
        grid_spec=pltpu.PrefetchScalarGridSpec(
            num_scalar_prefetch=1, grid=(N_CHIPS,), in_specs=[own, other],
            out_specs=(pl.BlockSpec((h, cc), lambda b, pos_ref: (0, 0)), other)),
        compiler_params=_params(("arbitrary",)),
    )(pos, full, got)


def _rx_shape(part_shape, kind):
    if kind == "blk":
        return (3, part_shape[1], part_shape[2])
    return (3, part_shape[0], part_shape[1] // N_CHIPS)


class _ChipExchange:
    def __init__(self, parts, rxs, kinds, send_sems, recv_sems):
        self.parts, self.rxs, self.kinds = parts, rxs, kinds
        self.send_sems, self.recv_sems = send_sems, recv_sems
        self.x, self.y, self.c = _my_pos()
        self.chips = _other_chips(self.x, self.y)

    def _copies(self):
        for a in range(len(self.parts)):
            for j, chip in enumerate(self.chips):
                cj = 2 * chip[0] + chip[1]
                if self.kinds[a] == "blk":
                    src = self.parts[a].at[cj]
                else:
                    cc = self.parts[a].shape[1] // N_CHIPS
                    src = self.parts[a].at[:, pl.ds(cj * cc, cc)]
                yield pltpu.make_async_remote_copy(
                    src_ref=src, dst_ref=self.rxs[a].at[j], send_sem=self.send_sems.at[a * 3 + j],
                    recv_sem=self.recv_sems.at[a * 3 + j], device_id=(*chip, self.c), device_id_type=MESH)

    def start(self):
        for cp in self._copies():
            cp.start()

    def wait(self):
        for cp in self._copies():
            cp.wait_recv()
        for cp in self._copies():
            cp.wait_send()

    @staticmethod
    def sems(n_arr):
        return [pltpu.SemaphoreType.DMA((n_arr * 3,)), pltpu.SemaphoreType.DMA((n_arr * 3,))]


HBM_SPEC = pl.BlockSpec(memory_space=pltpu.HBM)
SEM_SPEC = pl.BlockSpec(memory_space=pltpu.SEMAPHORE)
DATAFLOW = pltpu.SideEffectType.DATAFLOW_SIDE_EFFECTING


def _exchange_start(part, name):
    rx_shape = _rx_shape(part.shape, "blk")

    def body(part_ref, rx_ref, send_sems, recv_sems, part_thru, rx_thru, token):
        _ChipExchange([part_ref], [rx_ref], ["blk"], send_sems, recv_sems).start()
        token[...] = jnp.zeros_like(token)

    return pl.pallas_call(
        body, name=name,
        out_shape=(pltpu.SemaphoreType.DMA((3,)), pltpu.SemaphoreType.DMA((3,)), pltpu.HBM(part.shape, part.dtype),
                   pltpu.HBM(rx_shape, BF16), jax.ShapeDtypeStruct((8, LANES), F32)),
        in_specs=(HBM_SPEC, HBM_SPEC), out_specs=(SEM_SPEC, SEM_SPEC, HBM_SPEC, HBM_SPEC, VMEM_WHOLE),
        input_output_aliases={0: 2, 1: 3}, compiler_params=pltpu.CompilerParams(has_side_effects=DATAFLOW),
    )(pltpu.with_memory_space_constraint(part, pltpu.HBM),
      pltpu.with_memory_space_constraint(lax.empty(rx_shape, BF16), pltpu.HBM))


def _exchange_wait(send_sems, recv_sems, part_thru, rx_thru, after, name):
    def body(part_ref, rx_ref, send_sems, recv_sems, after_ref, part_dead, rx_out):
        _ChipExchange([part_ref], [rx_ref], ["blk"], send_sems, recv_sems).wait()

    return pl.pallas_call(
        body, name=name,
        out_shape=(pltpu.HBM(part_thru.shape, part_thru.dtype), pltpu.HBM(rx_thru.shape, rx_thru.dtype)),
        in_specs=(HBM_SPEC, HBM_SPEC, SEM_SPEC, SEM_SPEC, ANY), out_specs=(HBM_SPEC, HBM_SPEC),
        input_output_aliases={0: 0, 1: 1}, compiler_params=pltpu.CompilerParams(has_side_effects=DATAFLOW),
    )(part_thru, rx_thru, send_sems, recv_sems, after)[1]


def _sum_chips(part, rx, tr, name, after):
    _, h, cc = rx.shape
    flips = (2, 1, 3)

    def body(chip_ref, p_ref, rx_ref, after_ref, o_ref):
        own = p_ref[...]
        for mc in range(N_CHIPS):
            @pl.when(chip_ref[0] == mc)
            def _():
                terms = sorted([(mc, None)] + [(mc ^ f, j) for j, f in enumerate(flips)])
                acc = None
                for _, j in terms:
                    t = own if j is None else rx_ref[j].astype(F32)
                    acc = t if acc is None else acc + t
                o_ref[...] = acc

    return pl.pallas_call(
        body, name=name, out_shape=jax.ShapeDtypeStruct((h, cc), F32),
        grid_spec=pltpu.PrefetchScalarGridSpec(
            num_scalar_prefetch=1, grid=(h // tr,),
            in_specs=[pl.BlockSpec((tr, cc), lambda i, chip_ref: (i, 0)),
                      pl.BlockSpec((3, tr, cc), lambda i, chip_ref: (0, i, 0)), ANY],
            out_specs=pl.BlockSpec((tr, cc), lambda i, chip_ref: (i, 0))),
        compiler_params=_params(("arbitrary",)),
    )(_chip_index_scalar(), part, rx, after)


def _share_halves(halves, name):
    n_arr = len(halves)

    def body(*refs):
        ins, outs = refs[:n_arr], refs[n_arr:2 * n_arr]
        send_sems, recv_sems = refs[2 * n_arr:]
        x, y, c = _my_pos()
        cps = []
        for a in range(n_arr):
            cp = pltpu.make_async_remote_copy(
                src_ref=ins[a], dst_ref=outs[a], send_sem=send_sems.at[a], recv_sem=recv_sems.at[a],
                device_id=(x, y, 1 - c), device_id_type=MESH)
            cp.start()
            cps.append(cp)
        for cp in cps:
            cp.wait()

    return pl.pallas_call(
        body, name=name, out_shape=[jax.ShapeDtypeStruct(h.shape, h.dtype) for h in halves],
        in_specs=[ANY] * n_arr, out_specs=[ANY] * n_arr,
        scratch_shapes=[pltpu.SemaphoreType.DMA((n_arr,)), pltpu.SemaphoreType.DMA((n_arr,))],
    )(*halves)


def _bucket_table():
    qi = jnp.arange(BLOCK)[:, None]
    si = jnp.arange(2 * BLOCK)[None, :]
    dist = qi + BLOCK - si
    max_exact = N_BUCKETS // 2
    n = jnp.maximum(dist, 0)
    nf = jnp.maximum(n, max_exact).astype(F32)
    large = max_exact + (jnp.log(nf / max_exact) / math.log(MAX_DISTANCE / max_exact)
                         * (N_BUCKETS - max_exact)).astype(jnp.int32)
    large = jnp.minimum(large, N_BUCKETS - 1)
    return jnp.where(n < max_exact, n, large).astype(F32)


def _prep_tables(bucket, rel_bias_t, w_s):
    def body(bucket_ref, rb_ref, ws_ref, bias_ref, wsm_ref):
        qi = lax.broadcasted_iota(jnp.int32, (BLOCK, 2 * BLOCK), 0)
        si = lax.broadcasted_iota(jnp.int32, (BLOCK, 2 * BLOCK), 1)
        dist = qi + BLOCK - si
        in_window = (dist >= 0) & (dist < BLOCK)
        bk = bucket_ref[...]
        for h in range(N_HEADS):
            acc = jnp.zeros((BLOCK, 2 * BLOCK), F32)
            for b in range(N_BUCKETS):
                acc = jnp.where(bk == float(b), rb_ref[h, b], acc)
            bias_ref[h] = jnp.where(in_window, acc, NEG_INF)
        ti = lax.broadcasted_iota(jnp.int32, (BLOCK, BLOCK), 0)
        ui = lax.broadcasted_iota(jnp.int32, (BLOCK, BLOCK), 1)
        for g in range(N_GROUPS):
            wsm_ref[g] = jnp.where(ti >= ui, ws_ref[g], 0.0).astype(BF16)

    return pl.pallas_call(
        body, name="prep_tables",
        out_shape=(jax.ShapeDtypeStruct((N_HEADS, BLOCK, 2 * BLOCK), F32),
                   jax.ShapeDtypeStruct((N_GROUPS, BLOCK, BLOCK), BF16)),
        grid=(1,),
        in_specs=[_const_spec((BLOCK, 2 * BLOCK)), pl.BlockSpec(memory_space=pltpu.SMEM),
                  _const_spec((N_GROUPS, BLOCK, BLOCK))],
        out_specs=(_const_spec((N_HEADS, BLOCK, 2 * BLOCK)), _const_spec((N_GROUPS, BLOCK, BLOCK))),
        compiler_params=_params(("arbitrary",)),
    )(bucket, rel_bias_t, w_s)


def _fwd_in(x, modr, w_in, b_in, tm, shards, kinds):
    s = x.shape[0]
    n_steps = s // tm
    fwd_step, diag_step = (8 * n_steps) // 16, (13 * n_steps) // 16
    n_w = len(shards)

    def body(x_ref, mod_ref, w_ref, b_ref, *rest):
        h1_ref, q_ref, kv_ref, gu_ref, gv_ref, xb_ref = rest[n_w:n_w + 6]
        gathered_refs = rest[n_w + 6:2 * n_w + 6]
        send_sems, recv_sems = rest[2 * n_w + 6:]
        i = pl.program_id(0)
        gather = _WeightGather(gathered_refs, kinds, send_sems, recv_sems)

        @pl.when(i == 0)
        def _():
            gather.start()

        xv = x_ref[...]
        xb_ref[...] = xv.astype(BF16)
        h1 = (xv * (1.0 + mod_ref[1:2, :]) + mod_ref[0:1, :]).astype(BF16)
        h1_ref[...] = h1
        proj = _dot_nt(h1, w_ref[...]) + b_ref[...]
        q_ref[...] = (proj[:, :ATTN_W] * Q_SCALE).astype(BF16)
        kv_ref[...] = proj[:, ATTN_W:ATTN_W + 2 * KV_W].astype(BF16)
        gu_ref[...] = proj[:, ATTN_W + 2 * KV_W:ATTN_W + 2 * KV_W + GMLP_W]
        gv_ref[...] = proj[:, ATTN_W + 2 * KV_W + GMLP_W:]

        @pl.when(i == fwd_step)
        def _():
            gather.forward()

        @pl.when(i == diag_step)
        def _():
            gather.forward_diagonal()

        @pl.when(i == n_steps - 1)
        def _():
            gather.finish()

    row = lambda w: pl.BlockSpec((tm, w), lambda i: (i, 0))
    outs = pl.pallas_call(
        body, name="fwd_in",
        out_shape=[jax.ShapeDtypeStruct((s, D_MODEL), BF16), jax.ShapeDtypeStruct((s, ATTN_W), BF16),
                   jax.ShapeDtypeStruct((s, 2 * KV_W), BF16), jax.ShapeDtypeStruct((s, GMLP_W), F32),
                   jax.ShapeDtypeStruct((s, GMLP_W), F32), jax.ShapeDtypeStruct((s, D_MODEL), BF16)]
        + [jax.ShapeDtypeStruct(sh.shape, BF16) for sh in shards],
        grid=(n_steps,),
        in_specs=[row(D_MODEL), _const_spec((8, D_MODEL)), _const_spec(w_in.shape), _const_spec((1, IN_W))]
        + [ANY] * n_w,
        out_specs=[row(D_MODEL), row(ATTN_W), row(2 * KV_W), row(GMLP_W), row(GMLP_W), row(D_MODEL)] + [ANY] * n_w,
        input_output_aliases={4 + a: 6 + a for a in range(n_w)},
        scratch_shapes=_WeightGather.sems(n_w),
        compiler_params=_params(("arbitrary",)),
    )(x, modr, w_in, b_in, *shards)
    return outs[:6], outs[6:]


def _kv_variants(kk):
    kf = kk.astype(F32)
    lane = lax.broadcasted_iota(jnp.int32, kf.shape, 1)
    low = lane < HEAD_DIM
    k0_lo = jnp.where(low, kf, 0.0)
    k1_hi = jnp.where(low, 0.0, kf)
    k0_hi = pltpu.roll(k0_lo, HEAD_DIM, 1)
    k1_lo = pltpu.roll(k1_hi, HEAD_DIM, 1)
    return ((k0_lo.astype(BF16), k0_hi.astype(BF16)), (k1_lo.astype(BF16), k1_hi.astype(BF16)))


def _head_kv(h):
    return h // (N_HEADS // N_KV), h % 2


MIX_GROUP = 2


def _interleave(*gens):
    results = [None] * len(gens)
    active = list(enumerate(gens))
    while active:
        still = []
        for i, g in active:
            try:
                next(g)
                still.append((i, g))
            except StopIteration as done:
                results[i] = done.value
        active = still
    return results


def _attn_block_fwd(q_blk, kk, vv, bias_ref, sinks_ref, first_mask):
    kvar = _kv_variants(kk)
    vvar = _kv_variants(vv)
    heads = range(N_HEADS)
    q_pairs = [q_blk[:, (h // 2) * LANES:(h // 2 + 1) * LANES] for h in heads]
    logits = [_dot_nt(q_pairs[h], kvar[_head_kv(h)[0]][_head_kv(h)[1]]) + bias_ref[h] for h in heads]
    if first_mask is not None:
        logits = [jnp.where(first_mask, NEG_INF, lg) for lg in logits]
    yield
    ms = [jnp.maximum(jnp.max(logits[h], axis=-1, keepdims=True), sinks_ref[h]) for h in heads]
    yield
    es = [jnp.exp(logits[h] - ms[h]) for h in heads]
    ess = [jnp.exp(sinks_ref[h] - ms[h]) for h in heads]
    yield
    invs = [1.0 / (jnp.sum(es[h], axis=-1, keepdims=True) + ess[h]) for h in heads]
    probs = [(es[h] * invs[h], ess[h] * invs[h]) for h in heads]
    yield
    outs = [_dot(probs[h][0].astype(BF16), vvar[_head_kv(h)[0]][_head_kv(h)[1]]) for h in heads]
    pairs = [outs[2 * i] + outs[2 * i + 1] for i in range(N_HEADS // 2)]
    return jnp.concatenate(pairs, axis=1), probs, kvar, vvar


def _gmlp_chunk_fwd(gu, gv, ln_g, ln_b, wsm_ref, bsx, amat):
    u, tu = _gelu(gu)
    a, ta = _gelu(gv)
    yield
    mean = _split_dot(a, amat)
    d = a - mean
    yield
    var = _split_dot(d * d, amat)
    yield
    rstd = lax.rsqrt(var + LN_EPS)
    xhat = d * rstd
    vb = (xhat * ln_g + ln_b).astype(BF16)
    yield
    lane = lax.broadcasted_iota(jnp.int32, (BLOCK, LANES), 1)
    low = lane < GROUP_DIM
    cols = []
    for pair in range(N_GROUPS // 2):
        vp = vb[:, pair * LANES:(pair + 1) * LANES]
        cols.append(jnp.where(low, _dot(wsm_ref[2 * pair], vp), _dot(wsm_ref[2 * pair + 1], vp)))
    mixedv = jnp.concatenate(cols, axis=1) + bsx
    return u * mixedv, (u, tu, ta, xhat, rstd, vb, mixedv)


def _rms(a, g):
    r = lax.rsqrt(jnp.mean(a * a, axis=-1, keepdims=True) + LN_EPS)
    return a * r * g, r


def _fwd_mix(q, kv, gu, gv, x, modr, bias, sinks, gln_g, gln_b, wsm, bsx, amat, aog, gog, w_out, ln1_g, ln1_b, tm,
             ffn_shards, ffn_kinds):
    s = x.shape[0]
    nb = tm // BLOCK
    n_steps = s // tm
    fwd_step, diag_step = (7 * n_steps) // 16, (12 * n_steps) // 16
    n_w = len(ffn_shards)

    def body(q_ref, kv_ref, kvp_ref, gu_ref, gv_ref, x_ref, mod_ref, bias_ref, sinks_ref, glng_ref, glnb_ref, wsm_ref,
             bsx_ref, amat_ref, aog_ref, gog_ref, wout_ref, ln1g_ref, ln1b_ref, *rest):
        x1_ref, x1b_ref, y_ref, mixed_ref = rest[n_w:n_w + 4]
        gathered_refs = rest[n_w + 4:2 * n_w + 4]
        mix_scr, send_sems, recv_sems = rest[2 * n_w + 4:]
        i = pl.program_id(0)
        gather = _WeightGather(gathered_refs, ffn_kinds, send_sems, recv_sems)

        @pl.when(i == 0)
        def _():
            gather.start()

        col = lax.broadcasted_iota(jnp.int32, (BLOCK, 2 * BLOCK), 1)
        for b0 in range(0, nb, MIX_GROUP):
            gens = []
            for b in range(b0, min(b0 + MIX_GROUP, nb)):
                r0 = b * BLOCK
                if b == 0:
                    kvprev = kvp_ref[...]
                    first_mask = (col < BLOCK) & (i == 0)
                else:
                    kvprev = kv_ref[r0 - BLOCK:r0, :]
                    first_mask = None
                kvcur = kv_ref[r0:r0 + BLOCK, :]
                kk = jnp.concatenate([kvprev[:, :KV_W], kvcur[:, :KV_W]], axis=0)
                vv = jnp.concatenate([kvprev[:, KV_W:], kvcur[:, KV_W:]], axis=0)
                gens.append(_attn_block_fwd(q_ref[r0:r0 + BLOCK, :], kk, vv, bias_ref, sinks_ref, first_mask))
                gens.append(_gmlp_chunk_fwd(gu_ref[r0:r0 + BLOCK, :], gv_ref[r0:r0 + BLOCK, :], glng_ref[...],
                                            glnb_ref[...], wsm_ref, bsx_ref[...], amat_ref[...]))
            res = _interleave(*gens)
            for k, b in enumerate(range(b0, min(b0 + MIX_GROUP, nb))):
                r0 = b * BLOCK
                na, _ = _rms(res[2 * k][0], aog_ref[...])
                ng, _ = _rms(res[2 * k + 1][0], gog_ref[...])
                mix_scr[r0:r0 + BLOCK, :ATTN_W] = na.astype(BF16)
                mix_scr[r0:r0 + BLOCK, ATTN_W:] = ng.astype(BF16)
        mixed = mix_scr[...]
        mixed_ref[...] = mixed
        y = _dot(mixed, wout_ref[...])
        y_ref[...] = y.astype(BF16)
        z1 = ALPHA * x_ref[...] + mod_ref[2:3, :] * y
        xhat, _ = _ln_stats(z1)
        x1 = xhat * ln1g_ref[...] + ln1b_ref[...]
        x1_ref[...] = x1
        x1b_ref[...] = x1.astype(BF16)

        @pl.when(i == fwd_step)
        def _():
            gather.forward()

        @pl.when(i == diag_step)
        def _():
            gather.forward_diagonal()

        @pl.when(i == n_steps - 1)
        def _():
            gather.finish()

    row = lambda w: pl.BlockSpec((tm, w), lambda i: (i, 0))
    prev = pl.BlockSpec((BLOCK, 2 * KV_W), lambda i: (jnp.maximum(i * nb - 1, 0), 0))
    outs = pl.pallas_call(
        body, name="fwd_mix",
        out_shape=[jax.ShapeDtypeStruct((s, D_MODEL), F32)] + [jax.ShapeDtypeStruct((s, D_MODEL), BF16)] * 3
        + [jax.ShapeDtypeStruct(sh.shape, BF16) for sh in ffn_shards],
        grid=(n_steps,),
        in_specs=[row(ATTN_W), row(2 * KV_W), prev, row(GMLP_W), row(GMLP_W), row(D_MODEL), _const_spec((8, D_MODEL)),
                  _const_spec((N_HEADS, BLOCK, 2 * BLOCK)), pl.BlockSpec(memory_space=pltpu.SMEM),
                  _const_spec((1, GMLP_W)), _const_spec((1, GMLP_W)), _const_spec((N_GROUPS, BLOCK, BLOCK)),
                  _const_spec((BLOCK, GMLP_W)), _const_spec((GMLP_W, GMLP_W)), _const_spec((1, ATTN_W)),
                  _const_spec((1, GMLP_W)), _const_spec((D_MODEL, D_MODEL)), _const_spec((1, D_MODEL)),
                  _const_spec((1, D_MODEL))] + [ANY] * n_w,
        out_specs=[row(D_MODEL)] * 4 + [ANY] * n_w,
        input_output_aliases={19 + a: 4 + a for a in range(n_w)},
        scratch_shapes=[pltpu.VMEM((tm, D_MODEL), BF16)] + _WeightGather.sems(n_w),
        compiler_params=_params(("arbitrary",)),
    )(q, kv, kv, gu, gv, x, modr, bias, sinks, gln_g, gln_b, wsm, bsx, amat, aog, gog, w_out, ln1_g, ln1_b, *ffn_shards)
    return outs[:4], outs[4:]


FF_BLOCKS = N_CHIPS // 2
FF_CHUNK = D_FF // FF_BLOCKS
FFN_SUB = 256


def _sigmoid(x):
    return 1.0 / (1.0 + jnp.exp(-x))


def _fwd_ffn(x1, target, modr, ln2_g, ln2_b, w_gu, w_dn, tm):
    s = x1.shape[0]

    def body(x1_ref, t_ref, mod_ref, g_ref, b_ref, wgu_ref, wdn_ref, h2_ref, act_ref, dy2_ref, dx1a_ref, acc_ref):
        @pl.when(pl.program_id(0) == 0)
        def _():
            acc_ref[...] = jnp.zeros_like(acc_ref)

        x1v = x1_ref[...]
        h2 = (x1v * (1.0 + mod_ref[4:5, :]) + mod_ref[3:4, :]).astype(BF16)
        h2_ref[...] = h2
        y2 = None
        for cc in range(FF_BLOCKS):
            c0 = cc * FF_CHUNK
            gate = _dot(h2, wgu_ref[cc])
            up = _dot(h2, wgu_ref[FF_BLOCKS + cc])
            act_ref[:, c0:c0 + FF_CHUNK] = gate.astype(BF16)
            act_ref[:, D_FF + c0:D_FF + c0 + FF_CHUNK] = up.astype(BF16)
            a = (gate * _sigmoid(gate) * up).astype(BF16)
            part = _dot(a, wdn_ref[c0:c0 + FF_CHUNK, :])
            y2 = part if y2 is None else y2 + part
        g2 = mod_ref[5:6, :]
        z2 = ALPHA * x1v + g2 * y2
        xhat, rstd = _ln_stats(z2)
        gain = g_ref[...]
        diff = xhat * gain + b_ref[...] - t_ref[...]
        dx2 = diff * (1.0 / D_MODEL)
        dz2 = _ln_bwd(dx2 * gain, xhat, rstd)
        dx1a_ref[...] = ALPHA * dz2
        dy2_ref[...] = (g2 * dz2).astype(BF16)
        acc_ref[0:1, :] += _colsum(diff * diff)
        acc_ref[1:2, :] += _colsum(dx2 * xhat)
        acc_ref[2:3, :] += _colsum(dx2)
        acc_ref[3:4, :] += _colsum(dz2 * y2)

    row = lambda w: pl.BlockSpec((tm, w), lambda i: (i, 0))
    return pl.pallas_call(
        body, name="fwd_ffn",
        out_shape=(jax.ShapeDtypeStruct((s, D_MODEL), BF16), jax.ShapeDtypeStruct((s, 2 * D_FF), BF16),
                   jax.ShapeDtypeStruct((s, D_MODEL), BF16), jax.ShapeDtypeStruct((s, D_MODEL), F32),
                   jax.ShapeDtypeStruct((8, D_MODEL), F32)),
        grid=(s // tm,),
        in_specs=[row(D_MODEL), row(D_MODEL), _const_spec((8, D_MODEL)), _const_spec((1, D_MODEL)),
                  _const_spec((1, D_MODEL)), _const_spec((N_CHIPS, D_MODEL, FF_CHUNK), single=True),
                  _const_spec((D_FF, D_MODEL), single=True)],
        out_specs=(row(D_MODEL), row(2 * D_FF), row(D_MODEL), row(D_MODEL), _const_spec((8, D_MODEL))),
        compiler_params=_params(("arbitrary",)),
    )(x1, target, modr, ln2_g, ln2_b, w_gu, w_dn)


def _bwd_ffn(dy2, act, w_gu, w_dn, tm):
    s = dy2.shape[0]

    def body(dy2_ref, act_ref, wgu_ref, wdn_ref, a_ref, dgu_ref, dh2_ref):
        dy2v = dy2_ref[...]
        dh2 = None
        for cc in range(FF_BLOCKS):
            c0 = cc * FF_CHUNK
            da = _dot_nt(dy2v, wdn_ref[c0:c0 + FF_CHUNK, :])
            gate = act_ref[:, c0:c0 + FF_CHUNK].astype(F32)
            up = act_ref[:, D_FF + c0:D_FF + c0 + FF_CHUNK].astype(F32)
            sg = _sigmoid(gate)
            sl = gate * sg
            a_ref[:, c0:c0 + FF_CHUNK] = (sl * up).astype(BF16)
            dgate = (da * up * (sg * (1.0 + gate * (1.0 - sg)))).astype(BF16)
            dup = (da * sl).astype(BF16)
            dgu_ref[:, c0:c0 + FF_CHUNK] = dgate
            dgu_ref[:, D_FF + c0:D_FF + c0 + FF_CHUNK] = dup
            part = _dot_nt(dgate, wgu_ref[cc]) + _dot_nt(dup, wgu_ref[FF_BLOCKS + cc])
            dh2 = part if dh2 is None else dh2 + part
        dh2_ref[...] = dh2.astype(BF16)

    row = lambda w: pl.BlockSpec((tm, w), lambda i: (i, 0))
    return pl.pallas_call(
        body, name="bwd_ffn",
        out_shape=(jax.ShapeDtypeStruct((s, D_FF), BF16), jax.ShapeDtypeStruct((s, 2 * D_FF), BF16),
                   jax.ShapeDtypeStruct((s, D_MODEL), BF16)),
        grid=(s // tm,),
        in_specs=[row(D_MODEL), row(2 * D_FF), _const_spec((N_CHIPS, D_MODEL, FF_CHUNK), single=True),
                  _const_spec((D_FF, D_MODEL), single=True)],
        out_specs=(row(D_FF), row(2 * D_FF), row(D_MODEL)),
        compiler_params=_params(("parallel",)),
    )(dy2, act, w_gu, w_dn)


def _bwd_mid(dh2, dx1a, x1, x, y, modr, ln1_g, w_out, tm, swap_fulls, swap_kinds):
    s = x.shape[0]
    n_steps = s // tm
    n_g = len(swap_fulls)

    def body(dh2_ref, dx1a_ref, x1_ref, x_ref, y_ref, mod_ref, g_ref, wout_ref, *rest):
        full_refs = rest[:n_g]
        dxa_ref, dy_ref, dmix_ref, acc_ref = rest[n_g:n_g + 4]
        got_refs = rest[n_g + 4:2 * n_g + 4]
        swap = _HalfSwap(full_refs, got_refs, swap_kinds, *rest[2 * n_g + 4:])
        i = pl.program_id(0)

        @pl.when(i == 0)
        def _():
            swap.start()
            acc_ref[...] = jnp.zeros_like(acc_ref)

        dh2 = dh2_ref[...].astype(F32)
        x1v = x1_ref[...].astype(F32)
        yv = y_ref[...].astype(F32)
        g1 = mod_ref[2:3, :]
        dx1 = dx1a_ref[...] + dh2 * (1.0 + mod_ref[4:5, :])
        z1 = ALPHA * x_ref[...] + g1 * yv
        xhat, rstd = _ln_stats(z1)
        dz1 = _ln_bwd(dx1 * g_ref[...], xhat, rstd)
        dxa_ref[...] = (ALPHA * dz1).astype(BF16)
        dy = (g1 * dz1).astype(BF16)
        dy_ref[...] = dy
        dmix_ref[...] = _dot_nt(dy, wout_ref[...]).astype(BF16)
        acc_ref[0:1, :] += _colsum(dh2 * x1v)
        acc_ref[1:2, :] += _colsum(dh2)
        acc_ref[2:3, :] += _colsum(dx1 * xhat)
        acc_ref[3:4, :] += _colsum(dx1)
        acc_ref[4:5, :] += _colsum(dz1 * yv)

        @pl.when(i == n_steps - 1)
        def _():
            swap.wait()

    row = lambda w: pl.BlockSpec((tm, w), lambda i: (i, 0))
    outs = pl.pallas_call(
        body, name="bwd_mid",
        out_shape=[jax.ShapeDtypeStruct((s, D_MODEL), BF16), jax.ShapeDtypeStruct((s, D_MODEL), BF16),
                   jax.ShapeDtypeStruct((s, D_MODEL), BF16), jax.ShapeDtypeStruct((8, D_MODEL), F32)]
        + _HalfSwap.out_shapes(swap_fulls, swap_kinds),
        grid=(n_steps,),
        in_specs=[row(D_MODEL)] * 5 + [_const_spec((8, D_MODEL)), _const_spec((1, D_MODEL)),
                                       _const_spec((D_MODEL, D_MODEL))] + [ANY] * n_g,
        out_specs=[row(D_MODEL), row(D_MODEL), row(D_MODEL), _const_spec((8, D_MODEL))] + [ANY] * n_g,
        scratch_shapes=_HalfSwap.sems(n_g),
        compiler_params=_params(("arbitrary",)),
    )(dh2, dx1a, x1, x, y, modr, ln1_g, w_out, *swap_fulls)
    return outs[:4], outs[4:]


def _fold_kv(t0, t1):
    lane = lax.broadcasted_iota(jnp.int32, t0.shape, 1)
    f0 = t0 + pltpu.roll(t0, HEAD_DIM, 1)
    f1 = t1 + pltpu.roll(t1, HEAD_DIM, 1)
    return jnp.where(lane < HEAD_DIM, f0, f1)


def _bwd_mix(q, kv, gu, gv, dmix, bias, sinks, gln_g, gln_b, wsm, bsx, amat, aog, gog, grad_parts, grad_kinds):
    s = q.shape[0]
    tile = 2 * BLOCK
    n_steps = s // tile
    n_g = len(grad_parts)

    def body(q_ref, kv_ref, kvp_ref, gu_ref, gv_ref, dmix_ref, bias_ref, sinks_ref, glng_ref, glnb_ref, wsm_ref,
             bsx_ref, amat_ref, aog_ref, gog_ref, *rest):
        part_refs = rest[:n_g]
        dq_ref, dkv_ref, dgu_ref, dgv_ref, gbias_ref, dws_ref, dbs_ref, vec_ref, dsink_ref = rest[n_g:n_g + 9]
        rx_refs = rest[n_g + 9:2 * n_g + 9]
        carry, done, send_sems, recv_sems = rest[2 * n_g + 9:]
        n = pl.program_id(0)
        exchange = _ChipExchange(part_refs, rx_refs, grad_kinds, send_sems, recv_sems)

        @pl.when(n == 0)
        def _():
            exchange.start()
            carry[...] = jnp.zeros_like(carry)
            done[...] = jnp.zeros_like(done)
            gbias_ref[...] = jnp.zeros_like(gbias_ref)
            dws_ref[...] = jnp.zeros_like(dws_ref)
            dbs_ref[...] = jnp.zeros_like(dbs_ref)
            vec_ref[...] = jnp.zeros_like(vec_ref)
            dsink_ref[...] = jnp.zeros_like(dsink_ref)

        @pl.when(n == n_steps)
        def _():
            dkv_ref[:BLOCK, :] = done[...].astype(BF16)
            dkv_ref[BLOCK:, :] = carry[...].astype(BF16)
            exchange.wait()

        @pl.when(n < n_steps)
        def _():
            col = lax.broadcasted_iota(jnp.int32, (BLOCK, 2 * BLOCK), 1)
            lane = lax.broadcasted_iota(jnp.int32, (BLOCK, LANES), 1)
            low = lane < HEAD_DIM
            rows = [slice(0, BLOCK), slice(BLOCK, tile)]
            kv_blocks = [kvp_ref[...], kv_ref[rows[0], :], kv_ref[rows[1], :]]
            masks = [(col < BLOCK) & (n == 0), None]
            q_blks = [q_ref[r, :] for r in rows]
            fwd = []
            for b in range(2):
                kk = jnp.concatenate([kv_blocks[b][:, :KV_W], kv_blocks[b + 1][:, :KV_W]], axis=0)
                vv = jnp.concatenate([kv_blocks[b][:, KV_W:], kv_blocks[b + 1][:, KV_W:]], axis=0)
                fwd.append(_attn_block_fwd(q_blks[b], kk, vv, bias_ref, sinks_ref, masks[b]))
                fwd.append(_gmlp_chunk_fwd(gu_ref[rows[b], :], gv_ref[rows[b], :], glng_ref[...], glnb_ref[...],
                                           wsm_ref, bsx_ref[...], amat_ref[...]))
            res = _interleave(*fwd[:2]) + _interleave(*fwd[2:])

            def gating_bwd(b, d_gm, saved):
                u, tu, ta, xhat, rstd, vb, mixedv = saved
                dgu_ref[rows[b], :] = (d_gm * mixedv * _gelu_grad(gu_ref[rows[b], :], tu)).astype(BF16)
                dmx = d_gm * u
                dmxb = dmx.astype(BF16)
                yield
                dvn_cols, dws = [], []
                for pair in range(N_GROUPS // 2):
                    dp_ = dmxb[:, pair * LANES:(pair + 1) * LANES]
                    vp = vb[:, pair * LANES:(pair + 1) * LANES]
                    dvn_cols.append(
                        jnp.where(low, _dot_tn(wsm_ref[2 * pair], dp_), _dot_tn(wsm_ref[2 * pair + 1], dp_)))
                    zero = jnp.zeros_like(dp_)
                    dws.append(_dot_nt(jnp.where(low, dp_, zero), vp))
                    dws.append(_dot_nt(jnp.where(low, zero, dp_), vp))
                dvn = jnp.concatenate(dvn_cols, axis=1)
                yield
                dxh = dvn * glng_ref[...]
                am = amat_ref[...]
                m1 = _split_dot(dxh, am)
                m2 = _split_dot(dxh * xhat, am)
                yield
                da = rstd * (dxh - m1 - xhat * m2)
                dgv_ref[rows[b], :] = (da * _gelu_grad(gv_ref[rows[b], :], ta)).astype(BF16)
                return dmx, dws, _colsum(dvn * xhat), _colsum(dvn)

            def attention_bwd(b, d_attn, probs, kvar, vvar):
                heads = range(N_HEADS)
                sels = [low if h % 2 == 0 else jnp.logical_not(low) for h in heads]
                pair_of = lambda a, h: a[:, (h // 2) * LANES:(h // 2 + 1) * LANES]
                do_hs = [jnp.where(sels[h], pair_of(d_attn, h), 0.0).astype(BF16) for h in heads]
                q_hs = [jnp.where(sels[h], pair_of(q_blks[b], h), jnp.zeros((BLOCK, LANES), BF16)) for h in heads]
                dps = [_dot_nt(do_hs[h], vvar[_head_kv(h)[0]][_head_kv(h)[1]]) for h in heads]
                yield
                deltas = [jnp.sum(probs[h][0] * dps[h], axis=-1, keepdims=True) for h in heads]
                yield
                dss = [probs[h][0] * (dps[h] - deltas[h]) for h in heads]
                dsinks = [-(probs[h][1] * deltas[h]) for h in heads]
                dsbs = [ds.astype(BF16) for ds in dss]
                pbs = [probs[h][0].astype(BF16) for h in heads]
                yield
                dqs = [_dot(dsbs[h], kvar[_head_kv(h)[0]][_head_kv(h)[1]]) for h in heads]
                tks = [_dot_tn(dsbs[h], q_hs[h]) for h in heads]
                tvs = [_dot_tn(pbs[h], do_hs[h]) for h in heads]
                dq_cols = [dqs[2 * i] + dqs[2 * i + 1] for i in range(N_HEADS // 2)]
                dq_ref[rows[b], :] = (jnp.concatenate(dq_cols, axis=1) * Q_SCALE).astype(BF16)
                per_kv = N_HEADS // N_KV
                kv_sum = lambda ts, kvh: sum(ts[kvh * per_kv + 1:(kvh + 1) * per_kv], ts[kvh * per_kv])
                dkk = _fold_kv(kv_sum(tks, 0), kv_sum(tks, 1))
                dvv = _fold_kv(kv_sum(tvs, 0), kv_sum(tvs, 1))
                return jnp.concatenate([dkk, dvv], axis=1), dss, dsinks

            bwd, rms_g = [], []
            for b in range(2):
                attn, probs, kvar, vvar = res[2 * b]
                gm, saved = res[2 * b + 1]
                na_unit, r_a = _rms(attn, 1.0)
                ng_unit, r_g = _rms(gm, 1.0)
                dmix = dmix_ref[rows[b], :].astype(F32)
                dn_a = dmix[:, :ATTN_W]
                dn_g = dmix[:, ATTN_W:]
                rms_g.append((_colsum(dn_a * na_unit), _colsum(dn_g * ng_unit)))
                t_a = dn_a * aog_ref[...]
                d_attn = r_a * t_a - na_unit * (r_a * jnp.mean(t_a * na_unit, axis=-1, keepdims=True))
                t_g = dn_g * gog_ref[...]
                d_gm = r_g * t_g - ng_unit * (r_g * jnp.mean(t_g * ng_unit, axis=-1, keepdims=True))
                bwd.append(attention_bwd(b, d_attn, probs, kvar, vvar))
                bwd.append(gating_bwd(b, d_gm, saved))
            (dkv_a, dss_a, dsk_a), (dmx_a, dws_a, glg_a, glb_a) = _interleave(*bwd[:2])
            (dkv_b, dss_b, dsk_b), (dmx_b, dws_b, glg_b, glb_b) = _interleave(*bwd[2:])

            vec_ref[0:1, :] += rms_g[0][0] + rms_g[1][0]
            vec_ref[1:2, :] += rms_g[0][1] + rms_g[1][1]
            vec_ref[2:3, :] += glg_a + glg_b
            vec_ref[3:4, :] += glb_a + glb_b
            dbs_ref[...] += dmx_a + dmx_b
            for g in range(N_GROUPS):
                dws_ref[g] += dws_a[g] + dws_b[g]
            for h in range(N_HEADS):
                gbias_ref[h] += dss_a[h] + dss_b[h]
                dsink_ref[h] += dsk_a[h] + dsk_b[h]

            dkv_ref[:BLOCK, :] = done[...].astype(BF16)
            dkv_ref[BLOCK:, :] = (carry[...] + dkv_a[:BLOCK]).astype(BF16)
            done[...] = dkv_a[BLOCK:] + dkv_b[:BLOCK]
            carry[...] = dkv_b[BLOCK:]

    last = n_steps - 1
    cur = lambda w: pl.BlockSpec((tile, w), lambda n: (jnp.minimum(n, last), 0))
    late = lambda w: pl.BlockSpec((tile, w), lambda n: (jnp.clip(n - 1, 0, last), 0))
    before = pl.BlockSpec((BLOCK, 2 * KV_W), lambda n: (jnp.clip(2 * n - 1, 0, 2 * last + 1), 0))
    outs = pl.pallas_call(
        body, name="bwd_mix",
        out_shape=[jax.ShapeDtypeStruct((s, ATTN_W), BF16), jax.ShapeDtypeStruct((s, 2 * KV_W), BF16),
                   jax.ShapeDtypeStruct((s, GMLP_W), BF16), jax.ShapeDtypeStruct((s, GMLP_W), BF16),
                   jax.ShapeDtypeStruct((N_HEADS, BLOCK, 2 * BLOCK), F32),
                   jax.ShapeDtypeStruct((N_GROUPS, BLOCK, BLOCK), F32),
                   jax.ShapeDtypeStruct((BLOCK, GMLP_W), F32), jax.ShapeDtypeStruct((8, GMLP_W), F32),
                   jax.ShapeDtypeStruct((N_HEADS, BLOCK, 1), F32)]
        + [jax.ShapeDtypeStruct(_rx_shape(p.shape, k), BF16) for p, k in zip(grad_parts, grad_kinds)],
        grid=(n_steps + 1,),
        in_specs=[cur(ATTN_W), cur(2 * KV_W), before, cur(GMLP_W), cur(GMLP_W), cur(D_MODEL),
                  _const_spec((N_HEADS, BLOCK, 2 * BLOCK)), pl.BlockSpec(memory_space=pltpu.SMEM),
                  _const_spec((1, GMLP_W)), _const_spec((1, GMLP_W)), _const_spec((N_GROUPS, BLOCK, BLOCK)),
                  _const_spec((BLOCK, GMLP_W)), _const_spec((GMLP_W, GMLP_W)), _const_spec((1, ATTN_W)),
                  _const_spec((1, GMLP_W))] + [ANY] * n_g,
        out_specs=[cur(ATTN_W), late(2 * KV_W), cur(GMLP_W), cur(GMLP_W),
                   _const_spec((N_HEADS, BLOCK, 2 * BLOCK)), _const_spec((N_GROUPS, BLOCK, BLOCK)),
                   _const_spec((BLOCK, GMLP_W)), _const_spec((8, GMLP_W)), _const_spec((N_HEADS, BLOCK, 1))]
        + [ANY] * n_g,
        scratch_shapes=[pltpu.VMEM((BLOCK, 2 * KV_W), F32), pltpu.VMEM((BLOCK, 2 * KV_W), F32)]
        + _ChipExchange.sems(n_g),
        compiler_params=_params(("arbitrary",)),
    )(q, kv, kv, gu, gv, dmix, bias, sinks, gln_g, gln_b, wsm, bsx, amat, aog, gog, *grad_parts)
    return outs[:9], outs[9:]


def _mix_finalize(gbias, bucket, dws, dbs, dsink):
    def body(gb_ref, bucket_ref, dws_ref, dbs_ref, dsink_ref, tall_ref):
        bk = bucket_ref[...]
        lane = lax.broadcasted_iota(jnp.int32, (N_BUCKETS, LANES), 1)
        rowi = lax.broadcasted_iota(jnp.int32, (N_BUCKETS, LANES), 0)
        drb = jnp.zeros((N_BUCKETS, LANES), F32)
        dsk = jnp.zeros((8, LANES), F32)
        lane8 = lax.broadcasted_iota(jnp.int32, (8, LANES), 1)
        for h in range(N_HEADS):
            g = gb_ref[h]
            for b in range(N_BUCKETS):
                tot = jnp.sum(_colsum(jnp.where(bk == float(b), g, 0.0)), axis=1, keepdims=True)
                drb = jnp.where((rowi == h) & (lane == b), tot, drb)
            sk = jnp.sum(dsink_ref[h], axis=0, keepdims=True)
            dsk = jnp.where(lane8 == h, sk, dsk)
        tall_ref[TALL_RB:TALL_RB + N_BUCKETS, :] = drb
        tall_ref[TALL_SK:TALL_SK + 8, :] = dsk
        ti = lax.broadcasted_iota(jnp.int32, (BLOCK, BLOCK), 0)
        ui = lax.broadcasted_iota(jnp.int32, (BLOCK, BLOCK), 1)
        for g in range(N_GROUPS):
            tall_ref[g * BLOCK:(g + 1) * BLOCK, :] = jnp.where(ti >= ui, dws_ref[g], 0.0)
        gi = lax.broadcasted_iota(jnp.int32, (GMLP_W, LANES), 0) // GROUP_DIM
        li = lax.broadcasted_iota(jnp.int32, (GMLP_W, LANES), 1)
        ind = jnp.where(gi == li, 1.0, 0.0).astype(BF16)
        d = dbs_ref[...]
        hi = d.astype(BF16)
        r1 = d - hi.astype(F32)
        mid = r1.astype(BF16)
        lo = (r1 - mid.astype(F32)).astype(BF16)
        dbsg = _dot(hi, ind) + _dot(mid, ind) + _dot(lo, ind)
        tall_ref[TALL_BS:TALL_BS + N_GROUPS, :] = dbsg.T[:N_GROUPS, :]

    return pl.pallas_call(
        body, name="mix_finalize", out_shape=jax.ShapeDtypeStruct((TALL_ROWS, LANES), F32), grid=(1,),
        in_specs=[_const_spec((N_HEADS, BLOCK, 2 * BLOCK)), _const_spec((BLOCK, 2 * BLOCK)),
                  _const_spec((N_GROUPS, BLOCK, BLOCK)), _const_spec((BLOCK, GMLP_W)),
                  _const_spec((N_HEADS, BLOCK, 1))],
        out_specs=_const_spec((TALL_ROWS, LANES)),
        compiler_params=_params(("arbitrary",)),
    )(gbias, bucket, dws, dbs, dsink)


def _bwd_in(dq, dkv, dgu, dgv, dxa, x, modr, w_in, tm):
    s = x.shape[0]

    def body(dq_ref, dkv_ref, dgu_ref, dgv_ref, dxa_ref, x_ref, mod_ref, w_ref, gx_ref, acc_ref, db_ref):
        @pl.when(pl.program_id(0) == 0)
        def _():
            acc_ref[...] = jnp.zeros_like(acc_ref)
            db_ref[...] = jnp.zeros_like(db_ref)

        dproj = jnp.concatenate([dq_ref[...], dkv_ref[...], dgu_ref[...], dgv_ref[...]], axis=1)
        dh1 = _dot(dproj, w_ref[...])
        gx_ref[...] = dxa_ref[...].astype(F32) + dh1 * (1.0 + mod_ref[1:2, :])
        acc_ref[0:1, :] += _colsum(dh1 * x_ref[...].astype(F32))
        acc_ref[1:2, :] += _colsum(dh1)
        db_ref[0:1, :] += _colsum(dproj.astype(F32))

    row = lambda w: pl.BlockSpec((tm, w), lambda i: (i, 0))
    return pl.pallas_call(
        body, name="bwd_in",
        out_shape=(jax.ShapeDtypeStruct((s, D_MODEL), F32), jax.ShapeDtypeStruct((8, D_MODEL), F32),
                   jax.ShapeDtypeStruct((8, IN_W), F32)),
        grid=(s // tm,),
        in_specs=[row(ATTN_W), row(2 * KV_W), row(GMLP_W), row(GMLP_W), row(D_MODEL), row(D_MODEL),
                  _const_spec((8, D_MODEL)), _const_spec(w_in.shape)],
        out_specs=(row(D_MODEL), _const_spec((8, D_MODEL)), _const_spec((8, IN_W))),
        compiler_params=_params(("arbitrary",)),
    )(dq, dkv, dgu, dgv, dxa, x, modr, w_in)


RING_SLOTS = 3


def _wgrad(a, bs, tm, tk, name, transposed=False, gather_vs=(), ring=False):
    k_all, m = a.shape
    n = sum(b.shape[1] for b in bs)
    nk = k_all // tk
    nm = m // tm
    n_b = len(bs)
    n_v = len(gather_vs)

    n_sem = 1 + n_b

    def body(a_ref, *rest):
        b_refs, v_refs = rest[:n_b], rest[n_b:n_b + n_v]
        o_ref, ob_ref = rest[n_b + n_v:n_b + n_v + 2]
        vg_refs = rest[n_b + n_v + 2:n_b + 2 * n_v + 2]
        scratch = rest[n_b + 2 * n_v + 2:]
        i, k = pl.program_id(0), pl.program_id(1)
        if ring:
            a_buf, b_buf, ring_sems = scratch[-3:]
            scratch = scratch[:-3]
            step = i * nk + k

            def ring_copies(t):
                slot = t % RING_SLOTS
                rows = pl.ds(pl.multiple_of((t % nk) * tk, tk), tk)
                cols = pl.ds(pl.multiple_of((t // nk) * tm, tm), tm)
                copies = [pltpu.make_async_copy(a_ref.at[rows, cols], a_buf.at[slot], ring_sems.at[slot * n_sem])]
                off = 0
                for j, r in enumerate(b_refs):
                    width = r.shape[1]
                    copies.append(pltpu.make_async_copy(r.at[rows, :], b_buf.at[slot, :, pl.ds(off, width)],
                                                        ring_sems.at[slot * n_sem + 1 + j]))
                    off += width
                return copies

            @pl.when(step == 0)
            def _():
                for t in range(RING_SLOTS - 1):
                    for cp in ring_copies(t):
                        cp.start()

            @pl.when(step + RING_SLOTS - 1 < nm * nk)
            def _():
                for cp in ring_copies(step + RING_SLOTS - 1):
                    cp.start()

            for cp in ring_copies(step):
                cp.wait()
        if n_v:
            gather = _Gather8(v_refs, vg_refs, *scratch)

            @pl.when((i == 0) & (k == 0))
            def _():
                gather.start()

            @pl.when((i == nm - 1) & (k == 0))
            def _():
                gather.forward()

        @pl.when(k == 0)
        def _():
            o_ref[...] = jnp.zeros_like(o_ref)

        if ring:
            av, b = a_buf[step % RING_SLOTS], b_buf[step % RING_SLOTS]
        else:
            av = a_ref[...]
            b = b_refs[0][...] if n_b == 1 else jnp.concatenate([r[...] for r in b_refs], axis=1)
        if transposed:
            o_ref[...] += _dot_tn(b, av)
        else:
            o_ref[...] += _dot_tn(av, b)

        @pl.when(k == nk - 1)
        def _():
            ob_ref[...] = o_ref[...].astype(BF16)

        if n_v:
            @pl.when((i == nm - 1) & (k == nk - 1))
            def _():
                gather.finish()

    if transposed:
        out_spec = pl.BlockSpec((n, tm), lambda i, k: (0, i))
        shape = (n, m)
    else:
        out_spec = pl.BlockSpec((tm, n), lambda i, k: (i, 0))
        shape = (m, n)
    outs = pl.pallas_call(
        body, name=name,
        out_shape=[jax.ShapeDtypeStruct(shape, F32), jax.ShapeDtypeStruct(shape, BF16)] + _gathered8_shapes(gather_vs),
        grid=(nm, nk),
        in_specs=([ANY] * (1 + n_b) if ring else
                  [pl.BlockSpec((tk, tm), lambda i, k: (k, i))]
                  + [pl.BlockSpec((tk, b.shape[1]), lambda i, k: (k, 0)) for b in bs]) + [ANY] * n_v,
        out_specs=[out_spec, out_spec] + [ANY] * n_v,
        scratch_shapes=(_Gather8.sems(n_v) if n_v else [])
        + ([pltpu.VMEM((RING_SLOTS, tk, tm), BF16), pltpu.VMEM((RING_SLOTS, tk, n), BF16),
            pltpu.SemaphoreType.DMA((RING_SLOTS * n_sem,))] if ring else []),
        compiler_params=_params(("arbitrary", "arbitrary") if n_v or ring else ("parallel", "arbitrary")),
    )(a, *bs, *gather_vs)
    return outs[0], outs[1], outs[2:]


def _adam_math(w, g, m, v):
    m2 = ADAM_B1 * m + (1.0 - ADAM_B1) * g
    v2 = ADAM_B2 * v + (1.0 - ADAM_B2) * (g * g)
    m_hat = m2 / (1.0 - ADAM_B1 ** ADAM_STEP)
    v_hat = v2 / (1.0 - ADAM_B2 ** ADAM_STEP)
    delta = -ADAM_LR * (m_hat / (jnp.sqrt(v_hat) + ADAM_EPS) + ADAM_WD * w)
    return delta, m2, v2


def _adam_halves(w, mine, got, m, v, tr, name):
    r, cc = w.shape
    h = r // 2
    nt = h // tr

    def body(c_ref, w_ref, mine_ref, got_ref, m_ref, v_ref, g_ref, d_ref, m2_ref, v2_ref):
        g = jnp.where(pl.program_id(0) == c_ref[0], mine_ref[...], got_ref[...])
        g_ref[...] = g
        d, m2, v2 = _adam_math(w_ref[...], g, m_ref[...], v_ref[...])
        d_ref[...] = d
        m2_ref[...] = m2
        v2_ref[...] = v2

    full = pl.BlockSpec((tr, cc), lambda hh, i, c_ref: (hh * nt + i, 0))
    half = pl.BlockSpec((tr, cc), lambda hh, i, c_ref: (i, 0))
    shp = jax.ShapeDtypeStruct((r, cc), F32)
    return pl.pallas_call(
        body, name=name, out_shape=(shp, shp, shp, shp),
        grid_spec=pltpu.PrefetchScalarGridSpec(
            num_scalar_prefetch=1, grid=(2, nt), in_specs=[full, half, half, full, full],
            out_specs=(full, full, full, full)),
        compiler_params=_params(("arbitrary", "arbitrary")),
    )(_core_index_scalar(), w, mine, got, m, v)


def _adam_w_ada(sc_t, dmod_all, w, m, v, tr):
    r, cc = w.shape

    def body(chip_ref, sct_ref, dm_ref, w_ref, m_ref, v_ref, g_ref, d_ref, m2_ref, v2_ref):
        g = sct_ref[:, 0:1] * dm_ref[0:1, :]
        for k in range(1, N_DEV):
            g = g + sct_ref[:, k:k + 1] * dm_ref[k:k + 1, :]
        g_ref[...] = g
        d, m2, v2 = _adam_math(w_ref[...], g, m_ref[...], v_ref[...])
        d_ref[...] = d
        m2_ref[...] = m2
        v2_ref[...] = v2

    spec = pl.BlockSpec((tr, cc), lambda i, chip_ref: (i, 0))
    shp = jax.ShapeDtypeStruct((r, cc), F32)
    return pl.pallas_call(
        body, name="adam_w_ada", out_shape=(shp, shp, shp, shp),
        grid_spec=pltpu.PrefetchScalarGridSpec(
            num_scalar_prefetch=1, grid=(r // tr,),
            in_specs=[pl.BlockSpec((tr, N_DEV), lambda i, chip_ref: (i, 0)),
                      pl.BlockSpec((N_DEV, cc), lambda i, chip_ref: (0, chip_ref[0])), spec, spec, spec],
            out_specs=(spec, spec, spec, spec)),
        compiler_params=_params(("parallel",)),
    )(_chip_index_scalar(), sc_t, dmod_all, w, m, v)


def _pack_wide(acc_i, acc_m, acc_f, db_in, vec):
    arrs = [acc_i, acc_m, acc_f, db_in, vec]
    i_, m_, f_, b_, v_ = range(5)
    src = {"b_in": (b_, 0), "ln1_g": (m_, 2), "ln1_b": (m_, 3), "ln2_g": (f_, 1), "ln2_b": (f_, 2),
           "gmlp_ln_g": (v_, 2), "gmlp_ln_b": (v_, 3), "attn_out_g": (v_, 0), "gmlp_out_g": (v_, 1), "loss": (f_, 0)}
    dmod = [(i_, 1), (i_, 0), (m_, 4), (m_, 1), (m_, 0), (f_, 3)]

    def body(*refs):
        ins, wide_ref = refs[:5], refs[5]
        wide_ref[...] = jnp.zeros_like(wide_ref)
        for k, (a, row) in enumerate(dmod):
            wide_ref[0:1, k * D_MODEL:(k + 1) * D_MODEL] = ins[a][row:row + 1, :]
        for name, (a, row) in src.items():
            r, off, n = WIDE_LAYOUT[name]
            wide_ref[r:r + 1, off:off + n] = ins[a][row:row + 1, :]

    return pl.pallas_call(
        body, name="pack_wide", out_shape=jax.ShapeDtypeStruct((8, WIDE_W), F32), grid=(1,),
        in_specs=[_const_spec(a.shape) for a in arrs], out_specs=_const_spec((8, WIDE_W)),
        compiler_params=_params(("arbitrary",)),
    )(*arrs)


def _adam_small(gw, gt, wide_wmv, w_s, b_s, rel_bias, sinks, after):
    names = list(WIDE_PARAMS)
    tall = [("gmlp_w_s", w_s), ("gmlp_b_s", b_s), ("rel_bias", rel_bias), ("attn_sinks", sinks)]
    ins = [gw, gt]
    for n in names:
        ins += list(wide_wmv[n])
    for _, t in tall:
        ins += list(t)
    n_in = len(ins)

    def body(*refs):
        gw_ref, gt_ref = refs[0], refs[1]
        wmv = refs[2:n_in]
        dmod_ref, loss_ref, loss1_ref = refs[n_in + 1:n_in + 4]
        outs = refs[n_in + 4:]

        def tall_sum(r0, nr):
            g = gt_ref[r0:r0 + nr, :]
            for d in range(1, N_DEV):
                g = g + gt_ref[d * TALL_ROWS + r0:d * TALL_ROWS + r0 + nr, :]
            return g

        def emit(k, g, w_ref, m_ref, v_ref):
            d, m2, v2 = _adam_math(w_ref[...], g, m_ref[...], v_ref[...])
            outs[4 * k][...] = g
            outs[4 * k + 1][...] = d
            outs[4 * k + 2][...] = m2
            outs[4 * k + 3][...] = v2

        gsum = gw_ref[0:8, :]
        for d in range(1, N_DEV):
            gsum = gsum + gw_ref[8 * d:8 * d + 8, :]
        for d in range(N_DEV):
            dmod_ref[d:d + 1, :] = gw_ref[8 * d:8 * d + 1, :]
        for k, n in enumerate(names):
            r, off, sz = WIDE_LAYOUT[n]
            emit(k, gsum[r:r + 1, off:off + sz], *wmv[3 * k:3 * k + 3])
        r, off, sz = WIDE_LAYOUT["loss"]
        tot = jnp.sum(gsum[r:r + 1, off:off + sz], axis=1, keepdims=True)
        loss_ref[...] = jnp.broadcast_to(tot * (0.5 / D_MODEL), loss_ref.shape)
        loss1_ref[...] = tot * (0.5 / D_MODEL)

        k0 = len(names)
        ws_refs = wmv[3 * k0:3 * k0 + 3]
        for g in range(N_GROUPS):
            rows = slice(g * BLOCK, (g + 1) * BLOCK)
            gg = tall_sum(g * BLOCK, BLOCK)
            d, m2, v2 = _adam_math(ws_refs[0][rows, :], gg, ws_refs[1][rows, :], ws_refs[2][rows, :])
            outs[4 * k0][rows, :] = gg
            outs[4 * k0 + 1][rows, :] = d
            outs[4 * k0 + 2][rows, :] = m2
            outs[4 * k0 + 3][rows, :] = v2
        emit(k0 + 1, tall_sum(TALL_BS, N_GROUPS), *wmv[3 * (k0 + 1):3 * (k0 + 1) + 3])
        emit(k0 + 2, tall_sum(TALL_RB, N_HEADS)[:, :N_BUCKETS], *wmv[3 * (k0 + 2):3 * (k0 + 2) + 3])
        emit(k0 + 3, tall_sum(TALL_SK, 8)[0:1, :N_HEADS], *wmv[3 * (k0 + 3):3 * (k0 + 3) + 3])

    out_shapes = [jax.ShapeDtypeStruct((N_DEV, WIDE_W), F32), jax.ShapeDtypeStruct((8, LANES), F32),
                  jax.ShapeDtypeStruct((1, 1), F32)]
    for n in names:
        out_shapes += [jax.ShapeDtypeStruct(wide_wmv[n][0].shape, F32)] * 4
    for _, t in tall:
        out_shapes += [jax.ShapeDtypeStruct(t[0].shape, F32)] * 4
    res = pl.pallas_call(
        body, name="adam_small", out_shape=out_shapes, grid=(1,),
        in_specs=[_const_spec(a.shape) for a in ins] + [ANY], out_specs=[_const_spec(o.shape) for o in out_shapes],
        compiler_params=_params(("arbitrary",)),
    )(*ins, after)
    out = {}
    for k, n in enumerate(names + [t[0] for t in tall]):
        out[n] = tuple(res[3 + 4 * k:7 + 4 * k])
    return res[0], res[1], res[2], out


def kernel(x, c, rel_bias, w_ada, b_ada, w_in, b_in, attn_sinks, gmlp_ln_g, gmlp_ln_b, gmlp_w_s, gmlp_b_s, attn_out_g, gmlp_out_g, w_out, ln1_g, ln1_b, w_gate_up, w_down, ln2_g, ln2_b, loss_target, m_rel_bias, m_w_ada, m_b_ada, m_w_in, m_b_in, m_attn_sinks, m_gmlp_ln_g, m_gmlp_ln_b, m_gmlp_w_s, m_gmlp_b_s, m_attn_out_g, m_gmlp_out_g, m_w_out, m_ln1_g, m_ln1_b, m_w_gate_up, m_w_down, m_ln2_g, m_ln2_b, v_rel_bias, v_w_ada, v_b_ada, v_w_in, v_b_in, v_attn_sinks, v_gmlp_ln_g, v_gmlp_ln_b, v_gmlp_w_s, v_gmlp_b_s, v_attn_out_g, v_gmlp_out_g, v_w_out, v_ln1_g, v_ln1_b, v_w_gate_up, v_w_down, v_ln2_g, v_ln2_b):
    ix, iy, _ = _my_pos()
    chip = 2 * ix + iy
    s = x.shape[1]
    xs = x[0]
    tgt = loss_target[0]
    tm_big = min(512, s)
    tm_ffn = min(FFN_SUB, s)

    sc_all, modr, (w_in_g, w_out_g) = _prologue(
        jnp.pad(c, ((0, 7), (0, 0))), w_ada[0], b_ada, [_with_own_block(w_in[0].T, chip), _with_own_block(w_out[0], chip)])
    w_in_f = w_in_g.reshape(IN_W, D_MODEL)

    bucket = _bucket_table()
    bias, wsm = _prep_tables(bucket, rel_bias.T, gmlp_w_s[0])
    bsx = jnp.repeat(gmlp_b_s[0].T, GROUP_DIM, axis=1)
    amat = _group_mean_matrix()
    sinks = attn_sinks[0]

    (h1, q, kv, gu, gv, xb), (w_dn_g,) = _fwd_in(xs, modr, w_in_f, b_in, tm_big, [_with_own_block(w_down[0], chip)],
                                                 ["blk"])
    w_out_f = w_out_g.reshape(D_MODEL, D_MODEL)
    (x1, x1b, y, mixed), (w_gu_f,) = _fwd_mix(
        q, kv, gu, gv, xs, modr, bias, sinks, gmlp_ln_g, gmlp_ln_b, wsm, bsx, amat, attn_out_g, gmlp_out_g, w_out_f,
        ln1_g, ln1_b, tm_big, [_with_own_block(w_gate_up[0], chip)], ["blk"])
    assert w_gate_up.shape[2] == FF_CHUNK
    w_dn_f = w_dn_g.reshape(D_FF, D_MODEL)
    h2, act, dy2, dx1a, acc_f = _fwd_ffn(x1, tgt, modr, ln2_g, ln2_b, w_gu_f, w_dn_f, min(2 * FFN_SUB, s))

    a_act, dgu_ff, dh2 = _bwd_ffn(dy2, act, w_gu_f, w_dn_f, min(FFN_SUB, s))
    g_dn, g_dn_b, _ = _wgrad(a_act, [dy2], D_FF // 2, min(1024, s), "wgrad_down")
    g_gu, g_gu_b, _ = _wgrad(h2, [dgu_ff], 512, min(512, s), "wgrad_gate_up")
    blk3 = lambda a, rows: a.reshape(N_CHIPS, rows, a.shape[1])
    (dxa, dy, dmix, acc_m), (got_dn, got_gu) = _bwd_mid(
        dh2, dx1a, x1b, xs, y, modr, ln1_g, w_out_f, tm_big, [blk3(g_dn_b, D_FF // N_CHIPS), g_gu_b], ["blk", "cols"])
    g_out, g_out_b, _ = _wgrad(mixed, [dy], 512, min(2048, s), "wgrad_out")
    (got_out,) = _swap_halves([blk3(g_out_b, D_MODEL // N_CHIPS)], ["blk"], "rs_swap_out")
    kinds_a = ["blk", "cols", "blk"]
    fulls_a = [blk3(g_dn, D_FF // N_CHIPS), g_gu, blk3(g_out, D_MODEL // N_CHIPS)]
    gots_a = [got_dn, got_gu, got_out]
    parts_a = [_add_halves(f, g, k, "rs_add_a%d" % i) for i, (f, g, k) in enumerate(zip(fulls_a, gots_a, kinds_a))]
    (dq, dkv, dgu, dgv, gbias, dws, dbs, vec, dsink), rxs_a = _bwd_mix(
        q, kv, gu, gv, dmix, bias, sinks, gmlp_ln_g, gmlp_ln_b, wsm, bsx, amat, attn_out_g, gmlp_out_g,
        [p[1] for p in parts_a], kinds_a)
    tall_g = _mix_finalize(gbias, bucket, dws, dbs, dsink)
    grad_x, acc_i, db_in = _bwd_in(dq, dkv, dgu, dgv, dxa, xb, modr, w_in_f, tm_big)

    wide_g = _pack_wide(acc_i, acc_m, acc_f, db_in, vec)
    full_in, full_in_b, (gw, gt) = _wgrad(h1, [dq, dkv, dgu, dgv], 512, min(1024, s), "wgrad_in", transposed=True,
                                          gather_vs=[wide_g, tall_g], ring=True)
    (got_in,) = _swap_halves([blk3(full_in_b, IN_W // N_CHIPS)], ["blk"], "rs_swap_in")
    part_in = _add_halves(blk3(full_in, IN_W // N_CHIPS), got_in, "blk", "rs_add_in")
    in_send, in_recv, in_part, in_rx, token = _exchange_start(part_in[1], "rs_chips_in_start")
    wide_wmv ={"b_ada": (b_ada, m_b_ada, v_b_ada), "b_in": (b_in, m_b_in, v_b_in),
                "ln1_g": (ln1_g, m_ln1_g, v_ln1_g), "ln1_b": (ln1_b, m_ln1_b, v_ln1_b),
                "ln2_g": (ln2_g, m_ln2_g, v_ln2_g), "ln2_b": (ln2_b, m_ln2_b, v_ln2_b),
                "gmlp_ln_g": (gmlp_ln_g, m_gmlp_ln_g, v_gmlp_ln_g), "gmlp_ln_b": (gmlp_ln_b, m_gmlp_ln_b, v_gmlp_ln_b),
                "attn_out_g": (attn_out_g, m_attn_out_g, v_attn_out_g),
                "gmlp_out_g": (gmlp_out_g, m_gmlp_out_g, v_gmlp_out_g)}
    rows2 = lambda a: a.reshape(-1, a.shape[-1])
    dmod_all, loss_t, loss1, small = _adam_small(
        gw, gt, wide_wmv, tuple(rows2(a) for a in (gmlp_w_s, m_gmlp_w_s, v_gmlp_w_s)),
        tuple(rows2(a) for a in (gmlp_b_s, m_gmlp_b_s, v_gmlp_b_s)), (rel_bias.T, m_rel_bias.T, v_rel_bias.T),
        (attn_sinks, m_attn_sinks, v_attn_sinks), token)
    small["rel_bias"] = tuple(a.T for a in small["rel_bias"])
    loss = loss1.reshape(())

    g_ada, d_ada, m_ada, v_ada = _adam_w_ada(sc_all.T, dmod_all, w_ada[0], m_w_ada[0], v_w_ada[0], 256)

    sums = [(parts_a[0][0], rxs_a[0], 176), (parts_a[1][0], rxs_a[1], 256), (parts_a[2][0], rxs_a[2], 128)]
    mine = [_sum_chips(p, rx, tr, "rs_sum_%d" % i, loss_t) for i, (p, rx, tr) in enumerate(sums)]
    got = _share_halves(mine, "rs_share")
    gs_dn, d_dn, m_dn, v_dn = _adam_halves(w_down[0], mine[0], got[0], m_w_down[0], v_w_down[0], 176, "adam_w_down")
    gs_gu, d_gu, m_gu, v_gu = _adam_halves(w_gate_up[0], mine[1], got[1], m_w_gate_up[0], v_w_gate_up[0], 256,
                                           "adam_w_gate_up")
    gs_out, d_out, m_out, v_out = _adam_halves(w_out[0], mine[2], got[2], m_w_out[0], v_w_out[0], 128, "adam_w_out")

    rx_in = _exchange_wait(in_send, in_recv, in_part, in_rx, d_gu, "rs_chips_in_wait")
    mine_in = _sum_chips(part_in[0], rx_in, 112, "rs_sum_in", rx_in)
    (got_in_half,) = _share_halves([mine_in], "rs_share_in")
    in_t = _adam_halves(w_in[0].T, mine_in, got_in_half, m_w_in[0].T, v_w_in[0].T, 112, "adam_w_in")
    gs_in, d_in, m_in, v_in = (a.T for a in in_t)

    big = {"w_ada": (g_ada, d_ada, m_ada, v_ada), "w_in": (gs_in, d_in, m_in, v_in), "w_out": (gs_out, d_out, m_out, v_out),
           "w_gate_up": (gs_gu, d_gu, m_gu, v_gu), "w_down": (gs_dn, d_dn, m_dn, v_dn)}
    order = ["rel_bias", "w_ada", "b_ada", "w_in", "b_in", "attn_sinks", "gmlp_ln_g", "gmlp_ln_b", "gmlp_w_s", "gmlp_b_s",
             "attn_out_g", "gmlp_out_g", "w_out", "ln1_g", "ln1_b", "w_gate_up", "w_down", "ln2_g", "ln2_b"]
    shapes = {"gmlp_w_s": gmlp_w_s.shape, "gmlp_b_s": gmlp_b_s.shape}
    outs = [loss, grad_x[None]]
    for k in range(4):
        for name in order:
            if name in big:
                outs.append(big[name][k][None])
            elif name in shapes:
                outs.append(small[name][k].reshape(shapes[name]))
            else:
                outs.append(small[name][k])
    return tuple(outs)
```

```python
import math

import numpy as np
import jax
import jax.numpy as jnp
from jax import lax
from jax.experimental import pallas as pl
from jax.experimental.pallas import tpu as pltpu

F32 = jnp.float32
BF16 = jnp.bfloat16
MESH = pl.DeviceIdType.MESH

D_MODEL = 1024
N_HEADS = 8
N_KV = 2
HEAD_DIM = 64
ATTN_W = N_HEADS * HEAD_DIM
KV_W = N_KV * HEAD_DIM
N_GROUPS = 8
GROUP_DIM = 64
GMLP_W = N_GROUPS * GROUP_DIM
IN_W = ATTN_W + 2 * KV_W + 2 * GMLP_W
BLOCK = 128
N_BUCKETS = 32
MAX_DISTANCE = 128
D_FF = 2816
ALPHA = 2.0 ** 0.25
LN_EPS = 1e-5
NEG_INF = -1e30
ADAM_LR, ADAM_B1, ADAM_B2, ADAM_EPS, ADAM_WD, ADAM_STEP = 0.001, 0.9, 0.999, 1e-8, 0.01, 10
N_CHIPS = 4
N_DEV = 8
LANES = 128
V7X_VMEM_LIMIT = 56 * 2 ** 20
GELU_C = math.sqrt(2.0 / math.pi)
Q_SCALE = HEAD_DIM ** -0.5
ANY = pl.BlockSpec(memory_space=pl.ANY)

TALL_BS = N_GROUPS * BLOCK
TALL_RB = TALL_BS + 8
TALL_SK = TALL_RB + N_BUCKETS
TALL_ROWS = TALL_SK + 8
WIDE_W = 6 * D_MODEL
WIDE_LAYOUT = {
    "b_ada": (0, 0, 6 * D_MODEL),
    "b_in": (1, 0, IN_W), "ln1_g": (1, IN_W, D_MODEL), "ln1_b": (1, IN_W + D_MODEL, D_MODEL),
    "ln2_g": (1, IN_W + 2 * D_MODEL, D_MODEL), "ln2_b": (1, IN_W + 3 * D_MODEL, D_MODEL),
    "gmlp_ln_g": (2, 0, GMLP_W), "gmlp_ln_b": (2, GMLP_W, GMLP_W), "attn_out_g": (2, 2 * GMLP_W, ATTN_W),
    "gmlp_out_g": (2, 2 * GMLP_W + ATTN_W, GMLP_W), "loss": (2, 3 * GMLP_W + ATTN_W, D_MODEL)}
WIDE_PARAMS = tuple(n for n in WIDE_LAYOUT if n != "loss")


def _params(sem=None):
    return pltpu.CompilerParams(dimension_semantics=sem, vmem_limit_bytes=V7X_VMEM_LIMIT)


def _const_spec(shape, single=False):
    nd = len(shape)
    if single:
        return pl.BlockSpec(shape, lambda *_: (0,) * nd, pipeline_mode=pl.Buffered(1))
    return pl.BlockSpec(shape, lambda *_: (0,) * nd)


def _dot(a, b):
    return jnp.dot(a, b, preferred_element_type=F32)


def _dot_nt(a, b):
    return lax.dot_general(a, b, (((1,), (1,)), ((), ())), preferred_element_type=F32)


def _dot_tn(a, b):
    return lax.dot_general(a, b, (((0,), (0,)), ((), ())), preferred_element_type=F32)


def _gelu(x):
    t = jnp.tanh(GELU_C * (x + 0.044715 * x * x * x))
    return 0.5 * x * (1.0 + t), t


def _gelu_grad(x, t):
    return 0.5 * (1.0 + t) + 0.5 * x * (1.0 - t * t) * GELU_C * (1.0 + 3.0 * 0.044715 * x * x)


def _split_dot(x, a):
    hi = x.astype(BF16)
    lo = (x - hi.astype(F32)).astype(BF16)
    return _dot(hi, a) + _dot(lo, a)


def _group_mean_matrix():
    g = np.arange(GMLP_W) // GROUP_DIM
    return jnp.asarray((g[:, None] == g[None, :]).astype(np.float32) / GROUP_DIM, dtype=BF16)


def _ln_stats(z):
    mu = jnp.mean(z, axis=-1, keepdims=True)
    d = z - mu
    var = jnp.mean(d * d, axis=-1, keepdims=True)
    rstd = lax.rsqrt(var + LN_EPS)
    return d * rstd, rstd


def _ln_bwd(dxhat, xhat, rstd):
    m1 = jnp.mean(dxhat, axis=-1, keepdims=True)
    m2 = jnp.mean(dxhat * xhat, axis=-1, keepdims=True)
    return rstd * (dxhat - m1 - xhat * m2)


def _colsum(x):
    return jnp.sum(x, axis=0, keepdims=True)


def _my_pos():
    return lax.axis_index("x"), lax.axis_index("y"), lax.axis_index("c")


def _other_chips(x, y):
    return [(1 - x, y), (x, 1 - y), (1 - x, 1 - y)]


def _chip_index_scalar():
    ix, iy, _ = _my_pos()
    return jnp.reshape(2 * ix + iy, (1,)).astype(jnp.int32)


def _core_index_scalar():
    return jnp.reshape(lax.axis_index("c"), (1,)).astype(jnp.int32)


class _Gather8:
    def __init__(self, x_refs, out_refs, send_sems, recv_sems, local_sems):
        self.x_refs, self.out_refs = x_refs, out_refs
        self.send_sems, self.recv_sems, self.local_sems = send_sems, recv_sems, local_sems
        self.x, self.y, self.c = _my_pos()
        self.me, self.sibling = (self.x, self.y, self.c), (self.x, self.y, 1 - self.c)
        self.chips = _other_chips(self.x, self.y)

    def _rows(self, a, px, py, pc):
        m_per = self.x_refs[a].shape[0]
        return self.out_refs[a].at[pl.ds((4 * px + 2 * py + pc) * m_per, m_per), :]

    def _copy(self, a, k, block, to, src=None):
        return pltpu.make_async_remote_copy(
            src_ref=self._rows(a, *block) if src is None else src, dst_ref=self._rows(a, *block),
            send_sem=self.send_sems.at[7 * a + k], recv_sem=self.recv_sems.at[7 * a + k], device_id=to,
            device_id_type=MESH)

    def _local(self, a):
        return pltpu.make_async_copy(self.x_refs[a], self._rows(a, *self.me), self.local_sems.at[a])

    def start(self):
        for a in range(len(self.x_refs)):
            self._local(a).start()
            self._copy(a, 0, self.me, self.sibling, src=self.x_refs[a]).start()
            for j, chip in enumerate(self.chips):
                self._copy(a, 1 + j, self.me, (*chip, self.c), src=self.x_refs[a]).start()

    def forward(self):
        for a in range(len(self.x_refs)):
            for j, chip in enumerate(self.chips):
                self._copy(a, 1 + j, (*chip, self.c), self.me).wait_recv()
                self._copy(a, 4 + j, (*chip, self.c), self.sibling).start()

    def finish(self):
        for a in range(len(self.x_refs)):
            self._copy(a, 0, self.sibling, self.me).wait_recv()
            for j, chip in enumerate(self.chips):
                self._copy(a, 4 + j, (*chip, 1 - self.c), self.me).wait_recv()
        for a in range(len(self.x_refs)):
            for k in range(7):
                self._copy(a, k, self.me, self.me).wait_send()
            self._local(a).wait()

    @staticmethod
    def sems(n_v):
        return [pltpu.SemaphoreType.DMA((7 * n_v,)), pltpu.SemaphoreType.DMA((7 * n_v,)),
                pltpu.SemaphoreType.DMA((n_v,))]


def _gathered8_shapes(vs):
    return [jax.ShapeDtypeStruct((N_DEV * v.shape[0], v.shape[1]), v.dtype) for v in vs]


VMEM_WHOLE = pl.BlockSpec(memory_space=pltpu.VMEM)


def _prologue(c_pad, w_ada_s, b_ada, shards):
    n = w_ada_s.shape[1]
    n_w = len(shards)
    assert N_CHIPS * n == 6 * D_MODEL and n % LANES == 0

    def body(c_ref, w_ref, b_ref, *rest):
        sc_ref, modc_ref, modg_ref, modr_ref = rest[n_w:n_w + 4]
        gathered_refs = rest[n_w + 4:2 * n_w + 4]
        call_ref, w_vmem = rest[2 * n_w + 4:2 * n_w + 6]
        sems = rest[2 * n_w + 6:]
        ix, iy, ic = _my_pos()
        chip = 2 * ix + iy
        weights = _WeightGather(gathered_refs, ["blk"] * n_w, sems[0], sems[1])
        gather_c = _Gather8([c_ref], [call_ref], sems[2], sems[3], sems[4])
        gather_mod = _Gather8([modc_ref], [modg_ref], sems[5], sems[6], sems[7])
        load_w = pltpu.make_async_copy(w_ref, w_vmem, sems[8])
        weights.start()
        gather_c.start()
        load_w.start()
        gather_c.forward()
        gather_c.finish()
        cv = call_ref[...]
        sc = cv * _sigmoid(cv)
        a_hi = sc.astype(BF16)
        a_lo = (sc - a_hi.astype(F32)).astype(BF16)
        load_w.wait()
        w = w_vmem[...]
        w_hi = w.astype(BF16)
        w_lo = (w - w_hi.astype(F32)).astype(BF16)
        b = b_ref[:, 0:n]
        for k in range(1, N_CHIPS):
            b = jnp.where(chip == k, b_ref[:, k * n:(k + 1) * n], b)
        mod = _dot(a_hi, w_hi) + _dot(a_hi, w_lo) + _dot(a_lo, w_hi) + b
        for d in range(N_DEV):
            sc_ref[d:d + 1, :] = sc[8 * d:8 * d + 1, :]
            modc_ref[d:d + 1, :] = mod[8 * d:8 * d + 1, :]
        gather_mod.start()
        weights.forward()
        gather_mod.forward()
        gather_mod.finish()
        dev = 2 * chip + ic
        mine = jnp.concatenate([modg_ref[pl.ds(2 * 8 * k + dev, 1), :] for k in range(N_CHIPS)], axis=1)
        modr_ref[...] = jnp.zeros_like(modr_ref)
        for r in range(6):
            modr_ref[r:r + 1, :] = mine[:, r * D_MODEL:(r + 1) * D_MODEL]
        weights.forward_diagonal()
        weights.finish()

    outs = pl.pallas_call(
        body, name="prologue",
        out_shape=[jax.ShapeDtypeStruct((N_DEV, D_MODEL), F32), jax.ShapeDtypeStruct((N_DEV, n), F32),
                   jax.ShapeDtypeStruct((N_DEV * N_DEV, n), F32), jax.ShapeDtypeStruct((8, D_MODEL), F32)]
        + [jax.ShapeDtypeStruct(sh.shape, BF16) for sh in shards],
        in_specs=[VMEM_WHOLE, ANY, VMEM_WHOLE] + [ANY] * n_w,
        out_specs=[VMEM_WHOLE, VMEM_WHOLE, VMEM_WHOLE, VMEM_WHOLE] + [ANY] * n_w,
        input_output_aliases={3 + a: 4 + a for a in range(n_w)},
        scratch_shapes=[pltpu.VMEM((N_DEV * 8, D_MODEL), F32), pltpu.VMEM(w_ada_s.shape, F32)]
        + _WeightGather.sems(n_w) + _Gather8.sems(1) + _Gather8.sems(1) + [pltpu.SemaphoreType.DMA],
        compiler_params=pltpu.CompilerParams(vmem_limit_bytes=V7X_VMEM_LIMIT),
    )(c_pad, w_ada_s, b_ada, *shards)
    return outs[0], outs[3], outs[4:]


def _with_own_block(shard, chip):
    empty = lax.empty((N_CHIPS,) + shard.shape, BF16)
    return lax.dynamic_update_slice(empty, shard.astype(BF16)[None], (chip, 0, 0))


class _WeightGather:
    N_SEM = 8

    def __init__(self, gathered, kinds, send_sems, recv_sems):
        self.gathered, self.kinds = gathered, kinds
        self.send_sems, self.recv_sems = send_sems, recv_sems
        self.x, self.y, self.c = _my_pos()
        self.me, self.sibling = (self.x, self.y, self.c), (self.x, self.y, 1 - self.c)
        self.nbr = ((1 - self.x, self.y), (self.x, 1 - self.y))
        self.diag = 2 * (1 - self.x) + (1 - self.y)

    def _dst(self, a, chip, pc, quarter=None):
        r, cc = self._shard_shape(a)
        h = r // 2
        row0, rows = pc * h, h
        if quarter is not None:
            row0, rows = pc * h + quarter * (h // 2), h // 2
        g = self.gathered[a]
        if self.kinds[a] == "blk":
            return g.at[chip, pl.ds(row0, rows), :]
        return g.at[pl.ds(row0, rows), pl.ds(chip * cc, cc)]

    def _copy(self, a, k, region, to):
        return pltpu.make_async_remote_copy(
            src_ref=region, dst_ref=region, send_sem=self.send_sems.at[a * self.N_SEM + k],
            recv_sem=self.recv_sems.at[a * self.N_SEM + k], device_id=to, device_id_type=MESH)

    def _arrays(self):
        return range(len(self.gathered))

    def _shard_shape(self, a):
        shape = self.gathered[a].shape
        return shape[1:] if self.kinds[a] == "blk" else (shape[0], shape[1] // N_CHIPS)

    def start(self):
        my_chip = 2 * self.x + self.y
        for a in self._arrays():
            for j, chip in enumerate(self.nbr):
                self._copy(a, j, self._dst(a, my_chip, self.c), (*chip, self.c)).start()

    def forward(self):
        for a in self._arrays():
            for j, chip in enumerate(self.nbr):
                cj = 2 * chip[0] + chip[1]
                half = self._dst(a, cj, self.c)
                self._copy(a, j, half, self.me).wait_recv()
                self._copy(a, 2 + j, half, self.sibling).start()
                other = self.nbr[1 - j]
                self._copy(a, 4 + j, self._dst(a, cj, self.c, quarter=j), (*other, self.c)).start()

    def forward_diagonal(self):
        for a in self._arrays():
            for j in range(2):
                quarter = self._dst(a, self.diag, self.c, quarter=j)
                self._copy(a, 4 + j, quarter, self.me).wait_recv()
                self._copy(a, 6 + j, quarter, self.sibling).start()

    def finish(self):
        for a in self._arrays():
            for j, chip in enumerate(self.nbr):
                self._copy(a, 2 + j, self._dst(a, 2 * chip[0] + chip[1], 1 - self.c), self.me).wait_recv()
                self._copy(a, 6 + j, self._dst(a, self.diag, 1 - self.c, quarter=j), self.me).wait_recv()
        for a in self._arrays():
            half = self._dst(a, self.diag, self.c)
            quarter = self._dst(a, self.diag, self.c, quarter=0)
            for k in range(self.N_SEM):
                self._copy(a, k, half if k < 4 else quarter, self.me).wait_send()

    @classmethod
    def sems(cls, n_arr):
        return [pltpu.SemaphoreType.DMA((n_arr * cls.N_SEM,)), pltpu.SemaphoreType.DMA((n_arr * cls.N_SEM,))]


def _half_of_full(ref, kind, pc):
    if kind == "blk":
        h = ref.shape[1] // 2
        return ref.at[:, pl.ds(pc * h, h), :]
    h = ref.shape[0] // 2
    return ref.at[pl.ds(pc * h, h), :]


def _half_shape(shape, kind):
    return (shape[0], shape[1] // 2, shape[2]) if kind == "blk" else (shape[0] // 2, shape[1])


class _HalfSwap:
    def __init__(self, ins, outs, kinds, send_sems, recv_sems):
        self.ins, self.outs, self.kinds = ins, outs, kinds
        self.send_sems, self.recv_sems = send_sems, recv_sems
        self.x, self.y, self.c = _my_pos()

    def _copies(self):
        for a in range(len(self.ins)):
            yield pltpu.make_async_remote_copy(
                src_ref=_half_of_full(self.ins[a], self.kinds[a], 1 - self.c), dst_ref=self.outs[a],
                send_sem=self.send_sems.at[a], recv_sem=self.recv_sems.at[a],
                device_id=(self.x, self.y, 1 - self.c), device_id_type=MESH)

    def start(self):
        for cp in self._copies():
            cp.start()

    def wait(self):
        for cp in self._copies():
            cp.wait()

    @staticmethod
    def sems(n_arr):
        return [pltpu.SemaphoreType.DMA((n_arr,)), pltpu.SemaphoreType.DMA((n_arr,))]

    @staticmethod
    def out_shapes(fulls, kinds):
        return [jax.ShapeDtypeStruct(_half_shape(a.shape, k), a.dtype) for a, k in zip(fulls, kinds)]


def _swap_halves(fulls_bf16, kinds, name):
    n_arr = len(fulls_bf16)

    def body(*refs):
        swap = _HalfSwap(refs[:n_arr], refs[n_arr:2 * n_arr], kinds, *refs[2 * n_arr:])
        swap.start()
        swap.wait()

    return pl.pallas_call(
        body, name=name, out_shape=_HalfSwap.out_shapes(fulls_bf16, kinds),
        in_specs=[ANY] * n_arr, out_specs=[ANY] * n_arr, scratch_shapes=_HalfSwap.sems(n_arr),
    )(*fulls_bf16)


def _add_halves(full, got, kind, name):
    hs = _half_shape(full.shape, kind)

    def body(pos_ref, a_ref, b_ref, o_ref, ob_ref):
        p = a_ref[...] + b_ref[...].astype(F32)
        ob_ref[...] = p.astype(BF16)

        @pl.when(pl.program_id(0) == pos_ref[1])
        def _():
            o_ref[...] = p.reshape(o_ref.shape)

    if kind == "blk":
        nb, h, cc = hs
        own = pl.BlockSpec((1, h, cc), lambda b, pos_ref: (b, pos_ref[0], 0))
        other = pl.BlockSpec((1, h, cc), lambda b, pos_ref: (b, 0, 0))
    else:
        h, cc = hs[0], hs[1] // N_CHIPS
        own = pl.BlockSpec((h, cc), lambda b, pos_ref: (pos_ref[0], b))
        other = pl.BlockSpec((h, cc), lambda b, pos_ref: (0, b))
    pos = jnp.concatenate([_core_index_scalar(), _chip_index_scalar()])
    return pl.pallas_call(
        body, name=name, out_shape=(jax.ShapeDtypeStruct((h, cc), F32), jax.ShapeDtypeStruct(hs, BF16)),
        grid_spec=pltpu.PrefetchScalarGridSpec(
            num_scalar_prefetch=1, grid=(N_CHIPS,), in_specs=[own, other],
            out_specs=(pl.BlockSpec((h, cc), lambda b, pos_ref: (0, 0)), other)),
        compiler_params=_params(("arbitrary",)),
    )(pos, full, got)


def _rx_shape(part_shape, kind):
    if kind == "blk":
        return (3, part_shape[1], part_shape[2])
    return (3, part_shape[0], part_shape[1] // N_CHIPS)


class _ChipExchange:
    def __init__(self, parts, rxs, kinds, send_sems, recv_sems):
        self.parts, self.rxs, self.kinds = parts, rxs, kinds
        self.send_sems, self.recv_sems = send_sems, recv_sems
        self.x, self.y, self.c = _my_pos()
        self.chips = _other_chips(self.x, self.y)

    def _copies(self):
        for a in range(len(self.parts)):
            for j, chip in enumerate(self.chips):
                cj = 2 * chip[0] + chip[1]
                if self.kinds[a] == "blk":
                    src = self.parts[a].at[cj]
                else:
                    cc = self.parts[a].shape[1] // N_CHIPS
                    src = self.parts[a].at[:, pl.ds(cj * cc, cc)]
                yield pltpu.make_async_remote_copy(
                    src_ref=src, dst_ref=self.rxs[a].at[j], send_sem=self.send_sems.at[a * 3 + j],
                    recv_sem=self.recv_sems.at[a * 3 + j], device_id=(*chip, self.c), device_id_type=MESH)

    def start(self):
        for cp in self._copies():
            cp.start()

    def wait(self):
        for cp in self._copies():
            cp.wait_recv()
        for cp in self._copies():
            cp.wait_send()

    @staticmethod
    def sems(n_arr):
        return [pltpu.SemaphoreType.DMA((n_arr * 3,)), pltpu.SemaphoreType.DMA((n_arr * 3,))]


HBM_SPEC = pl.BlockSpec(memory_space=pltpu.HBM)
SEM_SPEC = pl.BlockSpec(memory_space=pltpu.SEMAPHORE)
DATAFLOW = pltpu.SideEffectType.DATAFLOW_SIDE_EFFECTING


def _exchange_start(part, name):
    rx_shape = _rx_shape(part.shape, "blk")

    def body(part_ref, rx_ref, send_sems, recv_sems, part_thru, rx_thru, token):
        _ChipExchange([part_ref], [rx_ref], ["blk"], send_sems, recv_sems).start()
        token[...] = jnp.zeros_like(token)

    return pl.pallas_call(
        body, name=name,
        out_shape=(pltpu.SemaphoreType.DMA((3,)), pltpu.SemaphoreType.DMA((3,)), pltpu.HBM(part.shape, part.dtype),
                   pltpu.HBM(rx_shape, BF16), jax.ShapeDtypeStruct((8, LANES), F32)),
        in_specs=(HBM_SPEC, HBM_SPEC), out_specs=(SEM_SPEC, SEM_SPEC, HBM_SPEC, HBM_SPEC, VMEM_WHOLE),
        input_output_aliases={0: 2, 1: 3}, compiler_params=pltpu.CompilerParams(has_side_effects=DATAFLOW),
    )(pltpu.with_memory_space_constraint(part, pltpu.HBM),
      pltpu.with_memory_space_constraint(lax.empty(rx_shape, BF16), pltpu.HBM))


def _exchange_wait(send_sems, recv_sems, part_thru, rx_thru, after, name):
    def body(part_ref, rx_ref, send_sems, recv_sems, after_ref, part_dead, rx_out):
        _ChipExchange([part_ref], [rx_ref], ["blk"], send_sems, recv_sems).wait()

    return pl.pallas_call(
        body, name=name,
        out_shape=(pltpu.HBM(part_thru.shape, part_thru.dtype), pltpu.HBM(rx_thru.shape, rx_thru.dtype)),
        in_specs=(HBM_SPEC, HBM_SPEC, SEM_SPEC, SEM_SPEC, ANY), out_specs=(HBM_SPEC, HBM_SPEC),
        input_output_aliases={0: 0, 1: 1}, compiler_params=pltpu.CompilerParams(has_side_effects=DATAFLOW),
    )(part_thru, rx_thru, send_sems, recv_sems, after)[1]


def _sum_chips(part, rx, tr, name, after):
    _, h, cc = rx.shape
    flips = (2, 1, 3)

    def body(chip_ref, p_ref, rx_ref, after_ref, o_ref):
        own = p_ref[...]
        for mc in range(N_CHIPS):
            @pl.when(chip_ref[0] == mc)
            def _():
                terms = sorted([(mc, None)] + [(mc ^ f, j) for j, f in enumerate(flips)])
                acc = None
                for _, j in terms:
                    t = own if j is None else rx_ref[j].astype(F32)
                    acc = t if acc is None else acc + t
                o_ref[...] = acc

    return pl.pallas_call(
        body, name=name, out_shape=jax.ShapeDtypeStruct((h, cc), F32),
        grid_spec=pltpu.PrefetchScalarGridSpec(
            num_scalar_prefetch=1, grid=(h // tr,),
            in_specs=[pl.BlockSpec((tr, cc), lambda i, chip_ref: (i, 0)),
                      pl.BlockSpec((3, tr, cc), lambda i, chip_ref: (0, i, 0)), ANY],
            out_specs=pl.BlockSpec((tr, cc), lambda i, chip_ref: (i, 0))),
        compiler_params=_params(("arbitrary",)),
    )(_chip_index_scalar(), part, rx, after)


def _share_halves(halves, name):
    n_arr = len(halves)

    def body(*refs):
        ins, outs = refs[:n_arr], refs[n_arr:2 * n_arr]
        send_sems, recv_sems = refs[2 * n_arr:]
        x, y, c = _my_pos()
        cps = []
        for a in range(n_arr):
            cp = pltpu.make_async_remote_copy(
                src_ref=ins[a], dst_ref=outs[a], send_sem=send_sems.at[a], recv_sem=recv_sems.at[a],
                device_id=(x, y, 1 - c), device_id_type=MESH)
            cp.start()
            cps.append(cp)
        for cp in cps:
            cp.wait()

    return pl.pallas_call(
        body, name=name, out_shape=[jax.ShapeDtypeStruct(h.shape, h.dtype) for h in halves],
        in_specs=[ANY] * n_arr, out_specs=[ANY] * n_arr,
        scratch_shapes=[pltpu.SemaphoreType.DMA((n_arr,)), pltpu.SemaphoreType.DMA((n_arr,))],
    )(*halves)


def _bucket_table():
    qi = jnp.arange(BLOCK)[:, None]
    si = jnp.arange(2 * BLOCK)[None, :]
    dist = qi + BLOCK - si
    max_exact = N_BUCKETS // 2
    n = jnp.maximum(dist, 0)
    nf = jnp.maximum(n, max_exact).astype(F32)
    large = max_exact + (jnp.log(nf / max_exact) / math.log(MAX_DISTANCE / max_exact)
                         * (N_BUCKETS - max_exact)).astype(jnp.int32)
    large = jnp.minimum(large, N_BUCKETS - 1)
    return jnp.where(n < max_exact, n, large).astype(F32)


def _prep_tables(bucket, rel_bias_t, w_s):
    def body(bucket_ref, rb_ref, ws_ref, bias_ref, wsm_ref):
        qi = lax.broadcasted_iota(jnp.int32, (BLOCK, 2 * BLOCK), 0)
        si = lax.broadcasted_iota(jnp.int32, (BLOCK, 2 * BLOCK), 1)
        dist = qi + BLOCK - si
        in_window = (dist >= 0) & (dist < BLOCK)
        bk = bucket_ref[...]
        for h in range(N_HEADS):
            acc = jnp.zeros((BLOCK, 2 * BLOCK), F32)
            for b in range(N_BUCKETS):
                acc = jnp.where(bk == float(b), rb_ref[h, b], acc)
            bias_ref[h] = jnp.where(in_window, acc, NEG_INF)
        ti = lax.broadcasted_iota(jnp.int32, (BLOCK, BLOCK), 0)
        ui = lax.broadcasted_iota(jnp.int32, (BLOCK, BLOCK), 1)
        for g in range(N_GROUPS):
            wsm_ref[g] = jnp.where(ti >= ui, ws_ref[g], 0.0).astype(BF16)

    return pl.pallas_call(
        body, name="prep_tables",
        out_shape=(jax.ShapeDtypeStruct((N_HEADS, BLOCK, 2 * BLOCK), F32),
                   jax.ShapeDtypeStruct((N_GROUPS, BLOCK, BLOCK), BF16)),
        grid=(1,),
        in_specs=[_const_spec((BLOCK, 2 * BLOCK)), pl.BlockSpec(memory_space=pltpu.SMEM),
                  _const_spec((N_GROUPS, BLOCK, BLOCK))],
        out_specs=(_const_spec((N_HEADS, BLOCK, 2 * BLOCK)), _const_spec((N_GROUPS, BLOCK, BLOCK))),
        compiler_params=_params(("arbitrary",)),
    )(bucket, rel_bias_t, w_s)


def _fwd_in(x, modr, w_in, b_in, tm, shards, kinds):
    s = x.shape[0]
    n_steps = s // tm
    fwd_step, diag_step = (8 * n_steps) // 16, (13 * n_steps) // 16
    n_w = len(shards)

    def body(x_ref, mod_ref, w_ref, b_ref, *rest):
        h1_ref, q_ref, kv_ref, gu_ref, gv_ref, xb_ref = rest[n_w:n_w + 6]
        gathered_refs = rest[n_w + 6:2 * n_w + 6]
        send_sems, recv_sems = rest[2 * n_w + 6:]
        i = pl.program_id(0)
        gather = _WeightGather(gathered_refs, kinds, send_sems, recv_sems)

        @pl.when(i == 0)
        def _():
            gather.start()

        xv = x_ref[...]
        xb_ref[...] = xv.astype(BF16)
        h1 = (xv * (1.0 + mod_ref[1:2, :]) + mod_ref[0:1, :]).astype(BF16)
        h1_ref[...] = h1
        proj = _dot_nt(h1, w_ref[...]) + b_ref[...]
        q_ref[...] = (proj[:, :ATTN_W] * Q_SCALE).astype(BF16)
        kv_ref[...] = proj[:, ATTN_W:ATTN_W + 2 * KV_W].astype(BF16)
        gu_ref[...] = proj[:, ATTN_W + 2 * KV_W:ATTN_W + 2 * KV_W + GMLP_W]
        gv_ref[...] = proj[:, ATTN_W + 2 * KV_W + GMLP_W:]

        @pl.when(i == fwd_step)
        def _():
            gather.forward()

        @pl.when(i == diag_step)
        def _():
            gather.forward_diagonal()

        @pl.when(i == n_steps - 1)
        def _():
            gather.finish()

    row = lambda w: pl.BlockSpec((tm, w), lambda i: (i, 0))
    outs = pl.pallas_call(
        body, name="fwd_in",
        out_shape=[jax.ShapeDtypeStruct((s, D_MODEL), BF16), jax.ShapeDtypeStruct((s, ATTN_W), BF16),
                   jax.ShapeDtypeStruct((s, 2 * KV_W), BF16), jax.ShapeDtypeStruct((s, GMLP_W), F32),
                   jax.ShapeDtypeStruct((s, GMLP_W), F32), jax.ShapeDtypeStruct((s, D_MODEL), BF16)]
        + [jax.ShapeDtypeStruct(sh.shape, BF16) for sh in shards],
        grid=(n_steps,),
        in_specs=[row(D_MODEL), _const_spec((8, D_MODEL)), _const_spec(w_in.shape), _const_spec((1, IN_W))]
        + [ANY] * n_w,
        out_specs=[row(D_MODEL), row(ATTN_W), row(2 * KV_W), row(GMLP_W), row(GMLP_W), row(D_MODEL)] + [ANY] * n_w,
        input_output_aliases={4 + a: 6 + a for a in range(n_w)},
        scratch_shapes=_WeightGather.sems(n_w),
        compiler_params=_params(("arbitrary",)),
    )(x, modr, w_in, b_in, *shards)
    return outs[:6], outs[6:]


def _kv_variants(kk):
    kf = kk.astype(F32)
    lane = lax.broadcasted_iota(jnp.int32, kf.shape, 1)
    low = lane < HEAD_DIM
    k0_lo = jnp.where(low, kf, 0.0)
    k1_hi = jnp.where(low, 0.0, kf)
    k0_hi = pltpu.roll(k0_lo, HEAD_DIM, 1)
    k1_lo = pltpu.roll(k1_hi, HEAD_DIM, 1)
    return ((k0_lo.astype(BF16), k0_hi.astype(BF16)), (k1_lo.astype(BF16), k1_hi.astype(BF16)))


def _head_kv(h):
    return h // (N_HEADS // N_KV), h % 2


MIX_GROUP = 2


def _interleave(*gens):
    results = [None] * len(gens)
    active = list(enumerate(gens))
    while active:
        still = []
        for i, g in active:
            try:
                next(g)
                still.append((i, g))
            except StopIteration as done:
                results[i] = done.value
        active = still
    return results


def _attn_block_fwd(q_blk, kk, vv, bias_ref, sinks_ref, first_mask):
    kvar = _kv_variants(kk)
    vvar = _kv_variants(vv)
    heads = range(N_HEADS)
    q_pairs = [q_blk[:, (h // 2) * LANES:(h // 2 + 1) * LANES] for h in heads]
    logits = [_dot_nt(q_pairs[h], kvar[_head_kv(h)[0]][_head_kv(h)[1]]) + bias_ref[h] for h in heads]
    if first_mask is not None:
        logits = [jnp.where(first_mask, NEG_INF, lg) for lg in logits]
    yield
    ms = [jnp.maximum(jnp.max(logits[h], axis=-1, keepdims=True), sinks_ref[h]) for h in heads]
    yield
    es = [jnp.exp(logits[h] - ms[h]) for h in heads]
    ess = [jnp.exp(sinks_ref[h] - ms[h]) for h in heads]
    yield
    invs = [1.0 / (jnp.sum(es[h], axis=-1, keepdims=True) + ess[h]) for h in heads]
    probs = [(es[h] * invs[h], ess[h] * invs[h]) for h in heads]
    yield
    outs = [_dot(probs[h][0].astype(BF16), vvar[_head_kv(h)[0]][_head_kv(h)[1]]) for h in heads]
    pairs = [outs[2 * i] + outs[2 * i + 1] for i in range(N_HEADS // 2)]
    return jnp.concatenate(pairs, axis=1), probs, kvar, vvar


def _gmlp_chunk_fwd(gu, gv, ln_g, ln_b, wsm_ref, bsx, amat):
    u, tu = _gelu(gu)
    a, ta = _gelu(gv)
    yield
    mean = _split_dot(a, amat)
    d = a - mean
    yield
    var = _split_dot(d * d, amat)
    yield
    rstd = lax.rsqrt(var + LN_EPS)
    xhat = d * rstd
    vb = (xhat * ln_g + ln_b).astype(BF16)
    yield
    lane = lax.broadcasted_iota(jnp.int32, (BLOCK, LANES), 1)
    low = lane < GROUP_DIM
    cols = []
    for pair in range(N_GROUPS // 2):
        vp = vb[:, pair * LANES:(pair + 1) * LANES]
        cols.append(jnp.where(low, _dot(wsm_ref[2 * pair], vp), _dot(wsm_ref[2 * pair + 1], vp)))
    mixedv = jnp.concatenate(cols, axis=1) + bsx
    return u * mixedv, (u, tu, ta, xhat, rstd, vb, mixedv)


def _rms(a, g):
    r = lax.rsqrt(jnp.mean(a * a, axis=-1, keepdims=True) + LN_EPS)
    return a * r * g, r


def _fwd_mix(q, kv, gu, gv, x, modr, bias, sinks, gln_g, gln_b, wsm, bsx, amat, aog, gog, w_out, ln1_g, ln1_b, tm,
             ffn_shards, ffn_kinds):
    s = x.shape[0]
    nb = tm // BLOCK
    n_steps = s // tm
    fwd_step, diag_step = (7 * n_steps) // 16, (12 * n_steps) // 16
    n_w = len(ffn_shards)

    def body(q_ref, kv_ref, kvp_ref, gu_ref, gv_ref, x_ref, mod_ref, bias_ref, sinks_ref, glng_ref, glnb_ref, wsm_ref,
             bsx_ref, amat_ref, aog_ref, gog_ref, wout_ref, ln1g_ref, ln1b_ref, *rest):
        x1_ref, x1b_ref, y_ref, mixed_ref = rest[n_w:n_w + 4]
        gathered_refs = rest[n_w + 4:2 * n_w + 4]
        mix_scr, send_sems, recv_sems = rest[2 * n_w + 4:]
        i = pl.program_id(0)
        gather = _WeightGather(gathered_refs, ffn_kinds, send_sems, recv_sems)

        @pl.when(i == 0)
        def _():
            gather.start()

        col = lax.broadcasted_iota(jnp.int32, (BLOCK, 2 * BLOCK), 1)
        for b0 in range(0, nb, MIX_GROUP):
            gens = []
            for b in range(b0, min(b0 + MIX_GROUP, nb)):
                r0 = b * BLOCK
                if b == 0:
                    kvprev = kvp_ref[...]
                    first_mask = (col < BLOCK) & (i == 0)
                else:
                    kvprev = kv_ref[r0 - BLOCK:r0, :]
                    first_mask = None
                kvcur = kv_ref[r0:r0 + BLOCK, :]
                kk = jnp.concatenate([kvprev[:, :KV_W], kvcur[:, :KV_W]], axis=0)
                vv = jnp.concatenate([kvprev[:, KV_W:], kvcur[:, KV_W:]], axis=0)
                gens.append(_attn_block_fwd(q_ref[r0:r0 + BLOCK, :], kk, vv, bias_ref, sinks_ref, first_mask))
                gens.append(_gmlp_chunk_fwd(gu_ref[r0:r0 + BLOCK, :], gv_ref[r0:r0 + BLOCK, :], glng_ref[...],
                                            glnb_ref[...], wsm_ref, bsx_ref[...], amat_ref[...]))
            res = _interleave(*gens)
            for k, b in enumerate(range(b0, min(b0 + MIX_GROUP, nb))):
                r0 = b * BLOCK
                na, _ = _rms(res[2 * k][0], aog_ref[...])
                ng, _ = _rms(res[2 * k + 1][0], gog_ref[...])
                mix_scr[r0:r0 + BLOCK, :ATTN_W] = na.astype(BF16)
                mix_scr[r0:r0 + BLOCK, ATTN_W:] = ng.astype(BF16)
        mixed = mix_scr[...]
        mixed_ref[...] = mixed
        y = _dot(mixed, wout_ref[...])
        y_ref[...] = y.astype(BF16)
        z1 = ALPHA * x_ref[...] + mod_ref[2:3, :] * y
        xhat, _ = _ln_stats(z1)
        x1 = xhat * ln1g_ref[...] + ln1b_ref[...]
        x1_ref[...] = x1
        x1b_ref[...] = x1.astype(BF16)

        @pl.when(i == fwd_step)
        def _():
            gather.forward()

        @pl.when(i == diag_step)
        def _():
            gather.forward_diagonal()

        @pl.when(i == n_steps - 1)
        def _():
            gather.finish()

    row = lambda w: pl.BlockSpec((tm, w), lambda i: (i, 0))
    prev = pl.BlockSpec((BLOCK, 2 * KV_W), lambda i: (jnp.maximum(i * nb - 1, 0), 0))
    outs = pl.pallas_call(
        body, name="fwd_mix",
        out_shape=[jax.ShapeDtypeStruct((s, D_MODEL), F32)] + [jax.ShapeDtypeStruct((s, D_MODEL), BF16)] * 3
        + [jax.ShapeDtypeStruct(sh.shape, BF16) for sh in ffn_shards],
        grid=(n_steps,),
        in_specs=[row(ATTN_W), row(2 * KV_W), prev, row(GMLP_W), row(GMLP_W), row(D_MODEL), _const_spec((8, D_MODEL)),
                  _const_spec((N_HEADS, BLOCK, 2 * BLOCK)), pl.BlockSpec(memory_space=pltpu.SMEM),
                  _const_spec((1, GMLP_W)), _const_spec((1, GMLP_W)), _const_spec((N_GROUPS, BLOCK, BLOCK)),
                  _const_spec((BLOCK, GMLP_W)), _const_spec((GMLP_W, GMLP_W)), _const_spec((1, ATTN_W)),
                  _const_spec((1, GMLP_W)), _const_spec((D_MODEL, D_MODEL)), _const_spec((1, D_MODEL)),
                  _const_spec((1, D_MODEL))] + [ANY] * n_w,
        out_specs=[row(D_MODEL)] * 4 + [ANY] * n_w,
        input_output_aliases={19 + a: 4 + a for a in range(n_w)},
        scratch_shapes=[pltpu.VMEM((tm, D_MODEL), BF16)] + _WeightGather.sems(n_w),
        compiler_params=_params(("arbitrary",)),
    )(q, kv, kv, gu, gv, x, modr, bias, sinks, gln_g, gln_b, wsm, bsx, amat, aog, gog, w_out, ln1_g, ln1_b, *ffn_shards)
    return outs[:4], outs[4:]


FF_BLOCKS = N_CHIPS // 2
FF_CHUNK = D_FF // FF_BLOCKS
FFN_SUB = 256


def _sigmoid(x):
    return 1.0 / (1.0 + jnp.exp(-x))


def _fwd_ffn(x1, target, modr, ln2_g, ln2_b, w_gu, w_dn, tm):
    s = x1.shape[0]

    def body(x1_ref, t_ref, mod_ref, g_ref, b_ref, wgu_ref, wdn_ref, h2_ref, act_ref, dy2_ref, dx1a_ref, acc_ref):
        @pl.when(pl.program_id(0) == 0)
        def _():
            acc_ref[...] = jnp.zeros_like(acc_ref)

        x1v = x1_ref[...]
        h2 = (x1v * (1.0 + mod_ref[4:5, :]) + mod_ref[3:4, :]).astype(BF16)
        h2_ref[...] = h2
        y2 = None
        for cc in range(FF_BLOCKS):
            c0 = cc * FF_CHUNK
            gate = _dot(h2, wgu_ref[cc])
            up = _dot(h2, wgu_ref[FF_BLOCKS + cc])
            act_ref[:, c0:c0 + FF_CHUNK] = gate.astype(BF16)
            act_ref[:, D_FF + c0:D_FF + c0 + FF_CHUNK] = up.astype(BF16)
            a = (gate * _sigmoid(gate) * up).astype(BF16)
            part = _dot(a, wdn_ref[c0:c0 + FF_CHUNK, :])
            y2 = part if y2 is None else y2 + part
        g2 = mod_ref[5:6, :]
        z2 = ALPHA * x1v + g2 * y2
        xhat, rstd = _ln_stats(z2)
        gain = g_ref[...]
        diff = xhat * gain + b_ref[...] - t_ref[...]
        dx2 = diff * (1.0 / D_MODEL)
        dz2 = _ln_bwd(dx2 * gain, xhat, rstd)
        dx1a_ref[...] = ALPHA * dz2
        dy2_ref[...] = (g2 * dz2).astype(BF16)
        acc_ref[0:1, :] += _colsum(diff * diff)
        acc_ref[1:2, :] += _colsum(dx2 * xhat)
        acc_ref[2:3, :] += _colsum(dx2)
        acc_ref[3:4, :] += _colsum(dz2 * y2)

    row = lambda w: pl.BlockSpec((tm, w), lambda i: (i, 0))
    return pl.pallas_call(
        body, name="fwd_ffn",
        out_shape=(jax.ShapeDtypeStruct((s, D_MODEL), BF16), jax.ShapeDtypeStruct((s, 2 * D_FF), BF16),
                   jax.ShapeDtypeStruct((s, D_MODEL), BF16), jax.ShapeDtypeStruct((s, D_MODEL), F32),
                   jax.ShapeDtypeStruct((8, D_MODEL), F32)),
        grid=(s // tm,),
        in_specs=[row(D_MODEL), row(D_MODEL), _const_spec((8, D_MODEL)), _const_spec((1, D_MODEL)),
                  _const_spec((1, D_MODEL)), _const_spec((N_CHIPS, D_MODEL, FF_CHUNK), single=True),
                  _const_spec((D_FF, D_MODEL), single=True)],
        out_specs=(row(D_MODEL), row(2 * D_FF), row(D_MODEL), row(D_MODEL), _const_spec((8, D_MODEL))),
        compiler_params=_params(("arbitrary",)),
    )(x1, target, modr, ln2_g, ln2_b, w_gu, w_dn)


def _bwd_ffn(dy2, act, w_gu, w_dn, tm):
    s = dy2.shape[0]

    def body(dy2_ref, act_ref, wgu_ref, wdn_ref, a_ref, dgu_ref, dh2_ref):
        dy2v = dy2_ref[...]
        dh2 = None
        for cc in range(FF_BLOCKS):
            c0 = cc * FF_CHUNK
            da = _dot_nt(dy2v, wdn_ref[c0:c0 + FF_CHUNK, :])
            gate = act_ref[:, c0:c0 + FF_CHUNK].astype(F32)
            up = act_ref[:, D_FF + c0:D_FF + c0 + FF_CHUNK].astype(F32)
            sg = _sigmoid(gate)
            sl = gate * sg
            a_ref[:, c0:c0 + FF_CHUNK] = (sl * up).astype(BF16)
            dgate = (da * up * (sg * (1.0 + gate * (1.0 - sg)))).astype(BF16)
            dup = (da * sl).astype(BF16)
            dgu_ref[:, c0:c0 + FF_CHUNK] = dgate
            dgu_ref[:, D_FF + c0:D_FF + c0 + FF_CHUNK] = dup
            part = _dot_nt(dgate, wgu_ref[cc]) + _dot_nt(dup, wgu_ref[FF_BLOCKS + cc])
            dh2 = part if dh2 is None else dh2 + part
        dh2_ref[...] = dh2.astype(BF16)

    row = lambda w: pl.BlockSpec((tm, w), lambda i: (i, 0))
    return pl.pallas_call(
        body, name="bwd_ffn",
        out_shape=(jax.ShapeDtypeStruct((s, D_FF), BF16), jax.ShapeDtypeStruct((s, 2 * D_FF), BF16),
                   jax.ShapeDtypeStruct((s, D_MODEL), BF16)),
        grid=(s // tm,),
        in_specs=[row(D_MODEL), row(2 * D_FF), _const_spec((N_CHIPS, D_MODEL, FF_CHUNK), single=True),
                  _const_spec((D_FF, D_MODEL), single=True)],
        out_specs=(row(D_FF), row(2 * D_FF), row(D_MODEL)),
        compiler_params=_params(("parallel",)),
    )(dy2, act, w_gu, w_dn)


def _bwd_mid(dh2, dx1a, x1, x, y, modr, ln1_g, w_out, tm, swap_fulls, swap_kinds):
    s = x.shape[0]
    n_steps = s // tm
    n_g = len(swap_fulls)

    def body(dh2_ref, dx1a_ref, x1_ref, x_ref, y_ref, mod_ref, g_ref, wout_ref, *rest):
        full_refs = rest[:n_g]
        dxa_ref, dy_ref, dmix_ref, acc_ref = rest[n_g:n_g + 4]
        got_refs = rest[n_g + 4:2 * n_g + 4]
        swap = _HalfSwap(full_refs, got_refs, swap_kinds, *rest[2 * n_g + 4:])
        i = pl.program_id(0)

        @pl.when(i == 0)
        def _():
            swap.start()
            acc_ref[...] = jnp.zeros_like(acc_ref)

        dh2 = dh2_ref[...].astype(F32)
        x1v = x1_ref[...].astype(F32)
        yv = y_ref[...].astype(F32)
        g1 = mod_ref[2:3, :]
        dx1 = dx1a_ref[...] + dh2 * (1.0 + mod_ref[4:5, :])
        z1 = ALPHA * x_ref[...] + g1 * yv
        xhat, rstd = _ln_stats(z1)
        dz1 = _ln_bwd(dx1 * g_ref[...], xhat, rstd)
        dxa_ref[...] = (ALPHA * dz1).astype(BF16)
        dy = (g1 * dz1).astype(BF16)
        dy_ref[...] = dy
        dmix_ref[...] = _dot_nt(dy, wout_ref[...]).astype(BF16)
        acc_ref[0:1, :] += _colsum(dh2 * x1v)
        acc_ref[1:2, :] += _colsum(dh2)
        acc_ref[2:3, :] += _colsum(dx1 * xhat)
        acc_ref[3:4, :] += _colsum(dx1)
        acc_ref[4:5, :] += _colsum(dz1 * yv)

        @pl.when(i == n_steps - 1)
        def _():
            swap.wait()

    row = lambda w: pl.BlockSpec((tm, w), lambda i: (i, 0))
    outs = pl.pallas_call(
        body, name="bwd_mid",
        out_shape=[jax.ShapeDtypeStruct((s, D_MODEL), BF16), jax.ShapeDtypeStruct((s, D_MODEL), BF16),
                   jax.ShapeDtypeStruct((s, D_MODEL), BF16), jax.ShapeDtypeStruct((8, D_MODEL), F32)]
        + _HalfSwap.out_shapes(swap_fulls, swap_kinds),
        grid=(n_steps,),
        in_specs=[row(D_MODEL)] * 5 + [_const_spec((8, D_MODEL)), _const_spec((1, D_MODEL)),
                                       _const_spec((D_MODEL, D_MODEL))] + [ANY] * n_g,
        out_specs=[row(D_MODEL), row(D_MODEL), row(D_MODEL), _const_spec((8, D_MODEL))] + [ANY] * n_g,
        scratch_shapes=_HalfSwap.sems(n_g),
        compiler_params=_params(("arbitrary",)),
    )(dh2, dx1a, x1, x, y, modr, ln1_g, w_out, *swap_fulls)
    return outs[:4], outs[4:]


def _fold_kv(t0, t1):
    lane = lax.broadcasted_iota(jnp.int32, t0.shape, 1)
    f0 = t0 + pltpu.roll(t0, HEAD_DIM, 1)
    f1 = t1 + pltpu.roll(t1, HEAD_DIM, 1)
    return jnp.where(lane < HEAD_DIM, f0, f1)


def _bwd_mix(q, kv, gu, gv, dmix, bias, sinks, gln_g, gln_b, wsm, bsx, amat, aog, gog, grad_parts, grad_kinds):
    s = q.shape[0]
    tile = 2 * BLOCK
    n_steps = s // tile
    n_g = len(grad_parts)

    def body(q_ref, kv_ref, kvp_ref, gu_ref, gv_ref, dmix_ref, bias_ref, sinks_ref, glng_ref, glnb_ref, wsm_ref,
             bsx_ref, amat_ref, aog_ref, gog_ref, *rest):
        part_refs = rest[:n_g]
        dq_ref, dkv_ref, dgu_ref, dgv_ref, gbias_ref, dws_ref, dbs_ref, vec_ref, dsink_ref = rest[n_g:n_g + 9]
        rx_refs = rest[n_g + 9:2 * n_g + 9]
        carry, done, send_sems, recv_sems = rest[2 * n_g + 9:]
        n = pl.program_id(0)
        exchange = _ChipExchange(part_refs, rx_refs, grad_kinds, send_sems, recv_sems)

        @pl.when(n == 0)
        def _():
            exchange.start()
            carry[...] = jnp.zeros_like(carry)
            done[...] = jnp.zeros_like(done)
            gbias_ref[...] = jnp.zeros_like(gbias_ref)
            dws_ref[...] = jnp.zeros_like(dws_ref)
            dbs_ref[...] = jnp.zeros_like(dbs_ref)
            vec_ref[...] = jnp.zeros_like(vec_ref)
            dsink_ref[...] = jnp.zeros_like(dsink_ref)

        @pl.when(n == n_steps)
        def _():
            dkv_ref[:BLOCK, :] = done[...].astype(BF16)
            dkv_ref[BLOCK:, :] = carry[...].astype(BF16)
            exchange.wait()

        @pl.when(n < n_steps)
        def _():
            col = lax.broadcasted_iota(jnp.int32, (BLOCK, 2 * BLOCK), 1)
            lane = lax.broadcasted_iota(jnp.int32, (BLOCK, LANES), 1)
            low = lane < HEAD_DIM
            rows = [slice(0, BLOCK), slice(BLOCK, tile)]
            kv_blocks = [kvp_ref[...], kv_ref[rows[0], :], kv_ref[rows[1], :]]
            masks = [(col < BLOCK) & (n == 0), None]
            q_blks = [q_ref[r, :] for r in rows]
            fwd = []
            for b in range(2):
                kk = jnp.concatenate([kv_blocks[b][:, :KV_W], kv_blocks[b + 1][:, :KV_W]], axis=0)
                vv = jnp.concatenate([kv_blocks[b][:, KV_W:], kv_blocks[b + 1][:, KV_W:]], axis=0)
                fwd.append(_attn_block_fwd(q_blks[b], kk, vv, bias_ref, sinks_ref, masks[b]))
                fwd.append(_gmlp_chunk_fwd(gu_ref[rows[b], :], gv_ref[rows[b], :], glng_ref[...], glnb_ref[...],
                                           wsm_ref, bsx_ref[...], amat_ref[...]))
            res = _interleave(*fwd[:2]) + _interleave(*fwd[2:])

            def gating_bwd(b, d_gm, saved):
                u, tu, ta, xhat, rstd, vb, mixedv = saved
                dgu_ref[rows[b], :] = (d_gm * mixedv * _gelu_grad(gu_ref[rows[b], :], tu)).astype(BF16)
                dmx = d_gm * u
                dmxb = dmx.astype(BF16)
                yield
                dvn_cols, dws = [], []
                for pair in range(N_GROUPS // 2):
                    dp_ = dmxb[:, pair * LANES:(pair + 1) * LANES]
                    vp = vb[:, pair * LANES:(pair + 1) * LANES]
                    dvn_cols.append(
                        jnp.where(low, _dot_tn(wsm_ref[2 * pair], dp_), _dot_tn(wsm_ref[2 * pair + 1], dp_)))
                    zero = jnp.zeros_like(dp_)
                    dws.append(_dot_nt(jnp.where(low, dp_, zero), vp))
                    dws.append(_dot_nt(jnp.where(low, zero, dp_), vp))
                dvn = jnp.concatenate(dvn_cols, axis=1)
                yield
                dxh = dvn * glng_ref[...]
                am = amat_ref[...]
                m1 = _split_dot(dxh, am)
                m2 = _split_dot(dxh * xhat, am)
                yield
                da = rstd * (dxh - m1 - xhat * m2)
                dgv_ref[rows[b], :] = (da * _gelu_grad(gv_ref[rows[b], :], ta)).astype(BF16)
                return dmx, dws, _colsum(dvn * xhat), _colsum(dvn)

            def attention_bwd(b, d_attn, probs, kvar, vvar):
                heads = range(N_HEADS)
                sels = [low if h % 2 == 0 else jnp.logical_not(low) for h in heads]
                pair_of = lambda a, h: a[:, (h // 2) * LANES:(h // 2 + 1) * LANES]
                do_hs = [jnp.where(sels[h], pair_of(d_attn, h), 0.0).astype(BF16) for h in heads]
                q_hs = [jnp.where(sels[h], pair_of(q_blks[b], h), jnp.zeros((BLOCK, LANES), BF16)) for h in heads]
                dps = [_dot_nt(do_hs[h], vvar[_head_kv(h)[0]][_head_kv(h)[1]]) for h in heads]
                yield
                deltas = [jnp.sum(probs[h][0] * dps[h], axis=-1, keepdims=True) for h in heads]
                yield
                dss = [probs[h][0] * (dps[h] - deltas[h]) for h in heads]
                dsinks = [-(probs[h][1] * deltas[h]) for h in heads]
                dsbs = [ds.astype(BF16) for ds in dss]
                pbs = [probs[h][0].astype(BF16) for h in heads]
                yield
                dqs = [_dot(dsbs[h], kvar[_head_kv(h)[0]][_head_kv(h)[1]]) for h in heads]
                tks = [_dot_tn(dsbs[h], q_hs[h]) for h in heads]
                tvs = [_dot_tn(pbs[h], do_hs[h]) for h in heads]
                dq_cols = [dqs[2 * i] + dqs[2 * i + 1] for i in range(N_HEADS // 2)]
                dq_ref[rows[b], :] = (jnp.concatenate(dq_cols, axis=1) * Q_SCALE).astype(BF16)
                per_kv = N_HEADS // N_KV
                kv_sum = lambda ts, kvh: sum(ts[kvh * per_kv + 1:(kvh + 1) * per_kv], ts[kvh * per_kv])
                dkk = _fold_kv(kv_sum(tks, 0), kv_sum(tks, 1))
                dvv = _fold_kv(kv_sum(tvs, 0), kv_sum(tvs, 1))
                return jnp.concatenate([dkk, dvv], axis=1), dss, dsinks

            bwd, rms_g = [], []
            for b in range(2):
                attn, probs, kvar, vvar = res[2 * b]
                gm, saved = res[2 * b + 1]
                na_unit, r_a = _rms(attn, 1.0)
                ng_unit, r_g = _rms(gm, 1.0)
                dmix = dmix_ref[rows[b], :].astype(F32)
                dn_a = dmix[:, :ATTN_W]
                dn_g = dmix[:, ATTN_W:]
                rms_g.append((_colsum(dn_a * na_unit), _colsum(dn_g * ng_unit)))
                t_a = dn_a * aog_ref[...]
                d_attn = r_a * t_a - na_unit * (r_a * jnp.mean(t_a * na_unit, axis=-1, keepdims=True))
                t_g = dn_g * gog_ref[...]
                d_gm = r_g * t_g - ng_unit * (r_g * jnp.mean(t_g * ng_unit, axis=-1, keepdims=True))
                bwd.append(attention_bwd(b, d_attn, probs, kvar, vvar))
                bwd.append(gating_bwd(b, d_gm, saved))
            (dkv_a, dss_a, dsk_a), (dmx_a, dws_a, glg_a, glb_a) = _interleave(*bwd[:2])
            (dkv_b, dss_b, dsk_b), (dmx_b, dws_b, glg_b, glb_b) = _interleave(*bwd[2:])

            vec_ref[0:1, :] += rms_g[0][0] + rms_g[1][0]
            vec_ref[1:2, :] += rms_g[0][1] + rms_g[1][1]
            vec_ref[2:3, :] += glg_a + glg_b
            vec_ref[3:4, :] += glb_a + glb_b
            dbs_ref[...] += dmx_a + dmx_b
            for g in range(N_GROUPS):
                dws_ref[g] += dws_a[g] + dws_b[g]
            for h in range(N_HEADS):
                gbias_ref[h] += dss_a[h] + dss_b[h]
                dsink_ref[h] += dsk_a[h] + dsk_b[h]

            dkv_ref[:BLOCK, :] = done[...].astype(BF16)
            dkv_ref[BLOCK:, :] = (carry[...] + dkv_a[:BLOCK]).astype(BF16)
            done[...] = dkv_a[BLOCK:] + dkv_b[:BLOCK]
            carry[...] = dkv_b[BLOCK:]

    last = n_steps - 1
    cur = lambda w: pl.BlockSpec((tile, w), lambda n: (jnp.minimum(n, last), 0))
    late = lambda w: pl.BlockSpec((tile, w), lambda n: (jnp.clip(n - 1, 0, last), 0))
    before = pl.BlockSpec((BLOCK, 2 * KV_W), lambda n: (jnp.clip(2 * n - 1, 0, 2 * last + 1), 0))
    outs = pl.pallas_call(
        body, name="bwd_mix",
        out_shape=[jax.ShapeDtypeStruct((s, ATTN_W), BF16), jax.ShapeDtypeStruct((s, 2 * KV_W), BF16),
                   jax.ShapeDtypeStruct((s, GMLP_W), BF16), jax.ShapeDtypeStruct((s, GMLP_W), BF16),
                   jax.ShapeDtypeStruct((N_HEADS, BLOCK, 2 * BLOCK), F32),
                   jax.ShapeDtypeStruct((N_GROUPS, BLOCK, BLOCK), F32),
                   jax.ShapeDtypeStruct((BLOCK, GMLP_W), F32), jax.ShapeDtypeStruct((8, GMLP_W), F32),
                   jax.ShapeDtypeStruct((N_HEADS, BLOCK, 1), F32)]
        + [jax.ShapeDtypeStruct(_rx_shape(p.shape, k), BF16) for p, k in zip(grad_parts, grad_kinds)],
        grid=(n_steps + 1,),
        in_specs=[cur(ATTN_W), cur(2 * KV_W), before, cur(GMLP_W), cur(GMLP_W), cur(D_MODEL),
                  _const_spec((N_HEADS, BLOCK, 2 * BLOCK)), pl.BlockSpec(memory_space=pltpu.SMEM),
                  _const_spec((1, GMLP_W)), _const_spec((1, GMLP_W)), _const_spec((N_GROUPS, BLOCK, BLOCK)),
                  _const_spec((BLOCK, GMLP_W)), _const_spec((GMLP_W, GMLP_W)), _const_spec((1, ATTN_W)),
                  _const_spec((1, GMLP_W))] + [ANY] * n_g,
        out_specs=[cur(ATTN_W), late(2 * KV_W), cur(GMLP_W), cur(GMLP_W),
                   _const_spec((N_HEADS, BLOCK, 2 * BLOCK)), _const_spec((N_GROUPS, BLOCK, BLOCK)),
                   _const_spec((BLOCK, GMLP_W)), _const_spec((8, GMLP_W)), _const_spec((N_HEADS, BLOCK, 1))]
        + [ANY] * n_g,
        scratch_shapes=[pltpu.VMEM((BLOCK, 2 * KV_W), F32), pltpu.VMEM((BLOCK, 2 * KV_W), F32)]
        + _ChipExchange.sems(n_g),
        compiler_params=_params(("arbitrary",)),
    )(q, kv, kv, gu, gv, dmix, bias, sinks, gln_g, gln_b, wsm, bsx, amat, aog, gog, *grad_parts)
    return outs[:9], outs[9:]


def _mix_finalize(gbias, bucket, dws, dbs, dsink):
    def body(gb_ref, bucket_ref, dws_ref, dbs_ref, dsink_ref, tall_ref):
        bk = bucket_ref[...]
        lane = lax.broadcasted_iota(jnp.int32, (N_BUCKETS, LANES), 1)
        rowi = lax.broadcasted_iota(jnp.int32, (N_BUCKETS, LANES), 0)
        drb = jnp.zeros((N_BUCKETS, LANES), F32)
        dsk = jnp.zeros((8, LANES), F32)
        lane8 = lax.broadcasted_iota(jnp.int32, (8, LANES), 1)
        for h in range(N_HEADS):
            g = gb_ref[h]
            for b in range(N_BUCKETS):
                tot = jnp.sum(_colsum(jnp.where(bk == float(b), g, 0.0)), axis=1, keepdims=True)
                drb = jnp.where((rowi == h) & (lane == b), tot, drb)
            sk = jnp.sum(dsink_ref[h], axis=0, keepdims=True)
            dsk = jnp.where(lane8 == h, sk, dsk)
        tall_ref[TALL_RB:TALL_RB + N_BUCKETS, :] = drb
        tall_ref[TALL_SK:TALL_SK + 8, :] = dsk
        ti = lax.broadcasted_iota(jnp.int32, (BLOCK, BLOCK), 0)
        ui = lax.broadcasted_iota(jnp.int32, (BLOCK, BLOCK), 1)
        for g in range(N_GROUPS):
            tall_ref[g * BLOCK:(g + 1) * BLOCK, :] = jnp.where(ti >= ui, dws_ref[g], 0.0)
        gi = lax.broadcasted_iota(jnp.int32, (GMLP_W, LANES), 0) // GROUP_DIM
        li = lax.broadcasted_iota(jnp.int32, (GMLP_W, LANES), 1)
        ind = jnp.where(gi == li, 1.0, 0.0).astype(BF16)
        d = dbs_ref[...]
        hi = d.astype(BF16)
        r1 = d - hi.astype(F32)
        mid = r1.astype(BF16)
        lo = (r1 - mid.astype(F32)).astype(BF16)
        dbsg = _dot(hi, ind) + _dot(mid, ind) + _dot(lo, ind)
        tall_ref[TALL_BS:TALL_BS + N_GROUPS, :] = dbsg.T[:N_GROUPS, :]

    return pl.pallas_call(
        body, name="mix_finalize", out_shape=jax.ShapeDtypeStruct((TALL_ROWS, LANES), F32), grid=(1,),
        in_specs=[_const_spec((N_HEADS, BLOCK, 2 * BLOCK)), _const_spec((BLOCK, 2 * BLOCK)),
                  _const_spec((N_GROUPS, BLOCK, BLOCK)), _const_spec((BLOCK, GMLP_W)),
                  _const_spec((N_HEADS, BLOCK, 1))],
        out_specs=_const_spec((TALL_ROWS, LANES)),
        compiler_params=_params(("arbitrary",)),
    )(gbias, bucket, dws, dbs, dsink)


def _bwd_in(dq, dkv, dgu, dgv, dxa, x, modr, w_in, tm):
    s = x.shape[0]

    def body(dq_ref, dkv_ref, dgu_ref, dgv_ref, dxa_ref, x_ref, mod_ref, w_ref, gx_ref, acc_ref, db_ref):
        @pl.when(pl.program_id(0) == 0)
        def _():
            acc_ref[...] = jnp.zeros_like(acc_ref)
            db_ref[...] = jnp.zeros_like(db_ref)

        dproj = jnp.concatenate([dq_ref[...], dkv_ref[...], dgu_ref[...], dgv_ref[...]], axis=1)
        dh1 = _dot(dproj, w_ref[...])
        gx_ref[...] = dxa_ref[...].astype(F32) + dh1 * (1.0 + mod_ref[1:2, :])
        acc_ref[0:1, :] += _colsum(dh1 * x_ref[...].astype(F32))
        acc_ref[1:2, :] += _colsum(dh1)
        db_ref[0:1, :] += _colsum(dproj.astype(F32))

    row = lambda w: pl.BlockSpec((tm, w), lambda i: (i, 0))
    return pl.pallas_call(
        body, name="bwd_in",
        out_shape=(jax.ShapeDtypeStruct((s, D_MODEL), F32), jax.ShapeDtypeStruct((8, D_MODEL), F32),
                   jax.ShapeDtypeStruct((8, IN_W), F32)),
        grid=(s // tm,),
        in_specs=[row(ATTN_W), row(2 * KV_W), row(GMLP_W), row(GMLP_W), row(D_MODEL), row(D_MODEL),
                  _const_spec((8, D_MODEL)), _const_spec(w_in.shape)],
        out_specs=(row(D_MODEL), _const_spec((8, D_MODEL)), _const_spec((8, IN_W))),
        compiler_params=_params(("arbitrary",)),
    )(dq, dkv, dgu, dgv, dxa, x, modr, w_in)


def _wgrad(a, bs, tm, tk, name, transposed=False, gather_vs=()):
    k_all, m = a.shape
    n = sum(b.shape[1] for b in bs)
    nk = k_all // tk
    nm = m // tm
    n_b = len(bs)
    n_v = len(gather_vs)

    def body(a_ref, *rest):
        b_refs, v_refs = rest[:n_b], rest[n_b:n_b + n_v]
        o_ref, ob_ref = rest[n_b + n_v:n_b + n_v + 2]
        vg_refs = rest[n_b + n_v + 2:n_b + 2 * n_v + 2]
        scratch = rest[n_b + 2 * n_v + 2:]
        acc_ref = scratch[-1] if transposed else o_ref
        i, k = pl.program_id(0), pl.program_id(1)
        if n_v:
            gather = _Gather8(v_refs, vg_refs, *scratch[:len(scratch) - transposed])

            @pl.when((i == 0) & (k == 0))
            def _():
                gather.start()

            @pl.when((i == nm - 1) & (k == 0))
            def _():
                gather.forward()

        @pl.when(k == 0)
        def _():
            acc_ref[...] = jnp.zeros_like(acc_ref)

        b = b_refs[0][...] if n_b == 1 else jnp.concatenate([r[...] for r in b_refs], axis=1)
        acc_ref[...] += _dot_tn(a_ref[...], b)

        @pl.when(k == nk - 1)
        def _():
            if transposed:
                o_ref[...] = acc_ref[...].T
            ob_ref[...] = o_ref[...].astype(BF16)

        if n_v:
            @pl.when((i == nm - 1) & (k == nk - 1))
            def _():
                gather.finish()

    if transposed:
        out_spec = pl.BlockSpec((n, tm), lambda i, k: (0, i))
        shape = (n, m)
    else:
        out_spec = pl.BlockSpec((tm, n), lambda i, k: (i, 0))
        shape = (m, n)
    outs = pl.pallas_call(
        body, name=name,
        out_shape=[jax.ShapeDtypeStruct(shape, F32), jax.ShapeDtypeStruct(shape, BF16)] + _gathered8_shapes(gather_vs),
        grid=(nm, nk),
        in_specs=[pl.BlockSpec((tk, tm), lambda i, k: (k, i))]
        + [pl.BlockSpec((tk, b.shape[1]), lambda i, k: (k, 0)) for b in bs] + [ANY] * n_v,
        out_specs=[out_spec, out_spec] + [ANY] * n_v,
        scratch_shapes=(_Gather8.sems(n_v) if n_v else []) + ([pltpu.VMEM((tm, n), F32)] if transposed else []),
        compiler_params=_params(("arbitrary", "arbitrary") if n_v else ("parallel", "arbitrary")),
    )(a, *bs, *gather_vs)
    return outs[0], outs[1], outs[2:]


def _adam_math(w, g, m, v):
    m2 = ADAM_B1 * m + (1.0 - ADAM_B1) * g
    v2 = ADAM_B2 * v + (1.0 - ADAM_B2) * (g * g)
    m_hat = m2 / (1.0 - ADAM_B1 ** ADAM_STEP)
    v_hat = v2 / (1.0 - ADAM_B2 ** ADAM_STEP)
    delta = -ADAM_LR * (m_hat / (jnp.sqrt(v_hat) + ADAM_EPS) + ADAM_WD * w)
    return delta, m2, v2


def _adam_halves(w, mine, got, m, v, tr, name):
    r, cc = w.shape
    h = r // 2
    nt = h // tr

    def body(c_ref, w_ref, mine_ref, got_ref, m_ref, v_ref, g_ref, d_ref, m2_ref, v2_ref):
        g = jnp.where(pl.program_id(0) == c_ref[0], mine_ref[...], got_ref[...])
        g_ref[...] = g
        d, m2, v2 = _adam_math(w_ref[...], g, m_ref[...], v_ref[...])
        d_ref[...] = d
        m2_ref[...] = m2
        v2_ref[...] = v2

    full = pl.BlockSpec((tr, cc), lambda hh, i, c_ref: (hh * nt + i, 0))
    half = pl.BlockSpec((tr, cc), lambda hh, i, c_ref: (i, 0))
    shp = jax.ShapeDtypeStruct((r, cc), F32)
    return pl.pallas_call(
        body, name=name, out_shape=(shp, shp, shp, shp),
        grid_spec=pltpu.PrefetchScalarGridSpec(
            num_scalar_prefetch=1, grid=(2, nt), in_specs=[full, half, half, full, full],
            out_specs=(full, full, full, full)),
        compiler_params=_params(("arbitrary", "arbitrary")),
    )(_core_index_scalar(), w, mine, got, m, v)


def _adam_w_ada(sc_t, dmod_all, w, m, v, tr):
    r, cc = w.shape

    def body(chip_ref, sct_ref, dm_ref, w_ref, m_ref, v_ref, g_ref, d_ref, m2_ref, v2_ref):
        g = sct_ref[:, 0:1] * dm_ref[0:1, :]
        for k in range(1, N_DEV):
            g = g + sct_ref[:, k:k + 1] * dm_ref[k:k + 1, :]
        g_ref[...] = g
        d, m2, v2 = _adam_math(w_ref[...], g, m_ref[...], v_ref[...])
        d_ref[...] = d
        m2_ref[...] = m2
        v2_ref[...] = v2

    spec = pl.BlockSpec((tr, cc), lambda i, chip_ref: (i, 0))
    shp = jax.ShapeDtypeStruct((r, cc), F32)
    return pl.pallas_call(
        body, name="adam_w_ada", out_shape=(shp, shp, shp, shp),
        grid_spec=pltpu.PrefetchScalarGridSpec(
            num_scalar_prefetch=1, grid=(r // tr,),
            in_specs=[pl.BlockSpec((tr, N_DEV), lambda i, chip_ref: (i, 0)),
                      pl.BlockSpec((N_DEV, cc), lambda i, chip_ref: (0, chip_ref[0])), spec, spec, spec],
            out_specs=(spec, spec, spec, spec)),
        compiler_params=_params(("parallel",)),
    )(_chip_index_scalar(), sc_t, dmod_all, w, m, v)


def _pack_wide(acc_i, acc_m, acc_f, db_in, vec):
    arrs = [acc_i, acc_m, acc_f, db_in, vec]
    i_, m_, f_, b_, v_ = range(5)
    src = {"b_in": (b_, 0), "ln1_g": (m_, 2), "ln1_b": (m_, 3), "ln2_g": (f_, 1), "ln2_b": (f_, 2),
           "gmlp_ln_g": (v_, 2), "gmlp_ln_b": (v_, 3), "attn_out_g": (v_, 0), "gmlp_out_g": (v_, 1), "loss": (f_, 0)}
    dmod = [(i_, 1), (i_, 0), (m_, 4), (m_, 1), (m_, 0), (f_, 3)]

    def body(*refs):
        ins, wide_ref = refs[:5], refs[5]
        wide_ref[...] = jnp.zeros_like(wide_ref)
        for k, (a, row) in enumerate(dmod):
            wide_ref[0:1, k * D_MODEL:(k + 1) * D_MODEL] = ins[a][row:row + 1, :]
        for name, (a, row) in src.items():
            r, off, n = WIDE_LAYOUT[name]
            wide_ref[r:r + 1, off:off + n] = ins[a][row:row + 1, :]

    return pl.pallas_call(
        body, name="pack_wide", out_shape=jax.ShapeDtypeStruct((8, WIDE_W), F32), grid=(1,),
        in_specs=[_const_spec(a.shape) for a in arrs], out_specs=_const_spec((8, WIDE_W)),
        compiler_params=_params(("arbitrary",)),
    )(*arrs)


def _adam_small(gw, gt, wide_wmv, w_s, b_s, rel_bias, sinks, after):
    names = list(WIDE_PARAMS)
    tall = [("gmlp_w_s", w_s), ("gmlp_b_s", b_s), ("rel_bias", rel_bias), ("attn_sinks", sinks)]
    ins = [gw, gt]
    for n in names:
        ins += list(wide_wmv[n])
    for _, t in tall:
        ins += list(t)
    n_in = len(ins)

    def body(*refs):
        gw_ref, gt_ref = refs[0], refs[1]
        wmv = refs[2:n_in]
        dmod_ref, loss_ref, loss1_ref = refs[n_in + 1:n_in + 4]
        outs = refs[n_in + 4:]

        def tall_sum(r0, nr):
            g = gt_ref[r0:r0 + nr, :]
            for d in range(1, N_DEV):
                g = g + gt_ref[d * TALL_ROWS + r0:d * TALL_ROWS + r0 + nr, :]
            return g

        def emit(k, g, w_ref, m_ref, v_ref):
            d, m2, v2 = _adam_math(w_ref[...], g, m_ref[...], v_ref[...])
            outs[4 * k][...] = g
            outs[4 * k + 1][...] = d
            outs[4 * k + 2][...] = m2
            outs[4 * k + 3][...] = v2

        gsum = gw_ref[0:8, :]
        for d in range(1, N_DEV):
            gsum = gsum + gw_ref[8 * d:8 * d + 8, :]
        for d in range(N_DEV):
            dmod_ref[d:d + 1, :] = gw_ref[8 * d:8 * d + 1, :]
        for k, n in enumerate(names):
            r, off, sz = WIDE_LAYOUT[n]
            emit(k, gsum[r:r + 1, off:off + sz], *wmv[3 * k:3 * k + 3])
        r, off, sz = WIDE_LAYOUT["loss"]
        tot = jnp.sum(gsum[r:r + 1, off:off + sz], axis=1, keepdims=True)
        loss_ref[...] = jnp.broadcast_to(tot * (0.5 / D_MODEL), loss_ref.shape)
        loss1_ref[...] = tot * (0.5 / D_MODEL)

        k0 = len(names)
        ws_refs = wmv[3 * k0:3 * k0 + 3]
        for g in range(N_GROUPS):
            rows = slice(g * BLOCK, (g + 1) * BLOCK)
            gg = tall_sum(g * BLOCK, BLOCK)
            d, m2, v2 = _adam_math(ws_refs[0][rows, :], gg, ws_refs[1][rows, :], ws_refs[2][rows, :])
            outs[4 * k0][rows, :] = gg
            outs[4 * k0 + 1][rows, :] = d
            outs[4 * k0 + 2][rows, :] = m2
            outs[4 * k0 + 3][rows, :] = v2
        emit(k0 + 1, tall_sum(TALL_BS, N_GROUPS), *wmv[3 * (k0 + 1):3 * (k0 + 1) + 3])
        emit(k0 + 2, tall_sum(TALL_RB, N_HEADS)[:, :N_BUCKETS], *wmv[3 * (k0 + 2):3 * (k0 + 2) + 3])
        emit(k0 + 3, tall_sum(TALL_SK, 8)[0:1, :N_HEADS], *wmv[3 * (k0 + 3):3 * (k0 + 3) + 3])

    out_shapes = [jax.ShapeDtypeStruct((N_DEV, WIDE_W), F32), jax.ShapeDtypeStruct((8, LANES), F32),
                  jax.ShapeDtypeStruct((1, 1), F32)]
    for n in names:
        out_shapes += [jax.ShapeDtypeStruct(wide_wmv[n][0].shape, F32)] * 4
    for _, t in tall:
        out_shapes += [jax.ShapeDtypeStruct(t[0].shape, F32)] * 4
    res = pl.pallas_call(
        body, name="adam_small", out_shape=out_shapes, grid=(1,),
        in_specs=[_const_spec(a.shape) for a in ins] + [ANY], out_specs=[_const_spec(o.shape) for o in out_shapes],
        compiler_params=_params(("arbitrary",)),
    )(*ins, after)
    out = {}
    for k, n in enumerate(names + [t[0] for t in tall]):
        out[n] = tuple(res[3 + 4 * k:7 + 4 * k])
    return res[0], res[1], res[2], out


def kernel(x, c, rel_bias, w_ada, b_ada, w_in, b_in, attn_sinks, gmlp_ln_g, gmlp_ln_b, gmlp_w_s, gmlp_b_s, attn_out_g, gmlp_out_g, w_out, ln1_g, ln1_b, w_gate_up, w_down, ln2_g, ln2_b, loss_target, m_rel_bias, m_w_ada, m_b_ada, m_w_in, m_b_in, m_attn_sinks, m_gmlp_ln_g, m_gmlp_ln_b, m_gmlp_w_s, m_gmlp_b_s, m_attn_out_g, m_gmlp_out_g, m_w_out, m_ln1_g, m_ln1_b, m_w_gate_up, m_w_down, m_ln2_g, m_ln2_b, v_rel_bias, v_w_ada, v_b_ada, v_w_in, v_b_in, v_attn_sinks, v_gmlp_ln_g, v_gmlp_ln_b, v_gmlp_w_s, v_gmlp_b_s, v_attn_out_g, v_gmlp_out_g, v_w_out, v_ln1_g, v_ln1_b, v_w_gate_up, v_w_down, v_ln2_g, v_ln2_b):
    ix, iy, _ = _my_pos()
    chip = 2 * ix + iy
    s = x.shape[1]
    xs = x[0]
    tgt = loss_target[0]
    tm_big = min(512, s)
    tm_ffn = min(FFN_SUB, s)

    sc_all, modr, (w_in_g, w_out_g) = _prologue(
        jnp.pad(c, ((0, 7), (0, 0))), w_ada[0], b_ada, [_with_own_block(w_in[0].T, chip), _with_own_block(w_out[0], chip)])
    w_in_f = w_in_g.reshape(IN_W, D_MODEL)

    bucket = _bucket_table()
    bias, wsm = _prep_tables(bucket, rel_bias.T, gmlp_w_s[0])
    bsx = jnp.repeat(gmlp_b_s[0].T, GROUP_DIM, axis=1)
    amat = _group_mean_matrix()
    sinks = attn_sinks[0]

    (h1, q, kv, gu, gv, xb), (w_dn_g,) = _fwd_in(xs, modr, w_in_f, b_in, tm_big, [_with_own_block(w_down[0], chip)],
                                                 ["blk"])
    w_out_f = w_out_g.reshape(D_MODEL, D_MODEL)
    (x1, x1b, y, mixed), (w_gu_f,) = _fwd_mix(
        q, kv, gu, gv, xs, modr, bias, sinks, gmlp_ln_g, gmlp_ln_b, wsm, bsx, amat, attn_out_g, gmlp_out_g, w_out_f,
        ln1_g, ln1_b, tm_big, [_with_own_block(w_gate_up[0], chip)], ["blk"])
    assert w_gate_up.shape[2] == FF_CHUNK
    w_dn_f = w_dn_g.reshape(D_FF, D_MODEL)
    h2, act, dy2, dx1a, acc_f = _fwd_ffn(x1, tgt, modr, ln2_g, ln2_b, w_gu_f, w_dn_f, min(2 * FFN_SUB, s))

    a_act, dgu_ff, dh2 = _bwd_ffn(dy2, act, w_gu_f, w_dn_f, min(FFN_SUB, s))
    g_dn, g_dn_b, _ = _wgrad(a_act, [dy2], D_FF // 2, min(1024, s), "wgrad_down")
    g_gu, g_gu_b, _ = _wgrad(h2, [dgu_ff], 512, min(512, s), "wgrad_gate_up")
    blk3 = lambda a, rows: a.reshape(N_CHIPS, rows, a.shape[1])
    (dxa, dy, dmix, acc_m), (got_dn, got_gu) = _bwd_mid(
        dh2, dx1a, x1b, xs, y, modr, ln1_g, w_out_f, tm_big, [blk3(g_dn_b, D_FF // N_CHIPS), g_gu_b], ["blk", "cols"])
    g_out, g_out_b, _ = _wgrad(mixed, [dy], 512, min(2048, s), "wgrad_out")
    (got_out,) = _swap_halves([blk3(g_out_b, D_MODEL // N_CHIPS)], ["blk"], "rs_swap_out")
    kinds_a = ["blk", "cols", "blk"]
    fulls_a = [blk3(g_dn, D_FF // N_CHIPS), g_gu, blk3(g_out, D_MODEL // N_CHIPS)]
    gots_a = [got_dn, got_gu, got_out]
    parts_a = [_add_halves(f, g, k, "rs_add_a%d" % i) for i, (f, g, k) in enumerate(zip(fulls_a, gots_a, kinds_a))]
    (dq, dkv, dgu, dgv, gbias, dws, dbs, vec, dsink), rxs_a = _bwd_mix(
        q, kv, gu, gv, dmix, bias, sinks, gmlp_ln_g, gmlp_ln_b, wsm, bsx, amat, attn_out_g, gmlp_out_g,
        [p[1] for p in parts_a], kinds_a)
    tall_g = _mix_finalize(gbias, bucket, dws, dbs, dsink)
    grad_x, acc_i, db_in = _bwd_in(dq, dkv, dgu, dgv, dxa, xb, modr, w_in_f, tm_big)

    wide_g = _pack_wide(acc_i, acc_m, acc_f, db_in, vec)
    full_in, full_in_b, (gw, gt) = _wgrad(h1, [dq, dkv, dgu, dgv], 512, min(1024, s), "wgrad_in", transposed=True,
                                          gather_vs=[wide_g, tall_g])
    (got_in,) = _swap_halves([blk3(full_in_b, IN_W // N_CHIPS)], ["blk"], "rs_swap_in")
    part_in = _add_halves(blk3(full_in, IN_W // N_CHIPS), got_in, "blk", "rs_add_in")
    in_send, in_recv, in_part, in_rx, token = _exchange_start(part_in[1], "rs_chips_in_start")
    wide_wmv ={"b_ada": (b_ada, m_b_ada, v_b_ada), "b_in": (b_in, m_b_in, v_b_in),
                "ln1_g": (ln1_g, m_ln1_g, v_ln1_g), "ln1_b": (ln1_b, m_ln1_b, v_ln1_b),
                "ln2_g": (ln2_g, m_ln2_g, v_ln2_g), "ln2_b": (ln2_b, m_ln2_b, v_ln2_b),
                "gmlp_ln_g": (gmlp_ln_g, m_gmlp_ln_g, v_gmlp_ln_g), "gmlp_ln_b": (gmlp_ln_b, m_gmlp_ln_b, v_gmlp_ln_b),
                "attn_out_g": (attn_out_g, m_attn_out_g, v_attn_out_g),
                "gmlp_out_g": (gmlp_out_g, m_gmlp_out_g, v_gmlp_out_g)}
    rows2 = lambda a: a.reshape(-1, a.shape[-1])
    dmod_all, loss_t, loss1, small = _adam_small(
        gw, gt, wide_wmv, tuple(rows2(a) for a in (gmlp_w_s, m_gmlp_w_s, v_gmlp_w_s)),
        tuple(rows2(a) for a in (gmlp_b_s, m_gmlp_b_s, v_gmlp_b_s)), (rel_bias.T, m_rel_bias.T, v_rel_bias.T),
        (attn_sinks, m_attn_sinks, v_attn_sinks), token)
    small["rel_bias"] = tuple(a.T for a in small["rel_bias"])
    loss = loss1.reshape(())

    g_ada, d_ada, m_ada, v_ada = _adam_w_ada(sc_all.T, dmod_all, w_ada[0], m_w_ada[0], v_w_ada[0], 256)

    sums = [(parts_a[0][0], rxs_a[0], 176), (parts_a[1][0], rxs_a[1], 256), (parts_a[2][0], rxs_a[2], 128)]
    mine = [_sum_chips(p, rx, tr, "rs_sum_%d" % i, loss_t) for i, (p, rx, tr) in enumerate(sums)]
    got = _share_halves(mine, "rs_share")
    gs_dn, d_dn, m_dn, v_dn = _adam_halves(w_down[0], mine[0], got[0], m_w_down[0], v_w_down[0], 176, "adam_w_down")
    gs_gu, d_gu, m_gu, v_gu = _adam_halves(w_gate_up[0], mine[1], got[1], m_w_gate_up[0], v_w_gate_up[0], 256,
                                           "adam_w_gate_up")
    gs_out, d_out, m_out, v_out = _adam_halves(w_out[0], mine[2], got[2], m_w_out[0], v_w_out[0], 128, "adam_w_out")

    rx_in = _exchange_wait(in_send, in_recv, in_part, in_rx, d_gu, "rs_chips_in_wait")
    mine_in = _sum_chips(part_in[0], rx_in, 112, "rs_sum_in", rx_in)
    (got_in_half,) = _share_halves([mine_in], "rs_share_in")
    in_t = _adam_halves(w_in[0].T, mine_in, got_in_half, m_w_in[0].T, v_w_in[0].T, 112, "adam_w_in")
    gs_in, d_in, m_in, v_in = (a.T for a in in_t)

    big = {"w_ada": (g_ada, d_ada, m_ada, v_ada), "w_in": (gs_in, d_in, m_in, v_in), "w_out": (gs_out, d_out, m_out, v_out),
           "w_gate_up": (gs_gu, d_gu, m_gu, v_gu), "w_down": (gs_dn, d_dn, m_dn, v_dn)}
    order = ["rel_bias", "w_ada", "b_ada", "w_in", "b_in", "attn_sinks", "gmlp_ln_g", "gmlp_ln_b", "gmlp_w_s", "gmlp_b_s",
             "attn_out_g", "gmlp_out_g", "w_out", "ln1_g", "ln1_b", "w_gate_up", "w_down", "ln2_g", "ln2_b"]
    shapes = {"gmlp_w_s": gmlp_w_s.shape, "gmlp_b_s": gmlp_b_s.shape}
    outs = [loss, grad_x[None]]
    for k in range(4):
        for name in order:
            if name in big:
                outs.append(big[name][k][None])
            elif name in shapes:
                outs.append(small[name][k].reshape(shapes[name]))
            else:
                outs.append(small[name][k])
    return tuple(outs)
```

```python
import math

import numpy as np
import jax
import jax.numpy as jnp
from jax import lax
from jax.experimental import pallas as pl
from jax.experimental.pallas import tpu as pltpu

F32 = jnp.float32
BF16 = jnp.bfloat16
MESH = pl.DeviceIdType.MESH

D_MODEL = 1024
N_HEADS = 8
N_KV = 2
HEAD_DIM = 64
ATTN_W = N_HEADS * HEAD_DIM
KV_W = N_KV * HEAD_DIM
N_GROUPS = 8
GROUP_DIM = 64
GMLP_W = N_GROUPS * GROUP_DIM
IN_W = ATTN_W + 2 * KV_W + 2 * GMLP_W
BLOCK = 128
N_BUCKETS = 32
MAX_DISTANCE = 128
D_FF = 2816
ALPHA = 2.0 ** 0.25
LN_EPS = 1e-5
NEG_INF = -1e30
ADAM_LR, ADAM_B1, ADAM_B2, ADAM_EPS, ADAM_WD, ADAM_STEP = 0.001, 0.9, 0.999, 1e-8, 0.01, 10
N_CHIPS = 4
N_DEV = 8
LANES = 128
V7X_VMEM_LIMIT = 56 * 2 ** 20
GELU_C = math.sqrt(2.0 / math.pi)
Q_SCALE = HEAD_DIM ** -0.5
ANY = pl.BlockSpec(memory_space=pl.ANY)
RING_SLOTS = 3

TALL_BS = N_GROUPS * BLOCK
TALL_RB = TALL_BS + 8
TALL_SK = TALL_RB + N_BUCKETS
TALL_ROWS = TALL_SK + 8
WIDE_W = 6 * D_MODEL
WIDE_LAYOUT = {
    "b_ada": (0, 0, 6 * D_MODEL),
    "b_in": (1, 0, IN_W), "ln1_g": (1, IN_W, D_MODEL), "ln1_b": (1, IN_W + D_MODEL, D_MODEL),
    "ln2_g": (1, IN_W + 2 * D_MODEL, D_MODEL), "ln2_b": (1, IN_W + 3 * D_MODEL, D_MODEL),
    "gmlp_ln_g": (2, 0, GMLP_W), "gmlp_ln_b": (2, GMLP_W, GMLP_W), "attn_out_g": (2, 2 * GMLP_W, ATTN_W),
    "gmlp_out_g": (2, 2 * GMLP_W + ATTN_W, GMLP_W), "loss": (2, 3 * GMLP_W + ATTN_W, D_MODEL)}
WIDE_PARAMS = tuple(n for n in WIDE_LAYOUT if n != "loss")


def _params(sem=None):
    return pltpu.CompilerParams(dimension_semantics=sem, vmem_limit_bytes=V7X_VMEM_LIMIT)


def _const_spec(shape, single=False):
    nd = len(shape)
    if single:
        return pl.BlockSpec(shape, lambda *_: (0,) * nd, pipeline_mode=pl.Buffered(1))
    return pl.BlockSpec(shape, lambda *_: (0,) * nd)


def _dot(a, b):
    return jnp.dot(a, b, preferred_element_type=F32)


def _dot_nt(a, b):
    return lax.dot_general(a, b, (((1,), (1,)), ((), ())), preferred_element_type=F32)


def _dot_tn(a, b):
    return lax.dot_general(a, b, (((0,), (0,)), ((), ())), preferred_element_type=F32)


def _gelu(x):
    t = jnp.tanh(GELU_C * (x + 0.044715 * x * x * x))
    return 0.5 * x * (1.0 + t), t


def _gelu_grad(x, t):
    return 0.5 * (1.0 + t) + 0.5 * x * (1.0 - t * t) * GELU_C * (1.0 + 3.0 * 0.044715 * x * x)


def _split_dot(x, a):
    hi = x.astype(BF16)
    lo = (x - hi.astype(F32)).astype(BF16)
    return _dot(hi, a) + _dot(lo, a)


def _group_mean_matrix():
    g = np.arange(GMLP_W) // GROUP_DIM
    return jnp.asarray((g[:, None] == g[None, :]).astype(np.float32) / GROUP_DIM, dtype=BF16)


def _ln_stats(z):
    mu = jnp.mean(z, axis=-1, keepdims=True)
    d = z - mu
    var = jnp.mean(d * d, axis=-1, keepdims=True)
    rstd = lax.rsqrt(var + LN_EPS)
    return d * rstd, rstd


def _ln_bwd(dxhat, xhat, rstd):
    m1 = jnp.mean(dxhat, axis=-1, keepdims=True)
    m2 = jnp.mean(dxhat * xhat, axis=-1, keepdims=True)
    return rstd * (dxhat - m1 - xhat * m2)


def _colsum(x):
    return jnp.sum(x, axis=0, keepdims=True)


def _my_pos():
    return lax.axis_index("x"), lax.axis_index("y"), lax.axis_index("c")


def _other_chips(x, y):
    return [(1 - x, y), (x, 1 - y), (1 - x, 1 - y)]


def _chip_index_scalar():
    ix, iy, _ = _my_pos()
    return jnp.reshape(2 * ix + iy, (1,)).astype(jnp.int32)


def _core_index_scalar():
    return jnp.reshape(lax.axis_index("c"), (1,)).astype(jnp.int32)


class _Gather8:
    def __init__(self, x_refs, out_refs, send_sems, recv_sems, local_sems):
        self.x_refs, self.out_refs = x_refs, out_refs
        self.send_sems, self.recv_sems, self.local_sems = send_sems, recv_sems, local_sems
        self.x, self.y, self.c = _my_pos()
        self.me, self.sibling = (self.x, self.y, self.c), (self.x, self.y, 1 - self.c)
        self.chips = _other_chips(self.x, self.y)

    def _rows(self, a, px, py, pc):
        m_per = self.x_refs[a].shape[0]
        return self.out_refs[a].at[pl.ds((4 * px + 2 * py + pc) * m_per, m_per), :]

    def _copy(self, a, k, block, to, src=None):
        return pltpu.make_async_remote_copy(
            src_ref=self._rows(a, *block) if src is None else src, dst_ref=self._rows(a, *block),
            send_sem=self.send_sems.at[7 * a + k], recv_sem=self.recv_sems.at[7 * a + k], device_id=to,
            device_id_type=MESH)

    def _local(self, a):
        return pltpu.make_async_copy(self.x_refs[a], self._rows(a, *self.me), self.local_sems.at[a])

    def start(self):
        for a in range(len(self.x_refs)):
            self._local(a).start()
            self._copy(a, 0, self.me, self.sibling, src=self.x_refs[a]).start()
            for j, chip in enumerate(self.chips):
                self._copy(a, 1 + j, self.me, (*chip, self.c), src=self.x_refs[a]).start()

    def forward(self):
        for a in range(len(self.x_refs)):
            for j, chip in enumerate(self.chips):
                self._copy(a, 1 + j, (*chip, self.c), self.me).wait_recv()
                self._copy(a, 4 + j, (*chip, self.c), self.sibling).start()

    def finish(self):
        for a in range(len(self.x_refs)):
            self._copy(a, 0, self.sibling, self.me).wait_recv()
            for j, chip in enumerate(self.chips):
                self._copy(a, 4 + j, (*chip, 1 - self.c), self.me).wait_recv()
        for a in range(len(self.x_refs)):
            for k in range(7):
                self._copy(a, k, self.me, self.me).wait_send()
            self._local(a).wait()

    @staticmethod
    def sems(n_v):
        return [pltpu.SemaphoreType.DMA((7 * n_v,)), pltpu.SemaphoreType.DMA((7 * n_v,)),
                pltpu.SemaphoreType.DMA((n_v,))]


def _gathered8_shapes(vs):
    return [jax.ShapeDtypeStruct((N_DEV * v.shape[0], v.shape[1]), v.dtype) for v in vs]


VMEM_WHOLE = pl.BlockSpec(memory_space=pltpu.VMEM)


def _prologue(c_pad, w_ada_s, b_ada, shards):
    n = w_ada_s.shape[1]
    n_w = len(shards)
    assert N_CHIPS * n == 6 * D_MODEL and n % LANES == 0

    def body(c_ref, w_ref, b_ref, *rest):
        sc_ref, modc_ref, modg_ref, modr_ref = rest[n_w:n_w + 4]
        gathered_refs = rest[n_w + 4:2 * n_w + 4]
        call_ref, w_vmem = rest[2 * n_w + 4:2 * n_w + 6]
        sems = rest[2 * n_w + 6:]
        ix, iy, ic = _my_pos()
        chip = 2 * ix + iy
        weights = _WeightGather(gathered_refs, ["blk"] * n_w, sems[0], sems[1])
        gather_c = _Gather8([c_ref], [call_ref], sems[2], sems[3], sems[4])
        gather_mod = _Gather8([modc_ref], [modg_ref], sems[5], sems[6], sems[7])
        load_w = pltpu.make_async_copy(w_ref, w_vmem, sems[8])
        weights.start()
        gather_c.start()
        load_w.start()
        gather_c.forward()
        gather_c.finish()
        cv = call_ref[...]
        sc = cv * _sigmoid(cv)
        a_hi = sc.astype(BF16)
        a_lo = (sc - a_hi.astype(F32)).astype(BF16)
        load_w.wait()
        w = w_vmem[...]
        w_hi = w.astype(BF16)
        w_lo = (w - w_hi.astype(F32)).astype(BF16)
        b = b_ref[:, 0:n]
        for k in range(1, N_CHIPS):
            b = jnp.where(chip == k, b_ref[:, k * n:(k + 1) * n], b)
        mod = _dot(a_hi, w_hi) + _dot(a_hi, w_lo) + _dot(a_lo, w_hi) + b
        for d in range(N_DEV):
            sc_ref[d:d + 1, :] = sc[8 * d:8 * d + 1, :]
            modc_ref[d:d + 1, :] = mod[8 * d:8 * d + 1, :]
        gather_mod.start()
        weights.forward()
        gather_mod.forward()
        gather_mod.finish()
        dev = 2 * chip + ic
        mine = jnp.concatenate([modg_ref[pl.ds(2 * 8 * k + dev, 1), :] for k in range(N_CHIPS)], axis=1)
        modr_ref[...] = jnp.zeros_like(modr_ref)
        for r in range(6):
            modr_ref[r:r + 1, :] = mine[:, r * D_MODEL:(r + 1) * D_MODEL]
        weights.forward_diagonal()
        weights.finish()

    outs = pl.pallas_call(
        body, name="prologue",
        out_shape=[jax.ShapeDtypeStruct((N_DEV, D_MODEL), F32), jax.ShapeDtypeStruct((N_DEV, n), F32),
                   jax.ShapeDtypeStruct((N_DEV * N_DEV, n), F32), jax.ShapeDtypeStruct((8, D_MODEL), F32)]
        + [jax.ShapeDtypeStruct(sh.shape, BF16) for sh in shards],
        in_specs=[VMEM_WHOLE, ANY, VMEM_WHOLE] + [ANY] * n_w,
        out_specs=[VMEM_WHOLE, VMEM_WHOLE, VMEM_WHOLE, VMEM_WHOLE] + [ANY] * n_w,
        input_output_aliases={3 + a: 4 + a for a in range(n_w)},
        scratch_shapes=[pltpu.VMEM((N_DEV * 8, D_MODEL), F32), pltpu.VMEM(w_ada_s.shape, F32)]
        + _WeightGather.sems(n_w) + _Gather8.sems(1) + _Gather8.sems(1) + [pltpu.SemaphoreType.DMA],
        compiler_params=pltpu.CompilerParams(vmem_limit_bytes=V7X_VMEM_LIMIT),
    )(c_pad, w_ada_s, b_ada, *shards)
    return outs[0], outs[3], outs[4:]


def _with_own_block(shard, chip):
    empty = lax.empty((N_CHIPS,) + shard.shape, BF16)
    return lax.dynamic_update_slice(empty, shard.astype(BF16)[None], (chip, 0, 0))


class _WeightGather:
    N_SEM = 8

    def __init__(self, gathered, kinds, send_sems, recv_sems):
        self.gathered, self.kinds = gathered, kinds
        self.send_sems, self.recv_sems = send_sems, recv_sems
        self.x, self.y, self.c = _my_pos()
        self.me, self.sibling = (self.x, self.y, self.c), (self.x, self.y, 1 - self.c)
        self.nbr = ((1 - self.x, self.y), (self.x, 1 - self.y))
        self.diag = 2 * (1 - self.x) + (1 - self.y)

    def _dst(self, a, chip, pc, quarter=None):
        r, cc = self._shard_shape(a)
        h = r // 2
        row0, rows = pc * h, h
        if quarter is not None:
            row0, rows = pc * h + quarter * (h // 2), h // 2
        g = self.gathered[a]
        if self.kinds[a] == "blk":
            return g.at[chip, pl.ds(row0, rows), :]
        return g.at[pl.ds(row0, rows), pl.ds(chip * cc, cc)]

    def _copy(self, a, k, region, to):
        return pltpu.make_async_remote_copy(
            src_ref=region, dst_ref=region, send_sem=self.send_sems.at[a * self.N_SEM + k],
            recv_sem=self.recv_sems.at[a * self.N_SEM + k], device_id=to, device_id_type=MESH)

    def _arrays(self):
        return range(len(self.gathered))

    def _shard_shape(self, a):
        shape = self.gathered[a].shape
        return shape[1:] if self.kinds[a] == "blk" else (shape[0], shape[1] // N_CHIPS)

    def start(self):
        my_chip = 2 * self.x + self.y
        for a in self._arrays():
            for j, chip in enumerate(self.nbr):
                self._copy(a, j, self._dst(a, my_chip, self.c), (*chip, self.c)).start()

    def forward(self):
        for a in self._arrays():
            for j, chip in enumerate(self.nbr):
                cj = 2 * chip[0] + chip[1]
                half = self._dst(a, cj, self.c)
                self._copy(a, j, half, self.me).wait_recv()
                self._copy(a, 2 + j, half, self.sibling).start()
                other = self.nbr[1 - j]
                self._copy(a, 4 + j, self._dst(a, cj, self.c, quarter=j), (*other, self.c)).start()

    def forward_diagonal(self):
        for a in self._arrays():
            for j in range(2):
                quarter = self._dst(a, self.diag, self.c, quarter=j)
                self._copy(a, 4 + j, quarter, self.me).wait_recv()
                self._copy(a, 6 + j, quarter, self.sibling).start()

    def finish(self):
        for a in self._arrays():
            for j, chip in enumerate(self.nbr):
                self._copy(a, 2 + j, self._dst(a, 2 * chip[0] + chip[1], 1 - self.c), self.me).wait_recv()
                self._copy(a, 6 + j, self._dst(a, self.diag, 1 - self.c, quarter=j), self.me).wait_recv()
        for a in self._arrays():
            half = self._dst(a, self.diag, self.c)
            quarter = self._dst(a, self.diag, self.c, quarter=0)
            for k in range(self.N_SEM):
                self._copy(a, k, half if k < 4 else quarter, self.me).wait_send()

    @classmethod
    def sems(cls, n_arr):
        return [pltpu.SemaphoreType.DMA((n_arr * cls.N_SEM,)), pltpu.SemaphoreType.DMA((n_arr * cls.N_SEM,))]


def _half_of_full(ref, kind, pc):
    if kind == "blk":
        h = ref.shape[1] // 2
        return ref.at[:, pl.ds(pc * h, h), :]
    h = ref.shape[0] // 2
    return ref.at[pl.ds(pc * h, h), :]


def _half_shape(shape, kind):
    return (shape[0], shape[1] // 2, shape[2]) if kind == "blk" else (shape[0] // 2, shape[1])


class _HalfSwap:
    def __init__(self, ins, outs, kinds, send_sems, recv_sems):
        self.ins, self.outs, self.kinds = ins, outs, kinds
        self.send_sems, self.recv_sems = send_sems, recv_sems
        self.x, self.y, self.c = _my_pos()

    def _copies(self):
        for a in range(len(self.ins)):
            yield pltpu.make_async_remote_copy(
                src_ref=_half_of_full(self.ins[a], self.kinds[a], 1 - self.c), dst_ref=self.outs[a],
                send_sem=self.send_sems.at[a], recv_sem=self.recv_sems.at[a],
                device_id=(self.x, self.y, 1 - self.c), device_id_type=MESH)

    def start(self):
        for cp in self._copies():
            cp.start()

    def wait(self):
        for cp in self._copies():
            cp.wait()

    @staticmethod
    def sems(n_arr):
        return [pltpu.SemaphoreType.DMA((n_arr,)), pltpu.SemaphoreType.DMA((n_arr,))]

    @staticmethod
    def out_shapes(fulls, kinds):
        return [jax.ShapeDtypeStruct(_half_shape(a.shape, k), a.dtype) for a, k in zip(fulls, kinds)]


def _swap_halves(fulls_bf16, kinds, name):
    n_arr = len(fulls_bf16)

    def body(*refs):
        swap = _HalfSwap(refs[:n_arr], refs[n_arr:2 * n_arr], kinds, *refs[2 * n_arr:])
        swap.start()
        swap.wait()

    return pl.pallas_call(
        body, name=name, out_shape=_HalfSwap.out_shapes(fulls_bf16, kinds),
        in_specs=[ANY] * n_arr, out_specs=[ANY] * n_arr, scratch_shapes=_HalfSwap.sems(n_arr),
    )(*fulls_bf16)


def _add_halves(full, got, kind, name):
    hs = _half_shape(full.shape, kind)

    def body(pos_ref, a_ref, b_ref, o_ref, ob_ref):
        p = a_ref[...] + b_ref[...].astype(F32)
        ob_ref[...] = p.astype(BF16)

        @pl.when(pl.program_id(0) == pos_ref[1])
        def _():
            o_ref[...] = p.reshape(o_ref.shape)

    if kind == "blk":
        nb, h, cc = hs
        own = pl.BlockSpec((1, h, cc), lambda b, pos_ref: (b, pos_ref[0], 0))
        other = pl.BlockSpec((1, h, cc), lambda b, pos_ref: (b, 0, 0))
    else:
        h, cc = hs[0], hs[1] // N_CHIPS
        own = pl.BlockSpec((h, cc), lambda b, pos_ref: (pos_ref[0], b))
        other = pl.BlockSpec((h, cc), lambda b, pos_ref: (0, b))
    pos = jnp.concatenate([_core_index_scalar(), _chip_index_scalar()])
    return pl.pallas_call(
        body, name=name, out_shape=(jax.ShapeDtypeStruct((h, cc), F32), jax.ShapeDtypeStruct(hs, BF16)),
        grid_spec=pltpu.PrefetchScalarGridSpec(
            num_scalar_prefetch=1, grid=(N_CHIPS,), in_specs=[own, other],
            out_specs=(pl.BlockSpec((h, cc), lambda b, pos_ref: (0, 0)), other)),
        compiler_params=_params(("arbitrary",)),
    )(pos, full, got)


def _rx_shape(part_shape, kind):
    if kind == "blk":
        return (3, part_shape[1], part_shape[2])
    return (3, part_shape[0], part_shape[1] // N_CHIPS)


class _ChipExchange:
    def __init__(self, parts, rxs, kinds, send_sems, recv_sems):
        self.parts, self.rxs, self.kinds = parts, rxs, kinds
        self.send_sems, self.recv_sems = send_sems, recv_sems
        self.x, self.y, self.c = _my_pos()
        self.chips = _other_chips(self.x, self.y)

    def _copies(self):
        for a in range(len(self.parts)):
            for j, chip in enumerate(self.chips):
                cj = 2 * chip[0] + chip[1]
                if self.kinds[a] == "blk":
                    src = self.parts[a].at[cj]
                else:
                    cc = self.parts[a].shape[1] // N_CHIPS
                    src = self.parts[a].at[:, pl.ds(cj * cc, cc)]
                yield pltpu.make_async_remote_copy(
                    src_ref=src, dst_ref=self.rxs[a].at[j], send_sem=self.send_sems.at[a * 3 + j],
                    recv_sem=self.recv_sems.at[a * 3 + j], device_id=(*chip, self.c), device_id_type=MESH)

    def start(self):
        for cp in self._copies():
            cp.start()

    def wait(self):
        for cp in self._copies():
            cp.wait_recv()
        for cp in self._copies():
            cp.wait_send()

    @staticmethod
    def sems(n_arr):
        return [pltpu.SemaphoreType.DMA((n_arr * 3,)), pltpu.SemaphoreType.DMA((n_arr * 3,))]


HBM_SPEC = pl.BlockSpec(memory_space=pltpu.HBM)
SEM_SPEC = pl.BlockSpec(memory_space=pltpu.SEMAPHORE)
DATAFLOW = pltpu.SideEffectType.DATAFLOW_SIDE_EFFECTING


def _exchange_start(part, name):
    rx_shape = _rx_shape(part.shape, "blk")

    def body(part_ref, rx_ref, send_sems, recv_sems, part_thru, rx_thru, token):
        _ChipExchange([part_ref], [rx_ref], ["blk"], send_sems, recv_sems).start()
        token[...] = jnp.zeros_like(token)

    return pl.pallas_call(
        body, name=name,
        out_shape=(pltpu.SemaphoreType.DMA((3,)), pltpu.SemaphoreType.DMA((3,)), pltpu.HBM(part.shape, part.dtype),
                   pltpu.HBM(rx_shape, BF16), jax.ShapeDtypeStruct((8, LANES), F32)),
        in_specs=(HBM_SPEC, HBM_SPEC), out_specs=(SEM_SPEC, SEM_SPEC, HBM_SPEC, HBM_SPEC, VMEM_WHOLE),
        input_output_aliases={0: 2, 1: 3}, compiler_params=pltpu.CompilerParams(has_side_effects=DATAFLOW),
    )(pltpu.with_memory_space_constraint(part, pltpu.HBM),
      pltpu.with_memory_space_constraint(lax.empty(rx_shape, BF16), pltpu.HBM))


def _exchange_wait(send_sems, recv_sems, part_thru, rx_thru, after, name):
    def body(part_ref, rx_ref, send_sems, recv_sems, after_ref, part_dead, rx_out):
        _ChipExchange([part_ref], [rx_ref], ["blk"], send_sems, recv_sems).wait()

    return pl.pallas_call(
        body, name=name,
        out_shape=(pltpu.HBM(part_thru.shape, part_thru.dtype), pltpu.HBM(rx_thru.shape, rx_thru.dtype)),
        in_specs=(HBM_SPEC, HBM_SPEC, SEM_SPEC, SEM_SPEC, ANY), out_specs=(HBM_SPEC, HBM_SPEC),
        input_output_aliases={0: 0, 1: 1}, compiler_params=pltpu.CompilerParams(has_side_effects=DATAFLOW),
    )(part_thru, rx_thru, send_sems, recv_sems, after)[1]


def _sum_chips(part, rx, tr, name, after):
    _, h, cc = rx.shape
    flips = (2, 1, 3)

    def body(chip_ref, p_ref, rx_ref, after_ref, o_ref):
        own = p_ref[...]
        for mc in range(N_CHIPS):
            @pl.when(chip_ref[0] == mc)
            def _():
                terms = sorted([(mc, None)] + [(mc ^ f, j) for j, f in enumerate(flips)])
                acc = None
                for _, j in terms:
                    t = own if j is None else rx_ref[j].astype(F32)
                    acc = t if acc is None else acc + t
                o_ref[...] = acc

    return pl.pallas_call(
        body, name=name, out_shape=jax.ShapeDtypeStruct((h, cc), F32),
        grid_spec=pltpu.PrefetchScalarGridSpec(
            num_scalar_prefetch=1, grid=(h // tr,),
            in_specs=[pl.BlockSpec((tr, cc), lambda i, chip_ref: (i, 0)),
                      pl.BlockSpec((3, tr, cc), lambda i, chip_ref: (0, i, 0)), ANY],
            out_specs=pl.BlockSpec((tr, cc), lambda i, chip_ref: (i, 0))),
        compiler_params=_params(("arbitrary",)),
    )(_chip_index_scalar(), part, rx, after)


def _share_halves(halves, name):
    n_arr = len(halves)

    def body(*refs):
        ins, outs = refs[:n_arr], refs[n_arr:2 * n_arr]
        send_sems, recv_sems = refs[2 * n_arr:]
        x, y, c = _my_pos()
        cps = []
        for a in range(n_arr):
            cp = pltpu.make_async_remote_copy(
                src_ref=ins[a], dst_ref=outs[a], send_sem=send_sems.at[a], recv_sem=recv_sems.at[a],
                device_id=(x, y, 1 - c), device_id_type=MESH)
            cp.start()
            cps.append(cp)
        for cp in cps:
            cp.wait()

    return pl.pallas_call(
        body, name=name, out_shape=[jax.ShapeDtypeStruct(h.shape, h.dtype) for h in halves],
        in_specs=[ANY] * n_arr, out_specs=[ANY] * n_arr,
        scratch_shapes=[pltpu.SemaphoreType.DMA((n_arr,)), pltpu.SemaphoreType.DMA((n_arr,))],
    )(*halves)


def _bucket_table():
    qi = jnp.arange(BLOCK)[:, None]
    si = jnp.arange(2 * BLOCK)[None, :]
    dist = qi + BLOCK - si
    max_exact = N_BUCKETS // 2
    n = jnp.maximum(dist, 0)
    nf = jnp.maximum(n, max_exact).astype(F32)
    large = max_exact + (jnp.log(nf / max_exact) / math.log(MAX_DISTANCE / max_exact)
                         * (N_BUCKETS - max_exact)).astype(jnp.int32)
    large = jnp.minimum(large, N_BUCKETS - 1)
    return jnp.where(n < max_exact, n, large).astype(F32)


def _prep_tables(bucket, rel_bias_t, w_s):
    def body(bucket_ref, rb_ref, ws_ref, bias_ref, wsm_ref):
        qi = lax.broadcasted_iota(jnp.int32, (BLOCK, 2 * BLOCK), 0)
        si = lax.broadcasted_iota(jnp.int32, (BLOCK, 2 * BLOCK), 1)
        dist = qi + BLOCK - si
        in_window = (dist >= 0) & (dist < BLOCK)
        bk = bucket_ref[...]
        for h in range(N_HEADS):
            acc = jnp.zeros((BLOCK, 2 * BLOCK), F32)
            for b in range(N_BUCKETS):
                acc = jnp.where(bk == float(b), rb_ref[h, b], acc)
            bias_ref[h] = jnp.where(in_window, acc, NEG_INF)
        ti = lax.broadcasted_iota(jnp.int32, (BLOCK, BLOCK), 0)
        ui = lax.broadcasted_iota(jnp.int32, (BLOCK, BLOCK), 1)
        for g in range(N_GROUPS):
            wsm_ref[g] = jnp.where(ti >= ui, ws_ref[g], 0.0).astype(BF16)

    return pl.pallas_call(
        body, name="prep_tables",
        out_shape=(jax.ShapeDtypeStruct((N_HEADS, BLOCK, 2 * BLOCK), F32),
                   jax.ShapeDtypeStruct((N_GROUPS, BLOCK, BLOCK), BF16)),
        grid=(1,),
        in_specs=[_const_spec((BLOCK, 2 * BLOCK)), pl.BlockSpec(memory_space=pltpu.SMEM),
                  _const_spec((N_GROUPS, BLOCK, BLOCK))],
        out_specs=(_const_spec((N_HEADS, BLOCK, 2 * BLOCK)), _const_spec((N_GROUPS, BLOCK, BLOCK))),
        compiler_params=_params(("arbitrary",)),
    )(bucket, rel_bias_t, w_s)


def _fwd_in(x, modr, w_in, b_in, tm, shards, kinds):
    s = x.shape[0]
    n_steps = s // tm
    fwd_step, diag_step = (8 * n_steps) // 16, (13 * n_steps) // 16
    n_w = len(shards)

    def body(x_ref, mod_ref, w_ref, b_ref, *rest):
        h1_ref, q_ref, kv_ref, gu_ref, gv_ref, xb_ref = rest[n_w:n_w + 6]
        gathered_refs = rest[n_w + 6:2 * n_w + 6]
        send_sems, recv_sems = rest[2 * n_w + 6:]
        i = pl.program_id(0)
        gather = _WeightGather(gathered_refs, kinds, send_sems, recv_sems)

        @pl.when(i == 0)
        def _():
            gather.start()

        xv = x_ref[...]
        xb_ref[...] = xv.astype(BF16)
        h1 = (xv * (1.0 + mod_ref[1:2, :]) + mod_ref[0:1, :]).astype(BF16)
        h1_ref[...] = h1
        proj = _dot_nt(h1, w_ref[...]) + b_ref[...]
        q_ref[...] = (proj[:, :ATTN_W] * Q_SCALE).astype(BF16)
        kv_ref[...] = proj[:, ATTN_W:ATTN_W + 2 * KV_W].astype(BF16)
        gu_ref[...] = proj[:, ATTN_W + 2 * KV_W:ATTN_W + 2 * KV_W + GMLP_W]
        gv_ref[...] = proj[:, ATTN_W + 2 * KV_W + GMLP_W:]

        @pl.when(i == fwd_step)
        def _():
            gather.forward()

        @pl.when(i == diag_step)
        def _():
            gather.forward_diagonal()

        @pl.when(i == n_steps - 1)
        def _():
            gather.finish()

    row = lambda w: pl.BlockSpec((tm, w), lambda i: (i, 0))
    outs = pl.pallas_call(
        body, name="fwd_in",
        out_shape=[jax.ShapeDtypeStruct((s, D_MODEL), BF16), jax.ShapeDtypeStruct((s, ATTN_W), BF16),
                   jax.ShapeDtypeStruct((s, 2 * KV_W), BF16), jax.ShapeDtypeStruct((s, GMLP_W), F32),
                   jax.ShapeDtypeStruct((s, GMLP_W), F32), jax.ShapeDtypeStruct((s, D_MODEL), BF16)]
        + [jax.ShapeDtypeStruct(sh.shape, BF16) for sh in shards],
        grid=(n_steps,),
        in_specs=[row(D_MODEL), _const_spec((8, D_MODEL)), _const_spec(w_in.shape), _const_spec((1, IN_W))]
        + [ANY] * n_w,
        out_specs=[row(D_MODEL), row(ATTN_W), row(2 * KV_W), row(GMLP_W), row(GMLP_W), row(D_MODEL)] + [ANY] * n_w,
        input_output_aliases={4 + a: 6 + a for a in range(n_w)},
        scratch_shapes=_WeightGather.sems(n_w),
        compiler_params=_params(("arbitrary",)),
    )(x, modr, w_in, b_in, *shards)
    return outs[:6], outs[6:]


def _kv_variants(kk):
    kf = kk.astype(F32)
    lane = lax.broadcasted_iota(jnp.int32, kf.shape, 1)
    low = lane < HEAD_DIM
    k0_lo = jnp.where(low, kf, 0.0)
    k1_hi = jnp.where(low, 0.0, kf)
    k0_hi = pltpu.roll(k0_lo, HEAD_DIM, 1)
    k1_lo = pltpu.roll(k1_hi, HEAD_DIM, 1)
    return ((k0_lo.astype(BF16), k0_hi.astype(BF16)), (k1_lo.astype(BF16), k1_hi.astype(BF16)))


def _head_kv(h):
    return h // (N_HEADS // N_KV), h % 2


MIX_GROUP = 2


def _interleave(*gens):
    results = [None] * len(gens)
    active = list(enumerate(gens))
    while active:
        still = []
        for i, g in active:
            try:
                next(g)
                still.append((i, g))
            except StopIteration as done:
                results[i] = done.value
        active = still
    return results


def _attn_block_fwd(q_blk, kk, vv, bias_ref, sinks_ref, first_mask):
    kvar = _kv_variants(kk)
    vvar = _kv_variants(vv)
    heads = range(N_HEADS)
    q_pairs = [q_blk[:, (h // 2) * LANES:(h // 2 + 1) * LANES] for h in heads]
    logits = [_dot_nt(q_pairs[h], kvar[_head_kv(h)[0]][_head_kv(h)[1]]) + bias_ref[h] for h in heads]
    if first_mask is not None:
        logits = [jnp.where(first_mask, NEG_INF, lg) for lg in logits]
    yield
    ms = [jnp.maximum(jnp.max(logits[h], axis=-1, keepdims=True), sinks_ref[h]) for h in heads]
    yield
    es = [jnp.exp(logits[h] - ms[h]) for h in heads]
    ess = [jnp.exp(sinks_ref[h] - ms[h]) for h in heads]
    yield
    invs = [1.0 / (jnp.sum(es[h], axis=-1, keepdims=True) + ess[h]) for h in heads]
    probs = [(es[h] * invs[h], ess[h] * invs[h]) for h in heads]
    yield
    outs = [_dot(probs[h][0].astype(BF16), vvar[_head_kv(h)[0]][_head_kv(h)[1]]) for h in heads]
    pairs = [outs[2 * i] + outs[2 * i + 1] for i in range(N_HEADS // 2)]
    return jnp.concatenate(pairs, axis=1), probs, kvar, vvar


def _gmlp_chunk_fwd(gu, gv, ln_g, ln_b, wsm_ref, bsx, amat):
    u, tu = _gelu(gu)
    a, ta = _gelu(gv)
    yield
    mean = _split_dot(a, amat)
    d = a - mean
    yield
    var = _split_dot(d * d, amat)
    yield
    rstd = lax.rsqrt(var + LN_EPS)
    xhat = d * rstd
    vb = (xhat * ln_g + ln_b).astype(BF16)
    yield
    lane = lax.broadcasted_iota(jnp.int32, (BLOCK, LANES), 1)
    low = lane < GROUP_DIM
    cols = []
    for pair in range(N_GROUPS // 2):
        vp = vb[:, pair * LANES:(pair + 1) * LANES]
        cols.append(jnp.where(low, _dot(wsm_ref[2 * pair], vp), _dot(wsm_ref[2 * pair + 1], vp)))
    mixedv = jnp.concatenate(cols, axis=1) + bsx
    return u * mixedv, (u, tu, ta, xhat, rstd, vb, mixedv)


def _rms(a, g):
    r = lax.rsqrt(jnp.mean(a * a, axis=-1, keepdims=True) + LN_EPS)
    return a * r * g, r


def _fwd_mix(q, kv, gu, gv, x, modr, bias, sinks, gln_g, gln_b, wsm, bsx, amat, aog, gog, w_out, ln1_g, ln1_b, tm,
             ffn_shards, ffn_kinds):
    s = x.shape[0]
    nb = tm // BLOCK
    n_steps = s // tm
    fwd_step, diag_step = (7 * n_steps) // 16, (12 * n_steps) // 16
    n_w = len(ffn_shards)

    def body(q_ref, kv_ref, kvp_ref, gu_ref, gv_ref, x_ref, mod_ref, bias_ref, sinks_ref, glng_ref, glnb_ref, wsm_ref,
             bsx_ref, amat_ref, aog_ref, gog_ref, wout_ref, ln1g_ref, ln1b_ref, *rest):
        x1_ref, x1b_ref, y_ref, mixed_ref = rest[n_w:n_w + 4]
        gathered_refs = rest[n_w + 4:2 * n_w + 4]
        mix_scr, send_sems, recv_sems = rest[2 * n_w + 4:]
        i = pl.program_id(0)
        gather = _WeightGather(gathered_refs, ffn_kinds, send_sems, recv_sems)

        @pl.when(i == 0)
        def _():
            gather.start()

        col = lax.broadcasted_iota(jnp.int32, (BLOCK, 2 * BLOCK), 1)
        for b0 in range(0, nb, MIX_GROUP):
            gens = []
            for b in range(b0, min(b0 + MIX_GROUP, nb)):
                r0 = b * BLOCK
                if b == 0:
                    kvprev = kvp_ref[...]
                    first_mask = (col < BLOCK) & (i == 0)
                else:
                    kvprev = kv_ref[r0 - BLOCK:r0, :]
                    first_mask = None
                kvcur = kv_ref[r0:r0 + BLOCK, :]
                kk = jnp.concatenate([kvprev[:, :KV_W], kvcur[:, :KV_W]], axis=0)
                vv = jnp.concatenate([kvprev[:, KV_W:], kvcur[:, KV_W:]], axis=0)
                gens.append(_attn_block_fwd(q_ref[r0:r0 + BLOCK, :], kk, vv, bias_ref, sinks_ref, first_mask))
                gens.append(_gmlp_chunk_fwd(gu_ref[r0:r0 + BLOCK, :], gv_ref[r0:r0 + BLOCK, :], glng_ref[...],
                                            glnb_ref[...], wsm_ref, bsx_ref[...], amat_ref[...]))
            res = _interleave(*gens)
            for k, b in enumerate(range(b0, min(b0 + MIX_GROUP, nb))):
                r0 = b * BLOCK
                na, _ = _rms(res[2 * k][0], aog_ref[...])
                ng, _ = _rms(res[2 * k + 1][0], gog_ref[...])
                mix_scr[r0:r0 + BLOCK, :ATTN_W] = na.astype(BF16)
                mix_scr[r0:r0 + BLOCK, ATTN_W:] = ng.astype(BF16)
        mixed = mix_scr[...]
        mixed_ref[...] = mixed
        y = _dot(mixed, wout_ref[...])
        y_ref[...] = y.astype(BF16)
        z1 = ALPHA * x_ref[...] + mod_ref[2:3, :] * y
        xhat, _ = _ln_stats(z1)
        x1 = xhat * ln1g_ref[...] + ln1b_ref[...]
        x1_ref[...] = x1
        x1b_ref[...] = x1.astype(BF16)

        @pl.when(i == fwd_step)
        def _():
            gather.forward()

        @pl.when(i == diag_step)
        def _():
            gather.forward_diagonal()

        @pl.when(i == n_steps - 1)
        def _():
            gather.finish()

    row = lambda w: pl.BlockSpec((tm, w), lambda i: (i, 0))
    prev = pl.BlockSpec((BLOCK, 2 * KV_W), lambda i: (jnp.maximum(i * nb - 1, 0), 0))
    outs = pl.pallas_call(
        body, name="fwd_mix",
        out_shape=[jax.ShapeDtypeStruct((s, D_MODEL), F32)] + [jax.ShapeDtypeStruct((s, D_MODEL), BF16)] * 3
        + [jax.ShapeDtypeStruct(sh.shape, BF16) for sh in ffn_shards],
        grid=(n_steps,),
        in_specs=[row(ATTN_W), row(2 * KV_W), prev, row(GMLP_W), row(GMLP_W), row(D_MODEL), _const_spec((8, D_MODEL)),
                  _const_spec((N_HEADS, BLOCK, 2 * BLOCK)), pl.BlockSpec(memory_space=pltpu.SMEM),
                  _const_spec((1, GMLP_W)), _const_spec((1, GMLP_W)), _const_spec((N_GROUPS, BLOCK, BLOCK)),
                  _const_spec((BLOCK, GMLP_W)), _const_spec((GMLP_W, GMLP_W)), _const_spec((1, ATTN_W)),
                  _const_spec((1, GMLP_W)), _const_spec((D_MODEL, D_MODEL)), _const_spec((1, D_MODEL)),
                  _const_spec((1, D_MODEL))] + [ANY] * n_w,
        out_specs=[row(D_MODEL)] * 4 + [ANY] * n_w,
        input_output_aliases={19 + a: 4 + a for a in range(n_w)},
        scratch_shapes=[pltpu.VMEM((tm, D_MODEL), BF16)] + _WeightGather.sems(n_w),
        compiler_params=_params(("arbitrary",)),
    )(q, kv, kv, gu, gv, x, modr, bias, sinks, gln_g, gln_b, wsm, bsx, amat, aog, gog, w_out, ln1_g, ln1_b, *ffn_shards)
    return outs[:4], outs[4:]


FF_BLOCKS = N_CHIPS // 2
FF_CHUNK = D_FF // FF_BLOCKS
FFN_SUB = 256


def _sigmoid(x):
    return 1.0 / (1.0 + jnp.exp(-x))


def _fwd_ffn(x1, target, modr, ln2_g, ln2_b, w_gu, w_dn, tm):
    s = x1.shape[0]

    def body(x1_ref, t_ref, mod_ref, g_ref, b_ref, wgu_ref, wdn_ref, h2_ref, act_ref, dy2_ref, dx1a_ref, acc_ref):
        @pl.when(pl.program_id(0) == 0)
        def _():
            acc_ref[...] = jnp.zeros_like(acc_ref)

        x1v = x1_ref[...]
        h2 = (x1v * (1.0 + mod_ref[4:5, :]) + mod_ref[3:4, :]).astype(BF16)
        h2_ref[...] = h2
        y2 = None
        for cc in range(FF_BLOCKS):
            c0 = cc * FF_CHUNK
            gate = _dot(h2, wgu_ref[cc])
            up = _dot(h2, wgu_ref[FF_BLOCKS + cc])
            act_ref[:, c0:c0 + FF_CHUNK] = gate.astype(BF16)
            act_ref[:, D_FF + c0:D_FF + c0 + FF_CHUNK] = up.astype(BF16)
            a = (gate * _sigmoid(gate) * up).astype(BF16)
            part = _dot(a, wdn_ref[c0:c0 + FF_CHUNK, :])
            y2 = part if y2 is None else y2 + part
        g2 = mod_ref[5:6, :]
        z2 = ALPHA * x1v + g2 * y2
        xhat, rstd = _ln_stats(z2)
        gain = g_ref[...]
        diff = xhat * gain + b_ref[...] - t_ref[...]
        dx2 = diff * (1.0 / D_MODEL)
        dz2 = _ln_bwd(dx2 * gain, xhat, rstd)
        dx1a_ref[...] = ALPHA * dz2
        dy2_ref[...] = (g2 * dz2).astype(BF16)
        acc_ref[0:1, :] += _colsum(diff * diff)
        acc_ref[1:2, :] += _colsum(dx2 * xhat)
        acc_ref[2:3, :] += _colsum(dx2)
        acc_ref[3:4, :] += _colsum(dz2 * y2)

    row = lambda w: pl.BlockSpec((tm, w), lambda i: (i, 0))
    return pl.pallas_call(
        body, name="fwd_ffn",
        out_shape=(jax.ShapeDtypeStruct((s, D_MODEL), BF16), jax.ShapeDtypeStruct((s, 2 * D_FF), BF16),
                   jax.ShapeDtypeStruct((s, D_MODEL), BF16), jax.ShapeDtypeStruct((s, D_MODEL), F32),
                   jax.ShapeDtypeStruct((8, D_MODEL), F32)),
        grid=(s // tm,),
        in_specs=[row(D_MODEL), row(D_MODEL), _const_spec((8, D_MODEL)), _const_spec((1, D_MODEL)),
                  _const_spec((1, D_MODEL)), _const_spec((N_CHIPS, D_MODEL, FF_CHUNK), single=True),
                  _const_spec((D_FF, D_MODEL), single=True)],
        out_specs=(row(D_MODEL), row(2 * D_FF), row(D_MODEL), row(D_MODEL), _const_spec((8, D_MODEL))),
        compiler_params=_params(("arbitrary",)),
    )(x1, target, modr, ln2_g, ln2_b, w_gu, w_dn)


def _bwd_ffn(dy2, act, w_gu, w_dn, tm):
    s = dy2.shape[0]

    def body(dy2_ref, act_ref, wgu_ref, wdn_ref, a_ref, dgu_ref, dh2_ref):
        dy2v = dy2_ref[...]
        dh2 = None
        for cc in range(FF_BLOCKS):
            c0 = cc * FF_CHUNK
            da = _dot_nt(dy2v, wdn_ref[c0:c0 + FF_CHUNK, :])
            gate = act_ref[:, c0:c0 + FF_CHUNK].astype(F32)
            up = act_ref[:, D_FF + c0:D_FF + c0 + FF_CHUNK].astype(F32)
            sg = _sigmoid(gate)
            sl = gate * sg
            a_ref[:, c0:c0 + FF_CHUNK] = (sl * up).astype(BF16)
            dgate = (da * up * (sg * (1.0 + gate * (1.0 - sg)))).astype(BF16)
            dup = (da * sl).astype(BF16)
            dgu_ref[:, c0:c0 + FF_CHUNK] = dgate
            dgu_ref[:, D_FF + c0:D_FF + c0 + FF_CHUNK] = dup
            part = _dot_nt(dgate, wgu_ref[cc]) + _dot_nt(dup, wgu_ref[FF_BLOCKS + cc])
            dh2 = part if dh2 is None else dh2 + part
        dh2_ref[...] = dh2.astype(BF16)

    row = lambda w: pl.BlockSpec((tm, w), lambda i: (i, 0))
    return pl.pallas_call(
        body, name="bwd_ffn",
        out_shape=(jax.ShapeDtypeStruct((s, D_FF), BF16), jax.ShapeDtypeStruct((s, 2 * D_FF), BF16),
                   jax.ShapeDtypeStruct((s, D_MODEL), BF16)),
        grid=(s // tm,),
        in_specs=[row(D_MODEL), row(2 * D_FF), _const_spec((N_CHIPS, D_MODEL, FF_CHUNK), single=True),
                  _const_spec((D_FF, D_MODEL), single=True)],
        out_specs=(row(D_FF), row(2 * D_FF), row(D_MODEL)),
        compiler_params=_params(("parallel",)),
    )(dy2, act, w_gu, w_dn)


def _bwd_mid(dh2, dx1a, x1, x, y, modr, ln1_g, w_out, tm, swap_fulls, swap_kinds):
    s = x.shape[0]
    n_steps = s // tm
    n_g = len(swap_fulls)

    def body(dh2_ref, dx1a_ref, x1_ref, x_ref, y_ref, mod_ref, g_ref, wout_ref, *rest):
        full_refs = rest[:n_g]
        dxa_ref, dy_ref, dmix_ref, acc_ref = rest[n_g:n_g + 4]
        got_refs = rest[n_g + 4:2 * n_g + 4]
        swap = _HalfSwap(full_refs, got_refs, swap_kinds, *rest[2 * n_g + 4:])
        i = pl.program_id(0)

        @pl.when(i == 0)
        def _():
            swap.start()
            acc_ref[...] = jnp.zeros_like(acc_ref)

        dh2 = dh2_ref[...].astype(F32)
        x1v = x1_ref[...].astype(F32)
        yv = y_ref[...].astype(F32)
        g1 = mod_ref[2:3, :]
        dx1 = dx1a_ref[...] + dh2 * (1.0 + mod_ref[4:5, :])
        z1 = ALPHA * x_ref[...] + g1 * yv
        xhat, rstd = _ln_stats(z1)
        dz1 = _ln_bwd(dx1 * g_ref[...], xhat, rstd)
        dxa_ref[...] = (ALPHA * dz1).astype(BF16)
        dy = (g1 * dz1).astype(BF16)
        dy_ref[...] = dy
        dmix_ref[...] = _dot_nt(dy, wout_ref[...]).astype(BF16)
        acc_ref[0:1, :] += _colsum(dh2 * x1v)
        acc_ref[1:2, :] += _colsum(dh2)
        acc_ref[2:3, :] += _colsum(dx1 * xhat)
        acc_ref[3:4, :] += _colsum(dx1)
        acc_ref[4:5, :] += _colsum(dz1 * yv)

        @pl.when(i == n_steps - 1)
        def _():
            swap.wait()

    row = lambda w: pl.BlockSpec((tm, w), lambda i: (i, 0))
    outs = pl.pallas_call(
        body, name="bwd_mid",
        out_shape=[jax.ShapeDtypeStruct((s, D_MODEL), BF16), jax.ShapeDtypeStruct((s, D_MODEL), BF16),
                   jax.ShapeDtypeStruct((s, D_MODEL), BF16), jax.ShapeDtypeStruct((8, D_MODEL), F32)]
        + _HalfSwap.out_shapes(swap_fulls, swap_kinds),
        grid=(n_steps,),
        in_specs=[row(D_MODEL)] * 5 + [_const_spec((8, D_MODEL)), _const_spec((1, D_MODEL)),
                                       _const_spec((D_MODEL, D_MODEL))] + [ANY] * n_g,
        out_specs=[row(D_MODEL), row(D_MODEL), row(D_MODEL), _const_spec((8, D_MODEL))] + [ANY] * n_g,
        scratch_shapes=_HalfSwap.sems(n_g),
        compiler_params=_params(("arbitrary",)),
    )(dh2, dx1a, x1, x, y, modr, ln1_g, w_out, *swap_fulls)
    return outs[:4], outs[4:]


def _fold_kv(t0, t1):
    lane = lax.broadcasted_iota(jnp.int32, t0.shape, 1)
    f0 = t0 + pltpu.roll(t0, HEAD_DIM, 1)
    f1 = t1 + pltpu.roll(t1, HEAD_DIM, 1)
    return jnp.where(lane < HEAD_DIM, f0, f1)


def _bwd_mix(q, kv, gu, gv, dmix, bias, sinks, gln_g, gln_b, wsm, bsx, amat, aog, gog, grad_parts, grad_kinds):
    s = q.shape[0]
    tile = 2 * BLOCK
    n_steps = s // tile
    n_g = len(grad_parts)

    def body(q_ref, kv_ref, kvp_ref, gu_ref, gv_ref, dmix_ref, bias_ref, sinks_ref, glng_ref, glnb_ref, wsm_ref,
             bsx_ref, amat_ref, aog_ref, gog_ref, *rest):
        part_refs = rest[:n_g]
        dq_ref, dkv_ref, dgu_ref, dgv_ref, gbias_ref, dws_ref, dbs_ref, vec_ref, dsink_ref = rest[n_g:n_g + 9]
        rx_refs = rest[n_g + 9:2 * n_g + 9]
        carry, done, send_sems, recv_sems = rest[2 * n_g + 9:]
        n = pl.program_id(0)
        exchange = _ChipExchange(part_refs, rx_refs, grad_kinds, send_sems, recv_sems)

        @pl.when(n == 0)
        def _():
            exchange.start()
            carry[...] = jnp.zeros_like(carry)
            done[...] = jnp.zeros_like(done)
            gbias_ref[...] = jnp.zeros_like(gbias_ref)
            dws_ref[...] = jnp.zeros_like(dws_ref)
            dbs_ref[...] = jnp.zeros_like(dbs_ref)
            vec_ref[...] = jnp.zeros_like(vec_ref)
            dsink_ref[...] = jnp.zeros_like(dsink_ref)

        @pl.when(n == n_steps)
        def _():
            dkv_ref[:BLOCK, :] = done[...].astype(BF16)
            dkv_ref[BLOCK:, :] = carry[...].astype(BF16)
            exchange.wait()

        @pl.when(n < n_steps)
        def _():
            col = lax.broadcasted_iota(jnp.int32, (BLOCK, 2 * BLOCK), 1)
            lane = lax.broadcasted_iota(jnp.int32, (BLOCK, LANES), 1)
            low = lane < HEAD_DIM
            rows = [slice(0, BLOCK), slice(BLOCK, tile)]
            kv_blocks = [kvp_ref[...], kv_ref[rows[0], :], kv_ref[rows[1], :]]
            masks = [(col < BLOCK) & (n == 0), None]
            q_blks = [q_ref[r, :] for r in rows]
            fwd = []
            for b in range(2):
                kk = jnp.concatenate([kv_blocks[b][:, :KV_W], kv_blocks[b + 1][:, :KV_W]], axis=0)
                vv = jnp.concatenate([kv_blocks[b][:, KV_W:], kv_blocks[b + 1][:, KV_W:]], axis=0)
                fwd.append(_attn_block_fwd(q_blks[b], kk, vv, bias_ref, sinks_ref, masks[b]))
                fwd.append(_gmlp_chunk_fwd(gu_ref[rows[b], :], gv_ref[rows[b], :], glng_ref[...], glnb_ref[...],
                                           wsm_ref, bsx_ref[...], amat_ref[...]))
            res = _interleave(*fwd[:2]) + _interleave(*fwd[2:])

            def gating_bwd(b, d_gm, saved):
                u, tu, ta, xhat, rstd, vb, mixedv = saved
                dgu_ref[rows[b], :] = (d_gm * mixedv * _gelu_grad(gu_ref[rows[b], :], tu)).astype(BF16)
                dmx = d_gm * u
                dmxb = dmx.astype(BF16)
                yield
                dvn_cols, dws = [], []
                for pair in range(N_GROUPS // 2):
                    dp_ = dmxb[:, pair * LANES:(pair + 1) * LANES]
                    vp = vb[:, pair * LANES:(pair + 1) * LANES]
                    dvn_cols.append(
                        jnp.where(low, _dot_tn(wsm_ref[2 * pair], dp_), _dot_tn(wsm_ref[2 * pair + 1], dp_)))
                    zero = jnp.zeros_like(dp_)
                    dws.append(_dot_nt(jnp.where(low, dp_, zero), vp))
                    dws.append(_dot_nt(jnp.where(low, zero, dp_), vp))
                dvn = jnp.concatenate(dvn_cols, axis=1)
                yield
                dxh = dvn * glng_ref[...]
                am = amat_ref[...]
                m1 = _split_dot(dxh, am)
                m2 = _split_dot(dxh * xhat, am)
                yield
                da = rstd * (dxh - m1 - xhat * m2)
                dgv_ref[rows[b], :] = (da * _gelu_grad(gv_ref[rows[b], :], ta)).astype(BF16)
                return dmx, dws, _colsum(dvn * xhat), _colsum(dvn)

            def attention_bwd(b, d_attn, probs, kvar, vvar):
                heads = range(N_HEADS)
                sels = [low if h % 2 == 0 else jnp.logical_not(low) for h in heads]
                pair_of = lambda a, h: a[:, (h // 2) * LANES:(h // 2 + 1) * LANES]
                do_hs = [jnp.where(sels[h], pair_of(d_attn, h), 0.0).astype(BF16) for h in heads]
                q_hs = [jnp.where(sels[h], pair_of(q_blks[b], h), jnp.zeros((BLOCK, LANES), BF16)) for h in heads]
                dps = [_dot_nt(do_hs[h], vvar[_head_kv(h)[0]][_head_kv(h)[1]]) for h in heads]
                yield
                deltas = [jnp.sum(probs[h][0] * dps[h], axis=-1, keepdims=True) for h in heads]
                yield
                dss = [probs[h][0] * (dps[h] - deltas[h]) for h in heads]
                dsinks = [-(probs[h][1] * deltas[h]) for h in heads]
                dsbs = [ds.astype(BF16) for ds in dss]
                pbs = [probs[h][0].astype(BF16) for h in heads]
                yield
                dqs = [_dot(dsbs[h], kvar[_head_kv(h)[0]][_head_kv(h)[1]]) for h in heads]
                tks = [_dot_tn(dsbs[h], q_hs[h]) for h in heads]
                tvs = [_dot_tn(pbs[h], do_hs[h]) for h in heads]
                dq_cols = [dqs[2 * i] + dqs[2 * i + 1] for i in range(N_HEADS // 2)]
                dq_ref[rows[b], :] = (jnp.concatenate(dq_cols, axis=1) * Q_SCALE).astype(BF16)
                per_kv = N_HEADS // N_KV
                kv_sum = lambda ts, kvh: sum(ts[kvh * per_kv + 1:(kvh + 1) * per_kv], ts[kvh * per_kv])
                dkk = _fold_kv(kv_sum(tks, 0), kv_sum(tks, 1))
                dvv = _fold_kv(kv_sum(tvs, 0), kv_sum(tvs, 1))
                return jnp.concatenate([dkk, dvv], axis=1), dss, dsinks

            bwd, rms_g = [], []
            for b in range(2):
                attn, probs, kvar, vvar = res[2 * b]
                gm, saved = res[2 * b + 1]
                na_unit, r_a = _rms(attn, 1.0)
                ng_unit, r_g = _rms(gm, 1.0)
                dmix = dmix_ref[rows[b], :].astype(F32)
                dn_a = dmix[:, :ATTN_W]
                dn_g = dmix[:, ATTN_W:]
                rms_g.append((_colsum(dn_a * na_unit), _colsum(dn_g * ng_unit)))
                t_a = dn_a * aog_ref[...]
                d_attn = r_a * t_a - na_unit * (r_a * jnp.mean(t_a * na_unit, axis=-1, keepdims=True))
                t_g = dn_g * gog_ref[...]
                d_gm = r_g * t_g - ng_unit * (r_g * jnp.mean(t_g * ng_unit, axis=-1, keepdims=True))
                bwd.append(attention_bwd(b, d_attn, probs, kvar, vvar))
                bwd.append(gating_bwd(b, d_gm, saved))
            (dkv_a, dss_a, dsk_a), (dmx_a, dws_a, glg_a, glb_a) = _interleave(*bwd[:2])
            (dkv_b, dss_b, dsk_b), (dmx_b, dws_b, glg_b, glb_b) = _interleave(*bwd[2:])

            vec_ref[0:1, :] += rms_g[0][0] + rms_g[1][0]
            vec_ref[1:2, :] += rms_g[0][1] + rms_g[1][1]
            vec_ref[2:3, :] += glg_a + glg_b
            vec_ref[3:4, :] += glb_a + glb_b
            dbs_ref[...] += dmx_a + dmx_b
            for g in range(N_GROUPS):
                dws_ref[g] += dws_a[g] + dws_b[g]
            for h in range(N_HEADS):
                gbias_ref[h] += dss_a[h] + dss_b[h]
                dsink_ref[h] += dsk_a[h] + dsk_b[h]

            dkv_ref[:BLOCK, :] = done[...].astype(BF16)
            dkv_ref[BLOCK:, :] = (carry[...] + dkv_a[:BLOCK]).astype(BF16)
            done[...] = dkv_a[BLOCK:] + dkv_b[:BLOCK]
            carry[...] = dkv_b[BLOCK:]

    last = n_steps - 1
    cur = lambda w: pl.BlockSpec((tile, w), lambda n: (jnp.minimum(n, last), 0))
    late = lambda w: pl.BlockSpec((tile, w), lambda n: (jnp.clip(n - 1, 0, last), 0))
    before = pl.BlockSpec((BLOCK, 2 * KV_W), lambda n: (jnp.clip(2 * n - 1, 0, 2 * last + 1), 0))
    outs = pl.pallas_call(
        body, name="bwd_mix",
        out_shape=[jax.ShapeDtypeStruct((s, ATTN_W), BF16), jax.ShapeDtypeStruct((s, 2 * KV_W), BF16),
                   jax.ShapeDtypeStruct((s, GMLP_W), BF16), jax.ShapeDtypeStruct((s, GMLP_W), BF16),
                   jax.ShapeDtypeStruct((N_HEADS, BLOCK, 2 * BLOCK), F32),
                   jax.ShapeDtypeStruct((N_GROUPS, BLOCK, BLOCK), F32),
                   jax.ShapeDtypeStruct((BLOCK, GMLP_W), F32), jax.ShapeDtypeStruct((8, GMLP_W), F32),
                   jax.ShapeDtypeStruct((N_HEADS, BLOCK, 1), F32)]
        + [jax.ShapeDtypeStruct(_rx_shape(p.shape, k), BF16) for p, k in zip(grad_parts, grad_kinds)],
        grid=(n_steps + 1,),
        in_specs=[cur(ATTN_W), cur(2 * KV_W), before, cur(GMLP_W), cur(GMLP_W), cur(D_MODEL),
                  _const_spec((N_HEADS, BLOCK, 2 * BLOCK)), pl.BlockSpec(memory_space=pltpu.SMEM),
                  _const_spec((1, GMLP_W)), _const_spec((1, GMLP_W)), _const_spec((N_GROUPS, BLOCK, BLOCK)),
                  _const_spec((BLOCK, GMLP_W)), _const_spec((GMLP_W, GMLP_W)), _const_spec((1, ATTN_W)),
                  _const_spec((1, GMLP_W))] + [ANY] * n_g,
        out_specs=[cur(ATTN_W), late(2 * KV_W), cur(GMLP_W), cur(GMLP_W),
                   _const_spec((N_HEADS, BLOCK, 2 * BLOCK)), _const_spec((N_GROUPS, BLOCK, BLOCK)),
                   _const_spec((BLOCK, GMLP_W)), _const_spec((8, GMLP_W)), _const_spec((N_HEADS, BLOCK, 1))]
        + [ANY] * n_g,
        scratch_shapes=[pltpu.VMEM((BLOCK, 2 * KV_W), F32), pltpu.VMEM((BLOCK, 2 * KV_W), F32)]
        + _ChipExchange.sems(n_g),
        compiler_params=_params(("arbitrary",)),
    )(q, kv, kv, gu, gv, dmix, bias, sinks, gln_g, gln_b, wsm, bsx, amat, aog, gog, *grad_parts)
    return outs[:9], outs[9:]


def _mix_finalize(gbias, bucket, dws, dbs, dsink):
    def body(gb_ref, bucket_ref, dws_ref, dbs_ref, dsink_ref, tall_ref):
        bk = bucket_ref[...]
        lane = lax.broadcasted_iota(jnp.int32, (N_BUCKETS, LANES), 1)
        rowi = lax.broadcasted_iota(jnp.int32, (N_BUCKETS, LANES), 0)
        drb = jnp.zeros((N_BUCKETS, LANES), F32)
        dsk = jnp.zeros((8, LANES), F32)
        lane8 = lax.broadcasted_iota(jnp.int32, (8, LANES), 1)
        for h in range(N_HEADS):
            g = gb_ref[h]
            for b in range(N_BUCKETS):
                tot = jnp.sum(_colsum(jnp.where(bk == float(b), g, 0.0)), axis=1, keepdims=True)
                drb = jnp.where((rowi == h) & (lane == b), tot, drb)
            sk = jnp.sum(dsink_ref[h], axis=0, keepdims=True)
            dsk = jnp.where(lane8 == h, sk, dsk)
        tall_ref[TALL_RB:TALL_RB + N_BUCKETS, :] = drb
        tall_ref[TALL_SK:TALL_SK + 8, :] = dsk
        ti = lax.broadcasted_iota(jnp.int32, (BLOCK, BLOCK), 0)
        ui = lax.broadcasted_iota(jnp.int32, (BLOCK, BLOCK), 1)
        for g in range(N_GROUPS):
            tall_ref[g * BLOCK:(g + 1) * BLOCK, :] = jnp.where(ti >= ui, dws_ref[g], 0.0)
        gi = lax.broadcasted_iota(jnp.int32, (GMLP_W, LANES), 0) // GROUP_DIM
        li = lax.broadcasted_iota(jnp.int32, (GMLP_W, LANES), 1)
        ind = jnp.where(gi == li, 1.0, 0.0).astype(BF16)
        d = dbs_ref[...]
        hi = d.astype(BF16)
        r1 = d - hi.astype(F32)
        mid = r1.astype(BF16)
        lo = (r1 - mid.astype(F32)).astype(BF16)
        dbsg = _dot(hi, ind) + _dot(mid, ind) + _dot(lo, ind)
        tall_ref[TALL_BS:TALL_BS + N_GROUPS, :] = dbsg.T[:N_GROUPS, :]

    return pl.pallas_call(
        body, name="mix_finalize", out_shape=jax.ShapeDtypeStruct((TALL_ROWS, LANES), F32), grid=(1,),
        in_specs=[_const_spec((N_HEADS, BLOCK, 2 * BLOCK)), _const_spec((BLOCK, 2 * BLOCK)),
                  _const_spec((N_GROUPS, BLOCK, BLOCK)), _const_spec((BLOCK, GMLP_W)),
                  _const_spec((N_HEADS, BLOCK, 1))],
        out_specs=_const_spec((TALL_ROWS, LANES)),
        compiler_params=_params(("arbitrary",)),
    )(gbias, bucket, dws, dbs, dsink)


def _bwd_in(dq, dkv, dgu, dgv, dxa, x, modr, w_in, tm):
    s = x.shape[0]
    n_steps = s // tm
    streams = (dq, dkv, dgu, dgv, dxa, x)
    n_in = len(streams)

    def body(*refs):
        srcs = refs[:n_in]
        mod_ref, w_ref, gx_ref, acc_ref, db_ref = refs[n_in:n_in + 5]
        bufs, sems = refs[n_in + 5:2 * n_in + 5], refs[2 * n_in + 5]
        i = pl.program_id(0)

        def ring_copies(t):
            slot = t % RING_SLOTS
            rows = pl.ds(pl.multiple_of(t * tm, tm), tm)
            return [pltpu.make_async_copy(src.at[rows, :], buf.at[slot], sems.at[slot * n_in + j])
                    for j, (src, buf) in enumerate(zip(srcs, bufs))]

        @pl.when(i == 0)
        def _():
            acc_ref[...] = jnp.zeros_like(acc_ref)
            db_ref[...] = jnp.zeros_like(db_ref)
            for t in range(RING_SLOTS - 1):
                for cp in ring_copies(t):
                    cp.start()

        @pl.when(i + RING_SLOTS - 1 < n_steps)
        def _():
            for cp in ring_copies(i + RING_SLOTS - 1):
                cp.start()

        for cp in ring_copies(i):
            cp.wait()
        dqv, dkvv, dguv, dgvv, dxav, xv = (buf[i % RING_SLOTS] for buf in bufs)
        dproj = jnp.concatenate([dqv, dkvv, dguv, dgvv], axis=1)
        dh1 = _dot(dproj, w_ref[...])
        gx_ref[...] = dxav.astype(F32) + dh1 * (1.0 + mod_ref[1:2, :])
        acc_ref[0:1, :] += _colsum(dh1 * xv.astype(F32))
        acc_ref[1:2, :] += _colsum(dh1)
        db_ref[0:1, :] += _colsum(dproj.astype(F32))

    row = lambda w: pl.BlockSpec((tm, w), lambda i: (i, 0))
    return pl.pallas_call(
        body, name="bwd_in",
        out_shape=(jax.ShapeDtypeStruct((s, D_MODEL), F32), jax.ShapeDtypeStruct((8, D_MODEL), F32),
                   jax.ShapeDtypeStruct((8, IN_W), F32)),
        grid=(n_steps,),
        in_specs=[ANY] * n_in + [_const_spec((8, D_MODEL)), _const_spec(w_in.shape)],
        out_specs=(row(D_MODEL), _const_spec((8, D_MODEL)), _const_spec((8, IN_W))),
        scratch_shapes=[pltpu.VMEM((RING_SLOTS, tm, a.shape[1]), a.dtype) for a in streams]
        + [pltpu.SemaphoreType.DMA((RING_SLOTS * n_in,))],
        compiler_params=_params(("arbitrary",)),
    )(*streams, modr, w_in)


def _wgrad(a, bs, tm, tk, name, transposed=False, gather_vs=()):
    k_all, m = a.shape
    n = sum(b.shape[1] for b in bs)
    nk = k_all // tk
    nm = m // tm
    n_b = len(bs)
    n_v = len(gather_vs)

    def body(a_ref, *rest):
        b_refs, v_refs = rest[:n_b], rest[n_b:n_b + n_v]
        o_ref, ob_ref = rest[n_b + n_v:n_b + n_v + 2]
        vg_refs = rest[n_b + n_v + 2:n_b + 2 * n_v + 2]
        i, k = pl.program_id(0), pl.program_id(1)
        if n_v:
            gather = _Gather8(v_refs, vg_refs, *rest[n_b + 2 * n_v + 2:])

            @pl.when((i == 0) & (k == 0))
            def _():
                gather.start()

            @pl.when((i == nm - 1) & (k == 0))
            def _():
                gather.forward()

        @pl.when(k == 0)
        def _():
            o_ref[...] = jnp.zeros_like(o_ref)

        b = b_refs[0][...] if n_b == 1 else jnp.concatenate([r[...] for r in b_refs], axis=1)
        if transposed:
            o_ref[...] += _dot_tn(b, a_ref[...])
        else:
            o_ref[...] += _dot_tn(a_ref[...], b)

        @pl.when(k == nk - 1)
        def _():
            ob_ref[...] = o_ref[...].astype(BF16)

        if n_v:
            @pl.when((i == nm - 1) & (k == nk - 1))
            def _():
                gather.finish()

    if transposed:
        out_spec = pl.BlockSpec((n, tm), lambda i, k: (0, i))
        shape = (n, m)
    else:
        out_spec = pl.BlockSpec((tm, n), lambda i, k: (i, 0))
        shape = (m, n)
    outs = pl.pallas_call(
        body, name=name,
        out_shape=[jax.ShapeDtypeStruct(shape, F32), jax.ShapeDtypeStruct(shape, BF16)] + _gathered8_shapes(gather_vs),
        grid=(nm, nk),
        in_specs=[pl.BlockSpec((tk, tm), lambda i, k: (k, i))]
        + [pl.BlockSpec((tk, b.shape[1]), lambda i, k: (k, 0)) for b in bs] + [ANY] * n_v,
        out_specs=[out_spec, out_spec] + [ANY] * n_v,
        scratch_shapes=_Gather8.sems(n_v) if n_v else [],
        compiler_params=_params(("arbitrary", "arbitrary") if n_v else ("parallel", "arbitrary")),
    )(a, *bs, *gather_vs)
    return outs[0], outs[1], outs[2:]


def _adam_math(w, g, m, v):
    m2 = ADAM_B1 * m + (1.0 - ADAM_B1) * g
    v2 = ADAM_B2 * v + (1.0 - ADAM_B2) * (g * g)
    m_hat = m2 / (1.0 - ADAM_B1 ** ADAM_STEP)
    v_hat = v2 / (1.0 - ADAM_B2 ** ADAM_STEP)
    delta = -ADAM_LR * (m_hat / (jnp.sqrt(v_hat) + ADAM_EPS) + ADAM_WD * w)
    return delta, m2, v2


def _adam_halves(w, mine, got, m, v, tr, name):
    r, cc = w.shape
    h = r // 2
    nt = h // tr

    def body(c_ref, w_ref, mine_ref, got_ref, m_ref, v_ref, g_ref, d_ref, m2_ref, v2_ref):
        g = jnp.where(pl.program_id(0) == c_ref[0], mine_ref[...], got_ref[...])
        g_ref[...] = g
        d, m2, v2 = _adam_math(w_ref[...], g, m_ref[...], v_ref[...])
        d_ref[...] = d
        m2_ref[...] = m2
        v2_ref[...] = v2

    full = pl.BlockSpec((tr, cc), lambda hh, i, c_ref: (hh * nt + i, 0))
    half = pl.BlockSpec((tr, cc), lambda hh, i, c_ref: (i, 0))
    shp = jax.ShapeDtypeStruct((r, cc), F32)
    return pl.pallas_call(
        body, name=name, out_shape=(shp, shp, shp, shp),
        grid_spec=pltpu.PrefetchScalarGridSpec(
            num_scalar_prefetch=1, grid=(2, nt), in_specs=[full, half, half, full, full],
            out_specs=(full, full, full, full)),
        compiler_params=_params(("arbitrary", "arbitrary")),
    )(_core_index_scalar(), w, mine, got, m, v)


def _adam_w_ada(sc_t, dmod_all, w, m, v, tr):
    r, cc = w.shape

    def body(chip_ref, sct_ref, dm_ref, w_ref, m_ref, v_ref, g_ref, d_ref, m2_ref, v2_ref):
        g = sct_ref[:, 0:1] * dm_ref[0:1, :]
        for k in range(1, N_DEV):
            g = g + sct_ref[:, k:k + 1] * dm_ref[k:k + 1, :]
        g_ref[...] = g
        d, m2, v2 = _adam_math(w_ref[...], g, m_ref[...], v_ref[...])
        d_ref[...] = d
        m2_ref[...] = m2
        v2_ref[...] = v2

    spec = pl.BlockSpec((tr, cc), lambda i, chip_ref: (i, 0))
    shp = jax.ShapeDtypeStruct((r, cc), F32)
    return pl.pallas_call(
        body, name="adam_w_ada", out_shape=(shp, shp, shp, shp),
        grid_spec=pltpu.PrefetchScalarGridSpec(
            num_scalar_prefetch=1, grid=(r // tr,),
            in_specs=[pl.BlockSpec((tr, N_DEV), lambda i, chip_ref: (i, 0)),
                      pl.BlockSpec((N_DEV, cc), lambda i, chip_ref: (0, chip_ref[0])), spec, spec, spec],
            out_specs=(spec, spec, spec, spec)),
        compiler_params=_params(("parallel",)),
    )(_chip_index_scalar(), sc_t, dmod_all, w, m, v)


def _pack_wide(acc_i, acc_m, acc_f, db_in, vec):
    arrs = [acc_i, acc_m, acc_f, db_in, vec]
    i_, m_, f_, b_, v_ = range(5)
    src = {"b_in": (b_, 0), "ln1_g": (m_, 2), "ln1_b": (m_, 3), "ln2_g": (f_, 1), "ln2_b": (f_, 2),
           "gmlp_ln_g": (v_, 2), "gmlp_ln_b": (v_, 3), "attn_out_g": (v_, 0), "gmlp_out_g": (v_, 1), "loss": (f_, 0)}
    dmod = [(i_, 1), (i_, 0), (m_, 4), (m_, 1), (m_, 0), (f_, 3)]

    def body(*refs):
        ins, wide_ref = refs[:5], refs[5]
        wide_ref[...] = jnp.zeros_like(wide_ref)
        for k, (a, row) in enumerate(dmod):
            wide_ref[0:1, k * D_MODEL:(k + 1) * D_MODEL] = ins[a][row:row + 1, :]
        for name, (a, row) in src.items():
            r, off, n = WIDE_LAYOUT[name]
            wide_ref[r:r + 1, off:off + n] = ins[a][row:row + 1, :]

    return pl.pallas_call(
        body, name="pack_wide", out_shape=jax.ShapeDtypeStruct((8, WIDE_W), F32), grid=(1,),
        in_specs=[_const_spec(a.shape) for a in arrs], out_specs=_const_spec((8, WIDE_W)),
        compiler_params=_params(("arbitrary",)),
    )(*arrs)


def _adam_small(gw, gt, wide_wmv, w_s, b_s, rel_bias, sinks, after):
    names = list(WIDE_PARAMS)
    tall = [("gmlp_w_s", w_s), ("gmlp_b_s", b_s), ("rel_bias", rel_bias), ("attn_sinks", sinks)]
    ins = [gw, gt]
    for n in names:
        ins += list(wide_wmv[n])
    for _, t in tall:
        ins += list(t)
    n_in = len(ins)

    def body(*refs):
        gw_ref, gt_ref = refs[0], refs[1]
        wmv = refs[2:n_in]
        dmod_ref, loss_ref, loss1_ref = refs[n_in + 1:n_in + 4]
        outs = refs[n_in + 4:]

        def tall_sum(r0, nr):
            g = gt_ref[r0:r0 + nr, :]
            for d in range(1, N_DEV):
                g = g + gt_ref[d * TALL_ROWS + r0:d * TALL_ROWS + r0 + nr, :]
            return g

        def emit(k, g, w_ref, m_ref, v_ref):
            d, m2, v2 = _adam_math(w_ref[...], g, m_ref[...], v_ref[...])
            outs[4 * k][...] = g
            outs[4 * k + 1][...] = d
            outs[4 * k + 2][...] = m2
            outs[4 * k + 3][...] = v2

        gsum = gw_ref[0:8, :]
        for d in range(1, N_DEV):
            gsum = gsum + gw_ref[8 * d:8 * d + 8, :]
        for d in range(N_DEV):
            dmod_ref[d:d + 1, :] = gw_ref[8 * d:8 * d + 1, :]
        for k, n in enumerate(names):
            r, off, sz = WIDE_LAYOUT[n]
            emit(k, gsum[r:r + 1, off:off + sz], *wmv[3 * k:3 * k + 3])
        r, off, sz = WIDE_LAYOUT["loss"]
        tot = jnp.sum(gsum[r:r + 1, off:off + sz], axis=1, keepdims=True)
        loss_ref[...] = jnp.broadcast_to(tot * (0.5 / D_MODEL), loss_ref.shape)
        loss1_ref[...] = tot * (0.5 / D_MODEL)

        k0 = len(names)
        ws_refs = wmv[3 * k0:3 * k0 + 3]
        for g in range(N_GROUPS):
            rows = slice(g * BLOCK, (g + 1) * BLOCK)
            gg = tall_sum(g * BLOCK, BLOCK)
            d, m2, v2 = _adam_math(ws_refs[0][rows, :], gg, ws_refs[1][rows, :], ws_refs[2][rows, :])
            outs[4 * k0][rows, :] = gg
            outs[4 * k0 + 1][rows, :] = d
            outs[4 * k0 + 2][rows, :] = m2
            outs[4 * k0 + 3][rows, :] = v2
        emit(k0 + 1, tall_sum(TALL_BS, N_GROUPS), *wmv[3 * (k0 + 1):3 * (k0 + 1) + 3])
        emit(k0 + 2, tall_sum(TALL_RB, N_HEADS)[:, :N_BUCKETS], *wmv[3 * (k0 + 2):3 * (k0 + 2) + 3])
        emit(k0 + 3, tall_sum(TALL_SK, 8)[0:1, :N_HEADS], *wmv[3 * (k0 + 3):3 * (k0 + 3) + 3])

    out_shapes = [jax.ShapeDtypeStruct((N_DEV, WIDE_W), F32), jax.ShapeDtypeStruct((8, LANES), F32),
                  jax.ShapeDtypeStruct((1, 1), F32)]
    for n in names:
        out_shapes += [jax.ShapeDtypeStruct(wide_wmv[n][0].shape, F32)] * 4
    for _, t in tall:
        out_shapes += [jax.ShapeDtypeStruct(t[0].shape, F32)] * 4
    res = pl.pallas_call(
        body, name="adam_small", out_shape=out_shapes, grid=(1,),
        in_specs=[_const_spec(a.shape) for a in ins] + [ANY], out_specs=[_const_spec(o.shape) for o in out_shapes],
        compiler_params=_params(("arbitrary",)),
    )(*ins, after)
    out = {}
    for k, n in enumerate(names + [t[0] for t in tall]):
        out[n] = tuple(res[3 + 4 * k:7 + 4 * k])
    return res[0], res[1], res[2], out


def kernel(x, c, rel_bias, w_ada, b_ada, w_in, b_in, attn_sinks, gmlp_ln_g, gmlp_ln_b, gmlp_w_s, gmlp_b_s, attn_out_g, gmlp_out_g, w_out, ln1_g, ln1_b, w_gate_up, w_down, ln2_g, ln2_b, loss_target, m_rel_bias, m_w_ada, m_b_ada, m_w_in, m_b_in, m_attn_sinks, m_gmlp_ln_g, m_gmlp_ln_b, m_gmlp_w_s, m_gmlp_b_s, m_attn_out_g, m_gmlp_out_g, m_w_out, m_ln1_g, m_ln1_b, m_w_gate_up, m_w_down, m_ln2_g, m_ln2_b, v_rel_bias, v_w_ada, v_b_ada, v_w_in, v_b_in, v_attn_sinks, v_gmlp_ln_g, v_gmlp_ln_b, v_gmlp_w_s, v_gmlp_b_s, v_attn_out_g, v_gmlp_out_g, v_w_out, v_ln1_g, v_ln1_b, v_w_gate_up, v_w_down, v_ln2_g, v_ln2_b):
    ix, iy, _ = _my_pos()
    chip = 2 * ix + iy
    s = x.shape[1]
    xs = x[0]
    tgt = loss_target[0]
    tm_big = min(512, s)
    tm_ffn = min(FFN_SUB, s)

    sc_all, modr, (w_in_g, w_out_g) = _prologue(
        jnp.pad(c, ((0, 7), (0, 0))), w_ada[0], b_ada, [_with_own_block(w_in[0].T, chip), _with_own_block(w_out[0], chip)])
    w_in_f = w_in_g.reshape(IN_W, D_MODEL)

    bucket = _bucket_table()
    bias, wsm = _prep_tables(bucket, rel_bias.T, gmlp_w_s[0])
    bsx = jnp.repeat(gmlp_b_s[0].T, GROUP_DIM, axis=1)
    amat = _group_mean_matrix()
    sinks = attn_sinks[0]

    (h1, q, kv, gu, gv, xb), (w_dn_g,) = _fwd_in(xs, modr, w_in_f, b_in, tm_big, [_with_own_block(w_down[0], chip)],
                                                 ["blk"])
    w_out_f = w_out_g.reshape(D_MODEL, D_MODEL)
    (x1, x1b, y, mixed), (w_gu_f,) = _fwd_mix(
        q, kv, gu, gv, xs, modr, bias, sinks, gmlp_ln_g, gmlp_ln_b, wsm, bsx, amat, attn_out_g, gmlp_out_g, w_out_f,
        ln1_g, ln1_b, tm_big, [_with_own_block(w_gate_up[0], chip)], ["blk"])
    assert w_gate_up.shape[2] == FF_CHUNK
    w_dn_f = w_dn_g.reshape(D_FF, D_MODEL)
    h2, act, dy2, dx1a, acc_f = _fwd_ffn(x1, tgt, modr, ln2_g, ln2_b, w_gu_f, w_dn_f, min(2 * FFN_SUB, s))

    a_act, dgu_ff, dh2 = _bwd_ffn(dy2, act, w_gu_f, w_dn_f, min(FFN_SUB, s))
    g_dn, g_dn_b, _ = _wgrad(a_act, [dy2], D_FF // 2, min(1024, s), "wgrad_down")
    g_gu, g_gu_b, _ = _wgrad(h2, [dgu_ff], 512, min(512, s), "wgrad_gate_up")
    blk3 = lambda a, rows: a.reshape(N_CHIPS, rows, a.shape[1])
    (dxa, dy, dmix, acc_m), (got_dn, got_gu) = _bwd_mid(
        dh2, dx1a, x1b, xs, y, modr, ln1_g, w_out_f, tm_big, [blk3(g_dn_b, D_FF // N_CHIPS), g_gu_b], ["blk", "cols"])
    g_out, g_out_b, _ = _wgrad(mixed, [dy], 512, min(2048, s), "wgrad_out")
    (got_out,) = _swap_halves([blk3(g_out_b, D_MODEL // N_CHIPS)], ["blk"], "rs_swap_out")
    kinds_a = ["blk", "cols", "blk"]
    fulls_a = [blk3(g_dn, D_FF // N_CHIPS), g_gu, blk3(g_out, D_MODEL // N_CHIPS)]
    gots_a = [got_dn, got_gu, got_out]
    parts_a = [_add_halves(f, g, k, "rs_add_a%d" % i) for i, (f, g, k) in enumerate(zip(fulls_a, gots_a, kinds_a))]
    (dq, dkv, dgu, dgv, gbias, dws, dbs, vec, dsink), rxs_a = _bwd_mix(
        q, kv, gu, gv, dmix, bias, sinks, gmlp_ln_g, gmlp_ln_b, wsm, bsx, amat, attn_out_g, gmlp_out_g,
        [p[1] for p in parts_a], kinds_a)
    tall_g = _mix_finalize(gbias, bucket, dws, dbs, dsink)
    grad_x, acc_i, db_in = _bwd_in(dq, dkv, dgu, dgv, dxa, xb, modr, w_in_f, tm_big)

    wide_g = _pack_wide(acc_i, acc_m, acc_f, db_in, vec)
    full_in, full_in_b, (gw, gt) = _wgrad(h1, [dq, dkv, dgu, dgv], 512, min(1024, s), "wgrad_in", transposed=True,
                                          gather_vs=[wide_g, tall_g])
    (got_in,) = _swap_halves([blk3(full_in_b, IN_W // N_CHIPS)], ["blk"], "rs_swap_in")
    part_in = _add_halves(blk3(full_in, IN_W // N_CHIPS), got_in, "blk", "rs_add_in")
    in_send, in_recv, in_part, in_rx, token = _exchange_start(part_in[1], "rs_chips_in_start")
    wide_wmv ={"b_ada": (b_ada, m_b_ada, v_b_ada), "b_in": (b_in, m_b_in, v_b_in),
                "ln1_g": (ln1_g, m_ln1_g, v_ln1_g), "ln1_b": (ln1_b, m_ln1_b, v_ln1_b),
                "ln2_g": (ln2_g, m_ln2_g, v_ln2_g), "ln2_b": (ln2_b, m_ln2_b, v_ln2_b),
                "gmlp_ln_g": (gmlp_ln_g, m_gmlp_ln_g, v_gmlp_ln_g), "gmlp_ln_b": (gmlp_ln_b, m_gmlp_ln_b, v_gmlp_ln_b),
                "attn_out_g": (attn_out_g, m_attn_out_g, v_attn_out_g),
                "gmlp_out_g": (gmlp_out_g, m_gmlp_out_g, v_gmlp_out_g)}
    rows2 = lambda a: a.reshape(-1, a.shape[-1])
    dmod_all, loss_t, loss1, small = _adam_small(
        gw, gt, wide_wmv, tuple(rows2(a) for a in (gmlp_w_s, m_gmlp_w_s, v_gmlp_w_s)),
        tuple(rows2(a) for a in (gmlp_b_s, m_gmlp_b_s, v_gmlp_b_s)), (rel_bias.T, m_rel_bias.T, v_rel_bias.T),
        (attn_sinks, m_attn_sinks, v_attn_sinks), token)
    small["rel_bias"] = tuple(a.T for a in small["rel_bias"])
    loss = loss1.reshape(())

    g_ada, d_ada, m_ada, v_ada = _adam_w_ada(sc_all.T, dmod_all, w_ada[0], m_w_ada[0], v_w_ada[0], 256)

    sums = [(parts_a[0][0], rxs_a[0], 176), (parts_a[1][0], rxs_a[1], 256), (parts_a[2][0], rxs_a[2], 128)]
    mine = [_sum_chips(p, rx, tr, "rs_sum_%d" % i, loss_t) for i, (p, rx, tr) in enumerate(sums)]
    got = _share_halves(mine, "rs_share")
    gs_dn, d_dn, m_dn, v_dn = _adam_halves(w_down[0], mine[0], got[0], m_w_down[0], v_w_down[0], 176, "adam_w_down")
    gs_gu, d_gu, m_gu, v_gu = _adam_halves(w_gate_up[0], mine[1], got[1], m_w_gate_up[0], v_w_gate_up[0], 256,
                                           "adam_w_gate_up")
    gs_out, d_out, m_out, v_out = _adam_halves(w_out[0], mine[2], got[2], m_w_out[0], v_w_out[0], 128, "adam_w_out")

    rx_in = _exchange_wait(in_send, in_recv, in_part, in_rx, d_gu, "rs_chips_in_wait")
    mine_in = _sum_chips(part_in[0], rx_in, 112, "rs_sum_in", rx_in)
    (got_in_half,) = _share_halves([mine_in], "rs_share_in")
    in_t = _adam_halves(w_in[0].T, mine_in, got_in_half, m_w_in[0].T, v_w_in[0].T, 112, "adam_w_in")
    gs_in, d_in, m_in, v_in = (a.T for a in in_t)

    big = {"w_ada": (g_ada, d_ada, m_ada, v_ada), "w_in": (gs_in, d_in, m_in, v_in), "w_out": (gs_out, d_out, m_out, v_out),
           "w_gate_up": (gs_gu, d_gu, m_gu, v_gu), "w_down": (gs_dn, d_dn, m_dn, v_dn)}
    order = ["rel_bias", "w_ada", "b_ada", "w_in", "b_in", "attn_sinks", "gmlp_ln_g", "gmlp_ln_b", "gmlp_w_s", "gmlp_b_s",
             "attn_out_g", "gmlp_out_g", "w_out", "ln1_g", "ln1_b", "w_gate_up", "w_down", "ln2_g", "ln2_b"]
    shapes = {"gmlp_w_s": gmlp_w_s.shape, "gmlp_b_s": gmlp_b_s.shape}
    outs = [loss, grad_x[None]]
    for k in range(4):
        for name in order:
            if name in big:
                outs.append(big[name][k][None])
            elif name in shapes:
                outs.append(small[name][k].reshape(shapes[name]))
            else:
                outs.append(small[name][k])
    return tuple(outs)
```

```python
import math

import numpy as np
import jax
import jax.numpy as jnp
from jax import lax
from jax.experimental import pallas as pl
from jax.experimental.pallas import tpu as pltpu

F32 = jnp.float32
BF16 = jnp.bfloat16
MESH = pl.DeviceIdType.MESH

D_MODEL = 1024
N_HEADS = 8
N_KV = 2
HEAD_DIM = 64
ATTN_W = N_HEADS * HEAD_DIM
KV_W = N_KV * HEAD_DIM
N_GROUPS = 8
GROUP_DIM = 64
GMLP_W = N_GROUPS * GROUP_DIM
IN_W = ATTN_W + 2 * KV_W + 2 * GMLP_W
BLOCK = 128
N_BUCKETS = 32
MAX_DISTANCE = 128
D_FF = 2816
ALPHA = 2.0 ** 0.25
LN_EPS = 1e-5
NEG_INF = -1e30
ADAM_LR, ADAM_B1, ADAM_B2, ADAM_EPS, ADAM_WD, ADAM_STEP = 0.001, 0.9, 0.999, 1e-8, 0.01, 10
N_CHIPS = 4
N_DEV = 8
LANES = 128
V7X_VMEM_LIMIT = 56 * 2 ** 20
GELU_C = math.sqrt(2.0 / math.pi)
Q_SCALE = HEAD_DIM ** -0.5
ANY = pl.BlockSpec(memory_space=pl.ANY)

TALL_BS = N_GROUPS * BLOCK
TALL_RB = TALL_BS + 8
TALL_SK = TALL_RB + N_BUCKETS
TALL_ROWS = TALL_SK + 8
WIDE_W = 6 * D_MODEL
WIDE_LAYOUT = {
    "b_ada": (0, 0, 6 * D_MODEL),
    "b_in": (1, 0, IN_W), "ln1_g": (1, IN_W, D_MODEL), "ln1_b": (1, IN_W + D_MODEL, D_MODEL),
    "ln2_g": (1, IN_W + 2 * D_MODEL, D_MODEL), "ln2_b": (1, IN_W + 3 * D_MODEL, D_MODEL),
    "gmlp_ln_g": (2, 0, GMLP_W), "gmlp_ln_b": (2, GMLP_W, GMLP_W), "attn_out_g": (2, 2 * GMLP_W, ATTN_W),
    "gmlp_out_g": (2, 2 * GMLP_W + ATTN_W, GMLP_W), "loss": (2, 3 * GMLP_W + ATTN_W, D_MODEL)}
WIDE_PARAMS = tuple(n for n in WIDE_LAYOUT if n != "loss")


def _params(sem=None):
    return pltpu.CompilerParams(dimension_semantics=sem, vmem_limit_bytes=V7X_VMEM_LIMIT)


def _const_spec(shape, single=False):
    nd = len(shape)
    if single:
        return pl.BlockSpec(shape, lambda *_: (0,) * nd, pipeline_mode=pl.Buffered(1))
    return pl.BlockSpec(shape, lambda *_: (0,) * nd)


def _dot(a, b):
    return jnp.dot(a, b, preferred_element_type=F32)


def _dot_nt(a, b):
    return lax.dot_general(a, b, (((1,), (1,)), ((), ())), preferred_element_type=F32)


def _dot_tn(a, b):
    return lax.dot_general(a, b, (((0,), (0,)), ((), ())), preferred_element_type=F32)


def _gelu(x):
    t = jnp.tanh(GELU_C * (x + 0.044715 * x * x * x))
    return 0.5 * x * (1.0 + t), t


def _gelu_grad(x, t):
    return 0.5 * (1.0 + t) + 0.5 * x * (1.0 - t * t) * GELU_C * (1.0 + 3.0 * 0.044715 * x * x)


def _split_dot(x, a):
    hi = x.astype(BF16)
    lo = (x - hi.astype(F32)).astype(BF16)
    return _dot(hi, a) + _dot(lo, a)


def _group_mean_matrix():
    g = np.arange(GMLP_W) // GROUP_DIM
    return jnp.asarray((g[:, None] == g[None, :]).astype(np.float32) / GROUP_DIM, dtype=BF16)


def _ln_stats(z):
    mu = jnp.mean(z, axis=-1, keepdims=True)
    d = z - mu
    var = jnp.mean(d * d, axis=-1, keepdims=True)
    rstd = lax.rsqrt(var + LN_EPS)
    return d * rstd, rstd


def _ln_bwd(dxhat, xhat, rstd):
    m1 = jnp.mean(dxhat, axis=-1, keepdims=True)
    m2 = jnp.mean(dxhat * xhat, axis=-1, keepdims=True)
    return rstd * (dxhat - m1 - xhat * m2)


def _colsum(x):
    return jnp.sum(x, axis=0, keepdims=True)


def _my_pos():
    return lax.axis_index("x"), lax.axis_index("y"), lax.axis_index("c")


def _other_chips(x, y):
    return [(1 - x, y), (x, 1 - y), (1 - x, 1 - y)]


def _chip_index_scalar():
    ix, iy, _ = _my_pos()
    return jnp.reshape(2 * ix + iy, (1,)).astype(jnp.int32)


def _core_index_scalar():
    return jnp.reshape(lax.axis_index("c"), (1,)).astype(jnp.int32)


class _Gather8:
    def __init__(self, x_refs, out_refs, send_sems, recv_sems, local_sems):
        self.x_refs, self.out_refs = x_refs, out_refs
        self.send_sems, self.recv_sems, self.local_sems = send_sems, recv_sems, local_sems
        self.x, self.y, self.c = _my_pos()
        self.me, self.sibling = (self.x, self.y, self.c), (self.x, self.y, 1 - self.c)
        self.chips = _other_chips(self.x, self.y)

    def _rows(self, a, px, py, pc):
        m_per = self.x_refs[a].shape[0]
        return self.out_refs[a].at[pl.ds((4 * px + 2 * py + pc) * m_per, m_per), :]

    def _copy(self, a, k, block, to, src=None):
        return pltpu.make_async_remote_copy(
            src_ref=self._rows(a, *block) if src is None else src, dst_ref=self._rows(a, *block),
            send_sem=self.send_sems.at[7 * a + k], recv_sem=self.recv_sems.at[7 * a + k], device_id=to,
            device_id_type=MESH)

    def _local(self, a):
        return pltpu.make_async_copy(self.x_refs[a], self._rows(a, *self.me), self.local_sems.at[a])

    def start(self):
        for a in range(len(self.x_refs)):
            self._local(a).start()
            self._copy(a, 0, self.me, self.sibling, src=self.x_refs[a]).start()
            for j, chip in enumerate(self.chips):
                self._copy(a, 1 + j, self.me, (*chip, self.c), src=self.x_refs[a]).start()

    def forward(self):
        for a in range(len(self.x_refs)):
            for j, chip in enumerate(self.chips):
                self._copy(a, 1 + j, (*chip, self.c), self.me).wait_recv()
                self._copy(a, 4 + j, (*chip, self.c), self.sibling).start()

    def finish(self):
        for a in range(len(self.x_refs)):
            self._copy(a, 0, self.sibling, self.me).wait_recv()
            for j, chip in enumerate(self.chips):
                self._copy(a, 4 + j, (*chip, 1 - self.c), self.me).wait_recv()
        for a in range(len(self.x_refs)):
            for k in range(7):
                self._copy(a, k, self.me, self.me).wait_send()
            self._local(a).wait()

    @staticmethod
    def sems(n_v):
        return [pltpu.SemaphoreType.DMA((7 * n_v,)), pltpu.SemaphoreType.DMA((7 * n_v,)),
                pltpu.SemaphoreType.DMA((n_v,))]


def _gathered8_shapes(vs):
    return [jax.ShapeDtypeStruct((N_DEV * v.shape[0], v.shape[1]), v.dtype) for v in vs]


VMEM_WHOLE = pl.BlockSpec(memory_space=pltpu.VMEM)


def _prologue(c_pad, w_ada_s, b_ada, shards):
    n = w_ada_s.shape[1]
    n_w = len(shards)
    assert N_CHIPS * n == 6 * D_MODEL and n % LANES == 0

    def body(c_ref, w_ref, b_ref, *rest):
        sc_ref, modc_ref, modg_ref, modr_ref = rest[n_w:n_w + 4]
        gathered_refs = rest[n_w + 4:2 * n_w + 4]
        call_ref, w_vmem = rest[2 * n_w + 4:2 * n_w + 6]
        sems = rest[2 * n_w + 6:]
        ix, iy, ic = _my_pos()
        chip = 2 * ix + iy
        weights = _WeightGather(gathered_refs, ["blk"] * n_w, sems[0], sems[1])
        gather_c = _Gather8([c_ref], [call_ref], sems[2], sems[3], sems[4])
        gather_mod = _Gather8([modc_ref], [modg_ref], sems[5], sems[6], sems[7])
        load_w = pltpu.make_async_copy(w_ref, w_vmem, sems[8])
        weights.start()
        gather_c.start()
        load_w.start()
        gather_c.forward()
        gather_c.finish()
        cv = call_ref[...]
        sc = cv * _sigmoid(cv)
        a_hi = sc.astype(BF16)
        a_lo = (sc - a_hi.astype(F32)).astype(BF16)
        load_w.wait()
        w = w_vmem[...]
        w_hi = w.astype(BF16)
        w_lo = (w - w_hi.astype(F32)).astype(BF16)
        b = b_ref[:, 0:n]
        for k in range(1, N_CHIPS):
            b = jnp.where(chip == k, b_ref[:, k * n:(k + 1) * n], b)
        mod = _dot(a_hi, w_hi) + _dot(a_hi, w_lo) + _dot(a_lo, w_hi) + b
        for d in range(N_DEV):
            sc_ref[d:d + 1, :] = sc[8 * d:8 * d + 1, :]
            modc_ref[d:d + 1, :] = mod[8 * d:8 * d + 1, :]
        gather_mod.start()
        weights.forward()
        gather_mod.forward()
        gather_mod.finish()
        dev = 2 * chip + ic
        mine = jnp.concatenate([modg_ref[pl.ds(2 * 8 * k + dev, 1), :] for k in range(N_CHIPS)], axis=1)
        modr_ref[...] = jnp.zeros_like(modr_ref)
        for r in range(6):
            modr_ref[r:r + 1, :] = mine[:, r * D_MODEL:(r + 1) * D_MODEL]
        weights.forward_diagonal()
        weights.finish()

    outs = pl.pallas_call(
        body, name="prologue",
        out_shape=[jax.ShapeDtypeStruct((N_DEV, D_MODEL), F32), jax.ShapeDtypeStruct((N_DEV, n), F32),
                   jax.ShapeDtypeStruct((N_DEV * N_DEV, n), F32), jax.ShapeDtypeStruct((8, D_MODEL), F32)]
        + [jax.ShapeDtypeStruct(sh.shape, BF16) for sh in shards],
        in_specs=[VMEM_WHOLE, ANY, VMEM_WHOLE] + [ANY] * n_w,
        out_specs=[VMEM_WHOLE, VMEM_WHOLE, VMEM_WHOLE, VMEM_WHOLE] + [ANY] * n_w,
        input_output_aliases={3 + a: 4 + a for a in range(n_w)},
        scratch_shapes=[pltpu.VMEM((N_DEV * 8, D_MODEL), F32), pltpu.VMEM(w_ada_s.shape, F32)]
        + _WeightGather.sems(n_w) + _Gather8.sems(1) + _Gather8.sems(1) + [pltpu.SemaphoreType.DMA],
        compiler_params=pltpu.CompilerParams(vmem_limit_bytes=V7X_VMEM_LIMIT),
    )(c_pad, w_ada_s, b_ada, *shards)
    return outs[0], outs[3], outs[4:]


def _with_own_block(shard, chip):
    empty = lax.empty((N_CHIPS,) + shard.shape, BF16)
    return lax.dynamic_update_slice(empty, shard.astype(BF16)[None], (chip, 0, 0))


class _WeightGather:
    N_SEM = 8

    def __init__(self, gathered, kinds, send_sems, recv_sems):
        self.gathered, self.kinds = gathered, kinds
        self.send_sems, self.recv_sems = send_sems, recv_sems
        self.x, self.y, self.c = _my_pos()
        self.me, self.sibling = (self.x, self.y, self.c), (self.x, self.y, 1 - self.c)
        self.nbr = ((1 - self.x, self.y), (self.x, 1 - self.y))
        self.diag = 2 * (1 - self.x) + (1 - self.y)

    def _dst(self, a, chip, pc, quarter=None):
        r, cc = self._shard_shape(a)
        h = r // 2
        row0, rows = pc * h, h
        if quarter is not None:
            row0, rows = pc * h + quarter * (h // 2), h // 2
        g = self.gathered[a]
        if self.kinds[a] == "blk":
            return g.at[chip, pl.ds(row0, rows), :]
        return g.at[pl.ds(row0, rows), pl.ds(chip * cc, cc)]

    def _copy(self, a, k, region, to):
        return pltpu.make_async_remote_copy(
            src_ref=region, dst_ref=region, send_sem=self.send_sems.at[a * self.N_SEM + k],
            recv_sem=self.recv_sems.at[a * self.N_SEM + k], device_id=to, device_id_type=MESH)

    def _arrays(self):
        return range(len(self.gathered))

    def _shard_shape(self, a):
        shape = self.gathered[a].shape
        return shape[1:] if self.kinds[a] == "blk" else (shape[0], shape[1] // N_CHIPS)

    def start(self):
        my_chip = 2 * self.x + self.y
        for a in self._arrays():
            for j, chip in enumerate(self.nbr):
                self._copy(a, j, self._dst(a, my_chip, self.c), (*chip, self.c)).start()

    def forward(self):
        for a in self._arrays():
            for j, chip in enumerate(self.nbr):
                cj = 2 * chip[0] + chip[1]
                half = self._dst(a, cj, self.c)
                self._copy(a, j, half, self.me).wait_recv()
                self._copy(a, 2 + j, half, self.sibling).start()
                other = self.nbr[1 - j]
                self._copy(a, 4 + j, self._dst(a, cj, self.c, quarter=j), (*other, self.c)).start()

    def forward_diagonal(self):
        for a in self._arrays():
            for j in range(2):
                quarter = self._dst(a, self.diag, self.c, quarter=j)
                self._copy(a, 4 + j, quarter, self.me).wait_recv()
                self._copy(a, 6 + j, quarter, self.sibling).start()

    def finish(self):
        for a in self._arrays():
            for j, chip in enumerate(self.nbr):
                self._copy(a, 2 + j, self._dst(a, 2 * chip[0] + chip[1], 1 - self.c), self.me).wait_recv()
                self._copy(a, 6 + j, self._dst(a, self.diag, 1 - self.c, quarter=j), self.me).wait_recv()
        for a in self._arrays():
            half = self._dst(a, self.diag, self.c)
            quarter = self._dst(a, self.diag, self.c, quarter=0)
            for k in range(self.N_SEM):
                self._copy(a, k, half if k < 4 else quarter, self.me).wait_send()

    @classmethod
    def sems(cls, n_arr):
        return [pltpu.SemaphoreType.DMA((n_arr * cls.N_SEM,)), pltpu.SemaphoreType.DMA((n_arr * cls.N_SEM,))]


def _half_of_full(ref, kind, pc):
    if kind == "blk":
        h = ref.shape[1] // 2
        return ref.at[:, pl.ds(pc * h, h), :]
    h = ref.shape[0] // 2
    return ref.at[pl.ds(pc * h, h), :]


def _half_shape(shape, kind):
    return (shape[0], shape[1] // 2, shape[2]) if kind == "blk" else (shape[0] // 2, shape[1])


class _HalfSwap:
    def __init__(self, ins, outs, kinds, send_sems, recv_sems):
        self.ins, self.outs, self.kinds = ins, outs, kinds
        self.send_sems, self.recv_sems = send_sems, recv_sems
        self.x, self.y, self.c = _my_pos()

    def _copies(self):
        for a in range(len(self.ins)):
            yield pltpu.make_async_remote_copy(
                src_ref=_half_of_full(self.ins[a], self.kinds[a], 1 - self.c), dst_ref=self.outs[a],
                send_sem=self.send_sems.at[a], recv_sem=self.recv_sems.at[a],
                device_id=(self.x, self.y, 1 - self.c), device_id_type=MESH)

    def start(self):
        for cp in self._copies():
            cp.start()

    def wait(self):
        for cp in self._copies():
            cp.wait()

    @staticmethod
    def sems(n_arr):
        return [pltpu.SemaphoreType.DMA((n_arr,)), pltpu.SemaphoreType.DMA((n_arr,))]

    @staticmethod
    def out_shapes(fulls, kinds):
        return [jax.ShapeDtypeStruct(_half_shape(a.shape, k), a.dtype) for a, k in zip(fulls, kinds)]


def _swap_halves(fulls_bf16, kinds, name):
    n_arr = len(fulls_bf16)

    def body(*refs):
        swap = _HalfSwap(refs[:n_arr], refs[n_arr:2 * n_arr], kinds, *refs[2 * n_arr:])
        swap.start()
        swap.wait()

    return pl.pallas_call(
        body, name=name, out_shape=_HalfSwap.out_shapes(fulls_bf16, kinds),
        in_specs=[ANY] * n_arr, out_specs=[ANY] * n_arr, scratch_shapes=_HalfSwap.sems(n_arr),
    )(*fulls_bf16)


def _add_halves(full, got, kind, name):
    hs = _half_shape(full.shape, kind)

    def body(pos_ref, a_ref, b_ref, o_ref, ob_ref):
        p = a_ref[...] + b_ref[...].astype(F32)
        ob_ref[...] = p.astype(BF16)

        @pl.when(pl.program_id(0) == pos_ref[1])
        def _():
            o_ref[...] = p.reshape(o_ref.shape)

    if kind == "blk":
        nb, h, cc = hs
        own = pl.BlockSpec((1, h, cc), lambda b, pos_ref: (b, pos_ref[0], 0))
        other = pl.BlockSpec((1, h, cc), lambda b, pos_ref: (b, 0, 0))
    else:
        h, cc = hs[0], hs[1] // N_CHIPS
        own = pl.BlockSpec((h, cc), lambda b, pos_ref: (pos_ref[0], b))
        other = pl.BlockSpec((h, cc), lambda b, pos_ref: (0, b))
    pos = jnp.concatenate([_core_index_scalar(), _chip_index_scalar()])
    return pl.pallas_call(
        body, name=name, out_shape=(jax.ShapeDtypeStruct((h, cc), F32), jax.ShapeDtypeStruct(hs, BF16)),
        grid_spec=pltpu.PrefetchScalarGridSpec(
            num_scalar_prefetch=1, grid=(N_CHIPS,), in_specs=[own, other],
            out_specs=(pl.BlockSpec((h, cc), lambda b, pos_ref: (0, 0)), other)),
        compiler_params=_params(("arbitrary",)),
    )(pos, full, got)


def _rx_shape(part_shape, kind):
    if kind == "blk":
        return (3, part_shape[1], part_shape[2])
    return (3, part_shape[0], part_shape[1] // N_CHIPS)


class _ChipExchange:
    def __init__(self, parts, rxs, kinds, send_sems, recv_sems):
        self.parts, self.rxs, self.kinds = parts, rxs, kinds
        self.send_sems, self.recv_sems = send_sems, recv_sems
        self.x, self.y, self.c = _my_pos()
        self.chips = _other_chips(self.x, self.y)

    def _copies(self):
        for a in range(len(self.parts)):
            for j, chip in enumerate(self.chips):
                cj = 2 * chip[0] + chip[1]
                if self.kinds[a] == "blk":
                    src = self.parts[a].at[cj]
                else:
                    cc = self.parts[a].shape[1] // N_CHIPS
                    src = self.parts[a].at[:, pl.ds(cj * cc, cc)]
                yield pltpu.make_async_remote_copy(
                    src_ref=src, dst_ref=self.rxs[a].at[j], send_sem=self.send_sems.at[a * 3 + j],
                    recv_sem=self.recv_sems.at[a * 3 + j], device_id=(*chip, self.c), device_id_type=MESH)

    def start(self):
        for cp in self._copies():
            cp.start()

    def wait(self):
        for cp in self._copies():
            cp.wait_recv()
        for cp in self._copies():
            cp.wait_send()

    @staticmethod
    def sems(n_arr):
        return [pltpu.SemaphoreType.DMA((n_arr * 3,)), pltpu.SemaphoreType.DMA((n_arr * 3,))]


HBM_SPEC = pl.BlockSpec(memory_space=pltpu.HBM)
SEM_SPEC = pl.BlockSpec(memory_space=pltpu.SEMAPHORE)
DATAFLOW = pltpu.SideEffectType.DATAFLOW_SIDE_EFFECTING


def _exchange_start(part, name):
    rx_shape = _rx_shape(part.shape, "blk")

    def body(part_ref, rx_ref, send_sems, recv_sems, part_thru, rx_thru, token):
        _ChipExchange([part_ref], [rx_ref], ["blk"], send_sems, recv_sems).start()
        token[...] = jnp.zeros_like(token)

    return pl.pallas_call(
        body, name=name,
        out_shape=(pltpu.SemaphoreType.DMA((3,)), pltpu.SemaphoreType.DMA((3,)), pltpu.HBM(part.shape, part.dtype),
                   pltpu.HBM(rx_shape, BF16), jax.ShapeDtypeStruct((8, LANES), F32)),
        in_specs=(HBM_SPEC, HBM_SPEC), out_specs=(SEM_SPEC, SEM_SPEC, HBM_SPEC, HBM_SPEC, VMEM_WHOLE),
        input_output_aliases={0: 2, 1: 3}, compiler_params=pltpu.CompilerParams(has_side_effects=DATAFLOW),
    )(pltpu.with_memory_space_constraint(part, pltpu.HBM),
      pltpu.with_memory_space_constraint(lax.empty(rx_shape, BF16), pltpu.HBM))


def _exchange_wait(send_sems, recv_sems, part_thru, rx_thru, after, name):
    def body(part_ref, rx_ref, send_sems, recv_sems, after_ref, part_dead, rx_out):
        _ChipExchange([part_ref], [rx_ref], ["blk"], send_sems, recv_sems).wait()

    return pl.pallas_call(
        body, name=name,
        out_shape=(pltpu.HBM(part_thru.shape, part_thru.dtype), pltpu.HBM(rx_thru.shape, rx_thru.dtype)),
        in_specs=(HBM_SPEC, HBM_SPEC, SEM_SPEC, SEM_SPEC, ANY), out_specs=(HBM_SPEC, HBM_SPEC),
        input_output_aliases={0: 0, 1: 1}, compiler_params=pltpu.CompilerParams(has_side_effects=DATAFLOW),
    )(part_thru, rx_thru, send_sems, recv_sems, after)[1]


def _sum_chips(part, rx, tr, name, after):
    _, h, cc = rx.shape
    flips = (2, 1, 3)

    def body(chip_ref, p_ref, rx_ref, after_ref, o_ref):
        own = p_ref[...]
        for mc in range(N_CHIPS):
            @pl.when(chip_ref[0] == mc)
            def _():
                terms = sorted([(mc, None)] + [(mc ^ f, j) for j, f in enumerate(flips)])
                acc = None
                for _, j in terms:
                    t = own if j is None else rx_ref[j].astype(F32)
                    acc = t if acc is None else acc + t
                o_ref[...] = acc

    return pl.pallas_call(
        body, name=name, out_shape=jax.ShapeDtypeStruct((h, cc), F32),
        grid_spec=pltpu.PrefetchScalarGridSpec(
            num_scalar_prefetch=1, grid=(h // tr,),
            in_specs=[pl.BlockSpec((tr, cc), lambda i, chip_ref: (i, 0)),
                      pl.BlockSpec((3, tr, cc), lambda i, chip_ref: (0, i, 0)), ANY],
            out_specs=pl.BlockSpec((tr, cc), lambda i, chip_ref: (i, 0))),
        compiler_params=_params(("arbitrary",)),
    )(_chip_index_scalar(), part, rx, after)


def _share_halves(halves, name):
    n_arr = len(halves)

    def body(*refs):
        ins, outs = refs[:n_arr], refs[n_arr:2 * n_arr]
        send_sems, recv_sems = refs[2 * n_arr:]
        x, y, c = _my_pos()
        cps = []
        for a in range(n_arr):
            cp = pltpu.make_async_remote_copy(
                src_ref=ins[a], dst_ref=outs[a], send_sem=send_sems.at[a], recv_sem=recv_sems.at[a],
                device_id=(x, y, 1 - c), device_id_type=MESH)
            cp.start()
            cps.append(cp)
        for cp in cps:
            cp.wait()

    return pl.pallas_call(
        body, name=name, out_shape=[jax.ShapeDtypeStruct(h.shape, h.dtype) for h in halves],
        in_specs=[ANY] * n_arr, out_specs=[ANY] * n_arr,
        scratch_shapes=[pltpu.SemaphoreType.DMA((n_arr,)), pltpu.SemaphoreType.DMA((n_arr,))],
    )(*halves)


def _bucket_table():
    qi = jnp.arange(BLOCK)[:, None]
    si = jnp.arange(2 * BLOCK)[None, :]
    dist = qi + BLOCK - si
    max_exact = N_BUCKETS // 2
    n = jnp.maximum(dist, 0)
    nf = jnp.maximum(n, max_exact).astype(F32)
    large = max_exact + (jnp.log(nf / max_exact) / math.log(MAX_DISTANCE / max_exact)
                         * (N_BUCKETS - max_exact)).astype(jnp.int32)
    large = jnp.minimum(large, N_BUCKETS - 1)
    return jnp.where(n < max_exact, n, large).astype(F32)


def _prep_tables(bucket, rel_bias_t, w_s):
    def body(bucket_ref, rb_ref, ws_ref, bias_ref, wsm_ref):
        qi = lax.broadcasted_iota(jnp.int32, (BLOCK, 2 * BLOCK), 0)
        si = lax.broadcasted_iota(jnp.int32, (BLOCK, 2 * BLOCK), 1)
        dist = qi + BLOCK - si
        in_window = (dist >= 0) & (dist < BLOCK)
        bk = bucket_ref[...]
        for h in range(N_HEADS):
            acc = jnp.zeros((BLOCK, 2 * BLOCK), F32)
            for b in range(N_BUCKETS):
                acc = jnp.where(bk == float(b), rb_ref[h, b], acc)
            bias_ref[h] = jnp.where(in_window, acc, NEG_INF)
        ti = lax.broadcasted_iota(jnp.int32, (BLOCK, BLOCK), 0)
        ui = lax.broadcasted_iota(jnp.int32, (BLOCK, BLOCK), 1)
        for g in range(N_GROUPS):
            wsm_ref[g] = jnp.where(ti >= ui, ws_ref[g], 0.0).astype(BF16)

    return pl.pallas_call(
        body, name="prep_tables",
        out_shape=(jax.ShapeDtypeStruct((N_HEADS, BLOCK, 2 * BLOCK), F32),
                   jax.ShapeDtypeStruct((N_GROUPS, BLOCK, BLOCK), BF16)),
        grid=(1,),
        in_specs=[_const_spec((BLOCK, 2 * BLOCK)), pl.BlockSpec(memory_space=pltpu.SMEM),
                  _const_spec((N_GROUPS, BLOCK, BLOCK))],
        out_specs=(_const_spec((N_HEADS, BLOCK, 2 * BLOCK)), _const_spec((N_GROUPS, BLOCK, BLOCK))),
        compiler_params=_params(("arbitrary",)),
    )(bucket, rel_bias_t, w_s)


def _fwd_in(x, modr, w_in, b_in, tm, shards, kinds):
    s = x.shape[0]
    n_steps = s // tm
    fwd_step, diag_step = (8 * n_steps) // 16, (13 * n_steps) // 16
    n_w = len(shards)

    def body(x_ref, mod_ref, w_ref, b_ref, *rest):
        h1_ref, q_ref, kv_ref, gu_ref, gv_ref, xb_ref = rest[n_w:n_w + 6]
        gathered_refs = rest[n_w + 6:2 * n_w + 6]
        send_sems, recv_sems = rest[2 * n_w + 6:]
        i = pl.program_id(0)
        gather = _WeightGather(gathered_refs, kinds, send_sems, recv_sems)

        @pl.when(i == 0)
        def _():
            gather.start()

        xv = x_ref[...]
        xb_ref[...] = xv.astype(BF16)
        h1 = (xv * (1.0 + mod_ref[1:2, :]) + mod_ref[0:1, :]).astype(BF16)
        h1_ref[...] = h1
        proj = _dot_nt(h1, w_ref[...]) + b_ref[...]
        q_ref[...] = (proj[:, :ATTN_W] * Q_SCALE).astype(BF16)
        kv_ref[...] = proj[:, ATTN_W:ATTN_W + 2 * KV_W].astype(BF16)
        gu_ref[...] = proj[:, ATTN_W + 2 * KV_W:ATTN_W + 2 * KV_W + GMLP_W]
        gv_ref[...] = proj[:, ATTN_W + 2 * KV_W + GMLP_W:]

        @pl.when(i == fwd_step)
        def _():
            gather.forward()

        @pl.when(i == diag_step)
        def _():
            gather.forward_diagonal()

        @pl.when(i == n_steps - 1)
        def _():
            gather.finish()

    row = lambda w: pl.BlockSpec((tm, w), lambda i: (i, 0))
    outs = pl.pallas_call(
        body, name="fwd_in",
        out_shape=[jax.ShapeDtypeStruct((s, D_MODEL), BF16), jax.ShapeDtypeStruct((s, ATTN_W), BF16),
                   jax.ShapeDtypeStruct((s, 2 * KV_W), BF16), jax.ShapeDtypeStruct((s, GMLP_W), F32),
                   jax.ShapeDtypeStruct((s, GMLP_W), F32), jax.ShapeDtypeStruct((s, D_MODEL), BF16)]
        + [jax.ShapeDtypeStruct(sh.shape, BF16) for sh in shards],
        grid=(n_steps,),
        in_specs=[row(D_MODEL), _const_spec((8, D_MODEL)), _const_spec(w_in.shape), _const_spec((1, IN_W))]
        + [ANY] * n_w,
        out_specs=[row(D_MODEL), row(ATTN_W), row(2 * KV_W), row(GMLP_W), row(GMLP_W), row(D_MODEL)] + [ANY] * n_w,
        input_output_aliases={4 + a: 6 + a for a in range(n_w)},
        scratch_shapes=_WeightGather.sems(n_w),
        compiler_params=_params(("arbitrary",)),
    )(x, modr, w_in, b_in, *shards)
    return outs[:6], outs[6:]


def _kv_variants(kk):
    kf = kk.astype(F32)
    lane = lax.broadcasted_iota(jnp.int32, kf.shape, 1)
    low = lane < HEAD_DIM
    k0_lo = jnp.where(low, kf, 0.0)
    k1_hi = jnp.where(low, 0.0, kf)
    k0_hi = pltpu.roll(k0_lo, HEAD_DIM, 1)
    k1_lo = pltpu.roll(k1_hi, HEAD_DIM, 1)
    return ((k0_lo.astype(BF16), k0_hi.astype(BF16)), (k1_lo.astype(BF16), k1_hi.astype(BF16)))


def _head_kv(h):
    return h // (N_HEADS // N_KV), h % 2


MIX_GROUP = 2


def _interleave(*gens):
    results = [None] * len(gens)
    active = list(enumerate(gens))
    while active:
        still = []
        for i, g in active:
            try:
                next(g)
                still.append((i, g))
            except StopIteration as done:
                results[i] = done.value
        active = still
    return results


def _attn_block_fwd(q_blk, kk, vv, bias_ref, sinks_ref, first_mask):
    kvar = _kv_variants(kk)
    vvar = _kv_variants(vv)
    heads = range(N_HEADS)
    q_pairs = [q_blk[:, (h // 2) * LANES:(h // 2 + 1) * LANES] for h in heads]
    logits = [_dot_nt(q_pairs[h], kvar[_head_kv(h)[0]][_head_kv(h)[1]]) + bias_ref[h] for h in heads]
    if first_mask is not None:
        logits = [jnp.where(first_mask, NEG_INF, lg) for lg in logits]
    yield
    ms = [jnp.maximum(jnp.max(logits[h], axis=-1, keepdims=True), sinks_ref[h]) for h in heads]
    yield
    es = [jnp.exp(logits[h] - ms[h]) for h in heads]
    ess = [jnp.exp(sinks_ref[h] - ms[h]) for h in heads]
    yield
    invs = [1.0 / (jnp.sum(es[h], axis=-1, keepdims=True) + ess[h]) for h in heads]
    probs = [(es[h] * invs[h], ess[h] * invs[h]) for h in heads]
    yield
    outs = [_dot(probs[h][0].astype(BF16), vvar[_head_kv(h)[0]][_head_kv(h)[1]]) for h in heads]
    pairs = [outs[2 * i] + outs[2 * i + 1] for i in range(N_HEADS // 2)]
    return jnp.concatenate(pairs, axis=1), probs, kvar, vvar


def _gmlp_chunk_fwd(gu, gv, ln_g, ln_b, wsm_ref, bsx, amat):
    u, tu = _gelu(gu)
    a, ta = _gelu(gv)
    yield
    mean = _split_dot(a, amat)
    d = a - mean
    yield
    var = _split_dot(d * d, amat)
    yield
    rstd = lax.rsqrt(var + LN_EPS)
    xhat = d * rstd
    vb = (xhat * ln_g + ln_b).astype(BF16)
    yield
    lane = lax.broadcasted_iota(jnp.int32, (BLOCK, LANES), 1)
    low = lane < GROUP_DIM
    cols = []
    for pair in range(N_GROUPS // 2):
        vp = vb[:, pair * LANES:(pair + 1) * LANES]
        cols.append(jnp.where(low, _dot(wsm_ref[2 * pair], vp), _dot(wsm_ref[2 * pair + 1], vp)))
    mixedv = jnp.concatenate(cols, axis=1) + bsx
    return u * mixedv, (u, tu, ta, xhat, rstd, vb, mixedv)


def _rms(a, g):
    r = lax.rsqrt(jnp.mean(a * a, axis=-1, keepdims=True) + LN_EPS)
    return a * r * g, r


def _fwd_mix(q, kv, gu, gv, x, modr, bias, sinks, gln_g, gln_b, wsm, bsx, amat, aog, gog, w_out, ln1_g, ln1_b, tm,
             ffn_shards, ffn_kinds):
    s = x.shape[0]
    nb = tm // BLOCK
    n_steps = s // tm
    fwd_step, diag_step = (7 * n_steps) // 16, (12 * n_steps) // 16
    n_w = len(ffn_shards)

    def body(q_ref, kv_ref, kvp_ref, gu_ref, gv_ref, x_ref, mod_ref, bias_ref, sinks_ref, glng_ref, glnb_ref, wsm_ref,
             bsx_ref, amat_ref, aog_ref, gog_ref, wout_ref, ln1g_ref, ln1b_ref, *rest):
        x1_ref, x1b_ref, y_ref, mixed_ref = rest[n_w:n_w + 4]
        gathered_refs = rest[n_w + 4:2 * n_w + 4]
        mix_scr, send_sems, recv_sems = rest[2 * n_w + 4:]
        i = pl.program_id(0)
        gather = _WeightGather(gathered_refs, ffn_kinds, send_sems, recv_sems)

        @pl.when(i == 0)
        def _():
            gather.start()

        col = lax.broadcasted_iota(jnp.int32, (BLOCK, 2 * BLOCK), 1)
        for b0 in range(0, nb, MIX_GROUP):
            gens = []
            for b in range(b0, min(b0 + MIX_GROUP, nb)):
                r0 = b * BLOCK
                if b == 0:
                    kvprev = kvp_ref[...]
                    first_mask = (col < BLOCK) & (i == 0)
                else:
                    kvprev = kv_ref[r0 - BLOCK:r0, :]
                    first_mask = None
                kvcur = kv_ref[r0:r0 + BLOCK, :]
                kk = jnp.concatenate([kvprev[:, :KV_W], kvcur[:, :KV_W]], axis=0)
                vv = jnp.concatenate([kvprev[:, KV_W:], kvcur[:, KV_W:]], axis=0)
                gens.append(_attn_block_fwd(q_ref[r0:r0 + BLOCK, :], kk, vv, bias_ref, sinks_ref, first_mask))
                gens.append(_gmlp_chunk_fwd(gu_ref[r0:r0 + BLOCK, :], gv_ref[r0:r0 + BLOCK, :], glng_ref[...],
                                            glnb_ref[...], wsm_ref, bsx_ref[...], amat_ref[...]))
            res = _interleave(*gens)
            for k, b in enumerate(range(b0, min(b0 + MIX_GROUP, nb))):
                r0 = b * BLOCK
                na, _ = _rms(res[2 * k][0], aog_ref[...])
                ng, _ = _rms(res[2 * k + 1][0], gog_ref[...])
                mix_scr[r0:r0 + BLOCK, :ATTN_W] = na.astype(BF16)
                mix_scr[r0:r0 + BLOCK, ATTN_W:] = ng.astype(BF16)
        mixed = mix_scr[...]
        mixed_ref[...] = mixed
        y = _dot(mixed, wout_ref[...])
        y_ref[...] = y.astype(BF16)
        z1 = ALPHA * x_ref[...] + mod_ref[2:3, :] * y
        xhat, _ = _ln_stats(z1)
        x1 = xhat * ln1g_ref[...] + ln1b_ref[...]
        x1_ref[...] = x1
        x1b_ref[...] = x1.astype(BF16)

        @pl.when(i == fwd_step)
        def _():
            gather.forward()

        @pl.when(i == diag_step)
        def _():
            gather.forward_diagonal()

        @pl.when(i == n_steps - 1)
        def _():
            gather.finish()

    row = lambda w: pl.BlockSpec((tm, w), lambda i: (i, 0))
    prev = pl.BlockSpec((BLOCK, 2 * KV_W), lambda i: (jnp.maximum(i * nb - 1, 0), 0))
    outs = pl.pallas_call(
        body, name="fwd_mix",
        out_shape=[jax.ShapeDtypeStruct((s, D_MODEL), F32)] + [jax.ShapeDtypeStruct((s, D_MODEL), BF16)] * 3
        + [jax.ShapeDtypeStruct(sh.shape, BF16) for sh in ffn_shards],
        grid=(n_steps,),
        in_specs=[row(ATTN_W), row(2 * KV_W), prev, row(GMLP_W), row(GMLP_W), row(D_MODEL), _const_spec((8, D_MODEL)),
                  _const_spec((N_HEADS, BLOCK, 2 * BLOCK)), pl.BlockSpec(memory_space=pltpu.SMEM),
                  _const_spec((1, GMLP_W)), _const_spec((1, GMLP_W)), _const_spec((N_GROUPS, BLOCK, BLOCK)),
                  _const_spec((BLOCK, GMLP_W)), _const_spec((GMLP_W, GMLP_W)), _const_spec((1, ATTN_W)),
                  _const_spec((1, GMLP_W)), _const_spec((D_MODEL, D_MODEL)), _const_spec((1, D_MODEL)),
                  _const_spec((1, D_MODEL))] + [ANY] * n_w,
        out_specs=[row(D_MODEL)] * 4 + [ANY] * n_w,
        input_output_aliases={19 + a: 4 + a for a in range(n_w)},
        scratch_shapes=[pltpu.VMEM((tm, D_MODEL), BF16)] + _WeightGather.sems(n_w),
        compiler_params=_params(("arbitrary",)),
    )(q, kv, kv, gu, gv, x, modr, bias, sinks, gln_g, gln_b, wsm, bsx, amat, aog, gog, w_out, ln1_g, ln1_b, *ffn_shards)
    return outs[:4], outs[4:]


FF_BLOCKS = N_CHIPS // 2
FF_CHUNK = D_FF // FF_BLOCKS
FFN_SUB = 256


def _sigmoid(x):
    return 1.0 / (1.0 + jnp.exp(-x))


def _fwd_ffn(x1, target, modr, ln2_g, ln2_b, w_gu, w_dn, tm):
    s = x1.shape[0]

    def body(x1_ref, t_ref, mod_ref, g_ref, b_ref, wgu_ref, wdn_ref, h2_ref, act_ref, dy2_ref, dx1a_ref, acc_ref):
        @pl.when(pl.program_id(0) == 0)
        def _():
            acc_ref[...] = jnp.zeros_like(acc_ref)

        x1v = x1_ref[...]
        h2 = (x1v * (1.0 + mod_ref[4:5, :]) + mod_ref[3:4, :]).astype(BF16)
        h2_ref[...] = h2
        y2 = None
        for cc in range(FF_BLOCKS):
            c0 = cc * FF_CHUNK
            gate = _dot(h2, wgu_ref[cc])
            up = _dot(h2, wgu_ref[FF_BLOCKS + cc])
            act_ref[:, c0:c0 + FF_CHUNK] = gate.astype(BF16)
            act_ref[:, D_FF + c0:D_FF + c0 + FF_CHUNK] = up.astype(BF16)
            a = (gate * _sigmoid(gate) * up).astype(BF16)
            part = _dot(a, wdn_ref[c0:c0 + FF_CHUNK, :])
            y2 = part if y2 is None else y2 + part
        g2 = mod_ref[5:6, :]
        z2 = ALPHA * x1v + g2 * y2
        xhat, rstd = _ln_stats(z2)
        gain = g_ref[...]
        diff = xhat * gain + b_ref[...] - t_ref[...]
        dx2 = diff * (1.0 / D_MODEL)
        dz2 = _ln_bwd(dx2 * gain, xhat, rstd)
        dx1a_ref[...] = ALPHA * dz2
        dy2_ref[...] = (g2 * dz2).astype(BF16)
        acc_ref[0:1, :] += _colsum(diff * diff)
        acc_ref[1:2, :] += _colsum(dx2 * xhat)
        acc_ref[2:3, :] += _colsum(dx2)
        acc_ref[3:4, :] += _colsum(dz2 * y2)

    row = lambda w: pl.BlockSpec((tm, w), lambda i: (i, 0))
    return pl.pallas_call(
        body, name="fwd_ffn",
        out_shape=(jax.ShapeDtypeStruct((s, D_MODEL), BF16), jax.ShapeDtypeStruct((s, 2 * D_FF), BF16),
                   jax.ShapeDtypeStruct((s, D_MODEL), BF16), jax.ShapeDtypeStruct((s, D_MODEL), F32),
                   jax.ShapeDtypeStruct((8, D_MODEL), F32)),
        grid=(s // tm,),
        in_specs=[row(D_MODEL), row(D_MODEL), _const_spec((8, D_MODEL)), _const_spec((1, D_MODEL)),
                  _const_spec((1, D_MODEL)), _const_spec((N_CHIPS, D_MODEL, FF_CHUNK), single=True),
                  _const_spec((D_FF, D_MODEL), single=True)],
        out_specs=(row(D_MODEL), row(2 * D_FF), row(D_MODEL), row(D_MODEL), _const_spec((8, D_MODEL))),
        compiler_params=_params(("arbitrary",)),
    )(x1, target, modr, ln2_g, ln2_b, w_gu, w_dn)


def _bwd_ffn(dy2, act, w_gu, w_dn, tm):
    s = dy2.shape[0]

    def body(dy2_ref, act_ref, wgu_ref, wdn_ref, a_ref, dgu_ref, dh2_ref):
        dy2v = dy2_ref[...]
        dh2 = None
        for cc in range(FF_BLOCKS):
            c0 = cc * FF_CHUNK
            da = _dot_nt(dy2v, wdn_ref[c0:c0 + FF_CHUNK, :])
            gate = act_ref[:, c0:c0 + FF_CHUNK].astype(F32)
            up = act_ref[:, D_FF + c0:D_FF + c0 + FF_CHUNK].astype(F32)
            sg = _sigmoid(gate)
            sl = gate * sg
            a_ref[:, c0:c0 + FF_CHUNK] = (sl * up).astype(BF16)
            dgate = (da * up * (sg * (1.0 + gate * (1.0 - sg)))).astype(BF16)
            dup = (da * sl).astype(BF16)
            dgu_ref[:, c0:c0 + FF_CHUNK] = dgate
            dgu_ref[:, D_FF + c0:D_FF + c0 + FF_CHUNK] = dup
            part = _dot_nt(dgate, wgu_ref[cc]) + _dot_nt(dup, wgu_ref[FF_BLOCKS + cc])
            dh2 = part if dh2 is None else dh2 + part
        dh2_ref[...] = dh2.astype(BF16)

    row = lambda w: pl.BlockSpec((tm, w), lambda i: (i, 0))
    return pl.pallas_call(
        body, name="bwd_ffn",
        out_shape=(jax.ShapeDtypeStruct((s, D_FF), BF16), jax.ShapeDtypeStruct((s, 2 * D_FF), BF16),
                   jax.ShapeDtypeStruct((s, D_MODEL), BF16)),
        grid=(s // tm,),
        in_specs=[row(D_MODEL), row(2 * D_FF), _const_spec((N_CHIPS, D_MODEL, FF_CHUNK), single=True),
                  _const_spec((D_FF, D_MODEL), single=True)],
        out_specs=(row(D_FF), row(2 * D_FF), row(D_MODEL)),
        compiler_params=_params(("parallel",)),
    )(dy2, act, w_gu, w_dn)


def _bwd_mid(dh2, dx1a, x1, x, y, modr, ln1_g, w_out, tm, swap_fulls, swap_kinds):
    s = x.shape[0]
    n_steps = s // tm
    n_g = len(swap_fulls)

    def body(dh2_ref, dx1a_ref, x1_ref, x_ref, y_ref, mod_ref, g_ref, wout_ref, *rest):
        full_refs = rest[:n_g]
        dxa_ref, dy_ref, dmix_ref, acc_ref = rest[n_g:n_g + 4]
        got_refs = rest[n_g + 4:2 * n_g + 4]
        swap = _HalfSwap(full_refs, got_refs, swap_kinds, *rest[2 * n_g + 4:])
        i = pl.program_id(0)

        @pl.when(i == 0)
        def _():
            swap.start()
            acc_ref[...] = jnp.zeros_like(acc_ref)

        dh2 = dh2_ref[...].astype(F32)
        x1v = x1_ref[...].astype(F32)
        yv = y_ref[...].astype(F32)
        g1 = mod_ref[2:3, :]
        dx1 = dx1a_ref[...] + dh2 * (1.0 + mod_ref[4:5, :])
        z1 = ALPHA * x_ref[...] + g1 * yv
        xhat, rstd = _ln_stats(z1)
        dz1 = _ln_bwd(dx1 * g_ref[...], xhat, rstd)
        dxa_ref[...] = (ALPHA * dz1).astype(BF16)
        dy = (g1 * dz1).astype(BF16)
        dy_ref[...] = dy
        dmix_ref[...] = _dot_nt(dy, wout_ref[...]).astype(BF16)
        acc_ref[0:1, :] += _colsum(dh2 * x1v)
        acc_ref[1:2, :] += _colsum(dh2)
        acc_ref[2:3, :] += _colsum(dx1 * xhat)
        acc_ref[3:4, :] += _colsum(dx1)
        acc_ref[4:5, :] += _colsum(dz1 * yv)

        @pl.when(i == n_steps - 1)
        def _():
            swap.wait()

    row = lambda w: pl.BlockSpec((tm, w), lambda i: (i, 0))
    outs = pl.pallas_call(
        body, name="bwd_mid",
        out_shape=[jax.ShapeDtypeStruct((s, D_MODEL), BF16), jax.ShapeDtypeStruct((s, D_MODEL), BF16),
                   jax.ShapeDtypeStruct((s, D_MODEL), BF16), jax.ShapeDtypeStruct((8, D_MODEL), F32)]
        + _HalfSwap.out_shapes(swap_fulls, swap_kinds),
        grid=(n_steps,),
        in_specs=[row(D_MODEL)] * 5 + [_const_spec((8, D_MODEL)), _const_spec((1, D_MODEL)),
                                       _const_spec((D_MODEL, D_MODEL))] + [ANY] * n_g,
        out_specs=[row(D_MODEL), row(D_MODEL), row(D_MODEL), _const_spec((8, D_MODEL))] + [ANY] * n_g,
        scratch_shapes=_HalfSwap.sems(n_g),
        compiler_params=_params(("arbitrary",)),
    )(dh2, dx1a, x1, x, y, modr, ln1_g, w_out, *swap_fulls)
    return outs[:4], outs[4:]


def _fold_kv(t0, t1):
    lane = lax.broadcasted_iota(jnp.int32, t0.shape, 1)
    f0 = t0 + pltpu.roll(t0, HEAD_DIM, 1)
    f1 = t1 + pltpu.roll(t1, HEAD_DIM, 1)
    return jnp.where(lane < HEAD_DIM, f0, f1)


def _bwd_mix(q, kv, gu, gv, dmix, bias, sinks, gln_g, gln_b, wsm, bsx, amat, aog, gog, grad_parts, grad_kinds):
    s = q.shape[0]
    tile = 2 * BLOCK
    n_steps = s // tile
    n_g = len(grad_parts)

    def body(q_ref, kv_ref, kvp_ref, gu_ref, gv_ref, dmix_ref, bias_ref, sinks_ref, glng_ref, glnb_ref, wsm_ref,
             bsx_ref, amat_ref, aog_ref, gog_ref, *rest):
        part_refs = rest[:n_g]
        dq_ref, dkv_ref, dgu_ref, dgv_ref, gbias_ref, dws_ref, dbs_ref, vec_ref, dsink_ref = rest[n_g:n_g + 9]
        rx_refs = rest[n_g + 9:2 * n_g + 9]
        carry, done, send_sems, recv_sems = rest[2 * n_g + 9:]
        n = pl.program_id(0)
        exchange = _ChipExchange(part_refs, rx_refs, grad_kinds, send_sems, recv_sems)

        @pl.when(n == 0)
        def _():
            exchange.start()
            carry[...] = jnp.zeros_like(carry)
            done[...] = jnp.zeros_like(done)
            gbias_ref[...] = jnp.zeros_like(gbias_ref)
            dws_ref[...] = jnp.zeros_like(dws_ref)
            dbs_ref[...] = jnp.zeros_like(dbs_ref)
            vec_ref[...] = jnp.zeros_like(vec_ref)
            dsink_ref[...] = jnp.zeros_like(dsink_ref)

        @pl.when(n == n_steps)
        def _():
            dkv_ref[:BLOCK, :] = done[...].astype(BF16)
            dkv_ref[BLOCK:, :] = carry[...].astype(BF16)
            exchange.wait()

        @pl.when(n < n_steps)
        def _():
            col = lax.broadcasted_iota(jnp.int32, (BLOCK, 2 * BLOCK), 1)
            lane = lax.broadcasted_iota(jnp.int32, (BLOCK, LANES), 1)
            low = lane < HEAD_DIM
            rows = [slice(0, BLOCK), slice(BLOCK, tile)]
            kv_blocks = [kvp_ref[...], kv_ref[rows[0], :], kv_ref[rows[1], :]]
            masks = [(col < BLOCK) & (n == 0), None]
            q_blks = [q_ref[r, :] for r in rows]
            fwd = []
            for b in range(2):
                kk = jnp.concatenate([kv_blocks[b][:, :KV_W], kv_blocks[b + 1][:, :KV_W]], axis=0)
                vv = jnp.concatenate([kv_blocks[b][:, KV_W:], kv_blocks[b + 1][:, KV_W:]], axis=0)
                fwd.append(_attn_block_fwd(q_blks[b], kk, vv, bias_ref, sinks_ref, masks[b]))
                fwd.append(_gmlp_chunk_fwd(gu_ref[rows[b], :], gv_ref[rows[b], :], glng_ref[...], glnb_ref[...],
                                           wsm_ref, bsx_ref[...], amat_ref[...]))
            res = _interleave(*fwd[:2]) + _interleave(*fwd[2:])

            def gating_bwd(b, d_gm, saved):
                u, tu, ta, xhat, rstd, vb, mixedv = saved
                dgu_ref[rows[b], :] = (d_gm * mixedv * _gelu_grad(gu_ref[rows[b], :], tu)).astype(BF16)
                dmx = d_gm * u
                dmxb = dmx.astype(BF16)
                yield
                dvn_cols, dws = [], []
                for pair in range(N_GROUPS // 2):
                    dp_ = dmxb[:, pair * LANES:(pair + 1) * LANES]
                    vp = vb[:, pair * LANES:(pair + 1) * LANES]
                    dvn_cols.append(
                        jnp.where(low, _dot_tn(wsm_ref[2 * pair], dp_), _dot_tn(wsm_ref[2 * pair + 1], dp_)))
                    zero = jnp.zeros_like(dp_)
                    dws.append(_dot_nt(jnp.where(low, dp_, zero), vp))
                    dws.append(_dot_nt(jnp.where(low, zero, dp_), vp))
                dvn = jnp.concatenate(dvn_cols, axis=1)
                yield
                dxh = dvn * glng_ref[...]
                am = amat_ref[...]
                m1 = _split_dot(dxh, am)
                m2 = _split_dot(dxh * xhat, am)
                yield
                da = rstd * (dxh - m1 - xhat * m2)
                dgv_ref[rows[b], :] = (da * _gelu_grad(gv_ref[rows[b], :], ta)).astype(BF16)
                return dmx, dws, _colsum(dvn * xhat), _colsum(dvn)

            def attention_bwd(b, d_attn, probs, kvar, vvar):
                heads = range(N_HEADS)
                sels = [low if h % 2 == 0 else jnp.logical_not(low) for h in heads]
                pair_of = lambda a, h: a[:, (h // 2) * LANES:(h // 2 + 1) * LANES]
                do_hs = [jnp.where(sels[h], pair_of(d_attn, h), 0.0).astype(BF16) for h in heads]
                q_hs = [jnp.where(sels[h], pair_of(q_blks[b], h), jnp.zeros((BLOCK, LANES), BF16)) for h in heads]
                dps = [_dot_nt(do_hs[h], vvar[_head_kv(h)[0]][_head_kv(h)[1]]) for h in heads]
                yield
                deltas = [jnp.sum(probs[h][0] * dps[h], axis=-1, keepdims=True) for h in heads]
                yield
                dss = [probs[h][0] * (dps[h] - deltas[h]) for h in heads]
                dsinks = [-(probs[h][1] * deltas[h]) for h in heads]
                dsbs = [ds.astype(BF16) for ds in dss]
                pbs = [probs[h][0].astype(BF16) for h in heads]
                yield
                dqs = [_dot(dsbs[h], kvar[_head_kv(h)[0]][_head_kv(h)[1]]) for h in heads]
                tks = [_dot_tn(dsbs[h], q_hs[h]) for h in heads]
                tvs = [_dot_tn(pbs[h], do_hs[h]) for h in heads]
                dq_cols = [dqs[2 * i] + dqs[2 * i + 1] for i in range(N_HEADS // 2)]
                dq_ref[rows[b], :] = (jnp.concatenate(dq_cols, axis=1) * Q_SCALE).astype(BF16)
                per_kv = N_HEADS // N_KV
                kv_sum = lambda ts, kvh: sum(ts[kvh * per_kv + 1:(kvh + 1) * per_kv], ts[kvh * per_kv])
                dkk = _fold_kv(kv_sum(tks, 0), kv_sum(tks, 1))
                dvv = _fold_kv(kv_sum(tvs, 0), kv_sum(tvs, 1))
                return jnp.concatenate([dkk, dvv], axis=1), dss, dsinks

            bwd, rms_g = [], []
            for b in range(2):
                attn, probs, kvar, vvar = res[2 * b]
                gm, saved = res[2 * b + 1]
                na_unit, r_a = _rms(attn, 1.0)
                ng_unit, r_g = _rms(gm, 1.0)
                dmix = dmix_ref[rows[b], :].astype(F32)
                dn_a = dmix[:, :ATTN_W]
                dn_g = dmix[:, ATTN_W:]
                rms_g.append((_colsum(dn_a * na_unit), _colsum(dn_g * ng_unit)))
                t_a = dn_a * aog_ref[...]
                d_attn = r_a * t_a - na_unit * (r_a * jnp.mean(t_a * na_unit, axis=-1, keepdims=True))
                t_g = dn_g * gog_ref[...]
                d_gm = r_g * t_g - ng_unit * (r_g * jnp.mean(t_g * ng_unit, axis=-1, keepdims=True))
                bwd.append(attention_bwd(b, d_attn, probs, kvar, vvar))
                bwd.append(gating_bwd(b, d_gm, saved))
            (dkv_a, dss_a, dsk_a), (dmx_a, dws_a, glg_a, glb_a) = _interleave(*bwd[:2])
            (dkv_b, dss_b, dsk_b), (dmx_b, dws_b, glg_b, glb_b) = _interleave(*bwd[2:])

            vec_ref[0:1, :] += rms_g[0][0] + rms_g[1][0]
            vec_ref[1:2, :] += rms_g[0][1] + rms_g[1][1]
            vec_ref[2:3, :] += glg_a + glg_b
            vec_ref[3:4, :] += glb_a + glb_b
            dbs_ref[...] += dmx_a + dmx_b
            for g in range(N_GROUPS):
                dws_ref[g] += dws_a[g] + dws_b[g]
            for h in range(N_HEADS):
                gbias_ref[h] += dss_a[h] + dss_b[h]
                dsink_ref[h] += dsk_a[h] + dsk_b[h]

            dkv_ref[:BLOCK, :] = done[...].astype(BF16)
            dkv_ref[BLOCK:, :] = (carry[...] + dkv_a[:BLOCK]).astype(BF16)
            done[...] = dkv_a[BLOCK:] + dkv_b[:BLOCK]
            carry[...] = dkv_b[BLOCK:]

    last = n_steps - 1
    cur = lambda w: pl.BlockSpec((tile, w), lambda n: (jnp.minimum(n, last), 0))
    late = lambda w: pl.BlockSpec((tile, w), lambda n: (jnp.clip(n - 1, 0, last), 0))
    before = pl.BlockSpec((BLOCK, 2 * KV_W), lambda n: (jnp.clip(2 * n - 1, 0, 2 * last + 1), 0))
    outs = pl.pallas_call(
        body, name="bwd_mix",
        out_shape=[jax.ShapeDtypeStruct((s, ATTN_W), BF16), jax.ShapeDtypeStruct((s, 2 * KV_W), BF16),
                   jax.ShapeDtypeStruct((s, GMLP_W), BF16), jax.ShapeDtypeStruct((s, GMLP_W), BF16),
                   jax.ShapeDtypeStruct((N_HEADS, BLOCK, 2 * BLOCK), F32),
                   jax.ShapeDtypeStruct((N_GROUPS, BLOCK, BLOCK), F32),
                   jax.ShapeDtypeStruct((BLOCK, GMLP_W), F32), jax.ShapeDtypeStruct((8, GMLP_W), F32),
                   jax.ShapeDtypeStruct((N_HEADS, BLOCK, 1), F32)]
        + [jax.ShapeDtypeStruct(_rx_shape(p.shape, k), BF16) for p, k in zip(grad_parts, grad_kinds)],
        grid=(n_steps + 1,),
        in_specs=[cur(ATTN_W), cur(2 * KV_W), before, cur(GMLP_W), cur(GMLP_W), cur(D_MODEL),
                  _const_spec((N_HEADS, BLOCK, 2 * BLOCK)), pl.BlockSpec(memory_space=pltpu.SMEM),
                  _const_spec((1, GMLP_W)), _const_spec((1, GMLP_W)), _const_spec((N_GROUPS, BLOCK, BLOCK)),
                  _const_spec((BLOCK, GMLP_W)), _const_spec((GMLP_W, GMLP_W)), _const_spec((1, ATTN_W)),
                  _const_spec((1, GMLP_W))] + [ANY] * n_g,
        out_specs=[cur(ATTN_W), late(2 * KV_W), cur(GMLP_W), cur(GMLP_W),
                   _const_spec((N_HEADS, BLOCK, 2 * BLOCK)), _const_spec((N_GROUPS, BLOCK, BLOCK)),
                   _const_spec((BLOCK, GMLP_W)), _const_spec((8, GMLP_W)), _const_spec((N_HEADS, BLOCK, 1))]
        + [ANY] * n_g,
        scratch_shapes=[pltpu.VMEM((BLOCK, 2 * KV_W), F32), pltpu.VMEM((BLOCK, 2 * KV_W), F32)]
        + _ChipExchange.sems(n_g),
        compiler_params=_params(("arbitrary",)),
    )(q, kv, kv, gu, gv, dmix, bias, sinks, gln_g, gln_b, wsm, bsx, amat, aog, gog, *grad_parts)
    return outs[:9], outs[9:]


def _mix_finalize(gbias, bucket, dws, dbs, dsink):
    def body(gb_ref, bucket_ref, dws_ref, dbs_ref, dsink_ref, tall_ref):
        bk = bucket_ref[...]
        lane = lax.broadcasted_iota(jnp.int32, (N_BUCKETS, LANES), 1)
        rowi = lax.broadcasted_iota(jnp.int32, (N_BUCKETS, LANES), 0)
        drb = jnp.zeros((N_BUCKETS, LANES), F32)
        dsk = jnp.zeros((8, LANES), F32)
        lane8 = lax.broadcasted_iota(jnp.int32, (8, LANES), 1)
        for b in range(N_BUCKETS):
            in_bucket = bk == float(b)
            for h in range(N_HEADS):
                tot = jnp.sum(_colsum(jnp.where(in_bucket, gb_ref[h], 0.0)), axis=1, keepdims=True)
                drb = jnp.where((rowi == h) & (lane == b), tot, drb)
        for h in range(N_HEADS):
            sk = jnp.sum(dsink_ref[h], axis=0, keepdims=True)
            dsk = jnp.where(lane8 == h, sk, dsk)
        tall_ref[TALL_RB:TALL_RB + N_BUCKETS, :] = drb
        tall_ref[TALL_SK:TALL_SK + 8, :] = dsk
        ti = lax.broadcasted_iota(jnp.int32, (BLOCK, BLOCK), 0)
        ui = lax.broadcasted_iota(jnp.int32, (BLOCK, BLOCK), 1)
        for g in range(N_GROUPS):
            tall_ref[g * BLOCK:(g + 1) * BLOCK, :] = jnp.where(ti >= ui, dws_ref[g], 0.0)
        gi = lax.broadcasted_iota(jnp.int32, (GMLP_W, LANES), 0) // GROUP_DIM
        li = lax.broadcasted_iota(jnp.int32, (GMLP_W, LANES), 1)
        ind = jnp.where(gi == li, 1.0, 0.0).astype(BF16)
        d = dbs_ref[...]
        hi = d.astype(BF16)
        r1 = d - hi.astype(F32)
        mid = r1.astype(BF16)
        lo = (r1 - mid.astype(F32)).astype(BF16)
        dbsg = _dot(hi, ind) + _dot(mid, ind) + _dot(lo, ind)
        tall_ref[TALL_BS:TALL_BS + N_GROUPS, :] = dbsg.T[:N_GROUPS, :]

    return pl.pallas_call(
        body, name="mix_finalize", out_shape=jax.ShapeDtypeStruct((TALL_ROWS, LANES), F32), grid=(1,),
        in_specs=[_const_spec((N_HEADS, BLOCK, 2 * BLOCK)), _const_spec((BLOCK, 2 * BLOCK)),
                  _const_spec((N_GROUPS, BLOCK, BLOCK)), _const_spec((BLOCK, GMLP_W)),
                  _const_spec((N_HEADS, BLOCK, 1))],
        out_specs=_const_spec((TALL_ROWS, LANES)),
        compiler_params=_params(("arbitrary",)),
    )(gbias, bucket, dws, dbs, dsink)


def _bwd_in(dq, dkv, dgu, dgv, dxa, x, modr, w_in, tm):
    s = x.shape[0]

    def body(dq_ref, dkv_ref, dgu_ref, dgv_ref, dxa_ref, x_ref, mod_ref, w_ref, gx_ref, acc_ref, db_ref):
        @pl.when(pl.program_id(0) == 0)
        def _():
            acc_ref[...] = jnp.zeros_like(acc_ref)
            db_ref[...] = jnp.zeros_like(db_ref)

        dproj = jnp.concatenate([dq_ref[...], dkv_ref[...], dgu_ref[...], dgv_ref[...]], axis=1)
        dh1 = _dot(dproj, w_ref[...])
        gx_ref[...] = dxa_ref[...].astype(F32) + dh1 * (1.0 + mod_ref[1:2, :])
        acc_ref[0:1, :] += _colsum(dh1 * x_ref[...].astype(F32))
        acc_ref[1:2, :] += _colsum(dh1)
        db_ref[0:1, :] += _colsum(dproj.astype(F32))

    row = lambda w: pl.BlockSpec((tm, w), lambda i: (i, 0))
    return pl.pallas_call(
        body, name="bwd_in",
        out_shape=(jax.ShapeDtypeStruct((s, D_MODEL), F32), jax.ShapeDtypeStruct((8, D_MODEL), F32),
                   jax.ShapeDtypeStruct((8, IN_W), F32)),
        grid=(s // tm,),
        in_specs=[row(ATTN_W), row(2 * KV_W), row(GMLP_W), row(GMLP_W), row(D_MODEL), row(D_MODEL),
                  _const_spec((8, D_MODEL)), _const_spec(w_in.shape)],
        out_specs=(row(D_MODEL), _const_spec((8, D_MODEL)), _const_spec((8, IN_W))),
        compiler_params=_params(("arbitrary",)),
    )(dq, dkv, dgu, dgv, dxa, x, modr, w_in)


def _wgrad(a, bs, tm, tk, name, transposed=False, gather_vs=()):
    k_all, m = a.shape
    n = sum(b.shape[1] for b in bs)
    nk = k_all // tk
    nm = m // tm
    n_b = len(bs)
    n_v = len(gather_vs)

    def body(a_ref, *rest):
        b_refs, v_refs = rest[:n_b], rest[n_b:n_b + n_v]
        o_ref, ob_ref = rest[n_b + n_v:n_b + n_v + 2]
        vg_refs = rest[n_b + n_v + 2:n_b + 2 * n_v + 2]
        i, k = pl.program_id(0), pl.program_id(1)
        if n_v:
            gather = _Gather8(v_refs, vg_refs, *rest[n_b + 2 * n_v + 2:])

            @pl.when((i == 0) & (k == 0))
            def _():
                gather.start()

            @pl.when((i == nm - 1) & (k == 0))
            def _():
                gather.forward()

        @pl.when(k == 0)
        def _():
            o_ref[...] = jnp.zeros_like(o_ref)

        b = b_refs[0][...] if n_b == 1 else jnp.concatenate([r[...] for r in b_refs], axis=1)
        if transposed:
            o_ref[...] += _dot_tn(b, a_ref[...])
        else:
            o_ref[...] += _dot_tn(a_ref[...], b)

        @pl.when(k == nk - 1)
        def _():
            ob_ref[...] = o_ref[...].astype(BF16)

        if n_v:
            @pl.when((i == nm - 1) & (k == nk - 1))
            def _():
                gather.finish()

    if transposed:
        out_spec = pl.BlockSpec((n, tm), lambda i, k: (0, i))
        shape = (n, m)
    else:
        out_spec = pl.BlockSpec((tm, n), lambda i, k: (i, 0))
        shape = (m, n)
    outs = pl.pallas_call(
        body, name=name,
        out_shape=[jax.ShapeDtypeStruct(shape, F32), jax.ShapeDtypeStruct(shape, BF16)] + _gathered8_shapes(gather_vs),
        grid=(nm, nk),
        in_specs=[pl.BlockSpec((tk, tm), lambda i, k: (k, i))]
        + [pl.BlockSpec((tk, b.shape[1]), lambda i, k: (k, 0)) for b in bs] + [ANY] * n_v,
        out_specs=[out_spec, out_spec] + [ANY] * n_v,
        scratch_shapes=_Gather8.sems(n_v) if n_v else [],
        compiler_params=_params(("arbitrary", "arbitrary") if n_v else ("parallel", "arbitrary")),
    )(a, *bs, *gather_vs)
    return outs[0], outs[1], outs[2:]


def _adam_math(w, g, m, v):
    m2 = ADAM_B1 * m + (1.0 - ADAM_B1) * g
    v2 = ADAM_B2 * v + (1.0 - ADAM_B2) * (g * g)
    m_hat = m2 / (1.0 - ADAM_B1 ** ADAM_STEP)
    v_hat = v2 / (1.0 - ADAM_B2 ** ADAM_STEP)
    delta = -ADAM_LR * (m_hat / (jnp.sqrt(v_hat) + ADAM_EPS) + ADAM_WD * w)
    return delta, m2, v2


def _adam_halves(w, mine, got, m, v, tr, name):
    r, cc = w.shape
    h = r // 2
    nt = h // tr

    def body(c_ref, w_ref, mine_ref, got_ref, m_ref, v_ref, g_ref, d_ref, m2_ref, v2_ref):
        g = jnp.where(pl.program_id(0) == c_ref[0], mine_ref[...], got_ref[...])
        g_ref[...] = g
        d, m2, v2 = _adam_math(w_ref[...], g, m_ref[...], v_ref[...])
        d_ref[...] = d
        m2_ref[...] = m2
        v2_ref[...] = v2

    full = pl.BlockSpec((tr, cc), lambda hh, i, c_ref: (hh * nt + i, 0))
    half = pl.BlockSpec((tr, cc), lambda hh, i, c_ref: (i, 0))
    shp = jax.ShapeDtypeStruct((r, cc), F32)
    return pl.pallas_call(
        body, name=name, out_shape=(shp, shp, shp, shp),
        grid_spec=pltpu.PrefetchScalarGridSpec(
            num_scalar_prefetch=1, grid=(2, nt), in_specs=[full, half, half, full, full],
            out_specs=(full, full, full, full)),
        compiler_params=_params(("arbitrary", "arbitrary")),
    )(_core_index_scalar(), w, mine, got, m, v)


def _adam_w_ada(sc_t, dmod_all, w, m, v, tr):
    r, cc = w.shape

    def body(chip_ref, sct_ref, dm_ref, w_ref, m_ref, v_ref, g_ref, d_ref, m2_ref, v2_ref):
        g = sct_ref[:, 0:1] * dm_ref[0:1, :]
        for k in range(1, N_DEV):
            g = g + sct_ref[:, k:k + 1] * dm_ref[k:k + 1, :]
        g_ref[...] = g
        d, m2, v2 = _adam_math(w_ref[...], g, m_ref[...], v_ref[...])
        d_ref[...] = d
        m2_ref[...] = m2
        v2_ref[...] = v2

    spec = pl.BlockSpec((tr, cc), lambda i, chip_ref: (i, 0))
    shp = jax.ShapeDtypeStruct((r, cc), F32)
    return pl.pallas_call(
        body, name="adam_w_ada", out_shape=(shp, shp, shp, shp),
        grid_spec=pltpu.PrefetchScalarGridSpec(
            num_scalar_prefetch=1, grid=(r // tr,),
            in_specs=[pl.BlockSpec((tr, N_DEV), lambda i, chip_ref: (i, 0)),
                      pl.BlockSpec((N_DEV, cc), lambda i, chip_ref: (0, chip_ref[0])), spec, spec, spec],
            out_specs=(spec, spec, spec, spec)),
        compiler_params=_params(("parallel",)),
    )(_chip_index_scalar(), sc_t, dmod_all, w, m, v)


def _pack_wide(acc_i, acc_m, acc_f, db_in, vec):
    arrs = [acc_i, acc_m, acc_f, db_in, vec]
    i_, m_, f_, b_, v_ = range(5)
    src = {"b_in": (b_, 0), "ln1_g": (m_, 2), "ln1_b": (m_, 3), "ln2_g": (f_, 1), "ln2_b": (f_, 2),
           "gmlp_ln_g": (v_, 2), "gmlp_ln_b": (v_, 3), "attn_out_g": (v_, 0), "gmlp_out_g": (v_, 1), "loss": (f_, 0)}
    dmod = [(i_, 1), (i_, 0), (m_, 4), (m_, 1), (m_, 0), (f_, 3)]

    def body(*refs):
        ins, wide_ref = refs[:5], refs[5]
        wide_ref[...] = jnp.zeros_like(wide_ref)
        for k, (a, row) in enumerate(dmod):
            wide_ref[0:1, k * D_MODEL:(k + 1) * D_MODEL] = ins[a][row:row + 1, :]
        for name, (a, row) in src.items():
            r, off, n = WIDE_LAYOUT[name]
            wide_ref[r:r + 1, off:off + n] = ins[a][row:row + 1, :]

    return pl.pallas_call(
        body, name="pack_wide", out_shape=jax.ShapeDtypeStruct((8, WIDE_W), F32), grid=(1,),
        in_specs=[_const_spec(a.shape) for a in arrs], out_specs=_const_spec((8, WIDE_W)),
        compiler_params=_params(("arbitrary",)),
    )(*arrs)


def _adam_small(gw, gt, wide_wmv, w_s, b_s, rel_bias, sinks, after):
    names = list(WIDE_PARAMS)
    tall = [("gmlp_w_s", w_s), ("gmlp_b_s", b_s), ("rel_bias", rel_bias), ("attn_sinks", sinks)]
    ins = [gw, gt]
    for n in names:
        ins += list(wide_wmv[n])
    for _, t in tall:
        ins += list(t)
    n_in = len(ins)

    def body(*refs):
        gw_ref, gt_ref = refs[0], refs[1]
        wmv = refs[2:n_in]
        dmod_ref, loss_ref, loss1_ref = refs[n_in + 1:n_in + 4]
        outs = refs[n_in + 4:]

        def tall_sum(r0, nr):
            g = gt_ref[r0:r0 + nr, :]
            for d in range(1, N_DEV):
                g = g + gt_ref[d * TALL_ROWS + r0:d * TALL_ROWS + r0 + nr, :]
            return g

        def emit(k, g, w_ref, m_ref, v_ref):
            d, m2, v2 = _adam_math(w_ref[...], g, m_ref[...], v_ref[...])
            outs[4 * k][...] = g
            outs[4 * k + 1][...] = d
            outs[4 * k + 2][...] = m2
            outs[4 * k + 3][...] = v2

        gsum = gw_ref[0:8, :]
        for d in range(1, N_DEV):
            gsum = gsum + gw_ref[8 * d:8 * d + 8, :]
        for d in range(N_DEV):
            dmod_ref[d:d + 1, :] = gw_ref[8 * d:8 * d + 1, :]
        for k, n in enumerate(names):
            r, off, sz = WIDE_LAYOUT[n]
            emit(k, gsum[r:r + 1, off:off + sz], *wmv[3 * k:3 * k + 3])
        r, off, sz = WIDE_LAYOUT["loss"]
        tot = jnp.sum(gsum[r:r + 1, off:off + sz], axis=1, keepdims=True)
        loss_ref[...] = jnp.broadcast_to(tot * (0.5 / D_MODEL), loss_ref.shape)
        loss1_ref[...] = tot * (0.5 / D_MODEL)

        k0 = len(names)
        ws_refs = wmv[3 * k0:3 * k0 + 3]
        for g in range(N_GROUPS):
            rows = slice(g * BLOCK, (g + 1) * BLOCK)
            gg = tall_sum(g * BLOCK, BLOCK)
            d, m2, v2 = _adam_math(ws_refs[0][rows, :], gg, ws_refs[1][rows, :], ws_refs[2][rows, :])
            outs[4 * k0][rows, :] = gg
            outs[4 * k0 + 1][rows, :] = d
            outs[4 * k0 + 2][rows, :] = m2
            outs[4 * k0 + 3][rows, :] = v2
        emit(k0 + 1, tall_sum(TALL_BS, N_GROUPS), *wmv[3 * (k0 + 1):3 * (k0 + 1) + 3])
        emit(k0 + 2, tall_sum(TALL_RB, N_HEADS)[:, :N_BUCKETS], *wmv[3 * (k0 + 2):3 * (k0 + 2) + 3])
        emit(k0 + 3, tall_sum(TALL_SK, 8)[0:1, :N_HEADS], *wmv[3 * (k0 + 3):3 * (k0 + 3) + 3])

    out_shapes = [jax.ShapeDtypeStruct((N_DEV, WIDE_W), F32), jax.ShapeDtypeStruct((8, LANES), F32),
                  jax.ShapeDtypeStruct((1, 1), F32)]
    for n in names:
        out_shapes += [jax.ShapeDtypeStruct(wide_wmv[n][0].shape, F32)] * 4
    for _, t in tall:
        out_shapes += [jax.ShapeDtypeStruct(t[0].shape, F32)] * 4
    res = pl.pallas_call(
        body, name="adam_small", out_shape=out_shapes, grid=(1,),
        in_specs=[_const_spec(a.shape) for a in ins] + [ANY], out_specs=[_const_spec(o.shape) for o in out_shapes],
        compiler_params=_params(("arbitrary",)),
    )(*ins, after)
    out = {}
    for k, n in enumerate(names + [t[0] for t in tall]):
        out[n] = tuple(res[3 + 4 * k:7 + 4 * k])
    return res[0], res[1], res[2], out


def kernel(x, c, rel_bias, w_ada, b_ada, w_in, b_in, attn_sinks, gmlp_ln_g, gmlp_ln_b, gmlp_w_s, gmlp_b_s, attn_out_g, gmlp_out_g, w_out, ln1_g, ln1_b, w_gate_up, w_down, ln2_g, ln2_b, loss_target, m_rel_bias, m_w_ada, m_b_ada, m_w_in, m_b_in, m_attn_sinks, m_gmlp_ln_g, m_gmlp_ln_b, m_gmlp_w_s, m_gmlp_b_s, m_attn_out_g, m_gmlp_out_g, m_w_out, m_ln1_g, m_ln1_b, m_w_gate_up, m_w_down, m_ln2_g, m_ln2_b, v_rel_bias, v_w_ada, v_b_ada, v_w_in, v_b_in, v_attn_sinks, v_gmlp_ln_g, v_gmlp_ln_b, v_gmlp_w_s, v_gmlp_b_s, v_attn_out_g, v_gmlp_out_g, v_w_out, v_ln1_g, v_ln1_b, v_w_gate_up, v_w_down, v_ln2_g, v_ln2_b):
    ix, iy, _ = _my_pos()
    chip = 2 * ix + iy
    s = x.shape[1]
    xs = x[0]
    tgt = loss_target[0]
    tm_big = min(512, s)
    tm_ffn = min(FFN_SUB, s)

    sc_all, modr, (w_in_g, w_out_g) = _prologue(
        jnp.pad(c, ((0, 7), (0, 0))), w_ada[0], b_ada, [_with_own_block(w_in[0].T, chip), _with_own_block(w_out[0], chip)])
    w_in_f = w_in_g.reshape(IN_W, D_MODEL)

    bucket = _bucket_table()
    bias, wsm = _prep_tables(bucket, rel_bias.T, gmlp_w_s[0])
    bsx = jnp.repeat(gmlp_b_s[0].T, GROUP_DIM, axis=1)
    amat = _group_mean_matrix()
    sinks = attn_sinks[0]

    (h1, q, kv, gu, gv, xb), (w_dn_g,) = _fwd_in(xs, modr, w_in_f, b_in, tm_big, [_with_own_block(w_down[0], chip)],
                                                 ["blk"])
    w_out_f = w_out_g.reshape(D_MODEL, D_MODEL)
    (x1, x1b, y, mixed), (w_gu_f,) = _fwd_mix(
        q, kv, gu, gv, xs, modr, bias, sinks, gmlp_ln_g, gmlp_ln_b, wsm, bsx, amat, attn_out_g, gmlp_out_g, w_out_f,
        ln1_g, ln1_b, tm_big, [_with_own_block(w_gate_up[0], chip)], ["blk"])
    assert w_gate_up.shape[2] == FF_CHUNK
    w_dn_f = w_dn_g.reshape(D_FF, D_MODEL)
    h2, act, dy2, dx1a, acc_f = _fwd_ffn(x1, tgt, modr, ln2_g, ln2_b, w_gu_f, w_dn_f, min(2 * FFN_SUB, s))

    a_act, dgu_ff, dh2 = _bwd_ffn(dy2, act, w_gu_f, w_dn_f, min(FFN_SUB, s))
    g_dn, g_dn_b, _ = _wgrad(a_act, [dy2], D_FF // 2, min(1024, s), "wgrad_down")
    g_gu, g_gu_b, _ = _wgrad(h2, [dgu_ff], 512, min(512, s), "wgrad_gate_up")
    blk3 = lambda a, rows: a.reshape(N_CHIPS, rows, a.shape[1])
    (dxa, dy, dmix, acc_m), (got_dn, got_gu) = _bwd_mid(
        dh2, dx1a, x1b, xs, y, modr, ln1_g, w_out_f, tm_big, [blk3(g_dn_b, D_FF // N_CHIPS), g_gu_b], ["blk", "cols"])
    g_out, g_out_b, _ = _wgrad(mixed, [dy], 512, min(2048, s), "wgrad_out")
    (got_out,) = _swap_halves([blk3(g_out_b, D_MODEL // N_CHIPS)], ["blk"], "rs_swap_out")
    kinds_a = ["blk", "cols", "blk"]
    fulls_a = [blk3(g_dn, D_FF // N_CHIPS), g_gu, blk3(g_out, D_MODEL // N_CHIPS)]
    gots_a = [got_dn, got_gu, got_out]
    parts_a = [_add_halves(f, g, k, "rs_add_a%d" % i) for i, (f, g, k) in enumerate(zip(fulls_a, gots_a, kinds_a))]
    (dq, dkv, dgu, dgv, gbias, dws, dbs, vec, dsink), rxs_a = _bwd_mix(
        q, kv, gu, gv, dmix, bias, sinks, gmlp_ln_g, gmlp_ln_b, wsm, bsx, amat, attn_out_g, gmlp_out_g,
        [p[1] for p in parts_a], kinds_a)
    tall_g = _mix_finalize(gbias, bucket, dws, dbs, dsink)
    grad_x, acc_i, db_in = _bwd_in(dq, dkv, dgu, dgv, dxa, xb, modr, w_in_f, tm_big)

    wide_g = _pack_wide(acc_i, acc_m, acc_f, db_in, vec)
    full_in, full_in_b, (gw, gt) = _wgrad(h1, [dq, dkv, dgu, dgv], 512, min(1024, s), "wgrad_in", transposed=True,
                                          gather_vs=[wide_g, tall_g])
    (got_in,) = _swap_halves([blk3(full_in_b, IN_W // N_CHIPS)], ["blk"], "rs_swap_in")
    part_in = _add_halves(blk3(full_in, IN_W // N_CHIPS), got_in, "blk", "rs_add_in")
    in_send, in_recv, in_part, in_rx, token = _exchange_start(part_in[1], "rs_chips_in_start")
    wide_wmv ={"b_ada": (b_ada, m_b_ada, v_b_ada), "b_in": (b_in, m_b_in, v_b_in),
                "ln1_g": (ln1_g, m_ln1_g, v_ln1_g), "ln1_b": (ln1_b, m_ln1_b, v_ln1_b),
                "ln2_g": (ln2_g, m_ln2_g, v_ln2_g), "ln2_b": (ln2_b, m_ln2_b, v_ln2_b),
                "gmlp_ln_g": (gmlp_ln_g, m_gmlp_ln_g, v_gmlp_ln_g), "gmlp_ln_b": (gmlp_ln_b, m_gmlp_ln_b, v_gmlp_ln_b),
                "attn_out_g": (attn_out_g, m_attn_out_g, v_attn_out_g),
                "gmlp_out_g": (gmlp_out_g, m_gmlp_out_g, v_gmlp_out_g)}
    rows2 = lambda a: a.reshape(-1, a.shape[-1])
    dmod_all, loss_t, loss1, small = _adam_small(
        gw, gt, wide_wmv, tuple(rows2(a) for a in (gmlp_w_s, m_gmlp_w_s, v_gmlp_w_s)),
        tuple(rows2(a) for a in (gmlp_b_s, m_gmlp_b_s, v_gmlp_b_s)), (rel_bias.T, m_rel_bias.T, v_rel_bias.T),
        (attn_sinks, m_attn_sinks, v_attn_sinks), token)
    small["rel_bias"] = tuple(a.T for a in small["rel_bias"])
    loss = loss1.reshape(())

    g_ada, d_ada, m_ada, v_ada = _adam_w_ada(sc_all.T, dmod_all, w_ada[0], m_w_ada[0], v_w_ada[0], 256)

    sums = [(parts_a[0][0], rxs_a[0], 176), (parts_a[1][0], rxs_a[1], 256), (parts_a[2][0], rxs_a[2], 128)]
    mine = [_sum_chips(p, rx, tr, "rs_sum_%d" % i, loss_t) for i, (p, rx, tr) in enumerate(sums)]
    got = _share_halves(mine, "rs_share")
    gs_dn, d_dn, m_dn, v_dn = _adam_halves(w_down[0], mine[0], got[0], m_w_down[0], v_w_down[0], 176, "adam_w_down")
    gs_gu, d_gu, m_gu, v_gu = _adam_halves(w_gate_up[0], mine[1], got[1], m_w_gate_up[0], v_w_gate_up[0], 256,
                                           "adam_w_gate_up")
    gs_out, d_out, m_out, v_out = _adam_halves(w_out[0], mine[2], got[2], m_w_out[0], v_w_out[0], 128, "adam_w_out")

    rx_in = _exchange_wait(in_send, in_recv, in_part, in_rx, d_gu, "rs_chips_in_wait")
    mine_in = _sum_chips(part_in[0], rx_in, 112, "rs_sum_in", rx_in)
    (got_in_half,) = _share_halves([mine_in], "rs_share_in")
    in_t = _adam_halves(w_in[0].T, mine_in, got_in_half, m_w_in[0].T, v_w_in[0].T, 112, "adam_w_in")
    gs_in, d_in, m_in, v_in = (a.T for a in in_t)

    big = {"w_ada": (g_ada, d_ada, m_ada, v_ada), "w_in": (gs_in, d_in, m_in, v_in), "w_out": (gs_out, d_out, m_out, v_out),
           "w_gate_up": (gs_gu, d_gu, m_gu, v_gu), "w_down": (gs_dn, d_dn, m_dn, v_dn)}
    order = ["rel_bias", "w_ada", "b_ada", "w_in", "b_in", "attn_sinks", "gmlp_ln_g", "gmlp_ln_b", "gmlp_w_s", "gmlp_b_s",
             "attn_out_g", "gmlp_out_g", "w_out", "ln1_g", "ln1_b", "w_gate_up", "w_down", "ln2_g", "ln2_b"]
    shapes = {"gmlp_w_s": gmlp_w_s.shape, "gmlp_b_s": gmlp_b_s.shape}
    outs = [loss, grad_x[None]]
    for k in range(4):
        for name in order:
            if name in big:
                outs.append(big[name][k][None])
            elif name in shapes:
                outs.append(small[name][k].reshape(shapes[name]))
            else:
                outs.append(small[name][k])
    return tuple(outs)
```
